```python
import math
import jax
import jax.numpy as jnp
from jax import lax
import numpy as np

D_MODEL = 1024
BATCH = 16
SEQ = 4096
DEPTH = 1

GRID_W = 64
CTX_LEN = 256
NORM_EPS = 1e-6

POOL_WINDOWS = (2, 4, 8, 16)
N_POOL_GROUPS = 4
POOL_WIDTH = D_MODEL
POOL_GROUP = POOL_WIDTH // N_POOL_GROUPS

SSD_EXPAND = 2
D_INNER = SSD_EXPAND * D_MODEL
HEAD_DIM = 64
N_HEADS = D_INNER // HEAD_DIM
D_STATE = 128
N_BC_GROUPS = 4
CONV_K = 4
CONV_LEFT = CONV_K // 2
CHUNK = 128
N_DIR = 2
SSD_NORM_GROUPS = N_BC_GROUPS
CONV_DIM = D_INNER + 2 * N_BC_GROUPS * D_STATE

N_BRANCH = 2
OFF_POOL_V = 0
OFF_POOL_Z = OFF_POOL_V + POOL_WIDTH
OFF_SSD_Z = OFF_POOL_Z + POOL_WIDTH
OFF_GATE = OFF_SSD_Z + D_INNER
OFF_XBC = OFF_GATE + N_BRANCH * D_MODEL
OFF_DT = OFF_XBC + CONV_DIM
IN_COLS = OFF_DT + N_DIR * N_HEADS

kernel_name = 'hybrid_pool_ssd_diffusion_block'


def rmsnorm(x, w):
    xf = x.astype(jnp.float32)
    y = xf * lax.rsqrt(jnp.mean(xf * xf, axis=-1, keepdims=True) + NORM_EPS)
    return (y * w.astype(jnp.float32)).astype(x.dtype)


def adaln(cond, w_ada, b_ada):
    mod = jax.nn.silu(cond) @ w_ada + b_ada
    return jnp.split(mod, 3, axis=-1)


def centred_dwconv(u, w, b):
    l = u.shape[1]
    up = jnp.pad(u, ((0, 0), (CONV_LEFT, CONV_K - 1 - CONV_LEFT), (0, 0)))
    out = up[:, 0:l] * w[0]
    for k in range(1, CONV_K):
        out = out + up[:, k:k + l] * w[k]
    return out + b


def box_mean(v, k, axis):
    n = v.shape[axis]
    lo, hi = k // 2, k - 1 - k // 2
    cs = jnp.cumsum(v.astype(jnp.float32), axis=axis)
    pad = [(0, 0)] * v.ndim
    pad[axis] = (1, 0)
    cs = jnp.pad(cs, pad)
    t = jnp.arange(n)
    i_hi = jnp.minimum(t + hi + 1, n)
    i_lo = jnp.maximum(t - lo, 0)
    s = jnp.take(cs, i_hi, axis=axis) - jnp.take(cs, i_lo, axis=axis)
    shape = [1] * v.ndim
    shape[axis] = n
    cnt = (i_hi - i_lo).astype(jnp.float32).reshape(shape)
    return (s / cnt).astype(v.dtype)


def pool_mixer(v, pool_w, pool_scale, rows):
    b, l, _ = v.shape
    diffs = []
    for gi, k in enumerate(POOL_WINDOWS):
        vg = v[..., gi * POOL_GROUP:(gi + 1) * POOL_GROUP]
        if rows is None:
            m = box_mean(vg, k, 1)
        else:
            vg2 = vg.reshape(b, rows, GRID_W, POOL_GROUP)
            m = box_mean(box_mean(vg2, k, 1), k, 2).reshape(b, l, POOL_GROUP)
        diffs.append(m - vg)
    d = jnp.stack(diffs, axis=2)
    y = jnp.einsum('blgi,gio->blgo', d, pool_w).reshape(b, l, POOL_WIDTH)
    return y * pool_scale


def ssd_prep(xbc_raw, dt_raw, conv_w, conv_b, dt_bias, a_log):
    b, l, _ = xbc_raw.shape
    xbc = jax.nn.silu(centred_dwconv(xbc_raw, conv_w, conv_b))
    bc = N_BC_GROUPS * D_STATE
    xs = xbc[..., :D_INNER].reshape(b, l, N_HEADS, HEAD_DIM)
    Bm = xbc[..., D_INNER:D_INNER + bc].reshape(b, l, N_BC_GROUPS, D_STATE)
    Cm = xbc[..., D_INNER + bc:].reshape(b, l, N_BC_GROUPS, D_STATE)
    dt = jax.nn.softplus((dt_raw.reshape(b, l, N_DIR, N_HEADS) + dt_bias).astype(jnp.float32))
    A = -jnp.exp(a_log.astype(jnp.float32))
    return xs, Bm, Cm, dt, A


def ssd_scan(xs, dt, A, Bm, Cm, h0, with_output):
    b, l, h, p = xs.shape
    g, n = Bm.shape[2], Bm.shape[3]
    r = h // g
    nc = l // CHUNK
    a_cs = jnp.cumsum((dt * A).reshape(b, nc, CHUNK, g, r), axis=2)
    xdt = (xs * dt[..., None].astype(xs.dtype)).reshape(b, nc, CHUNK, g, r, p)
    Br = Bm.reshape(b, nc, CHUNK, g, n)
    a_last = a_cs[:, :, -1]
    to_end = jnp.exp(a_last[:, :, None] - a_cs).astype(xs.dtype)
    states = jnp.einsum('bcjgn,bcjgrp->bcgrpn', Br, xdt * to_end[..., None])
    chunk_decay = jnp.exp(a_last).astype(xs.dtype)
    if h0 is None:
        h0 = jnp.zeros((b, h, p, n), xs.dtype)

    def step(carry, inp):
        s, d = inp
        nxt = d[..., None, None] * carry + s
        return nxt, (carry if with_output else None)

    h_final, h_start = lax.scan(step, h0.reshape(b, g, r, p, n),
                                (jnp.moveaxis(states, 1, 0), jnp.moveaxis(chunk_decay, 1, 0)))
    h_final = h_final.reshape(b, h, p, n)
    if not with_output:
        return None, h_final
    h_start = jnp.moveaxis(h_start, 0, 1)
    Cr = Cm.reshape(b, nc, CHUNK, g, n)
    seg = a_cs[:, :, :, None] - a_cs[:, :, None]
    lower = jnp.tril(jnp.ones((CHUNK, CHUNK), dtype=bool))[:, :, None, None]
    L = jnp.exp(jnp.where(lower, seg, -jnp.inf)).astype(xs.dtype)
    cb = jnp.einsum('bcign,bcjgn->bcijg', Cr, Br)
    y_diag = jnp.einsum('bcijgr,bcjgrp->bcigrp', cb[..., None] * L, xdt)
    y_off = jnp.einsum('bcign,bcgrpn->bcigrp', Cr, h_start) * jnp.exp(a_cs).astype(xs.dtype)[..., None]
    return (y_diag + y_off).reshape(b, l, h, p), h_final


def bidir_ssd(xs, dt, A, Bm, Cm, h0_f, h0_b, with_output):
    fl = lambda t: jnp.flip(t, axis=1)
    y_f, st_f = ssd_scan(xs, dt[:, :, 0], A[0], Bm, Cm, h0_f, with_output)
    y_b, st_b = ssd_scan(fl(xs), fl(dt[:, :, 1]), A[1], fl(Bm), fl(Cm), h0_b, with_output)
    y = (y_f + fl(y_b)) if with_output else None
    return y, st_f, st_b


def gated_group_rmsnorm(y, z, w):
    b, l, d = y.shape
    u = (y * jax.nn.silu(z)).astype(jnp.float32).reshape(b, l, SSD_NORM_GROUPS, d // SSD_NORM_GROUPS)
    u = u * lax.rsqrt(jnp.mean(u * u, axis=-1, keepdims=True) + NORM_EPS)
    return (u.reshape(b, l, d) * w.astype(jnp.float32)).astype(y.dtype)


def mixer(h, w_in, b_merge, pool_w, pool_scale, conv_w, conv_b, dt_bias, a_log, d_skip, ssd_norm,
          w_proj_pool, w_proj_ssd, w_out, h0_f, h0_b, rows):
    b, l, _ = h.shape
    proj = h @ w_in
    v = proj[..., OFF_POOL_V:OFF_POOL_Z]
    z_pool = proj[..., OFF_POOL_Z:OFF_SSD_Z]
    z_ssd = proj[..., OFF_SSD_Z:OFF_GATE]
    gates = jax.nn.sigmoid(proj[..., OFF_GATE:OFF_XBC] + b_merge)
    xbc_raw = proj[..., OFF_XBC:OFF_DT]
    dt_raw = proj[..., OFF_DT:]
    y_pool = pool_mixer(v, pool_w, pool_scale, rows) * jax.nn.silu(z_pool)
    xs, Bm, Cm, dt, A = ssd_prep(xbc_raw, dt_raw, conv_w, conv_b, dt_bias, a_log)
    y_ssd, st_f, st_b = bidir_ssd(xs, dt, A, Bm, Cm, h0_f, h0_b, True)
    y_ssd = (y_ssd + d_skip[:, None] * xs).reshape(b, l, D_INNER)
    y_ssd = gated_group_rmsnorm(y_ssd, z_ssd, ssd_norm)
    merged = gates[..., :D_MODEL] * (y_pool @ w_proj_pool) + gates[..., D_MODEL:] * (y_ssd @ w_proj_ssd)
    return merged @ w_out, st_f, st_b


def context_states(hc, w_in, conv_w, conv_b, dt_bias, a_log):
    proj = hc @ w_in[:, OFF_XBC:]
    xs, Bm, Cm, dt, A = ssd_prep(proj[..., :CONV_DIM], proj[..., CONV_DIM:], conv_w, conv_b, dt_bias, a_log)
    _, st_f, st_b = bidir_ssd(xs, dt, A, Bm, Cm, None, None, False)
    return st_f, st_b


def _normal(key, shape, scale):
    return jax.random.normal(key, shape, jnp.float32) * scale


def _fwd_setup_inputs(seed: int = 0) -> dict:
    key = jax.random.key(seed)
    ks = jax.random.split(key, 24)
    D = D_MODEL
    dt0 = jnp.exp(jax.random.uniform(ks[12], (DEPTH, N_DIR, N_HEADS), jnp.float32,
                                     minval=math.log(1e-3), maxval=math.log(1e-1)))
    return {
        'x': _normal(ks[0], (BATCH, SEQ, D), 1.0),
        'c': _normal(ks[1], (BATCH, D), 1.0),
        'ctx': _normal(ks[2], (BATCH, CTX_LEN, D), 1.0),
        'c_ctx': _normal(ks[3], (D,), 1.0),
        'w_ada': _normal(ks[4], (DEPTH, D, 3 * D), 0.5 * D ** -0.5),
        'b_ada': _normal(ks[5], (DEPTH, 3 * D), 0.02),
        'norm_pre': 1.0 + _normal(ks[6], (DEPTH, D), 0.05),
        'norm_post': 1.0 + _normal(ks[7], (DEPTH, D), 0.05),
        'w_in': _normal(ks[8], (DEPTH, D, IN_COLS), D ** -0.5),
        'b_merge': _normal(ks[9], (DEPTH, N_BRANCH * D), 0.02),
        'pool_w': _normal(ks[10], (DEPTH, N_POOL_GROUPS, POOL_GROUP, POOL_GROUP), POOL_GROUP ** -0.5),
        'pool_scale': 1.0 + _normal(ks[11], (DEPTH, POOL_WIDTH), 0.05),
        'conv_w': _normal(ks[13], (DEPTH, CONV_K, CONV_DIM), CONV_K ** -0.5),
        'conv_b': _normal(ks[14], (DEPTH, CONV_DIM), 0.02),
        'dt_bias': dt0 + jnp.log(-jnp.expm1(-dt0)),
        'a_log': jnp.log(jax.random.uniform(ks[15], (DEPTH, N_DIR, N_HEADS), jnp.float32, minval=1.0, maxval=16.0)),
        'd_skip': 1.0 + _normal(ks[16], (DEPTH, N_HEADS), 0.05),
        'ssd_norm': 1.0 + _normal(ks[17], (DEPTH, D_INNER), 0.05),
        'w_proj_pool': _normal(ks[18], (DEPTH, POOL_WIDTH, D), POOL_WIDTH ** -0.5),
        'w_proj_ssd': _normal(ks[19], (DEPTH, D_INNER, D), D_INNER ** -0.5),
        'w_out': _normal(ks[20], (DEPTH, D, D), D ** -0.5),
    }


def _fwd_reference(x, c, ctx, c_ctx, w_ada, b_ada, norm_pre, norm_post, w_in, b_merge, pool_w, pool_scale,
              conv_w, conv_b, dt_bias, a_log, d_skip, ssd_norm, w_proj_pool, w_proj_ssd, w_out):
    rows = x.shape[1] // GRID_W
    for layer in range(DEPTH):
        shift, scale, gate = adaln(c[:, None, :], w_ada[layer], b_ada[layer])
        shift_c, scale_c, gate_c = adaln(c_ctx, w_ada[layer], b_ada[layer])
        hc = rmsnorm(ctx, norm_pre[layer]) * (1.0 + scale_c) + shift_c
        if layer + 1 < DEPTH:
            out_c, st_f, st_b = mixer(hc, w_in[layer], b_merge[layer], pool_w[layer], pool_scale[layer],
                                      conv_w[layer], conv_b[layer], dt_bias[layer], a_log[layer],
                                      d_skip[layer], ssd_norm[layer], w_proj_pool[layer],
                                      w_proj_ssd[layer], w_out[layer], None, None, None)
            ctx_next = ctx + gate_c * rmsnorm(out_c, norm_post[layer])
        else:
            st_f, st_b = context_states(hc, w_in[layer], conv_w[layer], conv_b[layer],
                                        dt_bias[layer], a_log[layer])
            ctx_next = ctx
        hx = rmsnorm(x, norm_pre[layer]) * (1.0 + scale) + shift
        out_x, _, _ = mixer(hx, w_in[layer], b_merge[layer], pool_w[layer], pool_scale[layer],
                            conv_w[layer], conv_b[layer], dt_bias[layer], a_log[layer],
                            d_skip[layer], ssd_norm[layer], w_proj_pool[layer],
                            w_proj_ssd[layer], w_out[layer], st_f, st_b, rows)
        x = x + gate * rmsnorm(out_x, norm_post[layer])
        ctx = ctx_next
    return x


import jax as _jax
import jax.numpy as _jnp

TWIN_FORMAT = 'train_step'
FWD_PARAMS = ['x', 'c', 'ctx', 'c_ctx', 'w_ada', 'b_ada', 'norm_pre', 'norm_post', 'w_in', 'b_merge', 'pool_w', 'pool_scale', 'conv_w', 'conv_b', 'dt_bias', 'a_log', 'd_skip', 'ssd_norm', 'w_proj_pool', 'w_proj_ssd', 'w_out']
TWIN_WEIGHTS = ['c_ctx', 'w_ada', 'b_ada', 'norm_pre', 'norm_post', 'w_in', 'b_merge', 'pool_w', 'pool_scale', 'conv_w', 'conv_b', 'dt_bias', 'a_log', 'd_skip', 'ssd_norm', 'w_proj_pool', 'w_proj_ssd', 'w_out']
TWIN_DIFF_INPUT = 'x'
TWIN_INPUTS = ['x', 'c', 'ctx', 'c_ctx', 'w_ada', 'b_ada', 'norm_pre', 'norm_post', 'w_in', 'b_merge', 'pool_w', 'pool_scale', 'conv_w', 'conv_b', 'dt_bias', 'a_log', 'd_skip', 'ssd_norm', 'w_proj_pool', 'w_proj_ssd', 'w_out', 'loss_target', 'm_c_ctx', 'm_w_ada', 'm_b_ada', 'm_norm_pre', 'm_norm_post', 'm_w_in', 'm_b_merge', 'm_pool_w', 'm_pool_scale', 'm_conv_w', 'm_conv_b', 'm_dt_bias', 'm_a_log', 'm_d_skip', 'm_ssd_norm', 'm_w_proj_pool', 'm_w_proj_ssd', 'm_w_out', 'v_c_ctx', 'v_w_ada', 'v_b_ada', 'v_norm_pre', 'v_norm_post', 'v_w_in', 'v_b_merge', 'v_pool_w', 'v_pool_scale', 'v_conv_w', 'v_conv_b', 'v_dt_bias', 'v_a_log', 'v_d_skip', 'v_ssd_norm', 'v_w_proj_pool', 'v_w_proj_ssd', 'v_w_out']
TWIN_OUTPUTS = ['loss', 'grad_x', 'grad_c_ctx', 'grad_w_ada', 'grad_b_ada', 'grad_norm_pre', 'grad_norm_post', 'grad_w_in', 'grad_b_merge', 'grad_pool_w', 'grad_pool_scale', 'grad_conv_w', 'grad_conv_b', 'grad_dt_bias', 'grad_a_log', 'grad_d_skip', 'grad_ssd_norm', 'grad_w_proj_pool', 'grad_w_proj_ssd', 'grad_w_out', 'delta_c_ctx', 'delta_w_ada', 'delta_b_ada', 'delta_norm_pre', 'delta_norm_post', 'delta_w_in', 'delta_b_merge', 'delta_pool_w', 'delta_pool_scale', 'delta_conv_w', 'delta_conv_b', 'delta_dt_bias', 'delta_a_log', 'delta_d_skip', 'delta_ssd_norm', 'delta_w_proj_pool', 'delta_w_proj_ssd', 'delta_w_out', 'new_m_c_ctx', 'new_m_w_ada', 'new_m_b_ada', 'new_m_norm_pre', 'new_m_norm_post', 'new_m_w_in', 'new_m_b_merge', 'new_m_pool_w', 'new_m_pool_scale', 'new_m_conv_w', 'new_m_conv_b', 'new_m_dt_bias', 'new_m_a_log', 'new_m_d_skip', 'new_m_ssd_norm', 'new_m_w_proj_pool', 'new_m_w_proj_ssd', 'new_m_w_out', 'new_v_c_ctx', 'new_v_w_ada', 'new_v_b_ada', 'new_v_norm_pre', 'new_v_norm_post', 'new_v_w_in', 'new_v_b_merge', 'new_v_pool_w', 'new_v_pool_scale', 'new_v_conv_w', 'new_v_conv_b', 'new_v_dt_bias', 'new_v_a_log', 'new_v_d_skip', 'new_v_ssd_norm', 'new_v_w_proj_pool', 'new_v_w_proj_ssd', 'new_v_w_out']
TWIN_LEAF_KINDS = {'loss': 'loss', 'grad_x': 'grad_x', 'grad_c_ctx': 'grad_w', 'grad_w_ada': 'grad_w', 'grad_b_ada': 'grad_w', 'grad_norm_pre': 'grad_w', 'grad_norm_post': 'grad_w', 'grad_w_in': 'grad_w', 'grad_b_merge': 'grad_w', 'grad_pool_w': 'grad_w', 'grad_pool_scale': 'grad_w', 'grad_conv_w': 'grad_w', 'grad_conv_b': 'grad_w', 'grad_dt_bias': 'grad_w', 'grad_a_log': 'grad_w', 'grad_d_skip': 'grad_w', 'grad_ssd_norm': 'grad_w', 'grad_w_proj_pool': 'grad_w', 'grad_w_proj_ssd': 'grad_w', 'grad_w_out': 'grad_w', 'delta_c_ctx': 'delta_w', 'delta_w_ada': 'delta_w', 'delta_b_ada': 'delta_w', 'delta_norm_pre': 'delta_w', 'delta_norm_post': 'delta_w', 'delta_w_in': 'delta_w', 'delta_b_merge': 'delta_w', 'delta_pool_w': 'delta_w', 'delta_pool_scale': 'delta_w', 'delta_conv_w': 'delta_w', 'delta_conv_b': 'delta_w', 'delta_dt_bias': 'delta_w', 'delta_a_log': 'delta_w', 'delta_d_skip': 'delta_w', 'delta_ssd_norm': 'delta_w', 'delta_w_proj_pool': 'delta_w', 'delta_w_proj_ssd': 'delta_w', 'delta_w_out': 'delta_w', 'new_m_c_ctx': 'new_m', 'new_m_w_ada': 'new_m', 'new_m_b_ada': 'new_m', 'new_m_norm_pre': 'new_m', 'new_m_norm_post': 'new_m', 'new_m_w_in': 'new_m', 'new_m_b_merge': 'new_m', 'new_m_pool_w': 'new_m', 'new_m_pool_scale': 'new_m', 'new_m_conv_w': 'new_m', 'new_m_conv_b': 'new_m', 'new_m_dt_bias': 'new_m', 'new_m_a_log': 'new_m', 'new_m_d_skip': 'new_m', 'new_m_ssd_norm': 'new_m', 'new_m_w_proj_pool': 'new_m', 'new_m_w_proj_ssd': 'new_m', 'new_m_w_out': 'new_m', 'new_v_c_ctx': 'new_v', 'new_v_w_ada': 'new_v', 'new_v_b_ada': 'new_v', 'new_v_norm_pre': 'new_v', 'new_v_norm_post': 'new_v', 'new_v_w_in': 'new_v', 'new_v_b_merge': 'new_v', 'new_v_pool_w': 'new_v', 'new_v_pool_scale': 'new_v', 'new_v_conv_w': 'new_v', 'new_v_conv_b': 'new_v', 'new_v_dt_bias': 'new_v', 'new_v_a_log': 'new_v', 'new_v_d_skip': 'new_v', 'new_v_ssd_norm': 'new_v', 'new_v_w_proj_pool': 'new_v', 'new_v_w_proj_ssd': 'new_v', 'new_v_w_out': 'new_v'}


def _forward(args):
    return _fwd_reference(*[args[k] for k in FWD_PARAMS])


def _output_shape():
    out = _jax.eval_shape(lambda: _forward(_fwd_setup_inputs(0)))
    return out.shape, out.dtype

N_MICROBATCH = 1
ADAM_LR = 0.001
ADAM_B1 = 0.9
ADAM_B2 = 0.999
ADAM_EPS = 1e-08
ADAM_WD = 0.01
ADAM_STEP = 10
PER_EXAMPLE_BATCH_AXIS = {'x': 0, 'c': 0, 'ctx': 0, 'loss_target': 0}
SHARED_INPUTS = []
_WEIGHT_DTYPES = {'c_ctx': _jnp.float32, 'w_ada': _jnp.float32, 'b_ada': _jnp.float32, 'norm_pre': _jnp.float32, 'norm_post': _jnp.float32, 'w_in': _jnp.float32, 'b_merge': _jnp.float32, 'pool_w': _jnp.float32, 'pool_scale': _jnp.float32, 'conv_w': _jnp.float32, 'conv_b': _jnp.float32, 'dt_bias': _jnp.float32, 'a_log': _jnp.float32, 'd_skip': _jnp.float32, 'ssd_norm': _jnp.float32, 'w_proj_pool': _jnp.float32, 'w_proj_ssd': _jnp.float32, 'w_out': _jnp.float32}
MOMENT_SCALE = {'c_ctx': 8.959837e-03, 'w_ada': 3.418074e+00, 'b_ada': 6.357325e+00, 'norm_pre': 1.939699e-01, 'norm_post': 7.383513e+00, 'w_in': 7.163358e-02, 'b_merge': 5.647886e-02, 'pool_w': 7.183853e-02, 'pool_scale': 8.281599e-02, 'conv_w': 8.790635e-02, 'conv_b': 1.834491e-01, 'dt_bias': 2.610557e-01, 'a_log': 1.777618e-01, 'd_skip': 2.152001e-01, 'ssd_norm': 1.282908e-01, 'w_proj_pool': 7.443910e-02, 'w_proj_ssd': 1.878646e-01, 'w_out': 2.087652e-01}


def _to_microbatches(a, axis):
    t = _jnp.moveaxis(a, axis, 0)
    t = t.reshape((N_MICROBATCH, t.shape[0] // N_MICROBATCH) + t.shape[1:])
    return _jnp.moveaxis(t, 1, axis + 1)


def setup_inputs(seed: int = 0) -> dict:
    inp = _fwd_setup_inputs(seed)
    key = _jax.random.fold_in(_jax.random.key(seed), 7919)
    shape, _ = _output_shape()
    out = dict(inp)
    out["loss_target"] = _jax.random.normal(_jax.random.fold_in(key, 0), shape, _jnp.float32)
    for i, name in enumerate(TWIN_WEIGHTS):
        w = inp[name].astype(_jnp.float32)
        if MOMENT_SCALE is None:
            s = _jnp.sqrt(_jnp.mean(_jnp.square(w)) + 1e-30)
        else:
            s = MOMENT_SCALE[name]
        km, kv = _jax.random.split(_jax.random.fold_in(key, i + 1))
        out[name] = w
        out["m_" + name] = s * _jax.random.normal(km, w.shape, _jnp.float32)
        out["v_" + name] = (s * s) * _jax.random.uniform(kv, w.shape, _jnp.float32, 0.5, 1.5)
    if N_MICROBATCH > 1:
        for name, axis in PER_EXAMPLE_BATCH_AXIS.items():
            out[name] = _to_microbatches(out[name], axis)
    return {'x': out['x'], 'c': out['c'], 'ctx': out['ctx'], 'c_ctx': out['c_ctx'], 'w_ada': out['w_ada'], 'b_ada': out['b_ada'], 'norm_pre': out['norm_pre'], 'norm_post': out['norm_post'], 'w_in': out['w_in'], 'b_merge': out['b_merge'], 'pool_w': out['pool_w'], 'pool_scale': out['pool_scale'], 'conv_w': out['conv_w'], 'conv_b': out['conv_b'], 'dt_bias': out['dt_bias'], 'a_log': out['a_log'], 'd_skip': out['d_skip'], 'ssd_norm': out['ssd_norm'], 'w_proj_pool': out['w_proj_pool'], 'w_proj_ssd': out['w_proj_ssd'], 'w_out': out['w_out'], 'loss_target': out['loss_target'], 'm_c_ctx': out['m_c_ctx'], 'm_w_ada': out['m_w_ada'], 'm_b_ada': out['m_b_ada'], 'm_norm_pre': out['m_norm_pre'], 'm_norm_post': out['m_norm_post'], 'm_w_in': out['m_w_in'], 'm_b_merge': out['m_b_merge'], 'm_pool_w': out['m_pool_w'], 'm_pool_scale': out['m_pool_scale'], 'm_conv_w': out['m_conv_w'], 'm_conv_b': out['m_conv_b'], 'm_dt_bias': out['m_dt_bias'], 'm_a_log': out['m_a_log'], 'm_d_skip': out['m_d_skip'], 'm_ssd_norm': out['m_ssd_norm'], 'm_w_proj_pool': out['m_w_proj_pool'], 'm_w_proj_ssd': out['m_w_proj_ssd'], 'm_w_out': out['m_w_out'], 'v_c_ctx': out['v_c_ctx'], 'v_w_ada': out['v_w_ada'], 'v_b_ada': out['v_b_ada'], 'v_norm_pre': out['v_norm_pre'], 'v_norm_post': out['v_norm_post'], 'v_w_in': out['v_w_in'], 'v_b_merge': out['v_b_merge'], 'v_pool_w': out['v_pool_w'], 'v_pool_scale': out['v_pool_scale'], 'v_conv_w': out['v_conv_w'], 'v_conv_b': out['v_conv_b'], 'v_dt_bias': out['v_dt_bias'], 'v_a_log': out['v_a_log'], 'v_d_skip': out['v_d_skip'], 'v_ssd_norm': out['v_ssd_norm'], 'v_w_proj_pool': out['v_w_proj_pool'], 'v_w_proj_ssd': out['v_w_proj_ssd'], 'v_w_out': out['v_w_out']}


def _loss(weights, diff, rest, loss_target):
    with _jax.named_scope("forward"):
        args = {**rest, TWIN_DIFF_INPUT: diff, **{k: w.astype(_WEIGHT_DTYPES[k]) for k, w in weights.items()}}
        y = _forward(args)
    with _jax.named_scope("loss_head"):
        err = _jnp.square(y.astype(_jnp.float32) - loss_target)
        return 0.5 * _jnp.sum(_jnp.mean(err, axis=-1)) if err.ndim else 0.5 * err


def _adamw(w, g, m, v):
    m = ADAM_B1 * m + (1.0 - ADAM_B1) * g
    v = ADAM_B2 * v + (1.0 - ADAM_B2) * _jnp.square(g)
    m_hat = m / (1.0 - ADAM_B1 ** ADAM_STEP)
    v_hat = v / (1.0 - ADAM_B2 ** ADAM_STEP)
    delta = -ADAM_LR * (m_hat / (_jnp.sqrt(v_hat) + ADAM_EPS) + ADAM_WD * w)
    return delta, m, v


def reference(x, c, ctx, c_ctx, w_ada, b_ada, norm_pre, norm_post, w_in, b_merge, pool_w, pool_scale, conv_w, conv_b, dt_bias, a_log, d_skip, ssd_norm, w_proj_pool, w_proj_ssd, w_out, loss_target, m_c_ctx, m_w_ada, m_b_ada, m_norm_pre, m_norm_post, m_w_in, m_b_merge, m_pool_w, m_pool_scale, m_conv_w, m_conv_b, m_dt_bias, m_a_log, m_d_skip, m_ssd_norm, m_w_proj_pool, m_w_proj_ssd, m_w_out, v_c_ctx, v_w_ada, v_b_ada, v_norm_pre, v_norm_post, v_w_in, v_b_merge, v_pool_w, v_pool_scale, v_conv_w, v_conv_b, v_dt_bias, v_a_log, v_d_skip, v_ssd_norm, v_w_proj_pool, v_w_proj_ssd, v_w_out):
    given = dict(x=x, c=c, ctx=ctx, c_ctx=c_ctx, w_ada=w_ada, b_ada=b_ada, norm_pre=norm_pre, norm_post=norm_post, w_in=w_in, b_merge=b_merge, pool_w=pool_w, pool_scale=pool_scale, conv_w=conv_w, conv_b=conv_b, dt_bias=dt_bias, a_log=a_log, d_skip=d_skip, ssd_norm=ssd_norm, w_proj_pool=w_proj_pool, w_proj_ssd=w_proj_ssd, w_out=w_out, loss_target=loss_target, m_c_ctx=m_c_ctx, m_w_ada=m_w_ada, m_b_ada=m_b_ada, m_norm_pre=m_norm_pre, m_norm_post=m_norm_post, m_w_in=m_w_in, m_b_merge=m_b_merge, m_pool_w=m_pool_w, m_pool_scale=m_pool_scale, m_conv_w=m_conv_w, m_conv_b=m_conv_b, m_dt_bias=m_dt_bias, m_a_log=m_a_log, m_d_skip=m_d_skip, m_ssd_norm=m_ssd_norm, m_w_proj_pool=m_w_proj_pool, m_w_proj_ssd=m_w_proj_ssd, m_w_out=m_w_out, v_c_ctx=v_c_ctx, v_w_ada=v_w_ada, v_b_ada=v_b_ada, v_norm_pre=v_norm_pre, v_norm_post=v_norm_post, v_w_in=v_w_in, v_b_merge=v_b_merge, v_pool_w=v_pool_w, v_pool_scale=v_pool_scale, v_conv_w=v_conv_w, v_conv_b=v_conv_b, v_dt_bias=v_dt_bias, v_a_log=v_a_log, v_d_skip=v_d_skip, v_ssd_norm=v_ssd_norm, v_w_proj_pool=v_w_proj_pool, v_w_proj_ssd=v_w_proj_ssd, v_w_out=v_w_out)
    weights = {n: given[n] for n in TWIN_WEIGHTS}
    shared = {n: given[n] for n in SHARED_INPUTS}
    per_example = {n: given[n] for n in ['x', 'c', 'ctx']}
    grad_fn = _jax.value_and_grad(_loss, argnums=(0, 1))

    def one_microbatch(ex, loss_target):
        ex = dict(ex)
        diff = ex.pop(TWIN_DIFF_INPUT)
        return grad_fn(weights, diff, {**shared, **ex}, loss_target)

    if N_MICROBATCH == 1:
        loss, (grad_w, grad_x) = one_microbatch(per_example, given["loss_target"])
    else:
        def body(carry, xs):
            loss_sum, grad_sum = carry
            l_k, (gw_k, gx_k) = one_microbatch(xs[0], xs[1])
            with _jax.named_scope("update"):
                return (loss_sum + l_k, _jax.tree.map(_jnp.add, grad_sum, gw_k)), gx_k

        init = (_jnp.zeros((), _jnp.float32), _jax.tree.map(_jnp.zeros_like, weights))
        (loss, grad_w), grad_x = _jax.lax.scan(body, init, (per_example, given["loss_target"]))
    with _jax.named_scope("update"):
        delta_w, new_m, new_v = {}, {}, {}
        for n in TWIN_WEIGHTS:
            delta_w[n], new_m[n], new_v[n] = _adamw(weights[n], grad_w[n], given["m_" + n], given["v_" + n])
    return (loss, grad_x, *[grad_w[n] for n in TWIN_WEIGHTS], *[delta_w[n] for n in TWIN_WEIGHTS],
            *[new_m[n] for n in TWIN_WEIGHTS], *[new_v[n] for n in TWIN_WEIGHTS])
```

```python
import functools

import numpy as np
import jax
import jax.numpy as jnp
from jax import lax
from jax.experimental import pallas as pl
from jax.experimental.pallas import tpu as pltpu

F32, BF = jnp.float32, jnp.bfloat16

D = 1024
GRID_W = 64
EPS = 1e-6
POOL_WINDOWS = (2, 4, 8, 16)
PGW = 256
DIN = 2048
HEAD = 64
NST = 128
NG = 4
HPG = 8
GWID = HPG * HEAD
Q = 128
CONV_DIM = 3072
OFF_GATE, OFF_XBC, OFF_DT, IN_COLS = 4096, 6144, 9216, 9280
NDEV = 8
ADAM_LR, ADAM_B1, ADAM_B2, ADAM_EPS, ADAM_WD, ADAM_STEP = 0.001, 0.9, 0.999, 1e-08, 0.01, 10

V7X_VMEM_LIMIT = 56 * 2 ** 20
ROW_TILE = 256


def _params(sem=None):
    return pltpu.CompilerParams(dimension_semantics=sem, vmem_limit_bytes=V7X_VMEM_LIMIT)


def _dot(a, b):
    return jnp.dot(a.astype(BF), b.astype(BF), preferred_element_type=F32)


def _dot_nt(a, b):
    return lax.dot_general(a.astype(BF), b.astype(BF), (((1,), (1,)), ((), ())), preferred_element_type=F32)


def _dot_tn(a, b):
    return lax.dot_general(a.astype(BF), b.astype(BF), (((0,), (0,)), ((), ())), preferred_element_type=F32)


def _split(a, n):
    parts = []
    for _ in range(n):
        p = a.astype(BF)
        parts.append(p)
        a = a - p.astype(F32)
    return parts


def _dot_sl(a, b01, n=3):
    return sum(jnp.dot(p, b01, preferred_element_type=F32) for p in _split(a, n))


def _dot_sr(a01, b, n=3):
    return sum(jnp.dot(a01, p, preferred_element_type=F32) for p in _split(b, n))


def _dot_tn_sl(a, b01, n=2):
    return sum(lax.dot_general(p, b01, (((0,), (0,)), ((), ())), preferred_element_type=F32) for p in _split(a, n))


def _sigmoid(x):
    return 1.0 / (1.0 + jnp.exp(-x))


def _matmul(a, b, out_dtype, name, tm=512, tn=512, tk=1024):
    M, K = a.shape
    N = b.shape[1]
    tm, tn, tk = min(tm, M), min(tn, N), min(tk, K)
    assert M % tm == 0 and N % tn == 0 and K % tk == 0, (a.shape, b.shape)
    nk = K // tk

    def body(a_ref, b_ref, o_ref, acc):
        k = pl.program_id(2)
        p = _dot(a_ref[...], b_ref[...])

        @pl.when(k == 0)
        def _():
            acc[...] = p

        @pl.when(k > 0)
        def _():
            acc[...] += p

        @pl.when(k == nk - 1)
        def _():
            o_ref[...] = acc[...].astype(o_ref.dtype)

    return pl.pallas_call(
        body, name=name, grid=(M // tm, N // tn, nk),
        in_specs=[pl.BlockSpec((tm, tk), lambda i, j, k: (i, k)), pl.BlockSpec((tk, tn), lambda i, j, k: (k, j))],
        out_specs=pl.BlockSpec((tm, tn), lambda i, j, k: (i, j)),
        out_shape=jax.ShapeDtypeStruct((M, N), out_dtype),
        scratch_shapes=[pltpu.VMEM((tm, tn), F32)],
        compiler_params=_params(("parallel", "parallel", "arbitrary")),
    )(a, b)


def _matmul_tn(a, g, name, ta=512, tn=512, tr=512):
    M, Ka = a.shape
    N = g.shape[1]
    ta, tn, tr = min(ta, Ka), min(tn, N), min(tr, M)
    assert M % tr == 0 and N % tn == 0 and Ka % ta == 0, (a.shape, g.shape)
    nr = M // tr

    def body(a_ref, g_ref, o_ref):
        k = pl.program_id(2)
        p = _dot_tn(a_ref[...], g_ref[...])

        @pl.when(k == 0)
        def _():
            o_ref[...] = p

        @pl.when(k > 0)
        def _():
            o_ref[...] += p

    return pl.pallas_call(
        body, name=name, grid=(Ka // ta, N // tn, nr),
        in_specs=[pl.BlockSpec((tr, ta), lambda i, j, k: (k, i)), pl.BlockSpec((tr, tn), lambda i, j, k: (k, j))],
        out_specs=pl.BlockSpec((ta, tn), lambda i, j, k: (i, j)),
        out_shape=jax.ShapeDtypeStruct((Ka, N), F32),
        compiler_params=_params(("parallel", "parallel", "arbitrary")),
    )(a, g)


def _adaln_fwd(c16, w_ada_bf, b_ada):
    def body(c_ref, w_ref, b_ref, o_ref):
        cc = c_ref[...]
        o_ref[...] = _dot(cc * _sigmoid(cc), w_ref[...]) + b_ref[...]

    return pl.pallas_call(body, name="adaln_fwd", out_shape=jax.ShapeDtypeStruct((16, 3 * D), F32),
                          compiler_params=_params())(c16, w_ada_bf, b_ada)


def _adaln_bwd(acc_n, acc_f, mod16, c16, norm_pre, w_adaT_bf):
    def body(an_ref, af_ref, mod_ref, c_ref, np_ref, wt_ref, dw_ref, db_ref, sm_ref, dmod):
        npre = np_ref[...]
        dmod[...] = jnp.zeros_like(dmod)
        dnp = jnp.zeros((1, D), F32)
        dshift_c = jnp.zeros((1, D), F32)
        dgpre_c = jnp.zeros((1, D), F32)
        scale_c = mod_ref[2:3, D:2 * D]
        for e in range(2):
            dg_x, ds_x = an_ref[e, 0, 0:1, :], an_ref[e, 0, 1:2, :]
            dg_c, ds_c = an_ref[e, 1, 0:1, :], an_ref[e, 1, 1:2, :]
            dmod[e:e + 1, 0:D] = ds_x
            dmod[e:e + 1, D:2 * D] = dg_x * npre
            dmod[e:e + 1, 2 * D:3 * D] = af_ref[e, 0:1, :]
            dnp = dnp + dg_x * (1.0 + mod_ref[e:e + 1, D:2 * D]) + dg_c * (1.0 + scale_c)
            dshift_c = dshift_c + ds_c
            dgpre_c = dgpre_c + dg_c
        dmod[2:3, 0:D] = dshift_c
        dmod[2:3, D:2 * D] = dgpre_c * npre
        dm = dmod[...]
        cc = c_ref[...]
        sg = _sigmoid(cc)
        dw_ref[...] = _dot_tn(cc * sg, dm)
        db_ref[...] = jnp.zeros_like(db_ref)
        db_ref[0:1, :] = jnp.sum(dm, axis=0, keepdims=True)
        dsilu = sg * (1.0 + cc * (1.0 - sg))
        dcs = _dot(dm, wt_ref[...]) * dsilu
        sm_ref[...] = jnp.zeros_like(sm_ref)
        sm_ref[0:1, :] = dnp
        sm_ref[1:2, :] = dcs[2:3, :]

    return pl.pallas_call(
        body, name="adaln_bwd",
        out_shape=(jax.ShapeDtypeStruct((D, 3 * D), F32), jax.ShapeDtypeStruct((16, 3 * D), F32),
                   jax.ShapeDtypeStruct((8, D), F32)),
        scratch_shapes=[pltpu.VMEM((16, 3 * D), F32)],
        compiler_params=_params())(acc_n, acc_f, mod16, c16, norm_pre, w_adaT_bf)


def _norm_mod_fwd(xc, tab, L):
    _, R, _ = xc.shape
    nx = L // ROW_TILE

    def body(x_ref, t_ref, o_ref):
        x = x_ref[0]
        r = lax.rsqrt(jnp.mean(x * x, axis=-1, keepdims=True) + EPS)
        t = t_ref[0, 0]
        o_ref[0] = (x * r * t[0:1] + t[1:2]).astype(BF)

    return pl.pallas_call(
        body, name="norm_mod_fwd", grid=(2, R // ROW_TILE),
        in_specs=[pl.BlockSpec((1, ROW_TILE, D), lambda e, t: (e, t, 0)),
                  pl.BlockSpec((1, 1, 8, D), lambda e, t: (e, t // nx, 0, 0))],
        out_specs=pl.BlockSpec((1, ROW_TILE, D), lambda e, t: (e, t, 0)),
        out_shape=jax.ShapeDtypeStruct(xc.shape, BF),
        compiler_params=_params(("parallel", "parallel")),
    )(xc, tab)


def _norm_mod_bwd(dh_a, dh_b, xc, tab, dxo, L):
    _, R, _ = xc.shape
    nx = L // ROW_TILE

    def body(da_ref, db_ref, x_ref, t_ref, dxo_ref, gx_ref, acc_ref):
        t = pl.program_id(1)
        x = x_ref[0]
        r = lax.rsqrt(jnp.mean(x * x, axis=-1, keepdims=True) + EPS)
        xn = x * r
        dh = da_ref[0] + db_ref[0]

        @pl.when((t == 0) | (t == nx))
        def _():
            acc_ref[...] = jnp.zeros_like(acc_ref)

        acc_ref[0, 0, 0:1, :] += jnp.sum(dh * xn, axis=0, keepdims=True)
        acc_ref[0, 0, 1:2, :] += jnp.sum(dh, axis=0, keepdims=True)

        @pl.when(t < nx)
        def _():
            dxn = dh * t_ref[0, 0][0:1]
            dx = r * (dxn - xn * jnp.mean(dxn * xn, axis=-1, keepdims=True))
            gx_ref[0] = dxo_ref[0] + dx

    xmap = lambda e, t: (e, jnp.minimum(t, nx - 1), 0)
    return pl.pallas_call(
        body, name="norm_mod_bwd", grid=(2, R // ROW_TILE),
        in_specs=[pl.BlockSpec((1, ROW_TILE, D), lambda e, t: (e, t, 0)),
                  pl.BlockSpec((1, ROW_TILE, D), lambda e, t: (e, t, 0)),
                  pl.BlockSpec((1, ROW_TILE, D), lambda e, t: (e, t, 0)),
                  pl.BlockSpec((1, 1, 8, D), lambda e, t: (e, t // nx, 0, 0)),
                  pl.BlockSpec((1, ROW_TILE, D), xmap)],
        out_specs=(pl.BlockSpec((1, ROW_TILE, D), xmap),
                   pl.BlockSpec((1, 1, 8, D), lambda e, t: (e, t // nx, 0, 0))),
        out_shape=(jax.ShapeDtypeStruct((2, L, D), F32), jax.ShapeDtypeStruct((2, 2, 8, D), F32)),
        compiler_params=_params(("parallel", "arbitrary")),
    )(dh_a, dh_b, xc, tab, dxo)


POOL_TILE = 256
POOL_PAD = 8 * GRID_W


def _pool_tables(L):
    rows = L // GRID_W
    mats = np.zeros((4, POOL_TILE, POOL_TILE), np.float32)
    wts = np.zeros((4, 2, 17), np.float32)
    inv = np.zeros((4, L, 1), np.float32)
    for gi, k in enumerate(POOL_WINDOWS):
        lo, hi = k // 2, k - 1 - k // 2
        m = np.zeros((GRID_W, GRID_W), np.float32)
        for t in range(GRID_W):
            m[t, max(t - lo, 0):min(t + hi, GRID_W - 1) + 1] = 1.0
        for b in range(POOL_TILE // GRID_W):
            mats[gi, b * GRID_W:(b + 1) * GRID_W, b * GRID_W:(b + 1) * GRID_W] = m
        for o in range(-lo, hi + 1):
            wts[gi, 0, o + 8] = 1.0
            wts[gi, 1, -o + 8] = 1.0
        cnt_c = m.sum(1)
        cnt_r = np.array([min(r + hi, rows - 1) - max(r - lo, 0) + 1 for r in range(rows)], np.float32)
        inv[gi, :, 0] = (1.0 / (cnt_r[:, None] * cnt_c[None, :])).reshape(-1)
    matsT = np.ascontiguousarray(np.transpose(mats, (0, 2, 1)))
    return (jnp.asarray(mats, BF), jnp.asarray(matsT, BF), jnp.asarray(wts), jnp.asarray(inv))


def _pool_cols(get_tile, mat, pad_ref, L):
    pad_ref[pl.ds(0, POOL_PAD), :] = jnp.zeros((POOL_PAD, PGW), F32)
    pad_ref[pl.ds(POOL_PAD + L, POOL_PAD), :] = jnp.zeros((POOL_PAD, PGW), F32)

    def step(i, carry):
        off = pl.multiple_of(i * POOL_TILE, POOL_TILE)
        pad_ref[pl.ds(POOL_PAD + off, POOL_TILE), :] = _dot_sr(mat, get_tile(off), 2)
        return carry

    lax.fori_loop(0, L // POOL_TILE, step, 0)


def _pool_rows(pad_ref, wt, off, n):
    acc = jnp.zeros((n, PGW), F32)
    for o in range(-8, 9):
        acc = acc + wt[:, o + 8:o + 9] * pad_ref[pl.ds(POOL_PAD + GRID_W * o + off, n), :]
    return acc


def _pool_fwd(proj3, pool_w_bf, pool_scale, tables, L):
    mats, _, wts, inv = tables
    nt = L // POOL_TILE

    def body(v_ref, z_ref, pw_ref, ps_ref, m_ref, wt_ref, inv_ref, o_ref, pad_ref):
        _pool_cols(lambda off: v_ref[0, pl.ds(off, POOL_TILE), :], m_ref[0], pad_ref, L)
        wt = wt_ref[0, 0:1, :]

        def step(i, carry):
            off = pl.multiple_of(i * POOL_TILE, POOL_TILE)
            rows = pl.ds(off, POOL_TILE)
            v = v_ref[0, rows, :]
            diff = _pool_rows(pad_ref, wt, off, POOL_TILE) * inv_ref[0, rows, :] - v
            yp = _dot(diff, pw_ref[0])
            z = z_ref[0, rows, :]
            o_ref[0, rows, :] = (yp * ps_ref[...] * (z * _sigmoid(z))).astype(BF)
            return carry

        lax.fori_loop(0, nt, step, 0)

    return pl.pallas_call(
        body, name="pool_fwd", grid=(2, 4),
        in_specs=[pl.BlockSpec((1, L, PGW), lambda e, g: (e, 0, g)),
                  pl.BlockSpec((1, L, PGW), lambda e, g: (e, 0, 4 + g)),
                  pl.BlockSpec((1, PGW, PGW), lambda e, g: (g, 0, 0)),
                  pl.BlockSpec((1, PGW), lambda e, g: (0, g)),
                  pl.BlockSpec((1, POOL_TILE, POOL_TILE), lambda e, g: (g, 0, 0)),
                  pl.BlockSpec((1, 2, 17), lambda e, g: (g, 0, 0)),
                  pl.BlockSpec((1, L, 1), lambda e, g: (g, 0, 0))],
        out_specs=pl.BlockSpec((1, L, PGW), lambda e, g: (e, 0, g)),
        out_shape=jax.ShapeDtypeStruct((2, L, D), BF),
        scratch_shapes=[pltpu.VMEM((L + 2 * POOL_PAD, PGW), F32)],
        compiler_params=_params(("parallel", "parallel")),
    )(proj3, proj3, pool_w_bf, pool_scale, mats, wts, inv)


def _pool_bwd(proj3, d_ypool, pool_w_bf, pool_wT_bf, pool_scale, tables, L):
    mats, matsT, wts, inv = tables
    nt = L // POOL_TILE

    def body(v_ref, z_ref, dy_ref, pw_ref, pwt_ref, ps_ref, m_ref, mt_ref, wt_ref, inv_ref,
             dv_ref, dz_ref, dpw_ref, acc_ref, pad_ref, dd_ref):
        e = pl.program_id(1)

        @pl.when(e == 0)
        def _():
            dpw_ref[...] = jnp.zeros_like(dpw_ref)
            acc_ref[...] = jnp.zeros_like(acc_ref)

        _pool_cols(lambda off: v_ref[0, pl.ds(off, POOL_TILE), :], m_ref[0], pad_ref, L)
        wt = wt_ref[0, 0:1, :]
        ps = ps_ref[...]

        def step(i, carry):
            off = pl.multiple_of(i * POOL_TILE, POOL_TILE)
            rows = pl.ds(off, POOL_TILE)
            v = v_ref[0, rows, :]
            diff = _pool_rows(pad_ref, wt, off, POOL_TILE) * inv_ref[0, rows, :] - v
            yp = _dot(diff, pw_ref[0])
            z = z_ref[0, rows, :]
            sg = _sigmoid(z)
            sz = z * sg
            dy = dy_ref[0, rows, :]
            dz_ref[0, rows, :] = (dy * yp * ps * (sg * (1.0 + z * (1.0 - sg)))).astype(BF)
            dys = dy * sz
            acc_ref[0, 0:1, :] += jnp.sum(dys * yp, axis=0, keepdims=True)
            dyp = dys * ps
            dpw_ref[0] += _dot_tn(diff, dyp)
            dd_ref[rows, :] = _dot(dyp, pwt_ref[0])
            return carry

        lax.fori_loop(0, nt, step, 0)
        _pool_cols(lambda off: dd_ref[pl.ds(off, POOL_TILE), :] * inv_ref[0, pl.ds(off, POOL_TILE), :],
                   mt_ref[0], pad_ref, L)
        wtt = wt_ref[0, 1:2, :]

        def step2(i, carry):
            off = pl.multiple_of(i * POOL_TILE, POOL_TILE)
            rows = pl.ds(off, POOL_TILE)
            dv_ref[0, rows, :] = (_pool_rows(pad_ref, wtt, off, POOL_TILE) - dd_ref[rows, :]).astype(BF)
            return carry

        lax.fori_loop(0, nt, step2, 0)

    return pl.pallas_call(
        body, name="pool_bwd", grid=(4, 2),
        in_specs=[pl.BlockSpec((1, L, PGW), lambda g, e: (e, 0, g)),
                  pl.BlockSpec((1, L, PGW), lambda g, e: (e, 0, 4 + g)),
                  pl.BlockSpec((1, L, PGW), lambda g, e: (e, 0, g)),
                  pl.BlockSpec((1, PGW, PGW), lambda g, e: (g, 0, 0)),
                  pl.BlockSpec((1, PGW, PGW), lambda g, e: (g, 0, 0)),
                  pl.BlockSpec((1, PGW), lambda g, e: (0, g)),
                  pl.BlockSpec((1, POOL_TILE, POOL_TILE), lambda g, e: (g, 0, 0)),
                  pl.BlockSpec((1, POOL_TILE, POOL_TILE), lambda g, e: (g, 0, 0)),
                  pl.BlockSpec((1, 2, 17), lambda g, e: (g, 0, 0)),
                  pl.BlockSpec((1, L, 1), lambda g, e: (g, 0, 0))],
        out_specs=(pl.BlockSpec((1, L, PGW), lambda g, e: (e, 0, g)),
                   pl.BlockSpec((1, L, PGW), lambda g, e: (e, 0, g)),
                   pl.BlockSpec((1, PGW, PGW), lambda g, e: (g, 0, 0)),
                   pl.BlockSpec((1, 8, PGW), lambda g, e: (g, 0, 0))),
        out_shape=(jax.ShapeDtypeStruct((2, L, D), BF), jax.ShapeDtypeStruct((2, L, D), BF),
                   jax.ShapeDtypeStruct((4, PGW, PGW), F32), jax.ShapeDtypeStruct((4, 8, PGW), F32)),
        scratch_shapes=[pltpu.VMEM((L + 2 * POOL_PAD, PGW), F32), pltpu.VMEM((L, PGW), F32)],
        compiler_params=_params(("parallel", "arbitrary")),
    )(proj3, proj3, d_ypool, pool_w_bf, pool_wT_bf, pool_scale, mats, matsT, wts, inv)


CONV_BLOCK = 128


def _conv_tap(u, k, L):
    off = k - 2
    if off == 0:
        return u
    R = u.shape[0]
    r = lax.broadcasted_iota(jnp.int32, (R, 1), 0)
    pos = jnp.where(r < L, r, r - L) + off
    seg = jnp.where(r < L, L, R - L)
    return jnp.where((pos >= 0) & (pos < seg), pltpu.roll(u, (-off) % R, 0), 0.0)


def _conv_fwd(proj3, conv_w, conv_b, L):
    _, R, _ = proj3.shape
    cb0 = OFF_XBC // CONV_BLOCK

    def body(u_ref, w_ref, b_ref, o_ref):
        u = u_ref[0]
        w = w_ref[...]
        pre = b_ref[...] + sum(_conv_tap(u, k, L) * w[k:k + 1, :] for k in range(4))
        o_ref[0] = pre * _sigmoid(pre)

    return pl.pallas_call(
        body, name="conv_fwd", grid=(2, CONV_DIM // CONV_BLOCK),
        in_specs=[pl.BlockSpec((1, R, CONV_BLOCK), lambda e, j: (e, 0, cb0 + j)),
                  pl.BlockSpec((4, CONV_BLOCK), lambda e, j: (0, j)),
                  pl.BlockSpec((1, CONV_BLOCK), lambda e, j: (0, j))],
        out_specs=pl.BlockSpec((1, R, CONV_BLOCK), lambda e, j: (e, 0, j)),
        out_shape=jax.ShapeDtypeStruct((2, R, CONV_DIM), F32),
        compiler_params=_params(("parallel", "parallel")),
    )(proj3, conv_w, conv_b)


def _conv_bwd(proj3, addends, col0, ncols, in_maps, conv_w, conv_b, L, name):
    _, R, _ = proj3.shape
    cb0 = (OFF_XBC + col0) // CONV_BLOCK
    wb0 = col0 // CONV_BLOCK
    na = len(addends)

    def body(*refs):
        u_ref, w_ref, b_ref = refs[0], refs[1], refs[2]
        a_refs = refs[3:3 + na]
        o_ref, acc_ref = refs[3 + na], refs[4 + na]
        u = u_ref[0]
        w = w_ref[...]
        taps = [_conv_tap(u, k, L) for k in range(4)]
        pre = b_ref[...] + sum(taps[k] * w[k:k + 1, :] for k in range(4))
        sg = _sigmoid(pre)
        dxbc = a_refs[0][0]
        for a in a_refs[1:]:
            dxbc = dxbc + a[0]
        dpre = dxbc * (sg * (1.0 + pre * (1.0 - sg)))
        acc_ref[...] = jnp.zeros_like(acc_ref)
        for k in range(4):
            acc_ref[0, k:k + 1, :] = jnp.sum(dpre * taps[k], axis=0, keepdims=True)
        acc_ref[0, 4:5, :] = jnp.sum(dpre, axis=0, keepdims=True)
        du = sum(_conv_tap(dpre, 4 - k, L) * w[k:k + 1, :] for k in range(4))
        o_ref[0] = du.astype(BF)

    in_specs = [pl.BlockSpec((1, R, CONV_BLOCK), lambda e, j: (e, 0, cb0 + j)),
                pl.BlockSpec((4, CONV_BLOCK), lambda e, j: (0, wb0 + j)),
                pl.BlockSpec((1, CONV_BLOCK), lambda e, j: (0, wb0 + j))]
    for m in in_maps:
        in_specs.append(pl.BlockSpec((1, R, CONV_BLOCK), functools.partial(lambda e, j, m: (e, 0, m(j)), m=m)))
    return pl.pallas_call(
        body, name=name, grid=(2, ncols // CONV_BLOCK),
        in_specs=in_specs,
        out_specs=(pl.BlockSpec((1, R, CONV_BLOCK), lambda e, j: (e, 0, j)),
                   pl.BlockSpec((1, 8, CONV_BLOCK), lambda e, j: (e, 0, j))),
        out_shape=(jax.ShapeDtypeStruct((2, R, ncols), BF), jax.ShapeDtypeStruct((2, 8, ncols), F32)),
        compiler_params=_params(("parallel", "parallel")),
    )(proj3, conv_w, conv_b, *addends)


def _softplus(x):
    e = jnp.exp(-jnp.abs(x))
    u = 1.0 + e
    return jnp.maximum(x, 0.0) + jnp.where(u == 1.0, e, e * jnp.log(u) / (u - 1.0))


def _dt_fwd(dt_raw, bias128):
    _, R, _ = dt_raw.shape

    def body(x_ref, b_ref, o_ref):
        o_ref[0] = _softplus(x_ref[0] + b_ref[...])

    return pl.pallas_call(
        body, name="dt_fwd", grid=(2,),
        in_specs=[pl.BlockSpec((1, R, 128), lambda e: (e, 0, 0)), pl.BlockSpec((1, 128), lambda e: (0, 0))],
        out_specs=pl.BlockSpec((1, R, 128), lambda e: (e, 0, 0)),
        out_shape=jax.ShapeDtypeStruct(dt_raw.shape, F32),
        compiler_params=_params(("parallel",)),
    )(dt_raw, bias128)


def _dt_bwd(dt_raw, bias128, ddt_f, ddt_b):
    _, R, _ = dt_raw.shape

    def body(x_ref, b_ref, f_ref, g_ref, o_ref, acc_ref):
        d = (f_ref[0] + g_ref[0]) * _sigmoid(x_ref[0] + b_ref[...])
        o_ref[0] = d.astype(BF)
        acc_ref[...] = jnp.zeros_like(acc_ref)
        acc_ref[0, 0:1, :] = jnp.sum(d, axis=0, keepdims=True)

    blk = pl.BlockSpec((1, R, 128), lambda e: (e, 0, 0))
    return pl.pallas_call(
        body, name="dt_bwd", grid=(2,),
        in_specs=[blk, pl.BlockSpec((1, 128), lambda e: (0, 0)), blk, blk],
        out_specs=(blk, pl.BlockSpec((1, 8, 128), lambda e: (e, 0, 0))),
        out_shape=(jax.ShapeDtypeStruct(dt_raw.shape, BF), jax.ShapeDtypeStruct((2, 8, 128), F32)),
        compiler_params=_params(("parallel",)),
    )(dt_raw, bias128, ddt_f, ddt_b)


def _tri(d):
    i = lax.broadcasted_iota(jnp.int32, (Q, Q), 0)
    j = lax.broadcasted_iota(jnp.int32, (Q, Q), 1)
    return (i >= j) if d == 0 else (i <= j)


def _expand_mat(d):
    r = lax.broadcasted_iota(jnp.int32, (128, GWID), 0)
    c = lax.broadcasted_iota(jnp.int32, (128, GWID), 1)
    return (r == d * HPG + jnp.right_shift(c, 6)).astype(BF)


def _reduce_mat(d):
    r = lax.broadcasted_iota(jnp.int32, (GWID, 128), 0)
    c = lax.broadcasted_iota(jnp.int32, (GWID, 128), 1)
    return (c == d * HPG + jnp.right_shift(r, 6)).astype(BF)


def _ssd_chunk(d, dt, A, xs, B, C):
    mask = _tri(d)
    T = mask.astype(BF)
    Tt = _tri(1 - d).astype(BF)
    a = dt * A
    acs = _dot_sr(T, a)
    acsT = _dot_sl(a.T, Tt)
    E = _expand_mat(d)
    dt_e = _dot_sl(dt, E)
    acs_e = _dot_sl(acs, E)
    alast_e = acs_e[Q - 1:Q, :] if d == 0 else acs_e[0:1, :]
    return dict(mask=mask, T=T, Tt=Tt, acsT=acsT, dt_e=dt_e, acs_e=acs_e, lam=jnp.exp(acs_e),
                w=jnp.exp(alast_e - acs_e), decay=jnp.exp(alast_e), xt=xs * dt_e, CB=_dot_nt(C, B))


def _head_decay(q, d, hh):
    col = q["acs_e"][:, hh * HEAD:hh * HEAD + 1]
    row = q["acsT"][d * HPG + hh:d * HPG + hh + 1, :]
    return jnp.exp(jnp.where(q["mask"], col - row, -jnp.inf))


def _chunk_maps(NX, NS):
    cf = lambda s: lax.rem(s + NX, NS)
    cb = lambda s: NS - 1 - s
    return cf, cb


def _ssd_fwd(xbc, dt_loc, a_loc, L):
    _, R, _ = xbc.shape
    NX, NS = L // Q, R // Q
    cf, cb = _chunk_maps(NX, NS)

    def body(xs_f, b_f, c_f, dt_f, xs_b, b_b, c_b, dt_b, a_ref, y_f, hs_f, y_b, hs_b, hT):
        @pl.when(pl.program_id(2) == 0)
        def _():
            hT[...] = jnp.zeros_like(hT)

        A = a_ref[0, 0:1, :]
        lane = lax.broadcasted_iota(jnp.int32, (Q, 128), 1)
        for d, (xs_ref, b_ref, c_ref, dt_ref, y_ref, hs_ref) in enumerate(
                ((xs_f, b_f, c_f, dt_f, y_f, hs_f), (xs_b, b_b, c_b, dt_b, y_b, hs_b))):
            xs, B, C = xs_ref[0], b_ref[0], c_ref[0]
            q = _ssd_chunk(d, dt_ref[0, 0], A, xs, B, C)
            h = hT[d]
            hb = h.astype(BF)
            hs_ref[0, 0] = hb
            parts = []
            for pr in range(HPG // 2):
                xp = q["xt"][:, pr * 128:(pr + 1) * 128].astype(BF)
                r0 = _dot(q["CB"] * _head_decay(q, d, 2 * pr), xp)
                r1 = _dot(q["CB"] * _head_decay(q, d, 2 * pr + 1), xp)
                parts.append(jnp.where(lane < HEAD, r0, r1))
            y_ref[0] = jnp.concatenate(parts, axis=1) + _dot(C, hb) * q["lam"]
            hT[d] = q["decay"] * h + _dot_tn(B, q["xt"] * q["w"])

    def spec(shape, imap):
        return pl.BlockSpec(shape, imap)

    def ins(c):
        return [spec((1, Q, GWID), lambda e, g, s: (e, c(s), g)),
                spec((1, Q, NST), lambda e, g, s: (e, c(s), DIN // NST + g)),
                spec((1, Q, NST), lambda e, g, s: (e, c(s), DIN // NST + NG + g)),
                spec((1, 1, Q, 128), lambda e, g, s: (e, g, c(s), 0))]

    def outs(c):
        return [spec((1, Q, GWID), lambda e, g, s: (e, c(s), g)),
                spec((1, 1, NST, GWID), lambda e, g, s: (e, c(s), 0, g))]

    yshape = jax.ShapeDtypeStruct((2, R, DIN), F32)
    hshape = jax.ShapeDtypeStruct((2, NS, NST, DIN), BF)
    return pl.pallas_call(
        body, name="ssd_fwd", grid=(2, NG, NS),
        in_specs=ins(cf) + ins(cb) + [spec((1, 8, 128), lambda e, g, s: (g, 0, 0))],
        out_specs=tuple(outs(cf) + outs(cb)),
        out_shape=(yshape, hshape, yshape, hshape),
        scratch_shapes=[pltpu.VMEM((2, NST, GWID), F32)],
        compiler_params=_params(("parallel", "parallel", "arbitrary")),
    )(xbc, xbc, xbc, dt_loc, xbc, xbc, xbc, dt_loc, a_loc)


def _ssd_bwd(xbc, dt_loc, a_loc, hs_f, hs_b, dy, L):
    _, R, _ = xbc.shape
    NX, NS = L // Q, R // Q
    cf0, cb0 = _chunk_maps(NX, NS)
    cf = lambda sp: cf0(NS - 1 - sp)
    cb = lambda sp: cb0(NS - 1 - sp)

    def body(xs_f, b_f, c_f, dt_f, hs_f_, dy_f, xs_b, b_b, c_b, dt_b, hs_b_, dy_b, a_ref,
             dxs_f, dbc_f, ddt_f, dxs_b, dbc_b, ddt_b, da_ref, dhT):
        @pl.when(pl.program_id(2) == 0)
        def _():
            dhT[...] = jnp.zeros_like(dhT)
            da_ref[...] = jnp.zeros_like(da_ref)

        A = a_ref[0, 0:1, :]
        lane = lax.broadcasted_iota(jnp.int32, (Q, 128), 1)
        row = lax.broadcasted_iota(jnp.int32, (Q, 128), 0)
        for d, (xs_ref, b_ref, c_ref, dt_ref, hs_ref, dy_ref, dxs_ref, dbc_ref, ddt_ref) in enumerate(
                ((xs_f, b_f, c_f, dt_f, hs_f_, dy_f, dxs_f, dbc_f, ddt_f),
                 (xs_b, b_b, c_b, dt_b, hs_b_, dy_b, dxs_b, dbc_b, ddt_b))):
            xs, B, C, dt = xs_ref[0], b_ref[0], c_ref[0], dt_ref[0, 0]
            q = _ssd_chunk(d, dt, A, xs, B, C)
            xt, lam, w, decay = q["xt"], q["lam"], q["w"], q["decay"]
            H = hs_ref[0, 0]
            dyv = dy_ref[0]
            dh = dhT[d]
            dacs_e = dyv * (_dot(C, H) * lam)
            dZ = dyv * lam
            dC = _dot_nt(dZ, H)
            dH = _dot_tn(C, dZ)
            U = _dot(B, dh)
            xw = xt * w
            dxt = U * w
            Wt = U * xw
            dacs_e = dacs_e - Wt
            dalast_e = (jnp.sum(Wt, axis=0, keepdims=True)
                        + decay * jnp.sum(dh * H.astype(F32), axis=0, keepdims=True))
            dB = _dot_nt(xw, dh)
            dCB = jnp.zeros((Q, Q), F32)
            dacs = jnp.zeros((Q, 128), F32)
            dxt_parts = []
            for pr in range(HPG // 2):
                xp = xt[:, pr * 128:(pr + 1) * 128]
                dyp = dyv[:, pr * 128:(pr + 1) * 128]
                dxp = jnp.zeros((Q, 128), F32)
                for h2 in range(2):
                    hh = 2 * pr + h2
                    Lh = _head_decay(q, d, hh)
                    M = q["CB"] * Lh
                    dym = jnp.where((lane < HEAD) if h2 == 0 else (lane >= HEAD), dyp, 0.0)
                    dM = _dot_nt(dym, xp)
                    dxp = dxp + _dot_tn(M, dym)
                    G = dM * M
                    sel = (lane == d * HPG + hh).astype(BF)
                    dacs = dacs + _dot_sl(G, sel, 2) - _dot_tn_sl(G, sel, 2)
                    dCB = dCB + dM * Lh
                dxt_parts.append(dxp)
            dxt = dxt + jnp.concatenate(dxt_parts, axis=1)
            dC = dC + _dot(dCB, B)
            dB = dB + _dot_tn(dCB, C)
            Rm = _reduce_mat(d)
            dacs = dacs + _dot_sl(dacs_e, Rm, 2)
            dal = _dot_sl(jnp.broadcast_to(dalast_e, (8, GWID)), Rm, 2)[0:1, :]
            dacs = dacs + jnp.where(row == (Q - 1 if d == 0 else 0), dal, 0.0)
            da = _dot_sr(q["Tt"], dacs, 2)
            ddt_ref[0, 0] = da * A + _dot_sl(dxt * xs, Rm, 2)
            da_ref[0, 0, 0:1, :] += jnp.sum(da * dt, axis=0, keepdims=True)
            dxs_ref[0] = dxt * q["dt_e"]
            dbc_ref[0] = jnp.concatenate([dB, dC], axis=1)
            dhT[d] = decay * dh + dH

    def spec(shape, imap):
        return pl.BlockSpec(shape, imap)

    def ins(c):
        return [spec((1, Q, GWID), lambda e, g, s: (e, c(s), g)),
                spec((1, Q, NST), lambda e, g, s: (e, c(s), DIN // NST + g)),
                spec((1, Q, NST), lambda e, g, s: (e, c(s), DIN // NST + NG + g)),
                spec((1, 1, Q, 128), lambda e, g, s: (e, g, c(s), 0)),
                spec((1, 1, NST, GWID), lambda e, g, s: (e, c(s), 0, g)),
                spec((1, Q, GWID), lambda e, g, s: (e, c(s), g))]

    def outs(c):
        return [spec((1, Q, GWID), lambda e, g, s: (e, c(s), g)),
                spec((1, Q, 2 * NST), lambda e, g, s: (e, c(s), g)),
                spec((1, 1, Q, 128), lambda e, g, s: (e, g, c(s), 0))]

    s_xs = jax.ShapeDtypeStruct((2, R, DIN), F32)
    s_bc = jax.ShapeDtypeStruct((2, R, 2 * NG * NST), F32)
    s_dt = jax.ShapeDtypeStruct((2, NG, R, 128), F32)
    return pl.pallas_call(
        body, name="ssd_bwd", grid=(2, NG, NS),
        in_specs=ins(cf) + ins(cb) + [spec((1, 8, 128), lambda e, g, s: (g, 0, 0))],
        out_specs=tuple(outs(cf) + outs(cb) + [spec((1, 1, 8, 128), lambda e, g, s: (e, g, 0, 0))]),
        out_shape=(s_xs, s_bc, s_dt, s_xs, s_bc, s_dt, jax.ShapeDtypeStruct((2, NG, 8, 128), F32)),
        scratch_shapes=[pltpu.VMEM((2, NST, GWID), F32)],
        compiler_params=_params(("parallel", "parallel", "arbitrary")),
    )(xbc, xbc, xbc, dt_loc, hs_f, dy, xbc, xbc, xbc, dt_loc, hs_b, dy, a_loc)


def _ssd_post_fwd(y_f, y_b, xbc, proj3, dskip_e, ssd_norm, L):
    def body(yf_ref, yb_ref, xs_ref, z_ref, ds_ref, w_ref, o_ref):
        y2 = yf_ref[0] + yb_ref[0] + ds_ref[...] * xs_ref[0]
        z = z_ref[0]
        u = y2 * (z * _sigmoid(z))
        parts = []
        for g in range(NG):
            ug = u[:, g * GWID:(g + 1) * GWID]
            parts.append(ug * lax.rsqrt(jnp.mean(ug * ug, axis=-1, keepdims=True) + EPS))
        o_ref[0] = (jnp.concatenate(parts, axis=1) * w_ref[...]).astype(BF)

    blk = lambda c: pl.BlockSpec((1, ROW_TILE, DIN), lambda e, t: (e, t, c))
    vec = pl.BlockSpec((1, DIN), lambda e, t: (0, 0))
    return pl.pallas_call(
        body, name="ssd_post_fwd", grid=(2, L // ROW_TILE),
        in_specs=[blk(0), blk(0), blk(0), blk(1), vec, vec],
        out_specs=blk(0),
        out_shape=jax.ShapeDtypeStruct((2, L, DIN), BF),
        compiler_params=_params(("parallel", "parallel")),
    )(y_f, y_b, xbc, proj3, dskip_e, ssd_norm)


def _ssd_post_bwd(d_yn, y_f, y_b, xbc, proj3, dskip_e, ssd_norm, L):
    _, R, _ = y_f.shape
    nx = L // ROW_TILE

    def body(dyn_ref, yf_ref, yb_ref, xs_ref, z_ref, ds_ref, w_ref, dy_ref, dsk_ref, dz_ref, acc_ref):
        t = pl.program_id(1)

        @pl.when(t == 0)
        def _():
            acc_ref[...] = jnp.zeros_like(acc_ref)

        @pl.when(t >= nx)
        def _():
            dy_ref[...] = jnp.zeros_like(dy_ref)
            dsk_ref[...] = jnp.zeros_like(dsk_ref)
            dz_ref[...] = jnp.zeros_like(dz_ref)

        @pl.when(t < nx)
        def _():
            xs = xs_ref[0]
            y2 = yf_ref[0] + yb_ref[0] + ds_ref[...] * xs
            z = z_ref[0]
            sg = _sigmoid(z)
            sz = z * sg
            u = y2 * sz
            dyn = dyn_ref[0]
            dun = dyn * w_ref[...]
            uh_parts, du_parts = [], []
            for g in range(NG):
                sl = slice(g * GWID, (g + 1) * GWID)
                ug = u[:, sl]
                rg = lax.rsqrt(jnp.mean(ug * ug, axis=-1, keepdims=True) + EPS)
                uh = ug * rg
                dg = dun[:, sl]
                du_parts.append(rg * (dg - uh * jnp.mean(dg * uh, axis=-1, keepdims=True)))
                uh_parts.append(uh)
            du = jnp.concatenate(du_parts, axis=1)
            uh = jnp.concatenate(uh_parts, axis=1)
            dy2 = du * sz
            dy_ref[0] = dy2
            dsk_ref[0] = dy2 * ds_ref[...]
            dz_ref[0] = (du * y2 * (sg * (1.0 + z * (1.0 - sg)))).astype(BF)
            acc_ref[0, 0:1, :] += jnp.sum(dyn * uh, axis=0, keepdims=True)
            acc_ref[0, 1:2, :] += jnp.sum(dy2 * xs, axis=0, keepdims=True)

    xmap = lambda c: (lambda e, t: (e, jnp.minimum(t, nx - 1), c))
    blk = lambda c: pl.BlockSpec((1, ROW_TILE, DIN), xmap(c))
    oblk = pl.BlockSpec((1, ROW_TILE, DIN), lambda e, t: (e, t, 0))
    vec = pl.BlockSpec((1, DIN), lambda e, t: (0, 0))
    return pl.pallas_call(
        body, name="ssd_post_bwd", grid=(2, R // ROW_TILE),
        in_specs=[blk(0), blk(0), blk(0), blk(0), blk(1), vec, vec],
        out_specs=(oblk, oblk, oblk, pl.BlockSpec((1, 8, DIN), lambda e, t: (e, 0, 0))),
        out_shape=(jax.ShapeDtypeStruct((2, R, DIN), F32), jax.ShapeDtypeStruct((2, R, DIN), F32),
                   jax.ShapeDtypeStruct((2, R, DIN), BF),
                   jax.ShapeDtypeStruct((2, 8, DIN), F32)),
        compiler_params=_params(("parallel", "arbitrary")),
    )(d_yn, y_f, y_b, xbc, proj3, dskip_e, ssd_norm)


def _merge_fwd(proj3, P, S, b_merge, L):
    def body(gp_ref, p_ref, s_ref, b_ref, o_ref):
        gt = _sigmoid(gp_ref[0] + b_ref[...])
        o_ref[0] = (gt[:, :D] * p_ref[0] + gt[:, D:] * s_ref[0]).astype(BF)

    blk = pl.BlockSpec((1, ROW_TILE, D), lambda e, t: (e, t, 0))
    return pl.pallas_call(
        body, name="merge_fwd", grid=(2, L // ROW_TILE),
        in_specs=[pl.BlockSpec((1, ROW_TILE, 2 * D), lambda e, t: (e, t, OFF_GATE // (2 * D))), blk, blk,
                  pl.BlockSpec((1, 2 * D), lambda e, t: (0, 0))],
        out_specs=blk, out_shape=jax.ShapeDtypeStruct((2, L, D), BF),
        compiler_params=_params(("parallel", "parallel")),
    )(proj3, P, S, b_merge)


def _merge_bwd(d_merged, proj3, P, S, b_merge, L):
    _, R, _ = proj3.shape
    nx = L // ROW_TILE

    def body(dm_ref, gp_ref, p_ref, s_ref, b_ref, dp_ref, ds_ref, dg_ref, acc_ref):
        t = pl.program_id(1)

        @pl.when(t == 0)
        def _():
            acc_ref[...] = jnp.zeros_like(acc_ref)

        @pl.when(t >= nx)
        def _():
            dg_ref[...] = jnp.zeros_like(dg_ref)

        @pl.when(t < nx)
        def _():
            gt = _sigmoid(gp_ref[0] + b_ref[...])
            dm = dm_ref[0]
            g1, g2 = gt[:, :D], gt[:, D:]
            dp_ref[0] = (dm * g1).astype(BF)
            ds_ref[0] = (dm * g2).astype(BF)
            dgp = jnp.concatenate([dm * p_ref[0] * g1 * (1.0 - g1), dm * s_ref[0] * g2 * (1.0 - g2)], axis=1)
            dg_ref[0] = dgp.astype(BF)
            acc_ref[0, 0:1, :] += jnp.sum(dgp, axis=0, keepdims=True)

    xmap = lambda e, t: (e, jnp.minimum(t, nx - 1), 0)
    blk = pl.BlockSpec((1, ROW_TILE, D), xmap)
    return pl.pallas_call(
        body, name="merge_bwd", grid=(2, R // ROW_TILE),
        in_specs=[blk, pl.BlockSpec((1, ROW_TILE, 2 * D), lambda e, t: (e, jnp.minimum(t, nx - 1), OFF_GATE // (2 * D))),
                  blk, blk, pl.BlockSpec((1, 2 * D), lambda e, t: (0, 0))],
        out_specs=(blk, blk, pl.BlockSpec((1, ROW_TILE, 2 * D), lambda e, t: (e, t, 0)),
                   pl.BlockSpec((1, 8, 2 * D), lambda e, t: (e, 0, 0))),
        out_shape=(jax.ShapeDtypeStruct((2, L, D), BF), jax.ShapeDtypeStruct((2, L, D), BF),
                   jax.ShapeDtypeStruct((2, R, 2 * D), BF), jax.ShapeDtypeStruct((2, 8, 2 * D), F32)),
        compiler_params=_params(("parallel", "arbitrary")),
    )(d_merged, proj3, P, S, b_merge)


def _final(out3, x, tgt, gtab, norm_post, L):
    def body(o_ref, x_ref, t_ref, g_ref, n_ref, dxo_ref, do_ref, acc_ref):
        @pl.when(pl.program_id(1) == 0)
        def _():
            acc_ref[...] = jnp.zeros_like(acc_ref)

        o = o_ref[0]
        gate = g_ref[0, 0:1, :]
        npost = n_ref[...]
        r2 = lax.rsqrt(jnp.mean(o * o, axis=-1, keepdims=True) + EPS)
        nh = o * r2
        on = nh * npost
        err = x_ref[0] + gate * on - t_ref[0]
        dxo = err * (1.0 / D)
        dxo_ref[0] = dxo
        dnh = dxo * gate * npost
        do_ref[0] = (r2 * (dnh - nh * jnp.mean(dnh * nh, axis=-1, keepdims=True))).astype(BF)
        acc_ref[0, 0:1, :] += jnp.sum(dxo * on, axis=0, keepdims=True)
        acc_ref[0, 1:2, :] += jnp.sum(dxo * gate * nh, axis=0, keepdims=True)
        acc_ref[0, 2:3, :] += jnp.sum(err * err, axis=0, keepdims=True)

    blk = pl.BlockSpec((1, ROW_TILE, D), lambda e, t: (e, t, 0))
    return pl.pallas_call(
        body, name="final", grid=(2, L // ROW_TILE),
        in_specs=[blk, blk, blk, pl.BlockSpec((1, 8, D), lambda e, t: (e, 0, 0)),
                  pl.BlockSpec((1, D), lambda e, t: (0, 0))],
        out_specs=(blk, blk, pl.BlockSpec((1, 8, D), lambda e, t: (e, 0, 0))),
        out_shape=(jax.ShapeDtypeStruct((2, L, D), F32), jax.ShapeDtypeStruct((2, L, D), BF),
                   jax.ShapeDtypeStruct((2, 8, D), F32)),
        compiler_params=_params(("parallel", "arbitrary")),
    )(out3, x, tgt, gtab, norm_post)


def _local_step(x, c, ctx, loss_target, W):
    nb, L, _ = x.shape
    LC = ctx.shape[1]
    R = L + LC
    assert nb == 2 and L % ROW_TILE == 0 and LC % Q == 0 and L % POOL_TILE == 0
    w_in = W["w_in"]
    w_main = w_in[:, :OFF_DT]
    w_dt = jnp.pad(w_in[:, OFF_DT:], ((0, 0), (0, 64)))
    tables = _pool_tables(L)

    c16 = jnp.zeros((16, D), F32).at[0:2].set(c).at[2].set(W["c_ctx"])
    mod16 = _adaln_fwd(c16, W["w_ada"], W["b_ada"])
    shift, scale, gate = mod16[:, :D], mod16[:, D:2 * D], mod16[:, 2 * D:]
    npre = W["norm_pre"]
    tab = jnp.zeros((2, 2, 8, D), F32)
    for e in range(2):
        tab = tab.at[e, 0, 0].set(npre[0] * (1.0 + scale[e])).at[e, 0, 1].set(shift[e])
        tab = tab.at[e, 1, 0].set(npre[0] * (1.0 + scale[2])).at[e, 1, 1].set(shift[2])
    gtab = jnp.zeros((2, 8, D), F32).at[:, 0].set(gate[0:2])

    xc = jnp.concatenate([x, ctx], axis=1)
    hx = _norm_mod_fwd(xc, tab, L)
    hx2 = hx.reshape(2 * R, D)
    proj3 = _matmul(hx2, w_main, F32, "proj_main").reshape(2, R, OFF_DT)
    dt_raw = _matmul(hx2, w_dt, F32, "proj_dt").reshape(2, R, 128)
    ypool = _pool_fwd(proj3, W["pool_w"], W["pool_scale"], tables, L)
    xbc = _conv_fwd(proj3, W["conv_w"], W["conv_b"], L)
    bias128 = jnp.pad(W["dt_bias"].reshape(1, 64), ((0, 0), (0, 64)))
    dt = _dt_fwd(dt_raw, bias128)
    to_loc = lambda t: jnp.pad(t[:, :, :64].reshape(2, R, 2, NG, HPG).transpose(0, 3, 1, 2, 4).reshape(2, NG, R, 16),
                               ((0, 0), (0, 0), (0, 0), (0, 112)))
    from_loc = lambda t: jnp.pad(t[..., :16].reshape(2, NG, R, 2, HPG).transpose(0, 2, 3, 1, 4).reshape(2, R, 64),
                                 ((0, 0), (0, 0), (0, 64)))
    dt_loc = to_loc(dt)
    A = -jnp.exp(W["a_log"].reshape(2, NG, HPG))
    a_loc = jnp.zeros((NG, 8, 128), F32).at[:, 0, :16].set(A.transpose(1, 0, 2).reshape(NG, 16))
    y_f, hs_f, y_b, hs_b = _ssd_fwd(xbc, dt_loc, a_loc, L)
    dskip_e = jnp.repeat(W["d_skip"].reshape(1, 32), HEAD, axis=1)
    yn = _ssd_post_fwd(y_f, y_b, xbc, proj3, dskip_e, W["ssd_norm"], L)
    ypool2, yn2 = ypool.reshape(2 * L, D), yn.reshape(2 * L, DIN)
    P = _matmul(ypool2, W["w_proj_pool"], F32, "proj_pool").reshape(2, L, D)
    S = _matmul(yn2, W["w_proj_ssd"], F32, "proj_ssd").reshape(2, L, D)
    merged = _merge_fwd(proj3, P, S, W["b_merge"], L)
    merged2 = merged.reshape(2 * L, D)
    out3 = _matmul(merged2, W["w_out"], F32, "proj_out").reshape(2, L, D)
    dxo, dout, acc_f = _final(out3, x, loss_target, gtab, W["norm_post"], L)

    dout2 = dout.reshape(2 * L, D)
    g = {}
    g["w_out"] = _matmul_tn(merged2, dout2, "dw_out")
    d_merged = _matmul(dout2, W["w_out"].T, F32, "d_merged").reshape(2, L, D)
    dP, dS, dgp, acc_m = _merge_bwd(d_merged, proj3, P, S, W["b_merge"], L)
    dP2, dS2 = dP.reshape(2 * L, D), dS.reshape(2 * L, D)
    g["w_proj_pool"] = _matmul_tn(ypool2, dP2, "dw_proj_pool")
    g["w_proj_ssd"] = _matmul_tn(yn2, dS2, "dw_proj_ssd")
    d_ypool = _matmul(dP2, W["w_proj_pool"].T, F32, "d_ypool").reshape(2, L, D)
    d_yn = _matmul(dS2, W["w_proj_ssd"].T, F32, "d_yn").reshape(2, L, DIN)
    dv, dzp, g["pool_w"], acc_p = _pool_bwd(proj3, d_ypool, W["pool_w"], jnp.swapaxes(W["pool_w"], 1, 2),
                                            W["pool_scale"], tables, L)
    dy2, dxs_skip, dzs, acc_s = _ssd_post_bwd(d_yn, y_f, y_b, xbc, proj3, dskip_e, W["ssd_norm"], L)
    dxs_f, dbc_f, ddt_f, dxs_b, dbc_b, ddt_b, acc_a = _ssd_bwd(xbc, dt_loc, a_loc, hs_f, hs_b, dy2, L)
    ident = lambda j: j
    dxr_xs, acc_cx = _conv_bwd(proj3, [dxs_f, dxs_b, dxs_skip], 0, DIN, [ident, ident, ident],
                               W["conv_w"], W["conv_b"], L, "conv_bwd_xs")
    bcmap = lambda j: 2 * lax.rem(j, NG) + j // NG
    dxr_bc, acc_cb = _conv_bwd(proj3, [dbc_f, dbc_b], DIN, 2 * NG * NST, [bcmap, bcmap],
                               W["conv_w"], W["conv_b"], L, "conv_bwd_bc")
    ddtr, acc_d = _dt_bwd(dt_raw, bias128, from_loc(ddt_f), from_loc(ddt_b))
    padr = lambda t: jnp.pad(t, ((0, 0), (0, LC), (0, 0)))
    dproj2 = jnp.concatenate([padr(dv), padr(dzp), dzs, dgp, dxr_xs, dxr_bc], axis=2).reshape(2 * R, OFF_DT)
    ddtr2 = ddtr.reshape(2 * R, 128)
    g["w_in"] = jnp.concatenate([_matmul_tn(hx2, dproj2, "dw_in_main"), _matmul_tn(hx2, ddtr2, "dw_in_dt")[:, :64]], axis=1)
    dh_a = _matmul(dproj2, w_main.T, F32, "d_hx_main").reshape(2, R, D)
    dh_b = _matmul(ddtr2, w_dt.T, F32, "d_hx_dt").reshape(2, R, D)
    grad_x, acc_n = _norm_mod_bwd(dh_a, dh_b, xc, tab, dxo, L)
    g["w_ada"], db_rows, sm_rows = _adaln_bwd(acc_n, acc_f, mod16, c16, npre, W["w_ada"].T)

    g["b_ada"] = db_rows[0:1]
    g["norm_pre"] = sm_rows[0:1]
    g["c_ctx"] = sm_rows[1]
    g["norm_post"] = acc_f[0, 1:2] + acc_f[1, 1:2]
    g["b_merge"] = acc_m[0, 0:1] + acc_m[1, 0:1]
    g["pool_scale"] = acc_p[:, 0, :].reshape(1, D)
    acc_c = jnp.concatenate([acc_cx[0] + acc_cx[1], acc_cb[0] + acc_cb[1]], axis=1)
    g["conv_w"] = acc_c[0:4]
    g["conv_b"] = acc_c[4:5]
    g["dt_bias"] = (acc_d[0, 0, :64] + acc_d[1, 0, :64]).reshape(2, 32)
    dA = (acc_a[0, :, 0, :16] + acc_a[1, :, 0, :16]).reshape(NG, 2, HPG).transpose(1, 0, 2)
    g["a_log"] = (dA * A).reshape(2, 32)
    g["d_skip"] = (acc_s[0, 1] + acc_s[1, 1]).reshape(32, HEAD).sum(axis=1).reshape(1, 32)
    g["ssd_norm"] = acc_s[0, 0:1] + acc_s[1, 0:1]
    loss_lanes = acc_f[:, 2, :]
    return loss_lanes, grad_x, g


MESH = pl.DeviceIdType.MESH
ANY = pl.BlockSpec(memory_space=pl.ANY)


def _all_gather(shard):
    m_per, n = shard.shape

    def body(x_ref, out_ref, send_sems, recv_sems, local_sem):
        x, y, c = lax.axis_index("x"), lax.axis_index("y"), lax.axis_index("c")
        me, sibling = (x, y, c), (x, y, 1 - c)
        chips = [(1 - x, y), (x, 1 - y), (1 - x, 1 - y)]

        def rows(px, py, pc):
            return out_ref.at[pl.ds((4 * px + 2 * py + pc) * m_per, m_per), :]

        def copy(k, block, to, src=None):
            return pltpu.make_async_remote_copy(
                src_ref=rows(*block) if src is None else src, dst_ref=rows(*block),
                send_sem=send_sems.at[k], recv_sem=recv_sems.at[k], device_id=to, device_id_type=MESH)

        mine = pltpu.make_async_copy(x_ref, rows(*me), local_sem)
        mine.start()
        first = [copy(0, me, sibling, src=x_ref)]
        first += [copy(1 + j, me, (*chip, c), src=x_ref) for j, chip in enumerate(chips)]
        for cp in first:
            cp.start()
        passed = [copy(4 + j, (*chip, c), sibling) for j, chip in enumerate(chips)]
        for j, chip in enumerate(chips):
            copy(1 + j, (*chip, c), me).wait_recv()
            passed[j].start()
        copy(0, sibling, me).wait_recv()
        for j, chip in enumerate(chips):
            copy(4 + j, (*chip, 1 - c), me).wait_recv()
        for cp in first + passed:
            cp.wait_send()
        mine.wait()

    return pl.pallas_call(
        body, name="all_gather_weights",
        out_shape=jax.ShapeDtypeStruct((NDEV * m_per, n), shard.dtype),
        in_specs=[ANY], out_specs=ANY,
        scratch_shapes=[pltpu.SemaphoreType.DMA((7,)), pltpu.SemaphoreType.DMA((7,)), pltpu.SemaphoreType.DMA],
    )(shard)


def _all_to_all(parts):
    def body(in_ref, out_ref, send_sems, recv_sems, local_sem):
        x, y, c = lax.axis_index("x"), lax.axis_index("y"), lax.axis_index("c")
        me = 4 * x + 2 * y + c
        mine = pltpu.make_async_copy(in_ref.at[me], out_ref.at[me], local_sem)
        mine.start()

        def copy(k):
            px = 1 - x if k & 4 else x
            py = 1 - y if k & 2 else y
            pc = 1 - c if k & 1 else c
            peer = 4 * px + 2 * py + pc
            return (pltpu.make_async_remote_copy(
                src_ref=in_ref.at[peer], dst_ref=out_ref.at[me], send_sem=send_sems.at[k - 1],
                recv_sem=recv_sems.at[k - 1], device_id=(px, py, pc), device_id_type=MESH),
                pltpu.make_async_remote_copy(
                src_ref=in_ref.at[peer], dst_ref=out_ref.at[peer], send_sem=send_sems.at[k - 1],
                recv_sem=recv_sems.at[k - 1], device_id=(px, py, pc), device_id_type=MESH))

        copies = [copy(k) for k in range(1, NDEV)]
        for send, _ in copies:
            send.start()
        for send, _ in copies:
            send.wait_send()
        for _, recv in copies:
            recv.wait_recv()
        mine.wait()

    return pl.pallas_call(
        body, name="all_to_all_grads",
        out_shape=jax.ShapeDtypeStruct(parts.shape, parts.dtype),
        in_specs=[ANY], out_specs=ANY,
        scratch_shapes=[pltpu.SemaphoreType.DMA((7,)), pltpu.SemaphoreType.DMA((7,)), pltpu.SemaphoreType.DMA],
    )(parts)


ADAM_TILE = 64
PACK_W = 1024


def _adamw(recv, w, m, v):
    rp = w.shape[0]

    def body(r_ref, w_ref, m_ref, v_ref, g_ref, d_ref, nm_ref, nv_ref):
        g = r_ref[0]
        for i in range(1, NDEV):
            g = g + r_ref[i]
        m1 = ADAM_B1 * m_ref[...] + (1.0 - ADAM_B1) * g
        v1 = ADAM_B2 * v_ref[...] + (1.0 - ADAM_B2) * (g * g)
        m_hat = m1 / (1.0 - ADAM_B1 ** ADAM_STEP)
        v_hat = v1 / (1.0 - ADAM_B2 ** ADAM_STEP)
        g_ref[...] = g
        d_ref[...] = -ADAM_LR * (m_hat / (jnp.sqrt(v_hat) + ADAM_EPS) + ADAM_WD * w_ref[...])
        nm_ref[...] = m1
        nv_ref[...] = v1

    blk = pl.BlockSpec((ADAM_TILE, PACK_W), lambda i: (i, 0))
    shp = jax.ShapeDtypeStruct((rp, PACK_W), F32)
    return pl.pallas_call(
        body, name="adamw", grid=(rp // ADAM_TILE,),
        in_specs=[pl.BlockSpec((NDEV, ADAM_TILE, PACK_W), lambda i: (0, i, 0)), blk, blk, blk],
        out_specs=(blk, blk, blk, blk), out_shape=(shp, shp, shp, shp),
        compiler_params=_params(("parallel",)),
    )(recv, w, m, v)


BIG = {"w_ada": ((D, 3 * D), 1), "w_in": ((D, IN_COLS), 1), "pool_w": ((4, PGW, PGW), 1), "conv_w": ((4, CONV_DIM), 1),
       "w_proj_pool": ((D, D), 0), "w_proj_ssd": ((DIN, D), 0), "w_out": ((D, D), 0)}
SMALL = {"c_ctx": (D,), "b_ada": (1, 3 * D), "norm_pre": (1, D), "norm_post": (1, D), "b_merge": (1, 2 * D),
         "pool_scale": (1, D), "conv_b": (1, CONV_DIM), "dt_bias": (2, 32), "a_log": (2, 32), "d_skip": (1, 32),
         "ssd_norm": (1, DIN)}
LOSS_SLOT = 128
GATHER_BF16 = ("w_ada", "w_in", "pool_w", "w_proj_pool", "w_proj_ssd", "w_out")


def _shard_shape(name):
    shape, ax = BIG[name]
    return tuple(s // NDEV if i == ax else s for i, s in enumerate(shape))


def _to_chunks(full, name):
    shape, ax = BIG[name]
    split = shape[:ax] + (NDEV, shape[ax] // NDEV) + shape[ax + 1:]
    return jnp.moveaxis(full.reshape(split), ax, 0).reshape(NDEV, -1)


def _from_chunks(chunks, name):
    shape, ax = BIG[name]
    sh = _shard_shape(name)
    return jnp.moveaxis(chunks.reshape((NDEV,) + sh), 0, ax).reshape(shape)


def _pad_rows(flat, mult):
    n = flat.shape[-1]
    rows = -(-n // (PACK_W * mult)) * mult
    pad = [(0, 0)] * (flat.ndim - 1) + [(0, rows * PACK_W - n)]
    return jnp.pad(flat, pad).reshape(flat.shape[:-1] + (rows, PACK_W))


def _pack_state(t, extra=None):
    flat = [t[n].reshape(-1) for n in BIG] + [t[n].reshape(-1) for n in SMALL]
    flat.append(jnp.zeros((LOSS_SLOT,), F32) if extra is None else extra)
    return _pad_rows(jnp.concatenate(flat), ADAM_TILE)


def _pack_grads(g, loss_part):
    small = [g[n].reshape(-1) for n in SMALL] + [jnp.zeros((LOSS_SLOT,), F32).at[0].set(loss_part)]
    small = jnp.broadcast_to(jnp.concatenate(small)[None], (NDEV, sum(int(np.prod(SMALL[n])) for n in SMALL) + LOSS_SLOT))
    return _pad_rows(jnp.concatenate([_to_chunks(g[n], n) for n in BIG] + [small], axis=1), ADAM_TILE)


def _unpack_state(packed):
    flat = packed.reshape(-1)
    out, off = {}, 0
    for n in BIG:
        sh = _shard_shape(n)
        k = int(np.prod(sh))
        out[n] = flat[off:off + k].reshape(sh)
        off += k
    for n, sh in SMALL.items():
        k = int(np.prod(sh))
        out[n] = flat[off:off + k].reshape(sh)
        off += k
    out["loss"] = flat[off]
    return out


def _pack_gather(w):
    flat = [w[n].reshape(-1).astype(BF) for n in GATHER_BF16] + [p.reshape(-1) for p in _split(w["conv_w"], 3)]
    return _pad_rows(jnp.concatenate(flat), 16)


def _unpack_gather(gathered):
    g = gathered.reshape(NDEV, -1)
    out, off = {}, 0
    for n in GATHER_BF16:
        k = int(np.prod(_shard_shape(n)))
        out[n] = _from_chunks(g[:, off:off + k], n)
        off += k
    k = int(np.prod(_shard_shape("conv_w")))
    parts = [_from_chunks(g[:, off + i * k:off + (i + 1) * k], "conv_w").astype(F32) for i in range(3)]
    out["conv_w"] = parts[0] + parts[1] + parts[2]
    return out


PARAMS = ["c_ctx", "w_ada", "b_ada", "norm_pre", "norm_post", "w_in", "b_merge", "pool_w", "pool_scale", "conv_w", "conv_b",
          "dt_bias", "a_log", "d_skip", "ssd_norm", "w_proj_pool", "w_proj_ssd", "w_out"]


def kernel(x, c, ctx, c_ctx, w_ada, b_ada, norm_pre, norm_post, w_in, b_merge, pool_w, pool_scale, conv_w, conv_b, dt_bias, a_log, d_skip, ssd_norm, w_proj_pool, w_proj_ssd, w_out, loss_target, m_c_ctx, m_w_ada, m_b_ada, m_norm_pre, m_norm_post, m_w_in, m_b_merge, m_pool_w, m_pool_scale, m_conv_w, m_conv_b, m_dt_bias, m_a_log, m_d_skip, m_ssd_norm, m_w_proj_pool, m_w_proj_ssd, m_w_out, v_c_ctx, v_w_ada, v_b_ada, v_norm_pre, v_norm_post, v_w_in, v_b_merge, v_pool_w, v_pool_scale, v_conv_w, v_conv_b, v_dt_bias, v_a_log, v_d_skip, v_ssd_norm, v_w_proj_pool, v_w_proj_ssd, v_w_out):
    given = dict(locals())
    shapes = {n: given[n].shape for n in PARAMS}

    def local(prefix):
        return {n: (given[prefix + n] if n == "c_ctx" else given[prefix + n][0]).reshape(
            _shard_shape(n) if n in BIG else SMALL[n]) for n in PARAMS}

    w, m, v = local(""), local("m_"), local("v_")

    W = _unpack_gather(_all_gather(_pack_gather(w)))
    for n in SMALL:
        W[n] = w[n]
    lanes, grad_x, g = _local_step(x, c, ctx, loss_target, W)
    recv = _all_to_all(_pack_grads(g, (0.5 / D) * jnp.sum(lanes)))
    res = [_unpack_state(t) for t in _adamw(recv, _pack_state(w), _pack_state(m), _pack_state(v))]
    outs = [res[0]["loss"], grad_x]
    for r in res:
        outs += [r[n].reshape(shapes[n]) for n in PARAMS]
    return tuple(outs)
```

```python
import functools

import numpy as np
import jax
import jax.numpy as jnp
from jax import lax
from jax.experimental import pallas as pl
from jax.experimental.pallas import tpu as pltpu

F32, BF = jnp.float32, jnp.bfloat16

D = 1024
GRID_W = 64
EPS = 1e-6
POOL_WINDOWS = (2, 4, 8, 16)
PGW = 256
DIN = 2048
HEAD = 64
NST = 128
NG = 4
HPG = 8
GWID = HPG * HEAD
Q = 128
CONV_DIM = 3072
OFF_GATE, OFF_XBC, OFF_DT, IN_COLS = 4096, 6144, 9216, 9280
NDEV = 8
ADAM_LR, ADAM_B1, ADAM_B2, ADAM_EPS, ADAM_WD, ADAM_STEP = 0.001, 0.9, 0.999, 1e-08, 0.01, 10

V7X_VMEM_LIMIT = 56 * 2 ** 20
ROW_TILE = 256


def _params(sem=None):
    return pltpu.CompilerParams(dimension_semantics=sem, vmem_limit_bytes=V7X_VMEM_LIMIT)


def _dot(a, b):
    return jnp.dot(a.astype(BF), b.astype(BF), preferred_element_type=F32)


def _dot_nt(a, b):
    return lax.dot_general(a.astype(BF), b.astype(BF), (((1,), (1,)), ((), ())), preferred_element_type=F32)


def _dot_tn(a, b):
    return lax.dot_general(a.astype(BF), b.astype(BF), (((0,), (0,)), ((), ())), preferred_element_type=F32)


def _split(a, n):
    parts = []
    for _ in range(n):
        p = a.astype(BF)
        parts.append(p)
        a = a - p.astype(F32)
    return parts


def _dot_sl(a, b01, n=3):
    return sum(jnp.dot(p, b01, preferred_element_type=F32) for p in _split(a, n))


def _dot_sr(a01, b, n=3):
    return sum(jnp.dot(a01, p, preferred_element_type=F32) for p in _split(b, n))


def _dot_tn_sl(a, b01, n=2):
    return sum(lax.dot_general(p, b01, (((0,), (0,)), ((), ())), preferred_element_type=F32) for p in _split(a, n))


def _sigmoid(x):
    return 1.0 / (1.0 + jnp.exp(-x))


def _matmul(a, b, out_dtype, name, tm=512, tn=512, tk=1024, bt=False, n=None):
    M, K = a.shape
    N = n if n is not None else (b.shape[0] if bt else b.shape[1])
    tm, tn, tk = min(tm, M), min(tn, N), min(tk, K)
    assert M % tm == 0 and N % tn == 0 and K % tk == 0, (a.shape, b.shape)
    nk = K // tk

    def body(a_ref, b_ref, o_ref, acc):
        k = pl.program_id(2)
        p = _dot_nt(a_ref[...], b_ref[...]) if bt else _dot(a_ref[...], b_ref[...])

        @pl.when(k == 0)
        def _():
            acc[...] = p

        @pl.when(k > 0)
        def _():
            acc[...] += p

        @pl.when(k == nk - 1)
        def _():
            o_ref[...] = acc[...].astype(o_ref.dtype)

    return pl.pallas_call(
        body, name=name, grid=(M // tm, N // tn, nk),
        in_specs=[pl.BlockSpec((tm, tk), lambda i, j, k: (i, k)),
                  pl.BlockSpec((tn, tk), lambda i, j, k: (j, k)) if bt else pl.BlockSpec((tk, tn), lambda i, j, k: (k, j))],
        out_specs=pl.BlockSpec((tm, tn), lambda i, j, k: (i, j)),
        out_shape=jax.ShapeDtypeStruct((M, N), out_dtype),
        scratch_shapes=[pltpu.VMEM((tm, tn), F32)],
        compiler_params=_params(("parallel", "parallel", "arbitrary")),
    )(a, b)


def _matmul_tn(a, g, name, ta=512, tn=512, tr=512):
    M, Ka = a.shape
    N = g.shape[1]
    ta, tn, tr = min(ta, Ka), min(tn, N), min(tr, M)
    assert M % tr == 0 and N % tn == 0 and Ka % ta == 0, (a.shape, g.shape)
    nr = M // tr

    def body(a_ref, g_ref, o_ref):
        k = pl.program_id(2)
        p = _dot_tn(a_ref[...], g_ref[...])

        @pl.when(k == 0)
        def _():
            o_ref[...] = p

        @pl.when(k > 0)
        def _():
            o_ref[...] += p

    return pl.pallas_call(
        body, name=name, grid=(Ka // ta, N // tn, nr),
        in_specs=[pl.BlockSpec((tr, ta), lambda i, j, k: (k, i)), pl.BlockSpec((tr, tn), lambda i, j, k: (k, j))],
        out_specs=pl.BlockSpec((ta, tn), lambda i, j, k: (i, j)),
        out_shape=jax.ShapeDtypeStruct((Ka, N), F32),
        compiler_params=_params(("parallel", "parallel", "arbitrary")),
    )(a, g)


def _dhx(pieces, ddt, w_inT, w_dtT):
    _, R, _ = pieces[0].shape
    tm = R // 8
    kb = 1024
    starts, nblk = [], []
    for p in pieces:
        starts.append(sum(nblk))
        nblk.append(p.shape[2] // kb)
    nk = sum(nblk)
    assert nk * kb == OFF_DT and R % 128 == 0
    npc = len(pieces)

    def body(*refs):
        a_refs, dt_ref, w_ref, wdt_ref, o_ref, acc = refs[:npc], refs[npc], refs[npc + 1], refs[npc + 2], refs[npc + 3], refs[npc + 4]
        k = pl.program_id(2)

        @pl.when(k == 0)
        def _():
            acc[...] = _dot(dt_ref[0], wdt_ref[...])

        for p in range(npc):
            @pl.when((k >= starts[p]) & (k < starts[p] + nblk[p]))
            def _(p=p):
                acc[...] += _dot(a_refs[p][0], w_ref[...])

        @pl.when(k == nk - 1)
        def _():
            o_ref[0] = acc[...]

    in_specs = [pl.BlockSpec((1, tm, kb), functools.partial(
        lambda e, t, k, s, nb: (e, t, jnp.clip(k - s, 0, nb - 1)), s=starts[p], nb=nblk[p])) for p in range(npc)]
    in_specs += [pl.BlockSpec((1, tm, 128), lambda e, t, k: (e, t, 0)),
                 pl.BlockSpec((kb, D), lambda e, t, k: (k, 0)),
                 pl.BlockSpec((128, D), lambda e, t, k: (0, 0))]
    return pl.pallas_call(
        body, name="d_hx", grid=(2, R // tm, nk), in_specs=in_specs,
        out_specs=pl.BlockSpec((1, tm, D), lambda e, t, k: (e, t, 0)),
        out_shape=jax.ShapeDtypeStruct((2, R, D), F32),
        scratch_shapes=[pltpu.VMEM((tm, D), F32)],
        compiler_params=_params(("parallel", "parallel", "arbitrary")),
    )(*pieces, ddt, w_inT, w_dtT)


def _adaln_fwd(c16, w_ada_bf, b_ada):
    def body(c_ref, w_ref, b_ref, o_ref):
        cc = c_ref[...]
        o_ref[...] = _dot(cc * _sigmoid(cc), w_ref[...]) + b_ref[...]

    return pl.pallas_call(body, name="adaln_fwd", out_shape=jax.ShapeDtypeStruct((16, 3 * D), F32),
                          compiler_params=_params())(c16, w_ada_bf, b_ada)


def _adaln_bwd(acc_n, acc_f, mod16, c16, norm_pre, w_ada_bf):
    def body(an_ref, af_ref, mod_ref, c_ref, np_ref, wt_ref, dw_ref, db_ref, sm_ref, dmod):
        npre = np_ref[...]
        dmod[...] = jnp.zeros_like(dmod)
        dnp = jnp.zeros((1, D), F32)
        dshift_c = jnp.zeros((1, D), F32)
        dgpre_c = jnp.zeros((1, D), F32)
        scale_c = mod_ref[2:3, D:2 * D]
        for e in range(2):
            dg_x, ds_x = an_ref[e, 0, 0:1, :], an_ref[e, 0, 1:2, :]
            dg_c, ds_c = an_ref[e, 1, 0:1, :], an_ref[e, 1, 1:2, :]
            dmod[e:e + 1, 0:D] = ds_x
            dmod[e:e + 1, D:2 * D] = dg_x * npre
            dmod[e:e + 1, 2 * D:3 * D] = af_ref[e, 0:1, :]
            dnp = dnp + dg_x * (1.0 + mod_ref[e:e + 1, D:2 * D]) + dg_c * (1.0 + scale_c)
            dshift_c = dshift_c + ds_c
            dgpre_c = dgpre_c + dg_c
        dmod[2:3, 0:D] = dshift_c
        dmod[2:3, D:2 * D] = dgpre_c * npre
        dm = dmod[...]
        cc = c_ref[...]
        sg = _sigmoid(cc)
        dw_ref[...] = _dot_tn(cc * sg, dm)
        db_ref[...] = jnp.zeros_like(db_ref)
        db_ref[0:1, :] = jnp.sum(dm, axis=0, keepdims=True)
        dsilu = sg * (1.0 + cc * (1.0 - sg))
        dcs = _dot_nt(dm, wt_ref[...]) * dsilu
        sm_ref[...] = jnp.zeros_like(sm_ref)
        sm_ref[0:1, :] = dnp
        sm_ref[1:2, :] = dcs[2:3, :]

    return pl.pallas_call(
        body, name="adaln_bwd",
        out_shape=(jax.ShapeDtypeStruct((D, 3 * D), F32), jax.ShapeDtypeStruct((16, 3 * D), F32),
                   jax.ShapeDtypeStruct((8, D), F32)),
        scratch_shapes=[pltpu.VMEM((16, 3 * D), F32)],
        compiler_params=_params())(acc_n, acc_f, mod16, c16, norm_pre, w_ada_bf)


def _row_specs(L):
    nx = L // ROW_TILE
    return (pl.BlockSpec((1, ROW_TILE, D), lambda e, t: (e, jnp.minimum(t, nx - 1), 0)),
            pl.BlockSpec((1, ROW_TILE, D), lambda e, t: (e, jnp.maximum(t - nx, 0), 0)))


def _norm_mod_fwd(x, ctx, tab):
    L = x.shape[1]
    R = L + ctx.shape[1]
    nx = L // ROW_TILE

    def body(x_ref, c_ref, t_ref, o_ref):
        x = jnp.where(pl.program_id(1) < nx, x_ref[0], c_ref[0])
        r = lax.rsqrt(jnp.mean(x * x, axis=-1, keepdims=True) + EPS)
        t = t_ref[0, 0]
        o_ref[0] = (x * r * t[0:1] + t[1:2]).astype(BF)

    return pl.pallas_call(
        body, name="norm_mod_fwd", grid=(2, R // ROW_TILE),
        in_specs=[*_row_specs(L), pl.BlockSpec((1, 1, 8, D), lambda e, t: (e, t // nx, 0, 0))],
        out_specs=pl.BlockSpec((1, ROW_TILE, D), lambda e, t: (e, t, 0)),
        out_shape=jax.ShapeDtypeStruct((2, R, D), BF),
        compiler_params=_params(("parallel", "parallel")),
    )(x, ctx, tab)


def _norm_mod_bwd(dh, x, ctx, tab, dxo):
    L = x.shape[1]
    R = L + ctx.shape[1]
    nx = L // ROW_TILE

    def body(dh_ref, x_ref, c_ref, t_ref, dxo_ref, gx_ref, acc_ref):
        t = pl.program_id(1)
        x = jnp.where(t < nx, x_ref[0], c_ref[0])
        r = lax.rsqrt(jnp.mean(x * x, axis=-1, keepdims=True) + EPS)
        xn = x * r
        dh = dh_ref[0]

        @pl.when((t == 0) | (t == nx))
        def _():
            acc_ref[...] = jnp.zeros_like(acc_ref)

        acc_ref[0, 0, 0:1, :] += jnp.sum(dh * xn, axis=0, keepdims=True)
        acc_ref[0, 0, 1:2, :] += jnp.sum(dh, axis=0, keepdims=True)

        @pl.when(t < nx)
        def _():
            dxn = dh * t_ref[0, 0][0:1]
            dx = r * (dxn - xn * jnp.mean(dxn * xn, axis=-1, keepdims=True))
            gx_ref[0] = dxo_ref[0] + dx

    xspec, cspec = _row_specs(L)
    return pl.pallas_call(
        body, name="norm_mod_bwd", grid=(2, R // ROW_TILE),
        in_specs=[pl.BlockSpec((1, ROW_TILE, D), lambda e, t: (e, t, 0)), xspec, cspec,
                  pl.BlockSpec((1, 1, 8, D), lambda e, t: (e, t // nx, 0, 0)), xspec],
        out_specs=(xspec, pl.BlockSpec((1, 1, 8, D), lambda e, t: (e, t // nx, 0, 0))),
        out_shape=(jax.ShapeDtypeStruct((2, L, D), F32), jax.ShapeDtypeStruct((2, 2, 8, D), F32)),
        compiler_params=_params(("parallel", "arbitrary")),
    )(dh, x, ctx, tab, dxo)


POOL_TILE = 256
POOL_PAD = 8 * GRID_W


def _pool_tables(L):
    rows = L // GRID_W
    mats = np.zeros((4, POOL_TILE, POOL_TILE), np.float32)
    wts = np.zeros((4, 2, 17), np.float32)
    inv = np.zeros((4, L, 1), np.float32)
    for gi, k in enumerate(POOL_WINDOWS):
        lo, hi = k // 2, k - 1 - k // 2
        m = np.zeros((GRID_W, GRID_W), np.float32)
        for t in range(GRID_W):
            m[t, max(t - lo, 0):min(t + hi, GRID_W - 1) + 1] = 1.0
        for b in range(POOL_TILE // GRID_W):
            mats[gi, b * GRID_W:(b + 1) * GRID_W, b * GRID_W:(b + 1) * GRID_W] = m
        for o in range(-lo, hi + 1):
            wts[gi, 0, o + 8] = 1.0
            wts[gi, 1, -o + 8] = 1.0
        cnt_c = m.sum(1)
        cnt_r = np.array([min(r + hi, rows - 1) - max(r - lo, 0) + 1 for r in range(rows)], np.float32)
        inv[gi, :, 0] = (1.0 / (cnt_r[:, None] * cnt_c[None, :])).reshape(-1)
    matsT = np.ascontiguousarray(np.transpose(mats, (0, 2, 1)))
    return (jnp.asarray(mats, BF), jnp.asarray(matsT, BF), jnp.asarray(wts), jnp.asarray(inv))


def _pool_cols(get_tile, mat, pad_ref, L, n):
    pad_ref[pl.ds(0, POOL_PAD), :] = jnp.zeros((POOL_PAD, PGW), F32)
    pad_ref[pl.ds(POOL_PAD + L, POOL_PAD), :] = jnp.zeros((POOL_PAD, PGW), F32)

    def step(i, carry):
        off = pl.multiple_of(i * POOL_TILE, POOL_TILE)
        pad_ref[pl.ds(POOL_PAD + off, POOL_TILE), :] = _dot_sr(mat, get_tile(off).astype(F32), n)
        return carry

    lax.fori_loop(0, L // POOL_TILE, step, 0)


def _pool_rows(pad_ref, wt, off, n):
    acc = jnp.zeros((n, PGW), F32)
    for o in range(-8, 9):
        acc = acc + wt[:, o + 8:o + 9] * pad_ref[pl.ds(POOL_PAD + GRID_W * o + off, n), :]
    return acc


def _pool_fwd(proj3, pool_w_bf, pool_scale, tables, L):
    mats, _, wts, inv = tables
    nt = L // POOL_TILE

    def body(v_ref, z_ref, pw_ref, ps_ref, m_ref, wt_ref, inv_ref, o_ref, pad_ref):
        _pool_cols(lambda off: v_ref[0, pl.ds(off, POOL_TILE), :], m_ref[0], pad_ref, L, 1)
        wt = wt_ref[0, 0:1, :]

        def step(i, carry):
            off = pl.multiple_of(i * POOL_TILE, POOL_TILE)
            rows = pl.ds(off, POOL_TILE)
            v = v_ref[0, rows, :].astype(F32)
            diff = _pool_rows(pad_ref, wt, off, POOL_TILE) * inv_ref[0, rows, :] - v
            yp = _dot(diff, pw_ref[0])
            z = z_ref[0, rows, :].astype(F32)
            o_ref[0, rows, :] = (yp * ps_ref[...] * (z * _sigmoid(z))).astype(BF)
            return carry

        lax.fori_loop(0, nt, step, 0)

    return pl.pallas_call(
        body, name="pool_fwd", grid=(2, 4),
        in_specs=[pl.BlockSpec((1, L, PGW), lambda e, g: (e, 0, g)),
                  pl.BlockSpec((1, L, PGW), lambda e, g: (e, 0, 4 + g)),
                  pl.BlockSpec((1, PGW, PGW), lambda e, g: (g, 0, 0)),
                  pl.BlockSpec((1, PGW), lambda e, g: (0, g)),
                  pl.BlockSpec((1, POOL_TILE, POOL_TILE), lambda e, g: (g, 0, 0)),
                  pl.BlockSpec((1, 2, 17), lambda e, g: (g, 0, 0)),
                  pl.BlockSpec((1, L, 1), lambda e, g: (g, 0, 0))],
        out_specs=pl.BlockSpec((1, L, PGW), lambda e, g: (e, 0, g)),
        out_shape=jax.ShapeDtypeStruct((2, L, D), BF),
        scratch_shapes=[pltpu.VMEM((L + 2 * POOL_PAD, PGW), F32)],
        compiler_params=_params(("parallel", "parallel")),
    )(proj3, proj3, pool_w_bf, pool_scale, mats, wts, inv)


def _pool_bwd(proj3, d_ypool, pool_w_bf, pool_wT_bf, pool_scale, tables, L):
    mats, matsT, wts, inv = tables
    nt = L // POOL_TILE
    R = proj3.shape[1]

    def body(v_ref, z_ref, dy_ref, pw_ref, pwt_ref, ps_ref, m_ref, mt_ref, wt_ref, inv_ref,
             dv_ref, dz_ref, dpw_ref, acc_ref, pad_ref, dd_ref):
        e = pl.program_id(1)

        @pl.when(e == 0)
        def _():
            dpw_ref[...] = jnp.zeros_like(dpw_ref)
            acc_ref[...] = jnp.zeros_like(acc_ref)

        _pool_cols(lambda off: v_ref[0, pl.ds(off, POOL_TILE), :], m_ref[0], pad_ref, L, 1)
        wt = wt_ref[0, 0:1, :]
        ps = ps_ref[...]

        def step(i, carry):
            off = pl.multiple_of(i * POOL_TILE, POOL_TILE)
            rows = pl.ds(off, POOL_TILE)
            v = v_ref[0, rows, :].astype(F32)
            diff = _pool_rows(pad_ref, wt, off, POOL_TILE) * inv_ref[0, rows, :] - v
            yp = _dot(diff, pw_ref[0])
            z = z_ref[0, rows, :].astype(F32)
            sg = _sigmoid(z)
            sz = z * sg
            dy = dy_ref[0, rows, :]
            dz_ref[0, rows, :] = (dy * yp * ps * (sg * (1.0 + z * (1.0 - sg)))).astype(BF)
            dys = dy * sz
            acc_ref[0, 0:1, :] += jnp.sum(dys * yp, axis=0, keepdims=True)
            dyp = dys * ps
            dpw_ref[0] += _dot_tn(diff, dyp)
            dd_ref[rows, :] = _dot(dyp, pwt_ref[0])
            return carry

        lax.fori_loop(0, nt, step, 0)
        _pool_cols(lambda off: dd_ref[pl.ds(off, POOL_TILE), :] * inv_ref[0, pl.ds(off, POOL_TILE), :],
                   mt_ref[0], pad_ref, L, 2)
        wtt = wt_ref[0, 1:2, :]

        def step2(i, carry):
            off = pl.multiple_of(i * POOL_TILE, POOL_TILE)
            rows = pl.ds(off, POOL_TILE)
            dv_ref[0, rows, :] = (_pool_rows(pad_ref, wtt, off, POOL_TILE) - dd_ref[rows, :]).astype(BF)
            return carry

        lax.fori_loop(0, nt, step2, 0)
        dv_ref[0, pl.ds(L, R - L), :] = jnp.zeros((R - L, PGW), BF)
        dz_ref[0, pl.ds(L, R - L), :] = jnp.zeros((R - L, PGW), BF)

    return pl.pallas_call(
        body, name="pool_bwd", grid=(4, 2),
        in_specs=[pl.BlockSpec((1, L, PGW), lambda g, e: (e, 0, g)),
                  pl.BlockSpec((1, L, PGW), lambda g, e: (e, 0, 4 + g)),
                  pl.BlockSpec((1, L, PGW), lambda g, e: (e, 0, g)),
                  pl.BlockSpec((1, PGW, PGW), lambda g, e: (g, 0, 0)),
                  pl.BlockSpec((1, PGW, PGW), lambda g, e: (g, 0, 0)),
                  pl.BlockSpec((1, PGW), lambda g, e: (0, g)),
                  pl.BlockSpec((1, POOL_TILE, POOL_TILE), lambda g, e: (g, 0, 0)),
                  pl.BlockSpec((1, POOL_TILE, POOL_TILE), lambda g, e: (g, 0, 0)),
                  pl.BlockSpec((1, 2, 17), lambda g, e: (g, 0, 0)),
                  pl.BlockSpec((1, L, 1), lambda g, e: (g, 0, 0))],
        out_specs=(pl.BlockSpec((1, R, PGW), lambda g, e: (e, 0, g)),
                   pl.BlockSpec((1, R, PGW), lambda g, e: (e, 0, g)),
                   pl.BlockSpec((1, PGW, PGW), lambda g, e: (g, 0, 0)),
                   pl.BlockSpec((1, 8, PGW), lambda g, e: (g, 0, 0))),
        out_shape=(jax.ShapeDtypeStruct((2, R, D), BF), jax.ShapeDtypeStruct((2, R, D), BF),
                   jax.ShapeDtypeStruct((4, PGW, PGW), F32), jax.ShapeDtypeStruct((4, 8, PGW), F32)),
        scratch_shapes=[pltpu.VMEM((L + 2 * POOL_PAD, PGW), F32), pltpu.VMEM((L, PGW), F32)],
        compiler_params=_params(("parallel", "arbitrary")),
    )(proj3, proj3, d_ypool, pool_w_bf, pool_wT_bf, pool_scale, mats, matsT, wts, inv)


CONV_BLOCK = 128


def _conv_tap(u, k, L):
    off = k - 2
    if off == 0:
        return u
    R = u.shape[0]
    r = lax.broadcasted_iota(jnp.int32, (R, 1), 0)
    pos = jnp.where(r < L, r, r - L) + off
    seg = jnp.where(r < L, L, R - L)
    return jnp.where((pos >= 0) & (pos < seg), pltpu.roll(u, (-off) % R, 0), 0.0)


def _conv_fwd(proj3, conv_w, conv_b, L):
    _, R, _ = proj3.shape
    cb0 = OFF_XBC // CONV_BLOCK

    def body(u_ref, w_ref, b_ref, o_ref):
        u = u_ref[0].astype(F32)
        w = w_ref[...]
        pre = b_ref[...] + sum(_conv_tap(u, k, L) * w[k:k + 1, :] for k in range(4))
        o_ref[0] = (pre * _sigmoid(pre)).astype(BF)

    return pl.pallas_call(
        body, name="conv_fwd", grid=(2, CONV_DIM // CONV_BLOCK),
        in_specs=[pl.BlockSpec((1, R, CONV_BLOCK), lambda e, j: (e, 0, cb0 + j)),
                  pl.BlockSpec((4, CONV_BLOCK), lambda e, j: (0, j)),
                  pl.BlockSpec((1, CONV_BLOCK), lambda e, j: (0, j))],
        out_specs=pl.BlockSpec((1, R, CONV_BLOCK), lambda e, j: (e, 0, j)),
        out_shape=jax.ShapeDtypeStruct((2, R, CONV_DIM), BF),
        compiler_params=_params(("parallel", "parallel")),
    )(proj3, conv_w, conv_b)


def _conv_bwd(proj3, addends, col0, ncols, in_maps, conv_w, conv_b, L, name):
    _, R, _ = proj3.shape
    cb0 = (OFF_XBC + col0) // CONV_BLOCK
    wb0 = col0 // CONV_BLOCK
    na = len(addends)

    def body(*refs):
        u_ref, w_ref, b_ref = refs[0], refs[1], refs[2]
        a_refs = refs[3:3 + na]
        o_ref, acc_ref = refs[3 + na], refs[4 + na]
        u = u_ref[0].astype(F32)
        w = w_ref[...]
        taps = [_conv_tap(u, k, L) for k in range(4)]
        pre = b_ref[...] + sum(taps[k] * w[k:k + 1, :] for k in range(4))
        sg = _sigmoid(pre)
        dxbc = a_refs[0][0]
        for a in a_refs[1:]:
            dxbc = dxbc + a[0]
        dpre = dxbc * (sg * (1.0 + pre * (1.0 - sg)))
        acc_ref[...] = jnp.zeros_like(acc_ref)
        for k in range(4):
            acc_ref[0, k:k + 1, :] = jnp.sum(dpre * taps[k], axis=0, keepdims=True)
        acc_ref[0, 4:5, :] = jnp.sum(dpre, axis=0, keepdims=True)
        du = sum(_conv_tap(dpre, 4 - k, L) * w[k:k + 1, :] for k in range(4))
        o_ref[0] = du.astype(BF)

    in_specs = [pl.BlockSpec((1, R, CONV_BLOCK), lambda e, j: (e, 0, cb0 + j)),
                pl.BlockSpec((4, CONV_BLOCK), lambda e, j: (0, wb0 + j)),
                pl.BlockSpec((1, CONV_BLOCK), lambda e, j: (0, wb0 + j))]
    for m in in_maps:
        in_specs.append(pl.BlockSpec((1, R, CONV_BLOCK), functools.partial(lambda e, j, m: (e, 0, m(j)), m=m)))
    return pl.pallas_call(
        body, name=name, grid=(2, ncols // CONV_BLOCK),
        in_specs=in_specs,
        out_specs=(pl.BlockSpec((1, R, CONV_BLOCK), lambda e, j: (e, 0, j)),
                   pl.BlockSpec((1, 8, CONV_BLOCK), lambda e, j: (e, 0, j))),
        out_shape=(jax.ShapeDtypeStruct((2, R, ncols), BF), jax.ShapeDtypeStruct((2, 8, ncols), F32)),
        compiler_params=_params(("parallel", "parallel")),
    )(proj3, conv_w, conv_b, *addends)


def _softplus(x):
    e = jnp.exp(-jnp.abs(x))
    u = 1.0 + e
    return jnp.maximum(x, 0.0) + jnp.where(u == 1.0, e, e * jnp.log(u) / (u - 1.0))


def _dt_fwd(dt_raw, bias128):
    _, R, _ = dt_raw.shape

    def body(x_ref, b_ref, o_ref):
        o_ref[0] = _softplus(x_ref[0] + b_ref[...])

    return pl.pallas_call(
        body, name="dt_fwd", grid=(2,),
        in_specs=[pl.BlockSpec((1, R, 128), lambda e: (e, 0, 0)), pl.BlockSpec((1, 128), lambda e: (0, 0))],
        out_specs=pl.BlockSpec((1, R, 128), lambda e: (e, 0, 0)),
        out_shape=jax.ShapeDtypeStruct(dt_raw.shape, F32),
        compiler_params=_params(("parallel",)),
    )(dt_raw, bias128)


def _dt_bwd(dt_raw, bias128, ddt_f, ddt_b):
    _, R, _ = dt_raw.shape

    def body(x_ref, b_ref, f_ref, g_ref, o_ref, acc_ref):
        d = (f_ref[0] + g_ref[0]) * _sigmoid(x_ref[0] + b_ref[...])
        o_ref[0] = d.astype(BF)
        acc_ref[...] = jnp.zeros_like(acc_ref)
        acc_ref[0, 0:1, :] = jnp.sum(d, axis=0, keepdims=True)

    blk = pl.BlockSpec((1, R, 128), lambda e: (e, 0, 0))
    return pl.pallas_call(
        body, name="dt_bwd", grid=(2,),
        in_specs=[blk, pl.BlockSpec((1, 128), lambda e: (0, 0)), blk, blk],
        out_specs=(blk, pl.BlockSpec((1, 8, 128), lambda e: (e, 0, 0))),
        out_shape=(jax.ShapeDtypeStruct(dt_raw.shape, BF), jax.ShapeDtypeStruct((2, 8, 128), F32)),
        compiler_params=_params(("parallel",)),
    )(dt_raw, bias128, ddt_f, ddt_b)


def _tri(d):
    i = lax.broadcasted_iota(jnp.int32, (Q, Q), 0)
    j = lax.broadcasted_iota(jnp.int32, (Q, Q), 1)
    return (i >= j) if d == 0 else (i <= j)


def _expand_mat(d):
    r = lax.broadcasted_iota(jnp.int32, (128, GWID), 0)
    c = lax.broadcasted_iota(jnp.int32, (128, GWID), 1)
    return (r == d * HPG + jnp.right_shift(c, 6)).astype(BF)


def _reduce_mat(d):
    r = lax.broadcasted_iota(jnp.int32, (GWID, 128), 0)
    c = lax.broadcasted_iota(jnp.int32, (GWID, 128), 1)
    return (c == d * HPG + jnp.right_shift(r, 6)).astype(BF)


def _ssd_chunk(d, dt, A, xs, B, C):
    mask = _tri(d)
    T = mask.astype(BF)
    Tt = _tri(1 - d).astype(BF)
    a = dt * A
    acs = _dot_sr(T, a)
    acsT = _dot_sl(a.T, Tt)
    E = _expand_mat(d)
    dt_e = _dot_sl(dt, E)
    acs_e = _dot_sl(acs, E)
    alast_e = acs_e[Q - 1:Q, :] if d == 0 else acs_e[0:1, :]
    return dict(mask=mask, T=T, Tt=Tt, acsT=acsT, dt_e=dt_e, acs_e=acs_e, lam=jnp.exp(acs_e),
                w=jnp.exp(alast_e - acs_e), decay=jnp.exp(alast_e), xt=xs * dt_e, CB=_dot_nt(C, B))


def _head_decay(q, d, hh):
    col = q["acs_e"][:, hh * HEAD:hh * HEAD + 1]
    row = q["acsT"][d * HPG + hh:d * HPG + hh + 1, :]
    return jnp.exp(jnp.where(q["mask"], col - row, -jnp.inf))


def _chunk_maps(NX, NS):
    cf = lambda s: lax.rem(s + NX, NS)
    cb = lambda s: NS - 1 - s
    return cf, cb


def _ssd_fwd(xbc, dt_loc, a_loc, L):
    _, R, _ = xbc.shape
    NX, NS = L // Q, R // Q
    cf, cb = _chunk_maps(NX, NS)

    def body(xs_f, b_f, c_f, dt_f, xs_b, b_b, c_b, dt_b, a_ref, y_f, hs_f, y_b, hs_b, hT):
        @pl.when(pl.program_id(2) == 0)
        def _():
            hT[...] = jnp.zeros_like(hT)

        A = a_ref[0, 0:1, :]
        lane = lax.broadcasted_iota(jnp.int32, (Q, 128), 1)
        for d, (xs_ref, b_ref, c_ref, dt_ref, y_ref, hs_ref) in enumerate(
                ((xs_f, b_f, c_f, dt_f, y_f, hs_f), (xs_b, b_b, c_b, dt_b, y_b, hs_b))):
            xs, B, C = xs_ref[0].astype(F32), b_ref[0], c_ref[0]
            q = _ssd_chunk(d, dt_ref[0, 0], A, xs, B, C)
            h = hT[d]
            hb = h.astype(BF)
            hs_ref[0, 0] = hb
            parts = []
            for pr in range(HPG // 2):
                xp = q["xt"][:, pr * 128:(pr + 1) * 128].astype(BF)
                r0 = _dot(q["CB"] * _head_decay(q, d, 2 * pr), xp)
                r1 = _dot(q["CB"] * _head_decay(q, d, 2 * pr + 1), xp)
                parts.append(jnp.where(lane < HEAD, r0, r1))
            y_ref[0] = jnp.concatenate(parts, axis=1) + _dot(C, hb) * q["lam"]
            hT[d] = q["decay"] * h + _dot_tn(B, q["xt"] * q["w"])

    def spec(shape, imap):
        return pl.BlockSpec(shape, imap)

    def ins(c):
        return [spec((1, Q, GWID), lambda e, g, s: (e, c(s), g)),
                spec((1, Q, NST), lambda e, g, s: (e, c(s), DIN // NST + g)),
                spec((1, Q, NST), lambda e, g, s: (e, c(s), DIN // NST + NG + g)),
                spec((1, 1, Q, 128), lambda e, g, s: (e, g, c(s), 0))]

    def outs(c):
        return [spec((1, Q, GWID), lambda e, g, s: (e, c(s), g)),
                spec((1, 1, NST, GWID), lambda e, g, s: (e, c(s), 0, g))]

    yshape = jax.ShapeDtypeStruct((2, R, DIN), F32)
    hshape = jax.ShapeDtypeStruct((2, NS, NST, DIN), BF)
    return pl.pallas_call(
        body, name="ssd_fwd", grid=(2, NG, NS),
        in_specs=ins(cf) + ins(cb) + [spec((1, 8, 128), lambda e, g, s: (g, 0, 0))],
        out_specs=tuple(outs(cf) + outs(cb)),
        out_shape=(yshape, hshape, yshape, hshape),
        scratch_shapes=[pltpu.VMEM((2, NST, GWID), F32)],
        compiler_params=_params(("parallel", "parallel", "arbitrary")),
    )(xbc, xbc, xbc, dt_loc, xbc, xbc, xbc, dt_loc, a_loc)


def _ssd_bwd(xbc, dt_loc, a_loc, hs_f, hs_b, dy, L):
    _, R, _ = xbc.shape
    NX, NS = L // Q, R // Q
    cf0, cb0 = _chunk_maps(NX, NS)
    cf = lambda sp: cf0(NS - 1 - sp)
    cb = lambda sp: cb0(NS - 1 - sp)

    def body(xs_f, b_f, c_f, dt_f, hs_f_, dy_f, xs_b, b_b, c_b, dt_b, hs_b_, dy_b, a_ref,
             dxs_f, dbc_f, ddt_f, dxs_b, dbc_b, ddt_b, da_ref, dhT):
        @pl.when(pl.program_id(2) == 0)
        def _():
            dhT[...] = jnp.zeros_like(dhT)
            da_ref[...] = jnp.zeros_like(da_ref)

        A = a_ref[0, 0:1, :]
        lane = lax.broadcasted_iota(jnp.int32, (Q, 128), 1)
        row = lax.broadcasted_iota(jnp.int32, (Q, 128), 0)
        for d, (xs_ref, b_ref, c_ref, dt_ref, hs_ref, dy_ref, dxs_ref, dbc_ref, ddt_ref) in enumerate(
                ((xs_f, b_f, c_f, dt_f, hs_f_, dy_f, dxs_f, dbc_f, ddt_f),
                 (xs_b, b_b, c_b, dt_b, hs_b_, dy_b, dxs_b, dbc_b, ddt_b))):
            xs, B, C, dt = xs_ref[0].astype(F32), b_ref[0], c_ref[0], dt_ref[0, 0]
            q = _ssd_chunk(d, dt, A, xs, B, C)
            xt, lam, w, decay = q["xt"], q["lam"], q["w"], q["decay"]
            H = hs_ref[0, 0]
            dyv = dy_ref[0]
            dh = dhT[d]
            dacs_e = dyv * (_dot(C, H) * lam)
            dZ = dyv * lam
            dC = _dot_nt(dZ, H)
            dH = _dot_tn(C, dZ)
            U = _dot(B, dh)
            xw = xt * w
            dxt = U * w
            Wt = U * xw
            dacs_e = dacs_e - Wt
            dalast_e = (jnp.sum(Wt, axis=0, keepdims=True)
                        + decay * jnp.sum(dh * H.astype(F32), axis=0, keepdims=True))
            dB = _dot_nt(xw, dh)
            dCB = jnp.zeros((Q, Q), F32)
            dacs = jnp.zeros((Q, 128), F32)
            dxt_parts = []
            for pr in range(HPG // 2):
                xp = xt[:, pr * 128:(pr + 1) * 128]
                dyp = dyv[:, pr * 128:(pr + 1) * 128]
                dxp = jnp.zeros((Q, 128), F32)
                for h2 in range(2):
                    hh = 2 * pr + h2
                    Lh = _head_decay(q, d, hh)
                    M = q["CB"] * Lh
                    dym = jnp.where((lane < HEAD) if h2 == 0 else (lane >= HEAD), dyp, 0.0)
                    dM = _dot_nt(dym, xp)
                    dxp = dxp + _dot_tn(M, dym)
                    G = dM * M
                    sel = (lane == d * HPG + hh).astype(BF)
                    dacs = dacs + _dot_sl(G, sel, 2) - _dot_tn_sl(G, sel, 2)
                    dCB = dCB + dM * Lh
                dxt_parts.append(dxp)
            dxt = dxt + jnp.concatenate(dxt_parts, axis=1)
            dC = dC + _dot(dCB, B)
            dB = dB + _dot_tn(dCB, C)
            Rm = _reduce_mat(d)
            dacs = dacs + _dot_sl(dacs_e, Rm, 2)
            dal = _dot_sl(jnp.broadcast_to(dalast_e, (8, GWID)), Rm, 2)[0:1, :]
            dacs = dacs + jnp.where(row == (Q - 1 if d == 0 else 0), dal, 0.0)
            da = _dot_sr(q["Tt"], dacs, 2)
            ddt_ref[0, 0] = da * A + _dot_sl(dxt * xs, Rm, 2)
            da_ref[0, 0, 0:1, :] += jnp.sum(da * dt, axis=0, keepdims=True)
            dxs_ref[0] = dxt * q["dt_e"]
            dbc_ref[0] = jnp.concatenate([dB, dC], axis=1)
            dhT[d] = decay * dh + dH

    def spec(shape, imap):
        return pl.BlockSpec(shape, imap)

    def ins(c):
        return [spec((1, Q, GWID), lambda e, g, s: (e, c(s), g)),
                spec((1, Q, NST), lambda e, g, s: (e, c(s), DIN // NST + g)),
                spec((1, Q, NST), lambda e, g, s: (e, c(s), DIN // NST + NG + g)),
                spec((1, 1, Q, 128), lambda e, g, s: (e, g, c(s), 0)),
                spec((1, 1, NST, GWID), lambda e, g, s: (e, c(s), 0, g)),
                spec((1, Q, GWID), lambda e, g, s: (e, c(s), g))]

    def outs(c):
        return [spec((1, Q, GWID), lambda e, g, s: (e, c(s), g)),
                spec((1, Q, 2 * NST), lambda e, g, s: (e, c(s), g)),
                spec((1, 1, Q, 128), lambda e, g, s: (e, g, c(s), 0))]

    s_xs = jax.ShapeDtypeStruct((2, R, DIN), F32)
    s_bc = jax.ShapeDtypeStruct((2, R, 2 * NG * NST), F32)
    s_dt = jax.ShapeDtypeStruct((2, NG, R, 128), F32)
    return pl.pallas_call(
        body, name="ssd_bwd", grid=(2, NG, NS),
        in_specs=ins(cf) + ins(cb) + [spec((1, 8, 128), lambda e, g, s: (g, 0, 0))],
        out_specs=tuple(outs(cf) + outs(cb) + [spec((1, 1, 8, 128), lambda e, g, s: (e, g, 0, 0))]),
        out_shape=(s_xs, s_bc, s_dt, s_xs, s_bc, s_dt, jax.ShapeDtypeStruct((2, NG, 8, 128), F32)),
        scratch_shapes=[pltpu.VMEM((2, NST, GWID), F32)],
        compiler_params=_params(("parallel", "parallel", "arbitrary")),
    )(xbc, xbc, xbc, dt_loc, hs_f, dy, xbc, xbc, xbc, dt_loc, hs_b, dy, a_loc)


def _ssd_post_fwd(y_f, y_b, xbc, proj3, dskip_e, ssd_norm, L):
    def body(yf_ref, yb_ref, xs_ref, z_ref, ds_ref, w_ref, o_ref):
        y2 = yf_ref[0] + yb_ref[0] + ds_ref[...] * xs_ref[0].astype(F32)
        z = z_ref[0].astype(F32)
        u = y2 * (z * _sigmoid(z))
        parts = []
        for g in range(NG):
            ug = u[:, g * GWID:(g + 1) * GWID]
            parts.append(ug * lax.rsqrt(jnp.mean(ug * ug, axis=-1, keepdims=True) + EPS))
        o_ref[0] = (jnp.concatenate(parts, axis=1) * w_ref[...]).astype(BF)

    blk = lambda c: pl.BlockSpec((1, ROW_TILE, DIN), lambda e, t: (e, t, c))
    vec = pl.BlockSpec((1, DIN), lambda e, t: (0, 0))
    return pl.pallas_call(
        body, name="ssd_post_fwd", grid=(2, L // ROW_TILE),
        in_specs=[blk(0), blk(0), blk(0), blk(1), vec, vec],
        out_specs=blk(0),
        out_shape=jax.ShapeDtypeStruct((2, L, DIN), BF),
        compiler_params=_params(("parallel", "parallel")),
    )(y_f, y_b, xbc, proj3, dskip_e, ssd_norm)


def _ssd_post_bwd(d_yn, y_f, y_b, xbc, proj3, dskip_e, ssd_norm, L):
    _, R, _ = y_f.shape
    nx = L // ROW_TILE

    def body(dyn_ref, yf_ref, yb_ref, xs_ref, z_ref, ds_ref, w_ref, dy_ref, dsk_ref, dz_ref, acc_ref):
        t = pl.program_id(1)

        @pl.when(t == 0)
        def _():
            acc_ref[...] = jnp.zeros_like(acc_ref)

        @pl.when(t >= nx)
        def _():
            dy_ref[...] = jnp.zeros_like(dy_ref)
            dsk_ref[...] = jnp.zeros_like(dsk_ref)
            dz_ref[...] = jnp.zeros_like(dz_ref)

        @pl.when(t < nx)
        def _():
            xs = xs_ref[0].astype(F32)
            y2 = yf_ref[0] + yb_ref[0] + ds_ref[...] * xs
            z = z_ref[0].astype(F32)
            sg = _sigmoid(z)
            sz = z * sg
            u = y2 * sz
            dyn = dyn_ref[0]
            dun = dyn * w_ref[...]
            uh_parts, du_parts = [], []
            for g in range(NG):
                sl = slice(g * GWID, (g + 1) * GWID)
                ug = u[:, sl]
                rg = lax.rsqrt(jnp.mean(ug * ug, axis=-1, keepdims=True) + EPS)
                uh = ug * rg
                dg = dun[:, sl]
                du_parts.append(rg * (dg - uh * jnp.mean(dg * uh, axis=-1, keepdims=True)))
                uh_parts.append(uh)
            du = jnp.concatenate(du_parts, axis=1)
            uh = jnp.concatenate(uh_parts, axis=1)
            dy2 = du * sz
            dy_ref[0] = dy2
            dsk_ref[0] = dy2 * ds_ref[...]
            dz_ref[0] = (du * y2 * (sg * (1.0 + z * (1.0 - sg)))).astype(BF)
            acc_ref[0, 0:1, :] += jnp.sum(dyn * uh, axis=0, keepdims=True)
            acc_ref[0, 1:2, :] += jnp.sum(dy2 * xs, axis=0, keepdims=True)

    xmap = lambda c: (lambda e, t: (e, jnp.minimum(t, nx - 1), c))
    blk = lambda c: pl.BlockSpec((1, ROW_TILE, DIN), xmap(c))
    oblk = pl.BlockSpec((1, ROW_TILE, DIN), lambda e, t: (e, t, 0))
    vec = pl.BlockSpec((1, DIN), lambda e, t: (0, 0))
    return pl.pallas_call(
        body, name="ssd_post_bwd", grid=(2, R // ROW_TILE),
        in_specs=[blk(0), blk(0), blk(0), blk(0), blk(1), vec, vec],
        out_specs=(oblk, oblk, oblk, pl.BlockSpec((1, 8, DIN), lambda e, t: (e, 0, 0))),
        out_shape=(jax.ShapeDtypeStruct((2, R, DIN), F32), jax.ShapeDtypeStruct((2, R, DIN), F32),
                   jax.ShapeDtypeStruct((2, R, DIN), BF),
                   jax.ShapeDtypeStruct((2, 8, DIN), F32)),
        compiler_params=_params(("parallel", "arbitrary")),
    )(d_yn, y_f, y_b, xbc, proj3, dskip_e, ssd_norm)


def _merge_fwd(proj3, P, S, b_merge, L):
    def body(gp_ref, p_ref, s_ref, b_ref, o_ref):
        gt = _sigmoid(gp_ref[0].astype(F32) + b_ref[...])
        o_ref[0] = (gt[:, :D] * p_ref[0] + gt[:, D:] * s_ref[0]).astype(BF)

    blk = pl.BlockSpec((1, ROW_TILE, D), lambda e, t: (e, t, 0))
    return pl.pallas_call(
        body, name="merge_fwd", grid=(2, L // ROW_TILE),
        in_specs=[pl.BlockSpec((1, ROW_TILE, 2 * D), lambda e, t: (e, t, OFF_GATE // (2 * D))), blk, blk,
                  pl.BlockSpec((1, 2 * D), lambda e, t: (0, 0))],
        out_specs=blk, out_shape=jax.ShapeDtypeStruct((2, L, D), BF),
        compiler_params=_params(("parallel", "parallel")),
    )(proj3, P, S, b_merge)


def _merge_bwd(d_merged, proj3, P, S, b_merge, L):
    _, R, _ = proj3.shape
    nx = L // ROW_TILE

    def body(dm_ref, gp_ref, p_ref, s_ref, b_ref, dp_ref, ds_ref, dg_ref, acc_ref):
        t = pl.program_id(1)

        @pl.when(t == 0)
        def _():
            acc_ref[...] = jnp.zeros_like(acc_ref)

        @pl.when(t >= nx)
        def _():
            dg_ref[...] = jnp.zeros_like(dg_ref)

        @pl.when(t < nx)
        def _():
            gt = _sigmoid(gp_ref[0].astype(F32) + b_ref[...])
            dm = dm_ref[0]
            g1, g2 = gt[:, :D], gt[:, D:]
            dp_ref[0] = (dm * g1).astype(BF)
            ds_ref[0] = (dm * g2).astype(BF)
            dgp = jnp.concatenate([dm * p_ref[0] * g1 * (1.0 - g1), dm * s_ref[0] * g2 * (1.0 - g2)], axis=1)
            dg_ref[0] = dgp.astype(BF)
            acc_ref[0, 0:1, :] += jnp.sum(dgp, axis=0, keepdims=True)

    xmap = lambda e, t: (e, jnp.minimum(t, nx - 1), 0)
    blk = pl.BlockSpec((1, ROW_TILE, D), xmap)
    return pl.pallas_call(
        body, name="merge_bwd", grid=(2, R // ROW_TILE),
        in_specs=[blk, pl.BlockSpec((1, ROW_TILE, 2 * D), lambda e, t: (e, jnp.minimum(t, nx - 1), OFF_GATE // (2 * D))),
                  blk, blk, pl.BlockSpec((1, 2 * D), lambda e, t: (0, 0))],
        out_specs=(blk, blk, pl.BlockSpec((1, ROW_TILE, 2 * D), lambda e, t: (e, t, 0)),
                   pl.BlockSpec((1, 8, 2 * D), lambda e, t: (e, 0, 0))),
        out_shape=(jax.ShapeDtypeStruct((2, L, D), BF), jax.ShapeDtypeStruct((2, L, D), BF),
                   jax.ShapeDtypeStruct((2, R, 2 * D), BF), jax.ShapeDtypeStruct((2, 8, 2 * D), F32)),
        compiler_params=_params(("parallel", "arbitrary")),
    )(d_merged, proj3, P, S, b_merge)


def _final(out3, x, tgt, gtab, norm_post, L):
    def body(o_ref, x_ref, t_ref, g_ref, n_ref, dxo_ref, do_ref, acc_ref):
        @pl.when(pl.program_id(1) == 0)
        def _():
            acc_ref[...] = jnp.zeros_like(acc_ref)

        o = o_ref[0]
        gate = g_ref[0, 0:1, :]
        npost = n_ref[...]
        r2 = lax.rsqrt(jnp.mean(o * o, axis=-1, keepdims=True) + EPS)
        nh = o * r2
        on = nh * npost
        err = x_ref[0] + gate * on - t_ref[0]
        dxo = err * (1.0 / D)
        dxo_ref[0] = dxo
        dnh = dxo * gate * npost
        do_ref[0] = (r2 * (dnh - nh * jnp.mean(dnh * nh, axis=-1, keepdims=True))).astype(BF)
        acc_ref[0, 0:1, :] += jnp.sum(dxo * on, axis=0, keepdims=True)
        acc_ref[0, 1:2, :] += jnp.sum(dxo * gate * nh, axis=0, keepdims=True)
        acc_ref[0, 2:3, :] += jnp.sum(err * err, axis=0, keepdims=True)

    blk = pl.BlockSpec((1, ROW_TILE, D), lambda e, t: (e, t, 0))
    return pl.pallas_call(
        body, name="final", grid=(2, L // ROW_TILE),
        in_specs=[blk, blk, blk, pl.BlockSpec((1, 8, D), lambda e, t: (e, 0, 0)),
                  pl.BlockSpec((1, D), lambda e, t: (0, 0))],
        out_specs=(blk, blk, pl.BlockSpec((1, 8, D), lambda e, t: (e, 0, 0))),
        out_shape=(jax.ShapeDtypeStruct((2, L, D), F32), jax.ShapeDtypeStruct((2, L, D), BF),
                   jax.ShapeDtypeStruct((2, 8, D), F32)),
        compiler_params=_params(("parallel", "arbitrary")),
    )(out3, x, tgt, gtab, norm_post)


def _local_step(x, c, ctx, loss_target, W):
    nb, L, _ = x.shape
    LC = ctx.shape[1]
    R = L + LC
    assert nb == 2 and L % ROW_TILE == 0 and LC % Q == 0 and L % POOL_TILE == 0
    w_inT = W["w_in"]
    w_dtT = jnp.pad(w_inT[OFF_DT:], ((0, 64), (0, 0)))
    tables = _pool_tables(L)
    tr, tl = (2 * R) // 8, (2 * L) // 8

    c16 = jnp.zeros((16, D), F32).at[0:2].set(c).at[2].set(W["c_ctx"])
    mod16 = _adaln_fwd(c16, W["w_ada"], W["b_ada"])
    shift, scale, gate = mod16[:, :D], mod16[:, D:2 * D], mod16[:, 2 * D:]
    npre = W["norm_pre"]
    tab = jnp.zeros((2, 2, 8, D), F32)
    for e in range(2):
        tab = tab.at[e, 0, 0].set(npre[0] * (1.0 + scale[e])).at[e, 0, 1].set(shift[e])
        tab = tab.at[e, 1, 0].set(npre[0] * (1.0 + scale[2])).at[e, 1, 1].set(shift[2])
    gtab = jnp.zeros((2, 8, D), F32).at[:, 0].set(gate[0:2])

    hx = _norm_mod_fwd(x, ctx, tab)
    hx2 = hx.reshape(2 * R, D)
    proj3 = _matmul(hx2, w_inT, BF, "proj_main", tm=tr, tn=1024, bt=True, n=OFF_DT).reshape(2, R, OFF_DT)
    dt_raw = _matmul(hx2, w_dtT, F32, "proj_dt", tm=tr, bt=True).reshape(2, R, 128)
    ypool = _pool_fwd(proj3, W["pool_w"], W["pool_scale"], tables, L)
    xbc = _conv_fwd(proj3, W["conv_w"], W["conv_b"], L)
    bias128 = jnp.pad(W["dt_bias"].reshape(1, 64), ((0, 0), (0, 64)))
    dt = _dt_fwd(dt_raw, bias128)
    to_loc = lambda t: jnp.pad(t[:, :, :64].reshape(2, R, 2, NG, HPG).transpose(0, 3, 1, 2, 4).reshape(2, NG, R, 16),
                               ((0, 0), (0, 0), (0, 0), (0, 112)))
    from_loc = lambda t: jnp.pad(t[..., :16].reshape(2, NG, R, 2, HPG).transpose(0, 2, 3, 1, 4).reshape(2, R, 64),
                                 ((0, 0), (0, 0), (0, 64)))
    dt_loc = to_loc(dt)
    A = -jnp.exp(W["a_log"].reshape(2, NG, HPG))
    a_loc = jnp.zeros((NG, 8, 128), F32).at[:, 0, :16].set(A.transpose(1, 0, 2).reshape(NG, 16))
    y_f, hs_f, y_b, hs_b = _ssd_fwd(xbc, dt_loc, a_loc, L)
    dskip_e = jnp.repeat(W["d_skip"].reshape(1, 32), HEAD, axis=1)
    yn = _ssd_post_fwd(y_f, y_b, xbc, proj3, dskip_e, W["ssd_norm"], L)
    ypool2, yn2 = ypool.reshape(2 * L, D), yn.reshape(2 * L, DIN)
    P = _matmul(ypool2, W["w_proj_pool"], F32, "proj_pool", tm=tl, tn=1024).reshape(2, L, D)
    S = _matmul(yn2, W["w_proj_ssd"], F32, "proj_ssd", tm=tl, tn=1024).reshape(2, L, D)
    merged = _merge_fwd(proj3, P, S, W["b_merge"], L)
    merged2 = merged.reshape(2 * L, D)
    out3 = _matmul(merged2, W["w_out"], F32, "proj_out", tm=tl, tn=1024).reshape(2, L, D)
    dxo, dout, acc_f = _final(out3, x, loss_target, gtab, W["norm_post"], L)

    dout2 = dout.reshape(2 * L, D)
    g = {}
    g["w_out"] = _matmul_tn(merged2, dout2, "dw_out", ta=1024, tn=1024, tr=tl)
    d_merged = _matmul(dout2, W["w_out"], F32, "d_merged", tm=tl, tn=1024, bt=True).reshape(2, L, D)
    dP, dS, dgp, acc_m = _merge_bwd(d_merged, proj3, P, S, W["b_merge"], L)
    dP2, dS2 = dP.reshape(2 * L, D), dS.reshape(2 * L, D)
    g["w_proj_pool"] = _matmul_tn(ypool2, dP2, "dw_proj_pool", ta=1024, tn=1024, tr=tl)
    g["w_proj_ssd"] = _matmul_tn(yn2, dS2, "dw_proj_ssd", ta=1024, tn=1024, tr=tl)
    d_ypool = _matmul(dP2, W["w_proj_pool"], F32, "d_ypool", tm=tl, tn=1024, bt=True).reshape(2, L, D)
    d_yn = _matmul(dS2, W["w_proj_ssd"], F32, "d_yn", tm=tl, tn=1024, bt=True).reshape(2, L, DIN)
    dv, dzp, g["pool_w"], acc_p = _pool_bwd(proj3, d_ypool, W["pool_w"], jnp.swapaxes(W["pool_w"], 1, 2),
                                            W["pool_scale"], tables, L)
    dy2, dxs_skip, dzs, acc_s = _ssd_post_bwd(d_yn, y_f, y_b, xbc, proj3, dskip_e, W["ssd_norm"], L)
    dxs_f, dbc_f, ddt_f, dxs_b, dbc_b, ddt_b, acc_a = _ssd_bwd(xbc, dt_loc, a_loc, hs_f, hs_b, dy2, L)
    ident = lambda j: j
    dxr_xs, acc_cx = _conv_bwd(proj3, [dxs_f, dxs_b, dxs_skip], 0, DIN, [ident, ident, ident],
                               W["conv_w"], W["conv_b"], L, "conv_bwd_xs")
    bcmap = lambda j: 2 * lax.rem(j, NG) + j // NG
    dxr_bc, acc_cb = _conv_bwd(proj3, [dbc_f, dbc_b], DIN, 2 * NG * NST, [bcmap, bcmap],
                               W["conv_w"], W["conv_b"], L, "conv_bwd_bc")
    ddtr, acc_d = _dt_bwd(dt_raw, bias128, from_loc(ddt_f), from_loc(ddt_b))
    pieces = [dv, dzp, dzs, dgp, dxr_xs, dxr_bc]
    dw_rows = [_matmul_tn(p.reshape(2 * R, p.shape[2]), hx2, "dw_in_%d" % i, ta=1024, tn=1024, tr=tr)
               for i, p in enumerate(pieces)]
    dw_rows.append(_matmul_tn(ddtr.reshape(2 * R, 128), hx2, "dw_in_dt", ta=128, tn=1024, tr=tr)[:64])
    g["w_in"] = jnp.concatenate(dw_rows, axis=0)
    dh = _dhx(pieces, ddtr, w_inT, w_dtT)
    grad_x, acc_n = _norm_mod_bwd(dh, x, ctx, tab, dxo)
    g["w_ada"], db_rows, sm_rows = _adaln_bwd(acc_n, acc_f, mod16, c16, npre, W["w_ada"])

    g["b_ada"] = db_rows[0:1]
    g["norm_pre"] = sm_rows[0:1]
    g["c_ctx"] = sm_rows[1]
    g["norm_post"] = acc_f[0, 1:2] + acc_f[1, 1:2]
    g["b_merge"] = acc_m[0, 0:1] + acc_m[1, 0:1]
    g["pool_scale"] = acc_p[:, 0, :].reshape(1, D)
    acc_c = jnp.concatenate([acc_cx[0] + acc_cx[1], acc_cb[0] + acc_cb[1]], axis=1)
    g["conv_w"] = acc_c[0:4]
    g["conv_b"] = acc_c[4:5]
    g["dt_bias"] = (acc_d[0, 0, :64] + acc_d[1, 0, :64]).reshape(2, 32)
    dA = (acc_a[0, :, 0, :16] + acc_a[1, :, 0, :16]).reshape(NG, 2, HPG).transpose(1, 0, 2)
    g["a_log"] = (dA * A).reshape(2, 32)
    g["d_skip"] = (acc_s[0, 1] + acc_s[1, 1]).reshape(32, HEAD).sum(axis=1).reshape(1, 32)
    g["ssd_norm"] = acc_s[0, 0:1] + acc_s[1, 0:1]
    loss_lanes = acc_f[:, 2, :]
    return loss_lanes, grad_x, g


MESH = pl.DeviceIdType.MESH
ANY = pl.BlockSpec(memory_space=pl.ANY)


def _all_gather(shard):
    m_per, n = shard.shape

    def body(x_ref, out_ref, send_sems, recv_sems, local_sem):
        x, y, c = lax.axis_index("x"), lax.axis_index("y"), lax.axis_index("c")
        me, sibling = (x, y, c), (x, y, 1 - c)
        chips = [(1 - x, y), (x, 1 - y), (1 - x, 1 - y)]

        def rows(px, py, pc):
            return out_ref.at[pl.ds((4 * px + 2 * py + pc) * m_per, m_per), :]

        def copy(k, block, to, src=None):
            return pltpu.make_async_remote_copy(
                src_ref=rows(*block) if src is None else src, dst_ref=rows(*block),
                send_sem=send_sems.at[k], recv_sem=recv_sems.at[k], device_id=to, device_id_type=MESH)

        mine = pltpu.make_async_copy(x_ref, rows(*me), local_sem)
        mine.start()
        first = [copy(0, me, sibling, src=x_ref)]
        first += [copy(1 + j, me, (*chip, c), src=x_ref) for j, chip in enumerate(chips)]
        for cp in first:
            cp.start()
        passed = [copy(4 + j, (*chip, c), sibling) for j, chip in enumerate(chips)]
        for j, chip in enumerate(chips):
            copy(1 + j, (*chip, c), me).wait_recv()
            passed[j].start()
        copy(0, sibling, me).wait_recv()
        for j, chip in enumerate(chips):
            copy(4 + j, (*chip, 1 - c), me).wait_recv()
        for cp in first + passed:
            cp.wait_send()
        mine.wait()

    return pl.pallas_call(
        body, name="all_gather_weights",
        out_shape=jax.ShapeDtypeStruct((NDEV * m_per, n), shard.dtype),
        in_specs=[ANY], out_specs=ANY,
        scratch_shapes=[pltpu.SemaphoreType.DMA((7,)), pltpu.SemaphoreType.DMA((7,)), pltpu.SemaphoreType.DMA],
    )(shard)


def _all_to_all(big, small):
    def body(big_ref, small_ref, obig_ref, osmall_ref, send_sems, recv_sems, local_sems):
        x, y, c = lax.axis_index("x"), lax.axis_index("y"), lax.axis_index("c")
        me = 4 * x + 2 * y + c
        mine = [pltpu.make_async_copy(big_ref.at[me], obig_ref.at[me], local_sems.at[0]),
                pltpu.make_async_copy(small_ref, osmall_ref.at[me], local_sems.at[1])]
        for cp in mine:
            cp.start()

        def copies(k):
            px = 1 - x if k & 4 else x
            py = 1 - y if k & 2 else y
            pc = 1 - c if k & 1 else c
            peer = 4 * px + 2 * py + pc

            def rc(src, dst, sem):
                return pltpu.make_async_remote_copy(src_ref=src, dst_ref=dst, send_sem=send_sems.at[sem],
                                                    recv_sem=recv_sems.at[sem], device_id=(px, py, pc), device_id_type=MESH)

            sends = [rc(big_ref.at[peer], obig_ref.at[me], k - 1), rc(small_ref, osmall_ref.at[me], 6 + k)]
            recvs = [rc(big_ref.at[peer], obig_ref.at[peer], k - 1), rc(small_ref, osmall_ref.at[peer], 6 + k)]
            return sends, recvs

        allc = [copies(k) for k in range(1, NDEV)]
        for sends, _ in allc:
            for cp in sends:
                cp.start()
        for sends, _ in allc:
            for cp in sends:
                cp.wait_send()
        for _, recvs in allc:
            for cp in recvs:
                cp.wait_recv()
        for cp in mine:
            cp.wait()

    return pl.pallas_call(
        body, name="all_to_all_grads",
        out_shape=(jax.ShapeDtypeStruct(big.shape, big.dtype), jax.ShapeDtypeStruct((NDEV,) + small.shape, small.dtype)),
        in_specs=[ANY, ANY], out_specs=(ANY, ANY),
        scratch_shapes=[pltpu.SemaphoreType.DMA((14,)), pltpu.SemaphoreType.DMA((14,)), pltpu.SemaphoreType.DMA((2,))],
    )(big, small)


ADAM_TILE = 64
PACK_W = 1024


def _adamw(recv, w, m, v, name):
    rp = w.shape[0]
    tile = min(ADAM_TILE, rp)

    def body(r_ref, w_ref, m_ref, v_ref, g_ref, d_ref, nm_ref, nv_ref):
        g = r_ref[0].astype(F32)
        for i in range(1, NDEV):
            g = g + r_ref[i].astype(F32)
        m1 = ADAM_B1 * m_ref[...] + (1.0 - ADAM_B1) * g
        v1 = ADAM_B2 * v_ref[...] + (1.0 - ADAM_B2) * (g * g)
        m_hat = m1 / (1.0 - ADAM_B1 ** ADAM_STEP)
        v_hat = v1 / (1.0 - ADAM_B2 ** ADAM_STEP)
        g_ref[...] = g
        d_ref[...] = -ADAM_LR * (m_hat / (jnp.sqrt(v_hat) + ADAM_EPS) + ADAM_WD * w_ref[...])
        nm_ref[...] = m1
        nv_ref[...] = v1

    blk = pl.BlockSpec((tile, PACK_W), lambda i: (i, 0))
    shp = jax.ShapeDtypeStruct((rp, PACK_W), F32)
    return pl.pallas_call(
        body, name=name, grid=(rp // tile,),
        in_specs=[pl.BlockSpec((NDEV, tile, PACK_W), lambda i: (0, i, 0)), blk, blk, blk],
        out_specs=(blk, blk, blk, blk), out_shape=(shp, shp, shp, shp),
        compiler_params=_params(("parallel",)),
    )(recv, w, m, v)


BIG = {"w_ada": ((D, 3 * D), 1), "w_in": ((IN_COLS, D), 0), "pool_w": ((4, PGW, PGW), 1), "conv_w": ((4, CONV_DIM), 1),
       "w_proj_pool": ((D, D), 0), "w_proj_ssd": ((DIN, D), 0), "w_out": ((D, D), 0)}
SMALL = {"c_ctx": (D,), "b_ada": (1, 3 * D), "norm_pre": (1, D), "norm_post": (1, D), "b_merge": (1, 2 * D),
         "pool_scale": (1, D), "conv_b": (1, CONV_DIM), "dt_bias": (2, 32), "a_log": (2, 32), "d_skip": (1, 32),
         "ssd_norm": (1, DIN)}
LOSS_SLOT = 128
GATHER_BF16 = ("w_ada", "w_in", "pool_w", "w_proj_pool", "w_proj_ssd", "w_out")


def _shard_shape(name):
    shape, ax = BIG[name]
    return tuple(s // NDEV if i == ax else s for i, s in enumerate(shape))


def _to_chunks(full, name):
    shape, ax = BIG[name]
    split = shape[:ax] + (NDEV, shape[ax] // NDEV) + shape[ax + 1:]
    return jnp.moveaxis(full.reshape(split), ax, 0).reshape(NDEV, -1)


def _from_chunks(chunks, name):
    shape, ax = BIG[name]
    sh = _shard_shape(name)
    return jnp.moveaxis(chunks.reshape((NDEV,) + sh), 0, ax).reshape(shape)


def _pad_rows(flat, mult):
    n = flat.shape[-1]
    rows = -(-n // (PACK_W * mult)) * mult
    pad = [(0, 0)] * (flat.ndim - 1) + [(0, rows * PACK_W - n)]
    return jnp.pad(flat, pad).reshape(flat.shape[:-1] + (rows, PACK_W))


def _pack_state(t):
    big = _pad_rows(jnp.concatenate([t[n].reshape(-1) for n in BIG]), ADAM_TILE)
    small = _pad_rows(jnp.concatenate([t[n].reshape(-1) for n in SMALL] + [jnp.zeros((LOSS_SLOT,), F32)]), 16)
    return big, small


def _pack_grads(g, loss_part):
    big = _pad_rows(jnp.concatenate([_to_chunks(g[n], n).astype(BF) for n in BIG], axis=1), ADAM_TILE)
    small = [g[n].reshape(-1) for n in SMALL] + [jnp.zeros((LOSS_SLOT,), F32).at[0].set(loss_part)]
    return big, _pad_rows(jnp.concatenate(small), 16)


def _unpack_state(big, small):
    out, off = {}, 0
    flat = big.reshape(-1)
    for n in BIG:
        sh = _shard_shape(n)
        k = int(np.prod(sh))
        out[n] = flat[off:off + k].reshape(sh)
        off += k
    flat, off = small.reshape(-1), 0
    for n, sh in SMALL.items():
        k = int(np.prod(sh))
        out[n] = flat[off:off + k].reshape(sh)
        off += k
    out["loss"] = flat[off]
    return out


def _pack_gather(w):
    flat = [w[n].reshape(-1).astype(BF) for n in GATHER_BF16] + [p.reshape(-1) for p in _split(w["conv_w"], 3)]
    return _pad_rows(jnp.concatenate(flat), 16)


def _unpack_gather(gathered):
    g = gathered.reshape(NDEV, -1)
    out, off = {}, 0
    for n in GATHER_BF16:
        k = int(np.prod(_shard_shape(n)))
        out[n] = _from_chunks(g[:, off:off + k], n)
        off += k
    k = int(np.prod(_shard_shape("conv_w")))
    parts = [_from_chunks(g[:, off + i * k:off + (i + 1) * k], "conv_w").astype(F32) for i in range(3)]
    out["conv_w"] = parts[0] + parts[1] + parts[2]
    return out


PARAMS = ["c_ctx", "w_ada", "b_ada", "norm_pre", "norm_post", "w_in", "b_merge", "pool_w", "pool_scale", "conv_w", "conv_b",
          "dt_bias", "a_log", "d_skip", "ssd_norm", "w_proj_pool", "w_proj_ssd", "w_out"]


def kernel(x, c, ctx, c_ctx, w_ada, b_ada, norm_pre, norm_post, w_in, b_merge, pool_w, pool_scale, conv_w, conv_b, dt_bias, a_log, d_skip, ssd_norm, w_proj_pool, w_proj_ssd, w_out, loss_target, m_c_ctx, m_w_ada, m_b_ada, m_norm_pre, m_norm_post, m_w_in, m_b_merge, m_pool_w, m_pool_scale, m_conv_w, m_conv_b, m_dt_bias, m_a_log, m_d_skip, m_ssd_norm, m_w_proj_pool, m_w_proj_ssd, m_w_out, v_c_ctx, v_w_ada, v_b_ada, v_norm_pre, v_norm_post, v_w_in, v_b_merge, v_pool_w, v_pool_scale, v_conv_w, v_conv_b, v_dt_bias, v_a_log, v_d_skip, v_ssd_norm, v_w_proj_pool, v_w_proj_ssd, v_w_out):
    given = dict(locals())
    shapes = {n: given[n].shape for n in PARAMS}

    def local(prefix):
        t = {n: (given[prefix + n] if n == "c_ctx" else given[prefix + n][0]) for n in PARAMS}
        t["w_in"] = t["w_in"].T
        return {n: t[n].reshape(_shard_shape(n) if n in BIG else SMALL[n]) for n in PARAMS}

    w, m, v = local(""), local("m_"), local("v_")

    W = _unpack_gather(_all_gather(_pack_gather(w)))
    for n in SMALL:
        W[n] = w[n]
    lanes, grad_x, g = _local_step(x, c, ctx, loss_target, W)
    recv_big, recv_small = _all_to_all(*_pack_grads(g, (0.5 / D) * jnp.sum(lanes)))
    (wb, ws), (mb, ms), (vb, vs) = _pack_state(w), _pack_state(m), _pack_state(v)
    res = [_unpack_state(b, s) for b, s in zip(_adamw(recv_big, wb, mb, vb, "adamw_big"),
                                               _adamw(recv_small, ws, ms, vs, "adamw_small"))]
    outs = [res[0]["loss"], grad_x]
    for r in res:
        r["w_in"] = r["w_in"].T
        outs += [r[n].reshape(shapes[n]) for n in PARAMS]
    return tuple(outs)
```

```python
import functools

import numpy as np
import jax
import jax.numpy as jnp
from jax import lax
from jax.experimental import pallas as pl
from jax.experimental.pallas import tpu as pltpu

F32, BF = jnp.float32, jnp.bfloat16

D = 1024
GRID_W = 64
EPS = 1e-6
POOL_WINDOWS = (2, 4, 8, 16)
PGW = 256
DIN = 2048
HEAD = 64
NST = 128
NG = 4
HPG = 8
GWID = HPG * HEAD
Q = 128
CONV_DIM = 3072
OFF_GATE, OFF_XBC, OFF_DT, IN_COLS = 4096, 6144, 9216, 9280
NDEV = 8
ADAM_LR, ADAM_B1, ADAM_B2, ADAM_EPS, ADAM_WD, ADAM_STEP = 0.001, 0.9, 0.999, 1e-08, 0.01, 10

V7X_VMEM_LIMIT = 56 * 2 ** 20
ROW_TILE = 256


def _params(sem=None):
    return pltpu.CompilerParams(dimension_semantics=sem, vmem_limit_bytes=V7X_VMEM_LIMIT)


def _dot(a, b):
    return jnp.dot(a.astype(BF), b.astype(BF), preferred_element_type=F32)


def _dot_nt(a, b):
    return lax.dot_general(a.astype(BF), b.astype(BF), (((1,), (1,)), ((), ())), preferred_element_type=F32)


def _dot_tn(a, b):
    return lax.dot_general(a.astype(BF), b.astype(BF), (((0,), (0,)), ((), ())), preferred_element_type=F32)


def _split(a, n):
    parts = []
    for _ in range(n):
        p = a.astype(BF)
        parts.append(p)
        a = a - p.astype(F32)
    return parts


def _dot_sl(a, b01, n=3):
    return sum(jnp.dot(p, b01, preferred_element_type=F32) for p in _split(a, n))


def _dot_sr(a01, b, n=3):
    return sum(jnp.dot(a01, p, preferred_element_type=F32) for p in _split(b, n))


def _dot_tn_sl(a, b01, n=2):
    return sum(lax.dot_general(p, b01, (((0,), (0,)), ((), ())), preferred_element_type=F32) for p in _split(a, n))


def _sigmoid(x):
    return 1.0 / (1.0 + jnp.exp(-x))


def _matmul(a, b, out_dtype, name, tm=512, tn=512, tk=1024, bt=False, n=None):
    M, K = a.shape
    N = n if n is not None else (b.shape[0] if bt else b.shape[1])
    tm, tn, tk = min(tm, M), min(tn, N), min(tk, K)
    assert M % tm == 0 and N % tn == 0 and K % tk == 0, (a.shape, b.shape)
    nk = K // tk

    def body(a_ref, b_ref, o_ref, acc):
        k = pl.program_id(2)
        p = _dot_nt(a_ref[...], b_ref[...]) if bt else _dot(a_ref[...], b_ref[...])

        @pl.when(k == 0)
        def _():
            acc[...] = p

        @pl.when(k > 0)
        def _():
            acc[...] += p

        @pl.when(k == nk - 1)
        def _():
            o_ref[...] = acc[...].astype(o_ref.dtype)

    return pl.pallas_call(
        body, name=name, grid=(M // tm, N // tn, nk),
        in_specs=[pl.BlockSpec((tm, tk), lambda i, j, k: (i, k)),
                  pl.BlockSpec((tn, tk), lambda i, j, k: (j, k)) if bt else pl.BlockSpec((tk, tn), lambda i, j, k: (k, j))],
        out_specs=pl.BlockSpec((tm, tn), lambda i, j, k: (i, j)),
        out_shape=jax.ShapeDtypeStruct((M, N), out_dtype),
        scratch_shapes=[pltpu.VMEM((tm, tn), F32)],
        compiler_params=_params(("parallel", "parallel", "arbitrary")),
    )(a, b)


def _matmul_tn(a, g, name, ta=512, tn=512, tr=512):
    M, Ka = a.shape
    N = g.shape[1]
    ta, tn, tr = min(ta, Ka), min(tn, N), min(tr, M)
    assert M % tr == 0 and N % tn == 0 and Ka % ta == 0, (a.shape, g.shape)
    nr = M // tr

    def body(a_ref, g_ref, o_ref):
        k = pl.program_id(2)
        p = _dot_tn(a_ref[...], g_ref[...])

        @pl.when(k == 0)
        def _():
            o_ref[...] = p

        @pl.when(k > 0)
        def _():
            o_ref[...] += p

    return pl.pallas_call(
        body, name=name, grid=(Ka // ta, N // tn, nr),
        in_specs=[pl.BlockSpec((tr, ta), lambda i, j, k: (k, i)), pl.BlockSpec((tr, tn), lambda i, j, k: (k, j))],
        out_specs=pl.BlockSpec((ta, tn), lambda i, j, k: (i, j)),
        out_shape=jax.ShapeDtypeStruct((Ka, N), F32),
        compiler_params=_params(("parallel", "parallel", "arbitrary")),
    )(a, g)


def _dhx(pieces, ddt, w_inT, w_dtT):
    _, R, _ = pieces[0].shape
    tm = R // 8
    kb = 1024
    starts, nblk = [], []
    for p in pieces:
        starts.append(sum(nblk))
        nblk.append(p.shape[2] // kb)
    nk = sum(nblk)
    assert nk * kb == OFF_DT and R % 128 == 0
    npc = len(pieces)

    def body(*refs):
        a_refs, dt_ref, w_ref, wdt_ref, o_ref, acc = refs[:npc], refs[npc], refs[npc + 1], refs[npc + 2], refs[npc + 3], refs[npc + 4]
        k = pl.program_id(2)

        @pl.when(k == 0)
        def _():
            acc[...] = _dot(dt_ref[0], wdt_ref[...])

        for p in range(npc):
            @pl.when((k >= starts[p]) & (k < starts[p] + nblk[p]))
            def _(p=p):
                acc[...] += _dot(a_refs[p][0], w_ref[...])

        @pl.when(k == nk - 1)
        def _():
            o_ref[0] = acc[...]

    in_specs = [pl.BlockSpec((1, tm, kb), functools.partial(
        lambda e, t, k, s, nb: (e, t, jnp.clip(k - s, 0, nb - 1)), s=starts[p], nb=nblk[p])) for p in range(npc)]
    in_specs += [pl.BlockSpec((1, tm, 128), lambda e, t, k: (e, t, 0)),
                 pl.BlockSpec((kb, D), lambda e, t, k: (k, 0)),
                 pl.BlockSpec((128, D), lambda e, t, k: (0, 0))]
    return pl.pallas_call(
        body, name="d_hx", grid=(2, R // tm, nk), in_specs=in_specs,
        out_specs=pl.BlockSpec((1, tm, D), lambda e, t, k: (e, t, 0)),
        out_shape=jax.ShapeDtypeStruct((2, R, D), F32),
        scratch_shapes=[pltpu.VMEM((tm, D), F32)],
        compiler_params=_params(("parallel", "parallel", "arbitrary")),
    )(*pieces, ddt, w_inT, w_dtT)


def _adaln_fwd(c16, w_adaT_bf, b_ada):
    def body(c_ref, w_ref, b_ref, o_ref):
        cc = c_ref[...]
        o_ref[...] = _dot_nt(cc * _sigmoid(cc), w_ref[...]) + b_ref[...]

    return pl.pallas_call(body, name="adaln_fwd", out_shape=jax.ShapeDtypeStruct((16, 3 * D), F32),
                          compiler_params=_params())(c16, w_adaT_bf, b_ada)


def _adaln_bwd(acc_n, acc_f, mod16, c16, norm_pre, w_adaT_bf):
    def body(an_ref, af_ref, mod_ref, c_ref, np_ref, wt_ref, dw_ref, db_ref, sm_ref, dmod):
        npre = np_ref[...]
        dmod[...] = jnp.zeros_like(dmod)
        dnp = jnp.zeros((1, D), F32)
        dshift_c = jnp.zeros((1, D), F32)
        dgpre_c = jnp.zeros((1, D), F32)
        scale_c = mod_ref[2:3, D:2 * D]
        for e in range(2):
            dg_x, ds_x = an_ref[e, 0, 0:1, :], an_ref[e, 0, 1:2, :]
            dg_c, ds_c = an_ref[e, 1, 0:1, :], an_ref[e, 1, 1:2, :]
            dmod[e:e + 1, 0:D] = ds_x
            dmod[e:e + 1, D:2 * D] = dg_x * npre
            dmod[e:e + 1, 2 * D:3 * D] = af_ref[e, 0:1, :]
            dnp = dnp + dg_x * (1.0 + mod_ref[e:e + 1, D:2 * D]) + dg_c * (1.0 + scale_c)
            dshift_c = dshift_c + ds_c
            dgpre_c = dgpre_c + dg_c
        dmod[2:3, 0:D] = dshift_c
        dmod[2:3, D:2 * D] = dgpre_c * npre
        dm = dmod[...]
        cc = c_ref[...]
        sg = _sigmoid(cc)
        dw_ref[...] = _dot_tn(dm, cc * sg)
        db_ref[...] = jnp.zeros_like(db_ref)
        db_ref[0:1, :] = jnp.sum(dm, axis=0, keepdims=True)
        dsilu = sg * (1.0 + cc * (1.0 - sg))
        dcs = _dot(dm, wt_ref[...]) * dsilu
        sm_ref[...] = jnp.zeros_like(sm_ref)
        sm_ref[0:1, :] = dnp
        sm_ref[1:2, :] = dcs[2:3, :]

    return pl.pallas_call(
        body, name="adaln_bwd",
        out_shape=(jax.ShapeDtypeStruct((3 * D, D), F32), jax.ShapeDtypeStruct((16, 3 * D), F32),
                   jax.ShapeDtypeStruct((8, D), F32)),
        scratch_shapes=[pltpu.VMEM((16, 3 * D), F32)],
        compiler_params=_params())(acc_n, acc_f, mod16, c16, norm_pre, w_adaT_bf)


def _row_specs(L):
    nx = L // ROW_TILE
    return (pl.BlockSpec((1, ROW_TILE, D), lambda e, t: (e, jnp.minimum(t, nx - 1), 0)),
            pl.BlockSpec((1, ROW_TILE, D), lambda e, t: (e, jnp.maximum(t - nx, 0), 0)))


def _norm_mod_fwd(x, ctx, tab):
    L = x.shape[1]
    R = L + ctx.shape[1]
    nx = L // ROW_TILE

    def body(x_ref, c_ref, t_ref, o_ref):
        x = jnp.where(pl.program_id(1) < nx, x_ref[0], c_ref[0])
        r = lax.rsqrt(jnp.mean(x * x, axis=-1, keepdims=True) + EPS)
        t = t_ref[0, 0]
        o_ref[0] = (x * r * t[0:1] + t[1:2]).astype(BF)

    return pl.pallas_call(
        body, name="norm_mod_fwd", grid=(2, R // ROW_TILE),
        in_specs=[*_row_specs(L), pl.BlockSpec((1, 1, 8, D), lambda e, t: (e, t // nx, 0, 0))],
        out_specs=pl.BlockSpec((1, ROW_TILE, D), lambda e, t: (e, t, 0)),
        out_shape=jax.ShapeDtypeStruct((2, R, D), BF),
        compiler_params=_params(("parallel", "parallel")),
    )(x, ctx, tab)


def _norm_mod_bwd(dh, x, ctx, tab, dxo):
    L = x.shape[1]
    R = L + ctx.shape[1]
    nx = L // ROW_TILE

    def body(dh_ref, x_ref, c_ref, t_ref, dxo_ref, gx_ref, acc_ref):
        t = pl.program_id(1)
        x = jnp.where(t < nx, x_ref[0], c_ref[0])
        r = lax.rsqrt(jnp.mean(x * x, axis=-1, keepdims=True) + EPS)
        xn = x * r
        dh = dh_ref[0]

        @pl.when((t == 0) | (t == nx))
        def _():
            acc_ref[...] = jnp.zeros_like(acc_ref)

        acc_ref[0, 0, 0:1, :] += jnp.sum(dh * xn, axis=0, keepdims=True)
        acc_ref[0, 0, 1:2, :] += jnp.sum(dh, axis=0, keepdims=True)

        @pl.when(t < nx)
        def _():
            dxn = dh * t_ref[0, 0][0:1]
            dx = r * (dxn - xn * jnp.mean(dxn * xn, axis=-1, keepdims=True))
            gx_ref[0] = dxo_ref[0] + dx

    xspec, cspec = _row_specs(L)
    return pl.pallas_call(
        body, name="norm_mod_bwd", grid=(2, R // ROW_TILE),
        in_specs=[pl.BlockSpec((1, ROW_TILE, D), lambda e, t: (e, t, 0)), xspec, cspec,
                  pl.BlockSpec((1, 1, 8, D), lambda e, t: (e, t // nx, 0, 0)), xspec],
        out_specs=(xspec, pl.BlockSpec((1, 1, 8, D), lambda e, t: (e, t // nx, 0, 0))),
        out_shape=(jax.ShapeDtypeStruct((2, L, D), F32), jax.ShapeDtypeStruct((2, 2, 8, D), F32)),
        compiler_params=_params(("parallel", "arbitrary")),
    )(dh, x, ctx, tab, dxo)


POOL_TILE = 256
POOL_PAD = 8 * GRID_W


def _pool_tables(L):
    rows = L // GRID_W
    mats = np.zeros((4, POOL_TILE, POOL_TILE), np.float32)
    wts = np.zeros((4, 2, 17), np.float32)
    inv = np.zeros((4, L, 1), np.float32)
    for gi, k in enumerate(POOL_WINDOWS):
        lo, hi = k // 2, k - 1 - k // 2
        m = np.zeros((GRID_W, GRID_W), np.float32)
        for t in range(GRID_W):
            m[t, max(t - lo, 0):min(t + hi, GRID_W - 1) + 1] = 1.0
        for b in range(POOL_TILE // GRID_W):
            mats[gi, b * GRID_W:(b + 1) * GRID_W, b * GRID_W:(b + 1) * GRID_W] = m
        for o in range(-lo, hi + 1):
            wts[gi, 0, o + 8] = 1.0
            wts[gi, 1, -o + 8] = 1.0
        cnt_c = m.sum(1)
        cnt_r = np.array([min(r + hi, rows - 1) - max(r - lo, 0) + 1 for r in range(rows)], np.float32)
        inv[gi, :, 0] = (1.0 / (cnt_r[:, None] * cnt_c[None, :])).reshape(-1)
    matsT = np.ascontiguousarray(np.transpose(mats, (0, 2, 1)))
    return (jnp.asarray(mats, BF), jnp.asarray(matsT, BF), jnp.asarray(wts), jnp.asarray(inv))


def _pool_cols(get_tile, mat, pad_ref, L, n):
    pad_ref[pl.ds(0, POOL_PAD), :] = jnp.zeros((POOL_PAD, PGW), F32)
    pad_ref[pl.ds(POOL_PAD + L, POOL_PAD), :] = jnp.zeros((POOL_PAD, PGW), F32)

    def step(i, carry):
        off = pl.multiple_of(i * POOL_TILE, POOL_TILE)
        pad_ref[pl.ds(POOL_PAD + off, POOL_TILE), :] = _dot_sr(mat, get_tile(off).astype(F32), n)
        return carry

    lax.fori_loop(0, L // POOL_TILE, step, 0)


def _pool_rows(pad_ref, wt, off, n):
    acc = jnp.zeros((n, PGW), F32)
    for o in range(-8, 9):
        acc = acc + wt[:, o + 8:o + 9] * pad_ref[pl.ds(POOL_PAD + GRID_W * o + off, n), :]
    return acc


def _pool_fwd(proj3, pool_w_bf, pool_scale, tables, L):
    mats, _, wts, inv = tables
    nt = L // POOL_TILE

    def body(v_ref, z_ref, pw_ref, ps_ref, m_ref, wt_ref, inv_ref, o_ref, pad_ref):
        _pool_cols(lambda off: v_ref[0, pl.ds(off, POOL_TILE), :], m_ref[0], pad_ref, L, 1)
        wt = wt_ref[0, 0:1, :]

        def step(i, carry):
            off = pl.multiple_of(i * POOL_TILE, POOL_TILE)
            rows = pl.ds(off, POOL_TILE)
            v = v_ref[0, rows, :].astype(F32)
            diff = _pool_rows(pad_ref, wt, off, POOL_TILE) * inv_ref[0, rows, :] - v
            yp = _dot(diff, pw_ref[0])
            z = z_ref[0, rows, :].astype(F32)
            o_ref[0, rows, :] = (yp * ps_ref[...] * (z * _sigmoid(z))).astype(BF)
            return carry

        lax.fori_loop(0, nt, step, 0)

    return pl.pallas_call(
        body, name="pool_fwd", grid=(2, 4),
        in_specs=[pl.BlockSpec((1, L, PGW), lambda e, g: (e, 0, g)),
                  pl.BlockSpec((1, L, PGW), lambda e, g: (e, 0, 4 + g)),
                  pl.BlockSpec((1, PGW, PGW), lambda e, g: (g, 0, 0)),
                  pl.BlockSpec((1, PGW), lambda e, g: (0, g)),
                  pl.BlockSpec((1, POOL_TILE, POOL_TILE), lambda e, g: (g, 0, 0)),
                  pl.BlockSpec((1, 2, 17), lambda e, g: (g, 0, 0)),
                  pl.BlockSpec((1, L, 1), lambda e, g: (g, 0, 0))],
        out_specs=pl.BlockSpec((1, L, PGW), lambda e, g: (e, 0, g)),
        out_shape=jax.ShapeDtypeStruct((2, L, D), BF),
        scratch_shapes=[pltpu.VMEM((L + 2 * POOL_PAD, PGW), F32)],
        compiler_params=_params(("parallel", "parallel")),
    )(proj3, proj3, pool_w_bf, pool_scale, mats, wts, inv)


def _pool_bwd(proj3, d_ypool, pool_w_bf, pool_wT_bf, pool_scale, tables, L):
    mats, matsT, wts, inv = tables
    nt = L // POOL_TILE
    R = proj3.shape[1]

    def body(v_ref, z_ref, dy_ref, pw_ref, pwt_ref, ps_ref, m_ref, mt_ref, wt_ref, inv_ref,
             dv_ref, dz_ref, dpw_ref, acc_ref, pad_ref, dd_ref):
        e = pl.program_id(1)

        @pl.when(e == 0)
        def _():
            dpw_ref[...] = jnp.zeros_like(dpw_ref)
            acc_ref[...] = jnp.zeros_like(acc_ref)

        _pool_cols(lambda off: v_ref[0, pl.ds(off, POOL_TILE), :], m_ref[0], pad_ref, L, 1)
        wt = wt_ref[0, 0:1, :]
        ps = ps_ref[...]

        def step(i, carry):
            off = pl.multiple_of(i * POOL_TILE, POOL_TILE)
            rows = pl.ds(off, POOL_TILE)
            v = v_ref[0, rows, :].astype(F32)
            diff = _pool_rows(pad_ref, wt, off, POOL_TILE) * inv_ref[0, rows, :] - v
            yp = _dot(diff, pw_ref[0])
            z = z_ref[0, rows, :].astype(F32)
            sg = _sigmoid(z)
            sz = z * sg
            dy = dy_ref[0, rows, :]
            dz_ref[0, rows, :] = (dy * yp * ps * (sg * (1.0 + z * (1.0 - sg)))).astype(BF)
            dys = dy * sz
            acc_ref[0, 0:1, :] += jnp.sum(dys * yp, axis=0, keepdims=True)
            dyp = dys * ps
            dpw_ref[0] += _dot_tn(diff, dyp)
            dd_ref[rows, :] = _dot(dyp, pwt_ref[0])
            return carry

        lax.fori_loop(0, nt, step, 0)
        _pool_cols(lambda off: dd_ref[pl.ds(off, POOL_TILE), :] * inv_ref[0, pl.ds(off, POOL_TILE), :],
                   mt_ref[0], pad_ref, L, 2)
        wtt = wt_ref[0, 1:2, :]

        def step2(i, carry):
            off = pl.multiple_of(i * POOL_TILE, POOL_TILE)
            rows = pl.ds(off, POOL_TILE)
            dv_ref[0, rows, :] = (_pool_rows(pad_ref, wtt, off, POOL_TILE) - dd_ref[rows, :]).astype(BF)
            return carry

        lax.fori_loop(0, nt, step2, 0)
        dv_ref[0, pl.ds(L, R - L), :] = jnp.zeros((R - L, PGW), BF)
        dz_ref[0, pl.ds(L, R - L), :] = jnp.zeros((R - L, PGW), BF)

    return pl.pallas_call(
        body, name="pool_bwd", grid=(4, 2),
        in_specs=[pl.BlockSpec((1, L, PGW), lambda g, e: (e, 0, g)),
                  pl.BlockSpec((1, L, PGW), lambda g, e: (e, 0, 4 + g)),
                  pl.BlockSpec((1, L, PGW), lambda g, e: (e, 0, g)),
                  pl.BlockSpec((1, PGW, PGW), lambda g, e: (g, 0, 0)),
                  pl.BlockSpec((1, PGW, PGW), lambda g, e: (g, 0, 0)),
                  pl.BlockSpec((1, PGW), lambda g, e: (0, g)),
                  pl.BlockSpec((1, POOL_TILE, POOL_TILE), lambda g, e: (g, 0, 0)),
                  pl.BlockSpec((1, POOL_TILE, POOL_TILE), lambda g, e: (g, 0, 0)),
                  pl.BlockSpec((1, 2, 17), lambda g, e: (g, 0, 0)),
                  pl.BlockSpec((1, L, 1), lambda g, e: (g, 0, 0))],
        out_specs=(pl.BlockSpec((1, R, PGW), lambda g, e: (e, 0, g)),
                   pl.BlockSpec((1, R, PGW), lambda g, e: (e, 0, g)),
                   pl.BlockSpec((1, PGW, PGW), lambda g, e: (g, 0, 0)),
                   pl.BlockSpec((1, 8, PGW), lambda g, e: (g, 0, 0))),
        out_shape=(jax.ShapeDtypeStruct((2, R, D), BF), jax.ShapeDtypeStruct((2, R, D), BF),
                   jax.ShapeDtypeStruct((4, PGW, PGW), F32), jax.ShapeDtypeStruct((4, 8, PGW), F32)),
        scratch_shapes=[pltpu.VMEM((L + 2 * POOL_PAD, PGW), F32), pltpu.VMEM((L, PGW), F32)],
        compiler_params=_params(("parallel", "arbitrary")),
    )(proj3, proj3, d_ypool, pool_w_bf, pool_wT_bf, pool_scale, mats, matsT, wts, inv)


CONV_BLOCK = 128


def _conv_tap(u, k, L):
    off = k - 2
    if off == 0:
        return u
    R = u.shape[0]
    r = lax.broadcasted_iota(jnp.int32, (R, 1), 0)
    pos = jnp.where(r < L, r, r - L) + off
    seg = jnp.where(r < L, L, R - L)
    return jnp.where((pos >= 0) & (pos < seg), pltpu.roll(u, (-off) % R, 0), 0.0)


def _conv_fwd(proj3, conv_w, conv_b, L):
    _, R, _ = proj3.shape
    cb0 = OFF_XBC // CONV_BLOCK

    def body(u_ref, w_ref, b_ref, o_ref):
        u = u_ref[0].astype(F32)
        w = w_ref[...]
        pre = b_ref[...] + sum(_conv_tap(u, k, L) * w[k:k + 1, :] for k in range(4))
        o_ref[0] = (pre * _sigmoid(pre)).astype(BF)

    return pl.pallas_call(
        body, name="conv_fwd", grid=(2, CONV_DIM // CONV_BLOCK),
        in_specs=[pl.BlockSpec((1, R, CONV_BLOCK), lambda e, j: (e, 0, cb0 + j)),
                  pl.BlockSpec((4, CONV_BLOCK), lambda e, j: (0, j)),
                  pl.BlockSpec((1, CONV_BLOCK), lambda e, j: (0, j))],
        out_specs=pl.BlockSpec((1, R, CONV_BLOCK), lambda e, j: (e, 0, j)),
        out_shape=jax.ShapeDtypeStruct((2, R, CONV_DIM), BF),
        compiler_params=_params(("parallel", "parallel")),
    )(proj3, conv_w, conv_b)


def _conv_bwd(proj3, addends, col0, ncols, in_maps, conv_w, conv_b, L, name):
    _, R, _ = proj3.shape
    cb0 = (OFF_XBC + col0) // CONV_BLOCK
    wb0 = col0 // CONV_BLOCK
    na = len(addends)

    def body(*refs):
        u_ref, w_ref, b_ref = refs[0], refs[1], refs[2]
        a_refs = refs[3:3 + na]
        o_ref, acc_ref = refs[3 + na], refs[4 + na]
        u = u_ref[0].astype(F32)
        w = w_ref[...]
        taps = [_conv_tap(u, k, L) for k in range(4)]
        pre = b_ref[...] + sum(taps[k] * w[k:k + 1, :] for k in range(4))
        sg = _sigmoid(pre)
        dxbc = a_refs[0][0]
        for a in a_refs[1:]:
            dxbc = dxbc + a[0]
        dpre = dxbc * (sg * (1.0 + pre * (1.0 - sg)))
        acc_ref[...] = jnp.zeros_like(acc_ref)
        for k in range(4):
            acc_ref[0, k:k + 1, :] = jnp.sum(dpre * taps[k], axis=0, keepdims=True)
        acc_ref[0, 4:5, :] = jnp.sum(dpre, axis=0, keepdims=True)
        du = sum(_conv_tap(dpre, 4 - k, L) * w[k:k + 1, :] for k in range(4))
        o_ref[0] = du.astype(BF)

    in_specs = [pl.BlockSpec((1, R, CONV_BLOCK), lambda e, j: (e, 0, cb0 + j)),
                pl.BlockSpec((4, CONV_BLOCK), lambda e, j: (0, wb0 + j)),
                pl.BlockSpec((1, CONV_BLOCK), lambda e, j: (0, wb0 + j))]
    for m in in_maps:
        in_specs.append(pl.BlockSpec((1, R, CONV_BLOCK), functools.partial(lambda e, j, m: (e, 0, m(j)), m=m)))
    return pl.pallas_call(
        body, name=name, grid=(2, ncols // CONV_BLOCK),
        in_specs=in_specs,
        out_specs=(pl.BlockSpec((1, R, CONV_BLOCK), lambda e, j: (e, 0, j)),
                   pl.BlockSpec((1, 8, CONV_BLOCK), lambda e, j: (e, 0, j))),
        out_shape=(jax.ShapeDtypeStruct((2, R, ncols), BF), jax.ShapeDtypeStruct((2, 8, ncols), F32)),
        compiler_params=_params(("parallel", "parallel")),
    )(proj3, conv_w, conv_b, *addends)


def _softplus(x):
    e = jnp.exp(-jnp.abs(x))
    u = 1.0 + e
    return jnp.maximum(x, 0.0) + jnp.where(u == 1.0, e, e * jnp.log(u) / (u - 1.0))


def _dt_fwd(dt_raw, bias128):
    _, R, _ = dt_raw.shape

    def body(x_ref, b_ref, o_ref):
        o_ref[0] = _softplus(x_ref[0] + b_ref[...])

    return pl.pallas_call(
        body, name="dt_fwd", grid=(2,),
        in_specs=[pl.BlockSpec((1, R, 128), lambda e: (e, 0, 0)), pl.BlockSpec((1, 128), lambda e: (0, 0))],
        out_specs=pl.BlockSpec((1, R, 128), lambda e: (e, 0, 0)),
        out_shape=jax.ShapeDtypeStruct(dt_raw.shape, F32),
        compiler_params=_params(("parallel",)),
    )(dt_raw, bias128)


def _dt_bwd(dt_raw, bias128, ddt_f, ddt_b):
    _, R, _ = dt_raw.shape

    def body(x_ref, b_ref, f_ref, g_ref, o_ref, acc_ref):
        d = (f_ref[0] + g_ref[0]) * _sigmoid(x_ref[0] + b_ref[...])
        o_ref[0] = d.astype(BF)
        acc_ref[...] = jnp.zeros_like(acc_ref)
        acc_ref[0, 0:1, :] = jnp.sum(d, axis=0, keepdims=True)

    blk = pl.BlockSpec((1, R, 128), lambda e: (e, 0, 0))
    return pl.pallas_call(
        body, name="dt_bwd", grid=(2,),
        in_specs=[blk, pl.BlockSpec((1, 128), lambda e: (0, 0)), blk, blk],
        out_specs=(blk, pl.BlockSpec((1, 8, 128), lambda e: (e, 0, 0))),
        out_shape=(jax.ShapeDtypeStruct(dt_raw.shape, BF), jax.ShapeDtypeStruct((2, 8, 128), F32)),
        compiler_params=_params(("parallel",)),
    )(dt_raw, bias128, ddt_f, ddt_b)


def _tri(d):
    i = lax.broadcasted_iota(jnp.int32, (Q, Q), 0)
    j = lax.broadcasted_iota(jnp.int32, (Q, Q), 1)
    return (i >= j) if d == 0 else (i <= j)


def _expand_mat(d):
    r = lax.broadcasted_iota(jnp.int32, (128, GWID), 0)
    c = lax.broadcasted_iota(jnp.int32, (128, GWID), 1)
    return (r == d * HPG + jnp.right_shift(c, 6)).astype(BF)


def _reduce_mat(d):
    r = lax.broadcasted_iota(jnp.int32, (GWID, 128), 0)
    c = lax.broadcasted_iota(jnp.int32, (GWID, 128), 1)
    return (c == d * HPG + jnp.right_shift(r, 6)).astype(BF)


def _ssd_chunk(d, dt, A, xs, B, C):
    mask = _tri(d)
    T = mask.astype(BF)
    Tt = _tri(1 - d).astype(BF)
    a = dt * A
    acs = _dot_sr(T, a)
    E = _expand_mat(d)
    dt_e = _dot_sl(dt, E, 2)
    acs_e = _dot_sl(acs, E, 2)
    alast_e = acs_e[Q - 1:Q, :] if d == 0 else acs_e[0:1, :]
    return dict(mask=mask, T=T, Tt=Tt, acs=acs, acsT=acs.T, dt_e=dt_e, acs_e=acs_e, lam=jnp.exp(acs_e),
                w=jnp.exp(alast_e - acs_e), decay=jnp.exp(alast_e), xt=xs * dt_e, CB=_dot_nt(C, B))


def _head_decay(q, d, hh):
    col = q["acs"][:, d * HPG + hh:d * HPG + hh + 1]
    row = q["acsT"][d * HPG + hh:d * HPG + hh + 1, :]
    return jnp.exp(jnp.where(q["mask"], col - row, -jnp.inf))


def _chunk_maps(NX, NS):
    cf = lambda s: lax.rem(s + NX, NS)
    cb = lambda s: NS - 1 - s
    return cf, cb


def _ssd_fwd(xbc, dt_loc, a_loc, L):
    _, R, _ = xbc.shape
    NX, NS = L // Q, R // Q
    cf, cb = _chunk_maps(NX, NS)

    def body(xs_f, b_f, c_f, dt_f, xs_b, b_b, c_b, dt_b, a_ref, y_f, hs_f, y_b, hs_b, hT):
        @pl.when(pl.program_id(2) == 0)
        def _():
            hT[...] = jnp.zeros_like(hT)

        A = a_ref[0, 0:1, :]
        lane = lax.broadcasted_iota(jnp.int32, (Q, 128), 1)
        for d, (xs_ref, b_ref, c_ref, dt_ref, y_ref, hs_ref) in enumerate(
                ((xs_f, b_f, c_f, dt_f, y_f, hs_f), (xs_b, b_b, c_b, dt_b, y_b, hs_b))):
            xs, B, C = xs_ref[0].astype(F32), b_ref[0], c_ref[0]
            q = _ssd_chunk(d, dt_ref[0, 0], A, xs, B, C)
            h = hT[d]
            hb = h.astype(BF)
            hs_ref[0, 0] = hb
            parts = []
            for pr in range(HPG // 2):
                xp = q["xt"][:, pr * 128:(pr + 1) * 128].astype(BF)
                r0 = _dot(q["CB"] * _head_decay(q, d, 2 * pr), xp)
                r1 = _dot(q["CB"] * _head_decay(q, d, 2 * pr + 1), xp)
                parts.append(jnp.where(lane < HEAD, r0, r1))
            y_ref[0] = jnp.concatenate(parts, axis=1) + _dot(C, hb) * q["lam"]
            hT[d] = q["decay"] * h + _dot_tn(B, q["xt"] * q["w"])

    def spec(shape, imap):
        return pl.BlockSpec(shape, imap)

    def ins(c):
        return [spec((1, Q, GWID), lambda e, g, s: (e, c(s), g)),
                spec((1, Q, NST), lambda e, g, s: (e, c(s), DIN // NST + g)),
                spec((1, Q, NST), lambda e, g, s: (e, c(s), DIN // NST + NG + g)),
                spec((1, 1, Q, 128), lambda e, g, s: (e, g, c(s), 0))]

    def outs(c):
        return [spec((1, Q, GWID), lambda e, g, s: (e, c(s), g)),
                spec((1, 1, NST, GWID), lambda e, g, s: (e, c(s), 0, g))]

    yshape = jax.ShapeDtypeStruct((2, R, DIN), F32)
    hshape = jax.ShapeDtypeStruct((2, NS, NST, DIN), BF)
    return pl.pallas_call(
        body, name="ssd_fwd", grid=(2, NG, NS),
        in_specs=ins(cf) + ins(cb) + [spec((1, 8, 128), lambda e, g, s: (g, 0, 0))],
        out_specs=tuple(outs(cf) + outs(cb)),
        out_shape=(yshape, hshape, yshape, hshape),
        scratch_shapes=[pltpu.VMEM((2, NST, GWID), F32)],
        compiler_params=_params(("parallel", "parallel", "arbitrary")),
    )(xbc, xbc, xbc, dt_loc, xbc, xbc, xbc, dt_loc, a_loc)


def _ssd_bwd(xbc, dt_loc, a_loc, hs_f, hs_b, y_f, y_b, dy, L):
    _, R, _ = xbc.shape
    NX, NS = L // Q, R // Q
    cf0, cb0 = _chunk_maps(NX, NS)
    cf = lambda sp: cf0(NS - 1 - sp)
    cb = lambda sp: cb0(NS - 1 - sp)

    def body(xs_f, b_f, c_f, dt_f, hs_f_, dy_f, y_f_, xs_b, b_b, c_b, dt_b, hs_b_, dy_b, y_b_, a_ref,
             dxs_f, dbc_f, ddt_f, dxs_b, dbc_b, ddt_b, da_ref, dhT):
        @pl.when(pl.program_id(2) == 0)
        def _():
            dhT[...] = jnp.zeros_like(dhT)
            da_ref[...] = jnp.zeros_like(da_ref)

        A = a_ref[0, 0:1, :]
        lane = lax.broadcasted_iota(jnp.int32, (Q, 128), 1)
        row = lax.broadcasted_iota(jnp.int32, (Q, 128), 0)
        for d, (xs_ref, b_ref, c_ref, dt_ref, hs_ref, dy_ref, y_ref, dxs_ref, dbc_ref, ddt_ref) in enumerate(
                ((xs_f, b_f, c_f, dt_f, hs_f_, dy_f, y_f_, dxs_f, dbc_f, ddt_f),
                 (xs_b, b_b, c_b, dt_b, hs_b_, dy_b, y_b_, dxs_b, dbc_b, ddt_b))):
            xs, B, C, dt = xs_ref[0].astype(F32), b_ref[0], c_ref[0], dt_ref[0, 0]
            q = _ssd_chunk(d, dt, A, xs, B, C)
            xt, lam, w, decay = q["xt"], q["lam"], q["w"], q["decay"]
            H = hs_ref[0, 0]
            dyv = dy_ref[0]
            dh = dhT[d]
            dZ = dyv * lam
            dC = _dot_nt(dZ, H)
            dH = _dot_tn(C, dZ)
            U = _dot(B, dh)
            xw = xt * w
            dxt = U * w
            dalast_e = (jnp.sum(U * xw, axis=0, keepdims=True)
                        + decay * jnp.sum(dh * H.astype(F32), axis=0, keepdims=True))
            dB = _dot_nt(xw, dh)
            dCB = jnp.zeros((Q, Q), F32)
            dxt_parts = []
            for pr in range(HPG // 2):
                xp = xt[:, pr * 128:(pr + 1) * 128]
                dyp = dyv[:, pr * 128:(pr + 1) * 128]
                dxp = jnp.zeros((Q, 128), F32)
                for h2 in range(2):
                    Lh = _head_decay(q, d, 2 * pr + h2)
                    dym = jnp.where((lane < HEAD) if h2 == 0 else (lane >= HEAD), dyp, 0.0)
                    dxp = dxp + _dot_tn(q["CB"] * Lh, dym)
                    dCB = dCB + _dot_nt(dym, xp) * Lh
                dxt_parts.append(dxp)
            dxt_diag = jnp.concatenate(dxt_parts, axis=1)
            dC = dC + _dot(dCB, B)
            dB = dB + _dot_tn(dCB, C)
            Rm = _reduce_mat(d)
            rnd = lambda t: t.astype(BF).astype(F32)
            dacs = _dot_sl(rnd(dyv) * y_ref[0] - rnd(xt) * dxt_diag - U * xw, Rm, 2)
            dxt = dxt + dxt_diag
            dal = _dot_sl(jnp.broadcast_to(dalast_e, (8, GWID)), Rm, 2)[0:1, :]
            dacs = dacs + jnp.where(row == (Q - 1 if d == 0 else 0), dal, 0.0)
            da = _dot_sr(q["Tt"], dacs, 2)
            ddt_ref[0, 0] = da * A + _dot_sl(dxt * xs, Rm, 2)
            da_ref[0, 0, 0:1, :] += jnp.sum(da * dt, axis=0, keepdims=True)
            dxs_ref[0] = dxt * q["dt_e"]
            dbc_ref[0] = jnp.concatenate([dB, dC], axis=1)
            dhT[d] = decay * dh + dH

    def spec(shape, imap):
        return pl.BlockSpec(shape, imap)

    def ins(c):
        return [spec((1, Q, GWID), lambda e, g, s: (e, c(s), g)),
                spec((1, Q, NST), lambda e, g, s: (e, c(s), DIN // NST + g)),
                spec((1, Q, NST), lambda e, g, s: (e, c(s), DIN // NST + NG + g)),
                spec((1, 1, Q, 128), lambda e, g, s: (e, g, c(s), 0)),
                spec((1, 1, NST, GWID), lambda e, g, s: (e, c(s), 0, g)),
                spec((1, Q, GWID), lambda e, g, s: (e, c(s), g)),
                spec((1, Q, GWID), lambda e, g, s: (e, c(s), g))]

    def outs(c):
        return [spec((1, Q, GWID), lambda e, g, s: (e, c(s), g)),
                spec((1, Q, 2 * NST), lambda e, g, s: (e, c(s), g)),
                spec((1, 1, Q, 128), lambda e, g, s: (e, g, c(s), 0))]

    s_xs = jax.ShapeDtypeStruct((2, R, DIN), F32)
    s_bc = jax.ShapeDtypeStruct((2, R, 2 * NG * NST), F32)
    s_dt = jax.ShapeDtypeStruct((2, NG, R, 128), F32)
    return pl.pallas_call(
        body, name="ssd_bwd", grid=(2, NG, NS),
        in_specs=ins(cf) + ins(cb) + [spec((1, 8, 128), lambda e, g, s: (g, 0, 0))],
        out_specs=tuple(outs(cf) + outs(cb) + [spec((1, 1, 8, 128), lambda e, g, s: (e, g, 0, 0))]),
        out_shape=(s_xs, s_bc, s_dt, s_xs, s_bc, s_dt, jax.ShapeDtypeStruct((2, NG, 8, 128), F32)),
        scratch_shapes=[pltpu.VMEM((2, NST, GWID), F32)],
        compiler_params=_params(("parallel", "parallel", "arbitrary")),
    )(xbc, xbc, xbc, dt_loc, hs_f, dy, y_f, xbc, xbc, xbc, dt_loc, hs_b, dy, y_b, a_loc)


def _ssd_post_fwd(y_f, y_b, xbc, proj3, dskip_e, ssd_norm, L):
    def body(yf_ref, yb_ref, xs_ref, z_ref, ds_ref, w_ref, o_ref):
        y2 = yf_ref[0] + yb_ref[0] + ds_ref[...] * xs_ref[0].astype(F32)
        z = z_ref[0].astype(F32)
        u = y2 * (z * _sigmoid(z))
        parts = []
        for g in range(NG):
            ug = u[:, g * GWID:(g + 1) * GWID]
            parts.append(ug * lax.rsqrt(jnp.mean(ug * ug, axis=-1, keepdims=True) + EPS))
        o_ref[0] = (jnp.concatenate(parts, axis=1) * w_ref[...]).astype(BF)

    blk = lambda c: pl.BlockSpec((1, ROW_TILE, DIN), lambda e, t: (e, t, c))
    vec = pl.BlockSpec((1, DIN), lambda e, t: (0, 0))
    return pl.pallas_call(
        body, name="ssd_post_fwd", grid=(2, L // ROW_TILE),
        in_specs=[blk(0), blk(0), blk(0), blk(1), vec, vec],
        out_specs=blk(0),
        out_shape=jax.ShapeDtypeStruct((2, L, DIN), BF),
        compiler_params=_params(("parallel", "parallel")),
    )(y_f, y_b, xbc, proj3, dskip_e, ssd_norm)


def _ssd_post_bwd(d_yn, y_f, y_b, xbc, proj3, dskip_e, ssd_norm, L):
    _, R, _ = y_f.shape
    nx = L // ROW_TILE

    def body(dyn_ref, yf_ref, yb_ref, xs_ref, z_ref, ds_ref, w_ref, dy_ref, dsk_ref, dz_ref, acc_ref):
        t = pl.program_id(1)

        @pl.when(t == 0)
        def _():
            acc_ref[...] = jnp.zeros_like(acc_ref)

        @pl.when(t >= nx)
        def _():
            dy_ref[...] = jnp.zeros_like(dy_ref)
            dsk_ref[...] = jnp.zeros_like(dsk_ref)
            dz_ref[...] = jnp.zeros_like(dz_ref)

        @pl.when(t < nx)
        def _():
            xs = xs_ref[0].astype(F32)
            y2 = yf_ref[0] + yb_ref[0] + ds_ref[...] * xs
            z = z_ref[0].astype(F32)
            sg = _sigmoid(z)
            sz = z * sg
            u = y2 * sz
            dyn = dyn_ref[0]
            dun = dyn * w_ref[...]
            uh_parts, du_parts = [], []
            for g in range(NG):
                sl = slice(g * GWID, (g + 1) * GWID)
                ug = u[:, sl]
                rg = lax.rsqrt(jnp.mean(ug * ug, axis=-1, keepdims=True) + EPS)
                uh = ug * rg
                dg = dun[:, sl]
                du_parts.append(rg * (dg - uh * jnp.mean(dg * uh, axis=-1, keepdims=True)))
                uh_parts.append(uh)
            du = jnp.concatenate(du_parts, axis=1)
            uh = jnp.concatenate(uh_parts, axis=1)
            dy2 = du * sz
            dy_ref[0] = dy2
            dsk_ref[0] = dy2 * ds_ref[...]
            dz_ref[0] = (du * y2 * (sg * (1.0 + z * (1.0 - sg)))).astype(BF)
            acc_ref[0, 0:1, :] += jnp.sum(dyn * uh, axis=0, keepdims=True)
            acc_ref[0, 1:2, :] += jnp.sum(dy2 * xs, axis=0, keepdims=True)

    xmap = lambda c: (lambda e, t: (e, jnp.minimum(t, nx - 1), c))
    blk = lambda c: pl.BlockSpec((1, ROW_TILE, DIN), xmap(c))
    oblk = pl.BlockSpec((1, ROW_TILE, DIN), lambda e, t: (e, t, 0))
    vec = pl.BlockSpec((1, DIN), lambda e, t: (0, 0))
    return pl.pallas_call(
        body, name="ssd_post_bwd", grid=(2, R // ROW_TILE),
        in_specs=[blk(0), blk(0), blk(0), blk(0), blk(1), vec, vec],
        out_specs=(oblk, oblk, oblk, pl.BlockSpec((1, 8, DIN), lambda e, t: (e, 0, 0))),
        out_shape=(jax.ShapeDtypeStruct((2, R, DIN), F32), jax.ShapeDtypeStruct((2, R, DIN), F32),
                   jax.ShapeDtypeStruct((2, R, DIN), BF),
                   jax.ShapeDtypeStruct((2, 8, DIN), F32)),
        compiler_params=_params(("parallel", "arbitrary")),
    )(d_yn, y_f, y_b, xbc, proj3, dskip_e, ssd_norm)


def _merge_fwd(proj3, P, S, b_merge, L):
    def body(gp_ref, p_ref, s_ref, b_ref, o_ref):
        gt = _sigmoid(gp_ref[0].astype(F32) + b_ref[...])
        o_ref[0] = (gt[:, :D] * p_ref[0] + gt[:, D:] * s_ref[0]).astype(BF)

    blk = pl.BlockSpec((1, ROW_TILE, D), lambda e, t: (e, t, 0))
    return pl.pallas_call(
        body, name="merge_fwd", grid=(2, L // ROW_TILE),
        in_specs=[pl.BlockSpec((1, ROW_TILE, 2 * D), lambda e, t: (e, t, OFF_GATE // (2 * D))), blk, blk,
                  pl.BlockSpec((1, 2 * D), lambda e, t: (0, 0))],
        out_specs=blk, out_shape=jax.ShapeDtypeStruct((2, L, D), BF),
        compiler_params=_params(("parallel", "parallel")),
    )(proj3, P, S, b_merge)


def _merge_bwd(d_merged, proj3, P, S, b_merge, L):
    _, R, _ = proj3.shape
    nx = L // ROW_TILE

    def body(dm_ref, gp_ref, p_ref, s_ref, b_ref, dp_ref, ds_ref, dg_ref, acc_ref):
        t = pl.program_id(1)

        @pl.when(t == 0)
        def _():
            acc_ref[...] = jnp.zeros_like(acc_ref)

        @pl.when(t >= nx)
        def _():
            dg_ref[...] = jnp.zeros_like(dg_ref)

        @pl.when(t < nx)
        def _():
            gt = _sigmoid(gp_ref[0].astype(F32) + b_ref[...])
            dm = dm_ref[0]
            g1, g2 = gt[:, :D], gt[:, D:]
            dp_ref[0] = (dm * g1).astype(BF)
            ds_ref[0] = (dm * g2).astype(BF)
            dgp = jnp.concatenate([dm * p_ref[0] * g1 * (1.0 - g1), dm * s_ref[0] * g2 * (1.0 - g2)], axis=1)
            dg_ref[0] = dgp.astype(BF)
            acc_ref[0, 0:1, :] += jnp.sum(dgp, axis=0, keepdims=True)

    xmap = lambda e, t: (e, jnp.minimum(t, nx - 1), 0)
    blk = pl.BlockSpec((1, ROW_TILE, D), xmap)
    return pl.pallas_call(
        body, name="merge_bwd", grid=(2, R // ROW_TILE),
        in_specs=[blk, pl.BlockSpec((1, ROW_TILE, 2 * D), lambda e, t: (e, jnp.minimum(t, nx - 1), OFF_GATE // (2 * D))),
                  blk, blk, pl.BlockSpec((1, 2 * D), lambda e, t: (0, 0))],
        out_specs=(blk, blk, pl.BlockSpec((1, ROW_TILE, 2 * D), lambda e, t: (e, t, 0)),
                   pl.BlockSpec((1, 8, 2 * D), lambda e, t: (e, 0, 0))),
        out_shape=(jax.ShapeDtypeStruct((2, L, D), BF), jax.ShapeDtypeStruct((2, L, D), BF),
                   jax.ShapeDtypeStruct((2, R, 2 * D), BF), jax.ShapeDtypeStruct((2, 8, 2 * D), F32)),
        compiler_params=_params(("parallel", "arbitrary")),
    )(d_merged, proj3, P, S, b_merge)


def _final(out3, x, tgt, gtab, norm_post, L):
    def body(o_ref, x_ref, t_ref, g_ref, n_ref, dxo_ref, do_ref, acc_ref):
        @pl.when(pl.program_id(1) == 0)
        def _():
            acc_ref[...] = jnp.zeros_like(acc_ref)

        o = o_ref[0]
        gate = g_ref[0, 0:1, :]
        npost = n_ref[...]
        r2 = lax.rsqrt(jnp.mean(o * o, axis=-1, keepdims=True) + EPS)
        nh = o * r2
        on = nh * npost
        err = x_ref[0] + gate * on - t_ref[0]
        dxo = err * (1.0 / D)
        dxo_ref[0] = dxo
        dnh = dxo * gate * npost
        do_ref[0] = (r2 * (dnh - nh * jnp.mean(dnh * nh, axis=-1, keepdims=True))).astype(BF)
        acc_ref[0, 0:1, :] += jnp.sum(dxo * on, axis=0, keepdims=True)
        acc_ref[0, 1:2, :] += jnp.sum(dxo * gate * nh, axis=0, keepdims=True)
        acc_ref[0, 2:3, :] += jnp.sum(err * err, axis=0, keepdims=True)

    blk = pl.BlockSpec((1, ROW_TILE, D), lambda e, t: (e, t, 0))
    return pl.pallas_call(
        body, name="final", grid=(2, L // ROW_TILE),
        in_specs=[blk, blk, blk, pl.BlockSpec((1, 8, D), lambda e, t: (e, 0, 0)),
                  pl.BlockSpec((1, D), lambda e, t: (0, 0))],
        out_specs=(blk, blk, pl.BlockSpec((1, 8, D), lambda e, t: (e, 0, 0))),
        out_shape=(jax.ShapeDtypeStruct((2, L, D), F32), jax.ShapeDtypeStruct((2, L, D), BF),
                   jax.ShapeDtypeStruct((2, 8, D), F32)),
        compiler_params=_params(("parallel", "arbitrary")),
    )(out3, x, tgt, gtab, norm_post)


def _local_step(x, c, ctx, loss_target, W):
    nb, L, _ = x.shape
    LC = ctx.shape[1]
    R = L + LC
    assert nb == 2 and L % ROW_TILE == 0 and LC % Q == 0 and L % POOL_TILE == 0
    w_inT = W["w_in"]
    w_dtT = jnp.pad(w_inT[OFF_DT:], ((0, 64), (0, 0)))
    tables = _pool_tables(L)
    tr, tl = (2 * R) // 8, (2 * L) // 8

    c16 = jnp.zeros((16, D), F32).at[0:2].set(c).at[2].set(W["c_ctx"])
    mod16 = _adaln_fwd(c16, W["w_ada"], W["b_ada"])
    shift, scale, gate = mod16[:, :D], mod16[:, D:2 * D], mod16[:, 2 * D:]
    npre = W["norm_pre"]
    tab = jnp.zeros((2, 2, 8, D), F32)
    for e in range(2):
        tab = tab.at[e, 0, 0].set(npre[0] * (1.0 + scale[e])).at[e, 0, 1].set(shift[e])
        tab = tab.at[e, 1, 0].set(npre[0] * (1.0 + scale[2])).at[e, 1, 1].set(shift[2])
    gtab = jnp.zeros((2, 8, D), F32).at[:, 0].set(gate[0:2])

    hx = _norm_mod_fwd(x, ctx, tab)
    hx2 = hx.reshape(2 * R, D)
    proj3 = _matmul(hx2, w_inT, BF, "proj_main", tm=tr, tn=1024, bt=True, n=OFF_DT).reshape(2, R, OFF_DT)
    dt_raw = _matmul(hx2, w_dtT, F32, "proj_dt", tm=tr, bt=True).reshape(2, R, 128)
    ypool = _pool_fwd(proj3, W["pool_w"], W["pool_scale"], tables, L)
    xbc = _conv_fwd(proj3, W["conv_w"], W["conv_b"], L)
    bias128 = jnp.pad(W["dt_bias"].reshape(1, 64), ((0, 0), (0, 64)))
    dt = _dt_fwd(dt_raw, bias128)
    to_loc = lambda t: jnp.pad(t[:, :, :64].reshape(2, R, 2, NG, HPG).transpose(0, 3, 1, 2, 4).reshape(2, NG, R, 16),
                               ((0, 0), (0, 0), (0, 0), (0, 112)))
    from_loc = lambda t: jnp.pad(t[..., :16].reshape(2, NG, R, 2, HPG).transpose(0, 2, 3, 1, 4).reshape(2, R, 64),
                                 ((0, 0), (0, 0), (0, 64)))
    dt_loc = to_loc(dt)
    A = -jnp.exp(W["a_log"].reshape(2, NG, HPG))
    a_loc = jnp.zeros((NG, 8, 128), F32).at[:, 0, :16].set(A.transpose(1, 0, 2).reshape(NG, 16))
    y_f, hs_f, y_b, hs_b = _ssd_fwd(xbc, dt_loc, a_loc, L)
    dskip_e = jnp.repeat(W["d_skip"].reshape(1, 32), HEAD, axis=1)
    yn = _ssd_post_fwd(y_f, y_b, xbc, proj3, dskip_e, W["ssd_norm"], L)
    ypool2, yn2 = ypool.reshape(2 * L, D), yn.reshape(2 * L, DIN)
    P = _matmul(ypool2, W["w_proj_pool"], F32, "proj_pool", tm=tl, tn=1024).reshape(2, L, D)
    S = _matmul(yn2, W["w_proj_ssd"], F32, "proj_ssd", tm=tl, tn=1024).reshape(2, L, D)
    merged = _merge_fwd(proj3, P, S, W["b_merge"], L)
    merged2 = merged.reshape(2 * L, D)
    out3 = _matmul(merged2, W["w_out"], F32, "proj_out", tm=tl, tn=1024).reshape(2, L, D)
    dxo, dout, acc_f = _final(out3, x, loss_target, gtab, W["norm_post"], L)

    dout2 = dout.reshape(2 * L, D)
    g = {}
    g["w_out"] = _matmul_tn(merged2, dout2, "dw_out", ta=1024, tn=1024, tr=tl)
    d_merged = _matmul(dout2, W["w_out"], F32, "d_merged", tm=tl, tn=1024, bt=True).reshape(2, L, D)
    dP, dS, dgp, acc_m = _merge_bwd(d_merged, proj3, P, S, W["b_merge"], L)
    dP2, dS2 = dP.reshape(2 * L, D), dS.reshape(2 * L, D)
    g["w_proj_pool"] = _matmul_tn(ypool2, dP2, "dw_proj_pool", ta=1024, tn=1024, tr=tl)
    g["w_proj_ssd"] = _matmul_tn(yn2, dS2, "dw_proj_ssd", ta=1024, tn=1024, tr=tl)
    d_ypool = _matmul(dP2, W["w_proj_pool"], F32, "d_ypool", tm=tl, tn=1024, bt=True).reshape(2, L, D)
    d_yn = _matmul(dS2, W["w_proj_ssd"], F32, "d_yn", tm=tl, tn=1024, bt=True).reshape(2, L, DIN)
    dv, dzp, g["pool_w"], acc_p = _pool_bwd(proj3, d_ypool, W["pool_w"], jnp.swapaxes(W["pool_w"], 1, 2),
                                            W["pool_scale"], tables, L)
    dy2, dxs_skip, dzs, acc_s = _ssd_post_bwd(d_yn, y_f, y_b, xbc, proj3, dskip_e, W["ssd_norm"], L)
    dxs_f, dbc_f, ddt_f, dxs_b, dbc_b, ddt_b, acc_a = _ssd_bwd(xbc, dt_loc, a_loc, hs_f, hs_b, y_f, y_b, dy2, L)
    ident = lambda j: j
    dxr_xs, acc_cx = _conv_bwd(proj3, [dxs_f, dxs_b, dxs_skip], 0, DIN, [ident, ident, ident],
                               W["conv_w"], W["conv_b"], L, "conv_bwd_xs")
    bcmap = lambda j: 2 * lax.rem(j, NG) + j // NG
    dxr_bc, acc_cb = _conv_bwd(proj3, [dbc_f, dbc_b], DIN, 2 * NG * NST, [bcmap, bcmap],
                               W["conv_w"], W["conv_b"], L, "conv_bwd_bc")
    ddtr, acc_d = _dt_bwd(dt_raw, bias128, from_loc(ddt_f), from_loc(ddt_b))
    pieces = [dv, dzp, dzs, dgp, dxr_xs, dxr_bc]
    dw_rows = [_matmul_tn(p.reshape(2 * R, p.shape[2]), hx2, "dw_in_%d" % i, ta=1024, tn=1024, tr=tr)
               for i, p in enumerate(pieces)]
    dw_rows.append(_matmul_tn(ddtr.reshape(2 * R, 128), hx2, "dw_in_dt", ta=128, tn=1024, tr=tr)[:64])
    g["w_in"] = jnp.concatenate(dw_rows, axis=0)
    dh = _dhx(pieces, ddtr, w_inT, w_dtT)
    grad_x, acc_n = _norm_mod_bwd(dh, x, ctx, tab, dxo)
    g["w_ada"], db_rows, sm_rows = _adaln_bwd(acc_n, acc_f, mod16, c16, npre, W["w_ada"])

    g["b_ada"] = db_rows[0:1]
    g["norm_pre"] = sm_rows[0:1]
    g["c_ctx"] = sm_rows[1]
    g["norm_post"] = acc_f[0, 1:2] + acc_f[1, 1:2]
    g["b_merge"] = acc_m[0, 0:1] + acc_m[1, 0:1]
    g["pool_scale"] = acc_p[:, 0, :].reshape(1, D)
    acc_c = jnp.concatenate([acc_cx[0] + acc_cx[1], acc_cb[0] + acc_cb[1]], axis=1)
    g["conv_w"] = acc_c[0:4]
    g["conv_b"] = acc_c[4:5]
    g["dt_bias"] = (acc_d[0, 0, :64] + acc_d[1, 0, :64]).reshape(2, 32)
    dA = (acc_a[0, :, 0, :16] + acc_a[1, :, 0, :16]).reshape(NG, 2, HPG).transpose(1, 0, 2)
    g["a_log"] = (dA * A).reshape(2, 32)
    g["d_skip"] = (acc_s[0, 1] + acc_s[1, 1]).reshape(32, HEAD).sum(axis=1).reshape(1, 32)
    g["ssd_norm"] = acc_s[0, 0:1] + acc_s[1, 0:1]
    loss_lanes = acc_f[:, 2, :]
    return loss_lanes, grad_x, g


MESH = pl.DeviceIdType.MESH
ANY = pl.BlockSpec(memory_space=pl.ANY)


def _all_gather(shard):
    m_per, n = shard.shape

    def body(x_ref, out_ref, send_sems, recv_sems, local_sem):
        x, y, c = lax.axis_index("x"), lax.axis_index("y"), lax.axis_index("c")
        me, sibling = (x, y, c), (x, y, 1 - c)
        chips = [(1 - x, y), (x, 1 - y), (1 - x, 1 - y)]

        def rows(px, py, pc):
            return out_ref.at[pl.ds((4 * px + 2 * py + pc) * m_per, m_per), :]

        def copy(k, block, to, src=None):
            return pltpu.make_async_remote_copy(
                src_ref=rows(*block) if src is None else src, dst_ref=rows(*block),
                send_sem=send_sems.at[k], recv_sem=recv_sems.at[k], device_id=to, device_id_type=MESH)

        mine = pltpu.make_async_copy(x_ref, rows(*me), local_sem)
        mine.start()
        first = [copy(0, me, sibling, src=x_ref)]
        first += [copy(1 + j, me, (*chip, c), src=x_ref) for j, chip in enumerate(chips)]
        for cp in first:
            cp.start()
        passed = [copy(4 + j, (*chip, c), sibling) for j, chip in enumerate(chips)]
        for j, chip in enumerate(chips):
            copy(1 + j, (*chip, c), me).wait_recv()
            passed[j].start()
        copy(0, sibling, me).wait_recv()
        for j, chip in enumerate(chips):
            copy(4 + j, (*chip, 1 - c), me).wait_recv()
        for cp in first + passed:
            cp.wait_send()
        mine.wait()

    return pl.pallas_call(
        body, name="all_gather_weights",
        out_shape=jax.ShapeDtypeStruct((NDEV * m_per, n), shard.dtype),
        in_specs=[ANY], out_specs=ANY,
        scratch_shapes=[pltpu.SemaphoreType.DMA((7,)), pltpu.SemaphoreType.DMA((7,)), pltpu.SemaphoreType.DMA],
    )(shard)


def _all_to_all(big, small):
    def body(big_ref, small_ref, obig_ref, osmall_ref, send_sems, recv_sems, local_sems):
        x, y, c = lax.axis_index("x"), lax.axis_index("y"), lax.axis_index("c")
        me = 4 * x + 2 * y + c
        mine = [pltpu.make_async_copy(big_ref.at[me], obig_ref.at[me], local_sems.at[0]),
                pltpu.make_async_copy(small_ref, osmall_ref.at[me], local_sems.at[1])]
        for cp in mine:
            cp.start()

        def copies(k):
            px = 1 - x if k & 4 else x
            py = 1 - y if k & 2 else y
            pc = 1 - c if k & 1 else c
            peer = 4 * px + 2 * py + pc

            def rc(src, dst, sem):
                return pltpu.make_async_remote_copy(src_ref=src, dst_ref=dst, send_sem=send_sems.at[sem],
                                                    recv_sem=recv_sems.at[sem], device_id=(px, py, pc), device_id_type=MESH)

            sends = [rc(big_ref.at[peer], obig_ref.at[me], k - 1), rc(small_ref, osmall_ref.at[me], 6 + k)]
            recvs = [rc(big_ref.at[peer], obig_ref.at[peer], k - 1), rc(small_ref, osmall_ref.at[peer], 6 + k)]
            return sends, recvs

        allc = [copies(k) for k in range(1, NDEV)]
        for sends, _ in allc:
            for cp in sends:
                cp.start()
        for sends, _ in allc:
            for cp in sends:
                cp.wait_send()
        for _, recvs in allc:
            for cp in recvs:
                cp.wait_recv()
        for cp in mine:
            cp.wait()

    return pl.pallas_call(
        body, name="all_to_all_grads",
        out_shape=(jax.ShapeDtypeStruct(big.shape, big.dtype), jax.ShapeDtypeStruct((NDEV,) + small.shape, small.dtype)),
        in_specs=[ANY, ANY], out_specs=(ANY, ANY),
        scratch_shapes=[pltpu.SemaphoreType.DMA((14,)), pltpu.SemaphoreType.DMA((14,)), pltpu.SemaphoreType.DMA((2,))],
    )(big, small)


ADAM_TILE = 64
PACK_W = 1024


def _adamw(recv, w, m, v, name):
    rp = w.shape[0]
    tile = min(ADAM_TILE, rp)

    def body(r_ref, w_ref, m_ref, v_ref, g_ref, d_ref, nm_ref, nv_ref):
        g = r_ref[0].astype(F32)
        for i in range(1, NDEV):
            g = g + r_ref[i].astype(F32)
        m1 = ADAM_B1 * m_ref[...] + (1.0 - ADAM_B1) * g
        v1 = ADAM_B2 * v_ref[...] + (1.0 - ADAM_B2) * (g * g)
        m_hat = m1 / (1.0 - ADAM_B1 ** ADAM_STEP)
        v_hat = v1 / (1.0 - ADAM_B2 ** ADAM_STEP)
        g_ref[...] = g
        d_ref[...] = -ADAM_LR * (m_hat / (jnp.sqrt(v_hat) + ADAM_EPS) + ADAM_WD * w_ref[...])
        nm_ref[...] = m1
        nv_ref[...] = v1

    blk = pl.BlockSpec((tile, PACK_W), lambda i: (i, 0))
    shp = jax.ShapeDtypeStruct((rp, PACK_W), F32)
    return pl.pallas_call(
        body, name=name, grid=(rp // tile,),
        in_specs=[pl.BlockSpec((NDEV, tile, PACK_W), lambda i: (0, i, 0)), blk, blk, blk],
        out_specs=(blk, blk, blk, blk), out_shape=(shp, shp, shp, shp),
        compiler_params=_params(("parallel",)),
    )(recv, w, m, v)


BIG = {"w_ada": ((3 * D, D), 0), "pool_w": ((4, PGW, PGW), 1), "w_proj_pool": ((D, D), 0), "w_proj_ssd": ((DIN, D), 0),
       "w_out": ((D, D), 0), "w_in": ((IN_COLS, D), 0), "conv_w": ((4, CONV_DIM), 1)}
TRANSPOSED = ("w_ada", "w_in")
PACK_ROWS = {"w_ada": 384, "pool_w": 32, "w_proj_pool": 128, "w_proj_ssd": 256, "w_out": 128, "conv_w": 16, "w_in": 1168}
SMALL = {"c_ctx": (D,), "b_ada": (1, 3 * D), "norm_pre": (1, D), "norm_post": (1, D), "b_merge": (1, 2 * D),
         "pool_scale": (1, D), "conv_b": (1, CONV_DIM), "dt_bias": (2, 32), "a_log": (2, 32), "d_skip": (1, 32),
         "ssd_norm": (1, DIN)}
LOSS_SLOT = 128
BIG_ROWS = sum(PACK_ROWS.values())
assert BIG_ROWS % ADAM_TILE == 0
SMALL_ROWS = 16


def _shard_shape(name):
    shape, ax = BIG[name]
    return tuple(s // NDEV if i == ax else s for i, s in enumerate(shape))


def _as_rows(t, rows):
    pad = [(0, 0)] * (t.ndim - 1) + [(0, rows * PACK_W - t.shape[-1])]
    return jnp.pad(t, pad).reshape(t.shape[:-1] + (rows, PACK_W))


def _shard_rows(t, name):
    sh, r = _shard_shape(name), PACK_ROWS[name]
    lead = t.shape[:t.ndim - len(sh)]
    if len(sh) == 2 and sh[1] == PACK_W:
        return jnp.pad(t, [(0, 0)] * len(lead) + [(0, r - sh[0]), (0, 0)])
    if int(np.prod(sh)) == r * PACK_W:
        return t.reshape(lead + (r, PACK_W))
    return _as_rows(t.reshape(lead + (-1,)), r)


def _to_chunks(full, name):
    shape, ax = BIG[name]
    split = shape[:ax] + (NDEV, shape[ax] // NDEV) + shape[ax + 1:]
    return _shard_rows(jnp.moveaxis(full.reshape(split), ax, 0), name)


def _from_chunks(chunks, name):
    shape, ax = BIG[name]
    return jnp.moveaxis(chunks.reshape((NDEV,) + _shard_shape(name)), 0, ax).reshape(shape)


def _pack_state(t):
    big = jnp.concatenate([_shard_rows(t[n], n) for n in PACK_ROWS], axis=0)
    small = _as_rows(jnp.concatenate([t[n].reshape(-1) for n in SMALL] + [jnp.zeros((LOSS_SLOT,), F32)]), SMALL_ROWS)
    return big, small


def _pack_grads(g, loss_part):
    big = jnp.concatenate([_to_chunks(g[n], n).astype(BF) for n in PACK_ROWS], axis=1)
    small = [g[n].reshape(-1) for n in SMALL] + [jnp.zeros((LOSS_SLOT,), F32).at[0].set(loss_part)]
    return big, _as_rows(jnp.concatenate(small), SMALL_ROWS)


def _unpack_state(big, small):
    out, off = {}, 0
    for n, r in PACK_ROWS.items():
        sh = _shard_shape(n)
        k = int(np.prod(sh))
        if len(sh) == 2 and sh[1] == PACK_W:
            out[n] = big[off:off + sh[0]]
        else:
            out[n] = big[off:off + r].reshape(-1)[:k].reshape(sh)
        off += r
    flat, off = small.reshape(-1), 0
    for n, sh in SMALL.items():
        k = int(np.prod(sh))
        out[n] = flat[off:off + k].reshape(sh)
        off += k
    out["loss"] = flat[off]
    return out


def _pack_gather(w):
    conv = jnp.concatenate([p.reshape(-1) for p in _split(w["conv_w"], 3)])
    return jnp.concatenate([_as_rows(conv, PACK_ROWS[n]) if n == "conv_w" else _shard_rows(w[n], n).astype(BF)
                            for n in PACK_ROWS], axis=0)


def _unpack_gather(gathered):
    g = gathered.reshape(NDEV, BIG_ROWS, PACK_W)
    out, off = {}, 0
    for n, r in PACK_ROWS.items():
        sh = _shard_shape(n)
        if n == "conv_w":
            k = int(np.prod(sh))
            terms = g[:, off:off + r].reshape(NDEV, -1)[:, :3 * k].astype(F32).reshape(NDEV, 3, k)
            out[n] = _from_chunks(terms[:, 0] + terms[:, 1] + terms[:, 2], n)
        elif len(sh) == 2 and sh[1] == PACK_W:
            out[n] = _from_chunks(g[:, off:off + sh[0]], n)
        else:
            out[n] = _from_chunks(g[:, off:off + r], n)
        off += r
    return out


PARAMS = ["c_ctx", "w_ada", "b_ada", "norm_pre", "norm_post", "w_in", "b_merge", "pool_w", "pool_scale", "conv_w", "conv_b",
          "dt_bias", "a_log", "d_skip", "ssd_norm", "w_proj_pool", "w_proj_ssd", "w_out"]


def kernel(x, c, ctx, c_ctx, w_ada, b_ada, norm_pre, norm_post, w_in, b_merge, pool_w, pool_scale, conv_w, conv_b, dt_bias, a_log, d_skip, ssd_norm, w_proj_pool, w_proj_ssd, w_out, loss_target, m_c_ctx, m_w_ada, m_b_ada, m_norm_pre, m_norm_post, m_w_in, m_b_merge, m_pool_w, m_pool_scale, m_conv_w, m_conv_b, m_dt_bias, m_a_log, m_d_skip, m_ssd_norm, m_w_proj_pool, m_w_proj_ssd, m_w_out, v_c_ctx, v_w_ada, v_b_ada, v_norm_pre, v_norm_post, v_w_in, v_b_merge, v_pool_w, v_pool_scale, v_conv_w, v_conv_b, v_dt_bias, v_a_log, v_d_skip, v_ssd_norm, v_w_proj_pool, v_w_proj_ssd, v_w_out):
    given = dict(locals())
    shapes = {n: given[n].shape for n in PARAMS}

    def local(prefix):
        t = {n: (given[prefix + n] if n == "c_ctx" else given[prefix + n][0]) for n in PARAMS}
        for n in TRANSPOSED:
            t[n] = t[n].T
        return {n: t[n].reshape(_shard_shape(n) if n in BIG else SMALL[n]) for n in PARAMS}

    w, m, v = local(""), local("m_"), local("v_")

    W = _unpack_gather(_all_gather(_pack_gather(w)))
    for n in SMALL:
        W[n] = w[n]
    lanes, grad_x, g = _local_step(x, c, ctx, loss_target, W)
    recv_big, recv_small = _all_to_all(*_pack_grads(g, (0.5 / D) * jnp.sum(lanes)))
    (wb, ws), (mb, ms), (vb, vs) = _pack_state(w), _pack_state(m), _pack_state(v)
    res = [_unpack_state(b, s) for b, s in zip(_adamw(recv_big, wb, mb, vb, "adamw_big"),
                                               _adamw(recv_small, ws, ms, vs, "adamw_small"))]
    outs = [res[0]["loss"], grad_x]
    for r in res:
        for n in TRANSPOSED:
            r[n] = r[n].T
        outs += [r[n].reshape(shapes[n]) for n in PARAMS]
    return tuple(outs)
```

```python
import functools

import numpy as np
import jax
import jax.numpy as jnp
from jax import lax
from jax.experimental import pallas as pl
from jax.experimental.pallas import tpu as pltpu

F32, BF = jnp.float32, jnp.bfloat16

D = 1024
GRID_W = 64
EPS = 1e-6
POOL_WINDOWS = (2, 4, 8, 16)
PGW = 256
DIN = 2048
HEAD = 64
NST = 128
NG = 4
HPG = 8
GWID = HPG * HEAD
Q = 128
CONV_DIM = 3072
OFF_GATE, OFF_XBC, OFF_DT, IN_COLS = 4096, 6144, 9216, 9280
NDEV = 8
ADAM_LR, ADAM_B1, ADAM_B2, ADAM_EPS, ADAM_WD, ADAM_STEP = 0.001, 0.9, 0.999, 1e-08, 0.01, 10

V7X_VMEM_LIMIT = 56 * 2 ** 20
ROW_TILE = 256


def _params(sem=None):
    return pltpu.CompilerParams(dimension_semantics=sem, vmem_limit_bytes=V7X_VMEM_LIMIT)


def _dot(a, b):
    return jnp.dot(a.astype(BF), b.astype(BF), preferred_element_type=F32)


def _dot_nt(a, b):
    return lax.dot_general(a.astype(BF), b.astype(BF), (((1,), (1,)), ((), ())), preferred_element_type=F32)


def _dot_tn(a, b):
    return lax.dot_general(a.astype(BF), b.astype(BF), (((0,), (0,)), ((), ())), preferred_element_type=F32)


def _split(a, n):
    parts = []
    for _ in range(n):
        p = a.astype(BF)
        parts.append(p)
        a = a - p.astype(F32)
    return parts


def _dot_sl(a, b01, n=3):
    return sum(jnp.dot(p, b01, preferred_element_type=F32) for p in _split(a, n))


def _dot_sr(a01, b, n=3):
    return sum(jnp.dot(a01, p, preferred_element_type=F32) for p in _split(b, n))


def _dot_tn_sl(a, b01, n=2):
    return sum(lax.dot_general(p, b01, (((0,), (0,)), ((), ())), preferred_element_type=F32) for p in _split(a, n))


def _sigmoid(x):
    return 1.0 / (1.0 + jnp.exp(-x))


def _matmul(a, b, out_dtype, name, tm=512, tn=512, tk=1024, bt=False, n=None):
    M, K = a.shape
    N = n if n is not None else (b.shape[0] if bt else b.shape[1])
    tm, tn, tk = min(tm, M), min(tn, N), min(tk, K)
    assert M % tm == 0 and N % tn == 0 and K % tk == 0, (a.shape, b.shape)
    nk = K // tk

    def body(a_ref, b_ref, o_ref, acc):
        k = pl.program_id(2)
        p = _dot_nt(a_ref[...], b_ref[...]) if bt else _dot(a_ref[...], b_ref[...])

        @pl.when(k == 0)
        def _():
            acc[...] = p

        @pl.when(k > 0)
        def _():
            acc[...] += p

        @pl.when(k == nk - 1)
        def _():
            o_ref[...] = acc[...].astype(o_ref.dtype)

    return pl.pallas_call(
        body, name=name, grid=(M // tm, N // tn, nk),
        in_specs=[pl.BlockSpec((tm, tk), lambda i, j, k: (i, k)),
                  pl.BlockSpec((tn, tk), lambda i, j, k: (j, k)) if bt else pl.BlockSpec((tk, tn), lambda i, j, k: (k, j))],
        out_specs=pl.BlockSpec((tm, tn), lambda i, j, k: (i, j)),
        out_shape=jax.ShapeDtypeStruct((M, N), out_dtype),
        scratch_shapes=[pltpu.VMEM((tm, tn), F32)],
        compiler_params=_params(("parallel", "parallel", "arbitrary")),
    )(a, b)


def _matmul_tn(a, g, name, ta=512, tn=512, tr=512):
    M, Ka = a.shape
    N = g.shape[1]
    ta, tn, tr = min(ta, Ka), min(tn, N), min(tr, M)
    assert M % tr == 0 and N % tn == 0 and Ka % ta == 0, (a.shape, g.shape)
    nr = M // tr

    def body(a_ref, g_ref, o_ref):
        k = pl.program_id(2)
        p = _dot_tn(a_ref[...], g_ref[...])

        @pl.when(k == 0)
        def _():
            o_ref[...] = p

        @pl.when(k > 0)
        def _():
            o_ref[...] += p

    return pl.pallas_call(
        body, name=name, grid=(Ka // ta, N // tn, nr),
        in_specs=[pl.BlockSpec((tr, ta), lambda i, j, k: (k, i)), pl.BlockSpec((tr, tn), lambda i, j, k: (k, j))],
        out_specs=pl.BlockSpec((ta, tn), lambda i, j, k: (i, j)),
        out_shape=jax.ShapeDtypeStruct((Ka, N), F32),
        compiler_params=_params(("parallel", "parallel", "arbitrary")),
    )(a, g)


def _dhx(pieces, ddt, w_inT, w_dtT):
    _, R, _ = pieces[0].shape
    tm = R // 8
    kb = 1024
    starts, nblk = [], []
    for p in pieces:
        starts.append(sum(nblk))
        nblk.append(p.shape[2] // kb)
    nk = sum(nblk)
    assert nk * kb == OFF_DT and R % 128 == 0
    npc = len(pieces)

    def body(*refs):
        a_refs, dt_ref, w_ref, wdt_ref, o_ref, acc = refs[:npc], refs[npc], refs[npc + 1], refs[npc + 2], refs[npc + 3], refs[npc + 4]
        k = pl.program_id(2)

        @pl.when(k == 0)
        def _():
            acc[...] = _dot(dt_ref[0], wdt_ref[...])

        for p in range(npc):
            @pl.when((k >= starts[p]) & (k < starts[p] + nblk[p]))
            def _(p=p):
                acc[...] += _dot(a_refs[p][0], w_ref[...])

        @pl.when(k == nk - 1)
        def _():
            o_ref[0] = acc[...]

    in_specs = [pl.BlockSpec((1, tm, kb), functools.partial(
        lambda e, t, k, s, nb: (e, t, jnp.clip(k - s, 0, nb - 1)), s=starts[p], nb=nblk[p])) for p in range(npc)]
    in_specs += [pl.BlockSpec((1, tm, 128), lambda e, t, k: (e, t, 0)),
                 pl.BlockSpec((kb, D), lambda e, t, k: (k, 0)),
                 pl.BlockSpec((128, D), lambda e, t, k: (0, 0))]
    return pl.pallas_call(
        body, name="d_hx", grid=(2, R // tm, nk), in_specs=in_specs,
        out_specs=pl.BlockSpec((1, tm, D), lambda e, t, k: (e, t, 0)),
        out_shape=jax.ShapeDtypeStruct((2, R, D), F32),
        scratch_shapes=[pltpu.VMEM((tm, D), F32)],
        compiler_params=_params(("parallel", "parallel", "arbitrary")),
    )(*pieces, ddt, w_inT, w_dtT)


def _adaln_fwd(c16, w_adaT_bf, b_ada):
    def body(c_ref, w_ref, b_ref, o_ref):
        cc = c_ref[...]
        o_ref[...] = _dot_nt(cc * _sigmoid(cc), w_ref[...]) + b_ref[...]

    return pl.pallas_call(body, name="adaln_fwd", out_shape=jax.ShapeDtypeStruct((16, 3 * D), F32),
                          compiler_params=_params())(c16, w_adaT_bf, b_ada)


def _adaln_bwd(acc_n, acc_f, mod16, c16, norm_pre, w_adaT_bf):
    def body(an_ref, af_ref, mod_ref, c_ref, np_ref, wt_ref, dw_ref, db_ref, sm_ref, dmod):
        npre = np_ref[...]
        dmod[...] = jnp.zeros_like(dmod)
        dnp = jnp.zeros((1, D), F32)
        dshift_c = jnp.zeros((1, D), F32)
        dgpre_c = jnp.zeros((1, D), F32)
        scale_c = mod_ref[2:3, D:2 * D]
        for e in range(2):
            dg_x, ds_x = an_ref[e, 0, 0:1, :], an_ref[e, 0, 1:2, :]
            dg_c, ds_c = an_ref[e, 1, 0:1, :], an_ref[e, 1, 1:2, :]
            dmod[e:e + 1, 0:D] = ds_x
            dmod[e:e + 1, D:2 * D] = dg_x * npre
            dmod[e:e + 1, 2 * D:3 * D] = af_ref[e, 0:1, :]
            dnp = dnp + dg_x * (1.0 + mod_ref[e:e + 1, D:2 * D]) + dg_c * (1.0 + scale_c)
            dshift_c = dshift_c + ds_c
            dgpre_c = dgpre_c + dg_c
        dmod[2:3, 0:D] = dshift_c
        dmod[2:3, D:2 * D] = dgpre_c * npre
        dm = dmod[...]
        cc = c_ref[...]
        sg = _sigmoid(cc)
        dw_ref[...] = _dot_tn(dm, cc * sg)
        db_ref[...] = jnp.zeros_like(db_ref)
        db_ref[0:1, :] = jnp.sum(dm, axis=0, keepdims=True)
        dsilu = sg * (1.0 + cc * (1.0 - sg))
        dcs = _dot(dm, wt_ref[...]) * dsilu
        sm_ref[...] = jnp.zeros_like(sm_ref)
        sm_ref[0:1, :] = dnp
        sm_ref[1:2, :] = dcs[2:3, :]

    return pl.pallas_call(
        body, name="adaln_bwd",
        out_shape=(jax.ShapeDtypeStruct((3 * D, D), F32), jax.ShapeDtypeStruct((16, 3 * D), F32),
                   jax.ShapeDtypeStruct((8, D), F32)),
        scratch_shapes=[pltpu.VMEM((16, 3 * D), F32)],
        compiler_params=_params())(acc_n, acc_f, mod16, c16, norm_pre, w_adaT_bf)


def _row_specs(L):
    nx = L // ROW_TILE
    return (pl.BlockSpec((1, ROW_TILE, D), lambda e, t: (e, jnp.minimum(t, nx - 1), 0)),
            pl.BlockSpec((1, ROW_TILE, D), lambda e, t: (e, jnp.maximum(t - nx, 0), 0)))


def _norm_mod_fwd(x, ctx, tab):
    L = x.shape[1]
    R = L + ctx.shape[1]
    nx = L // ROW_TILE

    def body(x_ref, c_ref, t_ref, o_ref):
        x = jnp.where(pl.program_id(1) < nx, x_ref[0], c_ref[0])
        r = lax.rsqrt(jnp.mean(x * x, axis=-1, keepdims=True) + EPS)
        t = t_ref[0, 0]
        o_ref[0] = (x * r * t[0:1] + t[1:2]).astype(BF)

    return pl.pallas_call(
        body, name="norm_mod_fwd", grid=(2, R // ROW_TILE),
        in_specs=[*_row_specs(L), pl.BlockSpec((1, 1, 8, D), lambda e, t: (e, t // nx, 0, 0))],
        out_specs=pl.BlockSpec((1, ROW_TILE, D), lambda e, t: (e, t, 0)),
        out_shape=jax.ShapeDtypeStruct((2, R, D), BF),
        compiler_params=_params(("parallel", "parallel")),
    )(x, ctx, tab)


def _norm_mod_bwd(dh, x, ctx, tab, dxo):
    L = x.shape[1]
    R = L + ctx.shape[1]
    nx = L // ROW_TILE

    def body(dh_ref, x_ref, c_ref, t_ref, dxo_ref, gx_ref, acc_ref):
        t = pl.program_id(1)
        x = jnp.where(t < nx, x_ref[0], c_ref[0])
        r = lax.rsqrt(jnp.mean(x * x, axis=-1, keepdims=True) + EPS)
        xn = x * r
        dh = dh_ref[0]

        @pl.when((t == 0) | (t == nx))
        def _():
            acc_ref[...] = jnp.zeros_like(acc_ref)

        acc_ref[0, 0, 0:1, :] += jnp.sum(dh * xn, axis=0, keepdims=True)
        acc_ref[0, 0, 1:2, :] += jnp.sum(dh, axis=0, keepdims=True)

        @pl.when(t < nx)
        def _():
            dxn = dh * t_ref[0, 0][0:1]
            dx = r * (dxn - xn * jnp.mean(dxn * xn, axis=-1, keepdims=True))
            gx_ref[0] = dxo_ref[0] + dx

    xspec, cspec = _row_specs(L)
    return pl.pallas_call(
        body, name="norm_mod_bwd", grid=(2, R // ROW_TILE),
        in_specs=[pl.BlockSpec((1, ROW_TILE, D), lambda e, t: (e, t, 0)), xspec, cspec,
                  pl.BlockSpec((1, 1, 8, D), lambda e, t: (e, t // nx, 0, 0)), xspec],
        out_specs=(xspec, pl.BlockSpec((1, 1, 8, D), lambda e, t: (e, t // nx, 0, 0))),
        out_shape=(jax.ShapeDtypeStruct((2, L, D), F32), jax.ShapeDtypeStruct((2, 2, 8, D), F32)),
        compiler_params=_params(("parallel", "arbitrary")),
    )(dh, x, ctx, tab, dxo)


POOL_TILE = 256


def _pool_tables(L):
    rows = L // GRID_W
    mats = np.zeros((4, POOL_TILE, POOL_TILE), np.float32)
    inv = np.zeros((4, L, 1), np.float32)
    for gi, k in enumerate(POOL_WINDOWS):
        lo, hi = k // 2, k - 1 - k // 2
        m = np.zeros((GRID_W, GRID_W), np.float32)
        for t in range(GRID_W):
            m[t, max(t - lo, 0):min(t + hi, GRID_W - 1) + 1] = 1.0
        for b in range(POOL_TILE // GRID_W):
            mats[gi, b * GRID_W:(b + 1) * GRID_W, b * GRID_W:(b + 1) * GRID_W] = m
        cnt_c = m.sum(1)
        cnt_r = np.array([min(r + hi, rows - 1) - max(r - lo, 0) + 1 for r in range(rows)], np.float32)
        inv[gi, :, 0] = (1.0 / (cnt_r[:, None] * cnt_c[None, :])).reshape(-1)
    matsT = np.ascontiguousarray(np.transpose(mats, (0, 2, 1)))
    return (jnp.asarray(mats, BF), jnp.asarray(matsT, BF), jnp.asarray(inv))


def _pool_cols(get_tile, mat, cs_ref, L, n):
    def step(i, carry):
        off = pl.multiple_of(i * POOL_TILE, POOL_TILE)
        cs_ref[pl.ds(GRID_W + off, POOL_TILE), :] = _dot_sr(mat, get_tile(off).astype(F32), n)
        return carry

    lax.fori_loop(0, L // POOL_TILE, step, 0)
    cs_ref[pl.ds(0, GRID_W), :] = jnp.zeros((GRID_W, PGW), F32)

    def prefix(r, carry):
        o = pl.multiple_of(r * GRID_W, GRID_W)
        cs_ref[pl.ds(o + GRID_W, GRID_W), :] = cs_ref[pl.ds(o + GRID_W, GRID_W), :] + cs_ref[pl.ds(o, GRID_W), :]
        return carry

    lax.fori_loop(0, L // GRID_W, prefix, 0)


def _pool_rows(cs_ref, off, below, above, L):
    rows = L // GRID_W
    r0 = off // GRID_W
    parts = []
    for i in range(POOL_TILE // GRID_W):
        hi = pl.multiple_of(jnp.minimum(r0 + i + above + 1, rows) * GRID_W, GRID_W)
        lo = pl.multiple_of(jnp.maximum(r0 + i - below, 0) * GRID_W, GRID_W)
        parts.append(cs_ref[pl.ds(hi, GRID_W), :] - cs_ref[pl.ds(lo, GRID_W), :])
    return jnp.concatenate(parts, axis=0)


def _pool_fwd(proj3, pool_w_bf, pool_scale, tables, L):
    mats, _, inv = tables
    nt = L // POOL_TILE

    def body(v_ref, z_ref, pw_ref, ps_ref, m_ref, inv_ref, o_ref, cs_ref):
        _pool_cols(lambda off: v_ref[0, pl.ds(off, POOL_TILE), :], m_ref[0], cs_ref, L, 1)
        half = lax.shift_left(1, pl.program_id(1))

        def step(i, carry):
            off = pl.multiple_of(i * POOL_TILE, POOL_TILE)
            rows = pl.ds(off, POOL_TILE)
            v = v_ref[0, rows, :].astype(F32)
            diff = _pool_rows(cs_ref, off, half, half - 1, L) * inv_ref[0, rows, :] - v
            yp = _dot(diff, pw_ref[0])
            z = z_ref[0, rows, :].astype(F32)
            o_ref[0, rows, :] = (yp * ps_ref[...] * (z * _sigmoid(z))).astype(BF)
            return carry

        lax.fori_loop(0, nt, step, 0)

    return pl.pallas_call(
        body, name="pool_fwd", grid=(2, 4),
        in_specs=[pl.BlockSpec((1, L, PGW), lambda e, g: (e, 0, g)),
                  pl.BlockSpec((1, L, PGW), lambda e, g: (e, 0, 4 + g)),
                  pl.BlockSpec((1, PGW, PGW), lambda e, g: (g, 0, 0)),
                  pl.BlockSpec((1, PGW), lambda e, g: (0, g)),
                  pl.BlockSpec((1, POOL_TILE, POOL_TILE), lambda e, g: (g, 0, 0)),
                  pl.BlockSpec((1, L, 1), lambda e, g: (g, 0, 0))],
        out_specs=pl.BlockSpec((1, L, PGW), lambda e, g: (e, 0, g)),
        out_shape=jax.ShapeDtypeStruct((2, L, D), BF),
        scratch_shapes=[pltpu.VMEM((L + GRID_W, PGW), F32)],
        compiler_params=_params(("parallel", "parallel")),
    )(proj3, proj3, pool_w_bf, pool_scale, mats, inv)


def _pool_bwd(proj3, d_ypool, pool_w_bf, pool_wT_bf, pool_scale, tables, L):
    mats, matsT, inv = tables
    nt = L // POOL_TILE
    R = proj3.shape[1]

    def body(v_ref, z_ref, dy_ref, pw_ref, pwt_ref, ps_ref, m_ref, mt_ref, inv_ref,
             dv_ref, dz_ref, dpw_ref, acc_ref, cs_ref, dd_ref):
        e = pl.program_id(1)

        @pl.when(e == 0)
        def _():
            dpw_ref[...] = jnp.zeros_like(dpw_ref)
            acc_ref[...] = jnp.zeros_like(acc_ref)

        _pool_cols(lambda off: v_ref[0, pl.ds(off, POOL_TILE), :], m_ref[0], cs_ref, L, 1)
        half = lax.shift_left(1, pl.program_id(0))
        ps = ps_ref[...]

        def step(i, carry):
            off = pl.multiple_of(i * POOL_TILE, POOL_TILE)
            rows = pl.ds(off, POOL_TILE)
            v = v_ref[0, rows, :].astype(F32)
            diff = _pool_rows(cs_ref, off, half, half - 1, L) * inv_ref[0, rows, :] - v
            yp = _dot(diff, pw_ref[0])
            z = z_ref[0, rows, :].astype(F32)
            sg = _sigmoid(z)
            sz = z * sg
            dy = dy_ref[0, rows, :]
            dz_ref[0, rows, :] = (dy * yp * ps * (sg * (1.0 + z * (1.0 - sg)))).astype(BF)
            dys = dy * sz
            acc_ref[0, 0:1, :] += jnp.sum(dys * yp, axis=0, keepdims=True)
            dyp = dys * ps
            dpw_ref[0] += _dot_tn(diff, dyp)
            dd_ref[rows, :] = _dot(dyp, pwt_ref[0])
            return carry

        lax.fori_loop(0, nt, step, 0)
        _pool_cols(lambda off: dd_ref[pl.ds(off, POOL_TILE), :] * inv_ref[0, pl.ds(off, POOL_TILE), :],
                   mt_ref[0], cs_ref, L, 2)

        def step2(i, carry):
            off = pl.multiple_of(i * POOL_TILE, POOL_TILE)
            rows = pl.ds(off, POOL_TILE)
            dv_ref[0, rows, :] = (_pool_rows(cs_ref, off, half - 1, half, L) - dd_ref[rows, :]).astype(BF)
            return carry

        lax.fori_loop(0, nt, step2, 0)
        dv_ref[0, pl.ds(L, R - L), :] = jnp.zeros((R - L, PGW), BF)
        dz_ref[0, pl.ds(L, R - L), :] = jnp.zeros((R - L, PGW), BF)

    return pl.pallas_call(
        body, name="pool_bwd", grid=(4, 2),
        in_specs=[pl.BlockSpec((1, L, PGW), lambda g, e: (e, 0, g)),
                  pl.BlockSpec((1, L, PGW), lambda g, e: (e, 0, 4 + g)),
                  pl.BlockSpec((1, L, PGW), lambda g, e: (e, 0, g)),
                  pl.BlockSpec((1, PGW, PGW), lambda g, e: (g, 0, 0)),
                  pl.BlockSpec((1, PGW, PGW), lambda g, e: (g, 0, 0)),
                  pl.BlockSpec((1, PGW), lambda g, e: (0, g)),
                  pl.BlockSpec((1, POOL_TILE, POOL_TILE), lambda g, e: (g, 0, 0)),
                  pl.BlockSpec((1, POOL_TILE, POOL_TILE), lambda g, e: (g, 0, 0)),
                  pl.BlockSpec((1, L, 1), lambda g, e: (g, 0, 0))],
        out_specs=(pl.BlockSpec((1, R, PGW), lambda g, e: (e, 0, g)),
                   pl.BlockSpec((1, R, PGW), lambda g, e: (e, 0, g)),
                   pl.BlockSpec((1, PGW, PGW), lambda g, e: (g, 0, 0)),
                   pl.BlockSpec((1, 8, PGW), lambda g, e: (g, 0, 0))),
        out_shape=(jax.ShapeDtypeStruct((2, R, D), BF), jax.ShapeDtypeStruct((2, R, D), BF),
                   jax.ShapeDtypeStruct((4, PGW, PGW), F32), jax.ShapeDtypeStruct((4, 8, PGW), F32)),
        scratch_shapes=[pltpu.VMEM((L + GRID_W, PGW), F32), pltpu.VMEM((L, PGW), F32)],
        compiler_params=_params(("parallel", "arbitrary")),
    )(proj3, proj3, d_ypool, pool_w_bf, pool_wT_bf, pool_scale, mats, matsT, inv)


CONV_BLOCK = 128


def _conv_tap(u, k, L):
    off = k - 2
    if off == 0:
        return u
    R = u.shape[0]
    r = lax.broadcasted_iota(jnp.int32, (R, 1), 0)
    pos = jnp.where(r < L, r, r - L) + off
    seg = jnp.where(r < L, L, R - L)
    return jnp.where((pos >= 0) & (pos < seg), pltpu.roll(u, (-off) % R, 0), 0.0)


def _conv_fwd(proj3, conv_w, conv_b, L):
    _, R, _ = proj3.shape
    cb0 = OFF_XBC // CONV_BLOCK

    def body(u_ref, w_ref, b_ref, o_ref):
        u = u_ref[0].astype(F32)
        w = w_ref[...]
        pre = b_ref[...] + sum(_conv_tap(u, k, L) * w[k:k + 1, :] for k in range(4))
        o_ref[0] = (pre * _sigmoid(pre)).astype(BF)

    return pl.pallas_call(
        body, name="conv_fwd", grid=(2, CONV_DIM // CONV_BLOCK),
        in_specs=[pl.BlockSpec((1, R, CONV_BLOCK), lambda e, j: (e, 0, cb0 + j)),
                  pl.BlockSpec((4, CONV_BLOCK), lambda e, j: (0, j)),
                  pl.BlockSpec((1, CONV_BLOCK), lambda e, j: (0, j))],
        out_specs=pl.BlockSpec((1, R, CONV_BLOCK), lambda e, j: (e, 0, j)),
        out_shape=jax.ShapeDtypeStruct((2, R, CONV_DIM), BF),
        compiler_params=_params(("parallel", "parallel")),
    )(proj3, conv_w, conv_b)


def _conv_bwd(proj3, addends, scales, col0, ncols, in_maps, conv_w, conv_b, L, name):
    _, R, _ = proj3.shape
    cb0 = (OFF_XBC + col0) // CONV_BLOCK
    wb0 = col0 // CONV_BLOCK
    na = len(addends)
    scaled = [i for i in range(na) if scales[i] is not None]

    def body(*refs):
        u_ref, w_ref, b_ref = refs[0], refs[1], refs[2]
        a_refs = refs[3:3 + na]
        s_refs = dict(zip(scaled, refs[3 + na:3 + na + len(scaled)]))
        o_ref, acc_ref = refs[3 + na + len(scaled)], refs[4 + na + len(scaled)]
        u = u_ref[0].astype(F32)
        w = w_ref[...]
        taps = [_conv_tap(u, k, L) for k in range(4)]
        pre = b_ref[...] + sum(taps[k] * w[k:k + 1, :] for k in range(4))
        sg = _sigmoid(pre)
        dxbc = jnp.zeros(u.shape, F32)
        for i, a in enumerate(a_refs):
            t = a[0].astype(F32)
            dxbc = dxbc + (t * s_refs[i][...] if i in s_refs else t)
        dpre = dxbc * (sg * (1.0 + pre * (1.0 - sg)))
        acc_ref[...] = jnp.zeros_like(acc_ref)
        for k in range(4):
            acc_ref[0, k:k + 1, :] = jnp.sum(dpre * taps[k], axis=0, keepdims=True)
        acc_ref[0, 4:5, :] = jnp.sum(dpre, axis=0, keepdims=True)
        du = sum(_conv_tap(dpre, 4 - k, L) * w[k:k + 1, :] for k in range(4))
        o_ref[0] = du.astype(BF)

    in_specs = [pl.BlockSpec((1, R, CONV_BLOCK), lambda e, j: (e, 0, cb0 + j)),
                pl.BlockSpec((4, CONV_BLOCK), lambda e, j: (0, wb0 + j)),
                pl.BlockSpec((1, CONV_BLOCK), lambda e, j: (0, wb0 + j))]
    for m in in_maps:
        in_specs.append(pl.BlockSpec((1, R, CONV_BLOCK), functools.partial(lambda e, j, m: (e, 0, m(j)), m=m)))
    for i in scaled:
        in_specs.append(pl.BlockSpec((1, CONV_BLOCK), functools.partial(lambda e, j, m: (0, m(j)), m=in_maps[i])))
    return pl.pallas_call(
        body, name=name, grid=(2, ncols // CONV_BLOCK),
        in_specs=in_specs,
        out_specs=(pl.BlockSpec((1, R, CONV_BLOCK), lambda e, j: (e, 0, j)),
                   pl.BlockSpec((1, 8, CONV_BLOCK), lambda e, j: (e, 0, j))),
        out_shape=(jax.ShapeDtypeStruct((2, R, ncols), BF), jax.ShapeDtypeStruct((2, 8, ncols), F32)),
        compiler_params=_params(("parallel", "parallel")),
    )(proj3, conv_w, conv_b, *addends, *[scales[i] for i in scaled])


def _softplus(x):
    e = jnp.exp(-jnp.abs(x))
    u = 1.0 + e
    return jnp.maximum(x, 0.0) + jnp.where(u == 1.0, e, e * jnp.log(u) / (u - 1.0))


def _dt_fwd(dt_raw, bias128):
    _, R, _ = dt_raw.shape

    def body(x_ref, b_ref, o_ref):
        o_ref[0] = _softplus(x_ref[0] + b_ref[...])

    return pl.pallas_call(
        body, name="dt_fwd", grid=(2,),
        in_specs=[pl.BlockSpec((1, R, 128), lambda e: (e, 0, 0)), pl.BlockSpec((1, 128), lambda e: (0, 0))],
        out_specs=pl.BlockSpec((1, R, 128), lambda e: (e, 0, 0)),
        out_shape=jax.ShapeDtypeStruct(dt_raw.shape, F32),
        compiler_params=_params(("parallel",)),
    )(dt_raw, bias128)


def _dt_bwd(dt_raw, bias128, ddt_f, ddt_b):
    _, R, _ = dt_raw.shape

    def body(x_ref, b_ref, f_ref, g_ref, o_ref, acc_ref):
        d = (f_ref[0] + g_ref[0]) * _sigmoid(x_ref[0] + b_ref[...])
        o_ref[0] = d.astype(BF)
        acc_ref[...] = jnp.zeros_like(acc_ref)
        acc_ref[0, 0:1, :] = jnp.sum(d, axis=0, keepdims=True)

    blk = pl.BlockSpec((1, R, 128), lambda e: (e, 0, 0))
    return pl.pallas_call(
        body, name="dt_bwd", grid=(2,),
        in_specs=[blk, pl.BlockSpec((1, 128), lambda e: (0, 0)), blk, blk],
        out_specs=(blk, pl.BlockSpec((1, 8, 128), lambda e: (e, 0, 0))),
        out_shape=(jax.ShapeDtypeStruct(dt_raw.shape, BF), jax.ShapeDtypeStruct((2, 8, 128), F32)),
        compiler_params=_params(("parallel",)),
    )(dt_raw, bias128, ddt_f, ddt_b)


def _tri(d):
    i = lax.broadcasted_iota(jnp.int32, (Q, Q), 0)
    j = lax.broadcasted_iota(jnp.int32, (Q, Q), 1)
    return (i >= j) if d == 0 else (i <= j)


def _expand_mat(d):
    r = lax.broadcasted_iota(jnp.int32, (128, GWID), 0)
    c = lax.broadcasted_iota(jnp.int32, (128, GWID), 1)
    return (r == d * HPG + jnp.right_shift(c, 6)).astype(BF)


def _reduce_mat(d):
    r = lax.broadcasted_iota(jnp.int32, (GWID, 128), 0)
    c = lax.broadcasted_iota(jnp.int32, (GWID, 128), 1)
    return (c == d * HPG + jnp.right_shift(r, 6)).astype(BF)


def _ssd_chunk(d, dt, A, xs, B, C):
    mask = _tri(d)
    T = mask.astype(BF)
    Tt = _tri(1 - d).astype(BF)
    a = dt * A
    acs = _dot_sr(T, a)
    E = _expand_mat(d)
    dt_e = _dot_sl(dt, E, 2)
    acs_e = _dot_sl(acs, E, 2)
    alast_e = acs_e[Q - 1:Q, :] if d == 0 else acs_e[0:1, :]
    return dict(mask=mask, T=T, Tt=Tt, acs=acs, acsT=acs.T, dt_e=dt_e, acs_e=acs_e, lam=jnp.exp(acs_e),
                w=jnp.exp(alast_e - acs_e), decay=jnp.exp(alast_e), xt=xs * dt_e, CB=_dot_nt(C, B))


def _head_decay(q, d, hh):
    col = q["acs"][:, d * HPG + hh:d * HPG + hh + 1]
    row = q["acsT"][d * HPG + hh:d * HPG + hh + 1, :]
    return jnp.exp(jnp.where(q["mask"], col - row, -jnp.inf))


def _chunk_maps(NX, NS):
    cf = lambda s: lax.rem(s + NX, NS)
    cb = lambda s: NS - 1 - s
    return cf, cb


def _ssd_fwd(xbc, dt_loc, a_loc, L):
    _, R, _ = xbc.shape
    NX, NS = L // Q, R // Q
    cf, cb = _chunk_maps(NX, NS)

    def body(xs_f, b_f, c_f, dt_f, xs_b, b_b, c_b, dt_b, a_ref, y_f, hs_f, y_b, hs_b, hT):
        @pl.when(pl.program_id(2) == 0)
        def _():
            hT[...] = jnp.zeros_like(hT)

        A = a_ref[0, 0:1, :]
        lane = lax.broadcasted_iota(jnp.int32, (Q, 128), 1)
        for d, (xs_ref, b_ref, c_ref, dt_ref, y_ref, hs_ref) in enumerate(
                ((xs_f, b_f, c_f, dt_f, y_f, hs_f), (xs_b, b_b, c_b, dt_b, y_b, hs_b))):
            xs, B, C = xs_ref[0].astype(F32), b_ref[0], c_ref[0]
            q = _ssd_chunk(d, dt_ref[0, 0], A, xs, B, C)
            h = hT[d]
            hb = h.astype(BF)
            hs_ref[0, 0] = hb
            parts = []
            for pr in range(HPG // 2):
                xp = q["xt"][:, pr * 128:(pr + 1) * 128].astype(BF)
                r0 = _dot(q["CB"] * _head_decay(q, d, 2 * pr), xp)
                r1 = _dot(q["CB"] * _head_decay(q, d, 2 * pr + 1), xp)
                parts.append(jnp.where(lane < HEAD, r0, r1))
            y_ref[0] = jnp.concatenate(parts, axis=1) + _dot(C, hb) * q["lam"]
            hT[d] = q["decay"] * h + _dot_tn(B, q["xt"] * q["w"])

    def spec(shape, imap):
        return pl.BlockSpec(shape, imap)

    def ins(c):
        return [spec((1, Q, GWID), lambda e, g, s: (e, c(s), g)),
                spec((1, Q, NST), lambda e, g, s: (e, c(s), DIN // NST + g)),
                spec((1, Q, NST), lambda e, g, s: (e, c(s), DIN // NST + NG + g)),
                spec((1, 1, Q, 128), lambda e, g, s: (e, g, c(s), 0))]

    def outs(c):
        return [spec((1, Q, GWID), lambda e, g, s: (e, c(s), g)),
                spec((1, 1, NST, GWID), lambda e, g, s: (e, c(s), 0, g))]

    yshape = jax.ShapeDtypeStruct((2, R, DIN), F32)
    hshape = jax.ShapeDtypeStruct((2, NS, NST, DIN), BF)
    return pl.pallas_call(
        body, name="ssd_fwd", grid=(2, NG, NS),
        in_specs=ins(cf) + ins(cb) + [spec((1, 8, 128), lambda e, g, s: (g, 0, 0))],
        out_specs=tuple(outs(cf) + outs(cb)),
        out_shape=(yshape, hshape, yshape, hshape),
        scratch_shapes=[pltpu.VMEM((2, NST, GWID), F32)],
        compiler_params=_params(("parallel", "parallel", "arbitrary")),
    )(xbc, xbc, xbc, dt_loc, xbc, xbc, xbc, dt_loc, a_loc)


def _ssd_bwd(xbc, dt_loc, a_loc, hs_f, hs_b, y_f, y_b, dy, L):
    _, R, _ = xbc.shape
    NX, NS = L // Q, R // Q
    cf0, cb0 = _chunk_maps(NX, NS)
    cf = lambda sp: cf0(NS - 1 - sp)
    cb = lambda sp: cb0(NS - 1 - sp)

    def body(xs_f, b_f, c_f, dt_f, hs_f_, dy_f, y_f_, xs_b, b_b, c_b, dt_b, hs_b_, dy_b, y_b_, a_ref,
             dxs_f, dbc_f, ddt_f, dxs_b, dbc_b, ddt_b, da_ref, dhT):
        @pl.when(pl.program_id(2) == 0)
        def _():
            dhT[...] = jnp.zeros_like(dhT)
            da_ref[...] = jnp.zeros_like(da_ref)

        A = a_ref[0, 0:1, :]
        lane = lax.broadcasted_iota(jnp.int32, (Q, 128), 1)
        row = lax.broadcasted_iota(jnp.int32, (Q, 128), 0)
        for d, (xs_ref, b_ref, c_ref, dt_ref, hs_ref, dy_ref, y_ref, dxs_ref, dbc_ref, ddt_ref) in enumerate(
                ((xs_f, b_f, c_f, dt_f, hs_f_, dy_f, y_f_, dxs_f, dbc_f, ddt_f),
                 (xs_b, b_b, c_b, dt_b, hs_b_, dy_b, y_b_, dxs_b, dbc_b, ddt_b))):
            xs, B, C, dt = xs_ref[0].astype(F32), b_ref[0], c_ref[0], dt_ref[0, 0]
            q = _ssd_chunk(d, dt, A, xs, B, C)
            xt, lam, w, decay = q["xt"], q["lam"], q["w"], q["decay"]
            H = hs_ref[0, 0]
            dyv = dy_ref[0].astype(F32)
            dh = dhT[d]
            dZ = dyv * lam
            dC = _dot_nt(dZ, H)
            dH = _dot_tn(C, dZ)
            U = _dot(B, dh)
            xw = xt * w
            dxt = U * w
            dalast_e = (jnp.sum(U * xw, axis=0, keepdims=True)
                        + decay * jnp.sum(dh * H.astype(F32), axis=0, keepdims=True))
            dB = _dot_nt(xw, dh)
            dCB = jnp.zeros((Q, Q), F32)
            dxt_parts = []
            for pr in range(HPG // 2):
                xp = xt[:, pr * 128:(pr + 1) * 128]
                dyp = dyv[:, pr * 128:(pr + 1) * 128]
                dxp = jnp.zeros((Q, 128), F32)
                for h2 in range(2):
                    Lh = _head_decay(q, d, 2 * pr + h2)
                    dym = jnp.where((lane < HEAD) if h2 == 0 else (lane >= HEAD), dyp, 0.0)
                    dxp = dxp + _dot_tn(q["CB"] * Lh, dym)
                    dCB = dCB + _dot_nt(dym, xp) * Lh
                dxt_parts.append(dxp)
            dxt_diag = jnp.concatenate(dxt_parts, axis=1)
            dC = dC + _dot(dCB, B)
            dB = dB + _dot_tn(dCB, C)
            Rm = _reduce_mat(d)
            dacs = _dot_sl(dyv * y_ref[0] - xt.astype(BF).astype(F32) * dxt_diag - U * xw, Rm, 2)
            dxt = dxt + dxt_diag
            dal = _dot_sl(jnp.broadcast_to(dalast_e, (8, GWID)), Rm, 2)[0:1, :]
            dacs = dacs + jnp.where(row == (Q - 1 if d == 0 else 0), dal, 0.0)
            da = _dot_sr(q["Tt"], dacs, 2)
            ddt_ref[0, 0] = da * A + _dot_sl(dxt * xs, Rm, 2)
            da_ref[0, 0, 0:1, :] += jnp.sum(da * dt, axis=0, keepdims=True)
            dxs_ref[0] = (dxt * q["dt_e"]).astype(BF)
            dbc_ref[0] = jnp.concatenate([dB, dC], axis=1).astype(BF)
            dhT[d] = decay * dh + dH

    def spec(shape, imap):
        return pl.BlockSpec(shape, imap)

    def ins(c):
        return [spec((1, Q, GWID), lambda e, g, s: (e, c(s), g)),
                spec((1, Q, NST), lambda e, g, s: (e, c(s), DIN // NST + g)),
                spec((1, Q, NST), lambda e, g, s: (e, c(s), DIN // NST + NG + g)),
                spec((1, 1, Q, 128), lambda e, g, s: (e, g, c(s), 0)),
                spec((1, 1, NST, GWID), lambda e, g, s: (e, c(s), 0, g)),
                spec((1, Q, GWID), lambda e, g, s: (e, c(s), g)),
                spec((1, Q, GWID), lambda e, g, s: (e, c(s), g))]

    def outs(c):
        return [spec((1, Q, GWID), lambda e, g, s: (e, c(s), g)),
                spec((1, Q, 2 * NST), lambda e, g, s: (e, c(s), g)),
                spec((1, 1, Q, 128), lambda e, g, s: (e, g, c(s), 0))]

    s_xs = jax.ShapeDtypeStruct((2, R, DIN), BF)
    s_bc = jax.ShapeDtypeStruct((2, R, 2 * NG * NST), BF)
    s_dt = jax.ShapeDtypeStruct((2, NG, R, 128), F32)
    return pl.pallas_call(
        body, name="ssd_bwd", grid=(2, NG, NS),
        in_specs=ins(cf) + ins(cb) + [spec((1, 8, 128), lambda e, g, s: (g, 0, 0))],
        out_specs=tuple(outs(cf) + outs(cb) + [spec((1, 1, 8, 128), lambda e, g, s: (e, g, 0, 0))]),
        out_shape=(s_xs, s_bc, s_dt, s_xs, s_bc, s_dt, jax.ShapeDtypeStruct((2, NG, 8, 128), F32)),
        scratch_shapes=[pltpu.VMEM((2, NST, GWID), F32)],
        compiler_params=_params(("parallel", "parallel", "arbitrary")),
    )(xbc, xbc, xbc, dt_loc, hs_f, dy, y_f, xbc, xbc, xbc, dt_loc, hs_b, dy, y_b, a_loc)


def _ssd_post_fwd(y_f, y_b, xbc, proj3, dskip_e, ssd_norm, L):
    def body(yf_ref, yb_ref, xs_ref, z_ref, ds_ref, w_ref, o_ref):
        y2 = yf_ref[0] + yb_ref[0] + ds_ref[...] * xs_ref[0].astype(F32)
        z = z_ref[0].astype(F32)
        u = y2 * (z * _sigmoid(z))
        parts = []
        for g in range(NG):
            ug = u[:, g * GWID:(g + 1) * GWID]
            parts.append(ug * lax.rsqrt(jnp.mean(ug * ug, axis=-1, keepdims=True) + EPS))
        o_ref[0] = (jnp.concatenate(parts, axis=1) * w_ref[...]).astype(BF)

    blk = lambda c: pl.BlockSpec((1, ROW_TILE, DIN), lambda e, t: (e, t, c))
    vec = pl.BlockSpec((1, DIN), lambda e, t: (0, 0))
    return pl.pallas_call(
        body, name="ssd_post_fwd", grid=(2, L // ROW_TILE),
        in_specs=[blk(0), blk(0), blk(0), blk(1), vec, vec],
        out_specs=blk(0),
        out_shape=jax.ShapeDtypeStruct((2, L, DIN), BF),
        compiler_params=_params(("parallel", "parallel")),
    )(y_f, y_b, xbc, proj3, dskip_e, ssd_norm)


def _ssd_post_bwd(d_yn, y_f, y_b, xbc, proj3, dskip_e, ssd_norm, L):
    _, R, _ = y_f.shape
    nx = L // ROW_TILE

    def body(dyn_ref, yf_ref, yb_ref, xs_ref, z_ref, ds_ref, w_ref, dy_ref, dz_ref, acc_ref):
        t = pl.program_id(1)

        @pl.when(t == 0)
        def _():
            acc_ref[...] = jnp.zeros_like(acc_ref)

        @pl.when(t >= nx)
        def _():
            dy_ref[...] = jnp.zeros_like(dy_ref)
            dz_ref[...] = jnp.zeros_like(dz_ref)

        @pl.when(t < nx)
        def _():
            xs = xs_ref[0].astype(F32)
            y2 = yf_ref[0] + yb_ref[0] + ds_ref[...] * xs
            z = z_ref[0].astype(F32)
            sg = _sigmoid(z)
            sz = z * sg
            u = y2 * sz
            dyn = dyn_ref[0]
            dun = dyn * w_ref[...]
            uh_parts, du_parts = [], []
            for g in range(NG):
                sl = slice(g * GWID, (g + 1) * GWID)
                ug = u[:, sl]
                rg = lax.rsqrt(jnp.mean(ug * ug, axis=-1, keepdims=True) + EPS)
                uh = ug * rg
                dg = dun[:, sl]
                du_parts.append(rg * (dg - uh * jnp.mean(dg * uh, axis=-1, keepdims=True)))
                uh_parts.append(uh)
            du = jnp.concatenate(du_parts, axis=1)
            uh = jnp.concatenate(uh_parts, axis=1)
            dy2 = du * sz
            dy_ref[0] = dy2.astype(BF)
            dz_ref[0] = (du * y2 * (sg * (1.0 + z * (1.0 - sg)))).astype(BF)
            acc_ref[0, 0:1, :] += jnp.sum(dyn * uh, axis=0, keepdims=True)
            acc_ref[0, 1:2, :] += jnp.sum(dy2 * xs, axis=0, keepdims=True)

    xmap = lambda c: (lambda e, t: (e, jnp.minimum(t, nx - 1), c))
    blk = lambda c: pl.BlockSpec((1, ROW_TILE, DIN), xmap(c))
    oblk = pl.BlockSpec((1, ROW_TILE, DIN), lambda e, t: (e, t, 0))
    vec = pl.BlockSpec((1, DIN), lambda e, t: (0, 0))
    return pl.pallas_call(
        body, name="ssd_post_bwd", grid=(2, R // ROW_TILE),
        in_specs=[blk(0), blk(0), blk(0), blk(0), blk(1), vec, vec],
        out_specs=(oblk, oblk, pl.BlockSpec((1, 8, DIN), lambda e, t: (e, 0, 0))),
        out_shape=(jax.ShapeDtypeStruct((2, R, DIN), BF), jax.ShapeDtypeStruct((2, R, DIN), BF),
                   jax.ShapeDtypeStruct((2, 8, DIN), F32)),
        compiler_params=_params(("parallel", "arbitrary")),
    )(d_yn, y_f, y_b, xbc, proj3, dskip_e, ssd_norm)


def _merge_fwd(proj3, P, S, b_merge, L):
    def body(gp_ref, p_ref, s_ref, b_ref, o_ref):
        gt = _sigmoid(gp_ref[0].astype(F32) + b_ref[...])
        o_ref[0] = (gt[:, :D] * p_ref[0] + gt[:, D:] * s_ref[0]).astype(BF)

    blk = pl.BlockSpec((1, ROW_TILE, D), lambda e, t: (e, t, 0))
    return pl.pallas_call(
        body, name="merge_fwd", grid=(2, L // ROW_TILE),
        in_specs=[pl.BlockSpec((1, ROW_TILE, 2 * D), lambda e, t: (e, t, OFF_GATE // (2 * D))), blk, blk,
                  pl.BlockSpec((1, 2 * D), lambda e, t: (0, 0))],
        out_specs=blk, out_shape=jax.ShapeDtypeStruct((2, L, D), BF),
        compiler_params=_params(("parallel", "parallel")),
    )(proj3, P, S, b_merge)


def _merge_bwd(d_merged, proj3, P, S, b_merge, L):
    _, R, _ = proj3.shape
    nx = L // ROW_TILE

    def body(dm_ref, gp_ref, p_ref, s_ref, b_ref, dp_ref, ds_ref, dg_ref, acc_ref):
        t = pl.program_id(1)

        @pl.when(t == 0)
        def _():
            acc_ref[...] = jnp.zeros_like(acc_ref)

        @pl.when(t >= nx)
        def _():
            dg_ref[...] = jnp.zeros_like(dg_ref)

        @pl.when(t < nx)
        def _():
            gt = _sigmoid(gp_ref[0].astype(F32) + b_ref[...])
            dm = dm_ref[0]
            g1, g2 = gt[:, :D], gt[:, D:]
            dp_ref[0] = (dm * g1).astype(BF)
            ds_ref[0] = (dm * g2).astype(BF)
            dgp = jnp.concatenate([dm * p_ref[0] * g1 * (1.0 - g1), dm * s_ref[0] * g2 * (1.0 - g2)], axis=1)
            dg_ref[0] = dgp.astype(BF)
            acc_ref[0, 0:1, :] += jnp.sum(dgp, axis=0, keepdims=True)

    xmap = lambda e, t: (e, jnp.minimum(t, nx - 1), 0)
    blk = pl.BlockSpec((1, ROW_TILE, D), xmap)
    return pl.pallas_call(
        body, name="merge_bwd", grid=(2, R // ROW_TILE),
        in_specs=[blk, pl.BlockSpec((1, ROW_TILE, 2 * D), lambda e, t: (e, jnp.minimum(t, nx - 1), OFF_GATE // (2 * D))),
                  blk, blk, pl.BlockSpec((1, 2 * D), lambda e, t: (0, 0))],
        out_specs=(blk, blk, pl.BlockSpec((1, ROW_TILE, 2 * D), lambda e, t: (e, t, 0)),
                   pl.BlockSpec((1, 8, 2 * D), lambda e, t: (e, 0, 0))),
        out_shape=(jax.ShapeDtypeStruct((2, L, D), BF), jax.ShapeDtypeStruct((2, L, D), BF),
                   jax.ShapeDtypeStruct((2, R, 2 * D), BF), jax.ShapeDtypeStruct((2, 8, 2 * D), F32)),
        compiler_params=_params(("parallel", "arbitrary")),
    )(d_merged, proj3, P, S, b_merge)


def _final(out3, x, tgt, gtab, norm_post, L):
    def body(o_ref, x_ref, t_ref, g_ref, n_ref, dxo_ref, do_ref, acc_ref):
        @pl.when(pl.program_id(1) == 0)
        def _():
            acc_ref[...] = jnp.zeros_like(acc_ref)

        o = o_ref[0]
        gate = g_ref[0, 0:1, :]
        npost = n_ref[...]
        r2 = lax.rsqrt(jnp.mean(o * o, axis=-1, keepdims=True) + EPS)
        nh = o * r2
        on = nh * npost
        err = x_ref[0] + gate * on - t_ref[0]
        dxo = err * (1.0 / D)
        dxo_ref[0] = dxo
        dnh = dxo * gate * npost
        do_ref[0] = (r2 * (dnh - nh * jnp.mean(dnh * nh, axis=-1, keepdims=True))).astype(BF)
        acc_ref[0, 0:1, :] += jnp.sum(dxo * on, axis=0, keepdims=True)
        acc_ref[0, 1:2, :] += jnp.sum(dxo * gate * nh, axis=0, keepdims=True)
        acc_ref[0, 2:3, :] += jnp.sum(err * err, axis=0, keepdims=True)

    blk = pl.BlockSpec((1, ROW_TILE, D), lambda e, t: (e, t, 0))
    return pl.pallas_call(
        body, name="final", grid=(2, L // ROW_TILE),
        in_specs=[blk, blk, blk, pl.BlockSpec((1, 8, D), lambda e, t: (e, 0, 0)),
                  pl.BlockSpec((1, D), lambda e, t: (0, 0))],
        out_specs=(blk, blk, pl.BlockSpec((1, 8, D), lambda e, t: (e, 0, 0))),
        out_shape=(jax.ShapeDtypeStruct((2, L, D), F32), jax.ShapeDtypeStruct((2, L, D), BF),
                   jax.ShapeDtypeStruct((2, 8, D), F32)),
        compiler_params=_params(("parallel", "arbitrary")),
    )(out3, x, tgt, gtab, norm_post)


def _local_step(x, c, ctx, loss_target, W):
    nb, L, _ = x.shape
    LC = ctx.shape[1]
    R = L + LC
    assert nb == 2 and L % ROW_TILE == 0 and LC % Q == 0 and L % POOL_TILE == 0
    w_inT = W["w_in"]
    w_dtT = jnp.pad(w_inT[OFF_DT:], ((0, 64), (0, 0)))
    tables = _pool_tables(L)
    tr, tl = (2 * R) // 8, (2 * L) // 8

    c16 = jnp.zeros((16, D), F32).at[0:2].set(c).at[2].set(W["c_ctx"])
    mod16 = _adaln_fwd(c16, W["w_ada"], W["b_ada"])
    shift, scale, gate = mod16[:, :D], mod16[:, D:2 * D], mod16[:, 2 * D:]
    npre = W["norm_pre"]
    tab = jnp.zeros((2, 2, 8, D), F32)
    for e in range(2):
        tab = tab.at[e, 0, 0].set(npre[0] * (1.0 + scale[e])).at[e, 0, 1].set(shift[e])
        tab = tab.at[e, 1, 0].set(npre[0] * (1.0 + scale[2])).at[e, 1, 1].set(shift[2])
    gtab = jnp.zeros((2, 8, D), F32).at[:, 0].set(gate[0:2])

    hx = _norm_mod_fwd(x, ctx, tab)
    hx2 = hx.reshape(2 * R, D)
    proj3 = _matmul(hx2, w_inT, BF, "proj_main", tm=tr, tn=1024, bt=True, n=OFF_DT).reshape(2, R, OFF_DT)
    dt_raw = _matmul(hx2, w_dtT, F32, "proj_dt", tm=tr, bt=True).reshape(2, R, 128)
    ypool = _pool_fwd(proj3, W["pool_w"], W["pool_scale"], tables, L)
    xbc = _conv_fwd(proj3, W["conv_w"], W["conv_b"], L)
    bias128 = jnp.pad(W["dt_bias"].reshape(1, 64), ((0, 0), (0, 64)))
    dt = _dt_fwd(dt_raw, bias128)
    to_loc = lambda t: jnp.pad(t[:, :, :64].reshape(2, R, 2, NG, HPG).transpose(0, 3, 1, 2, 4).reshape(2, NG, R, 16),
                               ((0, 0), (0, 0), (0, 0), (0, 112)))
    from_loc = lambda t: jnp.pad(t[..., :16].reshape(2, NG, R, 2, HPG).transpose(0, 2, 3, 1, 4).reshape(2, R, 64),
                                 ((0, 0), (0, 0), (0, 64)))
    dt_loc = to_loc(dt)
    A = -jnp.exp(W["a_log"].reshape(2, NG, HPG))
    a_loc = jnp.zeros((NG, 8, 128), F32).at[:, 0, :16].set(A.transpose(1, 0, 2).reshape(NG, 16))
    y_f, hs_f, y_b, hs_b = _ssd_fwd(xbc, dt_loc, a_loc, L)
    dskip_e = jnp.repeat(W["d_skip"].reshape(1, 32), HEAD, axis=1)
    yn = _ssd_post_fwd(y_f, y_b, xbc, proj3, dskip_e, W["ssd_norm"], L)
    ypool2, yn2 = ypool.reshape(2 * L, D), yn.reshape(2 * L, DIN)
    P = _matmul(ypool2, W["w_proj_pool"], F32, "proj_pool", tm=tl, tn=1024).reshape(2, L, D)
    S = _matmul(yn2, W["w_proj_ssd"], F32, "proj_ssd", tm=tl, tn=1024).reshape(2, L, D)
    merged = _merge_fwd(proj3, P, S, W["b_merge"], L)
    merged2 = merged.reshape(2 * L, D)
    out3 = _matmul(merged2, W["w_out"], F32, "proj_out", tm=tl, tn=1024).reshape(2, L, D)
    dxo, dout, acc_f = _final(out3, x, loss_target, gtab, W["norm_post"], L)

    dout2 = dout.reshape(2 * L, D)
    g = {}
    g["w_out"] = _matmul_tn(merged2, dout2, "dw_out", ta=1024, tn=1024, tr=tl)
    d_merged = _matmul(dout2, W["w_out"], F32, "d_merged", tm=tl, tn=1024, bt=True).reshape(2, L, D)
    dP, dS, dgp, acc_m = _merge_bwd(d_merged, proj3, P, S, W["b_merge"], L)
    dP2, dS2 = dP.reshape(2 * L, D), dS.reshape(2 * L, D)
    g["w_proj_pool"] = _matmul_tn(ypool2, dP2, "dw_proj_pool", ta=1024, tn=1024, tr=tl)
    g["w_proj_ssd"] = _matmul_tn(yn2, dS2, "dw_proj_ssd", ta=1024, tn=1024, tr=tl)
    d_ypool = _matmul(dP2, W["w_proj_pool"], F32, "d_ypool", tm=tl, tn=1024, bt=True).reshape(2, L, D)
    d_yn = _matmul(dS2, W["w_proj_ssd"], F32, "d_yn", tm=tl, tn=1024, bt=True).reshape(2, L, DIN)
    dv, dzp, g["pool_w"], acc_p = _pool_bwd(proj3, d_ypool, W["pool_w"], jnp.swapaxes(W["pool_w"], 1, 2),
                                            W["pool_scale"], tables, L)
    dy2, dzs, acc_s = _ssd_post_bwd(d_yn, y_f, y_b, xbc, proj3, dskip_e, W["ssd_norm"], L)
    dxs_f, dbc_f, ddt_f, dxs_b, dbc_b, ddt_b, acc_a = _ssd_bwd(xbc, dt_loc, a_loc, hs_f, hs_b, y_f, y_b, dy2, L)
    ident = lambda j: j
    dxr_xs, acc_cx = _conv_bwd(proj3, [dxs_f, dxs_b, dy2], [None, None, dskip_e], 0, DIN, [ident, ident, ident],
                               W["conv_w"], W["conv_b"], L, "conv_bwd_xs")
    bcmap = lambda j: 2 * lax.rem(j, NG) + j // NG
    dxr_bc, acc_cb = _conv_bwd(proj3, [dbc_f, dbc_b], [None, None], DIN, 2 * NG * NST, [bcmap, bcmap],
                               W["conv_w"], W["conv_b"], L, "conv_bwd_bc")
    ddtr, acc_d = _dt_bwd(dt_raw, bias128, from_loc(ddt_f), from_loc(ddt_b))
    pieces = [dv, dzp, dzs, dgp, dxr_xs, dxr_bc]
    dw_rows = [_matmul_tn(p.reshape(2 * R, p.shape[2]), hx2, "dw_in_%d" % i, ta=1024, tn=1024, tr=tr)
               for i, p in enumerate(pieces)]
    dw_rows.append(_matmul_tn(ddtr.reshape(2 * R, 128), hx2, "dw_in_dt", ta=128, tn=1024, tr=tr)[:64])
    g["w_in"] = jnp.concatenate(dw_rows, axis=0)
    dh = _dhx(pieces, ddtr, w_inT, w_dtT)
    grad_x, acc_n = _norm_mod_bwd(dh, x, ctx, tab, dxo)
    g["w_ada"], db_rows, sm_rows = _adaln_bwd(acc_n, acc_f, mod16, c16, npre, W["w_ada"])

    g["b_ada"] = db_rows[0:1]
    g["norm_pre"] = sm_rows[0:1]
    g["c_ctx"] = sm_rows[1]
    g["norm_post"] = acc_f[0, 1:2] + acc_f[1, 1:2]
    g["b_merge"] = acc_m[0, 0:1] + acc_m[1, 0:1]
    g["pool_scale"] = acc_p[:, 0, :].reshape(1, D)
    acc_c = jnp.concatenate([acc_cx[0] + acc_cx[1], acc_cb[0] + acc_cb[1]], axis=1)
    g["conv_w"] = acc_c[0:4]
    g["conv_b"] = acc_c[4:5]
    g["dt_bias"] = (acc_d[0, 0, :64] + acc_d[1, 0, :64]).reshape(2, 32)
    dA = (acc_a[0, :, 0, :16] + acc_a[1, :, 0, :16]).reshape(NG, 2, HPG).transpose(1, 0, 2)
    g["a_log"] = (dA * A).reshape(2, 32)
    g["d_skip"] = (acc_s[0, 1] + acc_s[1, 1]).reshape(32, HEAD).sum(axis=1).reshape(1, 32)
    g["ssd_norm"] = acc_s[0, 0:1] + acc_s[1, 0:1]
    loss_lanes = acc_f[:, 2, :]
    return loss_lanes, grad_x, g


MESH = pl.DeviceIdType.MESH
ANY = pl.BlockSpec(memory_space=pl.ANY)


def _all_gather(shard):
    m_per, n = shard.shape

    def body(x_ref, out_ref, send_sems, recv_sems, local_sem):
        x, y, c = lax.axis_index("x"), lax.axis_index("y"), lax.axis_index("c")
        me, sibling = (x, y, c), (x, y, 1 - c)
        chips = [(1 - x, y), (x, 1 - y), (1 - x, 1 - y)]

        def rows(px, py, pc):
            return out_ref.at[pl.ds((4 * px + 2 * py + pc) * m_per, m_per), :]

        def copy(k, block, to, src=None):
            return pltpu.make_async_remote_copy(
                src_ref=rows(*block) if src is None else src, dst_ref=rows(*block),
                send_sem=send_sems.at[k], recv_sem=recv_sems.at[k], device_id=to, device_id_type=MESH)

        mine = pltpu.make_async_copy(x_ref, rows(*me), local_sem)
        mine.start()
        first = [copy(0, me, sibling, src=x_ref)]
        first += [copy(1 + j, me, (*chip, c), src=x_ref) for j, chip in enumerate(chips)]
        for cp in first:
            cp.start()
        passed = [copy(4 + j, (*chip, c), sibling) for j, chip in enumerate(chips)]
        for j, chip in enumerate(chips):
            copy(1 + j, (*chip, c), me).wait_recv()
            passed[j].start()
        copy(0, sibling, me).wait_recv()
        for j, chip in enumerate(chips):
            copy(4 + j, (*chip, 1 - c), me).wait_recv()
        for cp in first + passed:
            cp.wait_send()
        mine.wait()

    return pl.pallas_call(
        body, name="all_gather_weights",
        out_shape=jax.ShapeDtypeStruct((NDEV * m_per, n), shard.dtype),
        in_specs=[ANY], out_specs=ANY,
        scratch_shapes=[pltpu.SemaphoreType.DMA((7,)), pltpu.SemaphoreType.DMA((7,)), pltpu.SemaphoreType.DMA],
    )(shard)


def _xor_peer(k, x, y, c):
    return (1 - x if k & 4 else x, 1 - y if k & 2 else y, 1 - c if k & 1 else c)


def _pair_exchange(big, small):
    def body(big_ref, small_ref, own_ref, got_ref, osmall_ref, send_sems, recv_sems, local_sems):
        x, y, c = lax.axis_index("x"), lax.axis_index("y"), lax.axis_index("c")
        me = 4 * x + 2 * y + c
        mine = [pltpu.make_async_copy(big_ref.at[c], own_ref, local_sems.at[0]),
                pltpu.make_async_copy(small_ref, osmall_ref.at[me], local_sems.at[1])]
        for cp in mine:
            cp.start()

        def rc(src, dst, sem, peer):
            return pltpu.make_async_remote_copy(src_ref=src, dst_ref=dst, send_sem=send_sems.at[sem],
                                                recv_sem=recv_sems.at[sem], device_id=peer, device_id_type=MESH)

        sib = _xor_peer(1, x, y, c)
        sends = [rc(big_ref.at[1 - c], got_ref, 0, sib)]
        recvs = [rc(big_ref.at[1 - c], got_ref, 0, sib)]
        for k in range(1, NDEV):
            px, py, pc = _xor_peer(k, x, y, c)
            sends.append(rc(small_ref, osmall_ref.at[me], k, (px, py, pc)))
            recvs.append(rc(small_ref, osmall_ref.at[4 * px + 2 * py + pc], k, (px, py, pc)))
        for cp in sends:
            cp.start()
        for cp in sends:
            cp.wait_send()
        for cp in recvs:
            cp.wait_recv()
        for cp in mine:
            cp.wait()

    half = jax.ShapeDtypeStruct(big.shape[1:], big.dtype)
    return pl.pallas_call(
        body, name="grads_pair_exchange",
        out_shape=(half, half, jax.ShapeDtypeStruct((NDEV,) + small.shape, small.dtype)),
        in_specs=[ANY, ANY], out_specs=(ANY, ANY, ANY),
        scratch_shapes=[pltpu.SemaphoreType.DMA((8,)), pltpu.SemaphoreType.DMA((8,)), pltpu.SemaphoreType.DMA((2,))],
    )(big, small)


def _pair_add(own, got):
    _, rows, n = own.shape
    tile = rows // 4
    assert rows % 64 == 0

    def body(a_ref, b_ref, o_ref):
        o_ref[...] = (a_ref[...].astype(F32) + b_ref[...].astype(F32)).astype(BF)

    blk = pl.BlockSpec((1, tile, n), lambda q, i: (q, i, 0))
    return pl.pallas_call(
        body, name="grads_pair_add", grid=(4, rows // tile), in_specs=[blk, blk], out_specs=blk,
        out_shape=jax.ShapeDtypeStruct(own.shape, BF), compiler_params=_params(("parallel", "parallel")),
    )(own, got)


def _chip_exchange(pair):
    def body(in_ref, out_ref, send_sems, recv_sems, local_sem):
        x, y, c = lax.axis_index("x"), lax.axis_index("y"), lax.axis_index("c")
        q = 2 * x + y
        mine = pltpu.make_async_copy(in_ref.at[q], out_ref.at[q], local_sem)
        mine.start()
        sends, recvs = [], []
        for j in range(1, 4):
            px, py, pc = _xor_peer(2 * j, x, y, c)
            pq = 2 * px + py
            for lst, dst in ((sends, out_ref.at[q]), (recvs, out_ref.at[pq])):
                lst.append(pltpu.make_async_remote_copy(
                    src_ref=in_ref.at[pq], dst_ref=dst, send_sem=send_sems.at[j - 1], recv_sem=recv_sems.at[j - 1],
                    device_id=(px, py, pc), device_id_type=MESH))
        for cp in sends:
            cp.start()
        for cp in sends:
            cp.wait_send()
        for cp in recvs:
            cp.wait_recv()
        mine.wait()

    return pl.pallas_call(
        body, name="grads_chip_exchange",
        out_shape=jax.ShapeDtypeStruct(pair.shape, pair.dtype), in_specs=[ANY], out_specs=ANY,
        scratch_shapes=[pltpu.SemaphoreType.DMA((3,)), pltpu.SemaphoreType.DMA((3,)), pltpu.SemaphoreType.DMA],
    )(pair)


ADAM_TILE = 64
PACK_W = 1024


def _adamw(recv, w, m, v, name):
    rp = w.shape[0]
    tile = min(ADAM_TILE, rp)
    nsrc = recv.shape[0]

    def body(r_ref, w_ref, m_ref, v_ref, g_ref, d_ref, nm_ref, nv_ref):
        g = r_ref[0].astype(F32)
        for i in range(1, nsrc):
            g = g + r_ref[i].astype(F32)
        m1 = ADAM_B1 * m_ref[...] + (1.0 - ADAM_B1) * g
        v1 = ADAM_B2 * v_ref[...] + (1.0 - ADAM_B2) * (g * g)
        m_hat = m1 / (1.0 - ADAM_B1 ** ADAM_STEP)
        v_hat = v1 / (1.0 - ADAM_B2 ** ADAM_STEP)
        g_ref[...] = g
        d_ref[...] = -ADAM_LR * (m_hat / (jnp.sqrt(v_hat) + ADAM_EPS) + ADAM_WD * w_ref[...])
        nm_ref[...] = m1
        nv_ref[...] = v1

    blk = pl.BlockSpec((tile, PACK_W), lambda i: (i, 0))
    shp = jax.ShapeDtypeStruct((rp, PACK_W), F32)
    return pl.pallas_call(
        body, name=name, grid=(rp // tile,),
        in_specs=[pl.BlockSpec((nsrc, tile, PACK_W), lambda i: (0, i, 0)), blk, blk, blk],
        out_specs=(blk, blk, blk, blk), out_shape=(shp, shp, shp, shp),
        compiler_params=_params(("parallel",)),
    )(recv, w, m, v)


BIG = {"w_ada": ((3 * D, D), 0), "pool_w": ((4, PGW, PGW), 1), "w_proj_pool": ((D, D), 0), "w_proj_ssd": ((DIN, D), 0),
       "w_out": ((D, D), 0), "w_in": ((IN_COLS, D), 0), "conv_w": ((4, CONV_DIM), 1)}
TRANSPOSED = ("w_ada", "w_in")
PACK_ROWS = {"w_ada": 384, "pool_w": 32, "w_proj_pool": 128, "w_proj_ssd": 256, "w_out": 128, "conv_w": 16, "w_in": 1168}
SMALL = {"c_ctx": (D,), "b_ada": (1, 3 * D), "norm_pre": (1, D), "norm_post": (1, D), "b_merge": (1, 2 * D),
         "pool_scale": (1, D), "conv_b": (1, CONV_DIM), "dt_bias": (2, 32), "a_log": (2, 32), "d_skip": (1, 32),
         "ssd_norm": (1, DIN)}
LOSS_SLOT = 128
BIG_ROWS = sum(PACK_ROWS.values())
assert BIG_ROWS % ADAM_TILE == 0
SMALL_ROWS = 16


def _shard_shape(name):
    shape, ax = BIG[name]
    return tuple(s // NDEV if i == ax else s for i, s in enumerate(shape))


def _as_rows(t, rows):
    pad = [(0, 0)] * (t.ndim - 1) + [(0, rows * PACK_W - t.shape[-1])]
    return jnp.pad(t, pad).reshape(t.shape[:-1] + (rows, PACK_W))


def _shard_rows(t, name):
    sh, r = _shard_shape(name), PACK_ROWS[name]
    lead = t.shape[:t.ndim - len(sh)]
    if len(sh) == 2 and sh[1] == PACK_W:
        return jnp.pad(t, [(0, 0)] * len(lead) + [(0, r - sh[0]), (0, 0)])
    if int(np.prod(sh)) == r * PACK_W:
        return t.reshape(lead + (r, PACK_W))
    return _as_rows(t.reshape(lead + (-1,)), r)


def _to_chunks(full, name):
    shape, ax = BIG[name]
    split = shape[:ax] + (NDEV, shape[ax] // NDEV) + shape[ax + 1:]
    return _shard_rows(jnp.moveaxis(full.reshape(split), ax, 0), name)


def _from_chunks(chunks, name):
    shape, ax = BIG[name]
    return jnp.moveaxis(chunks.reshape((NDEV,) + _shard_shape(name)), 0, ax).reshape(shape)


def _pack_state(t):
    big = jnp.concatenate([_shard_rows(t[n], n) for n in PACK_ROWS], axis=0)
    small = _as_rows(jnp.concatenate([t[n].reshape(-1) for n in SMALL] + [jnp.zeros((LOSS_SLOT,), F32)]), SMALL_ROWS)
    return big, small


def _pack_grads(g, loss_part):
    big = jnp.concatenate([_to_chunks(g[n], n).astype(BF) for n in PACK_ROWS], axis=1)
    big = jnp.swapaxes(big.reshape(4, 2, BIG_ROWS, PACK_W), 0, 1)
    small = [g[n].reshape(-1) for n in SMALL] + [jnp.zeros((LOSS_SLOT,), F32).at[0].set(loss_part)]
    return big, _as_rows(jnp.concatenate(small), SMALL_ROWS)


def _unpack_state(big, small):
    out, off = {}, 0
    for n, r in PACK_ROWS.items():
        sh = _shard_shape(n)
        k = int(np.prod(sh))
        if len(sh) == 2 and sh[1] == PACK_W:
            out[n] = big[off:off + sh[0]]
        else:
            out[n] = big[off:off + r].reshape(-1)[:k].reshape(sh)
        off += r
    flat, off = small.reshape(-1), 0
    for n, sh in SMALL.items():
        k = int(np.prod(sh))
        out[n] = flat[off:off + k].reshape(sh)
        off += k
    out["loss"] = flat[off]
    return out


def _pack_gather(w):
    conv = jnp.concatenate([p.reshape(-1) for p in _split(w["conv_w"], 3)])
    return jnp.concatenate([_as_rows(conv, PACK_ROWS[n]) if n == "conv_w" else _shard_rows(w[n], n).astype(BF)
                            for n in PACK_ROWS], axis=0)


def _unpack_gather(gathered):
    g = gathered.reshape(NDEV, BIG_ROWS, PACK_W)
    out, off = {}, 0
    for n, r in PACK_ROWS.items():
        sh = _shard_shape(n)
        if n == "conv_w":
            k = int(np.prod(sh))
            terms = g[:, off:off + r].reshape(NDEV, -1)[:, :3 * k].astype(F32).reshape(NDEV, 3, k)
            out[n] = _from_chunks(terms[:, 0] + terms[:, 1] + terms[:, 2], n)
        elif len(sh) == 2 and sh[1] == PACK_W:
            out[n] = _from_chunks(g[:, off:off + sh[0]], n)
        else:
            out[n] = _from_chunks(g[:, off:off + r], n)
        off += r
    return out


PARAMS = ["c_ctx", "w_ada", "b_ada", "norm_pre", "norm_post", "w_in", "b_merge", "pool_w", "pool_scale", "conv_w", "conv_b",
          "dt_bias", "a_log", "d_skip", "ssd_norm", "w_proj_pool", "w_proj_ssd", "w_out"]


def kernel(x, c, ctx, c_ctx, w_ada, b_ada, norm_pre, norm_post, w_in, b_merge, pool_w, pool_scale, conv_w, conv_b, dt_bias, a_log, d_skip, ssd_norm, w_proj_pool, w_proj_ssd, w_out, loss_target, m_c_ctx, m_w_ada, m_b_ada, m_norm_pre, m_norm_post, m_w_in, m_b_merge, m_pool_w, m_pool_scale, m_conv_w, m_conv_b, m_dt_bias, m_a_log, m_d_skip, m_ssd_norm, m_w_proj_pool, m_w_proj_ssd, m_w_out, v_c_ctx, v_w_ada, v_b_ada, v_norm_pre, v_norm_post, v_w_in, v_b_merge, v_pool_w, v_pool_scale, v_conv_w, v_conv_b, v_dt_bias, v_a_log, v_d_skip, v_ssd_norm, v_w_proj_pool, v_w_proj_ssd, v_w_out):
    given = dict(locals())
    shapes = {n: given[n].shape for n in PARAMS}

    def local(prefix):
        t = {n: (given[prefix + n] if n == "c_ctx" else given[prefix + n][0]) for n in PARAMS}
        for n in TRANSPOSED:
            t[n] = t[n].T
        return {n: t[n].reshape(_shard_shape(n) if n in BIG else SMALL[n]) for n in PARAMS}

    w, m, v = local(""), local("m_"), local("v_")

    W = _unpack_gather(_all_gather(_pack_gather(w)))
    for n in SMALL:
        W[n] = w[n]
    lanes, grad_x, g = _local_step(x, c, ctx, loss_target, W)
    own, got, recv_small = _pair_exchange(*_pack_grads(g, (0.5 / D) * jnp.sum(lanes)))
    recv_big = _chip_exchange(_pair_add(own, got))
    (wb, ws), (mb, ms), (vb, vs) = _pack_state(w), _pack_state(m), _pack_state(v)
    res = [_unpack_state(b, s) for b, s in zip(_adamw(recv_big, wb, mb, vb, "adamw_big"),
                                               _adamw(recv_small, ws, ms, vs, "adamw_small"))]
    outs = [res[0]["loss"], grad_x]
    for r in res:
        for n in TRANSPOSED:
            r[n] = r[n].T
        outs += [r[n].reshape(shapes[n]) for n in PARAMS]
    return tuple(outs)
```

```python
import functools

import numpy as np
import jax
import jax.numpy as jnp
from jax import lax
from jax.experimental import pallas as pl
from jax.experimental.pallas import tpu as pltpu

F32, BF = jnp.float32, jnp.bfloat16

D = 1024
GRID_W = 64
EPS = 1e-6
POOL_WINDOWS = (2, 4, 8, 16)
PGW = 256
DIN = 2048
HEAD = 64
NST = 128
NG = 4
HPG = 8
GWID = HPG * HEAD
Q = 128
CONV_DIM = 3072
OFF_GATE, OFF_XBC, OFF_DT, IN_COLS = 4096, 6144, 9216, 9280
NDEV = 8
ADAM_LR, ADAM_B1, ADAM_B2, ADAM_EPS, ADAM_WD, ADAM_STEP = 0.001, 0.9, 0.999, 1e-08, 0.01, 10

V7X_VMEM_LIMIT = 56 * 2 ** 20
ROW_TILE = 256


def _params(sem=None):
    return pltpu.CompilerParams(dimension_semantics=sem, vmem_limit_bytes=V7X_VMEM_LIMIT)


def _dot(a, b):
    return jnp.dot(a.astype(BF), b.astype(BF), preferred_element_type=F32)


def _dot_nt(a, b):
    return lax.dot_general(a.astype(BF), b.astype(BF), (((1,), (1,)), ((), ())), preferred_element_type=F32)


def _dot_tn(a, b):
    return lax.dot_general(a.astype(BF), b.astype(BF), (((0,), (0,)), ((), ())), preferred_element_type=F32)


def _split(a, n):
    parts = []
    for _ in range(n):
        p = a.astype(BF)
        parts.append(p)
        a = a - p.astype(F32)
    return parts


def _dot_sl(a, b01, n=3):
    return sum(jnp.dot(p, b01, preferred_element_type=F32) for p in _split(a, n))


def _dot_sr(a01, b, n=3):
    return sum(jnp.dot(a01, p, preferred_element_type=F32) for p in _split(b, n))


def _dot_tn_sl(a, b01, n=2):
    return sum(lax.dot_general(p, b01, (((0,), (0,)), ((), ())), preferred_element_type=F32) for p in _split(a, n))


def _sigmoid(x):
    return 1.0 / (1.0 + jnp.exp(-x))


def _matmul(a, b, out_dtype, name, tm=512, tn=512, tk=1024, bt=False, n=None):
    M, K = a.shape
    N = n if n is not None else (b.shape[0] if bt else b.shape[1])
    tm, tn, tk = min(tm, M), min(tn, N), min(tk, K)
    assert M % tm == 0 and N % tn == 0 and K % tk == 0, (a.shape, b.shape)
    nk = K // tk

    def body(a_ref, b_ref, o_ref, acc):
        k = pl.program_id(2)
        p = _dot_nt(a_ref[...], b_ref[...]) if bt else _dot(a_ref[...], b_ref[...])

        @pl.when(k == 0)
        def _():
            acc[...] = p

        @pl.when(k > 0)
        def _():
            acc[...] += p

        @pl.when(k == nk - 1)
        def _():
            o_ref[...] = acc[...].astype(o_ref.dtype)

    return pl.pallas_call(
        body, name=name, grid=(M // tm, N // tn, nk),
        in_specs=[pl.BlockSpec((tm, tk), lambda i, j, k: (i, k)),
                  pl.BlockSpec((tn, tk), lambda i, j, k: (j, k)) if bt else pl.BlockSpec((tk, tn), lambda i, j, k: (k, j))],
        out_specs=pl.BlockSpec((tm, tn), lambda i, j, k: (i, j)),
        out_shape=jax.ShapeDtypeStruct((M, N), out_dtype),
        scratch_shapes=[pltpu.VMEM((tm, tn), F32)],
        compiler_params=_params(("parallel", "parallel", "arbitrary")),
    )(a, b)


def _matmul_tn(a, g, name, ta=512, tn=512, tr=512):
    M, Ka = a.shape
    N = g.shape[1]
    ta, tn, tr = min(ta, Ka), min(tn, N), min(tr, M)
    assert M % tr == 0 and N % tn == 0 and Ka % ta == 0, (a.shape, g.shape)
    nr = M // tr

    def body(a_ref, g_ref, o_ref):
        k = pl.program_id(2)
        p = _dot_tn(a_ref[...], g_ref[...])

        @pl.when(k == 0)
        def _():
            o_ref[...] = p

        @pl.when(k > 0)
        def _():
            o_ref[...] += p

    return pl.pallas_call(
        body, name=name, grid=(Ka // ta, N // tn, nr),
        in_specs=[pl.BlockSpec((tr, ta), lambda i, j, k: (k, i)), pl.BlockSpec((tr, tn), lambda i, j, k: (k, j))],
        out_specs=pl.BlockSpec((ta, tn), lambda i, j, k: (i, j)),
        out_shape=jax.ShapeDtypeStruct((Ka, N), F32),
        compiler_params=_params(("parallel", "parallel", "arbitrary")),
    )(a, g)


def _dhx(pieces, ddt, w_inT, w_dtT):
    _, R, _ = pieces[0].shape
    tm = R // 8
    kb = 1024
    starts, nblk = [], []
    for p in pieces:
        starts.append(sum(nblk))
        nblk.append(p.shape[2] // kb)
    nk = sum(nblk)
    assert nk * kb == OFF_DT and R % 128 == 0
    npc = len(pieces)

    def body(*refs):
        a_refs, dt_ref, w_ref, wdt_ref, o_ref, acc = refs[:npc], refs[npc], refs[npc + 1], refs[npc + 2], refs[npc + 3], refs[npc + 4]
        k = pl.program_id(2)

        @pl.when(k == 0)
        def _():
            acc[...] = _dot(dt_ref[0], wdt_ref[...])

        for p in range(npc):
            @pl.when((k >= starts[p]) & (k < starts[p] + nblk[p]))
            def _(p=p):
                acc[...] += _dot(a_refs[p][0], w_ref[...])

        @pl.when(k == nk - 1)
        def _():
            o_ref[0] = acc[...]

    in_specs = [pl.BlockSpec((1, tm, kb), functools.partial(
        lambda e, t, k, s, nb: (e, t, jnp.clip(k - s, 0, nb - 1)), s=starts[p], nb=nblk[p])) for p in range(npc)]
    in_specs += [pl.BlockSpec((1, tm, 128), lambda e, t, k: (e, t, 0)),
                 pl.BlockSpec((kb, D), lambda e, t, k: (k, 0)),
                 pl.BlockSpec((128, D), lambda e, t, k: (0, 0))]
    return pl.pallas_call(
        body, name="d_hx", grid=(2, R // tm, nk), in_specs=in_specs,
        out_specs=pl.BlockSpec((1, tm, D), lambda e, t, k: (e, t, 0)),
        out_shape=jax.ShapeDtypeStruct((2, R, D), F32),
        scratch_shapes=[pltpu.VMEM((tm, D), F32)],
        compiler_params=_params(("parallel", "parallel", "arbitrary")),
    )(*pieces, ddt, w_inT, w_dtT)


def _adaln_fwd(c16, w_adaT_bf, b_ada):
    def body(c_ref, w_ref, b_ref, o_ref):
        cc = c_ref[...]
        o_ref[...] = _dot_nt(cc * _sigmoid(cc), w_ref[...]) + b_ref[...]

    return pl.pallas_call(body, name="adaln_fwd", out_shape=jax.ShapeDtypeStruct((16, 3 * D), F32),
                          compiler_params=_params())(c16, w_adaT_bf, b_ada)


def _adaln_bwd(acc_n, acc_f, mod16, c16, norm_pre, w_adaT_bf):
    def body(an_ref, af_ref, mod_ref, c_ref, np_ref, wt_ref, dw_ref, db_ref, sm_ref, dmod):
        npre = np_ref[...]
        dmod[...] = jnp.zeros_like(dmod)
        dnp = jnp.zeros((1, D), F32)
        dshift_c = jnp.zeros((1, D), F32)
        dgpre_c = jnp.zeros((1, D), F32)
        scale_c = mod_ref[2:3, D:2 * D]
        for e in range(2):
            dg_x, ds_x = an_ref[e, 0, 0:1, :], an_ref[e, 0, 1:2, :]
            dg_c, ds_c = an_ref[e, 1, 0:1, :], an_ref[e, 1, 1:2, :]
            dmod[e:e + 1, 0:D] = ds_x
            dmod[e:e + 1, D:2 * D] = dg_x * npre
            dmod[e:e + 1, 2 * D:3 * D] = af_ref[e, 0:1, :]
            dnp = dnp + dg_x * (1.0 + mod_ref[e:e + 1, D:2 * D]) + dg_c * (1.0 + scale_c)
            dshift_c = dshift_c + ds_c
            dgpre_c = dgpre_c + dg_c
        dmod[2:3, 0:D] = dshift_c
        dmod[2:3, D:2 * D] = dgpre_c * npre
        dm = dmod[...]
        cc = c_ref[...]
        sg = _sigmoid(cc)
        dw_ref[...] = _dot_tn(dm, cc * sg)
        db_ref[...] = jnp.zeros_like(db_ref)
        db_ref[0:1, :] = jnp.sum(dm, axis=0, keepdims=True)
        dsilu = sg * (1.0 + cc * (1.0 - sg))
        dcs = _dot(dm, wt_ref[...]) * dsilu
        sm_ref[...] = jnp.zeros_like(sm_ref)
        sm_ref[0:1, :] = dnp
        sm_ref[1:2, :] = dcs[2:3, :]

    return pl.pallas_call(
        body, name="adaln_bwd",
        out_shape=(jax.ShapeDtypeStruct((3 * D, D), F32), jax.ShapeDtypeStruct((16, 3 * D), F32),
                   jax.ShapeDtypeStruct((8, D), F32)),
        scratch_shapes=[pltpu.VMEM((16, 3 * D), F32)],
        compiler_params=_params())(acc_n, acc_f, mod16, c16, norm_pre, w_adaT_bf)


def _row_specs(L):
    nx = L // ROW_TILE
    return (pl.BlockSpec((1, ROW_TILE, D), lambda e, t: (e, jnp.minimum(t, nx - 1), 0)),
            pl.BlockSpec((1, ROW_TILE, D), lambda e, t: (e, jnp.maximum(t - nx, 0), 0)))


def _norm_mod_fwd(x, ctx, tab):
    L = x.shape[1]
    R = L + ctx.shape[1]
    nx = L // ROW_TILE

    def body(x_ref, c_ref, t_ref, o_ref):
        x = jnp.where(pl.program_id(1) < nx, x_ref[0], c_ref[0])
        r = lax.rsqrt(jnp.mean(x * x, axis=-1, keepdims=True) + EPS)
        t = t_ref[0, 0]
        o_ref[0] = (x * r * t[0:1] + t[1:2]).astype(BF)

    return pl.pallas_call(
        body, name="norm_mod_fwd", grid=(2, R // ROW_TILE),
        in_specs=[*_row_specs(L), pl.BlockSpec((1, 1, 8, D), lambda e, t: (e, t // nx, 0, 0))],
        out_specs=pl.BlockSpec((1, ROW_TILE, D), lambda e, t: (e, t, 0)),
        out_shape=jax.ShapeDtypeStruct((2, R, D), BF),
        compiler_params=_params(("parallel", "parallel")),
    )(x, ctx, tab)


def _norm_mod_bwd(dh, x, ctx, tab, dxo):
    L = x.shape[1]
    R = L + ctx.shape[1]
    nx = L // ROW_TILE

    def body(dh_ref, x_ref, c_ref, t_ref, dxo_ref, gx_ref, acc_ref):
        t = pl.program_id(1)
        x = jnp.where(t < nx, x_ref[0], c_ref[0])
        r = lax.rsqrt(jnp.mean(x * x, axis=-1, keepdims=True) + EPS)
        xn = x * r
        dh = dh_ref[0]

        @pl.when((t == 0) | (t == nx))
        def _():
            acc_ref[...] = jnp.zeros_like(acc_ref)

        acc_ref[0, 0, 0:1, :] += jnp.sum(dh * xn, axis=0, keepdims=True)
        acc_ref[0, 0, 1:2, :] += jnp.sum(dh, axis=0, keepdims=True)

        @pl.when(t < nx)
        def _():
            dxn = dh * t_ref[0, 0][0:1]
            dx = r * (dxn - xn * jnp.mean(dxn * xn, axis=-1, keepdims=True))
            gx_ref[0] = dxo_ref[0] + dx

    xspec, cspec = _row_specs(L)
    return pl.pallas_call(
        body, name="norm_mod_bwd", grid=(2, R // ROW_TILE),
        in_specs=[pl.BlockSpec((1, ROW_TILE, D), lambda e, t: (e, t, 0)), xspec, cspec,
                  pl.BlockSpec((1, 1, 8, D), lambda e, t: (e, t // nx, 0, 0)), xspec],
        out_specs=(xspec, pl.BlockSpec((1, 1, 8, D), lambda e, t: (e, t // nx, 0, 0))),
        out_shape=(jax.ShapeDtypeStruct((2, L, D), F32), jax.ShapeDtypeStruct((2, 2, 8, D), F32)),
        compiler_params=_params(("parallel", "arbitrary")),
    )(dh, x, ctx, tab, dxo)


POOL_TILE = 256


def _pool_tables(L):
    rows = L // GRID_W
    mats = np.zeros((4, POOL_TILE, POOL_TILE), np.float32)
    inv = np.zeros((4, L, 1), np.float32)
    for gi, k in enumerate(POOL_WINDOWS):
        lo, hi = k // 2, k - 1 - k // 2
        m = np.zeros((GRID_W, GRID_W), np.float32)
        for t in range(GRID_W):
            m[t, max(t - lo, 0):min(t + hi, GRID_W - 1) + 1] = 1.0
        for b in range(POOL_TILE // GRID_W):
            mats[gi, b * GRID_W:(b + 1) * GRID_W, b * GRID_W:(b + 1) * GRID_W] = m
        cnt_c = m.sum(1)
        cnt_r = np.array([min(r + hi, rows - 1) - max(r - lo, 0) + 1 for r in range(rows)], np.float32)
        inv[gi, :, 0] = (1.0 / (cnt_r[:, None] * cnt_c[None, :])).reshape(-1)
    matsT = np.ascontiguousarray(np.transpose(mats, (0, 2, 1)))
    return (jnp.asarray(mats, BF), jnp.asarray(matsT, BF), jnp.asarray(inv))


def _pool_cols(get_tile, mat, cs_ref, L, n):
    def step(i, carry):
        off = pl.multiple_of(i * POOL_TILE, POOL_TILE)
        cs_ref[pl.ds(GRID_W + off, POOL_TILE), :] = _dot_sr(mat, get_tile(off).astype(F32), n)
        return carry

    lax.fori_loop(0, L // POOL_TILE, step, 0)
    cs_ref[pl.ds(0, GRID_W), :] = jnp.zeros((GRID_W, PGW), F32)

    def prefix(r, carry):
        o = pl.multiple_of(r * GRID_W, GRID_W)
        cs_ref[pl.ds(o + GRID_W, GRID_W), :] = cs_ref[pl.ds(o + GRID_W, GRID_W), :] + cs_ref[pl.ds(o, GRID_W), :]
        return carry

    lax.fori_loop(0, L // GRID_W, prefix, 0)


def _pool_rows(cs_ref, off, below, above, L):
    rows = L // GRID_W
    r0 = off // GRID_W
    parts = []
    for i in range(POOL_TILE // GRID_W):
        hi = pl.multiple_of(jnp.minimum(r0 + i + above + 1, rows) * GRID_W, GRID_W)
        lo = pl.multiple_of(jnp.maximum(r0 + i - below, 0) * GRID_W, GRID_W)
        parts.append(cs_ref[pl.ds(hi, GRID_W), :] - cs_ref[pl.ds(lo, GRID_W), :])
    return jnp.concatenate(parts, axis=0)


def _pool_fwd(proj3, pool_w_bf, pool_scale, tables, L):
    mats, _, inv = tables
    nt = L // POOL_TILE

    def body(v_ref, z_ref, pw_ref, ps_ref, m_ref, inv_ref, o_ref, cs_ref):
        _pool_cols(lambda off: v_ref[0, pl.ds(off, POOL_TILE), :], m_ref[0], cs_ref, L, 1)
        half = lax.shift_left(1, pl.program_id(1))

        def step(i, carry):
            off = pl.multiple_of(i * POOL_TILE, POOL_TILE)
            rows = pl.ds(off, POOL_TILE)
            v = v_ref[0, rows, :].astype(F32)
            diff = _pool_rows(cs_ref, off, half, half - 1, L) * inv_ref[0, rows, :] - v
            yp = _dot(diff, pw_ref[0])
            z = z_ref[0, rows, :].astype(F32)
            o_ref[0, rows, :] = (yp * ps_ref[...] * (z * _sigmoid(z))).astype(BF)
            return carry

        lax.fori_loop(0, nt, step, 0)

    return pl.pallas_call(
        body, name="pool_fwd", grid=(2, 4),
        in_specs=[pl.BlockSpec((1, L, PGW), lambda e, g: (e, 0, g)),
                  pl.BlockSpec((1, L, PGW), lambda e, g: (e, 0, 4 + g)),
                  pl.BlockSpec((1, PGW, PGW), lambda e, g: (g, 0, 0)),
                  pl.BlockSpec((1, PGW), lambda e, g: (0, g)),
                  pl.BlockSpec((1, POOL_TILE, POOL_TILE), lambda e, g: (g, 0, 0)),
                  pl.BlockSpec((1, L, 1), lambda e, g: (g, 0, 0))],
        out_specs=pl.BlockSpec((1, L, PGW), lambda e, g: (e, 0, g)),
        out_shape=jax.ShapeDtypeStruct((2, L, D), BF),
        scratch_shapes=[pltpu.VMEM((L + GRID_W, PGW), F32)],
        compiler_params=_params(("parallel", "parallel")),
    )(proj3, proj3, pool_w_bf, pool_scale, mats, inv)


def _pool_bwd(proj3, d_ypool, pool_w_bf, pool_wT_bf, pool_scale, tables, L):
    mats, matsT, inv = tables
    nt = L // POOL_TILE
    R = proj3.shape[1]

    def body(v_ref, z_ref, dy_ref, pw_ref, pwt_ref, ps_ref, m_ref, mt_ref, inv_ref,
             dv_ref, dz_ref, dpw_ref, acc_ref, cs_ref, dd_ref):
        e = pl.program_id(1)

        @pl.when(e == 0)
        def _():
            dpw_ref[...] = jnp.zeros_like(dpw_ref)
            acc_ref[...] = jnp.zeros_like(acc_ref)

        _pool_cols(lambda off: v_ref[0, pl.ds(off, POOL_TILE), :], m_ref[0], cs_ref, L, 1)
        half = lax.shift_left(1, pl.program_id(0))
        ps = ps_ref[...]

        def step(i, carry):
            off = pl.multiple_of(i * POOL_TILE, POOL_TILE)
            rows = pl.ds(off, POOL_TILE)
            v = v_ref[0, rows, :].astype(F32)
            diff = _pool_rows(cs_ref, off, half, half - 1, L) * inv_ref[0, rows, :] - v
            yp = _dot(diff, pw_ref[0])
            z = z_ref[0, rows, :].astype(F32)
            sg = _sigmoid(z)
            sz = z * sg
            dy = dy_ref[0, rows, :]
            dz_ref[0, rows, :] = (dy * yp * ps * (sg * (1.0 + z * (1.0 - sg)))).astype(BF)
            dys = dy * sz
            acc_ref[0, 0:1, :] += jnp.sum(dys * yp, axis=0, keepdims=True)
            dyp = dys * ps
            dpw_ref[0] += _dot_tn(diff, dyp)
            dd_ref[rows, :] = _dot(dyp, pwt_ref[0])
            return carry

        lax.fori_loop(0, nt, step, 0)
        _pool_cols(lambda off: dd_ref[pl.ds(off, POOL_TILE), :] * inv_ref[0, pl.ds(off, POOL_TILE), :],
                   mt_ref[0], cs_ref, L, 2)

        def step2(i, carry):
            off = pl.multiple_of(i * POOL_TILE, POOL_TILE)
            rows = pl.ds(off, POOL_TILE)
            dv_ref[0, rows, :] = (_pool_rows(cs_ref, off, half - 1, half, L) - dd_ref[rows, :]).astype(BF)
            return carry

        lax.fori_loop(0, nt, step2, 0)
        dv_ref[0, pl.ds(L, R - L), :] = jnp.zeros((R - L, PGW), BF)
        dz_ref[0, pl.ds(L, R - L), :] = jnp.zeros((R - L, PGW), BF)

    return pl.pallas_call(
        body, name="pool_bwd", grid=(4, 2),
        in_specs=[pl.BlockSpec((1, L, PGW), lambda g, e: (e, 0, g)),
                  pl.BlockSpec((1, L, PGW), lambda g, e: (e, 0, 4 + g)),
                  pl.BlockSpec((1, L, PGW), lambda g, e: (e, 0, g)),
                  pl.BlockSpec((1, PGW, PGW), lambda g, e: (g, 0, 0)),
                  pl.BlockSpec((1, PGW, PGW), lambda g, e: (g, 0, 0)),
                  pl.BlockSpec((1, PGW), lambda g, e: (0, g)),
                  pl.BlockSpec((1, POOL_TILE, POOL_TILE), lambda g, e: (g, 0, 0)),
                  pl.BlockSpec((1, POOL_TILE, POOL_TILE), lambda g, e: (g, 0, 0)),
                  pl.BlockSpec((1, L, 1), lambda g, e: (g, 0, 0))],
        out_specs=(pl.BlockSpec((1, R, PGW), lambda g, e: (e, 0, g)),
                   pl.BlockSpec((1, R, PGW), lambda g, e: (e, 0, g)),
                   pl.BlockSpec((1, PGW, PGW), lambda g, e: (g, 0, 0)),
                   pl.BlockSpec((1, 8, PGW), lambda g, e: (g, 0, 0))),
        out_shape=(jax.ShapeDtypeStruct((2, R, D), BF), jax.ShapeDtypeStruct((2, R, D), BF),
                   jax.ShapeDtypeStruct((4, PGW, PGW), F32), jax.ShapeDtypeStruct((4, 8, PGW), F32)),
        scratch_shapes=[pltpu.VMEM((L + GRID_W, PGW), F32), pltpu.VMEM((L, PGW), F32)],
        compiler_params=_params(("parallel", "arbitrary")),
    )(proj3, proj3, d_ypool, pool_w_bf, pool_wT_bf, pool_scale, mats, matsT, inv)


CONV_BLOCK = 128


def _conv_tap(u, k, L):
    off = k - 2
    if off == 0:
        return u
    R = u.shape[0]
    r = lax.broadcasted_iota(jnp.int32, (R, 1), 0)
    pos = jnp.where(r < L, r, r - L) + off
    seg = jnp.where(r < L, L, R - L)
    return jnp.where((pos >= 0) & (pos < seg), pltpu.roll(u, (-off) % R, 0), 0.0)


def _conv_fwd(proj3, conv_w, conv_b, L):
    _, R, _ = proj3.shape
    cb0 = OFF_XBC // CONV_BLOCK

    def body(u_ref, w_ref, b_ref, o_ref):
        u = u_ref[0].astype(F32)
        w = w_ref[...]
        pre = b_ref[...] + sum(_conv_tap(u, k, L) * w[k:k + 1, :] for k in range(4))
        o_ref[0] = (pre * _sigmoid(pre)).astype(BF)

    return pl.pallas_call(
        body, name="conv_fwd", grid=(2, CONV_DIM // CONV_BLOCK),
        in_specs=[pl.BlockSpec((1, R, CONV_BLOCK), lambda e, j: (e, 0, cb0 + j)),
                  pl.BlockSpec((4, CONV_BLOCK), lambda e, j: (0, j)),
                  pl.BlockSpec((1, CONV_BLOCK), lambda e, j: (0, j))],
        out_specs=pl.BlockSpec((1, R, CONV_BLOCK), lambda e, j: (e, 0, j)),
        out_shape=jax.ShapeDtypeStruct((2, R, CONV_DIM), BF),
        compiler_params=_params(("parallel", "parallel")),
    )(proj3, conv_w, conv_b)


def _conv_bwd(proj3, addends, scales, col0, ncols, in_maps, conv_w, conv_b, L, name):
    _, R, _ = proj3.shape
    cb0 = (OFF_XBC + col0) // CONV_BLOCK
    wb0 = col0 // CONV_BLOCK
    na = len(addends)
    scaled = [i for i in range(na) if scales[i] is not None]

    def body(*refs):
        u_ref, w_ref, b_ref = refs[0], refs[1], refs[2]
        a_refs = refs[3:3 + na]
        s_refs = dict(zip(scaled, refs[3 + na:3 + na + len(scaled)]))
        o_ref, acc_ref = refs[3 + na + len(scaled)], refs[4 + na + len(scaled)]
        u = u_ref[0].astype(F32)
        w = w_ref[...]
        taps = [_conv_tap(u, k, L) for k in range(4)]
        pre = b_ref[...] + sum(taps[k] * w[k:k + 1, :] for k in range(4))
        sg = _sigmoid(pre)
        dxbc = jnp.zeros(u.shape, F32)
        for i, a in enumerate(a_refs):
            t = a[0].astype(F32)
            dxbc = dxbc + (t * s_refs[i][...] if i in s_refs else t)
        dpre = dxbc * (sg * (1.0 + pre * (1.0 - sg)))
        acc_ref[...] = jnp.zeros_like(acc_ref)
        for k in range(4):
            acc_ref[0, k:k + 1, :] = jnp.sum(dpre * taps[k], axis=0, keepdims=True)
        acc_ref[0, 4:5, :] = jnp.sum(dpre, axis=0, keepdims=True)
        du = sum(_conv_tap(dpre, 4 - k, L) * w[k:k + 1, :] for k in range(4))
        o_ref[0] = du.astype(BF)

    in_specs = [pl.BlockSpec((1, R, CONV_BLOCK), lambda e, j: (e, 0, cb0 + j)),
                pl.BlockSpec((4, CONV_BLOCK), lambda e, j: (0, wb0 + j)),
                pl.BlockSpec((1, CONV_BLOCK), lambda e, j: (0, wb0 + j))]
    for m in in_maps:
        in_specs.append(pl.BlockSpec((1, R, CONV_BLOCK), functools.partial(lambda e, j, m: (e, 0, m(j)), m=m)))
    for i in scaled:
        in_specs.append(pl.BlockSpec((1, CONV_BLOCK), functools.partial(lambda e, j, m: (0, m(j)), m=in_maps[i])))
    return pl.pallas_call(
        body, name=name, grid=(2, ncols // CONV_BLOCK),
        in_specs=in_specs,
        out_specs=(pl.BlockSpec((1, R, CONV_BLOCK), lambda e, j: (e, 0, j)),
                   pl.BlockSpec((1, 8, CONV_BLOCK), lambda e, j: (e, 0, j))),
        out_shape=(jax.ShapeDtypeStruct((2, R, ncols), BF), jax.ShapeDtypeStruct((2, 8, ncols), F32)),
        compiler_params=_params(("parallel", "parallel")),
    )(proj3, conv_w, conv_b, *addends, *[scales[i] for i in scaled])


def _softplus(x):
    e = jnp.exp(-jnp.abs(x))
    u = 1.0 + e
    return jnp.maximum(x, 0.0) + jnp.where(u == 1.0, e, e * jnp.log(u) / (u - 1.0))


def _dt_fwd(dt_raw, bias128):
    _, R, _ = dt_raw.shape

    def body(x_ref, b_ref, o_ref):
        o_ref[0] = _softplus(x_ref[0] + b_ref[...])

    return pl.pallas_call(
        body, name="dt_fwd", grid=(2,),
        in_specs=[pl.BlockSpec((1, R, 128), lambda e: (e, 0, 0)), pl.BlockSpec((1, 128), lambda e: (0, 0))],
        out_specs=pl.BlockSpec((1, R, 128), lambda e: (e, 0, 0)),
        out_shape=jax.ShapeDtypeStruct(dt_raw.shape, F32),
        compiler_params=_params(("parallel",)),
    )(dt_raw, bias128)


def _dt_bwd(dt_raw, bias128, ddt_f, ddt_b):
    _, R, _ = dt_raw.shape

    def body(x_ref, b_ref, f_ref, g_ref, o_ref, acc_ref):
        d = (f_ref[0] + g_ref[0]) * _sigmoid(x_ref[0] + b_ref[...])
        o_ref[0] = d.astype(BF)
        acc_ref[...] = jnp.zeros_like(acc_ref)
        acc_ref[0, 0:1, :] = jnp.sum(d, axis=0, keepdims=True)

    blk = pl.BlockSpec((1, R, 128), lambda e: (e, 0, 0))
    return pl.pallas_call(
        body, name="dt_bwd", grid=(2,),
        in_specs=[blk, pl.BlockSpec((1, 128), lambda e: (0, 0)), blk, blk],
        out_specs=(blk, pl.BlockSpec((1, 8, 128), lambda e: (e, 0, 0))),
        out_shape=(jax.ShapeDtypeStruct(dt_raw.shape, BF), jax.ShapeDtypeStruct((2, 8, 128), F32)),
        compiler_params=_params(("parallel",)),
    )(dt_raw, bias128, ddt_f, ddt_b)


def _tri(d):
    i = lax.broadcasted_iota(jnp.int32, (Q, Q), 0)
    j = lax.broadcasted_iota(jnp.int32, (Q, Q), 1)
    return (i >= j) if d == 0 else (i <= j)


def _expand_mat(d):
    r = lax.broadcasted_iota(jnp.int32, (128, GWID), 0)
    c = lax.broadcasted_iota(jnp.int32, (128, GWID), 1)
    return (r == d * HPG + jnp.right_shift(c, 6)).astype(BF)


def _reduce_mat(d):
    r = lax.broadcasted_iota(jnp.int32, (GWID, 128), 0)
    c = lax.broadcasted_iota(jnp.int32, (GWID, 128), 1)
    return (c == d * HPG + jnp.right_shift(r, 6)).astype(BF)


def _ssd_chunk(d, dt, A, xs, B, C):
    mask = _tri(d)
    T = mask.astype(BF)
    Tt = _tri(1 - d).astype(BF)
    a = dt * A
    acs = _dot_sr(T, a)
    E = _expand_mat(d)
    dt_e = _dot_sl(dt, E, 2)
    acs_e = _dot_sl(acs, E, 2)
    alast_e = acs_e[Q - 1:Q, :] if d == 0 else acs_e[0:1, :]
    return dict(mask=mask, T=T, Tt=Tt, acs=acs, acsT=acs.T, dt_e=dt_e, acs_e=acs_e, lam=jnp.exp(acs_e),
                w=jnp.exp(alast_e - acs_e), decay=jnp.exp(alast_e), xt=xs * dt_e, CB=_dot_nt(C, B))


def _head_decay(q, d, hh):
    col = q["acs"][:, d * HPG + hh:d * HPG + hh + 1]
    row = q["acsT"][d * HPG + hh:d * HPG + hh + 1, :]
    return jnp.exp(jnp.where(q["mask"], col - row, -jnp.inf))


def _chunk_maps(NX, NS):
    cf = lambda s: lax.rem(s + NX, NS)
    cb = lambda s: NS - 1 - s
    return cf, cb


def _ssd_fwd(xbc, dt_loc, a_loc, L):
    _, R, _ = xbc.shape
    NX, NS = L // Q, R // Q
    cf, cb = _chunk_maps(NX, NS)

    def body(xs_f, b_f, c_f, dt_f, xs_b, b_b, c_b, dt_b, a_ref, y_f, hs_f, y_b, hs_b, hT):
        @pl.when(pl.program_id(2) == 0)
        def _():
            hT[...] = jnp.zeros_like(hT)

        A = a_ref[0, 0:1, :]
        lane = lax.broadcasted_iota(jnp.int32, (Q, 128), 1)
        for d, (xs_ref, b_ref, c_ref, dt_ref, y_ref, hs_ref) in enumerate(
                ((xs_f, b_f, c_f, dt_f, y_f, hs_f), (xs_b, b_b, c_b, dt_b, y_b, hs_b))):
            xs, B, C = xs_ref[0].astype(F32), b_ref[0], c_ref[0]
            q = _ssd_chunk(d, dt_ref[0, 0], A, xs, B, C)
            h = hT[d]
            hb = h.astype(BF)
            hs_ref[0, 0] = hb
            parts = []
            for pr in range(HPG // 2):
                xp = q["xt"][:, pr * 128:(pr + 1) * 128].astype(BF)
                r0 = _dot(q["CB"] * _head_decay(q, d, 2 * pr), xp)
                r1 = _dot(q["CB"] * _head_decay(q, d, 2 * pr + 1), xp)
                parts.append(jnp.where(lane < HEAD, r0, r1))
            y_ref[0] = jnp.concatenate(parts, axis=1) + _dot(C, hb) * q["lam"]
            hT[d] = q["decay"] * h + _dot_tn(B, q["xt"] * q["w"])

    def spec(shape, imap):
        return pl.BlockSpec(shape, imap)

    def ins(c):
        return [spec((1, Q, GWID), lambda e, g, s: (e, c(s), g)),
                spec((1, Q, NST), lambda e, g, s: (e, c(s), DIN // NST + g)),
                spec((1, Q, NST), lambda e, g, s: (e, c(s), DIN // NST + NG + g)),
                spec((1, 1, Q, 128), lambda e, g, s: (e, g, c(s), 0))]

    def outs(c):
        return [spec((1, Q, GWID), lambda e, g, s: (e, c(s), g)),
                spec((1, 1, NST, GWID), lambda e, g, s: (e, c(s), 0, g))]

    yshape = jax.ShapeDtypeStruct((2, R, DIN), F32)
    hshape = jax.ShapeDtypeStruct((2, NS, NST, DIN), BF)
    return pl.pallas_call(
        body, name="ssd_fwd", grid=(2, NG, NS),
        in_specs=ins(cf) + ins(cb) + [spec((1, 8, 128), lambda e, g, s: (g, 0, 0))],
        out_specs=tuple(outs(cf) + outs(cb)),
        out_shape=(yshape, hshape, yshape, hshape),
        scratch_shapes=[pltpu.VMEM((2, NST, GWID), F32)],
        compiler_params=_params(("parallel", "parallel", "arbitrary")),
    )(xbc, xbc, xbc, dt_loc, xbc, xbc, xbc, dt_loc, a_loc)


def _ssd_bwd(xbc, dt_loc, a_loc, hs_f, hs_b, y_f, y_b, dy, L):
    _, R, _ = xbc.shape
    NX, NS = L // Q, R // Q
    cf0, cb0 = _chunk_maps(NX, NS)
    cf = lambda sp: cf0(NS - 1 - sp)
    cb = lambda sp: cb0(NS - 1 - sp)

    def body(xs_f, b_f, c_f, dt_f, hs_f_, dy_f, y_f_, xs_b, b_b, c_b, dt_b, hs_b_, dy_b, y_b_, a_ref,
             dxs_f, dbc_f, ddt_f, dxs_b, dbc_b, ddt_b, da_ref, dhT):
        @pl.when(pl.program_id(2) == 0)
        def _():
            dhT[...] = jnp.zeros_like(dhT)
            da_ref[...] = jnp.zeros_like(da_ref)

        A = a_ref[0, 0:1, :]
        lane = lax.broadcasted_iota(jnp.int32, (Q, 128), 1)
        row = lax.broadcasted_iota(jnp.int32, (Q, 128), 0)
        for d, (xs_ref, b_ref, c_ref, dt_ref, hs_ref, dy_ref, y_ref, dxs_ref, dbc_ref, ddt_ref) in enumerate(
                ((xs_f, b_f, c_f, dt_f, hs_f_, dy_f, y_f_, dxs_f, dbc_f, ddt_f),
                 (xs_b, b_b, c_b, dt_b, hs_b_, dy_b, y_b_, dxs_b, dbc_b, ddt_b))):
            xs, B, C, dt = xs_ref[0].astype(F32), b_ref[0], c_ref[0], dt_ref[0, 0]
            q = _ssd_chunk(d, dt, A, xs, B, C)
            xt, lam, w, decay = q["xt"], q["lam"], q["w"], q["decay"]
            H = hs_ref[0, 0]
            dyv = dy_ref[0].astype(F32)
            dh = dhT[d]
            dZ = dyv * lam
            dC = _dot_nt(dZ, H)
            dH = _dot_tn(C, dZ)
            U = _dot(B, dh)
            xw = xt * w
            dxt = U * w
            dalast_e = (jnp.sum(U * xw, axis=0, keepdims=True)
                        + decay * jnp.sum(dh * H.astype(F32), axis=0, keepdims=True))
            dB = _dot_nt(xw, dh)
            dCB = jnp.zeros((Q, Q), F32)
            dxt_parts = []
            for pr in range(HPG // 2):
                xp = xt[:, pr * 128:(pr + 1) * 128]
                dyp = dyv[:, pr * 128:(pr + 1) * 128]
                dxp = jnp.zeros((Q, 128), F32)
                for h2 in range(2):
                    Lh = _head_decay(q, d, 2 * pr + h2)
                    dym = jnp.where((lane < HEAD) if h2 == 0 else (lane >= HEAD), dyp, 0.0)
                    dxp = dxp + _dot_tn(q["CB"] * Lh, dym)
                    dCB = dCB + _dot_nt(dym, xp) * Lh
                dxt_parts.append(dxp)
            dxt_diag = jnp.concatenate(dxt_parts, axis=1)
            dC = dC + _dot(dCB, B)
            dB = dB + _dot_tn(dCB, C)
            Rm = _reduce_mat(d)
            dacs = _dot_sl(dyv * y_ref[0] - xt.astype(BF).astype(F32) * dxt_diag - U * xw, Rm, 2)
            dxt = dxt + dxt_diag
            dal = _dot_sl(jnp.broadcast_to(dalast_e, (8, GWID)), Rm, 2)[0:1, :]
            dacs = dacs + jnp.where(row == (Q - 1 if d == 0 else 0), dal, 0.0)
            da = _dot_sr(q["Tt"], dacs, 2)
            ddt_ref[0, 0] = da * A + _dot_sl(dxt * xs, Rm, 2)
            da_ref[0, 0, 0:1, :] += jnp.sum(da * dt, axis=0, keepdims=True)
            dxs_ref[0] = (dxt * q["dt_e"]).astype(BF)
            dbc_ref[0] = jnp.concatenate([dB, dC], axis=1).astype(BF)
            dhT[d] = decay * dh + dH

    def spec(shape, imap):
        return pl.BlockSpec(shape, imap)

    def ins(c):
        return [spec((1, Q, GWID), lambda e, g, s: (e, c(s), g)),
                spec((1, Q, NST), lambda e, g, s: (e, c(s), DIN // NST + g)),
                spec((1, Q, NST), lambda e, g, s: (e, c(s), DIN // NST + NG + g)),
                spec((1, 1, Q, 128), lambda e, g, s: (e, g, c(s), 0)),
                spec((1, 1, NST, GWID), lambda e, g, s: (e, c(s), 0, g)),
                spec((1, Q, GWID), lambda e, g, s: (e, c(s), g)),
                spec((1, Q, GWID), lambda e, g, s: (e, c(s), g))]

    def outs(c):
        return [spec((1, Q, GWID), lambda e, g, s: (e, c(s), g)),
                spec((1, Q, 2 * NST), lambda e, g, s: (e, c(s), g)),
                spec((1, 1, Q, 128), lambda e, g, s: (e, g, c(s), 0))]

    s_xs = jax.ShapeDtypeStruct((2, R, DIN), BF)
    s_bc = jax.ShapeDtypeStruct((2, R, 2 * NG * NST), BF)
    s_dt = jax.ShapeDtypeStruct((2, NG, R, 128), F32)
    return pl.pallas_call(
        body, name="ssd_bwd", grid=(2, NG, NS),
        in_specs=ins(cf) + ins(cb) + [spec((1, 8, 128), lambda e, g, s: (g, 0, 0))],
        out_specs=tuple(outs(cf) + outs(cb) + [spec((1, 1, 8, 128), lambda e, g, s: (e, g, 0, 0))]),
        out_shape=(s_xs, s_bc, s_dt, s_xs, s_bc, s_dt, jax.ShapeDtypeStruct((2, NG, 8, 128), F32)),
        scratch_shapes=[pltpu.VMEM((2, NST, GWID), F32)],
        compiler_params=_params(("parallel", "parallel", "arbitrary")),
    )(xbc, xbc, xbc, dt_loc, hs_f, dy, y_f, xbc, xbc, xbc, dt_loc, hs_b, dy, y_b, a_loc)


def _ssd_post_fwd(y_f, y_b, xbc, proj3, dskip_e, ssd_norm, L):
    def body(yf_ref, yb_ref, xs_ref, z_ref, ds_ref, w_ref, o_ref):
        y2 = yf_ref[0] + yb_ref[0] + ds_ref[...] * xs_ref[0].astype(F32)
        z = z_ref[0].astype(F32)
        u = y2 * (z * _sigmoid(z))
        parts = []
        for g in range(NG):
            ug = u[:, g * GWID:(g + 1) * GWID]
            parts.append(ug * lax.rsqrt(jnp.mean(ug * ug, axis=-1, keepdims=True) + EPS))
        o_ref[0] = (jnp.concatenate(parts, axis=1) * w_ref[...]).astype(BF)

    blk = lambda c: pl.BlockSpec((1, ROW_TILE, DIN), lambda e, t: (e, t, c))
    vec = pl.BlockSpec((1, DIN), lambda e, t: (0, 0))
    return pl.pallas_call(
        body, name="ssd_post_fwd", grid=(2, L // ROW_TILE),
        in_specs=[blk(0), blk(0), blk(0), blk(1), vec, vec],
        out_specs=blk(0),
        out_shape=jax.ShapeDtypeStruct((2, L, DIN), BF),
        compiler_params=_params(("parallel", "parallel")),
    )(y_f, y_b, xbc, proj3, dskip_e, ssd_norm)


def _ssd_post_bwd(d_yn, y_f, y_b, xbc, proj3, dskip_e, ssd_norm, L):
    _, R, _ = y_f.shape
    nx = L // ROW_TILE

    def body(dyn_ref, yf_ref, yb_ref, xs_ref, z_ref, ds_ref, w_ref, dy_ref, dz_ref, acc_ref):
        t = pl.program_id(1)

        @pl.when(t == 0)
        def _():
            acc_ref[...] = jnp.zeros_like(acc_ref)

        @pl.when(t >= nx)
        def _():
            dy_ref[...] = jnp.zeros_like(dy_ref)
            dz_ref[...] = jnp.zeros_like(dz_ref)

        @pl.when(t < nx)
        def _():
            xs = xs_ref[0].astype(F32)
            y2 = yf_ref[0] + yb_ref[0] + ds_ref[...] * xs
            z = z_ref[0].astype(F32)
            sg = _sigmoid(z)
            sz = z * sg
            u = y2 * sz
            dyn = dyn_ref[0]
            dun = dyn * w_ref[...]
            uh_parts, du_parts = [], []
            for g in range(NG):
                sl = slice(g * GWID, (g + 1) * GWID)
                ug = u[:, sl]
                rg = lax.rsqrt(jnp.mean(ug * ug, axis=-1, keepdims=True) + EPS)
                uh = ug * rg
                dg = dun[:, sl]
                du_parts.append(rg * (dg - uh * jnp.mean(dg * uh, axis=-1, keepdims=True)))
                uh_parts.append(uh)
            du = jnp.concatenate(du_parts, axis=1)
            uh = jnp.concatenate(uh_parts, axis=1)
            dy2 = du * sz
            dy_ref[0] = dy2.astype(BF)
            dz_ref[0] = (du * y2 * (sg * (1.0 + z * (1.0 - sg)))).astype(BF)
            acc_ref[0, 0:1, :] += jnp.sum(dyn * uh, axis=0, keepdims=True)
            acc_ref[0, 1:2, :] += jnp.sum(dy2 * xs, axis=0, keepdims=True)

    xmap = lambda c: (lambda e, t: (e, jnp.minimum(t, nx - 1), c))
    blk = lambda c: pl.BlockSpec((1, ROW_TILE, DIN), xmap(c))
    oblk = pl.BlockSpec((1, ROW_TILE, DIN), lambda e, t: (e, t, 0))
    vec = pl.BlockSpec((1, DIN), lambda e, t: (0, 0))
    return pl.pallas_call(
        body, name="ssd_post_bwd", grid=(2, R // ROW_TILE),
        in_specs=[blk(0), blk(0), blk(0), blk(0), blk(1), vec, vec],
        out_specs=(oblk, oblk, pl.BlockSpec((1, 8, DIN), lambda e, t: (e, 0, 0))),
        out_shape=(jax.ShapeDtypeStruct((2, R, DIN), BF), jax.ShapeDtypeStruct((2, R, DIN), BF),
                   jax.ShapeDtypeStruct((2, 8, DIN), F32)),
        compiler_params=_params(("parallel", "arbitrary")),
    )(d_yn, y_f, y_b, xbc, proj3, dskip_e, ssd_norm)


def _merge_fwd(proj3, P, S, b_merge, L):
    def body(gp_ref, p_ref, s_ref, b_ref, o_ref):
        gt = _sigmoid(gp_ref[0].astype(F32) + b_ref[...])
        o_ref[0] = (gt[:, :D] * p_ref[0] + gt[:, D:] * s_ref[0]).astype(BF)

    blk = pl.BlockSpec((1, ROW_TILE, D), lambda e, t: (e, t, 0))
    return pl.pallas_call(
        body, name="merge_fwd", grid=(2, L // ROW_TILE),
        in_specs=[pl.BlockSpec((1, ROW_TILE, 2 * D), lambda e, t: (e, t, OFF_GATE // (2 * D))), blk, blk,
                  pl.BlockSpec((1, 2 * D), lambda e, t: (0, 0))],
        out_specs=blk, out_shape=jax.ShapeDtypeStruct((2, L, D), BF),
        compiler_params=_params(("parallel", "parallel")),
    )(proj3, P, S, b_merge)


def _merge_bwd(d_merged, proj3, P, S, b_merge, L):
    _, R, _ = proj3.shape
    nx = L // ROW_TILE

    def body(dm_ref, gp_ref, p_ref, s_ref, b_ref, dp_ref, ds_ref, dg_ref, acc_ref):
        t = pl.program_id(1)

        @pl.when(t == 0)
        def _():
            acc_ref[...] = jnp.zeros_like(acc_ref)

        @pl.when(t >= nx)
        def _():
            dg_ref[...] = jnp.zeros_like(dg_ref)

        @pl.when(t < nx)
        def _():
            gt = _sigmoid(gp_ref[0].astype(F32) + b_ref[...])
            dm = dm_ref[0]
            g1, g2 = gt[:, :D], gt[:, D:]
            dp_ref[0] = (dm * g1).astype(BF)
            ds_ref[0] = (dm * g2).astype(BF)
            dgp = jnp.concatenate([dm * p_ref[0] * g1 * (1.0 - g1), dm * s_ref[0] * g2 * (1.0 - g2)], axis=1)
            dg_ref[0] = dgp.astype(BF)
            acc_ref[0, 0:1, :] += jnp.sum(dgp, axis=0, keepdims=True)

    xmap = lambda e, t: (e, jnp.minimum(t, nx - 1), 0)
    blk = pl.BlockSpec((1, ROW_TILE, D), xmap)
    return pl.pallas_call(
        body, name="merge_bwd", grid=(2, R // ROW_TILE),
        in_specs=[blk, pl.BlockSpec((1, ROW_TILE, 2 * D), lambda e, t: (e, jnp.minimum(t, nx - 1), OFF_GATE // (2 * D))),
                  blk, blk, pl.BlockSpec((1, 2 * D), lambda e, t: (0, 0))],
        out_specs=(blk, blk, pl.BlockSpec((1, ROW_TILE, 2 * D), lambda e, t: (e, t, 0)),
                   pl.BlockSpec((1, 8, 2 * D), lambda e, t: (e, 0, 0))),
        out_shape=(jax.ShapeDtypeStruct((2, L, D), BF), jax.ShapeDtypeStruct((2, L, D), BF),
                   jax.ShapeDtypeStruct((2, R, 2 * D), BF), jax.ShapeDtypeStruct((2, 8, 2 * D), F32)),
        compiler_params=_params(("parallel", "arbitrary")),
    )(d_merged, proj3, P, S, b_merge)


def _final(out3, x, tgt, gtab, norm_post, L):
    def body(o_ref, x_ref, t_ref, g_ref, n_ref, dxo_ref, do_ref, acc_ref):
        @pl.when(pl.program_id(1) == 0)
        def _():
            acc_ref[...] = jnp.zeros_like(acc_ref)

        o = o_ref[0]
        gate = g_ref[0, 0:1, :]
        npost = n_ref[...]
        r2 = lax.rsqrt(jnp.mean(o * o, axis=-1, keepdims=True) + EPS)
        nh = o * r2
        on = nh * npost
        err = x_ref[0] + gate * on - t_ref[0]
        dxo = err * (1.0 / D)
        dxo_ref[0] = dxo
        dnh = dxo * gate * npost
        do_ref[0] = (r2 * (dnh - nh * jnp.mean(dnh * nh, axis=-1, keepdims=True))).astype(BF)
        acc_ref[0, 0:1, :] += jnp.sum(dxo * on, axis=0, keepdims=True)
        acc_ref[0, 1:2, :] += jnp.sum(dxo * gate * nh, axis=0, keepdims=True)
        acc_ref[0, 2:3, :] += jnp.sum(err * err, axis=0, keepdims=True)

    blk = pl.BlockSpec((1, ROW_TILE, D), lambda e, t: (e, t, 0))
    return pl.pallas_call(
        body, name="final", grid=(2, L // ROW_TILE),
        in_specs=[blk, blk, blk, pl.BlockSpec((1, 8, D), lambda e, t: (e, 0, 0)),
                  pl.BlockSpec((1, D), lambda e, t: (0, 0))],
        out_specs=(blk, blk, pl.BlockSpec((1, 8, D), lambda e, t: (e, 0, 0))),
        out_shape=(jax.ShapeDtypeStruct((2, L, D), F32), jax.ShapeDtypeStruct((2, L, D), BF),
                   jax.ShapeDtypeStruct((2, 8, D), F32)),
        compiler_params=_params(("parallel", "arbitrary")),
    )(out3, x, tgt, gtab, norm_post)


def _local_step(x, c, ctx, loss_target, W):
    nb, L, _ = x.shape
    LC = ctx.shape[1]
    R = L + LC
    assert nb == 2 and L % ROW_TILE == 0 and LC % Q == 0 and L % POOL_TILE == 0
    w_inT = W["w_in"]
    w_dtT = jnp.pad(w_inT[OFF_DT:], ((0, 64), (0, 0)))
    tables = _pool_tables(L)
    tr, tl = (2 * R) // 8, (2 * L) // 8

    c16 = jnp.zeros((16, D), F32).at[0:2].set(c).at[2].set(W["c_ctx"])
    mod16 = _adaln_fwd(c16, W["w_ada"], W["b_ada"])
    shift, scale, gate = mod16[:, :D], mod16[:, D:2 * D], mod16[:, 2 * D:]
    npre = W["norm_pre"]
    tab = jnp.zeros((2, 2, 8, D), F32)
    for e in range(2):
        tab = tab.at[e, 0, 0].set(npre[0] * (1.0 + scale[e])).at[e, 0, 1].set(shift[e])
        tab = tab.at[e, 1, 0].set(npre[0] * (1.0 + scale[2])).at[e, 1, 1].set(shift[2])
    gtab = jnp.zeros((2, 8, D), F32).at[:, 0].set(gate[0:2])

    hx = _norm_mod_fwd(x, ctx, tab)
    hx2 = hx.reshape(2 * R, D)
    proj3 = _matmul(hx2, w_inT, BF, "proj_main", tm=tr, tn=1024, bt=True, n=OFF_DT).reshape(2, R, OFF_DT)
    dt_raw = _matmul(hx2, w_dtT, F32, "proj_dt", tm=tr, bt=True).reshape(2, R, 128)
    ypool = _pool_fwd(proj3, W["pool_w"], W["pool_scale"], tables, L)
    xbc = _conv_fwd(proj3, W["conv_w"], W["conv_b"], L)
    bias128 = jnp.pad(W["dt_bias"].reshape(1, 64), ((0, 0), (0, 64)))
    dt = _dt_fwd(dt_raw, bias128)
    to_loc = lambda t: jnp.pad(t[:, :, :64].reshape(2, R, 2, NG, HPG).transpose(0, 3, 1, 2, 4).reshape(2, NG, R, 16),
                               ((0, 0), (0, 0), (0, 0), (0, 112)))
    from_loc = lambda t: jnp.pad(t[..., :16].reshape(2, NG, R, 2, HPG).transpose(0, 2, 3, 1, 4).reshape(2, R, 64),
                                 ((0, 0), (0, 0), (0, 64)))
    dt_loc = to_loc(dt)
    A = -jnp.exp(W["a_log"].reshape(2, NG, HPG))
    a_loc = jnp.zeros((NG, 8, 128), F32).at[:, 0, :16].set(A.transpose(1, 0, 2).reshape(NG, 16))
    y_f, hs_f, y_b, hs_b = _ssd_fwd(xbc, dt_loc, a_loc, L)
    dskip_e = jnp.repeat(W["d_skip"].reshape(1, 32), HEAD, axis=1)
    yn = _ssd_post_fwd(y_f, y_b, xbc, proj3, dskip_e, W["ssd_norm"], L)
    ypool2, yn2 = ypool.reshape(2 * L, D), yn.reshape(2 * L, DIN)
    P = _matmul(ypool2, W["w_proj_pool"], F32, "proj_pool", tm=tl, tn=1024).reshape(2, L, D)
    S = _matmul(yn2, W["w_proj_ssd"], F32, "proj_ssd", tm=tl, tn=1024).reshape(2, L, D)
    merged = _merge_fwd(proj3, P, S, W["b_merge"], L)
    merged2 = merged.reshape(2 * L, D)
    out3 = _matmul(merged2, W["w_out"], F32, "proj_out", tm=tl, tn=1024).reshape(2, L, D)
    dxo, dout, acc_f = _final(out3, x, loss_target, gtab, W["norm_post"], L)

    dout2 = dout.reshape(2 * L, D)
    g = {}
    g["w_out"] = _matmul_tn(merged2, dout2, "dw_out", ta=1024, tn=1024, tr=tl)
    d_merged = _matmul(dout2, W["w_out"], F32, "d_merged", tm=tl, tn=1024, bt=True).reshape(2, L, D)
    dP, dS, dgp, acc_m = _merge_bwd(d_merged, proj3, P, S, W["b_merge"], L)
    dP2, dS2 = dP.reshape(2 * L, D), dS.reshape(2 * L, D)
    g["w_proj_pool"] = _matmul_tn(ypool2, dP2, "dw_proj_pool", ta=1024, tn=1024, tr=tl)
    g["w_proj_ssd"] = _matmul_tn(yn2, dS2, "dw_proj_ssd", ta=1024, tn=1024, tr=tl)
    d_ypool = _matmul(dP2, W["w_proj_pool"], F32, "d_ypool", tm=tl, tn=1024, bt=True).reshape(2, L, D)
    d_yn = _matmul(dS2, W["w_proj_ssd"], F32, "d_yn", tm=tl, tn=1024, bt=True).reshape(2, L, DIN)
    dv, dzp, g["pool_w"], acc_p = _pool_bwd(proj3, d_ypool, W["pool_w"], jnp.swapaxes(W["pool_w"], 1, 2),
                                            W["pool_scale"], tables, L)
    dy2, dzs, acc_s = _ssd_post_bwd(d_yn, y_f, y_b, xbc, proj3, dskip_e, W["ssd_norm"], L)
    dxs_f, dbc_f, ddt_f, dxs_b, dbc_b, ddt_b, acc_a = _ssd_bwd(xbc, dt_loc, a_loc, hs_f, hs_b, y_f, y_b, dy2, L)
    ident = lambda j: j
    dxr_xs, acc_cx = _conv_bwd(proj3, [dxs_f, dxs_b, dy2], [None, None, dskip_e], 0, DIN, [ident, ident, ident],
                               W["conv_w"], W["conv_b"], L, "conv_bwd_xs")
    bcmap = lambda j: 2 * lax.rem(j, NG) + j // NG
    dxr_bc, acc_cb = _conv_bwd(proj3, [dbc_f, dbc_b], [None, None], DIN, 2 * NG * NST, [bcmap, bcmap],
                               W["conv_w"], W["conv_b"], L, "conv_bwd_bc")
    ddtr, acc_d = _dt_bwd(dt_raw, bias128, from_loc(ddt_f), from_loc(ddt_b))
    pieces = [dv, dzp, dzs, dgp, dxr_xs, dxr_bc]
    dw_rows = [_matmul_tn(p.reshape(2 * R, p.shape[2]), hx2, "dw_in_%d" % i, ta=1024, tn=1024, tr=tr)
               for i, p in enumerate(pieces)]
    dw_rows.append(_matmul_tn(ddtr.reshape(2 * R, 128), hx2, "dw_in_dt", ta=128, tn=1024, tr=tr)[:64])
    g["w_in"] = jnp.concatenate(dw_rows, axis=0)
    dh = _dhx(pieces, ddtr, w_inT, w_dtT)
    grad_x, acc_n = _norm_mod_bwd(dh, x, ctx, tab, dxo)
    g["w_ada"], db_rows, sm_rows = _adaln_bwd(acc_n, acc_f, mod16, c16, npre, W["w_ada"])

    g["b_ada"] = db_rows[0:1]
    g["norm_pre"] = sm_rows[0:1]
    g["c_ctx"] = sm_rows[1]
    g["norm_post"] = acc_f[0, 1:2] + acc_f[1, 1:2]
    g["b_merge"] = acc_m[0, 0:1] + acc_m[1, 0:1]
    g["pool_scale"] = acc_p[:, 0, :].reshape(1, D)
    acc_c = jnp.concatenate([acc_cx[0] + acc_cx[1], acc_cb[0] + acc_cb[1]], axis=1)
    g["conv_w"] = acc_c[0:4]
    g["conv_b"] = acc_c[4:5]
    g["dt_bias"] = (acc_d[0, 0, :64] + acc_d[1, 0, :64]).reshape(2, 32)
    dA = (acc_a[0, :, 0, :16] + acc_a[1, :, 0, :16]).reshape(NG, 2, HPG).transpose(1, 0, 2)
    g["a_log"] = (dA * A).reshape(2, 32)
    g["d_skip"] = (acc_s[0, 1] + acc_s[1, 1]).reshape(32, HEAD).sum(axis=1).reshape(1, 32)
    g["ssd_norm"] = acc_s[0, 0:1] + acc_s[1, 0:1]
    loss_lanes = acc_f[:, 2, :]
    return loss_lanes, grad_x, g


MESH = pl.DeviceIdType.MESH
ANY = pl.BlockSpec(memory_space=pl.ANY)


def _all_gather(shard):
    m_per, n = shard.shape

    def body(x_ref, out_ref, send_sems, recv_sems, local_sem):
        x, y, c = lax.axis_index("x"), lax.axis_index("y"), lax.axis_index("c")
        me, sibling = (x, y, c), (x, y, 1 - c)
        chips = [(1 - x, y), (x, 1 - y), (1 - x, 1 - y)]

        def rows(px, py, pc):
            return out_ref.at[pl.ds((4 * px + 2 * py + pc) * m_per, m_per), :]

        def copy(k, block, to, src=None):
            return pltpu.make_async_remote_copy(
                src_ref=rows(*block) if src is None else src, dst_ref=rows(*block),
                send_sem=send_sems.at[k], recv_sem=recv_sems.at[k], device_id=to, device_id_type=MESH)

        mine = pltpu.make_async_copy(x_ref, rows(*me), local_sem)
        mine.start()
        first = [copy(0, me, sibling, src=x_ref)]
        first += [copy(1 + j, me, (*chip, c), src=x_ref) for j, chip in enumerate(chips)]
        for cp in first:
            cp.start()
        passed = [copy(4 + j, (*chip, c), sibling) for j, chip in enumerate(chips)]
        for j, chip in enumerate(chips):
            copy(1 + j, (*chip, c), me).wait_recv()
            passed[j].start()
        copy(0, sibling, me).wait_recv()
        for j, chip in enumerate(chips):
            copy(4 + j, (*chip, 1 - c), me).wait_recv()
        for cp in first + passed:
            cp.wait_send()
        mine.wait()

    return pl.pallas_call(
        body, name="all_gather_weights",
        out_shape=jax.ShapeDtypeStruct((NDEV * m_per, n), shard.dtype),
        in_specs=[ANY], out_specs=ANY,
        scratch_shapes=[pltpu.SemaphoreType.DMA((7,)), pltpu.SemaphoreType.DMA((7,)), pltpu.SemaphoreType.DMA],
    )(shard)


PAIR_PIECES = 4


def _xor_peer(k, x, y, c):
    return (1 - x if k & 4 else x, 1 - y if k & 2 else y, 1 - c if k & 1 else c)


def _pair_exchange(big, small):
    _, nq, rows, n = big.shape
    piece = rows // PAIR_PIECES
    assert piece * PAIR_PIECES == rows and piece % 16 == 0

    def body(big_ref, small_ref, got_ref, osmall_ref, send_sems, recv_sems, local_sem):
        x, y, c = lax.axis_index("x"), lax.axis_index("y"), lax.axis_index("c")
        me = 4 * x + 2 * y + c
        mine = pltpu.make_async_copy(small_ref, osmall_ref.at[me], local_sem)
        mine.start()

        def rc(src, dst, sem, peer):
            return pltpu.make_async_remote_copy(src_ref=src, dst_ref=dst, send_sem=send_sems.at[sem],
                                                recv_sem=recv_sems.at[sem], device_id=peer, device_id_type=MESH)

        sib = _xor_peer(1, x, y, c)
        sends, recvs = [], []
        for q in range(nq):
            for h in range(PAIR_PIECES):
                rws = pl.ds(h * piece, piece)
                cp = rc(big_ref.at[1 - c, q, rws], got_ref.at[q, rws], 8 + q * PAIR_PIECES + h, sib)
                sends.append(cp)
                recvs.append(cp)
        for k in range(1, NDEV):
            px, py, pc = _xor_peer(k, x, y, c)
            sends.append(rc(small_ref, osmall_ref.at[me], k, (px, py, pc)))
            recvs.append(rc(small_ref, osmall_ref.at[4 * px + 2 * py + pc], k, (px, py, pc)))
        for cp in sends:
            cp.start()
        for cp in sends:
            cp.wait_send()
        for cp in recvs:
            cp.wait_recv()
        mine.wait()

    nsem = 8 + nq * PAIR_PIECES
    return pl.pallas_call(
        body, name="grads_pair_exchange",
        out_shape=(jax.ShapeDtypeStruct(big.shape[1:], big.dtype), jax.ShapeDtypeStruct((NDEV,) + small.shape, small.dtype)),
        in_specs=[ANY, ANY], out_specs=(ANY, ANY),
        scratch_shapes=[pltpu.SemaphoreType.DMA((nsem,)), pltpu.SemaphoreType.DMA((nsem,)), pltpu.SemaphoreType.DMA],
    )(big, small)


def _pair_add(big, got):
    _, nq, rows, n = big.shape
    tile = rows // 4
    assert rows % 64 == 0

    def body(c_ref, a_ref, b_ref, o_ref):
        o_ref[0] = (a_ref[0, 0].astype(F32) + b_ref[0].astype(F32)).astype(BF)

    blk = pl.BlockSpec((1, tile, n), lambda q, i, c_ref: (q, i, 0))
    return pl.pallas_call(
        body, name="grads_pair_add",
        grid_spec=pltpu.PrefetchScalarGridSpec(
            num_scalar_prefetch=1, grid=(nq, rows // tile),
            in_specs=[pl.BlockSpec((1, 1, tile, n), lambda q, i, c_ref: (c_ref[0], q, i, 0)), blk], out_specs=blk),
        out_shape=jax.ShapeDtypeStruct(got.shape, BF), compiler_params=_params(("parallel", "parallel")),
    )(lax.axis_index("c").astype(jnp.int32).reshape(1), big, got)


def _chip_exchange(pair):
    def body(in_ref, out_ref, send_sems, recv_sems, local_sem):
        x, y, c = lax.axis_index("x"), lax.axis_index("y"), lax.axis_index("c")
        q = 2 * x + y
        mine = pltpu.make_async_copy(in_ref.at[q], out_ref.at[q], local_sem)
        mine.start()
        sends, recvs = [], []
        for j in range(1, 4):
            px, py, pc = _xor_peer(2 * j, x, y, c)
            pq = 2 * px + py
            for lst, dst in ((sends, out_ref.at[q]), (recvs, out_ref.at[pq])):
                lst.append(pltpu.make_async_remote_copy(
                    src_ref=in_ref.at[pq], dst_ref=dst, send_sem=send_sems.at[j - 1], recv_sem=recv_sems.at[j - 1],
                    device_id=(px, py, pc), device_id_type=MESH))
        for cp in sends:
            cp.start()
        for cp in sends:
            cp.wait_send()
        for cp in recvs:
            cp.wait_recv()
        mine.wait()

    return pl.pallas_call(
        body, name="grads_chip_exchange",
        out_shape=jax.ShapeDtypeStruct(pair.shape, pair.dtype), in_specs=[ANY], out_specs=ANY,
        scratch_shapes=[pltpu.SemaphoreType.DMA((3,)), pltpu.SemaphoreType.DMA((3,)), pltpu.SemaphoreType.DMA],
    )(pair)


ADAM_TILE = 64
PACK_W = 1024


def _adamw(recv, w, m, v, name):
    rp = w.shape[0]
    tile = min(ADAM_TILE, rp)
    nsrc = recv.shape[0]

    def body(r_ref, w_ref, m_ref, v_ref, g_ref, d_ref, nm_ref, nv_ref):
        g = r_ref[0].astype(F32)
        for i in range(1, nsrc):
            g = g + r_ref[i].astype(F32)
        m1 = ADAM_B1 * m_ref[...] + (1.0 - ADAM_B1) * g
        v1 = ADAM_B2 * v_ref[...] + (1.0 - ADAM_B2) * (g * g)
        m_hat = m1 / (1.0 - ADAM_B1 ** ADAM_STEP)
        v_hat = v1 / (1.0 - ADAM_B2 ** ADAM_STEP)
        g_ref[...] = g
        d_ref[...] = -ADAM_LR * (m_hat / (jnp.sqrt(v_hat) + ADAM_EPS) + ADAM_WD * w_ref[...])
        nm_ref[...] = m1
        nv_ref[...] = v1

    blk = pl.BlockSpec((tile, PACK_W), lambda i: (i, 0))
    shp = jax.ShapeDtypeStruct((rp, PACK_W), F32)
    return pl.pallas_call(
        body, name=name, grid=(rp // tile,),
        in_specs=[pl.BlockSpec((nsrc, tile, PACK_W), lambda i: (0, i, 0)), blk, blk, blk],
        out_specs=(blk, blk, blk, blk), out_shape=(shp, shp, shp, shp),
        compiler_params=_params(("parallel",)),
    )(recv, w, m, v)


BIG = {"w_ada": ((3 * D, D), 0), "pool_w": ((4, PGW, PGW), 1), "w_proj_pool": ((D, D), 0), "w_proj_ssd": ((DIN, D), 0),
       "w_out": ((D, D), 0), "w_in": ((IN_COLS, D), 0), "conv_w": ((4, CONV_DIM), 1)}
TRANSPOSED = ("w_ada", "w_in")
PACK_ROWS = {"w_ada": 384, "pool_w": 32, "w_proj_pool": 128, "w_proj_ssd": 256, "w_out": 128, "conv_w": 16, "w_in": 1168}
SMALL = {"c_ctx": (D,), "b_ada": (1, 3 * D), "norm_pre": (1, D), "norm_post": (1, D), "b_merge": (1, 2 * D),
         "pool_scale": (1, D), "conv_b": (1, CONV_DIM), "dt_bias": (2, 32), "a_log": (2, 32), "d_skip": (1, 32),
         "ssd_norm": (1, DIN)}
LOSS_SLOT = 128
BIG_ROWS = sum(PACK_ROWS.values())
assert BIG_ROWS % ADAM_TILE == 0
SMALL_ROWS = 16


def _shard_shape(name):
    shape, ax = BIG[name]
    return tuple(s // NDEV if i == ax else s for i, s in enumerate(shape))


def _as_rows(t, rows):
    pad = [(0, 0)] * (t.ndim - 1) + [(0, rows * PACK_W - t.shape[-1])]
    return jnp.pad(t, pad).reshape(t.shape[:-1] + (rows, PACK_W))


def _shard_rows(t, name):
    sh, r = _shard_shape(name), PACK_ROWS[name]
    lead = t.shape[:t.ndim - len(sh)]
    if len(sh) == 2 and sh[1] == PACK_W:
        return jnp.pad(t, [(0, 0)] * len(lead) + [(0, r - sh[0]), (0, 0)])
    if int(np.prod(sh)) == r * PACK_W:
        return t.reshape(lead + (r, PACK_W))
    return _as_rows(t.reshape(lead + (-1,)), r)


def _to_chunks(full, name):
    shape, ax = BIG[name]
    split = shape[:ax] + (NDEV, shape[ax] // NDEV) + shape[ax + 1:]
    return _shard_rows(jnp.moveaxis(full.reshape(split), ax, 0), name)


def _from_chunks(chunks, name):
    shape, ax = BIG[name]
    return jnp.moveaxis(chunks.reshape((NDEV,) + _shard_shape(name)), 0, ax).reshape(shape)


def _pack_state(t):
    big = jnp.concatenate([_shard_rows(t[n], n) for n in PACK_ROWS], axis=0)
    small = _as_rows(jnp.concatenate([t[n].reshape(-1) for n in SMALL] + [jnp.zeros((LOSS_SLOT,), F32)]), SMALL_ROWS)
    return big, small


def _pack_grads(g, loss_part):
    big = jnp.concatenate([_to_chunks(g[n], n).astype(BF) for n in PACK_ROWS], axis=1)
    big = jnp.swapaxes(big.reshape(4, 2, BIG_ROWS, PACK_W), 0, 1)
    small = [g[n].reshape(-1) for n in SMALL] + [jnp.zeros((LOSS_SLOT,), F32).at[0].set(loss_part)]
    return big, _as_rows(jnp.concatenate(small), SMALL_ROWS)


def _unpack_state(big, small):
    out, off = {}, 0
    for n, r in PACK_ROWS.items():
        sh = _shard_shape(n)
        k = int(np.prod(sh))
        if len(sh) == 2 and sh[1] == PACK_W:
            out[n] = big[off:off + sh[0]]
        else:
            out[n] = big[off:off + r].reshape(-1)[:k].reshape(sh)
        off += r
    flat, off = small.reshape(-1), 0
    for n, sh in SMALL.items():
        k = int(np.prod(sh))
        out[n] = flat[off:off + k].reshape(sh)
        off += k
    out["loss"] = flat[off]
    return out


def _pack_gather(w):
    conv = jnp.concatenate([p.reshape(-1) for p in _split(w["conv_w"], 3)])
    return jnp.concatenate([_as_rows(conv, PACK_ROWS[n]) if n == "conv_w" else _shard_rows(w[n], n).astype(BF)
                            for n in PACK_ROWS], axis=0)


def _unpack_gather(gathered):
    g = gathered.reshape(NDEV, BIG_ROWS, PACK_W)
    out, off = {}, 0
    for n, r in PACK_ROWS.items():
        sh = _shard_shape(n)
        if n == "conv_w":
            k = int(np.prod(sh))
            terms = g[:, off:off + r].reshape(NDEV, -1)[:, :3 * k].astype(F32).reshape(NDEV, 3, k)
            out[n] = _from_chunks(terms[:, 0] + terms[:, 1] + terms[:, 2], n)
        elif len(sh) == 2 and sh[1] == PACK_W:
            out[n] = _from_chunks(g[:, off:off + sh[0]], n)
        else:
            out[n] = _from_chunks(g[:, off:off + r], n)
        off += r
    return out


PARAMS = ["c_ctx", "w_ada", "b_ada", "norm_pre", "norm_post", "w_in", "b_merge", "pool_w", "pool_scale", "conv_w", "conv_b",
          "dt_bias", "a_log", "d_skip", "ssd_norm", "w_proj_pool", "w_proj_ssd", "w_out"]


def kernel(x, c, ctx, c_ctx, w_ada, b_ada, norm_pre, norm_post, w_in, b_merge, pool_w, pool_scale, conv_w, conv_b, dt_bias, a_log, d_skip, ssd_norm, w_proj_pool, w_proj_ssd, w_out, loss_target, m_c_ctx, m_w_ada, m_b_ada, m_norm_pre, m_norm_post, m_w_in, m_b_merge, m_pool_w, m_pool_scale, m_conv_w, m_conv_b, m_dt_bias, m_a_log, m_d_skip, m_ssd_norm, m_w_proj_pool, m_w_proj_ssd, m_w_out, v_c_ctx, v_w_ada, v_b_ada, v_norm_pre, v_norm_post, v_w_in, v_b_merge, v_pool_w, v_pool_scale, v_conv_w, v_conv_b, v_dt_bias, v_a_log, v_d_skip, v_ssd_norm, v_w_proj_pool, v_w_proj_ssd, v_w_out):
    given = dict(locals())
    shapes = {n: given[n].shape for n in PARAMS}

    def local(prefix):
        t = {n: (given[prefix + n] if n == "c_ctx" else given[prefix + n][0]) for n in PARAMS}
        for n in TRANSPOSED:
            t[n] = t[n].T
        return {n: t[n].reshape(_shard_shape(n) if n in BIG else SMALL[n]) for n in PARAMS}

    w, m, v = local(""), local("m_"), local("v_")

    W = _unpack_gather(_all_gather(_pack_gather(w)))
    for n in SMALL:
        W[n] = w[n]
    lanes, grad_x, g = _local_step(x, c, ctx, loss_target, W)
    gbig, gsmall = _pack_grads(g, (0.5 / D) * jnp.sum(lanes))
    got, recv_small = _pair_exchange(gbig, gsmall)
    recv_big = _chip_exchange(_pair_add(gbig, got))
    (wb, ws), (mb, ms), (vb, vs) = _pack_state(w), _pack_state(m), _pack_state(v)
    res = [_unpack_state(b, s) for b, s in zip(_adamw(recv_big, wb, mb, vb, "adamw_big"),
                                               _adamw(recv_small, ws, ms, vs, "adamw_small"))]
    outs = [res[0]["loss"], grad_x]
    for r in res:
        for n in TRANSPOSED:
            r[n] = r[n].T
        outs += [r[n].reshape(shapes[n]) for n in PARAMS]
    return tuple(outs)
```

```python
import functools

import numpy as np
import jax
import jax.numpy as jnp
from jax import lax
from jax.experimental import pallas as pl
from jax.experimental.pallas import tpu as pltpu

F32, BF = jnp.float32, jnp.bfloat16

D = 1024
GRID_W = 64
EPS = 1e-6
POOL_WINDOWS = (2, 4, 8, 16)
PGW = 256
DIN = 2048
HEAD = 64
NST = 128
NG = 4
HPG = 8
GWID = HPG * HEAD
Q = 128
CONV_DIM = 3072
OFF_GATE, OFF_XBC, OFF_DT, IN_COLS = 4096, 6144, 9216, 9280
NDEV = 8
ADAM_LR, ADAM_B1, ADAM_B2, ADAM_EPS, ADAM_WD, ADAM_STEP = 0.001, 0.9, 0.999, 1e-08, 0.01, 10

V7X_VMEM_LIMIT = 56 * 2 ** 20
ROW_TILE = 256


def _params(sem=None):
    return pltpu.CompilerParams(dimension_semantics=sem, vmem_limit_bytes=V7X_VMEM_LIMIT)


def _dot(a, b):
    return jnp.dot(a.astype(BF), b.astype(BF), preferred_element_type=F32)


def _dot_nt(a, b):
    return lax.dot_general(a.astype(BF), b.astype(BF), (((1,), (1,)), ((), ())), preferred_element_type=F32)


def _dot_tn(a, b):
    return lax.dot_general(a.astype(BF), b.astype(BF), (((0,), (0,)), ((), ())), preferred_element_type=F32)


def _split(a, n):
    parts = []
    for _ in range(n):
        p = a.astype(BF)
        parts.append(p)
        a = a - p.astype(F32)
    return parts


def _dot_sl(a, b01, n=3):
    return sum(jnp.dot(p, b01, preferred_element_type=F32) for p in _split(a, n))


def _dot_sr(a01, b, n=3):
    return sum(jnp.dot(a01, p, preferred_element_type=F32) for p in _split(b, n))


def _dot_tn_sl(a, b01, n=2):
    return sum(lax.dot_general(p, b01, (((0,), (0,)), ((), ())), preferred_element_type=F32) for p in _split(a, n))


def _sigmoid(x):
    return 1.0 / (1.0 + jnp.exp(-x))


def _matmul(a, b, out_dtype, name, tm=512, tn=512, tk=1024, bt=False, n=None):
    M, K = a.shape
    N = n if n is not None else (b.shape[0] if bt else b.shape[1])
    tm, tn, tk = min(tm, M), min(tn, N), min(tk, K)
    assert M % tm == 0 and N % tn == 0 and K % tk == 0, (a.shape, b.shape)
    nk = K // tk

    def body(a_ref, b_ref, o_ref, acc):
        k = pl.program_id(2)
        p = _dot_nt(a_ref[...], b_ref[...]) if bt else _dot(a_ref[...], b_ref[...])

        @pl.when(k == 0)
        def _():
            acc[...] = p

        @pl.when(k > 0)
        def _():
            acc[...] += p

        @pl.when(k == nk - 1)
        def _():
            o_ref[...] = acc[...].astype(o_ref.dtype)

    return pl.pallas_call(
        body, name=name, grid=(M // tm, N // tn, nk),
        in_specs=[pl.BlockSpec((tm, tk), lambda i, j, k: (i, k)),
                  pl.BlockSpec((tn, tk), lambda i, j, k: (j, k)) if bt else pl.BlockSpec((tk, tn), lambda i, j, k: (k, j))],
        out_specs=pl.BlockSpec((tm, tn), lambda i, j, k: (i, j)),
        out_shape=jax.ShapeDtypeStruct((M, N), out_dtype),
        scratch_shapes=[pltpu.VMEM((tm, tn), F32)],
        compiler_params=_params(("parallel", "parallel", "arbitrary")),
    )(a, b)


def _matmul_tn(a, g, name, ta=512, tn=512, tr=512):
    M, Ka = a.shape
    N = g.shape[1]
    ta, tn, tr = min(ta, Ka), min(tn, N), min(tr, M)
    assert M % tr == 0 and N % tn == 0 and Ka % ta == 0, (a.shape, g.shape)
    nr = M // tr

    def body(a_ref, g_ref, o_ref):
        k = pl.program_id(2)
        p = _dot_tn(a_ref[...], g_ref[...])

        @pl.when(k == 0)
        def _():
            o_ref[...] = p

        @pl.when(k > 0)
        def _():
            o_ref[...] += p

    return pl.pallas_call(
        body, name=name, grid=(Ka // ta, N // tn, nr),
        in_specs=[pl.BlockSpec((tr, ta), lambda i, j, k: (k, i)), pl.BlockSpec((tr, tn), lambda i, j, k: (k, j))],
        out_specs=pl.BlockSpec((ta, tn), lambda i, j, k: (i, j)),
        out_shape=jax.ShapeDtypeStruct((Ka, N), F32),
        compiler_params=_params(("parallel", "parallel", "arbitrary")),
    )(a, g)


def _dhx(pieces, ddt, w_inT, w_dtT):
    _, R, _ = pieces[0].shape
    tm = R // 4
    kb = 1024
    starts, nblk = [], []
    for p in pieces:
        starts.append(sum(nblk))
        nblk.append(p.shape[2] // kb)
    nk = sum(nblk)
    assert nk * kb == OFF_DT and R % 128 == 0
    npc = len(pieces)

    def body(*refs):
        a_refs, dt_ref, w_ref, wdt_ref, o_ref, acc = refs[:npc], refs[npc], refs[npc + 1], refs[npc + 2], refs[npc + 3], refs[npc + 4]
        k = pl.program_id(2)

        @pl.when(k == 0)
        def _():
            acc[...] = _dot(dt_ref[0], wdt_ref[...])

        for p in range(npc):
            @pl.when((k >= starts[p]) & (k < starts[p] + nblk[p]))
            def _(p=p):
                acc[...] += _dot(a_refs[p][0], w_ref[...])

        @pl.when(k == nk - 1)
        def _():
            o_ref[0] = acc[...]

    in_specs = [pl.BlockSpec((1, tm, kb), functools.partial(
        lambda e, t, k, s, nb: (e, t, jnp.clip(k - s, 0, nb - 1)), s=starts[p], nb=nblk[p])) for p in range(npc)]
    in_specs += [pl.BlockSpec((1, tm, 128), lambda e, t, k: (e, t, 0)),
                 pl.BlockSpec((kb, D), lambda e, t, k: (k, 0)),
                 pl.BlockSpec((128, D), lambda e, t, k: (0, 0))]
    return pl.pallas_call(
        body, name="d_hx", grid=(2, R // tm, nk), in_specs=in_specs,
        out_specs=pl.BlockSpec((1, tm, D), lambda e, t, k: (e, t, 0)),
        out_shape=jax.ShapeDtypeStruct((2, R, D), F32),
        scratch_shapes=[pltpu.VMEM((tm, D), F32)],
        compiler_params=_params(("parallel", "parallel", "arbitrary")),
    )(*pieces, ddt, w_inT, w_dtT)


def _adaln_fwd(c16, w_adaT_bf, b_ada):
    def body(c_ref, w_ref, b_ref, o_ref):
        cc = c_ref[...]
        o_ref[...] = _dot_nt(cc * _sigmoid(cc), w_ref[...]) + b_ref[...]

    return pl.pallas_call(body, name="adaln_fwd", out_shape=jax.ShapeDtypeStruct((16, 3 * D), F32),
                          compiler_params=_params())(c16, w_adaT_bf, b_ada)


def _adaln_bwd(acc_n, acc_f, mod16, c16, norm_pre, w_adaT_bf):
    def body(an_ref, af_ref, mod_ref, c_ref, np_ref, wt_ref, dw_ref, db_ref, sm_ref, dmod):
        npre = np_ref[...]
        dmod[...] = jnp.zeros_like(dmod)
        dnp = jnp.zeros((1, D), F32)
        dshift_c = jnp.zeros((1, D), F32)
        dgpre_c = jnp.zeros((1, D), F32)
        scale_c = mod_ref[2:3, D:2 * D]
        for e in range(2):
            dg_x, ds_x = an_ref[e, 0, 0:1, :], an_ref[e, 0, 1:2, :]
            dg_c, ds_c = an_ref[e, 1, 0:1, :], an_ref[e, 1, 1:2, :]
            dmod[e:e + 1, 0:D] = ds_x
            dmod[e:e + 1, D:2 * D] = dg_x * npre
            dmod[e:e + 1, 2 * D:3 * D] = af_ref[e, 0:1, :]
            dnp = dnp + dg_x * (1.0 + mod_ref[e:e + 1, D:2 * D]) + dg_c * (1.0 + scale_c)
            dshift_c = dshift_c + ds_c
            dgpre_c = dgpre_c + dg_c
        dmod[2:3, 0:D] = dshift_c
        dmod[2:3, D:2 * D] = dgpre_c * npre
        dm = dmod[...]
        cc = c_ref[...]
        sg = _sigmoid(cc)
        dw_ref[...] = _dot_tn(dm, cc * sg)
        db_ref[...] = jnp.zeros_like(db_ref)
        db_ref[0:1, :] = jnp.sum(dm, axis=0, keepdims=True)
        dsilu = sg * (1.0 + cc * (1.0 - sg))
        dcs = _dot(dm, wt_ref[...]) * dsilu
        sm_ref[...] = jnp.zeros_like(sm_ref)
        sm_ref[0:1, :] = dnp
        sm_ref[1:2, :] = dcs[2:3, :]

    return pl.pallas_call(
        body, name="adaln_bwd",
        out_shape=(jax.ShapeDtypeStruct((3 * D, D), F32), jax.ShapeDtypeStruct((16, 3 * D), F32),
                   jax.ShapeDtypeStruct((8, D), F32)),
        scratch_shapes=[pltpu.VMEM((16, 3 * D), F32)],
        compiler_params=_params())(acc_n, acc_f, mod16, c16, norm_pre, w_adaT_bf)


def _row_specs(L):
    nx = L // ROW_TILE
    return (pl.BlockSpec((1, ROW_TILE, D), lambda e, t: (e, jnp.minimum(t, nx - 1), 0)),
            pl.BlockSpec((1, ROW_TILE, D), lambda e, t: (e, jnp.maximum(t - nx, 0), 0)))


def _norm_mod_fwd(x, ctx, tab):
    L = x.shape[1]
    R = L + ctx.shape[1]
    nx = L // ROW_TILE

    def body(x_ref, c_ref, t_ref, o_ref):
        x = jnp.where(pl.program_id(1) < nx, x_ref[0], c_ref[0])
        r = lax.rsqrt(jnp.mean(x * x, axis=-1, keepdims=True) + EPS)
        t = t_ref[0, 0]
        o_ref[0] = (x * r * t[0:1] + t[1:2]).astype(BF)

    return pl.pallas_call(
        body, name="norm_mod_fwd", grid=(2, R // ROW_TILE),
        in_specs=[*_row_specs(L), pl.BlockSpec((1, 1, 8, D), lambda e, t: (e, t // nx, 0, 0))],
        out_specs=pl.BlockSpec((1, ROW_TILE, D), lambda e, t: (e, t, 0)),
        out_shape=jax.ShapeDtypeStruct((2, R, D), BF),
        compiler_params=_params(("parallel", "parallel")),
    )(x, ctx, tab)


def _norm_mod_bwd(dh, x, ctx, tab, dxo):
    L = x.shape[1]
    R = L + ctx.shape[1]
    nx = L // ROW_TILE

    def body(dh_ref, x_ref, c_ref, t_ref, dxo_ref, gx_ref, acc_ref):
        t = pl.program_id(1)
        x = jnp.where(t < nx, x_ref[0], c_ref[0])
        r = lax.rsqrt(jnp.mean(x * x, axis=-1, keepdims=True) + EPS)
        xn = x * r
        dh = dh_ref[0]

        @pl.when((t == 0) | (t == nx))
        def _():
            acc_ref[...] = jnp.zeros_like(acc_ref)

        acc_ref[0, 0, 0:1, :] += jnp.sum(dh * xn, axis=0, keepdims=True)
        acc_ref[0, 0, 1:2, :] += jnp.sum(dh, axis=0, keepdims=True)

        @pl.when(t < nx)
        def _():
            dxn = dh * t_ref[0, 0][0:1]
            dx = r * (dxn - xn * jnp.mean(dxn * xn, axis=-1, keepdims=True))
            gx_ref[0] = dxo_ref[0] + dx

    xspec, cspec = _row_specs(L)
    return pl.pallas_call(
        body, name="norm_mod_bwd", grid=(2, R // ROW_TILE),
        in_specs=[pl.BlockSpec((1, ROW_TILE, D), lambda e, t: (e, t, 0)), xspec, cspec,
                  pl.BlockSpec((1, 1, 8, D), lambda e, t: (e, t // nx, 0, 0)), xspec],
        out_specs=(xspec, pl.BlockSpec((1, 1, 8, D), lambda e, t: (e, t // nx, 0, 0))),
        out_shape=(jax.ShapeDtypeStruct((2, L, D), F32), jax.ShapeDtypeStruct((2, 2, 8, D), F32)),
        compiler_params=_params(("parallel", "arbitrary")),
    )(dh, x, ctx, tab, dxo)


POOL_TILE = 256


def _pool_tables(L):
    rows = L // GRID_W
    mats = np.zeros((4, POOL_TILE, POOL_TILE), np.float32)
    inv = np.zeros((4, L, 1), np.float32)
    for gi, k in enumerate(POOL_WINDOWS):
        lo, hi = k // 2, k - 1 - k // 2
        m = np.zeros((GRID_W, GRID_W), np.float32)
        for t in range(GRID_W):
            m[t, max(t - lo, 0):min(t + hi, GRID_W - 1) + 1] = 1.0
        for b in range(POOL_TILE // GRID_W):
            mats[gi, b * GRID_W:(b + 1) * GRID_W, b * GRID_W:(b + 1) * GRID_W] = m
        cnt_c = m.sum(1)
        cnt_r = np.array([min(r + hi, rows - 1) - max(r - lo, 0) + 1 for r in range(rows)], np.float32)
        inv[gi, :, 0] = (1.0 / (cnt_r[:, None] * cnt_c[None, :])).reshape(-1)
    matsT = np.ascontiguousarray(np.transpose(mats, (0, 2, 1)))
    return (jnp.asarray(mats, BF), jnp.asarray(matsT, BF), jnp.asarray(inv))


def _pool_cols(get_tile, mat, cs_ref, L, n):
    def step(i, carry):
        off = pl.multiple_of(i * POOL_TILE, POOL_TILE)
        cs_ref[pl.ds(GRID_W + off, POOL_TILE), :] = _dot_sr(mat, get_tile(off).astype(F32), n)
        return carry

    lax.fori_loop(0, L // POOL_TILE, step, 0)
    cs_ref[pl.ds(0, GRID_W), :] = jnp.zeros((GRID_W, PGW), F32)

    def prefix(r, carry):
        o = pl.multiple_of(r * GRID_W, GRID_W)
        cs_ref[pl.ds(o + GRID_W, GRID_W), :] = cs_ref[pl.ds(o + GRID_W, GRID_W), :] + cs_ref[pl.ds(o, GRID_W), :]
        return carry

    lax.fori_loop(0, L // GRID_W, prefix, 0)


def _pool_rows(cs_ref, off, below, above, L):
    rows = L // GRID_W
    r0 = off // GRID_W
    parts = []
    for i in range(POOL_TILE // GRID_W):
        hi = pl.multiple_of(jnp.minimum(r0 + i + above + 1, rows) * GRID_W, GRID_W)
        lo = pl.multiple_of(jnp.maximum(r0 + i - below, 0) * GRID_W, GRID_W)
        parts.append(cs_ref[pl.ds(hi, GRID_W), :] - cs_ref[pl.ds(lo, GRID_W), :])
    return jnp.concatenate(parts, axis=0)


def _pool_fwd(proj3, pool_w_bf, pool_scale, tables, L):
    mats, _, inv = tables
    nt = L // POOL_TILE

    def body(v_ref, z_ref, pw_ref, ps_ref, m_ref, inv_ref, o_ref, cs_ref):
        _pool_cols(lambda off: v_ref[0, pl.ds(off, POOL_TILE), :], m_ref[0], cs_ref, L, 1)
        half = lax.shift_left(1, pl.program_id(1))

        def step(i, carry):
            off = pl.multiple_of(i * POOL_TILE, POOL_TILE)
            rows = pl.ds(off, POOL_TILE)
            v = v_ref[0, rows, :].astype(F32)
            diff = _pool_rows(cs_ref, off, half, half - 1, L) * inv_ref[0, rows, :] - v
            yp = _dot(diff, pw_ref[0])
            z = z_ref[0, rows, :].astype(F32)
            o_ref[0, rows, :] = (yp * ps_ref[...] * (z * _sigmoid(z))).astype(BF)
            return carry

        lax.fori_loop(0, nt, step, 0)

    return pl.pallas_call(
        body, name="pool_fwd", grid=(2, 4),
        in_specs=[pl.BlockSpec((1, L, PGW), lambda e, g: (e, 0, g)),
                  pl.BlockSpec((1, L, PGW), lambda e, g: (e, 0, 4 + g)),
                  pl.BlockSpec((1, PGW, PGW), lambda e, g: (g, 0, 0)),
                  pl.BlockSpec((1, PGW), lambda e, g: (0, g)),
                  pl.BlockSpec((1, POOL_TILE, POOL_TILE), lambda e, g: (g, 0, 0)),
                  pl.BlockSpec((1, L, 1), lambda e, g: (g, 0, 0))],
        out_specs=pl.BlockSpec((1, L, PGW), lambda e, g: (e, 0, g)),
        out_shape=jax.ShapeDtypeStruct((2, L, D), BF),
        scratch_shapes=[pltpu.VMEM((L + GRID_W, PGW), F32)],
        compiler_params=_params(("parallel", "parallel")),
    )(proj3, proj3, pool_w_bf, pool_scale, mats, inv)


def _pool_bwd(proj3, d_ypool, pool_w_bf, pool_wT_bf, pool_scale, tables, L):
    mats, matsT, inv = tables
    nt = L // POOL_TILE
    R = proj3.shape[1]

    def body(v_ref, z_ref, dy_ref, pw_ref, pwt_ref, ps_ref, m_ref, mt_ref, inv_ref,
             dv_ref, dz_ref, dpw_ref, acc_ref, cs_ref, dd_ref):
        e = pl.program_id(1)

        @pl.when(e == 0)
        def _():
            dpw_ref[...] = jnp.zeros_like(dpw_ref)
            acc_ref[...] = jnp.zeros_like(acc_ref)

        _pool_cols(lambda off: v_ref[0, pl.ds(off, POOL_TILE), :], m_ref[0], cs_ref, L, 1)
        half = lax.shift_left(1, pl.program_id(0))
        ps = ps_ref[...]

        def step(i, carry):
            off = pl.multiple_of(i * POOL_TILE, POOL_TILE)
            rows = pl.ds(off, POOL_TILE)
            v = v_ref[0, rows, :].astype(F32)
            diff = _pool_rows(cs_ref, off, half, half - 1, L) * inv_ref[0, rows, :] - v
            yp = _dot(diff, pw_ref[0])
            z = z_ref[0, rows, :].astype(F32)
            sg = _sigmoid(z)
            sz = z * sg
            dy = dy_ref[0, rows, :].astype(F32)
            dz_ref[0, rows, :] = (dy * yp * ps * (sg * (1.0 + z * (1.0 - sg)))).astype(BF)
            dys = dy * sz
            acc_ref[0, 0:1, :] += jnp.sum(dys * yp, axis=0, keepdims=True)
            dyp = dys * ps
            dpw_ref[0] += _dot_tn(diff, dyp)
            dd_ref[rows, :] = _dot(dyp, pwt_ref[0])
            return carry

        lax.fori_loop(0, nt, step, 0)
        _pool_cols(lambda off: dd_ref[pl.ds(off, POOL_TILE), :] * inv_ref[0, pl.ds(off, POOL_TILE), :],
                   mt_ref[0], cs_ref, L, 2)

        def step2(i, carry):
            off = pl.multiple_of(i * POOL_TILE, POOL_TILE)
            rows = pl.ds(off, POOL_TILE)
            dv_ref[0, rows, :] = (_pool_rows(cs_ref, off, half - 1, half, L) - dd_ref[rows, :]).astype(BF)
            return carry

        lax.fori_loop(0, nt, step2, 0)
        dv_ref[0, pl.ds(L, R - L), :] = jnp.zeros((R - L, PGW), BF)
        dz_ref[0, pl.ds(L, R - L), :] = jnp.zeros((R - L, PGW), BF)

    return pl.pallas_call(
        body, name="pool_bwd", grid=(4, 2),
        in_specs=[pl.BlockSpec((1, L, PGW), lambda g, e: (e, 0, g)),
                  pl.BlockSpec((1, L, PGW), lambda g, e: (e, 0, 4 + g)),
                  pl.BlockSpec((1, L, PGW), lambda g, e: (e, 0, g)),
                  pl.BlockSpec((1, PGW, PGW), lambda g, e: (g, 0, 0)),
                  pl.BlockSpec((1, PGW, PGW), lambda g, e: (g, 0, 0)),
                  pl.BlockSpec((1, PGW), lambda g, e: (0, g)),
                  pl.BlockSpec((1, POOL_TILE, POOL_TILE), lambda g, e: (g, 0, 0)),
                  pl.BlockSpec((1, POOL_TILE, POOL_TILE), lambda g, e: (g, 0, 0)),
                  pl.BlockSpec((1, L, 1), lambda g, e: (g, 0, 0))],
        out_specs=(pl.BlockSpec((1, R, PGW), lambda g, e: (e, 0, g)),
                   pl.BlockSpec((1, R, PGW), lambda g, e: (e, 0, g)),
                   pl.BlockSpec((1, PGW, PGW), lambda g, e: (g, 0, 0)),
                   pl.BlockSpec((1, 8, PGW), lambda g, e: (g, 0, 0))),
        out_shape=(jax.ShapeDtypeStruct((2, R, D), BF), jax.ShapeDtypeStruct((2, R, D), BF),
                   jax.ShapeDtypeStruct((4, PGW, PGW), F32), jax.ShapeDtypeStruct((4, 8, PGW), F32)),
        scratch_shapes=[pltpu.VMEM((L + GRID_W, PGW), F32), pltpu.VMEM((L, PGW), F32)],
        compiler_params=_params(("parallel", "arbitrary")),
    )(proj3, proj3, d_ypool, pool_w_bf, pool_wT_bf, pool_scale, mats, matsT, inv)


CONV_BLOCK = 128


CONV_CHUNK = 64
CONV_HALO = 8


def _halo_buf_init(buf, val, R):
    z = jnp.zeros((CONV_HALO, CONV_BLOCK), F32)
    buf[pl.ds(0, CONV_HALO), :] = z
    buf[pl.ds(CONV_HALO + R, CONV_HALO), :] = z
    if val is not None:
        buf[pl.ds(CONV_HALO, R), :] = val


def _chunk_taps(buf, start, offs, L):
    n = CONV_CHUNK + 2 * CONV_HALO
    ext = buf[pl.ds(start, n), :]
    t = start + lax.broadcasted_iota(jnp.int32, (CONV_CHUNK, 1), 0)
    out = []
    for off in offs:
        if off == 0:
            out.append(ext[CONV_HALO:CONV_HALO + CONV_CHUNK])
        else:
            r = pltpu.roll(ext, (-off) % n, 0)[CONV_HALO:CONV_HALO + CONV_CHUNK]
            out.append(jnp.where((t < L) == (t + off < L), r, 0.0))
    return out


def _fold8(x):
    return sum(x[i * 8:(i + 1) * 8] for i in range(CONV_CHUNK // 8))


def _conv_fwd(proj3, conv_w, conv_b, L):
    _, R, _ = proj3.shape
    cb0 = OFF_XBC // CONV_BLOCK

    def body(u_ref, w_ref, b_ref, o_ref, ubuf):
        _halo_buf_init(ubuf, u_ref[0].astype(F32), R)
        w = w_ref[...]
        b = b_ref[...]

        def step(c, carry):
            start = pl.multiple_of(c * CONV_CHUNK, CONV_CHUNK)
            taps = _chunk_taps(ubuf, start, (-2, -1, 0, 1), L)
            pre = b + sum(taps[k] * w[k:k + 1, :] for k in range(4))
            o_ref[0, pl.ds(start, CONV_CHUNK), :] = (pre * _sigmoid(pre)).astype(BF)
            return carry

        lax.fori_loop(0, R // CONV_CHUNK, step, 0)

    return pl.pallas_call(
        body, name="conv_fwd", grid=(2, CONV_DIM // CONV_BLOCK),
        in_specs=[pl.BlockSpec((1, R, CONV_BLOCK), lambda e, j: (e, 0, cb0 + j)),
                  pl.BlockSpec((4, CONV_BLOCK), lambda e, j: (0, j)),
                  pl.BlockSpec((1, CONV_BLOCK), lambda e, j: (0, j))],
        out_specs=pl.BlockSpec((1, R, CONV_BLOCK), lambda e, j: (e, 0, j)),
        out_shape=jax.ShapeDtypeStruct((2, R, CONV_DIM), BF),
        scratch_shapes=[pltpu.VMEM((R + 2 * CONV_HALO, CONV_BLOCK), F32)],
        compiler_params=_params(("parallel", "parallel")),
    )(proj3, conv_w, conv_b)


def _conv_bwd(proj3, addends, scales, col0, ncols, in_maps, conv_w, conv_b, L, name):
    _, R, _ = proj3.shape
    cb0 = (OFF_XBC + col0) // CONV_BLOCK
    wb0 = col0 // CONV_BLOCK
    na = len(addends)
    scaled = [i for i in range(na) if scales[i] is not None]

    def body(*refs):
        u_ref, w_ref, b_ref = refs[0], refs[1], refs[2]
        a_refs = refs[3:3 + na]
        s_refs = dict(zip(scaled, refs[3 + na:3 + na + len(scaled)]))
        o_ref, acc_ref, ubuf, dbuf = refs[3 + na + len(scaled):]
        _halo_buf_init(ubuf, u_ref[0].astype(F32), R)
        _halo_buf_init(dbuf, None, R)
        w = w_ref[...]
        b = b_ref[...]
        scl = {i: s_refs[i][...] for i in scaled}

        def step1(c, sums):
            start = pl.multiple_of(c * CONV_CHUNK, CONV_CHUNK)
            rows = pl.ds(start, CONV_CHUNK)
            taps = _chunk_taps(ubuf, start, (-2, -1, 0, 1), L)
            pre = b + sum(taps[k] * w[k:k + 1, :] for k in range(4))
            sg = _sigmoid(pre)
            dxbc = jnp.zeros((CONV_CHUNK, CONV_BLOCK), F32)
            for i, a in enumerate(a_refs):
                t = a[0, rows, :].astype(F32)
                dxbc = dxbc + (t * scl[i] if i in scl else t)
            dpre = dxbc * (sg * (1.0 + pre * (1.0 - sg)))
            dbuf[pl.ds(pl.multiple_of(start + CONV_HALO, CONV_HALO), CONV_CHUNK), :] = dpre
            return tuple(sums[k] + _fold8(dpre * taps[k]) for k in range(4)) + (sums[4] + _fold8(dpre),)

        zero = jnp.zeros((8, CONV_BLOCK), F32)
        sums = lax.fori_loop(0, R // CONV_CHUNK, step1, (zero,) * 5)
        acc_ref[...] = jnp.zeros_like(acc_ref)
        for k in range(5):
            acc_ref[0, k:k + 1, :] = jnp.sum(sums[k], axis=0, keepdims=True)

        def step2(c, carry):
            start = pl.multiple_of(c * CONV_CHUNK, CONV_CHUNK)
            d = _chunk_taps(dbuf, start, (2, 1, 0, -1), L)
            o_ref[0, pl.ds(start, CONV_CHUNK), :] = sum(d[k] * w[k:k + 1, :] for k in range(4)).astype(BF)
            return carry

        lax.fori_loop(0, R // CONV_CHUNK, step2, 0)

    in_specs = [pl.BlockSpec((1, R, CONV_BLOCK), lambda e, j: (e, 0, cb0 + j)),
                pl.BlockSpec((4, CONV_BLOCK), lambda e, j: (0, wb0 + j)),
                pl.BlockSpec((1, CONV_BLOCK), lambda e, j: (0, wb0 + j))]
    for m in in_maps:
        in_specs.append(pl.BlockSpec((1, R, CONV_BLOCK), functools.partial(lambda e, j, m: (e, 0, m(j)), m=m)))
    for i in scaled:
        in_specs.append(pl.BlockSpec((1, CONV_BLOCK), functools.partial(lambda e, j, m: (0, m(j)), m=in_maps[i])))
    return pl.pallas_call(
        body, name=name, grid=(2, ncols // CONV_BLOCK),
        in_specs=in_specs,
        out_specs=(pl.BlockSpec((1, R, CONV_BLOCK), lambda e, j: (e, 0, j)),
                   pl.BlockSpec((1, 8, CONV_BLOCK), lambda e, j: (e, 0, j))),
        out_shape=(jax.ShapeDtypeStruct((2, R, ncols), BF), jax.ShapeDtypeStruct((2, 8, ncols), F32)),
        scratch_shapes=[pltpu.VMEM((R + 2 * CONV_HALO, CONV_BLOCK), F32)] * 2,
        compiler_params=_params(("parallel", "parallel")),
    )(proj3, conv_w, conv_b, *addends, *[scales[i] for i in scaled])


def _softplus(x):
    e = jnp.exp(-jnp.abs(x))
    u = 1.0 + e
    return jnp.maximum(x, 0.0) + jnp.where(u == 1.0, e, e * jnp.log(u) / (u - 1.0))


def _dt_fwd(dt_raw, bias128):
    _, R, _ = dt_raw.shape

    def body(x_ref, b_ref, o_ref):
        o_ref[0] = _softplus(x_ref[0] + b_ref[...])

    return pl.pallas_call(
        body, name="dt_fwd", grid=(2,),
        in_specs=[pl.BlockSpec((1, R, 128), lambda e: (e, 0, 0)), pl.BlockSpec((1, 128), lambda e: (0, 0))],
        out_specs=pl.BlockSpec((1, R, 128), lambda e: (e, 0, 0)),
        out_shape=jax.ShapeDtypeStruct(dt_raw.shape, F32),
        compiler_params=_params(("parallel",)),
    )(dt_raw, bias128)


def _dt_bwd(dt_raw, bias128, ddt_f, ddt_b):
    _, R, _ = dt_raw.shape

    def body(x_ref, b_ref, f_ref, g_ref, o_ref, acc_ref):
        d = (f_ref[0] + g_ref[0]) * _sigmoid(x_ref[0] + b_ref[...])
        o_ref[0] = d.astype(BF)
        acc_ref[...] = jnp.zeros_like(acc_ref)
        acc_ref[0, 0:1, :] = jnp.sum(d, axis=0, keepdims=True)

    blk = pl.BlockSpec((1, R, 128), lambda e: (e, 0, 0))
    return pl.pallas_call(
        body, name="dt_bwd", grid=(2,),
        in_specs=[blk, pl.BlockSpec((1, 128), lambda e: (0, 0)), blk, blk],
        out_specs=(blk, pl.BlockSpec((1, 8, 128), lambda e: (e, 0, 0))),
        out_shape=(jax.ShapeDtypeStruct(dt_raw.shape, BF), jax.ShapeDtypeStruct((2, 8, 128), F32)),
        compiler_params=_params(("parallel",)),
    )(dt_raw, bias128, ddt_f, ddt_b)


def _tri(d):
    i = lax.broadcasted_iota(jnp.int32, (Q, Q), 0)
    j = lax.broadcasted_iota(jnp.int32, (Q, Q), 1)
    return (i >= j) if d == 0 else (i <= j)


def _expand_mat(d):
    r = lax.broadcasted_iota(jnp.int32, (128, GWID), 0)
    c = lax.broadcasted_iota(jnp.int32, (128, GWID), 1)
    return (r == d * HPG + jnp.right_shift(c, 6)).astype(BF)


def _reduce_mat(d):
    r = lax.broadcasted_iota(jnp.int32, (GWID, 128), 0)
    c = lax.broadcasted_iota(jnp.int32, (GWID, 128), 1)
    return (c == d * HPG + jnp.right_shift(r, 6)).astype(BF)


def _ssd_chunk(d, dt, A, xs, B, C):
    mask = _tri(d)
    T = mask.astype(BF)
    Tt = _tri(1 - d).astype(BF)
    a = dt * A
    acs = _dot_sr(T, a)
    E = _expand_mat(d)
    dt_e = _dot_sl(dt, E, 2)
    acs_e = _dot_sl(acs, E, 2)
    alast_e = acs_e[Q - 1:Q, :] if d == 0 else acs_e[0:1, :]
    return dict(mask=mask, T=T, Tt=Tt, acs=acs, acsT=acs.T, dt_e=dt_e, acs_e=acs_e, lam=jnp.exp(acs_e),
                w=jnp.exp(alast_e - acs_e), decay=jnp.exp(alast_e), xt=xs * dt_e, CB=_dot_nt(C, B))


def _head_decay(q, d, hh):
    col = q["acs"][:, d * HPG + hh:d * HPG + hh + 1]
    row = q["acsT"][d * HPG + hh:d * HPG + hh + 1, :]
    return jnp.exp(jnp.where(q["mask"], col - row, -jnp.inf))


def _chunk_maps(NX, NS):
    cf = lambda s: lax.rem(s + NX, NS)
    cb = lambda s: NS - 1 - s
    return cf, cb


def _ssd_fwd(xbc, dt_loc, a_loc, L):
    _, R, _ = xbc.shape
    NX, NS = L // Q, R // Q
    cf, cb = _chunk_maps(NX, NS)

    def body(xs_f, b_f, c_f, dt_f, xs_b, b_b, c_b, dt_b, a_ref, y_f, hs_f, y_b, hs_b, hT):
        @pl.when(pl.program_id(2) == 0)
        def _():
            hT[...] = jnp.zeros_like(hT)

        A = a_ref[0, 0:1, :]
        lane = lax.broadcasted_iota(jnp.int32, (Q, 128), 1)
        for d, (xs_ref, b_ref, c_ref, dt_ref, y_ref, hs_ref) in enumerate(
                ((xs_f, b_f, c_f, dt_f, y_f, hs_f), (xs_b, b_b, c_b, dt_b, y_b, hs_b))):
            xs, B, C = xs_ref[0].astype(F32), b_ref[0], c_ref[0]
            q = _ssd_chunk(d, dt_ref[0, 0], A, xs, B, C)
            h = hT[d]
            hb = h.astype(BF)
            hs_ref[0, 0] = hb
            parts = []
            for pr in range(HPG // 2):
                xp = q["xt"][:, pr * 128:(pr + 1) * 128].astype(BF)
                r0 = _dot(q["CB"] * _head_decay(q, d, 2 * pr), xp)
                r1 = _dot(q["CB"] * _head_decay(q, d, 2 * pr + 1), xp)
                parts.append(jnp.where(lane < HEAD, r0, r1))
            y_ref[0] = jnp.concatenate(parts, axis=1) + _dot(C, hb) * q["lam"]
            hT[d] = q["decay"] * h + _dot_tn(B, q["xt"] * q["w"])

    def spec(shape, imap):
        return pl.BlockSpec(shape, imap)

    def ins(c):
        return [spec((1, Q, GWID), lambda e, g, s: (e, c(s), g)),
                spec((1, Q, NST), lambda e, g, s: (e, c(s), DIN // NST + g)),
                spec((1, Q, NST), lambda e, g, s: (e, c(s), DIN // NST + NG + g)),
                spec((1, 1, Q, 128), lambda e, g, s: (e, g, c(s), 0))]

    def outs(c):
        return [spec((1, Q, GWID), lambda e, g, s: (e, c(s), g)),
                spec((1, 1, NST, GWID), lambda e, g, s: (e, c(s), 0, g))]

    yshape = jax.ShapeDtypeStruct((2, R, DIN), F32)
    hshape = jax.ShapeDtypeStruct((2, NS, NST, DIN), BF)
    return pl.pallas_call(
        body, name="ssd_fwd", grid=(2, NG, NS),
        in_specs=ins(cf) + ins(cb) + [spec((1, 8, 128), lambda e, g, s: (g, 0, 0))],
        out_specs=tuple(outs(cf) + outs(cb)),
        out_shape=(yshape, hshape, yshape, hshape),
        scratch_shapes=[pltpu.VMEM((2, NST, GWID), F32)],
        compiler_params=_params(("parallel", "parallel", "arbitrary")),
    )(xbc, xbc, xbc, dt_loc, xbc, xbc, xbc, dt_loc, a_loc)


def _ssd_bwd(xbc, dt_loc, a_loc, hs_f, hs_b, y_f, y_b, dy, L):
    _, R, _ = xbc.shape
    NX, NS = L // Q, R // Q
    cf0, cb0 = _chunk_maps(NX, NS)
    cf = lambda sp: cf0(NS - 1 - sp)
    cb = lambda sp: cb0(NS - 1 - sp)

    def body(xs_f, b_f, c_f, dt_f, hs_f_, dy_f, y_f_, xs_b, b_b, c_b, dt_b, hs_b_, dy_b, y_b_, a_ref,
             dxs_f, dbc_f, ddt_f, dxs_b, dbc_b, ddt_b, da_ref, dhT):
        @pl.when(pl.program_id(2) == 0)
        def _():
            dhT[...] = jnp.zeros_like(dhT)
            da_ref[...] = jnp.zeros_like(da_ref)

        A = a_ref[0, 0:1, :]
        lane = lax.broadcasted_iota(jnp.int32, (Q, 128), 1)
        row = lax.broadcasted_iota(jnp.int32, (Q, 128), 0)
        for d, (xs_ref, b_ref, c_ref, dt_ref, hs_ref, dy_ref, y_ref, dxs_ref, dbc_ref, ddt_ref) in enumerate(
                ((xs_f, b_f, c_f, dt_f, hs_f_, dy_f, y_f_, dxs_f, dbc_f, ddt_f),
                 (xs_b, b_b, c_b, dt_b, hs_b_, dy_b, y_b_, dxs_b, dbc_b, ddt_b))):
            xs, B, C, dt = xs_ref[0].astype(F32), b_ref[0], c_ref[0], dt_ref[0, 0]
            q = _ssd_chunk(d, dt, A, xs, B, C)
            xt, lam, w, decay = q["xt"], q["lam"], q["w"], q["decay"]
            H = hs_ref[0, 0]
            dyv = dy_ref[0].astype(F32)
            dh = dhT[d]
            dZ = dyv * lam
            dC = _dot_nt(dZ, H)
            dH = _dot_tn(C, dZ)
            U = _dot(B, dh)
            xw = xt * w
            dxt = U * w
            dalast_e = (jnp.sum(U * xw, axis=0, keepdims=True)
                        + decay * jnp.sum(dh * H.astype(F32), axis=0, keepdims=True))
            dB = _dot_nt(xw, dh)
            dCB = jnp.zeros((Q, Q), F32)
            dxt_parts = []
            for pr in range(HPG // 2):
                xp = xt[:, pr * 128:(pr + 1) * 128]
                dyp = dyv[:, pr * 128:(pr + 1) * 128]
                dxp = jnp.zeros((Q, 128), F32)
                for h2 in range(2):
                    Lh = _head_decay(q, d, 2 * pr + h2)
                    dym = jnp.where((lane < HEAD) if h2 == 0 else (lane >= HEAD), dyp, 0.0)
                    dxp = dxp + _dot_tn(q["CB"] * Lh, dym)
                    dCB = dCB + _dot_nt(dym, xp) * Lh
                dxt_parts.append(dxp)
            dxt_diag = jnp.concatenate(dxt_parts, axis=1)
            dC = dC + _dot(dCB, B)
            dB = dB + _dot_tn(dCB, C)
            Rm = _reduce_mat(d)
            dacs = _dot_sl(dyv * y_ref[0] - xt.astype(BF).astype(F32) * dxt_diag - U * xw, Rm, 2)
            dxt = dxt + dxt_diag
            dal = _dot_sl(jnp.broadcast_to(dalast_e, (8, GWID)), Rm, 2)[0:1, :]
            dacs = dacs + jnp.where(row == (Q - 1 if d == 0 else 0), dal, 0.0)
            da = _dot_sr(q["Tt"], dacs, 2)
            ddt_ref[0, 0] = da * A + _dot_sl(dxt * xs, Rm, 2)
            da_ref[0, 0, 0:1, :] += jnp.sum(da * dt, axis=0, keepdims=True)
            dxs_ref[0] = (dxt * q["dt_e"]).astype(BF)
            dbc_ref[0] = jnp.concatenate([dB, dC], axis=1).astype(BF)
            dhT[d] = decay * dh + dH

    def spec(shape, imap):
        return pl.BlockSpec(shape, imap)

    def ins(c):
        return [spec((1, Q, GWID), lambda e, g, s: (e, c(s), g)),
                spec((1, Q, NST), lambda e, g, s: (e, c(s), DIN // NST + g)),
                spec((1, Q, NST), lambda e, g, s: (e, c(s), DIN // NST + NG + g)),
                spec((1, 1, Q, 128), lambda e, g, s: (e, g, c(s), 0)),
                spec((1, 1, NST, GWID), lambda e, g, s: (e, c(s), 0, g)),
                spec((1, Q, GWID), lambda e, g, s: (e, c(s), g)),
                spec((1, Q, GWID), lambda e, g, s: (e, c(s), g))]

    def outs(c):
        return [spec((1, Q, GWID), lambda e, g, s: (e, c(s), g)),
                spec((1, Q, 2 * NST), lambda e, g, s: (e, c(s), g)),
                spec((1, 1, Q, 128), lambda e, g, s: (e, g, c(s), 0))]

    s_xs = jax.ShapeDtypeStruct((2, R, DIN), BF)
    s_bc = jax.ShapeDtypeStruct((2, R, 2 * NG * NST), BF)
    s_dt = jax.ShapeDtypeStruct((2, NG, R, 128), F32)
    return pl.pallas_call(
        body, name="ssd_bwd", grid=(2, NG, NS),
        in_specs=ins(cf) + ins(cb) + [spec((1, 8, 128), lambda e, g, s: (g, 0, 0))],
        out_specs=tuple(outs(cf) + outs(cb) + [spec((1, 1, 8, 128), lambda e, g, s: (e, g, 0, 0))]),
        out_shape=(s_xs, s_bc, s_dt, s_xs, s_bc, s_dt, jax.ShapeDtypeStruct((2, NG, 8, 128), F32)),
        scratch_shapes=[pltpu.VMEM((2, NST, GWID), F32)],
        compiler_params=_params(("parallel", "parallel", "arbitrary")),
    )(xbc, xbc, xbc, dt_loc, hs_f, dy, y_f, xbc, xbc, xbc, dt_loc, hs_b, dy, y_b, a_loc)


def _ssd_post_fwd(y_f, y_b, xbc, proj3, dskip_e, ssd_norm, L):
    def body(yf_ref, yb_ref, xs_ref, z_ref, ds_ref, w_ref, o_ref):
        y2 = yf_ref[0] + yb_ref[0] + ds_ref[...] * xs_ref[0].astype(F32)
        z = z_ref[0].astype(F32)
        u = y2 * (z * _sigmoid(z))
        parts = []
        for g in range(NG):
            ug = u[:, g * GWID:(g + 1) * GWID]
            parts.append(ug * lax.rsqrt(jnp.mean(ug * ug, axis=-1, keepdims=True) + EPS))
        o_ref[0] = (jnp.concatenate(parts, axis=1) * w_ref[...]).astype(BF)

    blk = lambda c: pl.BlockSpec((1, ROW_TILE, DIN), lambda e, t: (e, t, c))
    vec = pl.BlockSpec((1, DIN), lambda e, t: (0, 0))
    return pl.pallas_call(
        body, name="ssd_post_fwd", grid=(2, L // ROW_TILE),
        in_specs=[blk(0), blk(0), blk(0), blk(1), vec, vec],
        out_specs=blk(0),
        out_shape=jax.ShapeDtypeStruct((2, L, DIN), BF),
        compiler_params=_params(("parallel", "parallel")),
    )(y_f, y_b, xbc, proj3, dskip_e, ssd_norm)


def _ssd_post_bwd(d_yn, y_f, y_b, xbc, proj3, dskip_e, ssd_norm, L):
    _, R, _ = y_f.shape
    nx = L // ROW_TILE

    def body(dyn_ref, yf_ref, yb_ref, xs_ref, z_ref, ds_ref, w_ref, dy_ref, dz_ref, acc_ref):
        t = pl.program_id(1)

        @pl.when(t == 0)
        def _():
            acc_ref[...] = jnp.zeros_like(acc_ref)

        @pl.when(t >= nx)
        def _():
            dy_ref[...] = jnp.zeros_like(dy_ref)
            dz_ref[...] = jnp.zeros_like(dz_ref)

        @pl.when(t < nx)
        def _():
            xs = xs_ref[0].astype(F32)
            y2 = yf_ref[0] + yb_ref[0] + ds_ref[...] * xs
            z = z_ref[0].astype(F32)
            sg = _sigmoid(z)
            sz = z * sg
            u = y2 * sz
            dyn = dyn_ref[0].astype(F32)
            dun = dyn * w_ref[...]
            uh_parts, du_parts = [], []
            for g in range(NG):
                sl = slice(g * GWID, (g + 1) * GWID)
                ug = u[:, sl]
                rg = lax.rsqrt(jnp.mean(ug * ug, axis=-1, keepdims=True) + EPS)
                uh = ug * rg
                dg = dun[:, sl]
                du_parts.append(rg * (dg - uh * jnp.mean(dg * uh, axis=-1, keepdims=True)))
                uh_parts.append(uh)
            du = jnp.concatenate(du_parts, axis=1)
            uh = jnp.concatenate(uh_parts, axis=1)
            dy2 = du * sz
            dy_ref[0] = dy2.astype(BF)
            dz_ref[0] = (du * y2 * (sg * (1.0 + z * (1.0 - sg)))).astype(BF)
            acc_ref[0, 0:1, :] += jnp.sum(dyn * uh, axis=0, keepdims=True)
            acc_ref[0, 1:2, :] += jnp.sum(dy2 * xs, axis=0, keepdims=True)

    xmap = lambda c: (lambda e, t: (e, jnp.minimum(t, nx - 1), c))
    blk = lambda c: pl.BlockSpec((1, ROW_TILE, DIN), xmap(c))
    oblk = pl.BlockSpec((1, ROW_TILE, DIN), lambda e, t: (e, t, 0))
    vec = pl.BlockSpec((1, DIN), lambda e, t: (0, 0))
    return pl.pallas_call(
        body, name="ssd_post_bwd", grid=(2, R // ROW_TILE),
        in_specs=[blk(0), blk(0), blk(0), blk(0), blk(1), vec, vec],
        out_specs=(oblk, oblk, pl.BlockSpec((1, 8, DIN), lambda e, t: (e, 0, 0))),
        out_shape=(jax.ShapeDtypeStruct((2, R, DIN), BF), jax.ShapeDtypeStruct((2, R, DIN), BF),
                   jax.ShapeDtypeStruct((2, 8, DIN), F32)),
        compiler_params=_params(("parallel", "arbitrary")),
    )(d_yn, y_f, y_b, xbc, proj3, dskip_e, ssd_norm)


def _merge_fwd(proj3, P, S, b_merge, L):
    def body(gp_ref, p_ref, s_ref, b_ref, o_ref):
        gt = _sigmoid(gp_ref[0].astype(F32) + b_ref[...])
        o_ref[0] = (gt[:, :D] * p_ref[0].astype(F32) + gt[:, D:] * s_ref[0].astype(F32)).astype(BF)

    blk = pl.BlockSpec((1, ROW_TILE, D), lambda e, t: (e, t, 0))
    return pl.pallas_call(
        body, name="merge_fwd", grid=(2, L // ROW_TILE),
        in_specs=[pl.BlockSpec((1, ROW_TILE, 2 * D), lambda e, t: (e, t, OFF_GATE // (2 * D))), blk, blk,
                  pl.BlockSpec((1, 2 * D), lambda e, t: (0, 0))],
        out_specs=blk, out_shape=jax.ShapeDtypeStruct((2, L, D), BF),
        compiler_params=_params(("parallel", "parallel")),
    )(proj3, P, S, b_merge)


def _merge_bwd(d_merged, proj3, P, S, b_merge, L):
    _, R, _ = proj3.shape
    nx = L // ROW_TILE

    def body(dm_ref, gp_ref, p_ref, s_ref, b_ref, dp_ref, ds_ref, dg_ref, acc_ref):
        t = pl.program_id(1)

        @pl.when(t == 0)
        def _():
            acc_ref[...] = jnp.zeros_like(acc_ref)

        @pl.when(t >= nx)
        def _():
            dg_ref[...] = jnp.zeros_like(dg_ref)

        @pl.when(t < nx)
        def _():
            gt = _sigmoid(gp_ref[0].astype(F32) + b_ref[...])
            dm = dm_ref[0].astype(F32)
            g1, g2 = gt[:, :D], gt[:, D:]
            dp_ref[0] = (dm * g1).astype(BF)
            ds_ref[0] = (dm * g2).astype(BF)
            dgp = jnp.concatenate([dm * p_ref[0].astype(F32) * g1 * (1.0 - g1),
                                   dm * s_ref[0].astype(F32) * g2 * (1.0 - g2)], axis=1)
            dg_ref[0] = dgp.astype(BF)
            acc_ref[0, 0:1, :] += jnp.sum(dgp, axis=0, keepdims=True)

    xmap = lambda e, t: (e, jnp.minimum(t, nx - 1), 0)
    blk = pl.BlockSpec((1, ROW_TILE, D), xmap)
    return pl.pallas_call(
        body, name="merge_bwd", grid=(2, R // ROW_TILE),
        in_specs=[blk, pl.BlockSpec((1, ROW_TILE, 2 * D), lambda e, t: (e, jnp.minimum(t, nx - 1), OFF_GATE // (2 * D))),
                  blk, blk, pl.BlockSpec((1, 2 * D), lambda e, t: (0, 0))],
        out_specs=(blk, blk, pl.BlockSpec((1, ROW_TILE, 2 * D), lambda e, t: (e, t, 0)),
                   pl.BlockSpec((1, 8, 2 * D), lambda e, t: (e, 0, 0))),
        out_shape=(jax.ShapeDtypeStruct((2, L, D), BF), jax.ShapeDtypeStruct((2, L, D), BF),
                   jax.ShapeDtypeStruct((2, R, 2 * D), BF), jax.ShapeDtypeStruct((2, 8, 2 * D), F32)),
        compiler_params=_params(("parallel", "arbitrary")),
    )(d_merged, proj3, P, S, b_merge)


def _final(out3, x, tgt, gtab, norm_post, L):
    def body(o_ref, x_ref, t_ref, g_ref, n_ref, dxo_ref, do_ref, acc_ref):
        @pl.when(pl.program_id(1) == 0)
        def _():
            acc_ref[...] = jnp.zeros_like(acc_ref)

        o = o_ref[0].astype(F32)
        gate = g_ref[0, 0:1, :]
        npost = n_ref[...]
        r2 = lax.rsqrt(jnp.mean(o * o, axis=-1, keepdims=True) + EPS)
        nh = o * r2
        on = nh * npost
        err = x_ref[0] + gate * on - t_ref[0]
        dxo = err * (1.0 / D)
        dxo_ref[0] = dxo
        dnh = dxo * gate * npost
        do_ref[0] = (r2 * (dnh - nh * jnp.mean(dnh * nh, axis=-1, keepdims=True))).astype(BF)
        acc_ref[0, 0:1, :] += jnp.sum(dxo * on, axis=0, keepdims=True)
        acc_ref[0, 1:2, :] += jnp.sum(dxo * gate * nh, axis=0, keepdims=True)
        acc_ref[0, 2:3, :] += jnp.sum(err * err, axis=0, keepdims=True)

    blk = pl.BlockSpec((1, ROW_TILE, D), lambda e, t: (e, t, 0))
    return pl.pallas_call(
        body, name="final", grid=(2, L // ROW_TILE),
        in_specs=[blk, blk, blk, pl.BlockSpec((1, 8, D), lambda e, t: (e, 0, 0)),
                  pl.BlockSpec((1, D), lambda e, t: (0, 0))],
        out_specs=(blk, blk, pl.BlockSpec((1, 8, D), lambda e, t: (e, 0, 0))),
        out_shape=(jax.ShapeDtypeStruct((2, L, D), F32), jax.ShapeDtypeStruct((2, L, D), BF),
                   jax.ShapeDtypeStruct((2, 8, D), F32)),
        compiler_params=_params(("parallel", "arbitrary")),
    )(out3, x, tgt, gtab, norm_post)


def _local_step(x, c, ctx, loss_target, W):
    nb, L, _ = x.shape
    LC = ctx.shape[1]
    R = L + LC
    assert nb == 2 and L % ROW_TILE == 0 and LC % Q == 0 and L % POOL_TILE == 0
    w_inT = W["w_in"]
    w_dtT = jnp.pad(w_inT[OFF_DT:], ((0, 64), (0, 0)))
    tables = _pool_tables(L)
    tr, tl = (2 * R) // 8, (2 * L) // 8

    c16 = jnp.zeros((16, D), F32).at[0:2].set(c).at[2].set(W["c_ctx"])
    mod16 = _adaln_fwd(c16, W["w_ada"], W["b_ada"])
    shift, scale, gate = mod16[:, :D], mod16[:, D:2 * D], mod16[:, 2 * D:]
    npre = W["norm_pre"]
    tab = jnp.zeros((2, 2, 8, D), F32)
    for e in range(2):
        tab = tab.at[e, 0, 0].set(npre[0] * (1.0 + scale[e])).at[e, 0, 1].set(shift[e])
        tab = tab.at[e, 1, 0].set(npre[0] * (1.0 + scale[2])).at[e, 1, 1].set(shift[2])
    gtab = jnp.zeros((2, 8, D), F32).at[:, 0].set(gate[0:2])

    hx = _norm_mod_fwd(x, ctx, tab)
    hx2 = hx.reshape(2 * R, D)
    proj3 = _matmul(hx2, w_inT, BF, "proj_main", tm=tr, tn=1024, bt=True, n=OFF_DT).reshape(2, R, OFF_DT)
    dt_raw = _matmul(hx2, w_dtT, F32, "proj_dt", tm=tr, bt=True).reshape(2, R, 128)
    ypool = _pool_fwd(proj3, W["pool_w"], W["pool_scale"], tables, L)
    xbc = _conv_fwd(proj3, W["conv_w"], W["conv_b"], L)
    bias128 = jnp.pad(W["dt_bias"].reshape(1, 64), ((0, 0), (0, 64)))
    dt = _dt_fwd(dt_raw, bias128)
    to_loc = lambda t: jnp.pad(t[:, :, :64].reshape(2, R, 2, NG, HPG).transpose(0, 3, 1, 2, 4).reshape(2, NG, R, 16),
                               ((0, 0), (0, 0), (0, 0), (0, 112)))
    from_loc = lambda t: jnp.pad(t[..., :16].reshape(2, NG, R, 2, HPG).transpose(0, 2, 3, 1, 4).reshape(2, R, 64),
                                 ((0, 0), (0, 0), (0, 64)))
    dt_loc = to_loc(dt)
    A = -jnp.exp(W["a_log"].reshape(2, NG, HPG))
    a_loc = jnp.zeros((NG, 8, 128), F32).at[:, 0, :16].set(A.transpose(1, 0, 2).reshape(NG, 16))
    y_f, hs_f, y_b, hs_b = _ssd_fwd(xbc, dt_loc, a_loc, L)
    dskip_e = jnp.repeat(W["d_skip"].reshape(1, 32), HEAD, axis=1)
    yn = _ssd_post_fwd(y_f, y_b, xbc, proj3, dskip_e, W["ssd_norm"], L)
    ypool2, yn2 = ypool.reshape(2 * L, D), yn.reshape(2 * L, DIN)
    P = _matmul(ypool2, W["w_proj_pool"], BF, "proj_pool", tm=tl, tn=1024).reshape(2, L, D)
    S = _matmul(yn2, W["w_proj_ssd"], BF, "proj_ssd", tm=tl, tn=1024).reshape(2, L, D)
    merged = _merge_fwd(proj3, P, S, W["b_merge"], L)
    merged2 = merged.reshape(2 * L, D)
    out3 = _matmul(merged2, W["w_out"], BF, "proj_out", tm=tl, tn=1024).reshape(2, L, D)
    dxo, dout, acc_f = _final(out3, x, loss_target, gtab, W["norm_post"], L)

    dout2 = dout.reshape(2 * L, D)
    g = {}
    g["w_out"] = _matmul_tn(merged2, dout2, "dw_out", ta=1024, tn=1024, tr=tl)
    d_merged = _matmul(dout2, W["w_out"], BF, "d_merged", tm=tl, tn=1024, bt=True).reshape(2, L, D)
    dP, dS, dgp, acc_m = _merge_bwd(d_merged, proj3, P, S, W["b_merge"], L)
    dP2, dS2 = dP.reshape(2 * L, D), dS.reshape(2 * L, D)
    g["w_proj_pool"] = _matmul_tn(ypool2, dP2, "dw_proj_pool", ta=1024, tn=1024, tr=tl)
    g["w_proj_ssd"] = _matmul_tn(yn2, dS2, "dw_proj_ssd", ta=1024, tn=1024, tr=tl)
    d_ypool = _matmul(dP2, W["w_proj_pool"], BF, "d_ypool", tm=tl, tn=1024, bt=True).reshape(2, L, D)
    d_yn = _matmul(dS2, W["w_proj_ssd"], BF, "d_yn", tm=tl, tn=1024, bt=True).reshape(2, L, DIN)
    dv, dzp, g["pool_w"], acc_p = _pool_bwd(proj3, d_ypool, W["pool_w"], jnp.swapaxes(W["pool_w"], 1, 2),
                                            W["pool_scale"], tables, L)
    dy2, dzs, acc_s = _ssd_post_bwd(d_yn, y_f, y_b, xbc, proj3, dskip_e, W["ssd_norm"], L)
    dxs_f, dbc_f, ddt_f, dxs_b, dbc_b, ddt_b, acc_a = _ssd_bwd(xbc, dt_loc, a_loc, hs_f, hs_b, y_f, y_b, dy2, L)
    ident = lambda j: j
    dxr_xs, acc_cx = _conv_bwd(proj3, [dxs_f, dxs_b, dy2], [None, None, dskip_e], 0, DIN, [ident, ident, ident],
                               W["conv_w"], W["conv_b"], L, "conv_bwd_xs")
    bcmap = lambda j: 2 * lax.rem(j, NG) + j // NG
    dxr_bc, acc_cb = _conv_bwd(proj3, [dbc_f, dbc_b], [None, None], DIN, 2 * NG * NST, [bcmap, bcmap],
                               W["conv_w"], W["conv_b"], L, "conv_bwd_bc")
    ddtr, acc_d = _dt_bwd(dt_raw, bias128, from_loc(ddt_f), from_loc(ddt_b))
    pieces = [dv, dzp, dzs, dgp, dxr_xs, dxr_bc]
    dw_rows = [_matmul_tn(p.reshape(2 * R, p.shape[2]), hx2, "dw_in_%d" % i, ta=1024, tn=1024, tr=tr)
               for i, p in enumerate(pieces)]
    dw_rows.append(_matmul_tn(ddtr.reshape(2 * R, 128), hx2, "dw_in_dt", ta=128, tn=1024, tr=tr)[:64])
    g["w_in"] = jnp.concatenate(dw_rows, axis=0)
    dh = _dhx(pieces, ddtr, w_inT, w_dtT)
    grad_x, acc_n = _norm_mod_bwd(dh, x, ctx, tab, dxo)
    g["w_ada"], db_rows, sm_rows = _adaln_bwd(acc_n, acc_f, mod16, c16, npre, W["w_ada"])

    g["b_ada"] = db_rows[0:1]
    g["norm_pre"] = sm_rows[0:1]
    g["c_ctx"] = sm_rows[1]
    g["norm_post"] = acc_f[0, 1:2] + acc_f[1, 1:2]
    g["b_merge"] = acc_m[0, 0:1] + acc_m[1, 0:1]
    g["pool_scale"] = acc_p[:, 0, :].reshape(1, D)
    acc_c = jnp.concatenate([acc_cx[0] + acc_cx[1], acc_cb[0] + acc_cb[1]], axis=1)
    g["conv_w"] = acc_c[0:4]
    g["conv_b"] = acc_c[4:5]
    g["dt_bias"] = (acc_d[0, 0, :64] + acc_d[1, 0, :64]).reshape(2, 32)
    dA = (acc_a[0, :, 0, :16] + acc_a[1, :, 0, :16]).reshape(NG, 2, HPG).transpose(1, 0, 2)
    g["a_log"] = (dA * A).reshape(2, 32)
    g["d_skip"] = (acc_s[0, 1] + acc_s[1, 1]).reshape(32, HEAD).sum(axis=1).reshape(1, 32)
    g["ssd_norm"] = acc_s[0, 0:1] + acc_s[1, 0:1]
    loss_lanes = acc_f[:, 2, :]
    return loss_lanes, grad_x, g


MESH = pl.DeviceIdType.MESH
ANY = pl.BlockSpec(memory_space=pl.ANY)


def _all_gather(shard):
    m_per, n = shard.shape

    def body(x_ref, out_ref, send_sems, recv_sems, local_sem):
        x, y, c = lax.axis_index("x"), lax.axis_index("y"), lax.axis_index("c")
        me, sibling = (x, y, c), (x, y, 1 - c)
        chips = [(1 - x, y), (x, 1 - y), (1 - x, 1 - y)]

        def rows(px, py, pc):
            return out_ref.at[pl.ds((4 * px + 2 * py + pc) * m_per, m_per), :]

        def copy(k, block, to, src=None):
            return pltpu.make_async_remote_copy(
                src_ref=rows(*block) if src is None else src, dst_ref=rows(*block),
                send_sem=send_sems.at[k], recv_sem=recv_sems.at[k], device_id=to, device_id_type=MESH)

        mine = pltpu.make_async_copy(x_ref, rows(*me), local_sem)
        mine.start()
        first = [copy(0, me, sibling, src=x_ref)]
        first += [copy(1 + j, me, (*chip, c), src=x_ref) for j, chip in enumerate(chips)]
        for cp in first:
            cp.start()
        passed = [copy(4 + j, (*chip, c), sibling) for j, chip in enumerate(chips)]
        for j, chip in enumerate(chips):
            copy(1 + j, (*chip, c), me).wait_recv()
            passed[j].start()
        copy(0, sibling, me).wait_recv()
        for j, chip in enumerate(chips):
            copy(4 + j, (*chip, 1 - c), me).wait_recv()
        for cp in first + passed:
            cp.wait_send()
        mine.wait()

    return pl.pallas_call(
        body, name="all_gather_weights",
        out_shape=jax.ShapeDtypeStruct((NDEV * m_per, n), shard.dtype),
        in_specs=[ANY], out_specs=ANY,
        scratch_shapes=[pltpu.SemaphoreType.DMA((7,)), pltpu.SemaphoreType.DMA((7,)), pltpu.SemaphoreType.DMA],
    )(shard)


PAIR_PIECES = 4


def _xor_peer(k, x, y, c):
    return (1 - x if k & 4 else x, 1 - y if k & 2 else y, 1 - c if k & 1 else c)


def _pair_exchange(big, small):
    _, nq, rows, n = big.shape
    piece = rows // PAIR_PIECES
    assert piece * PAIR_PIECES == rows and piece % 16 == 0

    def body(big_ref, small_ref, got_ref, osmall_ref, send_sems, recv_sems, local_sem):
        x, y, c = lax.axis_index("x"), lax.axis_index("y"), lax.axis_index("c")
        me = 4 * x + 2 * y + c
        mine = pltpu.make_async_copy(small_ref, osmall_ref.at[me], local_sem)
        mine.start()

        def rc(src, dst, sem, peer):
            return pltpu.make_async_remote_copy(src_ref=src, dst_ref=dst, send_sem=send_sems.at[sem],
                                                recv_sem=recv_sems.at[sem], device_id=peer, device_id_type=MESH)

        sib = _xor_peer(1, x, y, c)
        sends, recvs = [], []
        for q in range(nq):
            for h in range(PAIR_PIECES):
                rws = pl.ds(h * piece, piece)
                cp = rc(big_ref.at[1 - c, q, rws], got_ref.at[q, rws], 8 + q * PAIR_PIECES + h, sib)
                sends.append(cp)
                recvs.append(cp)
        for k in range(1, NDEV):
            px, py, pc = _xor_peer(k, x, y, c)
            sends.append(rc(small_ref, osmall_ref.at[me], k, (px, py, pc)))
            recvs.append(rc(small_ref, osmall_ref.at[4 * px + 2 * py + pc], k, (px, py, pc)))
        for cp in sends:
            cp.start()
        for cp in sends:
            cp.wait_send()
        for cp in recvs:
            cp.wait_recv()
        mine.wait()

    nsem = 8 + nq * PAIR_PIECES
    return pl.pallas_call(
        body, name="grads_pair_exchange",
        out_shape=(jax.ShapeDtypeStruct(big.shape[1:], big.dtype), jax.ShapeDtypeStruct((NDEV,) + small.shape, small.dtype)),
        in_specs=[ANY, ANY], out_specs=(ANY, ANY),
        scratch_shapes=[pltpu.SemaphoreType.DMA((nsem,)), pltpu.SemaphoreType.DMA((nsem,)), pltpu.SemaphoreType.DMA],
    )(big, small)


def _pair_add(big, got):
    _, nq, rows, n = big.shape
    tile = rows // 4
    assert rows % 64 == 0

    def body(c_ref, a_ref, b_ref, o_ref):
        o_ref[0] = (a_ref[0, 0].astype(F32) + b_ref[0].astype(F32)).astype(BF)

    blk = pl.BlockSpec((1, tile, n), lambda q, i, c_ref: (q, i, 0))
    return pl.pallas_call(
        body, name="grads_pair_add",
        grid_spec=pltpu.PrefetchScalarGridSpec(
            num_scalar_prefetch=1, grid=(nq, rows // tile),
            in_specs=[pl.BlockSpec((1, 1, tile, n), lambda q, i, c_ref: (c_ref[0], q, i, 0)), blk], out_specs=blk),
        out_shape=jax.ShapeDtypeStruct(got.shape, BF), compiler_params=_params(("parallel", "parallel")),
    )(lax.axis_index("c").astype(jnp.int32).reshape(1), big, got)


def _chip_exchange(pair):
    def body(in_ref, out_ref, send_sems, recv_sems, local_sem):
        x, y, c = lax.axis_index("x"), lax.axis_index("y"), lax.axis_index("c")
        q = 2 * x + y
        mine = pltpu.make_async_copy(in_ref.at[q], out_ref.at[q], local_sem)
        mine.start()
        sends, recvs = [], []
        for j in range(1, 4):
            px, py, pc = _xor_peer(2 * j, x, y, c)
            pq = 2 * px + py
            for lst, dst in ((sends, out_ref.at[q]), (recvs, out_ref.at[pq])):
                lst.append(pltpu.make_async_remote_copy(
                    src_ref=in_ref.at[pq], dst_ref=dst, send_sem=send_sems.at[j - 1], recv_sem=recv_sems.at[j - 1],
                    device_id=(px, py, pc), device_id_type=MESH))
        for cp in sends:
            cp.start()
        for cp in sends:
            cp.wait_send()
        for cp in recvs:
            cp.wait_recv()
        mine.wait()

    return pl.pallas_call(
        body, name="grads_chip_exchange",
        out_shape=jax.ShapeDtypeStruct(pair.shape, pair.dtype), in_specs=[ANY], out_specs=ANY,
        scratch_shapes=[pltpu.SemaphoreType.DMA((3,)), pltpu.SemaphoreType.DMA((3,)), pltpu.SemaphoreType.DMA],
    )(pair)


ADAM_TILE = 64
PACK_W = 1024


def _adamw(recv, w, m, v, name):
    rp = w.shape[0]
    tile = min(ADAM_TILE, rp)
    nsrc = recv.shape[0]

    def body(r_ref, w_ref, m_ref, v_ref, g_ref, d_ref, nm_ref, nv_ref):
        g = r_ref[0].astype(F32)
        for i in range(1, nsrc):
            g = g + r_ref[i].astype(F32)
        m1 = ADAM_B1 * m_ref[...] + (1.0 - ADAM_B1) * g
        v1 = ADAM_B2 * v_ref[...] + (1.0 - ADAM_B2) * (g * g)
        m_hat = m1 / (1.0 - ADAM_B1 ** ADAM_STEP)
        v_hat = v1 / (1.0 - ADAM_B2 ** ADAM_STEP)
        g_ref[...] = g
        d_ref[...] = -ADAM_LR * (m_hat / (jnp.sqrt(v_hat) + ADAM_EPS) + ADAM_WD * w_ref[...])
        nm_ref[...] = m1
        nv_ref[...] = v1

    blk = pl.BlockSpec((tile, PACK_W), lambda i: (i, 0))
    shp = jax.ShapeDtypeStruct((rp, PACK_W), F32)
    return pl.pallas_call(
        body, name=name, grid=(rp // tile,),
        in_specs=[pl.BlockSpec((nsrc, tile, PACK_W), lambda i: (0, i, 0)), blk, blk, blk],
        out_specs=(blk, blk, blk, blk), out_shape=(shp, shp, shp, shp),
        compiler_params=_params(("parallel",)),
    )(recv, w, m, v)


BIG = {"w_ada": ((3 * D, D), 0), "pool_w": ((4, PGW, PGW), 1), "w_proj_pool": ((D, D), 0), "w_proj_ssd": ((DIN, D), 0),
       "w_out": ((D, D), 0), "w_in": ((IN_COLS, D), 0), "conv_w": ((4, CONV_DIM), 1)}
TRANSPOSED = ("w_ada", "w_in")
PACK_ROWS = {"w_ada": 384, "pool_w": 32, "w_proj_pool": 128, "w_proj_ssd": 256, "w_out": 128, "conv_w": 16, "w_in": 1168}
SMALL = {"c_ctx": (D,), "b_ada": (1, 3 * D), "norm_pre": (1, D), "norm_post": (1, D), "b_merge": (1, 2 * D),
         "pool_scale": (1, D), "conv_b": (1, CONV_DIM), "dt_bias": (2, 32), "a_log": (2, 32), "d_skip": (1, 32),
         "ssd_norm": (1, DIN)}
LOSS_SLOT = 128
BIG_ROWS = sum(PACK_ROWS.values())
assert BIG_ROWS % ADAM_TILE == 0
SMALL_ROWS = 16


def _shard_shape(name):
    shape, ax = BIG[name]
    return tuple(s // NDEV if i == ax else s for i, s in enumerate(shape))


def _as_rows(t, rows):
    pad = [(0, 0)] * (t.ndim - 1) + [(0, rows * PACK_W - t.shape[-1])]
    return jnp.pad(t, pad).reshape(t.shape[:-1] + (rows, PACK_W))


def _shard_rows(t, name):
    sh, r = _shard_shape(name), PACK_ROWS[name]
    lead = t.shape[:t.ndim - len(sh)]
    if len(sh) == 2 and sh[1] == PACK_W:
        return jnp.pad(t, [(0, 0)] * len(lead) + [(0, r - sh[0]), (0, 0)])
    if int(np.prod(sh)) == r * PACK_W:
        return t.reshape(lead + (r, PACK_W))
    return _as_rows(t.reshape(lead + (-1,)), r)


def _to_chunks(full, name):
    shape, ax = BIG[name]
    split = shape[:ax] + (NDEV, shape[ax] // NDEV) + shape[ax + 1:]
    return _shard_rows(jnp.moveaxis(full.reshape(split), ax, 0), name)


def _from_chunks(chunks, name):
    shape, ax = BIG[name]
    return jnp.moveaxis(chunks.reshape((NDEV,) + _shard_shape(name)), 0, ax).reshape(shape)


def _pack_state(t):
    big = jnp.concatenate([_shard_rows(t[n], n) for n in PACK_ROWS], axis=0)
    small = _as_rows(jnp.concatenate([t[n].reshape(-1) for n in SMALL] + [jnp.zeros((LOSS_SLOT,), F32)]), SMALL_ROWS)
    return big, small


def _pack_grads(g, loss_part):
    big = jnp.concatenate([_to_chunks(g[n], n).astype(BF) for n in PACK_ROWS], axis=1)
    big = jnp.swapaxes(big.reshape(4, 2, BIG_ROWS, PACK_W), 0, 1)
    small = [g[n].reshape(-1) for n in SMALL] + [jnp.zeros((LOSS_SLOT,), F32).at[0].set(loss_part)]
    return big, _as_rows(jnp.concatenate(small), SMALL_ROWS)


def _unpack_state(big, small):
    out, off = {}, 0
    for n, r in PACK_ROWS.items():
        sh = _shard_shape(n)
        k = int(np.prod(sh))
        if len(sh) == 2 and sh[1] == PACK_W:
            out[n] = big[off:off + sh[0]]
        else:
            out[n] = big[off:off + r].reshape(-1)[:k].reshape(sh)
        off += r
    flat, off = small.reshape(-1), 0
    for n, sh in SMALL.items():
        k = int(np.prod(sh))
        out[n] = flat[off:off + k].reshape(sh)
        off += k
    out["loss"] = flat[off]
    return out


def _pack_gather(w):
    conv = jnp.concatenate([p.reshape(-1) for p in _split(w["conv_w"], 3)])
    return jnp.concatenate([_as_rows(conv, PACK_ROWS[n]) if n == "conv_w" else _shard_rows(w[n], n).astype(BF)
                            for n in PACK_ROWS], axis=0)


def _unpack_gather(gathered):
    g = gathered.reshape(NDEV, BIG_ROWS, PACK_W)
    out, off = {}, 0
    for n, r in PACK_ROWS.items():
        sh = _shard_shape(n)
        if n == "conv_w":
            k = int(np.prod(sh))
            terms = g[:, off:off + r].reshape(NDEV, -1)[:, :3 * k].astype(F32).reshape(NDEV, 3, k)
            out[n] = _from_chunks(terms[:, 0] + terms[:, 1] + terms[:, 2], n)
        elif len(sh) == 2 and sh[1] == PACK_W:
            out[n] = _from_chunks(g[:, off:off + sh[0]], n)
        else:
            out[n] = _from_chunks(g[:, off:off + r], n)
        off += r
    return out


PARAMS = ["c_ctx", "w_ada", "b_ada", "norm_pre", "norm_post", "w_in", "b_merge", "pool_w", "pool_scale", "conv_w", "conv_b",
          "dt_bias", "a_log", "d_skip", "ssd_norm", "w_proj_pool", "w_proj_ssd", "w_out"]


def kernel(x, c, ctx, c_ctx, w_ada, b_ada, norm_pre, norm_post, w_in, b_merge, pool_w, pool_scale, conv_w, conv_b, dt_bias, a_log, d_skip, ssd_norm, w_proj_pool, w_proj_ssd, w_out, loss_target, m_c_ctx, m_w_ada, m_b_ada, m_norm_pre, m_norm_post, m_w_in, m_b_merge, m_pool_w, m_pool_scale, m_conv_w, m_conv_b, m_dt_bias, m_a_log, m_d_skip, m_ssd_norm, m_w_proj_pool, m_w_proj_ssd, m_w_out, v_c_ctx, v_w_ada, v_b_ada, v_norm_pre, v_norm_post, v_w_in, v_b_merge, v_pool_w, v_pool_scale, v_conv_w, v_conv_b, v_dt_bias, v_a_log, v_d_skip, v_ssd_norm, v_w_proj_pool, v_w_proj_ssd, v_w_out):
    given = dict(locals())
    shapes = {n: given[n].shape for n in PARAMS}

    def local(prefix):
        t = {n: (given[prefix + n] if n == "c_ctx" else given[prefix + n][0]) for n in PARAMS}
        for n in TRANSPOSED:
            t[n] = t[n].T
        return {n: t[n].reshape(_shard_shape(n) if n in BIG else SMALL[n]) for n in PARAMS}

    w, m, v = local(""), local("m_"), local("v_")

    W = _unpack_gather(_all_gather(_pack_gather(w)))
    for n in SMALL:
        W[n] = w[n]
    lanes, grad_x, g = _local_step(x, c, ctx, loss_target, W)
    gbig, gsmall = _pack_grads(g, (0.5 / D) * jnp.sum(lanes))
    got, recv_small = _pair_exchange(gbig, gsmall)
    recv_big = _chip_exchange(_pair_add(gbig, got))
    (wb, ws), (mb, ms), (vb, vs) = _pack_state(w), _pack_state(m), _pack_state(v)
    res = [_unpack_state(b, s) for b, s in zip(_adamw(recv_big, wb, mb, vb, "adamw_big"),
                                               _adamw(recv_small, ws, ms, vs, "adamw_small"))]
    outs = [res[0]["loss"], grad_x]
    for r in res:
        for n in TRANSPOSED:
            r[n] = r[n].T
        outs += [r[n].reshape(shapes[n]) for n in PARAMS]
    return tuple(outs)
```

```python
import functools

import numpy as np
import jax
import jax.numpy as jnp
from jax import lax
from jax.experimental import pallas as pl
from jax.experimental.pallas import tpu as pltpu

F32, BF = jnp.float32, jnp.bfloat16

D = 1024
GRID_W = 64
EPS = 1e-6
POOL_WINDOWS = (2, 4, 8, 16)
PGW = 256
DIN = 2048
HEAD = 64
NST = 128
NG = 4
HPG = 8
GWID = HPG * HEAD
Q = 128
CONV_DIM = 3072
OFF_GATE, OFF_XBC, OFF_DT, IN_COLS = 4096, 6144, 9216, 9280
NDEV = 8
ADAM_LR, ADAM_B1, ADAM_B2, ADAM_EPS, ADAM_WD, ADAM_STEP = 0.001, 0.9, 0.999, 1e-08, 0.01, 10

V7X_VMEM_LIMIT = 56 * 2 ** 20
ROW_TILE = 256


def _params(sem=None):
    return pltpu.CompilerParams(dimension_semantics=sem, vmem_limit_bytes=V7X_VMEM_LIMIT)


def _dot(a, b):
    return jnp.dot(a.astype(BF), b.astype(BF), preferred_element_type=F32)


def _dot_nt(a, b):
    return lax.dot_general(a.astype(BF), b.astype(BF), (((1,), (1,)), ((), ())), preferred_element_type=F32)


def _dot_tn(a, b):
    return lax.dot_general(a.astype(BF), b.astype(BF), (((0,), (0,)), ((), ())), preferred_element_type=F32)


def _split(a, n):
    parts = []
    for _ in range(n):
        p = a.astype(BF)
        parts.append(p)
        a = a - p.astype(F32)
    return parts


def _dot_sl(a, b01, n=3):
    return sum(jnp.dot(p, b01, preferred_element_type=F32) for p in _split(a, n))


def _dot_sr(a01, b, n=3):
    return sum(jnp.dot(a01, p, preferred_element_type=F32) for p in _split(b, n))


def _dot_tn_sl(a, b01, n=2):
    return sum(lax.dot_general(p, b01, (((0,), (0,)), ((), ())), preferred_element_type=F32) for p in _split(a, n))


def _sigmoid(x):
    return 1.0 / (1.0 + jnp.exp(-x))


def _matmul(a, b, out_dtype, name, tm=512, tn=512, tk=1024, bt=False, n=None):
    M, K = a.shape
    N = n if n is not None else (b.shape[0] if bt else b.shape[1])
    tm, tn, tk = min(tm, M), min(tn, N), min(tk, K)
    assert M % tm == 0 and N % tn == 0 and K % tk == 0, (a.shape, b.shape)
    nk = K // tk

    def body(a_ref, b_ref, o_ref, acc):
        k = pl.program_id(2)
        p = _dot_nt(a_ref[...], b_ref[...]) if bt else _dot(a_ref[...], b_ref[...])

        @pl.when(k == 0)
        def _():
            acc[...] = p

        @pl.when(k > 0)
        def _():
            acc[...] += p

        @pl.when(k == nk - 1)
        def _():
            o_ref[...] = acc[...].astype(o_ref.dtype)

    return pl.pallas_call(
        body, name=name, grid=(M // tm, N // tn, nk),
        in_specs=[pl.BlockSpec((tm, tk), lambda i, j, k: (i, k)),
                  pl.BlockSpec((tn, tk), lambda i, j, k: (j, k)) if bt else pl.BlockSpec((tk, tn), lambda i, j, k: (k, j))],
        out_specs=pl.BlockSpec((tm, tn), lambda i, j, k: (i, j)),
        out_shape=jax.ShapeDtypeStruct((M, N), out_dtype),
        scratch_shapes=[pltpu.VMEM((tm, tn), F32)],
        compiler_params=_params(("parallel", "parallel", "arbitrary")),
    )(a, b)


def _matmul_tn(a, g, name, ta=512, tn=512, tr=512):
    M, Ka = a.shape
    N = g.shape[1]
    ta, tn, tr = min(ta, Ka), min(tn, N), min(tr, M)
    assert M % tr == 0 and N % tn == 0 and Ka % ta == 0, (a.shape, g.shape)
    nr = M // tr

    def body(a_ref, g_ref, o_ref):
        k = pl.program_id(2)
        p = _dot_tn(a_ref[...], g_ref[...])

        @pl.when(k == 0)
        def _():
            o_ref[...] = p

        @pl.when(k > 0)
        def _():
            o_ref[...] += p

    return pl.pallas_call(
        body, name=name, grid=(Ka // ta, N // tn, nr),
        in_specs=[pl.BlockSpec((tr, ta), lambda i, j, k: (k, i)), pl.BlockSpec((tr, tn), lambda i, j, k: (k, j))],
        out_specs=pl.BlockSpec((ta, tn), lambda i, j, k: (i, j)),
        out_shape=jax.ShapeDtypeStruct((Ka, N), F32),
        compiler_params=_params(("parallel", "parallel", "arbitrary")),
    )(a, g)


def _dhx(pieces, ddt, w_inT, w_dtT):
    _, R, _ = pieces[0].shape
    tm = R // 4
    kb = 1024
    starts, nblk = [], []
    for p in pieces:
        starts.append(sum(nblk))
        nblk.append(p.shape[2] // kb)
    nk = sum(nblk)
    assert nk * kb == OFF_DT and R % 128 == 0
    npc = len(pieces)

    def body(*refs):
        a_refs, dt_ref, w_ref, wdt_ref, o_ref, acc = refs[:npc], refs[npc], refs[npc + 1], refs[npc + 2], refs[npc + 3], refs[npc + 4]
        k = pl.program_id(2)

        @pl.when(k == 0)
        def _():
            acc[...] = _dot(dt_ref[0], wdt_ref[...])

        for p in range(npc):
            @pl.when((k >= starts[p]) & (k < starts[p] + nblk[p]))
            def _(p=p):
                acc[...] += _dot(a_refs[p][0], w_ref[...])

        @pl.when(k == nk - 1)
        def _():
            o_ref[0] = acc[...]

    in_specs = [pl.BlockSpec((1, tm, kb), functools.partial(
        lambda e, t, k, s, nb: (e, t, jnp.clip(k - s, 0, nb - 1)), s=starts[p], nb=nblk[p])) for p in range(npc)]
    in_specs += [pl.BlockSpec((1, tm, 128), lambda e, t, k: (e, t, 0)),
                 pl.BlockSpec((kb, D), lambda e, t, k: (k, 0)),
                 pl.BlockSpec((128, D), lambda e, t, k: (0, 0))]
    return pl.pallas_call(
        body, name="d_hx", grid=(2, R // tm, nk), in_specs=in_specs,
        out_specs=pl.BlockSpec((1, tm, D), lambda e, t, k: (e, t, 0)),
        out_shape=jax.ShapeDtypeStruct((2, R, D), F32),
        scratch_shapes=[pltpu.VMEM((tm, D), F32)],
        compiler_params=_params(("parallel", "parallel", "arbitrary")),
    )(*pieces, ddt, w_inT, w_dtT)


def _adaln_fwd(c16, w_adaT_bf, b_ada):
    def body(c_ref, w_ref, b_ref, o_ref):
        cc = c_ref[...]
        o_ref[...] = _dot_nt(cc * _sigmoid(cc), w_ref[...]) + b_ref[...]

    return pl.pallas_call(body, name="adaln_fwd", out_shape=jax.ShapeDtypeStruct((16, 3 * D), F32),
                          compiler_params=_params())(c16, w_adaT_bf, b_ada)


def _adaln_bwd(acc_n, acc_f, mod16, c16, norm_pre, w_adaT_bf):
    def body(an_ref, af_ref, mod_ref, c_ref, np_ref, wt_ref, dw_ref, db_ref, sm_ref, dmod):
        npre = np_ref[...]
        dmod[...] = jnp.zeros_like(dmod)
        dnp = jnp.zeros((1, D), F32)
        dshift_c = jnp.zeros((1, D), F32)
        dgpre_c = jnp.zeros((1, D), F32)
        scale_c = mod_ref[2:3, D:2 * D]
        for e in range(2):
            dg_x, ds_x = an_ref[e, 0, 0:1, :], an_ref[e, 0, 1:2, :]
            dg_c, ds_c = an_ref[e, 1, 0:1, :], an_ref[e, 1, 1:2, :]
            dmod[e:e + 1, 0:D] = ds_x
            dmod[e:e + 1, D:2 * D] = dg_x * npre
            dmod[e:e + 1, 2 * D:3 * D] = af_ref[e, 0:1, :]
            dnp = dnp + dg_x * (1.0 + mod_ref[e:e + 1, D:2 * D]) + dg_c * (1.0 + scale_c)
            dshift_c = dshift_c + ds_c
            dgpre_c = dgpre_c + dg_c
        dmod[2:3, 0:D] = dshift_c
        dmod[2:3, D:2 * D] = dgpre_c * npre
        dm = dmod[...]
        cc = c_ref[...]
        sg = _sigmoid(cc)
        dw_ref[...] = _dot_tn(dm, cc * sg)
        db_ref[...] = jnp.zeros_like(db_ref)
        db_ref[0:1, :] = jnp.sum(dm, axis=0, keepdims=True)
        dsilu = sg * (1.0 + cc * (1.0 - sg))
        dcs = _dot(dm, wt_ref[...]) * dsilu
        sm_ref[...] = jnp.zeros_like(sm_ref)
        sm_ref[0:1, :] = dnp
        sm_ref[1:2, :] = dcs[2:3, :]

    return pl.pallas_call(
        body, name="adaln_bwd",
        out_shape=(jax.ShapeDtypeStruct((3 * D, D), F32), jax.ShapeDtypeStruct((16, 3 * D), F32),
                   jax.ShapeDtypeStruct((8, D), F32)),
        scratch_shapes=[pltpu.VMEM((16, 3 * D), F32)],
        compiler_params=_params())(acc_n, acc_f, mod16, c16, norm_pre, w_adaT_bf)


def _row_specs(L):
    nx = L // ROW_TILE
    return (pl.BlockSpec((1, ROW_TILE, D), lambda e, t: (e, jnp.minimum(t, nx - 1), 0)),
            pl.BlockSpec((1, ROW_TILE, D), lambda e, t: (e, jnp.maximum(t - nx, 0), 0)))


def _norm_mod_fwd(x, ctx, tab):
    L = x.shape[1]
    R = L + ctx.shape[1]
    nx = L // ROW_TILE

    def body(x_ref, c_ref, t_ref, o_ref):
        x = jnp.where(pl.program_id(1) < nx, x_ref[0], c_ref[0])
        r = lax.rsqrt(jnp.mean(x * x, axis=-1, keepdims=True) + EPS)
        t = t_ref[0, 0]
        o_ref[0] = (x * r * t[0:1] + t[1:2]).astype(BF)

    return pl.pallas_call(
        body, name="norm_mod_fwd", grid=(2, R // ROW_TILE),
        in_specs=[*_row_specs(L), pl.BlockSpec((1, 1, 8, D), lambda e, t: (e, t // nx, 0, 0))],
        out_specs=pl.BlockSpec((1, ROW_TILE, D), lambda e, t: (e, t, 0)),
        out_shape=jax.ShapeDtypeStruct((2, R, D), BF),
        compiler_params=_params(("parallel", "parallel")),
    )(x, ctx, tab)


def _norm_mod_bwd(dh, x, ctx, tab, dxo):
    L = x.shape[1]
    R = L + ctx.shape[1]
    nx = L // ROW_TILE

    def body(dh_ref, x_ref, c_ref, t_ref, dxo_ref, gx_ref, acc_ref):
        t = pl.program_id(1)
        x = jnp.where(t < nx, x_ref[0], c_ref[0])
        r = lax.rsqrt(jnp.mean(x * x, axis=-1, keepdims=True) + EPS)
        xn = x * r
        dh = dh_ref[0]

        @pl.when((t == 0) | (t == nx))
        def _():
            acc_ref[...] = jnp.zeros_like(acc_ref)

        acc_ref[0, 0, 0:1, :] += jnp.sum(dh * xn, axis=0, keepdims=True)
        acc_ref[0, 0, 1:2, :] += jnp.sum(dh, axis=0, keepdims=True)

        @pl.when(t < nx)
        def _():
            dxn = dh * t_ref[0, 0][0:1]
            dx = r * (dxn - xn * jnp.mean(dxn * xn, axis=-1, keepdims=True))
            gx_ref[0] = dxo_ref[0] + dx

    xspec, cspec = _row_specs(L)
    return pl.pallas_call(
        body, name="norm_mod_bwd", grid=(2, R // ROW_TILE),
        in_specs=[pl.BlockSpec((1, ROW_TILE, D), lambda e, t: (e, t, 0)), xspec, cspec,
                  pl.BlockSpec((1, 1, 8, D), lambda e, t: (e, t // nx, 0, 0)), xspec],
        out_specs=(xspec, pl.BlockSpec((1, 1, 8, D), lambda e, t: (e, t // nx, 0, 0))),
        out_shape=(jax.ShapeDtypeStruct((2, L, D), F32), jax.ShapeDtypeStruct((2, 2, 8, D), F32)),
        compiler_params=_params(("parallel", "arbitrary")),
    )(dh, x, ctx, tab, dxo)


POOL_TILE = 256


def _pool_tables(L):
    rows = L // GRID_W
    mats = np.zeros((4, POOL_TILE, POOL_TILE), np.float32)
    inv = np.zeros((4, L, 1), np.float32)
    for gi, k in enumerate(POOL_WINDOWS):
        lo, hi = k // 2, k - 1 - k // 2
        m = np.zeros((GRID_W, GRID_W), np.float32)
        for t in range(GRID_W):
            m[t, max(t - lo, 0):min(t + hi, GRID_W - 1) + 1] = 1.0
        for b in range(POOL_TILE // GRID_W):
            mats[gi, b * GRID_W:(b + 1) * GRID_W, b * GRID_W:(b + 1) * GRID_W] = m
        cnt_c = m.sum(1)
        cnt_r = np.array([min(r + hi, rows - 1) - max(r - lo, 0) + 1 for r in range(rows)], np.float32)
        inv[gi, :, 0] = (1.0 / (cnt_r[:, None] * cnt_c[None, :])).reshape(-1)
    matsT = np.ascontiguousarray(np.transpose(mats, (0, 2, 1)))
    return (jnp.asarray(mats, BF), jnp.asarray(matsT, BF), jnp.asarray(inv))


def _pool_cols(get_tile, mat, cs_ref, L, n):
    def step(i, carry):
        off = pl.multiple_of(i * POOL_TILE, POOL_TILE)
        cs_ref[pl.ds(GRID_W + off, POOL_TILE), :] = _dot_sr(mat, get_tile(off).astype(F32), n)
        return carry

    lax.fori_loop(0, L // POOL_TILE, step, 0)
    cs_ref[pl.ds(0, GRID_W), :] = jnp.zeros((GRID_W, PGW), F32)

    def prefix(r, carry):
        o = pl.multiple_of(r * GRID_W, GRID_W)
        cs_ref[pl.ds(o + GRID_W, GRID_W), :] = cs_ref[pl.ds(o + GRID_W, GRID_W), :] + cs_ref[pl.ds(o, GRID_W), :]
        return carry

    lax.fori_loop(0, L // GRID_W, prefix, 0)


def _pool_rows(cs_ref, off, below, above, L):
    rows = L // GRID_W
    r0 = off // GRID_W
    parts = []
    for i in range(POOL_TILE // GRID_W):
        hi = pl.multiple_of(jnp.minimum(r0 + i + above + 1, rows) * GRID_W, GRID_W)
        lo = pl.multiple_of(jnp.maximum(r0 + i - below, 0) * GRID_W, GRID_W)
        parts.append(cs_ref[pl.ds(hi, GRID_W), :] - cs_ref[pl.ds(lo, GRID_W), :])
    return jnp.concatenate(parts, axis=0)


def _pool_fwd(proj3, pool_w_bf, pool_scale, tables, L):
    mats, _, inv = tables
    nt = L // POOL_TILE

    def body(v_ref, z_ref, pw_ref, ps_ref, m_ref, inv_ref, o_ref, cs_ref):
        _pool_cols(lambda off: v_ref[0, pl.ds(off, POOL_TILE), :], m_ref[0], cs_ref, L, 1)
        half = lax.shift_left(1, pl.program_id(1))

        def step(i, carry):
            off = pl.multiple_of(i * POOL_TILE, POOL_TILE)
            rows = pl.ds(off, POOL_TILE)
            v = v_ref[0, rows, :].astype(F32)
            diff = _pool_rows(cs_ref, off, half, half - 1, L) * inv_ref[0, rows, :] - v
            yp = _dot(diff, pw_ref[0])
            z = z_ref[0, rows, :].astype(F32)
            o_ref[0, rows, :] = (yp * ps_ref[...] * (z * _sigmoid(z))).astype(BF)
            return carry

        lax.fori_loop(0, nt, step, 0)

    return pl.pallas_call(
        body, name="pool_fwd", grid=(2, 4),
        in_specs=[pl.BlockSpec((1, L, PGW), lambda e, g: (e, 0, g)),
                  pl.BlockSpec((1, L, PGW), lambda e, g: (e, 0, 4 + g)),
                  pl.BlockSpec((1, PGW, PGW), lambda e, g: (g, 0, 0)),
                  pl.BlockSpec((1, PGW), lambda e, g: (0, g)),
                  pl.BlockSpec((1, POOL_TILE, POOL_TILE), lambda e, g: (g, 0, 0)),
                  pl.BlockSpec((1, L, 1), lambda e, g: (g, 0, 0))],
        out_specs=pl.BlockSpec((1, L, PGW), lambda e, g: (e, 0, g)),
        out_shape=jax.ShapeDtypeStruct((2, L, D), BF),
        scratch_shapes=[pltpu.VMEM((L + GRID_W, PGW), F32)],
        compiler_params=_params(("parallel", "parallel")),
    )(proj3, proj3, pool_w_bf, pool_scale, mats, inv)


def _pool_bwd(proj3, d_ypool, pool_w_bf, pool_wT_bf, pool_scale, tables, L):
    mats, matsT, inv = tables
    nt = L // POOL_TILE
    R = proj3.shape[1]

    def body(v_ref, z_ref, dy_ref, pw_ref, pwt_ref, ps_ref, m_ref, mt_ref, inv_ref,
             dv_ref, dz_ref, dpw_ref, acc_ref, cs_ref, dd_ref):
        e = pl.program_id(1)

        @pl.when(e == 0)
        def _():
            dpw_ref[...] = jnp.zeros_like(dpw_ref)
            acc_ref[...] = jnp.zeros_like(acc_ref)

        _pool_cols(lambda off: v_ref[0, pl.ds(off, POOL_TILE), :], m_ref[0], cs_ref, L, 1)
        half = lax.shift_left(1, pl.program_id(0))
        ps = ps_ref[...]

        def step(i, carry):
            off = pl.multiple_of(i * POOL_TILE, POOL_TILE)
            rows = pl.ds(off, POOL_TILE)
            v = v_ref[0, rows, :].astype(F32)
            diff = _pool_rows(cs_ref, off, half, half - 1, L) * inv_ref[0, rows, :] - v
            yp = _dot(diff, pw_ref[0])
            z = z_ref[0, rows, :].astype(F32)
            sg = _sigmoid(z)
            sz = z * sg
            dy = dy_ref[0, rows, :].astype(F32)
            dz_ref[0, rows, :] = (dy * yp * ps * (sg * (1.0 + z * (1.0 - sg)))).astype(BF)
            dys = dy * sz
            acc_ref[0, 0:1, :] += jnp.sum(dys * yp, axis=0, keepdims=True)
            dyp = dys * ps
            dpw_ref[0] += _dot_tn(diff, dyp)
            dd_ref[rows, :] = _dot(dyp, pwt_ref[0])
            return carry

        lax.fori_loop(0, nt, step, 0)
        _pool_cols(lambda off: dd_ref[pl.ds(off, POOL_TILE), :] * inv_ref[0, pl.ds(off, POOL_TILE), :],
                   mt_ref[0], cs_ref, L, 2)

        def step2(i, carry):
            off = pl.multiple_of(i * POOL_TILE, POOL_TILE)
            rows = pl.ds(off, POOL_TILE)
            dv_ref[0, rows, :] = (_pool_rows(cs_ref, off, half - 1, half, L) - dd_ref[rows, :]).astype(BF)
            return carry

        lax.fori_loop(0, nt, step2, 0)
        dv_ref[0, pl.ds(L, R - L), :] = jnp.zeros((R - L, PGW), BF)
        dz_ref[0, pl.ds(L, R - L), :] = jnp.zeros((R - L, PGW), BF)

    return pl.pallas_call(
        body, name="pool_bwd", grid=(4, 2),
        in_specs=[pl.BlockSpec((1, L, PGW), lambda g, e: (e, 0, g)),
                  pl.BlockSpec((1, L, PGW), lambda g, e: (e, 0, 4 + g)),
                  pl.BlockSpec((1, L, PGW), lambda g, e: (e, 0, g)),
                  pl.BlockSpec((1, PGW, PGW), lambda g, e: (g, 0, 0)),
                  pl.BlockSpec((1, PGW, PGW), lambda g, e: (g, 0, 0)),
                  pl.BlockSpec((1, PGW), lambda g, e: (0, g)),
                  pl.BlockSpec((1, POOL_TILE, POOL_TILE), lambda g, e: (g, 0, 0)),
                  pl.BlockSpec((1, POOL_TILE, POOL_TILE), lambda g, e: (g, 0, 0)),
                  pl.BlockSpec((1, L, 1), lambda g, e: (g, 0, 0))],
        out_specs=(pl.BlockSpec((1, R, PGW), lambda g, e: (e, 0, g)),
                   pl.BlockSpec((1, R, PGW), lambda g, e: (e, 0, g)),
                   pl.BlockSpec((1, PGW, PGW), lambda g, e: (g, 0, 0)),
                   pl.BlockSpec((1, 8, PGW), lambda g, e: (g, 0, 0))),
        out_shape=(jax.ShapeDtypeStruct((2, R, D), BF), jax.ShapeDtypeStruct((2, R, D), BF),
                   jax.ShapeDtypeStruct((4, PGW, PGW), F32), jax.ShapeDtypeStruct((4, 8, PGW), F32)),
        scratch_shapes=[pltpu.VMEM((L + GRID_W, PGW), F32), pltpu.VMEM((L, PGW), F32)],
        compiler_params=_params(("parallel", "arbitrary")),
    )(proj3, proj3, d_ypool, pool_w_bf, pool_wT_bf, pool_scale, mats, matsT, inv)


CONV_BLOCK = 128


def _conv_tap(u, k, L):
    off = k - 2
    if off == 0:
        return u
    R = u.shape[0]
    r = lax.broadcasted_iota(jnp.int32, (R, 1), 0)
    pos = jnp.where(r < L, r, r - L) + off
    seg = jnp.where(r < L, L, R - L)
    return jnp.where((pos >= 0) & (pos < seg), pltpu.roll(u, (-off) % R, 0), 0.0)


def _conv_fwd(proj3, conv_w, conv_b, L):
    _, R, _ = proj3.shape
    cb0 = OFF_XBC // CONV_BLOCK

    def body(u_ref, w_ref, b_ref, o_ref):
        u = u_ref[0].astype(F32)
        w = w_ref[...]
        pre = b_ref[...] + sum(_conv_tap(u, k, L) * w[k:k + 1, :] for k in range(4))
        o_ref[0] = (pre * _sigmoid(pre)).astype(BF)

    return pl.pallas_call(
        body, name="conv_fwd", grid=(2, CONV_DIM // CONV_BLOCK),
        in_specs=[pl.BlockSpec((1, R, CONV_BLOCK), lambda e, j: (e, 0, cb0 + j)),
                  pl.BlockSpec((4, CONV_BLOCK), lambda e, j: (0, j)),
                  pl.BlockSpec((1, CONV_BLOCK), lambda e, j: (0, j))],
        out_specs=pl.BlockSpec((1, R, CONV_BLOCK), lambda e, j: (e, 0, j)),
        out_shape=jax.ShapeDtypeStruct((2, R, CONV_DIM), BF),
        compiler_params=_params(("parallel", "parallel")),
    )(proj3, conv_w, conv_b)


def _conv_bwd(proj3, addends, scales, col0, ncols, in_maps, conv_w, conv_b, L, name):
    _, R, _ = proj3.shape
    cb0 = (OFF_XBC + col0) // CONV_BLOCK
    wb0 = col0 // CONV_BLOCK
    na = len(addends)
    scaled = [i for i in range(na) if scales[i] is not None]

    def body(*refs):
        u_ref, w_ref, b_ref = refs[0], refs[1], refs[2]
        a_refs = refs[3:3 + na]
        s_refs = dict(zip(scaled, refs[3 + na:3 + na + len(scaled)]))
        o_ref, acc_ref = refs[3 + na + len(scaled)], refs[4 + na + len(scaled)]
        u = u_ref[0].astype(F32)
        w = w_ref[...]
        taps = [_conv_tap(u, k, L) for k in range(4)]
        pre = b_ref[...] + sum(taps[k] * w[k:k + 1, :] for k in range(4))
        sg = _sigmoid(pre)
        dxbc = jnp.zeros(u.shape, F32)
        for i, a in enumerate(a_refs):
            t = a[0].astype(F32)
            dxbc = dxbc + (t * s_refs[i][...] if i in s_refs else t)
        dpre = dxbc * (sg * (1.0 + pre * (1.0 - sg)))
        acc_ref[...] = jnp.zeros_like(acc_ref)
        for k in range(4):
            acc_ref[0, k:k + 1, :] = jnp.sum(dpre * taps[k], axis=0, keepdims=True)
        acc_ref[0, 4:5, :] = jnp.sum(dpre, axis=0, keepdims=True)
        du = sum(_conv_tap(dpre, 4 - k, L) * w[k:k + 1, :] for k in range(4))
        o_ref[0] = du.astype(BF)

    in_specs = [pl.BlockSpec((1, R, CONV_BLOCK), lambda e, j: (e, 0, cb0 + j)),
                pl.BlockSpec((4, CONV_BLOCK), lambda e, j: (0, wb0 + j)),
                pl.BlockSpec((1, CONV_BLOCK), lambda e, j: (0, wb0 + j))]
    for m in in_maps:
        in_specs.append(pl.BlockSpec((1, R, CONV_BLOCK), functools.partial(lambda e, j, m: (e, 0, m(j)), m=m)))
    for i in scaled:
        in_specs.append(pl.BlockSpec((1, CONV_BLOCK), functools.partial(lambda e, j, m: (0, m(j)), m=in_maps[i])))
    return pl.pallas_call(
        body, name=name, grid=(2, ncols // CONV_BLOCK),
        in_specs=in_specs,
        out_specs=(pl.BlockSpec((1, R, CONV_BLOCK), lambda e, j: (e, 0, j)),
                   pl.BlockSpec((1, 8, CONV_BLOCK), lambda e, j: (e, 0, j))),
        out_shape=(jax.ShapeDtypeStruct((2, R, ncols), BF), jax.ShapeDtypeStruct((2, 8, ncols), F32)),
        compiler_params=_params(("parallel", "parallel")),
    )(proj3, conv_w, conv_b, *addends, *[scales[i] for i in scaled])


def _softplus(x):
    e = jnp.exp(-jnp.abs(x))
    u = 1.0 + e
    return jnp.maximum(x, 0.0) + jnp.where(u == 1.0, e, e * jnp.log(u) / (u - 1.0))


def _to_local_mat(g, transpose=False):
    r = lax.broadcasted_iota(jnp.int32, (128, 128), 1 if transpose else 0)
    c = lax.broadcasted_iota(jnp.int32, (128, 128), 0 if transpose else 1)
    return ((c < 2 * HPG) & (r == jnp.right_shift(c, 3) * (NG * HPG) + g * HPG + (c & (HPG - 1)))).astype(BF)


def _dt_fwd(dt_raw, bias128):
    _, R, _ = dt_raw.shape

    def body(x_ref, b_ref, o_ref):
        dt = _softplus(x_ref[0] + b_ref[...])
        for g in range(NG):
            o_ref[0, g] = _dot_sl(dt, _to_local_mat(g))

    tr = R // 4
    return pl.pallas_call(
        body, name="dt_fwd", grid=(2, 4),
        in_specs=[pl.BlockSpec((1, tr, 128), lambda e, t: (e, t, 0)), pl.BlockSpec((1, 128), lambda e, t: (0, 0))],
        out_specs=pl.BlockSpec((1, NG, tr, 128), lambda e, t: (e, 0, t, 0)),
        out_shape=jax.ShapeDtypeStruct((2, NG, R, 128), F32),
        compiler_params=_params(("parallel", "parallel")),
    )(dt_raw, bias128)


def _dt_bwd(dt_raw, bias128, ddt_f, ddt_b):
    _, R, _ = dt_raw.shape

    def body(x_ref, b_ref, f_ref, g_ref, o_ref, acc_ref):
        ddt = sum(_dot_sl(f_ref[0, g] + g_ref[0, g], _to_local_mat(g, transpose=True)) for g in range(NG))
        d = ddt * _sigmoid(x_ref[0] + b_ref[...])
        o_ref[0] = d.astype(BF)

        @pl.when(pl.program_id(1) == 0)
        def _():
            acc_ref[...] = jnp.zeros_like(acc_ref)

        acc_ref[0, 0:1, :] += jnp.sum(d, axis=0, keepdims=True)

    tr = R // 4
    blk = pl.BlockSpec((1, tr, 128), lambda e, t: (e, t, 0))
    loc = pl.BlockSpec((1, NG, tr, 128), lambda e, t: (e, 0, t, 0))
    return pl.pallas_call(
        body, name="dt_bwd", grid=(2, 4),
        in_specs=[blk, pl.BlockSpec((1, 128), lambda e, t: (0, 0)), loc, loc],
        out_specs=(blk, pl.BlockSpec((1, 8, 128), lambda e, t: (e, 0, 0))),
        out_shape=(jax.ShapeDtypeStruct(dt_raw.shape, BF), jax.ShapeDtypeStruct((2, 8, 128), F32)),
        compiler_params=_params(("parallel", "arbitrary")),
    )(dt_raw, bias128, ddt_f, ddt_b)


def _tri(d):
    i = lax.broadcasted_iota(jnp.int32, (Q, Q), 0)
    j = lax.broadcasted_iota(jnp.int32, (Q, Q), 1)
    return (i >= j) if d == 0 else (i <= j)


def _expand_mat(d):
    r = lax.broadcasted_iota(jnp.int32, (128, GWID), 0)
    c = lax.broadcasted_iota(jnp.int32, (128, GWID), 1)
    return (r == d * HPG + jnp.right_shift(c, 6)).astype(BF)


def _reduce_mat(d):
    r = lax.broadcasted_iota(jnp.int32, (GWID, 128), 0)
    c = lax.broadcasted_iota(jnp.int32, (GWID, 128), 1)
    return (c == d * HPG + jnp.right_shift(r, 6)).astype(BF)


def _ssd_chunk(d, dt, A, xs, B, C):
    mask = _tri(d)
    T = mask.astype(BF)
    Tt = _tri(1 - d).astype(BF)
    a = dt * A
    acs = _dot_sr(T, a)
    E = _expand_mat(d)
    dt_e = _dot_sl(dt, E, 2)
    acs_e = _dot_sl(acs, E, 2)
    alast_e = acs_e[Q - 1:Q, :] if d == 0 else acs_e[0:1, :]
    return dict(mask=mask, T=T, Tt=Tt, acs=acs, acsT=acs.T, dt_e=dt_e, acs_e=acs_e, lam=jnp.exp(acs_e),
                w=jnp.exp(alast_e - acs_e), decay=jnp.exp(alast_e), xt=xs * dt_e, CB=_dot_nt(C, B))


def _head_decay(q, d, hh):
    col = q["acs"][:, d * HPG + hh:d * HPG + hh + 1]
    row = q["acsT"][d * HPG + hh:d * HPG + hh + 1, :]
    return jnp.exp(jnp.where(q["mask"], col - row, -jnp.inf))


def _chunk_maps(NX, NS):
    cf = lambda s: lax.rem(s + NX, NS)
    cb = lambda s: NS - 1 - s
    return cf, cb


def _ssd_fwd(xbc, dt_loc, a_loc, L):
    _, R, _ = xbc.shape
    NX, NS = L // Q, R // Q
    cf, cb = _chunk_maps(NX, NS)

    def body(xs_f, b_f, c_f, dt_f, xs_b, b_b, c_b, dt_b, a_ref, y_f, hs_f, y_b, hs_b, hT):
        @pl.when(pl.program_id(2) == 0)
        def _():
            hT[...] = jnp.zeros_like(hT)

        A = a_ref[0, 0:1, :]
        lane = lax.broadcasted_iota(jnp.int32, (Q, 128), 1)
        for d, (xs_ref, b_ref, c_ref, dt_ref, y_ref, hs_ref) in enumerate(
                ((xs_f, b_f, c_f, dt_f, y_f, hs_f), (xs_b, b_b, c_b, dt_b, y_b, hs_b))):
            xs, B, C = xs_ref[0].astype(F32), b_ref[0], c_ref[0]
            q = _ssd_chunk(d, dt_ref[0, 0], A, xs, B, C)
            h = hT[d]
            hb = h.astype(BF)
            hs_ref[0, 0] = hb
            parts = []
            for pr in range(HPG // 2):
                xp = q["xt"][:, pr * 128:(pr + 1) * 128].astype(BF)
                r0 = _dot(q["CB"] * _head_decay(q, d, 2 * pr), xp)
                r1 = _dot(q["CB"] * _head_decay(q, d, 2 * pr + 1), xp)
                parts.append(jnp.where(lane < HEAD, r0, r1))
            y_ref[0] = jnp.concatenate(parts, axis=1) + _dot(C, hb) * q["lam"]
            hT[d] = q["decay"] * h + _dot_tn(B, q["xt"] * q["w"])

    def spec(shape, imap):
        return pl.BlockSpec(shape, imap)

    def ins(c):
        return [spec((1, Q, GWID), lambda e, g, s: (e, c(s), g)),
                spec((1, Q, NST), lambda e, g, s: (e, c(s), DIN // NST + g)),
                spec((1, Q, NST), lambda e, g, s: (e, c(s), DIN // NST + NG + g)),
                spec((1, 1, Q, 128), lambda e, g, s: (e, g, c(s), 0))]

    def outs(c):
        return [spec((1, Q, GWID), lambda e, g, s: (e, c(s), g)),
                spec((1, 1, NST, GWID), lambda e, g, s: (e, c(s), 0, g))]

    yshape = jax.ShapeDtypeStruct((2, R, DIN), F32)
    hshape = jax.ShapeDtypeStruct((2, NS, NST, DIN), BF)
    return pl.pallas_call(
        body, name="ssd_fwd", grid=(2, NG, NS),
        in_specs=ins(cf) + ins(cb) + [spec((1, 8, 128), lambda e, g, s: (g, 0, 0))],
        out_specs=tuple(outs(cf) + outs(cb)),
        out_shape=(yshape, hshape, yshape, hshape),
        scratch_shapes=[pltpu.VMEM((2, NST, GWID), F32)],
        compiler_params=_params(("parallel", "parallel", "arbitrary")),
    )(xbc, xbc, xbc, dt_loc, xbc, xbc, xbc, dt_loc, a_loc)


def _ssd_bwd(xbc, dt_loc, a_loc, hs_f, hs_b, y_f, y_b, dy, L):
    _, R, _ = xbc.shape
    NX, NS = L // Q, R // Q
    cf0, cb0 = _chunk_maps(NX, NS)
    cf = lambda sp: cf0(NS - 1 - sp)
    cb = lambda sp: cb0(NS - 1 - sp)

    def body(xs_f, b_f, c_f, dt_f, hs_f_, dy_f, y_f_, xs_b, b_b, c_b, dt_b, hs_b_, dy_b, y_b_, a_ref,
             dxs_f, dbc_f, ddt_f, dxs_b, dbc_b, ddt_b, da_ref, dhT):
        @pl.when(pl.program_id(2) == 0)
        def _():
            dhT[...] = jnp.zeros_like(dhT)
            da_ref[...] = jnp.zeros_like(da_ref)

        A = a_ref[0, 0:1, :]
        lane = lax.broadcasted_iota(jnp.int32, (Q, 128), 1)
        row = lax.broadcasted_iota(jnp.int32, (Q, 128), 0)
        for d, (xs_ref, b_ref, c_ref, dt_ref, hs_ref, dy_ref, y_ref, dxs_ref, dbc_ref, ddt_ref) in enumerate(
                ((xs_f, b_f, c_f, dt_f, hs_f_, dy_f, y_f_, dxs_f, dbc_f, ddt_f),
                 (xs_b, b_b, c_b, dt_b, hs_b_, dy_b, y_b_, dxs_b, dbc_b, ddt_b))):
            xs, B, C, dt = xs_ref[0].astype(F32), b_ref[0], c_ref[0], dt_ref[0, 0]
            q = _ssd_chunk(d, dt, A, xs, B, C)
            xt, lam, w, decay = q["xt"], q["lam"], q["w"], q["decay"]
            H = hs_ref[0, 0]
            dyv = dy_ref[0].astype(F32)
            dh = dhT[d]
            dZ = dyv * lam
            dC = _dot_nt(dZ, H)
            dH = _dot_tn(C, dZ)
            U = _dot(B, dh)
            xw = xt * w
            dxt = U * w
            dalast_e = (jnp.sum(U * xw, axis=0, keepdims=True)
                        + decay * jnp.sum(dh * H.astype(F32), axis=0, keepdims=True))
            dB = _dot_nt(xw, dh)
            dCB = jnp.zeros((Q, Q), F32)
            dxt_parts = []
            for pr in range(HPG // 2):
                xp = xt[:, pr * 128:(pr + 1) * 128]
                dyp = dyv[:, pr * 128:(pr + 1) * 128]
                dxp = jnp.zeros((Q, 128), F32)
                for h2 in range(2):
                    Lh = _head_decay(q, d, 2 * pr + h2)
                    dym = jnp.where((lane < HEAD) if h2 == 0 else (lane >= HEAD), dyp, 0.0)
                    dxp = dxp + _dot_tn(q["CB"] * Lh, dym)
                    dCB = dCB + _dot_nt(dym, xp) * Lh
                dxt_parts.append(dxp)
            dxt_diag = jnp.concatenate(dxt_parts, axis=1)
            dC = dC + _dot(dCB, B)
            dB = dB + _dot_tn(dCB, C)
            Rm = _reduce_mat(d)
            dacs = _dot_sl(dyv * y_ref[0] - xt.astype(BF).astype(F32) * dxt_diag - U * xw, Rm, 2)
            dxt = dxt + dxt_diag
            dal = _dot_sl(jnp.broadcast_to(dalast_e, (8, GWID)), Rm, 2)[0:1, :]
            dacs = dacs + jnp.where(row == (Q - 1 if d == 0 else 0), dal, 0.0)
            da = _dot_sr(q["Tt"], dacs, 2)
            ddt_ref[0, 0] = da * A + _dot_sl(dxt * xs, Rm, 2)
            da_ref[0, 0, 0:1, :] += jnp.sum(da * dt, axis=0, keepdims=True)
            dxs_ref[0] = (dxt * q["dt_e"]).astype(BF)
            dbc_ref[0] = jnp.concatenate([dB, dC], axis=1).astype(BF)
            dhT[d] = decay * dh + dH

    def spec(shape, imap):
        return pl.BlockSpec(shape, imap)

    def ins(c):
        return [spec((1, Q, GWID), lambda e, g, s: (e, c(s), g)),
                spec((1, Q, NST), lambda e, g, s: (e, c(s), DIN // NST + g)),
                spec((1, Q, NST), lambda e, g, s: (e, c(s), DIN // NST + NG + g)),
                spec((1, 1, Q, 128), lambda e, g, s: (e, g, c(s), 0)),
                spec((1, 1, NST, GWID), lambda e, g, s: (e, c(s), 0, g)),
                spec((1, Q, GWID), lambda e, g, s: (e, c(s), g)),
                spec((1, Q, GWID), lambda e, g, s: (e, c(s), g))]

    def outs(c):
        return [spec((1, Q, GWID), lambda e, g, s: (e, c(s), g)),
                spec((1, Q, 2 * NST), lambda e, g, s: (e, c(s), g)),
                spec((1, 1, Q, 128), lambda e, g, s: (e, g, c(s), 0))]

    s_xs = jax.ShapeDtypeStruct((2, R, DIN), BF)
    s_bc = jax.ShapeDtypeStruct((2, R, 2 * NG * NST), BF)
    s_dt = jax.ShapeDtypeStruct((2, NG, R, 128), F32)
    return pl.pallas_call(
        body, name="ssd_bwd", grid=(2, NG, NS),
        in_specs=ins(cf) + ins(cb) + [spec((1, 8, 128), lambda e, g, s: (g, 0, 0))],
        out_specs=tuple(outs(cf) + outs(cb) + [spec((1, 1, 8, 128), lambda e, g, s: (e, g, 0, 0))]),
        out_shape=(s_xs, s_bc, s_dt, s_xs, s_bc, s_dt, jax.ShapeDtypeStruct((2, NG, 8, 128), F32)),
        scratch_shapes=[pltpu.VMEM((2, NST, GWID), F32)],
        compiler_params=_params(("parallel", "parallel", "arbitrary")),
    )(xbc, xbc, xbc, dt_loc, hs_f, dy, y_f, xbc, xbc, xbc, dt_loc, hs_b, dy, y_b, a_loc)


def _ssd_post_fwd(y_f, y_b, xbc, proj3, dskip_e, ssd_norm, L):
    def body(yf_ref, yb_ref, xs_ref, z_ref, ds_ref, w_ref, o_ref):
        y2 = yf_ref[0] + yb_ref[0] + ds_ref[...] * xs_ref[0].astype(F32)
        z = z_ref[0].astype(F32)
        u = y2 * (z * _sigmoid(z))
        parts = []
        for g in range(NG):
            ug = u[:, g * GWID:(g + 1) * GWID]
            parts.append(ug * lax.rsqrt(jnp.mean(ug * ug, axis=-1, keepdims=True) + EPS))
        o_ref[0] = (jnp.concatenate(parts, axis=1) * w_ref[...]).astype(BF)

    blk = lambda c: pl.BlockSpec((1, ROW_TILE, DIN), lambda e, t: (e, t, c))
    vec = pl.BlockSpec((1, DIN), lambda e, t: (0, 0))
    return pl.pallas_call(
        body, name="ssd_post_fwd", grid=(2, L // ROW_TILE),
        in_specs=[blk(0), blk(0), blk(0), blk(1), vec, vec],
        out_specs=blk(0),
        out_shape=jax.ShapeDtypeStruct((2, L, DIN), BF),
        compiler_params=_params(("parallel", "parallel")),
    )(y_f, y_b, xbc, proj3, dskip_e, ssd_norm)


def _ssd_post_bwd(d_yn, y_f, y_b, xbc, proj3, dskip_e, ssd_norm, L):
    _, R, _ = y_f.shape
    nx = L // ROW_TILE

    def body(dyn_ref, yf_ref, yb_ref, xs_ref, z_ref, ds_ref, w_ref, dy_ref, dz_ref, acc_ref):
        t = pl.program_id(1)

        @pl.when(t == 0)
        def _():
            acc_ref[...] = jnp.zeros_like(acc_ref)

        @pl.when(t >= nx)
        def _():
            dy_ref[...] = jnp.zeros_like(dy_ref)
            dz_ref[...] = jnp.zeros_like(dz_ref)

        @pl.when(t < nx)
        def _():
            xs = xs_ref[0].astype(F32)
            y2 = yf_ref[0] + yb_ref[0] + ds_ref[...] * xs
            z = z_ref[0].astype(F32)
            sg = _sigmoid(z)
            sz = z * sg
            u = y2 * sz
            dyn = dyn_ref[0].astype(F32)
            dun = dyn * w_ref[...]
            uh_parts, du_parts = [], []
            for g in range(NG):
                sl = slice(g * GWID, (g + 1) * GWID)
                ug = u[:, sl]
                rg = lax.rsqrt(jnp.mean(ug * ug, axis=-1, keepdims=True) + EPS)
                uh = ug * rg
                dg = dun[:, sl]
                du_parts.append(rg * (dg - uh * jnp.mean(dg * uh, axis=-1, keepdims=True)))
                uh_parts.append(uh)
            du = jnp.concatenate(du_parts, axis=1)
            uh = jnp.concatenate(uh_parts, axis=1)
            dy2 = du * sz
            dy_ref[0] = dy2.astype(BF)
            dz_ref[0] = (du * y2 * (sg * (1.0 + z * (1.0 - sg)))).astype(BF)
            acc_ref[0, 0:1, :] += jnp.sum(dyn * uh, axis=0, keepdims=True)
            acc_ref[0, 1:2, :] += jnp.sum(dy2 * xs, axis=0, keepdims=True)

    xmap = lambda c: (lambda e, t: (e, jnp.minimum(t, nx - 1), c))
    blk = lambda c: pl.BlockSpec((1, ROW_TILE, DIN), xmap(c))
    oblk = pl.BlockSpec((1, ROW_TILE, DIN), lambda e, t: (e, t, 0))
    vec = pl.BlockSpec((1, DIN), lambda e, t: (0, 0))
    return pl.pallas_call(
        body, name="ssd_post_bwd", grid=(2, R // ROW_TILE),
        in_specs=[blk(0), blk(0), blk(0), blk(0), blk(1), vec, vec],
        out_specs=(oblk, oblk, pl.BlockSpec((1, 8, DIN), lambda e, t: (e, 0, 0))),
        out_shape=(jax.ShapeDtypeStruct((2, R, DIN), BF), jax.ShapeDtypeStruct((2, R, DIN), BF),
                   jax.ShapeDtypeStruct((2, 8, DIN), F32)),
        compiler_params=_params(("parallel", "arbitrary")),
    )(d_yn, y_f, y_b, xbc, proj3, dskip_e, ssd_norm)


def _merge_fwd(proj3, P, S, b_merge, L):
    def body(gp_ref, p_ref, s_ref, b_ref, o_ref):
        gt = _sigmoid(gp_ref[0].astype(F32) + b_ref[...])
        o_ref[0] = (gt[:, :D] * p_ref[0].astype(F32) + gt[:, D:] * s_ref[0].astype(F32)).astype(BF)

    blk = pl.BlockSpec((1, ROW_TILE, D), lambda e, t: (e, t, 0))
    return pl.pallas_call(
        body, name="merge_fwd", grid=(2, L // ROW_TILE),
        in_specs=[pl.BlockSpec((1, ROW_TILE, 2 * D), lambda e, t: (e, t, OFF_GATE // (2 * D))), blk, blk,
                  pl.BlockSpec((1, 2 * D), lambda e, t: (0, 0))],
        out_specs=blk, out_shape=jax.ShapeDtypeStruct((2, L, D), BF),
        compiler_params=_params(("parallel", "parallel")),
    )(proj3, P, S, b_merge)


def _merge_bwd(d_merged, proj3, P, S, b_merge, L):
    _, R, _ = proj3.shape
    nx = L // ROW_TILE

    def body(dm_ref, gp_ref, p_ref, s_ref, b_ref, dp_ref, ds_ref, dg_ref, acc_ref):
        t = pl.program_id(1)

        @pl.when(t == 0)
        def _():
            acc_ref[...] = jnp.zeros_like(acc_ref)

        @pl.when(t >= nx)
        def _():
            dg_ref[...] = jnp.zeros_like(dg_ref)

        @pl.when(t < nx)
        def _():
            gt = _sigmoid(gp_ref[0].astype(F32) + b_ref[...])
            dm = dm_ref[0].astype(F32)
            g1, g2 = gt[:, :D], gt[:, D:]
            dp_ref[0] = (dm * g1).astype(BF)
            ds_ref[0] = (dm * g2).astype(BF)
            dgp = jnp.concatenate([dm * p_ref[0].astype(F32) * g1 * (1.0 - g1),
                                   dm * s_ref[0].astype(F32) * g2 * (1.0 - g2)], axis=1)
            dg_ref[0] = dgp.astype(BF)
            acc_ref[0, 0:1, :] += jnp.sum(dgp, axis=0, keepdims=True)

    xmap = lambda e, t: (e, jnp.minimum(t, nx - 1), 0)
    blk = pl.BlockSpec((1, ROW_TILE, D), xmap)
    return pl.pallas_call(
        body, name="merge_bwd", grid=(2, R // ROW_TILE),
        in_specs=[blk, pl.BlockSpec((1, ROW_TILE, 2 * D), lambda e, t: (e, jnp.minimum(t, nx - 1), OFF_GATE // (2 * D))),
                  blk, blk, pl.BlockSpec((1, 2 * D), lambda e, t: (0, 0))],
        out_specs=(blk, blk, pl.BlockSpec((1, ROW_TILE, 2 * D), lambda e, t: (e, t, 0)),
                   pl.BlockSpec((1, 8, 2 * D), lambda e, t: (e, 0, 0))),
        out_shape=(jax.ShapeDtypeStruct((2, L, D), BF), jax.ShapeDtypeStruct((2, L, D), BF),
                   jax.ShapeDtypeStruct((2, R, 2 * D), BF), jax.ShapeDtypeStruct((2, 8, 2 * D), F32)),
        compiler_params=_params(("parallel", "arbitrary")),
    )(d_merged, proj3, P, S, b_merge)


def _final(out3, x, tgt, gtab, norm_post, L):
    def body(o_ref, x_ref, t_ref, g_ref, n_ref, dxo_ref, do_ref, acc_ref):
        @pl.when(pl.program_id(1) == 0)
        def _():
            acc_ref[...] = jnp.zeros_like(acc_ref)

        o = o_ref[0].astype(F32)
        gate = g_ref[0, 0:1, :]
        npost = n_ref[...]
        r2 = lax.rsqrt(jnp.mean(o * o, axis=-1, keepdims=True) + EPS)
        nh = o * r2
        on = nh * npost
        err = x_ref[0] + gate * on - t_ref[0]
        dxo = err * (1.0 / D)
        dxo_ref[0] = dxo
        dnh = dxo * gate * npost
        do_ref[0] = (r2 * (dnh - nh * jnp.mean(dnh * nh, axis=-1, keepdims=True))).astype(BF)
        acc_ref[0, 0:1, :] += jnp.sum(dxo * on, axis=0, keepdims=True)
        acc_ref[0, 1:2, :] += jnp.sum(dxo * gate * nh, axis=0, keepdims=True)
        acc_ref[0, 2:3, :] += jnp.sum(err * err, axis=0, keepdims=True)

    blk = pl.BlockSpec((1, ROW_TILE, D), lambda e, t: (e, t, 0))
    return pl.pallas_call(
        body, name="final", grid=(2, L // ROW_TILE),
        in_specs=[blk, blk, blk, pl.BlockSpec((1, 8, D), lambda e, t: (e, 0, 0)),
                  pl.BlockSpec((1, D), lambda e, t: (0, 0))],
        out_specs=(blk, blk, pl.BlockSpec((1, 8, D), lambda e, t: (e, 0, 0))),
        out_shape=(jax.ShapeDtypeStruct((2, L, D), F32), jax.ShapeDtypeStruct((2, L, D), BF),
                   jax.ShapeDtypeStruct((2, 8, D), F32)),
        compiler_params=_params(("parallel", "arbitrary")),
    )(out3, x, tgt, gtab, norm_post)


def _local_step(x, c, ctx, loss_target, W):
    nb, L, _ = x.shape
    LC = ctx.shape[1]
    R = L + LC
    assert nb == 2 and L % ROW_TILE == 0 and LC % Q == 0 and L % POOL_TILE == 0
    w_inT = W["w_in"]
    w_dtT = jnp.pad(w_inT[OFF_DT:], ((0, 64), (0, 0)))
    tables = _pool_tables(L)
    tr, tl = (2 * R) // 8, (2 * L) // 8

    c16 = jnp.zeros((16, D), F32).at[0:2].set(c).at[2].set(W["c_ctx"])
    mod16 = _adaln_fwd(c16, W["w_ada"], W["b_ada"])
    shift, scale, gate = mod16[:, :D], mod16[:, D:2 * D], mod16[:, 2 * D:]
    npre = W["norm_pre"]
    tab = jnp.zeros((2, 2, 8, D), F32)
    for e in range(2):
        tab = tab.at[e, 0, 0].set(npre[0] * (1.0 + scale[e])).at[e, 0, 1].set(shift[e])
        tab = tab.at[e, 1, 0].set(npre[0] * (1.0 + scale[2])).at[e, 1, 1].set(shift[2])
    gtab = jnp.zeros((2, 8, D), F32).at[:, 0].set(gate[0:2])

    hx = _norm_mod_fwd(x, ctx, tab)
    hx2 = hx.reshape(2 * R, D)
    proj3 = _matmul(hx2, w_inT, BF, "proj_main", tm=tr, tn=1024, bt=True, n=OFF_DT).reshape(2, R, OFF_DT)
    dt_raw = _matmul(hx2, w_dtT, F32, "proj_dt", tm=tr, bt=True).reshape(2, R, 128)
    ypool = _pool_fwd(proj3, W["pool_w"], W["pool_scale"], tables, L)
    xbc = _conv_fwd(proj3, W["conv_w"], W["conv_b"], L)
    bias128 = jnp.pad(W["dt_bias"].reshape(1, 64), ((0, 0), (0, 64)))
    dt_loc = _dt_fwd(dt_raw, bias128)
    A = -jnp.exp(W["a_log"].reshape(2, NG, HPG))
    a_loc = jnp.zeros((NG, 8, 128), F32).at[:, 0, :16].set(A.transpose(1, 0, 2).reshape(NG, 16))
    y_f, hs_f, y_b, hs_b = _ssd_fwd(xbc, dt_loc, a_loc, L)
    dskip_e = jnp.repeat(W["d_skip"].reshape(1, 32), HEAD, axis=1)
    yn = _ssd_post_fwd(y_f, y_b, xbc, proj3, dskip_e, W["ssd_norm"], L)
    ypool2, yn2 = ypool.reshape(2 * L, D), yn.reshape(2 * L, DIN)
    P = _matmul(ypool2, W["w_proj_pool"], BF, "proj_pool", tm=tl, tn=1024).reshape(2, L, D)
    S = _matmul(yn2, W["w_proj_ssd"], BF, "proj_ssd", tm=tl, tn=1024).reshape(2, L, D)
    merged = _merge_fwd(proj3, P, S, W["b_merge"], L)
    merged2 = merged.reshape(2 * L, D)
    out3 = _matmul(merged2, W["w_out"], BF, "proj_out", tm=tl, tn=1024).reshape(2, L, D)
    dxo, dout, acc_f = _final(out3, x, loss_target, gtab, W["norm_post"], L)

    dout2 = dout.reshape(2 * L, D)
    g = {}
    g["w_out"] = _matmul_tn(merged2, dout2, "dw_out", ta=1024, tn=1024, tr=tl)
    d_merged = _matmul(dout2, W["w_out"], BF, "d_merged", tm=tl, tn=1024, bt=True).reshape(2, L, D)
    dP, dS, dgp, acc_m = _merge_bwd(d_merged, proj3, P, S, W["b_merge"], L)
    dP2, dS2 = dP.reshape(2 * L, D), dS.reshape(2 * L, D)
    g["w_proj_pool"] = _matmul_tn(ypool2, dP2, "dw_proj_pool", ta=1024, tn=1024, tr=tl)
    g["w_proj_ssd"] = _matmul_tn(yn2, dS2, "dw_proj_ssd", ta=1024, tn=1024, tr=tl)
    d_ypool = _matmul(dP2, W["w_proj_pool"], BF, "d_ypool", tm=tl, tn=1024, bt=True).reshape(2, L, D)
    d_yn = _matmul(dS2, W["w_proj_ssd"], BF, "d_yn", tm=tl, tn=1024, bt=True).reshape(2, L, DIN)
    dv, dzp, g["pool_w"], acc_p = _pool_bwd(proj3, d_ypool, W["pool_w"], jnp.swapaxes(W["pool_w"], 1, 2),
                                            W["pool_scale"], tables, L)
    dy2, dzs, acc_s = _ssd_post_bwd(d_yn, y_f, y_b, xbc, proj3, dskip_e, W["ssd_norm"], L)
    dxs_f, dbc_f, ddt_f, dxs_b, dbc_b, ddt_b, acc_a = _ssd_bwd(xbc, dt_loc, a_loc, hs_f, hs_b, y_f, y_b, dy2, L)
    ident = lambda j: j
    dxr_xs, acc_cx = _conv_bwd(proj3, [dxs_f, dxs_b, dy2], [None, None, dskip_e], 0, DIN, [ident, ident, ident],
                               W["conv_w"], W["conv_b"], L, "conv_bwd_xs")
    bcmap = lambda j: 2 * lax.rem(j, NG) + j // NG
    dxr_bc, acc_cb = _conv_bwd(proj3, [dbc_f, dbc_b], [None, None], DIN, 2 * NG * NST, [bcmap, bcmap],
                               W["conv_w"], W["conv_b"], L, "conv_bwd_bc")
    ddtr, acc_d = _dt_bwd(dt_raw, bias128, ddt_f, ddt_b)
    pieces = [dv, dzp, dzs, dgp, dxr_xs, dxr_bc]
    dw_rows = [_matmul_tn(p.reshape(2 * R, p.shape[2]), hx2, "dw_in_%d" % i, ta=1024, tn=1024, tr=tr)
               for i, p in enumerate(pieces)]
    dw_rows.append(_matmul_tn(ddtr.reshape(2 * R, 128), hx2, "dw_in_dt", ta=128, tn=1024, tr=tr)[:64])
    g["w_in"] = jnp.concatenate(dw_rows, axis=0)
    dh = _dhx(pieces, ddtr, w_inT, w_dtT)
    grad_x, acc_n = _norm_mod_bwd(dh, x, ctx, tab, dxo)
    g["w_ada"], db_rows, sm_rows = _adaln_bwd(acc_n, acc_f, mod16, c16, npre, W["w_ada"])

    g["b_ada"] = db_rows[0:1]
    g["norm_pre"] = sm_rows[0:1]
    g["c_ctx"] = sm_rows[1]
    g["norm_post"] = acc_f[0, 1:2] + acc_f[1, 1:2]
    g["b_merge"] = acc_m[0, 0:1] + acc_m[1, 0:1]
    g["pool_scale"] = acc_p[:, 0, :].reshape(1, D)
    acc_c = jnp.concatenate([acc_cx[0] + acc_cx[1], acc_cb[0] + acc_cb[1]], axis=1)
    g["conv_w"] = acc_c[0:4]
    g["conv_b"] = acc_c[4:5]
    g["dt_bias"] = (acc_d[0, 0, :64] + acc_d[1, 0, :64]).reshape(2, 32)
    dA = (acc_a[0, :, 0, :16] + acc_a[1, :, 0, :16]).reshape(NG, 2, HPG).transpose(1, 0, 2)
    g["a_log"] = (dA * A).reshape(2, 32)
    g["d_skip"] = (acc_s[0, 1] + acc_s[1, 1]).reshape(32, HEAD).sum(axis=1).reshape(1, 32)
    g["ssd_norm"] = acc_s[0, 0:1] + acc_s[1, 0:1]
    loss_lanes = acc_f[:, 2, :]
    return loss_lanes, grad_x, g


MESH = pl.DeviceIdType.MESH
ANY = pl.BlockSpec(memory_space=pl.ANY)


def _all_gather(shard):
    m_per, n = shard.shape

    def body(x_ref, out_ref, send_sems, recv_sems, local_sem):
        x, y, c = lax.axis_index("x"), lax.axis_index("y"), lax.axis_index("c")
        me, sibling = (x, y, c), (x, y, 1 - c)
        chips = [(1 - x, y), (x, 1 - y), (1 - x, 1 - y)]

        def rows(px, py, pc):
            return out_ref.at[pl.ds((4 * px + 2 * py + pc) * m_per, m_per), :]

        def copy(k, block, to, src=None):
            return pltpu.make_async_remote_copy(
                src_ref=rows(*block) if src is None else src, dst_ref=rows(*block),
                send_sem=send_sems.at[k], recv_sem=recv_sems.at[k], device_id=to, device_id_type=MESH)

        mine = pltpu.make_async_copy(x_ref, rows(*me), local_sem)
        mine.start()
        first = [copy(0, me, sibling, src=x_ref)]
        first += [copy(1 + j, me, (*chip, c), src=x_ref) for j, chip in enumerate(chips)]
        for cp in first:
            cp.start()
        passed = [copy(4 + j, (*chip, c), sibling) for j, chip in enumerate(chips)]
        for j, chip in enumerate(chips):
            copy(1 + j, (*chip, c), me).wait_recv()
            passed[j].start()
        copy(0, sibling, me).wait_recv()
        for j, chip in enumerate(chips):
            copy(4 + j, (*chip, 1 - c), me).wait_recv()
        for cp in first + passed:
            cp.wait_send()
        mine.wait()

    return pl.pallas_call(
        body, name="all_gather_weights",
        out_shape=jax.ShapeDtypeStruct((NDEV * m_per, n), shard.dtype),
        in_specs=[ANY], out_specs=ANY,
        scratch_shapes=[pltpu.SemaphoreType.DMA((7,)), pltpu.SemaphoreType.DMA((7,)), pltpu.SemaphoreType.DMA],
    )(shard)


PAIR_PIECES = 4


def _xor_peer(k, x, y, c):
    return (1 - x if k & 4 else x, 1 - y if k & 2 else y, 1 - c if k & 1 else c)


def _pair_exchange(big, small):
    _, nq, rows, n = big.shape
    piece = rows // PAIR_PIECES
    assert piece * PAIR_PIECES == rows and piece % 16 == 0

    def body(big_ref, small_ref, got_ref, osmall_ref, send_sems, recv_sems, local_sem):
        x, y, c = lax.axis_index("x"), lax.axis_index("y"), lax.axis_index("c")
        me = 4 * x + 2 * y + c
        mine = pltpu.make_async_copy(small_ref, osmall_ref.at[me], local_sem)
        mine.start()

        def rc(src, dst, sem, peer):
            return pltpu.make_async_remote_copy(src_ref=src, dst_ref=dst, send_sem=send_sems.at[sem],
                                                recv_sem=recv_sems.at[sem], device_id=peer, device_id_type=MESH)

        sib = _xor_peer(1, x, y, c)
        sends, recvs = [], []
        for q in range(nq):
            for h in range(PAIR_PIECES):
                rws = pl.ds(h * piece, piece)
                cp = rc(big_ref.at[1 - c, q, rws], got_ref.at[q, rws], 8 + q * PAIR_PIECES + h, sib)
                sends.append(cp)
                recvs.append(cp)
        for k in range(1, NDEV):
            px, py, pc = _xor_peer(k, x, y, c)
            sends.append(rc(small_ref, osmall_ref.at[me], k, (px, py, pc)))
            recvs.append(rc(small_ref, osmall_ref.at[4 * px + 2 * py + pc], k, (px, py, pc)))
        for cp in sends:
            cp.start()
        for cp in sends:
            cp.wait_send()
        for cp in recvs:
            cp.wait_recv()
        mine.wait()

    nsem = 8 + nq * PAIR_PIECES
    return pl.pallas_call(
        body, name="grads_pair_exchange",
        out_shape=(jax.ShapeDtypeStruct(big.shape[1:], big.dtype), jax.ShapeDtypeStruct((NDEV,) + small.shape, small.dtype)),
        in_specs=[ANY, ANY], out_specs=(ANY, ANY),
        scratch_shapes=[pltpu.SemaphoreType.DMA((nsem,)), pltpu.SemaphoreType.DMA((nsem,)), pltpu.SemaphoreType.DMA],
    )(big, small)


def _pair_add(big, got):
    _, nq, rows, n = big.shape
    tile = rows // 4
    assert rows % 64 == 0

    def body(c_ref, a_ref, b_ref, o_ref):
        o_ref[0] = (a_ref[0, 0].astype(F32) + b_ref[0].astype(F32)).astype(BF)

    blk = pl.BlockSpec((1, tile, n), lambda q, i, c_ref: (q, i, 0))
    return pl.pallas_call(
        body, name="grads_pair_add",
        grid_spec=pltpu.PrefetchScalarGridSpec(
            num_scalar_prefetch=1, grid=(nq, rows // tile),
            in_specs=[pl.BlockSpec((1, 1, tile, n), lambda q, i, c_ref: (c_ref[0], q, i, 0)), blk], out_specs=blk),
        out_shape=jax.ShapeDtypeStruct(got.shape, BF), compiler_params=_params(("parallel", "parallel")),
    )(lax.axis_index("c").astype(jnp.int32).reshape(1), big, got)


def _chip_exchange(pair):
    def body(in_ref, out_ref, send_sems, recv_sems, local_sem):
        x, y, c = lax.axis_index("x"), lax.axis_index("y"), lax.axis_index("c")
        q = 2 * x + y
        mine = pltpu.make_async_copy(in_ref.at[q], out_ref.at[q], local_sem)
        mine.start()
        sends, recvs = [], []
        for j in range(1, 4):
            px, py, pc = _xor_peer(2 * j, x, y, c)
            pq = 2 * px + py
            for lst, dst in ((sends, out_ref.at[q]), (recvs, out_ref.at[pq])):
                lst.append(pltpu.make_async_remote_copy(
                    src_ref=in_ref.at[pq], dst_ref=dst, send_sem=send_sems.at[j - 1], recv_sem=recv_sems.at[j - 1],
                    device_id=(px, py, pc), device_id_type=MESH))
        for cp in sends:
            cp.start()
        for cp in sends:
            cp.wait_send()
        for cp in recvs:
            cp.wait_recv()
        mine.wait()

    return pl.pallas_call(
        body, name="grads_chip_exchange",
        out_shape=jax.ShapeDtypeStruct(pair.shape, pair.dtype), in_specs=[ANY], out_specs=ANY,
        scratch_shapes=[pltpu.SemaphoreType.DMA((3,)), pltpu.SemaphoreType.DMA((3,)), pltpu.SemaphoreType.DMA],
    )(pair)


ADAM_TILE = 64
PACK_W = 1024


def _adamw(recv, w, m, v, name):
    rp = w.shape[0]
    tile = min(ADAM_TILE, rp)
    nsrc = recv.shape[0]

    def body(r_ref, w_ref, m_ref, v_ref, g_ref, d_ref, nm_ref, nv_ref):
        g = r_ref[0].astype(F32)
        for i in range(1, nsrc):
            g = g + r_ref[i].astype(F32)
        m1 = ADAM_B1 * m_ref[...] + (1.0 - ADAM_B1) * g
        v1 = ADAM_B2 * v_ref[...] + (1.0 - ADAM_B2) * (g * g)
        m_hat = m1 / (1.0 - ADAM_B1 ** ADAM_STEP)
        v_hat = v1 / (1.0 - ADAM_B2 ** ADAM_STEP)
        g_ref[...] = g
        d_ref[...] = -ADAM_LR * (m_hat / (jnp.sqrt(v_hat) + ADAM_EPS) + ADAM_WD * w_ref[...])
        nm_ref[...] = m1
        nv_ref[...] = v1

    blk = pl.BlockSpec((tile, PACK_W), lambda i: (i, 0))
    shp = jax.ShapeDtypeStruct((rp, PACK_W), F32)
    return pl.pallas_call(
        body, name=name, grid=(rp // tile,),
        in_specs=[pl.BlockSpec((nsrc, tile, PACK_W), lambda i: (0, i, 0)), blk, blk, blk],
        out_specs=(blk, blk, blk, blk), out_shape=(shp, shp, shp, shp),
        compiler_params=_params(("parallel",)),
    )(recv, w, m, v)


BIG = {"w_ada": ((3 * D, D), 0), "pool_w": ((4, PGW, PGW), 1), "w_proj_pool": ((D, D), 0), "w_proj_ssd": ((DIN, D), 0),
       "w_out": ((D, D), 0), "w_in": ((IN_COLS, D), 0), "conv_w": ((4, CONV_DIM), 1)}
TRANSPOSED = ("w_ada", "w_in")
PACK_ROWS = {"w_ada": 384, "pool_w": 32, "w_proj_pool": 128, "w_proj_ssd": 256, "w_out": 128, "conv_w": 16, "w_in": 1168}
SMALL = {"c_ctx": (D,), "b_ada": (1, 3 * D), "norm_pre": (1, D), "norm_post": (1, D), "b_merge": (1, 2 * D),
         "pool_scale": (1, D), "conv_b": (1, CONV_DIM), "dt_bias": (2, 32), "a_log": (2, 32), "d_skip": (1, 32),
         "ssd_norm": (1, DIN)}
LOSS_SLOT = 128
BIG_ROWS = sum(PACK_ROWS.values())
assert BIG_ROWS % ADAM_TILE == 0
SMALL_ROWS = 16


def _shard_shape(name):
    shape, ax = BIG[name]
    return tuple(s // NDEV if i == ax else s for i, s in enumerate(shape))


def _as_rows(t, rows):
    pad = [(0, 0)] * (t.ndim - 1) + [(0, rows * PACK_W - t.shape[-1])]
    return jnp.pad(t, pad).reshape(t.shape[:-1] + (rows, PACK_W))


def _shard_rows(t, name):
    sh, r = _shard_shape(name), PACK_ROWS[name]
    lead = t.shape[:t.ndim - len(sh)]
    if len(sh) == 2 and sh[1] == PACK_W:
        return jnp.pad(t, [(0, 0)] * len(lead) + [(0, r - sh[0]), (0, 0)])
    if int(np.prod(sh)) == r * PACK_W:
        return t.reshape(lead + (r, PACK_W))
    return _as_rows(t.reshape(lead + (-1,)), r)


def _to_chunks(full, name):
    shape, ax = BIG[name]
    split = shape[:ax] + (NDEV, shape[ax] // NDEV) + shape[ax + 1:]
    return _shard_rows(jnp.moveaxis(full.reshape(split), ax, 0), name)


def _from_chunks(chunks, name):
    shape, ax = BIG[name]
    return jnp.moveaxis(chunks.reshape((NDEV,) + _shard_shape(name)), 0, ax).reshape(shape)


def _pack_state(t):
    big = jnp.concatenate([_shard_rows(t[n], n) for n in PACK_ROWS], axis=0)
    small = _as_rows(jnp.concatenate([t[n].reshape(-1) for n in SMALL] + [jnp.zeros((LOSS_SLOT,), F32)]), SMALL_ROWS)
    return big, small


def _pack_grads(g, loss_part):
    big = jnp.concatenate([_to_chunks(g[n], n).astype(BF) for n in PACK_ROWS], axis=1)
    big = jnp.swapaxes(big.reshape(4, 2, BIG_ROWS, PACK_W), 0, 1)
    small = [g[n].reshape(-1) for n in SMALL] + [jnp.zeros((LOSS_SLOT,), F32).at[0].set(loss_part)]
    return big, _as_rows(jnp.concatenate(small), SMALL_ROWS)


def _unpack_state(big, small):
    out, off = {}, 0
    for n, r in PACK_ROWS.items():
        sh = _shard_shape(n)
        k = int(np.prod(sh))
        if len(sh) == 2 and sh[1] == PACK_W:
            out[n] = big[off:off + sh[0]]
        else:
            out[n] = big[off:off + r].reshape(-1)[:k].reshape(sh)
        off += r
    flat, off = small.reshape(-1), 0
    for n, sh in SMALL.items():
        k = int(np.prod(sh))
        out[n] = flat[off:off + k].reshape(sh)
        off += k
    out["loss"] = flat[off]
    return out


def _pack_gather(w):
    conv = jnp.concatenate([p.reshape(-1) for p in _split(w["conv_w"], 3)])
    return jnp.concatenate([_as_rows(conv, PACK_ROWS[n]) if n == "conv_w" else _shard_rows(w[n], n).astype(BF)
                            for n in PACK_ROWS], axis=0)


def _unpack_gather(gathered):
    g = gathered.reshape(NDEV, BIG_ROWS, PACK_W)
    out, off = {}, 0
    for n, r in PACK_ROWS.items():
        sh = _shard_shape(n)
        if n == "conv_w":
            k = int(np.prod(sh))
            terms = g[:, off:off + r].reshape(NDEV, -1)[:, :3 * k].astype(F32).reshape(NDEV, 3, k)
            out[n] = _from_chunks(terms[:, 0] + terms[:, 1] + terms[:, 2], n)
        elif len(sh) == 2 and sh[1] == PACK_W:
            out[n] = _from_chunks(g[:, off:off + sh[0]], n)
        else:
            out[n] = _from_chunks(g[:, off:off + r], n)
        off += r
    return out


PARAMS = ["c_ctx", "w_ada", "b_ada", "norm_pre", "norm_post", "w_in", "b_merge", "pool_w", "pool_scale", "conv_w", "conv_b",
          "dt_bias", "a_log", "d_skip", "ssd_norm", "w_proj_pool", "w_proj_ssd", "w_out"]


def kernel(x, c, ctx, c_ctx, w_ada, b_ada, norm_pre, norm_post, w_in, b_merge, pool_w, pool_scale, conv_w, conv_b, dt_bias, a_log, d_skip, ssd_norm, w_proj_pool, w_proj_ssd, w_out, loss_target, m_c_ctx, m_w_ada, m_b_ada, m_norm_pre, m_norm_post, m_w_in, m_b_merge, m_pool_w, m_pool_scale, m_conv_w, m_conv_b, m_dt_bias, m_a_log, m_d_skip, m_ssd_norm, m_w_proj_pool, m_w_proj_ssd, m_w_out, v_c_ctx, v_w_ada, v_b_ada, v_norm_pre, v_norm_post, v_w_in, v_b_merge, v_pool_w, v_pool_scale, v_conv_w, v_conv_b, v_dt_bias, v_a_log, v_d_skip, v_ssd_norm, v_w_proj_pool, v_w_proj_ssd, v_w_out):
    given = dict(locals())
    shapes = {n: given[n].shape for n in PARAMS}

    def local(prefix):
        t = {n: (given[prefix + n] if n == "c_ctx" else given[prefix + n][0]) for n in PARAMS}
        for n in TRANSPOSED:
            t[n] = t[n].T
        return {n: t[n].reshape(_shard_shape(n) if n in BIG else SMALL[n]) for n in PARAMS}

    w, m, v = local(""), local("m_"), local("v_")

    W = _unpack_gather(_all_gather(_pack_gather(w)))
    for n in SMALL:
        W[n] = w[n]
    lanes, grad_x, g = _local_step(x, c, ctx, loss_target, W)
    gbig, gsmall = _pack_grads(g, (0.5 / D) * jnp.sum(lanes))
    got, recv_small = _pair_exchange(gbig, gsmall)
    recv_big = _chip_exchange(_pair_add(gbig, got))
    (wb, ws), (mb, ms), (vb, vs) = _pack_state(w), _pack_state(m), _pack_state(v)
    res = [_unpack_state(b, s) for b, s in zip(_adamw(recv_big, wb, mb, vb, "adamw_big"),
                                               _adamw(recv_small, ws, ms, vs, "adamw_small"))]
    outs = [res[0]["loss"], grad_x]
    for r in res:
        for n in TRANSPOSED:
            r[n] = r[n].T
        outs += [r[n].reshape(shapes[n]) for n in PARAMS]
    return tuple(outs)
```

```python
import functools

import numpy as np
import jax
import jax.numpy as jnp
from jax import lax
from jax.experimental import pallas as pl
from jax.experimental.pallas import tpu as pltpu

F32, BF = jnp.float32, jnp.bfloat16

D = 1024
GRID_W = 64
EPS = 1e-6
POOL_WINDOWS = (2, 4, 8, 16)
PGW = 256
DIN = 2048
HEAD = 64
NST = 128
NG = 4
HPG = 8
GWID = HPG * HEAD
Q = 128
CONV_DIM = 3072
OFF_GATE, OFF_XBC, OFF_DT, IN_COLS = 4096, 6144, 9216, 9280
NDEV = 8
ADAM_LR, ADAM_B1, ADAM_B2, ADAM_EPS, ADAM_WD, ADAM_STEP = 0.001, 0.9, 0.999, 1e-08, 0.01, 10

V7X_VMEM_LIMIT = 56 * 2 ** 20
ROW_TILE = 256


def _params(sem=None):
    return pltpu.CompilerParams(dimension_semantics=sem, vmem_limit_bytes=V7X_VMEM_LIMIT)


def _dot(a, b):
    return jnp.dot(a.astype(BF), b.astype(BF), preferred_element_type=F32)


def _dot_nt(a, b):
    return lax.dot_general(a.astype(BF), b.astype(BF), (((1,), (1,)), ((), ())), preferred_element_type=F32)


def _dot_tn(a, b):
    return lax.dot_general(a.astype(BF), b.astype(BF), (((0,), (0,)), ((), ())), preferred_element_type=F32)


def _split(a, n):
    parts = []
    for _ in range(n):
        p = a.astype(BF)
        parts.append(p)
        a = a - p.astype(F32)
    return parts


def _dot_sl(a, b01, n=3):
    return sum(jnp.dot(p, b01, preferred_element_type=F32) for p in _split(a, n))


def _dot_sr(a01, b, n=3):
    return sum(jnp.dot(a01, p, preferred_element_type=F32) for p in _split(b, n))


def _dot_tn_sl(a, b01, n=2):
    return sum(lax.dot_general(p, b01, (((0,), (0,)), ((), ())), preferred_element_type=F32) for p in _split(a, n))


def _sigmoid(x):
    return 1.0 / (1.0 + jnp.exp(-x))


class _SideCopies:
    NSEM = 8

    def __init__(self, inputs, out_shapes, make):
        self.inputs, self.out_shapes, self.make = list(inputs), list(out_shapes), make

    def scratch(self):
        return [pltpu.SemaphoreType.DMA((self.NSEM,)), pltpu.SemaphoreType.DMA((self.NSEM,)), pltpu.SemaphoreType.DMA]

    def start(self, grid, in_refs, out_refs, sems):
        @pl.when(functools.reduce(lambda p, q: p & q, [pl.program_id(i) == 0 for i in range(len(grid))]))
        def _():
            local, sends, _ = self.make(in_refs, out_refs, *sems, arrivals=False)
            for cp in local + sends:
                cp.start()

    def wait(self, grid, in_refs, out_refs, sems):
        @pl.when(functools.reduce(lambda p, q: p & q, [pl.program_id(i) == n - 1 for i, n in enumerate(grid)]))
        def _():
            local, sends, recvs = self.make(in_refs, out_refs, *sems)
            for cp in sends:
                cp.wait_send()
            for cp in recvs:
                cp.wait_recv()
            for cp in local:
                cp.wait()


def _matmul(a, b, out_dtype, name, tm=512, tn=512, tk=1024, bt=False, n=None, side=None):
    M, K = a.shape
    N = n if n is not None else (b.shape[0] if bt else b.shape[1])
    tm, tn, tk = min(tm, M), min(tn, N), min(tk, K)
    assert M % tm == 0 and N % tn == 0 and K % tk == 0, (a.shape, b.shape)
    nk = K // tk
    grid = (M // tm, N // tn, nk)
    n_si, n_so = (len(side.inputs), len(side.out_shapes)) if side else (0, 0)

    def body(*refs):
        a_ref, b_ref, o_ref = refs[0], refs[1], refs[2 + n_si]
        acc = refs[3 + n_si + n_so]
        side_refs = (refs[2:2 + n_si], refs[3 + n_si:3 + n_si + n_so], refs[4 + n_si + n_so:])
        if side:
            side.start(grid, *side_refs)
        k = pl.program_id(2)
        p = _dot_nt(a_ref[...], b_ref[...]) if bt else _dot(a_ref[...], b_ref[...])

        @pl.when(k == 0)
        def _():
            acc[...] = p

        @pl.when(k > 0)
        def _():
            acc[...] += p

        @pl.when(k == nk - 1)
        def _():
            o_ref[...] = acc[...].astype(o_ref.dtype)

        if side:
            side.wait(grid, *side_refs)

    out = pl.pallas_call(
        body, name=name, grid=grid,
        in_specs=[pl.BlockSpec((tm, tk), lambda i, j, k: (i, k)),
                  pl.BlockSpec((tn, tk), lambda i, j, k: (j, k)) if bt else pl.BlockSpec((tk, tn), lambda i, j, k: (k, j))]
        + [ANY] * n_si,
        out_specs=(pl.BlockSpec((tm, tn), lambda i, j, k: (i, j)),) + (ANY,) * n_so,
        out_shape=(jax.ShapeDtypeStruct((M, N), out_dtype),) + tuple(side.out_shapes if side else ()),
        scratch_shapes=[pltpu.VMEM((tm, tn), F32)] + (side.scratch() if side else []),
        compiler_params=_params(("arbitrary",) * 3 if side else ("parallel", "parallel", "arbitrary")),
    )(a, b, *(side.inputs if side else ()))
    return out if side else out[0]


def _matmul_tn(a, g, name, ta=512, tn=512, tr=512):
    M, Ka = a.shape
    N = g.shape[1]
    ta, tn, tr = min(ta, Ka), min(tn, N), min(tr, M)
    assert M % tr == 0 and N % tn == 0 and Ka % ta == 0, (a.shape, g.shape)
    nr = M // tr

    def body(a_ref, g_ref, o_ref):
        k = pl.program_id(2)
        p = _dot_tn(a_ref[...], g_ref[...])

        @pl.when(k == 0)
        def _():
            o_ref[...] = p

        @pl.when(k > 0)
        def _():
            o_ref[...] += p

    return pl.pallas_call(
        body, name=name, grid=(Ka // ta, N // tn, nr),
        in_specs=[pl.BlockSpec((tr, ta), lambda i, j, k: (k, i)), pl.BlockSpec((tr, tn), lambda i, j, k: (k, j))],
        out_specs=pl.BlockSpec((ta, tn), lambda i, j, k: (i, j)),
        out_shape=jax.ShapeDtypeStruct((Ka, N), F32),
        compiler_params=_params(("parallel", "parallel", "arbitrary")),
    )(a, g)


def _dhx(pieces, ddt, w_inT, w_dtT, side=None):
    _, R, _ = pieces[0].shape
    tm = R // 4
    kb = 1024
    starts, nblk = [], []
    for p in pieces:
        starts.append(sum(nblk))
        nblk.append(p.shape[2] // kb)
    nk = sum(nblk)
    assert nk * kb == OFF_DT and R % 128 == 0
    npc = len(pieces)
    grid = (2, R // tm, nk)
    n_si, n_so = (len(side.inputs), len(side.out_shapes)) if side else (0, 0)

    def body(*refs):
        a_refs, dt_ref, w_ref, wdt_ref = refs[:npc], refs[npc], refs[npc + 1], refs[npc + 2]
        o_ref, acc = refs[npc + 3 + n_si], refs[npc + 4 + n_si + n_so]
        side_refs = (refs[npc + 3:npc + 3 + n_si], refs[npc + 4 + n_si:npc + 4 + n_si + n_so], refs[npc + 5 + n_si + n_so:])
        if side:
            side.start(grid, *side_refs)
        k = pl.program_id(2)

        @pl.when(k == 0)
        def _():
            acc[...] = _dot(dt_ref[0], wdt_ref[...])

        for p in range(npc):
            @pl.when((k >= starts[p]) & (k < starts[p] + nblk[p]))
            def _(p=p):
                acc[...] += _dot(a_refs[p][0], w_ref[...])

        @pl.when(k == nk - 1)
        def _():
            o_ref[0] = acc[...]

        if side:
            side.wait(grid, *side_refs)

    in_specs = [pl.BlockSpec((1, tm, kb), functools.partial(
        lambda e, t, k, s, nb: (e, t, jnp.clip(k - s, 0, nb - 1)), s=starts[p], nb=nblk[p])) for p in range(npc)]
    in_specs += [pl.BlockSpec((1, tm, 128), lambda e, t, k: (e, t, 0)),
                 pl.BlockSpec((kb, D), lambda e, t, k: (k, 0)),
                 pl.BlockSpec((128, D), lambda e, t, k: (0, 0))]
    out = pl.pallas_call(
        body, name="d_hx", grid=grid, in_specs=in_specs + [ANY] * n_si,
        out_specs=(pl.BlockSpec((1, tm, D), lambda e, t, k: (e, t, 0)),) + (ANY,) * n_so,
        out_shape=(jax.ShapeDtypeStruct((2, R, D), F32),) + tuple(side.out_shapes if side else ()),
        scratch_shapes=[pltpu.VMEM((tm, D), F32)] + (side.scratch() if side else []),
        compiler_params=_params(("arbitrary",) * 3 if side else ("parallel", "parallel", "arbitrary")),
    )(*pieces, ddt, w_inT, w_dtT, *(side.inputs if side else ()))
    return out if side else out[0]


def _adaln_fwd(c16, w_adaT_bf, b_ada):
    def body(c_ref, w_ref, b_ref, o_ref):
        cc = c_ref[...]
        o_ref[...] = _dot_nt(cc * _sigmoid(cc), w_ref[...]) + b_ref[...]

    return pl.pallas_call(body, name="adaln_fwd", out_shape=jax.ShapeDtypeStruct((16, 3 * D), F32),
                          compiler_params=_params())(c16, w_adaT_bf, b_ada)


def _adaln_bwd(acc_n, acc_f, mod16, c16, norm_pre, w_adaT_bf):
    def body(an_ref, af_ref, mod_ref, c_ref, np_ref, wt_ref, dw_ref, db_ref, sm_ref, dmod):
        npre = np_ref[...]
        dmod[...] = jnp.zeros_like(dmod)
        dnp = jnp.zeros((1, D), F32)
        dshift_c = jnp.zeros((1, D), F32)
        dgpre_c = jnp.zeros((1, D), F32)
        scale_c = mod_ref[2:3, D:2 * D]
        for e in range(2):
            dg_x, ds_x = an_ref[e, 0, 0:1, :], an_ref[e, 0, 1:2, :]
            dg_c, ds_c = an_ref[e, 1, 0:1, :], an_ref[e, 1, 1:2, :]
            dmod[e:e + 1, 0:D] = ds_x
            dmod[e:e + 1, D:2 * D] = dg_x * npre
            dmod[e:e + 1, 2 * D:3 * D] = af_ref[e, 0:1, :]
            dnp = dnp + dg_x * (1.0 + mod_ref[e:e + 1, D:2 * D]) + dg_c * (1.0 + scale_c)
            dshift_c = dshift_c + ds_c
            dgpre_c = dgpre_c + dg_c
        dmod[2:3, 0:D] = dshift_c
        dmod[2:3, D:2 * D] = dgpre_c * npre
        dm = dmod[...]
        cc = c_ref[...]
        sg = _sigmoid(cc)
        dw_ref[...] = _dot_tn(dm, cc * sg)
        db_ref[...] = jnp.zeros_like(db_ref)
        db_ref[0:1, :] = jnp.sum(dm, axis=0, keepdims=True)
        dsilu = sg * (1.0 + cc * (1.0 - sg))
        dcs = _dot(dm, wt_ref[...]) * dsilu
        sm_ref[...] = jnp.zeros_like(sm_ref)
        sm_ref[0:1, :] = dnp
        sm_ref[1:2, :] = dcs[2:3, :]

    return pl.pallas_call(
        body, name="adaln_bwd",
        out_shape=(jax.ShapeDtypeStruct((3 * D, D), F32), jax.ShapeDtypeStruct((16, 3 * D), F32),
                   jax.ShapeDtypeStruct((8, D), F32)),
        scratch_shapes=[pltpu.VMEM((16, 3 * D), F32)],
        compiler_params=_params())(acc_n, acc_f, mod16, c16, norm_pre, w_adaT_bf)


def _row_specs(L):
    nx = L // ROW_TILE
    return (pl.BlockSpec((1, ROW_TILE, D), lambda e, t: (e, jnp.minimum(t, nx - 1), 0)),
            pl.BlockSpec((1, ROW_TILE, D), lambda e, t: (e, jnp.maximum(t - nx, 0), 0)))


def _norm_mod_fwd(x, ctx, tab):
    L = x.shape[1]
    R = L + ctx.shape[1]
    nx = L // ROW_TILE

    def body(x_ref, c_ref, t_ref, o_ref):
        x = jnp.where(pl.program_id(1) < nx, x_ref[0], c_ref[0])
        r = lax.rsqrt(jnp.mean(x * x, axis=-1, keepdims=True) + EPS)
        t = t_ref[0, 0]
        o_ref[0] = (x * r * t[0:1] + t[1:2]).astype(BF)

    return pl.pallas_call(
        body, name="norm_mod_fwd", grid=(2, R // ROW_TILE),
        in_specs=[*_row_specs(L), pl.BlockSpec((1, 1, 8, D), lambda e, t: (e, t // nx, 0, 0))],
        out_specs=pl.BlockSpec((1, ROW_TILE, D), lambda e, t: (e, t, 0)),
        out_shape=jax.ShapeDtypeStruct((2, R, D), BF),
        compiler_params=_params(("parallel", "parallel")),
    )(x, ctx, tab)


def _norm_mod_bwd(dh, x, ctx, tab, dxo):
    L = x.shape[1]
    R = L + ctx.shape[1]
    nx = L // ROW_TILE

    def body(dh_ref, x_ref, c_ref, t_ref, dxo_ref, gx_ref, acc_ref):
        t = pl.program_id(1)
        x = jnp.where(t < nx, x_ref[0], c_ref[0])
        r = lax.rsqrt(jnp.mean(x * x, axis=-1, keepdims=True) + EPS)
        xn = x * r
        dh = dh_ref[0]

        @pl.when((t == 0) | (t == nx))
        def _():
            acc_ref[...] = jnp.zeros_like(acc_ref)

        acc_ref[0, 0, 0:1, :] += jnp.sum(dh * xn, axis=0, keepdims=True)
        acc_ref[0, 0, 1:2, :] += jnp.sum(dh, axis=0, keepdims=True)

        @pl.when(t < nx)
        def _():
            dxn = dh * t_ref[0, 0][0:1]
            dx = r * (dxn - xn * jnp.mean(dxn * xn, axis=-1, keepdims=True))
            gx_ref[0] = dxo_ref[0] + dx

    xspec, cspec = _row_specs(L)
    return pl.pallas_call(
        body, name="norm_mod_bwd", grid=(2, R // ROW_TILE),
        in_specs=[pl.BlockSpec((1, ROW_TILE, D), lambda e, t: (e, t, 0)), xspec, cspec,
                  pl.BlockSpec((1, 1, 8, D), lambda e, t: (e, t // nx, 0, 0)), xspec],
        out_specs=(xspec, pl.BlockSpec((1, 1, 8, D), lambda e, t: (e, t // nx, 0, 0))),
        out_shape=(jax.ShapeDtypeStruct((2, L, D), F32), jax.ShapeDtypeStruct((2, 2, 8, D), F32)),
        compiler_params=_params(("parallel", "arbitrary")),
    )(dh, x, ctx, tab, dxo)


POOL_TILE = 256


def _pool_tables(L):
    rows = L // GRID_W
    mats = np.zeros((4, POOL_TILE, POOL_TILE), np.float32)
    inv = np.zeros((4, L, 1), np.float32)
    for gi, k in enumerate(POOL_WINDOWS):
        lo, hi = k // 2, k - 1 - k // 2
        m = np.zeros((GRID_W, GRID_W), np.float32)
        for t in range(GRID_W):
            m[t, max(t - lo, 0):min(t + hi, GRID_W - 1) + 1] = 1.0
        for b in range(POOL_TILE // GRID_W):
            mats[gi, b * GRID_W:(b + 1) * GRID_W, b * GRID_W:(b + 1) * GRID_W] = m
        cnt_c = m.sum(1)
        cnt_r = np.array([min(r + hi, rows - 1) - max(r - lo, 0) + 1 for r in range(rows)], np.float32)
        inv[gi, :, 0] = (1.0 / (cnt_r[:, None] * cnt_c[None, :])).reshape(-1)
    matsT = np.ascontiguousarray(np.transpose(mats, (0, 2, 1)))
    return (jnp.asarray(mats, BF), jnp.asarray(matsT, BF), jnp.asarray(inv))


def _pool_cols(get_tile, mat, cs_ref, L, n):
    def step(i, carry):
        off = pl.multiple_of(i * POOL_TILE, POOL_TILE)
        cs_ref[pl.ds(GRID_W + off, POOL_TILE), :] = _dot_sr(mat, get_tile(off).astype(F32), n)
        return carry

    lax.fori_loop(0, L // POOL_TILE, step, 0)
    cs_ref[pl.ds(0, GRID_W), :] = jnp.zeros((GRID_W, PGW), F32)

    def prefix(r, carry):
        o = pl.multiple_of(r * GRID_W, GRID_W)
        cs_ref[pl.ds(o + GRID_W, GRID_W), :] = cs_ref[pl.ds(o + GRID_W, GRID_W), :] + cs_ref[pl.ds(o, GRID_W), :]
        return carry

    lax.fori_loop(0, L // GRID_W, prefix, 0)


def _pool_rows(cs_ref, off, below, above, L):
    rows = L // GRID_W
    r0 = off // GRID_W
    parts = []
    for i in range(POOL_TILE // GRID_W):
        hi = pl.multiple_of(jnp.minimum(r0 + i + above + 1, rows) * GRID_W, GRID_W)
        lo = pl.multiple_of(jnp.maximum(r0 + i - below, 0) * GRID_W, GRID_W)
        parts.append(cs_ref[pl.ds(hi, GRID_W), :] - cs_ref[pl.ds(lo, GRID_W), :])
    return jnp.concatenate(parts, axis=0)


def _pool_fwd(proj3, pool_w_bf, pool_scale, tables, L):
    mats, _, inv = tables
    nt = L // POOL_TILE

    def body(v_ref, z_ref, pw_ref, ps_ref, m_ref, inv_ref, o_ref, cs_ref):
        _pool_cols(lambda off: v_ref[0, pl.ds(off, POOL_TILE), :], m_ref[0], cs_ref, L, 1)
        half = lax.shift_left(1, pl.program_id(1))

        def step(i, carry):
            off = pl.multiple_of(i * POOL_TILE, POOL_TILE)
            rows = pl.ds(off, POOL_TILE)
            v = v_ref[0, rows, :].astype(F32)
            diff = _pool_rows(cs_ref, off, half, half - 1, L) * inv_ref[0, rows, :] - v
            yp = _dot(diff, pw_ref[0])
            z = z_ref[0, rows, :].astype(F32)
            o_ref[0, rows, :] = (yp * ps_ref[...] * (z * _sigmoid(z))).astype(BF)
            return carry

        lax.fori_loop(0, nt, step, 0)

    return pl.pallas_call(
        body, name="pool_fwd", grid=(2, 4),
        in_specs=[pl.BlockSpec((1, L, PGW), lambda e, g: (e, 0, g)),
                  pl.BlockSpec((1, L, PGW), lambda e, g: (e, 0, 4 + g)),
                  pl.BlockSpec((1, PGW, PGW), lambda e, g: (g, 0, 0)),
                  pl.BlockSpec((1, PGW), lambda e, g: (0, g)),
                  pl.BlockSpec((1, POOL_TILE, POOL_TILE), lambda e, g: (g, 0, 0)),
                  pl.BlockSpec((1, L, 1), lambda e, g: (g, 0, 0))],
        out_specs=pl.BlockSpec((1, L, PGW), lambda e, g: (e, 0, g)),
        out_shape=jax.ShapeDtypeStruct((2, L, D), BF),
        scratch_shapes=[pltpu.VMEM((L + GRID_W, PGW), F32)],
        compiler_params=_params(("parallel", "parallel")),
    )(proj3, proj3, pool_w_bf, pool_scale, mats, inv)


def _pool_bwd(proj3, d_ypool, pool_w_bf, pool_wT_bf, pool_scale, tables, L):
    mats, matsT, inv = tables
    nt = L // POOL_TILE
    R = proj3.shape[1]

    def body(v_ref, z_ref, dy_ref, pw_ref, pwt_ref, ps_ref, m_ref, mt_ref, inv_ref,
             dv_ref, dz_ref, dpw_ref, acc_ref, cs_ref, dd_ref):
        e = pl.program_id(1)

        @pl.when(e == 0)
        def _():
            dpw_ref[...] = jnp.zeros_like(dpw_ref)
            acc_ref[...] = jnp.zeros_like(acc_ref)

        _pool_cols(lambda off: v_ref[0, pl.ds(off, POOL_TILE), :], m_ref[0], cs_ref, L, 1)
        half = lax.shift_left(1, pl.program_id(0))
        ps = ps_ref[...]

        def step(i, carry):
            off = pl.multiple_of(i * POOL_TILE, POOL_TILE)
            rows = pl.ds(off, POOL_TILE)
            v = v_ref[0, rows, :].astype(F32)
            diff = _pool_rows(cs_ref, off, half, half - 1, L) * inv_ref[0, rows, :] - v
            yp = _dot(diff, pw_ref[0])
            z = z_ref[0, rows, :].astype(F32)
            sg = _sigmoid(z)
            sz = z * sg
            dy = dy_ref[0, rows, :].astype(F32)
            dz_ref[0, rows, :] = (dy * yp * ps * (sg * (1.0 + z * (1.0 - sg)))).astype(BF)
            dys = dy * sz
            acc_ref[0, 0:1, :] += jnp.sum(dys * yp, axis=0, keepdims=True)
            dyp = dys * ps
            dpw_ref[0] += _dot_tn(diff, dyp)
            dd_ref[rows, :] = _dot(dyp, pwt_ref[0])
            return carry

        lax.fori_loop(0, nt, step, 0)
        _pool_cols(lambda off: dd_ref[pl.ds(off, POOL_TILE), :] * inv_ref[0, pl.ds(off, POOL_TILE), :],
                   mt_ref[0], cs_ref, L, 2)

        def step2(i, carry):
            off = pl.multiple_of(i * POOL_TILE, POOL_TILE)
            rows = pl.ds(off, POOL_TILE)
            dv_ref[0, rows, :] = (_pool_rows(cs_ref, off, half - 1, half, L) - dd_ref[rows, :]).astype(BF)
            return carry

        lax.fori_loop(0, nt, step2, 0)
        dv_ref[0, pl.ds(L, R - L), :] = jnp.zeros((R - L, PGW), BF)
        dz_ref[0, pl.ds(L, R - L), :] = jnp.zeros((R - L, PGW), BF)

    return pl.pallas_call(
        body, name="pool_bwd", grid=(4, 2),
        in_specs=[pl.BlockSpec((1, L, PGW), lambda g, e: (e, 0, g)),
                  pl.BlockSpec((1, L, PGW), lambda g, e: (e, 0, 4 + g)),
                  pl.BlockSpec((1, L, PGW), lambda g, e: (e, 0, g)),
                  pl.BlockSpec((1, PGW, PGW), lambda g, e: (g, 0, 0)),
                  pl.BlockSpec((1, PGW, PGW), lambda g, e: (g, 0, 0)),
                  pl.BlockSpec((1, PGW), lambda g, e: (0, g)),
                  pl.BlockSpec((1, POOL_TILE, POOL_TILE), lambda g, e: (g, 0, 0)),
                  pl.BlockSpec((1, POOL_TILE, POOL_TILE), lambda g, e: (g, 0, 0)),
                  pl.BlockSpec((1, L, 1), lambda g, e: (g, 0, 0))],
        out_specs=(pl.BlockSpec((1, R, PGW), lambda g, e: (e, 0, g)),
                   pl.BlockSpec((1, R, PGW), lambda g, e: (e, 0, g)),
                   pl.BlockSpec((1, PGW, PGW), lambda g, e: (g, 0, 0)),
                   pl.BlockSpec((1, 8, PGW), lambda g, e: (g, 0, 0))),
        out_shape=(jax.ShapeDtypeStruct((2, R, D), BF), jax.ShapeDtypeStruct((2, R, D), BF),
                   jax.ShapeDtypeStruct((4, PGW, PGW), F32), jax.ShapeDtypeStruct((4, 8, PGW), F32)),
        scratch_shapes=[pltpu.VMEM((L + GRID_W, PGW), F32), pltpu.VMEM((L, PGW), F32)],
        compiler_params=_params(("parallel", "arbitrary")),
    )(proj3, proj3, d_ypool, pool_w_bf, pool_wT_bf, pool_scale, mats, matsT, inv)


CONV_BLOCK = 128


def _conv_tap(u, k, L):
    off = k - 2
    if off == 0:
        return u
    R = u.shape[0]
    r = lax.broadcasted_iota(jnp.int32, (R, 1), 0)
    pos = jnp.where(r < L, r, r - L) + off
    seg = jnp.where(r < L, L, R - L)
    return jnp.where((pos >= 0) & (pos < seg), pltpu.roll(u, (-off) % R, 0), 0.0)


def _conv_fwd(proj3, conv_w, conv_b, L):
    _, R, _ = proj3.shape
    cb0 = OFF_XBC // CONV_BLOCK

    def body(u_ref, w_ref, b_ref, o_ref):
        u = u_ref[0].astype(F32)
        w = w_ref[...]
        pre = b_ref[...] + sum(_conv_tap(u, k, L) * w[k:k + 1, :] for k in range(4))
        o_ref[0] = (pre * _sigmoid(pre)).astype(BF)

    return pl.pallas_call(
        body, name="conv_fwd", grid=(2, CONV_DIM // CONV_BLOCK),
        in_specs=[pl.BlockSpec((1, R, CONV_BLOCK), lambda e, j: (e, 0, cb0 + j)),
                  pl.BlockSpec((4, CONV_BLOCK), lambda e, j: (0, j)),
                  pl.BlockSpec((1, CONV_BLOCK), lambda e, j: (0, j))],
        out_specs=pl.BlockSpec((1, R, CONV_BLOCK), lambda e, j: (e, 0, j)),
        out_shape=jax.ShapeDtypeStruct((2, R, CONV_DIM), BF),
        compiler_params=_params(("parallel", "parallel")),
    )(proj3, conv_w, conv_b)


def _conv_bwd(proj3, addends, scales, col0, ncols, in_maps, conv_w, conv_b, L, name):
    _, R, _ = proj3.shape
    cb0 = (OFF_XBC + col0) // CONV_BLOCK
    wb0 = col0 // CONV_BLOCK
    na = len(addends)
    scaled = [i for i in range(na) if scales[i] is not None]

    def body(*refs):
        u_ref, w_ref, b_ref = refs[0], refs[1], refs[2]
        a_refs = refs[3:3 + na]
        s_refs = dict(zip(scaled, refs[3 + na:3 + na + len(scaled)]))
        o_ref, acc_ref = refs[3 + na + len(scaled)], refs[4 + na + len(scaled)]
        u = u_ref[0].astype(F32)
        w = w_ref[...]
        taps = [_conv_tap(u, k, L) for k in range(4)]
        pre = b_ref[...] + sum(taps[k] * w[k:k + 1, :] for k in range(4))
        sg = _sigmoid(pre)
        dxbc = jnp.zeros(u.shape, F32)
        for i, a in enumerate(a_refs):
            t = a[0].astype(F32)
            dxbc = dxbc + (t * s_refs[i][...] if i in s_refs else t)
        dpre = dxbc * (sg * (1.0 + pre * (1.0 - sg)))
        acc_ref[...] = jnp.zeros_like(acc_ref)
        for k in range(4):
            acc_ref[0, k:k + 1, :] = jnp.sum(dpre * taps[k], axis=0, keepdims=True)
        acc_ref[0, 4:5, :] = jnp.sum(dpre, axis=0, keepdims=True)
        du = sum(_conv_tap(dpre, 4 - k, L) * w[k:k + 1, :] for k in range(4))
        o_ref[0] = du.astype(BF)

    in_specs = [pl.BlockSpec((1, R, CONV_BLOCK), lambda e, j: (e, 0, cb0 + j)),
                pl.BlockSpec((4, CONV_BLOCK), lambda e, j: (0, wb0 + j)),
                pl.BlockSpec((1, CONV_BLOCK), lambda e, j: (0, wb0 + j))]
    for m in in_maps:
        in_specs.append(pl.BlockSpec((1, R, CONV_BLOCK), functools.partial(lambda e, j, m: (e, 0, m(j)), m=m)))
    for i in scaled:
        in_specs.append(pl.BlockSpec((1, CONV_BLOCK), functools.partial(lambda e, j, m: (0, m(j)), m=in_maps[i])))
    return pl.pallas_call(
        body, name=name, grid=(2, ncols // CONV_BLOCK),
        in_specs=in_specs,
        out_specs=(pl.BlockSpec((1, R, CONV_BLOCK), lambda e, j: (e, 0, j)),
                   pl.BlockSpec((1, 8, CONV_BLOCK), lambda e, j: (e, 0, j))),
        out_shape=(jax.ShapeDtypeStruct((2, R, ncols), BF), jax.ShapeDtypeStruct((2, 8, ncols), F32)),
        compiler_params=_params(("parallel", "parallel")),
    )(proj3, conv_w, conv_b, *addends, *[scales[i] for i in scaled])


def _softplus(x):
    e = jnp.exp(-jnp.abs(x))
    u = 1.0 + e
    return jnp.maximum(x, 0.0) + jnp.where(u == 1.0, e, e * jnp.log(u) / (u - 1.0))


def _to_local_mat(g, transpose=False):
    r = lax.broadcasted_iota(jnp.int32, (128, 128), 1 if transpose else 0)
    c = lax.broadcasted_iota(jnp.int32, (128, 128), 0 if transpose else 1)
    return ((c < 2 * HPG) & (r == jnp.right_shift(c, 3) * (NG * HPG) + g * HPG + (c & (HPG - 1)))).astype(BF)


def _dt_fwd(dt_raw, bias128):
    _, R, _ = dt_raw.shape

    def body(x_ref, b_ref, o_ref):
        dt = _softplus(x_ref[0] + b_ref[...])
        for g in range(NG):
            o_ref[0, g] = _dot_sl(dt, _to_local_mat(g))

    tr = R // 4
    return pl.pallas_call(
        body, name="dt_fwd", grid=(2, 4),
        in_specs=[pl.BlockSpec((1, tr, 128), lambda e, t: (e, t, 0)), pl.BlockSpec((1, 128), lambda e, t: (0, 0))],
        out_specs=pl.BlockSpec((1, NG, tr, 128), lambda e, t: (e, 0, t, 0)),
        out_shape=jax.ShapeDtypeStruct((2, NG, R, 128), F32),
        compiler_params=_params(("parallel", "parallel")),
    )(dt_raw, bias128)


def _dt_bwd(dt_raw, bias128, ddt_f, ddt_b):
    _, R, _ = dt_raw.shape

    def body(x_ref, b_ref, f_ref, g_ref, o_ref, acc_ref):
        ddt = sum(_dot_sl(f_ref[0, g] + g_ref[0, g], _to_local_mat(g, transpose=True)) for g in range(NG))
        d = ddt * _sigmoid(x_ref[0] + b_ref[...])
        o_ref[0] = d.astype(BF)

        @pl.when(pl.program_id(1) == 0)
        def _():
            acc_ref[...] = jnp.zeros_like(acc_ref)

        acc_ref[0, 0:1, :] += jnp.sum(d, axis=0, keepdims=True)

    tr = R // 4
    blk = pl.BlockSpec((1, tr, 128), lambda e, t: (e, t, 0))
    loc = pl.BlockSpec((1, NG, tr, 128), lambda e, t: (e, 0, t, 0))
    return pl.pallas_call(
        body, name="dt_bwd", grid=(2, 4),
        in_specs=[blk, pl.BlockSpec((1, 128), lambda e, t: (0, 0)), loc, loc],
        out_specs=(blk, pl.BlockSpec((1, 8, 128), lambda e, t: (e, 0, 0))),
        out_shape=(jax.ShapeDtypeStruct(dt_raw.shape, BF), jax.ShapeDtypeStruct((2, 8, 128), F32)),
        compiler_params=_params(("parallel", "arbitrary")),
    )(dt_raw, bias128, ddt_f, ddt_b)


def _tri(d):
    i = lax.broadcasted_iota(jnp.int32, (Q, Q), 0)
    j = lax.broadcasted_iota(jnp.int32, (Q, Q), 1)
    return (i >= j) if d == 0 else (i <= j)


def _expand_mat(d):
    r = lax.broadcasted_iota(jnp.int32, (128, GWID), 0)
    c = lax.broadcasted_iota(jnp.int32, (128, GWID), 1)
    return (r == d * HPG + jnp.right_shift(c, 6)).astype(BF)


def _reduce_mat(d):
    r = lax.broadcasted_iota(jnp.int32, (GWID, 128), 0)
    c = lax.broadcasted_iota(jnp.int32, (GWID, 128), 1)
    return (c == d * HPG + jnp.right_shift(r, 6)).astype(BF)


def _ssd_chunk(d, dt, A, xs, B, C):
    mask = _tri(d)
    T = mask.astype(BF)
    Tt = _tri(1 - d).astype(BF)
    a = dt * A
    acs = _dot_sr(T, a)
    E = _expand_mat(d)
    dt_e = _dot_sl(dt, E, 2)
    acs_e = _dot_sl(acs, E, 2)
    alast_e = acs_e[Q - 1:Q, :] if d == 0 else acs_e[0:1, :]
    return dict(mask=mask, T=T, Tt=Tt, acs=acs, acsT=acs.T, dt_e=dt_e, acs_e=acs_e, lam=jnp.exp(acs_e),
                w=jnp.exp(alast_e - acs_e), decay=jnp.exp(alast_e), xt=xs * dt_e, CB=_dot_nt(C, B))


def _head_decay(q, d, hh):
    col = q["acs"][:, d * HPG + hh:d * HPG + hh + 1]
    row = q["acsT"][d * HPG + hh:d * HPG + hh + 1, :]
    return jnp.exp(jnp.where(q["mask"], col - row, -jnp.inf))


def _chunk_maps(NX, NS):
    cf = lambda s: lax.rem(s + NX, NS)
    cb = lambda s: NS - 1 - s
    return cf, cb


def _ssd_fwd(xbc, dt_loc, a_loc, L):
    _, R, _ = xbc.shape
    NX, NS = L // Q, R // Q
    cf, cb = _chunk_maps(NX, NS)

    def body(xs_f, b_f, c_f, dt_f, xs_b, b_b, c_b, dt_b, a_ref, y_f, hs_f, y_b, hs_b, hT):
        @pl.when(pl.program_id(2) == 0)
        def _():
            hT[...] = jnp.zeros_like(hT)

        A = a_ref[0, 0:1, :]
        lane = lax.broadcasted_iota(jnp.int32, (Q, 128), 1)
        for d, (xs_ref, b_ref, c_ref, dt_ref, y_ref, hs_ref) in enumerate(
                ((xs_f, b_f, c_f, dt_f, y_f, hs_f), (xs_b, b_b, c_b, dt_b, y_b, hs_b))):
            xs, B, C = xs_ref[0].astype(F32), b_ref[0], c_ref[0]
            q = _ssd_chunk(d, dt_ref[0, 0], A, xs, B, C)
            h = hT[d]
            hb = h.astype(BF)
            hs_ref[0, 0] = hb
            parts = []
            for pr in range(HPG // 2):
                xp = q["xt"][:, pr * 128:(pr + 1) * 128].astype(BF)
                r0 = _dot(q["CB"] * _head_decay(q, d, 2 * pr), xp)
                r1 = _dot(q["CB"] * _head_decay(q, d, 2 * pr + 1), xp)
                parts.append(jnp.where(lane < HEAD, r0, r1))
            y_ref[0] = jnp.concatenate(parts, axis=1) + _dot(C, hb) * q["lam"]
            hT[d] = q["decay"] * h + _dot_tn(B, q["xt"] * q["w"])

    def spec(shape, imap):
        return pl.BlockSpec(shape, imap)

    def ins(c):
        return [spec((1, Q, GWID), lambda e, g, s: (e, c(s), g)),
                spec((1, Q, NST), lambda e, g, s: (e, c(s), DIN // NST + g)),
                spec((1, Q, NST), lambda e, g, s: (e, c(s), DIN // NST + NG + g)),
                spec((1, 1, Q, 128), lambda e, g, s: (e, g, c(s), 0))]

    def outs(c):
        return [spec((1, Q, GWID), lambda e, g, s: (e, c(s), g)),
                spec((1, 1, NST, GWID), lambda e, g, s: (e, c(s), 0, g))]

    yshape = jax.ShapeDtypeStruct((2, R, DIN), F32)
    hshape = jax.ShapeDtypeStruct((2, NS, NST, DIN), BF)
    return pl.pallas_call(
        body, name="ssd_fwd", grid=(2, NG, NS),
        in_specs=ins(cf) + ins(cb) + [spec((1, 8, 128), lambda e, g, s: (g, 0, 0))],
        out_specs=tuple(outs(cf) + outs(cb)),
        out_shape=(yshape, hshape, yshape, hshape),
        scratch_shapes=[pltpu.VMEM((2, NST, GWID), F32)],
        compiler_params=_params(("parallel", "parallel", "arbitrary")),
    )(xbc, xbc, xbc, dt_loc, xbc, xbc, xbc, dt_loc, a_loc)


def _ssd_bwd(xbc, dt_loc, a_loc, hs_f, hs_b, y_f, y_b, dy, L):
    _, R, _ = xbc.shape
    NX, NS = L // Q, R // Q
    cf0, cb0 = _chunk_maps(NX, NS)
    cf = lambda sp: cf0(NS - 1 - sp)
    cb = lambda sp: cb0(NS - 1 - sp)

    def body(xs_f, b_f, c_f, dt_f, hs_f_, dy_f, y_f_, xs_b, b_b, c_b, dt_b, hs_b_, dy_b, y_b_, a_ref,
             dxs_f, dbc_f, ddt_f, dxs_b, dbc_b, ddt_b, da_ref, dhT):
        @pl.when(pl.program_id(2) == 0)
        def _():
            dhT[...] = jnp.zeros_like(dhT)
            da_ref[...] = jnp.zeros_like(da_ref)

        A = a_ref[0, 0:1, :]
        lane = lax.broadcasted_iota(jnp.int32, (Q, 128), 1)
        row = lax.broadcasted_iota(jnp.int32, (Q, 128), 0)
        for d, (xs_ref, b_ref, c_ref, dt_ref, hs_ref, dy_ref, y_ref, dxs_ref, dbc_ref, ddt_ref) in enumerate(
                ((xs_f, b_f, c_f, dt_f, hs_f_, dy_f, y_f_, dxs_f, dbc_f, ddt_f),
                 (xs_b, b_b, c_b, dt_b, hs_b_, dy_b, y_b_, dxs_b, dbc_b, ddt_b))):
            xs, B, C, dt = xs_ref[0].astype(F32), b_ref[0], c_ref[0], dt_ref[0, 0]
            q = _ssd_chunk(d, dt, A, xs, B, C)
            xt, lam, w, decay = q["xt"], q["lam"], q["w"], q["decay"]
            H = hs_ref[0, 0]
            dyv = dy_ref[0].astype(F32)
            dh = dhT[d]
            dZ = dyv * lam
            dC = _dot_nt(dZ, H)
            dH = _dot_tn(C, dZ)
            U = _dot(B, dh)
            xw = xt * w
            dxt = U * w
            dalast_e = (jnp.sum(U * xw, axis=0, keepdims=True)
                        + decay * jnp.sum(dh * H.astype(F32), axis=0, keepdims=True))
            dB = _dot_nt(xw, dh)
            dCB = jnp.zeros((Q, Q), F32)
            dxt_parts = []
            for pr in range(HPG // 2):
                xp = xt[:, pr * 128:(pr + 1) * 128]
                dyp = dyv[:, pr * 128:(pr + 1) * 128]
                dxp = jnp.zeros((Q, 128), F32)
                for h2 in range(2):
                    Lh = _head_decay(q, d, 2 * pr + h2)
                    dym = jnp.where((lane < HEAD) if h2 == 0 else (lane >= HEAD), dyp, 0.0)
                    dxp = dxp + _dot_tn(q["CB"] * Lh, dym)
                    dCB = dCB + _dot_nt(dym, xp) * Lh
                dxt_parts.append(dxp)
            dxt_diag = jnp.concatenate(dxt_parts, axis=1)
            dC = dC + _dot(dCB, B)
            dB = dB + _dot_tn(dCB, C)
            Rm = _reduce_mat(d)
            dacs = _dot_sl(dyv * y_ref[0] - xt.astype(BF).astype(F32) * dxt_diag - U * xw, Rm, 2)
            dxt = dxt + dxt_diag
            dal = _dot_sl(jnp.broadcast_to(dalast_e, (8, GWID)), Rm, 2)[0:1, :]
            dacs = dacs + jnp.where(row == (Q - 1 if d == 0 else 0), dal, 0.0)
            da = _dot_sr(q["Tt"], dacs, 2)
            ddt_ref[0, 0] = da * A + _dot_sl(dxt * xs, Rm, 2)
            da_ref[0, 0, 0:1, :] += jnp.sum(da * dt, axis=0, keepdims=True)
            dxs_ref[0] = (dxt * q["dt_e"]).astype(BF)
            dbc_ref[0] = jnp.concatenate([dB, dC], axis=1).astype(BF)
            dhT[d] = decay * dh + dH

    def spec(shape, imap):
        return pl.BlockSpec(shape, imap)

    def ins(c):
        return [spec((1, Q, GWID), lambda e, g, s: (e, c(s), g)),
                spec((1, Q, NST), lambda e, g, s: (e, c(s), DIN // NST + g)),
                spec((1, Q, NST), lambda e, g, s: (e, c(s), DIN // NST + NG + g)),
                spec((1, 1, Q, 128), lambda e, g, s: (e, g, c(s), 0)),
                spec((1, 1, NST, GWID), lambda e, g, s: (e, c(s), 0, g)),
                spec((1, Q, GWID), lambda e, g, s: (e, c(s), g)),
                spec((1, Q, GWID), lambda e, g, s: (e, c(s), g))]

    def outs(c):
        return [spec((1, Q, GWID), lambda e, g, s: (e, c(s), g)),
                spec((1, Q, 2 * NST), lambda e, g, s: (e, c(s), g)),
                spec((1, 1, Q, 128), lambda e, g, s: (e, g, c(s), 0))]

    s_xs = jax.ShapeDtypeStruct((2, R, DIN), BF)
    s_bc = jax.ShapeDtypeStruct((2, R, 2 * NG * NST), BF)
    s_dt = jax.ShapeDtypeStruct((2, NG, R, 128), F32)
    return pl.pallas_call(
        body, name="ssd_bwd", grid=(2, NG, NS),
        in_specs=ins(cf) + ins(cb) + [spec((1, 8, 128), lambda e, g, s: (g, 0, 0))],
        out_specs=tuple(outs(cf) + outs(cb) + [spec((1, 1, 8, 128), lambda e, g, s: (e, g, 0, 0))]),
        out_shape=(s_xs, s_bc, s_dt, s_xs, s_bc, s_dt, jax.ShapeDtypeStruct((2, NG, 8, 128), F32)),
        scratch_shapes=[pltpu.VMEM((2, NST, GWID), F32)],
        compiler_params=_params(("parallel", "parallel", "arbitrary")),
    )(xbc, xbc, xbc, dt_loc, hs_f, dy, y_f, xbc, xbc, xbc, dt_loc, hs_b, dy, y_b, a_loc)


def _ssd_post_fwd(y_f, y_b, xbc, proj3, dskip_e, ssd_norm, L):
    def body(yf_ref, yb_ref, xs_ref, z_ref, ds_ref, w_ref, o_ref):
        y2 = yf_ref[0] + yb_ref[0] + ds_ref[...] * xs_ref[0].astype(F32)
        z = z_ref[0].astype(F32)
        u = y2 * (z * _sigmoid(z))
        parts = []
        for g in range(NG):
            ug = u[:, g * GWID:(g + 1) * GWID]
            parts.append(ug * lax.rsqrt(jnp.mean(ug * ug, axis=-1, keepdims=True) + EPS))
        o_ref[0] = (jnp.concatenate(parts, axis=1) * w_ref[...]).astype(BF)

    blk = lambda c: pl.BlockSpec((1, ROW_TILE, DIN), lambda e, t: (e, t, c))
    vec = pl.BlockSpec((1, DIN), lambda e, t: (0, 0))
    return pl.pallas_call(
        body, name="ssd_post_fwd", grid=(2, L // ROW_TILE),
        in_specs=[blk(0), blk(0), blk(0), blk(1), vec, vec],
        out_specs=blk(0),
        out_shape=jax.ShapeDtypeStruct((2, L, DIN), BF),
        compiler_params=_params(("parallel", "parallel")),
    )(y_f, y_b, xbc, proj3, dskip_e, ssd_norm)


def _ssd_post_bwd(d_yn, y_f, y_b, xbc, proj3, dskip_e, ssd_norm, L):
    _, R, _ = y_f.shape
    nx = L // ROW_TILE

    def body(dyn_ref, yf_ref, yb_ref, xs_ref, z_ref, ds_ref, w_ref, dy_ref, dz_ref, acc_ref):
        t = pl.program_id(1)

        @pl.when(t == 0)
        def _():
            acc_ref[...] = jnp.zeros_like(acc_ref)

        @pl.when(t >= nx)
        def _():
            dy_ref[...] = jnp.zeros_like(dy_ref)
            dz_ref[...] = jnp.zeros_like(dz_ref)

        @pl.when(t < nx)
        def _():
            xs = xs_ref[0].astype(F32)
            y2 = yf_ref[0] + yb_ref[0] + ds_ref[...] * xs
            z = z_ref[0].astype(F32)
            sg = _sigmoid(z)
            sz = z * sg
            u = y2 * sz
            dyn = dyn_ref[0].astype(F32)
            dun = dyn * w_ref[...]
            uh_parts, du_parts = [], []
            for g in range(NG):
                sl = slice(g * GWID, (g + 1) * GWID)
                ug = u[:, sl]
                rg = lax.rsqrt(jnp.mean(ug * ug, axis=-1, keepdims=True) + EPS)
                uh = ug * rg
                dg = dun[:, sl]
                du_parts.append(rg * (dg - uh * jnp.mean(dg * uh, axis=-1, keepdims=True)))
                uh_parts.append(uh)
            du = jnp.concatenate(du_parts, axis=1)
            uh = jnp.concatenate(uh_parts, axis=1)
            dy2 = du * sz
            dy_ref[0] = dy2.astype(BF)
            dz_ref[0] = (du * y2 * (sg * (1.0 + z * (1.0 - sg)))).astype(BF)
            acc_ref[0, 0:1, :] += jnp.sum(dyn * uh, axis=0, keepdims=True)
            acc_ref[0, 1:2, :] += jnp.sum(dy2 * xs, axis=0, keepdims=True)

    xmap = lambda c: (lambda e, t: (e, jnp.minimum(t, nx - 1), c))
    blk = lambda c: pl.BlockSpec((1, ROW_TILE, DIN), xmap(c))
    oblk = pl.BlockSpec((1, ROW_TILE, DIN), lambda e, t: (e, t, 0))
    vec = pl.BlockSpec((1, DIN), lambda e, t: (0, 0))
    return pl.pallas_call(
        body, name="ssd_post_bwd", grid=(2, R // ROW_TILE),
        in_specs=[blk(0), blk(0), blk(0), blk(0), blk(1), vec, vec],
        out_specs=(oblk, oblk, pl.BlockSpec((1, 8, DIN), lambda e, t: (e, 0, 0))),
        out_shape=(jax.ShapeDtypeStruct((2, R, DIN), BF), jax.ShapeDtypeStruct((2, R, DIN), BF),
                   jax.ShapeDtypeStruct((2, 8, DIN), F32)),
        compiler_params=_params(("parallel", "arbitrary")),
    )(d_yn, y_f, y_b, xbc, proj3, dskip_e, ssd_norm)


def _merge_fwd(proj3, P, S, b_merge, L):
    def body(gp_ref, p_ref, s_ref, b_ref, o_ref):
        gt = _sigmoid(gp_ref[0].astype(F32) + b_ref[...])
        o_ref[0] = (gt[:, :D] * p_ref[0].astype(F32) + gt[:, D:] * s_ref[0].astype(F32)).astype(BF)

    blk = pl.BlockSpec((1, ROW_TILE, D), lambda e, t: (e, t, 0))
    return pl.pallas_call(
        body, name="merge_fwd", grid=(2, L // ROW_TILE),
        in_specs=[pl.BlockSpec((1, ROW_TILE, 2 * D), lambda e, t: (e, t, OFF_GATE // (2 * D))), blk, blk,
                  pl.BlockSpec((1, 2 * D), lambda e, t: (0, 0))],
        out_specs=blk, out_shape=jax.ShapeDtypeStruct((2, L, D), BF),
        compiler_params=_params(("parallel", "parallel")),
    )(proj3, P, S, b_merge)


def _merge_bwd(d_merged, proj3, P, S, b_merge, L):
    _, R, _ = proj3.shape
    nx = L // ROW_TILE

    def body(dm_ref, gp_ref, p_ref, s_ref, b_ref, dp_ref, ds_ref, dg_ref, acc_ref):
        t = pl.program_id(1)

        @pl.when(t == 0)
        def _():
            acc_ref[...] = jnp.zeros_like(acc_ref)

        @pl.when(t >= nx)
        def _():
            dg_ref[...] = jnp.zeros_like(dg_ref)

        @pl.when(t < nx)
        def _():
            gt = _sigmoid(gp_ref[0].astype(F32) + b_ref[...])
            dm = dm_ref[0].astype(F32)
            g1, g2 = gt[:, :D], gt[:, D:]
            dp_ref[0] = (dm * g1).astype(BF)
            ds_ref[0] = (dm * g2).astype(BF)
            dgp = jnp.concatenate([dm * p_ref[0].astype(F32) * g1 * (1.0 - g1),
                                   dm * s_ref[0].astype(F32) * g2 * (1.0 - g2)], axis=1)
            dg_ref[0] = dgp.astype(BF)
            acc_ref[0, 0:1, :] += jnp.sum(dgp, axis=0, keepdims=True)

    xmap = lambda e, t: (e, jnp.minimum(t, nx - 1), 0)
    blk = pl.BlockSpec((1, ROW_TILE, D), xmap)
    return pl.pallas_call(
        body, name="merge_bwd", grid=(2, R // ROW_TILE),
        in_specs=[blk, pl.BlockSpec((1, ROW_TILE, 2 * D), lambda e, t: (e, jnp.minimum(t, nx - 1), OFF_GATE // (2 * D))),
                  blk, blk, pl.BlockSpec((1, 2 * D), lambda e, t: (0, 0))],
        out_specs=(blk, blk, pl.BlockSpec((1, ROW_TILE, 2 * D), lambda e, t: (e, t, 0)),
                   pl.BlockSpec((1, 8, 2 * D), lambda e, t: (e, 0, 0))),
        out_shape=(jax.ShapeDtypeStruct((2, L, D), BF), jax.ShapeDtypeStruct((2, L, D), BF),
                   jax.ShapeDtypeStruct((2, R, 2 * D), BF), jax.ShapeDtypeStruct((2, 8, 2 * D), F32)),
        compiler_params=_params(("parallel", "arbitrary")),
    )(d_merged, proj3, P, S, b_merge)


def _final(out3, x, tgt, gtab, norm_post, L):
    def body(o_ref, x_ref, t_ref, g_ref, n_ref, dxo_ref, do_ref, acc_ref):
        @pl.when(pl.program_id(1) == 0)
        def _():
            acc_ref[...] = jnp.zeros_like(acc_ref)

        o = o_ref[0].astype(F32)
        gate = g_ref[0, 0:1, :]
        npost = n_ref[...]
        r2 = lax.rsqrt(jnp.mean(o * o, axis=-1, keepdims=True) + EPS)
        nh = o * r2
        on = nh * npost
        err = x_ref[0] + gate * on - t_ref[0]
        dxo = err * (1.0 / D)
        dxo_ref[0] = dxo
        dnh = dxo * gate * npost
        do_ref[0] = (r2 * (dnh - nh * jnp.mean(dnh * nh, axis=-1, keepdims=True))).astype(BF)
        acc_ref[0, 0:1, :] += jnp.sum(dxo * on, axis=0, keepdims=True)
        acc_ref[0, 1:2, :] += jnp.sum(dxo * gate * nh, axis=0, keepdims=True)
        acc_ref[0, 2:3, :] += jnp.sum(err * err, axis=0, keepdims=True)

    blk = pl.BlockSpec((1, ROW_TILE, D), lambda e, t: (e, t, 0))
    return pl.pallas_call(
        body, name="final", grid=(2, L // ROW_TILE),
        in_specs=[blk, blk, blk, pl.BlockSpec((1, 8, D), lambda e, t: (e, 0, 0)),
                  pl.BlockSpec((1, D), lambda e, t: (0, 0))],
        out_specs=(blk, blk, pl.BlockSpec((1, 8, D), lambda e, t: (e, 0, 0))),
        out_shape=(jax.ShapeDtypeStruct((2, L, D), F32), jax.ShapeDtypeStruct((2, L, D), BF),
                   jax.ShapeDtypeStruct((2, 8, D), F32)),
        compiler_params=_params(("parallel", "arbitrary")),
    )(out3, x, tgt, gtab, norm_post)


def _local_step(x, c, ctx, loss_target, W, late_shard=None, exchange=False):
    nb, L, _ = x.shape
    LC = ctx.shape[1]
    R = L + LC
    assert nb == 2 and L % ROW_TILE == 0 and LC % Q == 0 and L % POOL_TILE == 0
    w_inT = W["w_in"]
    w_dtT = jnp.pad(w_inT[OFF_DT:], ((0, 64), (0, 0)))
    tables = _pool_tables(L)
    tr, tl = (2 * R) // 8, (2 * L) // 8

    c16 = jnp.zeros((16, D), F32).at[0:2].set(c).at[2].set(W["c_ctx"])
    mod16 = _adaln_fwd(c16, W["w_ada"], W["b_ada"])
    shift, scale, gate = mod16[:, :D], mod16[:, D:2 * D], mod16[:, 2 * D:]
    npre = W["norm_pre"]
    tab = jnp.zeros((2, 2, 8, D), F32)
    for e in range(2):
        tab = tab.at[e, 0, 0].set(npre[0] * (1.0 + scale[e])).at[e, 0, 1].set(shift[e])
        tab = tab.at[e, 1, 0].set(npre[0] * (1.0 + scale[2])).at[e, 1, 1].set(shift[2])
    gtab = jnp.zeros((2, 8, D), F32).at[:, 0].set(gate[0:2])

    hx = _norm_mod_fwd(x, ctx, tab)
    hx2 = hx.reshape(2 * R, D)
    if late_shard is None:
        proj = _matmul(hx2, w_inT, BF, "proj_main", tm=tr, tn=1024, bt=True, n=OFF_DT)
    else:
        proj, late = _matmul(hx2, w_inT, BF, "proj_main", tm=tr, tn=1024, bt=True, n=OFF_DT, side=_gather_side(late_shard))
        W = {**W, **_unpack_gather(late, GATHER_LATE)}
    proj3 = proj.reshape(2, R, OFF_DT)
    dt_raw = _matmul(hx2, w_dtT, F32, "proj_dt", tm=tr, bt=True).reshape(2, R, 128)
    ypool = _pool_fwd(proj3, W["pool_w"], W["pool_scale"], tables, L)
    xbc = _conv_fwd(proj3, W["conv_w"], W["conv_b"], L)
    bias128 = jnp.pad(W["dt_bias"].reshape(1, 64), ((0, 0), (0, 64)))
    dt_loc = _dt_fwd(dt_raw, bias128)
    A = -jnp.exp(W["a_log"].reshape(2, NG, HPG))
    a_loc = jnp.zeros((NG, 8, 128), F32).at[:, 0, :16].set(A.transpose(1, 0, 2).reshape(NG, 16))
    y_f, hs_f, y_b, hs_b = _ssd_fwd(xbc, dt_loc, a_loc, L)
    dskip_e = jnp.repeat(W["d_skip"].reshape(1, 32), HEAD, axis=1)
    yn = _ssd_post_fwd(y_f, y_b, xbc, proj3, dskip_e, W["ssd_norm"], L)
    ypool2, yn2 = ypool.reshape(2 * L, D), yn.reshape(2 * L, DIN)
    P = _matmul(ypool2, W["w_proj_pool"], BF, "proj_pool", tm=tl, tn=1024).reshape(2, L, D)
    S = _matmul(yn2, W["w_proj_ssd"], BF, "proj_ssd", tm=tl, tn=1024).reshape(2, L, D)
    merged = _merge_fwd(proj3, P, S, W["b_merge"], L)
    merged2 = merged.reshape(2 * L, D)
    out3 = _matmul(merged2, W["w_out"], BF, "proj_out", tm=tl, tn=1024).reshape(2, L, D)
    dxo, dout, acc_f = _final(out3, x, loss_target, gtab, W["norm_post"], L)

    dout2 = dout.reshape(2 * L, D)
    g = {}
    g["w_out"] = _matmul_tn(merged2, dout2, "dw_out", ta=1024, tn=1024, tr=tl)
    d_merged = _matmul(dout2, W["w_out"], BF, "d_merged", tm=tl, tn=1024, bt=True).reshape(2, L, D)
    dP, dS, dgp, acc_m = _merge_bwd(d_merged, proj3, P, S, W["b_merge"], L)
    dP2, dS2 = dP.reshape(2 * L, D), dS.reshape(2 * L, D)
    g["w_proj_pool"] = _matmul_tn(ypool2, dP2, "dw_proj_pool", ta=1024, tn=1024, tr=tl)
    g["w_proj_ssd"] = _matmul_tn(yn2, dS2, "dw_proj_ssd", ta=1024, tn=1024, tr=tl)
    d_ypool = _matmul(dP2, W["w_proj_pool"], BF, "d_ypool", tm=tl, tn=1024, bt=True).reshape(2, L, D)
    d_yn = _matmul(dS2, W["w_proj_ssd"], BF, "d_yn", tm=tl, tn=1024, bt=True).reshape(2, L, DIN)
    dv, dzp, g["pool_w"], acc_p = _pool_bwd(proj3, d_ypool, W["pool_w"], jnp.swapaxes(W["pool_w"], 1, 2),
                                            W["pool_scale"], tables, L)
    dy2, dzs, acc_s = _ssd_post_bwd(d_yn, y_f, y_b, xbc, proj3, dskip_e, W["ssd_norm"], L)
    dxs_f, dbc_f, ddt_f, dxs_b, dbc_b, ddt_b, acc_a = _ssd_bwd(xbc, dt_loc, a_loc, hs_f, hs_b, y_f, y_b, dy2, L)
    ident = lambda j: j
    dxr_xs, acc_cx = _conv_bwd(proj3, [dxs_f, dxs_b, dy2], [None, None, dskip_e], 0, DIN, [ident, ident, ident],
                               W["conv_w"], W["conv_b"], L, "conv_bwd_xs")
    bcmap = lambda j: 2 * lax.rem(j, NG) + j // NG
    dxr_bc, acc_cb = _conv_bwd(proj3, [dbc_f, dbc_b], [None, None], DIN, 2 * NG * NST, [bcmap, bcmap],
                               W["conv_w"], W["conv_b"], L, "conv_bwd_bc")
    ddtr, acc_d = _dt_bwd(dt_raw, bias128, ddt_f, ddt_b)
    pieces = [dv, dzp, dzs, dgp, dxr_xs, dxr_bc]
    dw_rows = [_matmul_tn(p.reshape(2 * R, p.shape[2]), hx2, "dw_in_%d" % i, ta=1024, tn=1024, tr=tr)
               for i, p in enumerate(pieces)]
    dw_rows.append(_matmul_tn(ddtr.reshape(2 * R, 128), hx2, "dw_in_dt", ta=128, tn=1024, tr=tr)[:64])
    g["w_in"] = jnp.concatenate(dw_rows, axis=0)
    acc_c = jnp.concatenate([acc_cx[0] + acc_cx[1], acc_cb[0] + acc_cb[1]], axis=1)
    g["conv_w"] = acc_c[0:4]
    g["conv_b"] = acc_c[4:5]
    if exchange:
        gb = _pack_grads(g, GRADS_EARLY)
        pair = _pair_add(gb, _pair_exchange(gb, None, "grads_pair_exchange_early"), "grads_pair_add_early")
        dh, recv_early = _dhx(pieces, ddtr, w_inT, w_dtT, side=_chip_exchange_side(pair))
    else:
        dh, recv_early = _dhx(pieces, ddtr, w_inT, w_dtT), None
    grad_x, acc_n = _norm_mod_bwd(dh, x, ctx, tab, dxo)
    g["w_ada"], db_rows, sm_rows = _adaln_bwd(acc_n, acc_f, mod16, c16, npre, W["w_ada"])

    g["b_ada"] = db_rows[0:1]
    g["norm_pre"] = sm_rows[0:1]
    g["c_ctx"] = sm_rows[1]
    g["norm_post"] = acc_f[0, 1:2] + acc_f[1, 1:2]
    g["b_merge"] = acc_m[0, 0:1] + acc_m[1, 0:1]
    g["pool_scale"] = acc_p[:, 0, :].reshape(1, D)
    g["dt_bias"] = (acc_d[0, 0, :64] + acc_d[1, 0, :64]).reshape(2, 32)
    dA = (acc_a[0, :, 0, :16] + acc_a[1, :, 0, :16]).reshape(NG, 2, HPG).transpose(1, 0, 2)
    g["a_log"] = (dA * A).reshape(2, 32)
    g["d_skip"] = (acc_s[0, 1] + acc_s[1, 1]).reshape(32, HEAD).sum(axis=1).reshape(1, 32)
    g["ssd_norm"] = acc_s[0, 0:1] + acc_s[1, 0:1]
    loss_lanes = acc_f[:, 2, :]
    return loss_lanes, grad_x, g, recv_early


MESH = pl.DeviceIdType.MESH
ANY = pl.BlockSpec(memory_space=pl.ANY)


def _all_gather(shard):
    m_per, n = shard.shape

    def body(x_ref, out_ref, send_sems, recv_sems, local_sem):
        x, y, c = lax.axis_index("x"), lax.axis_index("y"), lax.axis_index("c")
        me, sibling = (x, y, c), (x, y, 1 - c)
        chips = [(1 - x, y), (x, 1 - y), (1 - x, 1 - y)]

        def rows(px, py, pc):
            return out_ref.at[pl.ds((4 * px + 2 * py + pc) * m_per, m_per), :]

        def copy(k, block, to, src=None):
            return pltpu.make_async_remote_copy(
                src_ref=rows(*block) if src is None else src, dst_ref=rows(*block),
                send_sem=send_sems.at[k], recv_sem=recv_sems.at[k], device_id=to, device_id_type=MESH)

        mine = pltpu.make_async_copy(x_ref, rows(*me), local_sem)
        mine.start()
        first = [copy(0, me, sibling, src=x_ref)]
        first += [copy(1 + j, me, (*chip, c), src=x_ref) for j, chip in enumerate(chips)]
        for cp in first:
            cp.start()
        passed = [copy(4 + j, (*chip, c), sibling) for j, chip in enumerate(chips)]
        for j, chip in enumerate(chips):
            copy(1 + j, (*chip, c), me).wait_recv()
            passed[j].start()
        copy(0, sibling, me).wait_recv()
        for j, chip in enumerate(chips):
            copy(4 + j, (*chip, 1 - c), me).wait_recv()
        for cp in first + passed:
            cp.wait_send()
        mine.wait()

    return pl.pallas_call(
        body, name="all_gather_weights",
        out_shape=jax.ShapeDtypeStruct((NDEV * m_per, n), shard.dtype),
        in_specs=[ANY], out_specs=ANY,
        scratch_shapes=[pltpu.SemaphoreType.DMA((7,)), pltpu.SemaphoreType.DMA((7,)), pltpu.SemaphoreType.DMA],
    )(shard)


PAIR_PIECES = 4


def _xor_peer(k, x, y, c):
    return (1 - x if k & 4 else x, 1 - y if k & 2 else y, 1 - c if k & 1 else c)


def _pair_exchange(big, small, name):
    _, nq, rows, n = big.shape
    piece = rows // PAIR_PIECES
    assert piece * PAIR_PIECES == rows and piece % 16 == 0
    with_small = small is not None

    def body(*refs):
        if with_small:
            big_ref, small_ref, got_ref, osmall_ref, send_sems, recv_sems, local_sem = refs
        else:
            big_ref, got_ref, send_sems, recv_sems, local_sem = refs
        x, y, c = lax.axis_index("x"), lax.axis_index("y"), lax.axis_index("c")
        me = 4 * x + 2 * y + c

        def rc(src, dst, sem, peer):
            return pltpu.make_async_remote_copy(src_ref=src, dst_ref=dst, send_sem=send_sems.at[sem],
                                                recv_sem=recv_sems.at[sem], device_id=peer, device_id_type=MESH)

        sib = _xor_peer(1, x, y, c)
        local, sends, recvs = [], [], []
        for q in range(nq):
            for h in range(PAIR_PIECES):
                rws = pl.ds(h * piece, piece)
                cp = rc(big_ref.at[1 - c, q, rws], got_ref.at[q, rws], 8 + q * PAIR_PIECES + h, sib)
                sends.append(cp)
                recvs.append(cp)
        if with_small:
            local.append(pltpu.make_async_copy(small_ref, osmall_ref.at[me], local_sem))
            for k in range(1, NDEV):
                px, py, pc = _xor_peer(k, x, y, c)
                sends.append(rc(small_ref, osmall_ref.at[me], k, (px, py, pc)))
                recvs.append(rc(small_ref, osmall_ref.at[4 * px + 2 * py + pc], k, (px, py, pc)))
        for cp in local + sends:
            cp.start()
        for cp in sends:
            cp.wait_send()
        for cp in recvs:
            cp.wait_recv()
        for cp in local:
            cp.wait()

    nsem = 8 + nq * PAIR_PIECES
    out_shape = [jax.ShapeDtypeStruct(big.shape[1:], big.dtype)]
    if with_small:
        out_shape.append(jax.ShapeDtypeStruct((NDEV,) + small.shape, small.dtype))
    out = pl.pallas_call(
        body, name=name, out_shape=tuple(out_shape),
        in_specs=[ANY] * (1 + with_small), out_specs=(ANY,) * (1 + with_small),
        scratch_shapes=[pltpu.SemaphoreType.DMA((nsem,)), pltpu.SemaphoreType.DMA((nsem,)), pltpu.SemaphoreType.DMA],
    )(*((big, small) if with_small else (big,)))
    return out if with_small else out[0]


def _pair_add(big, got, name):
    _, nq, rows, n = big.shape
    tile = rows // 4
    assert rows % 64 == 0

    def body(c_ref, a_ref, b_ref, o_ref):
        o_ref[0] = (a_ref[0, 0].astype(F32) + b_ref[0].astype(F32)).astype(BF)

    blk = pl.BlockSpec((1, tile, n), lambda q, i, c_ref: (q, i, 0))
    return pl.pallas_call(
        body, name=name,
        grid_spec=pltpu.PrefetchScalarGridSpec(
            num_scalar_prefetch=1, grid=(nq, rows // tile),
            in_specs=[pl.BlockSpec((1, 1, tile, n), lambda q, i, c_ref: (c_ref[0], q, i, 0)), blk], out_specs=blk),
        out_shape=jax.ShapeDtypeStruct(got.shape, BF), compiler_params=_params(("parallel", "parallel")),
    )(lax.axis_index("c").astype(jnp.int32).reshape(1), big, got)


def _chip_exchange_side(pair):
    def make(in_refs, out_refs, send_sems, recv_sems, local_sem, arrivals=True):
        (in_ref,), (out_ref,) = in_refs, out_refs
        x, y, c = lax.axis_index("x"), lax.axis_index("y"), lax.axis_index("c")
        q = 2 * x + y
        local = [pltpu.make_async_copy(in_ref.at[q], out_ref.at[q], local_sem)]
        sends, recvs = [], []
        for j in range(1, 4):
            px, py, pc = _xor_peer(2 * j, x, y, c)
            pq = 2 * px + py
            for lst, dst in ((sends, out_ref.at[q]), (recvs, out_ref.at[pq]))[:1 + arrivals]:
                lst.append(pltpu.make_async_remote_copy(
                    src_ref=in_ref.at[pq], dst_ref=dst, send_sem=send_sems.at[j - 1], recv_sem=recv_sems.at[j - 1],
                    device_id=(px, py, pc), device_id_type=MESH))
        return local, sends, recvs

    return _SideCopies([pair], [jax.ShapeDtypeStruct(pair.shape, pair.dtype)], make)


def _gather_side(shard):
    def make(in_refs, out_refs, send_sems, recv_sems, local_sem, arrivals=True):
        (src,), (dst,) = in_refs, out_refs
        x, y, c = lax.axis_index("x"), lax.axis_index("y"), lax.axis_index("c")
        me = 4 * x + 2 * y + c
        local = [pltpu.make_async_copy(src, dst.at[me], local_sem)]
        sends, recvs = [], []
        for k in range(1, NDEV):
            px, py, pc = _xor_peer(k, x, y, c)
            for lst, slot in ((sends, me), (recvs, 4 * px + 2 * py + pc))[:1 + arrivals]:
                lst.append(pltpu.make_async_remote_copy(
                    src_ref=src, dst_ref=dst.at[slot], send_sem=send_sems.at[k - 1], recv_sem=recv_sems.at[k - 1],
                    device_id=(px, py, pc), device_id_type=MESH))
        return local, sends, recvs

    return _SideCopies([shard], [jax.ShapeDtypeStruct((NDEV,) + shard.shape, shard.dtype)], make)


def _run_side(side, name):
    n_si, n_so = len(side.inputs), len(side.out_shapes)

    def body(*refs):
        local, sends, recvs = side.make(refs[:n_si], refs[n_si:n_si + n_so], *refs[n_si + n_so:])
        for cp in local + sends:
            cp.start()
        for cp in sends:
            cp.wait_send()
        for cp in recvs:
            cp.wait_recv()
        for cp in local:
            cp.wait()

    return pl.pallas_call(body, name=name, out_shape=tuple(side.out_shapes), in_specs=[ANY] * n_si,
                          out_specs=(ANY,) * n_so, scratch_shapes=side.scratch())(*side.inputs)


ADAM_TILE = 64
PACK_W = 1024


def _adamw(recv, w, m, v, name):
    rp = w.shape[0]
    tile = min(ADAM_TILE, rp)
    nsrc = recv.shape[0]

    def body(r_ref, w_ref, m_ref, v_ref, g_ref, d_ref, nm_ref, nv_ref):
        g = r_ref[0].astype(F32)
        for i in range(1, nsrc):
            g = g + r_ref[i].astype(F32)
        m1 = ADAM_B1 * m_ref[...] + (1.0 - ADAM_B1) * g
        v1 = ADAM_B2 * v_ref[...] + (1.0 - ADAM_B2) * (g * g)
        m_hat = m1 / (1.0 - ADAM_B1 ** ADAM_STEP)
        v_hat = v1 / (1.0 - ADAM_B2 ** ADAM_STEP)
        g_ref[...] = g
        d_ref[...] = -ADAM_LR * (m_hat / (jnp.sqrt(v_hat) + ADAM_EPS) + ADAM_WD * w_ref[...])
        nm_ref[...] = m1
        nv_ref[...] = v1

    blk = pl.BlockSpec((tile, PACK_W), lambda i: (i, 0))
    shp = jax.ShapeDtypeStruct((rp, PACK_W), F32)
    return pl.pallas_call(
        body, name=name, grid=(rp // tile,),
        in_specs=[pl.BlockSpec((nsrc, tile, PACK_W), lambda i: (0, i, 0)), blk, blk, blk],
        out_specs=(blk, blk, blk, blk), out_shape=(shp, shp, shp, shp),
        compiler_params=_params(("parallel",)),
    )(recv, w, m, v)


BIG = {"w_ada": ((3 * D, D), 0), "pool_w": ((4, PGW, PGW), 1), "w_proj_pool": ((D, D), 0), "w_proj_ssd": ((DIN, D), 0),
       "w_out": ((D, D), 0), "w_in": ((IN_COLS, D), 0), "conv_w": ((4, CONV_DIM), 1)}
TRANSPOSED = ("w_ada", "w_in")
PACK_ROWS = {"w_ada": 384, "w_in": 1168, "conv_w": 16, "pool_w": 32, "w_proj_pool": 128, "w_proj_ssd": 256, "w_out": 128}
GATHER_EARLY = ("w_ada", "w_in", "conv_w")
GATHER_LATE = ("pool_w", "w_proj_pool", "w_proj_ssd", "w_out")
GRADS_LATE = ("w_ada",)
GRADS_EARLY = tuple(n for n in PACK_ROWS if n not in GRADS_LATE)
SMALL = {"c_ctx": (D,), "b_ada": (1, 3 * D), "norm_pre": (1, D), "norm_post": (1, D), "b_merge": (1, 2 * D),
         "pool_scale": (1, D), "conv_b": (1, CONV_DIM), "dt_bias": (2, 32), "a_log": (2, 32), "d_skip": (1, 32),
         "ssd_norm": (1, DIN)}
LOSS_SLOT = 128
assert all(_r % 16 == 0 for _r in PACK_ROWS.values())
SMALL_ROWS = 16


def _shard_shape(name):
    shape, ax = BIG[name]
    return tuple(s // NDEV if i == ax else s for i, s in enumerate(shape))


def _as_rows(t, rows):
    pad = [(0, 0)] * (t.ndim - 1) + [(0, rows * PACK_W - t.shape[-1])]
    return jnp.pad(t, pad).reshape(t.shape[:-1] + (rows, PACK_W))


def _shard_rows(t, name):
    sh, r = _shard_shape(name), PACK_ROWS[name]
    lead = t.shape[:t.ndim - len(sh)]
    if len(sh) == 2 and sh[1] == PACK_W:
        return jnp.pad(t, [(0, 0)] * len(lead) + [(0, r - sh[0]), (0, 0)])
    if int(np.prod(sh)) == r * PACK_W:
        return t.reshape(lead + (r, PACK_W))
    return _as_rows(t.reshape(lead + (-1,)), r)


def _to_chunks(full, name):
    shape, ax = BIG[name]
    split = shape[:ax] + (NDEV, shape[ax] // NDEV) + shape[ax + 1:]
    return _shard_rows(jnp.moveaxis(full.reshape(split), ax, 0), name)


def _from_chunks(chunks, name):
    shape, ax = BIG[name]
    return jnp.moveaxis(chunks.reshape((NDEV,) + _shard_shape(name)), 0, ax).reshape(shape)


def _rows_of(names):
    return sum(PACK_ROWS[n] for n in names)


def _pack_state(t, names):
    return jnp.concatenate([_shard_rows(t[n], n) for n in names], axis=0)


def _pack_small(t, loss_part=None):
    slot = jnp.zeros((LOSS_SLOT,), F32)
    if loss_part is not None:
        slot = slot.at[0].set(loss_part)
    return _as_rows(jnp.concatenate([t[n].reshape(-1) for n in SMALL] + [slot]), SMALL_ROWS)


def _pack_grads(g, names):
    big = jnp.concatenate([_to_chunks(g[n], n).astype(BF) for n in names], axis=1)
    return jnp.swapaxes(big.reshape(4, 2, _rows_of(names), PACK_W), 0, 1)


def _unpack_state(big, names):
    out, off = {}, 0
    for n in names:
        sh, r = _shard_shape(n), PACK_ROWS[n]
        k = int(np.prod(sh))
        if len(sh) == 2 and sh[1] == PACK_W:
            out[n] = big[off:off + sh[0]]
        else:
            out[n] = big[off:off + r].reshape(-1)[:k].reshape(sh)
        off += r
    return out


def _unpack_small(small):
    out, flat, off = {}, small.reshape(-1), 0
    for n, sh in SMALL.items():
        k = int(np.prod(sh))
        out[n] = flat[off:off + k].reshape(sh)
        off += k
    out["loss"] = flat[off]
    return out


def _pack_gather(w, names):
    pieces = []
    for n in names:
        if n == "conv_w":
            pieces.append(_as_rows(jnp.concatenate([p.reshape(-1) for p in _split(w[n], 3)]), PACK_ROWS[n]))
        else:
            pieces.append(_shard_rows(w[n], n).astype(BF))
    return jnp.concatenate(pieces, axis=0)


def _unpack_gather(gathered, names):
    g = gathered.reshape(NDEV, _rows_of(names), PACK_W)
    out, off = {}, 0
    for n in names:
        r = PACK_ROWS[n]
        sh = _shard_shape(n)
        if n == "conv_w":
            k = int(np.prod(sh))
            terms = g[:, off:off + r].reshape(NDEV, -1)[:, :3 * k].astype(F32).reshape(NDEV, 3, k)
            out[n] = _from_chunks(terms[:, 0] + terms[:, 1] + terms[:, 2], n)
        elif len(sh) == 2 and sh[1] == PACK_W:
            out[n] = _from_chunks(g[:, off:off + sh[0]], n)
        else:
            out[n] = _from_chunks(g[:, off:off + r], n)
        off += r
    return out


PARAMS = ["c_ctx", "w_ada", "b_ada", "norm_pre", "norm_post", "w_in", "b_merge", "pool_w", "pool_scale", "conv_w", "conv_b",
          "dt_bias", "a_log", "d_skip", "ssd_norm", "w_proj_pool", "w_proj_ssd", "w_out"]


def kernel(x, c, ctx, c_ctx, w_ada, b_ada, norm_pre, norm_post, w_in, b_merge, pool_w, pool_scale, conv_w, conv_b, dt_bias, a_log, d_skip, ssd_norm, w_proj_pool, w_proj_ssd, w_out, loss_target, m_c_ctx, m_w_ada, m_b_ada, m_norm_pre, m_norm_post, m_w_in, m_b_merge, m_pool_w, m_pool_scale, m_conv_w, m_conv_b, m_dt_bias, m_a_log, m_d_skip, m_ssd_norm, m_w_proj_pool, m_w_proj_ssd, m_w_out, v_c_ctx, v_w_ada, v_b_ada, v_norm_pre, v_norm_post, v_w_in, v_b_merge, v_pool_w, v_pool_scale, v_conv_w, v_conv_b, v_dt_bias, v_a_log, v_d_skip, v_ssd_norm, v_w_proj_pool, v_w_proj_ssd, v_w_out):
    given = dict(locals())
    shapes = {n: given[n].shape for n in PARAMS}

    def local(prefix):
        t = {n: (given[prefix + n] if n == "c_ctx" else given[prefix + n][0]) for n in PARAMS}
        for n in TRANSPOSED:
            t[n] = t[n].T
        return {n: t[n].reshape(_shard_shape(n) if n in BIG else SMALL[n]) for n in PARAMS}

    w, m, v = local(""), local("m_"), local("v_")

    W = _unpack_gather(_all_gather(_pack_gather(w, GATHER_EARLY)), GATHER_EARLY)
    for n in SMALL:
        W[n] = w[n]
    lanes, grad_x, g, recv_early = _local_step(x, c, ctx, loss_target, W, late_shard=_pack_gather(w, GATHER_LATE),
                                               exchange=True)
    gb = _pack_grads(g, GRADS_LATE)
    got, recv_small = _pair_exchange(gb, _pack_small(g, (0.5 / D) * jnp.sum(lanes)), "grads_pair_exchange_late")
    recv_late, = _run_side(_chip_exchange_side(_pair_add(gb, got, "grads_pair_add_late")), "grads_chip_exchange_late")
    res = [{} for _ in range(4)]
    for names, recv, tag in ((GRADS_EARLY, recv_early, "early"), (GRADS_LATE, recv_late, "late")):
        for r, t in zip(res, _adamw(recv, *[_pack_state(s, names) for s in (w, m, v)], "adamw_" + tag)):
            r.update(_unpack_state(t, names))
    for r, t in zip(res, _adamw(recv_small, *[_pack_small(s) for s in (w, m, v)], "adamw_small")):
        r.update(_unpack_small(t))
    outs = [res[0]["loss"], grad_x]
    for r in res:
        for n in TRANSPOSED:
            r[n] = r[n].T
        outs += [r[n].reshape(shapes[n]) for n in PARAMS]
    return tuple(outs)
```

```python
import functools

import numpy as np
import jax
import jax.numpy as jnp
from jax import lax
from jax.experimental import pallas as pl
from jax.experimental.pallas import tpu as pltpu

F32, BF = jnp.float32, jnp.bfloat16

D = 1024
GRID_W = 64
EPS = 1e-6
POOL_WINDOWS = (2, 4, 8, 16)
PGW = 256
DIN = 2048
HEAD = 64
NST = 128
NG = 4
HPG = 8
GWID = HPG * HEAD
Q = 128
CONV_DIM = 3072
OFF_GATE, OFF_XBC, OFF_DT, IN_COLS = 4096, 6144, 9216, 9280
NDEV = 8
ADAM_LR, ADAM_B1, ADAM_B2, ADAM_EPS, ADAM_WD, ADAM_STEP = 0.001, 0.9, 0.999, 1e-08, 0.01, 10

V7X_VMEM_LIMIT = 56 * 2 ** 20
ROW_TILE = 256


def _params(sem=None):
    return pltpu.CompilerParams(dimension_semantics=sem, vmem_limit_bytes=V7X_VMEM_LIMIT)


def _dot(a, b):
    return jnp.dot(a.astype(BF), b.astype(BF), preferred_element_type=F32)


def _dot_nt(a, b):
    return lax.dot_general(a.astype(BF), b.astype(BF), (((1,), (1,)), ((), ())), preferred_element_type=F32)


def _dot_tn(a, b):
    return lax.dot_general(a.astype(BF), b.astype(BF), (((0,), (0,)), ((), ())), preferred_element_type=F32)


def _split(a, n):
    parts = []
    for _ in range(n):
        p = a.astype(BF)
        parts.append(p)
        a = a - p.astype(F32)
    return parts


def _dot_sl(a, b01, n=3):
    parts = _split(a, n)
    m = a.shape[0]
    if n == 1 or m % 16:
        return sum(jnp.dot(p, b01, preferred_element_type=F32) for p in parts)
    r = jnp.dot(jnp.concatenate(parts, axis=0), b01, preferred_element_type=F32)
    return sum(r[i * m:(i + 1) * m] for i in range(n))


def _dot_sr(a01, b, n=3):
    parts = _split(b, n)
    k = b.shape[1]
    if n == 1 or k % 128:
        return sum(jnp.dot(a01, p, preferred_element_type=F32) for p in parts)
    r = jnp.dot(a01, jnp.concatenate(parts, axis=1), preferred_element_type=F32)
    return sum(r[:, i * k:(i + 1) * k] for i in range(n))


def _sigmoid(x):
    return 1.0 / (1.0 + jnp.exp(-x))


class _SideCopies:
    NSEM = 8

    def __init__(self, inputs, out_shapes, make):
        self.inputs, self.out_shapes, self.make = list(inputs), list(out_shapes), make

    def scratch(self):
        return [pltpu.SemaphoreType.DMA((self.NSEM,)), pltpu.SemaphoreType.DMA((self.NSEM,)), pltpu.SemaphoreType.DMA]

    def start(self, grid, in_refs, out_refs, sems):
        @pl.when(functools.reduce(lambda p, q: p & q, [pl.program_id(i) == 0 for i in range(len(grid))]))
        def _():
            local, sends, _ = self.make(in_refs, out_refs, *sems, arrivals=False)
            for cp in local + sends:
                cp.start()

    def wait(self, grid, in_refs, out_refs, sems):
        @pl.when(functools.reduce(lambda p, q: p & q, [pl.program_id(i) == n - 1 for i, n in enumerate(grid)]))
        def _():
            local, sends, recvs = self.make(in_refs, out_refs, *sems)
            for cp in sends:
                cp.wait_send()
            for cp in recvs:
                cp.wait_recv()
            for cp in local:
                cp.wait()


def _matmul(a, b, out_dtype, name, tm=512, tn=512, tk=1024, bt=False, n=None, side=None):
    M, K = a.shape
    N = n if n is not None else (b.shape[0] if bt else b.shape[1])
    tm, tn, tk = min(tm, M), min(tn, N), min(tk, K)
    assert M % tm == 0 and N % tn == 0 and K % tk == 0, (a.shape, b.shape)
    nk = K // tk
    grid = (M // tm, N // tn, nk)
    n_si, n_so = (len(side.inputs), len(side.out_shapes)) if side else (0, 0)

    def body(*refs):
        a_ref, b_ref, o_ref = refs[0], refs[1], refs[2 + n_si]
        acc = refs[3 + n_si + n_so]
        side_refs = (refs[2:2 + n_si], refs[3 + n_si:3 + n_si + n_so], refs[4 + n_si + n_so:])
        if side:
            side.start(grid, *side_refs)
        k = pl.program_id(2)
        p = _dot_nt(a_ref[...], b_ref[...]) if bt else _dot(a_ref[...], b_ref[...])

        @pl.when(k == 0)
        def _():
            acc[...] = p

        @pl.when(k > 0)
        def _():
            acc[...] += p

        @pl.when(k == nk - 1)
        def _():
            o_ref[...] = acc[...].astype(o_ref.dtype)

        if side:
            side.wait(grid, *side_refs)

    out = pl.pallas_call(
        body, name=name, grid=grid,
        in_specs=[pl.BlockSpec((tm, tk), lambda i, j, k: (i, k)),
                  pl.BlockSpec((tn, tk), lambda i, j, k: (j, k)) if bt else pl.BlockSpec((tk, tn), lambda i, j, k: (k, j))]
        + [ANY] * n_si,
        out_specs=(pl.BlockSpec((tm, tn), lambda i, j, k: (i, j)),) + (ANY,) * n_so,
        out_shape=(jax.ShapeDtypeStruct((M, N), out_dtype),) + tuple(side.out_shapes if side else ()),
        scratch_shapes=[pltpu.VMEM((tm, tn), F32)] + (side.scratch() if side else []),
        compiler_params=_params(("arbitrary",) * 3 if side else ("parallel", "parallel", "arbitrary")),
    )(a, b, *(side.inputs if side else ()))
    return out if side else out[0]


def _matmul_tn(a, g, name, ta=512, tn=512, tr=512):
    M, Ka = a.shape
    N = g.shape[1]
    ta, tn, tr = min(ta, Ka), min(tn, N), min(tr, M)
    assert M % tr == 0 and N % tn == 0 and Ka % ta == 0, (a.shape, g.shape)
    nr = M // tr

    def body(a_ref, g_ref, o_ref):
        k = pl.program_id(2)
        p = _dot_tn(a_ref[...], g_ref[...])

        @pl.when(k == 0)
        def _():
            o_ref[...] = p

        @pl.when(k > 0)
        def _():
            o_ref[...] += p

    return pl.pallas_call(
        body, name=name, grid=(Ka // ta, N // tn, nr),
        in_specs=[pl.BlockSpec((tr, ta), lambda i, j, k: (k, i)), pl.BlockSpec((tr, tn), lambda i, j, k: (k, j))],
        out_specs=pl.BlockSpec((ta, tn), lambda i, j, k: (i, j)),
        out_shape=jax.ShapeDtypeStruct((Ka, N), F32),
        compiler_params=_params(("parallel", "parallel", "arbitrary")),
    )(a, g)


def _dhx(pieces, ddt, w_inT, w_dtT, side=None):
    _, R, _ = pieces[0].shape
    tm = R // 4
    kb = 1024
    starts, nblk = [], []
    for p in pieces:
        starts.append(sum(nblk))
        nblk.append(p.shape[2] // kb)
    nk = sum(nblk)
    assert nk * kb == OFF_DT and R % 128 == 0
    npc = len(pieces)
    grid = (2, R // tm, nk)
    n_si, n_so = (len(side.inputs), len(side.out_shapes)) if side else (0, 0)

    def body(*refs):
        a_refs, dt_ref, w_ref, wdt_ref = refs[:npc], refs[npc], refs[npc + 1], refs[npc + 2]
        o_ref, acc = refs[npc + 3 + n_si], refs[npc + 4 + n_si + n_so]
        side_refs = (refs[npc + 3:npc + 3 + n_si], refs[npc + 4 + n_si:npc + 4 + n_si + n_so], refs[npc + 5 + n_si + n_so:])
        if side:
            side.start(grid, *side_refs)
        k = pl.program_id(2)

        @pl.when(k == 0)
        def _():
            acc[...] = _dot(dt_ref[0], wdt_ref[...])

        for p in range(npc):
            @pl.when((k >= starts[p]) & (k < starts[p] + nblk[p]))
            def _(p=p):
                acc[...] += _dot(a_refs[p][0], w_ref[...])

        @pl.when(k == nk - 1)
        def _():
            o_ref[0] = acc[...]

        if side:
            side.wait(grid, *side_refs)

    in_specs = [pl.BlockSpec((1, tm, kb), functools.partial(
        lambda e, t, k, s, nb: (e, t, jnp.clip(k - s, 0, nb - 1)), s=starts[p], nb=nblk[p])) for p in range(npc)]
    in_specs += [pl.BlockSpec((1, tm, 128), lambda e, t, k: (e, t, 0)),
                 pl.BlockSpec((kb, D), lambda e, t, k: (k, 0)),
                 pl.BlockSpec((128, D), lambda e, t, k: (0, 0))]
    out = pl.pallas_call(
        body, name="d_hx", grid=grid, in_specs=in_specs + [ANY] * n_si,
        out_specs=(pl.BlockSpec((1, tm, D), lambda e, t, k: (e, t, 0)),) + (ANY,) * n_so,
        out_shape=(jax.ShapeDtypeStruct((2, R, D), F32),) + tuple(side.out_shapes if side else ()),
        scratch_shapes=[pltpu.VMEM((tm, D), F32)] + (side.scratch() if side else []),
        compiler_params=_params(("arbitrary",) * 3 if side else ("parallel", "parallel", "arbitrary")),
    )(*pieces, ddt, w_inT, w_dtT, *(side.inputs if side else ()))
    return out if side else out[0]


def _adaln_fwd(c16, w_adaT_bf, b_ada):
    def body(c_ref, w_ref, b_ref, o_ref):
        cc = c_ref[...]
        o_ref[...] = _dot_nt(cc * _sigmoid(cc), w_ref[...]) + b_ref[...]

    return pl.pallas_call(body, name="adaln_fwd", out_shape=jax.ShapeDtypeStruct((16, 3 * D), F32),
                          compiler_params=_params())(c16, w_adaT_bf, b_ada)


def _adaln_bwd(acc_n, acc_f, mod16, c16, norm_pre, w_adaT_bf):
    def body(an_ref, af_ref, mod_ref, c_ref, np_ref, wt_ref, dw_ref, db_ref, sm_ref, dmod):
        npre = np_ref[...]
        dmod[...] = jnp.zeros_like(dmod)
        dnp = jnp.zeros((1, D), F32)
        dshift_c = jnp.zeros((1, D), F32)
        dgpre_c = jnp.zeros((1, D), F32)
        scale_c = mod_ref[2:3, D:2 * D]
        for e in range(2):
            dg_x, ds_x = an_ref[e, 0, 0:1, :], an_ref[e, 0, 1:2, :]
            dg_c, ds_c = an_ref[e, 1, 0:1, :], an_ref[e, 1, 1:2, :]
            dmod[e:e + 1, 0:D] = ds_x
            dmod[e:e + 1, D:2 * D] = dg_x * npre
            dmod[e:e + 1, 2 * D:3 * D] = af_ref[e, 0:1, :]
            dnp = dnp + dg_x * (1.0 + mod_ref[e:e + 1, D:2 * D]) + dg_c * (1.0 + scale_c)
            dshift_c = dshift_c + ds_c
            dgpre_c = dgpre_c + dg_c
        dmod[2:3, 0:D] = dshift_c
        dmod[2:3, D:2 * D] = dgpre_c * npre
        dm = dmod[...]
        cc = c_ref[...]
        sg = _sigmoid(cc)
        dw_ref[...] = _dot_tn(dm, cc * sg)
        db_ref[...] = jnp.zeros_like(db_ref)
        db_ref[0:1, :] = jnp.sum(dm, axis=0, keepdims=True)
        dsilu = sg * (1.0 + cc * (1.0 - sg))
        dcs = _dot(dm, wt_ref[...]) * dsilu
        sm_ref[...] = jnp.zeros_like(sm_ref)
        sm_ref[0:1, :] = dnp
        sm_ref[1:2, :] = dcs[2:3, :]

    return pl.pallas_call(
        body, name="adaln_bwd",
        out_shape=(jax.ShapeDtypeStruct((3 * D, D), F32), jax.ShapeDtypeStruct((16, 3 * D), F32),
                   jax.ShapeDtypeStruct((8, D), F32)),
        scratch_shapes=[pltpu.VMEM((16, 3 * D), F32)],
        compiler_params=_params())(acc_n, acc_f, mod16, c16, norm_pre, w_adaT_bf)


def _row_specs(L):
    nx = L // ROW_TILE
    return (pl.BlockSpec((1, ROW_TILE, D), lambda e, t: (e, jnp.minimum(t, nx - 1), 0)),
            pl.BlockSpec((1, ROW_TILE, D), lambda e, t: (e, jnp.maximum(t - nx, 0), 0)))


def _norm_mod_fwd(x, ctx, tab):
    L = x.shape[1]
    R = L + ctx.shape[1]
    nx = L // ROW_TILE

    def body(x_ref, c_ref, t_ref, o_ref):
        x = jnp.where(pl.program_id(1) < nx, x_ref[0], c_ref[0])
        r = lax.rsqrt(jnp.mean(x * x, axis=-1, keepdims=True) + EPS)
        t = t_ref[0, 0]
        o_ref[0] = (x * r * t[0:1] + t[1:2]).astype(BF)

    return pl.pallas_call(
        body, name="norm_mod_fwd", grid=(2, R // ROW_TILE),
        in_specs=[*_row_specs(L), pl.BlockSpec((1, 1, 8, D), lambda e, t: (e, t // nx, 0, 0))],
        out_specs=pl.BlockSpec((1, ROW_TILE, D), lambda e, t: (e, t, 0)),
        out_shape=jax.ShapeDtypeStruct((2, R, D), BF),
        compiler_params=_params(("parallel", "parallel")),
    )(x, ctx, tab)


def _norm_mod_bwd(dh, x, ctx, tab, dxo):
    L = x.shape[1]
    R = L + ctx.shape[1]
    nx = L // ROW_TILE

    def body(dh_ref, x_ref, c_ref, t_ref, dxo_ref, gx_ref, acc_ref):
        t = pl.program_id(1)
        x = jnp.where(t < nx, x_ref[0], c_ref[0])
        r = lax.rsqrt(jnp.mean(x * x, axis=-1, keepdims=True) + EPS)
        xn = x * r
        dh = dh_ref[0]

        @pl.when((t == 0) | (t == nx))
        def _():
            acc_ref[...] = jnp.zeros_like(acc_ref)

        acc_ref[0, 0, 0:1, :] += jnp.sum(dh * xn, axis=0, keepdims=True)
        acc_ref[0, 0, 1:2, :] += jnp.sum(dh, axis=0, keepdims=True)

        @pl.when(t < nx)
        def _():
            dxn = dh * t_ref[0, 0][0:1]
            dx = r * (dxn - xn * jnp.mean(dxn * xn, axis=-1, keepdims=True))
            gx_ref[0] = dxo_ref[0] + dx

    xspec, cspec = _row_specs(L)
    return pl.pallas_call(
        body, name="norm_mod_bwd", grid=(2, R // ROW_TILE),
        in_specs=[pl.BlockSpec((1, ROW_TILE, D), lambda e, t: (e, t, 0)), xspec, cspec,
                  pl.BlockSpec((1, 1, 8, D), lambda e, t: (e, t // nx, 0, 0)), xspec],
        out_specs=(xspec, pl.BlockSpec((1, 1, 8, D), lambda e, t: (e, t // nx, 0, 0))),
        out_shape=(jax.ShapeDtypeStruct((2, L, D), F32), jax.ShapeDtypeStruct((2, 2, 8, D), F32)),
        compiler_params=_params(("parallel", "arbitrary")),
    )(dh, x, ctx, tab, dxo)


POOL_TILE = 256


def _pool_tables(L):
    rows = L // GRID_W
    mats = np.zeros((4, POOL_TILE, POOL_TILE), np.float32)
    inv = np.zeros((4, L, 1), np.float32)
    for gi, k in enumerate(POOL_WINDOWS):
        lo, hi = k // 2, k - 1 - k // 2
        m = np.zeros((GRID_W, GRID_W), np.float32)
        for t in range(GRID_W):
            m[t, max(t - lo, 0):min(t + hi, GRID_W - 1) + 1] = 1.0
        for b in range(POOL_TILE // GRID_W):
            mats[gi, b * GRID_W:(b + 1) * GRID_W, b * GRID_W:(b + 1) * GRID_W] = m
        cnt_c = m.sum(1)
        cnt_r = np.array([min(r + hi, rows - 1) - max(r - lo, 0) + 1 for r in range(rows)], np.float32)
        inv[gi, :, 0] = (1.0 / (cnt_r[:, None] * cnt_c[None, :])).reshape(-1)
    matsT = np.ascontiguousarray(np.transpose(mats, (0, 2, 1)))
    return (jnp.asarray(mats, BF), jnp.asarray(matsT, BF), jnp.asarray(inv))


def _pool_cols(get_tile, mat, cs_ref, L, n):
    def step(i, carry):
        off = pl.multiple_of(i * POOL_TILE, POOL_TILE)
        cs_ref[pl.ds(GRID_W + off, POOL_TILE), :] = _dot_sr(mat, get_tile(off).astype(F32), n)
        return carry

    lax.fori_loop(0, L // POOL_TILE, step, 0)
    cs_ref[pl.ds(0, GRID_W), :] = jnp.zeros((GRID_W, PGW), F32)

    def prefix(r, carry):
        o = pl.multiple_of(r * GRID_W, GRID_W)
        cs_ref[pl.ds(o + GRID_W, GRID_W), :] = cs_ref[pl.ds(o + GRID_W, GRID_W), :] + cs_ref[pl.ds(o, GRID_W), :]
        return carry

    lax.fori_loop(0, L // GRID_W, prefix, 0)


def _pool_rows(cs_ref, off, below, above, L):
    rows = L // GRID_W
    r0 = off // GRID_W
    parts = []
    for i in range(POOL_TILE // GRID_W):
        hi = pl.multiple_of(jnp.minimum(r0 + i + above + 1, rows) * GRID_W, GRID_W)
        lo = pl.multiple_of(jnp.maximum(r0 + i - below, 0) * GRID_W, GRID_W)
        parts.append(cs_ref[pl.ds(hi, GRID_W), :] - cs_ref[pl.ds(lo, GRID_W), :])
    return jnp.concatenate(parts, axis=0)


def _pool_fwd(proj3, pool_w_bf, pool_scale, tables, L):
    mats, _, inv = tables
    nt = L // POOL_TILE

    def body(v_ref, z_ref, pw_ref, ps_ref, m_ref, inv_ref, o_ref, cs_ref):
        _pool_cols(lambda off: v_ref[0, pl.ds(off, POOL_TILE), :], m_ref[0], cs_ref, L, 1)
        half = lax.shift_left(1, pl.program_id(1))

        def step(i, carry):
            off = pl.multiple_of(i * POOL_TILE, POOL_TILE)
            rows = pl.ds(off, POOL_TILE)
            v = v_ref[0, rows, :].astype(F32)
            diff = _pool_rows(cs_ref, off, half, half - 1, L) * inv_ref[0, rows, :] - v
            yp = _dot(diff, pw_ref[0])
            z = z_ref[0, rows, :].astype(F32)
            o_ref[0, rows, :] = (yp * ps_ref[...] * (z * _sigmoid(z))).astype(BF)
            return carry

        lax.fori_loop(0, nt, step, 0)

    return pl.pallas_call(
        body, name="pool_fwd", grid=(2, 4),
        in_specs=[pl.BlockSpec((1, L, PGW), lambda e, g: (e, 0, g)),
                  pl.BlockSpec((1, L, PGW), lambda e, g: (e, 0, 4 + g)),
                  pl.BlockSpec((1, PGW, PGW), lambda e, g: (g, 0, 0)),
                  pl.BlockSpec((1, PGW), lambda e, g: (0, g)),
                  pl.BlockSpec((1, POOL_TILE, POOL_TILE), lambda e, g: (g, 0, 0)),
                  pl.BlockSpec((1, L, 1), lambda e, g: (g, 0, 0))],
        out_specs=pl.BlockSpec((1, L, PGW), lambda e, g: (e, 0, g)),
        out_shape=jax.ShapeDtypeStruct((2, L, D), BF),
        scratch_shapes=[pltpu.VMEM((L + GRID_W, PGW), F32)],
        compiler_params=_params(("parallel", "parallel")),
    )(proj3, proj3, pool_w_bf, pool_scale, mats, inv)


def _pool_bwd(proj3, d_ypool, pool_w_bf, pool_wT_bf, pool_scale, tables, L):
    mats, matsT, inv = tables
    nt = L // POOL_TILE
    R = proj3.shape[1]

    def body(v_ref, z_ref, dy_ref, pw_ref, pwt_ref, ps_ref, m_ref, mt_ref, inv_ref,
             dv_ref, dz_ref, dpw_ref, acc_ref, cs_ref, dd_ref):
        e = pl.program_id(1)

        @pl.when(e == 0)
        def _():
            dpw_ref[...] = jnp.zeros_like(dpw_ref)
            acc_ref[...] = jnp.zeros_like(acc_ref)

        _pool_cols(lambda off: v_ref[0, pl.ds(off, POOL_TILE), :], m_ref[0], cs_ref, L, 1)
        half = lax.shift_left(1, pl.program_id(0))
        ps = ps_ref[...]

        def step(i, carry):
            off = pl.multiple_of(i * POOL_TILE, POOL_TILE)
            rows = pl.ds(off, POOL_TILE)
            v = v_ref[0, rows, :].astype(F32)
            diff = _pool_rows(cs_ref, off, half, half - 1, L) * inv_ref[0, rows, :] - v
            yp = _dot(diff, pw_ref[0])
            z = z_ref[0, rows, :].astype(F32)
            sg = _sigmoid(z)
            sz = z * sg
            dy = dy_ref[0, rows, :].astype(F32)
            dz_ref[0, rows, :] = (dy * yp * ps * (sg * (1.0 + z * (1.0 - sg)))).astype(BF)
            dys = dy * sz
            acc_ref[0, 0:1, :] += jnp.sum(dys * yp, axis=0, keepdims=True)
            dyp = dys * ps
            dpw_ref[0] += _dot_tn(diff, dyp)
            dd_ref[rows, :] = _dot(dyp, pwt_ref[0])
            return carry

        lax.fori_loop(0, nt, step, 0)
        _pool_cols(lambda off: dd_ref[pl.ds(off, POOL_TILE), :] * inv_ref[0, pl.ds(off, POOL_TILE), :],
                   mt_ref[0], cs_ref, L, 2)

        def step2(i, carry):
            off = pl.multiple_of(i * POOL_TILE, POOL_TILE)
            rows = pl.ds(off, POOL_TILE)
            dv_ref[0, rows, :] = (_pool_rows(cs_ref, off, half - 1, half, L) - dd_ref[rows, :]).astype(BF)
            return carry

        lax.fori_loop(0, nt, step2, 0)
        dv_ref[0, pl.ds(L, R - L), :] = jnp.zeros((R - L, PGW), BF)
        dz_ref[0, pl.ds(L, R - L), :] = jnp.zeros((R - L, PGW), BF)

    return pl.pallas_call(
        body, name="pool_bwd", grid=(4, 2),
        in_specs=[pl.BlockSpec((1, L, PGW), lambda g, e: (e, 0, g)),
                  pl.BlockSpec((1, L, PGW), lambda g, e: (e, 0, 4 + g)),
                  pl.BlockSpec((1, L, PGW), lambda g, e: (e, 0, g)),
                  pl.BlockSpec((1, PGW, PGW), lambda g, e: (g, 0, 0)),
                  pl.BlockSpec((1, PGW, PGW), lambda g, e: (g, 0, 0)),
                  pl.BlockSpec((1, PGW), lambda g, e: (0, g)),
                  pl.BlockSpec((1, POOL_TILE, POOL_TILE), lambda g, e: (g, 0, 0)),
                  pl.BlockSpec((1, POOL_TILE, POOL_TILE), lambda g, e: (g, 0, 0)),
                  pl.BlockSpec((1, L, 1), lambda g, e: (g, 0, 0))],
        out_specs=(pl.BlockSpec((1, R, PGW), lambda g, e: (e, 0, g)),
                   pl.BlockSpec((1, R, PGW), lambda g, e: (e, 0, g)),
                   pl.BlockSpec((1, PGW, PGW), lambda g, e: (g, 0, 0)),
                   pl.BlockSpec((1, 8, PGW), lambda g, e: (g, 0, 0))),
        out_shape=(jax.ShapeDtypeStruct((2, R, D), BF), jax.ShapeDtypeStruct((2, R, D), BF),
                   jax.ShapeDtypeStruct((4, PGW, PGW), F32), jax.ShapeDtypeStruct((4, 8, PGW), F32)),
        scratch_shapes=[pltpu.VMEM((L + GRID_W, PGW), F32), pltpu.VMEM((L, PGW), F32)],
        compiler_params=_params(("parallel", "arbitrary")),
    )(proj3, proj3, d_ypool, pool_w_bf, pool_wT_bf, pool_scale, mats, matsT, inv)


CONV_BLOCK = 128


def _conv_tap(u, k, L):
    off = k - 2
    if off == 0:
        return u
    R = u.shape[0]
    r = lax.broadcasted_iota(jnp.int32, (R, 1), 0)
    pos = jnp.where(r < L, r, r - L) + off
    seg = jnp.where(r < L, L, R - L)
    return jnp.where((pos >= 0) & (pos < seg), pltpu.roll(u, (-off) % R, 0), 0.0)


def _conv_fwd(proj3, conv_w, conv_b, L):
    _, R, _ = proj3.shape
    cb0 = OFF_XBC // CONV_BLOCK

    def body(u_ref, w_ref, b_ref, o_ref):
        u = u_ref[0].astype(F32)
        w = w_ref[...]
        pre = b_ref[...] + sum(_conv_tap(u, k, L) * w[k:k + 1, :] for k in range(4))
        o_ref[0] = (pre * _sigmoid(pre)).astype(BF)

    return pl.pallas_call(
        body, name="conv_fwd", grid=(2, CONV_DIM // CONV_BLOCK),
        in_specs=[pl.BlockSpec((1, R, CONV_BLOCK), lambda e, j: (e, 0, cb0 + j)),
                  pl.BlockSpec((4, CONV_BLOCK), lambda e, j: (0, j)),
                  pl.BlockSpec((1, CONV_BLOCK), lambda e, j: (0, j))],
        out_specs=pl.BlockSpec((1, R, CONV_BLOCK), lambda e, j: (e, 0, j)),
        out_shape=jax.ShapeDtypeStruct((2, R, CONV_DIM), BF),
        compiler_params=_params(("parallel", "parallel")),
    )(proj3, conv_w, conv_b)


def _conv_bwd(proj3, addends, scales, col0, ncols, in_maps, conv_w, conv_b, L, name):
    _, R, _ = proj3.shape
    cb0 = (OFF_XBC + col0) // CONV_BLOCK
    wb0 = col0 // CONV_BLOCK
    na = len(addends)
    scaled = [i for i in range(na) if scales[i] is not None]

    def body(*refs):
        u_ref, w_ref, b_ref = refs[0], refs[1], refs[2]
        a_refs = refs[3:3 + na]
        s_refs = dict(zip(scaled, refs[3 + na:3 + na + len(scaled)]))
        o_ref, acc_ref = refs[3 + na + len(scaled)], refs[4 + na + len(scaled)]
        u = u_ref[0].astype(F32)
        w = w_ref[...]
        taps = [_conv_tap(u, k, L) for k in range(4)]
        pre = b_ref[...] + sum(taps[k] * w[k:k + 1, :] for k in range(4))
        sg = _sigmoid(pre)
        dxbc = jnp.zeros(u.shape, F32)
        for i, a in enumerate(a_refs):
            t = a[0].astype(F32)
            dxbc = dxbc + (t * s_refs[i][...] if i in s_refs else t)
        dpre = dxbc * (sg * (1.0 + pre * (1.0 - sg)))
        acc_ref[...] = jnp.zeros_like(acc_ref)
        for k in range(4):
            acc_ref[0, k:k + 1, :] = jnp.sum(dpre * taps[k], axis=0, keepdims=True)
        acc_ref[0, 4:5, :] = jnp.sum(dpre, axis=0, keepdims=True)
        du = sum(_conv_tap(dpre, 4 - k, L) * w[k:k + 1, :] for k in range(4))
        o_ref[0] = du.astype(BF)

    in_specs = [pl.BlockSpec((1, R, CONV_BLOCK), lambda e, j: (e, 0, cb0 + j)),
                pl.BlockSpec((4, CONV_BLOCK), lambda e, j: (0, wb0 + j)),
                pl.BlockSpec((1, CONV_BLOCK), lambda e, j: (0, wb0 + j))]
    for m in in_maps:
        in_specs.append(pl.BlockSpec((1, R, CONV_BLOCK), functools.partial(lambda e, j, m: (e, 0, m(j)), m=m)))
    for i in scaled:
        in_specs.append(pl.BlockSpec((1, CONV_BLOCK), functools.partial(lambda e, j, m: (0, m(j)), m=in_maps[i])))
    return pl.pallas_call(
        body, name=name, grid=(2, ncols // CONV_BLOCK),
        in_specs=in_specs,
        out_specs=(pl.BlockSpec((1, R, CONV_BLOCK), lambda e, j: (e, 0, j)),
                   pl.BlockSpec((1, 8, CONV_BLOCK), lambda e, j: (e, 0, j))),
        out_shape=(jax.ShapeDtypeStruct((2, R, ncols), BF), jax.ShapeDtypeStruct((2, 8, ncols), F32)),
        compiler_params=_params(("parallel", "parallel")),
    )(proj3, conv_w, conv_b, *addends, *[scales[i] for i in scaled])


def _softplus(x):
    e = jnp.exp(-jnp.abs(x))
    u = 1.0 + e
    return jnp.maximum(x, 0.0) + jnp.where(u == 1.0, e, e * jnp.log(u) / (u - 1.0))


def _to_local_mat(g, transpose=False):
    r = lax.broadcasted_iota(jnp.int32, (128, 128), 1 if transpose else 0)
    c = lax.broadcasted_iota(jnp.int32, (128, 128), 0 if transpose else 1)
    return ((c < 2 * HPG) & (r == jnp.right_shift(c, 3) * (NG * HPG) + g * HPG + (c & (HPG - 1)))).astype(BF)


def _dt_fwd(dt_raw, bias128):
    _, R, _ = dt_raw.shape

    def body(x_ref, b_ref, o_ref):
        dt = _softplus(x_ref[0] + b_ref[...])
        for g in range(NG):
            o_ref[0, g] = _dot_sl(dt, _to_local_mat(g))

    tr = R // 4
    return pl.pallas_call(
        body, name="dt_fwd", grid=(2, 4),
        in_specs=[pl.BlockSpec((1, tr, 128), lambda e, t: (e, t, 0)), pl.BlockSpec((1, 128), lambda e, t: (0, 0))],
        out_specs=pl.BlockSpec((1, NG, tr, 128), lambda e, t: (e, 0, t, 0)),
        out_shape=jax.ShapeDtypeStruct((2, NG, R, 128), F32),
        compiler_params=_params(("parallel", "parallel")),
    )(dt_raw, bias128)


def _dt_bwd(dt_raw, bias128, ddt_f, ddt_b):
    _, R, _ = dt_raw.shape

    def body(x_ref, b_ref, f_ref, g_ref, o_ref, acc_ref):
        ddt = sum(_dot_sl(f_ref[0, g] + g_ref[0, g], _to_local_mat(g, transpose=True)) for g in range(NG))
        d = ddt * _sigmoid(x_ref[0] + b_ref[...])
        o_ref[0] = d.astype(BF)

        @pl.when(pl.program_id(1) == 0)
        def _():
            acc_ref[...] = jnp.zeros_like(acc_ref)

        acc_ref[0, 0:1, :] += jnp.sum(d, axis=0, keepdims=True)

    tr = R // 4
    blk = pl.BlockSpec((1, tr, 128), lambda e, t: (e, t, 0))
    loc = pl.BlockSpec((1, NG, tr, 128), lambda e, t: (e, 0, t, 0))
    return pl.pallas_call(
        body, name="dt_bwd", grid=(2, 4),
        in_specs=[blk, pl.BlockSpec((1, 128), lambda e, t: (0, 0)), loc, loc],
        out_specs=(blk, pl.BlockSpec((1, 8, 128), lambda e, t: (e, 0, 0))),
        out_shape=(jax.ShapeDtypeStruct(dt_raw.shape, BF), jax.ShapeDtypeStruct((2, 8, 128), F32)),
        compiler_params=_params(("parallel", "arbitrary")),
    )(dt_raw, bias128, ddt_f, ddt_b)


def _tri(d):
    i = lax.broadcasted_iota(jnp.int32, (Q, Q), 0)
    j = lax.broadcasted_iota(jnp.int32, (Q, Q), 1)
    return (i >= j) if d == 0 else (i <= j)


def _expand_mat(d):
    r = lax.broadcasted_iota(jnp.int32, (128, GWID), 0)
    c = lax.broadcasted_iota(jnp.int32, (128, GWID), 1)
    return (r == d * HPG + jnp.right_shift(c, 6)).astype(BF)


def _reduce_mat(d):
    r = lax.broadcasted_iota(jnp.int32, (GWID, 128), 0)
    c = lax.broadcasted_iota(jnp.int32, (GWID, 128), 1)
    return (c == d * HPG + jnp.right_shift(r, 6)).astype(BF)


def _ssd_chunk(d, dt, A, xs, B, C):
    mask = _tri(d)
    T = mask.astype(BF)
    Tt = _tri(1 - d).astype(BF)
    a = dt * A
    acs = _dot_sr(T, a)
    E = _expand_mat(d)
    dt_e = _dot_sl(dt, E, 2)
    acs_e = _dot_sl(acs, E, 2)
    alast_e = acs_e[Q - 1:Q, :] if d == 0 else acs_e[0:1, :]
    return dict(mask=mask, T=T, Tt=Tt, acs=acs, acsT=acs.T, dt_e=dt_e, acs_e=acs_e, lam=jnp.exp(acs_e),
                w=jnp.exp(alast_e - acs_e), decay=jnp.exp(alast_e), xt=xs * dt_e, CB=_dot_nt(C, B))


def _head_decay(q, d, hh):
    col = q["acs"][:, d * HPG + hh:d * HPG + hh + 1]
    row = q["acsT"][d * HPG + hh:d * HPG + hh + 1, :]
    return jnp.exp(jnp.where(q["mask"], col - row, -jnp.inf))


def _chunk_maps(NX, NS):
    cf = lambda s: lax.rem(s + NX, NS)
    cb = lambda s: NS - 1 - s
    return cf, cb


def _ssd_fwd(xbc, dt_loc, a_loc, L):
    _, R, _ = xbc.shape
    NX, NS = L // Q, R // Q
    cf, cb = _chunk_maps(NX, NS)

    def body(xs_f, b_f, c_f, dt_f, xs_b, b_b, c_b, dt_b, a_ref, y_f, hs_f, y_b, hs_b, hT):
        @pl.when(pl.program_id(2) == 0)
        def _():
            hT[...] = jnp.zeros_like(hT)

        A = a_ref[0, 0:1, :]
        lane = lax.broadcasted_iota(jnp.int32, (Q, 128), 1)
        for d, (xs_ref, b_ref, c_ref, dt_ref, y_ref, hs_ref) in enumerate(
                ((xs_f, b_f, c_f, dt_f, y_f, hs_f), (xs_b, b_b, c_b, dt_b, y_b, hs_b))):
            xs, B, C = xs_ref[0].astype(F32), b_ref[0], c_ref[0]
            q = _ssd_chunk(d, dt_ref[0, 0], A, xs, B, C)
            h = hT[d]
            hb = h.astype(BF)
            hs_ref[0, 0] = hb
            parts = []
            for pr in range(HPG // 2):
                xp = q["xt"][:, pr * 128:(pr + 1) * 128]
                xst = jnp.concatenate([jnp.where(lane < HEAD, xp, 0.0), jnp.where(lane < HEAD, 0.0, xp)], axis=0)
                mst = jnp.concatenate([(q["CB"] * _head_decay(q, d, 2 * pr)).astype(BF),
                                       (q["CB"] * _head_decay(q, d, 2 * pr + 1)).astype(BF)], axis=1)
                parts.append(_dot(mst, xst))
            y_ref[0] = jnp.concatenate(parts, axis=1) + _dot(C, hb) * q["lam"]
            hT[d] = q["decay"] * h + _dot_tn(B, q["xt"] * q["w"])

    def spec(shape, imap):
        return pl.BlockSpec(shape, imap)

    def ins(c):
        return [spec((1, Q, GWID), lambda e, g, s: (e, c(s), g)),
                spec((1, Q, NST), lambda e, g, s: (e, c(s), DIN // NST + g)),
                spec((1, Q, NST), lambda e, g, s: (e, c(s), DIN // NST + NG + g)),
                spec((1, 1, Q, 128), lambda e, g, s: (e, g, c(s), 0))]

    def outs(c):
        return [spec((1, Q, GWID), lambda e, g, s: (e, c(s), g)),
                spec((1, 1, NST, GWID), lambda e, g, s: (e, c(s), 0, g))]

    yshape = jax.ShapeDtypeStruct((2, R, DIN), F32)
    hshape = jax.ShapeDtypeStruct((2, NS, NST, DIN), BF)
    return pl.pallas_call(
        body, name="ssd_fwd", grid=(2, NG, NS),
        in_specs=ins(cf) + ins(cb) + [spec((1, 8, 128), lambda e, g, s: (g, 0, 0))],
        out_specs=tuple(outs(cf) + outs(cb)),
        out_shape=(yshape, hshape, yshape, hshape),
        scratch_shapes=[pltpu.VMEM((2, NST, GWID), F32)],
        compiler_params=_params(("parallel", "parallel", "arbitrary")),
    )(xbc, xbc, xbc, dt_loc, xbc, xbc, xbc, dt_loc, a_loc)


def _ssd_bwd(xbc, dt_loc, a_loc, hs_f, hs_b, y_f, y_b, dy, L):
    _, R, _ = xbc.shape
    NX, NS = L // Q, R // Q
    cf0, cb0 = _chunk_maps(NX, NS)
    cf = lambda sp: cf0(NS - 1 - sp)
    cb = lambda sp: cb0(NS - 1 - sp)

    def body(xs_f, b_f, c_f, dt_f, hs_f_, dy_f, y_f_, xs_b, b_b, c_b, dt_b, hs_b_, dy_b, y_b_, a_ref,
             dxs_f, dbc_f, ddt_f, dxs_b, dbc_b, ddt_b, da_ref, dhT):
        @pl.when(pl.program_id(2) == 0)
        def _():
            dhT[...] = jnp.zeros_like(dhT)
            da_ref[...] = jnp.zeros_like(da_ref)

        A = a_ref[0, 0:1, :]
        lane = lax.broadcasted_iota(jnp.int32, (Q, 128), 1)
        row = lax.broadcasted_iota(jnp.int32, (Q, 128), 0)
        for d, (xs_ref, b_ref, c_ref, dt_ref, hs_ref, dy_ref, y_ref, dxs_ref, dbc_ref, ddt_ref) in enumerate(
                ((xs_f, b_f, c_f, dt_f, hs_f_, dy_f, y_f_, dxs_f, dbc_f, ddt_f),
                 (xs_b, b_b, c_b, dt_b, hs_b_, dy_b, y_b_, dxs_b, dbc_b, ddt_b))):
            xs, B, C, dt = xs_ref[0].astype(F32), b_ref[0], c_ref[0], dt_ref[0, 0]
            q = _ssd_chunk(d, dt, A, xs, B, C)
            xt, lam, w, decay = q["xt"], q["lam"], q["w"], q["decay"]
            H = hs_ref[0, 0]
            dyv = dy_ref[0].astype(F32)
            dh = dhT[d]
            dZ = dyv * lam
            dC = _dot_nt(dZ, H)
            dH = _dot_tn(C, dZ)
            U = _dot(B, dh)
            xw = xt * w
            dxt = U * w
            dalast_e = (jnp.sum(U * xw, axis=0, keepdims=True)
                        + decay * jnp.sum(dh * H.astype(F32), axis=0, keepdims=True))
            dB = _dot_nt(xw, dh)
            dCB = jnp.zeros((Q, Q), F32)
            dxt_parts = []
            for pr in range(HPG // 2):
                xp = xt[:, pr * 128:(pr + 1) * 128]
                dyp = dyv[:, pr * 128:(pr + 1) * 128]
                L0, L1 = _head_decay(q, d, 2 * pr), _head_decay(q, d, 2 * pr + 1)
                dyst = jnp.concatenate([jnp.where(lane < HEAD, dyp, 0.0), jnp.where(lane < HEAD, 0.0, dyp)], axis=0)
                mst = jnp.concatenate([(q["CB"] * L0).astype(BF), (q["CB"] * L1).astype(BF)], axis=0)
                dxt_parts.append(_dot_tn(mst, dyst))
                dmst = _dot_nt(dyst, xp)
                dCB = dCB + dmst[:Q] * L0 + dmst[Q:] * L1
            dxt_diag = jnp.concatenate(dxt_parts, axis=1)
            dC = dC + _dot(dCB, B)
            dB = dB + _dot_tn(dCB, C)
            Rm = _reduce_mat(d)
            dacs = _dot_sl(dyv * y_ref[0] - xt.astype(BF).astype(F32) * dxt_diag - U * xw, Rm, 2)
            dxt = dxt + dxt_diag
            dal = _dot_sl(jnp.broadcast_to(dalast_e, (8, GWID)), Rm, 2)[0:1, :]
            dacs = dacs + jnp.where(row == (Q - 1 if d == 0 else 0), dal, 0.0)
            da = _dot_sr(q["Tt"], dacs, 2)
            ddt_ref[0, 0] = da * A + _dot_sl(dxt * xs, Rm, 2)
            da_ref[0, 0, 0:1, :] += jnp.sum(da * dt, axis=0, keepdims=True)
            dxs_ref[0] = (dxt * q["dt_e"]).astype(BF)
            dbc_ref[0] = jnp.concatenate([dB, dC], axis=1).astype(BF)
            dhT[d] = decay * dh + dH

    def spec(shape, imap):
        return pl.BlockSpec(shape, imap)

    def ins(c):
        return [spec((1, Q, GWID), lambda e, g, s: (e, c(s), g)),
                spec((1, Q, NST), lambda e, g, s: (e, c(s), DIN // NST + g)),
                spec((1, Q, NST), lambda e, g, s: (e, c(s), DIN // NST + NG + g)),
                spec((1, 1, Q, 128), lambda e, g, s: (e, g, c(s), 0)),
                spec((1, 1, NST, GWID), lambda e, g, s: (e, c(s), 0, g)),
                spec((1, Q, GWID), lambda e, g, s: (e, c(s), g)),
                spec((1, Q, GWID), lambda e, g, s: (e, c(s), g))]

    def outs(c):
        return [spec((1, Q, GWID), lambda e, g, s: (e, c(s), g)),
                spec((1, Q, 2 * NST), lambda e, g, s: (e, c(s), g)),
                spec((1, 1, Q, 128), lambda e, g, s: (e, g, c(s), 0))]

    s_xs = jax.ShapeDtypeStruct((2, R, DIN), BF)
    s_bc = jax.ShapeDtypeStruct((2, R, 2 * NG * NST), BF)
    s_dt = jax.ShapeDtypeStruct((2, NG, R, 128), F32)
    return pl.pallas_call(
        body, name="ssd_bwd", grid=(2, NG, NS),
        in_specs=ins(cf) + ins(cb) + [spec((1, 8, 128), lambda e, g, s: (g, 0, 0))],
        out_specs=tuple(outs(cf) + outs(cb) + [spec((1, 1, 8, 128), lambda e, g, s: (e, g, 0, 0))]),
        out_shape=(s_xs, s_bc, s_dt, s_xs, s_bc, s_dt, jax.ShapeDtypeStruct((2, NG, 8, 128), F32)),
        scratch_shapes=[pltpu.VMEM((2, NST, GWID), F32)],
        compiler_params=_params(("parallel", "parallel", "arbitrary")),
    )(xbc, xbc, xbc, dt_loc, hs_f, dy, y_f, xbc, xbc, xbc, dt_loc, hs_b, dy, y_b, a_loc)


def _ssd_post_fwd(y_f, y_b, xbc, proj3, dskip_e, ssd_norm, L):
    def body(yf_ref, yb_ref, xs_ref, z_ref, ds_ref, w_ref, o_ref):
        y2 = yf_ref[0] + yb_ref[0] + ds_ref[...] * xs_ref[0].astype(F32)
        z = z_ref[0].astype(F32)
        u = y2 * (z * _sigmoid(z))
        parts = []
        for g in range(NG):
            ug = u[:, g * GWID:(g + 1) * GWID]
            parts.append(ug * lax.rsqrt(jnp.mean(ug * ug, axis=-1, keepdims=True) + EPS))
        o_ref[0] = (jnp.concatenate(parts, axis=1) * w_ref[...]).astype(BF)

    blk = lambda c: pl.BlockSpec((1, ROW_TILE, DIN), lambda e, t: (e, t, c))
    vec = pl.BlockSpec((1, DIN), lambda e, t: (0, 0))
    return pl.pallas_call(
        body, name="ssd_post_fwd", grid=(2, L // ROW_TILE),
        in_specs=[blk(0), blk(0), blk(0), blk(1), vec, vec],
        out_specs=blk(0),
        out_shape=jax.ShapeDtypeStruct((2, L, DIN), BF),
        compiler_params=_params(("parallel", "parallel")),
    )(y_f, y_b, xbc, proj3, dskip_e, ssd_norm)


def _ssd_post_bwd(d_yn, y_f, y_b, xbc, proj3, dskip_e, ssd_norm, L):
    _, R, _ = y_f.shape
    nx = L // ROW_TILE

    def body(dyn_ref, yf_ref, yb_ref, xs_ref, z_ref, ds_ref, w_ref, dy_ref, dz_ref, acc_ref):
        t = pl.program_id(1)

        @pl.when(t == 0)
        def _():
            acc_ref[...] = jnp.zeros_like(acc_ref)

        @pl.when(t >= nx)
        def _():
            dy_ref[...] = jnp.zeros_like(dy_ref)
            dz_ref[...] = jnp.zeros_like(dz_ref)

        @pl.when(t < nx)
        def _():
            xs = xs_ref[0].astype(F32)
            y2 = yf_ref[0] + yb_ref[0] + ds_ref[...] * xs
            z = z_ref[0].astype(F32)
            sg = _sigmoid(z)
            sz = z * sg
            u = y2 * sz
            dyn = dyn_ref[0].astype(F32)
            dun = dyn * w_ref[...]
            uh_parts, du_parts = [], []
            for g in range(NG):
                sl = slice(g * GWID, (g + 1) * GWID)
                ug = u[:, sl]
                rg = lax.rsqrt(jnp.mean(ug * ug, axis=-1, keepdims=True) + EPS)
                uh = ug * rg
                dg = dun[:, sl]
                du_parts.append(rg * (dg - uh * jnp.mean(dg * uh, axis=-1, keepdims=True)))
                uh_parts.append(uh)
            du = jnp.concatenate(du_parts, axis=1)
            uh = jnp.concatenate(uh_parts, axis=1)
            dy2 = du * sz
            dy_ref[0] = dy2.astype(BF)
            dz_ref[0] = (du * y2 * (sg * (1.0 + z * (1.0 - sg)))).astype(BF)
            acc_ref[0, 0:1, :] += jnp.sum(dyn * uh, axis=0, keepdims=True)
            acc_ref[0, 1:2, :] += jnp.sum(dy2 * xs, axis=0, keepdims=True)

    xmap = lambda c: (lambda e, t: (e, jnp.minimum(t, nx - 1), c))
    blk = lambda c: pl.BlockSpec((1, ROW_TILE, DIN), xmap(c))
    oblk = pl.BlockSpec((1, ROW_TILE, DIN), lambda e, t: (e, t, 0))
    vec = pl.BlockSpec((1, DIN), lambda e, t: (0, 0))
    return pl.pallas_call(
        body, name="ssd_post_bwd", grid=(2, R // ROW_TILE),
        in_specs=[blk(0), blk(0), blk(0), blk(0), blk(1), vec, vec],
        out_specs=(oblk, oblk, pl.BlockSpec((1, 8, DIN), lambda e, t: (e, 0, 0))),
        out_shape=(jax.ShapeDtypeStruct((2, R, DIN), BF), jax.ShapeDtypeStruct((2, R, DIN), BF),
                   jax.ShapeDtypeStruct((2, 8, DIN), F32)),
        compiler_params=_params(("parallel", "arbitrary")),
    )(d_yn, y_f, y_b, xbc, proj3, dskip_e, ssd_norm)


def _merge_fwd(proj3, P, S, b_merge, L):
    def body(gp_ref, p_ref, s_ref, b_ref, o_ref):
        gt = _sigmoid(gp_ref[0].astype(F32) + b_ref[...])
        o_ref[0] = (gt[:, :D] * p_ref[0].astype(F32) + gt[:, D:] * s_ref[0].astype(F32)).astype(BF)

    blk = pl.BlockSpec((1, ROW_TILE, D), lambda e, t: (e, t, 0))
    return pl.pallas_call(
        body, name="merge_fwd", grid=(2, L // ROW_TILE),
        in_specs=[pl.BlockSpec((1, ROW_TILE, 2 * D), lambda e, t: (e, t, OFF_GATE // (2 * D))), blk, blk,
                  pl.BlockSpec((1, 2 * D), lambda e, t: (0, 0))],
        out_specs=blk, out_shape=jax.ShapeDtypeStruct((2, L, D), BF),
        compiler_params=_params(("parallel", "parallel")),
    )(proj3, P, S, b_merge)


def _merge_bwd(d_merged, proj3, P, S, b_merge, L):
    _, R, _ = proj3.shape
    nx = L // ROW_TILE

    def body(dm_ref, gp_ref, p_ref, s_ref, b_ref, dp_ref, ds_ref, dg_ref, acc_ref):
        t = pl.program_id(1)

        @pl.when(t == 0)
        def _():
            acc_ref[...] = jnp.zeros_like(acc_ref)

        @pl.when(t >= nx)
        def _():
            dg_ref[...] = jnp.zeros_like(dg_ref)

        @pl.when(t < nx)
        def _():
            gt = _sigmoid(gp_ref[0].astype(F32) + b_ref[...])
            dm = dm_ref[0].astype(F32)
            g1, g2 = gt[:, :D], gt[:, D:]
            dp_ref[0] = (dm * g1).astype(BF)
            ds_ref[0] = (dm * g2).astype(BF)
            dgp = jnp.concatenate([dm * p_ref[0].astype(F32) * g1 * (1.0 - g1),
                                   dm * s_ref[0].astype(F32) * g2 * (1.0 - g2)], axis=1)
            dg_ref[0] = dgp.astype(BF)
            acc_ref[0, 0:1, :] += jnp.sum(dgp, axis=0, keepdims=True)

    xmap = lambda e, t: (e, jnp.minimum(t, nx - 1), 0)
    blk = pl.BlockSpec((1, ROW_TILE, D), xmap)
    return pl.pallas_call(
        body, name="merge_bwd", grid=(2, R // ROW_TILE),
        in_specs=[blk, pl.BlockSpec((1, ROW_TILE, 2 * D), lambda e, t: (e, jnp.minimum(t, nx - 1), OFF_GATE // (2 * D))),
                  blk, blk, pl.BlockSpec((1, 2 * D), lambda e, t: (0, 0))],
        out_specs=(blk, blk, pl.BlockSpec((1, ROW_TILE, 2 * D), lambda e, t: (e, t, 0)),
                   pl.BlockSpec((1, 8, 2 * D), lambda e, t: (e, 0, 0))),
        out_shape=(jax.ShapeDtypeStruct((2, L, D), BF), jax.ShapeDtypeStruct((2, L, D), BF),
                   jax.ShapeDtypeStruct((2, R, 2 * D), BF), jax.ShapeDtypeStruct((2, 8, 2 * D), F32)),
        compiler_params=_params(("parallel", "arbitrary")),
    )(d_merged, proj3, P, S, b_merge)


def _final(out3, x, tgt, gtab, norm_post, L):
    def body(o_ref, x_ref, t_ref, g_ref, n_ref, dxo_ref, do_ref, acc_ref):
        @pl.when(pl.program_id(1) == 0)
        def _():
            acc_ref[...] = jnp.zeros_like(acc_ref)

        o = o_ref[0].astype(F32)
        gate = g_ref[0, 0:1, :]
        npost = n_ref[...]
        r2 = lax.rsqrt(jnp.mean(o * o, axis=-1, keepdims=True) + EPS)
        nh = o * r2
        on = nh * npost
        err = x_ref[0] + gate * on - t_ref[0]
        dxo = err * (1.0 / D)
        dxo_ref[0] = dxo
        dnh = dxo * gate * npost
        do_ref[0] = (r2 * (dnh - nh * jnp.mean(dnh * nh, axis=-1, keepdims=True))).astype(BF)
        acc_ref[0, 0:1, :] += jnp.sum(dxo * on, axis=0, keepdims=True)
        acc_ref[0, 1:2, :] += jnp.sum(dxo * gate * nh, axis=0, keepdims=True)
        acc_ref[0, 2:3, :] += jnp.sum(err * err, axis=0, keepdims=True)

    blk = pl.BlockSpec((1, ROW_TILE, D), lambda e, t: (e, t, 0))
    return pl.pallas_call(
        body, name="final", grid=(2, L // ROW_TILE),
        in_specs=[blk, blk, blk, pl.BlockSpec((1, 8, D), lambda e, t: (e, 0, 0)),
                  pl.BlockSpec((1, D), lambda e, t: (0, 0))],
        out_specs=(blk, blk, pl.BlockSpec((1, 8, D), lambda e, t: (e, 0, 0))),
        out_shape=(jax.ShapeDtypeStruct((2, L, D), F32), jax.ShapeDtypeStruct((2, L, D), BF),
                   jax.ShapeDtypeStruct((2, 8, D), F32)),
        compiler_params=_params(("parallel", "arbitrary")),
    )(out3, x, tgt, gtab, norm_post)


def _local_step(x, c, ctx, loss_target, W, late_shard=None, exchange=False):
    nb, L, _ = x.shape
    LC = ctx.shape[1]
    R = L + LC
    assert nb == 2 and L % ROW_TILE == 0 and LC % Q == 0 and L % POOL_TILE == 0
    w_inT = W["w_in"]
    w_dtT = jnp.pad(w_inT[OFF_DT:], ((0, 64), (0, 0)))
    tables = _pool_tables(L)
    tr, tl = (2 * R) // 8, (2 * L) // 8

    c16 = jnp.zeros((16, D), F32).at[0:2].set(c).at[2].set(W["c_ctx"])
    mod16 = _adaln_fwd(c16, W["w_ada"], W["b_ada"])
    shift, scale, gate = mod16[:, :D], mod16[:, D:2 * D], mod16[:, 2 * D:]
    npre = W["norm_pre"]
    tab = jnp.zeros((2, 2, 8, D), F32)
    for e in range(2):
        tab = tab.at[e, 0, 0].set(npre[0] * (1.0 + scale[e])).at[e, 0, 1].set(shift[e])
        tab = tab.at[e, 1, 0].set(npre[0] * (1.0 + scale[2])).at[e, 1, 1].set(shift[2])
    gtab = jnp.zeros((2, 8, D), F32).at[:, 0].set(gate[0:2])

    hx = _norm_mod_fwd(x, ctx, tab)
    hx2 = hx.reshape(2 * R, D)
    if late_shard is None:
        proj = _matmul(hx2, w_inT, BF, "proj_main", tm=tr, tn=1024, bt=True, n=OFF_DT)
    else:
        proj, late = _matmul(hx2, w_inT, BF, "proj_main", tm=tr, tn=1024, bt=True, n=OFF_DT, side=_gather_side(late_shard))
        W = {**W, **_unpack_gather(late, GATHER_LATE)}
    proj3 = proj.reshape(2, R, OFF_DT)
    dt_raw = _matmul(hx2, w_dtT, F32, "proj_dt", tm=tr, bt=True).reshape(2, R, 128)
    ypool = _pool_fwd(proj3, W["pool_w"], W["pool_scale"], tables, L)
    xbc = _conv_fwd(proj3, W["conv_w"], W["conv_b"], L)
    bias128 = jnp.pad(W["dt_bias"].reshape(1, 64), ((0, 0), (0, 64)))
    dt_loc = _dt_fwd(dt_raw, bias128)
    A = -jnp.exp(W["a_log"].reshape(2, NG, HPG))
    a_loc = jnp.zeros((NG, 8, 128), F32).at[:, 0, :16].set(A.transpose(1, 0, 2).reshape(NG, 16))
    y_f, hs_f, y_b, hs_b = _ssd_fwd(xbc, dt_loc, a_loc, L)
    dskip_e = jnp.repeat(W["d_skip"].reshape(1, 32), HEAD, axis=1)
    yn = _ssd_post_fwd(y_f, y_b, xbc, proj3, dskip_e, W["ssd_norm"], L)
    ypool2, yn2 = ypool.reshape(2 * L, D), yn.reshape(2 * L, DIN)
    P = _matmul(ypool2, W["w_proj_pool"], BF, "proj_pool", tm=tl, tn=1024).reshape(2, L, D)
    S = _matmul(yn2, W["w_proj_ssd"], BF, "proj_ssd", tm=tl, tn=1024).reshape(2, L, D)
    merged = _merge_fwd(proj3, P, S, W["b_merge"], L)
    merged2 = merged.reshape(2 * L, D)
    out3 = _matmul(merged2, W["w_out"], BF, "proj_out", tm=tl, tn=1024).reshape(2, L, D)
    dxo, dout, acc_f = _final(out3, x, loss_target, gtab, W["norm_post"], L)

    dout2 = dout.reshape(2 * L, D)
    g = {}
    g["w_out"] = _matmul_tn(merged2, dout2, "dw_out", ta=1024, tn=1024, tr=tl)
    d_merged = _matmul(dout2, W["w_out"], BF, "d_merged", tm=tl, tn=1024, bt=True).reshape(2, L, D)
    dP, dS, dgp, acc_m = _merge_bwd(d_merged, proj3, P, S, W["b_merge"], L)
    dP2, dS2 = dP.reshape(2 * L, D), dS.reshape(2 * L, D)
    g["w_proj_pool"] = _matmul_tn(ypool2, dP2, "dw_proj_pool", ta=1024, tn=1024, tr=tl)
    g["w_proj_ssd"] = _matmul_tn(yn2, dS2, "dw_proj_ssd", ta=1024, tn=1024, tr=tl)
    d_ypool = _matmul(dP2, W["w_proj_pool"], BF, "d_ypool", tm=tl, tn=1024, bt=True).reshape(2, L, D)
    d_yn = _matmul(dS2, W["w_proj_ssd"], BF, "d_yn", tm=tl, tn=1024, bt=True).reshape(2, L, DIN)
    dv, dzp, g["pool_w"], acc_p = _pool_bwd(proj3, d_ypool, W["pool_w"], jnp.swapaxes(W["pool_w"], 1, 2),
                                            W["pool_scale"], tables, L)
    dy2, dzs, acc_s = _ssd_post_bwd(d_yn, y_f, y_b, xbc, proj3, dskip_e, W["ssd_norm"], L)
    dxs_f, dbc_f, ddt_f, dxs_b, dbc_b, ddt_b, acc_a = _ssd_bwd(xbc, dt_loc, a_loc, hs_f, hs_b, y_f, y_b, dy2, L)
    ident = lambda j: j
    dxr_xs, acc_cx = _conv_bwd(proj3, [dxs_f, dxs_b, dy2], [None, None, dskip_e], 0, DIN, [ident, ident, ident],
                               W["conv_w"], W["conv_b"], L, "conv_bwd_xs")
    bcmap = lambda j: 2 * lax.rem(j, NG) + j // NG
    dxr_bc, acc_cb = _conv_bwd(proj3, [dbc_f, dbc_b], [None, None], DIN, 2 * NG * NST, [bcmap, bcmap],
                               W["conv_w"], W["conv_b"], L, "conv_bwd_bc")
    ddtr, acc_d = _dt_bwd(dt_raw, bias128, ddt_f, ddt_b)
    pieces = [dv, dzp, dzs, dgp, dxr_xs, dxr_bc]
    dw_rows = [_matmul_tn(p.reshape(2 * R, p.shape[2]), hx2, "dw_in_%d" % i, ta=1024, tn=1024, tr=tr)
               for i, p in enumerate(pieces)]
    dw_rows.append(_matmul_tn(ddtr.reshape(2 * R, 128), hx2, "dw_in_dt", ta=128, tn=1024, tr=tr)[:64])
    g["w_in"] = jnp.concatenate(dw_rows, axis=0)
    acc_c = jnp.concatenate([acc_cx[0] + acc_cx[1], acc_cb[0] + acc_cb[1]], axis=1)
    g["conv_w"] = acc_c[0:4]
    g["conv_b"] = acc_c[4:5]
    if exchange:
        gb = _pack_grads(g, GRADS_EARLY)
        pair = _pair_add(gb, _pair_exchange(gb, None, "grads_pair_exchange_early"), "grads_pair_add_early")
        dh, recv_early = _dhx(pieces, ddtr, w_inT, w_dtT, side=_chip_exchange_side(pair))
    else:
        dh, recv_early = _dhx(pieces, ddtr, w_inT, w_dtT), None
    grad_x, acc_n = _norm_mod_bwd(dh, x, ctx, tab, dxo)
    g["w_ada"], db_rows, sm_rows = _adaln_bwd(acc_n, acc_f, mod16, c16, npre, W["w_ada"])

    g["b_ada"] = db_rows[0:1]
    g["norm_pre"] = sm_rows[0:1]
    g["c_ctx"] = sm_rows[1]
    g["norm_post"] = acc_f[0, 1:2] + acc_f[1, 1:2]
    g["b_merge"] = acc_m[0, 0:1] + acc_m[1, 0:1]
    g["pool_scale"] = acc_p[:, 0, :].reshape(1, D)
    g["dt_bias"] = (acc_d[0, 0, :64] + acc_d[1, 0, :64]).reshape(2, 32)
    dA = (acc_a[0, :, 0, :16] + acc_a[1, :, 0, :16]).reshape(NG, 2, HPG).transpose(1, 0, 2)
    g["a_log"] = (dA * A).reshape(2, 32)
    g["d_skip"] = (acc_s[0, 1] + acc_s[1, 1]).reshape(32, HEAD).sum(axis=1).reshape(1, 32)
    g["ssd_norm"] = acc_s[0, 0:1] + acc_s[1, 0:1]
    loss_lanes = acc_f[:, 2, :]
    return loss_lanes, grad_x, g, recv_early


MESH = pl.DeviceIdType.MESH
ANY = pl.BlockSpec(memory_space=pl.ANY)


def _all_gather(shard):
    m_per, n = shard.shape

    def body(x_ref, out_ref, send_sems, recv_sems, local_sem):
        x, y, c = lax.axis_index("x"), lax.axis_index("y"), lax.axis_index("c")
        me, sibling = (x, y, c), (x, y, 1 - c)
        chips = [(1 - x, y), (x, 1 - y), (1 - x, 1 - y)]

        def rows(px, py, pc):
            return out_ref.at[pl.ds((4 * px + 2 * py + pc) * m_per, m_per), :]

        def copy(k, block, to, src=None):
            return pltpu.make_async_remote_copy(
                src_ref=rows(*block) if src is None else src, dst_ref=rows(*block),
                send_sem=send_sems.at[k], recv_sem=recv_sems.at[k], device_id=to, device_id_type=MESH)

        mine = pltpu.make_async_copy(x_ref, rows(*me), local_sem)
        mine.start()
        first = [copy(0, me, sibling, src=x_ref)]
        first += [copy(1 + j, me, (*chip, c), src=x_ref) for j, chip in enumerate(chips)]
        for cp in first:
            cp.start()
        passed = [copy(4 + j, (*chip, c), sibling) for j, chip in enumerate(chips)]
        for j, chip in enumerate(chips):
            copy(1 + j, (*chip, c), me).wait_recv()
            passed[j].start()
        copy(0, sibling, me).wait_recv()
        for j, chip in enumerate(chips):
            copy(4 + j, (*chip, 1 - c), me).wait_recv()
        for cp in first + passed:
            cp.wait_send()
        mine.wait()

    return pl.pallas_call(
        body, name="all_gather_weights",
        out_shape=jax.ShapeDtypeStruct((NDEV * m_per, n), shard.dtype),
        in_specs=[ANY], out_specs=ANY,
        scratch_shapes=[pltpu.SemaphoreType.DMA((7,)), pltpu.SemaphoreType.DMA((7,)), pltpu.SemaphoreType.DMA],
    )(shard)


PAIR_PIECES = 4


def _xor_peer(k, x, y, c):
    return (1 - x if k & 4 else x, 1 - y if k & 2 else y, 1 - c if k & 1 else c)


def _pair_exchange(big, small, name):
    _, nq, rows, n = big.shape
    piece = rows // PAIR_PIECES
    assert piece * PAIR_PIECES == rows and piece % 16 == 0
    with_small = small is not None

    def body(*refs):
        if with_small:
            big_ref, small_ref, got_ref, osmall_ref, send_sems, recv_sems, local_sem = refs
        else:
            big_ref, got_ref, send_sems, recv_sems, local_sem = refs
        x, y, c = lax.axis_index("x"), lax.axis_index("y"), lax.axis_index("c")
        me = 4 * x + 2 * y + c

        def rc(src, dst, sem, peer):
            return pltpu.make_async_remote_copy(src_ref=src, dst_ref=dst, send_sem=send_sems.at[sem],
                                                recv_sem=recv_sems.at[sem], device_id=peer, device_id_type=MESH)

        sib = _xor_peer(1, x, y, c)
        local, sends, recvs = [], [], []
        for q in range(nq):
            for h in range(PAIR_PIECES):
                rws = pl.ds(h * piece, piece)
                cp = rc(big_ref.at[1 - c, q, rws], got_ref.at[q, rws], 8 + q * PAIR_PIECES + h, sib)
                sends.append(cp)
                recvs.append(cp)
        if with_small:
            local.append(pltpu.make_async_copy(small_ref, osmall_ref.at[me], local_sem))
            for k in range(1, NDEV):
                px, py, pc = _xor_peer(k, x, y, c)
                sends.append(rc(small_ref, osmall_ref.at[me], k, (px, py, pc)))
                recvs.append(rc(small_ref, osmall_ref.at[4 * px + 2 * py + pc], k, (px, py, pc)))
        for cp in local + sends:
            cp.start()
        for cp in sends:
            cp.wait_send()
        for cp in recvs:
            cp.wait_recv()
        for cp in local:
            cp.wait()

    nsem = 8 + nq * PAIR_PIECES
    out_shape = [jax.ShapeDtypeStruct(big.shape[1:], big.dtype)]
    if with_small:
        out_shape.append(jax.ShapeDtypeStruct((NDEV,) + small.shape, small.dtype))
    out = pl.pallas_call(
        body, name=name, out_shape=tuple(out_shape),
        in_specs=[ANY] * (1 + with_small), out_specs=(ANY,) * (1 + with_small),
        scratch_shapes=[pltpu.SemaphoreType.DMA((nsem,)), pltpu.SemaphoreType.DMA((nsem,)), pltpu.SemaphoreType.DMA],
    )(*((big, small) if with_small else (big,)))
    return out if with_small else out[0]


def _pair_add(big, got, name):
    _, nq, rows, n = big.shape
    tile = rows // 4
    assert rows % 64 == 0

    def body(c_ref, a_ref, b_ref, o_ref):
        o_ref[0] = (a_ref[0, 0].astype(F32) + b_ref[0].astype(F32)).astype(BF)

    blk = pl.BlockSpec((1, tile, n), lambda q, i, c_ref: (q, i, 0))
    return pl.pallas_call(
        body, name=name,
        grid_spec=pltpu.PrefetchScalarGridSpec(
            num_scalar_prefetch=1, grid=(nq, rows // tile),
            in_specs=[pl.BlockSpec((1, 1, tile, n), lambda q, i, c_ref: (c_ref[0], q, i, 0)), blk], out_specs=blk),
        out_shape=jax.ShapeDtypeStruct(got.shape, BF), compiler_params=_params(("parallel", "parallel")),
    )(lax.axis_index("c").astype(jnp.int32).reshape(1), big, got)


def _chip_exchange_side(pair):
    def make(in_refs, out_refs, send_sems, recv_sems, local_sem, arrivals=True):
        (in_ref,), (out_ref,) = in_refs, out_refs
        x, y, c = lax.axis_index("x"), lax.axis_index("y"), lax.axis_index("c")
        q = 2 * x + y
        local = [pltpu.make_async_copy(in_ref.at[q], out_ref.at[q], local_sem)]
        sends, recvs = [], []
        for j in range(1, 4):
            px, py, pc = _xor_peer(2 * j, x, y, c)
            pq = 2 * px + py
            for lst, dst in ((sends, out_ref.at[q]), (recvs, out_ref.at[pq]))[:1 + arrivals]:
                lst.append(pltpu.make_async_remote_copy(
                    src_ref=in_ref.at[pq], dst_ref=dst, send_sem=send_sems.at[j - 1], recv_sem=recv_sems.at[j - 1],
                    device_id=(px, py, pc), device_id_type=MESH))
        return local, sends, recvs

    return _SideCopies([pair], [jax.ShapeDtypeStruct(pair.shape, pair.dtype)], make)


def _gather_side(shard):
    def make(in_refs, out_refs, send_sems, recv_sems, local_sem, arrivals=True):
        (src,), (dst,) = in_refs, out_refs
        x, y, c = lax.axis_index("x"), lax.axis_index("y"), lax.axis_index("c")
        me = 4 * x + 2 * y + c
        local = [pltpu.make_async_copy(src, dst.at[me], local_sem)]
        sends, recvs = [], []
        for k in range(1, NDEV):
            px, py, pc = _xor_peer(k, x, y, c)
            for lst, slot in ((sends, me), (recvs, 4 * px + 2 * py + pc))[:1 + arrivals]:
                lst.append(pltpu.make_async_remote_copy(
                    src_ref=src, dst_ref=dst.at[slot], send_sem=send_sems.at[k - 1], recv_sem=recv_sems.at[k - 1],
                    device_id=(px, py, pc), device_id_type=MESH))
        return local, sends, recvs

    return _SideCopies([shard], [jax.ShapeDtypeStruct((NDEV,) + shard.shape, shard.dtype)], make)


ADAM_TILE = 64
PACK_W = 1024


def _adamw(recv, w, m, v, name, side=None):
    rp = w.shape[0]
    tile = min(ADAM_TILE, rp)
    nsrc = recv.shape[0]
    grid = (rp // tile,)
    n_si, n_so = (len(side.inputs), len(side.out_shapes)) if side else (0, 0)

    def body(*refs):
        r_ref, w_ref, m_ref, v_ref = refs[:4]
        g_ref, d_ref, nm_ref, nv_ref = refs[4 + n_si:8 + n_si]
        side_refs = (refs[4:4 + n_si], refs[8 + n_si:8 + n_si + n_so], refs[8 + n_si + n_so:])
        if side:
            side.start(grid, *side_refs)
        g = r_ref[0].astype(F32)
        for i in range(1, nsrc):
            g = g + r_ref[i].astype(F32)
        m1 = ADAM_B1 * m_ref[...] + (1.0 - ADAM_B1) * g
        v1 = ADAM_B2 * v_ref[...] + (1.0 - ADAM_B2) * (g * g)
        m_hat = m1 / (1.0 - ADAM_B1 ** ADAM_STEP)
        v_hat = v1 / (1.0 - ADAM_B2 ** ADAM_STEP)
        g_ref[...] = g
        d_ref[...] = -ADAM_LR * (m_hat / (jnp.sqrt(v_hat) + ADAM_EPS) + ADAM_WD * w_ref[...])
        nm_ref[...] = m1
        nv_ref[...] = v1
        if side:
            side.wait(grid, *side_refs)

    blk = pl.BlockSpec((tile, PACK_W), lambda i: (i, 0))
    shp = jax.ShapeDtypeStruct((rp, PACK_W), F32)
    return pl.pallas_call(
        body, name=name, grid=grid,
        in_specs=[pl.BlockSpec((nsrc, tile, PACK_W), lambda i: (0, i, 0)), blk, blk, blk] + [ANY] * n_si,
        out_specs=(blk, blk, blk, blk) + (ANY,) * n_so,
        out_shape=(shp, shp, shp, shp) + tuple(side.out_shapes if side else ()),
        scratch_shapes=side.scratch() if side else [],
        compiler_params=_params(("arbitrary",) if side else ("parallel",)),
    )(recv, w, m, v, *(side.inputs if side else ()))


BIG = {"w_ada": ((3 * D, D), 0), "pool_w": ((4, PGW, PGW), 1), "w_proj_pool": ((D, D), 0), "w_proj_ssd": ((DIN, D), 0),
       "w_out": ((D, D), 0), "w_in": ((IN_COLS, D), 0), "conv_w": ((4, CONV_DIM), 1)}
TRANSPOSED = ("w_ada", "w_in")
PACK_ROWS = {"w_ada": 384, "w_in": 1168, "conv_w": 16, "pool_w": 32, "w_proj_pool": 128, "w_proj_ssd": 256, "w_out": 128}
GATHER_EARLY = ("w_ada", "w_in", "conv_w")
GATHER_LATE = ("pool_w", "w_proj_pool", "w_proj_ssd", "w_out")
GRADS_LATE = ("w_ada",)
GRADS_EARLY = tuple(n for n in PACK_ROWS if n not in GRADS_LATE)
SMALL = {"c_ctx": (D,), "b_ada": (1, 3 * D), "norm_pre": (1, D), "norm_post": (1, D), "b_merge": (1, 2 * D),
         "pool_scale": (1, D), "conv_b": (1, CONV_DIM), "dt_bias": (2, 32), "a_log": (2, 32), "d_skip": (1, 32),
         "ssd_norm": (1, DIN)}
LOSS_SLOT = 128
assert all(_r % 16 == 0 for _r in PACK_ROWS.values())
SMALL_ROWS = 16


def _shard_shape(name):
    shape, ax = BIG[name]
    return tuple(s // NDEV if i == ax else s for i, s in enumerate(shape))


def _as_rows(t, rows):
    pad = [(0, 0)] * (t.ndim - 1) + [(0, rows * PACK_W - t.shape[-1])]
    return jnp.pad(t, pad).reshape(t.shape[:-1] + (rows, PACK_W))


def _shard_rows(t, name):
    sh, r = _shard_shape(name), PACK_ROWS[name]
    lead = t.shape[:t.ndim - len(sh)]
    if len(sh) == 2 and sh[1] == PACK_W:
        return jnp.pad(t, [(0, 0)] * len(lead) + [(0, r - sh[0]), (0, 0)])
    if int(np.prod(sh)) == r * PACK_W:
        return t.reshape(lead + (r, PACK_W))
    return _as_rows(t.reshape(lead + (-1,)), r)


def _to_chunks(full, name):
    shape, ax = BIG[name]
    split = shape[:ax] + (NDEV, shape[ax] // NDEV) + shape[ax + 1:]
    return _shard_rows(jnp.moveaxis(full.reshape(split), ax, 0), name)


def _from_chunks(chunks, name):
    shape, ax = BIG[name]
    return jnp.moveaxis(chunks.reshape((NDEV,) + _shard_shape(name)), 0, ax).reshape(shape)


def _rows_of(names):
    return sum(PACK_ROWS[n] for n in names)


def _pack_state(t, names):
    return jnp.concatenate([_shard_rows(t[n], n) for n in names], axis=0)


def _pack_small(t, loss_part=None):
    slot = jnp.zeros((LOSS_SLOT,), F32)
    if loss_part is not None:
        slot = slot.at[0].set(loss_part)
    return _as_rows(jnp.concatenate([t[n].reshape(-1) for n in SMALL] + [slot]), SMALL_ROWS)


def _pack_grads(g, names):
    big = jnp.concatenate([_to_chunks(g[n], n).astype(BF) for n in names], axis=1)
    return jnp.swapaxes(big.reshape(4, 2, _rows_of(names), PACK_W), 0, 1)


def _unpack_state(big, names):
    out, off = {}, 0
    for n in names:
        sh, r = _shard_shape(n), PACK_ROWS[n]
        k = int(np.prod(sh))
        if len(sh) == 2 and sh[1] == PACK_W:
            out[n] = big[off:off + sh[0]]
        else:
            out[n] = big[off:off + r].reshape(-1)[:k].reshape(sh)
        off += r
    return out


def _unpack_small(small):
    out, flat, off = {}, small.reshape(-1), 0
    for n, sh in SMALL.items():
        k = int(np.prod(sh))
        out[n] = flat[off:off + k].reshape(sh)
        off += k
    out["loss"] = flat[off]
    return out


def _pack_gather(w, names):
    pieces = []
    for n in names:
        if n == "conv_w":
            pieces.append(_as_rows(jnp.concatenate([p.reshape(-1) for p in _split(w[n], 3)]), PACK_ROWS[n]))
        else:
            pieces.append(_shard_rows(w[n], n).astype(BF))
    return jnp.concatenate(pieces, axis=0)


def _unpack_gather(gathered, names):
    g = gathered.reshape(NDEV, _rows_of(names), PACK_W)
    out, off = {}, 0
    for n in names:
        r = PACK_ROWS[n]
        sh = _shard_shape(n)
        if n == "conv_w":
            k = int(np.prod(sh))
            terms = g[:, off:off + r].reshape(NDEV, -1)[:, :3 * k].astype(F32).reshape(NDEV, 3, k)
            out[n] = _from_chunks(terms[:, 0] + terms[:, 1] + terms[:, 2], n)
        elif len(sh) == 2 and sh[1] == PACK_W:
            out[n] = _from_chunks(g[:, off:off + sh[0]], n)
        else:
            out[n] = _from_chunks(g[:, off:off + r], n)
        off += r
    return out


PARAMS = ["c_ctx", "w_ada", "b_ada", "norm_pre", "norm_post", "w_in", "b_merge", "pool_w", "pool_scale", "conv_w", "conv_b",
          "dt_bias", "a_log", "d_skip", "ssd_norm", "w_proj_pool", "w_proj_ssd", "w_out"]


def kernel(x, c, ctx, c_ctx, w_ada, b_ada, norm_pre, norm_post, w_in, b_merge, pool_w, pool_scale, conv_w, conv_b, dt_bias, a_log, d_skip, ssd_norm, w_proj_pool, w_proj_ssd, w_out, loss_target, m_c_ctx, m_w_ada, m_b_ada, m_norm_pre, m_norm_post, m_w_in, m_b_merge, m_pool_w, m_pool_scale, m_conv_w, m_conv_b, m_dt_bias, m_a_log, m_d_skip, m_ssd_norm, m_w_proj_pool, m_w_proj_ssd, m_w_out, v_c_ctx, v_w_ada, v_b_ada, v_norm_pre, v_norm_post, v_w_in, v_b_merge, v_pool_w, v_pool_scale, v_conv_w, v_conv_b, v_dt_bias, v_a_log, v_d_skip, v_ssd_norm, v_w_proj_pool, v_w_proj_ssd, v_w_out):
    given = dict(locals())
    shapes = {n: given[n].shape for n in PARAMS}

    def local(prefix):
        t = {n: (given[prefix + n] if n == "c_ctx" else given[prefix + n][0]) for n in PARAMS}
        for n in TRANSPOSED:
            t[n] = t[n].T
        return {n: t[n].reshape(_shard_shape(n) if n in BIG else SMALL[n]) for n in PARAMS}

    w, m, v = local(""), local("m_"), local("v_")

    W = _unpack_gather(_all_gather(_pack_gather(w, GATHER_EARLY)), GATHER_EARLY)
    for n in SMALL:
        W[n] = w[n]
    lanes, grad_x, g, recv_early = _local_step(x, c, ctx, loss_target, W, late_shard=_pack_gather(w, GATHER_LATE),
                                               exchange=True)
    gb = _pack_grads(g, GRADS_LATE)
    got, recv_small = _pair_exchange(gb, _pack_small(g, (0.5 / D) * jnp.sum(lanes)), "grads_pair_exchange_late")
    late = _chip_exchange_side(_pair_add(gb, got, "grads_pair_add_late"))
    res = [{} for _ in range(4)]
    *early, recv_late = _adamw(recv_early, *[_pack_state(s, GRADS_EARLY) for s in (w, m, v)], "adamw_early", side=late)
    for r, t in zip(res, early):
        r.update(_unpack_state(t, GRADS_EARLY))
    for r, t in zip(res, _adamw(recv_late, *[_pack_state(s, GRADS_LATE) for s in (w, m, v)], "adamw_late")):
        r.update(_unpack_state(t, GRADS_LATE))
    for r, t in zip(res, _adamw(recv_small, *[_pack_small(s) for s in (w, m, v)], "adamw_small")):
        r.update(_unpack_small(t))
    outs = [res[0]["loss"], grad_x]
    for r in res:
        for n in TRANSPOSED:
            r[n] = r[n].T
        outs += [r[n].reshape(shapes[n]) for n in PARAMS]
    return tuple(outs)
```

```python
import functools

import numpy as np
import jax
import jax.numpy as jnp
from jax import lax
from jax.experimental import pallas as pl
from jax.experimental.pallas import tpu as pltpu

F32, BF = jnp.float32, jnp.bfloat16

D = 1024
GRID_W = 64
EPS = 1e-6
POOL_WINDOWS = (2, 4, 8, 16)
PGW = 256
DIN = 2048
HEAD = 64
NST = 128
NG = 4
HPG = 8
GWID = HPG * HEAD
Q = 128
CONV_DIM = 3072
OFF_GATE, OFF_XBC, OFF_DT, IN_COLS = 4096, 6144, 9216, 9280
NDEV = 8
ADAM_LR, ADAM_B1, ADAM_B2, ADAM_EPS, ADAM_WD, ADAM_STEP = 0.001, 0.9, 0.999, 1e-08, 0.01, 10

V7X_VMEM_LIMIT = 56 * 2 ** 20
ROW_TILE = 256


def _params(sem=None):
    return pltpu.CompilerParams(dimension_semantics=sem, vmem_limit_bytes=V7X_VMEM_LIMIT)


def _dot(a, b):
    return jnp.dot(a.astype(BF), b.astype(BF), preferred_element_type=F32)


def _dot_nt(a, b):
    return lax.dot_general(a.astype(BF), b.astype(BF), (((1,), (1,)), ((), ())), preferred_element_type=F32)


def _dot_tn(a, b):
    return lax.dot_general(a.astype(BF), b.astype(BF), (((0,), (0,)), ((), ())), preferred_element_type=F32)


def _split(a, n):
    parts = []
    for _ in range(n):
        p = a.astype(BF)
        parts.append(p)
        a = a - p.astype(F32)
    return parts


def _dot_sl(a, b01, n=3):
    parts = _split(a, n)
    m = a.shape[0]
    if n == 1 or m % 16:
        return sum(jnp.dot(p, b01, preferred_element_type=F32) for p in parts)
    r = jnp.dot(jnp.concatenate(parts, axis=0), b01, preferred_element_type=F32)
    return sum(r[i * m:(i + 1) * m] for i in range(n))


def _dot_sr(a01, b, n=3):
    parts = _split(b, n)
    k = b.shape[1]
    if n == 1 or k % 128:
        return sum(jnp.dot(a01, p, preferred_element_type=F32) for p in parts)
    r = jnp.dot(a01, jnp.concatenate(parts, axis=1), preferred_element_type=F32)
    return sum(r[:, i * k:(i + 1) * k] for i in range(n))


def _sigmoid(x):
    return 1.0 / (1.0 + jnp.exp(-x))


class _SideCopies:
    NSEM = 8

    def __init__(self, inputs, out_shapes, make):
        self.inputs, self.out_shapes, self.make = list(inputs), list(out_shapes), make

    def scratch(self):
        return [pltpu.SemaphoreType.DMA((self.NSEM,)), pltpu.SemaphoreType.DMA((self.NSEM,)), pltpu.SemaphoreType.DMA]

    def start(self, grid, in_refs, out_refs, sems):
        @pl.when(functools.reduce(lambda p, q: p & q, [pl.program_id(i) == 0 for i in range(len(grid))]))
        def _():
            local, sends, _ = self.make(in_refs, out_refs, *sems, arrivals=False)
            for cp in local + sends:
                cp.start()

    def wait(self, grid, in_refs, out_refs, sems):
        @pl.when(functools.reduce(lambda p, q: p & q, [pl.program_id(i) == n - 1 for i, n in enumerate(grid)]))
        def _():
            local, sends, recvs = self.make(in_refs, out_refs, *sems)
            for cp in sends:
                cp.wait_send()
            for cp in recvs:
                cp.wait_recv()
            for cp in local:
                cp.wait()


def _matmul(a, b, out_dtype, name, tm=512, tn=512, tk=1024, bt=False, n=None, side=None):
    M, K = a.shape
    N = n if n is not None else (b.shape[0] if bt else b.shape[1])
    tm, tn, tk = min(tm, M), min(tn, N), min(tk, K)
    assert M % tm == 0 and N % tn == 0 and K % tk == 0, (a.shape, b.shape)
    nk = K // tk
    grid = (M // tm, N // tn, nk)
    n_si, n_so = (len(side.inputs), len(side.out_shapes)) if side else (0, 0)

    def body(*refs):
        a_ref, b_ref, o_ref = refs[0], refs[1], refs[2 + n_si]
        acc = refs[3 + n_si + n_so]
        side_refs = (refs[2:2 + n_si], refs[3 + n_si:3 + n_si + n_so], refs[4 + n_si + n_so:])
        if side:
            side.start(grid, *side_refs)
        k = pl.program_id(2)
        p = _dot_nt(a_ref[...], b_ref[...]) if bt else _dot(a_ref[...], b_ref[...])

        @pl.when(k == 0)
        def _():
            acc[...] = p

        @pl.when(k > 0)
        def _():
            acc[...] += p

        @pl.when(k == nk - 1)
        def _():
            o_ref[...] = acc[...].astype(o_ref.dtype)

        if side:
            side.wait(grid, *side_refs)

    out = pl.pallas_call(
        body, name=name, grid=grid,
        in_specs=[pl.BlockSpec((tm, tk), lambda i, j, k: (i, k)),
                  pl.BlockSpec((tn, tk), lambda i, j, k: (j, k)) if bt else pl.BlockSpec((tk, tn), lambda i, j, k: (k, j))]
        + [ANY] * n_si,
        out_specs=(pl.BlockSpec((tm, tn), lambda i, j, k: (i, j)),) + (ANY,) * n_so,
        out_shape=(jax.ShapeDtypeStruct((M, N), out_dtype),) + tuple(side.out_shapes if side else ()),
        scratch_shapes=[pltpu.VMEM((tm, tn), F32)] + (side.scratch() if side else []),
        compiler_params=_params(("arbitrary",) * 3 if side else ("parallel", "parallel", "arbitrary")),
    )(a, b, *(side.inputs if side else ()))
    return out if side else out[0]


def _matmul_tn(a, g, name, ta=512, tn=512, tr=512):
    M, Ka = a.shape
    N = g.shape[1]
    ta, tn, tr = min(ta, Ka), min(tn, N), min(tr, M)
    assert M % tr == 0 and N % tn == 0 and Ka % ta == 0, (a.shape, g.shape)
    nr = M // tr

    def body(a_ref, g_ref, o_ref):
        k = pl.program_id(2)
        p = _dot_tn(a_ref[...], g_ref[...])

        @pl.when(k == 0)
        def _():
            o_ref[...] = p

        @pl.when(k > 0)
        def _():
            o_ref[...] += p

    return pl.pallas_call(
        body, name=name, grid=(Ka // ta, N // tn, nr),
        in_specs=[pl.BlockSpec((tr, ta), lambda i, j, k: (k, i)), pl.BlockSpec((tr, tn), lambda i, j, k: (k, j))],
        out_specs=pl.BlockSpec((ta, tn), lambda i, j, k: (i, j)),
        out_shape=jax.ShapeDtypeStruct((Ka, N), F32),
        compiler_params=_params(("parallel", "parallel", "arbitrary")),
    )(a, g)


def _dhx(pieces, ddt, w_inT, w_dtT, side=None):
    _, R, _ = pieces[0].shape
    tm = R // 4
    kb = 1024
    starts, nblk = [], []
    for p in pieces:
        starts.append(sum(nblk))
        nblk.append(p.shape[2] // kb)
    nk = sum(nblk)
    assert nk * kb == OFF_DT and R % 128 == 0
    npc = len(pieces)
    grid = (2, R // tm, nk)
    n_si, n_so = (len(side.inputs), len(side.out_shapes)) if side else (0, 0)

    def body(*refs):
        a_refs, dt_ref, w_ref, wdt_ref = refs[:npc], refs[npc], refs[npc + 1], refs[npc + 2]
        o_ref, acc = refs[npc + 3 + n_si], refs[npc + 4 + n_si + n_so]
        side_refs = (refs[npc + 3:npc + 3 + n_si], refs[npc + 4 + n_si:npc + 4 + n_si + n_so], refs[npc + 5 + n_si + n_so:])
        if side:
            side.start(grid, *side_refs)
        k = pl.program_id(2)

        @pl.when(k == 0)
        def _():
            acc[...] = _dot(dt_ref[0], wdt_ref[...])

        for p in range(npc):
            @pl.when((k >= starts[p]) & (k < starts[p] + nblk[p]))
            def _(p=p):
                acc[...] += _dot(a_refs[p][0], w_ref[...])

        @pl.when(k == nk - 1)
        def _():
            o_ref[0] = acc[...].astype(BF)

        if side:
            side.wait(grid, *side_refs)

    in_specs = [pl.BlockSpec((1, tm, kb), functools.partial(
        lambda e, t, k, s, nb: (e, t, jnp.clip(k - s, 0, nb - 1)), s=starts[p], nb=nblk[p])) for p in range(npc)]
    in_specs += [pl.BlockSpec((1, tm, 128), lambda e, t, k: (e, t, 0)),
                 pl.BlockSpec((kb, D), lambda e, t, k: (k, 0)),
                 pl.BlockSpec((128, D), lambda e, t, k: (0, 0))]
    out = pl.pallas_call(
        body, name="d_hx", grid=grid, in_specs=in_specs + [ANY] * n_si,
        out_specs=(pl.BlockSpec((1, tm, D), lambda e, t, k: (e, t, 0)),) + (ANY,) * n_so,
        out_shape=(jax.ShapeDtypeStruct((2, R, D), BF),) + tuple(side.out_shapes if side else ()),
        scratch_shapes=[pltpu.VMEM((tm, D), F32)] + (side.scratch() if side else []),
        compiler_params=_params(("arbitrary",) * 3 if side else ("parallel", "parallel", "arbitrary")),
    )(*pieces, ddt, w_inT, w_dtT, *(side.inputs if side else ()))
    return out if side else out[0]


def _adaln_fwd(c16, w_adaT_bf, b_ada):
    def body(c_ref, w_ref, b_ref, o_ref):
        cc = c_ref[...]
        o_ref[...] = _dot_nt(cc * _sigmoid(cc), w_ref[...]) + b_ref[...]

    return pl.pallas_call(body, name="adaln_fwd", out_shape=jax.ShapeDtypeStruct((16, 3 * D), F32),
                          compiler_params=_params())(c16, w_adaT_bf, b_ada)


def _adaln_bwd(acc_n, acc_f, mod16, c16, norm_pre, w_adaT_bf):
    def body(an_ref, af_ref, mod_ref, c_ref, np_ref, wt_ref, dw_ref, db_ref, sm_ref, dmod):
        npre = np_ref[...]
        dmod[...] = jnp.zeros_like(dmod)
        dnp = jnp.zeros((1, D), F32)
        dshift_c = jnp.zeros((1, D), F32)
        dgpre_c = jnp.zeros((1, D), F32)
        scale_c = mod_ref[2:3, D:2 * D]
        for e in range(2):
            dg_x, ds_x = an_ref[e, 0, 0:1, :], an_ref[e, 0, 1:2, :]
            dg_c, ds_c = an_ref[e, 1, 0:1, :], an_ref[e, 1, 1:2, :]
            dmod[e:e + 1, 0:D] = ds_x
            dmod[e:e + 1, D:2 * D] = dg_x * npre
            dmod[e:e + 1, 2 * D:3 * D] = af_ref[e, 0:1, :]
            dnp = dnp + dg_x * (1.0 + mod_ref[e:e + 1, D:2 * D]) + dg_c * (1.0 + scale_c)
            dshift_c = dshift_c + ds_c
            dgpre_c = dgpre_c + dg_c
        dmod[2:3, 0:D] = dshift_c
        dmod[2:3, D:2 * D] = dgpre_c * npre
        dm = dmod[...]
        cc = c_ref[...]
        sg = _sigmoid(cc)
        dw_ref[...] = _dot_tn(dm, cc * sg)
        db_ref[...] = jnp.zeros_like(db_ref)
        db_ref[0:1, :] = jnp.sum(dm, axis=0, keepdims=True)
        dsilu = sg * (1.0 + cc * (1.0 - sg))
        dcs = _dot(dm, wt_ref[...]) * dsilu
        sm_ref[...] = jnp.zeros_like(sm_ref)
        sm_ref[0:1, :] = dnp
        sm_ref[1:2, :] = dcs[2:3, :]

    return pl.pallas_call(
        body, name="adaln_bwd",
        out_shape=(jax.ShapeDtypeStruct((3 * D, D), F32), jax.ShapeDtypeStruct((16, 3 * D), F32),
                   jax.ShapeDtypeStruct((8, D), F32)),
        scratch_shapes=[pltpu.VMEM((16, 3 * D), F32)],
        compiler_params=_params())(acc_n, acc_f, mod16, c16, norm_pre, w_adaT_bf)


def _row_specs(L):
    nx = L // ROW_TILE
    return (pl.BlockSpec((1, ROW_TILE, D), lambda e, t: (e, jnp.minimum(t, nx - 1), 0)),
            pl.BlockSpec((1, ROW_TILE, D), lambda e, t: (e, jnp.maximum(t - nx, 0), 0)))


def _norm_mod_fwd(x, ctx, tab):
    L = x.shape[1]
    R = L + ctx.shape[1]
    nx = L // ROW_TILE

    def body(x_ref, c_ref, t_ref, o_ref):
        x = jnp.where(pl.program_id(1) < nx, x_ref[0], c_ref[0])
        r = lax.rsqrt(jnp.mean(x * x, axis=-1, keepdims=True) + EPS)
        t = t_ref[0, 0]
        o_ref[0] = (x * r * t[0:1] + t[1:2]).astype(BF)

    return pl.pallas_call(
        body, name="norm_mod_fwd", grid=(2, R // ROW_TILE),
        in_specs=[*_row_specs(L), pl.BlockSpec((1, 1, 8, D), lambda e, t: (e, t // nx, 0, 0))],
        out_specs=pl.BlockSpec((1, ROW_TILE, D), lambda e, t: (e, t, 0)),
        out_shape=jax.ShapeDtypeStruct((2, R, D), BF),
        compiler_params=_params(("parallel", "parallel")),
    )(x, ctx, tab)


def _norm_mod_bwd(dh, x, ctx, tab, dxo):
    L = x.shape[1]
    R = L + ctx.shape[1]
    nx = L // ROW_TILE

    def body(dh_ref, x_ref, c_ref, t_ref, dxo_ref, gx_ref, acc_ref):
        t = pl.program_id(1)
        x = jnp.where(t < nx, x_ref[0], c_ref[0])
        r = lax.rsqrt(jnp.mean(x * x, axis=-1, keepdims=True) + EPS)
        xn = x * r
        dh = dh_ref[0].astype(F32)

        @pl.when((t == 0) | (t == nx))
        def _():
            acc_ref[...] = jnp.zeros_like(acc_ref)

        acc_ref[0, 0, 0:1, :] += jnp.sum(dh * xn, axis=0, keepdims=True)
        acc_ref[0, 0, 1:2, :] += jnp.sum(dh, axis=0, keepdims=True)

        @pl.when(t < nx)
        def _():
            dxn = dh * t_ref[0, 0][0:1]
            dx = r * (dxn - xn * jnp.mean(dxn * xn, axis=-1, keepdims=True))
            gx_ref[0] = dxo_ref[0] + dx

    xspec, cspec = _row_specs(L)
    return pl.pallas_call(
        body, name="norm_mod_bwd", grid=(2, R // ROW_TILE),
        in_specs=[pl.BlockSpec((1, ROW_TILE, D), lambda e, t: (e, t, 0)), xspec, cspec,
                  pl.BlockSpec((1, 1, 8, D), lambda e, t: (e, t // nx, 0, 0)), xspec],
        out_specs=(xspec, pl.BlockSpec((1, 1, 8, D), lambda e, t: (e, t // nx, 0, 0))),
        out_shape=(jax.ShapeDtypeStruct((2, L, D), F32), jax.ShapeDtypeStruct((2, 2, 8, D), F32)),
        compiler_params=_params(("parallel", "arbitrary")),
    )(dh, x, ctx, tab, dxo)


POOL_TILE = 256


def _pool_tables(L):
    rows = L // GRID_W
    mats = np.zeros((4, POOL_TILE, POOL_TILE), np.float32)
    inv = np.zeros((4, L, 1), np.float32)
    for gi, k in enumerate(POOL_WINDOWS):
        lo, hi = k // 2, k - 1 - k // 2
        m = np.zeros((GRID_W, GRID_W), np.float32)
        for t in range(GRID_W):
            m[t, max(t - lo, 0):min(t + hi, GRID_W - 1) + 1] = 1.0
        for b in range(POOL_TILE // GRID_W):
            mats[gi, b * GRID_W:(b + 1) * GRID_W, b * GRID_W:(b + 1) * GRID_W] = m
        cnt_c = m.sum(1)
        cnt_r = np.array([min(r + hi, rows - 1) - max(r - lo, 0) + 1 for r in range(rows)], np.float32)
        inv[gi, :, 0] = (1.0 / (cnt_r[:, None] * cnt_c[None, :])).reshape(-1)
    matsT = np.ascontiguousarray(np.transpose(mats, (0, 2, 1)))
    return (jnp.asarray(mats, BF), jnp.asarray(matsT, BF), jnp.asarray(inv))


def _pool_cols(get_tile, mat, cs_ref, L, n):
    def step(i, carry):
        off = pl.multiple_of(i * POOL_TILE, POOL_TILE)
        cs_ref[pl.ds(GRID_W + off, POOL_TILE), :] = _dot_sr(mat, get_tile(off).astype(F32), n)
        return carry

    lax.fori_loop(0, L // POOL_TILE, step, 0)
    cs_ref[pl.ds(0, GRID_W), :] = jnp.zeros((GRID_W, PGW), F32)

    def prefix(r, carry):
        o = pl.multiple_of(r * GRID_W, GRID_W)
        cs_ref[pl.ds(o + GRID_W, GRID_W), :] = cs_ref[pl.ds(o + GRID_W, GRID_W), :] + cs_ref[pl.ds(o, GRID_W), :]
        return carry

    lax.fori_loop(0, L // GRID_W, prefix, 0)


def _pool_rows(cs_ref, off, below, above, L):
    rows = L // GRID_W
    r0 = off // GRID_W
    parts = []
    for i in range(POOL_TILE // GRID_W):
        hi = pl.multiple_of(jnp.minimum(r0 + i + above + 1, rows) * GRID_W, GRID_W)
        lo = pl.multiple_of(jnp.maximum(r0 + i - below, 0) * GRID_W, GRID_W)
        parts.append(cs_ref[pl.ds(hi, GRID_W), :] - cs_ref[pl.ds(lo, GRID_W), :])
    return jnp.concatenate(parts, axis=0)


def _pool_fwd(proj3, pool_w_bf, pool_scale, tables, L):
    mats, _, inv = tables
    nt = L // POOL_TILE

    def body(v_ref, z_ref, pw_ref, ps_ref, m_ref, inv_ref, o_ref, cs_ref):
        _pool_cols(lambda off: v_ref[0, pl.ds(off, POOL_TILE), :], m_ref[0], cs_ref, L, 1)
        half = lax.shift_left(1, pl.program_id(1))

        def step(i, carry):
            off = pl.multiple_of(i * POOL_TILE, POOL_TILE)
            rows = pl.ds(off, POOL_TILE)
            v = v_ref[0, rows, :].astype(F32)
            diff = _pool_rows(cs_ref, off, half, half - 1, L) * inv_ref[0, rows, :] - v
            yp = _dot(diff, pw_ref[0])
            z = z_ref[0, rows, :].astype(F32)
            o_ref[0, rows, :] = (yp * ps_ref[...] * (z * _sigmoid(z))).astype(BF)
            return carry

        lax.fori_loop(0, nt, step, 0)

    return pl.pallas_call(
        body, name="pool_fwd", grid=(2, 4),
        in_specs=[pl.BlockSpec((1, L, PGW), lambda e, g: (e, 0, g)),
                  pl.BlockSpec((1, L, PGW), lambda e, g: (e, 0, 4 + g)),
                  pl.BlockSpec((1, PGW, PGW), lambda e, g: (g, 0, 0)),
                  pl.BlockSpec((1, PGW), lambda e, g: (0, g)),
                  pl.BlockSpec((1, POOL_TILE, POOL_TILE), lambda e, g: (g, 0, 0)),
                  pl.BlockSpec((1, L, 1), lambda e, g: (g, 0, 0))],
        out_specs=pl.BlockSpec((1, L, PGW), lambda e, g: (e, 0, g)),
        out_shape=jax.ShapeDtypeStruct((2, L, D), BF),
        scratch_shapes=[pltpu.VMEM((L + GRID_W, PGW), F32)],
        compiler_params=_params(("parallel", "parallel")),
    )(proj3, proj3, pool_w_bf, pool_scale, mats, inv)


def _pool_bwd(proj3, d_ypool, pool_w_bf, pool_wT_bf, pool_scale, tables, L):
    mats, matsT, inv = tables
    nt = L // POOL_TILE
    R = proj3.shape[1]

    def body(v_ref, z_ref, dy_ref, pw_ref, pwt_ref, ps_ref, m_ref, mt_ref, inv_ref,
             dv_ref, dz_ref, dpw_ref, acc_ref, cs_ref, dd_ref):
        e = pl.program_id(1)

        @pl.when(e == 0)
        def _():
            dpw_ref[...] = jnp.zeros_like(dpw_ref)
            acc_ref[...] = jnp.zeros_like(acc_ref)

        _pool_cols(lambda off: v_ref[0, pl.ds(off, POOL_TILE), :], m_ref[0], cs_ref, L, 1)
        half = lax.shift_left(1, pl.program_id(0))
        ps = ps_ref[...]

        def step(i, carry):
            off = pl.multiple_of(i * POOL_TILE, POOL_TILE)
            rows = pl.ds(off, POOL_TILE)
            v = v_ref[0, rows, :].astype(F32)
            diff = _pool_rows(cs_ref, off, half, half - 1, L) * inv_ref[0, rows, :] - v
            yp = _dot(diff, pw_ref[0])
            z = z_ref[0, rows, :].astype(F32)
            sg = _sigmoid(z)
            sz = z * sg
            dy = dy_ref[0, rows, :].astype(F32)
            dz_ref[0, rows, :] = (dy * yp * ps * (sg * (1.0 + z * (1.0 - sg)))).astype(BF)
            dys = dy * sz
            acc_ref[0, 0:1, :] += jnp.sum(dys * yp, axis=0, keepdims=True)
            dyp = dys * ps
            dpw_ref[0] += _dot_tn(diff, dyp)
            dd_ref[rows, :] = _dot(dyp, pwt_ref[0])
            return carry

        lax.fori_loop(0, nt, step, 0)
        _pool_cols(lambda off: dd_ref[pl.ds(off, POOL_TILE), :] * inv_ref[0, pl.ds(off, POOL_TILE), :],
                   mt_ref[0], cs_ref, L, 2)

        def step2(i, carry):
            off = pl.multiple_of(i * POOL_TILE, POOL_TILE)
            rows = pl.ds(off, POOL_TILE)
            dv_ref[0, rows, :] = (_pool_rows(cs_ref, off, half - 1, half, L) - dd_ref[rows, :]).astype(BF)
            return carry

        lax.fori_loop(0, nt, step2, 0)
        dv_ref[0, pl.ds(L, R - L), :] = jnp.zeros((R - L, PGW), BF)
        dz_ref[0, pl.ds(L, R - L), :] = jnp.zeros((R - L, PGW), BF)

    return pl.pallas_call(
        body, name="pool_bwd", grid=(4, 2),
        in_specs=[pl.BlockSpec((1, L, PGW), lambda g, e: (e, 0, g)),
                  pl.BlockSpec((1, L, PGW), lambda g, e: (e, 0, 4 + g)),
                  pl.BlockSpec((1, L, PGW), lambda g, e: (e, 0, g)),
                  pl.BlockSpec((1, PGW, PGW), lambda g, e: (g, 0, 0)),
                  pl.BlockSpec((1, PGW, PGW), lambda g, e: (g, 0, 0)),
                  pl.BlockSpec((1, PGW), lambda g, e: (0, g)),
                  pl.BlockSpec((1, POOL_TILE, POOL_TILE), lambda g, e: (g, 0, 0)),
                  pl.BlockSpec((1, POOL_TILE, POOL_TILE), lambda g, e: (g, 0, 0)),
                  pl.BlockSpec((1, L, 1), lambda g, e: (g, 0, 0))],
        out_specs=(pl.BlockSpec((1, R, PGW), lambda g, e: (e, 0, g)),
                   pl.BlockSpec((1, R, PGW), lambda g, e: (e, 0, g)),
                   pl.BlockSpec((1, PGW, PGW), lambda g, e: (g, 0, 0)),
                   pl.BlockSpec((1, 8, PGW), lambda g, e: (g, 0, 0))),
        out_shape=(jax.ShapeDtypeStruct((2, R, D), BF), jax.ShapeDtypeStruct((2, R, D), BF),
                   jax.ShapeDtypeStruct((4, PGW, PGW), F32), jax.ShapeDtypeStruct((4, 8, PGW), F32)),
        scratch_shapes=[pltpu.VMEM((L + GRID_W, PGW), F32), pltpu.VMEM((L, PGW), F32)],
        compiler_params=_params(("parallel", "arbitrary")),
    )(proj3, proj3, d_ypool, pool_w_bf, pool_wT_bf, pool_scale, mats, matsT, inv)


CONV_BLOCK = 128


def _conv_tap(u, k, L):
    off = k - 2
    if off == 0:
        return u
    R = u.shape[0]
    r = lax.broadcasted_iota(jnp.int32, (R, 1), 0)
    pos = jnp.where(r < L, r, r - L) + off
    seg = jnp.where(r < L, L, R - L)
    return jnp.where((pos >= 0) & (pos < seg), pltpu.roll(u, (-off) % R, 0), 0.0)


def _conv_fwd(proj3, conv_w, conv_b, L):
    _, R, _ = proj3.shape
    cb0 = OFF_XBC // CONV_BLOCK

    def body(u_ref, w_ref, b_ref, o_ref):
        u = u_ref[0].astype(F32)
        w = w_ref[...]
        pre = b_ref[...] + sum(_conv_tap(u, k, L) * w[k:k + 1, :] for k in range(4))
        o_ref[0] = (pre * _sigmoid(pre)).astype(BF)

    return pl.pallas_call(
        body, name="conv_fwd", grid=(2, CONV_DIM // CONV_BLOCK),
        in_specs=[pl.BlockSpec((1, R, CONV_BLOCK), lambda e, j: (e, 0, cb0 + j)),
                  pl.BlockSpec((4, CONV_BLOCK), lambda e, j: (0, j)),
                  pl.BlockSpec((1, CONV_BLOCK), lambda e, j: (0, j))],
        out_specs=pl.BlockSpec((1, R, CONV_BLOCK), lambda e, j: (e, 0, j)),
        out_shape=jax.ShapeDtypeStruct((2, R, CONV_DIM), BF),
        compiler_params=_params(("parallel", "parallel")),
    )(proj3, conv_w, conv_b)


def _conv_bwd(proj3, addends, scales, col0, ncols, in_maps, conv_w, conv_b, L, name):
    _, R, _ = proj3.shape
    cb0 = (OFF_XBC + col0) // CONV_BLOCK
    wb0 = col0 // CONV_BLOCK
    na = len(addends)
    scaled = [i for i in range(na) if scales[i] is not None]

    def body(*refs):
        u_ref, w_ref, b_ref = refs[0], refs[1], refs[2]
        a_refs = refs[3:3 + na]
        s_refs = dict(zip(scaled, refs[3 + na:3 + na + len(scaled)]))
        o_ref, acc_ref = refs[3 + na + len(scaled)], refs[4 + na + len(scaled)]
        u = u_ref[0].astype(F32)
        w = w_ref[...]
        taps = [_conv_tap(u, k, L) for k in range(4)]
        pre = b_ref[...] + sum(taps[k] * w[k:k + 1, :] for k in range(4))
        sg = _sigmoid(pre)
        dxbc = jnp.zeros(u.shape, F32)
        for i, a in enumerate(a_refs):
            t = a[0].astype(F32)
            dxbc = dxbc + (t * s_refs[i][...] if i in s_refs else t)
        dpre = dxbc * (sg * (1.0 + pre * (1.0 - sg)))
        acc_ref[...] = jnp.zeros_like(acc_ref)
        for k in range(4):
            acc_ref[0, k:k + 1, :] = jnp.sum(dpre * taps[k], axis=0, keepdims=True)
        acc_ref[0, 4:5, :] = jnp.sum(dpre, axis=0, keepdims=True)
        du = sum(_conv_tap(dpre, 4 - k, L) * w[k:k + 1, :] for k in range(4))
        o_ref[0] = du.astype(BF)

    in_specs = [pl.BlockSpec((1, R, CONV_BLOCK), lambda e, j: (e, 0, cb0 + j)),
                pl.BlockSpec((4, CONV_BLOCK), lambda e, j: (0, wb0 + j)),
                pl.BlockSpec((1, CONV_BLOCK), lambda e, j: (0, wb0 + j))]
    for m in in_maps:
        in_specs.append(pl.BlockSpec((1, R, CONV_BLOCK), functools.partial(lambda e, j, m: (e, 0, m(j)), m=m)))
    for i in scaled:
        in_specs.append(pl.BlockSpec((1, CONV_BLOCK), functools.partial(lambda e, j, m: (0, m(j)), m=in_maps[i])))
    return pl.pallas_call(
        body, name=name, grid=(2, ncols // CONV_BLOCK),
        in_specs=in_specs,
        out_specs=(pl.BlockSpec((1, R, CONV_BLOCK), lambda e, j: (e, 0, j)),
                   pl.BlockSpec((1, 8, CONV_BLOCK), lambda e, j: (e, 0, j))),
        out_shape=(jax.ShapeDtypeStruct((2, R, ncols), BF), jax.ShapeDtypeStruct((2, 8, ncols), F32)),
        compiler_params=_params(("parallel", "parallel")),
    )(proj3, conv_w, conv_b, *addends, *[scales[i] for i in scaled])


def _softplus(x):
    e = jnp.exp(-jnp.abs(x))
    u = 1.0 + e
    return jnp.maximum(x, 0.0) + jnp.where(u == 1.0, e, e * jnp.log(u) / (u - 1.0))


def _to_local_mat(g, transpose=False):
    r = lax.broadcasted_iota(jnp.int32, (128, 128), 1 if transpose else 0)
    c = lax.broadcasted_iota(jnp.int32, (128, 128), 0 if transpose else 1)
    return ((c < 2 * HPG) & (r == jnp.right_shift(c, 3) * (NG * HPG) + g * HPG + (c & (HPG - 1)))).astype(BF)


def _dt_fwd(dt_raw, bias128):
    _, R, _ = dt_raw.shape

    def body(x_ref, b_ref, o_ref):
        dt = _softplus(x_ref[0] + b_ref[...])
        for g in range(NG):
            o_ref[0, g] = _dot_sl(dt, _to_local_mat(g))

    tr = R // 4
    return pl.pallas_call(
        body, name="dt_fwd", grid=(2, 4),
        in_specs=[pl.BlockSpec((1, tr, 128), lambda e, t: (e, t, 0)), pl.BlockSpec((1, 128), lambda e, t: (0, 0))],
        out_specs=pl.BlockSpec((1, NG, tr, 128), lambda e, t: (e, 0, t, 0)),
        out_shape=jax.ShapeDtypeStruct((2, NG, R, 128), F32),
        compiler_params=_params(("parallel", "parallel")),
    )(dt_raw, bias128)


def _dt_bwd(dt_raw, bias128, ddt_f, ddt_b):
    _, R, _ = dt_raw.shape

    def body(x_ref, b_ref, f_ref, g_ref, o_ref, acc_ref):
        ddt = sum(_dot_sl(f_ref[0, g] + g_ref[0, g], _to_local_mat(g, transpose=True)) for g in range(NG))
        d = ddt * _sigmoid(x_ref[0] + b_ref[...])
        o_ref[0] = d.astype(BF)

        @pl.when(pl.program_id(1) == 0)
        def _():
            acc_ref[...] = jnp.zeros_like(acc_ref)

        acc_ref[0, 0:1, :] += jnp.sum(d, axis=0, keepdims=True)

    tr = R // 4
    blk = pl.BlockSpec((1, tr, 128), lambda e, t: (e, t, 0))
    loc = pl.BlockSpec((1, NG, tr, 128), lambda e, t: (e, 0, t, 0))
    return pl.pallas_call(
        body, name="dt_bwd", grid=(2, 4),
        in_specs=[blk, pl.BlockSpec((1, 128), lambda e, t: (0, 0)), loc, loc],
        out_specs=(blk, pl.BlockSpec((1, 8, 128), lambda e, t: (e, 0, 0))),
        out_shape=(jax.ShapeDtypeStruct(dt_raw.shape, BF), jax.ShapeDtypeStruct((2, 8, 128), F32)),
        compiler_params=_params(("parallel", "arbitrary")),
    )(dt_raw, bias128, ddt_f, ddt_b)


def _tri(d):
    i = lax.broadcasted_iota(jnp.int32, (Q, Q), 0)
    j = lax.broadcasted_iota(jnp.int32, (Q, Q), 1)
    return (i >= j) if d == 0 else (i <= j)


def _expand_mat(d):
    r = lax.broadcasted_iota(jnp.int32, (128, GWID), 0)
    c = lax.broadcasted_iota(jnp.int32, (128, GWID), 1)
    return (r == d * HPG + jnp.right_shift(c, 6)).astype(BF)


def _reduce_mat(d):
    r = lax.broadcasted_iota(jnp.int32, (GWID, 128), 0)
    c = lax.broadcasted_iota(jnp.int32, (GWID, 128), 1)
    return (c == d * HPG + jnp.right_shift(r, 6)).astype(BF)


def _ssd_chunk(d, dt, A, xs, B, C):
    mask = _tri(d)
    T = mask.astype(BF)
    Tt = _tri(1 - d).astype(BF)
    a = dt * A
    acs = _dot_sr(T, a)
    E = _expand_mat(d)
    dt_e = _dot_sl(dt, E, 2)
    acs_e = _dot_sl(acs, E, 2)
    alast_e = acs_e[Q - 1:Q, :] if d == 0 else acs_e[0:1, :]
    return dict(mask=mask, T=T, Tt=Tt, acs=acs, acsT=acs.T, dt_e=dt_e, acs_e=acs_e, lam=jnp.exp(acs_e),
                w=jnp.exp(alast_e - acs_e), decay=jnp.exp(alast_e), xt=xs * dt_e, CB=_dot_nt(C, B))


def _head_decay(q, d, hh):
    col = q["acs"][:, d * HPG + hh:d * HPG + hh + 1]
    row = q["acsT"][d * HPG + hh:d * HPG + hh + 1, :]
    return jnp.exp(jnp.where(q["mask"], col - row, -jnp.inf))


def _chunk_maps(NX, NS):
    cf = lambda s: lax.rem(s + NX, NS)
    cb = lambda s: NS - 1 - s
    return cf, cb


def _ssd_fwd(xbc, dt_loc, a_loc, L):
    _, R, _ = xbc.shape
    NX, NS = L // Q, R // Q
    cf, cb = _chunk_maps(NX, NS)

    def body(xs_f, b_f, c_f, dt_f, xs_b, b_b, c_b, dt_b, a_ref, y_f, hs_f, y_b, hs_b, hT):
        @pl.when(pl.program_id(2) == 0)
        def _():
            hT[...] = jnp.zeros_like(hT)

        A = a_ref[0, 0:1, :]
        lane = lax.broadcasted_iota(jnp.int32, (Q, 128), 1)
        for d, (xs_ref, b_ref, c_ref, dt_ref, y_ref, hs_ref) in enumerate(
                ((xs_f, b_f, c_f, dt_f, y_f, hs_f), (xs_b, b_b, c_b, dt_b, y_b, hs_b))):
            xs, B, C = xs_ref[0].astype(F32), b_ref[0], c_ref[0]
            q = _ssd_chunk(d, dt_ref[0, 0], A, xs, B, C)
            h = hT[d]
            hb = h.astype(BF)
            hs_ref[0, 0] = hb
            parts = []
            for pr in range(HPG // 2):
                xp = q["xt"][:, pr * 128:(pr + 1) * 128]
                xst = jnp.concatenate([jnp.where(lane < HEAD, xp, 0.0), jnp.where(lane < HEAD, 0.0, xp)], axis=0)
                mst = jnp.concatenate([(q["CB"] * _head_decay(q, d, 2 * pr)).astype(BF),
                                       (q["CB"] * _head_decay(q, d, 2 * pr + 1)).astype(BF)], axis=1)
                parts.append(_dot(mst, xst))
            y_ref[0] = jnp.concatenate(parts, axis=1) + _dot(C, hb) * q["lam"]
            hT[d] = q["decay"] * h + _dot_tn(B, q["xt"] * q["w"])

    def spec(shape, imap):
        return pl.BlockSpec(shape, imap)

    def ins(c):
        return [spec((1, Q, GWID), lambda e, g, s: (e, c(s), g)),
                spec((1, Q, NST), lambda e, g, s: (e, c(s), DIN // NST + g)),
                spec((1, Q, NST), lambda e, g, s: (e, c(s), DIN // NST + NG + g)),
                spec((1, 1, Q, 128), lambda e, g, s: (e, g, c(s), 0))]

    def outs(c):
        return [spec((1, Q, GWID), lambda e, g, s: (e, c(s), g)),
                spec((1, 1, NST, GWID), lambda e, g, s: (e, c(s), 0, g))]

    yshape = jax.ShapeDtypeStruct((2, R, DIN), F32)
    hshape = jax.ShapeDtypeStruct((2, NS, NST, DIN), BF)
    return pl.pallas_call(
        body, name="ssd_fwd", grid=(2, NG, NS),
        in_specs=ins(cf) + ins(cb) + [spec((1, 8, 128), lambda e, g, s: (g, 0, 0))],
        out_specs=tuple(outs(cf) + outs(cb)),
        out_shape=(yshape, hshape, yshape, hshape),
        scratch_shapes=[pltpu.VMEM((2, NST, GWID), F32)],
        compiler_params=_params(("parallel", "parallel", "arbitrary")),
    )(xbc, xbc, xbc, dt_loc, xbc, xbc, xbc, dt_loc, a_loc)


def _ssd_bwd(xbc, dt_loc, a_loc, hs_f, hs_b, y_f, y_b, dy, L):
    _, R, _ = xbc.shape
    NX, NS = L // Q, R // Q
    cf0, cb0 = _chunk_maps(NX, NS)
    cf = lambda sp: cf0(NS - 1 - sp)
    cb = lambda sp: cb0(NS - 1 - sp)

    def body(xs_f, b_f, c_f, dt_f, hs_f_, dy_f, y_f_, xs_b, b_b, c_b, dt_b, hs_b_, dy_b, y_b_, a_ref,
             dxs_f, dbc_f, ddt_f, dxs_b, dbc_b, ddt_b, da_ref, dhT):
        @pl.when(pl.program_id(2) == 0)
        def _():
            dhT[...] = jnp.zeros_like(dhT)
            da_ref[...] = jnp.zeros_like(da_ref)

        A = a_ref[0, 0:1, :]
        lane = lax.broadcasted_iota(jnp.int32, (Q, 128), 1)
        row = lax.broadcasted_iota(jnp.int32, (Q, 128), 0)
        for d, (xs_ref, b_ref, c_ref, dt_ref, hs_ref, dy_ref, y_ref, dxs_ref, dbc_ref, ddt_ref) in enumerate(
                ((xs_f, b_f, c_f, dt_f, hs_f_, dy_f, y_f_, dxs_f, dbc_f, ddt_f),
                 (xs_b, b_b, c_b, dt_b, hs_b_, dy_b, y_b_, dxs_b, dbc_b, ddt_b))):
            xs, B, C, dt = xs_ref[0].astype(F32), b_ref[0], c_ref[0], dt_ref[0, 0]
            q = _ssd_chunk(d, dt, A, xs, B, C)
            xt, lam, w, decay = q["xt"], q["lam"], q["w"], q["decay"]
            H = hs_ref[0, 0]
            dyv = dy_ref[0].astype(F32)
            dh = dhT[d]
            dZ = dyv * lam
            dC = _dot_nt(dZ, H)
            dH = _dot_tn(C, dZ)
            U = _dot(B, dh)
            xw = xt * w
            dxt = U * w
            dalast_e = (jnp.sum(U * xw, axis=0, keepdims=True)
                        + decay * jnp.sum(dh * H.astype(F32), axis=0, keepdims=True))
            dB = _dot_nt(xw, dh)
            dCB = jnp.zeros((Q, Q), F32)
            dxt_parts = []
            for pr in range(HPG // 2):
                xp = xt[:, pr * 128:(pr + 1) * 128]
                dyp = dyv[:, pr * 128:(pr + 1) * 128]
                L0, L1 = _head_decay(q, d, 2 * pr), _head_decay(q, d, 2 * pr + 1)
                dyst = jnp.concatenate([jnp.where(lane < HEAD, dyp, 0.0), jnp.where(lane < HEAD, 0.0, dyp)], axis=0)
                mst = jnp.concatenate([(q["CB"] * L0).astype(BF), (q["CB"] * L1).astype(BF)], axis=0)
                dxt_parts.append(_dot_tn(mst, dyst))
                dmst = _dot_nt(dyst, xp)
                dCB = dCB + dmst[:Q] * L0 + dmst[Q:] * L1
            dxt_diag = jnp.concatenate(dxt_parts, axis=1)
            dC = dC + _dot(dCB, B)
            dB = dB + _dot_tn(dCB, C)
            Rm = _reduce_mat(d)
            dacs = _dot_sl(dyv * y_ref[0] - xt.astype(BF).astype(F32) * dxt_diag - U * xw, Rm, 2)
            dxt = dxt + dxt_diag
            dal = _dot_sl(jnp.broadcast_to(dalast_e, (8, GWID)), Rm, 2)[0:1, :]
            dacs = dacs + jnp.where(row == (Q - 1 if d == 0 else 0), dal, 0.0)
            da = _dot_sr(q["Tt"], dacs, 2)
            ddt_ref[0, 0] = da * A + _dot_sl(dxt * xs, Rm, 2)
            da_ref[0, 0, 0:1, :] += jnp.sum(da * dt, axis=0, keepdims=True)
            dxs_ref[0] = (dxt * q["dt_e"]).astype(BF)
            dbc_ref[0] = jnp.concatenate([dB, dC], axis=1).astype(BF)
            dhT[d] = decay * dh + dH

    def spec(shape, imap):
        return pl.BlockSpec(shape, imap)

    def ins(c):
        return [spec((1, Q, GWID), lambda e, g, s: (e, c(s), g)),
                spec((1, Q, NST), lambda e, g, s: (e, c(s), DIN // NST + g)),
                spec((1, Q, NST), lambda e, g, s: (e, c(s), DIN // NST + NG + g)),
                spec((1, 1, Q, 128), lambda e, g, s: (e, g, c(s), 0)),
                spec((1, 1, NST, GWID), lambda e, g, s: (e, c(s), 0, g)),
                spec((1, Q, GWID), lambda e, g, s: (e, c(s), g)),
                spec((1, Q, GWID), lambda e, g, s: (e, c(s), g))]

    def outs(c):
        return [spec((1, Q, GWID), lambda e, g, s: (e, c(s), g)),
                spec((1, Q, 2 * NST), lambda e, g, s: (e, c(s), g)),
                spec((1, 1, Q, 128), lambda e, g, s: (e, g, c(s), 0))]

    s_xs = jax.ShapeDtypeStruct((2, R, DIN), BF)
    s_bc = jax.ShapeDtypeStruct((2, R, 2 * NG * NST), BF)
    s_dt = jax.ShapeDtypeStruct((2, NG, R, 128), F32)
    return pl.pallas_call(
        body, name="ssd_bwd", grid=(2, NG, NS),
        in_specs=ins(cf) + ins(cb) + [spec((1, 8, 128), lambda e, g, s: (g, 0, 0))],
        out_specs=tuple(outs(cf) + outs(cb) + [spec((1, 1, 8, 128), lambda e, g, s: (e, g, 0, 0))]),
        out_shape=(s_xs, s_bc, s_dt, s_xs, s_bc, s_dt, jax.ShapeDtypeStruct((2, NG, 8, 128), F32)),
        scratch_shapes=[pltpu.VMEM((2, NST, GWID), F32)],
        compiler_params=_params(("parallel", "parallel", "arbitrary")),
    )(xbc, xbc, xbc, dt_loc, hs_f, dy, y_f, xbc, xbc, xbc, dt_loc, hs_b, dy, y_b, a_loc)


def _ssd_post_fwd(y_f, y_b, xbc, proj3, dskip_e, ssd_norm, L):
    def body(yf_ref, yb_ref, xs_ref, z_ref, ds_ref, w_ref, o_ref):
        y2 = yf_ref[0] + yb_ref[0] + ds_ref[...] * xs_ref[0].astype(F32)
        z = z_ref[0].astype(F32)
        u = y2 * (z * _sigmoid(z))
        parts = []
        for g in range(NG):
            ug = u[:, g * GWID:(g + 1) * GWID]
            parts.append(ug * lax.rsqrt(jnp.mean(ug * ug, axis=-1, keepdims=True) + EPS))
        o_ref[0] = (jnp.concatenate(parts, axis=1) * w_ref[...]).astype(BF)

    blk = lambda c: pl.BlockSpec((1, ROW_TILE, DIN), lambda e, t: (e, t, c))
    vec = pl.BlockSpec((1, DIN), lambda e, t: (0, 0))
    return pl.pallas_call(
        body, name="ssd_post_fwd", grid=(2, L // ROW_TILE),
        in_specs=[blk(0), blk(0), blk(0), blk(1), vec, vec],
        out_specs=blk(0),
        out_shape=jax.ShapeDtypeStruct((2, L, DIN), BF),
        compiler_params=_params(("parallel", "parallel")),
    )(y_f, y_b, xbc, proj3, dskip_e, ssd_norm)


def _ssd_post_bwd(d_yn, y_f, y_b, xbc, proj3, dskip_e, ssd_norm, L):
    _, R, _ = y_f.shape
    nx = L // ROW_TILE

    def body(dyn_ref, yf_ref, yb_ref, xs_ref, z_ref, ds_ref, w_ref, dy_ref, dz_ref, acc_ref):
        t = pl.program_id(1)

        @pl.when(t == 0)
        def _():
            acc_ref[...] = jnp.zeros_like(acc_ref)

        @pl.when(t >= nx)
        def _():
            dy_ref[...] = jnp.zeros_like(dy_ref)
            dz_ref[...] = jnp.zeros_like(dz_ref)

        @pl.when(t < nx)
        def _():
            xs = xs_ref[0].astype(F32)
            y2 = yf_ref[0] + yb_ref[0] + ds_ref[...] * xs
            z = z_ref[0].astype(F32)
            sg = _sigmoid(z)
            sz = z * sg
            u = y2 * sz
            dyn = dyn_ref[0].astype(F32)
            dun = dyn * w_ref[...]
            uh_parts, du_parts = [], []
            for g in range(NG):
                sl = slice(g * GWID, (g + 1) * GWID)
                ug = u[:, sl]
                rg = lax.rsqrt(jnp.mean(ug * ug, axis=-1, keepdims=True) + EPS)
                uh = ug * rg
                dg = dun[:, sl]
                du_parts.append(rg * (dg - uh * jnp.mean(dg * uh, axis=-1, keepdims=True)))
                uh_parts.append(uh)
            du = jnp.concatenate(du_parts, axis=1)
            uh = jnp.concatenate(uh_parts, axis=1)
            dy2 = du * sz
            dy_ref[0] = dy2.astype(BF)
            dz_ref[0] = (du * y2 * (sg * (1.0 + z * (1.0 - sg)))).astype(BF)
            acc_ref[0, 0:1, :] += jnp.sum(dyn * uh, axis=0, keepdims=True)
            acc_ref[0, 1:2, :] += jnp.sum(dy2 * xs, axis=0, keepdims=True)

    xmap = lambda c: (lambda e, t: (e, jnp.minimum(t, nx - 1), c))
    blk = lambda c: pl.BlockSpec((1, ROW_TILE, DIN), xmap(c))
    oblk = pl.BlockSpec((1, ROW_TILE, DIN), lambda e, t: (e, t, 0))
    vec = pl.BlockSpec((1, DIN), lambda e, t: (0, 0))
    return pl.pallas_call(
        body, name="ssd_post_bwd", grid=(2, R // ROW_TILE),
        in_specs=[blk(0), blk(0), blk(0), blk(0), blk(1), vec, vec],
        out_specs=(oblk, oblk, pl.BlockSpec((1, 8, DIN), lambda e, t: (e, 0, 0))),
        out_shape=(jax.ShapeDtypeStruct((2, R, DIN), BF), jax.ShapeDtypeStruct((2, R, DIN), BF),
                   jax.ShapeDtypeStruct((2, 8, DIN), F32)),
        compiler_params=_params(("parallel", "arbitrary")),
    )(d_yn, y_f, y_b, xbc, proj3, dskip_e, ssd_norm)


def _merge_fwd(proj3, P, S, b_merge, L):
    def body(gp_ref, p_ref, s_ref, b_ref, o_ref):
        gt = _sigmoid(gp_ref[0].astype(F32) + b_ref[...])
        o_ref[0] = (gt[:, :D] * p_ref[0].astype(F32) + gt[:, D:] * s_ref[0].astype(F32)).astype(BF)

    blk = pl.BlockSpec((1, ROW_TILE, D), lambda e, t: (e, t, 0))
    return pl.pallas_call(
        body, name="merge_fwd", grid=(2, L // ROW_TILE),
        in_specs=[pl.BlockSpec((1, ROW_TILE, 2 * D), lambda e, t: (e, t, OFF_GATE // (2 * D))), blk, blk,
                  pl.BlockSpec((1, 2 * D), lambda e, t: (0, 0))],
        out_specs=blk, out_shape=jax.ShapeDtypeStruct((2, L, D), BF),
        compiler_params=_params(("parallel", "parallel")),
    )(proj3, P, S, b_merge)


def _merge_bwd(d_merged, proj3, P, S, b_merge, L):
    _, R, _ = proj3.shape
    nx = L // ROW_TILE

    def body(dm_ref, gp_ref, p_ref, s_ref, b_ref, dp_ref, ds_ref, dg_ref, acc_ref):
        t = pl.program_id(1)

        @pl.when(t == 0)
        def _():
            acc_ref[...] = jnp.zeros_like(acc_ref)

        @pl.when(t >= nx)
        def _():
            dg_ref[...] = jnp.zeros_like(dg_ref)

        @pl.when(t < nx)
        def _():
            gt = _sigmoid(gp_ref[0].astype(F32) + b_ref[...])
            dm = dm_ref[0].astype(F32)
            g1, g2 = gt[:, :D], gt[:, D:]
            dp_ref[0] = (dm * g1).astype(BF)
            ds_ref[0] = (dm * g2).astype(BF)
            dgp = jnp.concatenate([dm * p_ref[0].astype(F32) * g1 * (1.0 - g1),
                                   dm * s_ref[0].astype(F32) * g2 * (1.0 - g2)], axis=1)
            dg_ref[0] = dgp.astype(BF)
            acc_ref[0, 0:1, :] += jnp.sum(dgp, axis=0, keepdims=True)

    xmap = lambda e, t: (e, jnp.minimum(t, nx - 1), 0)
    blk = pl.BlockSpec((1, ROW_TILE, D), xmap)
    return pl.pallas_call(
        body, name="merge_bwd", grid=(2, R // ROW_TILE),
        in_specs=[blk, pl.BlockSpec((1, ROW_TILE, 2 * D), lambda e, t: (e, jnp.minimum(t, nx - 1), OFF_GATE // (2 * D))),
                  blk, blk, pl.BlockSpec((1, 2 * D), lambda e, t: (0, 0))],
        out_specs=(blk, blk, pl.BlockSpec((1, ROW_TILE, 2 * D), lambda e, t: (e, t, 0)),
                   pl.BlockSpec((1, 8, 2 * D), lambda e, t: (e, 0, 0))),
        out_shape=(jax.ShapeDtypeStruct((2, L, D), BF), jax.ShapeDtypeStruct((2, L, D), BF),
                   jax.ShapeDtypeStruct((2, R, 2 * D), BF), jax.ShapeDtypeStruct((2, 8, 2 * D), F32)),
        compiler_params=_params(("parallel", "arbitrary")),
    )(d_merged, proj3, P, S, b_merge)


def _final(out3, x, tgt, gtab, norm_post, L):
    def body(o_ref, x_ref, t_ref, g_ref, n_ref, dxo_ref, do_ref, acc_ref):
        @pl.when(pl.program_id(1) == 0)
        def _():
            acc_ref[...] = jnp.zeros_like(acc_ref)

        o = o_ref[0].astype(F32)
        gate = g_ref[0, 0:1, :]
        npost = n_ref[...]
        r2 = lax.rsqrt(jnp.mean(o * o, axis=-1, keepdims=True) + EPS)
        nh = o * r2
        on = nh * npost
        err = x_ref[0] + gate * on - t_ref[0]
        dxo = err * (1.0 / D)
        dxo_ref[0] = dxo
        dnh = dxo * gate * npost
        do_ref[0] = (r2 * (dnh - nh * jnp.mean(dnh * nh, axis=-1, keepdims=True))).astype(BF)
        acc_ref[0, 0:1, :] += jnp.sum(dxo * on, axis=0, keepdims=True)
        acc_ref[0, 1:2, :] += jnp.sum(dxo * gate * nh, axis=0, keepdims=True)
        acc_ref[0, 2:3, :] += jnp.sum(err * err, axis=0, keepdims=True)

    blk = pl.BlockSpec((1, ROW_TILE, D), lambda e, t: (e, t, 0))
    return pl.pallas_call(
        body, name="final", grid=(2, L // ROW_TILE),
        in_specs=[blk, blk, blk, pl.BlockSpec((1, 8, D), lambda e, t: (e, 0, 0)),
                  pl.BlockSpec((1, D), lambda e, t: (0, 0))],
        out_specs=(blk, blk, pl.BlockSpec((1, 8, D), lambda e, t: (e, 0, 0))),
        out_shape=(jax.ShapeDtypeStruct((2, L, D), F32), jax.ShapeDtypeStruct((2, L, D), BF),
                   jax.ShapeDtypeStruct((2, 8, D), F32)),
        compiler_params=_params(("parallel", "arbitrary")),
    )(out3, x, tgt, gtab, norm_post)


def _local_step(x, c, ctx, loss_target, W, late_shard=None, exchange=False):
    nb, L, _ = x.shape
    LC = ctx.shape[1]
    R = L + LC
    assert nb == 2 and L % ROW_TILE == 0 and LC % Q == 0 and L % POOL_TILE == 0
    w_inT = W["w_in"]
    w_dtT = jnp.pad(w_inT[OFF_DT:], ((0, 64), (0, 0)))
    tables = _pool_tables(L)
    tr, tl = (2 * R) // 8, (2 * L) // 8

    c16 = jnp.zeros((16, D), F32).at[0:2].set(c).at[2].set(W["c_ctx"])
    mod16 = _adaln_fwd(c16, W["w_ada"], W["b_ada"])
    shift, scale, gate = mod16[:, :D], mod16[:, D:2 * D], mod16[:, 2 * D:]
    npre = W["norm_pre"]
    tab = jnp.zeros((2, 2, 8, D), F32)
    for e in range(2):
        tab = tab.at[e, 0, 0].set(npre[0] * (1.0 + scale[e])).at[e, 0, 1].set(shift[e])
        tab = tab.at[e, 1, 0].set(npre[0] * (1.0 + scale[2])).at[e, 1, 1].set(shift[2])
    gtab = jnp.zeros((2, 8, D), F32).at[:, 0].set(gate[0:2])

    hx = _norm_mod_fwd(x, ctx, tab)
    hx2 = hx.reshape(2 * R, D)
    if late_shard is None:
        proj = _matmul(hx2, w_inT, BF, "proj_main", tm=tr, tn=1024, bt=True, n=OFF_DT)
    else:
        proj, late = _matmul(hx2, w_inT, BF, "proj_main", tm=tr, tn=1024, bt=True, n=OFF_DT, side=_gather_side(late_shard))
        W = {**W, **_unpack_gather(late, GATHER_LATE)}
    proj3 = proj.reshape(2, R, OFF_DT)
    dt_raw = _matmul(hx2, w_dtT, F32, "proj_dt", tm=tr, bt=True).reshape(2, R, 128)
    ypool = _pool_fwd(proj3, W["pool_w"], W["pool_scale"], tables, L)
    xbc = _conv_fwd(proj3, W["conv_w"], W["conv_b"], L)
    bias128 = jnp.pad(W["dt_bias"].reshape(1, 64), ((0, 0), (0, 64)))
    dt_loc = _dt_fwd(dt_raw, bias128)
    A = -jnp.exp(W["a_log"].reshape(2, NG, HPG))
    a_loc = jnp.zeros((NG, 8, 128), F32).at[:, 0, :16].set(A.transpose(1, 0, 2).reshape(NG, 16))
    y_f, hs_f, y_b, hs_b = _ssd_fwd(xbc, dt_loc, a_loc, L)
    dskip_e = jnp.repeat(W["d_skip"].reshape(1, 32), HEAD, axis=1)
    yn = _ssd_post_fwd(y_f, y_b, xbc, proj3, dskip_e, W["ssd_norm"], L)
    ypool2, yn2 = ypool.reshape(2 * L, D), yn.reshape(2 * L, DIN)
    P = _matmul(ypool2, W["w_proj_pool"], BF, "proj_pool", tm=tl, tn=1024).reshape(2, L, D)
    S = _matmul(yn2, W["w_proj_ssd"], BF, "proj_ssd", tm=tl, tn=1024).reshape(2, L, D)
    merged = _merge_fwd(proj3, P, S, W["b_merge"], L)
    merged2 = merged.reshape(2 * L, D)
    out3 = _matmul(merged2, W["w_out"], BF, "proj_out", tm=tl, tn=1024).reshape(2, L, D)
    dxo, dout, acc_f = _final(out3, x, loss_target, gtab, W["norm_post"], L)

    dout2 = dout.reshape(2 * L, D)
    g = {}
    g["w_out"] = _matmul_tn(merged2, dout2, "dw_out", ta=1024, tn=1024, tr=2 * tl)
    d_merged = _matmul(dout2, W["w_out"], BF, "d_merged", tm=tl, tn=1024, bt=True).reshape(2, L, D)
    dP, dS, dgp, acc_m = _merge_bwd(d_merged, proj3, P, S, W["b_merge"], L)
    dP2, dS2 = dP.reshape(2 * L, D), dS.reshape(2 * L, D)
    g["w_proj_pool"] = _matmul_tn(ypool2, dP2, "dw_proj_pool", ta=1024, tn=1024, tr=2 * tl)
    g["w_proj_ssd"] = _matmul_tn(yn2, dS2, "dw_proj_ssd", ta=1024, tn=1024, tr=2 * tl)
    d_ypool = _matmul(dP2, W["w_proj_pool"], BF, "d_ypool", tm=tl, tn=1024, bt=True).reshape(2, L, D)
    d_yn = _matmul(dS2, W["w_proj_ssd"], BF, "d_yn", tm=tl, tn=1024, bt=True).reshape(2, L, DIN)
    dv, dzp, g["pool_w"], acc_p = _pool_bwd(proj3, d_ypool, W["pool_w"], jnp.swapaxes(W["pool_w"], 1, 2),
                                            W["pool_scale"], tables, L)
    dy2, dzs, acc_s = _ssd_post_bwd(d_yn, y_f, y_b, xbc, proj3, dskip_e, W["ssd_norm"], L)
    dxs_f, dbc_f, ddt_f, dxs_b, dbc_b, ddt_b, acc_a = _ssd_bwd(xbc, dt_loc, a_loc, hs_f, hs_b, y_f, y_b, dy2, L)
    ident = lambda j: j
    dxr_xs, acc_cx = _conv_bwd(proj3, [dxs_f, dxs_b, dy2], [None, None, dskip_e], 0, DIN, [ident, ident, ident],
                               W["conv_w"], W["conv_b"], L, "conv_bwd_xs")
    bcmap = lambda j: 2 * lax.rem(j, NG) + j // NG
    dxr_bc, acc_cb = _conv_bwd(proj3, [dbc_f, dbc_b], [None, None], DIN, 2 * NG * NST, [bcmap, bcmap],
                               W["conv_w"], W["conv_b"], L, "conv_bwd_bc")
    ddtr, acc_d = _dt_bwd(dt_raw, bias128, ddt_f, ddt_b)
    pieces = [dv, dzp, dzs, dgp, dxr_xs, dxr_bc]
    dw_rows = [_matmul_tn(p.reshape(2 * R, p.shape[2]), hx2, "dw_in_%d" % i, ta=1024, tn=1024, tr=2 * tr)
               for i, p in enumerate(pieces)]
    dw_rows.append(_matmul_tn(ddtr.reshape(2 * R, 128), hx2, "dw_in_dt", ta=128, tn=1024, tr=tr)[:64])
    g["w_in"] = jnp.concatenate(dw_rows, axis=0)
    acc_c = jnp.concatenate([acc_cx[0] + acc_cx[1], acc_cb[0] + acc_cb[1]], axis=1)
    g["conv_w"] = acc_c[0:4]
    g["conv_b"] = acc_c[4:5]
    if exchange:
        gb = _pack_grads(g, GRADS_EARLY)
        pair = _pair_add(gb, _pair_exchange(gb, None, "grads_pair_exchange_early"), "grads_pair_add_early")
        dh, recv_early = _dhx(pieces, ddtr, w_inT, w_dtT, side=_chip_exchange_side(pair))
    else:
        dh, recv_early = _dhx(pieces, ddtr, w_inT, w_dtT), None
    grad_x, acc_n = _norm_mod_bwd(dh, x, ctx, tab, dxo)
    g["w_ada"], db_rows, sm_rows = _adaln_bwd(acc_n, acc_f, mod16, c16, npre, W["w_ada"])

    g["b_ada"] = db_rows[0:1]
    g["norm_pre"] = sm_rows[0:1]
    g["c_ctx"] = sm_rows[1]
    g["norm_post"] = acc_f[0, 1:2] + acc_f[1, 1:2]
    g["b_merge"] = acc_m[0, 0:1] + acc_m[1, 0:1]
    g["pool_scale"] = acc_p[:, 0, :].reshape(1, D)
    g["dt_bias"] = (acc_d[0, 0, :64] + acc_d[1, 0, :64]).reshape(2, 32)
    dA = (acc_a[0, :, 0, :16] + acc_a[1, :, 0, :16]).reshape(NG, 2, HPG).transpose(1, 0, 2)
    g["a_log"] = (dA * A).reshape(2, 32)
    g["d_skip"] = (acc_s[0, 1] + acc_s[1, 1]).reshape(32, HEAD).sum(axis=1).reshape(1, 32)
    g["ssd_norm"] = acc_s[0, 0:1] + acc_s[1, 0:1]
    loss_lanes = acc_f[:, 2, :]
    return loss_lanes, grad_x, g, recv_early


MESH = pl.DeviceIdType.MESH
ANY = pl.BlockSpec(memory_space=pl.ANY)


def _all_gather(shard):
    m_per, n = shard.shape

    def body(x_ref, out_ref, send_sems, recv_sems, local_sem):
        x, y, c = lax.axis_index("x"), lax.axis_index("y"), lax.axis_index("c")
        me, sibling = (x, y, c), (x, y, 1 - c)
        chips = [(1 - x, y), (x, 1 - y), (1 - x, 1 - y)]

        def rows(px, py, pc):
            return out_ref.at[pl.ds((4 * px + 2 * py + pc) * m_per, m_per), :]

        def copy(k, block, to, src=None):
            return pltpu.make_async_remote_copy(
                src_ref=rows(*block) if src is None else src, dst_ref=rows(*block),
                send_sem=send_sems.at[k], recv_sem=recv_sems.at[k], device_id=to, device_id_type=MESH)

        mine = pltpu.make_async_copy(x_ref, rows(*me), local_sem)
        mine.start()
        first = [copy(0, me, sibling, src=x_ref)]
        first += [copy(1 + j, me, (*chip, c), src=x_ref) for j, chip in enumerate(chips)]
        for cp in first:
            cp.start()
        passed = [copy(4 + j, (*chip, c), sibling) for j, chip in enumerate(chips)]
        for j, chip in enumerate(chips):
            copy(1 + j, (*chip, c), me).wait_recv()
            passed[j].start()
        copy(0, sibling, me).wait_recv()
        for j, chip in enumerate(chips):
            copy(4 + j, (*chip, 1 - c), me).wait_recv()
        for cp in first + passed:
            cp.wait_send()
        mine.wait()

    return pl.pallas_call(
        body, name="all_gather_weights",
        out_shape=jax.ShapeDtypeStruct((NDEV * m_per, n), shard.dtype),
        in_specs=[ANY], out_specs=ANY,
        scratch_shapes=[pltpu.SemaphoreType.DMA((7,)), pltpu.SemaphoreType.DMA((7,)), pltpu.SemaphoreType.DMA],
    )(shard)


PAIR_PIECES = 4


def _xor_peer(k, x, y, c):
    return (1 - x if k & 4 else x, 1 - y if k & 2 else y, 1 - c if k & 1 else c)


def _pair_exchange(big, small, name):
    _, nq, rows, n = big.shape
    piece = rows // PAIR_PIECES
    assert piece * PAIR_PIECES == rows and piece % 16 == 0
    with_small = small is not None

    def body(*refs):
        if with_small:
            big_ref, small_ref, got_ref, osmall_ref, send_sems, recv_sems, local_sem = refs
        else:
            big_ref, got_ref, send_sems, recv_sems, local_sem = refs
        x, y, c = lax.axis_index("x"), lax.axis_index("y"), lax.axis_index("c")
        me = 4 * x + 2 * y + c

        def rc(src, dst, sem, peer):
            return pltpu.make_async_remote_copy(src_ref=src, dst_ref=dst, send_sem=send_sems.at[sem],
                                                recv_sem=recv_sems.at[sem], device_id=peer, device_id_type=MESH)

        sib = _xor_peer(1, x, y, c)
        local, sends, recvs = [], [], []
        for q in range(nq):
            for h in range(PAIR_PIECES):
                rws = pl.ds(h * piece, piece)
                cp = rc(big_ref.at[1 - c, q, rws], got_ref.at[q, rws], 8 + q * PAIR_PIECES + h, sib)
                sends.append(cp)
                recvs.append(cp)
        if with_small:
            local.append(pltpu.make_async_copy(small_ref, osmall_ref.at[me], local_sem))
            for k in range(1, NDEV):
                px, py, pc = _xor_peer(k, x, y, c)
                sends.append(rc(small_ref, osmall_ref.at[me], k, (px, py, pc)))
                recvs.append(rc(small_ref, osmall_ref.at[4 * px + 2 * py + pc], k, (px, py, pc)))
        for cp in local + sends:
            cp.start()
        for cp in sends:
            cp.wait_send()
        for cp in recvs:
            cp.wait_recv()
        for cp in local:
            cp.wait()

    nsem = 8 + nq * PAIR_PIECES
    out_shape = [jax.ShapeDtypeStruct(big.shape[1:], big.dtype)]
    if with_small:
        out_shape.append(jax.ShapeDtypeStruct((NDEV,) + small.shape, small.dtype))
    out = pl.pallas_call(
        body, name=name, out_shape=tuple(out_shape),
        in_specs=[ANY] * (1 + with_small), out_specs=(ANY,) * (1 + with_small),
        scratch_shapes=[pltpu.SemaphoreType.DMA((nsem,)), pltpu.SemaphoreType.DMA((nsem,)), pltpu.SemaphoreType.DMA],
    )(*((big, small) if with_small else (big,)))
    return out if with_small else out[0]


def _pair_add(big, got, name):
    _, nq, rows, n = big.shape
    tile = rows // 4
    assert rows % 64 == 0

    def body(c_ref, a_ref, b_ref, o_ref):
        o_ref[0] = (a_ref[0, 0].astype(F32) + b_ref[0].astype(F32)).astype(BF)

    blk = pl.BlockSpec((1, tile, n), lambda q, i, c_ref: (q, i, 0))
    return pl.pallas_call(
        body, name=name,
        grid_spec=pltpu.PrefetchScalarGridSpec(
            num_scalar_prefetch=1, grid=(nq, rows // tile),
            in_specs=[pl.BlockSpec((1, 1, tile, n), lambda q, i, c_ref: (c_ref[0], q, i, 0)), blk], out_specs=blk),
        out_shape=jax.ShapeDtypeStruct(got.shape, BF), compiler_params=_params(("parallel", "parallel")),
    )(lax.axis_index("c").astype(jnp.int32).reshape(1), big, got)


def _chip_exchange_side(pair):
    def make(in_refs, out_refs, send_sems, recv_sems, local_sem, arrivals=True):
        (in_ref,), (out_ref,) = in_refs, out_refs
        x, y, c = lax.axis_index("x"), lax.axis_index("y"), lax.axis_index("c")
        q = 2 * x + y
        local = [pltpu.make_async_copy(in_ref.at[q], out_ref.at[q], local_sem)]
        sends, recvs = [], []
        for j in range(1, 4):
            px, py, pc = _xor_peer(2 * j, x, y, c)
            pq = 2 * px + py
            for lst, dst in ((sends, out_ref.at[q]), (recvs, out_ref.at[pq]))[:1 + arrivals]:
                lst.append(pltpu.make_async_remote_copy(
                    src_ref=in_ref.at[pq], dst_ref=dst, send_sem=send_sems.at[j - 1], recv_sem=recv_sems.at[j - 1],
                    device_id=(px, py, pc), device_id_type=MESH))
        return local, sends, recvs

    return _SideCopies([pair], [jax.ShapeDtypeStruct(pair.shape, pair.dtype)], make)


def _gather_side(shard):
    def make(in_refs, out_refs, send_sems, recv_sems, local_sem, arrivals=True):
        (src,), (dst,) = in_refs, out_refs
        x, y, c = lax.axis_index("x"), lax.axis_index("y"), lax.axis_index("c")
        me = 4 * x + 2 * y + c
        local = [pltpu.make_async_copy(src, dst.at[me], local_sem)]
        sends, recvs = [], []
        for k in range(1, NDEV):
            px, py, pc = _xor_peer(k, x, y, c)
            for lst, slot in ((sends, me), (recvs, 4 * px + 2 * py + pc))[:1 + arrivals]:
                lst.append(pltpu.make_async_remote_copy(
                    src_ref=src, dst_ref=dst.at[slot], send_sem=send_sems.at[k - 1], recv_sem=recv_sems.at[k - 1],
                    device_id=(px, py, pc), device_id_type=MESH))
        return local, sends, recvs

    return _SideCopies([shard], [jax.ShapeDtypeStruct((NDEV,) + shard.shape, shard.dtype)], make)


ADAM_TILE = 64
PACK_W = 1024


def _adamw(recv, w, m, v, name, side=None):
    rp = w.shape[0]
    tile = min(ADAM_TILE, rp)
    nsrc = recv.shape[0]
    grid = (rp // tile,)
    n_si, n_so = (len(side.inputs), len(side.out_shapes)) if side else (0, 0)

    def body(*refs):
        r_ref, w_ref, m_ref, v_ref = refs[:4]
        g_ref, d_ref, nm_ref, nv_ref = refs[4 + n_si:8 + n_si]
        side_refs = (refs[4:4 + n_si], refs[8 + n_si:8 + n_si + n_so], refs[8 + n_si + n_so:])
        if side:
            side.start(grid, *side_refs)
        g = r_ref[0].astype(F32)
        for i in range(1, nsrc):
            g = g + r_ref[i].astype(F32)
        m1 = ADAM_B1 * m_ref[...] + (1.0 - ADAM_B1) * g
        v1 = ADAM_B2 * v_ref[...] + (1.0 - ADAM_B2) * (g * g)
        m_hat = m1 / (1.0 - ADAM_B1 ** ADAM_STEP)
        v_hat = v1 / (1.0 - ADAM_B2 ** ADAM_STEP)
        g_ref[...] = g
        d_ref[...] = -ADAM_LR * (m_hat / (jnp.sqrt(v_hat) + ADAM_EPS) + ADAM_WD * w_ref[...])
        nm_ref[...] = m1
        nv_ref[...] = v1
        if side:
            side.wait(grid, *side_refs)

    blk = pl.BlockSpec((tile, PACK_W), lambda i: (i, 0))
    shp = jax.ShapeDtypeStruct((rp, PACK_W), F32)
    return pl.pallas_call(
        body, name=name, grid=grid,
        in_specs=[pl.BlockSpec((nsrc, tile, PACK_W), lambda i: (0, i, 0)), blk, blk, blk] + [ANY] * n_si,
        out_specs=(blk, blk, blk, blk) + (ANY,) * n_so,
        out_shape=(shp, shp, shp, shp) + tuple(side.out_shapes if side else ()),
        scratch_shapes=side.scratch() if side else [],
        compiler_params=_params(("arbitrary",) if side else ("parallel",)),
    )(recv, w, m, v, *(side.inputs if side else ()))


BIG = {"w_ada": ((3 * D, D), 0), "pool_w": ((4, PGW, PGW), 1), "w_proj_pool": ((D, D), 0), "w_proj_ssd": ((DIN, D), 0),
       "w_out": ((D, D), 0), "w_in": ((IN_COLS, D), 0), "conv_w": ((4, CONV_DIM), 1)}
TRANSPOSED = ("w_ada", "w_in")
PACK_ROWS = {"w_ada": 384, "w_in": 1168, "conv_w": 16, "pool_w": 32, "w_proj_pool": 128, "w_proj_ssd": 256, "w_out": 128}
GATHER_EARLY = ("w_ada", "w_in", "conv_w")
GATHER_LATE = ("pool_w", "w_proj_pool", "w_proj_ssd", "w_out")
GRADS_LATE = ("w_ada",)
GRADS_EARLY = tuple(n for n in PACK_ROWS if n not in GRADS_LATE)
SMALL = {"c_ctx": (D,), "b_ada": (1, 3 * D), "norm_pre": (1, D), "norm_post": (1, D), "b_merge": (1, 2 * D),
         "pool_scale": (1, D), "conv_b": (1, CONV_DIM), "dt_bias": (2, 32), "a_log": (2, 32), "d_skip": (1, 32),
         "ssd_norm": (1, DIN)}
LOSS_SLOT = 128
assert all(_r % 16 == 0 for _r in PACK_ROWS.values())
SMALL_ROWS = 16


def _shard_shape(name):
    shape, ax = BIG[name]
    return tuple(s // NDEV if i == ax else s for i, s in enumerate(shape))


def _as_rows(t, rows):
    pad = [(0, 0)] * (t.ndim - 1) + [(0, rows * PACK_W - t.shape[-1])]
    return jnp.pad(t, pad).reshape(t.shape[:-1] + (rows, PACK_W))


def _shard_rows(t, name):
    sh, r = _shard_shape(name), PACK_ROWS[name]
    lead = t.shape[:t.ndim - len(sh)]
    if len(sh) == 2 and sh[1] == PACK_W:
        return jnp.pad(t, [(0, 0)] * len(lead) + [(0, r - sh[0]), (0, 0)])
    if int(np.prod(sh)) == r * PACK_W:
        return t.reshape(lead + (r, PACK_W))
    return _as_rows(t.reshape(lead + (-1,)), r)


def _to_chunks(full, name):
    shape, ax = BIG[name]
    split = shape[:ax] + (NDEV, shape[ax] // NDEV) + shape[ax + 1:]
    return _shard_rows(jnp.moveaxis(full.reshape(split), ax, 0), name)


def _from_chunks(chunks, name):
    shape, ax = BIG[name]
    return jnp.moveaxis(chunks.reshape((NDEV,) + _shard_shape(name)), 0, ax).reshape(shape)


def _rows_of(names):
    return sum(PACK_ROWS[n] for n in names)


def _pack_state(t, names):
    return jnp.concatenate([_shard_rows(t[n], n) for n in names], axis=0)


def _pack_small(t, loss_part=None):
    slot = jnp.zeros((LOSS_SLOT,), F32)
    if loss_part is not None:
        slot = slot.at[0].set(loss_part)
    return _as_rows(jnp.concatenate([t[n].reshape(-1) for n in SMALL] + [slot]), SMALL_ROWS)


def _pack_grads(g, names):
    big = jnp.concatenate([_to_chunks(g[n], n).astype(BF) for n in names], axis=1)
    return jnp.swapaxes(big.reshape(4, 2, _rows_of(names), PACK_W), 0, 1)


def _unpack_state(big, names):
    out, off = {}, 0
    for n in names:
        sh, r = _shard_shape(n), PACK_ROWS[n]
        k = int(np.prod(sh))
        if len(sh) == 2 and sh[1] == PACK_W:
            out[n] = big[off:off + sh[0]]
        else:
            out[n] = big[off:off + r].reshape(-1)[:k].reshape(sh)
        off += r
    return out


def _unpack_small(small):
    out, flat, off = {}, small.reshape(-1), 0
    for n, sh in SMALL.items():
        k = int(np.prod(sh))
        out[n] = flat[off:off + k].reshape(sh)
        off += k
    out["loss"] = flat[off]
    return out


def _pack_gather(w, names):
    pieces = []
    for n in names:
        if n == "conv_w":
            pieces.append(_as_rows(jnp.concatenate([p.reshape(-1) for p in _split(w[n], 3)]), PACK_ROWS[n]))
        else:
            pieces.append(_shard_rows(w[n], n).astype(BF))
    return jnp.concatenate(pieces, axis=0)


def _unpack_gather(gathered, names):
    g = gathered.reshape(NDEV, _rows_of(names), PACK_W)
    out, off = {}, 0
    for n in names:
        r = PACK_ROWS[n]
        sh = _shard_shape(n)
        if n == "conv_w":
            k = int(np.prod(sh))
            terms = g[:, off:off + r].reshape(NDEV, -1)[:, :3 * k].astype(F32).reshape(NDEV, 3, k)
            out[n] = _from_chunks(terms[:, 0] + terms[:, 1] + terms[:, 2], n)
        elif len(sh) == 2 and sh[1] == PACK_W:
            out[n] = _from_chunks(g[:, off:off + sh[0]], n)
        else:
            out[n] = _from_chunks(g[:, off:off + r], n)
        off += r
    return out


PARAMS = ["c_ctx", "w_ada", "b_ada", "norm_pre", "norm_post", "w_in", "b_merge", "pool_w", "pool_scale", "conv_w", "conv_b",
          "dt_bias", "a_log", "d_skip", "ssd_norm", "w_proj_pool", "w_proj_ssd", "w_out"]


def kernel(x, c, ctx, c_ctx, w_ada, b_ada, norm_pre, norm_post, w_in, b_merge, pool_w, pool_scale, conv_w, conv_b, dt_bias, a_log, d_skip, ssd_norm, w_proj_pool, w_proj_ssd, w_out, loss_target, m_c_ctx, m_w_ada, m_b_ada, m_norm_pre, m_norm_post, m_w_in, m_b_merge, m_pool_w, m_pool_scale, m_conv_w, m_conv_b, m_dt_bias, m_a_log, m_d_skip, m_ssd_norm, m_w_proj_pool, m_w_proj_ssd, m_w_out, v_c_ctx, v_w_ada, v_b_ada, v_norm_pre, v_norm_post, v_w_in, v_b_merge, v_pool_w, v_pool_scale, v_conv_w, v_conv_b, v_dt_bias, v_a_log, v_d_skip, v_ssd_norm, v_w_proj_pool, v_w_proj_ssd, v_w_out):
    given = dict(locals())
    shapes = {n: given[n].shape for n in PARAMS}

    def local(prefix):
        t = {n: (given[prefix + n] if n == "c_ctx" else given[prefix + n][0]) for n in PARAMS}
        for n in TRANSPOSED:
            t[n] = t[n].T
        return {n: t[n].reshape(_shard_shape(n) if n in BIG else SMALL[n]) for n in PARAMS}

    w, m, v = local(""), local("m_"), local("v_")

    W = _unpack_gather(_all_gather(_pack_gather(w, GATHER_EARLY)), GATHER_EARLY)
    for n in SMALL:
        W[n] = w[n]
    lanes, grad_x, g, recv_early = _local_step(x, c, ctx, loss_target, W, late_shard=_pack_gather(w, GATHER_LATE),
                                               exchange=True)
    gb = _pack_grads(g, GRADS_LATE)
    got, recv_small = _pair_exchange(gb, _pack_small(g, (0.5 / D) * jnp.sum(lanes)), "grads_pair_exchange_late")
    late = _chip_exchange_side(_pair_add(gb, got, "grads_pair_add_late"))
    res = [{} for _ in range(4)]
    *early, recv_late = _adamw(recv_early, *[_pack_state(s, GRADS_EARLY) for s in (w, m, v)], "adamw_early", side=late)
    for r, t in zip(res, early):
        r.update(_unpack_state(t, GRADS_EARLY))
    for r, t in zip(res, _adamw(recv_late, *[_pack_state(s, GRADS_LATE) for s in (w, m, v)], "adamw_late")):
        r.update(_unpack_state(t, GRADS_LATE))
    for r, t in zip(res, _adamw(recv_small, *[_pack_small(s) for s in (w, m, v)], "adamw_small")):
        r.update(_unpack_small(t))
    outs = [res[0]["loss"], grad_x]
    for r in res:
        for n in TRANSPOSED:
            r[n] = r[n].T
        outs += [r[n].reshape(shapes[n]) for n in PARAMS]
    return tuple(outs)
```

```python
import functools

import numpy as np
import jax
import jax.numpy as jnp
from jax import lax
from jax.experimental import pallas as pl
from jax.experimental.pallas import tpu as pltpu

F32, BF = jnp.float32, jnp.bfloat16

D = 1024
GRID_W = 64
EPS = 1e-6
POOL_WINDOWS = (2, 4, 8, 16)
PGW = 256
DIN = 2048
HEAD = 64
NST = 128
NG = 4
HPG = 8
GWID = HPG * HEAD
Q = 128
CONV_DIM = 3072
OFF_GATE, OFF_XBC, OFF_DT, IN_COLS = 4096, 6144, 9216, 9280
NDEV = 8
ADAM_LR, ADAM_B1, ADAM_B2, ADAM_EPS, ADAM_WD, ADAM_STEP = 0.001, 0.9, 0.999, 1e-08, 0.01, 10

V7X_VMEM_LIMIT = 56 * 2 ** 20
ROW_TILE = 256


def _params(sem=None):
    return pltpu.CompilerParams(dimension_semantics=sem, vmem_limit_bytes=V7X_VMEM_LIMIT)


def _dot(a, b):
    return jnp.dot(a.astype(BF), b.astype(BF), preferred_element_type=F32)


def _dot_nt(a, b):
    return lax.dot_general(a.astype(BF), b.astype(BF), (((1,), (1,)), ((), ())), preferred_element_type=F32)


def _dot_tn(a, b):
    return lax.dot_general(a.astype(BF), b.astype(BF), (((0,), (0,)), ((), ())), preferred_element_type=F32)


def _split(a, n):
    parts = []
    for _ in range(n):
        p = a.astype(BF)
        parts.append(p)
        a = a - p.astype(F32)
    return parts


def _dot_sl(a, b01, n=3):
    parts = _split(a, n)
    m = a.shape[0]
    if n == 1 or m % 16:
        return sum(jnp.dot(p, b01, preferred_element_type=F32) for p in parts)
    r = jnp.dot(jnp.concatenate(parts, axis=0), b01, preferred_element_type=F32)
    return sum(r[i * m:(i + 1) * m] for i in range(n))


def _dot_sr(a01, b, n=3):
    parts = _split(b, n)
    k = b.shape[1]
    if n == 1 or k % 128:
        return sum(jnp.dot(a01, p, preferred_element_type=F32) for p in parts)
    r = jnp.dot(a01, jnp.concatenate(parts, axis=1), preferred_element_type=F32)
    return sum(r[:, i * k:(i + 1) * k] for i in range(n))


def _sigmoid(x):
    return 1.0 / (1.0 + jnp.exp(-x))


class _SideCopies:
    NSEM = 8

    def __init__(self, inputs, out_shapes, make):
        self.inputs, self.out_shapes, self.make = list(inputs), list(out_shapes), make

    def scratch(self):
        return [pltpu.SemaphoreType.DMA((self.NSEM,)), pltpu.SemaphoreType.DMA((self.NSEM,)), pltpu.SemaphoreType.DMA]

    def start(self, grid, in_refs, out_refs, sems):
        @pl.when(functools.reduce(lambda p, q: p & q, [pl.program_id(i) == 0 for i in range(len(grid))]))
        def _():
            local, sends, _ = self.make(in_refs, out_refs, *sems, arrivals=False)
            for cp in local + sends:
                cp.start()

    def wait(self, grid, in_refs, out_refs, sems):
        @pl.when(functools.reduce(lambda p, q: p & q, [pl.program_id(i) == n - 1 for i, n in enumerate(grid)]))
        def _():
            local, sends, recvs = self.make(in_refs, out_refs, *sems)
            for cp in sends:
                cp.wait_send()
            for cp in recvs:
                cp.wait_recv()
            for cp in local:
                cp.wait()


def _matmul(a, b, out_dtype, name, tm=512, tn=512, tk=1024, bt=False, n=None, side=None):
    M, K = a.shape
    N = n if n is not None else (b.shape[0] if bt else b.shape[1])
    tm, tn, tk = min(tm, M), min(tn, N), min(tk, K)
    assert M % tm == 0 and N % tn == 0 and K % tk == 0, (a.shape, b.shape)
    nk = K // tk
    grid = (M // tm, N // tn, nk)
    n_si, n_so = (len(side.inputs), len(side.out_shapes)) if side else (0, 0)

    def body(*refs):
        a_ref, b_ref, o_ref = refs[0], refs[1], refs[2 + n_si]
        acc = refs[3 + n_si + n_so]
        side_refs = (refs[2:2 + n_si], refs[3 + n_si:3 + n_si + n_so], refs[4 + n_si + n_so:])
        if side:
            side.start(grid, *side_refs)
        k = pl.program_id(2)
        p = _dot_nt(a_ref[...], b_ref[...]) if bt else _dot(a_ref[...], b_ref[...])

        @pl.when(k == 0)
        def _():
            acc[...] = p

        @pl.when(k > 0)
        def _():
            acc[...] += p

        @pl.when(k == nk - 1)
        def _():
            o_ref[...] = acc[...].astype(o_ref.dtype)

        if side:
            side.wait(grid, *side_refs)

    out = pl.pallas_call(
        body, name=name, grid=grid,
        in_specs=[pl.BlockSpec((tm, tk), lambda i, j, k: (i, k)),
                  pl.BlockSpec((tn, tk), lambda i, j, k: (j, k)) if bt else pl.BlockSpec((tk, tn), lambda i, j, k: (k, j))]
        + [ANY] * n_si,
        out_specs=(pl.BlockSpec((tm, tn), lambda i, j, k: (i, j)),) + (ANY,) * n_so,
        out_shape=(jax.ShapeDtypeStruct((M, N), out_dtype),) + tuple(side.out_shapes if side else ()),
        scratch_shapes=[pltpu.VMEM((tm, tn), F32)] + (side.scratch() if side else []),
        compiler_params=_params(("arbitrary",) * 3 if side else ("parallel", "parallel", "arbitrary")),
    )(a, b, *(side.inputs if side else ()))
    return out if side else out[0]


def _matmul_tn(a, g, name, ta=512, tn=512, tr=512):
    M, Ka = a.shape
    N = g.shape[1]
    ta, tn, tr = min(ta, Ka), min(tn, N), min(tr, M)
    assert M % tr == 0 and N % tn == 0 and Ka % ta == 0, (a.shape, g.shape)
    nr = M // tr

    def body(a_ref, g_ref, o_ref):
        k = pl.program_id(2)
        p = _dot_tn(a_ref[...], g_ref[...])

        @pl.when(k == 0)
        def _():
            o_ref[...] = p

        @pl.when(k > 0)
        def _():
            o_ref[...] += p

    return pl.pallas_call(
        body, name=name, grid=(Ka // ta, N // tn, nr),
        in_specs=[pl.BlockSpec((tr, ta), lambda i, j, k: (k, i)), pl.BlockSpec((tr, tn), lambda i, j, k: (k, j))],
        out_specs=pl.BlockSpec((ta, tn), lambda i, j, k: (i, j)),
        out_shape=jax.ShapeDtypeStruct((Ka, N), F32),
        compiler_params=_params(("parallel", "parallel", "arbitrary")),
    )(a, g)


def _dhx(pieces, ddt, w_inT, w_dtT, side=None):
    _, R, _ = pieces[0].shape
    tm = R // 4
    kb = 1024
    starts, nblk = [], []
    for p in pieces:
        starts.append(sum(nblk))
        nblk.append(p.shape[2] // kb)
    nk = sum(nblk)
    assert nk * kb == OFF_DT and R % 128 == 0
    npc = len(pieces)
    grid = (2, R // tm, nk)
    n_si, n_so = (len(side.inputs), len(side.out_shapes)) if side else (0, 0)

    def body(*refs):
        a_refs, dt_ref, w_ref, wdt_ref = refs[:npc], refs[npc], refs[npc + 1], refs[npc + 2]
        o_ref, acc = refs[npc + 3 + n_si], refs[npc + 4 + n_si + n_so]
        side_refs = (refs[npc + 3:npc + 3 + n_si], refs[npc + 4 + n_si:npc + 4 + n_si + n_so], refs[npc + 5 + n_si + n_so:])
        if side:
            side.start(grid, *side_refs)
        k = pl.program_id(2)

        @pl.when(k == 0)
        def _():
            acc[...] = _dot(dt_ref[0], wdt_ref[...])

        for p in range(npc):
            @pl.when((k >= starts[p]) & (k < starts[p] + nblk[p]))
            def _(p=p):
                acc[...] += _dot(a_refs[p][0], w_ref[...])

        @pl.when(k == nk - 1)
        def _():
            o_ref[0] = acc[...].astype(BF)

        if side:
            side.wait(grid, *side_refs)

    in_specs = [pl.BlockSpec((1, tm, kb), functools.partial(
        lambda e, t, k, s, nb: (e, t, jnp.clip(k - s, 0, nb - 1)), s=starts[p], nb=nblk[p])) for p in range(npc)]
    in_specs += [pl.BlockSpec((1, tm, 128), lambda e, t, k: (e, t, 0)),
                 pl.BlockSpec((kb, D), lambda e, t, k: (k, 0)),
                 pl.BlockSpec((128, D), lambda e, t, k: (0, 0))]
    out = pl.pallas_call(
        body, name="d_hx", grid=grid, in_specs=in_specs + [ANY] * n_si,
        out_specs=(pl.BlockSpec((1, tm, D), lambda e, t, k: (e, t, 0)),) + (ANY,) * n_so,
        out_shape=(jax.ShapeDtypeStruct((2, R, D), BF),) + tuple(side.out_shapes if side else ()),
        scratch_shapes=[pltpu.VMEM((tm, D), F32)] + (side.scratch() if side else []),
        compiler_params=_params(("arbitrary",) * 3 if side else ("parallel", "parallel", "arbitrary")),
    )(*pieces, ddt, w_inT, w_dtT, *(side.inputs if side else ()))
    return out if side else out[0]


def _adaln_fwd(c16, w_adaT_bf, b_ada):
    def body(c_ref, w_ref, b_ref, o_ref):
        cc = c_ref[...]
        o_ref[...] = _dot_nt(cc * _sigmoid(cc), w_ref[...]) + b_ref[...]

    return pl.pallas_call(body, name="adaln_fwd", out_shape=jax.ShapeDtypeStruct((16, 3 * D), F32),
                          compiler_params=_params())(c16, w_adaT_bf, b_ada)


def _adaln_bwd(acc_n, acc_f, mod16, c16, norm_pre, w_adaT_bf):
    def body(an_ref, af_ref, mod_ref, c_ref, np_ref, wt_ref, dw_ref, db_ref, sm_ref, dmod):
        npre = np_ref[...]
        dmod[...] = jnp.zeros_like(dmod)
        dnp = jnp.zeros((1, D), F32)
        dshift_c = jnp.zeros((1, D), F32)
        dgpre_c = jnp.zeros((1, D), F32)
        scale_c = mod_ref[2:3, D:2 * D]
        for e in range(2):
            dg_x, ds_x = an_ref[e, 0, 0:1, :], an_ref[e, 0, 1:2, :]
            dg_c, ds_c = an_ref[e, 1, 0:1, :], an_ref[e, 1, 1:2, :]
            dmod[e:e + 1, 0:D] = ds_x
            dmod[e:e + 1, D:2 * D] = dg_x * npre
            dmod[e:e + 1, 2 * D:3 * D] = af_ref[e, 0:1, :]
            dnp = dnp + dg_x * (1.0 + mod_ref[e:e + 1, D:2 * D]) + dg_c * (1.0 + scale_c)
            dshift_c = dshift_c + ds_c
            dgpre_c = dgpre_c + dg_c
        dmod[2:3, 0:D] = dshift_c
        dmod[2:3, D:2 * D] = dgpre_c * npre
        dm = dmod[...]
        cc = c_ref[...]
        sg = _sigmoid(cc)
        dw_ref[...] = _dot_tn(dm, cc * sg)
        db_ref[...] = jnp.zeros_like(db_ref)
        db_ref[0:1, :] = jnp.sum(dm, axis=0, keepdims=True)
        dsilu = sg * (1.0 + cc * (1.0 - sg))
        dcs = _dot(dm, wt_ref[...]) * dsilu
        sm_ref[...] = jnp.zeros_like(sm_ref)
        sm_ref[0:1, :] = dnp
        sm_ref[1:2, :] = dcs[2:3, :]

    return pl.pallas_call(
        body, name="adaln_bwd",
        out_shape=(jax.ShapeDtypeStruct((3 * D, D), F32), jax.ShapeDtypeStruct((16, 3 * D), F32),
                   jax.ShapeDtypeStruct((8, D), F32)),
        scratch_shapes=[pltpu.VMEM((16, 3 * D), F32)],
        compiler_params=_params())(acc_n, acc_f, mod16, c16, norm_pre, w_adaT_bf)


def _row_specs(L):
    nx = L // ROW_TILE
    return (pl.BlockSpec((1, ROW_TILE, D), lambda e, t: (e, jnp.minimum(t, nx - 1), 0)),
            pl.BlockSpec((1, ROW_TILE, D), lambda e, t: (e, jnp.maximum(t - nx, 0), 0)))


def _norm_mod_fwd(x, ctx, tab):
    L = x.shape[1]
    R = L + ctx.shape[1]
    nx = L // ROW_TILE

    def body(x_ref, c_ref, t_ref, o_ref):
        x = jnp.where(pl.program_id(1) < nx, x_ref[0], c_ref[0])
        r = lax.rsqrt(jnp.mean(x * x, axis=-1, keepdims=True) + EPS)
        t = t_ref[0, 0]
        o_ref[0] = (x * r * t[0:1] + t[1:2]).astype(BF)

    return pl.pallas_call(
        body, name="norm_mod_fwd", grid=(2, R // ROW_TILE),
        in_specs=[*_row_specs(L), pl.BlockSpec((1, 1, 8, D), lambda e, t: (e, t // nx, 0, 0))],
        out_specs=pl.BlockSpec((1, ROW_TILE, D), lambda e, t: (e, t, 0)),
        out_shape=jax.ShapeDtypeStruct((2, R, D), BF),
        compiler_params=_params(("parallel", "parallel")),
    )(x, ctx, tab)


def _norm_mod_bwd(dh, x, ctx, tab, dxo):
    L = x.shape[1]
    R = L + ctx.shape[1]
    nx = L // ROW_TILE

    def body(dh_ref, x_ref, c_ref, t_ref, dxo_ref, gx_ref, acc_ref):
        t = pl.program_id(1)
        x = jnp.where(t < nx, x_ref[0], c_ref[0])
        r = lax.rsqrt(jnp.mean(x * x, axis=-1, keepdims=True) + EPS)
        xn = x * r
        dh = dh_ref[0].astype(F32)

        @pl.when((t == 0) | (t == nx))
        def _():
            acc_ref[...] = jnp.zeros_like(acc_ref)

        acc_ref[0, 0, 0:1, :] += jnp.sum(dh * xn, axis=0, keepdims=True)
        acc_ref[0, 0, 1:2, :] += jnp.sum(dh, axis=0, keepdims=True)

        @pl.when(t < nx)
        def _():
            dxn = dh * t_ref[0, 0][0:1]
            dx = r * (dxn - xn * jnp.mean(dxn * xn, axis=-1, keepdims=True))
            gx_ref[0] = dxo_ref[0] + dx

    xspec, cspec = _row_specs(L)
    return pl.pallas_call(
        body, name="norm_mod_bwd", grid=(2, R // ROW_TILE),
        in_specs=[pl.BlockSpec((1, ROW_TILE, D), lambda e, t: (e, t, 0)), xspec, cspec,
                  pl.BlockSpec((1, 1, 8, D), lambda e, t: (e, t // nx, 0, 0)), xspec],
        out_specs=(xspec, pl.BlockSpec((1, 1, 8, D), lambda e, t: (e, t // nx, 0, 0))),
        out_shape=(jax.ShapeDtypeStruct((2, L, D), F32), jax.ShapeDtypeStruct((2, 2, 8, D), F32)),
        compiler_params=_params(("parallel", "arbitrary")),
    )(dh, x, ctx, tab, dxo)


POOL_TILE = 256


def _pool_tables(L):
    rows = L // GRID_W
    mats = np.zeros((4, POOL_TILE, POOL_TILE), np.float32)
    inv = np.zeros((4, L, 1), np.float32)
    for gi, k in enumerate(POOL_WINDOWS):
        lo, hi = k // 2, k - 1 - k // 2
        m = np.zeros((GRID_W, GRID_W), np.float32)
        for t in range(GRID_W):
            m[t, max(t - lo, 0):min(t + hi, GRID_W - 1) + 1] = 1.0
        for b in range(POOL_TILE // GRID_W):
            mats[gi, b * GRID_W:(b + 1) * GRID_W, b * GRID_W:(b + 1) * GRID_W] = m
        cnt_c = m.sum(1)
        cnt_r = np.array([min(r + hi, rows - 1) - max(r - lo, 0) + 1 for r in range(rows)], np.float32)
        inv[gi, :, 0] = (1.0 / (cnt_r[:, None] * cnt_c[None, :])).reshape(-1)
    matsT = np.ascontiguousarray(np.transpose(mats, (0, 2, 1)))
    return (jnp.asarray(mats, BF), jnp.asarray(matsT, BF), jnp.asarray(inv))


def _pool_cols(get_tile, mat, cs_ref, L, n):
    def step(i, carry):
        off = pl.multiple_of(i * POOL_TILE, POOL_TILE)
        cs_ref[pl.ds(GRID_W + off, POOL_TILE), :] = _dot_sr(mat, get_tile(off).astype(F32), n)
        return carry

    lax.fori_loop(0, L // POOL_TILE, step, 0)
    cs_ref[pl.ds(0, GRID_W), :] = jnp.zeros((GRID_W, PGW), F32)

    def prefix(r, carry):
        o = pl.multiple_of(r * GRID_W, GRID_W)
        cs_ref[pl.ds(o + GRID_W, GRID_W), :] = cs_ref[pl.ds(o + GRID_W, GRID_W), :] + cs_ref[pl.ds(o, GRID_W), :]
        return carry

    lax.fori_loop(0, L // GRID_W, prefix, 0)


def _pool_rows(cs_ref, off, below, above, L):
    rows = L // GRID_W
    r0 = off // GRID_W
    parts = []
    for i in range(POOL_TILE // GRID_W):
        hi = pl.multiple_of(jnp.minimum(r0 + i + above + 1, rows) * GRID_W, GRID_W)
        lo = pl.multiple_of(jnp.maximum(r0 + i - below, 0) * GRID_W, GRID_W)
        parts.append(cs_ref[pl.ds(hi, GRID_W), :] - cs_ref[pl.ds(lo, GRID_W), :])
    return jnp.concatenate(parts, axis=0)


def _pool_fwd(proj3, pool_w_bf, pool_scale, tables, L):
    mats, _, inv = tables
    nt = L // POOL_TILE

    def body(v_ref, z_ref, pw_ref, ps_ref, m_ref, inv_ref, o_ref, cs_ref):
        _pool_cols(lambda off: v_ref[0, pl.ds(off, POOL_TILE), :], m_ref[0], cs_ref, L, 1)
        half = lax.shift_left(1, pl.program_id(1))

        def step(i, carry):
            off = pl.multiple_of(i * POOL_TILE, POOL_TILE)
            rows = pl.ds(off, POOL_TILE)
            v = v_ref[0, rows, :].astype(F32)
            diff = _pool_rows(cs_ref, off, half, half - 1, L) * inv_ref[0, rows, :] - v
            yp = _dot(diff, pw_ref[0])
            z = z_ref[0, rows, :].astype(F32)
            o_ref[0, rows, :] = (yp * ps_ref[...] * (z * _sigmoid(z))).astype(BF)
            return carry

        lax.fori_loop(0, nt, step, 0)

    return pl.pallas_call(
        body, name="pool_fwd", grid=(2, 4),
        in_specs=[pl.BlockSpec((1, L, PGW), lambda e, g: (e, 0, g)),
                  pl.BlockSpec((1, L, PGW), lambda e, g: (e, 0, 4 + g)),
                  pl.BlockSpec((1, PGW, PGW), lambda e, g: (g, 0, 0)),
                  pl.BlockSpec((1, PGW), lambda e, g: (0, g)),
                  pl.BlockSpec((1, POOL_TILE, POOL_TILE), lambda e, g: (g, 0, 0)),
                  pl.BlockSpec((1, L, 1), lambda e, g: (g, 0, 0))],
        out_specs=pl.BlockSpec((1, L, PGW), lambda e, g: (e, 0, g)),
        out_shape=jax.ShapeDtypeStruct((2, L, D), BF),
        scratch_shapes=[pltpu.VMEM((L + GRID_W, PGW), F32)],
        compiler_params=_params(("parallel", "parallel")),
    )(proj3, proj3, pool_w_bf, pool_scale, mats, inv)


def _pool_bwd(proj3, d_ypool, pool_w_bf, pool_wT_bf, pool_scale, tables, L):
    mats, matsT, inv = tables
    nt = L // POOL_TILE
    R = proj3.shape[1]

    def body(v_ref, z_ref, dy_ref, pw_ref, pwt_ref, ps_ref, m_ref, mt_ref, inv_ref,
             dv_ref, dz_ref, dpw_ref, acc_ref, cs_ref, dd_ref):
        e = pl.program_id(1)

        @pl.when(e == 0)
        def _():
            dpw_ref[...] = jnp.zeros_like(dpw_ref)
            acc_ref[...] = jnp.zeros_like(acc_ref)

        _pool_cols(lambda off: v_ref[0, pl.ds(off, POOL_TILE), :], m_ref[0], cs_ref, L, 1)
        half = lax.shift_left(1, pl.program_id(0))
        ps = ps_ref[...]

        def step(i, carry):
            off = pl.multiple_of(i * POOL_TILE, POOL_TILE)
            rows = pl.ds(off, POOL_TILE)
            v = v_ref[0, rows, :].astype(F32)
            diff = _pool_rows(cs_ref, off, half, half - 1, L) * inv_ref[0, rows, :] - v
            yp = _dot(diff, pw_ref[0])
            z = z_ref[0, rows, :].astype(F32)
            sg = _sigmoid(z)
            sz = z * sg
            dy = dy_ref[0, rows, :].astype(F32)
            dz_ref[0, rows, :] = (dy * yp * ps * (sg * (1.0 + z * (1.0 - sg)))).astype(BF)
            dys = dy * sz
            acc_ref[0, 0:1, :] += jnp.sum(dys * yp, axis=0, keepdims=True)
            dyp = dys * ps
            dpw_ref[0] += _dot_tn(diff, dyp)
            dd_ref[rows, :] = _dot(dyp, pwt_ref[0])
            return carry

        lax.fori_loop(0, nt, step, 0)
        _pool_cols(lambda off: dd_ref[pl.ds(off, POOL_TILE), :] * inv_ref[0, pl.ds(off, POOL_TILE), :],
                   mt_ref[0], cs_ref, L, 2)

        def step2(i, carry):
            off = pl.multiple_of(i * POOL_TILE, POOL_TILE)
            rows = pl.ds(off, POOL_TILE)
            dv_ref[0, rows, :] = (_pool_rows(cs_ref, off, half - 1, half, L) - dd_ref[rows, :]).astype(BF)
            return carry

        lax.fori_loop(0, nt, step2, 0)
        dv_ref[0, pl.ds(L, R - L), :] = jnp.zeros((R - L, PGW), BF)
        dz_ref[0, pl.ds(L, R - L), :] = jnp.zeros((R - L, PGW), BF)

    return pl.pallas_call(
        body, name="pool_bwd", grid=(4, 2),
        in_specs=[pl.BlockSpec((1, L, PGW), lambda g, e: (e, 0, g)),
                  pl.BlockSpec((1, L, PGW), lambda g, e: (e, 0, 4 + g)),
                  pl.BlockSpec((1, L, PGW), lambda g, e: (e, 0, g)),
                  pl.BlockSpec((1, PGW, PGW), lambda g, e: (g, 0, 0)),
                  pl.BlockSpec((1, PGW, PGW), lambda g, e: (g, 0, 0)),
                  pl.BlockSpec((1, PGW), lambda g, e: (0, g)),
                  pl.BlockSpec((1, POOL_TILE, POOL_TILE), lambda g, e: (g, 0, 0)),
                  pl.BlockSpec((1, POOL_TILE, POOL_TILE), lambda g, e: (g, 0, 0)),
                  pl.BlockSpec((1, L, 1), lambda g, e: (g, 0, 0))],
        out_specs=(pl.BlockSpec((1, R, PGW), lambda g, e: (e, 0, g)),
                   pl.BlockSpec((1, R, PGW), lambda g, e: (e, 0, g)),
                   pl.BlockSpec((1, PGW, PGW), lambda g, e: (g, 0, 0)),
                   pl.BlockSpec((1, 8, PGW), lambda g, e: (g, 0, 0))),
        out_shape=(jax.ShapeDtypeStruct((2, R, D), BF), jax.ShapeDtypeStruct((2, R, D), BF),
                   jax.ShapeDtypeStruct((4, PGW, PGW), F32), jax.ShapeDtypeStruct((4, 8, PGW), F32)),
        scratch_shapes=[pltpu.VMEM((L + GRID_W, PGW), F32), pltpu.VMEM((L, PGW), F32)],
        compiler_params=_params(("parallel", "arbitrary")),
    )(proj3, proj3, d_ypool, pool_w_bf, pool_wT_bf, pool_scale, mats, matsT, inv)


CONV_BLOCK = 128


def _conv_tap(u, k, L):
    off = k - 2
    if off == 0:
        return u
    R = u.shape[0]
    r = lax.broadcasted_iota(jnp.int32, (R, 1), 0)
    pos = jnp.where(r < L, r, r - L) + off
    seg = jnp.where(r < L, L, R - L)
    return jnp.where((pos >= 0) & (pos < seg), pltpu.roll(u, (-off) % R, 0), 0.0)


def _conv_fwd(proj3, conv_w, conv_b, L):
    _, R, _ = proj3.shape
    cb0 = OFF_XBC // CONV_BLOCK

    def body(u_ref, w_ref, b_ref, o_ref):
        u = u_ref[0].astype(F32)
        w = w_ref[...]
        pre = b_ref[...] + sum(_conv_tap(u, k, L) * w[k:k + 1, :] for k in range(4))
        o_ref[0] = (pre * _sigmoid(pre)).astype(BF)

    return pl.pallas_call(
        body, name="conv_fwd", grid=(2, CONV_DIM // CONV_BLOCK),
        in_specs=[pl.BlockSpec((1, R, CONV_BLOCK), lambda e, j: (e, 0, cb0 + j)),
                  pl.BlockSpec((4, CONV_BLOCK), lambda e, j: (0, j)),
                  pl.BlockSpec((1, CONV_BLOCK), lambda e, j: (0, j))],
        out_specs=pl.BlockSpec((1, R, CONV_BLOCK), lambda e, j: (e, 0, j)),
        out_shape=jax.ShapeDtypeStruct((2, R, CONV_DIM), BF),
        compiler_params=_params(("parallel", "parallel")),
    )(proj3, conv_w, conv_b)


def _conv_bwd(proj3, addends, scales, col0, ncols, in_maps, conv_w, conv_b, L, name):
    _, R, _ = proj3.shape
    cb0 = (OFF_XBC + col0) // CONV_BLOCK
    wb0 = col0 // CONV_BLOCK
    na = len(addends)
    scaled = [i for i in range(na) if scales[i] is not None]

    def body(*refs):
        u_ref, w_ref, b_ref = refs[0], refs[1], refs[2]
        a_refs = refs[3:3 + na]
        s_refs = dict(zip(scaled, refs[3 + na:3 + na + len(scaled)]))
        o_ref, acc_ref = refs[3 + na + len(scaled)], refs[4 + na + len(scaled)]
        u = u_ref[0].astype(F32)
        w = w_ref[...]
        taps = [_conv_tap(u, k, L) for k in range(4)]
        pre = b_ref[...] + sum(taps[k] * w[k:k + 1, :] for k in range(4))
        sg = _sigmoid(pre)
        dxbc = jnp.zeros(u.shape, F32)
        for i, a in enumerate(a_refs):
            t = a[0].astype(F32)
            dxbc = dxbc + (t * s_refs[i][...] if i in s_refs else t)
        dpre = dxbc * (sg * (1.0 + pre * (1.0 - sg)))
        acc_ref[...] = jnp.zeros_like(acc_ref)
        for k in range(4):
            acc_ref[0, k:k + 1, :] = jnp.sum(dpre * taps[k], axis=0, keepdims=True)
        acc_ref[0, 4:5, :] = jnp.sum(dpre, axis=0, keepdims=True)
        du = sum(_conv_tap(dpre, 4 - k, L) * w[k:k + 1, :] for k in range(4))
        o_ref[0] = du.astype(BF)

    in_specs = [pl.BlockSpec((1, R, CONV_BLOCK), lambda e, j: (e, 0, cb0 + j)),
                pl.BlockSpec((4, CONV_BLOCK), lambda e, j: (0, wb0 + j)),
                pl.BlockSpec((1, CONV_BLOCK), lambda e, j: (0, wb0 + j))]
    for m in in_maps:
        in_specs.append(pl.BlockSpec((1, R, CONV_BLOCK), functools.partial(lambda e, j, m: (e, 0, m(j)), m=m)))
    for i in scaled:
        in_specs.append(pl.BlockSpec((1, CONV_BLOCK), functools.partial(lambda e, j, m: (0, m(j)), m=in_maps[i])))
    return pl.pallas_call(
        body, name=name, grid=(2, ncols // CONV_BLOCK),
        in_specs=in_specs,
        out_specs=(pl.BlockSpec((1, R, CONV_BLOCK), lambda e, j: (e, 0, j)),
                   pl.BlockSpec((1, 8, CONV_BLOCK), lambda e, j: (e, 0, j))),
        out_shape=(jax.ShapeDtypeStruct((2, R, ncols), BF), jax.ShapeDtypeStruct((2, 8, ncols), F32)),
        compiler_params=_params(("parallel", "parallel")),
    )(proj3, conv_w, conv_b, *addends, *[scales[i] for i in scaled])


def _softplus(x):
    e = jnp.exp(-jnp.abs(x))
    u = 1.0 + e
    return jnp.maximum(x, 0.0) + jnp.where(u == 1.0, e, e * jnp.log(u) / (u - 1.0))


def _to_local_mat(g, transpose=False):
    r = lax.broadcasted_iota(jnp.int32, (128, 128), 1 if transpose else 0)
    c = lax.broadcasted_iota(jnp.int32, (128, 128), 0 if transpose else 1)
    return ((c < 2 * HPG) & (r == jnp.right_shift(c, 3) * (NG * HPG) + g * HPG + (c & (HPG - 1)))).astype(BF)


def _dt_fwd(dt_raw, bias128):
    _, R, _ = dt_raw.shape

    def body(x_ref, b_ref, o_ref):
        dt = _softplus(x_ref[0] + b_ref[...])
        for g in range(NG):
            o_ref[0, g] = _dot_sl(dt, _to_local_mat(g))

    tr = R // 4
    return pl.pallas_call(
        body, name="dt_fwd", grid=(2, 4),
        in_specs=[pl.BlockSpec((1, tr, 128), lambda e, t: (e, t, 0)), pl.BlockSpec((1, 128), lambda e, t: (0, 0))],
        out_specs=pl.BlockSpec((1, NG, tr, 128), lambda e, t: (e, 0, t, 0)),
        out_shape=jax.ShapeDtypeStruct((2, NG, R, 128), F32),
        compiler_params=_params(("parallel", "parallel")),
    )(dt_raw, bias128)


def _dt_bwd(dt_raw, bias128, ddt_f, ddt_b):
    _, R, _ = dt_raw.shape

    def body(x_ref, b_ref, f_ref, g_ref, o_ref, acc_ref):
        ddt = sum(_dot_sl(f_ref[0, g] + g_ref[0, g], _to_local_mat(g, transpose=True)) for g in range(NG))
        d = ddt * _sigmoid(x_ref[0] + b_ref[...])
        o_ref[0] = d.astype(BF)

        @pl.when(pl.program_id(1) == 0)
        def _():
            acc_ref[...] = jnp.zeros_like(acc_ref)

        acc_ref[0, 0:1, :] += jnp.sum(d, axis=0, keepdims=True)

    tr = R // 4
    blk = pl.BlockSpec((1, tr, 128), lambda e, t: (e, t, 0))
    loc = pl.BlockSpec((1, NG, tr, 128), lambda e, t: (e, 0, t, 0))
    return pl.pallas_call(
        body, name="dt_bwd", grid=(2, 4),
        in_specs=[blk, pl.BlockSpec((1, 128), lambda e, t: (0, 0)), loc, loc],
        out_specs=(blk, pl.BlockSpec((1, 8, 128), lambda e, t: (e, 0, 0))),
        out_shape=(jax.ShapeDtypeStruct(dt_raw.shape, BF), jax.ShapeDtypeStruct((2, 8, 128), F32)),
        compiler_params=_params(("parallel", "arbitrary")),
    )(dt_raw, bias128, ddt_f, ddt_b)


GPS = 4


def _tri(d):
    i = lax.broadcasted_iota(jnp.int32, (Q, Q), 0)
    j = lax.broadcasted_iota(jnp.int32, (Q, Q), 1)
    return (i >= j) if d == 0 else (i <= j)


def _expand_mat(d):
    r = lax.broadcasted_iota(jnp.int32, (128, GWID), 0)
    c = lax.broadcasted_iota(jnp.int32, (128, GWID), 1)
    return (r == d * HPG + jnp.right_shift(c, 6)).astype(BF)


def _reduce_mat(d):
    r = lax.broadcasted_iota(jnp.int32, (GWID, 128), 0)
    c = lax.broadcasted_iota(jnp.int32, (GWID, 128), 1)
    return (c == d * HPG + jnp.right_shift(r, 6)).astype(BF)


def _ssd_chunk(d, dt, A, xs, B, C):
    mask = _tri(d)
    T = mask.astype(BF)
    Tt = _tri(1 - d).astype(BF)
    a = dt * A
    acs = _dot_sr(T, a)
    E = _expand_mat(d)
    dt_e = _dot_sl(dt, E, 2)
    acs_e = _dot_sl(acs, E, 2)
    alast_e = acs_e[Q - 1:Q, :] if d == 0 else acs_e[0:1, :]
    return dict(mask=mask, T=T, Tt=Tt, acs=acs, acsT=acs.T, dt_e=dt_e, acs_e=acs_e, lam=jnp.exp(acs_e),
                w=jnp.exp(alast_e - acs_e), decay=jnp.exp(alast_e), xt=xs * dt_e, CB=_dot_nt(C, B))


def _head_decay(q, d, hh):
    col = q["acs"][:, d * HPG + hh:d * HPG + hh + 1]
    row = q["acsT"][d * HPG + hh:d * HPG + hh + 1, :]
    return jnp.exp(jnp.where(q["mask"], col - row, -jnp.inf))


def _chunk_maps(NX, NS):
    cf = lambda s: lax.rem(s + NX, NS)
    cb = lambda s: NS - 1 - s
    return cf, cb


def _ssd_fwd(xbc, dt_loc, a_loc, L):
    _, R, _ = xbc.shape
    NX, NS = L // Q, R // Q
    cf, cb = _chunk_maps(NX, NS)

    def body(xs_f, b_f, c_f, dt_f, xs_b, b_b, c_b, dt_b, a_ref, y_f, hs_f, y_b, hs_b, hT):
        @pl.when(pl.program_id(2) == 0)
        def _():
            hT[...] = jnp.zeros_like(hT)

        lane = lax.broadcasted_iota(jnp.int32, (Q, 128), 1)
        for d, (xs_ref, b_ref, c_ref, dt_ref, y_ref, hs_ref) in enumerate(
                ((xs_f, b_f, c_f, dt_f, y_f, hs_f), (xs_b, b_b, c_b, dt_b, y_b, hs_b))):
            for gi in range(GPS):
                cols = slice(gi * GWID, (gi + 1) * GWID)
                xs = xs_ref[0, :, cols].astype(F32)
                B, C = b_ref[0, :, gi * NST:(gi + 1) * NST], c_ref[0, :, gi * NST:(gi + 1) * NST]
                q = _ssd_chunk(d, dt_ref[0, gi], a_ref[gi, 0:1, :], xs, B, C)
                h = hT[d, :, cols]
                hb = h.astype(BF)
                hs_ref[0, 0, :, cols] = hb
                parts = []
                for pr in range(HPG // 2):
                    xp = q["xt"][:, pr * 128:(pr + 1) * 128]
                    xst = jnp.concatenate([jnp.where(lane < HEAD, xp, 0.0), jnp.where(lane < HEAD, 0.0, xp)], axis=0)
                    mst = jnp.concatenate([(q["CB"] * _head_decay(q, d, 2 * pr)).astype(BF),
                                           (q["CB"] * _head_decay(q, d, 2 * pr + 1)).astype(BF)], axis=1)
                    parts.append(_dot(mst, xst))
                y_ref[0, :, cols] = jnp.concatenate(parts, axis=1) + _dot(C, hb) * q["lam"]
                hT[d, :, cols] = q["decay"] * h + _dot_tn(B, q["xt"] * q["w"])

    def spec(shape, imap):
        return pl.BlockSpec(shape, imap)

    bc0 = DIN // (GPS * NST)

    def ins(c):
        return [spec((1, Q, GPS * GWID), lambda e, g, s: (e, c(s), g)),
                spec((1, Q, GPS * NST), lambda e, g, s: (e, c(s), bc0 + g)),
                spec((1, Q, GPS * NST), lambda e, g, s: (e, c(s), bc0 + NG // GPS + g)),
                spec((1, GPS, Q, 128), lambda e, g, s: (e, g, c(s), 0))]

    def outs(c):
        return [spec((1, Q, GPS * GWID), lambda e, g, s: (e, c(s), g)),
                spec((1, 1, NST, GPS * GWID), lambda e, g, s: (e, c(s), 0, g))]

    yshape = jax.ShapeDtypeStruct((2, R, DIN), F32)
    hshape = jax.ShapeDtypeStruct((2, NS, NST, DIN), BF)
    return pl.pallas_call(
        body, name="ssd_fwd", grid=(2, NG // GPS, NS),
        in_specs=ins(cf) + ins(cb) + [spec((GPS, 8, 128), lambda e, g, s: (g, 0, 0))],
        out_specs=tuple(outs(cf) + outs(cb)),
        out_shape=(yshape, hshape, yshape, hshape),
        scratch_shapes=[pltpu.VMEM((2, NST, GPS * GWID), F32)],
        compiler_params=_params(("parallel", "parallel", "arbitrary")),
    )(xbc, xbc, xbc, dt_loc, xbc, xbc, xbc, dt_loc, a_loc)


def _ssd_bwd(xbc, dt_loc, a_loc, hs_f, hs_b, y_f, y_b, dy, L):
    _, R, _ = xbc.shape
    NX, NS = L // Q, R // Q
    cf0, cb0 = _chunk_maps(NX, NS)
    cf = lambda sp: cf0(NS - 1 - sp)
    cb = lambda sp: cb0(NS - 1 - sp)

    def body(xs_f, b_f, c_f, dt_f, hs_f_, dy_f, y_f_, xs_b, b_b, c_b, dt_b, hs_b_, dy_b, y_b_, a_ref,
             dxs_f, dbc_f, ddt_f, dxs_b, dbc_b, ddt_b, da_ref, dhT):
        @pl.when(pl.program_id(2) == 0)
        def _():
            dhT[...] = jnp.zeros_like(dhT)
            da_ref[...] = jnp.zeros_like(da_ref)

        lane = lax.broadcasted_iota(jnp.int32, (Q, 128), 1)
        row = lax.broadcasted_iota(jnp.int32, (Q, 128), 0)

        def one_chain(d, gi, xs_ref, b_ref, c_ref, dt_ref, hs_ref, dy_ref, y_ref, dxs_ref, dbc_ref, ddt_ref):
            cols = slice(gi * GWID, (gi + 1) * GWID)
            A = a_ref[gi, 0:1, :]
            xs, dt = xs_ref[0, :, cols].astype(F32), dt_ref[0, gi]
            B, C = b_ref[0, :, gi * NST:(gi + 1) * NST], c_ref[0, :, gi * NST:(gi + 1) * NST]
            q = _ssd_chunk(d, dt, A, xs, B, C)
            xt, lam, w, decay = q["xt"], q["lam"], q["w"], q["decay"]
            H = hs_ref[0, 0, :, cols]
            dyv = dy_ref[0, :, cols].astype(F32)
            dh = dhT[d, :, cols]
            dZ = dyv * lam
            dC = _dot_nt(dZ, H)
            dH = _dot_tn(C, dZ)
            U = _dot(B, dh)
            xw = xt * w
            dxt = U * w
            dalast_e = (jnp.sum(U * xw, axis=0, keepdims=True)
                        + decay * jnp.sum(dh * H.astype(F32), axis=0, keepdims=True))
            dB = _dot_nt(xw, dh)
            dCB = jnp.zeros((Q, Q), F32)
            dxt_parts = []
            for pr in range(HPG // 2):
                xp = xt[:, pr * 128:(pr + 1) * 128]
                dyp = dyv[:, pr * 128:(pr + 1) * 128]
                L0, L1 = _head_decay(q, d, 2 * pr), _head_decay(q, d, 2 * pr + 1)
                dyst = jnp.concatenate([jnp.where(lane < HEAD, dyp, 0.0), jnp.where(lane < HEAD, 0.0, dyp)], axis=0)
                mst = jnp.concatenate([(q["CB"] * L0).astype(BF), (q["CB"] * L1).astype(BF)], axis=0)
                dxt_parts.append(_dot_tn(mst, dyst))
                dmst = _dot_nt(dyst, xp)
                dCB = dCB + dmst[:Q] * L0 + dmst[Q:] * L1
            dxt_diag = jnp.concatenate(dxt_parts, axis=1)
            dC = dC + _dot(dCB, B)
            dB = dB + _dot_tn(dCB, C)
            Rm = _reduce_mat(d)
            dacs = _dot_sl(dyv * y_ref[0, :, cols] - xt.astype(BF).astype(F32) * dxt_diag - U * xw, Rm, 2)
            dxt = dxt + dxt_diag
            dal = _dot_sl(jnp.broadcast_to(dalast_e, (8, GWID)), Rm, 2)[0:1, :]
            dacs = dacs + jnp.where(row == (Q - 1 if d == 0 else 0), dal, 0.0)
            da = _dot_sr(q["Tt"], dacs, 2)
            ddt_ref[0, gi] = da * A + _dot_sl(dxt * xs, Rm, 2)
            da_ref[0, gi, 0:1, :] += jnp.sum(da * dt, axis=0, keepdims=True)
            dxs_ref[0, :, cols] = (dxt * q["dt_e"]).astype(BF)
            dbc_ref[0, :, gi * 2 * NST:(gi + 1) * 2 * NST] = jnp.concatenate([dB, dC], axis=1).astype(BF)
            dhT[d, :, cols] = decay * dh + dH

        for gi in range(GPS):
            one_chain(0, gi, xs_f, b_f, c_f, dt_f, hs_f_, dy_f, y_f_, dxs_f, dbc_f, ddt_f)
            one_chain(1, gi, xs_b, b_b, c_b, dt_b, hs_b_, dy_b, y_b_, dxs_b, dbc_b, ddt_b)

    def spec(shape, imap):
        return pl.BlockSpec(shape, imap)

    bc0 = DIN // (GPS * NST)

    def ins(c):
        return [spec((1, Q, GPS * GWID), lambda e, g, s: (e, c(s), g)),
                spec((1, Q, GPS * NST), lambda e, g, s: (e, c(s), bc0 + g)),
                spec((1, Q, GPS * NST), lambda e, g, s: (e, c(s), bc0 + NG // GPS + g)),
                spec((1, GPS, Q, 128), lambda e, g, s: (e, g, c(s), 0)),
                spec((1, 1, NST, GPS * GWID), lambda e, g, s: (e, c(s), 0, g)),
                spec((1, Q, GPS * GWID), lambda e, g, s: (e, c(s), g)),
                spec((1, Q, GPS * GWID), lambda e, g, s: (e, c(s), g))]

    def outs(c):
        return [spec((1, Q, GPS * GWID), lambda e, g, s: (e, c(s), g)),
                spec((1, Q, GPS * 2 * NST), lambda e, g, s: (e, c(s), g)),
                spec((1, GPS, Q, 128), lambda e, g, s: (e, g, c(s), 0))]

    s_xs = jax.ShapeDtypeStruct((2, R, DIN), BF)
    s_bc = jax.ShapeDtypeStruct((2, R, 2 * NG * NST), BF)
    s_dt = jax.ShapeDtypeStruct((2, NG, R, 128), F32)
    return pl.pallas_call(
        body, name="ssd_bwd", grid=(2, NG // GPS, NS),
        in_specs=ins(cf) + ins(cb) + [spec((GPS, 8, 128), lambda e, g, s: (g, 0, 0))],
        out_specs=tuple(outs(cf) + outs(cb) + [spec((1, GPS, 8, 128), lambda e, g, s: (e, g, 0, 0))]),
        out_shape=(s_xs, s_bc, s_dt, s_xs, s_bc, s_dt, jax.ShapeDtypeStruct((2, NG, 8, 128), F32)),
        scratch_shapes=[pltpu.VMEM((2, NST, GPS * GWID), F32)],
        compiler_params=_params(("parallel", "parallel", "arbitrary")),
    )(xbc, xbc, xbc, dt_loc, hs_f, dy, y_f, xbc, xbc, xbc, dt_loc, hs_b, dy, y_b, a_loc)


def _ssd_post_fwd(y_f, y_b, xbc, proj3, dskip_e, ssd_norm, L):
    def body(yf_ref, yb_ref, xs_ref, z_ref, ds_ref, w_ref, o_ref):
        y2 = yf_ref[0] + yb_ref[0] + ds_ref[...] * xs_ref[0].astype(F32)
        z = z_ref[0].astype(F32)
        u = y2 * (z * _sigmoid(z))
        parts = []
        for g in range(NG):
            ug = u[:, g * GWID:(g + 1) * GWID]
            parts.append(ug * lax.rsqrt(jnp.mean(ug * ug, axis=-1, keepdims=True) + EPS))
        o_ref[0] = (jnp.concatenate(parts, axis=1) * w_ref[...]).astype(BF)

    blk = lambda c: pl.BlockSpec((1, ROW_TILE, DIN), lambda e, t: (e, t, c))
    vec = pl.BlockSpec((1, DIN), lambda e, t: (0, 0))
    return pl.pallas_call(
        body, name="ssd_post_fwd", grid=(2, L // ROW_TILE),
        in_specs=[blk(0), blk(0), blk(0), blk(1), vec, vec],
        out_specs=blk(0),
        out_shape=jax.ShapeDtypeStruct((2, L, DIN), BF),
        compiler_params=_params(("parallel", "parallel")),
    )(y_f, y_b, xbc, proj3, dskip_e, ssd_norm)


def _ssd_post_bwd(d_yn, y_f, y_b, xbc, proj3, dskip_e, ssd_norm, L):
    _, R, _ = y_f.shape
    nx = L // ROW_TILE

    def body(dyn_ref, yf_ref, yb_ref, xs_ref, z_ref, ds_ref, w_ref, dy_ref, dz_ref, acc_ref):
        t = pl.program_id(1)

        @pl.when(t == 0)
        def _():
            acc_ref[...] = jnp.zeros_like(acc_ref)

        @pl.when(t >= nx)
        def _():
            dy_ref[...] = jnp.zeros_like(dy_ref)
            dz_ref[...] = jnp.zeros_like(dz_ref)

        @pl.when(t < nx)
        def _():
            xs = xs_ref[0].astype(F32)
            y2 = yf_ref[0] + yb_ref[0] + ds_ref[...] * xs
            z = z_ref[0].astype(F32)
            sg = _sigmoid(z)
            sz = z * sg
            u = y2 * sz
            dyn = dyn_ref[0].astype(F32)
            dun = dyn * w_ref[...]
            uh_parts, du_parts = [], []
            for g in range(NG):
                sl = slice(g * GWID, (g + 1) * GWID)
                ug = u[:, sl]
                rg = lax.rsqrt(jnp.mean(ug * ug, axis=-1, keepdims=True) + EPS)
                uh = ug * rg
                dg = dun[:, sl]
                du_parts.append(rg * (dg - uh * jnp.mean(dg * uh, axis=-1, keepdims=True)))
                uh_parts.append(uh)
            du = jnp.concatenate(du_parts, axis=1)
            uh = jnp.concatenate(uh_parts, axis=1)
            dy2 = du * sz
            dy_ref[0] = dy2.astype(BF)
            dz_ref[0] = (du * y2 * (sg * (1.0 + z * (1.0 - sg)))).astype(BF)
            acc_ref[0, 0:1, :] += jnp.sum(dyn * uh, axis=0, keepdims=True)
            acc_ref[0, 1:2, :] += jnp.sum(dy2 * xs, axis=0, keepdims=True)

    xmap = lambda c: (lambda e, t: (e, jnp.minimum(t, nx - 1), c))
    blk = lambda c: pl.BlockSpec((1, ROW_TILE, DIN), xmap(c))
    oblk = pl.BlockSpec((1, ROW_TILE, DIN), lambda e, t: (e, t, 0))
    vec = pl.BlockSpec((1, DIN), lambda e, t: (0, 0))
    return pl.pallas_call(
        body, name="ssd_post_bwd", grid=(2, R // ROW_TILE),
        in_specs=[blk(0), blk(0), blk(0), blk(0), blk(1), vec, vec],
        out_specs=(oblk, oblk, pl.BlockSpec((1, 8, DIN), lambda e, t: (e, 0, 0))),
        out_shape=(jax.ShapeDtypeStruct((2, R, DIN), BF), jax.ShapeDtypeStruct((2, R, DIN), BF),
                   jax.ShapeDtypeStruct((2, 8, DIN), F32)),
        compiler_params=_params(("parallel", "arbitrary")),
    )(d_yn, y_f, y_b, xbc, proj3, dskip_e, ssd_norm)


def _merge_fwd(proj3, P, S, b_merge, L):
    def body(gp_ref, p_ref, s_ref, b_ref, o_ref):
        gt = _sigmoid(gp_ref[0].astype(F32) + b_ref[...])
        o_ref[0] = (gt[:, :D] * p_ref[0].astype(F32) + gt[:, D:] * s_ref[0].astype(F32)).astype(BF)

    blk = pl.BlockSpec((1, ROW_TILE, D), lambda e, t: (e, t, 0))
    return pl.pallas_call(
        body, name="merge_fwd", grid=(2, L // ROW_TILE),
        in_specs=[pl.BlockSpec((1, ROW_TILE, 2 * D), lambda e, t: (e, t, OFF_GATE // (2 * D))), blk, blk,
                  pl.BlockSpec((1, 2 * D), lambda e, t: (0, 0))],
        out_specs=blk, out_shape=jax.ShapeDtypeStruct((2, L, D), BF),
        compiler_params=_params(("parallel", "parallel")),
    )(proj3, P, S, b_merge)


def _merge_bwd(d_merged, proj3, P, S, b_merge, L):
    _, R, _ = proj3.shape
    nx = L // ROW_TILE

    def body(dm_ref, gp_ref, p_ref, s_ref, b_ref, dp_ref, ds_ref, dg_ref, acc_ref):
        t = pl.program_id(1)

        @pl.when(t == 0)
        def _():
            acc_ref[...] = jnp.zeros_like(acc_ref)

        @pl.when(t >= nx)
        def _():
            dg_ref[...] = jnp.zeros_like(dg_ref)

        @pl.when(t < nx)
        def _():
            gt = _sigmoid(gp_ref[0].astype(F32) + b_ref[...])
            dm = dm_ref[0].astype(F32)
            g1, g2 = gt[:, :D], gt[:, D:]
            dp_ref[0] = (dm * g1).astype(BF)
            ds_ref[0] = (dm * g2).astype(BF)
            dgp = jnp.concatenate([dm * p_ref[0].astype(F32) * g1 * (1.0 - g1),
                                   dm * s_ref[0].astype(F32) * g2 * (1.0 - g2)], axis=1)
            dg_ref[0] = dgp.astype(BF)
            acc_ref[0, 0:1, :] += jnp.sum(dgp, axis=0, keepdims=True)

    xmap = lambda e, t: (e, jnp.minimum(t, nx - 1), 0)
    blk = pl.BlockSpec((1, ROW_TILE, D), xmap)
    return pl.pallas_call(
        body, name="merge_bwd", grid=(2, R // ROW_TILE),
        in_specs=[blk, pl.BlockSpec((1, ROW_TILE, 2 * D), lambda e, t: (e, jnp.minimum(t, nx - 1), OFF_GATE // (2 * D))),
                  blk, blk, pl.BlockSpec((1, 2 * D), lambda e, t: (0, 0))],
        out_specs=(blk, blk, pl.BlockSpec((1, ROW_TILE, 2 * D), lambda e, t: (e, t, 0)),
                   pl.BlockSpec((1, 8, 2 * D), lambda e, t: (e, 0, 0))),
        out_shape=(jax.ShapeDtypeStruct((2, L, D), BF), jax.ShapeDtypeStruct((2, L, D), BF),
                   jax.ShapeDtypeStruct((2, R, 2 * D), BF), jax.ShapeDtypeStruct((2, 8, 2 * D), F32)),
        compiler_params=_params(("parallel", "arbitrary")),
    )(d_merged, proj3, P, S, b_merge)


def _final(out3, x, tgt, gtab, norm_post, L):
    def body(o_ref, x_ref, t_ref, g_ref, n_ref, dxo_ref, do_ref, acc_ref):
        @pl.when(pl.program_id(1) == 0)
        def _():
            acc_ref[...] = jnp.zeros_like(acc_ref)

        o = o_ref[0].astype(F32)
        gate = g_ref[0, 0:1, :]
        npost = n_ref[...]
        r2 = lax.rsqrt(jnp.mean(o * o, axis=-1, keepdims=True) + EPS)
        nh = o * r2
        on = nh * npost
        err = x_ref[0] + gate * on - t_ref[0]
        dxo = err * (1.0 / D)
        dxo_ref[0] = dxo
        dnh = dxo * gate * npost
        do_ref[0] = (r2 * (dnh - nh * jnp.mean(dnh * nh, axis=-1, keepdims=True))).astype(BF)
        acc_ref[0, 0:1, :] += jnp.sum(dxo * on, axis=0, keepdims=True)
        acc_ref[0, 1:2, :] += jnp.sum(dxo * gate * nh, axis=0, keepdims=True)
        acc_ref[0, 2:3, :] += jnp.sum(err * err, axis=0, keepdims=True)

    blk = pl.BlockSpec((1, ROW_TILE, D), lambda e, t: (e, t, 0))
    return pl.pallas_call(
        body, name="final", grid=(2, L // ROW_TILE),
        in_specs=[blk, blk, blk, pl.BlockSpec((1, 8, D), lambda e, t: (e, 0, 0)),
                  pl.BlockSpec((1, D), lambda e, t: (0, 0))],
        out_specs=(blk, blk, pl.BlockSpec((1, 8, D), lambda e, t: (e, 0, 0))),
        out_shape=(jax.ShapeDtypeStruct((2, L, D), F32), jax.ShapeDtypeStruct((2, L, D), BF),
                   jax.ShapeDtypeStruct((2, 8, D), F32)),
        compiler_params=_params(("parallel", "arbitrary")),
    )(out3, x, tgt, gtab, norm_post)


def _local_step(x, c, ctx, loss_target, W, late_shard=None, exchange=False):
    nb, L, _ = x.shape
    LC = ctx.shape[1]
    R = L + LC
    assert nb == 2 and L % ROW_TILE == 0 and LC % Q == 0 and L % POOL_TILE == 0
    w_inT = W["w_in"]
    w_dtT = jnp.pad(w_inT[OFF_DT:], ((0, 64), (0, 0)))
    tables = _pool_tables(L)
    tr, tl = (2 * R) // 8, (2 * L) // 8

    c16 = jnp.zeros((16, D), F32).at[0:2].set(c).at[2].set(W["c_ctx"])
    mod16 = _adaln_fwd(c16, W["w_ada"], W["b_ada"])
    shift, scale, gate = mod16[:, :D], mod16[:, D:2 * D], mod16[:, 2 * D:]
    npre = W["norm_pre"]
    tab = jnp.zeros((2, 2, 8, D), F32)
    for e in range(2):
        tab = tab.at[e, 0, 0].set(npre[0] * (1.0 + scale[e])).at[e, 0, 1].set(shift[e])
        tab = tab.at[e, 1, 0].set(npre[0] * (1.0 + scale[2])).at[e, 1, 1].set(shift[2])
    gtab = jnp.zeros((2, 8, D), F32).at[:, 0].set(gate[0:2])

    hx = _norm_mod_fwd(x, ctx, tab)
    hx2 = hx.reshape(2 * R, D)
    if late_shard is None:
        proj = _matmul(hx2, w_inT, BF, "proj_main", tm=tr, tn=1024, bt=True, n=OFF_DT)
    else:
        proj, late = _matmul(hx2, w_inT, BF, "proj_main", tm=tr, tn=1024, bt=True, n=OFF_DT, side=_gather_side(late_shard))
        W = {**W, **_unpack_gather(late, GATHER_LATE)}
    proj3 = proj.reshape(2, R, OFF_DT)
    dt_raw = _matmul(hx2, w_dtT, F32, "proj_dt", tm=tr, bt=True).reshape(2, R, 128)
    ypool = _pool_fwd(proj3, W["pool_w"], W["pool_scale"], tables, L)
    xbc = _conv_fwd(proj3, W["conv_w"], W["conv_b"], L)
    bias128 = jnp.pad(W["dt_bias"].reshape(1, 64), ((0, 0), (0, 64)))
    dt_loc = _dt_fwd(dt_raw, bias128)
    A = -jnp.exp(W["a_log"].reshape(2, NG, HPG))
    a_loc = jnp.zeros((NG, 8, 128), F32).at[:, 0, :16].set(A.transpose(1, 0, 2).reshape(NG, 16))
    y_f, hs_f, y_b, hs_b = _ssd_fwd(xbc, dt_loc, a_loc, L)
    dskip_e = jnp.repeat(W["d_skip"].reshape(1, 32), HEAD, axis=1)
    yn = _ssd_post_fwd(y_f, y_b, xbc, proj3, dskip_e, W["ssd_norm"], L)
    ypool2, yn2 = ypool.reshape(2 * L, D), yn.reshape(2 * L, DIN)
    P = _matmul(ypool2, W["w_proj_pool"], BF, "proj_pool", tm=tl, tn=1024).reshape(2, L, D)
    S = _matmul(yn2, W["w_proj_ssd"], BF, "proj_ssd", tm=tl, tn=1024).reshape(2, L, D)
    merged = _merge_fwd(proj3, P, S, W["b_merge"], L)
    merged2 = merged.reshape(2 * L, D)
    out3 = _matmul(merged2, W["w_out"], BF, "proj_out", tm=tl, tn=1024).reshape(2, L, D)
    dxo, dout, acc_f = _final(out3, x, loss_target, gtab, W["norm_post"], L)

    dout2 = dout.reshape(2 * L, D)
    g = {}
    g["w_out"] = _matmul_tn(merged2, dout2, "dw_out", ta=1024, tn=1024, tr=2 * tl)
    d_merged = _matmul(dout2, W["w_out"], BF, "d_merged", tm=tl, tn=1024, bt=True).reshape(2, L, D)
    dP, dS, dgp, acc_m = _merge_bwd(d_merged, proj3, P, S, W["b_merge"], L)
    dP2, dS2 = dP.reshape(2 * L, D), dS.reshape(2 * L, D)
    g["w_proj_pool"] = _matmul_tn(ypool2, dP2, "dw_proj_pool", ta=1024, tn=1024, tr=2 * tl)
    g["w_proj_ssd"] = _matmul_tn(yn2, dS2, "dw_proj_ssd", ta=1024, tn=1024, tr=2 * tl)
    d_ypool = _matmul(dP2, W["w_proj_pool"], BF, "d_ypool", tm=tl, tn=1024, bt=True).reshape(2, L, D)
    d_yn = _matmul(dS2, W["w_proj_ssd"], BF, "d_yn", tm=tl, tn=1024, bt=True).reshape(2, L, DIN)
    dv, dzp, g["pool_w"], acc_p = _pool_bwd(proj3, d_ypool, W["pool_w"], jnp.swapaxes(W["pool_w"], 1, 2),
                                            W["pool_scale"], tables, L)
    dy2, dzs, acc_s = _ssd_post_bwd(d_yn, y_f, y_b, xbc, proj3, dskip_e, W["ssd_norm"], L)
    dxs_f, dbc_f, ddt_f, dxs_b, dbc_b, ddt_b, acc_a = _ssd_bwd(xbc, dt_loc, a_loc, hs_f, hs_b, y_f, y_b, dy2, L)
    ident = lambda j: j
    dxr_xs, acc_cx = _conv_bwd(proj3, [dxs_f, dxs_b, dy2], [None, None, dskip_e], 0, DIN, [ident, ident, ident],
                               W["conv_w"], W["conv_b"], L, "conv_bwd_xs")
    bcmap = lambda j: 2 * lax.rem(j, NG) + j // NG
    dxr_bc, acc_cb = _conv_bwd(proj3, [dbc_f, dbc_b], [None, None], DIN, 2 * NG * NST, [bcmap, bcmap],
                               W["conv_w"], W["conv_b"], L, "conv_bwd_bc")
    ddtr, acc_d = _dt_bwd(dt_raw, bias128, ddt_f, ddt_b)
    pieces = [dv, dzp, dzs, dgp, dxr_xs, dxr_bc]
    dw_rows = [_matmul_tn(p.reshape(2 * R, p.shape[2]), hx2, "dw_in_%d" % i, ta=1024, tn=1024, tr=2 * tr)
               for i, p in enumerate(pieces)]
    dw_rows.append(_matmul_tn(ddtr.reshape(2 * R, 128), hx2, "dw_in_dt", ta=128, tn=1024, tr=tr)[:64])
    g["w_in"] = jnp.concatenate(dw_rows, axis=0)
    acc_c = jnp.concatenate([acc_cx[0] + acc_cx[1], acc_cb[0] + acc_cb[1]], axis=1)
    g["conv_w"] = acc_c[0:4]
    g["conv_b"] = acc_c[4:5]
    if exchange:
        gb = _pack_grads(g, GRADS_EARLY)
        pair = _pair_add(gb, _pair_exchange(gb, None, "grads_pair_exchange_early"), "grads_pair_add_early")
        dh, recv_early = _dhx(pieces, ddtr, w_inT, w_dtT, side=_chip_exchange_side(pair))
    else:
        dh, recv_early = _dhx(pieces, ddtr, w_inT, w_dtT), None
    grad_x, acc_n = _norm_mod_bwd(dh, x, ctx, tab, dxo)
    g["w_ada"], db_rows, sm_rows = _adaln_bwd(acc_n, acc_f, mod16, c16, npre, W["w_ada"])

    g["b_ada"] = db_rows[0:1]
    g["norm_pre"] = sm_rows[0:1]
    g["c_ctx"] = sm_rows[1]
    g["norm_post"] = acc_f[0, 1:2] + acc_f[1, 1:2]
    g["b_merge"] = acc_m[0, 0:1] + acc_m[1, 0:1]
    g["pool_scale"] = acc_p[:, 0, :].reshape(1, D)
    g["dt_bias"] = (acc_d[0, 0, :64] + acc_d[1, 0, :64]).reshape(2, 32)
    dA = (acc_a[0, :, 0, :16] + acc_a[1, :, 0, :16]).reshape(NG, 2, HPG).transpose(1, 0, 2)
    g["a_log"] = (dA * A).reshape(2, 32)
    g["d_skip"] = (acc_s[0, 1] + acc_s[1, 1]).reshape(32, HEAD).sum(axis=1).reshape(1, 32)
    g["ssd_norm"] = acc_s[0, 0:1] + acc_s[1, 0:1]
    loss_lanes = acc_f[:, 2, :]
    return loss_lanes, grad_x, g, recv_early


MESH = pl.DeviceIdType.MESH
ANY = pl.BlockSpec(memory_space=pl.ANY)


def _all_gather(shard):
    m_per, n = shard.shape

    def body(x_ref, out_ref, send_sems, recv_sems, local_sem):
        x, y, c = lax.axis_index("x"), lax.axis_index("y"), lax.axis_index("c")
        me, sibling = (x, y, c), (x, y, 1 - c)
        chips = [(1 - x, y), (x, 1 - y), (1 - x, 1 - y)]

        def rows(px, py, pc):
            return out_ref.at[pl.ds((4 * px + 2 * py + pc) * m_per, m_per), :]

        def copy(k, block, to, src=None):
            return pltpu.make_async_remote_copy(
                src_ref=rows(*block) if src is None else src, dst_ref=rows(*block),
                send_sem=send_sems.at[k], recv_sem=recv_sems.at[k], device_id=to, device_id_type=MESH)

        mine = pltpu.make_async_copy(x_ref, rows(*me), local_sem)
        mine.start()
        first = [copy(0, me, sibling, src=x_ref)]
        first += [copy(1 + j, me, (*chip, c), src=x_ref) for j, chip in enumerate(chips)]
        for cp in first:
            cp.start()
        passed = [copy(4 + j, (*chip, c), sibling) for j, chip in enumerate(chips)]
        for j, chip in enumerate(chips):
            copy(1 + j, (*chip, c), me).wait_recv()
            passed[j].start()
        copy(0, sibling, me).wait_recv()
        for j, chip in enumerate(chips):
            copy(4 + j, (*chip, 1 - c), me).wait_recv()
        for cp in first + passed:
            cp.wait_send()
        mine.wait()

    return pl.pallas_call(
        body, name="all_gather_weights",
        out_shape=jax.ShapeDtypeStruct((NDEV * m_per, n), shard.dtype),
        in_specs=[ANY], out_specs=ANY,
        scratch_shapes=[pltpu.SemaphoreType.DMA((7,)), pltpu.SemaphoreType.DMA((7,)), pltpu.SemaphoreType.DMA],
    )(shard)


PAIR_PIECES = 4


def _xor_peer(k, x, y, c):
    return (1 - x if k & 4 else x, 1 - y if k & 2 else y, 1 - c if k & 1 else c)


def _pair_exchange(big, small, name):
    _, nq, rows, n = big.shape
    piece = rows // PAIR_PIECES
    assert piece * PAIR_PIECES == rows and piece % 16 == 0
    with_small = small is not None

    def body(*refs):
        if with_small:
            big_ref, small_ref, got_ref, osmall_ref, send_sems, recv_sems, local_sem = refs
        else:
            big_ref, got_ref, send_sems, recv_sems, local_sem = refs
        x, y, c = lax.axis_index("x"), lax.axis_index("y"), lax.axis_index("c")
        me = 4 * x + 2 * y + c

        def rc(src, dst, sem, peer):
            return pltpu.make_async_remote_copy(src_ref=src, dst_ref=dst, send_sem=send_sems.at[sem],
                                                recv_sem=recv_sems.at[sem], device_id=peer, device_id_type=MESH)

        sib = _xor_peer(1, x, y, c)
        local, sends, recvs = [], [], []
        for q in range(nq):
            for h in range(PAIR_PIECES):
                rws = pl.ds(h * piece, piece)
                cp = rc(big_ref.at[1 - c, q, rws], got_ref.at[q, rws], 8 + q * PAIR_PIECES + h, sib)
                sends.append(cp)
                recvs.append(cp)
        if with_small:
            local.append(pltpu.make_async_copy(small_ref, osmall_ref.at[me], local_sem))
            for k in range(1, NDEV):
                px, py, pc = _xor_peer(k, x, y, c)
                sends.append(rc(small_ref, osmall_ref.at[me], k, (px, py, pc)))
                recvs.append(rc(small_ref, osmall_ref.at[4 * px + 2 * py + pc], k, (px, py, pc)))
        for cp in local + sends:
            cp.start()
        for cp in sends:
            cp.wait_send()
        for cp in recvs:
            cp.wait_recv()
        for cp in local:
            cp.wait()

    nsem = 8 + nq * PAIR_PIECES
    out_shape = [jax.ShapeDtypeStruct(big.shape[1:], big.dtype)]
    if with_small:
        out_shape.append(jax.ShapeDtypeStruct((NDEV,) + small.shape, small.dtype))
    out = pl.pallas_call(
        body, name=name, out_shape=tuple(out_shape),
        in_specs=[ANY] * (1 + with_small), out_specs=(ANY,) * (1 + with_small),
        scratch_shapes=[pltpu.SemaphoreType.DMA((nsem,)), pltpu.SemaphoreType.DMA((nsem,)), pltpu.SemaphoreType.DMA],
    )(*((big, small) if with_small else (big,)))
    return out if with_small else out[0]


def _pair_add(big, got, name):
    _, nq, rows, n = big.shape
    tile = rows // 4
    assert rows % 64 == 0

    def body(c_ref, a_ref, b_ref, o_ref):
        o_ref[0] = (a_ref[0, 0].astype(F32) + b_ref[0].astype(F32)).astype(BF)

    blk = pl.BlockSpec((1, tile, n), lambda q, i, c_ref: (q, i, 0))
    return pl.pallas_call(
        body, name=name,
        grid_spec=pltpu.PrefetchScalarGridSpec(
            num_scalar_prefetch=1, grid=(nq, rows // tile),
            in_specs=[pl.BlockSpec((1, 1, tile, n), lambda q, i, c_ref: (c_ref[0], q, i, 0)), blk], out_specs=blk),
        out_shape=jax.ShapeDtypeStruct(got.shape, BF), compiler_params=_params(("parallel", "parallel")),
    )(lax.axis_index("c").astype(jnp.int32).reshape(1), big, got)


def _chip_exchange_side(pair):
    def make(in_refs, out_refs, send_sems, recv_sems, local_sem, arrivals=True):
        (in_ref,), (out_ref,) = in_refs, out_refs
        x, y, c = lax.axis_index("x"), lax.axis_index("y"), lax.axis_index("c")
        q = 2 * x + y
        local = [pltpu.make_async_copy(in_ref.at[q], out_ref.at[q], local_sem)]
        sends, recvs = [], []
        for j in range(1, 4):
            px, py, pc = _xor_peer(2 * j, x, y, c)
            pq = 2 * px + py
            for lst, dst in ((sends, out_ref.at[q]), (recvs, out_ref.at[pq]))[:1 + arrivals]:
                lst.append(pltpu.make_async_remote_copy(
                    src_ref=in_ref.at[pq], dst_ref=dst, send_sem=send_sems.at[j - 1], recv_sem=recv_sems.at[j - 1],
                    device_id=(px, py, pc), device_id_type=MESH))
        return local, sends, recvs

    return _SideCopies([pair], [jax.ShapeDtypeStruct(pair.shape, pair.dtype)], make)


def _gather_side(shard):
    def make(in_refs, out_refs, send_sems, recv_sems, local_sem, arrivals=True):
        (src,), (dst,) = in_refs, out_refs
        x, y, c = lax.axis_index("x"), lax.axis_index("y"), lax.axis_index("c")
        me = 4 * x + 2 * y + c
        local = [pltpu.make_async_copy(src, dst.at[me], local_sem)]
        sends, recvs = [], []
        for k in range(1, NDEV):
            px, py, pc = _xor_peer(k, x, y, c)
            for lst, slot in ((sends, me), (recvs, 4 * px + 2 * py + pc))[:1 + arrivals]:
                lst.append(pltpu.make_async_remote_copy(
                    src_ref=src, dst_ref=dst.at[slot], send_sem=send_sems.at[k - 1], recv_sem=recv_sems.at[k - 1],
                    device_id=(px, py, pc), device_id_type=MESH))
        return local, sends, recvs

    return _SideCopies([shard], [jax.ShapeDtypeStruct((NDEV,) + shard.shape, shard.dtype)], make)


ADAM_TILE = 64
PACK_W = 1024


def _adamw(recv, w, m, v, name, side=None):
    rp = w.shape[0]
    tile = min(ADAM_TILE, rp)
    nsrc = recv.shape[0]
    grid = (rp // tile,)
    n_si, n_so = (len(side.inputs), len(side.out_shapes)) if side else (0, 0)

    def body(*refs):
        r_ref, w_ref, m_ref, v_ref = refs[:4]
        g_ref, d_ref, nm_ref, nv_ref = refs[4 + n_si:8 + n_si]
        side_refs = (refs[4:4 + n_si], refs[8 + n_si:8 + n_si + n_so], refs[8 + n_si + n_so:])
        if side:
            side.start(grid, *side_refs)
        g = r_ref[0].astype(F32)
        for i in range(1, nsrc):
            g = g + r_ref[i].astype(F32)
        m1 = ADAM_B1 * m_ref[...] + (1.0 - ADAM_B1) * g
        v1 = ADAM_B2 * v_ref[...] + (1.0 - ADAM_B2) * (g * g)
        m_hat = m1 / (1.0 - ADAM_B1 ** ADAM_STEP)
        v_hat = v1 / (1.0 - ADAM_B2 ** ADAM_STEP)
        g_ref[...] = g
        d_ref[...] = -ADAM_LR * (m_hat / (jnp.sqrt(v_hat) + ADAM_EPS) + ADAM_WD * w_ref[...])
        nm_ref[...] = m1
        nv_ref[...] = v1
        if side:
            side.wait(grid, *side_refs)

    blk = pl.BlockSpec((tile, PACK_W), lambda i: (i, 0))
    shp = jax.ShapeDtypeStruct((rp, PACK_W), F32)
    return pl.pallas_call(
        body, name=name, grid=grid,
        in_specs=[pl.BlockSpec((nsrc, tile, PACK_W), lambda i: (0, i, 0)), blk, blk, blk] + [ANY] * n_si,
        out_specs=(blk, blk, blk, blk) + (ANY,) * n_so,
        out_shape=(shp, shp, shp, shp) + tuple(side.out_shapes if side else ()),
        scratch_shapes=side.scratch() if side else [],
        compiler_params=_params(("arbitrary",) if side else ("parallel",)),
    )(recv, w, m, v, *(side.inputs if side else ()))


BIG = {"w_ada": ((3 * D, D), 0), "pool_w": ((4, PGW, PGW), 1), "w_proj_pool": ((D, D), 0), "w_proj_ssd": ((DIN, D), 0),
       "w_out": ((D, D), 0), "w_in": ((IN_COLS, D), 0), "conv_w": ((4, CONV_DIM), 1)}
TRANSPOSED = ("w_ada", "w_in")
PACK_ROWS = {"w_ada": 384, "w_in": 1168, "conv_w": 16, "pool_w": 32, "w_proj_pool": 128, "w_proj_ssd": 256, "w_out": 128}
GATHER_EARLY = ("w_ada", "w_in", "conv_w")
GATHER_LATE = ("pool_w", "w_proj_pool", "w_proj_ssd", "w_out")
GRADS_LATE = ("w_ada",)
GRADS_EARLY = tuple(n for n in PACK_ROWS if n not in GRADS_LATE)
SMALL = {"c_ctx": (D,), "b_ada": (1, 3 * D), "norm_pre": (1, D), "norm_post": (1, D), "b_merge": (1, 2 * D),
         "pool_scale": (1, D), "conv_b": (1, CONV_DIM), "dt_bias": (2, 32), "a_log": (2, 32), "d_skip": (1, 32),
         "ssd_norm": (1, DIN)}
LOSS_SLOT = 128
assert all(_r % 16 == 0 for _r in PACK_ROWS.values())
SMALL_ROWS = 16


def _shard_shape(name):
    shape, ax = BIG[name]
    return tuple(s // NDEV if i == ax else s for i, s in enumerate(shape))


def _as_rows(t, rows):
    pad = [(0, 0)] * (t.ndim - 1) + [(0, rows * PACK_W - t.shape[-1])]
    return jnp.pad(t, pad).reshape(t.shape[:-1] + (rows, PACK_W))


def _shard_rows(t, name):
    sh, r = _shard_shape(name), PACK_ROWS[name]
    lead = t.shape[:t.ndim - len(sh)]
    if len(sh) == 2 and sh[1] == PACK_W:
        return jnp.pad(t, [(0, 0)] * len(lead) + [(0, r - sh[0]), (0, 0)])
    if int(np.prod(sh)) == r * PACK_W:
        return t.reshape(lead + (r, PACK_W))
    return _as_rows(t.reshape(lead + (-1,)), r)


def _to_chunks(full, name):
    shape, ax = BIG[name]
    split = shape[:ax] + (NDEV, shape[ax] // NDEV) + shape[ax + 1:]
    return _shard_rows(jnp.moveaxis(full.reshape(split), ax, 0), name)


def _from_chunks(chunks, name):
    shape, ax = BIG[name]
    return jnp.moveaxis(chunks.reshape((NDEV,) + _shard_shape(name)), 0, ax).reshape(shape)


def _rows_of(names):
    return sum(PACK_ROWS[n] for n in names)


def _pack_state(t, names):
    return jnp.concatenate([_shard_rows(t[n], n) for n in names], axis=0)


def _pack_small(t, loss_part=None):
    slot = jnp.zeros((LOSS_SLOT,), F32)
    if loss_part is not None:
        slot = slot.at[0].set(loss_part)
    return _as_rows(jnp.concatenate([t[n].reshape(-1) for n in SMALL] + [slot]), SMALL_ROWS)


def _pack_grads(g, names):
    big = jnp.concatenate([_to_chunks(g[n], n).astype(BF) for n in names], axis=1)
    return jnp.swapaxes(big.reshape(4, 2, _rows_of(names), PACK_W), 0, 1)


def _unpack_state(big, names):
    out, off = {}, 0
    for n in names:
        sh, r = _shard_shape(n), PACK_ROWS[n]
        k = int(np.prod(sh))
        if len(sh) == 2 and sh[1] == PACK_W:
            out[n] = big[off:off + sh[0]]
        else:
            out[n] = big[off:off + r].reshape(-1)[:k].reshape(sh)
        off += r
    return out


def _unpack_small(small):
    out, flat, off = {}, small.reshape(-1), 0
    for n, sh in SMALL.items():
        k = int(np.prod(sh))
        out[n] = flat[off:off + k].reshape(sh)
        off += k
    out["loss"] = flat[off]
    return out


def _pack_gather(w, names):
    pieces = []
    for n in names:
        if n == "conv_w":
            pieces.append(_as_rows(jnp.concatenate([p.reshape(-1) for p in _split(w[n], 3)]), PACK_ROWS[n]))
        else:
            pieces.append(_shard_rows(w[n], n).astype(BF))
    return jnp.concatenate(pieces, axis=0)


def _unpack_gather(gathered, names):
    g = gathered.reshape(NDEV, _rows_of(names), PACK_W)
    out, off = {}, 0
    for n in names:
        r = PACK_ROWS[n]
        sh = _shard_shape(n)
        if n == "conv_w":
            k = int(np.prod(sh))
            terms = g[:, off:off + r].reshape(NDEV, -1)[:, :3 * k].astype(F32).reshape(NDEV, 3, k)
            out[n] = _from_chunks(terms[:, 0] + terms[:, 1] + terms[:, 2], n)
        elif len(sh) == 2 and sh[1] == PACK_W:
            out[n] = _from_chunks(g[:, off:off + sh[0]], n)
        else:
            out[n] = _from_chunks(g[:, off:off + r], n)
        off += r
    return out


PARAMS = ["c_ctx", "w_ada", "b_ada", "norm_pre", "norm_post", "w_in", "b_merge", "pool_w", "pool_scale", "conv_w", "conv_b",
          "dt_bias", "a_log", "d_skip", "ssd_norm", "w_proj_pool", "w_proj_ssd", "w_out"]


def kernel(x, c, ctx, c_ctx, w_ada, b_ada, norm_pre, norm_post, w_in, b_merge, pool_w, pool_scale, conv_w, conv_b, dt_bias, a_log, d_skip, ssd_norm, w_proj_pool, w_proj_ssd, w_out, loss_target, m_c_ctx, m_w_ada, m_b_ada, m_norm_pre, m_norm_post, m_w_in, m_b_merge, m_pool_w, m_pool_scale, m_conv_w, m_conv_b, m_dt_bias, m_a_log, m_d_skip, m_ssd_norm, m_w_proj_pool, m_w_proj_ssd, m_w_out, v_c_ctx, v_w_ada, v_b_ada, v_norm_pre, v_norm_post, v_w_in, v_b_merge, v_pool_w, v_pool_scale, v_conv_w, v_conv_b, v_dt_bias, v_a_log, v_d_skip, v_ssd_norm, v_w_proj_pool, v_w_proj_ssd, v_w_out):
    given = dict(locals())
    shapes = {n: given[n].shape for n in PARAMS}

    def local(prefix):
        t = {n: (given[prefix + n] if n == "c_ctx" else given[prefix + n][0]) for n in PARAMS}
        for n in TRANSPOSED:
            t[n] = t[n].T
        return {n: t[n].reshape(_shard_shape(n) if n in BIG else SMALL[n]) for n in PARAMS}

    w, m, v = local(""), local("m_"), local("v_")

    W = _unpack_gather(_all_gather(_pack_gather(w, GATHER_EARLY)), GATHER_EARLY)
    for n in SMALL:
        W[n] = w[n]
    lanes, grad_x, g, recv_early = _local_step(x, c, ctx, loss_target, W, late_shard=_pack_gather(w, GATHER_LATE),
                                               exchange=True)
    gb = _pack_grads(g, GRADS_LATE)
    got, recv_small = _pair_exchange(gb, _pack_small(g, (0.5 / D) * jnp.sum(lanes)), "grads_pair_exchange_late")
    late = _chip_exchange_side(_pair_add(gb, got, "grads_pair_add_late"))
    res = [{} for _ in range(4)]
    *early, recv_late = _adamw(recv_early, *[_pack_state(s, GRADS_EARLY) for s in (w, m, v)], "adamw_early", side=late)
    for r, t in zip(res, early):
        r.update(_unpack_state(t, GRADS_EARLY))
    for r, t in zip(res, _adamw(recv_late, *[_pack_state(s, GRADS_LATE) for s in (w, m, v)], "adamw_late")):
        r.update(_unpack_state(t, GRADS_LATE))
    for r, t in zip(res, _adamw(recv_small, *[_pack_small(s) for s in (w, m, v)], "adamw_small")):
        r.update(_unpack_small(t))
    outs = [res[0]["loss"], grad_x]
    for r in res:
        for n in TRANSPOSED:
            r[n] = r[n].T
        outs += [r[n].reshape(shapes[n]) for n in PARAMS]
    return tuple(outs)
```

```python
import functools

import numpy as np
import jax
import jax.numpy as jnp
from jax import lax
from jax.experimental import pallas as pl
from jax.experimental.pallas import tpu as pltpu

F32, BF = jnp.float32, jnp.bfloat16

D = 1024
GRID_W = 64
EPS = 1e-6
POOL_WINDOWS = (2, 4, 8, 16)
PGW = 256
DIN = 2048
HEAD = 64
NST = 128
NG = 4
HPG = 8
GWID = HPG * HEAD
Q = 128
CONV_DIM = 3072
OFF_GATE, OFF_XBC, OFF_DT, IN_COLS = 4096, 6144, 9216, 9280
NDEV = 8
ADAM_LR, ADAM_B1, ADAM_B2, ADAM_EPS, ADAM_WD, ADAM_STEP = 0.001, 0.9, 0.999, 1e-08, 0.01, 10

V7X_VMEM_LIMIT = 56 * 2 ** 20
ROW_TILE = 256


def _params(sem=None):
    return pltpu.CompilerParams(dimension_semantics=sem, vmem_limit_bytes=V7X_VMEM_LIMIT)


def _dot(a, b):
    return jnp.dot(a.astype(BF), b.astype(BF), preferred_element_type=F32)


def _dot_nt(a, b):
    return lax.dot_general(a.astype(BF), b.astype(BF), (((1,), (1,)), ((), ())), preferred_element_type=F32)


def _dot_tn(a, b):
    return lax.dot_general(a.astype(BF), b.astype(BF), (((0,), (0,)), ((), ())), preferred_element_type=F32)


def _split(a, n):
    parts = []
    for _ in range(n):
        p = a.astype(BF)
        parts.append(p)
        a = a - p.astype(F32)
    return parts


def _dot_sl(a, b01, n=3):
    parts = _split(a, n)
    m = a.shape[0]
    if n == 1 or m % 16:
        return sum(jnp.dot(p, b01, preferred_element_type=F32) for p in parts)
    r = jnp.dot(jnp.concatenate(parts, axis=0), b01, preferred_element_type=F32)
    return sum(r[i * m:(i + 1) * m] for i in range(n))


def _dot_sr(a01, b, n=3):
    parts = _split(b, n)
    k = b.shape[1]
    if n == 1 or k % 128:
        return sum(jnp.dot(a01, p, preferred_element_type=F32) for p in parts)
    r = jnp.dot(a01, jnp.concatenate(parts, axis=1), preferred_element_type=F32)
    return sum(r[:, i * k:(i + 1) * k] for i in range(n))


def _sigmoid(x):
    return 1.0 / (1.0 + jnp.exp(-x))


class _SideCopies:
    NSEM = 8

    def __init__(self, inputs, out_shapes, make):
        self.inputs, self.out_shapes, self.make = list(inputs), list(out_shapes), make

    def scratch(self):
        return [pltpu.SemaphoreType.DMA((self.NSEM,)), pltpu.SemaphoreType.DMA((self.NSEM,)), pltpu.SemaphoreType.DMA]

    def start(self, grid, in_refs, out_refs, sems):
        @pl.when(functools.reduce(lambda p, q: p & q, [pl.program_id(i) == 0 for i in range(len(grid))]))
        def _():
            local, sends, _ = self.make(in_refs, out_refs, *sems, arrivals=False)
            for cp in local + sends:
                cp.start()

    def wait(self, grid, in_refs, out_refs, sems):
        @pl.when(functools.reduce(lambda p, q: p & q, [pl.program_id(i) == n - 1 for i, n in enumerate(grid)]))
        def _():
            local, sends, recvs = self.make(in_refs, out_refs, *sems)
            for cp in sends:
                cp.wait_send()
            for cp in recvs:
                cp.wait_recv()
            for cp in local:
                cp.wait()


def _matmul(a, b, out_dtype, name, tm=512, tn=512, tk=1024, bt=False, n=None, side=None):
    M, K = a.shape
    N = n if n is not None else (b.shape[0] if bt else b.shape[1])
    tm, tn, tk = min(tm, M), min(tn, N), min(tk, K)
    assert M % tm == 0 and N % tn == 0 and K % tk == 0, (a.shape, b.shape)
    nk = K // tk
    grid = (M // tm, N // tn, nk)
    n_si, n_so = (len(side.inputs), len(side.out_shapes)) if side else (0, 0)

    def body(*refs):
        a_ref, b_ref, o_ref = refs[0], refs[1], refs[2 + n_si]
        acc = refs[3 + n_si + n_so]
        side_refs = (refs[2:2 + n_si], refs[3 + n_si:3 + n_si + n_so], refs[4 + n_si + n_so:])
        if side:
            side.start(grid, *side_refs)
        k = pl.program_id(2)
        p = _dot_nt(a_ref[...], b_ref[...]) if bt else _dot(a_ref[...], b_ref[...])

        @pl.when(k == 0)
        def _():
            acc[...] = p

        @pl.when(k > 0)
        def _():
            acc[...] += p

        @pl.when(k == nk - 1)
        def _():
            o_ref[...] = acc[...].astype(o_ref.dtype)

        if side:
            side.wait(grid, *side_refs)

    out = pl.pallas_call(
        body, name=name, grid=grid,
        in_specs=[pl.BlockSpec((tm, tk), lambda i, j, k: (i, k)),
                  pl.BlockSpec((tn, tk), lambda i, j, k: (j, k)) if bt else pl.BlockSpec((tk, tn), lambda i, j, k: (k, j))]
        + [ANY] * n_si,
        out_specs=(pl.BlockSpec((tm, tn), lambda i, j, k: (i, j)),) + (ANY,) * n_so,
        out_shape=(jax.ShapeDtypeStruct((M, N), out_dtype),) + tuple(side.out_shapes if side else ()),
        scratch_shapes=[pltpu.VMEM((tm, tn), F32)] + (side.scratch() if side else []),
        compiler_params=_params(("arbitrary",) * 3 if side else ("parallel", "parallel", "arbitrary")),
    )(a, b, *(side.inputs if side else ()))
    return out if side else out[0]


def _matmul_tn(a, g, name, ta=512, tn=512, tr=512):
    M, Ka = a.shape
    N = g.shape[1]
    ta, tn, tr = min(ta, Ka), min(tn, N), min(tr, M)
    assert M % tr == 0 and N % tn == 0 and Ka % ta == 0, (a.shape, g.shape)
    nr = M // tr

    def body(a_ref, g_ref, o_ref):
        k = pl.program_id(2)
        p = _dot_tn(a_ref[...], g_ref[...])

        @pl.when(k == 0)
        def _():
            o_ref[...] = p

        @pl.when(k > 0)
        def _():
            o_ref[...] += p

    return pl.pallas_call(
        body, name=name, grid=(Ka // ta, N // tn, nr),
        in_specs=[pl.BlockSpec((tr, ta), lambda i, j, k: (k, i)), pl.BlockSpec((tr, tn), lambda i, j, k: (k, j))],
        out_specs=pl.BlockSpec((ta, tn), lambda i, j, k: (i, j)),
        out_shape=jax.ShapeDtypeStruct((Ka, N), F32),
        compiler_params=_params(("parallel", "parallel", "arbitrary")),
    )(a, g)


def _dhx(pieces, ddt, w_inT, w_dtT, side=None):
    _, R, _ = pieces[0].shape
    tm = R // 4
    kb = 1024
    starts, nblk = [], []
    for p in pieces:
        starts.append(sum(nblk))
        nblk.append(p.shape[2] // kb)
    nk = sum(nblk)
    assert nk * kb == OFF_DT and R % 128 == 0
    npc = len(pieces)
    grid = (2, R // tm, nk)
    n_si, n_so = (len(side.inputs), len(side.out_shapes)) if side else (0, 0)

    def body(*refs):
        a_refs, dt_ref, w_ref, wdt_ref = refs[:npc], refs[npc], refs[npc + 1], refs[npc + 2]
        o_ref, acc = refs[npc + 3 + n_si], refs[npc + 4 + n_si + n_so]
        side_refs = (refs[npc + 3:npc + 3 + n_si], refs[npc + 4 + n_si:npc + 4 + n_si + n_so], refs[npc + 5 + n_si + n_so:])
        if side:
            side.start(grid, *side_refs)
        k = pl.program_id(2)

        @pl.when(k == 0)
        def _():
            acc[...] = _dot(dt_ref[0], wdt_ref[...])

        for p in range(npc):
            @pl.when((k >= starts[p]) & (k < starts[p] + nblk[p]))
            def _(p=p):
                acc[...] += _dot(a_refs[p][0], w_ref[...])

        @pl.when(k == nk - 1)
        def _():
            o_ref[0] = acc[...].astype(BF)

        if side:
            side.wait(grid, *side_refs)

    in_specs = [pl.BlockSpec((1, tm, kb), functools.partial(
        lambda e, t, k, s, nb: (e, t, jnp.clip(k - s, 0, nb - 1)), s=starts[p], nb=nblk[p])) for p in range(npc)]
    in_specs += [pl.BlockSpec((1, tm, 128), lambda e, t, k: (e, t, 0)),
                 pl.BlockSpec((kb, D), lambda e, t, k: (k, 0)),
                 pl.BlockSpec((128, D), lambda e, t, k: (0, 0))]
    out = pl.pallas_call(
        body, name="d_hx", grid=grid, in_specs=in_specs + [ANY] * n_si,
        out_specs=(pl.BlockSpec((1, tm, D), lambda e, t, k: (e, t, 0)),) + (ANY,) * n_so,
        out_shape=(jax.ShapeDtypeStruct((2, R, D), BF),) + tuple(side.out_shapes if side else ()),
        scratch_shapes=[pltpu.VMEM((tm, D), F32)] + (side.scratch() if side else []),
        compiler_params=_params(("arbitrary",) * 3 if side else ("parallel", "parallel", "arbitrary")),
    )(*pieces, ddt, w_inT, w_dtT, *(side.inputs if side else ()))
    return out if side else out[0]


def _adaln_fwd(c16, w_adaT_bf, b_ada):
    def body(c_ref, w_ref, b_ref, o_ref):
        cc = c_ref[...]
        o_ref[...] = _dot_nt(cc * _sigmoid(cc), w_ref[...]) + b_ref[...]

    return pl.pallas_call(body, name="adaln_fwd", out_shape=jax.ShapeDtypeStruct((16, 3 * D), F32),
                          compiler_params=_params())(c16, w_adaT_bf, b_ada)


def _adaln_bwd(acc_n, acc_f, mod16, c16, norm_pre, w_adaT_bf):
    def body(an_ref, af_ref, mod_ref, c_ref, np_ref, wt_ref, dw_ref, db_ref, sm_ref, dmod):
        npre = np_ref[...]
        dmod[...] = jnp.zeros_like(dmod)
        dnp = jnp.zeros((1, D), F32)
        dshift_c = jnp.zeros((1, D), F32)
        dgpre_c = jnp.zeros((1, D), F32)
        scale_c = mod_ref[2:3, D:2 * D]
        for e in range(2):
            dg_x, ds_x = an_ref[e, 0, 0:1, :], an_ref[e, 0, 1:2, :]
            dg_c, ds_c = an_ref[e, 1, 0:1, :], an_ref[e, 1, 1:2, :]
            dmod[e:e + 1, 0:D] = ds_x
            dmod[e:e + 1, D:2 * D] = dg_x * npre
            dmod[e:e + 1, 2 * D:3 * D] = af_ref[e, 0:1, :]
            dnp = dnp + dg_x * (1.0 + mod_ref[e:e + 1, D:2 * D]) + dg_c * (1.0 + scale_c)
            dshift_c = dshift_c + ds_c
            dgpre_c = dgpre_c + dg_c
        dmod[2:3, 0:D] = dshift_c
        dmod[2:3, D:2 * D] = dgpre_c * npre
        dm = dmod[...]
        cc = c_ref[...]
        sg = _sigmoid(cc)
        dw_ref[...] = _dot_tn(dm, cc * sg)
        db_ref[...] = jnp.zeros_like(db_ref)
        db_ref[0:1, :] = jnp.sum(dm, axis=0, keepdims=True)
        dsilu = sg * (1.0 + cc * (1.0 - sg))
        dcs = _dot(dm, wt_ref[...]) * dsilu
        sm_ref[...] = jnp.zeros_like(sm_ref)
        sm_ref[0:1, :] = dnp
        sm_ref[1:2, :] = dcs[2:3, :]

    return pl.pallas_call(
        body, name="adaln_bwd",
        out_shape=(jax.ShapeDtypeStruct((3 * D, D), F32), jax.ShapeDtypeStruct((16, 3 * D), F32),
                   jax.ShapeDtypeStruct((8, D), F32)),
        scratch_shapes=[pltpu.VMEM((16, 3 * D), F32)],
        compiler_params=_params())(acc_n, acc_f, mod16, c16, norm_pre, w_adaT_bf)


def _row_specs(L):
    nx = L // ROW_TILE
    return (pl.BlockSpec((1, ROW_TILE, D), lambda e, t: (e, jnp.minimum(t, nx - 1), 0)),
            pl.BlockSpec((1, ROW_TILE, D), lambda e, t: (e, jnp.maximum(t - nx, 0), 0)))


def _norm_mod_fwd(x, ctx, tab):
    L = x.shape[1]
    R = L + ctx.shape[1]
    nx = L // ROW_TILE

    def body(x_ref, c_ref, t_ref, o_ref):
        x = jnp.where(pl.program_id(1) < nx, x_ref[0], c_ref[0])
        r = lax.rsqrt(jnp.mean(x * x, axis=-1, keepdims=True) + EPS)
        t = t_ref[0, 0]
        o_ref[0] = (x * r * t[0:1] + t[1:2]).astype(BF)

    return pl.pallas_call(
        body, name="norm_mod_fwd", grid=(2, R // ROW_TILE),
        in_specs=[*_row_specs(L), pl.BlockSpec((1, 1, 8, D), lambda e, t: (e, t // nx, 0, 0))],
        out_specs=pl.BlockSpec((1, ROW_TILE, D), lambda e, t: (e, t, 0)),
        out_shape=jax.ShapeDtypeStruct((2, R, D), BF),
        compiler_params=_params(("parallel", "parallel")),
    )(x, ctx, tab)


def _norm_mod_bwd(dh, x, ctx, tab, dxo):
    L = x.shape[1]
    R = L + ctx.shape[1]
    nx = L // ROW_TILE

    def body(dh_ref, x_ref, c_ref, t_ref, dxo_ref, gx_ref, acc_ref):
        t = pl.program_id(1)
        x = jnp.where(t < nx, x_ref[0], c_ref[0])
        r = lax.rsqrt(jnp.mean(x * x, axis=-1, keepdims=True) + EPS)
        xn = x * r
        dh = dh_ref[0].astype(F32)

        @pl.when((t == 0) | (t == nx))
        def _():
            acc_ref[...] = jnp.zeros_like(acc_ref)

        acc_ref[0, 0, 0:1, :] += jnp.sum(dh * xn, axis=0, keepdims=True)
        acc_ref[0, 0, 1:2, :] += jnp.sum(dh, axis=0, keepdims=True)

        @pl.when(t < nx)
        def _():
            dxn = dh * t_ref[0, 0][0:1]
            dx = r * (dxn - xn * jnp.mean(dxn * xn, axis=-1, keepdims=True))
            gx_ref[0] = dxo_ref[0] + dx

    xspec, cspec = _row_specs(L)
    return pl.pallas_call(
        body, name="norm_mod_bwd", grid=(2, R // ROW_TILE),
        in_specs=[pl.BlockSpec((1, ROW_TILE, D), lambda e, t: (e, t, 0)), xspec, cspec,
                  pl.BlockSpec((1, 1, 8, D), lambda e, t: (e, t // nx, 0, 0)), xspec],
        out_specs=(xspec, pl.BlockSpec((1, 1, 8, D), lambda e, t: (e, t // nx, 0, 0))),
        out_shape=(jax.ShapeDtypeStruct((2, L, D), F32), jax.ShapeDtypeStruct((2, 2, 8, D), F32)),
        compiler_params=_params(("parallel", "arbitrary")),
    )(dh, x, ctx, tab, dxo)


POOL_TILE = 256


def _pool_tables(L):
    rows = L // GRID_W
    mats = np.zeros((4, POOL_TILE, POOL_TILE), np.float32)
    inv = np.zeros((4, L, 1), np.float32)
    for gi, k in enumerate(POOL_WINDOWS):
        lo, hi = k // 2, k - 1 - k // 2
        m = np.zeros((GRID_W, GRID_W), np.float32)
        for t in range(GRID_W):
            m[t, max(t - lo, 0):min(t + hi, GRID_W - 1) + 1] = 1.0
        for b in range(POOL_TILE // GRID_W):
            mats[gi, b * GRID_W:(b + 1) * GRID_W, b * GRID_W:(b + 1) * GRID_W] = m
        cnt_c = m.sum(1)
        cnt_r = np.array([min(r + hi, rows - 1) - max(r - lo, 0) + 1 for r in range(rows)], np.float32)
        inv[gi, :, 0] = (1.0 / (cnt_r[:, None] * cnt_c[None, :])).reshape(-1)
    matsT = np.ascontiguousarray(np.transpose(mats, (0, 2, 1)))
    return (jnp.asarray(mats, BF), jnp.asarray(matsT, BF), jnp.asarray(inv))


def _pool_cols(get_tile, mat, cs_ref, L, n):
    def step(i, carry):
        off = pl.multiple_of(i * POOL_TILE, POOL_TILE)
        cs_ref[pl.ds(GRID_W + off, POOL_TILE), :] = _dot_sr(mat, get_tile(off).astype(F32), n)
        return carry

    lax.fori_loop(0, L // POOL_TILE, step, 0)
    cs_ref[pl.ds(0, GRID_W), :] = jnp.zeros((GRID_W, PGW), F32)

    def prefix(r, carry):
        o = pl.multiple_of(r * GRID_W, GRID_W)
        cs_ref[pl.ds(o + GRID_W, GRID_W), :] = cs_ref[pl.ds(o + GRID_W, GRID_W), :] + cs_ref[pl.ds(o, GRID_W), :]
        return carry

    lax.fori_loop(0, L // GRID_W, prefix, 0)


def _pool_rows(cs_ref, off, below, above, L):
    rows = L // GRID_W
    r0 = off // GRID_W
    parts = []
    for i in range(POOL_TILE // GRID_W):
        hi = pl.multiple_of(jnp.minimum(r0 + i + above + 1, rows) * GRID_W, GRID_W)
        lo = pl.multiple_of(jnp.maximum(r0 + i - below, 0) * GRID_W, GRID_W)
        parts.append(cs_ref[pl.ds(hi, GRID_W), :] - cs_ref[pl.ds(lo, GRID_W), :])
    return jnp.concatenate(parts, axis=0)


def _pool_fwd(proj3, pool_w_bf, pool_scale, tables, L):
    mats, _, inv = tables
    nt = L // POOL_TILE

    def body(v_ref, z_ref, pw_ref, ps_ref, m_ref, inv_ref, o_ref, cs_ref):
        _pool_cols(lambda off: v_ref[0, pl.ds(off, POOL_TILE), :], m_ref[0], cs_ref, L, 1)
        half = lax.shift_left(1, pl.program_id(1))

        def step(i, carry):
            off = pl.multiple_of(i * POOL_TILE, POOL_TILE)
            rows = pl.ds(off, POOL_TILE)
            v = v_ref[0, rows, :].astype(F32)
            diff = _pool_rows(cs_ref, off, half, half - 1, L) * inv_ref[0, rows, :] - v
            yp = _dot(diff, pw_ref[0])
            z = z_ref[0, rows, :].astype(F32)
            o_ref[0, rows, :] = (yp * ps_ref[...] * (z * _sigmoid(z))).astype(BF)
            return carry

        lax.fori_loop(0, nt, step, 0)

    return pl.pallas_call(
        body, name="pool_fwd", grid=(2, 4),
        in_specs=[pl.BlockSpec((1, L, PGW), lambda e, g: (e, 0, g)),
                  pl.BlockSpec((1, L, PGW), lambda e, g: (e, 0, 4 + g)),
                  pl.BlockSpec((1, PGW, PGW), lambda e, g: (g, 0, 0)),
                  pl.BlockSpec((1, PGW), lambda e, g: (0, g)),
                  pl.BlockSpec((1, POOL_TILE, POOL_TILE), lambda e, g: (g, 0, 0)),
                  pl.BlockSpec((1, L, 1), lambda e, g: (g, 0, 0))],
        out_specs=pl.BlockSpec((1, L, PGW), lambda e, g: (e, 0, g)),
        out_shape=jax.ShapeDtypeStruct((2, L, D), BF),
        scratch_shapes=[pltpu.VMEM((L + GRID_W, PGW), F32)],
        compiler_params=_params(("parallel", "parallel")),
    )(proj3, proj3, pool_w_bf, pool_scale, mats, inv)


def _pool_bwd(proj3, d_ypool, pool_w_bf, pool_wT_bf, pool_scale, tables, L):
    mats, matsT, inv = tables
    nt = L // POOL_TILE
    R = proj3.shape[1]

    def body(v_ref, z_ref, dy_ref, pw_ref, pwt_ref, ps_ref, m_ref, mt_ref, inv_ref,
             dv_ref, dz_ref, dpw_ref, acc_ref, cs_ref, dd_ref):
        e = pl.program_id(1)

        @pl.when(e == 0)
        def _():
            dpw_ref[...] = jnp.zeros_like(dpw_ref)
            acc_ref[...] = jnp.zeros_like(acc_ref)

        _pool_cols(lambda off: v_ref[0, pl.ds(off, POOL_TILE), :], m_ref[0], cs_ref, L, 1)
        half = lax.shift_left(1, pl.program_id(0))
        ps = ps_ref[...]

        def step(i, carry):
            off = pl.multiple_of(i * POOL_TILE, POOL_TILE)
            rows = pl.ds(off, POOL_TILE)
            v = v_ref[0, rows, :].astype(F32)
            diff = _pool_rows(cs_ref, off, half, half - 1, L) * inv_ref[0, rows, :] - v
            yp = _dot(diff, pw_ref[0])
            z = z_ref[0, rows, :].astype(F32)
            sg = _sigmoid(z)
            sz = z * sg
            dy = dy_ref[0, rows, :].astype(F32)
            dz_ref[0, rows, :] = (dy * yp * ps * (sg * (1.0 + z * (1.0 - sg)))).astype(BF)
            dys = dy * sz
            acc_ref[0, 0:1, :] += jnp.sum(dys * yp, axis=0, keepdims=True)
            dyp = dys * ps
            dpw_ref[0] += _dot_tn(diff, dyp)
            dd_ref[rows, :] = _dot(dyp, pwt_ref[0])
            return carry

        lax.fori_loop(0, nt, step, 0)
        _pool_cols(lambda off: dd_ref[pl.ds(off, POOL_TILE), :] * inv_ref[0, pl.ds(off, POOL_TILE), :],
                   mt_ref[0], cs_ref, L, 2)

        def step2(i, carry):
            off = pl.multiple_of(i * POOL_TILE, POOL_TILE)
            rows = pl.ds(off, POOL_TILE)
            dv_ref[0, rows, :] = (_pool_rows(cs_ref, off, half - 1, half, L) - dd_ref[rows, :]).astype(BF)
            return carry

        lax.fori_loop(0, nt, step2, 0)
        dv_ref[0, pl.ds(L, R - L), :] = jnp.zeros((R - L, PGW), BF)
        dz_ref[0, pl.ds(L, R - L), :] = jnp.zeros((R - L, PGW), BF)

    return pl.pallas_call(
        body, name="pool_bwd", grid=(4, 2),
        in_specs=[pl.BlockSpec((1, L, PGW), lambda g, e: (e, 0, g)),
                  pl.BlockSpec((1, L, PGW), lambda g, e: (e, 0, 4 + g)),
                  pl.BlockSpec((1, L, PGW), lambda g, e: (e, 0, g)),
                  pl.BlockSpec((1, PGW, PGW), lambda g, e: (g, 0, 0)),
                  pl.BlockSpec((1, PGW, PGW), lambda g, e: (g, 0, 0)),
                  pl.BlockSpec((1, PGW), lambda g, e: (0, g)),
                  pl.BlockSpec((1, POOL_TILE, POOL_TILE), lambda g, e: (g, 0, 0)),
                  pl.BlockSpec((1, POOL_TILE, POOL_TILE), lambda g, e: (g, 0, 0)),
                  pl.BlockSpec((1, L, 1), lambda g, e: (g, 0, 0))],
        out_specs=(pl.BlockSpec((1, R, PGW), lambda g, e: (e, 0, g)),
                   pl.BlockSpec((1, R, PGW), lambda g, e: (e, 0, g)),
                   pl.BlockSpec((1, PGW, PGW), lambda g, e: (g, 0, 0)),
                   pl.BlockSpec((1, 8, PGW), lambda g, e: (g, 0, 0))),
        out_shape=(jax.ShapeDtypeStruct((2, R, D), BF), jax.ShapeDtypeStruct((2, R, D), BF),
                   jax.ShapeDtypeStruct((4, PGW, PGW), F32), jax.ShapeDtypeStruct((4, 8, PGW), F32)),
        scratch_shapes=[pltpu.VMEM((L + GRID_W, PGW), F32), pltpu.VMEM((L, PGW), F32)],
        compiler_params=_params(("parallel", "arbitrary")),
    )(proj3, proj3, d_ypool, pool_w_bf, pool_wT_bf, pool_scale, mats, matsT, inv)


CONV_BLOCK = 128


CONV_CHUNK = 64
CONV_HALO = 8


def _halo_buf_init(buf, val, R):
    z = jnp.zeros((CONV_HALO, CONV_BLOCK), F32)
    buf[pl.ds(0, CONV_HALO), :] = z
    buf[pl.ds(CONV_HALO + R, CONV_HALO), :] = z
    if val is not None:
        buf[pl.ds(CONV_HALO, R), :] = val


def _chunk_taps(buf, start, offs, L):
    n = CONV_CHUNK + 2 * CONV_HALO
    ext = buf[pl.ds(start, n), :]
    out = []
    for off in offs:
        if off == 0:
            out.append(ext[CONV_HALO:CONV_HALO + CONV_CHUNK])
            continue
        r = pltpu.roll(ext, (-off) % n, 0)[CONV_HALO:CONV_HALO + CONV_CHUNK]
        lo, hi = (start, start + CONV_CHUNK - 1 + off) if off > 0 else (start + off, start + CONV_CHUNK - 1)
        if lo < L <= hi:
            t = start + lax.broadcasted_iota(jnp.int32, (CONV_CHUNK, 1), 0)
            r = jnp.where((t < L) == (t + off < L), r, 0.0)
        out.append(r)
    return out


def _fold8(x):
    return sum(x[i * 8:(i + 1) * 8] for i in range(CONV_CHUNK // 8))


def _conv_fwd(proj3, conv_w, conv_b, L):
    _, R, _ = proj3.shape
    cb0 = OFF_XBC // CONV_BLOCK

    def body(u_ref, w_ref, b_ref, o_ref, ubuf):
        _halo_buf_init(ubuf, u_ref[0].astype(F32), R)
        w = w_ref[...]
        b = b_ref[...]
        for start in range(0, R, CONV_CHUNK):
            taps = _chunk_taps(ubuf, start, (-2, -1, 0, 1), L)
            pre = b + sum(taps[k] * w[k:k + 1, :] for k in range(4))
            o_ref[0, pl.ds(start, CONV_CHUNK), :] = (pre * _sigmoid(pre)).astype(BF)

    return pl.pallas_call(
        body, name="conv_fwd", grid=(2, CONV_DIM // CONV_BLOCK),
        in_specs=[pl.BlockSpec((1, R, CONV_BLOCK), lambda e, j: (e, 0, cb0 + j)),
                  pl.BlockSpec((4, CONV_BLOCK), lambda e, j: (0, j)),
                  pl.BlockSpec((1, CONV_BLOCK), lambda e, j: (0, j))],
        out_specs=pl.BlockSpec((1, R, CONV_BLOCK), lambda e, j: (e, 0, j)),
        out_shape=jax.ShapeDtypeStruct((2, R, CONV_DIM), BF),
        scratch_shapes=[pltpu.VMEM((R + 2 * CONV_HALO, CONV_BLOCK), F32)],
        compiler_params=_params(("parallel", "parallel")),
    )(proj3, conv_w, conv_b)


def _conv_bwd(proj3, addends, scales, col0, ncols, in_maps, conv_w, conv_b, L, name):
    _, R, _ = proj3.shape
    cb0 = (OFF_XBC + col0) // CONV_BLOCK
    wb0 = col0 // CONV_BLOCK
    na = len(addends)
    scaled = [i for i in range(na) if scales[i] is not None]

    def body(*refs):
        u_ref, w_ref, b_ref = refs[0], refs[1], refs[2]
        a_refs = refs[3:3 + na]
        s_refs = dict(zip(scaled, refs[3 + na:3 + na + len(scaled)]))
        o_ref, acc_ref, ubuf, dbuf = refs[3 + na + len(scaled):]
        _halo_buf_init(ubuf, u_ref[0].astype(F32), R)
        _halo_buf_init(dbuf, None, R)
        w = w_ref[...]
        b = b_ref[...]
        scl = {i: s_refs[i][...] for i in scaled}
        sums = [jnp.zeros((8, CONV_BLOCK), F32) for _ in range(5)]
        for start in range(0, R, CONV_CHUNK):
            rows = pl.ds(start, CONV_CHUNK)
            taps = _chunk_taps(ubuf, start, (-2, -1, 0, 1), L)
            pre = b + sum(taps[k] * w[k:k + 1, :] for k in range(4))
            sg = _sigmoid(pre)
            dxbc = None
            for i, a in enumerate(a_refs):
                t = a[0, rows, :].astype(F32)
                t = t * scl[i] if i in scl else t
                dxbc = t if dxbc is None else dxbc + t
            dpre = dxbc * (sg * (1.0 + pre * (1.0 - sg)))
            dbuf[pl.ds(start + CONV_HALO, CONV_CHUNK), :] = dpre
            for k in range(4):
                sums[k] = sums[k] + _fold8(dpre * taps[k])
            sums[4] = sums[4] + _fold8(dpre)
        acc_ref[...] = jnp.zeros_like(acc_ref)
        for k in range(5):
            acc_ref[0, k:k + 1, :] = jnp.sum(sums[k], axis=0, keepdims=True)
        for start in range(0, R, CONV_CHUNK):
            d = _chunk_taps(dbuf, start, (2, 1, 0, -1), L)
            o_ref[0, pl.ds(start, CONV_CHUNK), :] = sum(d[k] * w[k:k + 1, :] for k in range(4)).astype(BF)

    in_specs = [pl.BlockSpec((1, R, CONV_BLOCK), lambda e, j: (e, 0, cb0 + j)),
                pl.BlockSpec((4, CONV_BLOCK), lambda e, j: (0, wb0 + j)),
                pl.BlockSpec((1, CONV_BLOCK), lambda e, j: (0, wb0 + j))]
    for m in in_maps:
        in_specs.append(pl.BlockSpec((1, R, CONV_BLOCK), functools.partial(lambda e, j, m: (e, 0, m(j)), m=m)))
    for i in scaled:
        in_specs.append(pl.BlockSpec((1, CONV_BLOCK), functools.partial(lambda e, j, m: (0, m(j)), m=in_maps[i])))
    return pl.pallas_call(
        body, name=name, grid=(2, ncols // CONV_BLOCK),
        in_specs=in_specs,
        out_specs=(pl.BlockSpec((1, R, CONV_BLOCK), lambda e, j: (e, 0, j)),
                   pl.BlockSpec((1, 8, CONV_BLOCK), lambda e, j: (e, 0, j))),
        out_shape=(jax.ShapeDtypeStruct((2, R, ncols), BF), jax.ShapeDtypeStruct((2, 8, ncols), F32)),
        scratch_shapes=[pltpu.VMEM((R + 2 * CONV_HALO, CONV_BLOCK), F32)] * 2,
        compiler_params=_params(("parallel", "parallel")),
    )(proj3, conv_w, conv_b, *addends, *[scales[i] for i in scaled])


def _softplus(x):
    e = jnp.exp(-jnp.abs(x))
    u = 1.0 + e
    return jnp.maximum(x, 0.0) + jnp.where(u == 1.0, e, e * jnp.log(u) / (u - 1.0))


def _to_local_mat(g, transpose=False):
    r = lax.broadcasted_iota(jnp.int32, (128, 128), 1 if transpose else 0)
    c = lax.broadcasted_iota(jnp.int32, (128, 128), 0 if transpose else 1)
    return ((c < 2 * HPG) & (r == jnp.right_shift(c, 3) * (NG * HPG) + g * HPG + (c & (HPG - 1)))).astype(BF)


def _dt_fwd(dt_raw, bias128):
    _, R, _ = dt_raw.shape

    def body(x_ref, b_ref, o_ref):
        dt = _softplus(x_ref[0] + b_ref[...])
        for g in range(NG):
            o_ref[0, g] = _dot_sl(dt, _to_local_mat(g))

    tr = R // 4
    return pl.pallas_call(
        body, name="dt_fwd", grid=(2, 4),
        in_specs=[pl.BlockSpec((1, tr, 128), lambda e, t: (e, t, 0)), pl.BlockSpec((1, 128), lambda e, t: (0, 0))],
        out_specs=pl.BlockSpec((1, NG, tr, 128), lambda e, t: (e, 0, t, 0)),
        out_shape=jax.ShapeDtypeStruct((2, NG, R, 128), F32),
        compiler_params=_params(("parallel", "parallel")),
    )(dt_raw, bias128)


def _dt_bwd(dt_raw, bias128, ddt_f, ddt_b):
    _, R, _ = dt_raw.shape

    def body(x_ref, b_ref, f_ref, g_ref, o_ref, acc_ref):
        ddt = sum(_dot_sl(f_ref[0, g] + g_ref[0, g], _to_local_mat(g, transpose=True)) for g in range(NG))
        d = ddt * _sigmoid(x_ref[0] + b_ref[...])
        o_ref[0] = d.astype(BF)

        @pl.when(pl.program_id(1) == 0)
        def _():
            acc_ref[...] = jnp.zeros_like(acc_ref)

        acc_ref[0, 0:1, :] += jnp.sum(d, axis=0, keepdims=True)

    tr = R // 4
    blk = pl.BlockSpec((1, tr, 128), lambda e, t: (e, t, 0))
    loc = pl.BlockSpec((1, NG, tr, 128), lambda e, t: (e, 0, t, 0))
    return pl.pallas_call(
        body, name="dt_bwd", grid=(2, 4),
        in_specs=[blk, pl.BlockSpec((1, 128), lambda e, t: (0, 0)), loc, loc],
        out_specs=(blk, pl.BlockSpec((1, 8, 128), lambda e, t: (e, 0, 0))),
        out_shape=(jax.ShapeDtypeStruct(dt_raw.shape, BF), jax.ShapeDtypeStruct((2, 8, 128), F32)),
        compiler_params=_params(("parallel", "arbitrary")),
    )(dt_raw, bias128, ddt_f, ddt_b)


GPS = 4


def _tri(d):
    i = lax.broadcasted_iota(jnp.int32, (Q, Q), 0)
    j = lax.broadcasted_iota(jnp.int32, (Q, Q), 1)
    return (i >= j) if d == 0 else (i <= j)


def _expand_mat(d):
    r = lax.broadcasted_iota(jnp.int32, (128, GWID), 0)
    c = lax.broadcasted_iota(jnp.int32, (128, GWID), 1)
    return (r == d * HPG + jnp.right_shift(c, 6)).astype(BF)


def _reduce_mat(d):
    r = lax.broadcasted_iota(jnp.int32, (GWID, 128), 0)
    c = lax.broadcasted_iota(jnp.int32, (GWID, 128), 1)
    return (c == d * HPG + jnp.right_shift(r, 6)).astype(BF)


def _ssd_chunk(d, dt, A, xs, B, C):
    mask = _tri(d)
    T = mask.astype(BF)
    Tt = _tri(1 - d).astype(BF)
    a = dt * A
    acs = _dot_sr(T, a)
    E = _expand_mat(d)
    dt_e = _dot_sl(dt, E, 2)
    acs_e = _dot_sl(acs, E, 2)
    alast_e = acs_e[Q - 1:Q, :] if d == 0 else acs_e[0:1, :]
    return dict(mask=mask, T=T, Tt=Tt, acs=acs, acsT=acs.T, dt_e=dt_e, acs_e=acs_e, lam=jnp.exp(acs_e),
                w=jnp.exp(alast_e - acs_e), decay=jnp.exp(alast_e), xt=xs * dt_e, CB=_dot_nt(C, B))


def _head_decay(q, d, hh):
    col = q["acs"][:, d * HPG + hh:d * HPG + hh + 1]
    row = q["acsT"][d * HPG + hh:d * HPG + hh + 1, :]
    return jnp.exp(jnp.where(q["mask"], col - row, -jnp.inf))


def _chunk_maps(NX, NS):
    cf = lambda s: lax.rem(s + NX, NS)
    cb = lambda s: NS - 1 - s
    return cf, cb


def _ssd_fwd(xbc, dt_loc, a_loc, L):
    _, R, _ = xbc.shape
    NX, NS = L // Q, R // Q
    cf, cb = _chunk_maps(NX, NS)

    def body(xs_f, b_f, c_f, dt_f, xs_b, b_b, c_b, dt_b, a_ref, y_f, hs_f, y_b, hs_b, hT):
        @pl.when(pl.program_id(2) == 0)
        def _():
            hT[...] = jnp.zeros_like(hT)

        lane = lax.broadcasted_iota(jnp.int32, (Q, 128), 1)
        for d, (xs_ref, b_ref, c_ref, dt_ref, y_ref, hs_ref) in enumerate(
                ((xs_f, b_f, c_f, dt_f, y_f, hs_f), (xs_b, b_b, c_b, dt_b, y_b, hs_b))):
            for gi in range(GPS):
                cols = slice(gi * GWID, (gi + 1) * GWID)
                xs = xs_ref[0, :, cols].astype(F32)
                B, C = b_ref[0, :, gi * NST:(gi + 1) * NST], c_ref[0, :, gi * NST:(gi + 1) * NST]
                q = _ssd_chunk(d, dt_ref[0, gi], a_ref[gi, 0:1, :], xs, B, C)
                h = hT[d, :, cols]
                hb = h.astype(BF)
                hs_ref[0, 0, :, cols] = hb
                parts = []
                for pr in range(HPG // 2):
                    xp = q["xt"][:, pr * 128:(pr + 1) * 128]
                    xst = jnp.concatenate([jnp.where(lane < HEAD, xp, 0.0), jnp.where(lane < HEAD, 0.0, xp)], axis=0)
                    mst = jnp.concatenate([(q["CB"] * _head_decay(q, d, 2 * pr)).astype(BF),
                                           (q["CB"] * _head_decay(q, d, 2 * pr + 1)).astype(BF)], axis=1)
                    parts.append(_dot(mst, xst))
                y_ref[0, :, cols] = jnp.concatenate(parts, axis=1) + _dot(C, hb) * q["lam"]
                hT[d, :, cols] = q["decay"] * h + _dot_tn(B, q["xt"] * q["w"])

    def spec(shape, imap):
        return pl.BlockSpec(shape, imap)

    bc0 = DIN // (GPS * NST)

    def ins(c):
        return [spec((1, Q, GPS * GWID), lambda e, g, s: (e, c(s), g)),
                spec((1, Q, GPS * NST), lambda e, g, s: (e, c(s), bc0 + g)),
                spec((1, Q, GPS * NST), lambda e, g, s: (e, c(s), bc0 + NG // GPS + g)),
                spec((1, GPS, Q, 128), lambda e, g, s: (e, g, c(s), 0))]

    def outs(c):
        return [spec((1, Q, GPS * GWID), lambda e, g, s: (e, c(s), g)),
                spec((1, 1, NST, GPS * GWID), lambda e, g, s: (e, c(s), 0, g))]

    yshape = jax.ShapeDtypeStruct((2, R, DIN), F32)
    hshape = jax.ShapeDtypeStruct((2, NS, NST, DIN), BF)
    return pl.pallas_call(
        body, name="ssd_fwd", grid=(2, NG // GPS, NS),
        in_specs=ins(cf) + ins(cb) + [spec((GPS, 8, 128), lambda e, g, s: (g, 0, 0))],
        out_specs=tuple(outs(cf) + outs(cb)),
        out_shape=(yshape, hshape, yshape, hshape),
        scratch_shapes=[pltpu.VMEM((2, NST, GPS * GWID), F32)],
        compiler_params=_params(("parallel", "parallel", "arbitrary")),
    )(xbc, xbc, xbc, dt_loc, xbc, xbc, xbc, dt_loc, a_loc)


def _ssd_bwd(xbc, dt_loc, a_loc, hs_f, hs_b, y_f, y_b, dy, L):
    _, R, _ = xbc.shape
    NX, NS = L // Q, R // Q
    cf0, cb0 = _chunk_maps(NX, NS)
    cf = lambda sp: cf0(NS - 1 - sp)
    cb = lambda sp: cb0(NS - 1 - sp)

    def body(xs_f, b_f, c_f, dt_f, hs_f_, dy_f, y_f_, xs_b, b_b, c_b, dt_b, hs_b_, dy_b, y_b_, a_ref,
             dxs_f, dbc_f, ddt_f, dxs_b, dbc_b, ddt_b, da_ref, dhT):
        @pl.when(pl.program_id(2) == 0)
        def _():
            dhT[...] = jnp.zeros_like(dhT)
            da_ref[...] = jnp.zeros_like(da_ref)

        lane = lax.broadcasted_iota(jnp.int32, (Q, 128), 1)
        row = lax.broadcasted_iota(jnp.int32, (Q, 128), 0)

        def one_chain(d, gi, xs_ref, b_ref, c_ref, dt_ref, hs_ref, dy_ref, y_ref, dxs_ref, dbc_ref, ddt_ref):
            cols = slice(gi * GWID, (gi + 1) * GWID)
            A = a_ref[gi, 0:1, :]
            xs, dt = xs_ref[0, :, cols].astype(F32), dt_ref[0, gi]
            B, C = b_ref[0, :, gi * NST:(gi + 1) * NST], c_ref[0, :, gi * NST:(gi + 1) * NST]
            q = _ssd_chunk(d, dt, A, xs, B, C)
            xt, lam, w, decay = q["xt"], q["lam"], q["w"], q["decay"]
            H = hs_ref[0, 0, :, cols]
            dyv = dy_ref[0, :, cols].astype(F32)
            dh = dhT[d, :, cols]
            dZ = dyv * lam
            dC = _dot_nt(dZ, H)
            dH = _dot_tn(C, dZ)
            U = _dot(B, dh)
            xw = xt * w
            dxt = U * w
            dalast_e = (jnp.sum(U * xw, axis=0, keepdims=True)
                        + decay * jnp.sum(dh * H.astype(F32), axis=0, keepdims=True))
            dB = _dot_nt(xw, dh)
            dCB = jnp.zeros((Q, Q), F32)
            dxt_parts = []
            for pr in range(HPG // 2):
                xp = xt[:, pr * 128:(pr + 1) * 128]
                dyp = dyv[:, pr * 128:(pr + 1) * 128]
                L0, L1 = _head_decay(q, d, 2 * pr), _head_decay(q, d, 2 * pr + 1)
                dyst = jnp.concatenate([jnp.where(lane < HEAD, dyp, 0.0), jnp.where(lane < HEAD, 0.0, dyp)], axis=0)
                mst = jnp.concatenate([(q["CB"] * L0).astype(BF), (q["CB"] * L1).astype(BF)], axis=0)
                dxt_parts.append(_dot_tn(mst, dyst))
                dmst = _dot_nt(dyst, xp)
                dCB = dCB + dmst[:Q] * L0 + dmst[Q:] * L1
            dxt_diag = jnp.concatenate(dxt_parts, axis=1)
            dC = dC + _dot(dCB, B)
            dB = dB + _dot_tn(dCB, C)
            Rm = _reduce_mat(d)
            dacs = _dot_sl(dyv * y_ref[0, :, cols] - xt.astype(BF).astype(F32) * dxt_diag - U * xw, Rm, 2)
            dxt = dxt + dxt_diag
            dal = _dot_sl(jnp.broadcast_to(dalast_e, (8, GWID)), Rm, 2)[0:1, :]
            dacs = dacs + jnp.where(row == (Q - 1 if d == 0 else 0), dal, 0.0)
            da = _dot_sr(q["Tt"], dacs, 2)
            ddt_ref[0, gi] = da * A + _dot_sl(dxt * xs, Rm, 2)
            da_ref[0, gi, 0:1, :] += jnp.sum(da * dt, axis=0, keepdims=True)
            dxs_ref[0, :, cols] = (dxt * q["dt_e"]).astype(BF)
            dbc_ref[0, :, gi * 2 * NST:(gi + 1) * 2 * NST] = jnp.concatenate([dB, dC], axis=1).astype(BF)
            dhT[d, :, cols] = decay * dh + dH

        for gi in range(GPS):
            one_chain(0, gi, xs_f, b_f, c_f, dt_f, hs_f_, dy_f, y_f_, dxs_f, dbc_f, ddt_f)
            one_chain(1, gi, xs_b, b_b, c_b, dt_b, hs_b_, dy_b, y_b_, dxs_b, dbc_b, ddt_b)

    def spec(shape, imap):
        return pl.BlockSpec(shape, imap)

    bc0 = DIN // (GPS * NST)

    def ins(c):
        return [spec((1, Q, GPS * GWID), lambda e, g, s: (e, c(s), g)),
                spec((1, Q, GPS * NST), lambda e, g, s: (e, c(s), bc0 + g)),
                spec((1, Q, GPS * NST), lambda e, g, s: (e, c(s), bc0 + NG // GPS + g)),
                spec((1, GPS, Q, 128), lambda e, g, s: (e, g, c(s), 0)),
                spec((1, 1, NST, GPS * GWID), lambda e, g, s: (e, c(s), 0, g)),
                spec((1, Q, GPS * GWID), lambda e, g, s: (e, c(s), g)),
                spec((1, Q, GPS * GWID), lambda e, g, s: (e, c(s), g))]

    def outs(c):
        return [spec((1, Q, GPS * GWID), lambda e, g, s: (e, c(s), g)),
                spec((1, Q, GPS * 2 * NST), lambda e, g, s: (e, c(s), g)),
                spec((1, GPS, Q, 128), lambda e, g, s: (e, g, c(s), 0))]

    s_xs = jax.ShapeDtypeStruct((2, R, DIN), BF)
    s_bc = jax.ShapeDtypeStruct((2, R, 2 * NG * NST), BF)
    s_dt = jax.ShapeDtypeStruct((2, NG, R, 128), F32)
    return pl.pallas_call(
        body, name="ssd_bwd", grid=(2, NG // GPS, NS),
        in_specs=ins(cf) + ins(cb) + [spec((GPS, 8, 128), lambda e, g, s: (g, 0, 0))],
        out_specs=tuple(outs(cf) + outs(cb) + [spec((1, GPS, 8, 128), lambda e, g, s: (e, g, 0, 0))]),
        out_shape=(s_xs, s_bc, s_dt, s_xs, s_bc, s_dt, jax.ShapeDtypeStruct((2, NG, 8, 128), F32)),
        scratch_shapes=[pltpu.VMEM((2, NST, GPS * GWID), F32)],
        compiler_params=_params(("parallel", "parallel", "arbitrary")),
    )(xbc, xbc, xbc, dt_loc, hs_f, dy, y_f, xbc, xbc, xbc, dt_loc, hs_b, dy, y_b, a_loc)


def _ssd_post_fwd(y_f, y_b, xbc, proj3, dskip_e, ssd_norm, L):
    def body(yf_ref, yb_ref, xs_ref, z_ref, ds_ref, w_ref, o_ref):
        y2 = yf_ref[0] + yb_ref[0] + ds_ref[...] * xs_ref[0].astype(F32)
        z = z_ref[0].astype(F32)
        u = y2 * (z * _sigmoid(z))
        parts = []
        for g in range(NG):
            ug = u[:, g * GWID:(g + 1) * GWID]
            parts.append(ug * lax.rsqrt(jnp.mean(ug * ug, axis=-1, keepdims=True) + EPS))
        o_ref[0] = (jnp.concatenate(parts, axis=1) * w_ref[...]).astype(BF)

    blk = lambda c: pl.BlockSpec((1, ROW_TILE, DIN), lambda e, t: (e, t, c))
    vec = pl.BlockSpec((1, DIN), lambda e, t: (0, 0))
    return pl.pallas_call(
        body, name="ssd_post_fwd", grid=(2, L // ROW_TILE),
        in_specs=[blk(0), blk(0), blk(0), blk(1), vec, vec],
        out_specs=blk(0),
        out_shape=jax.ShapeDtypeStruct((2, L, DIN), BF),
        compiler_params=_params(("parallel", "parallel")),
    )(y_f, y_b, xbc, proj3, dskip_e, ssd_norm)


def _ssd_post_bwd(d_yn, y_f, y_b, xbc, proj3, dskip_e, ssd_norm, L):
    _, R, _ = y_f.shape
    nx = L // ROW_TILE

    def body(dyn_ref, yf_ref, yb_ref, xs_ref, z_ref, ds_ref, w_ref, dy_ref, dz_ref, acc_ref):
        t = pl.program_id(1)

        @pl.when(t == 0)
        def _():
            acc_ref[...] = jnp.zeros_like(acc_ref)

        @pl.when(t >= nx)
        def _():
            dy_ref[...] = jnp.zeros_like(dy_ref)
            dz_ref[...] = jnp.zeros_like(dz_ref)

        @pl.when(t < nx)
        def _():
            xs = xs_ref[0].astype(F32)
            y2 = yf_ref[0] + yb_ref[0] + ds_ref[...] * xs
            z = z_ref[0].astype(F32)
            sg = _sigmoid(z)
            sz = z * sg
            u = y2 * sz
            dyn = dyn_ref[0].astype(F32)
            dun = dyn * w_ref[...]
            uh_parts, du_parts = [], []
            for g in range(NG):
                sl = slice(g * GWID, (g + 1) * GWID)
                ug = u[:, sl]
                rg = lax.rsqrt(jnp.mean(ug * ug, axis=-1, keepdims=True) + EPS)
                uh = ug * rg
                dg = dun[:, sl]
                du_parts.append(rg * (dg - uh * jnp.mean(dg * uh, axis=-1, keepdims=True)))
                uh_parts.append(uh)
            du = jnp.concatenate(du_parts, axis=1)
            uh = jnp.concatenate(uh_parts, axis=1)
            dy2 = du * sz
            dy_ref[0] = dy2.astype(BF)
            dz_ref[0] = (du * y2 * (sg * (1.0 + z * (1.0 - sg)))).astype(BF)
            acc_ref[0, 0:1, :] += jnp.sum(dyn * uh, axis=0, keepdims=True)
            acc_ref[0, 1:2, :] += jnp.sum(dy2 * xs, axis=0, keepdims=True)

    xmap = lambda c: (lambda e, t: (e, jnp.minimum(t, nx - 1), c))
    blk = lambda c: pl.BlockSpec((1, ROW_TILE, DIN), xmap(c))
    oblk = pl.BlockSpec((1, ROW_TILE, DIN), lambda e, t: (e, t, 0))
    vec = pl.BlockSpec((1, DIN), lambda e, t: (0, 0))
    return pl.pallas_call(
        body, name="ssd_post_bwd", grid=(2, R // ROW_TILE),
        in_specs=[blk(0), blk(0), blk(0), blk(0), blk(1), vec, vec],
        out_specs=(oblk, oblk, pl.BlockSpec((1, 8, DIN), lambda e, t: (e, 0, 0))),
        out_shape=(jax.ShapeDtypeStruct((2, R, DIN), BF), jax.ShapeDtypeStruct((2, R, DIN), BF),
                   jax.ShapeDtypeStruct((2, 8, DIN), F32)),
        compiler_params=_params(("parallel", "arbitrary")),
    )(d_yn, y_f, y_b, xbc, proj3, dskip_e, ssd_norm)


def _merge_fwd(proj3, P, S, b_merge, L):
    def body(gp_ref, p_ref, s_ref, b_ref, o_ref):
        gt = _sigmoid(gp_ref[0].astype(F32) + b_ref[...])
        o_ref[0] = (gt[:, :D] * p_ref[0].astype(F32) + gt[:, D:] * s_ref[0].astype(F32)).astype(BF)

    blk = pl.BlockSpec((1, ROW_TILE, D), lambda e, t: (e, t, 0))
    return pl.pallas_call(
        body, name="merge_fwd", grid=(2, L // ROW_TILE),
        in_specs=[pl.BlockSpec((1, ROW_TILE, 2 * D), lambda e, t: (e, t, OFF_GATE // (2 * D))), blk, blk,
                  pl.BlockSpec((1, 2 * D), lambda e, t: (0, 0))],
        out_specs=blk, out_shape=jax.ShapeDtypeStruct((2, L, D), BF),
        compiler_params=_params(("parallel", "parallel")),
    )(proj3, P, S, b_merge)


def _merge_bwd(d_merged, proj3, P, S, b_merge, L):
    _, R, _ = proj3.shape
    nx = L // ROW_TILE

    def body(dm_ref, gp_ref, p_ref, s_ref, b_ref, dp_ref, ds_ref, dg_ref, acc_ref):
        t = pl.program_id(1)

        @pl.when(t == 0)
        def _():
            acc_ref[...] = jnp.zeros_like(acc_ref)

        @pl.when(t >= nx)
        def _():
            dg_ref[...] = jnp.zeros_like(dg_ref)

        @pl.when(t < nx)
        def _():
            gt = _sigmoid(gp_ref[0].astype(F32) + b_ref[...])
            dm = dm_ref[0].astype(F32)
            g1, g2 = gt[:, :D], gt[:, D:]
            dp_ref[0] = (dm * g1).astype(BF)
            ds_ref[0] = (dm * g2).astype(BF)
            dgp = jnp.concatenate([dm * p_ref[0].astype(F32) * g1 * (1.0 - g1),
                                   dm * s_ref[0].astype(F32) * g2 * (1.0 - g2)], axis=1)
            dg_ref[0] = dgp.astype(BF)
            acc_ref[0, 0:1, :] += jnp.sum(dgp, axis=0, keepdims=True)

    xmap = lambda e, t: (e, jnp.minimum(t, nx - 1), 0)
    blk = pl.BlockSpec((1, ROW_TILE, D), xmap)
    return pl.pallas_call(
        body, name="merge_bwd", grid=(2, R // ROW_TILE),
        in_specs=[blk, pl.BlockSpec((1, ROW_TILE, 2 * D), lambda e, t: (e, jnp.minimum(t, nx - 1), OFF_GATE // (2 * D))),
                  blk, blk, pl.BlockSpec((1, 2 * D), lambda e, t: (0, 0))],
        out_specs=(blk, blk, pl.BlockSpec((1, ROW_TILE, 2 * D), lambda e, t: (e, t, 0)),
                   pl.BlockSpec((1, 8, 2 * D), lambda e, t: (e, 0, 0))),
        out_shape=(jax.ShapeDtypeStruct((2, L, D), BF), jax.ShapeDtypeStruct((2, L, D), BF),
                   jax.ShapeDtypeStruct((2, R, 2 * D), BF), jax.ShapeDtypeStruct((2, 8, 2 * D), F32)),
        compiler_params=_params(("parallel", "arbitrary")),
    )(d_merged, proj3, P, S, b_merge)


def _final(out3, x, tgt, gtab, norm_post, L):
    def body(o_ref, x_ref, t_ref, g_ref, n_ref, dxo_ref, do_ref, acc_ref):
        @pl.when(pl.program_id(1) == 0)
        def _():
            acc_ref[...] = jnp.zeros_like(acc_ref)

        o = o_ref[0].astype(F32)
        gate = g_ref[0, 0:1, :]
        npost = n_ref[...]
        r2 = lax.rsqrt(jnp.mean(o * o, axis=-1, keepdims=True) + EPS)
        nh = o * r2
        on = nh * npost
        err = x_ref[0] + gate * on - t_ref[0]
        dxo = err * (1.0 / D)
        dxo_ref[0] = dxo
        dnh = dxo * gate * npost
        do_ref[0] = (r2 * (dnh - nh * jnp.mean(dnh * nh, axis=-1, keepdims=True))).astype(BF)
        acc_ref[0, 0:1, :] += jnp.sum(dxo * on, axis=0, keepdims=True)
        acc_ref[0, 1:2, :] += jnp.sum(dxo * gate * nh, axis=0, keepdims=True)
        acc_ref[0, 2:3, :] += jnp.sum(err * err, axis=0, keepdims=True)

    blk = pl.BlockSpec((1, ROW_TILE, D), lambda e, t: (e, t, 0))
    return pl.pallas_call(
        body, name="final", grid=(2, L // ROW_TILE),
        in_specs=[blk, blk, blk, pl.BlockSpec((1, 8, D), lambda e, t: (e, 0, 0)),
                  pl.BlockSpec((1, D), lambda e, t: (0, 0))],
        out_specs=(blk, blk, pl.BlockSpec((1, 8, D), lambda e, t: (e, 0, 0))),
        out_shape=(jax.ShapeDtypeStruct((2, L, D), F32), jax.ShapeDtypeStruct((2, L, D), BF),
                   jax.ShapeDtypeStruct((2, 8, D), F32)),
        compiler_params=_params(("parallel", "arbitrary")),
    )(out3, x, tgt, gtab, norm_post)


def _local_step(x, c, ctx, loss_target, W, late_shard=None, exchange=False):
    nb, L, _ = x.shape
    LC = ctx.shape[1]
    R = L + LC
    assert nb == 2 and L % ROW_TILE == 0 and LC % Q == 0 and L % POOL_TILE == 0
    w_inT = W["w_in"]
    w_dtT = jnp.pad(w_inT[OFF_DT:], ((0, 64), (0, 0)))
    tables = _pool_tables(L)
    tr, tl = (2 * R) // 8, (2 * L) // 8

    c16 = jnp.zeros((16, D), F32).at[0:2].set(c).at[2].set(W["c_ctx"])
    mod16 = _adaln_fwd(c16, W["w_ada"], W["b_ada"])
    shift, scale, gate = mod16[:, :D], mod16[:, D:2 * D], mod16[:, 2 * D:]
    npre = W["norm_pre"]
    tab = jnp.zeros((2, 2, 8, D), F32)
    for e in range(2):
        tab = tab.at[e, 0, 0].set(npre[0] * (1.0 + scale[e])).at[e, 0, 1].set(shift[e])
        tab = tab.at[e, 1, 0].set(npre[0] * (1.0 + scale[2])).at[e, 1, 1].set(shift[2])
    gtab = jnp.zeros((2, 8, D), F32).at[:, 0].set(gate[0:2])

    hx = _norm_mod_fwd(x, ctx, tab)
    hx2 = hx.reshape(2 * R, D)
    if late_shard is None:
        proj = _matmul(hx2, w_inT, BF, "proj_main", tm=tr, tn=1024, bt=True, n=OFF_DT)
    else:
        proj, late = _matmul(hx2, w_inT, BF, "proj_main", tm=tr, tn=1024, bt=True, n=OFF_DT, side=_gather_side(late_shard))
        W = {**W, **_unpack_gather(late, GATHER_LATE)}
    proj3 = proj.reshape(2, R, OFF_DT)
    dt_raw = _matmul(hx2, w_dtT, F32, "proj_dt", tm=tr, bt=True).reshape(2, R, 128)
    ypool = _pool_fwd(proj3, W["pool_w"], W["pool_scale"], tables, L)
    xbc = _conv_fwd(proj3, W["conv_w"], W["conv_b"], L)
    bias128 = jnp.pad(W["dt_bias"].reshape(1, 64), ((0, 0), (0, 64)))
    dt_loc = _dt_fwd(dt_raw, bias128)
    A = -jnp.exp(W["a_log"].reshape(2, NG, HPG))
    a_loc = jnp.zeros((NG, 8, 128), F32).at[:, 0, :16].set(A.transpose(1, 0, 2).reshape(NG, 16))
    y_f, hs_f, y_b, hs_b = _ssd_fwd(xbc, dt_loc, a_loc, L)
    dskip_e = jnp.repeat(W["d_skip"].reshape(1, 32), HEAD, axis=1)
    yn = _ssd_post_fwd(y_f, y_b, xbc, proj3, dskip_e, W["ssd_norm"], L)
    ypool2, yn2 = ypool.reshape(2 * L, D), yn.reshape(2 * L, DIN)
    P = _matmul(ypool2, W["w_proj_pool"], BF, "proj_pool", tm=tl, tn=1024).reshape(2, L, D)
    S = _matmul(yn2, W["w_proj_ssd"], BF, "proj_ssd", tm=tl, tn=1024).reshape(2, L, D)
    merged = _merge_fwd(proj3, P, S, W["b_merge"], L)
    merged2 = merged.reshape(2 * L, D)
    out3 = _matmul(merged2, W["w_out"], BF, "proj_out", tm=tl, tn=1024).reshape(2, L, D)
    dxo, dout, acc_f = _final(out3, x, loss_target, gtab, W["norm_post"], L)

    dout2 = dout.reshape(2 * L, D)
    g = {}
    g["w_out"] = _matmul_tn(merged2, dout2, "dw_out", ta=1024, tn=1024, tr=2 * tl)
    d_merged = _matmul(dout2, W["w_out"], BF, "d_merged", tm=tl, tn=1024, bt=True).reshape(2, L, D)
    dP, dS, dgp, acc_m = _merge_bwd(d_merged, proj3, P, S, W["b_merge"], L)
    dP2, dS2 = dP.reshape(2 * L, D), dS.reshape(2 * L, D)
    g["w_proj_pool"] = _matmul_tn(ypool2, dP2, "dw_proj_pool", ta=1024, tn=1024, tr=2 * tl)
    g["w_proj_ssd"] = _matmul_tn(yn2, dS2, "dw_proj_ssd", ta=1024, tn=1024, tr=2 * tl)
    d_ypool = _matmul(dP2, W["w_proj_pool"], BF, "d_ypool", tm=tl, tn=1024, bt=True).reshape(2, L, D)
    d_yn = _matmul(dS2, W["w_proj_ssd"], BF, "d_yn", tm=tl, tn=1024, bt=True).reshape(2, L, DIN)
    dv, dzp, g["pool_w"], acc_p = _pool_bwd(proj3, d_ypool, W["pool_w"], jnp.swapaxes(W["pool_w"], 1, 2),
                                            W["pool_scale"], tables, L)
    dy2, dzs, acc_s = _ssd_post_bwd(d_yn, y_f, y_b, xbc, proj3, dskip_e, W["ssd_norm"], L)
    dxs_f, dbc_f, ddt_f, dxs_b, dbc_b, ddt_b, acc_a = _ssd_bwd(xbc, dt_loc, a_loc, hs_f, hs_b, y_f, y_b, dy2, L)
    ident = lambda j: j
    dxr_xs, acc_cx = _conv_bwd(proj3, [dxs_f, dxs_b, dy2], [None, None, dskip_e], 0, DIN, [ident, ident, ident],
                               W["conv_w"], W["conv_b"], L, "conv_bwd_xs")
    bcmap = lambda j: 2 * lax.rem(j, NG) + j // NG
    dxr_bc, acc_cb = _conv_bwd(proj3, [dbc_f, dbc_b], [None, None], DIN, 2 * NG * NST, [bcmap, bcmap],
                               W["conv_w"], W["conv_b"], L, "conv_bwd_bc")
    ddtr, acc_d = _dt_bwd(dt_raw, bias128, ddt_f, ddt_b)
    pieces = [dv, dzp, dzs, dgp, dxr_xs, dxr_bc]
    dw_rows = [_matmul_tn(p.reshape(2 * R, p.shape[2]), hx2, "dw_in_%d" % i, ta=1024, tn=1024, tr=2 * tr)
               for i, p in enumerate(pieces)]
    dw_rows.append(_matmul_tn(ddtr.reshape(2 * R, 128), hx2, "dw_in_dt", ta=128, tn=1024, tr=tr)[:64])
    g["w_in"] = jnp.concatenate(dw_rows, axis=0)
    acc_c = jnp.concatenate([acc_cx[0] + acc_cx[1], acc_cb[0] + acc_cb[1]], axis=1)
    g["conv_w"] = acc_c[0:4]
    g["conv_b"] = acc_c[4:5]
    if exchange:
        gb = _pack_grads(g, GRADS_EARLY)
        pair = _pair_add(gb, _pair_exchange(gb, None, "grads_pair_exchange_early"), "grads_pair_add_early")
        dh, recv_early = _dhx(pieces, ddtr, w_inT, w_dtT, side=_chip_exchange_side(pair))
    else:
        dh, recv_early = _dhx(pieces, ddtr, w_inT, w_dtT), None
    grad_x, acc_n = _norm_mod_bwd(dh, x, ctx, tab, dxo)
    g["w_ada"], db_rows, sm_rows = _adaln_bwd(acc_n, acc_f, mod16, c16, npre, W["w_ada"])

    g["b_ada"] = db_rows[0:1]
    g["norm_pre"] = sm_rows[0:1]
    g["c_ctx"] = sm_rows[1]
    g["norm_post"] = acc_f[0, 1:2] + acc_f[1, 1:2]
    g["b_merge"] = acc_m[0, 0:1] + acc_m[1, 0:1]
    g["pool_scale"] = acc_p[:, 0, :].reshape(1, D)
    g["dt_bias"] = (acc_d[0, 0, :64] + acc_d[1, 0, :64]).reshape(2, 32)
    dA = (acc_a[0, :, 0, :16] + acc_a[1, :, 0, :16]).reshape(NG, 2, HPG).transpose(1, 0, 2)
    g["a_log"] = (dA * A).reshape(2, 32)
    g["d_skip"] = (acc_s[0, 1] + acc_s[1, 1]).reshape(32, HEAD).sum(axis=1).reshape(1, 32)
    g["ssd_norm"] = acc_s[0, 0:1] + acc_s[1, 0:1]
    loss_lanes = acc_f[:, 2, :]
    return loss_lanes, grad_x, g, recv_early


MESH = pl.DeviceIdType.MESH
ANY = pl.BlockSpec(memory_space=pl.ANY)


def _all_gather(shard):
    m_per, n = shard.shape

    def body(x_ref, out_ref, send_sems, recv_sems, local_sem):
        x, y, c = lax.axis_index("x"), lax.axis_index("y"), lax.axis_index("c")
        me, sibling = (x, y, c), (x, y, 1 - c)
        chips = [(1 - x, y), (x, 1 - y), (1 - x, 1 - y)]

        def rows(px, py, pc):
            return out_ref.at[pl.ds((4 * px + 2 * py + pc) * m_per, m_per), :]

        def copy(k, block, to, src=None):
            return pltpu.make_async_remote_copy(
                src_ref=rows(*block) if src is None else src, dst_ref=rows(*block),
                send_sem=send_sems.at[k], recv_sem=recv_sems.at[k], device_id=to, device_id_type=MESH)

        mine = pltpu.make_async_copy(x_ref, rows(*me), local_sem)
        mine.start()
        first = [copy(0, me, sibling, src=x_ref)]
        first += [copy(1 + j, me, (*chip, c), src=x_ref) for j, chip in enumerate(chips)]
        for cp in first:
            cp.start()
        passed = [copy(4 + j, (*chip, c), sibling) for j, chip in enumerate(chips)]
        for j, chip in enumerate(chips):
            copy(1 + j, (*chip, c), me).wait_recv()
            passed[j].start()
        copy(0, sibling, me).wait_recv()
        for j, chip in enumerate(chips):
            copy(4 + j, (*chip, 1 - c), me).wait_recv()
        for cp in first + passed:
            cp.wait_send()
        mine.wait()

    return pl.pallas_call(
        body, name="all_gather_weights",
        out_shape=jax.ShapeDtypeStruct((NDEV * m_per, n), shard.dtype),
        in_specs=[ANY], out_specs=ANY,
        scratch_shapes=[pltpu.SemaphoreType.DMA((7,)), pltpu.SemaphoreType.DMA((7,)), pltpu.SemaphoreType.DMA],
    )(shard)


PAIR_PIECES = 4


def _xor_peer(k, x, y, c):
    return (1 - x if k & 4 else x, 1 - y if k & 2 else y, 1 - c if k & 1 else c)


def _pair_exchange(big, small, name):
    _, nq, rows, n = big.shape
    piece = rows // PAIR_PIECES
    assert piece * PAIR_PIECES == rows and piece % 16 == 0
    with_small = small is not None

    def body(*refs):
        if with_small:
            big_ref, small_ref, got_ref, osmall_ref, send_sems, recv_sems, local_sem = refs
        else:
            big_ref, got_ref, send_sems, recv_sems, local_sem = refs
        x, y, c = lax.axis_index("x"), lax.axis_index("y"), lax.axis_index("c")
        me = 4 * x + 2 * y + c

        def rc(src, dst, sem, peer):
            return pltpu.make_async_remote_copy(src_ref=src, dst_ref=dst, send_sem=send_sems.at[sem],
                                                recv_sem=recv_sems.at[sem], device_id=peer, device_id_type=MESH)

        sib = _xor_peer(1, x, y, c)
        local, sends, recvs = [], [], []
        for q in range(nq):
            for h in range(PAIR_PIECES):
                rws = pl.ds(h * piece, piece)
                cp = rc(big_ref.at[1 - c, q, rws], got_ref.at[q, rws], 8 + q * PAIR_PIECES + h, sib)
                sends.append(cp)
                recvs.append(cp)
        if with_small:
            local.append(pltpu.make_async_copy(small_ref, osmall_ref.at[me], local_sem))
            for k in range(1, NDEV):
                px, py, pc = _xor_peer(k, x, y, c)
                sends.append(rc(small_ref, osmall_ref.at[me], k, (px, py, pc)))
                recvs.append(rc(small_ref, osmall_ref.at[4 * px + 2 * py + pc], k, (px, py, pc)))
        for cp in local + sends:
            cp.start()
        for cp in sends:
            cp.wait_send()
        for cp in recvs:
            cp.wait_recv()
        for cp in local:
            cp.wait()

    nsem = 8 + nq * PAIR_PIECES
    out_shape = [jax.ShapeDtypeStruct(big.shape[1:], big.dtype)]
    if with_small:
        out_shape.append(jax.ShapeDtypeStruct((NDEV,) + small.shape, small.dtype))
    out = pl.pallas_call(
        body, name=name, out_shape=tuple(out_shape),
        in_specs=[ANY] * (1 + with_small), out_specs=(ANY,) * (1 + with_small),
        scratch_shapes=[pltpu.SemaphoreType.DMA((nsem,)), pltpu.SemaphoreType.DMA((nsem,)), pltpu.SemaphoreType.DMA],
    )(*((big, small) if with_small else (big,)))
    return out if with_small else out[0]


def _pair_add(big, got, name):
    _, nq, rows, n = big.shape
    tile = rows // 4
    assert rows % 64 == 0

    def body(c_ref, a_ref, b_ref, o_ref):
        o_ref[0] = (a_ref[0, 0].astype(F32) + b_ref[0].astype(F32)).astype(BF)

    blk = pl.BlockSpec((1, tile, n), lambda q, i, c_ref: (q, i, 0))
    return pl.pallas_call(
        body, name=name,
        grid_spec=pltpu.PrefetchScalarGridSpec(
            num_scalar_prefetch=1, grid=(nq, rows // tile),
            in_specs=[pl.BlockSpec((1, 1, tile, n), lambda q, i, c_ref: (c_ref[0], q, i, 0)), blk], out_specs=blk),
        out_shape=jax.ShapeDtypeStruct(got.shape, BF), compiler_params=_params(("parallel", "parallel")),
    )(lax.axis_index("c").astype(jnp.int32).reshape(1), big, got)


def _chip_exchange_side(pair):
    def make(in_refs, out_refs, send_sems, recv_sems, local_sem, arrivals=True):
        (in_ref,), (out_ref,) = in_refs, out_refs
        x, y, c = lax.axis_index("x"), lax.axis_index("y"), lax.axis_index("c")
        q = 2 * x + y
        local = [pltpu.make_async_copy(in_ref.at[q], out_ref.at[q], local_sem)]
        sends, recvs = [], []
        for j in range(1, 4):
            px, py, pc = _xor_peer(2 * j, x, y, c)
            pq = 2 * px + py
            for lst, dst in ((sends, out_ref.at[q]), (recvs, out_ref.at[pq]))[:1 + arrivals]:
                lst.append(pltpu.make_async_remote_copy(
                    src_ref=in_ref.at[pq], dst_ref=dst, send_sem=send_sems.at[j - 1], recv_sem=recv_sems.at[j - 1],
                    device_id=(px, py, pc), device_id_type=MESH))
        return local, sends, recvs

    return _SideCopies([pair], [jax.ShapeDtypeStruct(pair.shape, pair.dtype)], make)


def _gather_side(shard):
    def make(in_refs, out_refs, send_sems, recv_sems, local_sem, arrivals=True):
        (src,), (dst,) = in_refs, out_refs
        x, y, c = lax.axis_index("x"), lax.axis_index("y"), lax.axis_index("c")
        me = 4 * x + 2 * y + c
        local = [pltpu.make_async_copy(src, dst.at[me], local_sem)]
        sends, recvs = [], []
        for k in range(1, NDEV):
            px, py, pc = _xor_peer(k, x, y, c)
            for lst, slot in ((sends, me), (recvs, 4 * px + 2 * py + pc))[:1 + arrivals]:
                lst.append(pltpu.make_async_remote_copy(
                    src_ref=src, dst_ref=dst.at[slot], send_sem=send_sems.at[k - 1], recv_sem=recv_sems.at[k - 1],
                    device_id=(px, py, pc), device_id_type=MESH))
        return local, sends, recvs

    return _SideCopies([shard], [jax.ShapeDtypeStruct((NDEV,) + shard.shape, shard.dtype)], make)


ADAM_TILE = 64
PACK_W = 1024


def _adamw(recv, w, m, v, name, side=None):
    rp = w.shape[0]
    tile = min(ADAM_TILE, rp)
    nsrc = recv.shape[0]
    grid = (rp // tile,)
    n_si, n_so = (len(side.inputs), len(side.out_shapes)) if side else (0, 0)

    def body(*refs):
        r_ref, w_ref, m_ref, v_ref = refs[:4]
        g_ref, d_ref, nm_ref, nv_ref = refs[4 + n_si:8 + n_si]
        side_refs = (refs[4:4 + n_si], refs[8 + n_si:8 + n_si + n_so], refs[8 + n_si + n_so:])
        if side:
            side.start(grid, *side_refs)
        g = r_ref[0].astype(F32)
        for i in range(1, nsrc):
            g = g + r_ref[i].astype(F32)
        m1 = ADAM_B1 * m_ref[...] + (1.0 - ADAM_B1) * g
        v1 = ADAM_B2 * v_ref[...] + (1.0 - ADAM_B2) * (g * g)
        m_hat = m1 / (1.0 - ADAM_B1 ** ADAM_STEP)
        v_hat = v1 / (1.0 - ADAM_B2 ** ADAM_STEP)
        g_ref[...] = g
        d_ref[...] = -ADAM_LR * (m_hat / (jnp.sqrt(v_hat) + ADAM_EPS) + ADAM_WD * w_ref[...])
        nm_ref[...] = m1
        nv_ref[...] = v1
        if side:
            side.wait(grid, *side_refs)

    blk = pl.BlockSpec((tile, PACK_W), lambda i: (i, 0))
    shp = jax.ShapeDtypeStruct((rp, PACK_W), F32)
    return pl.pallas_call(
        body, name=name, grid=grid,
        in_specs=[pl.BlockSpec((nsrc, tile, PACK_W), lambda i: (0, i, 0)), blk, blk, blk] + [ANY] * n_si,
        out_specs=(blk, blk, blk, blk) + (ANY,) * n_so,
        out_shape=(shp, shp, shp, shp) + tuple(side.out_shapes if side else ()),
        scratch_shapes=side.scratch() if side else [],
        compiler_params=_params(("arbitrary",) if side else ("parallel",)),
    )(recv, w, m, v, *(side.inputs if side else ()))


BIG = {"w_ada": ((3 * D, D), 0), "pool_w": ((4, PGW, PGW), 1), "w_proj_pool": ((D, D), 0), "w_proj_ssd": ((DIN, D), 0),
       "w_out": ((D, D), 0), "w_in": ((IN_COLS, D), 0), "conv_w": ((4, CONV_DIM), 1)}
TRANSPOSED = ("w_ada", "w_in")
PACK_ROWS = {"w_ada": 384, "w_in": 1168, "conv_w": 16, "pool_w": 32, "w_proj_pool": 128, "w_proj_ssd": 256, "w_out": 128}
GATHER_EARLY = ("w_ada", "w_in", "conv_w")
GATHER_LATE = ("pool_w", "w_proj_pool", "w_proj_ssd", "w_out")
GRADS_LATE = ("w_ada",)
GRADS_EARLY = tuple(n for n in PACK_ROWS if n not in GRADS_LATE)
SMALL = {"c_ctx": (D,), "b_ada": (1, 3 * D), "norm_pre": (1, D), "norm_post": (1, D), "b_merge": (1, 2 * D),
         "pool_scale": (1, D), "conv_b": (1, CONV_DIM), "dt_bias": (2, 32), "a_log": (2, 32), "d_skip": (1, 32),
         "ssd_norm": (1, DIN)}
LOSS_SLOT = 128
assert all(_r % 16 == 0 for _r in PACK_ROWS.values())
SMALL_ROWS = 16


def _shard_shape(name):
    shape, ax = BIG[name]
    return tuple(s // NDEV if i == ax else s for i, s in enumerate(shape))


def _as_rows(t, rows):
    pad = [(0, 0)] * (t.ndim - 1) + [(0, rows * PACK_W - t.shape[-1])]
    return jnp.pad(t, pad).reshape(t.shape[:-1] + (rows, PACK_W))


def _shard_rows(t, name):
    sh, r = _shard_shape(name), PACK_ROWS[name]
    lead = t.shape[:t.ndim - len(sh)]
    if len(sh) == 2 and sh[1] == PACK_W:
        return jnp.pad(t, [(0, 0)] * len(lead) + [(0, r - sh[0]), (0, 0)])
    if int(np.prod(sh)) == r * PACK_W:
        return t.reshape(lead + (r, PACK_W))
    return _as_rows(t.reshape(lead + (-1,)), r)


def _to_chunks(full, name):
    shape, ax = BIG[name]
    split = shape[:ax] + (NDEV, shape[ax] // NDEV) + shape[ax + 1:]
    return _shard_rows(jnp.moveaxis(full.reshape(split), ax, 0), name)


def _from_chunks(chunks, name):
    shape, ax = BIG[name]
    return jnp.moveaxis(chunks.reshape((NDEV,) + _shard_shape(name)), 0, ax).reshape(shape)


def _rows_of(names):
    return sum(PACK_ROWS[n] for n in names)


def _pack_state(t, names):
    return jnp.concatenate([_shard_rows(t[n], n) for n in names], axis=0)


def _pack_small(t, loss_part=None):
    slot = jnp.zeros((LOSS_SLOT,), F32)
    if loss_part is not None:
        slot = slot.at[0].set(loss_part)
    return _as_rows(jnp.concatenate([t[n].reshape(-1) for n in SMALL] + [slot]), SMALL_ROWS)


def _pack_grads(g, names):
    big = jnp.concatenate([_to_chunks(g[n], n).astype(BF) for n in names], axis=1)
    return jnp.swapaxes(big.reshape(4, 2, _rows_of(names), PACK_W), 0, 1)


def _unpack_state(big, names):
    out, off = {}, 0
    for n in names:
        sh, r = _shard_shape(n), PACK_ROWS[n]
        k = int(np.prod(sh))
        if len(sh) == 2 and sh[1] == PACK_W:
            out[n] = big[off:off + sh[0]]
        else:
            out[n] = big[off:off + r].reshape(-1)[:k].reshape(sh)
        off += r
    return out


def _unpack_small(small):
    out, flat, off = {}, small.reshape(-1), 0
    for n, sh in SMALL.items():
        k = int(np.prod(sh))
        out[n] = flat[off:off + k].reshape(sh)
        off += k
    out["loss"] = flat[off]
    return out


def _pack_gather(w, names):
    pieces = []
    for n in names:
        if n == "conv_w":
            pieces.append(_as_rows(jnp.concatenate([p.reshape(-1) for p in _split(w[n], 3)]), PACK_ROWS[n]))
        else:
            pieces.append(_shard_rows(w[n], n).astype(BF))
    return jnp.concatenate(pieces, axis=0)


def _unpack_gather(gathered, names):
    g = gathered.reshape(NDEV, _rows_of(names), PACK_W)
    out, off = {}, 0
    for n in names:
        r = PACK_ROWS[n]
        sh = _shard_shape(n)
        if n == "conv_w":
            k = int(np.prod(sh))
            terms = g[:, off:off + r].reshape(NDEV, -1)[:, :3 * k].astype(F32).reshape(NDEV, 3, k)
            out[n] = _from_chunks(terms[:, 0] + terms[:, 1] + terms[:, 2], n)
        elif len(sh) == 2 and sh[1] == PACK_W:
            out[n] = _from_chunks(g[:, off:off + sh[0]], n)
        else:
            out[n] = _from_chunks(g[:, off:off + r], n)
        off += r
    return out


PARAMS = ["c_ctx", "w_ada", "b_ada", "norm_pre", "norm_post", "w_in", "b_merge", "pool_w", "pool_scale", "conv_w", "conv_b",
          "dt_bias", "a_log", "d_skip", "ssd_norm", "w_proj_pool", "w_proj_ssd", "w_out"]


def kernel(x, c, ctx, c_ctx, w_ada, b_ada, norm_pre, norm_post, w_in, b_merge, pool_w, pool_scale, conv_w, conv_b, dt_bias, a_log, d_skip, ssd_norm, w_proj_pool, w_proj_ssd, w_out, loss_target, m_c_ctx, m_w_ada, m_b_ada, m_norm_pre, m_norm_post, m_w_in, m_b_merge, m_pool_w, m_pool_scale, m_conv_w, m_conv_b, m_dt_bias, m_a_log, m_d_skip, m_ssd_norm, m_w_proj_pool, m_w_proj_ssd, m_w_out, v_c_ctx, v_w_ada, v_b_ada, v_norm_pre, v_norm_post, v_w_in, v_b_merge, v_pool_w, v_pool_scale, v_conv_w, v_conv_b, v_dt_bias, v_a_log, v_d_skip, v_ssd_norm, v_w_proj_pool, v_w_proj_ssd, v_w_out):
    given = dict(locals())
    shapes = {n: given[n].shape for n in PARAMS}

    def local(prefix):
        t = {n: (given[prefix + n] if n == "c_ctx" else given[prefix + n][0]) for n in PARAMS}
        for n in TRANSPOSED:
            t[n] = t[n].T
        return {n: t[n].reshape(_shard_shape(n) if n in BIG else SMALL[n]) for n in PARAMS}

    w, m, v = local(""), local("m_"), local("v_")

    W = _unpack_gather(_all_gather(_pack_gather(w, GATHER_EARLY)), GATHER_EARLY)
    for n in SMALL:
        W[n] = w[n]
    lanes, grad_x, g, recv_early = _local_step(x, c, ctx, loss_target, W, late_shard=_pack_gather(w, GATHER_LATE),
                                               exchange=True)
    gb = _pack_grads(g, GRADS_LATE)
    got, recv_small = _pair_exchange(gb, _pack_small(g, (0.5 / D) * jnp.sum(lanes)), "grads_pair_exchange_late")
    late = _chip_exchange_side(_pair_add(gb, got, "grads_pair_add_late"))
    res = [{} for _ in range(4)]
    *early, recv_late = _adamw(recv_early, *[_pack_state(s, GRADS_EARLY) for s in (w, m, v)], "adamw_early", side=late)
    for r, t in zip(res, early):
        r.update(_unpack_state(t, GRADS_EARLY))
    for r, t in zip(res, _adamw(recv_late, *[_pack_state(s, GRADS_LATE) for s in (w, m, v)], "adamw_late")):
        r.update(_unpack_state(t, GRADS_LATE))
    for r, t in zip(res, _adamw(recv_small, *[_pack_small(s) for s in (w, m, v)], "adamw_small")):
        r.update(_unpack_small(t))
    outs = [res[0]["loss"], grad_x]
    for r in res:
        for n in TRANSPOSED:
            r[n] = r[n].T
        outs += [r[n].reshape(shapes[n]) for n in PARAMS]
    return tuple(outs)
```

```python
import functools

import numpy as np
import jax
import jax.numpy as jnp
from jax import lax
from jax.experimental import pallas as pl
from jax.experimental.pallas import tpu as pltpu

F32, BF = jnp.float32, jnp.bfloat16

D = 1024
GRID_W = 64
EPS = 1e-6
POOL_WINDOWS = (2, 4, 8, 16)
PGW = 256
DIN = 2048
HEAD = 64
NST = 128
NG = 4
HPG = 8
GWID = HPG * HEAD
Q = 128
CONV_DIM = 3072
OFF_GATE, OFF_XBC, OFF_DT, IN_COLS = 4096, 6144, 9216, 9280
NDEV = 8
ADAM_LR, ADAM_B1, ADAM_B2, ADAM_EPS, ADAM_WD, ADAM_STEP = 0.001, 0.9, 0.999, 1e-08, 0.01, 10

V7X_VMEM_LIMIT = 56 * 2 ** 20
ROW_TILE = 256


def _params(sem=None):
    return pltpu.CompilerParams(dimension_semantics=sem, vmem_limit_bytes=V7X_VMEM_LIMIT)


def _dot(a, b):
    return jnp.dot(a.astype(BF), b.astype(BF), preferred_element_type=F32)


def _dot_nt(a, b):
    return lax.dot_general(a.astype(BF), b.astype(BF), (((1,), (1,)), ((), ())), preferred_element_type=F32)


def _dot_tn(a, b):
    return lax.dot_general(a.astype(BF), b.astype(BF), (((0,), (0,)), ((), ())), preferred_element_type=F32)


def _split(a, n):
    parts = []
    for _ in range(n):
        p = a.astype(BF)
        parts.append(p)
        a = a - p.astype(F32)
    return parts


def _dot_sl(a, b01, n=3):
    parts = _split(a, n)
    m = a.shape[0]
    if n == 1 or m % 16:
        return sum(jnp.dot(p, b01, preferred_element_type=F32) for p in parts)
    r = jnp.dot(jnp.concatenate(parts, axis=0), b01, preferred_element_type=F32)
    return sum(r[i * m:(i + 1) * m] for i in range(n))


def _dot_sr(a01, b, n=3):
    parts = _split(b, n)
    k = b.shape[1]
    if n == 1 or k % 128:
        return sum(jnp.dot(a01, p, preferred_element_type=F32) for p in parts)
    r = jnp.dot(a01, jnp.concatenate(parts, axis=1), preferred_element_type=F32)
    return sum(r[:, i * k:(i + 1) * k] for i in range(n))


def _sigmoid(x):
    return 1.0 / (1.0 + jnp.exp(-x))


class _SideCopies:
    NSEM = 8

    def __init__(self, inputs, out_shapes, make):
        self.inputs, self.out_shapes, self.make = list(inputs), list(out_shapes), make

    def scratch(self):
        return [pltpu.SemaphoreType.DMA((self.NSEM,)), pltpu.SemaphoreType.DMA((self.NSEM,)), pltpu.SemaphoreType.DMA]

    def start(self, grid, in_refs, out_refs, sems):
        @pl.when(functools.reduce(lambda p, q: p & q, [pl.program_id(i) == 0 for i in range(len(grid))]))
        def _():
            local, sends, _ = self.make(in_refs, out_refs, *sems, arrivals=False)
            for cp in local + sends:
                cp.start()

    def wait(self, grid, in_refs, out_refs, sems):
        @pl.when(functools.reduce(lambda p, q: p & q, [pl.program_id(i) == n - 1 for i, n in enumerate(grid)]))
        def _():
            local, sends, recvs = self.make(in_refs, out_refs, *sems)
            for cp in sends:
                cp.wait_send()
            for cp in recvs:
                cp.wait_recv()
            for cp in local:
                cp.wait()


def _matmul(a, b, out_dtype, name, tm=512, tn=512, tk=1024, bt=False, n=None, side=None):
    M, K = a.shape
    N = n if n is not None else (b.shape[0] if bt else b.shape[1])
    tm, tn, tk = min(tm, M), min(tn, N), min(tk, K)
    assert M % tm == 0 and N % tn == 0 and K % tk == 0, (a.shape, b.shape)
    nk = K // tk
    grid = (M // tm, N // tn, nk)
    n_si, n_so = (len(side.inputs), len(side.out_shapes)) if side else (0, 0)

    def body(*refs):
        a_ref, b_ref, o_ref = refs[0], refs[1], refs[2 + n_si]
        acc = refs[3 + n_si + n_so]
        side_refs = (refs[2:2 + n_si], refs[3 + n_si:3 + n_si + n_so], refs[4 + n_si + n_so:])
        if side:
            side.start(grid, *side_refs)
        k = pl.program_id(2)
        p = _dot_nt(a_ref[...], b_ref[...]) if bt else _dot(a_ref[...], b_ref[...])

        @pl.when(k == 0)
        def _():
            acc[...] = p

        @pl.when(k > 0)
        def _():
            acc[...] += p

        @pl.when(k == nk - 1)
        def _():
            o_ref[...] = acc[...].astype(o_ref.dtype)

        if side:
            side.wait(grid, *side_refs)

    out = pl.pallas_call(
        body, name=name, grid=grid,
        in_specs=[pl.BlockSpec((tm, tk), lambda i, j, k: (i, k)),
                  pl.BlockSpec((tn, tk), lambda i, j, k: (j, k)) if bt else pl.BlockSpec((tk, tn), lambda i, j, k: (k, j))]
        + [ANY] * n_si,
        out_specs=(pl.BlockSpec((tm, tn), lambda i, j, k: (i, j)),) + (ANY,) * n_so,
        out_shape=(jax.ShapeDtypeStruct((M, N), out_dtype),) + tuple(side.out_shapes if side else ()),
        scratch_shapes=[pltpu.VMEM((tm, tn), F32)] + (side.scratch() if side else []),
        compiler_params=_params(("arbitrary",) * 3 if side else ("parallel", "parallel", "arbitrary")),
    )(a, b, *(side.inputs if side else ()))
    return out if side else out[0]


def _matmul_tn(a, g, name, ta=512, tn=512, tr=512):
    M, Ka = a.shape
    N = g.shape[1]
    ta, tn, tr = min(ta, Ka), min(tn, N), min(tr, M)
    assert M % tr == 0 and N % tn == 0 and Ka % ta == 0, (a.shape, g.shape)
    nr = M // tr

    def body(a_ref, g_ref, o_ref):
        k = pl.program_id(2)
        p = _dot_tn(a_ref[...], g_ref[...])

        @pl.when(k == 0)
        def _():
            o_ref[...] = p

        @pl.when(k > 0)
        def _():
            o_ref[...] += p

    return pl.pallas_call(
        body, name=name, grid=(Ka // ta, N // tn, nr),
        in_specs=[pl.BlockSpec((tr, ta), lambda i, j, k: (k, i)), pl.BlockSpec((tr, tn), lambda i, j, k: (k, j))],
        out_specs=pl.BlockSpec((ta, tn), lambda i, j, k: (i, j)),
        out_shape=jax.ShapeDtypeStruct((Ka, N), F32),
        compiler_params=_params(("parallel", "parallel", "arbitrary")),
    )(a, g)


def _dhx(pieces, ddt, w_inT, w_dtT, side=None):
    _, R, _ = pieces[0].shape
    tm = R // 4
    kb = 1024
    starts, nblk = [], []
    for p in pieces:
        starts.append(sum(nblk))
        nblk.append(p.shape[2] // kb)
    nk = sum(nblk)
    assert nk * kb == OFF_DT and R % 128 == 0
    npc = len(pieces)
    grid = (2, R // tm, nk)
    n_si, n_so = (len(side.inputs), len(side.out_shapes)) if side else (0, 0)

    def body(*refs):
        a_refs, dt_ref, w_ref, wdt_ref = refs[:npc], refs[npc], refs[npc + 1], refs[npc + 2]
        o_ref, acc = refs[npc + 3 + n_si], refs[npc + 4 + n_si + n_so]
        side_refs = (refs[npc + 3:npc + 3 + n_si], refs[npc + 4 + n_si:npc + 4 + n_si + n_so], refs[npc + 5 + n_si + n_so:])
        if side:
            side.start(grid, *side_refs)
        k = pl.program_id(2)

        @pl.when(k == 0)
        def _():
            acc[...] = _dot(dt_ref[0], wdt_ref[...])

        for p in range(npc):
            @pl.when((k >= starts[p]) & (k < starts[p] + nblk[p]))
            def _(p=p):
                acc[...] += _dot(a_refs[p][0], w_ref[...])

        @pl.when(k == nk - 1)
        def _():
            o_ref[0] = acc[...].astype(BF)

        if side:
            side.wait(grid, *side_refs)

    in_specs = [pl.BlockSpec((1, tm, kb), functools.partial(
        lambda e, t, k, s, nb: (e, t, jnp.clip(k - s, 0, nb - 1)), s=starts[p], nb=nblk[p])) for p in range(npc)]
    in_specs += [pl.BlockSpec((1, tm, 128), lambda e, t, k: (e, t, 0)),
                 pl.BlockSpec((kb, D), lambda e, t, k: (k, 0)),
                 pl.BlockSpec((128, D), lambda e, t, k: (0, 0))]
    out = pl.pallas_call(
        body, name="d_hx", grid=grid, in_specs=in_specs + [ANY] * n_si,
        out_specs=(pl.BlockSpec((1, tm, D), lambda e, t, k: (e, t, 0)),) + (ANY,) * n_so,
        out_shape=(jax.ShapeDtypeStruct((2, R, D), BF),) + tuple(side.out_shapes if side else ()),
        scratch_shapes=[pltpu.VMEM((tm, D), F32)] + (side.scratch() if side else []),
        compiler_params=_params(("arbitrary",) * 3 if side else ("parallel", "parallel", "arbitrary")),
    )(*pieces, ddt, w_inT, w_dtT, *(side.inputs if side else ()))
    return out if side else out[0]


def _adaln_fwd(c16, w_adaT_bf, b_ada):
    def body(c_ref, w_ref, b_ref, o_ref):
        cc = c_ref[...]
        o_ref[...] = _dot_nt(cc * _sigmoid(cc), w_ref[...]) + b_ref[...]

    return pl.pallas_call(body, name="adaln_fwd", out_shape=jax.ShapeDtypeStruct((16, 3 * D), F32),
                          compiler_params=_params())(c16, w_adaT_bf, b_ada)


def _adaln_bwd(acc_n, acc_f, mod16, c16, norm_pre, w_adaT_bf):
    def body(an_ref, af_ref, mod_ref, c_ref, np_ref, wt_ref, dw_ref, db_ref, sm_ref, dmod):
        npre = np_ref[...]
        dmod[...] = jnp.zeros_like(dmod)
        dnp = jnp.zeros((1, D), F32)
        dshift_c = jnp.zeros((1, D), F32)
        dgpre_c = jnp.zeros((1, D), F32)
        scale_c = mod_ref[2:3, D:2 * D]
        for e in range(2):
            dg_x, ds_x = an_ref[e, 0, 0:1, :], an_ref[e, 0, 1:2, :]
            dg_c, ds_c = an_ref[e, 1, 0:1, :], an_ref[e, 1, 1:2, :]
            dmod[e:e + 1, 0:D] = ds_x
            dmod[e:e + 1, D:2 * D] = dg_x * npre
            dmod[e:e + 1, 2 * D:3 * D] = af_ref[e, 0:1, :]
            dnp = dnp + dg_x * (1.0 + mod_ref[e:e + 1, D:2 * D]) + dg_c * (1.0 + scale_c)
            dshift_c = dshift_c + ds_c
            dgpre_c = dgpre_c + dg_c
        dmod[2:3, 0:D] = dshift_c
        dmod[2:3, D:2 * D] = dgpre_c * npre
        dm = dmod[...]
        cc = c_ref[...]
        sg = _sigmoid(cc)
        dw_ref[...] = _dot_tn(dm, cc * sg)
        db_ref[...] = jnp.zeros_like(db_ref)
        db_ref[0:1, :] = jnp.sum(dm, axis=0, keepdims=True)
        dsilu = sg * (1.0 + cc * (1.0 - sg))
        dcs = _dot(dm, wt_ref[...]) * dsilu
        sm_ref[...] = jnp.zeros_like(sm_ref)
        sm_ref[0:1, :] = dnp
        sm_ref[1:2, :] = dcs[2:3, :]

    return pl.pallas_call(
        body, name="adaln_bwd",
        out_shape=(jax.ShapeDtypeStruct((3 * D, D), F32), jax.ShapeDtypeStruct((16, 3 * D), F32),
                   jax.ShapeDtypeStruct((8, D), F32)),
        scratch_shapes=[pltpu.VMEM((16, 3 * D), F32)],
        compiler_params=_params())(acc_n, acc_f, mod16, c16, norm_pre, w_adaT_bf)


def _row_specs(L):
    nx = L // ROW_TILE
    return (pl.BlockSpec((1, ROW_TILE, D), lambda e, t: (e, jnp.minimum(t, nx - 1), 0)),
            pl.BlockSpec((1, ROW_TILE, D), lambda e, t: (e, jnp.maximum(t - nx, 0), 0)))


def _norm_mod_fwd(x, ctx, tab):
    L = x.shape[1]
    R = L + ctx.shape[1]
    nx = L // ROW_TILE

    def body(x_ref, c_ref, t_ref, o_ref):
        x = jnp.where(pl.program_id(1) < nx, x_ref[0], c_ref[0])
        r = lax.rsqrt(jnp.mean(x * x, axis=-1, keepdims=True) + EPS)
        t = t_ref[0, 0]
        o_ref[0] = (x * r * t[0:1] + t[1:2]).astype(BF)

    return pl.pallas_call(
        body, name="norm_mod_fwd", grid=(2, R // ROW_TILE),
        in_specs=[*_row_specs(L), pl.BlockSpec((1, 1, 8, D), lambda e, t: (e, t // nx, 0, 0))],
        out_specs=pl.BlockSpec((1, ROW_TILE, D), lambda e, t: (e, t, 0)),
        out_shape=jax.ShapeDtypeStruct((2, R, D), BF),
        compiler_params=_params(("parallel", "parallel")),
    )(x, ctx, tab)


def _norm_mod_bwd(dh, x, ctx, tab, dxo):
    L = x.shape[1]
    R = L + ctx.shape[1]
    nx = L // ROW_TILE

    def body(dh_ref, x_ref, c_ref, t_ref, dxo_ref, gx_ref, acc_ref):
        t = pl.program_id(1)
        x = jnp.where(t < nx, x_ref[0], c_ref[0])
        r = lax.rsqrt(jnp.mean(x * x, axis=-1, keepdims=True) + EPS)
        xn = x * r
        dh = dh_ref[0].astype(F32)

        @pl.when((t == 0) | (t == nx))
        def _():
            acc_ref[...] = jnp.zeros_like(acc_ref)

        acc_ref[0, 0, 0:1, :] += jnp.sum(dh * xn, axis=0, keepdims=True)
        acc_ref[0, 0, 1:2, :] += jnp.sum(dh, axis=0, keepdims=True)

        @pl.when(t < nx)
        def _():
            dxn = dh * t_ref[0, 0][0:1]
            dx = r * (dxn - xn * jnp.mean(dxn * xn, axis=-1, keepdims=True))
            gx_ref[0] = dxo_ref[0] + dx

    xspec, cspec = _row_specs(L)
    return pl.pallas_call(
        body, name="norm_mod_bwd", grid=(2, R // ROW_TILE),
        in_specs=[pl.BlockSpec((1, ROW_TILE, D), lambda e, t: (e, t, 0)), xspec, cspec,
                  pl.BlockSpec((1, 1, 8, D), lambda e, t: (e, t // nx, 0, 0)), xspec],
        out_specs=(xspec, pl.BlockSpec((1, 1, 8, D), lambda e, t: (e, t // nx, 0, 0))),
        out_shape=(jax.ShapeDtypeStruct((2, L, D), F32), jax.ShapeDtypeStruct((2, 2, 8, D), F32)),
        compiler_params=_params(("parallel", "arbitrary")),
    )(dh, x, ctx, tab, dxo)


POOL_TILE = 256


def _pool_tables(L):
    rows = L // GRID_W
    mats = np.zeros((4, POOL_TILE, POOL_TILE), np.float32)
    inv = np.zeros((4, L, 1), np.float32)
    for gi, k in enumerate(POOL_WINDOWS):
        lo, hi = k // 2, k - 1 - k // 2
        m = np.zeros((GRID_W, GRID_W), np.float32)
        for t in range(GRID_W):
            m[t, max(t - lo, 0):min(t + hi, GRID_W - 1) + 1] = 1.0
        for b in range(POOL_TILE // GRID_W):
            mats[gi, b * GRID_W:(b + 1) * GRID_W, b * GRID_W:(b + 1) * GRID_W] = m
        cnt_c = m.sum(1)
        cnt_r = np.array([min(r + hi, rows - 1) - max(r - lo, 0) + 1 for r in range(rows)], np.float32)
        inv[gi, :, 0] = (1.0 / (cnt_r[:, None] * cnt_c[None, :])).reshape(-1)
    matsT = np.ascontiguousarray(np.transpose(mats, (0, 2, 1)))
    return (jnp.asarray(mats, BF), jnp.asarray(matsT, BF), jnp.asarray(inv))


def _pool_cols(get_tile, mat, cs_ref, L, n):
    def step(i, carry):
        off = pl.multiple_of(i * POOL_TILE, POOL_TILE)
        t = get_tile(off)
        cs_ref[pl.ds(GRID_W + off, POOL_TILE), :] = (jnp.dot(mat, t.astype(BF), preferred_element_type=F32) if n == 1
                                                     else _dot_sr(mat, t.astype(F32), n))
        return carry

    lax.fori_loop(0, L // POOL_TILE, step, 0)
    cs_ref[pl.ds(0, GRID_W), :] = jnp.zeros((GRID_W, PGW), F32)

    def prefix(r, carry):
        o = pl.multiple_of(r * GRID_W, GRID_W)
        cs_ref[pl.ds(o + GRID_W, GRID_W), :] = cs_ref[pl.ds(o + GRID_W, GRID_W), :] + cs_ref[pl.ds(o, GRID_W), :]
        return carry

    lax.fori_loop(0, L // GRID_W, prefix, 0)


def _pool_rows(cs_ref, off, below, above, L):
    rows = L // GRID_W
    r0 = off // GRID_W
    parts = []
    for i in range(POOL_TILE // GRID_W):
        hi = pl.multiple_of(jnp.minimum(r0 + i + above + 1, rows) * GRID_W, GRID_W)
        lo = pl.multiple_of(jnp.maximum(r0 + i - below, 0) * GRID_W, GRID_W)
        parts.append(cs_ref[pl.ds(hi, GRID_W), :] - cs_ref[pl.ds(lo, GRID_W), :])
    return jnp.concatenate(parts, axis=0)


def _pool_fwd(proj3, pool_w_bf, pool_scale, tables, L):
    mats, _, inv = tables
    nt = L // POOL_TILE

    def body(v_ref, z_ref, pw_ref, ps_ref, m_ref, inv_ref, o_ref, cs_ref):
        _pool_cols(lambda off: v_ref[0, pl.ds(off, POOL_TILE), :], m_ref[0], cs_ref, L, 1)
        half = lax.shift_left(1, pl.program_id(1))

        def step(i, carry):
            off = pl.multiple_of(i * POOL_TILE, POOL_TILE)
            rows = pl.ds(off, POOL_TILE)
            v = v_ref[0, rows, :].astype(F32)
            diff = _pool_rows(cs_ref, off, half, half - 1, L) * inv_ref[0, rows, :] - v
            yp = _dot(diff, pw_ref[0])
            z = z_ref[0, rows, :].astype(F32)
            o_ref[0, rows, :] = (yp * ps_ref[...] * (z * _sigmoid(z))).astype(BF)
            return carry

        lax.fori_loop(0, nt, step, 0)

    return pl.pallas_call(
        body, name="pool_fwd", grid=(2, 4),
        in_specs=[pl.BlockSpec((1, L, PGW), lambda e, g: (e, 0, g)),
                  pl.BlockSpec((1, L, PGW), lambda e, g: (e, 0, 4 + g)),
                  pl.BlockSpec((1, PGW, PGW), lambda e, g: (g, 0, 0)),
                  pl.BlockSpec((1, PGW), lambda e, g: (0, g)),
                  pl.BlockSpec((1, POOL_TILE, POOL_TILE), lambda e, g: (g, 0, 0)),
                  pl.BlockSpec((1, L, 1), lambda e, g: (g, 0, 0))],
        out_specs=pl.BlockSpec((1, L, PGW), lambda e, g: (e, 0, g)),
        out_shape=jax.ShapeDtypeStruct((2, L, D), BF),
        scratch_shapes=[pltpu.VMEM((L + GRID_W, PGW), F32)],
        compiler_params=_params(("parallel", "parallel")),
    )(proj3, proj3, pool_w_bf, pool_scale, mats, inv)


def _pool_bwd(proj3, d_ypool, pool_w_bf, pool_wT_bf, pool_scale, tables, L):
    mats, matsT, inv = tables
    nt = L // POOL_TILE
    R = proj3.shape[1]

    def body(v_ref, z_ref, dy_ref, pw_ref, pwt_ref, ps_ref, m_ref, mt_ref, inv_ref,
             dv_ref, dz_ref, dpw_ref, acc_ref, cs_ref, dd_ref):
        e = pl.program_id(1)

        @pl.when(e == 0)
        def _():
            dpw_ref[...] = jnp.zeros_like(dpw_ref)
            acc_ref[...] = jnp.zeros_like(acc_ref)

        _pool_cols(lambda off: v_ref[0, pl.ds(off, POOL_TILE), :], m_ref[0], cs_ref, L, 1)
        half = lax.shift_left(1, pl.program_id(0))
        ps = ps_ref[...]

        def step(i, carry):
            off = pl.multiple_of(i * POOL_TILE, POOL_TILE)
            rows = pl.ds(off, POOL_TILE)
            v = v_ref[0, rows, :].astype(F32)
            diff = _pool_rows(cs_ref, off, half, half - 1, L) * inv_ref[0, rows, :] - v
            yp = _dot(diff, pw_ref[0])
            z = z_ref[0, rows, :].astype(F32)
            sg = _sigmoid(z)
            sz = z * sg
            dy = dy_ref[0, rows, :].astype(F32)
            dz_ref[0, rows, :] = (dy * yp * ps * (sg * (1.0 + z * (1.0 - sg)))).astype(BF)
            dys = dy * sz
            acc_ref[0, 0:1, :] += jnp.sum(dys * yp, axis=0, keepdims=True)
            dyp = dys * ps
            dpw_ref[0] += _dot_tn(diff, dyp)
            dd_ref[rows, :] = _dot(dyp, pwt_ref[0])
            return carry

        lax.fori_loop(0, nt, step, 0)
        _pool_cols(lambda off: dd_ref[pl.ds(off, POOL_TILE), :] * inv_ref[0, pl.ds(off, POOL_TILE), :],
                   mt_ref[0], cs_ref, L, 1)

        def step2(i, carry):
            off = pl.multiple_of(i * POOL_TILE, POOL_TILE)
            rows = pl.ds(off, POOL_TILE)
            dv_ref[0, rows, :] = (_pool_rows(cs_ref, off, half - 1, half, L) - dd_ref[rows, :]).astype(BF)
            return carry

        lax.fori_loop(0, nt, step2, 0)
        dv_ref[0, pl.ds(L, R - L), :] = jnp.zeros((R - L, PGW), BF)
        dz_ref[0, pl.ds(L, R - L), :] = jnp.zeros((R - L, PGW), BF)

    return pl.pallas_call(
        body, name="pool_bwd", grid=(4, 2),
        in_specs=[pl.BlockSpec((1, L, PGW), lambda g, e: (e, 0, g)),
                  pl.BlockSpec((1, L, PGW), lambda g, e: (e, 0, 4 + g)),
                  pl.BlockSpec((1, L, PGW), lambda g, e: (e, 0, g)),
                  pl.BlockSpec((1, PGW, PGW), lambda g, e: (g, 0, 0)),
                  pl.BlockSpec((1, PGW, PGW), lambda g, e: (g, 0, 0)),
                  pl.BlockSpec((1, PGW), lambda g, e: (0, g)),
                  pl.BlockSpec((1, POOL_TILE, POOL_TILE), lambda g, e: (g, 0, 0)),
                  pl.BlockSpec((1, POOL_TILE, POOL_TILE), lambda g, e: (g, 0, 0)),
                  pl.BlockSpec((1, L, 1), lambda g, e: (g, 0, 0))],
        out_specs=(pl.BlockSpec((1, R, PGW), lambda g, e: (e, 0, g)),
                   pl.BlockSpec((1, R, PGW), lambda g, e: (e, 0, g)),
                   pl.BlockSpec((1, PGW, PGW), lambda g, e: (g, 0, 0)),
                   pl.BlockSpec((1, 8, PGW), lambda g, e: (g, 0, 0))),
        out_shape=(jax.ShapeDtypeStruct((2, R, D), BF), jax.ShapeDtypeStruct((2, R, D), BF),
                   jax.ShapeDtypeStruct((4, PGW, PGW), F32), jax.ShapeDtypeStruct((4, 8, PGW), F32)),
        scratch_shapes=[pltpu.VMEM((L + GRID_W, PGW), F32), pltpu.VMEM((L, PGW), F32)],
        compiler_params=_params(("parallel", "arbitrary")),
    )(proj3, proj3, d_ypool, pool_w_bf, pool_wT_bf, pool_scale, mats, matsT, inv)


CONV_BLOCK = 128


CONV_CHUNK = 64
CONV_HALO = 8


def _halo_buf_init(buf, val, R):
    z = jnp.zeros((CONV_HALO, CONV_BLOCK), F32)
    buf[pl.ds(0, CONV_HALO), :] = z
    buf[pl.ds(CONV_HALO + R, CONV_HALO), :] = z
    if val is not None:
        buf[pl.ds(CONV_HALO, R), :] = val


def _chunk_taps(buf, start, offs, L):
    n = CONV_CHUNK + 2 * CONV_HALO
    ext = buf[pl.ds(start, n), :]
    out = []
    for off in offs:
        if off == 0:
            out.append(ext[CONV_HALO:CONV_HALO + CONV_CHUNK])
            continue
        r = pltpu.roll(ext, (-off) % n, 0)[CONV_HALO:CONV_HALO + CONV_CHUNK]
        lo, hi = (start, start + CONV_CHUNK - 1 + off) if off > 0 else (start + off, start + CONV_CHUNK - 1)
        if lo < L <= hi:
            t = start + lax.broadcasted_iota(jnp.int32, (CONV_CHUNK, 1), 0)
            r = jnp.where((t < L) == (t + off < L), r, 0.0)
        out.append(r)
    return out


def _fold8(x):
    return sum(x[i * 8:(i + 1) * 8] for i in range(CONV_CHUNK // 8))


def _conv_fwd(proj3, conv_w, conv_b, L):
    _, R, _ = proj3.shape
    cb0 = OFF_XBC // CONV_BLOCK

    def body(u_ref, w_ref, b_ref, o_ref, ubuf):
        _halo_buf_init(ubuf, u_ref[0].astype(F32), R)
        w = w_ref[...]
        b = b_ref[...]
        for start in range(0, R, CONV_CHUNK):
            taps = _chunk_taps(ubuf, start, (-2, -1, 0, 1), L)
            pre = b + sum(taps[k] * w[k:k + 1, :] for k in range(4))
            o_ref[0, pl.ds(start, CONV_CHUNK), :] = (pre * _sigmoid(pre)).astype(BF)

    return pl.pallas_call(
        body, name="conv_fwd", grid=(2, CONV_DIM // CONV_BLOCK),
        in_specs=[pl.BlockSpec((1, R, CONV_BLOCK), lambda e, j: (e, 0, cb0 + j)),
                  pl.BlockSpec((4, CONV_BLOCK), lambda e, j: (0, j)),
                  pl.BlockSpec((1, CONV_BLOCK), lambda e, j: (0, j))],
        out_specs=pl.BlockSpec((1, R, CONV_BLOCK), lambda e, j: (e, 0, j)),
        out_shape=jax.ShapeDtypeStruct((2, R, CONV_DIM), BF),
        scratch_shapes=[pltpu.VMEM((R + 2 * CONV_HALO, CONV_BLOCK), F32)],
        compiler_params=_params(("parallel", "parallel")),
    )(proj3, conv_w, conv_b)


def _conv_bwd(proj3, addends, scales, col0, ncols, in_maps, conv_w, conv_b, L, name):
    _, R, _ = proj3.shape
    cb0 = (OFF_XBC + col0) // CONV_BLOCK
    wb0 = col0 // CONV_BLOCK
    na = len(addends)
    scaled = [i for i in range(na) if scales[i] is not None]

    def body(*refs):
        u_ref, w_ref, b_ref = refs[0], refs[1], refs[2]
        a_refs = refs[3:3 + na]
        s_refs = dict(zip(scaled, refs[3 + na:3 + na + len(scaled)]))
        o_ref, acc_ref, ubuf, dbuf = refs[3 + na + len(scaled):]
        _halo_buf_init(ubuf, u_ref[0].astype(F32), R)
        _halo_buf_init(dbuf, None, R)
        w = w_ref[...]
        b = b_ref[...]
        scl = {i: s_refs[i][...] for i in scaled}
        sums = [jnp.zeros((8, CONV_BLOCK), F32) for _ in range(5)]
        for start in range(0, R, CONV_CHUNK):
            rows = pl.ds(start, CONV_CHUNK)
            taps = _chunk_taps(ubuf, start, (-2, -1, 0, 1), L)
            pre = b + sum(taps[k] * w[k:k + 1, :] for k in range(4))
            sg = _sigmoid(pre)
            dxbc = None
            for i, a in enumerate(a_refs):
                t = a[0, rows, :].astype(F32)
                t = t * scl[i] if i in scl else t
                dxbc = t if dxbc is None else dxbc + t
            dpre = dxbc * (sg * (1.0 + pre * (1.0 - sg)))
            dbuf[pl.ds(start + CONV_HALO, CONV_CHUNK), :] = dpre
            for k in range(4):
                sums[k] = sums[k] + _fold8(dpre * taps[k])
            sums[4] = sums[4] + _fold8(dpre)
        acc_ref[...] = jnp.zeros_like(acc_ref)
        for k in range(5):
            acc_ref[0, k:k + 1, :] = jnp.sum(sums[k], axis=0, keepdims=True)
        for start in range(0, R, CONV_CHUNK):
            d = _chunk_taps(dbuf, start, (2, 1, 0, -1), L)
            o_ref[0, pl.ds(start, CONV_CHUNK), :] = sum(d[k] * w[k:k + 1, :] for k in range(4)).astype(BF)

    in_specs = [pl.BlockSpec((1, R, CONV_BLOCK), lambda e, j: (e, 0, cb0 + j)),
                pl.BlockSpec((4, CONV_BLOCK), lambda e, j: (0, wb0 + j)),
                pl.BlockSpec((1, CONV_BLOCK), lambda e, j: (0, wb0 + j))]
    for m in in_maps:
        in_specs.append(pl.BlockSpec((1, R, CONV_BLOCK), functools.partial(lambda e, j, m: (e, 0, m(j)), m=m)))
    for i in scaled:
        in_specs.append(pl.BlockSpec((1, CONV_BLOCK), functools.partial(lambda e, j, m: (0, m(j)), m=in_maps[i])))
    return pl.pallas_call(
        body, name=name, grid=(2, ncols // CONV_BLOCK),
        in_specs=in_specs,
        out_specs=(pl.BlockSpec((1, R, CONV_BLOCK), lambda e, j: (e, 0, j)),
                   pl.BlockSpec((1, 8, CONV_BLOCK), lambda e, j: (e, 0, j))),
        out_shape=(jax.ShapeDtypeStruct((2, R, ncols), BF), jax.ShapeDtypeStruct((2, 8, ncols), F32)),
        scratch_shapes=[pltpu.VMEM((R + 2 * CONV_HALO, CONV_BLOCK), F32)] * 2,
        compiler_params=_params(("parallel", "parallel")),
    )(proj3, conv_w, conv_b, *addends, *[scales[i] for i in scaled])


def _softplus(x):
    e = jnp.exp(-jnp.abs(x))
    u = 1.0 + e
    return jnp.maximum(x, 0.0) + jnp.where(u == 1.0, e, e * jnp.log(u) / (u - 1.0))


def _to_local_mat(g, transpose=False):
    r = lax.broadcasted_iota(jnp.int32, (128, 128), 1 if transpose else 0)
    c = lax.broadcasted_iota(jnp.int32, (128, 128), 0 if transpose else 1)
    return ((c < 2 * HPG) & (r == jnp.right_shift(c, 3) * (NG * HPG) + g * HPG + (c & (HPG - 1)))).astype(BF)


def _dt_fwd(dt_raw, bias128):
    _, R, _ = dt_raw.shape

    def body(x_ref, b_ref, o_ref):
        dt = _softplus(x_ref[0] + b_ref[...])
        for g in range(NG):
            o_ref[0, g] = _dot_sl(dt, _to_local_mat(g))

    tr = R // 4
    return pl.pallas_call(
        body, name="dt_fwd", grid=(2, 4),
        in_specs=[pl.BlockSpec((1, tr, 128), lambda e, t: (e, t, 0)), pl.BlockSpec((1, 128), lambda e, t: (0, 0))],
        out_specs=pl.BlockSpec((1, NG, tr, 128), lambda e, t: (e, 0, t, 0)),
        out_shape=jax.ShapeDtypeStruct((2, NG, R, 128), F32),
        compiler_params=_params(("parallel", "parallel")),
    )(dt_raw, bias128)


def _dt_bwd(dt_raw, bias128, ddt_f, ddt_b):
    _, R, _ = dt_raw.shape

    def body(x_ref, b_ref, f_ref, g_ref, o_ref, acc_ref):
        ddt = sum(_dot_sl(f_ref[0, g] + g_ref[0, g], _to_local_mat(g, transpose=True)) for g in range(NG))
        d = ddt * _sigmoid(x_ref[0] + b_ref[...])
        o_ref[0] = d.astype(BF)

        @pl.when(pl.program_id(1) == 0)
        def _():
            acc_ref[...] = jnp.zeros_like(acc_ref)

        acc_ref[0, 0:1, :] += jnp.sum(d, axis=0, keepdims=True)

    tr = R // 4
    blk = pl.BlockSpec((1, tr, 128), lambda e, t: (e, t, 0))
    loc = pl.BlockSpec((1, NG, tr, 128), lambda e, t: (e, 0, t, 0))
    return pl.pallas_call(
        body, name="dt_bwd", grid=(2, 4),
        in_specs=[blk, pl.BlockSpec((1, 128), lambda e, t: (0, 0)), loc, loc],
        out_specs=(blk, pl.BlockSpec((1, 8, 128), lambda e, t: (e, 0, 0))),
        out_shape=(jax.ShapeDtypeStruct(dt_raw.shape, BF), jax.ShapeDtypeStruct((2, 8, 128), F32)),
        compiler_params=_params(("parallel", "arbitrary")),
    )(dt_raw, bias128, ddt_f, ddt_b)


GPS = 4


def _tri(d):
    i = lax.broadcasted_iota(jnp.int32, (Q, Q), 0)
    j = lax.broadcasted_iota(jnp.int32, (Q, Q), 1)
    return (i >= j) if d == 0 else (i <= j)


def _expand_mat(d):
    r = lax.broadcasted_iota(jnp.int32, (128, GWID), 0)
    c = lax.broadcasted_iota(jnp.int32, (128, GWID), 1)
    return (r == d * HPG + jnp.right_shift(c, 6)).astype(BF)


def _reduce_mat(d):
    r = lax.broadcasted_iota(jnp.int32, (GWID, 128), 0)
    c = lax.broadcasted_iota(jnp.int32, (GWID, 128), 1)
    return (c == d * HPG + jnp.right_shift(r, 6)).astype(BF)


def _ssd_chunk(d, dt, A, xs, B, C):
    mask = _tri(d)
    T = mask.astype(BF)
    Tt = _tri(1 - d).astype(BF)
    a = dt * A
    acs = _dot_sr(T, a)
    E = _expand_mat(d)
    dt_e = _dot_sl(dt, E, 2)
    acs_e = _dot_sl(acs, E, 2)
    alast_e = acs_e[Q - 1:Q, :] if d == 0 else acs_e[0:1, :]
    return dict(mask=mask, T=T, Tt=Tt, acs=acs, acsT=acs.T, dt_e=dt_e, acs_e=acs_e, lam=jnp.exp(acs_e),
                w=jnp.exp(alast_e - acs_e), decay=jnp.exp(alast_e), xt=xs * dt_e, CB=_dot_nt(C, B))


def _head_decay(q, d, hh):
    col = q["acs"][:, d * HPG + hh:d * HPG + hh + 1]
    row = q["acsT"][d * HPG + hh:d * HPG + hh + 1, :]
    return jnp.exp(jnp.where(q["mask"], col - row, -jnp.inf))


def _chunk_maps(NX, NS):
    cf = lambda s: lax.rem(s + NX, NS)
    cb = lambda s: NS - 1 - s
    return cf, cb


def _ssd_fwd(xbc, dt_loc, a_loc, L):
    _, R, _ = xbc.shape
    NX, NS = L // Q, R // Q
    cf, cb = _chunk_maps(NX, NS)

    def body(xs_f, b_f, c_f, dt_f, xs_b, b_b, c_b, dt_b, a_ref, y_f, hs_f, y_b, hs_b, hT):
        @pl.when(pl.program_id(2) == 0)
        def _():
            hT[...] = jnp.zeros_like(hT)

        lane = lax.broadcasted_iota(jnp.int32, (Q, 128), 1)
        for d, (xs_ref, b_ref, c_ref, dt_ref, y_ref, hs_ref) in enumerate(
                ((xs_f, b_f, c_f, dt_f, y_f, hs_f), (xs_b, b_b, c_b, dt_b, y_b, hs_b))):
            for gi in range(GPS):
                cols = slice(gi * GWID, (gi + 1) * GWID)
                xs = xs_ref[0, :, cols].astype(F32)
                B, C = b_ref[0, :, gi * NST:(gi + 1) * NST], c_ref[0, :, gi * NST:(gi + 1) * NST]
                q = _ssd_chunk(d, dt_ref[0, gi], a_ref[gi, 0:1, :], xs, B, C)
                h = hT[d, :, cols]
                hb = h.astype(BF)
                hs_ref[0, 0, :, cols] = hb
                parts = []
                for pr in range(HPG // 2):
                    xp = q["xt"][:, pr * 128:(pr + 1) * 128]
                    xst = jnp.concatenate([jnp.where(lane < HEAD, xp, 0.0), jnp.where(lane < HEAD, 0.0, xp)], axis=0)
                    mst = jnp.concatenate([(q["CB"] * _head_decay(q, d, 2 * pr)).astype(BF),
                                           (q["CB"] * _head_decay(q, d, 2 * pr + 1)).astype(BF)], axis=1)
                    parts.append(_dot(mst, xst))
                y_ref[0, :, cols] = jnp.concatenate(parts, axis=1) + _dot(C, hb) * q["lam"]
                hT[d, :, cols] = q["decay"] * h + _dot_tn(B, q["xt"] * q["w"])

    def spec(shape, imap):
        return pl.BlockSpec(shape, imap)

    bc0 = DIN // (GPS * NST)

    def ins(c):
        return [spec((1, Q, GPS * GWID), lambda e, g, s: (e, c(s), g)),
                spec((1, Q, GPS * NST), lambda e, g, s: (e, c(s), bc0 + g)),
                spec((1, Q, GPS * NST), lambda e, g, s: (e, c(s), bc0 + NG // GPS + g)),
                spec((1, GPS, Q, 128), lambda e, g, s: (e, g, c(s), 0))]

    def outs(c):
        return [spec((1, Q, GPS * GWID), lambda e, g, s: (e, c(s), g)),
                spec((1, 1, NST, GPS * GWID), lambda e, g, s: (e, c(s), 0, g))]

    yshape = jax.ShapeDtypeStruct((2, R, DIN), F32)
    hshape = jax.ShapeDtypeStruct((2, NS, NST, DIN), BF)
    return pl.pallas_call(
        body, name="ssd_fwd", grid=(2, NG // GPS, NS),
        in_specs=ins(cf) + ins(cb) + [spec((GPS, 8, 128), lambda e, g, s: (g, 0, 0))],
        out_specs=tuple(outs(cf) + outs(cb)),
        out_shape=(yshape, hshape, yshape, hshape),
        scratch_shapes=[pltpu.VMEM((2, NST, GPS * GWID), F32)],
        compiler_params=_params(("parallel", "parallel", "arbitrary")),
    )(xbc, xbc, xbc, dt_loc, xbc, xbc, xbc, dt_loc, a_loc)


def _ssd_bwd(xbc, dt_loc, a_loc, hs_f, hs_b, y_f, y_b, dy, L):
    _, R, _ = xbc.shape
    NX, NS = L // Q, R // Q
    cf0, cb0 = _chunk_maps(NX, NS)
    cf = lambda sp: cf0(NS - 1 - sp)
    cb = lambda sp: cb0(NS - 1 - sp)

    def body(xs_f, b_f, c_f, dt_f, hs_f_, dy_f, y_f_, xs_b, b_b, c_b, dt_b, hs_b_, dy_b, y_b_, a_ref,
             dxs_f, dbc_f, ddt_f, dxs_b, dbc_b, ddt_b, da_ref, dhT):
        @pl.when(pl.program_id(2) == 0)
        def _():
            dhT[...] = jnp.zeros_like(dhT)
            da_ref[...] = jnp.zeros_like(da_ref)

        lane = lax.broadcasted_iota(jnp.int32, (Q, 128), 1)
        row = lax.broadcasted_iota(jnp.int32, (Q, 128), 0)

        def one_chain(d, gi, xs_ref, b_ref, c_ref, dt_ref, hs_ref, dy_ref, y_ref, dxs_ref, dbc_ref, ddt_ref):
            cols = slice(gi * GWID, (gi + 1) * GWID)
            A = a_ref[gi, 0:1, :]
            xs, dt = xs_ref[0, :, cols].astype(F32), dt_ref[0, gi]
            B, C = b_ref[0, :, gi * NST:(gi + 1) * NST], c_ref[0, :, gi * NST:(gi + 1) * NST]
            q = _ssd_chunk(d, dt, A, xs, B, C)
            xt, lam, w, decay = q["xt"], q["lam"], q["w"], q["decay"]
            H = hs_ref[0, 0, :, cols]
            dyv = dy_ref[0, :, cols].astype(F32)
            dh = dhT[d, :, cols]
            dZ = dyv * lam
            dC = _dot_nt(dZ, H)
            dH = _dot_tn(C, dZ)
            U = _dot(B, dh)
            xw = xt * w
            dxt = U * w
            dalast_e = (jnp.sum(U * xw, axis=0, keepdims=True)
                        + decay * jnp.sum(dh * H.astype(F32), axis=0, keepdims=True))
            dB = _dot_nt(xw, dh)
            dCB = jnp.zeros((Q, Q), F32)
            dxt_parts = []
            for pr in range(HPG // 2):
                xp = xt[:, pr * 128:(pr + 1) * 128]
                dyp = dyv[:, pr * 128:(pr + 1) * 128]
                L0, L1 = _head_decay(q, d, 2 * pr), _head_decay(q, d, 2 * pr + 1)
                dyst = jnp.concatenate([jnp.where(lane < HEAD, dyp, 0.0), jnp.where(lane < HEAD, 0.0, dyp)], axis=0)
                mst = jnp.concatenate([(q["CB"] * L0).astype(BF), (q["CB"] * L1).astype(BF)], axis=0)
                dxt_parts.append(_dot_tn(mst, dyst))
                dmst = _dot_nt(dyst, xp)
                dCB = dCB + dmst[:Q] * L0 + dmst[Q:] * L1
            dxt_diag = jnp.concatenate(dxt_parts, axis=1)
            dC = dC + _dot(dCB, B)
            dB = dB + _dot_tn(dCB, C)
            Rm = _reduce_mat(d)
            dacs = _dot_sl(dyv * y_ref[0, :, cols] - xt.astype(BF).astype(F32) * dxt_diag - U * xw, Rm, 2)
            dxt = dxt + dxt_diag
            dal = _dot_sl(jnp.broadcast_to(dalast_e, (8, GWID)), Rm, 2)[0:1, :]
            dacs = dacs + jnp.where(row == (Q - 1 if d == 0 else 0), dal, 0.0)
            da = _dot_sr(q["Tt"], dacs, 2)
            ddt_ref[0, gi] = da * A + _dot_sl(dxt * xs, Rm, 2)
            da_ref[0, gi, 0:1, :] += jnp.sum(da * dt, axis=0, keepdims=True)
            dxs_ref[0, :, cols] = (dxt * q["dt_e"]).astype(BF)
            dbc_ref[0, :, gi * 2 * NST:(gi + 1) * 2 * NST] = jnp.concatenate([dB, dC], axis=1).astype(BF)
            dhT[d, :, cols] = decay * dh + dH

        for gi in range(GPS):
            one_chain(0, gi, xs_f, b_f, c_f, dt_f, hs_f_, dy_f, y_f_, dxs_f, dbc_f, ddt_f)
            one_chain(1, gi, xs_b, b_b, c_b, dt_b, hs_b_, dy_b, y_b_, dxs_b, dbc_b, ddt_b)

    def spec(shape, imap):
        return pl.BlockSpec(shape, imap)

    bc0 = DIN // (GPS * NST)

    def ins(c):
        return [spec((1, Q, GPS * GWID), lambda e, g, s: (e, c(s), g)),
                spec((1, Q, GPS * NST), lambda e, g, s: (e, c(s), bc0 + g)),
                spec((1, Q, GPS * NST), lambda e, g, s: (e, c(s), bc0 + NG // GPS + g)),
                spec((1, GPS, Q, 128), lambda e, g, s: (e, g, c(s), 0)),
                spec((1, 1, NST, GPS * GWID), lambda e, g, s: (e, c(s), 0, g)),
                spec((1, Q, GPS * GWID), lambda e, g, s: (e, c(s), g)),
                spec((1, Q, GPS * GWID), lambda e, g, s: (e, c(s), g))]

    def outs(c):
        return [spec((1, Q, GPS * GWID), lambda e, g, s: (e, c(s), g)),
                spec((1, Q, GPS * 2 * NST), lambda e, g, s: (e, c(s), g)),
                spec((1, GPS, Q, 128), lambda e, g, s: (e, g, c(s), 0))]

    s_xs = jax.ShapeDtypeStruct((2, R, DIN), BF)
    s_bc = jax.ShapeDtypeStruct((2, R, 2 * NG * NST), BF)
    s_dt = jax.ShapeDtypeStruct((2, NG, R, 128), F32)
    return pl.pallas_call(
        body, name="ssd_bwd", grid=(2, NG // GPS, NS),
        in_specs=ins(cf) + ins(cb) + [spec((GPS, 8, 128), lambda e, g, s: (g, 0, 0))],
        out_specs=tuple(outs(cf) + outs(cb) + [spec((1, GPS, 8, 128), lambda e, g, s: (e, g, 0, 0))]),
        out_shape=(s_xs, s_bc, s_dt, s_xs, s_bc, s_dt, jax.ShapeDtypeStruct((2, NG, 8, 128), F32)),
        scratch_shapes=[pltpu.VMEM((2, NST, GPS * GWID), F32)],
        compiler_params=_params(("parallel", "parallel", "arbitrary")),
    )(xbc, xbc, xbc, dt_loc, hs_f, dy, y_f, xbc, xbc, xbc, dt_loc, hs_b, dy, y_b, a_loc)


def _ssd_post_fwd(y_f, y_b, xbc, proj3, dskip_e, ssd_norm, L):
    def body(yf_ref, yb_ref, xs_ref, z_ref, ds_ref, w_ref, o_ref, y2_ref):
        y2 = yf_ref[0] + yb_ref[0] + ds_ref[...] * xs_ref[0].astype(F32)
        y2_ref[0] = y2.astype(BF)
        z = z_ref[0].astype(F32)
        u = y2 * (z * _sigmoid(z))
        parts = []
        for g in range(NG):
            ug = u[:, g * GWID:(g + 1) * GWID]
            parts.append(ug * lax.rsqrt(jnp.mean(ug * ug, axis=-1, keepdims=True) + EPS))
        o_ref[0] = (jnp.concatenate(parts, axis=1) * w_ref[...]).astype(BF)

    blk = lambda c: pl.BlockSpec((1, ROW_TILE, DIN), lambda e, t: (e, t, c))
    vec = pl.BlockSpec((1, DIN), lambda e, t: (0, 0))
    return pl.pallas_call(
        body, name="ssd_post_fwd", grid=(2, L // ROW_TILE),
        in_specs=[blk(0), blk(0), blk(0), blk(1), vec, vec],
        out_specs=(blk(0), blk(0)),
        out_shape=(jax.ShapeDtypeStruct((2, L, DIN), BF), jax.ShapeDtypeStruct((2, L, DIN), BF)),
        compiler_params=_params(("parallel", "parallel")),
    )(y_f, y_b, xbc, proj3, dskip_e, ssd_norm)


def _ssd_post_bwd(d_yn, y2b, xbc, proj3, ssd_norm, L):
    _, R, _ = xbc.shape
    nx = L // ROW_TILE

    def body(dyn_ref, y2_ref, xs_ref, z_ref, w_ref, dy_ref, dz_ref, acc_ref):
        t = pl.program_id(1)

        @pl.when(t == 0)
        def _():
            acc_ref[...] = jnp.zeros_like(acc_ref)

        @pl.when(t >= nx)
        def _():
            dy_ref[...] = jnp.zeros_like(dy_ref)
            dz_ref[...] = jnp.zeros_like(dz_ref)

        @pl.when(t < nx)
        def _():
            xs = xs_ref[0].astype(F32)
            y2 = y2_ref[0].astype(F32)
            z = z_ref[0].astype(F32)
            sg = _sigmoid(z)
            sz = z * sg
            u = y2 * sz
            dyn = dyn_ref[0].astype(F32)
            dun = dyn * w_ref[...]
            uh_parts, du_parts = [], []
            for g in range(NG):
                sl = slice(g * GWID, (g + 1) * GWID)
                ug = u[:, sl]
                rg = lax.rsqrt(jnp.mean(ug * ug, axis=-1, keepdims=True) + EPS)
                uh = ug * rg
                dg = dun[:, sl]
                du_parts.append(rg * (dg - uh * jnp.mean(dg * uh, axis=-1, keepdims=True)))
                uh_parts.append(uh)
            du = jnp.concatenate(du_parts, axis=1)
            uh = jnp.concatenate(uh_parts, axis=1)
            dy2 = du * sz
            dy_ref[0] = dy2.astype(BF)
            dz_ref[0] = (du * y2 * (sg * (1.0 + z * (1.0 - sg)))).astype(BF)
            acc_ref[0, 0:1, :] += jnp.sum(dyn * uh, axis=0, keepdims=True)
            acc_ref[0, 1:2, :] += jnp.sum(dy2 * xs, axis=0, keepdims=True)

    xmap = lambda c: (lambda e, t: (e, jnp.minimum(t, nx - 1), c))
    blk = lambda c: pl.BlockSpec((1, ROW_TILE, DIN), xmap(c))
    oblk = pl.BlockSpec((1, ROW_TILE, DIN), lambda e, t: (e, t, 0))
    vec = pl.BlockSpec((1, DIN), lambda e, t: (0, 0))
    return pl.pallas_call(
        body, name="ssd_post_bwd", grid=(2, R // ROW_TILE),
        in_specs=[blk(0), blk(0), blk(0), blk(1), vec],
        out_specs=(oblk, oblk, pl.BlockSpec((1, 8, DIN), lambda e, t: (e, 0, 0))),
        out_shape=(jax.ShapeDtypeStruct((2, R, DIN), BF), jax.ShapeDtypeStruct((2, R, DIN), BF),
                   jax.ShapeDtypeStruct((2, 8, DIN), F32)),
        compiler_params=_params(("parallel", "arbitrary")),
    )(d_yn, y2b, xbc, proj3, ssd_norm)


def _merge_fwd(proj3, P, S, b_merge, L):
    def body(gp_ref, p_ref, s_ref, b_ref, o_ref):
        gt = _sigmoid(gp_ref[0].astype(F32) + b_ref[...])
        o_ref[0] = (gt[:, :D] * p_ref[0].astype(F32) + gt[:, D:] * s_ref[0].astype(F32)).astype(BF)

    blk = pl.BlockSpec((1, ROW_TILE, D), lambda e, t: (e, t, 0))
    return pl.pallas_call(
        body, name="merge_fwd", grid=(2, L // ROW_TILE),
        in_specs=[pl.BlockSpec((1, ROW_TILE, 2 * D), lambda e, t: (e, t, OFF_GATE // (2 * D))), blk, blk,
                  pl.BlockSpec((1, 2 * D), lambda e, t: (0, 0))],
        out_specs=blk, out_shape=jax.ShapeDtypeStruct((2, L, D), BF),
        compiler_params=_params(("parallel", "parallel")),
    )(proj3, P, S, b_merge)


def _merge_bwd(d_merged, proj3, P, S, b_merge, L):
    _, R, _ = proj3.shape
    nx = L // ROW_TILE

    def body(dm_ref, gp_ref, p_ref, s_ref, b_ref, dp_ref, ds_ref, dg_ref, acc_ref):
        t = pl.program_id(1)

        @pl.when(t == 0)
        def _():
            acc_ref[...] = jnp.zeros_like(acc_ref)

        @pl.when(t >= nx)
        def _():
            dg_ref[...] = jnp.zeros_like(dg_ref)

        @pl.when(t < nx)
        def _():
            gt = _sigmoid(gp_ref[0].astype(F32) + b_ref[...])
            dm = dm_ref[0].astype(F32)
            g1, g2 = gt[:, :D], gt[:, D:]
            dp_ref[0] = (dm * g1).astype(BF)
            ds_ref[0] = (dm * g2).astype(BF)
            dgp = jnp.concatenate([dm * p_ref[0].astype(F32) * g1 * (1.0 - g1),
                                   dm * s_ref[0].astype(F32) * g2 * (1.0 - g2)], axis=1)
            dg_ref[0] = dgp.astype(BF)
            acc_ref[0, 0:1, :] += jnp.sum(dgp, axis=0, keepdims=True)

    xmap = lambda e, t: (e, jnp.minimum(t, nx - 1), 0)
    blk = pl.BlockSpec((1, ROW_TILE, D), xmap)
    return pl.pallas_call(
        body, name="merge_bwd", grid=(2, R // ROW_TILE),
        in_specs=[blk, pl.BlockSpec((1, ROW_TILE, 2 * D), lambda e, t: (e, jnp.minimum(t, nx - 1), OFF_GATE // (2 * D))),
                  blk, blk, pl.BlockSpec((1, 2 * D), lambda e, t: (0, 0))],
        out_specs=(blk, blk, pl.BlockSpec((1, ROW_TILE, 2 * D), lambda e, t: (e, t, 0)),
                   pl.BlockSpec((1, 8, 2 * D), lambda e, t: (e, 0, 0))),
        out_shape=(jax.ShapeDtypeStruct((2, L, D), BF), jax.ShapeDtypeStruct((2, L, D), BF),
                   jax.ShapeDtypeStruct((2, R, 2 * D), BF), jax.ShapeDtypeStruct((2, 8, 2 * D), F32)),
        compiler_params=_params(("parallel", "arbitrary")),
    )(d_merged, proj3, P, S, b_merge)


def _final(out3, x, tgt, gtab, norm_post, L):
    def body(o_ref, x_ref, t_ref, g_ref, n_ref, dxo_ref, do_ref, acc_ref):
        @pl.when(pl.program_id(1) == 0)
        def _():
            acc_ref[...] = jnp.zeros_like(acc_ref)

        o = o_ref[0].astype(F32)
        gate = g_ref[0, 0:1, :]
        npost = n_ref[...]
        r2 = lax.rsqrt(jnp.mean(o * o, axis=-1, keepdims=True) + EPS)
        nh = o * r2
        on = nh * npost
        err = x_ref[0] + gate * on - t_ref[0]
        dxo = err * (1.0 / D)
        dxo_ref[0] = dxo
        dnh = dxo * gate * npost
        do_ref[0] = (r2 * (dnh - nh * jnp.mean(dnh * nh, axis=-1, keepdims=True))).astype(BF)
        acc_ref[0, 0:1, :] += jnp.sum(dxo * on, axis=0, keepdims=True)
        acc_ref[0, 1:2, :] += jnp.sum(dxo * gate * nh, axis=0, keepdims=True)
        acc_ref[0, 2:3, :] += jnp.sum(err * err, axis=0, keepdims=True)

    blk = pl.BlockSpec((1, ROW_TILE, D), lambda e, t: (e, t, 0))
    return pl.pallas_call(
        body, name="final", grid=(2, L // ROW_TILE),
        in_specs=[blk, blk, blk, pl.BlockSpec((1, 8, D), lambda e, t: (e, 0, 0)),
                  pl.BlockSpec((1, D), lambda e, t: (0, 0))],
        out_specs=(blk, blk, pl.BlockSpec((1, 8, D), lambda e, t: (e, 0, 0))),
        out_shape=(jax.ShapeDtypeStruct((2, L, D), F32), jax.ShapeDtypeStruct((2, L, D), BF),
                   jax.ShapeDtypeStruct((2, 8, D), F32)),
        compiler_params=_params(("parallel", "arbitrary")),
    )(out3, x, tgt, gtab, norm_post)


def _local_step(x, c, ctx, loss_target, W, late_shard=None, exchange=False):
    nb, L, _ = x.shape
    LC = ctx.shape[1]
    R = L + LC
    assert nb == 2 and L % ROW_TILE == 0 and LC % Q == 0 and L % POOL_TILE == 0
    w_inT = W["w_in"]
    w_dtT = jnp.pad(w_inT[OFF_DT:], ((0, 64), (0, 0)))
    tables = _pool_tables(L)
    tr, tl = (2 * R) // 8, (2 * L) // 8

    c16 = jnp.zeros((16, D), F32).at[0:2].set(c).at[2].set(W["c_ctx"])
    mod16 = _adaln_fwd(c16, W["w_ada"], W["b_ada"])
    shift, scale, gate = mod16[:, :D], mod16[:, D:2 * D], mod16[:, 2 * D:]
    npre = W["norm_pre"]
    tab = jnp.zeros((2, 2, 8, D), F32)
    for e in range(2):
        tab = tab.at[e, 0, 0].set(npre[0] * (1.0 + scale[e])).at[e, 0, 1].set(shift[e])
        tab = tab.at[e, 1, 0].set(npre[0] * (1.0 + scale[2])).at[e, 1, 1].set(shift[2])
    gtab = jnp.zeros((2, 8, D), F32).at[:, 0].set(gate[0:2])

    hx = _norm_mod_fwd(x, ctx, tab)
    hx2 = hx.reshape(2 * R, D)
    if late_shard is None:
        proj = _matmul(hx2, w_inT, BF, "proj_main", tm=tr, tn=1024, bt=True, n=OFF_DT)
    else:
        proj, late = _matmul(hx2, w_inT, BF, "proj_main", tm=tr, tn=1024, bt=True, n=OFF_DT, side=_gather_side(late_shard))
        W = {**W, **_unpack_gather(late, GATHER_LATE)}
    proj3 = proj.reshape(2, R, OFF_DT)
    dt_raw = _matmul(hx2, w_dtT, F32, "proj_dt", tm=tr, bt=True).reshape(2, R, 128)
    ypool = _pool_fwd(proj3, W["pool_w"], W["pool_scale"], tables, L)
    xbc = _conv_fwd(proj3, W["conv_w"], W["conv_b"], L)
    bias128 = jnp.pad(W["dt_bias"].reshape(1, 64), ((0, 0), (0, 64)))
    dt_loc = _dt_fwd(dt_raw, bias128)
    A = -jnp.exp(W["a_log"].reshape(2, NG, HPG))
    a_loc = jnp.zeros((NG, 8, 128), F32).at[:, 0, :16].set(A.transpose(1, 0, 2).reshape(NG, 16))
    y_f, hs_f, y_b, hs_b = _ssd_fwd(xbc, dt_loc, a_loc, L)
    dskip_e = jnp.repeat(W["d_skip"].reshape(1, 32), HEAD, axis=1)
    yn, y2b = _ssd_post_fwd(y_f, y_b, xbc, proj3, dskip_e, W["ssd_norm"], L)
    ypool2, yn2 = ypool.reshape(2 * L, D), yn.reshape(2 * L, DIN)
    P = _matmul(ypool2, W["w_proj_pool"], BF, "proj_pool", tm=tl, tn=1024).reshape(2, L, D)
    S = _matmul(yn2, W["w_proj_ssd"], BF, "proj_ssd", tm=tl, tn=1024).reshape(2, L, D)
    merged = _merge_fwd(proj3, P, S, W["b_merge"], L)
    merged2 = merged.reshape(2 * L, D)
    out3 = _matmul(merged2, W["w_out"], BF, "proj_out", tm=tl, tn=1024).reshape(2, L, D)
    dxo, dout, acc_f = _final(out3, x, loss_target, gtab, W["norm_post"], L)

    dout2 = dout.reshape(2 * L, D)
    g = {}
    g["w_out"] = _matmul_tn(merged2, dout2, "dw_out", ta=1024, tn=1024, tr=2 * tl)
    d_merged = _matmul(dout2, W["w_out"], BF, "d_merged", tm=tl, tn=1024, bt=True).reshape(2, L, D)
    dP, dS, dgp, acc_m = _merge_bwd(d_merged, proj3, P, S, W["b_merge"], L)
    dP2, dS2 = dP.reshape(2 * L, D), dS.reshape(2 * L, D)
    g["w_proj_pool"] = _matmul_tn(ypool2, dP2, "dw_proj_pool", ta=1024, tn=1024, tr=2 * tl)
    g["w_proj_ssd"] = _matmul_tn(yn2, dS2, "dw_proj_ssd", ta=1024, tn=1024, tr=2 * tl)
    d_ypool = _matmul(dP2, W["w_proj_pool"], BF, "d_ypool", tm=tl, tn=1024, bt=True).reshape(2, L, D)
    d_yn = _matmul(dS2, W["w_proj_ssd"], BF, "d_yn", tm=tl, tn=1024, bt=True).reshape(2, L, DIN)
    dv, dzp, g["pool_w"], acc_p = _pool_bwd(proj3, d_ypool, W["pool_w"], jnp.swapaxes(W["pool_w"], 1, 2),
                                            W["pool_scale"], tables, L)
    dy2, dzs, acc_s = _ssd_post_bwd(d_yn, y2b, xbc, proj3, W["ssd_norm"], L)
    dxs_f, dbc_f, ddt_f, dxs_b, dbc_b, ddt_b, acc_a = _ssd_bwd(xbc, dt_loc, a_loc, hs_f, hs_b, y_f, y_b, dy2, L)
    ident = lambda j: j
    dxr_xs, acc_cx = _conv_bwd(proj3, [dxs_f, dxs_b, dy2], [None, None, dskip_e], 0, DIN, [ident, ident, ident],
                               W["conv_w"], W["conv_b"], L, "conv_bwd_xs")
    bcmap = lambda j: 2 * lax.rem(j, NG) + j // NG
    dxr_bc, acc_cb = _conv_bwd(proj3, [dbc_f, dbc_b], [None, None], DIN, 2 * NG * NST, [bcmap, bcmap],
                               W["conv_w"], W["conv_b"], L, "conv_bwd_bc")
    ddtr, acc_d = _dt_bwd(dt_raw, bias128, ddt_f, ddt_b)
    pieces = [dv, dzp, dzs, dgp, dxr_xs, dxr_bc]
    dw_rows = [_matmul_tn(p.reshape(2 * R, p.shape[2]), hx2, "dw_in_%d" % i, ta=1024, tn=1024, tr=2 * tr)
               for i, p in enumerate(pieces)]
    dw_rows.append(_matmul_tn(ddtr.reshape(2 * R, 128), hx2, "dw_in_dt", ta=128, tn=1024, tr=tr)[:64])
    g["w_in"] = jnp.concatenate(dw_rows, axis=0)
    acc_c = jnp.concatenate([acc_cx[0] + acc_cx[1], acc_cb[0] + acc_cb[1]], axis=1)
    g["conv_w"] = acc_c[0:4]
    g["conv_b"] = acc_c[4:5]
    if exchange:
        gb = _pack_grads(g, GRADS_EARLY)
        pair = _pair_add(gb, _pair_exchange(gb, None, "grads_pair_exchange_early"), "grads_pair_add_early")
        dh, recv_early = _dhx(pieces, ddtr, w_inT, w_dtT, side=_chip_exchange_side(pair))
    else:
        dh, recv_early = _dhx(pieces, ddtr, w_inT, w_dtT), None
    grad_x, acc_n = _norm_mod_bwd(dh, x, ctx, tab, dxo)
    g["w_ada"], db_rows, sm_rows = _adaln_bwd(acc_n, acc_f, mod16, c16, npre, W["w_ada"])

    g["b_ada"] = db_rows[0:1]
    g["norm_pre"] = sm_rows[0:1]
    g["c_ctx"] = sm_rows[1]
    g["norm_post"] = acc_f[0, 1:2] + acc_f[1, 1:2]
    g["b_merge"] = acc_m[0, 0:1] + acc_m[1, 0:1]
    g["pool_scale"] = acc_p[:, 0, :].reshape(1, D)
    g["dt_bias"] = (acc_d[0, 0, :64] + acc_d[1, 0, :64]).reshape(2, 32)
    dA = (acc_a[0, :, 0, :16] + acc_a[1, :, 0, :16]).reshape(NG, 2, HPG).transpose(1, 0, 2)
    g["a_log"] = (dA * A).reshape(2, 32)
    g["d_skip"] = (acc_s[0, 1] + acc_s[1, 1]).reshape(32, HEAD).sum(axis=1).reshape(1, 32)
    g["ssd_norm"] = acc_s[0, 0:1] + acc_s[1, 0:1]
    loss_lanes = acc_f[:, 2, :]
    return loss_lanes, grad_x, g, recv_early


MESH = pl.DeviceIdType.MESH
ANY = pl.BlockSpec(memory_space=pl.ANY)


def _all_gather(shard):
    m_per, n = shard.shape

    def body(x_ref, out_ref, send_sems, recv_sems, local_sem):
        x, y, c = lax.axis_index("x"), lax.axis_index("y"), lax.axis_index("c")
        me, sibling = (x, y, c), (x, y, 1 - c)
        chips = [(1 - x, y), (x, 1 - y), (1 - x, 1 - y)]

        def rows(px, py, pc):
            return out_ref.at[pl.ds((4 * px + 2 * py + pc) * m_per, m_per), :]

        def copy(k, block, to, src=None):
            return pltpu.make_async_remote_copy(
                src_ref=rows(*block) if src is None else src, dst_ref=rows(*block),
                send_sem=send_sems.at[k], recv_sem=recv_sems.at[k], device_id=to, device_id_type=MESH)

        mine = pltpu.make_async_copy(x_ref, rows(*me), local_sem)
        mine.start()
        first = [copy(0, me, sibling, src=x_ref)]
        first += [copy(1 + j, me, (*chip, c), src=x_ref) for j, chip in enumerate(chips)]
        for cp in first:
            cp.start()
        passed = [copy(4 + j, (*chip, c), sibling) for j, chip in enumerate(chips)]
        for j, chip in enumerate(chips):
            copy(1 + j, (*chip, c), me).wait_recv()
            passed[j].start()
        copy(0, sibling, me).wait_recv()
        for j, chip in enumerate(chips):
            copy(4 + j, (*chip, 1 - c), me).wait_recv()
        for cp in first + passed:
            cp.wait_send()
        mine.wait()

    return pl.pallas_call(
        body, name="all_gather_weights",
        out_shape=jax.ShapeDtypeStruct((NDEV * m_per, n), shard.dtype),
        in_specs=[ANY], out_specs=ANY,
        scratch_shapes=[pltpu.SemaphoreType.DMA((7,)), pltpu.SemaphoreType.DMA((7,)), pltpu.SemaphoreType.DMA],
    )(shard)


PAIR_PIECES = 4


def _xor_peer(k, x, y, c):
    return (1 - x if k & 4 else x, 1 - y if k & 2 else y, 1 - c if k & 1 else c)


def _pair_exchange(big, small, name):
    _, nq, rows, n = big.shape
    piece = rows // PAIR_PIECES
    assert piece * PAIR_PIECES == rows and piece % 16 == 0
    with_small = small is not None

    def body(*refs):
        if with_small:
            big_ref, small_ref, got_ref, osmall_ref, send_sems, recv_sems, local_sem = refs
        else:
            big_ref, got_ref, send_sems, recv_sems, local_sem = refs
        x, y, c = lax.axis_index("x"), lax.axis_index("y"), lax.axis_index("c")
        me = 4 * x + 2 * y + c

        def rc(src, dst, sem, peer):
            return pltpu.make_async_remote_copy(src_ref=src, dst_ref=dst, send_sem=send_sems.at[sem],
                                                recv_sem=recv_sems.at[sem], device_id=peer, device_id_type=MESH)

        sib = _xor_peer(1, x, y, c)
        local, sends, recvs = [], [], []
        for q in range(nq):
            for h in range(PAIR_PIECES):
                rws = pl.ds(h * piece, piece)
                cp = rc(big_ref.at[1 - c, q, rws], got_ref.at[q, rws], 8 + q * PAIR_PIECES + h, sib)
                sends.append(cp)
                recvs.append(cp)
        if with_small:
            local.append(pltpu.make_async_copy(small_ref, osmall_ref.at[me], local_sem))
            for k in range(1, NDEV):
                px, py, pc = _xor_peer(k, x, y, c)
                sends.append(rc(small_ref, osmall_ref.at[me], k, (px, py, pc)))
                recvs.append(rc(small_ref, osmall_ref.at[4 * px + 2 * py + pc], k, (px, py, pc)))
        for cp in local + sends:
            cp.start()
        for cp in sends:
            cp.wait_send()
        for cp in recvs:
            cp.wait_recv()
        for cp in local:
            cp.wait()

    nsem = 8 + nq * PAIR_PIECES
    out_shape = [jax.ShapeDtypeStruct(big.shape[1:], big.dtype)]
    if with_small:
        out_shape.append(jax.ShapeDtypeStruct((NDEV,) + small.shape, small.dtype))
    out = pl.pallas_call(
        body, name=name, out_shape=tuple(out_shape),
        in_specs=[ANY] * (1 + with_small), out_specs=(ANY,) * (1 + with_small),
        scratch_shapes=[pltpu.SemaphoreType.DMA((nsem,)), pltpu.SemaphoreType.DMA((nsem,)), pltpu.SemaphoreType.DMA],
    )(*((big, small) if with_small else (big,)))
    return out if with_small else out[0]


def _pair_add(big, got, name):
    _, nq, rows, n = big.shape
    tile = rows // 4
    assert rows % 64 == 0

    def body(c_ref, a_ref, b_ref, o_ref):
        o_ref[0] = (a_ref[0, 0].astype(F32) + b_ref[0].astype(F32)).astype(BF)

    blk = pl.BlockSpec((1, tile, n), lambda q, i, c_ref: (q, i, 0))
    return pl.pallas_call(
        body, name=name,
        grid_spec=pltpu.PrefetchScalarGridSpec(
            num_scalar_prefetch=1, grid=(nq, rows // tile),
            in_specs=[pl.BlockSpec((1, 1, tile, n), lambda q, i, c_ref: (c_ref[0], q, i, 0)), blk], out_specs=blk),
        out_shape=jax.ShapeDtypeStruct(got.shape, BF), compiler_params=_params(("parallel", "parallel")),
    )(lax.axis_index("c").astype(jnp.int32).reshape(1), big, got)


def _chip_exchange_side(pair):
    def make(in_refs, out_refs, send_sems, recv_sems, local_sem, arrivals=True):
        (in_ref,), (out_ref,) = in_refs, out_refs
        x, y, c = lax.axis_index("x"), lax.axis_index("y"), lax.axis_index("c")
        q = 2 * x + y
        local = [pltpu.make_async_copy(in_ref.at[q], out_ref.at[q], local_sem)]
        sends, recvs = [], []
        for j in range(1, 4):
            px, py, pc = _xor_peer(2 * j, x, y, c)
            pq = 2 * px + py
            for lst, dst in ((sends, out_ref.at[q]), (recvs, out_ref.at[pq]))[:1 + arrivals]:
                lst.append(pltpu.make_async_remote_copy(
                    src_ref=in_ref.at[pq], dst_ref=dst, send_sem=send_sems.at[j - 1], recv_sem=recv_sems.at[j - 1],
                    device_id=(px, py, pc), device_id_type=MESH))
        return local, sends, recvs

    return _SideCopies([pair], [jax.ShapeDtypeStruct(pair.shape, pair.dtype)], make)


def _gather_side(shard):
    def make(in_refs, out_refs, send_sems, recv_sems, local_sem, arrivals=True):
        (src,), (dst,) = in_refs, out_refs
        x, y, c = lax.axis_index("x"), lax.axis_index("y"), lax.axis_index("c")
        me = 4 * x + 2 * y + c
        local = [pltpu.make_async_copy(src, dst.at[me], local_sem)]
        sends, recvs = [], []
        for k in range(1, NDEV):
            px, py, pc = _xor_peer(k, x, y, c)
            for lst, slot in ((sends, me), (recvs, 4 * px + 2 * py + pc))[:1 + arrivals]:
                lst.append(pltpu.make_async_remote_copy(
                    src_ref=src, dst_ref=dst.at[slot], send_sem=send_sems.at[k - 1], recv_sem=recv_sems.at[k - 1],
                    device_id=(px, py, pc), device_id_type=MESH))
        return local, sends, recvs

    return _SideCopies([shard], [jax.ShapeDtypeStruct((NDEV,) + shard.shape, shard.dtype)], make)


ADAM_TILE = 64
PACK_W = 1024


def _adamw(recv, w, m, v, name, side=None):
    rp = w.shape[0]
    tile = min(ADAM_TILE, rp)
    nsrc = recv.shape[0]
    grid = (rp // tile,)
    n_si, n_so = (len(side.inputs), len(side.out_shapes)) if side else (0, 0)

    def body(*refs):
        r_ref, w_ref, m_ref, v_ref = refs[:4]
        g_ref, d_ref, nm_ref, nv_ref = refs[4 + n_si:8 + n_si]
        side_refs = (refs[4:4 + n_si], refs[8 + n_si:8 + n_si + n_so], refs[8 + n_si + n_so:])
        if side:
            side.start(grid, *side_refs)
        g = r_ref[0].astype(F32)
        for i in range(1, nsrc):
            g = g + r_ref[i].astype(F32)
        m1 = ADAM_B1 * m_ref[...] + (1.0 - ADAM_B1) * g
        v1 = ADAM_B2 * v_ref[...] + (1.0 - ADAM_B2) * (g * g)
        m_hat = m1 / (1.0 - ADAM_B1 ** ADAM_STEP)
        v_hat = v1 / (1.0 - ADAM_B2 ** ADAM_STEP)
        g_ref[...] = g
        d_ref[...] = -ADAM_LR * (m_hat / (jnp.sqrt(v_hat) + ADAM_EPS) + ADAM_WD * w_ref[...])
        nm_ref[...] = m1
        nv_ref[...] = v1
        if side:
            side.wait(grid, *side_refs)

    blk = pl.BlockSpec((tile, PACK_W), lambda i: (i, 0))
    shp = jax.ShapeDtypeStruct((rp, PACK_W), F32)
    return pl.pallas_call(
        body, name=name, grid=grid,
        in_specs=[pl.BlockSpec((nsrc, tile, PACK_W), lambda i: (0, i, 0)), blk, blk, blk] + [ANY] * n_si,
        out_specs=(blk, blk, blk, blk) + (ANY,) * n_so,
        out_shape=(shp, shp, shp, shp) + tuple(side.out_shapes if side else ()),
        scratch_shapes=side.scratch() if side else [],
        compiler_params=_params(("arbitrary",) if side else ("parallel",)),
    )(recv, w, m, v, *(side.inputs if side else ()))


BIG = {"w_ada": ((3 * D, D), 0), "pool_w": ((4, PGW, PGW), 1), "w_proj_pool": ((D, D), 0), "w_proj_ssd": ((DIN, D), 0),
       "w_out": ((D, D), 0), "w_in": ((IN_COLS, D), 0), "conv_w": ((4, CONV_DIM), 1)}
TRANSPOSED = ("w_ada", "w_in")
PACK_ROWS = {"w_ada": 384, "w_in": 1168, "conv_w": 16, "pool_w": 32, "w_proj_pool": 128, "w_proj_ssd": 256, "w_out": 128}
GATHER_EARLY = ("w_ada", "w_in", "conv_w")
GATHER_LATE = ("pool_w", "w_proj_pool", "w_proj_ssd", "w_out")
GRADS_LATE = ("w_ada",)
GRADS_EARLY = tuple(n for n in PACK_ROWS if n not in GRADS_LATE)
SMALL = {"c_ctx": (D,), "b_ada": (1, 3 * D), "norm_pre": (1, D), "norm_post": (1, D), "b_merge": (1, 2 * D),
         "pool_scale": (1, D), "conv_b": (1, CONV_DIM), "dt_bias": (2, 32), "a_log": (2, 32), "d_skip": (1, 32),
         "ssd_norm": (1, DIN)}
LOSS_SLOT = 128
assert all(_r % 16 == 0 for _r in PACK_ROWS.values())
SMALL_ROWS = 16


def _shard_shape(name):
    shape, ax = BIG[name]
    return tuple(s // NDEV if i == ax else s for i, s in enumerate(shape))


def _as_rows(t, rows):
    pad = [(0, 0)] * (t.ndim - 1) + [(0, rows * PACK_W - t.shape[-1])]
    return jnp.pad(t, pad).reshape(t.shape[:-1] + (rows, PACK_W))


def _shard_rows(t, name):
    sh, r = _shard_shape(name), PACK_ROWS[name]
    lead = t.shape[:t.ndim - len(sh)]
    if len(sh) == 2 and sh[1] == PACK_W:
        return jnp.pad(t, [(0, 0)] * len(lead) + [(0, r - sh[0]), (0, 0)])
    if int(np.prod(sh)) == r * PACK_W:
        return t.reshape(lead + (r, PACK_W))
    return _as_rows(t.reshape(lead + (-1,)), r)


def _to_chunks(full, name):
    shape, ax = BIG[name]
    split = shape[:ax] + (NDEV, shape[ax] // NDEV) + shape[ax + 1:]
    return _shard_rows(jnp.moveaxis(full.reshape(split), ax, 0), name)


def _from_chunks(chunks, name):
    shape, ax = BIG[name]
    return jnp.moveaxis(chunks.reshape((NDEV,) + _shard_shape(name)), 0, ax).reshape(shape)


def _rows_of(names):
    return sum(PACK_ROWS[n] for n in names)


def _pack_state(t, names):
    return jnp.concatenate([_shard_rows(t[n], n) for n in names], axis=0)


def _pack_small(t, loss_part=None):
    slot = jnp.zeros((LOSS_SLOT,), F32)
    if loss_part is not None:
        slot = slot.at[0].set(loss_part)
    return _as_rows(jnp.concatenate([t[n].reshape(-1) for n in SMALL] + [slot]), SMALL_ROWS)


def _pack_grads(g, names):
    big = jnp.concatenate([_to_chunks(g[n], n).astype(BF) for n in names], axis=1)
    return jnp.swapaxes(big.reshape(4, 2, _rows_of(names), PACK_W), 0, 1)


def _unpack_state(big, names):
    out, off = {}, 0
    for n in names:
        sh, r = _shard_shape(n), PACK_ROWS[n]
        k = int(np.prod(sh))
        if len(sh) == 2 and sh[1] == PACK_W:
            out[n] = big[off:off + sh[0]]
        else:
            out[n] = big[off:off + r].reshape(-1)[:k].reshape(sh)
        off += r
    return out


def _unpack_small(small):
    out, flat, off = {}, small.reshape(-1), 0
    for n, sh in SMALL.items():
        k = int(np.prod(sh))
        out[n] = flat[off:off + k].reshape(sh)
        off += k
    out["loss"] = flat[off]
    return out


def _pack_gather(w, names):
    pieces = []
    for n in names:
        if n == "conv_w":
            pieces.append(_as_rows(jnp.concatenate([p.reshape(-1) for p in _split(w[n], 3)]), PACK_ROWS[n]))
        else:
            pieces.append(_shard_rows(w[n], n).astype(BF))
    return jnp.concatenate(pieces, axis=0)


def _unpack_gather(gathered, names):
    g = gathered.reshape(NDEV, _rows_of(names), PACK_W)
    out, off = {}, 0
    for n in names:
        r = PACK_ROWS[n]
        sh = _shard_shape(n)
        if n == "conv_w":
            k = int(np.prod(sh))
            terms = g[:, off:off + r].reshape(NDEV, -1)[:, :3 * k].astype(F32).reshape(NDEV, 3, k)
            out[n] = _from_chunks(terms[:, 0] + terms[:, 1] + terms[:, 2], n)
        elif len(sh) == 2 and sh[1] == PACK_W:
            out[n] = _from_chunks(g[:, off:off + sh[0]], n)
        else:
            out[n] = _from_chunks(g[:, off:off + r], n)
        off += r
    return out


PARAMS = ["c_ctx", "w_ada", "b_ada", "norm_pre", "norm_post", "w_in", "b_merge", "pool_w", "pool_scale", "conv_w", "conv_b",
          "dt_bias", "a_log", "d_skip", "ssd_norm", "w_proj_pool", "w_proj_ssd", "w_out"]


def kernel(x, c, ctx, c_ctx, w_ada, b_ada, norm_pre, norm_post, w_in, b_merge, pool_w, pool_scale, conv_w, conv_b, dt_bias, a_log, d_skip, ssd_norm, w_proj_pool, w_proj_ssd, w_out, loss_target, m_c_ctx, m_w_ada, m_b_ada, m_norm_pre, m_norm_post, m_w_in, m_b_merge, m_pool_w, m_pool_scale, m_conv_w, m_conv_b, m_dt_bias, m_a_log, m_d_skip, m_ssd_norm, m_w_proj_pool, m_w_proj_ssd, m_w_out, v_c_ctx, v_w_ada, v_b_ada, v_norm_pre, v_norm_post, v_w_in, v_b_merge, v_pool_w, v_pool_scale, v_conv_w, v_conv_b, v_dt_bias, v_a_log, v_d_skip, v_ssd_norm, v_w_proj_pool, v_w_proj_ssd, v_w_out):
    given = dict(locals())
    shapes = {n: given[n].shape for n in PARAMS}

    def local(prefix):
        t = {n: (given[prefix + n] if n == "c_ctx" else given[prefix + n][0]) for n in PARAMS}
        for n in TRANSPOSED:
            t[n] = t[n].T
        return {n: t[n].reshape(_shard_shape(n) if n in BIG else SMALL[n]) for n in PARAMS}

    w, m, v = local(""), local("m_"), local("v_")

    W = _unpack_gather(_all_gather(_pack_gather(w, GATHER_EARLY)), GATHER_EARLY)
    for n in SMALL:
        W[n] = w[n]
    lanes, grad_x, g, recv_early = _local_step(x, c, ctx, loss_target, W, late_shard=_pack_gather(w, GATHER_LATE),
                                               exchange=True)
    gb = _pack_grads(g, GRADS_LATE)
    got, recv_small = _pair_exchange(gb, _pack_small(g, (0.5 / D) * jnp.sum(lanes)), "grads_pair_exchange_late")
    late = _chip_exchange_side(_pair_add(gb, got, "grads_pair_add_late"))
    res = [{} for _ in range(4)]
    *early, recv_late = _adamw(recv_early, *[_pack_state(s, GRADS_EARLY) for s in (w, m, v)], "adamw_early", side=late)
    for r, t in zip(res, early):
        r.update(_unpack_state(t, GRADS_EARLY))
    for r, t in zip(res, _adamw(recv_late, *[_pack_state(s, GRADS_LATE) for s in (w, m, v)], "adamw_late")):
        r.update(_unpack_state(t, GRADS_LATE))
    for r, t in zip(res, _adamw(recv_small, *[_pack_small(s) for s in (w, m, v)], "adamw_small")):
        r.update(_unpack_small(t))
    outs = [res[0]["loss"], grad_x]
    for r in res:
        for n in TRANSPOSED:
            r[n] = r[n].T
        outs += [r[n].reshape(shapes[n]) for n in PARAMS]
    return tuple(outs)
```

```python
import functools

import numpy as np
import jax
import jax.numpy as jnp
from jax import lax
from jax.experimental import pallas as pl
from jax.experimental.pallas import tpu as pltpu

F32, BF = jnp.float32, jnp.bfloat16

D = 1024
GRID_W = 64
EPS = 1e-6
POOL_WINDOWS = (2, 4, 8, 16)
PGW = 256
DIN = 2048
HEAD = 64
NST = 128
NG = 4
HPG = 8
GWID = HPG * HEAD
Q = 128
CONV_DIM = 3072
OFF_GATE, OFF_XBC, OFF_DT, IN_COLS = 4096, 6144, 9216, 9280
NDEV = 8
ADAM_LR, ADAM_B1, ADAM_B2, ADAM_EPS, ADAM_WD, ADAM_STEP = 0.001, 0.9, 0.999, 1e-08, 0.01, 10

V7X_VMEM_LIMIT = 56 * 2 ** 20
ROW_TILE = 256


def _params(sem=None):
    return pltpu.CompilerParams(dimension_semantics=sem, vmem_limit_bytes=V7X_VMEM_LIMIT)


def _dot(a, b):
    return jnp.dot(a.astype(BF), b.astype(BF), preferred_element_type=F32)


def _dot_nt(a, b):
    return lax.dot_general(a.astype(BF), b.astype(BF), (((1,), (1,)), ((), ())), preferred_element_type=F32)


def _dot_tn(a, b):
    return lax.dot_general(a.astype(BF), b.astype(BF), (((0,), (0,)), ((), ())), preferred_element_type=F32)


def _split(a, n):
    parts = []
    for _ in range(n):
        p = a.astype(BF)
        parts.append(p)
        a = a - p.astype(F32)
    return parts


def _dot_sl(a, b01, n=3):
    parts = _split(a, n)
    m = a.shape[0]
    if n == 1 or m % 16:
        return sum(jnp.dot(p, b01, preferred_element_type=F32) for p in parts)
    r = jnp.dot(jnp.concatenate(parts, axis=0), b01, preferred_element_type=F32)
    return sum(r[i * m:(i + 1) * m] for i in range(n))


def _dot_sr(a01, b, n=3):
    parts = _split(b, n)
    k = b.shape[1]
    if n == 1 or k % 128:
        return sum(jnp.dot(a01, p, preferred_element_type=F32) for p in parts)
    r = jnp.dot(a01, jnp.concatenate(parts, axis=1), preferred_element_type=F32)
    return sum(r[:, i * k:(i + 1) * k] for i in range(n))


def _sigmoid(x):
    return 1.0 / (1.0 + jnp.exp(-x))


class _SideCopies:
    NSEM = 8

    def __init__(self, inputs, out_shapes, make):
        self.inputs, self.out_shapes, self.make = list(inputs), list(out_shapes), make

    def scratch(self):
        return [pltpu.SemaphoreType.DMA((self.NSEM,)), pltpu.SemaphoreType.DMA((self.NSEM,)), pltpu.SemaphoreType.DMA]

    def start(self, grid, in_refs, out_refs, sems):
        @pl.when(functools.reduce(lambda p, q: p & q, [pl.program_id(i) == 0 for i in range(len(grid))]))
        def _():
            local, sends, _ = self.make(in_refs, out_refs, *sems, arrivals=False)
            for cp in local + sends:
                cp.start()

    def wait(self, grid, in_refs, out_refs, sems):
        @pl.when(functools.reduce(lambda p, q: p & q, [pl.program_id(i) == n - 1 for i, n in enumerate(grid)]))
        def _():
            local, sends, recvs = self.make(in_refs, out_refs, *sems)
            for cp in sends:
                cp.wait_send()
            for cp in recvs:
                cp.wait_recv()
            for cp in local:
                cp.wait()


def _matmul(a, b, out_dtype, name, tm=512, tn=512, tk=1024, bt=False, n=None, side=None):
    M, K = a.shape
    N = n if n is not None else (b.shape[0] if bt else b.shape[1])
    tm, tn, tk = min(tm, M), min(tn, N), min(tk, K)
    assert M % tm == 0 and N % tn == 0 and K % tk == 0, (a.shape, b.shape)
    nk = K // tk
    grid = (M // tm, N // tn, nk)
    n_si, n_so = (len(side.inputs), len(side.out_shapes)) if side else (0, 0)

    def body(*refs):
        a_ref, b_ref, o_ref = refs[0], refs[1], refs[2 + n_si]
        acc = refs[3 + n_si + n_so]
        side_refs = (refs[2:2 + n_si], refs[3 + n_si:3 + n_si + n_so], refs[4 + n_si + n_so:])
        if side:
            side.start(grid, *side_refs)
        k = pl.program_id(2)
        p = _dot_nt(a_ref[...], b_ref[...]) if bt else _dot(a_ref[...], b_ref[...])

        @pl.when(k == 0)
        def _():
            acc[...] = p

        @pl.when(k > 0)
        def _():
            acc[...] += p

        @pl.when(k == nk - 1)
        def _():
            o_ref[...] = acc[...].astype(o_ref.dtype)

        if side:
            side.wait(grid, *side_refs)

    out = pl.pallas_call(
        body, name=name, grid=grid,
        in_specs=[pl.BlockSpec((tm, tk), lambda i, j, k: (i, k)),
                  pl.BlockSpec((tn, tk), lambda i, j, k: (j, k)) if bt else pl.BlockSpec((tk, tn), lambda i, j, k: (k, j))]
        + [ANY] * n_si,
        out_specs=(pl.BlockSpec((tm, tn), lambda i, j, k: (i, j)),) + (ANY,) * n_so,
        out_shape=(jax.ShapeDtypeStruct((M, N), out_dtype),) + tuple(side.out_shapes if side else ()),
        scratch_shapes=[pltpu.VMEM((tm, tn), F32)] + (side.scratch() if side else []),
        compiler_params=_params(("arbitrary",) * 3 if side else ("parallel", "parallel", "arbitrary")),
    )(a, b, *(side.inputs if side else ()))
    return out if side else out[0]


def _matmul_tn(a, g, name, ta=512, tn=512, tr=512):
    M, Ka = a.shape
    N = g.shape[1]
    ta, tn, tr = min(ta, Ka), min(tn, N), min(tr, M)
    assert M % tr == 0 and N % tn == 0 and Ka % ta == 0, (a.shape, g.shape)
    nr = M // tr

    def body(a_ref, g_ref, o_ref):
        k = pl.program_id(2)
        p = _dot_tn(a_ref[...], g_ref[...])

        @pl.when(k == 0)
        def _():
            o_ref[...] = p

        @pl.when(k > 0)
        def _():
            o_ref[...] += p

    return pl.pallas_call(
        body, name=name, grid=(Ka // ta, N // tn, nr),
        in_specs=[pl.BlockSpec((tr, ta), lambda i, j, k: (k, i)), pl.BlockSpec((tr, tn), lambda i, j, k: (k, j))],
        out_specs=pl.BlockSpec((ta, tn), lambda i, j, k: (i, j)),
        out_shape=jax.ShapeDtypeStruct((Ka, N), F32),
        compiler_params=_params(("parallel", "parallel", "arbitrary")),
    )(a, g)


def _dhx(pieces, ddt, w_inT, w_dtT, side=None):
    _, R, _ = pieces[0].shape
    tm = R // 4
    kb = 1024
    starts, nblk = [], []
    for p in pieces:
        starts.append(sum(nblk))
        nblk.append(p.shape[2] // kb)
    nk = sum(nblk)
    assert nk * kb == OFF_DT and R % 128 == 0
    npc = len(pieces)
    grid = (2, R // tm, nk)
    n_si, n_so = (len(side.inputs), len(side.out_shapes)) if side else (0, 0)

    def body(*refs):
        a_refs, dt_ref, w_ref, wdt_ref = refs[:npc], refs[npc], refs[npc + 1], refs[npc + 2]
        o_ref, acc = refs[npc + 3 + n_si], refs[npc + 4 + n_si + n_so]
        side_refs = (refs[npc + 3:npc + 3 + n_si], refs[npc + 4 + n_si:npc + 4 + n_si + n_so], refs[npc + 5 + n_si + n_so:])
        if side:
            side.start(grid, *side_refs)
        k = pl.program_id(2)

        @pl.when(k == 0)
        def _():
            acc[...] = _dot(dt_ref[0], wdt_ref[...])

        for p in range(npc):
            @pl.when((k >= starts[p]) & (k < starts[p] + nblk[p]))
            def _(p=p):
                acc[...] += _dot(a_refs[p][0], w_ref[...])

        @pl.when(k == nk - 1)
        def _():
            o_ref[0] = acc[...].astype(BF)

        if side:
            side.wait(grid, *side_refs)

    in_specs = [pl.BlockSpec((1, tm, kb), functools.partial(
        lambda e, t, k, s, nb: (e, t, jnp.clip(k - s, 0, nb - 1)), s=starts[p], nb=nblk[p])) for p in range(npc)]
    in_specs += [pl.BlockSpec((1, tm, 128), lambda e, t, k: (e, t, 0)),
                 pl.BlockSpec((kb, D), lambda e, t, k: (k, 0)),
                 pl.BlockSpec((128, D), lambda e, t, k: (0, 0))]
    out = pl.pallas_call(
        body, name="d_hx", grid=grid, in_specs=in_specs + [ANY] * n_si,
        out_specs=(pl.BlockSpec((1, tm, D), lambda e, t, k: (e, t, 0)),) + (ANY,) * n_so,
        out_shape=(jax.ShapeDtypeStruct((2, R, D), BF),) + tuple(side.out_shapes if side else ()),
        scratch_shapes=[pltpu.VMEM((tm, D), F32)] + (side.scratch() if side else []),
        compiler_params=_params(("arbitrary",) * 3 if side else ("parallel", "parallel", "arbitrary")),
    )(*pieces, ddt, w_inT, w_dtT, *(side.inputs if side else ()))
    return out if side else out[0]


def _adaln_fwd(c16, w_adaT_bf, b_ada):
    def body(c_ref, w_ref, b_ref, o_ref):
        cc = c_ref[...]
        o_ref[...] = _dot_nt(cc * _sigmoid(cc), w_ref[...]) + b_ref[...]

    return pl.pallas_call(body, name="adaln_fwd", out_shape=jax.ShapeDtypeStruct((16, 3 * D), F32),
                          compiler_params=_params())(c16, w_adaT_bf, b_ada)


def _adaln_bwd(acc_n, acc_f, mod16, c16, norm_pre, w_adaT_bf):
    def body(an_ref, af_ref, mod_ref, c_ref, np_ref, wt_ref, dw_ref, db_ref, sm_ref, dmod):
        npre = np_ref[...]
        dmod[...] = jnp.zeros_like(dmod)
        dnp = jnp.zeros((1, D), F32)
        dshift_c = jnp.zeros((1, D), F32)
        dgpre_c = jnp.zeros((1, D), F32)
        scale_c = mod_ref[2:3, D:2 * D]
        for e in range(2):
            dg_x, ds_x = an_ref[e, 0, 0:1, :], an_ref[e, 0, 1:2, :]
            dg_c, ds_c = an_ref[e, 1, 0:1, :], an_ref[e, 1, 1:2, :]
            dmod[e:e + 1, 0:D] = ds_x
            dmod[e:e + 1, D:2 * D] = dg_x * npre
            dmod[e:e + 1, 2 * D:3 * D] = af_ref[e, 0:1, :]
            dnp = dnp + dg_x * (1.0 + mod_ref[e:e + 1, D:2 * D]) + dg_c * (1.0 + scale_c)
            dshift_c = dshift_c + ds_c
            dgpre_c = dgpre_c + dg_c
        dmod[2:3, 0:D] = dshift_c
        dmod[2:3, D:2 * D] = dgpre_c * npre
        dm = dmod[...]
        cc = c_ref[...]
        sg = _sigmoid(cc)
        dw_ref[...] = _dot_tn(dm, cc * sg)
        db_ref[...] = jnp.zeros_like(db_ref)
        db_ref[0:1, :] = jnp.sum(dm, axis=0, keepdims=True)
        dsilu = sg * (1.0 + cc * (1.0 - sg))
        dcs = _dot(dm, wt_ref[...]) * dsilu
        sm_ref[...] = jnp.zeros_like(sm_ref)
        sm_ref[0:1, :] = dnp
        sm_ref[1:2, :] = dcs[2:3, :]

    return pl.pallas_call(
        body, name="adaln_bwd",
        out_shape=(jax.ShapeDtypeStruct((3 * D, D), F32), jax.ShapeDtypeStruct((16, 3 * D), F32),
                   jax.ShapeDtypeStruct((8, D), F32)),
        scratch_shapes=[pltpu.VMEM((16, 3 * D), F32)],
        compiler_params=_params())(acc_n, acc_f, mod16, c16, norm_pre, w_adaT_bf)


def _row_specs(L):
    nx = L // ROW_TILE
    return (pl.BlockSpec((1, ROW_TILE, D), lambda e, t: (e, jnp.minimum(t, nx - 1), 0)),
            pl.BlockSpec((1, ROW_TILE, D), lambda e, t: (e, jnp.maximum(t - nx, 0), 0)))


def _norm_mod_fwd(x, ctx, tab):
    L = x.shape[1]
    R = L + ctx.shape[1]
    nx = L // ROW_TILE

    def body(x_ref, c_ref, t_ref, o_ref):
        x = jnp.where(pl.program_id(1) < nx, x_ref[0], c_ref[0])
        r = lax.rsqrt(jnp.mean(x * x, axis=-1, keepdims=True) + EPS)
        t = t_ref[0, 0]
        o_ref[0] = (x * r * t[0:1] + t[1:2]).astype(BF)

    return pl.pallas_call(
        body, name="norm_mod_fwd", grid=(2, R // ROW_TILE),
        in_specs=[*_row_specs(L), pl.BlockSpec((1, 1, 8, D), lambda e, t: (e, t // nx, 0, 0))],
        out_specs=pl.BlockSpec((1, ROW_TILE, D), lambda e, t: (e, t, 0)),
        out_shape=jax.ShapeDtypeStruct((2, R, D), BF),
        compiler_params=_params(("parallel", "parallel")),
    )(x, ctx, tab)


def _norm_mod_bwd(dh, x, ctx, tab, dxo):
    L = x.shape[1]
    R = L + ctx.shape[1]
    nx = L // ROW_TILE

    def body(dh_ref, x_ref, c_ref, t_ref, dxo_ref, gx_ref, acc_ref):
        t = pl.program_id(1)
        x = jnp.where(t < nx, x_ref[0], c_ref[0])
        r = lax.rsqrt(jnp.mean(x * x, axis=-1, keepdims=True) + EPS)
        xn = x * r
        dh = dh_ref[0].astype(F32)

        @pl.when((t == 0) | (t == nx))
        def _():
            acc_ref[...] = jnp.zeros_like(acc_ref)

        acc_ref[0, 0, 0:1, :] += jnp.sum(dh * xn, axis=0, keepdims=True)
        acc_ref[0, 0, 1:2, :] += jnp.sum(dh, axis=0, keepdims=True)

        @pl.when(t < nx)
        def _():
            dxn = dh * t_ref[0, 0][0:1]
            dx = r * (dxn - xn * jnp.mean(dxn * xn, axis=-1, keepdims=True))
            gx_ref[0] = dxo_ref[0].astype(F32) + dx

    xspec, cspec = _row_specs(L)
    return pl.pallas_call(
        body, name="norm_mod_bwd", grid=(2, R // ROW_TILE),
        in_specs=[pl.BlockSpec((1, ROW_TILE, D), lambda e, t: (e, t, 0)), xspec, cspec,
                  pl.BlockSpec((1, 1, 8, D), lambda e, t: (e, t // nx, 0, 0)), xspec],
        out_specs=(xspec, pl.BlockSpec((1, 1, 8, D), lambda e, t: (e, t // nx, 0, 0))),
        out_shape=(jax.ShapeDtypeStruct((2, L, D), F32), jax.ShapeDtypeStruct((2, 2, 8, D), F32)),
        compiler_params=_params(("parallel", "arbitrary")),
    )(dh, x, ctx, tab, dxo)


POOL_TILE = 256


def _pool_tables(L):
    rows = L // GRID_W
    mats = np.zeros((4, POOL_TILE, POOL_TILE), np.float32)
    inv = np.zeros((4, L, 1), np.float32)
    for gi, k in enumerate(POOL_WINDOWS):
        lo, hi = k // 2, k - 1 - k // 2
        m = np.zeros((GRID_W, GRID_W), np.float32)
        for t in range(GRID_W):
            m[t, max(t - lo, 0):min(t + hi, GRID_W - 1) + 1] = 1.0
        for b in range(POOL_TILE // GRID_W):
            mats[gi, b * GRID_W:(b + 1) * GRID_W, b * GRID_W:(b + 1) * GRID_W] = m
        cnt_c = m.sum(1)
        cnt_r = np.array([min(r + hi, rows - 1) - max(r - lo, 0) + 1 for r in range(rows)], np.float32)
        inv[gi, :, 0] = (1.0 / (cnt_r[:, None] * cnt_c[None, :])).reshape(-1)
    matsT = np.ascontiguousarray(np.transpose(mats, (0, 2, 1)))
    return (jnp.asarray(mats, BF), jnp.asarray(matsT, BF), jnp.asarray(inv))


def _pool_cols(get_tile, mat, cs_ref, L, n):
    def step(i, carry):
        off = pl.multiple_of(i * POOL_TILE, POOL_TILE)
        t = get_tile(off)
        cs_ref[pl.ds(GRID_W + off, POOL_TILE), :] = (jnp.dot(mat, t.astype(BF), preferred_element_type=F32) if n == 1
                                                     else _dot_sr(mat, t.astype(F32), n))
        return carry

    lax.fori_loop(0, L // POOL_TILE, step, 0)
    cs_ref[pl.ds(0, GRID_W), :] = jnp.zeros((GRID_W, PGW), F32)

    def prefix(r, carry):
        o = pl.multiple_of(r * GRID_W, GRID_W)
        cs_ref[pl.ds(o + GRID_W, GRID_W), :] = cs_ref[pl.ds(o + GRID_W, GRID_W), :] + cs_ref[pl.ds(o, GRID_W), :]
        return carry

    lax.fori_loop(0, L // GRID_W, prefix, 0)


def _pool_rows(cs_ref, off, below, above, L):
    rows = L // GRID_W
    r0 = off // GRID_W
    parts = []
    for i in range(POOL_TILE // GRID_W):
        hi = pl.multiple_of(jnp.minimum(r0 + i + above + 1, rows) * GRID_W, GRID_W)
        lo = pl.multiple_of(jnp.maximum(r0 + i - below, 0) * GRID_W, GRID_W)
        parts.append(cs_ref[pl.ds(hi, GRID_W), :] - cs_ref[pl.ds(lo, GRID_W), :])
    return jnp.concatenate(parts, axis=0)


def _pool_fwd(proj3, pool_w_bf, pool_scale, tables, L):
    mats, _, inv = tables
    nt = L // POOL_TILE

    def body(v_ref, z_ref, pw_ref, ps_ref, m_ref, inv_ref, o_ref, cs_ref):
        _pool_cols(lambda off: v_ref[0, pl.ds(off, POOL_TILE), :], m_ref[0], cs_ref, L, 1)
        half = lax.shift_left(1, pl.program_id(1))

        def step(i, carry):
            off = pl.multiple_of(i * POOL_TILE, POOL_TILE)
            rows = pl.ds(off, POOL_TILE)
            v = v_ref[0, rows, :].astype(F32)
            diff = _pool_rows(cs_ref, off, half, half - 1, L) * inv_ref[0, rows, :] - v
            yp = _dot(diff, pw_ref[0])
            z = z_ref[0, rows, :].astype(F32)
            o_ref[0, rows, :] = (yp * ps_ref[...] * (z * _sigmoid(z))).astype(BF)
            return carry

        lax.fori_loop(0, nt, step, 0)

    return pl.pallas_call(
        body, name="pool_fwd", grid=(2, 4),
        in_specs=[pl.BlockSpec((1, L, PGW), lambda e, g: (e, 0, g)),
                  pl.BlockSpec((1, L, PGW), lambda e, g: (e, 0, 4 + g)),
                  pl.BlockSpec((1, PGW, PGW), lambda e, g: (g, 0, 0)),
                  pl.BlockSpec((1, PGW), lambda e, g: (0, g)),
                  pl.BlockSpec((1, POOL_TILE, POOL_TILE), lambda e, g: (g, 0, 0)),
                  pl.BlockSpec((1, L, 1), lambda e, g: (g, 0, 0))],
        out_specs=pl.BlockSpec((1, L, PGW), lambda e, g: (e, 0, g)),
        out_shape=jax.ShapeDtypeStruct((2, L, D), BF),
        scratch_shapes=[pltpu.VMEM((L + GRID_W, PGW), F32)],
        compiler_params=_params(("parallel", "parallel")),
    )(proj3, proj3, pool_w_bf, pool_scale, mats, inv)


def _pool_bwd(proj3, d_ypool, pool_w_bf, pool_wT_bf, pool_scale, tables, L):
    mats, matsT, inv = tables
    nt = L // POOL_TILE
    R = proj3.shape[1]

    def body(v_ref, z_ref, dy_ref, pw_ref, pwt_ref, ps_ref, m_ref, mt_ref, inv_ref,
             dv_ref, dz_ref, dpw_ref, acc_ref, cs_ref, dd_ref):
        e = pl.program_id(1)

        @pl.when(e == 0)
        def _():
            dpw_ref[...] = jnp.zeros_like(dpw_ref)
            acc_ref[...] = jnp.zeros_like(acc_ref)

        _pool_cols(lambda off: v_ref[0, pl.ds(off, POOL_TILE), :], m_ref[0], cs_ref, L, 1)
        half = lax.shift_left(1, pl.program_id(0))
        ps = ps_ref[...]

        def step(i, carry):
            off = pl.multiple_of(i * POOL_TILE, POOL_TILE)
            rows = pl.ds(off, POOL_TILE)
            v = v_ref[0, rows, :].astype(F32)
            diff = _pool_rows(cs_ref, off, half, half - 1, L) * inv_ref[0, rows, :] - v
            yp = _dot(diff, pw_ref[0])
            z = z_ref[0, rows, :].astype(F32)
            sg = _sigmoid(z)
            sz = z * sg
            dy = dy_ref[0, rows, :].astype(F32)
            dz_ref[0, rows, :] = (dy * yp * ps * (sg * (1.0 + z * (1.0 - sg)))).astype(BF)
            dys = dy * sz
            acc_ref[0, 0:1, :] += jnp.sum(dys * yp, axis=0, keepdims=True)
            dyp = dys * ps
            dpw_ref[0] += _dot_tn(diff, dyp)
            dd_ref[rows, :] = _dot(dyp, pwt_ref[0])
            return carry

        lax.fori_loop(0, nt, step, 0)
        _pool_cols(lambda off: dd_ref[pl.ds(off, POOL_TILE), :] * inv_ref[0, pl.ds(off, POOL_TILE), :],
                   mt_ref[0], cs_ref, L, 1)

        def step2(i, carry):
            off = pl.multiple_of(i * POOL_TILE, POOL_TILE)
            rows = pl.ds(off, POOL_TILE)
            dv_ref[0, rows, :] = (_pool_rows(cs_ref, off, half - 1, half, L) - dd_ref[rows, :]).astype(BF)
            return carry

        lax.fori_loop(0, nt, step2, 0)
        dv_ref[0, pl.ds(L, R - L), :] = jnp.zeros((R - L, PGW), BF)
        dz_ref[0, pl.ds(L, R - L), :] = jnp.zeros((R - L, PGW), BF)

    return pl.pallas_call(
        body, name="pool_bwd", grid=(4, 2),
        in_specs=[pl.BlockSpec((1, L, PGW), lambda g, e: (e, 0, g)),
                  pl.BlockSpec((1, L, PGW), lambda g, e: (e, 0, 4 + g)),
                  pl.BlockSpec((1, L, PGW), lambda g, e: (e, 0, g)),
                  pl.BlockSpec((1, PGW, PGW), lambda g, e: (g, 0, 0)),
                  pl.BlockSpec((1, PGW, PGW), lambda g, e: (g, 0, 0)),
                  pl.BlockSpec((1, PGW), lambda g, e: (0, g)),
                  pl.BlockSpec((1, POOL_TILE, POOL_TILE), lambda g, e: (g, 0, 0)),
                  pl.BlockSpec((1, POOL_TILE, POOL_TILE), lambda g, e: (g, 0, 0)),
                  pl.BlockSpec((1, L, 1), lambda g, e: (g, 0, 0))],
        out_specs=(pl.BlockSpec((1, R, PGW), lambda g, e: (e, 0, g)),
                   pl.BlockSpec((1, R, PGW), lambda g, e: (e, 0, g)),
                   pl.BlockSpec((1, PGW, PGW), lambda g, e: (g, 0, 0)),
                   pl.BlockSpec((1, 8, PGW), lambda g, e: (g, 0, 0))),
        out_shape=(jax.ShapeDtypeStruct((2, R, D), BF), jax.ShapeDtypeStruct((2, R, D), BF),
                   jax.ShapeDtypeStruct((4, PGW, PGW), F32), jax.ShapeDtypeStruct((4, 8, PGW), F32)),
        scratch_shapes=[pltpu.VMEM((L + GRID_W, PGW), F32), pltpu.VMEM((L, PGW), F32)],
        compiler_params=_params(("parallel", "arbitrary")),
    )(proj3, proj3, d_ypool, pool_w_bf, pool_wT_bf, pool_scale, mats, matsT, inv)


CONV_BLOCK = 128


CONV_CHUNK = 64
CONV_HALO = 8


def _halo_buf_init(buf, val, R):
    z = jnp.zeros((CONV_HALO, CONV_BLOCK), F32)
    buf[pl.ds(0, CONV_HALO), :] = z
    buf[pl.ds(CONV_HALO + R, CONV_HALO), :] = z
    if val is not None:
        buf[pl.ds(CONV_HALO, R), :] = val


def _chunk_taps(buf, start, offs, L):
    n = CONV_CHUNK + 2 * CONV_HALO
    ext = buf[pl.ds(start, n), :]
    out = []
    for off in offs:
        if off == 0:
            out.append(ext[CONV_HALO:CONV_HALO + CONV_CHUNK])
            continue
        r = pltpu.roll(ext, (-off) % n, 0)[CONV_HALO:CONV_HALO + CONV_CHUNK]
        lo, hi = (start, start + CONV_CHUNK - 1 + off) if off > 0 else (start + off, start + CONV_CHUNK - 1)
        if lo < L <= hi:
            t = start + lax.broadcasted_iota(jnp.int32, (CONV_CHUNK, 1), 0)
            r = jnp.where((t < L) == (t + off < L), r, 0.0)
        out.append(r)
    return out


def _fold8(x):
    return sum(x[i * 8:(i + 1) * 8] for i in range(CONV_CHUNK // 8))


def _conv_fwd(proj3, conv_w, conv_b, L):
    _, R, _ = proj3.shape
    cb0 = OFF_XBC // CONV_BLOCK

    def body(u_ref, w_ref, b_ref, o_ref, ubuf):
        _halo_buf_init(ubuf, u_ref[0].astype(F32), R)
        w = w_ref[...]
        b = b_ref[...]
        for start in range(0, R, CONV_CHUNK):
            taps = _chunk_taps(ubuf, start, (-2, -1, 0, 1), L)
            pre = b + sum(taps[k] * w[k:k + 1, :] for k in range(4))
            o_ref[0, pl.ds(start, CONV_CHUNK), :] = (pre * _sigmoid(pre)).astype(BF)

    return pl.pallas_call(
        body, name="conv_fwd", grid=(2, CONV_DIM // CONV_BLOCK),
        in_specs=[pl.BlockSpec((1, R, CONV_BLOCK), lambda e, j: (e, 0, cb0 + j)),
                  pl.BlockSpec((4, CONV_BLOCK), lambda e, j: (0, j)),
                  pl.BlockSpec((1, CONV_BLOCK), lambda e, j: (0, j))],
        out_specs=pl.BlockSpec((1, R, CONV_BLOCK), lambda e, j: (e, 0, j)),
        out_shape=jax.ShapeDtypeStruct((2, R, CONV_DIM), BF),
        scratch_shapes=[pltpu.VMEM((R + 2 * CONV_HALO, CONV_BLOCK), F32)],
        compiler_params=_params(("parallel", "parallel")),
    )(proj3, conv_w, conv_b)


def _conv_bwd(proj3, addends, scales, col0, ncols, in_maps, conv_w, conv_b, L, name):
    _, R, _ = proj3.shape
    cb0 = (OFF_XBC + col0) // CONV_BLOCK
    wb0 = col0 // CONV_BLOCK
    na = len(addends)
    scaled = [i for i in range(na) if scales[i] is not None]

    def body(*refs):
        u_ref, w_ref, b_ref = refs[0], refs[1], refs[2]
        a_refs = refs[3:3 + na]
        s_refs = dict(zip(scaled, refs[3 + na:3 + na + len(scaled)]))
        o_ref, acc_ref, ubuf, dbuf = refs[3 + na + len(scaled):]
        _halo_buf_init(ubuf, u_ref[0].astype(F32), R)
        _halo_buf_init(dbuf, None, R)
        w = w_ref[...]
        b = b_ref[...]
        scl = {i: s_refs[i][...] for i in scaled}
        sums = [jnp.zeros((8, CONV_BLOCK), F32) for _ in range(5)]
        for start in range(0, R, CONV_CHUNK):
            rows = pl.ds(start, CONV_CHUNK)
            taps = _chunk_taps(ubuf, start, (-2, -1, 0, 1), L)
            pre = b + sum(taps[k] * w[k:k + 1, :] for k in range(4))
            sg = _sigmoid(pre)
            dxbc = None
            for i, a in enumerate(a_refs):
                t = a[0, rows, :].astype(F32)
                t = t * scl[i] if i in scl else t
                dxbc = t if dxbc is None else dxbc + t
            dpre = dxbc * (sg * (1.0 + pre * (1.0 - sg)))
            dbuf[pl.ds(start + CONV_HALO, CONV_CHUNK), :] = dpre
            for k in range(4):
                sums[k] = sums[k] + _fold8(dpre * taps[k])
            sums[4] = sums[4] + _fold8(dpre)
        acc_ref[...] = jnp.zeros_like(acc_ref)
        for k in range(5):
            acc_ref[0, k:k + 1, :] = jnp.sum(sums[k], axis=0, keepdims=True)
        for start in range(0, R, CONV_CHUNK):
            d = _chunk_taps(dbuf, start, (2, 1, 0, -1), L)
            o_ref[0, pl.ds(start, CONV_CHUNK), :] = sum(d[k] * w[k:k + 1, :] for k in range(4)).astype(BF)

    in_specs = [pl.BlockSpec((1, R, CONV_BLOCK), lambda e, j: (e, 0, cb0 + j)),
                pl.BlockSpec((4, CONV_BLOCK), lambda e, j: (0, wb0 + j)),
                pl.BlockSpec((1, CONV_BLOCK), lambda e, j: (0, wb0 + j))]
    for m in in_maps:
        in_specs.append(pl.BlockSpec((1, R, CONV_BLOCK), functools.partial(lambda e, j, m: (e, 0, m(j)), m=m)))
    for i in scaled:
        in_specs.append(pl.BlockSpec((1, CONV_BLOCK), functools.partial(lambda e, j, m: (0, m(j)), m=in_maps[i])))
    return pl.pallas_call(
        body, name=name, grid=(2, ncols // CONV_BLOCK),
        in_specs=in_specs,
        out_specs=(pl.BlockSpec((1, R, CONV_BLOCK), lambda e, j: (e, 0, j)),
                   pl.BlockSpec((1, 8, CONV_BLOCK), lambda e, j: (e, 0, j))),
        out_shape=(jax.ShapeDtypeStruct((2, R, ncols), BF), jax.ShapeDtypeStruct((2, 8, ncols), F32)),
        scratch_shapes=[pltpu.VMEM((R + 2 * CONV_HALO, CONV_BLOCK), F32)] * 2,
        compiler_params=_params(("parallel", "parallel")),
    )(proj3, conv_w, conv_b, *addends, *[scales[i] for i in scaled])


def _softplus(x):
    e = jnp.exp(-jnp.abs(x))
    u = 1.0 + e
    return jnp.maximum(x, 0.0) + jnp.where(u == 1.0, e, e * jnp.log(u) / (u - 1.0))


def _to_local_mat(g, transpose=False):
    r = lax.broadcasted_iota(jnp.int32, (128, 128), 1 if transpose else 0)
    c = lax.broadcasted_iota(jnp.int32, (128, 128), 0 if transpose else 1)
    return ((c < 2 * HPG) & (r == jnp.right_shift(c, 3) * (NG * HPG) + g * HPG + (c & (HPG - 1)))).astype(BF)


def _dt_fwd(dt_raw, bias128):
    _, R, _ = dt_raw.shape

    def body(x_ref, b_ref, o_ref):
        dt = _softplus(x_ref[0] + b_ref[...])
        for g in range(NG):
            o_ref[0, g] = _dot_sl(dt, _to_local_mat(g))

    tr = R // 4
    return pl.pallas_call(
        body, name="dt_fwd", grid=(2, 4),
        in_specs=[pl.BlockSpec((1, tr, 128), lambda e, t: (e, t, 0)), pl.BlockSpec((1, 128), lambda e, t: (0, 0))],
        out_specs=pl.BlockSpec((1, NG, tr, 128), lambda e, t: (e, 0, t, 0)),
        out_shape=jax.ShapeDtypeStruct((2, NG, R, 128), F32),
        compiler_params=_params(("parallel", "parallel")),
    )(dt_raw, bias128)


def _dt_bwd(dt_raw, bias128, ddt_f, ddt_b):
    _, R, _ = dt_raw.shape

    def body(x_ref, b_ref, f_ref, g_ref, o_ref, acc_ref):
        ddt = sum(_dot_sl(f_ref[0, g] + g_ref[0, g], _to_local_mat(g, transpose=True)) for g in range(NG))
        d = ddt * _sigmoid(x_ref[0] + b_ref[...])
        o_ref[0] = d.astype(BF)

        @pl.when(pl.program_id(1) == 0)
        def _():
            acc_ref[...] = jnp.zeros_like(acc_ref)

        acc_ref[0, 0:1, :] += jnp.sum(d, axis=0, keepdims=True)

    tr = R // 4
    blk = pl.BlockSpec((1, tr, 128), lambda e, t: (e, t, 0))
    loc = pl.BlockSpec((1, NG, tr, 128), lambda e, t: (e, 0, t, 0))
    return pl.pallas_call(
        body, name="dt_bwd", grid=(2, 4),
        in_specs=[blk, pl.BlockSpec((1, 128), lambda e, t: (0, 0)), loc, loc],
        out_specs=(blk, pl.BlockSpec((1, 8, 128), lambda e, t: (e, 0, 0))),
        out_shape=(jax.ShapeDtypeStruct(dt_raw.shape, BF), jax.ShapeDtypeStruct((2, 8, 128), F32)),
        compiler_params=_params(("parallel", "arbitrary")),
    )(dt_raw, bias128, ddt_f, ddt_b)


GPS = 4


def _tri(d):
    i = lax.broadcasted_iota(jnp.int32, (Q, Q), 0)
    j = lax.broadcasted_iota(jnp.int32, (Q, Q), 1)
    return (i >= j) if d == 0 else (i <= j)


def _expand_mat(d):
    r = lax.broadcasted_iota(jnp.int32, (128, GWID), 0)
    c = lax.broadcasted_iota(jnp.int32, (128, GWID), 1)
    return (r == d * HPG + jnp.right_shift(c, 6)).astype(BF)


def _reduce_mat(d):
    r = lax.broadcasted_iota(jnp.int32, (GWID, 128), 0)
    c = lax.broadcasted_iota(jnp.int32, (GWID, 128), 1)
    return (c == d * HPG + jnp.right_shift(r, 6)).astype(BF)


def _ssd_chunk(d, dt, A, xs, B, C):
    mask = _tri(d)
    T = mask.astype(BF)
    Tt = _tri(1 - d).astype(BF)
    a = dt * A
    acs = _dot_sr(T, a)
    E = _expand_mat(d)
    dt_e = _dot_sl(dt, E, 2)
    acs_e = _dot_sl(acs, E, 2)
    alast_e = acs_e[Q - 1:Q, :] if d == 0 else acs_e[0:1, :]
    return dict(mask=mask, T=T, Tt=Tt, acs=acs, acsT=acs.T, dt_e=dt_e, acs_e=acs_e, lam=jnp.exp(acs_e),
                w=jnp.exp(alast_e - acs_e), decay=jnp.exp(alast_e), xt=xs * dt_e, CB=_dot_nt(C, B))


def _head_decay(q, d, hh):
    col = q["acs"][:, d * HPG + hh:d * HPG + hh + 1]
    row = q["acsT"][d * HPG + hh:d * HPG + hh + 1, :]
    return jnp.exp(jnp.where(q["mask"], col - row, -jnp.inf))


def _chunk_maps(NX, NS):
    cf = lambda s: lax.rem(s + NX, NS)
    cb = lambda s: NS - 1 - s
    return cf, cb


def _ssd_fwd(xbc, dt_loc, a_loc, L):
    _, R, _ = xbc.shape
    NX, NS = L // Q, R // Q
    cf, cb = _chunk_maps(NX, NS)

    def body(xs_f, b_f, c_f, dt_f, xs_b, b_b, c_b, dt_b, a_ref, y_f, hs_f, y_b, hs_b, hT):
        @pl.when(pl.program_id(2) == 0)
        def _():
            hT[...] = jnp.zeros_like(hT)

        lane = lax.broadcasted_iota(jnp.int32, (Q, 128), 1)
        for d, (xs_ref, b_ref, c_ref, dt_ref, y_ref, hs_ref) in enumerate(
                ((xs_f, b_f, c_f, dt_f, y_f, hs_f), (xs_b, b_b, c_b, dt_b, y_b, hs_b))):
            for gi in range(GPS):
                cols = slice(gi * GWID, (gi + 1) * GWID)
                xs = xs_ref[0, :, cols].astype(F32)
                B, C = b_ref[0, :, gi * NST:(gi + 1) * NST], c_ref[0, :, gi * NST:(gi + 1) * NST]
                q = _ssd_chunk(d, dt_ref[0, gi], a_ref[gi, 0:1, :], xs, B, C)
                h = hT[d, :, cols]
                hb = h.astype(BF)
                hs_ref[0, 0, :, cols] = hb
                parts = []
                for pr in range(HPG // 2):
                    xp = q["xt"][:, pr * 128:(pr + 1) * 128]
                    xst = jnp.concatenate([jnp.where(lane < HEAD, xp, 0.0), jnp.where(lane < HEAD, 0.0, xp)], axis=0)
                    mst = jnp.concatenate([(q["CB"] * _head_decay(q, d, 2 * pr)).astype(BF),
                                           (q["CB"] * _head_decay(q, d, 2 * pr + 1)).astype(BF)], axis=1)
                    parts.append(_dot(mst, xst))
                y_ref[0, :, cols] = jnp.concatenate(parts, axis=1) + _dot(C, hb) * q["lam"]
                hT[d, :, cols] = q["decay"] * h + _dot_tn(B, q["xt"] * q["w"])

    def spec(shape, imap):
        return pl.BlockSpec(shape, imap)

    bc0 = DIN // (GPS * NST)

    def ins(c):
        return [spec((1, Q, GPS * GWID), lambda e, g, s: (e, c(s), g)),
                spec((1, Q, GPS * NST), lambda e, g, s: (e, c(s), bc0 + g)),
                spec((1, Q, GPS * NST), lambda e, g, s: (e, c(s), bc0 + NG // GPS + g)),
                spec((1, GPS, Q, 128), lambda e, g, s: (e, g, c(s), 0))]

    def outs(c):
        return [spec((1, Q, GPS * GWID), lambda e, g, s: (e, c(s), g)),
                spec((1, 1, NST, GPS * GWID), lambda e, g, s: (e, c(s), 0, g))]

    yshape = jax.ShapeDtypeStruct((2, R, DIN), F32)
    hshape = jax.ShapeDtypeStruct((2, NS, NST, DIN), BF)
    return pl.pallas_call(
        body, name="ssd_fwd", grid=(2, NG // GPS, NS),
        in_specs=ins(cf) + ins(cb) + [spec((GPS, 8, 128), lambda e, g, s: (g, 0, 0))],
        out_specs=tuple(outs(cf) + outs(cb)),
        out_shape=(yshape, hshape, yshape, hshape),
        scratch_shapes=[pltpu.VMEM((2, NST, GPS * GWID), F32)],
        compiler_params=_params(("parallel", "parallel", "arbitrary")),
    )(xbc, xbc, xbc, dt_loc, xbc, xbc, xbc, dt_loc, a_loc)


def _ssd_bwd(xbc, dt_loc, a_loc, hs_f, hs_b, y_f, y_b, dy, L):
    _, R, _ = xbc.shape
    NX, NS = L // Q, R // Q
    cf0, cb0 = _chunk_maps(NX, NS)
    cf = lambda sp: cf0(NS - 1 - sp)
    cb = lambda sp: cb0(NS - 1 - sp)

    def body(xs_f, b_f, c_f, dt_f, hs_f_, dy_f, y_f_, xs_b, b_b, c_b, dt_b, hs_b_, dy_b, y_b_, a_ref,
             dxs_f, dbc_f, ddt_f, dxs_b, dbc_b, ddt_b, da_ref, dhT):
        @pl.when(pl.program_id(2) == 0)
        def _():
            dhT[...] = jnp.zeros_like(dhT)
            da_ref[...] = jnp.zeros_like(da_ref)

        lane = lax.broadcasted_iota(jnp.int32, (Q, 128), 1)
        row = lax.broadcasted_iota(jnp.int32, (Q, 128), 0)

        def one_chain(d, gi, xs_ref, b_ref, c_ref, dt_ref, hs_ref, dy_ref, y_ref, dxs_ref, dbc_ref, ddt_ref):
            cols = slice(gi * GWID, (gi + 1) * GWID)
            A = a_ref[gi, 0:1, :]
            xs, dt = xs_ref[0, :, cols].astype(F32), dt_ref[0, gi]
            B, C = b_ref[0, :, gi * NST:(gi + 1) * NST], c_ref[0, :, gi * NST:(gi + 1) * NST]
            q = _ssd_chunk(d, dt, A, xs, B, C)
            xt, lam, w, decay = q["xt"], q["lam"], q["w"], q["decay"]
            H = hs_ref[0, 0, :, cols]
            dyv = dy_ref[0, :, cols].astype(F32)
            dh = dhT[d, :, cols]
            dZ = dyv * lam
            dC = _dot_nt(dZ, H)
            dH = _dot_tn(C, dZ)
            U = _dot(B, dh)
            xw = xt * w
            dxt = U * w
            dalast_e = (jnp.sum(U * xw, axis=0, keepdims=True)
                        + decay * jnp.sum(dh * H.astype(F32), axis=0, keepdims=True))
            dB = _dot_nt(xw, dh)
            dCB = jnp.zeros((Q, Q), F32)
            dxt_parts = []
            for pr in range(HPG // 2):
                xp = xt[:, pr * 128:(pr + 1) * 128]
                dyp = dyv[:, pr * 128:(pr + 1) * 128]
                L0, L1 = _head_decay(q, d, 2 * pr), _head_decay(q, d, 2 * pr + 1)
                dyst = jnp.concatenate([jnp.where(lane < HEAD, dyp, 0.0), jnp.where(lane < HEAD, 0.0, dyp)], axis=0)
                mst = jnp.concatenate([(q["CB"] * L0).astype(BF), (q["CB"] * L1).astype(BF)], axis=0)
                dxt_parts.append(_dot_tn(mst, dyst))
                dmst = _dot_nt(dyst, xp)
                dCB = dCB + dmst[:Q] * L0 + dmst[Q:] * L1
            dxt_diag = jnp.concatenate(dxt_parts, axis=1)
            dC = dC + _dot(dCB, B)
            dB = dB + _dot_tn(dCB, C)
            Rm = _reduce_mat(d)
            dacs = _dot_sl(dyv * y_ref[0, :, cols] - xt.astype(BF).astype(F32) * dxt_diag - U * xw, Rm, 2)
            dxt = dxt + dxt_diag
            dal = _dot_sl(jnp.broadcast_to(dalast_e, (8, GWID)), Rm, 2)[0:1, :]
            dacs = dacs + jnp.where(row == (Q - 1 if d == 0 else 0), dal, 0.0)
            da = _dot_sr(q["Tt"], dacs, 2)
            ddt_ref[0, gi] = da * A + _dot_sl(dxt * xs, Rm, 2)
            da_ref[0, gi, 0:1, :] += jnp.sum(da * dt, axis=0, keepdims=True)
            dxs_ref[0, :, cols] = (dxt * q["dt_e"]).astype(BF)
            dbc_ref[0, :, gi * 2 * NST:(gi + 1) * 2 * NST] = jnp.concatenate([dB, dC], axis=1).astype(BF)
            dhT[d, :, cols] = decay * dh + dH

        for gi in range(GPS):
            one_chain(0, gi, xs_f, b_f, c_f, dt_f, hs_f_, dy_f, y_f_, dxs_f, dbc_f, ddt_f)
            one_chain(1, gi, xs_b, b_b, c_b, dt_b, hs_b_, dy_b, y_b_, dxs_b, dbc_b, ddt_b)

    def spec(shape, imap):
        return pl.BlockSpec(shape, imap)

    bc0 = DIN // (GPS * NST)

    def ins(c):
        return [spec((1, Q, GPS * GWID), lambda e, g, s: (e, c(s), g)),
                spec((1, Q, GPS * NST), lambda e, g, s: (e, c(s), bc0 + g)),
                spec((1, Q, GPS * NST), lambda e, g, s: (e, c(s), bc0 + NG // GPS + g)),
                spec((1, GPS, Q, 128), lambda e, g, s: (e, g, c(s), 0)),
                spec((1, 1, NST, GPS * GWID), lambda e, g, s: (e, c(s), 0, g)),
                spec((1, Q, GPS * GWID), lambda e, g, s: (e, c(s), g)),
                spec((1, Q, GPS * GWID), lambda e, g, s: (e, c(s), g))]

    def outs(c):
        return [spec((1, Q, GPS * GWID), lambda e, g, s: (e, c(s), g)),
                spec((1, Q, GPS * 2 * NST), lambda e, g, s: (e, c(s), g)),
                spec((1, GPS, Q, 128), lambda e, g, s: (e, g, c(s), 0))]

    s_xs = jax.ShapeDtypeStruct((2, R, DIN), BF)
    s_bc = jax.ShapeDtypeStruct((2, R, 2 * NG * NST), BF)
    s_dt = jax.ShapeDtypeStruct((2, NG, R, 128), F32)
    return pl.pallas_call(
        body, name="ssd_bwd", grid=(2, NG // GPS, NS),
        in_specs=ins(cf) + ins(cb) + [spec((GPS, 8, 128), lambda e, g, s: (g, 0, 0))],
        out_specs=tuple(outs(cf) + outs(cb) + [spec((1, GPS, 8, 128), lambda e, g, s: (e, g, 0, 0))]),
        out_shape=(s_xs, s_bc, s_dt, s_xs, s_bc, s_dt, jax.ShapeDtypeStruct((2, NG, 8, 128), F32)),
        scratch_shapes=[pltpu.VMEM((2, NST, GPS * GWID), F32)],
        compiler_params=_params(("parallel", "parallel", "arbitrary")),
    )(xbc, xbc, xbc, dt_loc, hs_f, dy, y_f, xbc, xbc, xbc, dt_loc, hs_b, dy, y_b, a_loc)


def _ssd_post_fwd(y_f, y_b, xbc, proj3, dskip_e, ssd_norm, L):
    def body(yf_ref, yb_ref, xs_ref, z_ref, ds_ref, w_ref, o_ref, y2_ref):
        y2 = yf_ref[0] + yb_ref[0] + ds_ref[...] * xs_ref[0].astype(F32)
        y2_ref[0] = y2.astype(BF)
        z = z_ref[0].astype(F32)
        u = y2 * (z * _sigmoid(z))
        parts = []
        for g in range(NG):
            ug = u[:, g * GWID:(g + 1) * GWID]
            parts.append(ug * lax.rsqrt(jnp.mean(ug * ug, axis=-1, keepdims=True) + EPS))
        o_ref[0] = (jnp.concatenate(parts, axis=1) * w_ref[...]).astype(BF)

    blk = lambda c: pl.BlockSpec((1, ROW_TILE, DIN), lambda e, t: (e, t, c))
    vec = pl.BlockSpec((1, DIN), lambda e, t: (0, 0))
    return pl.pallas_call(
        body, name="ssd_post_fwd", grid=(2, L // ROW_TILE),
        in_specs=[blk(0), blk(0), blk(0), blk(1), vec, vec],
        out_specs=(blk(0), blk(0)),
        out_shape=(jax.ShapeDtypeStruct((2, L, DIN), BF), jax.ShapeDtypeStruct((2, L, DIN), BF)),
        compiler_params=_params(("parallel", "parallel")),
    )(y_f, y_b, xbc, proj3, dskip_e, ssd_norm)


def _ssd_post_bwd(d_yn, y2b, xbc, proj3, ssd_norm, L):
    _, R, _ = xbc.shape
    nx = L // ROW_TILE

    def body(dyn_ref, y2_ref, xs_ref, z_ref, w_ref, dy_ref, dz_ref, acc_ref):
        t = pl.program_id(1)

        @pl.when(t == 0)
        def _():
            acc_ref[...] = jnp.zeros_like(acc_ref)

        @pl.when(t >= nx)
        def _():
            dy_ref[...] = jnp.zeros_like(dy_ref)
            dz_ref[...] = jnp.zeros_like(dz_ref)

        @pl.when(t < nx)
        def _():
            xs = xs_ref[0].astype(F32)
            y2 = y2_ref[0].astype(F32)
            z = z_ref[0].astype(F32)
            sg = _sigmoid(z)
            sz = z * sg
            u = y2 * sz
            dyn = dyn_ref[0].astype(F32)
            dun = dyn * w_ref[...]
            uh_parts, du_parts = [], []
            for g in range(NG):
                sl = slice(g * GWID, (g + 1) * GWID)
                ug = u[:, sl]
                rg = lax.rsqrt(jnp.mean(ug * ug, axis=-1, keepdims=True) + EPS)
                uh = ug * rg
                dg = dun[:, sl]
                du_parts.append(rg * (dg - uh * jnp.mean(dg * uh, axis=-1, keepdims=True)))
                uh_parts.append(uh)
            du = jnp.concatenate(du_parts, axis=1)
            uh = jnp.concatenate(uh_parts, axis=1)
            dy2 = du * sz
            dy_ref[0] = dy2.astype(BF)
            dz_ref[0] = (du * y2 * (sg * (1.0 + z * (1.0 - sg)))).astype(BF)
            acc_ref[0, 0:1, :] += jnp.sum(dyn * uh, axis=0, keepdims=True)
            acc_ref[0, 1:2, :] += jnp.sum(dy2 * xs, axis=0, keepdims=True)

    xmap = lambda c: (lambda e, t: (e, jnp.minimum(t, nx - 1), c))
    blk = lambda c: pl.BlockSpec((1, ROW_TILE, DIN), xmap(c))
    oblk = pl.BlockSpec((1, ROW_TILE, DIN), lambda e, t: (e, t, 0))
    vec = pl.BlockSpec((1, DIN), lambda e, t: (0, 0))
    return pl.pallas_call(
        body, name="ssd_post_bwd", grid=(2, R // ROW_TILE),
        in_specs=[blk(0), blk(0), blk(0), blk(1), vec],
        out_specs=(oblk, oblk, pl.BlockSpec((1, 8, DIN), lambda e, t: (e, 0, 0))),
        out_shape=(jax.ShapeDtypeStruct((2, R, DIN), BF), jax.ShapeDtypeStruct((2, R, DIN), BF),
                   jax.ShapeDtypeStruct((2, 8, DIN), F32)),
        compiler_params=_params(("parallel", "arbitrary")),
    )(d_yn, y2b, xbc, proj3, ssd_norm)


def _merge_fwd(proj3, P, S, b_merge, L):
    def body(gp_ref, p_ref, s_ref, b_ref, o_ref):
        gt = _sigmoid(gp_ref[0].astype(F32) + b_ref[...])
        o_ref[0] = (gt[:, :D] * p_ref[0].astype(F32) + gt[:, D:] * s_ref[0].astype(F32)).astype(BF)

    blk = pl.BlockSpec((1, ROW_TILE, D), lambda e, t: (e, t, 0))
    return pl.pallas_call(
        body, name="merge_fwd", grid=(2, L // ROW_TILE),
        in_specs=[pl.BlockSpec((1, ROW_TILE, 2 * D), lambda e, t: (e, t, OFF_GATE // (2 * D))), blk, blk,
                  pl.BlockSpec((1, 2 * D), lambda e, t: (0, 0))],
        out_specs=blk, out_shape=jax.ShapeDtypeStruct((2, L, D), BF),
        compiler_params=_params(("parallel", "parallel")),
    )(proj3, P, S, b_merge)


def _merge_bwd(d_merged, proj3, P, S, b_merge, L):
    _, R, _ = proj3.shape
    nx = L // ROW_TILE

    def body(dm_ref, gp_ref, p_ref, s_ref, b_ref, dp_ref, ds_ref, dg_ref, acc_ref):
        t = pl.program_id(1)

        @pl.when(t == 0)
        def _():
            acc_ref[...] = jnp.zeros_like(acc_ref)

        @pl.when(t >= nx)
        def _():
            dg_ref[...] = jnp.zeros_like(dg_ref)

        @pl.when(t < nx)
        def _():
            gt = _sigmoid(gp_ref[0].astype(F32) + b_ref[...])
            dm = dm_ref[0].astype(F32)
            g1, g2 = gt[:, :D], gt[:, D:]
            dp_ref[0] = (dm * g1).astype(BF)
            ds_ref[0] = (dm * g2).astype(BF)
            dgp = jnp.concatenate([dm * p_ref[0].astype(F32) * g1 * (1.0 - g1),
                                   dm * s_ref[0].astype(F32) * g2 * (1.0 - g2)], axis=1)
            dg_ref[0] = dgp.astype(BF)
            acc_ref[0, 0:1, :] += jnp.sum(dgp, axis=0, keepdims=True)

    xmap = lambda e, t: (e, jnp.minimum(t, nx - 1), 0)
    blk = pl.BlockSpec((1, ROW_TILE, D), xmap)
    return pl.pallas_call(
        body, name="merge_bwd", grid=(2, R // ROW_TILE),
        in_specs=[blk, pl.BlockSpec((1, ROW_TILE, 2 * D), lambda e, t: (e, jnp.minimum(t, nx - 1), OFF_GATE // (2 * D))),
                  blk, blk, pl.BlockSpec((1, 2 * D), lambda e, t: (0, 0))],
        out_specs=(blk, blk, pl.BlockSpec((1, ROW_TILE, 2 * D), lambda e, t: (e, t, 0)),
                   pl.BlockSpec((1, 8, 2 * D), lambda e, t: (e, 0, 0))),
        out_shape=(jax.ShapeDtypeStruct((2, L, D), BF), jax.ShapeDtypeStruct((2, L, D), BF),
                   jax.ShapeDtypeStruct((2, R, 2 * D), BF), jax.ShapeDtypeStruct((2, 8, 2 * D), F32)),
        compiler_params=_params(("parallel", "arbitrary")),
    )(d_merged, proj3, P, S, b_merge)


def _final(out3, x, tgt, gtab, norm_post, L):
    def body(o_ref, x_ref, t_ref, g_ref, n_ref, dxo_ref, do_ref, acc_ref):
        @pl.when(pl.program_id(1) == 0)
        def _():
            acc_ref[...] = jnp.zeros_like(acc_ref)

        o = o_ref[0].astype(F32)
        gate = g_ref[0, 0:1, :]
        npost = n_ref[...]
        r2 = lax.rsqrt(jnp.mean(o * o, axis=-1, keepdims=True) + EPS)
        nh = o * r2
        on = nh * npost
        err = x_ref[0] + gate * on - t_ref[0]
        dxo = err * (1.0 / D)
        dxo_ref[0] = dxo.astype(BF)
        dnh = dxo * gate * npost
        do_ref[0] = (r2 * (dnh - nh * jnp.mean(dnh * nh, axis=-1, keepdims=True))).astype(BF)
        acc_ref[0, 0:1, :] += jnp.sum(dxo * on, axis=0, keepdims=True)
        acc_ref[0, 1:2, :] += jnp.sum(dxo * gate * nh, axis=0, keepdims=True)
        acc_ref[0, 2:3, :] += jnp.sum(err * err, axis=0, keepdims=True)

    blk = pl.BlockSpec((1, ROW_TILE, D), lambda e, t: (e, t, 0))
    return pl.pallas_call(
        body, name="final", grid=(2, L // ROW_TILE),
        in_specs=[blk, blk, blk, pl.BlockSpec((1, 8, D), lambda e, t: (e, 0, 0)),
                  pl.BlockSpec((1, D), lambda e, t: (0, 0))],
        out_specs=(blk, blk, pl.BlockSpec((1, 8, D), lambda e, t: (e, 0, 0))),
        out_shape=(jax.ShapeDtypeStruct((2, L, D), BF), jax.ShapeDtypeStruct((2, L, D), BF),
                   jax.ShapeDtypeStruct((2, 8, D), F32)),
        compiler_params=_params(("parallel", "arbitrary")),
    )(out3, x, tgt, gtab, norm_post)


def _local_step(x, c, ctx, loss_target, W, late_shard=None, exchange=False):
    nb, L, _ = x.shape
    LC = ctx.shape[1]
    R = L + LC
    assert nb == 2 and L % ROW_TILE == 0 and LC % Q == 0 and L % POOL_TILE == 0
    w_inT = W["w_in"]
    w_dtT = jnp.pad(w_inT[OFF_DT:], ((0, 64), (0, 0)))
    tables = _pool_tables(L)
    tr, tl = (2 * R) // 8, (2 * L) // 8

    c16 = jnp.zeros((16, D), F32).at[0:2].set(c).at[2].set(W["c_ctx"])
    mod16 = _adaln_fwd(c16, W["w_ada"], W["b_ada"])
    shift, scale, gate = mod16[:, :D], mod16[:, D:2 * D], mod16[:, 2 * D:]
    npre = W["norm_pre"]
    tab = jnp.zeros((2, 2, 8, D), F32)
    for e in range(2):
        tab = tab.at[e, 0, 0].set(npre[0] * (1.0 + scale[e])).at[e, 0, 1].set(shift[e])
        tab = tab.at[e, 1, 0].set(npre[0] * (1.0 + scale[2])).at[e, 1, 1].set(shift[2])
    gtab = jnp.zeros((2, 8, D), F32).at[:, 0].set(gate[0:2])

    hx = _norm_mod_fwd(x, ctx, tab)
    hx2 = hx.reshape(2 * R, D)
    if late_shard is None:
        proj = _matmul(hx2, w_inT, BF, "proj_main", tm=tr, tn=1024, bt=True, n=OFF_DT)
    else:
        proj, late = _matmul(hx2, w_inT, BF, "proj_main", tm=tr, tn=1024, bt=True, n=OFF_DT, side=_gather_side(late_shard))
        W = {**W, **_unpack_gather(late, GATHER_LATE)}
    proj3 = proj.reshape(2, R, OFF_DT)
    dt_raw = _matmul(hx2, w_dtT, F32, "proj_dt", tm=tr, bt=True).reshape(2, R, 128)
    ypool = _pool_fwd(proj3, W["pool_w"], W["pool_scale"], tables, L)
    xbc = _conv_fwd(proj3, W["conv_w"], W["conv_b"], L)
    bias128 = jnp.pad(W["dt_bias"].reshape(1, 64), ((0, 0), (0, 64)))
    dt_loc = _dt_fwd(dt_raw, bias128)
    A = -jnp.exp(W["a_log"].reshape(2, NG, HPG))
    a_loc = jnp.zeros((NG, 8, 128), F32).at[:, 0, :16].set(A.transpose(1, 0, 2).reshape(NG, 16))
    y_f, hs_f, y_b, hs_b = _ssd_fwd(xbc, dt_loc, a_loc, L)
    dskip_e = jnp.repeat(W["d_skip"].reshape(1, 32), HEAD, axis=1)
    yn, y2b = _ssd_post_fwd(y_f, y_b, xbc, proj3, dskip_e, W["ssd_norm"], L)
    ypool2, yn2 = ypool.reshape(2 * L, D), yn.reshape(2 * L, DIN)
    P = _matmul(ypool2, W["w_proj_pool"], BF, "proj_pool", tm=tl, tn=1024).reshape(2, L, D)
    S = _matmul(yn2, W["w_proj_ssd"], BF, "proj_ssd", tm=tl, tn=1024).reshape(2, L, D)
    merged = _merge_fwd(proj3, P, S, W["b_merge"], L)
    merged2 = merged.reshape(2 * L, D)
    out3 = _matmul(merged2, W["w_out"], BF, "proj_out", tm=tl, tn=1024).reshape(2, L, D)
    dxo, dout, acc_f = _final(out3, x, loss_target, gtab, W["norm_post"], L)

    dout2 = dout.reshape(2 * L, D)
    g = {}
    g["w_out"] = _matmul_tn(merged2, dout2, "dw_out", ta=1024, tn=1024, tr=4 * tl)
    d_merged = _matmul(dout2, W["w_out"], BF, "d_merged", tm=tl, tn=1024, bt=True).reshape(2, L, D)
    dP, dS, dgp, acc_m = _merge_bwd(d_merged, proj3, P, S, W["b_merge"], L)
    dP2, dS2 = dP.reshape(2 * L, D), dS.reshape(2 * L, D)
    g["w_proj_pool"] = _matmul_tn(ypool2, dP2, "dw_proj_pool", ta=1024, tn=1024, tr=4 * tl)
    g["w_proj_ssd"] = _matmul_tn(yn2, dS2, "dw_proj_ssd", ta=1024, tn=1024, tr=4 * tl)
    d_ypool = _matmul(dP2, W["w_proj_pool"], BF, "d_ypool", tm=tl, tn=1024, bt=True).reshape(2, L, D)
    d_yn = _matmul(dS2, W["w_proj_ssd"], BF, "d_yn", tm=tl, tn=1024, bt=True).reshape(2, L, DIN)
    dv, dzp, g["pool_w"], acc_p = _pool_bwd(proj3, d_ypool, W["pool_w"], jnp.swapaxes(W["pool_w"], 1, 2),
                                            W["pool_scale"], tables, L)
    dy2, dzs, acc_s = _ssd_post_bwd(d_yn, y2b, xbc, proj3, W["ssd_norm"], L)
    dxs_f, dbc_f, ddt_f, dxs_b, dbc_b, ddt_b, acc_a = _ssd_bwd(xbc, dt_loc, a_loc, hs_f, hs_b, y_f, y_b, dy2, L)
    ident = lambda j: j
    dxr_xs, acc_cx = _conv_bwd(proj3, [dxs_f, dxs_b, dy2], [None, None, dskip_e], 0, DIN, [ident, ident, ident],
                               W["conv_w"], W["conv_b"], L, "conv_bwd_xs")
    bcmap = lambda j: 2 * lax.rem(j, NG) + j // NG
    dxr_bc, acc_cb = _conv_bwd(proj3, [dbc_f, dbc_b], [None, None], DIN, 2 * NG * NST, [bcmap, bcmap],
                               W["conv_w"], W["conv_b"], L, "conv_bwd_bc")
    ddtr, acc_d = _dt_bwd(dt_raw, bias128, ddt_f, ddt_b)
    pieces = [dv, dzp, dzs, dgp, dxr_xs, dxr_bc]
    dw_rows = [_matmul_tn(p.reshape(2 * R, p.shape[2]), hx2, "dw_in_%d" % i, ta=1024, tn=1024, tr=4 * tr)
               for i, p in enumerate(pieces)]
    dw_rows.append(_matmul_tn(ddtr.reshape(2 * R, 128), hx2, "dw_in_dt", ta=128, tn=1024, tr=tr)[:64])
    g["w_in"] = jnp.concatenate(dw_rows, axis=0)
    acc_c = jnp.concatenate([acc_cx[0] + acc_cx[1], acc_cb[0] + acc_cb[1]], axis=1)
    g["conv_w"] = acc_c[0:4]
    g["conv_b"] = acc_c[4:5]
    if exchange:
        gb = _pack_grads(g, GRADS_EARLY)
        pair = _pair_add(gb, _pair_exchange(gb, None, "grads_pair_exchange_early"), "grads_pair_add_early")
        dh, recv_early = _dhx(pieces, ddtr, w_inT, w_dtT, side=_chip_exchange_side(pair))
    else:
        dh, recv_early = _dhx(pieces, ddtr, w_inT, w_dtT), None
    grad_x, acc_n = _norm_mod_bwd(dh, x, ctx, tab, dxo)
    g["w_ada"], db_rows, sm_rows = _adaln_bwd(acc_n, acc_f, mod16, c16, npre, W["w_ada"])

    g["b_ada"] = db_rows[0:1]
    g["norm_pre"] = sm_rows[0:1]
    g["c_ctx"] = sm_rows[1]
    g["norm_post"] = acc_f[0, 1:2] + acc_f[1, 1:2]
    g["b_merge"] = acc_m[0, 0:1] + acc_m[1, 0:1]
    g["pool_scale"] = acc_p[:, 0, :].reshape(1, D)
    g["dt_bias"] = (acc_d[0, 0, :64] + acc_d[1, 0, :64]).reshape(2, 32)
    dA = (acc_a[0, :, 0, :16] + acc_a[1, :, 0, :16]).reshape(NG, 2, HPG).transpose(1, 0, 2)
    g["a_log"] = (dA * A).reshape(2, 32)
    g["d_skip"] = (acc_s[0, 1] + acc_s[1, 1]).reshape(32, HEAD).sum(axis=1).reshape(1, 32)
    g["ssd_norm"] = acc_s[0, 0:1] + acc_s[1, 0:1]
    loss_lanes = acc_f[:, 2, :]
    return loss_lanes, grad_x, g, recv_early


MESH = pl.DeviceIdType.MESH
ANY = pl.BlockSpec(memory_space=pl.ANY)


def _all_gather(shard):
    m_per, n = shard.shape

    def body(x_ref, out_ref, send_sems, recv_sems, local_sem):
        x, y, c = lax.axis_index("x"), lax.axis_index("y"), lax.axis_index("c")
        me, sibling = (x, y, c), (x, y, 1 - c)
        chips = [(1 - x, y), (x, 1 - y), (1 - x, 1 - y)]

        def rows(px, py, pc):
            return out_ref.at[pl.ds((4 * px + 2 * py + pc) * m_per, m_per), :]

        def copy(k, block, to, src=None):
            return pltpu.make_async_remote_copy(
                src_ref=rows(*block) if src is None else src, dst_ref=rows(*block),
                send_sem=send_sems.at[k], recv_sem=recv_sems.at[k], device_id=to, device_id_type=MESH)

        mine = pltpu.make_async_copy(x_ref, rows(*me), local_sem)
        mine.start()
        first = [copy(0, me, sibling, src=x_ref)]
        first += [copy(1 + j, me, (*chip, c), src=x_ref) for j, chip in enumerate(chips)]
        for cp in first:
            cp.start()
        passed = [copy(4 + j, (*chip, c), sibling) for j, chip in enumerate(chips)]
        for j, chip in enumerate(chips):
            copy(1 + j, (*chip, c), me).wait_recv()
            passed[j].start()
        copy(0, sibling, me).wait_recv()
        for j, chip in enumerate(chips):
            copy(4 + j, (*chip, 1 - c), me).wait_recv()
        for cp in first + passed:
            cp.wait_send()
        mine.wait()

    return pl.pallas_call(
        body, name="all_gather_weights",
        out_shape=jax.ShapeDtypeStruct((NDEV * m_per, n), shard.dtype),
        in_specs=[ANY], out_specs=ANY,
        scratch_shapes=[pltpu.SemaphoreType.DMA((7,)), pltpu.SemaphoreType.DMA((7,)), pltpu.SemaphoreType.DMA],
    )(shard)


PAIR_PIECES = 4


def _xor_peer(k, x, y, c):
    return (1 - x if k & 4 else x, 1 - y if k & 2 else y, 1 - c if k & 1 else c)


def _pair_exchange(big, small, name):
    _, nq, rows, n = big.shape
    piece = rows // PAIR_PIECES
    assert piece * PAIR_PIECES == rows and piece % 16 == 0
    with_small = small is not None

    def body(*refs):
        if with_small:
            big_ref, small_ref, got_ref, osmall_ref, send_sems, recv_sems, local_sem = refs
        else:
            big_ref, got_ref, send_sems, recv_sems, local_sem = refs
        x, y, c = lax.axis_index("x"), lax.axis_index("y"), lax.axis_index("c")
        me = 4 * x + 2 * y + c

        def rc(src, dst, sem, peer):
            return pltpu.make_async_remote_copy(src_ref=src, dst_ref=dst, send_sem=send_sems.at[sem],
                                                recv_sem=recv_sems.at[sem], device_id=peer, device_id_type=MESH)

        sib = _xor_peer(1, x, y, c)
        local, sends, recvs = [], [], []
        for q in range(nq):
            for h in range(PAIR_PIECES):
                rws = pl.ds(h * piece, piece)
                cp = rc(big_ref.at[1 - c, q, rws], got_ref.at[q, rws], 8 + q * PAIR_PIECES + h, sib)
                sends.append(cp)
                recvs.append(cp)
        if with_small:
            local.append(pltpu.make_async_copy(small_ref, osmall_ref.at[me], local_sem))
            for k in range(1, NDEV):
                px, py, pc = _xor_peer(k, x, y, c)
                sends.append(rc(small_ref, osmall_ref.at[me], k, (px, py, pc)))
                recvs.append(rc(small_ref, osmall_ref.at[4 * px + 2 * py + pc], k, (px, py, pc)))
        for cp in local + sends:
            cp.start()
        for cp in sends:
            cp.wait_send()
        for cp in recvs:
            cp.wait_recv()
        for cp in local:
            cp.wait()

    nsem = 8 + nq * PAIR_PIECES
    out_shape = [jax.ShapeDtypeStruct(big.shape[1:], big.dtype)]
    if with_small:
        out_shape.append(jax.ShapeDtypeStruct((NDEV,) + small.shape, small.dtype))
    out = pl.pallas_call(
        body, name=name, out_shape=tuple(out_shape),
        in_specs=[ANY] * (1 + with_small), out_specs=(ANY,) * (1 + with_small),
        scratch_shapes=[pltpu.SemaphoreType.DMA((nsem,)), pltpu.SemaphoreType.DMA((nsem,)), pltpu.SemaphoreType.DMA],
    )(*((big, small) if with_small else (big,)))
    return out if with_small else out[0]


def _pair_add(big, got, name):
    _, nq, rows, n = big.shape
    tile = rows // 4
    assert rows % 64 == 0

    def body(c_ref, a_ref, b_ref, o_ref):
        o_ref[0] = (a_ref[0, 0].astype(F32) + b_ref[0].astype(F32)).astype(BF)

    blk = pl.BlockSpec((1, tile, n), lambda q, i, c_ref: (q, i, 0))
    return pl.pallas_call(
        body, name=name,
        grid_spec=pltpu.PrefetchScalarGridSpec(
            num_scalar_prefetch=1, grid=(nq, rows // tile),
            in_specs=[pl.BlockSpec((1, 1, tile, n), lambda q, i, c_ref: (c_ref[0], q, i, 0)), blk], out_specs=blk),
        out_shape=jax.ShapeDtypeStruct(got.shape, BF), compiler_params=_params(("parallel", "parallel")),
    )(lax.axis_index("c").astype(jnp.int32).reshape(1), big, got)


def _chip_exchange_side(pair):
    def make(in_refs, out_refs, send_sems, recv_sems, local_sem, arrivals=True):
        (in_ref,), (out_ref,) = in_refs, out_refs
        x, y, c = lax.axis_index("x"), lax.axis_index("y"), lax.axis_index("c")
        q = 2 * x + y
        local = [pltpu.make_async_copy(in_ref.at[q], out_ref.at[q], local_sem)]
        sends, recvs = [], []
        for j in range(1, 4):
            px, py, pc = _xor_peer(2 * j, x, y, c)
            pq = 2 * px + py
            for lst, dst in ((sends, out_ref.at[q]), (recvs, out_ref.at[pq]))[:1 + arrivals]:
                lst.append(pltpu.make_async_remote_copy(
                    src_ref=in_ref.at[pq], dst_ref=dst, send_sem=send_sems.at[j - 1], recv_sem=recv_sems.at[j - 1],
                    device_id=(px, py, pc), device_id_type=MESH))
        return local, sends, recvs

    return _SideCopies([pair], [jax.ShapeDtypeStruct(pair.shape, pair.dtype)], make)


def _gather_side(shard):
    def make(in_refs, out_refs, send_sems, recv_sems, local_sem, arrivals=True):
        (src,), (dst,) = in_refs, out_refs
        x, y, c = lax.axis_index("x"), lax.axis_index("y"), lax.axis_index("c")
        me = 4 * x + 2 * y + c
        local = [pltpu.make_async_copy(src, dst.at[me], local_sem)]
        sends, recvs = [], []
        for k in range(1, NDEV):
            px, py, pc = _xor_peer(k, x, y, c)
            for lst, slot in ((sends, me), (recvs, 4 * px + 2 * py + pc))[:1 + arrivals]:
                lst.append(pltpu.make_async_remote_copy(
                    src_ref=src, dst_ref=dst.at[slot], send_sem=send_sems.at[k - 1], recv_sem=recv_sems.at[k - 1],
                    device_id=(px, py, pc), device_id_type=MESH))
        return local, sends, recvs

    return _SideCopies([shard], [jax.ShapeDtypeStruct((NDEV,) + shard.shape, shard.dtype)], make)


ADAM_TILE = 64
PACK_W = 1024


def _adamw(recv, w, m, v, name, side=None):
    rp = w.shape[0]
    tile = min(ADAM_TILE, rp)
    nsrc = recv.shape[0]
    grid = (rp // tile,)
    n_si, n_so = (len(side.inputs), len(side.out_shapes)) if side else (0, 0)

    def body(*refs):
        r_ref, w_ref, m_ref, v_ref = refs[:4]
        g_ref, d_ref, nm_ref, nv_ref = refs[4 + n_si:8 + n_si]
        side_refs = (refs[4:4 + n_si], refs[8 + n_si:8 + n_si + n_so], refs[8 + n_si + n_so:])
        if side:
            side.start(grid, *side_refs)
        g = r_ref[0].astype(F32)
        for i in range(1, nsrc):
            g = g + r_ref[i].astype(F32)
        m1 = ADAM_B1 * m_ref[...] + (1.0 - ADAM_B1) * g
        v1 = ADAM_B2 * v_ref[...] + (1.0 - ADAM_B2) * (g * g)
        m_hat = m1 / (1.0 - ADAM_B1 ** ADAM_STEP)
        v_hat = v1 / (1.0 - ADAM_B2 ** ADAM_STEP)
        g_ref[...] = g
        d_ref[...] = -ADAM_LR * (m_hat / (jnp.sqrt(v_hat) + ADAM_EPS) + ADAM_WD * w_ref[...])
        nm_ref[...] = m1
        nv_ref[...] = v1
        if side:
            side.wait(grid, *side_refs)

    blk = pl.BlockSpec((tile, PACK_W), lambda i: (i, 0))
    shp = jax.ShapeDtypeStruct((rp, PACK_W), F32)
    return pl.pallas_call(
        body, name=name, grid=grid,
        in_specs=[pl.BlockSpec((nsrc, tile, PACK_W), lambda i: (0, i, 0)), blk, blk, blk] + [ANY] * n_si,
        out_specs=(blk, blk, blk, blk) + (ANY,) * n_so,
        out_shape=(shp, shp, shp, shp) + tuple(side.out_shapes if side else ()),
        scratch_shapes=side.scratch() if side else [],
        compiler_params=_params(("arbitrary",) if side else ("parallel",)),
    )(recv, w, m, v, *(side.inputs if side else ()))


BIG = {"w_ada": ((3 * D, D), 0), "pool_w": ((4, PGW, PGW), 1), "w_proj_pool": ((D, D), 0), "w_proj_ssd": ((DIN, D), 0),
       "w_out": ((D, D), 0), "w_in": ((IN_COLS, D), 0), "conv_w": ((4, CONV_DIM), 1)}
TRANSPOSED = ("w_ada", "w_in")
PACK_ROWS = {"w_ada": 384, "w_in": 1168, "conv_w": 16, "pool_w": 32, "w_proj_pool": 128, "w_proj_ssd": 256, "w_out": 128}
GATHER_EARLY = ("w_ada", "w_in", "conv_w")
GATHER_LATE = ("pool_w", "w_proj_pool", "w_proj_ssd", "w_out")
GRADS_LATE = ("w_ada",)
GRADS_EARLY = tuple(n for n in PACK_ROWS if n not in GRADS_LATE)
SMALL = {"c_ctx": (D,), "b_ada": (1, 3 * D), "norm_pre": (1, D), "norm_post": (1, D), "b_merge": (1, 2 * D),
         "pool_scale": (1, D), "conv_b": (1, CONV_DIM), "dt_bias": (2, 32), "a_log": (2, 32), "d_skip": (1, 32),
         "ssd_norm": (1, DIN)}
LOSS_SLOT = 128
assert all(_r % 16 == 0 for _r in PACK_ROWS.values())
SMALL_ROWS = 16


def _shard_shape(name):
    shape, ax = BIG[name]
    return tuple(s // NDEV if i == ax else s for i, s in enumerate(shape))


def _as_rows(t, rows):
    pad = [(0, 0)] * (t.ndim - 1) + [(0, rows * PACK_W - t.shape[-1])]
    return jnp.pad(t, pad).reshape(t.shape[:-1] + (rows, PACK_W))


def _shard_rows(t, name):
    sh, r = _shard_shape(name), PACK_ROWS[name]
    lead = t.shape[:t.ndim - len(sh)]
    if len(sh) == 2 and sh[1] == PACK_W:
        return jnp.pad(t, [(0, 0)] * len(lead) + [(0, r - sh[0]), (0, 0)])
    if int(np.prod(sh)) == r * PACK_W:
        return t.reshape(lead + (r, PACK_W))
    return _as_rows(t.reshape(lead + (-1,)), r)


def _to_chunks(full, name):
    shape, ax = BIG[name]
    split = shape[:ax] + (NDEV, shape[ax] // NDEV) + shape[ax + 1:]
    return _shard_rows(jnp.moveaxis(full.reshape(split), ax, 0), name)


def _from_chunks(chunks, name):
    shape, ax = BIG[name]
    return jnp.moveaxis(chunks.reshape((NDEV,) + _shard_shape(name)), 0, ax).reshape(shape)


def _rows_of(names):
    return sum(PACK_ROWS[n] for n in names)


def _pack_state(t, names):
    return jnp.concatenate([_shard_rows(t[n], n) for n in names], axis=0)


def _pack_small(t, loss_part=None):
    slot = jnp.zeros((LOSS_SLOT,), F32)
    if loss_part is not None:
        slot = slot.at[0].set(loss_part)
    return _as_rows(jnp.concatenate([t[n].reshape(-1) for n in SMALL] + [slot]), SMALL_ROWS)


def _pack_grads(g, names):
    big = jnp.concatenate([_to_chunks(g[n], n).astype(BF) for n in names], axis=1)
    return jnp.swapaxes(big.reshape(4, 2, _rows_of(names), PACK_W), 0, 1)


def _unpack_state(big, names):
    out, off = {}, 0
    for n in names:
        sh, r = _shard_shape(n), PACK_ROWS[n]
        k = int(np.prod(sh))
        if len(sh) == 2 and sh[1] == PACK_W:
            out[n] = big[off:off + sh[0]]
        else:
            out[n] = big[off:off + r].reshape(-1)[:k].reshape(sh)
        off += r
    return out


def _unpack_small(small):
    out, flat, off = {}, small.reshape(-1), 0
    for n, sh in SMALL.items():
        k = int(np.prod(sh))
        out[n] = flat[off:off + k].reshape(sh)
        off += k
    out["loss"] = flat[off]
    return out


def _pack_gather(w, names):
    pieces = []
    for n in names:
        if n == "conv_w":
            pieces.append(_as_rows(jnp.concatenate([p.reshape(-1) for p in _split(w[n], 3)]), PACK_ROWS[n]))
        else:
            pieces.append(_shard_rows(w[n], n).astype(BF))
    return jnp.concatenate(pieces, axis=0)


def _unpack_gather(gathered, names):
    g = gathered.reshape(NDEV, _rows_of(names), PACK_W)
    out, off = {}, 0
    for n in names:
        r = PACK_ROWS[n]
        sh = _shard_shape(n)
        if n == "conv_w":
            k = int(np.prod(sh))
            terms = g[:, off:off + r].reshape(NDEV, -1)[:, :3 * k].astype(F32).reshape(NDEV, 3, k)
            out[n] = _from_chunks(terms[:, 0] + terms[:, 1] + terms[:, 2], n)
        elif len(sh) == 2 and sh[1] == PACK_W:
            out[n] = _from_chunks(g[:, off:off + sh[0]], n)
        else:
            out[n] = _from_chunks(g[:, off:off + r], n)
        off += r
    return out


PARAMS = ["c_ctx", "w_ada", "b_ada", "norm_pre", "norm_post", "w_in", "b_merge", "pool_w", "pool_scale", "conv_w", "conv_b",
          "dt_bias", "a_log", "d_skip", "ssd_norm", "w_proj_pool", "w_proj_ssd", "w_out"]


def kernel(x, c, ctx, c_ctx, w_ada, b_ada, norm_pre, norm_post, w_in, b_merge, pool_w, pool_scale, conv_w, conv_b, dt_bias, a_log, d_skip, ssd_norm, w_proj_pool, w_proj_ssd, w_out, loss_target, m_c_ctx, m_w_ada, m_b_ada, m_norm_pre, m_norm_post, m_w_in, m_b_merge, m_pool_w, m_pool_scale, m_conv_w, m_conv_b, m_dt_bias, m_a_log, m_d_skip, m_ssd_norm, m_w_proj_pool, m_w_proj_ssd, m_w_out, v_c_ctx, v_w_ada, v_b_ada, v_norm_pre, v_norm_post, v_w_in, v_b_merge, v_pool_w, v_pool_scale, v_conv_w, v_conv_b, v_dt_bias, v_a_log, v_d_skip, v_ssd_norm, v_w_proj_pool, v_w_proj_ssd, v_w_out):
    given = dict(locals())
    shapes = {n: given[n].shape for n in PARAMS}

    def local(prefix):
        t = {n: (given[prefix + n] if n == "c_ctx" else given[prefix + n][0]) for n in PARAMS}
        for n in TRANSPOSED:
            t[n] = t[n].T
        return {n: t[n].reshape(_shard_shape(n) if n in BIG else SMALL[n]) for n in PARAMS}

    w, m, v = local(""), local("m_"), local("v_")

    W = _unpack_gather(_all_gather(_pack_gather(w, GATHER_EARLY)), GATHER_EARLY)
    for n in SMALL:
        W[n] = w[n]
    lanes, grad_x, g, recv_early = _local_step(x, c, ctx, loss_target, W, late_shard=_pack_gather(w, GATHER_LATE),
                                               exchange=True)
    gb = _pack_grads(g, GRADS_LATE)
    got, recv_small = _pair_exchange(gb, _pack_small(g, (0.5 / D) * jnp.sum(lanes)), "grads_pair_exchange_late")
    late = _chip_exchange_side(_pair_add(gb, got, "grads_pair_add_late"))
    res = [{} for _ in range(4)]
    *early, recv_late = _adamw(recv_early, *[_pack_state(s, GRADS_EARLY) for s in (w, m, v)], "adamw_early", side=late)
    for r, t in zip(res, early):
        r.update(_unpack_state(t, GRADS_EARLY))
    for r, t in zip(res, _adamw(recv_late, *[_pack_state(s, GRADS_LATE) for s in (w, m, v)], "adamw_late")):
        r.update(_unpack_state(t, GRADS_LATE))
    for r, t in zip(res, _adamw(recv_small, *[_pack_small(s) for s in (w, m, v)], "adamw_small")):
        r.update(_unpack_small(t))
    outs = [res[0]["loss"], grad_x]
    for r in res:
        for n in TRANSPOSED:
            r[n] = r[n].T
        outs += [r[n].reshape(shapes[n]) for n in PARAMS]
    return tuple(outs)
```

```python
import functools

import numpy as np
import jax
import jax.numpy as jnp
from jax import lax
from jax.experimental import pallas as pl
from jax.experimental.pallas import tpu as pltpu

F32, BF = jnp.float32, jnp.bfloat16

D = 1024
GRID_W = 64
EPS = 1e-6
POOL_WINDOWS = (2, 4, 8, 16)
PGW = 256
DIN = 2048
HEAD = 64
NST = 128
NG = 4
HPG = 8
GWID = HPG * HEAD
Q = 128
CONV_DIM = 3072
OFF_GATE, OFF_XBC, OFF_DT, IN_COLS = 4096, 6144, 9216, 9280
NDEV = 8
ADAM_LR, ADAM_B1, ADAM_B2, ADAM_EPS, ADAM_WD, ADAM_STEP = 0.001, 0.9, 0.999, 1e-08, 0.01, 10

V7X_VMEM_LIMIT = 56 * 2 ** 20
ROW_TILE = 256


def _params(sem=None):
    return pltpu.CompilerParams(dimension_semantics=sem, vmem_limit_bytes=V7X_VMEM_LIMIT)


def _dot(a, b):
    return jnp.dot(a.astype(BF), b.astype(BF), preferred_element_type=F32)


def _dot_nt(a, b):
    return lax.dot_general(a.astype(BF), b.astype(BF), (((1,), (1,)), ((), ())), preferred_element_type=F32)


def _dot_tn(a, b):
    return lax.dot_general(a.astype(BF), b.astype(BF), (((0,), (0,)), ((), ())), preferred_element_type=F32)


def _split(a, n):
    parts = []
    for _ in range(n):
        p = a.astype(BF)
        parts.append(p)
        a = a - p.astype(F32)
    return parts


def _dot_sl(a, b01, n=3):
    parts = _split(a, n)
    m = a.shape[0]
    if n == 1 or m % 16:
        return sum(jnp.dot(p, b01, preferred_element_type=F32) for p in parts)
    r = jnp.dot(jnp.concatenate(parts, axis=0), b01, preferred_element_type=F32)
    return sum(r[i * m:(i + 1) * m] for i in range(n))


def _dot_sr(a01, b, n=3):
    parts = _split(b, n)
    k = b.shape[1]
    if n == 1 or k % 128:
        return sum(jnp.dot(a01, p, preferred_element_type=F32) for p in parts)
    r = jnp.dot(a01, jnp.concatenate(parts, axis=1), preferred_element_type=F32)
    return sum(r[:, i * k:(i + 1) * k] for i in range(n))


def _sigmoid(x):
    return 1.0 / (1.0 + jnp.exp(-x))


class _SideCopies:
    NSEM = 8

    def __init__(self, inputs, out_shapes, make):
        self.inputs, self.out_shapes, self.make = list(inputs), list(out_shapes), make

    def scratch(self):
        return [pltpu.SemaphoreType.DMA((self.NSEM,)), pltpu.SemaphoreType.DMA((self.NSEM,)), pltpu.SemaphoreType.DMA]

    def start(self, grid, in_refs, out_refs, sems):
        @pl.when(functools.reduce(lambda p, q: p & q, [pl.program_id(i) == 0 for i in range(len(grid))]))
        def _():
            local, sends, _ = self.make(in_refs, out_refs, *sems, arrivals=False)
            for cp in local + sends:
                cp.start()

    def wait(self, grid, in_refs, out_refs, sems):
        @pl.when(functools.reduce(lambda p, q: p & q, [pl.program_id(i) == n - 1 for i, n in enumerate(grid)]))
        def _():
            local, sends, recvs = self.make(in_refs, out_refs, *sems)
            for cp in sends:
                cp.wait_send()
            for cp in recvs:
                cp.wait_recv()
            for cp in local:
                cp.wait()


def _matmul(a, b, out_dtype, name, tm=512, tn=512, tk=1024, bt=False, n=None, side=None):
    M, K = a.shape
    N = n if n is not None else (b.shape[0] if bt else b.shape[1])
    tm, tn, tk = min(tm, M), min(tn, N), min(tk, K)
    assert M % tm == 0 and N % tn == 0 and K % tk == 0, (a.shape, b.shape)
    nk = K // tk
    grid = (M // tm, N // tn, nk)
    n_si, n_so = (len(side.inputs), len(side.out_shapes)) if side else (0, 0)

    def body(*refs):
        a_ref, b_ref, o_ref = refs[0], refs[1], refs[2 + n_si]
        acc = refs[3 + n_si + n_so]
        side_refs = (refs[2:2 + n_si], refs[3 + n_si:3 + n_si + n_so], refs[4 + n_si + n_so:])
        if side:
            side.start(grid, *side_refs)
        k = pl.program_id(2)
        p = _dot_nt(a_ref[...], b_ref[...]) if bt else _dot(a_ref[...], b_ref[...])

        @pl.when(k == 0)
        def _():
            acc[...] = p

        @pl.when(k > 0)
        def _():
            acc[...] += p

        @pl.when(k == nk - 1)
        def _():
            o_ref[...] = acc[...].astype(o_ref.dtype)

        if side:
            side.wait(grid, *side_refs)

    out = pl.pallas_call(
        body, name=name, grid=grid,
        in_specs=[pl.BlockSpec((tm, tk), lambda i, j, k: (i, k)),
                  pl.BlockSpec((tn, tk), lambda i, j, k: (j, k)) if bt else pl.BlockSpec((tk, tn), lambda i, j, k: (k, j))]
        + [ANY] * n_si,
        out_specs=(pl.BlockSpec((tm, tn), lambda i, j, k: (i, j)),) + (ANY,) * n_so,
        out_shape=(jax.ShapeDtypeStruct((M, N), out_dtype),) + tuple(side.out_shapes if side else ()),
        scratch_shapes=[pltpu.VMEM((tm, tn), F32)] + (side.scratch() if side else []),
        compiler_params=_params(("arbitrary",) * 3 if side else ("parallel", "parallel", "arbitrary")),
    )(a, b, *(side.inputs if side else ()))
    return out if side else out[0]


def _matmul_tn(a, g, name, ta=512, tn=512, tr=512):
    M, Ka = a.shape
    N = g.shape[1]
    ta, tn, tr = min(ta, Ka), min(tn, N), min(tr, M)
    assert M % tr == 0 and N % tn == 0 and Ka % ta == 0, (a.shape, g.shape)
    nr = M // tr

    def body(a_ref, g_ref, o_ref):
        k = pl.program_id(2)
        p = _dot_tn(a_ref[...], g_ref[...])

        @pl.when(k == 0)
        def _():
            o_ref[...] = p

        @pl.when(k > 0)
        def _():
            o_ref[...] += p

    return pl.pallas_call(
        body, name=name, grid=(Ka // ta, N // tn, nr),
        in_specs=[pl.BlockSpec((tr, ta), lambda i, j, k: (k, i)), pl.BlockSpec((tr, tn), lambda i, j, k: (k, j))],
        out_specs=pl.BlockSpec((ta, tn), lambda i, j, k: (i, j)),
        out_shape=jax.ShapeDtypeStruct((Ka, N), F32),
        compiler_params=_params(("parallel", "parallel", "arbitrary")),
    )(a, g)


def _dhx(pieces, ddt, w_inT, w_dtT, side=None):
    _, R, _ = pieces[0].shape
    tm = R // 4
    kb = 1024
    starts, nblk = [], []
    for p in pieces:
        starts.append(sum(nblk))
        nblk.append(p.shape[2] // kb)
    nk = sum(nblk)
    assert nk * kb == OFF_DT and R % 128 == 0
    npc = len(pieces)
    grid = (2, R // tm, nk)
    n_si, n_so = (len(side.inputs), len(side.out_shapes)) if side else (0, 0)

    def body(*refs):
        a_refs, dt_ref, w_ref, wdt_ref = refs[:npc], refs[npc], refs[npc + 1], refs[npc + 2]
        o_ref, acc = refs[npc + 3 + n_si], refs[npc + 4 + n_si + n_so]
        side_refs = (refs[npc + 3:npc + 3 + n_si], refs[npc + 4 + n_si:npc + 4 + n_si + n_so], refs[npc + 5 + n_si + n_so:])
        if side:
            side.start(grid, *side_refs)
        k = pl.program_id(2)

        @pl.when(k == 0)
        def _():
            acc[...] = _dot(dt_ref[0], wdt_ref[...])

        for p in range(npc):
            @pl.when((k >= starts[p]) & (k < starts[p] + nblk[p]))
            def _(p=p):
                acc[...] += _dot(a_refs[p][0], w_ref[...])

        @pl.when(k == nk - 1)
        def _():
            o_ref[0] = acc[...].astype(BF)

        if side:
            side.wait(grid, *side_refs)

    in_specs = [pl.BlockSpec((1, tm, kb), functools.partial(
        lambda e, t, k, s, nb: (e, t, jnp.clip(k - s, 0, nb - 1)), s=starts[p], nb=nblk[p])) for p in range(npc)]
    in_specs += [pl.BlockSpec((1, tm, 128), lambda e, t, k: (e, t, 0)),
                 pl.BlockSpec((kb, D), lambda e, t, k: (k, 0)),
                 pl.BlockSpec((128, D), lambda e, t, k: (0, 0))]
    out = pl.pallas_call(
        body, name="d_hx", grid=grid, in_specs=in_specs + [ANY] * n_si,
        out_specs=(pl.BlockSpec((1, tm, D), lambda e, t, k: (e, t, 0)),) + (ANY,) * n_so,
        out_shape=(jax.ShapeDtypeStruct((2, R, D), BF),) + tuple(side.out_shapes if side else ()),
        scratch_shapes=[pltpu.VMEM((tm, D), F32)] + (side.scratch() if side else []),
        compiler_params=_params(("arbitrary",) * 3 if side else ("parallel", "parallel", "arbitrary")),
    )(*pieces, ddt, w_inT, w_dtT, *(side.inputs if side else ()))
    return out if side else out[0]


def _adaln_fwd(c16, w_adaT_bf, b_ada):
    def body(c_ref, w_ref, b_ref, o_ref):
        cc = c_ref[...]
        o_ref[...] = _dot_nt(cc * _sigmoid(cc), w_ref[...]) + b_ref[...]

    return pl.pallas_call(body, name="adaln_fwd", out_shape=jax.ShapeDtypeStruct((16, 3 * D), F32),
                          compiler_params=_params())(c16, w_adaT_bf, b_ada)


def _adaln_bwd(acc_n, acc_f, mod16, c16, norm_pre, w_adaT_bf):
    def body(an_ref, af_ref, mod_ref, c_ref, np_ref, wt_ref, dw_ref, db_ref, sm_ref, dmod):
        npre = np_ref[...]
        dmod[...] = jnp.zeros_like(dmod)
        dnp = jnp.zeros((1, D), F32)
        dshift_c = jnp.zeros((1, D), F32)
        dgpre_c = jnp.zeros((1, D), F32)
        scale_c = mod_ref[2:3, D:2 * D]
        for e in range(2):
            dg_x, ds_x = an_ref[e, 0, 0:1, :], an_ref[e, 0, 1:2, :]
            dg_c, ds_c = an_ref[e, 1, 0:1, :], an_ref[e, 1, 1:2, :]
            dmod[e:e + 1, 0:D] = ds_x
            dmod[e:e + 1, D:2 * D] = dg_x * npre
            dmod[e:e + 1, 2 * D:3 * D] = af_ref[e, 0:1, :]
            dnp = dnp + dg_x * (1.0 + mod_ref[e:e + 1, D:2 * D]) + dg_c * (1.0 + scale_c)
            dshift_c = dshift_c + ds_c
            dgpre_c = dgpre_c + dg_c
        dmod[2:3, 0:D] = dshift_c
        dmod[2:3, D:2 * D] = dgpre_c * npre
        dm = dmod[...]
        cc = c_ref[...]
        sg = _sigmoid(cc)
        dw_ref[...] = _dot_tn(dm, cc * sg)
        db_ref[...] = jnp.zeros_like(db_ref)
        db_ref[0:1, :] = jnp.sum(dm, axis=0, keepdims=True)
        dsilu = sg * (1.0 + cc * (1.0 - sg))
        dcs = _dot(dm, wt_ref[...]) * dsilu
        sm_ref[...] = jnp.zeros_like(sm_ref)
        sm_ref[0:1, :] = dnp
        sm_ref[1:2, :] = dcs[2:3, :]

    return pl.pallas_call(
        body, name="adaln_bwd",
        out_shape=(jax.ShapeDtypeStruct((3 * D, D), F32), jax.ShapeDtypeStruct((16, 3 * D), F32),
                   jax.ShapeDtypeStruct((8, D), F32)),
        scratch_shapes=[pltpu.VMEM((16, 3 * D), F32)],
        compiler_params=_params())(acc_n, acc_f, mod16, c16, norm_pre, w_adaT_bf)


def _row_specs(L):
    nx = L // ROW_TILE
    return (pl.BlockSpec((1, ROW_TILE, D), lambda e, t: (e, jnp.minimum(t, nx - 1), 0)),
            pl.BlockSpec((1, ROW_TILE, D), lambda e, t: (e, jnp.maximum(t - nx, 0), 0)))


def _norm_mod_fwd(x, ctx, tab):
    L = x.shape[1]
    R = L + ctx.shape[1]
    nx = L // ROW_TILE

    def body(x_ref, c_ref, t_ref, o_ref):
        t = t_ref[0, 0]
        gain, shift = t[0:1], t[1:2]

        def run(src):
            for r0 in range(0, ROW_TILE, 32):
                x = src[0, pl.ds(r0, 32), :]
                r = lax.rsqrt(jnp.mean(x * x, axis=-1, keepdims=True) + EPS)
                o_ref[0, pl.ds(r0, 32), :] = (x * r * gain + shift).astype(BF)

        @pl.when(pl.program_id(1) < nx)
        def _():
            run(x_ref)

        @pl.when(pl.program_id(1) >= nx)
        def _():
            run(c_ref)

    return pl.pallas_call(
        body, name="norm_mod_fwd", grid=(2, R // ROW_TILE),
        in_specs=[*_row_specs(L), pl.BlockSpec((1, 1, 8, D), lambda e, t: (e, t // nx, 0, 0))],
        out_specs=pl.BlockSpec((1, ROW_TILE, D), lambda e, t: (e, t, 0)),
        out_shape=jax.ShapeDtypeStruct((2, R, D), BF),
        compiler_params=_params(("parallel", "parallel")),
    )(x, ctx, tab)


def _norm_mod_bwd(dh, x, ctx, tab, dxo):
    L = x.shape[1]
    R = L + ctx.shape[1]
    nx = L // ROW_TILE

    def body(dh_ref, x_ref, c_ref, t_ref, dxo_ref, gx_ref, acc_ref):
        t = pl.program_id(1)
        x = jnp.where(t < nx, x_ref[0], c_ref[0])
        r = lax.rsqrt(jnp.mean(x * x, axis=-1, keepdims=True) + EPS)
        xn = x * r
        dh = dh_ref[0].astype(F32)

        @pl.when((t == 0) | (t == nx))
        def _():
            acc_ref[...] = jnp.zeros_like(acc_ref)

        acc_ref[0, 0, 0:1, :] += jnp.sum(dh * xn, axis=0, keepdims=True)
        acc_ref[0, 0, 1:2, :] += jnp.sum(dh, axis=0, keepdims=True)

        @pl.when(t < nx)
        def _():
            dxn = dh * t_ref[0, 0][0:1]
            dx = r * (dxn - xn * jnp.mean(dxn * xn, axis=-1, keepdims=True))
            gx_ref[0] = dxo_ref[0].astype(F32) + dx

    xspec, cspec = _row_specs(L)
    return pl.pallas_call(
        body, name="norm_mod_bwd", grid=(2, R // ROW_TILE),
        in_specs=[pl.BlockSpec((1, ROW_TILE, D), lambda e, t: (e, t, 0)), xspec, cspec,
                  pl.BlockSpec((1, 1, 8, D), lambda e, t: (e, t // nx, 0, 0)), xspec],
        out_specs=(xspec, pl.BlockSpec((1, 1, 8, D), lambda e, t: (e, t // nx, 0, 0))),
        out_shape=(jax.ShapeDtypeStruct((2, L, D), F32), jax.ShapeDtypeStruct((2, 2, 8, D), F32)),
        compiler_params=_params(("parallel", "arbitrary")),
    )(dh, x, ctx, tab, dxo)


POOL_TILE = 256


def _pool_tables(L):
    rows = L // GRID_W
    mats = np.zeros((4, POOL_TILE, POOL_TILE), np.float32)
    inv = np.zeros((4, L, 1), np.float32)
    for gi, k in enumerate(POOL_WINDOWS):
        lo, hi = k // 2, k - 1 - k // 2
        m = np.zeros((GRID_W, GRID_W), np.float32)
        for t in range(GRID_W):
            m[t, max(t - lo, 0):min(t + hi, GRID_W - 1) + 1] = 1.0
        for b in range(POOL_TILE // GRID_W):
            mats[gi, b * GRID_W:(b + 1) * GRID_W, b * GRID_W:(b + 1) * GRID_W] = m
        cnt_c = m.sum(1)
        cnt_r = np.array([min(r + hi, rows - 1) - max(r - lo, 0) + 1 for r in range(rows)], np.float32)
        inv[gi, :, 0] = (1.0 / (cnt_r[:, None] * cnt_c[None, :])).reshape(-1)
    matsT = np.ascontiguousarray(np.transpose(mats, (0, 2, 1)))
    return (jnp.asarray(mats, BF), jnp.asarray(matsT, BF), jnp.asarray(inv))


def _pool_cols(get_tile, mat, cs_ref, L, n):
    def step(i, carry):
        off = pl.multiple_of(i * POOL_TILE, POOL_TILE)
        t = get_tile(off)
        cs_ref[pl.ds(GRID_W + off, POOL_TILE), :] = (jnp.dot(mat, t.astype(BF), preferred_element_type=F32) if n == 1
                                                     else _dot_sr(mat, t.astype(F32), n))
        return carry

    lax.fori_loop(0, L // POOL_TILE, step, 0)
    cs_ref[pl.ds(0, GRID_W), :] = jnp.zeros((GRID_W, PGW), F32)

    def prefix(r, carry):
        o = pl.multiple_of(r * GRID_W, GRID_W)
        cs_ref[pl.ds(o + GRID_W, GRID_W), :] = cs_ref[pl.ds(o + GRID_W, GRID_W), :] + cs_ref[pl.ds(o, GRID_W), :]
        return carry

    lax.fori_loop(0, L // GRID_W, prefix, 0)


def _pool_rows(cs_ref, off, below, above, L):
    rows = L // GRID_W
    r0 = off // GRID_W
    parts = []
    for i in range(POOL_TILE // GRID_W):
        hi = pl.multiple_of(jnp.minimum(r0 + i + above + 1, rows) * GRID_W, GRID_W)
        lo = pl.multiple_of(jnp.maximum(r0 + i - below, 0) * GRID_W, GRID_W)
        parts.append(cs_ref[pl.ds(hi, GRID_W), :] - cs_ref[pl.ds(lo, GRID_W), :])
    return jnp.concatenate(parts, axis=0)


def _pool_fwd(proj3, pool_w_bf, pool_scale, tables, L):
    mats, _, inv = tables
    nt = L // POOL_TILE

    def body(v_ref, z_ref, pw_ref, ps_ref, m_ref, inv_ref, o_ref, cs_ref):
        _pool_cols(lambda off: v_ref[0, pl.ds(off, POOL_TILE), :], m_ref[0], cs_ref, L, 1)
        half = lax.shift_left(1, pl.program_id(1))

        def step(i, carry):
            off = pl.multiple_of(i * POOL_TILE, POOL_TILE)
            rows = pl.ds(off, POOL_TILE)
            v = v_ref[0, rows, :].astype(F32)
            diff = _pool_rows(cs_ref, off, half, half - 1, L) * inv_ref[0, rows, :] - v
            yp = _dot(diff, pw_ref[0])
            z = z_ref[0, rows, :].astype(F32)
            o_ref[0, rows, :] = (yp * ps_ref[...] * (z * _sigmoid(z))).astype(BF)
            return carry

        lax.fori_loop(0, nt, step, 0)

    return pl.pallas_call(
        body, name="pool_fwd", grid=(2, 4),
        in_specs=[pl.BlockSpec((1, L, PGW), lambda e, g: (e, 0, g)),
                  pl.BlockSpec((1, L, PGW), lambda e, g: (e, 0, 4 + g)),
                  pl.BlockSpec((1, PGW, PGW), lambda e, g: (g, 0, 0)),
                  pl.BlockSpec((1, PGW), lambda e, g: (0, g)),
                  pl.BlockSpec((1, POOL_TILE, POOL_TILE), lambda e, g: (g, 0, 0)),
                  pl.BlockSpec((1, L, 1), lambda e, g: (g, 0, 0))],
        out_specs=pl.BlockSpec((1, L, PGW), lambda e, g: (e, 0, g)),
        out_shape=jax.ShapeDtypeStruct((2, L, D), BF),
        scratch_shapes=[pltpu.VMEM((L + GRID_W, PGW), F32)],
        compiler_params=_params(("parallel", "parallel")),
    )(proj3, proj3, pool_w_bf, pool_scale, mats, inv)


def _pool_bwd(proj3, d_ypool, pool_w_bf, pool_wT_bf, pool_scale, tables, L):
    mats, matsT, inv = tables
    nt = L // POOL_TILE
    R = proj3.shape[1]

    def body(v_ref, z_ref, dy_ref, pw_ref, pwt_ref, ps_ref, m_ref, mt_ref, inv_ref,
             dv_ref, dz_ref, dpw_ref, acc_ref, cs_ref, dd_ref):
        e = pl.program_id(1)

        @pl.when(e == 0)
        def _():
            dpw_ref[...] = jnp.zeros_like(dpw_ref)
            acc_ref[...] = jnp.zeros_like(acc_ref)

        _pool_cols(lambda off: v_ref[0, pl.ds(off, POOL_TILE), :], m_ref[0], cs_ref, L, 1)
        half = lax.shift_left(1, pl.program_id(0))
        ps = ps_ref[...]

        def step(i, carry):
            off = pl.multiple_of(i * POOL_TILE, POOL_TILE)
            rows = pl.ds(off, POOL_TILE)
            v = v_ref[0, rows, :].astype(F32)
            diff = _pool_rows(cs_ref, off, half, half - 1, L) * inv_ref[0, rows, :] - v
            yp = _dot(diff, pw_ref[0])
            z = z_ref[0, rows, :].astype(F32)
            sg = _sigmoid(z)
            sz = z * sg
            dy = dy_ref[0, rows, :].astype(F32)
            dz_ref[0, rows, :] = (dy * yp * ps * (sg * (1.0 + z * (1.0 - sg)))).astype(BF)
            dys = dy * sz
            acc_ref[0, 0:1, :] += jnp.sum(dys * yp, axis=0, keepdims=True)
            dyp = dys * ps
            dpw_ref[0] += _dot_tn(diff, dyp)
            dd_ref[rows, :] = _dot(dyp, pwt_ref[0])
            return carry

        lax.fori_loop(0, nt, step, 0)
        _pool_cols(lambda off: dd_ref[pl.ds(off, POOL_TILE), :] * inv_ref[0, pl.ds(off, POOL_TILE), :],
                   mt_ref[0], cs_ref, L, 1)

        def step2(i, carry):
            off = pl.multiple_of(i * POOL_TILE, POOL_TILE)
            rows = pl.ds(off, POOL_TILE)
            dv_ref[0, rows, :] = (_pool_rows(cs_ref, off, half - 1, half, L) - dd_ref[rows, :]).astype(BF)
            return carry

        lax.fori_loop(0, nt, step2, 0)
        dv_ref[0, pl.ds(L, R - L), :] = jnp.zeros((R - L, PGW), BF)
        dz_ref[0, pl.ds(L, R - L), :] = jnp.zeros((R - L, PGW), BF)

    return pl.pallas_call(
        body, name="pool_bwd", grid=(4, 2),
        in_specs=[pl.BlockSpec((1, L, PGW), lambda g, e: (e, 0, g)),
                  pl.BlockSpec((1, L, PGW), lambda g, e: (e, 0, 4 + g)),
                  pl.BlockSpec((1, L, PGW), lambda g, e: (e, 0, g)),
                  pl.BlockSpec((1, PGW, PGW), lambda g, e: (g, 0, 0)),
                  pl.BlockSpec((1, PGW, PGW), lambda g, e: (g, 0, 0)),
                  pl.BlockSpec((1, PGW), lambda g, e: (0, g)),
                  pl.BlockSpec((1, POOL_TILE, POOL_TILE), lambda g, e: (g, 0, 0)),
                  pl.BlockSpec((1, POOL_TILE, POOL_TILE), lambda g, e: (g, 0, 0)),
                  pl.BlockSpec((1, L, 1), lambda g, e: (g, 0, 0))],
        out_specs=(pl.BlockSpec((1, R, PGW), lambda g, e: (e, 0, g)),
                   pl.BlockSpec((1, R, PGW), lambda g, e: (e, 0, g)),
                   pl.BlockSpec((1, PGW, PGW), lambda g, e: (g, 0, 0)),
                   pl.BlockSpec((1, 8, PGW), lambda g, e: (g, 0, 0))),
        out_shape=(jax.ShapeDtypeStruct((2, R, D), BF), jax.ShapeDtypeStruct((2, R, D), BF),
                   jax.ShapeDtypeStruct((4, PGW, PGW), F32), jax.ShapeDtypeStruct((4, 8, PGW), F32)),
        scratch_shapes=[pltpu.VMEM((L + GRID_W, PGW), F32), pltpu.VMEM((L, PGW), F32)],
        compiler_params=_params(("parallel", "arbitrary")),
    )(proj3, proj3, d_ypool, pool_w_bf, pool_wT_bf, pool_scale, mats, matsT, inv)


CONV_BLOCK = 128


CONV_CHUNK = 64
CONV_HALO = 8


def _halo_buf_init(buf, val, R):
    z = jnp.zeros((CONV_HALO, CONV_BLOCK), F32)
    buf[pl.ds(0, CONV_HALO), :] = z
    buf[pl.ds(CONV_HALO + R, CONV_HALO), :] = z
    if val is not None:
        buf[pl.ds(CONV_HALO, R), :] = val


def _chunk_taps(buf, start, offs, L):
    n = CONV_CHUNK + 2 * CONV_HALO
    ext = buf[pl.ds(start, n), :]
    out = []
    for off in offs:
        if off == 0:
            out.append(ext[CONV_HALO:CONV_HALO + CONV_CHUNK])
            continue
        r = pltpu.roll(ext, (-off) % n, 0)[CONV_HALO:CONV_HALO + CONV_CHUNK]
        lo, hi = (start, start + CONV_CHUNK - 1 + off) if off > 0 else (start + off, start + CONV_CHUNK - 1)
        if lo < L <= hi:
            t = start + lax.broadcasted_iota(jnp.int32, (CONV_CHUNK, 1), 0)
            r = jnp.where((t < L) == (t + off < L), r, 0.0)
        out.append(r)
    return out


def _fold8(x):
    return sum(x[i * 8:(i + 1) * 8] for i in range(CONV_CHUNK // 8))


def _conv_fwd(proj3, conv_w, conv_b, L):
    _, R, _ = proj3.shape
    cb0 = OFF_XBC // CONV_BLOCK

    def body(u_ref, w_ref, b_ref, o_ref, ubuf):
        _halo_buf_init(ubuf, u_ref[0].astype(F32), R)
        w = w_ref[...]
        b = b_ref[...]
        for start in range(0, R, CONV_CHUNK):
            taps = _chunk_taps(ubuf, start, (-2, -1, 0, 1), L)
            pre = b + sum(taps[k] * w[k:k + 1, :] for k in range(4))
            o_ref[0, pl.ds(start, CONV_CHUNK), :] = (pre * _sigmoid(pre)).astype(BF)

    return pl.pallas_call(
        body, name="conv_fwd", grid=(2, CONV_DIM // CONV_BLOCK),
        in_specs=[pl.BlockSpec((1, R, CONV_BLOCK), lambda e, j: (e, 0, cb0 + j)),
                  pl.BlockSpec((4, CONV_BLOCK), lambda e, j: (0, j)),
                  pl.BlockSpec((1, CONV_BLOCK), lambda e, j: (0, j))],
        out_specs=pl.BlockSpec((1, R, CONV_BLOCK), lambda e, j: (e, 0, j)),
        out_shape=jax.ShapeDtypeStruct((2, R, CONV_DIM), BF),
        scratch_shapes=[pltpu.VMEM((R + 2 * CONV_HALO, CONV_BLOCK), F32)],
        compiler_params=_params(("parallel", "parallel")),
    )(proj3, conv_w, conv_b)


def _conv_bwd(proj3, addends, scales, col0, ncols, in_maps, conv_w, conv_b, L, name):
    _, R, _ = proj3.shape
    cb0 = (OFF_XBC + col0) // CONV_BLOCK
    wb0 = col0 // CONV_BLOCK
    na = len(addends)
    scaled = [i for i in range(na) if scales[i] is not None]

    def body(*refs):
        u_ref, w_ref, b_ref = refs[0], refs[1], refs[2]
        a_refs = refs[3:3 + na]
        s_refs = dict(zip(scaled, refs[3 + na:3 + na + len(scaled)]))
        o_ref, acc_ref, ubuf, dbuf = refs[3 + na + len(scaled):]
        _halo_buf_init(ubuf, u_ref[0].astype(F32), R)
        _halo_buf_init(dbuf, None, R)
        w = w_ref[...]
        b = b_ref[...]
        scl = {i: s_refs[i][...] for i in scaled}
        sums = [jnp.zeros((8, CONV_BLOCK), F32) for _ in range(5)]
        for start in range(0, R, CONV_CHUNK):
            rows = pl.ds(start, CONV_CHUNK)
            taps = _chunk_taps(ubuf, start, (-2, -1, 0, 1), L)
            pre = b + sum(taps[k] * w[k:k + 1, :] for k in range(4))
            sg = _sigmoid(pre)
            dxbc = None
            for i, a in enumerate(a_refs):
                t = a[0, rows, :].astype(F32)
                t = t * scl[i] if i in scl else t
                dxbc = t if dxbc is None else dxbc + t
            dpre = dxbc * (sg * (1.0 + pre * (1.0 - sg)))
            dbuf[pl.ds(start + CONV_HALO, CONV_CHUNK), :] = dpre
            for k in range(4):
                sums[k] = sums[k] + _fold8(dpre * taps[k])
            sums[4] = sums[4] + _fold8(dpre)
        acc_ref[...] = jnp.zeros_like(acc_ref)
        for k in range(5):
            acc_ref[0, k:k + 1, :] = jnp.sum(sums[k], axis=0, keepdims=True)
        for start in range(0, R, CONV_CHUNK):
            d = _chunk_taps(dbuf, start, (2, 1, 0, -1), L)
            o_ref[0, pl.ds(start, CONV_CHUNK), :] = sum(d[k] * w[k:k + 1, :] for k in range(4)).astype(BF)

    in_specs = [pl.BlockSpec((1, R, CONV_BLOCK), lambda e, j: (e, 0, cb0 + j)),
                pl.BlockSpec((4, CONV_BLOCK), lambda e, j: (0, wb0 + j)),
                pl.BlockSpec((1, CONV_BLOCK), lambda e, j: (0, wb0 + j))]
    for m in in_maps:
        in_specs.append(pl.BlockSpec((1, R, CONV_BLOCK), functools.partial(lambda e, j, m: (e, 0, m(j)), m=m)))
    for i in scaled:
        in_specs.append(pl.BlockSpec((1, CONV_BLOCK), functools.partial(lambda e, j, m: (0, m(j)), m=in_maps[i])))
    return pl.pallas_call(
        body, name=name, grid=(2, ncols // CONV_BLOCK),
        in_specs=in_specs,
        out_specs=(pl.BlockSpec((1, R, CONV_BLOCK), lambda e, j: (e, 0, j)),
                   pl.BlockSpec((1, 8, CONV_BLOCK), lambda e, j: (e, 0, j))),
        out_shape=(jax.ShapeDtypeStruct((2, R, ncols), BF), jax.ShapeDtypeStruct((2, 8, ncols), F32)),
        scratch_shapes=[pltpu.VMEM((R + 2 * CONV_HALO, CONV_BLOCK), F32)] * 2,
        compiler_params=_params(("parallel", "parallel")),
    )(proj3, conv_w, conv_b, *addends, *[scales[i] for i in scaled])


def _softplus(x):
    e = jnp.exp(-jnp.abs(x))
    u = 1.0 + e
    return jnp.maximum(x, 0.0) + jnp.where(u == 1.0, e, e * jnp.log(u) / (u - 1.0))


def _to_local_mat(g, transpose=False):
    r = lax.broadcasted_iota(jnp.int32, (128, 128), 1 if transpose else 0)
    c = lax.broadcasted_iota(jnp.int32, (128, 128), 0 if transpose else 1)
    return ((c < 2 * HPG) & (r == jnp.right_shift(c, 3) * (NG * HPG) + g * HPG + (c & (HPG - 1)))).astype(BF)


def _dt_fwd(dt_raw, bias128):
    _, R, _ = dt_raw.shape

    def body(x_ref, b_ref, o_ref):
        dt = _softplus(x_ref[0] + b_ref[...])
        for g in range(NG):
            o_ref[0, g] = _dot_sl(dt, _to_local_mat(g))

    tr = R // 4
    return pl.pallas_call(
        body, name="dt_fwd", grid=(2, 4),
        in_specs=[pl.BlockSpec((1, tr, 128), lambda e, t: (e, t, 0)), pl.BlockSpec((1, 128), lambda e, t: (0, 0))],
        out_specs=pl.BlockSpec((1, NG, tr, 128), lambda e, t: (e, 0, t, 0)),
        out_shape=jax.ShapeDtypeStruct((2, NG, R, 128), F32),
        compiler_params=_params(("parallel", "parallel")),
    )(dt_raw, bias128)


def _dt_bwd(dt_raw, bias128, ddt_f, ddt_b):
    _, R, _ = dt_raw.shape

    def body(x_ref, b_ref, f_ref, g_ref, o_ref, acc_ref):
        ddt = sum(_dot_sl(f_ref[0, g] + g_ref[0, g], _to_local_mat(g, transpose=True)) for g in range(NG))
        d = ddt * _sigmoid(x_ref[0] + b_ref[...])
        o_ref[0] = d.astype(BF)

        @pl.when(pl.program_id(1) == 0)
        def _():
            acc_ref[...] = jnp.zeros_like(acc_ref)

        acc_ref[0, 0:1, :] += jnp.sum(d, axis=0, keepdims=True)

    tr = R // 4
    blk = pl.BlockSpec((1, tr, 128), lambda e, t: (e, t, 0))
    loc = pl.BlockSpec((1, NG, tr, 128), lambda e, t: (e, 0, t, 0))
    return pl.pallas_call(
        body, name="dt_bwd", grid=(2, 4),
        in_specs=[blk, pl.BlockSpec((1, 128), lambda e, t: (0, 0)), loc, loc],
        out_specs=(blk, pl.BlockSpec((1, 8, 128), lambda e, t: (e, 0, 0))),
        out_shape=(jax.ShapeDtypeStruct(dt_raw.shape, BF), jax.ShapeDtypeStruct((2, 8, 128), F32)),
        compiler_params=_params(("parallel", "arbitrary")),
    )(dt_raw, bias128, ddt_f, ddt_b)


GPS = 4


def _tri(d):
    i = lax.broadcasted_iota(jnp.int32, (Q, Q), 0)
    j = lax.broadcasted_iota(jnp.int32, (Q, Q), 1)
    return (i >= j) if d == 0 else (i <= j)


def _expand_mat(d):
    r = lax.broadcasted_iota(jnp.int32, (128, GWID), 0)
    c = lax.broadcasted_iota(jnp.int32, (128, GWID), 1)
    return (r == d * HPG + jnp.right_shift(c, 6)).astype(BF)


def _reduce_mat(d):
    r = lax.broadcasted_iota(jnp.int32, (GWID, 128), 0)
    c = lax.broadcasted_iota(jnp.int32, (GWID, 128), 1)
    return (c == d * HPG + jnp.right_shift(r, 6)).astype(BF)


def _ssd_chunk(d, dt, A, xs, B, C):
    mask = _tri(d)
    T = mask.astype(BF)
    Tt = _tri(1 - d).astype(BF)
    a = dt * A
    acs = _dot_sr(T, a)
    E = _expand_mat(d)
    dt_e = _dot_sl(dt, E, 2)
    acs_e = _dot_sl(acs, E, 2)
    alast_e = acs_e[Q - 1:Q, :] if d == 0 else acs_e[0:1, :]
    return dict(mask=mask, T=T, Tt=Tt, acs=acs, acsT=acs.T, dt_e=dt_e, acs_e=acs_e, lam=jnp.exp(acs_e),
                w=jnp.exp(alast_e - acs_e), decay=jnp.exp(alast_e), xt=xs * dt_e, CB=_dot_nt(C, B))


def _head_decay(q, d, hh):
    col = q["acs"][:, d * HPG + hh:d * HPG + hh + 1]
    row = q["acsT"][d * HPG + hh:d * HPG + hh + 1, :]
    return jnp.exp(jnp.where(q["mask"], col - row, -jnp.inf))


def _chunk_maps(NX, NS):
    cf = lambda s: lax.rem(s + NX, NS)
    cb = lambda s: NS - 1 - s
    return cf, cb


def _ssd_fwd(xbc, dt_loc, a_loc, L):
    _, R, _ = xbc.shape
    NX, NS = L // Q, R // Q
    cf, cb = _chunk_maps(NX, NS)

    def body(xs_f, b_f, c_f, dt_f, xs_b, b_b, c_b, dt_b, a_ref, y_f, hs_f, y_b, hs_b, hT):
        @pl.when(pl.program_id(2) == 0)
        def _():
            hT[...] = jnp.zeros_like(hT)

        lane = lax.broadcasted_iota(jnp.int32, (Q, 128), 1)
        for d, (xs_ref, b_ref, c_ref, dt_ref, y_ref, hs_ref) in enumerate(
                ((xs_f, b_f, c_f, dt_f, y_f, hs_f), (xs_b, b_b, c_b, dt_b, y_b, hs_b))):
            for gi in range(GPS):
                cols = slice(gi * GWID, (gi + 1) * GWID)
                xs = xs_ref[0, :, cols].astype(F32)
                B, C = b_ref[0, :, gi * NST:(gi + 1) * NST], c_ref[0, :, gi * NST:(gi + 1) * NST]
                q = _ssd_chunk(d, dt_ref[0, gi], a_ref[gi, 0:1, :], xs, B, C)
                h = hT[d, :, cols]
                hb = h.astype(BF)
                hs_ref[0, 0, :, cols] = hb
                parts = []
                for pr in range(HPG // 2):
                    xp = q["xt"][:, pr * 128:(pr + 1) * 128]
                    xst = jnp.concatenate([jnp.where(lane < HEAD, xp, 0.0), jnp.where(lane < HEAD, 0.0, xp)], axis=0)
                    mst = jnp.concatenate([(q["CB"] * _head_decay(q, d, 2 * pr)).astype(BF),
                                           (q["CB"] * _head_decay(q, d, 2 * pr + 1)).astype(BF)], axis=1)
                    parts.append(_dot(mst, xst))
                y_ref[0, :, cols] = jnp.concatenate(parts, axis=1) + _dot(C, hb) * q["lam"]
                hT[d, :, cols] = q["decay"] * h + _dot_tn(B, q["xt"] * q["w"])

    def spec(shape, imap):
        return pl.BlockSpec(shape, imap)

    bc0 = DIN // (GPS * NST)

    def ins(c):
        return [spec((1, Q, GPS * GWID), lambda e, g, s: (e, c(s), g)),
                spec((1, Q, GPS * NST), lambda e, g, s: (e, c(s), bc0 + g)),
                spec((1, Q, GPS * NST), lambda e, g, s: (e, c(s), bc0 + NG // GPS + g)),
                spec((1, GPS, Q, 128), lambda e, g, s: (e, g, c(s), 0))]

    def outs(c):
        return [spec((1, Q, GPS * GWID), lambda e, g, s: (e, c(s), g)),
                spec((1, 1, NST, GPS * GWID), lambda e, g, s: (e, c(s), 0, g))]

    yshape = jax.ShapeDtypeStruct((2, R, DIN), F32)
    hshape = jax.ShapeDtypeStruct((2, NS, NST, DIN), BF)
    return pl.pallas_call(
        body, name="ssd_fwd", grid=(2, NG // GPS, NS),
        in_specs=ins(cf) + ins(cb) + [spec((GPS, 8, 128), lambda e, g, s: (g, 0, 0))],
        out_specs=tuple(outs(cf) + outs(cb)),
        out_shape=(yshape, hshape, yshape, hshape),
        scratch_shapes=[pltpu.VMEM((2, NST, GPS * GWID), F32)],
        compiler_params=_params(("parallel", "parallel", "arbitrary")),
    )(xbc, xbc, xbc, dt_loc, xbc, xbc, xbc, dt_loc, a_loc)


def _ssd_bwd(xbc, dt_loc, a_loc, hs_f, hs_b, y_f, y_b, dy, L):
    _, R, _ = xbc.shape
    NX, NS = L // Q, R // Q
    cf0, cb0 = _chunk_maps(NX, NS)
    cf = lambda sp: cf0(NS - 1 - sp)
    cb = lambda sp: cb0(NS - 1 - sp)

    def body(xs_f, b_f, c_f, dt_f, hs_f_, dy_f, y_f_, xs_b, b_b, c_b, dt_b, hs_b_, dy_b, y_b_, a_ref,
             dxs_f, dbc_f, ddt_f, dxs_b, dbc_b, ddt_b, da_ref, dhT):
        @pl.when(pl.program_id(2) == 0)
        def _():
            dhT[...] = jnp.zeros_like(dhT)
            da_ref[...] = jnp.zeros_like(da_ref)

        lane = lax.broadcasted_iota(jnp.int32, (Q, 128), 1)
        row = lax.broadcasted_iota(jnp.int32, (Q, 128), 0)

        def one_chain(d, gi, xs_ref, b_ref, c_ref, dt_ref, hs_ref, dy_ref, y_ref, dxs_ref, dbc_ref, ddt_ref):
            cols = slice(gi * GWID, (gi + 1) * GWID)
            A = a_ref[gi, 0:1, :]
            xs, dt = xs_ref[0, :, cols].astype(F32), dt_ref[0, gi]
            B, C = b_ref[0, :, gi * NST:(gi + 1) * NST], c_ref[0, :, gi * NST:(gi + 1) * NST]
            q = _ssd_chunk(d, dt, A, xs, B, C)
            xt, lam, w, decay = q["xt"], q["lam"], q["w"], q["decay"]
            H = hs_ref[0, 0, :, cols]
            dyv = dy_ref[0, :, cols].astype(F32)
            dh = dhT[d, :, cols]
            dZ = dyv * lam
            dC = _dot_nt(dZ, H)
            dH = _dot_tn(C, dZ)
            U = _dot(B, dh)
            xw = xt * w
            dxt = U * w
            dalast_e = (jnp.sum(U * xw, axis=0, keepdims=True)
                        + decay * jnp.sum(dh * H.astype(F32), axis=0, keepdims=True))
            dB = _dot_nt(xw, dh)
            dCB = jnp.zeros((Q, Q), F32)
            dxt_parts = []
            for pr in range(HPG // 2):
                xp = xt[:, pr * 128:(pr + 1) * 128]
                dyp = dyv[:, pr * 128:(pr + 1) * 128]
                L0, L1 = _head_decay(q, d, 2 * pr), _head_decay(q, d, 2 * pr + 1)
                dyst = jnp.concatenate([jnp.where(lane < HEAD, dyp, 0.0), jnp.where(lane < HEAD, 0.0, dyp)], axis=0)
                mst = jnp.concatenate([(q["CB"] * L0).astype(BF), (q["CB"] * L1).astype(BF)], axis=0)
                dxt_parts.append(_dot_tn(mst, dyst))
                dmst = _dot_nt(dyst, xp)
                dCB = dCB + dmst[:Q] * L0 + dmst[Q:] * L1
            dxt_diag = jnp.concatenate(dxt_parts, axis=1)
            dC = dC + _dot(dCB, B)
            dB = dB + _dot_tn(dCB, C)
            Rm = _reduce_mat(d)
            dacs = _dot_sl(dyv * y_ref[0, :, cols] - xt.astype(BF).astype(F32) * dxt_diag - U * xw, Rm, 2)
            dxt = dxt + dxt_diag
            dal = _dot_sl(jnp.broadcast_to(dalast_e, (8, GWID)), Rm, 2)[0:1, :]
            dacs = dacs + jnp.where(row == (Q - 1 if d == 0 else 0), dal, 0.0)
            da = _dot_sr(q["Tt"], dacs, 2)
            ddt_ref[0, gi] = da * A + _dot_sl(dxt * xs, Rm, 2)
            da_ref[0, gi, 0:1, :] += jnp.sum(da * dt, axis=0, keepdims=True)
            dxs_ref[0, :, cols] = (dxt * q["dt_e"]).astype(BF)
            dbc_ref[0, :, gi * 2 * NST:(gi + 1) * 2 * NST] = jnp.concatenate([dB, dC], axis=1).astype(BF)
            dhT[d, :, cols] = decay * dh + dH

        for gi in range(GPS):
            one_chain(0, gi, xs_f, b_f, c_f, dt_f, hs_f_, dy_f, y_f_, dxs_f, dbc_f, ddt_f)
            one_chain(1, gi, xs_b, b_b, c_b, dt_b, hs_b_, dy_b, y_b_, dxs_b, dbc_b, ddt_b)

    def spec(shape, imap):
        return pl.BlockSpec(shape, imap)

    bc0 = DIN // (GPS * NST)

    def ins(c):
        return [spec((1, Q, GPS * GWID), lambda e, g, s: (e, c(s), g)),
                spec((1, Q, GPS * NST), lambda e, g, s: (e, c(s), bc0 + g)),
                spec((1, Q, GPS * NST), lambda e, g, s: (e, c(s), bc0 + NG // GPS + g)),
                spec((1, GPS, Q, 128), lambda e, g, s: (e, g, c(s), 0)),
                spec((1, 1, NST, GPS * GWID), lambda e, g, s: (e, c(s), 0, g)),
                spec((1, Q, GPS * GWID), lambda e, g, s: (e, c(s), g)),
                spec((1, Q, GPS * GWID), lambda e, g, s: (e, c(s), g))]

    def outs(c):
        return [spec((1, Q, GPS * GWID), lambda e, g, s: (e, c(s), g)),
                spec((1, Q, GPS * 2 * NST), lambda e, g, s: (e, c(s), g)),
                spec((1, GPS, Q, 128), lambda e, g, s: (e, g, c(s), 0))]

    s_xs = jax.ShapeDtypeStruct((2, R, DIN), BF)
    s_bc = jax.ShapeDtypeStruct((2, R, 2 * NG * NST), BF)
    s_dt = jax.ShapeDtypeStruct((2, NG, R, 128), F32)
    return pl.pallas_call(
        body, name="ssd_bwd", grid=(2, NG // GPS, NS),
        in_specs=ins(cf) + ins(cb) + [spec((GPS, 8, 128), lambda e, g, s: (g, 0, 0))],
        out_specs=tuple(outs(cf) + outs(cb) + [spec((1, GPS, 8, 128), lambda e, g, s: (e, g, 0, 0))]),
        out_shape=(s_xs, s_bc, s_dt, s_xs, s_bc, s_dt, jax.ShapeDtypeStruct((2, NG, 8, 128), F32)),
        scratch_shapes=[pltpu.VMEM((2, NST, GPS * GWID), F32)],
        compiler_params=_params(("parallel", "parallel", "arbitrary")),
    )(xbc, xbc, xbc, dt_loc, hs_f, dy, y_f, xbc, xbc, xbc, dt_loc, hs_b, dy, y_b, a_loc)


def _ssd_post_fwd(y_f, y_b, xbc, proj3, dskip_e, ssd_norm, L):
    def body(yf_ref, yb_ref, xs_ref, z_ref, ds_ref, w_ref, o_ref, y2_ref):
        y2 = yf_ref[0] + yb_ref[0] + ds_ref[...] * xs_ref[0].astype(F32)
        y2_ref[0] = y2.astype(BF)
        z = z_ref[0].astype(F32)
        u = y2 * (z * _sigmoid(z))
        parts = []
        for g in range(NG):
            ug = u[:, g * GWID:(g + 1) * GWID]
            parts.append(ug * lax.rsqrt(jnp.mean(ug * ug, axis=-1, keepdims=True) + EPS))
        o_ref[0] = (jnp.concatenate(parts, axis=1) * w_ref[...]).astype(BF)

    blk = lambda c: pl.BlockSpec((1, ROW_TILE, DIN), lambda e, t: (e, t, c))
    vec = pl.BlockSpec((1, DIN), lambda e, t: (0, 0))
    return pl.pallas_call(
        body, name="ssd_post_fwd", grid=(2, L // ROW_TILE),
        in_specs=[blk(0), blk(0), blk(0), blk(1), vec, vec],
        out_specs=(blk(0), blk(0)),
        out_shape=(jax.ShapeDtypeStruct((2, L, DIN), BF), jax.ShapeDtypeStruct((2, L, DIN), BF)),
        compiler_params=_params(("parallel", "parallel")),
    )(y_f, y_b, xbc, proj3, dskip_e, ssd_norm)


def _ssd_post_bwd(d_yn, y2b, xbc, proj3, ssd_norm, L):
    _, R, _ = xbc.shape
    nx = L // ROW_TILE

    def body(dyn_ref, y2_ref, xs_ref, z_ref, w_ref, dy_ref, dz_ref, acc_ref):
        t = pl.program_id(1)

        @pl.when(t == 0)
        def _():
            acc_ref[...] = jnp.zeros_like(acc_ref)

        @pl.when(t >= nx)
        def _():
            dy_ref[...] = jnp.zeros_like(dy_ref)
            dz_ref[...] = jnp.zeros_like(dz_ref)

        @pl.when(t < nx)
        def _():
            xs = xs_ref[0].astype(F32)
            y2 = y2_ref[0].astype(F32)
            z = z_ref[0].astype(F32)
            sg = _sigmoid(z)
            sz = z * sg
            u = y2 * sz
            dyn = dyn_ref[0].astype(F32)
            dun = dyn * w_ref[...]
            uh_parts, du_parts = [], []
            for g in range(NG):
                sl = slice(g * GWID, (g + 1) * GWID)
                ug = u[:, sl]
                rg = lax.rsqrt(jnp.mean(ug * ug, axis=-1, keepdims=True) + EPS)
                uh = ug * rg
                dg = dun[:, sl]
                du_parts.append(rg * (dg - uh * jnp.mean(dg * uh, axis=-1, keepdims=True)))
                uh_parts.append(uh)
            du = jnp.concatenate(du_parts, axis=1)
            uh = jnp.concatenate(uh_parts, axis=1)
            dy2 = du * sz
            dy_ref[0] = dy2.astype(BF)
            dz_ref[0] = (du * y2 * (sg * (1.0 + z * (1.0 - sg)))).astype(BF)
            acc_ref[0, 0:1, :] += jnp.sum(dyn * uh, axis=0, keepdims=True)
            acc_ref[0, 1:2, :] += jnp.sum(dy2 * xs, axis=0, keepdims=True)

    xmap = lambda c: (lambda e, t: (e, jnp.minimum(t, nx - 1), c))
    blk = lambda c: pl.BlockSpec((1, ROW_TILE, DIN), xmap(c))
    oblk = pl.BlockSpec((1, ROW_TILE, DIN), lambda e, t: (e, t, 0))
    vec = pl.BlockSpec((1, DIN), lambda e, t: (0, 0))
    return pl.pallas_call(
        body, name="ssd_post_bwd", grid=(2, R // ROW_TILE),
        in_specs=[blk(0), blk(0), blk(0), blk(1), vec],
        out_specs=(oblk, oblk, pl.BlockSpec((1, 8, DIN), lambda e, t: (e, 0, 0))),
        out_shape=(jax.ShapeDtypeStruct((2, R, DIN), BF), jax.ShapeDtypeStruct((2, R, DIN), BF),
                   jax.ShapeDtypeStruct((2, 8, DIN), F32)),
        compiler_params=_params(("parallel", "arbitrary")),
    )(d_yn, y2b, xbc, proj3, ssd_norm)


def _merge_fwd(proj3, P, S, b_merge, L):
    def body(gp_ref, p_ref, s_ref, b_ref, o_ref):
        gt = _sigmoid(gp_ref[0].astype(F32) + b_ref[...])
        o_ref[0] = (gt[:, :D] * p_ref[0].astype(F32) + gt[:, D:] * s_ref[0].astype(F32)).astype(BF)

    blk = pl.BlockSpec((1, ROW_TILE, D), lambda e, t: (e, t, 0))
    return pl.pallas_call(
        body, name="merge_fwd", grid=(2, L // ROW_TILE),
        in_specs=[pl.BlockSpec((1, ROW_TILE, 2 * D), lambda e, t: (e, t, OFF_GATE // (2 * D))), blk, blk,
                  pl.BlockSpec((1, 2 * D), lambda e, t: (0, 0))],
        out_specs=blk, out_shape=jax.ShapeDtypeStruct((2, L, D), BF),
        compiler_params=_params(("parallel", "parallel")),
    )(proj3, P, S, b_merge)


def _merge_bwd(d_merged, proj3, P, S, b_merge, L):
    _, R, _ = proj3.shape
    nx = L // ROW_TILE

    def body(dm_ref, gp_ref, p_ref, s_ref, b_ref, dp_ref, ds_ref, dg_ref, acc_ref):
        t = pl.program_id(1)

        @pl.when(t == 0)
        def _():
            acc_ref[...] = jnp.zeros_like(acc_ref)

        @pl.when(t >= nx)
        def _():
            dg_ref[...] = jnp.zeros_like(dg_ref)

        @pl.when(t < nx)
        def _():
            gt = _sigmoid(gp_ref[0].astype(F32) + b_ref[...])
            dm = dm_ref[0].astype(F32)
            g1, g2 = gt[:, :D], gt[:, D:]
            dp_ref[0] = (dm * g1).astype(BF)
            ds_ref[0] = (dm * g2).astype(BF)
            dgp = jnp.concatenate([dm * p_ref[0].astype(F32) * g1 * (1.0 - g1),
                                   dm * s_ref[0].astype(F32) * g2 * (1.0 - g2)], axis=1)
            dg_ref[0] = dgp.astype(BF)
            acc_ref[0, 0:1, :] += jnp.sum(dgp, axis=0, keepdims=True)

    xmap = lambda e, t: (e, jnp.minimum(t, nx - 1), 0)
    blk = pl.BlockSpec((1, ROW_TILE, D), xmap)
    return pl.pallas_call(
        body, name="merge_bwd", grid=(2, R // ROW_TILE),
        in_specs=[blk, pl.BlockSpec((1, ROW_TILE, 2 * D), lambda e, t: (e, jnp.minimum(t, nx - 1), OFF_GATE // (2 * D))),
                  blk, blk, pl.BlockSpec((1, 2 * D), lambda e, t: (0, 0))],
        out_specs=(blk, blk, pl.BlockSpec((1, ROW_TILE, 2 * D), lambda e, t: (e, t, 0)),
                   pl.BlockSpec((1, 8, 2 * D), lambda e, t: (e, 0, 0))),
        out_shape=(jax.ShapeDtypeStruct((2, L, D), BF), jax.ShapeDtypeStruct((2, L, D), BF),
                   jax.ShapeDtypeStruct((2, R, 2 * D), BF), jax.ShapeDtypeStruct((2, 8, 2 * D), F32)),
        compiler_params=_params(("parallel", "arbitrary")),
    )(d_merged, proj3, P, S, b_merge)


def _final(out3, x, tgt, gtab, norm_post, L):
    def body(o_ref, x_ref, t_ref, g_ref, n_ref, dxo_ref, do_ref, acc_ref):
        @pl.when(pl.program_id(1) == 0)
        def _():
            acc_ref[...] = jnp.zeros_like(acc_ref)

        o = o_ref[0].astype(F32)
        gate = g_ref[0, 0:1, :]
        npost = n_ref[...]
        r2 = lax.rsqrt(jnp.mean(o * o, axis=-1, keepdims=True) + EPS)
        nh = o * r2
        on = nh * npost
        err = x_ref[0] + gate * on - t_ref[0]
        dxo = err * (1.0 / D)
        dxo_ref[0] = dxo.astype(BF)
        dnh = dxo * gate * npost
        do_ref[0] = (r2 * (dnh - nh * jnp.mean(dnh * nh, axis=-1, keepdims=True))).astype(BF)
        acc_ref[0, 0:1, :] += jnp.sum(dxo * on, axis=0, keepdims=True)
        acc_ref[0, 1:2, :] += jnp.sum(dxo * gate * nh, axis=0, keepdims=True)
        acc_ref[0, 2:3, :] += jnp.sum(err * err, axis=0, keepdims=True)

    blk = pl.BlockSpec((1, ROW_TILE, D), lambda e, t: (e, t, 0))
    return pl.pallas_call(
        body, name="final", grid=(2, L // ROW_TILE),
        in_specs=[blk, blk, blk, pl.BlockSpec((1, 8, D), lambda e, t: (e, 0, 0)),
                  pl.BlockSpec((1, D), lambda e, t: (0, 0))],
        out_specs=(blk, blk, pl.BlockSpec((1, 8, D), lambda e, t: (e, 0, 0))),
        out_shape=(jax.ShapeDtypeStruct((2, L, D), BF), jax.ShapeDtypeStruct((2, L, D), BF),
                   jax.ShapeDtypeStruct((2, 8, D), F32)),
        compiler_params=_params(("parallel", "arbitrary")),
    )(out3, x, tgt, gtab, norm_post)


def _local_step(x, c, ctx, loss_target, W, late_shard=None, exchange=False):
    nb, L, _ = x.shape
    LC = ctx.shape[1]
    R = L + LC
    assert nb == 2 and L % ROW_TILE == 0 and LC % Q == 0 and L % POOL_TILE == 0
    w_inT = W["w_in"]
    w_dtT = jnp.pad(w_inT[OFF_DT:], ((0, 64), (0, 0)))
    tables = _pool_tables(L)
    tr, tl = (2 * R) // 8, (2 * L) // 8

    c16 = jnp.zeros((16, D), F32).at[0:2].set(c).at[2].set(W["c_ctx"])
    mod16 = _adaln_fwd(c16, W["w_ada"], W["b_ada"])
    shift, scale, gate = mod16[:, :D], mod16[:, D:2 * D], mod16[:, 2 * D:]
    npre = W["norm_pre"]
    tab = jnp.zeros((2, 2, 8, D), F32)
    for e in range(2):
        tab = tab.at[e, 0, 0].set(npre[0] * (1.0 + scale[e])).at[e, 0, 1].set(shift[e])
        tab = tab.at[e, 1, 0].set(npre[0] * (1.0 + scale[2])).at[e, 1, 1].set(shift[2])
    gtab = jnp.zeros((2, 8, D), F32).at[:, 0].set(gate[0:2])

    hx = _norm_mod_fwd(x, ctx, tab)
    hx2 = hx.reshape(2 * R, D)
    if late_shard is None:
        proj = _matmul(hx2, w_inT, BF, "proj_main", tm=tr, tn=1024, bt=True, n=OFF_DT)
    else:
        proj, late = _matmul(hx2, w_inT, BF, "proj_main", tm=tr, tn=1024, bt=True, n=OFF_DT, side=_gather_side(late_shard))
        W = {**W, **_unpack_gather(late, GATHER_LATE)}
    proj3 = proj.reshape(2, R, OFF_DT)
    dt_raw = _matmul(hx2, w_dtT, F32, "proj_dt", tm=tr, bt=True).reshape(2, R, 128)
    ypool = _pool_fwd(proj3, W["pool_w"], W["pool_scale"], tables, L)
    xbc = _conv_fwd(proj3, W["conv_w"], W["conv_b"], L)
    bias128 = jnp.pad(W["dt_bias"].reshape(1, 64), ((0, 0), (0, 64)))
    dt_loc = _dt_fwd(dt_raw, bias128)
    A = -jnp.exp(W["a_log"].reshape(2, NG, HPG))
    a_loc = jnp.zeros((NG, 8, 128), F32).at[:, 0, :16].set(A.transpose(1, 0, 2).reshape(NG, 16))
    y_f, hs_f, y_b, hs_b = _ssd_fwd(xbc, dt_loc, a_loc, L)
    dskip_e = jnp.repeat(W["d_skip"].reshape(1, 32), HEAD, axis=1)
    yn, y2b = _ssd_post_fwd(y_f, y_b, xbc, proj3, dskip_e, W["ssd_norm"], L)
    ypool2, yn2 = ypool.reshape(2 * L, D), yn.reshape(2 * L, DIN)
    P = _matmul(ypool2, W["w_proj_pool"], BF, "proj_pool", tm=tl, tn=1024).reshape(2, L, D)
    S = _matmul(yn2, W["w_proj_ssd"], BF, "proj_ssd", tm=tl, tn=1024).reshape(2, L, D)
    merged = _merge_fwd(proj3, P, S, W["b_merge"], L)
    merged2 = merged.reshape(2 * L, D)
    out3 = _matmul(merged2, W["w_out"], BF, "proj_out", tm=tl, tn=1024).reshape(2, L, D)
    dxo, dout, acc_f = _final(out3, x, loss_target, gtab, W["norm_post"], L)

    dout2 = dout.reshape(2 * L, D)
    g = {}
    g["w_out"] = _matmul_tn(merged2, dout2, "dw_out", ta=1024, tn=1024, tr=4 * tl)
    d_merged = _matmul(dout2, W["w_out"], BF, "d_merged", tm=tl, tn=1024, bt=True).reshape(2, L, D)
    dP, dS, dgp, acc_m = _merge_bwd(d_merged, proj3, P, S, W["b_merge"], L)
    dP2, dS2 = dP.reshape(2 * L, D), dS.reshape(2 * L, D)
    g["w_proj_pool"] = _matmul_tn(ypool2, dP2, "dw_proj_pool", ta=1024, tn=1024, tr=4 * tl)
    g["w_proj_ssd"] = _matmul_tn(yn2, dS2, "dw_proj_ssd", ta=1024, tn=1024, tr=4 * tl)
    d_ypool = _matmul(dP2, W["w_proj_pool"], BF, "d_ypool", tm=tl, tn=1024, bt=True).reshape(2, L, D)
    d_yn = _matmul(dS2, W["w_proj_ssd"], BF, "d_yn", tm=tl, tn=1024, bt=True).reshape(2, L, DIN)
    dv, dzp, g["pool_w"], acc_p = _pool_bwd(proj3, d_ypool, W["pool_w"], jnp.swapaxes(W["pool_w"], 1, 2),
                                            W["pool_scale"], tables, L)
    dy2, dzs, acc_s = _ssd_post_bwd(d_yn, y2b, xbc, proj3, W["ssd_norm"], L)
    dxs_f, dbc_f, ddt_f, dxs_b, dbc_b, ddt_b, acc_a = _ssd_bwd(xbc, dt_loc, a_loc, hs_f, hs_b, y_f, y_b, dy2, L)
    ident = lambda j: j
    dxr_xs, acc_cx = _conv_bwd(proj3, [dxs_f, dxs_b, dy2], [None, None, dskip_e], 0, DIN, [ident, ident, ident],
                               W["conv_w"], W["conv_b"], L, "conv_bwd_xs")
    bcmap = lambda j: 2 * lax.rem(j, NG) + j // NG
    dxr_bc, acc_cb = _conv_bwd(proj3, [dbc_f, dbc_b], [None, None], DIN, 2 * NG * NST, [bcmap, bcmap],
                               W["conv_w"], W["conv_b"], L, "conv_bwd_bc")
    ddtr, acc_d = _dt_bwd(dt_raw, bias128, ddt_f, ddt_b)
    pieces = [dv, dzp, dzs, dgp, dxr_xs, dxr_bc]
    dw_rows = [_matmul_tn(p.reshape(2 * R, p.shape[2]), hx2, "dw_in_%d" % i, ta=1024, tn=1024, tr=4 * tr)
               for i, p in enumerate(pieces)]
    dw_rows.append(_matmul_tn(ddtr.reshape(2 * R, 128), hx2, "dw_in_dt", ta=128, tn=1024, tr=tr)[:64])
    g["w_in"] = jnp.concatenate(dw_rows, axis=0)
    acc_c = jnp.concatenate([acc_cx[0] + acc_cx[1], acc_cb[0] + acc_cb[1]], axis=1)
    g["conv_w"] = acc_c[0:4]
    g["conv_b"] = acc_c[4:5]
    if exchange:
        gb = _pack_grads(g, GRADS_EARLY)
        pair = _pair_add(gb, _pair_exchange(gb, None, "grads_pair_exchange_early"), "grads_pair_add_early")
        dh, recv_early = _dhx(pieces, ddtr, w_inT, w_dtT, side=_chip_exchange_side(pair))
    else:
        dh, recv_early = _dhx(pieces, ddtr, w_inT, w_dtT), None
    grad_x, acc_n = _norm_mod_bwd(dh, x, ctx, tab, dxo)
    g["w_ada"], db_rows, sm_rows = _adaln_bwd(acc_n, acc_f, mod16, c16, npre, W["w_ada"])

    g["b_ada"] = db_rows[0:1]
    g["norm_pre"] = sm_rows[0:1]
    g["c_ctx"] = sm_rows[1]
    g["norm_post"] = acc_f[0, 1:2] + acc_f[1, 1:2]
    g["b_merge"] = acc_m[0, 0:1] + acc_m[1, 0:1]
    g["pool_scale"] = acc_p[:, 0, :].reshape(1, D)
    g["dt_bias"] = (acc_d[0, 0, :64] + acc_d[1, 0, :64]).reshape(2, 32)
    dA = (acc_a[0, :, 0, :16] + acc_a[1, :, 0, :16]).reshape(NG, 2, HPG).transpose(1, 0, 2)
    g["a_log"] = (dA * A).reshape(2, 32)
    g["d_skip"] = (acc_s[0, 1] + acc_s[1, 1]).reshape(32, HEAD).sum(axis=1).reshape(1, 32)
    g["ssd_norm"] = acc_s[0, 0:1] + acc_s[1, 0:1]
    loss_lanes = acc_f[:, 2, :]
    return loss_lanes, grad_x, g, recv_early


MESH = pl.DeviceIdType.MESH
ANY = pl.BlockSpec(memory_space=pl.ANY)


def _all_gather(shard):
    m_per, n = shard.shape

    def body(x_ref, out_ref, send_sems, recv_sems, local_sem):
        x, y, c = lax.axis_index("x"), lax.axis_index("y"), lax.axis_index("c")
        me, sibling = (x, y, c), (x, y, 1 - c)
        chips = [(1 - x, y), (x, 1 - y), (1 - x, 1 - y)]

        def rows(px, py, pc):
            return out_ref.at[pl.ds((4 * px + 2 * py + pc) * m_per, m_per), :]

        def copy(k, block, to, src=None):
            return pltpu.make_async_remote_copy(
                src_ref=rows(*block) if src is None else src, dst_ref=rows(*block),
                send_sem=send_sems.at[k], recv_sem=recv_sems.at[k], device_id=to, device_id_type=MESH)

        mine = pltpu.make_async_copy(x_ref, rows(*me), local_sem)
        mine.start()
        first = [copy(0, me, sibling, src=x_ref)]
        first += [copy(1 + j, me, (*chip, c), src=x_ref) for j, chip in enumerate(chips)]
        for cp in first:
            cp.start()
        passed = [copy(4 + j, (*chip, c), sibling) for j, chip in enumerate(chips)]
        for j, chip in enumerate(chips):
            copy(1 + j, (*chip, c), me).wait_recv()
            passed[j].start()
        copy(0, sibling, me).wait_recv()
        for j, chip in enumerate(chips):
            copy(4 + j, (*chip, 1 - c), me).wait_recv()
        for cp in first + passed:
            cp.wait_send()
        mine.wait()

    return pl.pallas_call(
        body, name="all_gather_weights",
        out_shape=jax.ShapeDtypeStruct((NDEV * m_per, n), shard.dtype),
        in_specs=[ANY], out_specs=ANY,
        scratch_shapes=[pltpu.SemaphoreType.DMA((7,)), pltpu.SemaphoreType.DMA((7,)), pltpu.SemaphoreType.DMA],
    )(shard)


PAIR_PIECES = 4


def _xor_peer(k, x, y, c):
    return (1 - x if k & 4 else x, 1 - y if k & 2 else y, 1 - c if k & 1 else c)


def _pair_exchange(big, small, name):
    _, nq, rows, n = big.shape
    piece = rows // PAIR_PIECES
    assert piece * PAIR_PIECES == rows and piece % 16 == 0
    with_small = small is not None

    def body(*refs):
        if with_small:
            big_ref, small_ref, got_ref, osmall_ref, send_sems, recv_sems, local_sem = refs
        else:
            big_ref, got_ref, send_sems, recv_sems, local_sem = refs
        x, y, c = lax.axis_index("x"), lax.axis_index("y"), lax.axis_index("c")
        me = 4 * x + 2 * y + c

        def rc(src, dst, sem, peer):
            return pltpu.make_async_remote_copy(src_ref=src, dst_ref=dst, send_sem=send_sems.at[sem],
                                                recv_sem=recv_sems.at[sem], device_id=peer, device_id_type=MESH)

        sib = _xor_peer(1, x, y, c)
        local, sends, recvs = [], [], []
        for q in range(nq):
            for h in range(PAIR_PIECES):
                rws = pl.ds(h * piece, piece)
                cp = rc(big_ref.at[1 - c, q, rws], got_ref.at[q, rws], 8 + q * PAIR_PIECES + h, sib)
                sends.append(cp)
                recvs.append(cp)
        if with_small:
            local.append(pltpu.make_async_copy(small_ref, osmall_ref.at[me], local_sem))
            for k in range(1, NDEV):
                px, py, pc = _xor_peer(k, x, y, c)
                sends.append(rc(small_ref, osmall_ref.at[me], k, (px, py, pc)))
                recvs.append(rc(small_ref, osmall_ref.at[4 * px + 2 * py + pc], k, (px, py, pc)))
        for cp in local + sends:
            cp.start()
        for cp in sends:
            cp.wait_send()
        for cp in recvs:
            cp.wait_recv()
        for cp in local:
            cp.wait()

    nsem = 8 + nq * PAIR_PIECES
    out_shape = [jax.ShapeDtypeStruct(big.shape[1:], big.dtype)]
    if with_small:
        out_shape.append(jax.ShapeDtypeStruct((NDEV,) + small.shape, small.dtype))
    out = pl.pallas_call(
        body, name=name, out_shape=tuple(out_shape),
        in_specs=[ANY] * (1 + with_small), out_specs=(ANY,) * (1 + with_small),
        scratch_shapes=[pltpu.SemaphoreType.DMA((nsem,)), pltpu.SemaphoreType.DMA((nsem,)), pltpu.SemaphoreType.DMA],
    )(*((big, small) if with_small else (big,)))
    return out if with_small else out[0]


def _pair_add(big, got, name):
    _, nq, rows, n = big.shape
    tile = rows // 4
    assert rows % 64 == 0

    def body(c_ref, a_ref, b_ref, o_ref):
        o_ref[0] = (a_ref[0, 0].astype(F32) + b_ref[0].astype(F32)).astype(BF)

    blk = pl.BlockSpec((1, tile, n), lambda q, i, c_ref: (q, i, 0))
    return pl.pallas_call(
        body, name=name,
        grid_spec=pltpu.PrefetchScalarGridSpec(
            num_scalar_prefetch=1, grid=(nq, rows // tile),
            in_specs=[pl.BlockSpec((1, 1, tile, n), lambda q, i, c_ref: (c_ref[0], q, i, 0)), blk], out_specs=blk),
        out_shape=jax.ShapeDtypeStruct(got.shape, BF), compiler_params=_params(("parallel", "parallel")),
    )(lax.axis_index("c").astype(jnp.int32).reshape(1), big, got)


def _chip_exchange_side(pair):
    def make(in_refs, out_refs, send_sems, recv_sems, local_sem, arrivals=True):
        (in_ref,), (out_ref,) = in_refs, out_refs
        x, y, c = lax.axis_index("x"), lax.axis_index("y"), lax.axis_index("c")
        q = 2 * x + y
        local = [pltpu.make_async_copy(in_ref.at[q], out_ref.at[q], local_sem)]
        sends, recvs = [], []
        for j in range(1, 4):
            px, py, pc = _xor_peer(2 * j, x, y, c)
            pq = 2 * px + py
            for lst, dst in ((sends, out_ref.at[q]), (recvs, out_ref.at[pq]))[:1 + arrivals]:
                lst.append(pltpu.make_async_remote_copy(
                    src_ref=in_ref.at[pq], dst_ref=dst, send_sem=send_sems.at[j - 1], recv_sem=recv_sems.at[j - 1],
                    device_id=(px, py, pc), device_id_type=MESH))
        return local, sends, recvs

    return _SideCopies([pair], [jax.ShapeDtypeStruct(pair.shape, pair.dtype)], make)


def _gather_side(shard):
    def make(in_refs, out_refs, send_sems, recv_sems, local_sem, arrivals=True):
        (src,), (dst,) = in_refs, out_refs
        x, y, c = lax.axis_index("x"), lax.axis_index("y"), lax.axis_index("c")
        me = 4 * x + 2 * y + c
        local = [pltpu.make_async_copy(src, dst.at[me], local_sem)]
        sends, recvs = [], []
        for k in range(1, NDEV):
            px, py, pc = _xor_peer(k, x, y, c)
            for lst, slot in ((sends, me), (recvs, 4 * px + 2 * py + pc))[:1 + arrivals]:
                lst.append(pltpu.make_async_remote_copy(
                    src_ref=src, dst_ref=dst.at[slot], send_sem=send_sems.at[k - 1], recv_sem=recv_sems.at[k - 1],
                    device_id=(px, py, pc), device_id_type=MESH))
        return local, sends, recvs

    return _SideCopies([shard], [jax.ShapeDtypeStruct((NDEV,) + shard.shape, shard.dtype)], make)


ADAM_TILE = 64
PACK_W = 1024


def _adamw(recv, w, m, v, name, side=None):
    rp = w.shape[0]
    tile = min(ADAM_TILE, rp)
    nsrc = recv.shape[0]
    grid = (rp // tile,)
    n_si, n_so = (len(side.inputs), len(side.out_shapes)) if side else (0, 0)

    def body(*refs):
        r_ref, w_ref, m_ref, v_ref = refs[:4]
        g_ref, d_ref, nm_ref, nv_ref = refs[4 + n_si:8 + n_si]
        side_refs = (refs[4:4 + n_si], refs[8 + n_si:8 + n_si + n_so], refs[8 + n_si + n_so:])
        if side:
            side.start(grid, *side_refs)
        g = r_ref[0].astype(F32)
        for i in range(1, nsrc):
            g = g + r_ref[i].astype(F32)
        m1 = ADAM_B1 * m_ref[...] + (1.0 - ADAM_B1) * g
        v1 = ADAM_B2 * v_ref[...] + (1.0 - ADAM_B2) * (g * g)
        m_hat = m1 / (1.0 - ADAM_B1 ** ADAM_STEP)
        v_hat = v1 / (1.0 - ADAM_B2 ** ADAM_STEP)
        g_ref[...] = g
        d_ref[...] = -ADAM_LR * (m_hat / (jnp.sqrt(v_hat) + ADAM_EPS) + ADAM_WD * w_ref[...])
        nm_ref[...] = m1
        nv_ref[...] = v1
        if side:
            side.wait(grid, *side_refs)

    blk = pl.BlockSpec((tile, PACK_W), lambda i: (i, 0))
    shp = jax.ShapeDtypeStruct((rp, PACK_W), F32)
    return pl.pallas_call(
        body, name=name, grid=grid,
        in_specs=[pl.BlockSpec((nsrc, tile, PACK_W), lambda i: (0, i, 0)), blk, blk, blk] + [ANY] * n_si,
        out_specs=(blk, blk, blk, blk) + (ANY,) * n_so,
        out_shape=(shp, shp, shp, shp) + tuple(side.out_shapes if side else ()),
        scratch_shapes=side.scratch() if side else [],
        compiler_params=_params(("arbitrary",) if side else ("parallel",)),
    )(recv, w, m, v, *(side.inputs if side else ()))


BIG = {"w_ada": ((3 * D, D), 0), "pool_w": ((4, PGW, PGW), 1), "w_proj_pool": ((D, D), 0), "w_proj_ssd": ((DIN, D), 0),
       "w_out": ((D, D), 0), "w_in": ((IN_COLS, D), 0), "conv_w": ((4, CONV_DIM), 1)}
TRANSPOSED = ("w_ada", "w_in")
PACK_ROWS = {"w_ada": 384, "w_in": 1168, "conv_w": 16, "pool_w": 32, "w_proj_pool": 128, "w_proj_ssd": 256, "w_out": 128}
GATHER_EARLY = ("w_ada", "w_in", "conv_w")
GATHER_LATE = ("pool_w", "w_proj_pool", "w_proj_ssd", "w_out")
GRADS_LATE = ("w_ada",)
GRADS_EARLY = tuple(n for n in PACK_ROWS if n not in GRADS_LATE)
SMALL = {"c_ctx": (D,), "b_ada": (1, 3 * D), "norm_pre": (1, D), "norm_post": (1, D), "b_merge": (1, 2 * D),
         "pool_scale": (1, D), "conv_b": (1, CONV_DIM), "dt_bias": (2, 32), "a_log": (2, 32), "d_skip": (1, 32),
         "ssd_norm": (1, DIN)}
LOSS_SLOT = 128
assert all(_r % 16 == 0 for _r in PACK_ROWS.values())
SMALL_ROWS = 16


def _shard_shape(name):
    shape, ax = BIG[name]
    return tuple(s // NDEV if i == ax else s for i, s in enumerate(shape))


def _as_rows(t, rows):
    pad = [(0, 0)] * (t.ndim - 1) + [(0, rows * PACK_W - t.shape[-1])]
    return jnp.pad(t, pad).reshape(t.shape[:-1] + (rows, PACK_W))


def _shard_rows(t, name):
    sh, r = _shard_shape(name), PACK_ROWS[name]
    lead = t.shape[:t.ndim - len(sh)]
    if len(sh) == 2 and sh[1] == PACK_W:
        return jnp.pad(t, [(0, 0)] * len(lead) + [(0, r - sh[0]), (0, 0)])
    if int(np.prod(sh)) == r * PACK_W:
        return t.reshape(lead + (r, PACK_W))
    return _as_rows(t.reshape(lead + (-1,)), r)


def _to_chunks(full, name):
    shape, ax = BIG[name]
    split = shape[:ax] + (NDEV, shape[ax] // NDEV) + shape[ax + 1:]
    return _shard_rows(jnp.moveaxis(full.reshape(split), ax, 0), name)


def _from_chunks(chunks, name):
    shape, ax = BIG[name]
    return jnp.moveaxis(chunks.reshape((NDEV,) + _shard_shape(name)), 0, ax).reshape(shape)


def _rows_of(names):
    return sum(PACK_ROWS[n] for n in names)


def _pack_state(t, names):
    return jnp.concatenate([_shard_rows(t[n], n) for n in names], axis=0)


def _pack_small(t, loss_part=None):
    slot = jnp.zeros((LOSS_SLOT,), F32)
    if loss_part is not None:
        slot = slot.at[0].set(loss_part)
    return _as_rows(jnp.concatenate([t[n].reshape(-1) for n in SMALL] + [slot]), SMALL_ROWS)


def _pack_grads(g, names):
    big = jnp.concatenate([_to_chunks(g[n], n).astype(BF) for n in names], axis=1)
    return jnp.swapaxes(big.reshape(4, 2, _rows_of(names), PACK_W), 0, 1)


def _unpack_state(big, names):
    out, off = {}, 0
    for n in names:
        sh, r = _shard_shape(n), PACK_ROWS[n]
        k = int(np.prod(sh))
        if len(sh) == 2 and sh[1] == PACK_W:
            out[n] = big[off:off + sh[0]]
        else:
            out[n] = big[off:off + r].reshape(-1)[:k].reshape(sh)
        off += r
    return out


def _unpack_small(small):
    out, flat, off = {}, small.reshape(-1), 0
    for n, sh in SMALL.items():
        k = int(np.prod(sh))
        out[n] = flat[off:off + k].reshape(sh)
        off += k
    out["loss"] = flat[off]
    return out


def _pack_gather(w, names):
    pieces = []
    for n in names:
        if n == "conv_w":
            pieces.append(_as_rows(jnp.concatenate([p.reshape(-1) for p in _split(w[n], 3)]), PACK_ROWS[n]))
        else:
            pieces.append(_shard_rows(w[n], n).astype(BF))
    return jnp.concatenate(pieces, axis=0)


def _unpack_gather(gathered, names):
    g = gathered.reshape(NDEV, _rows_of(names), PACK_W)
    out, off = {}, 0
    for n in names:
        r = PACK_ROWS[n]
        sh = _shard_shape(n)
        if n == "conv_w":
            k = int(np.prod(sh))
            terms = g[:, off:off + r].reshape(NDEV, -1)[:, :3 * k].astype(F32).reshape(NDEV, 3, k)
            out[n] = _from_chunks(terms[:, 0] + terms[:, 1] + terms[:, 2], n)
        elif len(sh) == 2 and sh[1] == PACK_W:
            out[n] = _from_chunks(g[:, off:off + sh[0]], n)
        else:
            out[n] = _from_chunks(g[:, off:off + r], n)
        off += r
    return out


PARAMS = ["c_ctx", "w_ada", "b_ada", "norm_pre", "norm_post", "w_in", "b_merge", "pool_w", "pool_scale", "conv_w", "conv_b",
          "dt_bias", "a_log", "d_skip", "ssd_norm", "w_proj_pool", "w_proj_ssd", "w_out"]


def kernel(x, c, ctx, c_ctx, w_ada, b_ada, norm_pre, norm_post, w_in, b_merge, pool_w, pool_scale, conv_w, conv_b, dt_bias, a_log, d_skip, ssd_norm, w_proj_pool, w_proj_ssd, w_out, loss_target, m_c_ctx, m_w_ada, m_b_ada, m_norm_pre, m_norm_post, m_w_in, m_b_merge, m_pool_w, m_pool_scale, m_conv_w, m_conv_b, m_dt_bias, m_a_log, m_d_skip, m_ssd_norm, m_w_proj_pool, m_w_proj_ssd, m_w_out, v_c_ctx, v_w_ada, v_b_ada, v_norm_pre, v_norm_post, v_w_in, v_b_merge, v_pool_w, v_pool_scale, v_conv_w, v_conv_b, v_dt_bias, v_a_log, v_d_skip, v_ssd_norm, v_w_proj_pool, v_w_proj_ssd, v_w_out):
    given = dict(locals())
    shapes = {n: given[n].shape for n in PARAMS}

    def local(prefix):
        t = {n: (given[prefix + n] if n == "c_ctx" else given[prefix + n][0]) for n in PARAMS}
        for n in TRANSPOSED:
            t[n] = t[n].T
        return {n: t[n].reshape(_shard_shape(n) if n in BIG else SMALL[n]) for n in PARAMS}

    w, m, v = local(""), local("m_"), local("v_")

    W = _unpack_gather(_all_gather(_pack_gather(w, GATHER_EARLY)), GATHER_EARLY)
    for n in SMALL:
        W[n] = w[n]
    lanes, grad_x, g, recv_early = _local_step(x, c, ctx, loss_target, W, late_shard=_pack_gather(w, GATHER_LATE),
                                               exchange=True)
    gb = _pack_grads(g, GRADS_LATE)
    got, recv_small = _pair_exchange(gb, _pack_small(g, (0.5 / D) * jnp.sum(lanes)), "grads_pair_exchange_late")
    late = _chip_exchange_side(_pair_add(gb, got, "grads_pair_add_late"))
    res = [{} for _ in range(4)]
    *early, recv_late = _adamw(recv_early, *[_pack_state(s, GRADS_EARLY) for s in (w, m, v)], "adamw_early", side=late)
    for r, t in zip(res, early):
        r.update(_unpack_state(t, GRADS_EARLY))
    for r, t in zip(res, _adamw(recv_late, *[_pack_state(s, GRADS_LATE) for s in (w, m, v)], "adamw_late")):
        r.update(_unpack_state(t, GRADS_LATE))
    for r, t in zip(res, _adamw(recv_small, *[_pack_small(s) for s in (w, m, v)], "adamw_small")):
        r.update(_unpack_small(t))
    outs = [res[0]["loss"], grad_x]
    for r in res:
        for n in TRANSPOSED:
            r[n] = r[n].T
        outs += [r[n].reshape(shapes[n]) for n in PARAMS]
    return tuple(outs)
```

```python
import functools

import numpy as np
import jax
import jax.numpy as jnp
from jax import lax
from jax.experimental import pallas as pl
from jax.experimental.pallas import tpu as pltpu

F32, BF = jnp.float32, jnp.bfloat16

D = 1024
GRID_W = 64
EPS = 1e-6
POOL_WINDOWS = (2, 4, 8, 16)
PGW = 256
DIN = 2048
HEAD = 64
NST = 128
NG = 4
HPG = 8
GWID = HPG * HEAD
Q = 128
CONV_DIM = 3072
OFF_GATE, OFF_XBC, OFF_DT, IN_COLS = 4096, 6144, 9216, 9280
NDEV = 8
ADAM_LR, ADAM_B1, ADAM_B2, ADAM_EPS, ADAM_WD, ADAM_STEP = 0.001, 0.9, 0.999, 1e-08, 0.01, 10

V7X_VMEM_LIMIT = 56 * 2 ** 20
ROW_TILE = 256
X_TILE = 512


def _params(sem=None):
    return pltpu.CompilerParams(dimension_semantics=sem, vmem_limit_bytes=V7X_VMEM_LIMIT)


def _dot(a, b):
    return jnp.dot(a.astype(BF), b.astype(BF), preferred_element_type=F32)


def _dot_nt(a, b):
    return lax.dot_general(a.astype(BF), b.astype(BF), (((1,), (1,)), ((), ())), preferred_element_type=F32)


def _dot_tn(a, b):
    return lax.dot_general(a.astype(BF), b.astype(BF), (((0,), (0,)), ((), ())), preferred_element_type=F32)


def _split(a, n):
    parts = []
    for _ in range(n):
        p = a.astype(BF)
        parts.append(p)
        a = a - p.astype(F32)
    return parts


def _dot_sl(a, b01, n=3):
    parts = _split(a, n)
    m = a.shape[0]
    if n == 1 or m % 16:
        return sum(jnp.dot(p, b01, preferred_element_type=F32) for p in parts)
    r = jnp.dot(jnp.concatenate(parts, axis=0), b01, preferred_element_type=F32)
    return sum(r[i * m:(i + 1) * m] for i in range(n))


def _dot_sr(a01, b, n=3):
    parts = _split(b, n)
    k = b.shape[1]
    if n == 1 or k % 128:
        return sum(jnp.dot(a01, p, preferred_element_type=F32) for p in parts)
    r = jnp.dot(a01, jnp.concatenate(parts, axis=1), preferred_element_type=F32)
    return sum(r[:, i * k:(i + 1) * k] for i in range(n))


def _sigmoid(x):
    return 1.0 / (1.0 + jnp.exp(-x))


class _SideCopies:
    NSEM = 8

    def __init__(self, inputs, out_shapes, make):
        self.inputs, self.out_shapes, self.make = list(inputs), list(out_shapes), make

    def scratch(self):
        return [pltpu.SemaphoreType.DMA((self.NSEM,)), pltpu.SemaphoreType.DMA((self.NSEM,)), pltpu.SemaphoreType.DMA]

    def start(self, grid, in_refs, out_refs, sems):
        @pl.when(functools.reduce(lambda p, q: p & q, [pl.program_id(i) == 0 for i in range(len(grid))]))
        def _():
            local, sends, _ = self.make(in_refs, out_refs, *sems, arrivals=False)
            for cp in local + sends:
                cp.start()

    def wait(self, grid, in_refs, out_refs, sems):
        @pl.when(functools.reduce(lambda p, q: p & q, [pl.program_id(i) == n - 1 for i, n in enumerate(grid)]))
        def _():
            local, sends, recvs = self.make(in_refs, out_refs, *sems)
            for cp in sends:
                cp.wait_send()
            for cp in recvs:
                cp.wait_recv()
            for cp in local:
                cp.wait()


def _matmul(a, b, out_dtype, name, tm=512, tn=512, tk=1024, bt=False, n=None, side=None):
    M, K = a.shape
    N = n if n is not None else (b.shape[0] if bt else b.shape[1])
    tm, tn, tk = min(tm, M), min(tn, N), min(tk, K)
    assert M % tm == 0 and N % tn == 0 and K % tk == 0, (a.shape, b.shape)
    nk = K // tk
    grid = (M // tm, N // tn, nk)
    n_si, n_so = (len(side.inputs), len(side.out_shapes)) if side else (0, 0)

    def body(*refs):
        a_ref, b_ref, o_ref = refs[0], refs[1], refs[2 + n_si]
        acc = refs[3 + n_si + n_so]
        side_refs = (refs[2:2 + n_si], refs[3 + n_si:3 + n_si + n_so], refs[4 + n_si + n_so:])
        if side:
            side.start(grid, *side_refs)
        k = pl.program_id(2)
        p = _dot_nt(a_ref[...], b_ref[...]) if bt else _dot(a_ref[...], b_ref[...])

        @pl.when(k == 0)
        def _():
            acc[...] = p

        @pl.when(k > 0)
        def _():
            acc[...] += p

        @pl.when(k == nk - 1)
        def _():
            o_ref[...] = acc[...].astype(o_ref.dtype)

        if side:
            side.wait(grid, *side_refs)

    out = pl.pallas_call(
        body, name=name, grid=grid,
        in_specs=[pl.BlockSpec((tm, tk), lambda i, j, k: (i, k)),
                  pl.BlockSpec((tn, tk), lambda i, j, k: (j, k)) if bt else pl.BlockSpec((tk, tn), lambda i, j, k: (k, j))]
        + [ANY] * n_si,
        out_specs=(pl.BlockSpec((tm, tn), lambda i, j, k: (i, j)),) + (ANY,) * n_so,
        out_shape=(jax.ShapeDtypeStruct((M, N), out_dtype),) + tuple(side.out_shapes if side else ()),
        scratch_shapes=[pltpu.VMEM((tm, tn), F32)] + (side.scratch() if side else []),
        compiler_params=_params(("arbitrary",) * 3 if side else ("parallel", "parallel", "arbitrary")),
    )(a, b, *(side.inputs if side else ()))
    return out if side else out[0]


def _matmul_tn(a, g, name, ta=512, tn=512, tr=512):
    M, Ka = a.shape
    N = g.shape[1]
    ta, tn, tr = min(ta, Ka), min(tn, N), min(tr, M)
    assert M % tr == 0 and N % tn == 0 and Ka % ta == 0, (a.shape, g.shape)
    nr = M // tr

    def body(a_ref, g_ref, o_ref):
        k = pl.program_id(2)
        p = _dot_tn(a_ref[...], g_ref[...])

        @pl.when(k == 0)
        def _():
            o_ref[...] = p

        @pl.when(k > 0)
        def _():
            o_ref[...] += p

    return pl.pallas_call(
        body, name=name, grid=(Ka // ta, N // tn, nr),
        in_specs=[pl.BlockSpec((tr, ta), lambda i, j, k: (k, i)), pl.BlockSpec((tr, tn), lambda i, j, k: (k, j))],
        out_specs=pl.BlockSpec((ta, tn), lambda i, j, k: (i, j)),
        out_shape=jax.ShapeDtypeStruct((Ka, N), F32),
        compiler_params=_params(("parallel", "parallel", "arbitrary")),
    )(a, g)


def _dhx(pieces, ddt, w_inT, w_dtT, side=None):
    _, R, _ = pieces[0].shape
    tm = R // 4
    kb = 1024
    starts, nblk = [], []
    for p in pieces:
        starts.append(sum(nblk))
        nblk.append(p.shape[2] // kb)
    nk = sum(nblk)
    assert nk * kb == OFF_DT and R % 128 == 0
    npc = len(pieces)
    grid = (2, R // tm, nk)
    n_si, n_so = (len(side.inputs), len(side.out_shapes)) if side else (0, 0)

    def body(*refs):
        a_refs, dt_ref, w_ref, wdt_ref = refs[:npc], refs[npc], refs[npc + 1], refs[npc + 2]
        o_ref, acc = refs[npc + 3 + n_si], refs[npc + 4 + n_si + n_so]
        side_refs = (refs[npc + 3:npc + 3 + n_si], refs[npc + 4 + n_si:npc + 4 + n_si + n_so], refs[npc + 5 + n_si + n_so:])
        if side:
            side.start(grid, *side_refs)
        k = pl.program_id(2)

        @pl.when(k == 0)
        def _():
            acc[...] = _dot(dt_ref[0], wdt_ref[...])

        for p in range(npc):
            @pl.when((k >= starts[p]) & (k < starts[p] + nblk[p]))
            def _(p=p):
                acc[...] += _dot(a_refs[p][0], w_ref[...])

        @pl.when(k == nk - 1)
        def _():
            o_ref[0] = acc[...].astype(BF)

        if side:
            side.wait(grid, *side_refs)

    in_specs = [pl.BlockSpec((1, tm, kb), functools.partial(
        lambda e, t, k, s, nb: (e, t, jnp.clip(k - s, 0, nb - 1)), s=starts[p], nb=nblk[p])) for p in range(npc)]
    in_specs += [pl.BlockSpec((1, tm, 128), lambda e, t, k: (e, t, 0)),
                 pl.BlockSpec((kb, D), lambda e, t, k: (k, 0)),
                 pl.BlockSpec((128, D), lambda e, t, k: (0, 0))]
    out = pl.pallas_call(
        body, name="d_hx", grid=grid, in_specs=in_specs + [ANY] * n_si,
        out_specs=(pl.BlockSpec((1, tm, D), lambda e, t, k: (e, t, 0)),) + (ANY,) * n_so,
        out_shape=(jax.ShapeDtypeStruct((2, R, D), BF),) + tuple(side.out_shapes if side else ()),
        scratch_shapes=[pltpu.VMEM((tm, D), F32)] + (side.scratch() if side else []),
        compiler_params=_params(("arbitrary",) * 3 if side else ("parallel", "parallel", "arbitrary")),
    )(*pieces, ddt, w_inT, w_dtT, *(side.inputs if side else ()))
    return out if side else out[0]


def _adaln_fwd(c16, w_adaT_bf, b_ada):
    def body(c_ref, w_ref, b_ref, o_ref):
        cc = c_ref[...]
        o_ref[...] = _dot_nt(cc * _sigmoid(cc), w_ref[...]) + b_ref[...]

    return pl.pallas_call(body, name="adaln_fwd", out_shape=jax.ShapeDtypeStruct((16, 3 * D), F32),
                          compiler_params=_params())(c16, w_adaT_bf, b_ada)


def _adaln_bwd(acc_n, acc_f, mod16, c16, norm_pre, w_adaT_bf):
    def body(an_ref, af_ref, mod_ref, c_ref, np_ref, wt_ref, dw_ref, db_ref, sm_ref, dmod):
        npre = np_ref[...]
        dmod[...] = jnp.zeros_like(dmod)
        dnp = jnp.zeros((1, D), F32)
        dshift_c = jnp.zeros((1, D), F32)
        dgpre_c = jnp.zeros((1, D), F32)
        scale_c = mod_ref[2:3, D:2 * D]
        for e in range(2):
            dg_x, ds_x = an_ref[e, 0, 0:1, :], an_ref[e, 0, 1:2, :]
            dg_c, ds_c = an_ref[e, 1, 0:1, :], an_ref[e, 1, 1:2, :]
            dmod[e:e + 1, 0:D] = ds_x
            dmod[e:e + 1, D:2 * D] = dg_x * npre
            dmod[e:e + 1, 2 * D:3 * D] = af_ref[e, 0:1, :]
            dnp = dnp + dg_x * (1.0 + mod_ref[e:e + 1, D:2 * D]) + dg_c * (1.0 + scale_c)
            dshift_c = dshift_c + ds_c
            dgpre_c = dgpre_c + dg_c
        dmod[2:3, 0:D] = dshift_c
        dmod[2:3, D:2 * D] = dgpre_c * npre
        dm = dmod[...]
        cc = c_ref[...]
        sg = _sigmoid(cc)
        dw_ref[...] = _dot_tn(dm, cc * sg)
        db_ref[...] = jnp.zeros_like(db_ref)
        db_ref[0:1, :] = jnp.sum(dm, axis=0, keepdims=True)
        dsilu = sg * (1.0 + cc * (1.0 - sg))
        dcs = _dot(dm, wt_ref[...]) * dsilu
        sm_ref[...] = jnp.zeros_like(sm_ref)
        sm_ref[0:1, :] = dnp
        sm_ref[1:2, :] = dcs[2:3, :]

    return pl.pallas_call(
        body, name="adaln_bwd",
        out_shape=(jax.ShapeDtypeStruct((3 * D, D), F32), jax.ShapeDtypeStruct((16, 3 * D), F32),
                   jax.ShapeDtypeStruct((8, D), F32)),
        scratch_shapes=[pltpu.VMEM((16, 3 * D), F32)],
        compiler_params=_params())(acc_n, acc_f, mod16, c16, norm_pre, w_adaT_bf)


def _row_specs(L):
    nx = L // ROW_TILE
    return (pl.BlockSpec((1, ROW_TILE, D), lambda e, t: (e, jnp.minimum(t, nx - 1), 0)),
            pl.BlockSpec((1, ROW_TILE, D), lambda e, t: (e, jnp.maximum(t - nx, 0), 0)))


def _norm_mod_fwd(x, ctx, tab):
    L = x.shape[1]
    R = L + ctx.shape[1]
    nx = L // ROW_TILE

    def body(x_ref, c_ref, t_ref, o_ref):
        t = t_ref[0, 0]
        gain, shift = t[0:1], t[1:2]

        def run(src):
            for r0 in range(0, ROW_TILE, 32):
                x = src[0, pl.ds(r0, 32), :]
                r = lax.rsqrt(jnp.mean(x * x, axis=-1, keepdims=True) + EPS)
                o_ref[0, pl.ds(r0, 32), :] = (x * r * gain + shift).astype(BF)

        @pl.when(pl.program_id(1) < nx)
        def _():
            run(x_ref)

        @pl.when(pl.program_id(1) >= nx)
        def _():
            run(c_ref)

    return pl.pallas_call(
        body, name="norm_mod_fwd", grid=(2, R // ROW_TILE),
        in_specs=[*_row_specs(L), pl.BlockSpec((1, 1, 8, D), lambda e, t: (e, t // nx, 0, 0))],
        out_specs=pl.BlockSpec((1, ROW_TILE, D), lambda e, t: (e, t, 0)),
        out_shape=jax.ShapeDtypeStruct((2, R, D), BF),
        compiler_params=_params(("parallel", "parallel")),
    )(x, ctx, tab)


def _norm_mod_bwd(dh, x, ctx, tab, dxo):
    L = x.shape[1]
    R = L + ctx.shape[1]
    nx = L // ROW_TILE

    def body(dh_ref, x_ref, c_ref, t_ref, dxo_ref, gx_ref, acc_ref):
        t = pl.program_id(1)
        x = jnp.where(t < nx, x_ref[0], c_ref[0])
        r = lax.rsqrt(jnp.mean(x * x, axis=-1, keepdims=True) + EPS)
        xn = x * r
        dh = dh_ref[0].astype(F32)

        @pl.when((t == 0) | (t == nx))
        def _():
            acc_ref[...] = jnp.zeros_like(acc_ref)

        acc_ref[0, 0, 0:1, :] += jnp.sum(dh * xn, axis=0, keepdims=True)
        acc_ref[0, 0, 1:2, :] += jnp.sum(dh, axis=0, keepdims=True)

        @pl.when(t < nx)
        def _():
            dxn = dh * t_ref[0, 0][0:1]
            dx = r * (dxn - xn * jnp.mean(dxn * xn, axis=-1, keepdims=True))
            gx_ref[0] = dxo_ref[0].astype(F32) + dx

    xspec, cspec = _row_specs(L)
    return pl.pallas_call(
        body, name="norm_mod_bwd", grid=(2, R // ROW_TILE),
        in_specs=[pl.BlockSpec((1, ROW_TILE, D), lambda e, t: (e, t, 0)), xspec, cspec,
                  pl.BlockSpec((1, 1, 8, D), lambda e, t: (e, t // nx, 0, 0)), xspec],
        out_specs=(xspec, pl.BlockSpec((1, 1, 8, D), lambda e, t: (e, t // nx, 0, 0))),
        out_shape=(jax.ShapeDtypeStruct((2, L, D), F32), jax.ShapeDtypeStruct((2, 2, 8, D), F32)),
        compiler_params=_params(("parallel", "arbitrary")),
    )(dh, x, ctx, tab, dxo)


POOL_TILE = 256


def _pool_tables(L):
    rows = L // GRID_W
    mats = np.zeros((4, POOL_TILE, POOL_TILE), np.float32)
    inv = np.zeros((4, L, 1), np.float32)
    for gi, k in enumerate(POOL_WINDOWS):
        lo, hi = k // 2, k - 1 - k // 2
        m = np.zeros((GRID_W, GRID_W), np.float32)
        for t in range(GRID_W):
            m[t, max(t - lo, 0):min(t + hi, GRID_W - 1) + 1] = 1.0
        for b in range(POOL_TILE // GRID_W):
            mats[gi, b * GRID_W:(b + 1) * GRID_W, b * GRID_W:(b + 1) * GRID_W] = m
        cnt_c = m.sum(1)
        cnt_r = np.array([min(r + hi, rows - 1) - max(r - lo, 0) + 1 for r in range(rows)], np.float32)
        inv[gi, :, 0] = (1.0 / (cnt_r[:, None] * cnt_c[None, :])).reshape(-1)
    matsT = np.ascontiguousarray(np.transpose(mats, (0, 2, 1)))
    return (jnp.asarray(mats, BF), jnp.asarray(matsT, BF), jnp.asarray(inv))


def _pool_cols(get_tile, mat, cs_ref, L, n):
    def step(i, carry):
        off = pl.multiple_of(i * POOL_TILE, POOL_TILE)
        t = get_tile(off)
        cs_ref[pl.ds(GRID_W + off, POOL_TILE), :] = (jnp.dot(mat, t.astype(BF), preferred_element_type=F32) if n == 1
                                                     else _dot_sr(mat, t.astype(F32), n))
        return carry

    lax.fori_loop(0, L // POOL_TILE, step, 0)
    cs_ref[pl.ds(0, GRID_W), :] = jnp.zeros((GRID_W, PGW), F32)

    def prefix(r, carry):
        o = pl.multiple_of(r * GRID_W, GRID_W)
        cs_ref[pl.ds(o + GRID_W, GRID_W), :] = cs_ref[pl.ds(o + GRID_W, GRID_W), :] + cs_ref[pl.ds(o, GRID_W), :]
        return carry

    lax.fori_loop(0, L // GRID_W, prefix, 0)


def _pool_rows(cs_ref, off, below, above, L):
    rows = L // GRID_W
    r0 = off // GRID_W
    parts = []
    for i in range(POOL_TILE // GRID_W):
        hi = pl.multiple_of(jnp.minimum(r0 + i + above + 1, rows) * GRID_W, GRID_W)
        lo = pl.multiple_of(jnp.maximum(r0 + i - below, 0) * GRID_W, GRID_W)
        parts.append(cs_ref[pl.ds(hi, GRID_W), :] - cs_ref[pl.ds(lo, GRID_W), :])
    return jnp.concatenate(parts, axis=0)


def _pool_fwd(proj3, pool_w_bf, pool_scale, tables, L):
    mats, _, inv = tables
    nt = L // POOL_TILE

    def body(v_ref, z_ref, pw_ref, ps_ref, m_ref, inv_ref, o_ref, cs_ref):
        _pool_cols(lambda off: v_ref[0, pl.ds(off, POOL_TILE), :], m_ref[0], cs_ref, L, 1)
        half = lax.shift_left(1, pl.program_id(1))

        def step(i, carry):
            off = pl.multiple_of(i * POOL_TILE, POOL_TILE)
            rows = pl.ds(off, POOL_TILE)
            v = v_ref[0, rows, :].astype(F32)
            diff = _pool_rows(cs_ref, off, half, half - 1, L) * inv_ref[0, rows, :] - v
            yp = _dot(diff, pw_ref[0])
            z = z_ref[0, rows, :].astype(F32)
            o_ref[0, rows, :] = (yp * ps_ref[...] * (z * _sigmoid(z))).astype(BF)
            return carry

        lax.fori_loop(0, nt, step, 0)

    return pl.pallas_call(
        body, name="pool_fwd", grid=(2, 4),
        in_specs=[pl.BlockSpec((1, L, PGW), lambda e, g: (e, 0, g)),
                  pl.BlockSpec((1, L, PGW), lambda e, g: (e, 0, 4 + g)),
                  pl.BlockSpec((1, PGW, PGW), lambda e, g: (g, 0, 0)),
                  pl.BlockSpec((1, PGW), lambda e, g: (0, g)),
                  pl.BlockSpec((1, POOL_TILE, POOL_TILE), lambda e, g: (g, 0, 0)),
                  pl.BlockSpec((1, L, 1), lambda e, g: (g, 0, 0))],
        out_specs=pl.BlockSpec((1, L, PGW), lambda e, g: (e, 0, g)),
        out_shape=jax.ShapeDtypeStruct((2, L, D), BF),
        scratch_shapes=[pltpu.VMEM((L + GRID_W, PGW), F32)],
        compiler_params=_params(("parallel", "parallel")),
    )(proj3, proj3, pool_w_bf, pool_scale, mats, inv)


def _pool_bwd(proj3, d_ypool, pool_w_bf, pool_wT_bf, pool_scale, tables, L):
    mats, matsT, inv = tables
    nt = L // POOL_TILE
    R = proj3.shape[1]

    def body(v_ref, z_ref, dy_ref, pw_ref, pwt_ref, ps_ref, m_ref, mt_ref, inv_ref,
             dv_ref, dz_ref, dpw_ref, acc_ref, cs_ref, dd_ref):
        e = pl.program_id(1)

        @pl.when(e == 0)
        def _():
            dpw_ref[...] = jnp.zeros_like(dpw_ref)
            acc_ref[...] = jnp.zeros_like(acc_ref)

        _pool_cols(lambda off: v_ref[0, pl.ds(off, POOL_TILE), :], m_ref[0], cs_ref, L, 1)
        half = lax.shift_left(1, pl.program_id(0))
        ps = ps_ref[...]

        def step(i, carry):
            off = pl.multiple_of(i * POOL_TILE, POOL_TILE)
            rows = pl.ds(off, POOL_TILE)
            v = v_ref[0, rows, :].astype(F32)
            diff = _pool_rows(cs_ref, off, half, half - 1, L) * inv_ref[0, rows, :] - v
            yp = _dot(diff, pw_ref[0])
            z = z_ref[0, rows, :].astype(F32)
            sg = _sigmoid(z)
            sz = z * sg
            dy = dy_ref[0, rows, :].astype(F32)
            dz_ref[0, rows, :] = (dy * yp * ps * (sg * (1.0 + z * (1.0 - sg)))).astype(BF)
            dys = dy * sz
            acc_ref[0, 0:1, :] += jnp.sum(dys * yp, axis=0, keepdims=True)
            dyp = dys * ps
            dpw_ref[0] += _dot_tn(diff, dyp)
            dd_ref[rows, :] = _dot(dyp, pwt_ref[0])
            return carry

        lax.fori_loop(0, nt, step, 0)
        _pool_cols(lambda off: dd_ref[pl.ds(off, POOL_TILE), :] * inv_ref[0, pl.ds(off, POOL_TILE), :],
                   mt_ref[0], cs_ref, L, 1)

        def step2(i, carry):
            off = pl.multiple_of(i * POOL_TILE, POOL_TILE)
            rows = pl.ds(off, POOL_TILE)
            dv_ref[0, rows, :] = (_pool_rows(cs_ref, off, half - 1, half, L) - dd_ref[rows, :]).astype(BF)
            return carry

        lax.fori_loop(0, nt, step2, 0)
        dv_ref[0, pl.ds(L, R - L), :] = jnp.zeros((R - L, PGW), BF)
        dz_ref[0, pl.ds(L, R - L), :] = jnp.zeros((R - L, PGW), BF)

    return pl.pallas_call(
        body, name="pool_bwd", grid=(4, 2),
        in_specs=[pl.BlockSpec((1, L, PGW), lambda g, e: (e, 0, g)),
                  pl.BlockSpec((1, L, PGW), lambda g, e: (e, 0, 4 + g)),
                  pl.BlockSpec((1, L, PGW), lambda g, e: (e, 0, g)),
                  pl.BlockSpec((1, PGW, PGW), lambda g, e: (g, 0, 0)),
                  pl.BlockSpec((1, PGW, PGW), lambda g, e: (g, 0, 0)),
                  pl.BlockSpec((1, PGW), lambda g, e: (0, g)),
                  pl.BlockSpec((1, POOL_TILE, POOL_TILE), lambda g, e: (g, 0, 0)),
                  pl.BlockSpec((1, POOL_TILE, POOL_TILE), lambda g, e: (g, 0, 0)),
                  pl.BlockSpec((1, L, 1), lambda g, e: (g, 0, 0))],
        out_specs=(pl.BlockSpec((1, R, PGW), lambda g, e: (e, 0, g)),
                   pl.BlockSpec((1, R, PGW), lambda g, e: (e, 0, g)),
                   pl.BlockSpec((1, PGW, PGW), lambda g, e: (g, 0, 0)),
                   pl.BlockSpec((1, 8, PGW), lambda g, e: (g, 0, 0))),
        out_shape=(jax.ShapeDtypeStruct((2, R, D), BF), jax.ShapeDtypeStruct((2, R, D), BF),
                   jax.ShapeDtypeStruct((4, PGW, PGW), F32), jax.ShapeDtypeStruct((4, 8, PGW), F32)),
        scratch_shapes=[pltpu.VMEM((L + GRID_W, PGW), F32), pltpu.VMEM((L, PGW), F32)],
        compiler_params=_params(("parallel", "arbitrary")),
    )(proj3, proj3, d_ypool, pool_w_bf, pool_wT_bf, pool_scale, mats, matsT, inv)


CONV_BLOCK = 128


CONV_CHUNK = 64
CONV_HALO = 8


def _halo_buf_init(buf, val, R):
    z = jnp.zeros((CONV_HALO, CONV_BLOCK), F32)
    buf[pl.ds(0, CONV_HALO), :] = z
    buf[pl.ds(CONV_HALO + R, CONV_HALO), :] = z
    if val is not None:
        buf[pl.ds(CONV_HALO, R), :] = val


def _chunk_taps(buf, start, offs, L):
    n = CONV_CHUNK + 2 * CONV_HALO
    ext = buf[pl.ds(start, n), :]
    out = []
    for off in offs:
        if off == 0:
            out.append(ext[CONV_HALO:CONV_HALO + CONV_CHUNK])
            continue
        r = pltpu.roll(ext, (-off) % n, 0)[CONV_HALO:CONV_HALO + CONV_CHUNK]
        lo, hi = (start, start + CONV_CHUNK - 1 + off) if off > 0 else (start + off, start + CONV_CHUNK - 1)
        if lo < L <= hi:
            t = start + lax.broadcasted_iota(jnp.int32, (CONV_CHUNK, 1), 0)
            r = jnp.where((t < L) == (t + off < L), r, 0.0)
        out.append(r)
    return out


def _fold8(x):
    return sum(x[i * 8:(i + 1) * 8] for i in range(CONV_CHUNK // 8))


def _conv_fwd(proj3, conv_w, conv_b, L):
    _, R, _ = proj3.shape
    cb0 = OFF_XBC // CONV_BLOCK

    def body(u_ref, w_ref, b_ref, o_ref, ubuf):
        _halo_buf_init(ubuf, u_ref[0].astype(F32), R)
        w = w_ref[...]
        b = b_ref[...]
        for start in range(0, R, CONV_CHUNK):
            taps = _chunk_taps(ubuf, start, (-2, -1, 0, 1), L)
            pre = b + sum(taps[k] * w[k:k + 1, :] for k in range(4))
            o_ref[0, pl.ds(start, CONV_CHUNK), :] = (pre * _sigmoid(pre)).astype(BF)

    return pl.pallas_call(
        body, name="conv_fwd", grid=(2, CONV_DIM // CONV_BLOCK),
        in_specs=[pl.BlockSpec((1, R, CONV_BLOCK), lambda e, j: (e, 0, cb0 + j)),
                  pl.BlockSpec((4, CONV_BLOCK), lambda e, j: (0, j)),
                  pl.BlockSpec((1, CONV_BLOCK), lambda e, j: (0, j))],
        out_specs=pl.BlockSpec((1, R, CONV_BLOCK), lambda e, j: (e, 0, j)),
        out_shape=jax.ShapeDtypeStruct((2, R, CONV_DIM), BF),
        scratch_shapes=[pltpu.VMEM((R + 2 * CONV_HALO, CONV_BLOCK), F32)],
        compiler_params=_params(("parallel", "parallel")),
    )(proj3, conv_w, conv_b)


def _conv_bwd(proj3, addends, scales, col0, ncols, in_maps, conv_w, conv_b, L, name):
    _, R, _ = proj3.shape
    cb0 = (OFF_XBC + col0) // CONV_BLOCK
    wb0 = col0 // CONV_BLOCK
    na = len(addends)
    scaled = [i for i in range(na) if scales[i] is not None]

    def body(*refs):
        u_ref, w_ref, b_ref = refs[0], refs[1], refs[2]
        a_refs = refs[3:3 + na]
        s_refs = dict(zip(scaled, refs[3 + na:3 + na + len(scaled)]))
        o_ref, acc_ref, ubuf, dbuf = refs[3 + na + len(scaled):]
        _halo_buf_init(ubuf, u_ref[0].astype(F32), R)
        _halo_buf_init(dbuf, None, R)
        w = w_ref[...]
        b = b_ref[...]
        scl = {i: s_refs[i][...] for i in scaled}
        sums = [jnp.zeros((8, CONV_BLOCK), F32) for _ in range(5)]
        for start in range(0, R, CONV_CHUNK):
            rows = pl.ds(start, CONV_CHUNK)
            taps = _chunk_taps(ubuf, start, (-2, -1, 0, 1), L)
            pre = b + sum(taps[k] * w[k:k + 1, :] for k in range(4))
            sg = _sigmoid(pre)
            dxbc = None
            for i, a in enumerate(a_refs):
                t = a[0, rows, :].astype(F32)
                t = t * scl[i] if i in scl else t
                dxbc = t if dxbc is None else dxbc + t
            dpre = dxbc * (sg * (1.0 + pre * (1.0 - sg)))
            dbuf[pl.ds(start + CONV_HALO, CONV_CHUNK), :] = dpre
            for k in range(4):
                sums[k] = sums[k] + _fold8(dpre * taps[k])
            sums[4] = sums[4] + _fold8(dpre)
        acc_ref[...] = jnp.zeros_like(acc_ref)
        for k in range(5):
            acc_ref[0, k:k + 1, :] = jnp.sum(sums[k], axis=0, keepdims=True)
        for start in range(0, R, CONV_CHUNK):
            d = _chunk_taps(dbuf, start, (2, 1, 0, -1), L)
            o_ref[0, pl.ds(start, CONV_CHUNK), :] = sum(d[k] * w[k:k + 1, :] for k in range(4)).astype(BF)

    in_specs = [pl.BlockSpec((1, R, CONV_BLOCK), lambda e, j: (e, 0, cb0 + j)),
                pl.BlockSpec((4, CONV_BLOCK), lambda e, j: (0, wb0 + j)),
                pl.BlockSpec((1, CONV_BLOCK), lambda e, j: (0, wb0 + j))]
    for m in in_maps:
        in_specs.append(pl.BlockSpec((1, R, CONV_BLOCK), functools.partial(lambda e, j, m: (e, 0, m(j)), m=m)))
    for i in scaled:
        in_specs.append(pl.BlockSpec((1, CONV_BLOCK), functools.partial(lambda e, j, m: (0, m(j)), m=in_maps[i])))
    return pl.pallas_call(
        body, name=name, grid=(2, ncols // CONV_BLOCK),
        in_specs=in_specs,
        out_specs=(pl.BlockSpec((1, R, CONV_BLOCK), lambda e, j: (e, 0, j)),
                   pl.BlockSpec((1, 8, CONV_BLOCK), lambda e, j: (e, 0, j))),
        out_shape=(jax.ShapeDtypeStruct((2, R, ncols), BF), jax.ShapeDtypeStruct((2, 8, ncols), F32)),
        scratch_shapes=[pltpu.VMEM((R + 2 * CONV_HALO, CONV_BLOCK), F32)] * 2,
        compiler_params=_params(("parallel", "parallel")),
    )(proj3, conv_w, conv_b, *addends, *[scales[i] for i in scaled])


def _softplus(x):
    e = jnp.exp(-jnp.abs(x))
    u = 1.0 + e
    return jnp.maximum(x, 0.0) + jnp.where(u == 1.0, e, e * jnp.log(u) / (u - 1.0))


def _to_local_mat(g, transpose=False):
    r = lax.broadcasted_iota(jnp.int32, (128, 128), 1 if transpose else 0)
    c = lax.broadcasted_iota(jnp.int32, (128, 128), 0 if transpose else 1)
    return ((c < 2 * HPG) & (r == jnp.right_shift(c, 3) * (NG * HPG) + g * HPG + (c & (HPG - 1)))).astype(BF)


def _dt_fwd(dt_raw, bias128):
    _, R, _ = dt_raw.shape

    def body(x_ref, b_ref, o_ref):
        dt = _softplus(x_ref[0] + b_ref[...])
        for g in range(NG):
            o_ref[0, g] = _dot_sl(dt, _to_local_mat(g))

    tr = R // 4
    return pl.pallas_call(
        body, name="dt_fwd", grid=(2, 4),
        in_specs=[pl.BlockSpec((1, tr, 128), lambda e, t: (e, t, 0)), pl.BlockSpec((1, 128), lambda e, t: (0, 0))],
        out_specs=pl.BlockSpec((1, NG, tr, 128), lambda e, t: (e, 0, t, 0)),
        out_shape=jax.ShapeDtypeStruct((2, NG, R, 128), F32),
        compiler_params=_params(("parallel", "parallel")),
    )(dt_raw, bias128)


def _dt_bwd(dt_raw, bias128, ddt_f, ddt_b):
    _, R, _ = dt_raw.shape

    def body(x_ref, b_ref, f_ref, g_ref, o_ref, acc_ref):
        ddt = sum(_dot_sl(f_ref[0, g] + g_ref[0, g], _to_local_mat(g, transpose=True)) for g in range(NG))
        d = ddt * _sigmoid(x_ref[0] + b_ref[...])
        o_ref[0] = d.astype(BF)

        @pl.when(pl.program_id(1) == 0)
        def _():
            acc_ref[...] = jnp.zeros_like(acc_ref)

        acc_ref[0, 0:1, :] += jnp.sum(d, axis=0, keepdims=True)

    tr = R // 4
    blk = pl.BlockSpec((1, tr, 128), lambda e, t: (e, t, 0))
    loc = pl.BlockSpec((1, NG, tr, 128), lambda e, t: (e, 0, t, 0))
    return pl.pallas_call(
        body, name="dt_bwd", grid=(2, 4),
        in_specs=[blk, pl.BlockSpec((1, 128), lambda e, t: (0, 0)), loc, loc],
        out_specs=(blk, pl.BlockSpec((1, 8, 128), lambda e, t: (e, 0, 0))),
        out_shape=(jax.ShapeDtypeStruct(dt_raw.shape, BF), jax.ShapeDtypeStruct((2, 8, 128), F32)),
        compiler_params=_params(("parallel", "arbitrary")),
    )(dt_raw, bias128, ddt_f, ddt_b)


GPS = 4


def _tri(d):
    i = lax.broadcasted_iota(jnp.int32, (Q, Q), 0)
    j = lax.broadcasted_iota(jnp.int32, (Q, Q), 1)
    return (i >= j) if d == 0 else (i <= j)


def _expand_mat(d):
    r = lax.broadcasted_iota(jnp.int32, (128, GWID), 0)
    c = lax.broadcasted_iota(jnp.int32, (128, GWID), 1)
    return (r == d * HPG + jnp.right_shift(c, 6)).astype(BF)


def _reduce_mat(d):
    r = lax.broadcasted_iota(jnp.int32, (GWID, 128), 0)
    c = lax.broadcasted_iota(jnp.int32, (GWID, 128), 1)
    return (c == d * HPG + jnp.right_shift(r, 6)).astype(BF)


def _ssd_chunk(d, dt, A, xs, B, C):
    mask = _tri(d)
    T = mask.astype(BF)
    Tt = _tri(1 - d).astype(BF)
    a = dt * A
    acs = _dot_sr(T, a)
    E = _expand_mat(d)
    dt_e = _dot_sl(dt, E, 2)
    acs_e = _dot_sl(acs, E, 2)
    alast_e = acs_e[Q - 1:Q, :] if d == 0 else acs_e[0:1, :]
    return dict(mask=mask, T=T, Tt=Tt, acs=acs, acsT=acs.T, dt_e=dt_e, acs_e=acs_e, lam=jnp.exp(acs_e),
                w=jnp.exp(alast_e - acs_e), decay=jnp.exp(alast_e), xt=xs * dt_e, CB=_dot_nt(C, B))


def _head_decay(q, d, hh):
    col = q["acs"][:, d * HPG + hh:d * HPG + hh + 1]
    row = q["acsT"][d * HPG + hh:d * HPG + hh + 1, :]
    return jnp.exp(jnp.where(q["mask"], col - row, -jnp.inf))


def _chunk_maps(NX, NS):
    cf = lambda s: lax.rem(s + NX, NS)
    cb = lambda s: NS - 1 - s
    return cf, cb


def _ssd_fwd(xbc, dt_loc, a_loc, L):
    _, R, _ = xbc.shape
    NX, NS = L // Q, R // Q
    cf, cb = _chunk_maps(NX, NS)

    def body(xs_f, b_f, c_f, dt_f, xs_b, b_b, c_b, dt_b, a_ref, y_f, hs_f, y_b, hs_b, hT):
        @pl.when(pl.program_id(2) == 0)
        def _():
            hT[...] = jnp.zeros_like(hT)

        lane = lax.broadcasted_iota(jnp.int32, (Q, 128), 1)
        for d, (xs_ref, b_ref, c_ref, dt_ref, y_ref, hs_ref) in enumerate(
                ((xs_f, b_f, c_f, dt_f, y_f, hs_f), (xs_b, b_b, c_b, dt_b, y_b, hs_b))):
            for gi in range(GPS):
                cols = slice(gi * GWID, (gi + 1) * GWID)
                xs = xs_ref[0, :, cols].astype(F32)
                B, C = b_ref[0, :, gi * NST:(gi + 1) * NST], c_ref[0, :, gi * NST:(gi + 1) * NST]
                q = _ssd_chunk(d, dt_ref[0, gi], a_ref[gi, 0:1, :], xs, B, C)
                h = hT[d, :, cols]
                hb = h.astype(BF)
                hs_ref[0, 0, :, cols] = hb
                parts = []
                for pr in range(HPG // 2):
                    xp = q["xt"][:, pr * 128:(pr + 1) * 128]
                    xst = jnp.concatenate([jnp.where(lane < HEAD, xp, 0.0), jnp.where(lane < HEAD, 0.0, xp)], axis=0)
                    mst = jnp.concatenate([(q["CB"] * _head_decay(q, d, 2 * pr)).astype(BF),
                                           (q["CB"] * _head_decay(q, d, 2 * pr + 1)).astype(BF)], axis=1)
                    parts.append(_dot(mst, xst))
                y_ref[0, :, cols] = jnp.concatenate(parts, axis=1) + _dot(C, hb) * q["lam"]
                hT[d, :, cols] = q["decay"] * h + _dot_tn(B, q["xt"] * q["w"])

    def spec(shape, imap):
        return pl.BlockSpec(shape, imap)

    bc0 = DIN // (GPS * NST)

    def ins(c):
        return [spec((1, Q, GPS * GWID), lambda e, g, s: (e, c(s), g)),
                spec((1, Q, GPS * NST), lambda e, g, s: (e, c(s), bc0 + g)),
                spec((1, Q, GPS * NST), lambda e, g, s: (e, c(s), bc0 + NG // GPS + g)),
                spec((1, GPS, Q, 128), lambda e, g, s: (e, g, c(s), 0))]

    def outs(c):
        return [spec((1, Q, GPS * GWID), lambda e, g, s: (e, c(s), g)),
                spec((1, 1, NST, GPS * GWID), lambda e, g, s: (e, c(s), 0, g))]

    yshape = jax.ShapeDtypeStruct((2, R, DIN), F32)
    hshape = jax.ShapeDtypeStruct((2, NS, NST, DIN), BF)
    return pl.pallas_call(
        body, name="ssd_fwd", grid=(2, NG // GPS, NS),
        in_specs=ins(cf) + ins(cb) + [spec((GPS, 8, 128), lambda e, g, s: (g, 0, 0))],
        out_specs=tuple(outs(cf) + outs(cb)),
        out_shape=(yshape, hshape, yshape, hshape),
        scratch_shapes=[pltpu.VMEM((2, NST, GPS * GWID), F32)],
        compiler_params=_params(("parallel", "parallel", "arbitrary")),
    )(xbc, xbc, xbc, dt_loc, xbc, xbc, xbc, dt_loc, a_loc)


def _ssd_bwd(xbc, dt_loc, a_loc, hs_f, hs_b, y_f, y_b, dy, L):
    _, R, _ = xbc.shape
    NX, NS = L // Q, R // Q
    cf0, cb0 = _chunk_maps(NX, NS)
    cf = lambda sp: cf0(NS - 1 - sp)
    cb = lambda sp: cb0(NS - 1 - sp)

    def body(xs_f, b_f, c_f, dt_f, hs_f_, dy_f, y_f_, xs_b, b_b, c_b, dt_b, hs_b_, dy_b, y_b_, a_ref,
             dxs_f, dbc_f, ddt_f, dxs_b, dbc_b, ddt_b, da_ref, dhT):
        @pl.when(pl.program_id(2) == 0)
        def _():
            dhT[...] = jnp.zeros_like(dhT)
            da_ref[...] = jnp.zeros_like(da_ref)

        lane = lax.broadcasted_iota(jnp.int32, (Q, 128), 1)
        row = lax.broadcasted_iota(jnp.int32, (Q, 128), 0)

        def one_chain(d, gi, xs_ref, b_ref, c_ref, dt_ref, hs_ref, dy_ref, y_ref, dxs_ref, dbc_ref, ddt_ref):
            cols = slice(gi * GWID, (gi + 1) * GWID)
            A = a_ref[gi, 0:1, :]
            xs, dt = xs_ref[0, :, cols].astype(F32), dt_ref[0, gi]
            B, C = b_ref[0, :, gi * NST:(gi + 1) * NST], c_ref[0, :, gi * NST:(gi + 1) * NST]
            q = _ssd_chunk(d, dt, A, xs, B, C)
            xt, lam, w, decay = q["xt"], q["lam"], q["w"], q["decay"]
            H = hs_ref[0, 0, :, cols]
            dyv = dy_ref[0, :, cols].astype(F32)
            dh = dhT[d, :, cols]
            dZ = dyv * lam
            dC = _dot_nt(dZ, H)
            dH = _dot_tn(C, dZ)
            U = _dot(B, dh)
            xw = xt * w
            dxt = U * w
            dalast_e = (jnp.sum(U * xw, axis=0, keepdims=True)
                        + decay * jnp.sum(dh * H.astype(F32), axis=0, keepdims=True))
            dB = _dot_nt(xw, dh)
            dCB = jnp.zeros((Q, Q), F32)
            dxt_parts = []
            for pr in range(HPG // 2):
                xp = xt[:, pr * 128:(pr + 1) * 128]
                dyp = dyv[:, pr * 128:(pr + 1) * 128]
                L0, L1 = _head_decay(q, d, 2 * pr), _head_decay(q, d, 2 * pr + 1)
                dyst = jnp.concatenate([jnp.where(lane < HEAD, dyp, 0.0), jnp.where(lane < HEAD, 0.0, dyp)], axis=0)
                mst = jnp.concatenate([(q["CB"] * L0).astype(BF), (q["CB"] * L1).astype(BF)], axis=0)
                dxt_parts.append(_dot_tn(mst, dyst))
                dmst = _dot_nt(dyst, xp)
                dCB = dCB + dmst[:Q] * L0 + dmst[Q:] * L1
            dxt_diag = jnp.concatenate(dxt_parts, axis=1)
            dC = dC + _dot(dCB, B)
            dB = dB + _dot_tn(dCB, C)
            Rm = _reduce_mat(d)
            dacs = _dot_sl(dyv * y_ref[0, :, cols] - xt.astype(BF).astype(F32) * dxt_diag - U * xw, Rm, 2)
            dxt = dxt + dxt_diag
            dal = _dot_sl(jnp.broadcast_to(dalast_e, (8, GWID)), Rm, 2)[0:1, :]
            dacs = dacs + jnp.where(row == (Q - 1 if d == 0 else 0), dal, 0.0)
            da = _dot_sr(q["Tt"], dacs, 2)
            ddt_ref[0, gi] = da * A + _dot_sl(dxt * xs, Rm, 2)
            da_ref[0, gi, 0:1, :] += jnp.sum(da * dt, axis=0, keepdims=True)
            dxs_ref[0, :, cols] = (dxt * q["dt_e"]).astype(BF)
            dbc_ref[0, :, gi * 2 * NST:(gi + 1) * 2 * NST] = jnp.concatenate([dB, dC], axis=1).astype(BF)
            dhT[d, :, cols] = decay * dh + dH

        for gi in range(GPS):
            one_chain(0, gi, xs_f, b_f, c_f, dt_f, hs_f_, dy_f, y_f_, dxs_f, dbc_f, ddt_f)
            one_chain(1, gi, xs_b, b_b, c_b, dt_b, hs_b_, dy_b, y_b_, dxs_b, dbc_b, ddt_b)

    def spec(shape, imap):
        return pl.BlockSpec(shape, imap)

    bc0 = DIN // (GPS * NST)

    def ins(c):
        return [spec((1, Q, GPS * GWID), lambda e, g, s: (e, c(s), g)),
                spec((1, Q, GPS * NST), lambda e, g, s: (e, c(s), bc0 + g)),
                spec((1, Q, GPS * NST), lambda e, g, s: (e, c(s), bc0 + NG // GPS + g)),
                spec((1, GPS, Q, 128), lambda e, g, s: (e, g, c(s), 0)),
                spec((1, 1, NST, GPS * GWID), lambda e, g, s: (e, c(s), 0, g)),
                spec((1, Q, GPS * GWID), lambda e, g, s: (e, c(s), g)),
                spec((1, Q, GPS * GWID), lambda e, g, s: (e, c(s), g))]

    def outs(c):
        return [spec((1, Q, GPS * GWID), lambda e, g, s: (e, c(s), g)),
                spec((1, Q, GPS * 2 * NST), lambda e, g, s: (e, c(s), g)),
                spec((1, GPS, Q, 128), lambda e, g, s: (e, g, c(s), 0))]

    s_xs = jax.ShapeDtypeStruct((2, R, DIN), BF)
    s_bc = jax.ShapeDtypeStruct((2, R, 2 * NG * NST), BF)
    s_dt = jax.ShapeDtypeStruct((2, NG, R, 128), F32)
    return pl.pallas_call(
        body, name="ssd_bwd", grid=(2, NG // GPS, NS),
        in_specs=ins(cf) + ins(cb) + [spec((GPS, 8, 128), lambda e, g, s: (g, 0, 0))],
        out_specs=tuple(outs(cf) + outs(cb) + [spec((1, GPS, 8, 128), lambda e, g, s: (e, g, 0, 0))]),
        out_shape=(s_xs, s_bc, s_dt, s_xs, s_bc, s_dt, jax.ShapeDtypeStruct((2, NG, 8, 128), F32)),
        scratch_shapes=[pltpu.VMEM((2, NST, GPS * GWID), F32)],
        compiler_params=_params(("parallel", "parallel", "arbitrary")),
    )(xbc, xbc, xbc, dt_loc, hs_f, dy, y_f, xbc, xbc, xbc, dt_loc, hs_b, dy, y_b, a_loc)


def _ssd_post_fwd(y_f, y_b, xbc, proj3, dskip_e, ssd_norm, L):
    def body(yf_ref, yb_ref, xs_ref, z_ref, ds_ref, w_ref, o_ref, y2_ref):
        y2 = yf_ref[0] + yb_ref[0] + ds_ref[...] * xs_ref[0].astype(F32)
        y2_ref[0] = y2.astype(BF)
        z = z_ref[0].astype(F32)
        u = y2 * (z * _sigmoid(z))
        parts = []
        for g in range(NG):
            ug = u[:, g * GWID:(g + 1) * GWID]
            parts.append(ug * lax.rsqrt(jnp.mean(ug * ug, axis=-1, keepdims=True) + EPS))
        o_ref[0] = (jnp.concatenate(parts, axis=1) * w_ref[...]).astype(BF)

    blk = lambda c: pl.BlockSpec((1, X_TILE, DIN), lambda e, t: (e, t, c))
    vec = pl.BlockSpec((1, DIN), lambda e, t: (0, 0))
    return pl.pallas_call(
        body, name="ssd_post_fwd", grid=(2, L // X_TILE),
        in_specs=[blk(0), blk(0), blk(0), blk(1), vec, vec],
        out_specs=(blk(0), blk(0)),
        out_shape=(jax.ShapeDtypeStruct((2, L, DIN), BF), jax.ShapeDtypeStruct((2, L, DIN), BF)),
        compiler_params=_params(("parallel", "parallel")),
    )(y_f, y_b, xbc, proj3, dskip_e, ssd_norm)


def _ssd_post_bwd(d_yn, y2b, xbc, proj3, ssd_norm, L):
    _, R, _ = xbc.shape
    nx = L // ROW_TILE

    def body(dyn_ref, y2_ref, xs_ref, z_ref, w_ref, dy_ref, dz_ref, acc_ref):
        t = pl.program_id(1)

        @pl.when(t == 0)
        def _():
            acc_ref[...] = jnp.zeros_like(acc_ref)

        @pl.when(t >= nx)
        def _():
            dy_ref[...] = jnp.zeros_like(dy_ref)
            dz_ref[...] = jnp.zeros_like(dz_ref)

        @pl.when(t < nx)
        def _():
            xs = xs_ref[0].astype(F32)
            y2 = y2_ref[0].astype(F32)
            z = z_ref[0].astype(F32)
            sg = _sigmoid(z)
            sz = z * sg
            u = y2 * sz
            dyn = dyn_ref[0].astype(F32)
            dun = dyn * w_ref[...]
            uh_parts, du_parts = [], []
            for g in range(NG):
                sl = slice(g * GWID, (g + 1) * GWID)
                ug = u[:, sl]
                rg = lax.rsqrt(jnp.mean(ug * ug, axis=-1, keepdims=True) + EPS)
                uh = ug * rg
                dg = dun[:, sl]
                du_parts.append(rg * (dg - uh * jnp.mean(dg * uh, axis=-1, keepdims=True)))
                uh_parts.append(uh)
            du = jnp.concatenate(du_parts, axis=1)
            uh = jnp.concatenate(uh_parts, axis=1)
            dy2 = du * sz
            dy_ref[0] = dy2.astype(BF)
            dz_ref[0] = (du * y2 * (sg * (1.0 + z * (1.0 - sg)))).astype(BF)
            acc_ref[0, 0:1, :] += jnp.sum(dyn * uh, axis=0, keepdims=True)
            acc_ref[0, 1:2, :] += jnp.sum(dy2 * xs, axis=0, keepdims=True)

    xmap = lambda c: (lambda e, t: (e, jnp.minimum(t, nx - 1), c))
    blk = lambda c: pl.BlockSpec((1, ROW_TILE, DIN), xmap(c))
    oblk = pl.BlockSpec((1, ROW_TILE, DIN), lambda e, t: (e, t, 0))
    vec = pl.BlockSpec((1, DIN), lambda e, t: (0, 0))
    return pl.pallas_call(
        body, name="ssd_post_bwd", grid=(2, R // ROW_TILE),
        in_specs=[blk(0), blk(0), blk(0), blk(1), vec],
        out_specs=(oblk, oblk, pl.BlockSpec((1, 8, DIN), lambda e, t: (e, 0, 0))),
        out_shape=(jax.ShapeDtypeStruct((2, R, DIN), BF), jax.ShapeDtypeStruct((2, R, DIN), BF),
                   jax.ShapeDtypeStruct((2, 8, DIN), F32)),
        compiler_params=_params(("parallel", "arbitrary")),
    )(d_yn, y2b, xbc, proj3, ssd_norm)


def _merge_fwd(proj3, P, S, b_merge, L):
    def body(gp_ref, p_ref, s_ref, b_ref, o_ref):
        gt = _sigmoid(gp_ref[0].astype(F32) + b_ref[...])
        o_ref[0] = (gt[:, :D] * p_ref[0].astype(F32) + gt[:, D:] * s_ref[0].astype(F32)).astype(BF)

    blk = pl.BlockSpec((1, X_TILE, D), lambda e, t: (e, t, 0))
    return pl.pallas_call(
        body, name="merge_fwd", grid=(2, L // X_TILE),
        in_specs=[pl.BlockSpec((1, X_TILE, 2 * D), lambda e, t: (e, t, OFF_GATE // (2 * D))), blk, blk,
                  pl.BlockSpec((1, 2 * D), lambda e, t: (0, 0))],
        out_specs=blk, out_shape=jax.ShapeDtypeStruct((2, L, D), BF),
        compiler_params=_params(("parallel", "parallel")),
    )(proj3, P, S, b_merge)


def _merge_bwd(d_merged, proj3, P, S, b_merge, L):
    _, R, _ = proj3.shape
    nx = L // ROW_TILE

    def body(dm_ref, gp_ref, p_ref, s_ref, b_ref, dp_ref, ds_ref, dg_ref, acc_ref):
        t = pl.program_id(1)

        @pl.when(t == 0)
        def _():
            acc_ref[...] = jnp.zeros_like(acc_ref)

        @pl.when(t >= nx)
        def _():
            dg_ref[...] = jnp.zeros_like(dg_ref)

        @pl.when(t < nx)
        def _():
            gt = _sigmoid(gp_ref[0].astype(F32) + b_ref[...])
            dm = dm_ref[0].astype(F32)
            g1, g2 = gt[:, :D], gt[:, D:]
            dp_ref[0] = (dm * g1).astype(BF)
            ds_ref[0] = (dm * g2).astype(BF)
            dgp = jnp.concatenate([dm * p_ref[0].astype(F32) * g1 * (1.0 - g1),
                                   dm * s_ref[0].astype(F32) * g2 * (1.0 - g2)], axis=1)
            dg_ref[0] = dgp.astype(BF)
            acc_ref[0, 0:1, :] += jnp.sum(dgp, axis=0, keepdims=True)

    xmap = lambda e, t: (e, jnp.minimum(t, nx - 1), 0)
    blk = pl.BlockSpec((1, ROW_TILE, D), xmap)
    return pl.pallas_call(
        body, name="merge_bwd", grid=(2, R // ROW_TILE),
        in_specs=[blk, pl.BlockSpec((1, ROW_TILE, 2 * D), lambda e, t: (e, jnp.minimum(t, nx - 1), OFF_GATE // (2 * D))),
                  blk, blk, pl.BlockSpec((1, 2 * D), lambda e, t: (0, 0))],
        out_specs=(blk, blk, pl.BlockSpec((1, ROW_TILE, 2 * D), lambda e, t: (e, t, 0)),
                   pl.BlockSpec((1, 8, 2 * D), lambda e, t: (e, 0, 0))),
        out_shape=(jax.ShapeDtypeStruct((2, L, D), BF), jax.ShapeDtypeStruct((2, L, D), BF),
                   jax.ShapeDtypeStruct((2, R, 2 * D), BF), jax.ShapeDtypeStruct((2, 8, 2 * D), F32)),
        compiler_params=_params(("parallel", "arbitrary")),
    )(d_merged, proj3, P, S, b_merge)


def _final(out3, x, tgt, gtab, norm_post, L):
    def body(o_ref, x_ref, t_ref, g_ref, n_ref, dxo_ref, do_ref, acc_ref):
        @pl.when(pl.program_id(1) == 0)
        def _():
            acc_ref[...] = jnp.zeros_like(acc_ref)

        o = o_ref[0].astype(F32)
        gate = g_ref[0, 0:1, :]
        npost = n_ref[...]
        r2 = lax.rsqrt(jnp.mean(o * o, axis=-1, keepdims=True) + EPS)
        nh = o * r2
        on = nh * npost
        err = x_ref[0] + gate * on - t_ref[0]
        dxo = err * (1.0 / D)
        dxo_ref[0] = dxo.astype(BF)
        dnh = dxo * gate * npost
        do_ref[0] = (r2 * (dnh - nh * jnp.mean(dnh * nh, axis=-1, keepdims=True))).astype(BF)
        acc_ref[0, 0:1, :] += jnp.sum(dxo * on, axis=0, keepdims=True)
        acc_ref[0, 1:2, :] += jnp.sum(dxo * gate * nh, axis=0, keepdims=True)
        acc_ref[0, 2:3, :] += jnp.sum(err * err, axis=0, keepdims=True)

    blk = pl.BlockSpec((1, X_TILE, D), lambda e, t: (e, t, 0))
    return pl.pallas_call(
        body, name="final", grid=(2, L // X_TILE),
        in_specs=[blk, blk, blk, pl.BlockSpec((1, 8, D), lambda e, t: (e, 0, 0)),
                  pl.BlockSpec((1, D), lambda e, t: (0, 0))],
        out_specs=(blk, blk, pl.BlockSpec((1, 8, D), lambda e, t: (e, 0, 0))),
        out_shape=(jax.ShapeDtypeStruct((2, L, D), BF), jax.ShapeDtypeStruct((2, L, D), BF),
                   jax.ShapeDtypeStruct((2, 8, D), F32)),
        compiler_params=_params(("parallel", "arbitrary")),
    )(out3, x, tgt, gtab, norm_post)


def _local_step(x, c, ctx, loss_target, W, late_shard=None, exchange=False):
    nb, L, _ = x.shape
    LC = ctx.shape[1]
    R = L + LC
    assert nb == 2 and L % ROW_TILE == 0 and LC % Q == 0 and L % POOL_TILE == 0
    w_inT = W["w_in"]
    w_dtT = jnp.pad(w_inT[OFF_DT:], ((0, 64), (0, 0)))
    tables = _pool_tables(L)
    tr, tl = (2 * R) // 8, (2 * L) // 8

    c16 = jnp.zeros((16, D), F32).at[0:2].set(c).at[2].set(W["c_ctx"])
    mod16 = _adaln_fwd(c16, W["w_ada"], W["b_ada"])
    shift, scale, gate = mod16[:, :D], mod16[:, D:2 * D], mod16[:, 2 * D:]
    npre = W["norm_pre"]
    tab = jnp.zeros((2, 2, 8, D), F32)
    for e in range(2):
        tab = tab.at[e, 0, 0].set(npre[0] * (1.0 + scale[e])).at[e, 0, 1].set(shift[e])
        tab = tab.at[e, 1, 0].set(npre[0] * (1.0 + scale[2])).at[e, 1, 1].set(shift[2])
    gtab = jnp.zeros((2, 8, D), F32).at[:, 0].set(gate[0:2])

    hx = _norm_mod_fwd(x, ctx, tab)
    hx2 = hx.reshape(2 * R, D)
    if late_shard is None:
        proj = _matmul(hx2, w_inT, BF, "proj_main", tm=tr, tn=1024, bt=True, n=OFF_DT)
    else:
        proj, late = _matmul(hx2, w_inT, BF, "proj_main", tm=tr, tn=1024, bt=True, n=OFF_DT, side=_gather_side(late_shard))
        W = {**W, **_unpack_gather(late, GATHER_LATE)}
    proj3 = proj.reshape(2, R, OFF_DT)
    dt_raw = _matmul(hx2, w_dtT, F32, "proj_dt", tm=tr, bt=True).reshape(2, R, 128)
    ypool = _pool_fwd(proj3, W["pool_w"], W["pool_scale"], tables, L)
    xbc = _conv_fwd(proj3, W["conv_w"], W["conv_b"], L)
    bias128 = jnp.pad(W["dt_bias"].reshape(1, 64), ((0, 0), (0, 64)))
    dt_loc = _dt_fwd(dt_raw, bias128)
    A = -jnp.exp(W["a_log"].reshape(2, NG, HPG))
    a_loc = jnp.zeros((NG, 8, 128), F32).at[:, 0, :16].set(A.transpose(1, 0, 2).reshape(NG, 16))
    y_f, hs_f, y_b, hs_b = _ssd_fwd(xbc, dt_loc, a_loc, L)
    dskip_e = jnp.repeat(W["d_skip"].reshape(1, 32), HEAD, axis=1)
    yn, y2b = _ssd_post_fwd(y_f, y_b, xbc, proj3, dskip_e, W["ssd_norm"], L)
    ypool2, yn2 = ypool.reshape(2 * L, D), yn.reshape(2 * L, DIN)
    P = _matmul(ypool2, W["w_proj_pool"], BF, "proj_pool", tm=tl, tn=1024).reshape(2, L, D)
    S = _matmul(yn2, W["w_proj_ssd"], BF, "proj_ssd", tm=tl, tn=1024).reshape(2, L, D)
    merged = _merge_fwd(proj3, P, S, W["b_merge"], L)
    merged2 = merged.reshape(2 * L, D)
    out3 = _matmul(merged2, W["w_out"], BF, "proj_out", tm=tl, tn=1024).reshape(2, L, D)
    dxo, dout, acc_f = _final(out3, x, loss_target, gtab, W["norm_post"], L)

    dout2 = dout.reshape(2 * L, D)
    g = {}
    g["w_out"] = _matmul_tn(merged2, dout2, "dw_out", ta=1024, tn=1024, tr=4 * tl)
    d_merged = _matmul(dout2, W["w_out"], BF, "d_merged", tm=tl, tn=1024, bt=True).reshape(2, L, D)
    dP, dS, dgp, acc_m = _merge_bwd(d_merged, proj3, P, S, W["b_merge"], L)
    dP2, dS2 = dP.reshape(2 * L, D), dS.reshape(2 * L, D)
    g["w_proj_pool"] = _matmul_tn(ypool2, dP2, "dw_proj_pool", ta=1024, tn=1024, tr=4 * tl)
    g["w_proj_ssd"] = _matmul_tn(yn2, dS2, "dw_proj_ssd", ta=1024, tn=1024, tr=4 * tl)
    d_ypool = _matmul(dP2, W["w_proj_pool"], BF, "d_ypool", tm=tl, tn=1024, bt=True).reshape(2, L, D)
    d_yn = _matmul(dS2, W["w_proj_ssd"], BF, "d_yn", tm=tl, tn=1024, bt=True).reshape(2, L, DIN)
    dv, dzp, g["pool_w"], acc_p = _pool_bwd(proj3, d_ypool, W["pool_w"], jnp.swapaxes(W["pool_w"], 1, 2),
                                            W["pool_scale"], tables, L)
    dy2, dzs, acc_s = _ssd_post_bwd(d_yn, y2b, xbc, proj3, W["ssd_norm"], L)
    dxs_f, dbc_f, ddt_f, dxs_b, dbc_b, ddt_b, acc_a = _ssd_bwd(xbc, dt_loc, a_loc, hs_f, hs_b, y_f, y_b, dy2, L)
    ident = lambda j: j
    dxr_xs, acc_cx = _conv_bwd(proj3, [dxs_f, dxs_b, dy2], [None, None, dskip_e], 0, DIN, [ident, ident, ident],
                               W["conv_w"], W["conv_b"], L, "conv_bwd_xs")
    bcmap = lambda j: 2 * lax.rem(j, NG) + j // NG
    dxr_bc, acc_cb = _conv_bwd(proj3, [dbc_f, dbc_b], [None, None], DIN, 2 * NG * NST, [bcmap, bcmap],
                               W["conv_w"], W["conv_b"], L, "conv_bwd_bc")
    ddtr, acc_d = _dt_bwd(dt_raw, bias128, ddt_f, ddt_b)
    pieces = [dv, dzp, dzs, dgp, dxr_xs, dxr_bc]
    dw_rows = [_matmul_tn(p.reshape(2 * R, p.shape[2]), hx2, "dw_in_%d" % i, ta=1024, tn=1024, tr=4 * tr)
               for i, p in enumerate(pieces)]
    dw_rows.append(_matmul_tn(ddtr.reshape(2 * R, 128), hx2, "dw_in_dt", ta=128, tn=1024, tr=tr)[:64])
    g["w_in"] = jnp.concatenate(dw_rows, axis=0)
    acc_c = jnp.concatenate([acc_cx[0] + acc_cx[1], acc_cb[0] + acc_cb[1]], axis=1)
    g["conv_w"] = acc_c[0:4]
    g["conv_b"] = acc_c[4:5]
    if exchange:
        gb = _pack_grads(g, GRADS_EARLY)
        pair = _pair_add(gb, _pair_exchange(gb, None, "grads_pair_exchange_early"), "grads_pair_add_early")
        dh, recv_early = _dhx(pieces, ddtr, w_inT, w_dtT, side=_chip_exchange_side(pair))
    else:
        dh, recv_early = _dhx(pieces, ddtr, w_inT, w_dtT), None
    grad_x, acc_n = _norm_mod_bwd(dh, x, ctx, tab, dxo)
    g["w_ada"], db_rows, sm_rows = _adaln_bwd(acc_n, acc_f, mod16, c16, npre, W["w_ada"])

    g["b_ada"] = db_rows[0:1]
    g["norm_pre"] = sm_rows[0:1]
    g["c_ctx"] = sm_rows[1]
    g["norm_post"] = acc_f[0, 1:2] + acc_f[1, 1:2]
    g["b_merge"] = acc_m[0, 0:1] + acc_m[1, 0:1]
    g["pool_scale"] = acc_p[:, 0, :].reshape(1, D)
    g["dt_bias"] = (acc_d[0, 0, :64] + acc_d[1, 0, :64]).reshape(2, 32)
    dA = (acc_a[0, :, 0, :16] + acc_a[1, :, 0, :16]).reshape(NG, 2, HPG).transpose(1, 0, 2)
    g["a_log"] = (dA * A).reshape(2, 32)
    g["d_skip"] = (acc_s[0, 1] + acc_s[1, 1]).reshape(32, HEAD).sum(axis=1).reshape(1, 32)
    g["ssd_norm"] = acc_s[0, 0:1] + acc_s[1, 0:1]
    loss_lanes = acc_f[:, 2, :]
    return loss_lanes, grad_x, g, recv_early


MESH = pl.DeviceIdType.MESH
ANY = pl.BlockSpec(memory_space=pl.ANY)


def _all_gather(shard):
    m_per, n = shard.shape

    def body(x_ref, out_ref, send_sems, recv_sems, local_sem):
        x, y, c = lax.axis_index("x"), lax.axis_index("y"), lax.axis_index("c")
        me, sibling = (x, y, c), (x, y, 1 - c)
        chips = [(1 - x, y), (x, 1 - y), (1 - x, 1 - y)]

        def rows(px, py, pc):
            return out_ref.at[pl.ds((4 * px + 2 * py + pc) * m_per, m_per), :]

        def copy(k, block, to, src=None):
            return pltpu.make_async_remote_copy(
                src_ref=rows(*block) if src is None else src, dst_ref=rows(*block),
                send_sem=send_sems.at[k], recv_sem=recv_sems.at[k], device_id=to, device_id_type=MESH)

        mine = pltpu.make_async_copy(x_ref, rows(*me), local_sem)
        mine.start()
        first = [copy(0, me, sibling, src=x_ref)]
        first += [copy(1 + j, me, (*chip, c), src=x_ref) for j, chip in enumerate(chips)]
        for cp in first:
            cp.start()
        passed = [copy(4 + j, (*chip, c), sibling) for j, chip in enumerate(chips)]
        for j, chip in enumerate(chips):
            copy(1 + j, (*chip, c), me).wait_recv()
            passed[j].start()
        copy(0, sibling, me).wait_recv()
        for j, chip in enumerate(chips):
            copy(4 + j, (*chip, 1 - c), me).wait_recv()
        for cp in first + passed:
            cp.wait_send()
        mine.wait()

    return pl.pallas_call(
        body, name="all_gather_weights",
        out_shape=jax.ShapeDtypeStruct((NDEV * m_per, n), shard.dtype),
        in_specs=[ANY], out_specs=ANY,
        scratch_shapes=[pltpu.SemaphoreType.DMA((7,)), pltpu.SemaphoreType.DMA((7,)), pltpu.SemaphoreType.DMA],
    )(shard)


PAIR_PIECES = 12


def _xor_peer(k, x, y, c):
    return (1 - x if k & 4 else x, 1 - y if k & 2 else y, 1 - c if k & 1 else c)


def _pair_exchange(big, small, name):
    _, nq, rows, n = big.shape
    piece = rows // PAIR_PIECES
    assert piece * PAIR_PIECES == rows and piece % 16 == 0
    with_small = small is not None

    def body(*refs):
        if with_small:
            big_ref, small_ref, got_ref, osmall_ref, send_sems, recv_sems, local_sem = refs
        else:
            big_ref, got_ref, send_sems, recv_sems, local_sem = refs
        x, y, c = lax.axis_index("x"), lax.axis_index("y"), lax.axis_index("c")
        me = 4 * x + 2 * y + c

        def rc(src, dst, sem, peer):
            return pltpu.make_async_remote_copy(src_ref=src, dst_ref=dst, send_sem=send_sems.at[sem],
                                                recv_sem=recv_sems.at[sem], device_id=peer, device_id_type=MESH)

        sib = _xor_peer(1, x, y, c)
        local, sends, recvs = [], [], []
        for q in range(nq):
            for h in range(PAIR_PIECES):
                rws = pl.ds(h * piece, piece)
                cp = rc(big_ref.at[1 - c, q, rws], got_ref.at[q, rws], 8 + q * PAIR_PIECES + h, sib)
                sends.append(cp)
                recvs.append(cp)
        if with_small:
            local.append(pltpu.make_async_copy(small_ref, osmall_ref.at[me], local_sem))
            for k in range(1, NDEV):
                px, py, pc = _xor_peer(k, x, y, c)
                sends.append(rc(small_ref, osmall_ref.at[me], k, (px, py, pc)))
                recvs.append(rc(small_ref, osmall_ref.at[4 * px + 2 * py + pc], k, (px, py, pc)))
        for cp in local + sends:
            cp.start()
        for cp in sends:
            cp.wait_send()
        for cp in recvs:
            cp.wait_recv()
        for cp in local:
            cp.wait()

    nsem = 8 + nq * PAIR_PIECES
    out_shape = [jax.ShapeDtypeStruct(big.shape[1:], big.dtype)]
    if with_small:
        out_shape.append(jax.ShapeDtypeStruct((NDEV,) + small.shape, small.dtype))
    out = pl.pallas_call(
        body, name=name, out_shape=tuple(out_shape),
        in_specs=[ANY] * (1 + with_small), out_specs=(ANY,) * (1 + with_small),
        scratch_shapes=[pltpu.SemaphoreType.DMA((nsem,)), pltpu.SemaphoreType.DMA((nsem,)), pltpu.SemaphoreType.DMA],
    )(*((big, small) if with_small else (big,)))
    return out if with_small else out[0]


def _pair_add(big, got, name):
    _, nq, rows, n = big.shape
    tile = rows // 4
    assert rows % 64 == 0

    def body(c_ref, a_ref, b_ref, o_ref):
        o_ref[0] = (a_ref[0, 0].astype(F32) + b_ref[0].astype(F32)).astype(BF)

    blk = pl.BlockSpec((1, tile, n), lambda q, i, c_ref: (q, i, 0))
    return pl.pallas_call(
        body, name=name,
        grid_spec=pltpu.PrefetchScalarGridSpec(
            num_scalar_prefetch=1, grid=(nq, rows // tile),
            in_specs=[pl.BlockSpec((1, 1, tile, n), lambda q, i, c_ref: (c_ref[0], q, i, 0)), blk], out_specs=blk),
        out_shape=jax.ShapeDtypeStruct(got.shape, BF), compiler_params=_params(("parallel", "parallel")),
    )(lax.axis_index("c").astype(jnp.int32).reshape(1), big, got)


def _chip_exchange_side(pair):
    def make(in_refs, out_refs, send_sems, recv_sems, local_sem, arrivals=True):
        (in_ref,), (out_ref,) = in_refs, out_refs
        x, y, c = lax.axis_index("x"), lax.axis_index("y"), lax.axis_index("c")
        q = 2 * x + y
        local = [pltpu.make_async_copy(in_ref.at[q], out_ref.at[q], local_sem)]
        sends, recvs = [], []
        for j in range(1, 4):
            px, py, pc = _xor_peer(2 * j, x, y, c)
            pq = 2 * px + py
            for lst, dst in ((sends, out_ref.at[q]), (recvs, out_ref.at[pq]))[:1 + arrivals]:
                lst.append(pltpu.make_async_remote_copy(
                    src_ref=in_ref.at[pq], dst_ref=dst, send_sem=send_sems.at[j - 1], recv_sem=recv_sems.at[j - 1],
                    device_id=(px, py, pc), device_id_type=MESH))
        return local, sends, recvs

    return _SideCopies([pair], [jax.ShapeDtypeStruct(pair.shape, pair.dtype)], make)


def _gather_side(shard):
    def make(in_refs, out_refs, send_sems, recv_sems, local_sem, arrivals=True):
        (src,), (dst,) = in_refs, out_refs
        x, y, c = lax.axis_index("x"), lax.axis_index("y"), lax.axis_index("c")
        me = 4 * x + 2 * y + c
        local = [pltpu.make_async_copy(src, dst.at[me], local_sem)]
        sends, recvs = [], []
        for k in range(1, NDEV):
            px, py, pc = _xor_peer(k, x, y, c)
            for lst, slot in ((sends, me), (recvs, 4 * px + 2 * py + pc))[:1 + arrivals]:
                lst.append(pltpu.make_async_remote_copy(
                    src_ref=src, dst_ref=dst.at[slot], send_sem=send_sems.at[k - 1], recv_sem=recv_sems.at[k - 1],
                    device_id=(px, py, pc), device_id_type=MESH))
        return local, sends, recvs

    return _SideCopies([shard], [jax.ShapeDtypeStruct((NDEV,) + shard.shape, shard.dtype)], make)


ADAM_TILE = 64
PACK_W = 1024


def _adamw(recv, w, m, v, name, side=None):
    rp = w.shape[0]
    tile = min(ADAM_TILE, rp)
    nsrc = recv.shape[0]
    grid = (rp // tile,)
    n_si, n_so = (len(side.inputs), len(side.out_shapes)) if side else (0, 0)

    def body(*refs):
        r_ref, w_ref, m_ref, v_ref = refs[:4]
        g_ref, d_ref, nm_ref, nv_ref = refs[4 + n_si:8 + n_si]
        side_refs = (refs[4:4 + n_si], refs[8 + n_si:8 + n_si + n_so], refs[8 + n_si + n_so:])
        if side:
            side.start(grid, *side_refs)
        g = r_ref[0].astype(F32)
        for i in range(1, nsrc):
            g = g + r_ref[i].astype(F32)
        m1 = ADAM_B1 * m_ref[...] + (1.0 - ADAM_B1) * g
        v1 = ADAM_B2 * v_ref[...] + (1.0 - ADAM_B2) * (g * g)
        m_hat = m1 / (1.0 - ADAM_B1 ** ADAM_STEP)
        v_hat = v1 / (1.0 - ADAM_B2 ** ADAM_STEP)
        g_ref[...] = g
        d_ref[...] = -ADAM_LR * (m_hat / (jnp.sqrt(v_hat) + ADAM_EPS) + ADAM_WD * w_ref[...])
        nm_ref[...] = m1
        nv_ref[...] = v1
        if side:
            side.wait(grid, *side_refs)

    blk = pl.BlockSpec((tile, PACK_W), lambda i: (i, 0))
    shp = jax.ShapeDtypeStruct((rp, PACK_W), F32)
    return pl.pallas_call(
        body, name=name, grid=grid,
        in_specs=[pl.BlockSpec((nsrc, tile, PACK_W), lambda i: (0, i, 0)), blk, blk, blk] + [ANY] * n_si,
        out_specs=(blk, blk, blk, blk) + (ANY,) * n_so,
        out_shape=(shp, shp, shp, shp) + tuple(side.out_shapes if side else ()),
        scratch_shapes=side.scratch() if side else [],
        compiler_params=_params(("arbitrary",) if side else ("parallel",)),
    )(recv, w, m, v, *(side.inputs if side else ()))


BIG = {"w_ada": ((3 * D, D), 0), "pool_w": ((4, PGW, PGW), 1), "w_proj_pool": ((D, D), 0), "w_proj_ssd": ((DIN, D), 0),
       "w_out": ((D, D), 0), "w_in": ((IN_COLS, D), 0), "conv_w": ((4, CONV_DIM), 1)}
TRANSPOSED = ("w_ada", "w_in")
PACK_ROWS = {"w_ada": 384, "w_in": 1168, "conv_w": 16, "pool_w": 32, "w_proj_pool": 128, "w_proj_ssd": 256, "w_out": 128}
GATHER_EARLY = ("w_ada", "w_in", "conv_w")
GATHER_LATE = ("pool_w", "w_proj_pool", "w_proj_ssd", "w_out")
GRADS_LATE = ("w_ada",)
GRADS_EARLY = tuple(n for n in PACK_ROWS if n not in GRADS_LATE)
SMALL = {"c_ctx": (D,), "b_ada": (1, 3 * D), "norm_pre": (1, D), "norm_post": (1, D), "b_merge": (1, 2 * D),
         "pool_scale": (1, D), "conv_b": (1, CONV_DIM), "dt_bias": (2, 32), "a_log": (2, 32), "d_skip": (1, 32),
         "ssd_norm": (1, DIN)}
LOSS_SLOT = 128
assert all(_r % 16 == 0 for _r in PACK_ROWS.values())
SMALL_ROWS = 16


def _shard_shape(name):
    shape, ax = BIG[name]
    return tuple(s // NDEV if i == ax else s for i, s in enumerate(shape))


def _as_rows(t, rows):
    pad = [(0, 0)] * (t.ndim - 1) + [(0, rows * PACK_W - t.shape[-1])]
    return jnp.pad(t, pad).reshape(t.shape[:-1] + (rows, PACK_W))


def _shard_rows(t, name):
    sh, r = _shard_shape(name), PACK_ROWS[name]
    lead = t.shape[:t.ndim - len(sh)]
    if len(sh) == 2 and sh[1] == PACK_W:
        return jnp.pad(t, [(0, 0)] * len(lead) + [(0, r - sh[0]), (0, 0)])
    if int(np.prod(sh)) == r * PACK_W:
        return t.reshape(lead + (r, PACK_W))
    return _as_rows(t.reshape(lead + (-1,)), r)


def _to_chunks(full, name):
    shape, ax = BIG[name]
    split = shape[:ax] + (NDEV, shape[ax] // NDEV) + shape[ax + 1:]
    return _shard_rows(jnp.moveaxis(full.reshape(split), ax, 0), name)


def _from_chunks(chunks, name):
    shape, ax = BIG[name]
    return jnp.moveaxis(chunks.reshape((NDEV,) + _shard_shape(name)), 0, ax).reshape(shape)


def _rows_of(names):
    return sum(PACK_ROWS[n] for n in names)


def _pack_state(t, names):
    return jnp.concatenate([_shard_rows(t[n], n) for n in names], axis=0)


def _pack_small(t, loss_part=None):
    slot = jnp.zeros((LOSS_SLOT,), F32)
    if loss_part is not None:
        slot = slot.at[0].set(loss_part)
    return _as_rows(jnp.concatenate([t[n].reshape(-1) for n in SMALL] + [slot]), SMALL_ROWS)


def _pack_grads(g, names):
    big = jnp.concatenate([_to_chunks(g[n], n).astype(BF) for n in names], axis=1)
    return jnp.swapaxes(big.reshape(4, 2, _rows_of(names), PACK_W), 0, 1)


def _unpack_state(big, names):
    out, off = {}, 0
    for n in names:
        sh, r = _shard_shape(n), PACK_ROWS[n]
        k = int(np.prod(sh))
        if len(sh) == 2 and sh[1] == PACK_W:
            out[n] = big[off:off + sh[0]]
        else:
            out[n] = big[off:off + r].reshape(-1)[:k].reshape(sh)
        off += r
    return out


def _unpack_small(small):
    out, flat, off = {}, small.reshape(-1), 0
    for n, sh in SMALL.items():
        k = int(np.prod(sh))
        out[n] = flat[off:off + k].reshape(sh)
        off += k
    out["loss"] = flat[off]
    return out


def _pack_gather(w, names):
    pieces = []
    for n in names:
        if n == "conv_w":
            pieces.append(_as_rows(jnp.concatenate([p.reshape(-1) for p in _split(w[n], 3)]), PACK_ROWS[n]))
        else:
            pieces.append(_shard_rows(w[n], n).astype(BF))
    return jnp.concatenate(pieces, axis=0)


def _unpack_gather(gathered, names):
    g = gathered.reshape(NDEV, _rows_of(names), PACK_W)
    out, off = {}, 0
    for n in names:
        r = PACK_ROWS[n]
        sh = _shard_shape(n)
        if n == "conv_w":
            k = int(np.prod(sh))
            terms = g[:, off:off + r].reshape(NDEV, -1)[:, :3 * k].astype(F32).reshape(NDEV, 3, k)
            out[n] = _from_chunks(terms[:, 0] + terms[:, 1] + terms[:, 2], n)
        elif len(sh) == 2 and sh[1] == PACK_W:
            out[n] = _from_chunks(g[:, off:off + sh[0]], n)
        else:
            out[n] = _from_chunks(g[:, off:off + r], n)
        off += r
    return out


PARAMS = ["c_ctx", "w_ada", "b_ada", "norm_pre", "norm_post", "w_in", "b_merge", "pool_w", "pool_scale", "conv_w", "conv_b",
          "dt_bias", "a_log", "d_skip", "ssd_norm", "w_proj_pool", "w_proj_ssd", "w_out"]


def kernel(x, c, ctx, c_ctx, w_ada, b_ada, norm_pre, norm_post, w_in, b_merge, pool_w, pool_scale, conv_w, conv_b, dt_bias, a_log, d_skip, ssd_norm, w_proj_pool, w_proj_ssd, w_out, loss_target, m_c_ctx, m_w_ada, m_b_ada, m_norm_pre, m_norm_post, m_w_in, m_b_merge, m_pool_w, m_pool_scale, m_conv_w, m_conv_b, m_dt_bias, m_a_log, m_d_skip, m_ssd_norm, m_w_proj_pool, m_w_proj_ssd, m_w_out, v_c_ctx, v_w_ada, v_b_ada, v_norm_pre, v_norm_post, v_w_in, v_b_merge, v_pool_w, v_pool_scale, v_conv_w, v_conv_b, v_dt_bias, v_a_log, v_d_skip, v_ssd_norm, v_w_proj_pool, v_w_proj_ssd, v_w_out):
    given = dict(locals())
    shapes = {n: given[n].shape for n in PARAMS}

    def local(prefix):
        t = {n: (given[prefix + n] if n == "c_ctx" else given[prefix + n][0]) for n in PARAMS}
        for n in TRANSPOSED:
            t[n] = t[n].T
        return {n: t[n].reshape(_shard_shape(n) if n in BIG else SMALL[n]) for n in PARAMS}

    w, m, v = local(""), local("m_"), local("v_")

    W = _unpack_gather(_all_gather(_pack_gather(w, GATHER_EARLY)), GATHER_EARLY)
    for n in SMALL:
        W[n] = w[n]
    lanes, grad_x, g, recv_early = _local_step(x, c, ctx, loss_target, W, late_shard=_pack_gather(w, GATHER_LATE),
                                               exchange=True)
    gb = _pack_grads(g, GRADS_LATE)
    got, recv_small = _pair_exchange(gb, _pack_small(g, (0.5 / D) * jnp.sum(lanes)), "grads_pair_exchange_late")
    late = _chip_exchange_side(_pair_add(gb, got, "grads_pair_add_late"))
    res = [{} for _ in range(4)]
    *early, recv_late = _adamw(recv_early, *[_pack_state(s, GRADS_EARLY) for s in (w, m, v)], "adamw_early", side=late)
    for r, t in zip(res, early):
        r.update(_unpack_state(t, GRADS_EARLY))
    for r, t in zip(res, _adamw(recv_late, *[_pack_state(s, GRADS_LATE) for s in (w, m, v)], "adamw_late")):
        r.update(_unpack_state(t, GRADS_LATE))
    for r, t in zip(res, _adamw(recv_small, *[_pack_small(s) for s in (w, m, v)], "adamw_small")):
        r.update(_unpack_small(t))
    outs = [res[0]["loss"], grad_x]
    for r in res:
        for n in TRANSPOSED:
            r[n] = r[n].T
        outs += [r[n].reshape(shapes[n]) for n in PARAMS]
    return tuple(outs)
```

```python
import functools

import numpy as np
import jax
import jax.numpy as jnp
from jax import lax
from jax.experimental import pallas as pl
from jax.experimental.pallas import tpu as pltpu

F32, BF = jnp.float32, jnp.bfloat16

D = 1024
GRID_W = 64
EPS = 1e-6
POOL_WINDOWS = (2, 4, 8, 16)
PGW = 256
DIN = 2048
HEAD = 64
NST = 128
NG = 4
HPG = 8
GWID = HPG * HEAD
Q = 128
CONV_DIM = 3072
OFF_GATE, OFF_XBC, OFF_DT, IN_COLS = 4096, 6144, 9216, 9280
NDEV = 8
ADAM_LR, ADAM_B1, ADAM_B2, ADAM_EPS, ADAM_WD, ADAM_STEP = 0.001, 0.9, 0.999, 1e-08, 0.01, 10

V7X_VMEM_LIMIT = 56 * 2 ** 20
ROW_TILE = 256
X_TILE = 512


def _params(sem=None):
    return pltpu.CompilerParams(dimension_semantics=sem, vmem_limit_bytes=V7X_VMEM_LIMIT)


def _dot(a, b):
    return jnp.dot(a.astype(BF), b.astype(BF), preferred_element_type=F32)


def _dot_nt(a, b):
    return lax.dot_general(a.astype(BF), b.astype(BF), (((1,), (1,)), ((), ())), preferred_element_type=F32)


def _dot_tn(a, b):
    return lax.dot_general(a.astype(BF), b.astype(BF), (((0,), (0,)), ((), ())), preferred_element_type=F32)


def _split(a, n):
    parts = []
    for _ in range(n):
        p = a.astype(BF)
        parts.append(p)
        a = a - p.astype(F32)
    return parts


def _dot_sl(a, b01, n=3):
    parts = _split(a, n)
    m = a.shape[0]
    if n == 1 or m % 16:
        return sum(jnp.dot(p, b01, preferred_element_type=F32) for p in parts)
    r = jnp.dot(jnp.concatenate(parts, axis=0), b01, preferred_element_type=F32)
    return sum(r[i * m:(i + 1) * m] for i in range(n))


def _dot_sr(a01, b, n=3):
    parts = _split(b, n)
    k = b.shape[1]
    if n == 1 or k % 128:
        return sum(jnp.dot(a01, p, preferred_element_type=F32) for p in parts)
    r = jnp.dot(a01, jnp.concatenate(parts, axis=1), preferred_element_type=F32)
    return sum(r[:, i * k:(i + 1) * k] for i in range(n))


def _sigmoid(x):
    return 1.0 / (1.0 + jnp.exp(-x))


class _SideCopies:
    NSEM = 8

    def __init__(self, inputs, out_shapes, make):
        self.inputs, self.out_shapes, self.make = list(inputs), list(out_shapes), make

    def scratch(self):
        return [pltpu.SemaphoreType.DMA((self.NSEM,)), pltpu.SemaphoreType.DMA((self.NSEM,)), pltpu.SemaphoreType.DMA]

    def start(self, grid, in_refs, out_refs, sems):
        @pl.when(functools.reduce(lambda p, q: p & q, [pl.program_id(i) == 0 for i in range(len(grid))]))
        def _():
            local, sends, _ = self.make(in_refs, out_refs, *sems, arrivals=False)
            for cp in local + sends:
                cp.start()

    def wait(self, grid, in_refs, out_refs, sems):
        @pl.when(functools.reduce(lambda p, q: p & q, [pl.program_id(i) == n - 1 for i, n in enumerate(grid)]))
        def _():
            local, sends, recvs = self.make(in_refs, out_refs, *sems)
            for cp in sends:
                cp.wait_send()
            for cp in recvs:
                cp.wait_recv()
            for cp in local:
                cp.wait()


def _matmul(a, b, out_dtype, name, tm=512, tn=512, tk=1024, bt=False, n=None, side=None):
    M, K = a.shape
    N = n if n is not None else (b.shape[0] if bt else b.shape[1])
    tm, tn, tk = min(tm, M), min(tn, N), min(tk, K)
    assert M % tm == 0 and N % tn == 0 and K % tk == 0, (a.shape, b.shape)
    nk = K // tk
    grid = (M // tm, N // tn, nk)
    n_si, n_so = (len(side.inputs), len(side.out_shapes)) if side else (0, 0)

    def body(*refs):
        a_ref, b_ref, o_ref = refs[0], refs[1], refs[2 + n_si]
        acc = refs[3 + n_si + n_so]
        side_refs = (refs[2:2 + n_si], refs[3 + n_si:3 + n_si + n_so], refs[4 + n_si + n_so:])
        if side:
            side.start(grid, *side_refs)
        k = pl.program_id(2)
        p = _dot_nt(a_ref[...], b_ref[...]) if bt else _dot(a_ref[...], b_ref[...])

        @pl.when(k == 0)
        def _():
            acc[...] = p

        @pl.when(k > 0)
        def _():
            acc[...] += p

        @pl.when(k == nk - 1)
        def _():
            o_ref[...] = acc[...].astype(o_ref.dtype)

        if side:
            side.wait(grid, *side_refs)

    out = pl.pallas_call(
        body, name=name, grid=grid,
        in_specs=[pl.BlockSpec((tm, tk), lambda i, j, k: (i, k)),
                  pl.BlockSpec((tn, tk), lambda i, j, k: (j, k)) if bt else pl.BlockSpec((tk, tn), lambda i, j, k: (k, j))]
        + [ANY] * n_si,
        out_specs=(pl.BlockSpec((tm, tn), lambda i, j, k: (i, j)),) + (ANY,) * n_so,
        out_shape=(jax.ShapeDtypeStruct((M, N), out_dtype),) + tuple(side.out_shapes if side else ()),
        scratch_shapes=[pltpu.VMEM((tm, tn), F32)] + (side.scratch() if side else []),
        compiler_params=_params(("arbitrary",) * 3 if side else ("parallel", "parallel", "arbitrary")),
    )(a, b, *(side.inputs if side else ()))
    return out if side else out[0]


def _matmul_tn(a, g, name, ta=512, tn=512, tr=512):
    M, Ka = a.shape
    N = g.shape[1]
    ta, tn, tr = min(ta, Ka), min(tn, N), min(tr, M)
    assert M % tr == 0 and N % tn == 0 and Ka % ta == 0, (a.shape, g.shape)
    nr = M // tr

    def body(a_ref, g_ref, o_ref):
        k = pl.program_id(2)
        p = _dot_tn(a_ref[...], g_ref[...])

        @pl.when(k == 0)
        def _():
            o_ref[...] = p

        @pl.when(k > 0)
        def _():
            o_ref[...] += p

    return pl.pallas_call(
        body, name=name, grid=(Ka // ta, N // tn, nr),
        in_specs=[pl.BlockSpec((tr, ta), lambda i, j, k: (k, i)), pl.BlockSpec((tr, tn), lambda i, j, k: (k, j))],
        out_specs=pl.BlockSpec((ta, tn), lambda i, j, k: (i, j)),
        out_shape=jax.ShapeDtypeStruct((Ka, N), F32),
        compiler_params=_params(("parallel", "parallel", "arbitrary")),
    )(a, g)


def _dhx(pieces, ddt, w_inT, w_dtT, side=None):
    _, R, _ = pieces[0].shape
    tm = R // 4
    kb = 1024
    starts, nblk = [], []
    for p in pieces:
        starts.append(sum(nblk))
        nblk.append(p.shape[2] // kb)
    nk = sum(nblk)
    assert nk * kb == OFF_DT and R % 128 == 0
    npc = len(pieces)
    grid = (2, R // tm, nk)
    n_si, n_so = (len(side.inputs), len(side.out_shapes)) if side else (0, 0)

    def body(*refs):
        a_refs, dt_ref, w_ref, wdt_ref = refs[:npc], refs[npc], refs[npc + 1], refs[npc + 2]
        o_ref, acc = refs[npc + 3 + n_si], refs[npc + 4 + n_si + n_so]
        side_refs = (refs[npc + 3:npc + 3 + n_si], refs[npc + 4 + n_si:npc + 4 + n_si + n_so], refs[npc + 5 + n_si + n_so:])
        if side:
            side.start(grid, *side_refs)
        k = pl.program_id(2)

        @pl.when(k == 0)
        def _():
            acc[...] = _dot(dt_ref[0], wdt_ref[...])

        for p in range(npc):
            @pl.when((k >= starts[p]) & (k < starts[p] + nblk[p]))
            def _(p=p):
                acc[...] += _dot(a_refs[p][0], w_ref[...])

        @pl.when(k == nk - 1)
        def _():
            o_ref[0] = acc[...].astype(BF)

        if side:
            side.wait(grid, *side_refs)

    in_specs = [pl.BlockSpec((1, tm, kb), functools.partial(
        lambda e, t, k, s, nb: (e, t, jnp.clip(k - s, 0, nb - 1)), s=starts[p], nb=nblk[p])) for p in range(npc)]
    in_specs += [pl.BlockSpec((1, tm, 128), lambda e, t, k: (e, t, 0)),
                 pl.BlockSpec((kb, D), lambda e, t, k: (k, 0)),
                 pl.BlockSpec((128, D), lambda e, t, k: (0, 0))]
    out = pl.pallas_call(
        body, name="d_hx", grid=grid, in_specs=in_specs + [ANY] * n_si,
        out_specs=(pl.BlockSpec((1, tm, D), lambda e, t, k: (e, t, 0)),) + (ANY,) * n_so,
        out_shape=(jax.ShapeDtypeStruct((2, R, D), BF),) + tuple(side.out_shapes if side else ()),
        scratch_shapes=[pltpu.VMEM((tm, D), F32)] + (side.scratch() if side else []),
        compiler_params=_params(("arbitrary",) * 3 if side else ("parallel", "parallel", "arbitrary")),
    )(*pieces, ddt, w_inT, w_dtT, *(side.inputs if side else ()))
    return out if side else out[0]


def _adaln_fwd(c16, w_adaT_bf, b_ada):
    def body(c_ref, w_ref, b_ref, o_ref):
        cc = c_ref[...]
        o_ref[...] = _dot_nt(cc * _sigmoid(cc), w_ref[...]) + b_ref[...]

    return pl.pallas_call(body, name="adaln_fwd", out_shape=jax.ShapeDtypeStruct((16, 3 * D), F32),
                          compiler_params=_params())(c16, w_adaT_bf, b_ada)


def _adaln_bwd(acc_n, acc_f, mod16, c16, norm_pre, w_adaT_bf):
    def body(an_ref, af_ref, mod_ref, c_ref, np_ref, wt_ref, dw_ref, db_ref, sm_ref, dmod):
        npre = np_ref[...]
        dmod[...] = jnp.zeros_like(dmod)
        dnp = jnp.zeros((1, D), F32)
        dshift_c = jnp.zeros((1, D), F32)
        dgpre_c = jnp.zeros((1, D), F32)
        scale_c = mod_ref[2:3, D:2 * D]
        for e in range(2):
            dg_x, ds_x = an_ref[e, 0, 0:1, :], an_ref[e, 0, 1:2, :]
            dg_c, ds_c = an_ref[e, 1, 0:1, :], an_ref[e, 1, 1:2, :]
            dmod[e:e + 1, 0:D] = ds_x
            dmod[e:e + 1, D:2 * D] = dg_x * npre
            dmod[e:e + 1, 2 * D:3 * D] = af_ref[e, 0:1, :]
            dnp = dnp + dg_x * (1.0 + mod_ref[e:e + 1, D:2 * D]) + dg_c * (1.0 + scale_c)
            dshift_c = dshift_c + ds_c
            dgpre_c = dgpre_c + dg_c
        dmod[2:3, 0:D] = dshift_c
        dmod[2:3, D:2 * D] = dgpre_c * npre
        dm = dmod[...]
        cc = c_ref[...]
        sg = _sigmoid(cc)
        dw_ref[...] = _dot_tn(dm, cc * sg)
        db_ref[...] = jnp.zeros_like(db_ref)
        db_ref[0:1, :] = jnp.sum(dm, axis=0, keepdims=True)
        dsilu = sg * (1.0 + cc * (1.0 - sg))
        dcs = _dot(dm, wt_ref[...]) * dsilu
        sm_ref[...] = jnp.zeros_like(sm_ref)
        sm_ref[0:1, :] = dnp
        sm_ref[1:2, :] = dcs[2:3, :]

    return pl.pallas_call(
        body, name="adaln_bwd",
        out_shape=(jax.ShapeDtypeStruct((3 * D, D), F32), jax.ShapeDtypeStruct((16, 3 * D), F32),
                   jax.ShapeDtypeStruct((8, D), F32)),
        scratch_shapes=[pltpu.VMEM((16, 3 * D), F32)],
        compiler_params=_params())(acc_n, acc_f, mod16, c16, norm_pre, w_adaT_bf)


def _row_specs(L):
    nx = L // ROW_TILE
    return (pl.BlockSpec((1, ROW_TILE, D), lambda e, t: (e, jnp.minimum(t, nx - 1), 0)),
            pl.BlockSpec((1, ROW_TILE, D), lambda e, t: (e, jnp.maximum(t - nx, 0), 0)))


def _norm_mod_fwd(x, ctx, tab):
    L = x.shape[1]
    R = L + ctx.shape[1]
    nx = L // ROW_TILE

    def body(x_ref, c_ref, t_ref, o_ref):
        t = t_ref[0, 0]
        gain, shift = t[0:1], t[1:2]

        def run(src):
            for r0 in range(0, ROW_TILE, 32):
                x = src[0, pl.ds(r0, 32), :]
                r = lax.rsqrt(jnp.mean(x * x, axis=-1, keepdims=True) + EPS)
                o_ref[0, pl.ds(r0, 32), :] = (x * r * gain + shift).astype(BF)

        @pl.when(pl.program_id(1) < nx)
        def _():
            run(x_ref)

        @pl.when(pl.program_id(1) >= nx)
        def _():
            run(c_ref)

    return pl.pallas_call(
        body, name="norm_mod_fwd", grid=(2, R // ROW_TILE),
        in_specs=[*_row_specs(L), pl.BlockSpec((1, 1, 8, D), lambda e, t: (e, t // nx, 0, 0))],
        out_specs=pl.BlockSpec((1, ROW_TILE, D), lambda e, t: (e, t, 0)),
        out_shape=jax.ShapeDtypeStruct((2, R, D), BF),
        compiler_params=_params(("parallel", "parallel")),
    )(x, ctx, tab)


def _norm_mod_bwd(dh, x, ctx, tab, dxo):
    L = x.shape[1]
    R = L + ctx.shape[1]
    nx = L // ROW_TILE

    def body(dh_ref, x_ref, c_ref, t_ref, dxo_ref, gx_ref, acc_ref):
        t = pl.program_id(1)
        x = jnp.where(t < nx, x_ref[0], c_ref[0])
        r = lax.rsqrt(jnp.mean(x * x, axis=-1, keepdims=True) + EPS)
        xn = x * r
        dh = dh_ref[0].astype(F32)

        @pl.when((t == 0) | (t == nx))
        def _():
            acc_ref[...] = jnp.zeros_like(acc_ref)

        acc_ref[0, 0, 0:1, :] += jnp.sum(dh * xn, axis=0, keepdims=True)
        acc_ref[0, 0, 1:2, :] += jnp.sum(dh, axis=0, keepdims=True)

        @pl.when(t < nx)
        def _():
            dxn = dh * t_ref[0, 0][0:1]
            dx = r * (dxn - xn * jnp.mean(dxn * xn, axis=-1, keepdims=True))
            gx_ref[0] = dxo_ref[0].astype(F32) + dx

    xspec, cspec = _row_specs(L)
    return pl.pallas_call(
        body, name="norm_mod_bwd", grid=(2, R // ROW_TILE),
        in_specs=[pl.BlockSpec((1, ROW_TILE, D), lambda e, t: (e, t, 0)), xspec, cspec,
                  pl.BlockSpec((1, 1, 8, D), lambda e, t: (e, t // nx, 0, 0)), xspec],
        out_specs=(xspec, pl.BlockSpec((1, 1, 8, D), lambda e, t: (e, t // nx, 0, 0))),
        out_shape=(jax.ShapeDtypeStruct((2, L, D), F32), jax.ShapeDtypeStruct((2, 2, 8, D), F32)),
        compiler_params=_params(("parallel", "arbitrary")),
    )(dh, x, ctx, tab, dxo)


POOL_TILE = 256


def _pool_tables(L):
    rows = L // GRID_W
    mats = np.zeros((4, POOL_TILE, POOL_TILE), np.float32)
    for gi, k in enumerate(POOL_WINDOWS):
        lo, hi = k // 2, k - 1 - k // 2
        m = np.zeros((GRID_W, GRID_W), np.float32)
        for t in range(GRID_W):
            m[t, max(t - lo, 0):min(t + hi, GRID_W - 1) + 1] = 1.0
        for b in range(POOL_TILE // GRID_W):
            mats[gi, b * GRID_W:(b + 1) * GRID_W, b * GRID_W:(b + 1) * GRID_W] = m
    matsT = np.ascontiguousarray(np.transpose(mats, (0, 2, 1)))
    return (jnp.asarray(mats, BF), jnp.asarray(matsT, BF))


def _pool_cols(get_tile, mat, cs_ref, L, n):
    def step(i, carry):
        off = pl.multiple_of(i * POOL_TILE, POOL_TILE)
        t = get_tile(off)
        cs_ref[pl.ds(GRID_W + off, POOL_TILE), :] = (jnp.dot(mat, t.astype(BF), preferred_element_type=F32) if n == 1
                                                     else _dot_sr(mat, t.astype(F32), n))
        return carry

    lax.fori_loop(0, L // POOL_TILE, step, 0)
    cs_ref[pl.ds(0, GRID_W), :] = jnp.zeros((GRID_W, PGW), F32)

    def prefix(r, carry):
        o = pl.multiple_of(r * GRID_W, GRID_W)
        cs_ref[pl.ds(o + GRID_W, GRID_W), :] = cs_ref[pl.ds(o + GRID_W, GRID_W), :] + cs_ref[pl.ds(o, GRID_W), :]
        return carry

    lax.fori_loop(0, L // GRID_W, prefix, 0)


def _pool_rows(cs_ref, off, below, above, L):
    rows = L // GRID_W
    r0 = off // GRID_W
    parts = []
    for i in range(POOL_TILE // GRID_W):
        hi = pl.multiple_of(jnp.minimum(r0 + i + above + 1, rows) * GRID_W, GRID_W)
        lo = pl.multiple_of(jnp.maximum(r0 + i - below, 0) * GRID_W, GRID_W)
        parts.append(cs_ref[pl.ds(hi, GRID_W), :] - cs_ref[pl.ds(lo, GRID_W), :])
    return jnp.concatenate(parts, axis=0)


PGS = 2


def _inv_count(off, half, L):
    t = off + lax.broadcasted_iota(jnp.int32, (POOL_TILE, 1), 0)
    r, c = jnp.right_shift(t, 6), t & (GRID_W - 1)
    cr = jnp.minimum(r + half - 1, L // GRID_W - 1) - jnp.maximum(r - half, 0) + 1
    cc = jnp.minimum(c + half - 1, GRID_W - 1) - jnp.maximum(c - half, 0) + 1
    return 1.0 / (cr * cc).astype(F32)


def _pool_fwd(proj3, pool_w_bf, pool_scale, tables, L):
    mats, _ = tables
    nt = L // POOL_TILE

    def body(v_ref, z_ref, pw_ref, ps_ref, m_ref, o_ref, cs_ref):
        for j in range(PGS):
            cols = slice(j * PGW, (j + 1) * PGW)
            _pool_cols(lambda off: v_ref[0, pl.ds(off, POOL_TILE), cols], m_ref[j], cs_ref, L, 1)
            half = lax.shift_left(1, PGS * pl.program_id(1) + j)

            def step(i, carry, j=j, cols=cols, half=half):
                off = pl.multiple_of(i * POOL_TILE, POOL_TILE)
                rows = pl.ds(off, POOL_TILE)
                v = v_ref[0, rows, cols].astype(F32)
                diff = _pool_rows(cs_ref, off, half, half - 1, L) * _inv_count(off, half, L) - v
                yp = _dot(diff, pw_ref[j])
                z = z_ref[0, rows, cols].astype(F32)
                o_ref[0, rows, cols] = (yp * ps_ref[:, cols] * (z * _sigmoid(z))).astype(BF)
                return carry

            lax.fori_loop(0, nt, step, 0)

    wide = PGS * PGW
    return pl.pallas_call(
        body, name="pool_fwd", grid=(2, 4 // PGS),
        in_specs=[pl.BlockSpec((1, L, wide), lambda e, g: (e, 0, g)),
                  pl.BlockSpec((1, L, wide), lambda e, g: (e, 0, 4 // PGS + g)),
                  pl.BlockSpec((PGS, PGW, PGW), lambda e, g: (g, 0, 0)),
                  pl.BlockSpec((1, wide), lambda e, g: (0, g)),
                  pl.BlockSpec((PGS, POOL_TILE, POOL_TILE), lambda e, g: (g, 0, 0))],
        out_specs=pl.BlockSpec((1, L, wide), lambda e, g: (e, 0, g)),
        out_shape=jax.ShapeDtypeStruct((2, L, D), BF),
        scratch_shapes=[pltpu.VMEM((L + GRID_W, PGW), F32)],
        compiler_params=_params(("parallel", "parallel")),
    )(proj3, proj3, pool_w_bf, pool_scale, mats)


def _pool_bwd(proj3, d_ypool, pool_w_bf, pool_wT_bf, pool_scale, tables, L):
    mats, matsT = tables
    nt = L // POOL_TILE
    R = proj3.shape[1]

    def body(v_ref, z_ref, dy_ref, pw_ref, pwt_ref, ps_ref, m_ref, mt_ref,
             dv_ref, dz_ref, dpw_ref, acc_ref, cs_ref, dd_ref):
        e = pl.program_id(1)

        @pl.when(e == 0)
        def _():
            dpw_ref[...] = jnp.zeros_like(dpw_ref)
            acc_ref[...] = jnp.zeros_like(acc_ref)

        for j in range(PGS):
            cols = slice(j * PGW, (j + 1) * PGW)
            _pool_cols(lambda off: v_ref[0, pl.ds(off, POOL_TILE), cols], m_ref[j], cs_ref, L, 1)
            half = lax.shift_left(1, PGS * pl.program_id(0) + j)
            ps = ps_ref[:, cols]

            def step(i, carry, j=j, cols=cols, half=half, ps=ps):
                off = pl.multiple_of(i * POOL_TILE, POOL_TILE)
                rows = pl.ds(off, POOL_TILE)
                v = v_ref[0, rows, cols].astype(F32)
                diff = _pool_rows(cs_ref, off, half, half - 1, L) * _inv_count(off, half, L) - v
                yp = _dot(diff, pw_ref[j])
                z = z_ref[0, rows, cols].astype(F32)
                sg = _sigmoid(z)
                sz = z * sg
                dy = dy_ref[0, rows, cols].astype(F32)
                dz_ref[0, rows, cols] = (dy * yp * ps * (sg * (1.0 + z * (1.0 - sg)))).astype(BF)
                dys = dy * sz
                acc_ref[j, 0:1, :] += jnp.sum(dys * yp, axis=0, keepdims=True)
                dyp = dys * ps
                dpw_ref[j] += _dot_tn(diff, dyp)
                dd_ref[rows, :] = _dot(dyp, pwt_ref[j])
                return carry

            lax.fori_loop(0, nt, step, 0)
            _pool_cols(lambda off, half=half: dd_ref[pl.ds(off, POOL_TILE), :] * _inv_count(off, half, L),
                       mt_ref[j], cs_ref, L, 1)

            def step2(i, carry, cols=cols, half=half):
                off = pl.multiple_of(i * POOL_TILE, POOL_TILE)
                rows = pl.ds(off, POOL_TILE)
                dv_ref[0, rows, cols] = (_pool_rows(cs_ref, off, half - 1, half, L) - dd_ref[rows, :]).astype(BF)
                return carry

            lax.fori_loop(0, nt, step2, 0)
        dv_ref[0, pl.ds(L, R - L), :] = jnp.zeros((R - L, PGS * PGW), BF)
        dz_ref[0, pl.ds(L, R - L), :] = jnp.zeros((R - L, PGS * PGW), BF)

    wide = PGS * PGW
    return pl.pallas_call(
        body, name="pool_bwd", grid=(4 // PGS, 2),
        in_specs=[pl.BlockSpec((1, L, wide), lambda g, e: (e, 0, g)),
                  pl.BlockSpec((1, L, wide), lambda g, e: (e, 0, 4 // PGS + g)),
                  pl.BlockSpec((1, L, wide), lambda g, e: (e, 0, g)),
                  pl.BlockSpec((PGS, PGW, PGW), lambda g, e: (g, 0, 0)),
                  pl.BlockSpec((PGS, PGW, PGW), lambda g, e: (g, 0, 0)),
                  pl.BlockSpec((1, wide), lambda g, e: (0, g)),
                  pl.BlockSpec((PGS, POOL_TILE, POOL_TILE), lambda g, e: (g, 0, 0)),
                  pl.BlockSpec((PGS, POOL_TILE, POOL_TILE), lambda g, e: (g, 0, 0))],
        out_specs=(pl.BlockSpec((1, R, wide), lambda g, e: (e, 0, g)),
                   pl.BlockSpec((1, R, wide), lambda g, e: (e, 0, g)),
                   pl.BlockSpec((PGS, PGW, PGW), lambda g, e: (g, 0, 0)),
                   pl.BlockSpec((PGS, 8, PGW), lambda g, e: (g, 0, 0))),
        out_shape=(jax.ShapeDtypeStruct((2, R, D), BF), jax.ShapeDtypeStruct((2, R, D), BF),
                   jax.ShapeDtypeStruct((4, PGW, PGW), F32), jax.ShapeDtypeStruct((4, 8, PGW), F32)),
        scratch_shapes=[pltpu.VMEM((L + GRID_W, PGW), F32), pltpu.VMEM((L, PGW), F32)],
        compiler_params=_params(("parallel", "arbitrary")),
    )(proj3, proj3, d_ypool, pool_w_bf, pool_wT_bf, pool_scale, mats, matsT)


CONV_BLOCK = 128


CONV_CHUNK = 64
CONV_HALO = 8


def _halo_buf_init(buf, val, R):
    z = jnp.zeros((CONV_HALO, CONV_BLOCK), F32)
    buf[pl.ds(0, CONV_HALO), :] = z
    buf[pl.ds(CONV_HALO + R, CONV_HALO), :] = z
    if val is not None:
        buf[pl.ds(CONV_HALO, R), :] = val


def _chunk_taps(buf, start, offs, L):
    n = CONV_CHUNK + 2 * CONV_HALO
    ext = buf[pl.ds(start, n), :]
    out = []
    for off in offs:
        if off == 0:
            out.append(ext[CONV_HALO:CONV_HALO + CONV_CHUNK])
            continue
        r = pltpu.roll(ext, (-off) % n, 0)[CONV_HALO:CONV_HALO + CONV_CHUNK]
        lo, hi = (start, start + CONV_CHUNK - 1 + off) if off > 0 else (start + off, start + CONV_CHUNK - 1)
        if lo < L <= hi:
            t = start + lax.broadcasted_iota(jnp.int32, (CONV_CHUNK, 1), 0)
            r = jnp.where((t < L) == (t + off < L), r, 0.0)
        out.append(r)
    return out


def _fold8(x):
    return sum(x[i * 8:(i + 1) * 8] for i in range(CONV_CHUNK // 8))


def _conv_fwd(proj3, conv_w, conv_b, L):
    _, R, _ = proj3.shape
    cb0 = OFF_XBC // CONV_BLOCK

    def body(u_ref, w_ref, b_ref, o_ref, ubuf):
        _halo_buf_init(ubuf, u_ref[0].astype(F32), R)
        w = w_ref[...]
        b = b_ref[...]
        for start in range(0, R, CONV_CHUNK):
            taps = _chunk_taps(ubuf, start, (-2, -1, 0, 1), L)
            pre = b + sum(taps[k] * w[k:k + 1, :] for k in range(4))
            o_ref[0, pl.ds(start, CONV_CHUNK), :] = (pre * _sigmoid(pre)).astype(BF)

    return pl.pallas_call(
        body, name="conv_fwd", grid=(2, CONV_DIM // CONV_BLOCK),
        in_specs=[pl.BlockSpec((1, R, CONV_BLOCK), lambda e, j: (e, 0, cb0 + j)),
                  pl.BlockSpec((4, CONV_BLOCK), lambda e, j: (0, j)),
                  pl.BlockSpec((1, CONV_BLOCK), lambda e, j: (0, j))],
        out_specs=pl.BlockSpec((1, R, CONV_BLOCK), lambda e, j: (e, 0, j)),
        out_shape=jax.ShapeDtypeStruct((2, R, CONV_DIM), BF),
        scratch_shapes=[pltpu.VMEM((R + 2 * CONV_HALO, CONV_BLOCK), F32)],
        compiler_params=_params(("parallel", "parallel")),
    )(proj3, conv_w, conv_b)


def _conv_bwd(proj3, addends, scales, col0, ncols, in_maps, conv_w, conv_b, L, name):
    _, R, _ = proj3.shape
    cb0 = (OFF_XBC + col0) // CONV_BLOCK
    wb0 = col0 // CONV_BLOCK
    na = len(addends)
    scaled = [i for i in range(na) if scales[i] is not None]

    def body(*refs):
        u_ref, w_ref, b_ref = refs[0], refs[1], refs[2]
        a_refs = refs[3:3 + na]
        s_refs = dict(zip(scaled, refs[3 + na:3 + na + len(scaled)]))
        o_ref, acc_ref, ubuf, dbuf = refs[3 + na + len(scaled):]
        _halo_buf_init(ubuf, u_ref[0].astype(F32), R)
        _halo_buf_init(dbuf, None, R)
        w = w_ref[...]
        b = b_ref[...]
        scl = {i: s_refs[i][...] for i in scaled}
        sums = [jnp.zeros((8, CONV_BLOCK), F32) for _ in range(5)]
        for start in range(0, R, CONV_CHUNK):
            rows = pl.ds(start, CONV_CHUNK)
            taps = _chunk_taps(ubuf, start, (-2, -1, 0, 1), L)
            pre = b + sum(taps[k] * w[k:k + 1, :] for k in range(4))
            sg = _sigmoid(pre)
            dxbc = None
            for i, a in enumerate(a_refs):
                t = a[0, rows, :].astype(F32)
                t = t * scl[i] if i in scl else t
                dxbc = t if dxbc is None else dxbc + t
            dpre = dxbc * (sg * (1.0 + pre * (1.0 - sg)))
            dbuf[pl.ds(start + CONV_HALO, CONV_CHUNK), :] = dpre
            for k in range(4):
                sums[k] = sums[k] + _fold8(dpre * taps[k])
            sums[4] = sums[4] + _fold8(dpre)
        acc_ref[...] = jnp.zeros_like(acc_ref)
        for k in range(5):
            acc_ref[0, k:k + 1, :] = jnp.sum(sums[k], axis=0, keepdims=True)
        for start in range(0, R, CONV_CHUNK):
            d = _chunk_taps(dbuf, start, (2, 1, 0, -1), L)
            o_ref[0, pl.ds(start, CONV_CHUNK), :] = sum(d[k] * w[k:k + 1, :] for k in range(4)).astype(BF)

    in_specs = [pl.BlockSpec((1, R, CONV_BLOCK), lambda e, j: (e, 0, cb0 + j)),
                pl.BlockSpec((4, CONV_BLOCK), lambda e, j: (0, wb0 + j)),
                pl.BlockSpec((1, CONV_BLOCK), lambda e, j: (0, wb0 + j))]
    for m in in_maps:
        in_specs.append(pl.BlockSpec((1, R, CONV_BLOCK), functools.partial(lambda e, j, m: (e, 0, m(j)), m=m)))
    for i in scaled:
        in_specs.append(pl.BlockSpec((1, CONV_BLOCK), functools.partial(lambda e, j, m: (0, m(j)), m=in_maps[i])))
    return pl.pallas_call(
        body, name=name, grid=(2, ncols // CONV_BLOCK),
        in_specs=in_specs,
        out_specs=(pl.BlockSpec((1, R, CONV_BLOCK), lambda e, j: (e, 0, j)),
                   pl.BlockSpec((1, 8, CONV_BLOCK), lambda e, j: (e, 0, j))),
        out_shape=(jax.ShapeDtypeStruct((2, R, ncols), BF), jax.ShapeDtypeStruct((2, 8, ncols), F32)),
        scratch_shapes=[pltpu.VMEM((R + 2 * CONV_HALO, CONV_BLOCK), F32)] * 2,
        compiler_params=_params(("parallel", "parallel")),
    )(proj3, conv_w, conv_b, *addends, *[scales[i] for i in scaled])


def _softplus(x):
    e = jnp.exp(-jnp.abs(x))
    u = 1.0 + e
    return jnp.maximum(x, 0.0) + jnp.where(u == 1.0, e, e * jnp.log(u) / (u - 1.0))


def _to_local_mat(g, transpose=False):
    r = lax.broadcasted_iota(jnp.int32, (128, 128), 1 if transpose else 0)
    c = lax.broadcasted_iota(jnp.int32, (128, 128), 0 if transpose else 1)
    return ((c < 2 * HPG) & (r == jnp.right_shift(c, 3) * (NG * HPG) + g * HPG + (c & (HPG - 1)))).astype(BF)


def _dt_fwd(dt_raw, bias128):
    _, R, _ = dt_raw.shape

    def body(x_ref, b_ref, o_ref):
        dt = _softplus(x_ref[0] + b_ref[...])
        for g in range(NG):
            o_ref[0, g] = _dot_sl(dt, _to_local_mat(g))

    tr = R // 4
    return pl.pallas_call(
        body, name="dt_fwd", grid=(2, 4),
        in_specs=[pl.BlockSpec((1, tr, 128), lambda e, t: (e, t, 0)), pl.BlockSpec((1, 128), lambda e, t: (0, 0))],
        out_specs=pl.BlockSpec((1, NG, tr, 128), lambda e, t: (e, 0, t, 0)),
        out_shape=jax.ShapeDtypeStruct((2, NG, R, 128), F32),
        compiler_params=_params(("parallel", "parallel")),
    )(dt_raw, bias128)


def _dt_bwd(dt_raw, bias128, ddt_f, ddt_b):
    _, R, _ = dt_raw.shape

    def body(x_ref, b_ref, f_ref, g_ref, o_ref, acc_ref):
        ddt = sum(_dot_sl(f_ref[0, g] + g_ref[0, g], _to_local_mat(g, transpose=True)) for g in range(NG))
        d = ddt * _sigmoid(x_ref[0] + b_ref[...])
        o_ref[0] = d.astype(BF)

        @pl.when(pl.program_id(1) == 0)
        def _():
            acc_ref[...] = jnp.zeros_like(acc_ref)

        acc_ref[0, 0:1, :] += jnp.sum(d, axis=0, keepdims=True)

    tr = R // 4
    blk = pl.BlockSpec((1, tr, 128), lambda e, t: (e, t, 0))
    loc = pl.BlockSpec((1, NG, tr, 128), lambda e, t: (e, 0, t, 0))
    return pl.pallas_call(
        body, name="dt_bwd", grid=(2, 4),
        in_specs=[blk, pl.BlockSpec((1, 128), lambda e, t: (0, 0)), loc, loc],
        out_specs=(blk, pl.BlockSpec((1, 8, 128), lambda e, t: (e, 0, 0))),
        out_shape=(jax.ShapeDtypeStruct(dt_raw.shape, BF), jax.ShapeDtypeStruct((2, 8, 128), F32)),
        compiler_params=_params(("parallel", "arbitrary")),
    )(dt_raw, bias128, ddt_f, ddt_b)


GPS = 4


def _tri(d):
    i = lax.broadcasted_iota(jnp.int32, (Q, Q), 0)
    j = lax.broadcasted_iota(jnp.int32, (Q, Q), 1)
    return (i >= j) if d == 0 else (i <= j)


def _expand_mat(d):
    r = lax.broadcasted_iota(jnp.int32, (128, GWID), 0)
    c = lax.broadcasted_iota(jnp.int32, (128, GWID), 1)
    return (r == d * HPG + jnp.right_shift(c, 6)).astype(BF)


def _reduce_mat(d):
    r = lax.broadcasted_iota(jnp.int32, (GWID, 128), 0)
    c = lax.broadcasted_iota(jnp.int32, (GWID, 128), 1)
    return (c == d * HPG + jnp.right_shift(r, 6)).astype(BF)


def _ssd_chunk(d, dt, A, xs, B, C):
    mask = _tri(d)
    T = mask.astype(BF)
    Tt = _tri(1 - d).astype(BF)
    a = dt * A
    acs = _dot_sr(T, a)
    E = _expand_mat(d)
    dt_e = _dot_sl(dt, E, 2)
    acs_e = _dot_sl(acs, E, 2)
    alast_e = acs_e[Q - 1:Q, :] if d == 0 else acs_e[0:1, :]
    return dict(mask=mask, T=T, Tt=Tt, acs=acs, acsT=acs.T, dt_e=dt_e, acs_e=acs_e, lam=jnp.exp(acs_e),
                w=jnp.exp(alast_e - acs_e), decay=jnp.exp(alast_e), xt=xs * dt_e, CB=_dot_nt(C, B))


def _head_decay(q, d, hh):
    col = q["acs"][:, d * HPG + hh:d * HPG + hh + 1]
    row = q["acsT"][d * HPG + hh:d * HPG + hh + 1, :]
    return jnp.exp(jnp.where(q["mask"], col - row, -jnp.inf))


def _chunk_maps(NX, NS):
    cf = lambda s: lax.rem(s + NX, NS)
    cb = lambda s: NS - 1 - s
    return cf, cb


def _ssd_fwd(xbc, dt_loc, a_loc, L):
    _, R, _ = xbc.shape
    NX, NS = L // Q, R // Q
    cf, cb = _chunk_maps(NX, NS)

    def body(xs_f, b_f, c_f, dt_f, xs_b, b_b, c_b, dt_b, a_ref, y_f, hs_f, y_b, hs_b, hT):
        @pl.when(pl.program_id(2) == 0)
        def _():
            hT[...] = jnp.zeros_like(hT)

        lane = lax.broadcasted_iota(jnp.int32, (Q, 128), 1)
        for d, (xs_ref, b_ref, c_ref, dt_ref, y_ref, hs_ref) in enumerate(
                ((xs_f, b_f, c_f, dt_f, y_f, hs_f), (xs_b, b_b, c_b, dt_b, y_b, hs_b))):
            for gi in range(GPS):
                cols = slice(gi * GWID, (gi + 1) * GWID)
                xs = xs_ref[0, :, cols].astype(F32)
                B, C = b_ref[0, :, gi * NST:(gi + 1) * NST], c_ref[0, :, gi * NST:(gi + 1) * NST]
                q = _ssd_chunk(d, dt_ref[0, gi], a_ref[gi, 0:1, :], xs, B, C)
                h = hT[d, :, cols]
                hb = h.astype(BF)
                hs_ref[0, 0, :, cols] = hb
                parts = []
                for pr in range(HPG // 2):
                    xp = q["xt"][:, pr * 128:(pr + 1) * 128]
                    xst = jnp.concatenate([jnp.where(lane < HEAD, xp, 0.0), jnp.where(lane < HEAD, 0.0, xp)], axis=0)
                    mst = jnp.concatenate([(q["CB"] * _head_decay(q, d, 2 * pr)).astype(BF),
                                           (q["CB"] * _head_decay(q, d, 2 * pr + 1)).astype(BF)], axis=1)
                    parts.append(_dot(mst, xst))
                y_ref[0, :, cols] = jnp.concatenate(parts, axis=1) + _dot(C, hb) * q["lam"]
                hT[d, :, cols] = q["decay"] * h + _dot_tn(B, q["xt"] * q["w"])

    def spec(shape, imap):
        return pl.BlockSpec(shape, imap)

    bc0 = DIN // (GPS * NST)

    def ins(c):
        return [spec((1, Q, GPS * GWID), lambda e, g, s: (e, c(s), g)),
                spec((1, Q, GPS * NST), lambda e, g, s: (e, c(s), bc0 + g)),
                spec((1, Q, GPS * NST), lambda e, g, s: (e, c(s), bc0 + NG // GPS + g)),
                spec((1, GPS, Q, 128), lambda e, g, s: (e, g, c(s), 0))]

    def outs(c):
        return [spec((1, Q, GPS * GWID), lambda e, g, s: (e, c(s), g)),
                spec((1, 1, NST, GPS * GWID), lambda e, g, s: (e, c(s), 0, g))]

    yshape = jax.ShapeDtypeStruct((2, R, DIN), F32)
    hshape = jax.ShapeDtypeStruct((2, NS, NST, DIN), BF)
    return pl.pallas_call(
        body, name="ssd_fwd", grid=(2, NG // GPS, NS),
        in_specs=ins(cf) + ins(cb) + [spec((GPS, 8, 128), lambda e, g, s: (g, 0, 0))],
        out_specs=tuple(outs(cf) + outs(cb)),
        out_shape=(yshape, hshape, yshape, hshape),
        scratch_shapes=[pltpu.VMEM((2, NST, GPS * GWID), F32)],
        compiler_params=_params(("parallel", "parallel", "arbitrary")),
    )(xbc, xbc, xbc, dt_loc, xbc, xbc, xbc, dt_loc, a_loc)


def _ssd_bwd(xbc, dt_loc, a_loc, hs_f, hs_b, y_f, y_b, dy, L):
    _, R, _ = xbc.shape
    NX, NS = L // Q, R // Q
    cf0, cb0 = _chunk_maps(NX, NS)
    cf = lambda sp: cf0(NS - 1 - sp)
    cb = lambda sp: cb0(NS - 1 - sp)

    def body(xs_f, b_f, c_f, dt_f, hs_f_, dy_f, y_f_, xs_b, b_b, c_b, dt_b, hs_b_, dy_b, y_b_, a_ref,
             dxs_f, dbc_f, ddt_f, dxs_b, dbc_b, ddt_b, da_ref, dhT):
        @pl.when(pl.program_id(2) == 0)
        def _():
            dhT[...] = jnp.zeros_like(dhT)
            da_ref[...] = jnp.zeros_like(da_ref)

        lane = lax.broadcasted_iota(jnp.int32, (Q, 128), 1)
        row = lax.broadcasted_iota(jnp.int32, (Q, 128), 0)

        def one_chain(d, gi, xs_ref, b_ref, c_ref, dt_ref, hs_ref, dy_ref, y_ref, dxs_ref, dbc_ref, ddt_ref):
            cols = slice(gi * GWID, (gi + 1) * GWID)
            A = a_ref[gi, 0:1, :]
            xs, dt = xs_ref[0, :, cols].astype(F32), dt_ref[0, gi]
            B, C = b_ref[0, :, gi * NST:(gi + 1) * NST], c_ref[0, :, gi * NST:(gi + 1) * NST]
            q = _ssd_chunk(d, dt, A, xs, B, C)
            xt, lam, w, decay = q["xt"], q["lam"], q["w"], q["decay"]
            H = hs_ref[0, 0, :, cols]
            dyv = dy_ref[0, :, cols].astype(F32)
            dh = dhT[d, :, cols]
            dZ = dyv * lam
            dC = _dot_nt(dZ, H)
            dH = _dot_tn(C, dZ)
            U = _dot(B, dh)
            xw = xt * w
            dxt = U * w
            dalast_e = (jnp.sum(U * xw, axis=0, keepdims=True)
                        + decay * jnp.sum(dh * H.astype(F32), axis=0, keepdims=True))
            dB = _dot_nt(xw, dh)
            dCB = jnp.zeros((Q, Q), F32)
            dxt_parts = []
            for pr in range(HPG // 2):
                xp = xt[:, pr * 128:(pr + 1) * 128]
                dyp = dyv[:, pr * 128:(pr + 1) * 128]
                L0, L1 = _head_decay(q, d, 2 * pr), _head_decay(q, d, 2 * pr + 1)
                dyst = jnp.concatenate([jnp.where(lane < HEAD, dyp, 0.0), jnp.where(lane < HEAD, 0.0, dyp)], axis=0)
                mst = jnp.concatenate([(q["CB"] * L0).astype(BF), (q["CB"] * L1).astype(BF)], axis=0)
                dxt_parts.append(_dot_tn(mst, dyst))
                dmst = _dot_nt(dyst, xp)
                dCB = dCB + dmst[:Q] * L0 + dmst[Q:] * L1
            dxt_diag = jnp.concatenate(dxt_parts, axis=1)
            dC = dC + _dot(dCB, B)
            dB = dB + _dot_tn(dCB, C)
            Rm = _reduce_mat(d)
            dacs = _dot_sl(dyv * y_ref[0, :, cols] - xt.astype(BF).astype(F32) * dxt_diag - U * xw, Rm, 2)
            dxt = dxt + dxt_diag
            dal = _dot_sl(jnp.broadcast_to(dalast_e, (8, GWID)), Rm, 2)[0:1, :]
            dacs = dacs + jnp.where(row == (Q - 1 if d == 0 else 0), dal, 0.0)
            da = _dot_sr(q["Tt"], dacs, 2)
            ddt_ref[0, gi] = da * A + _dot_sl(dxt * xs, Rm, 2)
            da_ref[0, gi, 0:1, :] += jnp.sum(da * dt, axis=0, keepdims=True)
            dxs_ref[0, :, cols] = (dxt * q["dt_e"]).astype(BF)
            dbc_ref[0, :, gi * 2 * NST:(gi + 1) * 2 * NST] = jnp.concatenate([dB, dC], axis=1).astype(BF)
            dhT[d, :, cols] = decay * dh + dH

        for gi in range(GPS):
            one_chain(0, gi, xs_f, b_f, c_f, dt_f, hs_f_, dy_f, y_f_, dxs_f, dbc_f, ddt_f)
            one_chain(1, gi, xs_b, b_b, c_b, dt_b, hs_b_, dy_b, y_b_, dxs_b, dbc_b, ddt_b)

    def spec(shape, imap):
        return pl.BlockSpec(shape, imap)

    bc0 = DIN // (GPS * NST)

    def ins(c):
        return [spec((1, Q, GPS * GWID), lambda e, g, s: (e, c(s), g)),
                spec((1, Q, GPS * NST), lambda e, g, s: (e, c(s), bc0 + g)),
                spec((1, Q, GPS * NST), lambda e, g, s: (e, c(s), bc0 + NG // GPS + g)),
                spec((1, GPS, Q, 128), lambda e, g, s: (e, g, c(s), 0)),
                spec((1, 1, NST, GPS * GWID), lambda e, g, s: (e, c(s), 0, g)),
                spec((1, Q, GPS * GWID), lambda e, g, s: (e, c(s), g)),
                spec((1, Q, GPS * GWID), lambda e, g, s: (e, c(s), g))]

    def outs(c):
        return [spec((1, Q, GPS * GWID), lambda e, g, s: (e, c(s), g)),
                spec((1, Q, GPS * 2 * NST), lambda e, g, s: (e, c(s), g)),
                spec((1, GPS, Q, 128), lambda e, g, s: (e, g, c(s), 0))]

    s_xs = jax.ShapeDtypeStruct((2, R, DIN), BF)
    s_bc = jax.ShapeDtypeStruct((2, R, 2 * NG * NST), BF)
    s_dt = jax.ShapeDtypeStruct((2, NG, R, 128), F32)
    return pl.pallas_call(
        body, name="ssd_bwd", grid=(2, NG // GPS, NS),
        in_specs=ins(cf) + ins(cb) + [spec((GPS, 8, 128), lambda e, g, s: (g, 0, 0))],
        out_specs=tuple(outs(cf) + outs(cb) + [spec((1, GPS, 8, 128), lambda e, g, s: (e, g, 0, 0))]),
        out_shape=(s_xs, s_bc, s_dt, s_xs, s_bc, s_dt, jax.ShapeDtypeStruct((2, NG, 8, 128), F32)),
        scratch_shapes=[pltpu.VMEM((2, NST, GPS * GWID), F32)],
        compiler_params=_params(("parallel", "parallel", "arbitrary")),
    )(xbc, xbc, xbc, dt_loc, hs_f, dy, y_f, xbc, xbc, xbc, dt_loc, hs_b, dy, y_b, a_loc)


def _ssd_post_fwd(y_f, y_b, xbc, proj3, dskip_e, ssd_norm, L):
    def body(yf_ref, yb_ref, xs_ref, z_ref, ds_ref, w_ref, o_ref, y2_ref):
        y2 = yf_ref[0] + yb_ref[0] + ds_ref[...] * xs_ref[0].astype(F32)
        y2_ref[0] = y2.astype(BF)
        z = z_ref[0].astype(F32)
        u = y2 * (z * _sigmoid(z))
        parts = []
        for g in range(NG):
            ug = u[:, g * GWID:(g + 1) * GWID]
            parts.append(ug * lax.rsqrt(jnp.mean(ug * ug, axis=-1, keepdims=True) + EPS))
        o_ref[0] = (jnp.concatenate(parts, axis=1) * w_ref[...]).astype(BF)

    blk = lambda c: pl.BlockSpec((1, X_TILE, DIN), lambda e, t: (e, t, c))
    vec = pl.BlockSpec((1, DIN), lambda e, t: (0, 0))
    return pl.pallas_call(
        body, name="ssd_post_fwd", grid=(2, L // X_TILE),
        in_specs=[blk(0), blk(0), blk(0), blk(1), vec, vec],
        out_specs=(blk(0), blk(0)),
        out_shape=(jax.ShapeDtypeStruct((2, L, DIN), BF), jax.ShapeDtypeStruct((2, L, DIN), BF)),
        compiler_params=_params(("parallel", "parallel")),
    )(y_f, y_b, xbc, proj3, dskip_e, ssd_norm)


def _ssd_post_bwd(d_yn, y2b, xbc, proj3, ssd_norm, L):
    _, R, _ = xbc.shape
    nx = L // ROW_TILE

    def body(dyn_ref, y2_ref, xs_ref, z_ref, w_ref, dy_ref, dz_ref, acc_ref):
        t = pl.program_id(1)

        @pl.when(t == 0)
        def _():
            acc_ref[...] = jnp.zeros_like(acc_ref)

        @pl.when(t >= nx)
        def _():
            dy_ref[...] = jnp.zeros_like(dy_ref)
            dz_ref[...] = jnp.zeros_like(dz_ref)

        @pl.when(t < nx)
        def _():
            xs = xs_ref[0].astype(F32)
            y2 = y2_ref[0].astype(F32)
            z = z_ref[0].astype(F32)
            sg = _sigmoid(z)
            sz = z * sg
            u = y2 * sz
            dyn = dyn_ref[0].astype(F32)
            dun = dyn * w_ref[...]
            uh_parts, du_parts = [], []
            for g in range(NG):
                sl = slice(g * GWID, (g + 1) * GWID)
                ug = u[:, sl]
                rg = lax.rsqrt(jnp.mean(ug * ug, axis=-1, keepdims=True) + EPS)
                uh = ug * rg
                dg = dun[:, sl]
                du_parts.append(rg * (dg - uh * jnp.mean(dg * uh, axis=-1, keepdims=True)))
                uh_parts.append(uh)
            du = jnp.concatenate(du_parts, axis=1)
            uh = jnp.concatenate(uh_parts, axis=1)
            dy2 = du * sz
            dy_ref[0] = dy2.astype(BF)
            dz_ref[0] = (du * y2 * (sg * (1.0 + z * (1.0 - sg)))).astype(BF)
            acc_ref[0, 0:1, :] += jnp.sum(dyn * uh, axis=0, keepdims=True)
            acc_ref[0, 1:2, :] += jnp.sum(dy2 * xs, axis=0, keepdims=True)

    xmap = lambda c: (lambda e, t: (e, jnp.minimum(t, nx - 1), c))
    blk = lambda c: pl.BlockSpec((1, ROW_TILE, DIN), xmap(c))
    oblk = pl.BlockSpec((1, ROW_TILE, DIN), lambda e, t: (e, t, 0))
    vec = pl.BlockSpec((1, DIN), lambda e, t: (0, 0))
    return pl.pallas_call(
        body, name="ssd_post_bwd", grid=(2, R // ROW_TILE),
        in_specs=[blk(0), blk(0), blk(0), blk(1), vec],
        out_specs=(oblk, oblk, pl.BlockSpec((1, 8, DIN), lambda e, t: (e, 0, 0))),
        out_shape=(jax.ShapeDtypeStruct((2, R, DIN), BF), jax.ShapeDtypeStruct((2, R, DIN), BF),
                   jax.ShapeDtypeStruct((2, 8, DIN), F32)),
        compiler_params=_params(("parallel", "arbitrary")),
    )(d_yn, y2b, xbc, proj3, ssd_norm)


def _merge_fwd(proj3, P, S, b_merge, L):
    def body(gp_ref, p_ref, s_ref, b_ref, o_ref):
        gt = _sigmoid(gp_ref[0].astype(F32) + b_ref[...])
        o_ref[0] = (gt[:, :D] * p_ref[0].astype(F32) + gt[:, D:] * s_ref[0].astype(F32)).astype(BF)

    blk = pl.BlockSpec((1, X_TILE, D), lambda e, t: (e, t, 0))
    return pl.pallas_call(
        body, name="merge_fwd", grid=(2, L // X_TILE),
        in_specs=[pl.BlockSpec((1, X_TILE, 2 * D), lambda e, t: (e, t, OFF_GATE // (2 * D))), blk, blk,
                  pl.BlockSpec((1, 2 * D), lambda e, t: (0, 0))],
        out_specs=blk, out_shape=jax.ShapeDtypeStruct((2, L, D), BF),
        compiler_params=_params(("parallel", "parallel")),
    )(proj3, P, S, b_merge)


def _merge_bwd(d_merged, proj3, P, S, b_merge, L):
    _, R, _ = proj3.shape
    nx = L // ROW_TILE

    def body(dm_ref, gp_ref, p_ref, s_ref, b_ref, dp_ref, ds_ref, dg_ref, acc_ref):
        t = pl.program_id(1)

        @pl.when(t == 0)
        def _():
            acc_ref[...] = jnp.zeros_like(acc_ref)

        @pl.when(t >= nx)
        def _():
            dg_ref[...] = jnp.zeros_like(dg_ref)

        @pl.when(t < nx)
        def _():
            gt = _sigmoid(gp_ref[0].astype(F32) + b_ref[...])
            dm = dm_ref[0].astype(F32)
            g1, g2 = gt[:, :D], gt[:, D:]
            dp_ref[0] = (dm * g1).astype(BF)
            ds_ref[0] = (dm * g2).astype(BF)
            dgp = jnp.concatenate([dm * p_ref[0].astype(F32) * g1 * (1.0 - g1),
                                   dm * s_ref[0].astype(F32) * g2 * (1.0 - g2)], axis=1)
            dg_ref[0] = dgp.astype(BF)
            acc_ref[0, 0:1, :] += jnp.sum(dgp, axis=0, keepdims=True)

    xmap = lambda e, t: (e, jnp.minimum(t, nx - 1), 0)
    blk = pl.BlockSpec((1, ROW_TILE, D), xmap)
    return pl.pallas_call(
        body, name="merge_bwd", grid=(2, R // ROW_TILE),
        in_specs=[blk, pl.BlockSpec((1, ROW_TILE, 2 * D), lambda e, t: (e, jnp.minimum(t, nx - 1), OFF_GATE // (2 * D))),
                  blk, blk, pl.BlockSpec((1, 2 * D), lambda e, t: (0, 0))],
        out_specs=(blk, blk, pl.BlockSpec((1, ROW_TILE, 2 * D), lambda e, t: (e, t, 0)),
                   pl.BlockSpec((1, 8, 2 * D), lambda e, t: (e, 0, 0))),
        out_shape=(jax.ShapeDtypeStruct((2, L, D), BF), jax.ShapeDtypeStruct((2, L, D), BF),
                   jax.ShapeDtypeStruct((2, R, 2 * D), BF), jax.ShapeDtypeStruct((2, 8, 2 * D), F32)),
        compiler_params=_params(("parallel", "arbitrary")),
    )(d_merged, proj3, P, S, b_merge)


def _final(out3, x, tgt, gtab, norm_post, L):
    def body(o_ref, x_ref, t_ref, g_ref, n_ref, dxo_ref, do_ref, acc_ref):
        @pl.when(pl.program_id(1) == 0)
        def _():
            acc_ref[...] = jnp.zeros_like(acc_ref)

        o = o_ref[0].astype(F32)
        gate = g_ref[0, 0:1, :]
        npost = n_ref[...]
        r2 = lax.rsqrt(jnp.mean(o * o, axis=-1, keepdims=True) + EPS)
        nh = o * r2
        on = nh * npost
        err = x_ref[0] + gate * on - t_ref[0]
        dxo = err * (1.0 / D)
        dxo_ref[0] = dxo.astype(BF)
        dnh = dxo * gate * npost
        do_ref[0] = (r2 * (dnh - nh * jnp.mean(dnh * nh, axis=-1, keepdims=True))).astype(BF)
        acc_ref[0, 0:1, :] += jnp.sum(dxo * on, axis=0, keepdims=True)
        acc_ref[0, 1:2, :] += jnp.sum(dxo * gate * nh, axis=0, keepdims=True)
        acc_ref[0, 2:3, :] += jnp.sum(err * err, axis=0, keepdims=True)

    blk = pl.BlockSpec((1, X_TILE, D), lambda e, t: (e, t, 0))
    return pl.pallas_call(
        body, name="final", grid=(2, L // X_TILE),
        in_specs=[blk, blk, blk, pl.BlockSpec((1, 8, D), lambda e, t: (e, 0, 0)),
                  pl.BlockSpec((1, D), lambda e, t: (0, 0))],
        out_specs=(blk, blk, pl.BlockSpec((1, 8, D), lambda e, t: (e, 0, 0))),
        out_shape=(jax.ShapeDtypeStruct((2, L, D), BF), jax.ShapeDtypeStruct((2, L, D), BF),
                   jax.ShapeDtypeStruct((2, 8, D), F32)),
        compiler_params=_params(("parallel", "arbitrary")),
    )(out3, x, tgt, gtab, norm_post)


def _local_step(x, c, ctx, loss_target, W, late_shard=None, exchange=False):
    nb, L, _ = x.shape
    LC = ctx.shape[1]
    R = L + LC
    assert nb == 2 and L % ROW_TILE == 0 and LC % Q == 0 and L % POOL_TILE == 0
    w_inT = W["w_in"]
    w_dtT = jnp.pad(w_inT[OFF_DT:], ((0, 64), (0, 0)))
    tables = _pool_tables(L)
    tr, tl = (2 * R) // 8, (2 * L) // 8

    c16 = jnp.zeros((16, D), F32).at[0:2].set(c).at[2].set(W["c_ctx"])
    mod16 = _adaln_fwd(c16, W["w_ada"], W["b_ada"])
    shift, scale, gate = mod16[:, :D], mod16[:, D:2 * D], mod16[:, 2 * D:]
    npre = W["norm_pre"]
    tab = jnp.zeros((2, 2, 8, D), F32)
    for e in range(2):
        tab = tab.at[e, 0, 0].set(npre[0] * (1.0 + scale[e])).at[e, 0, 1].set(shift[e])
        tab = tab.at[e, 1, 0].set(npre[0] * (1.0 + scale[2])).at[e, 1, 1].set(shift[2])
    gtab = jnp.zeros((2, 8, D), F32).at[:, 0].set(gate[0:2])

    hx = _norm_mod_fwd(x, ctx, tab)
    hx2 = hx.reshape(2 * R, D)
    if late_shard is None:
        proj = _matmul(hx2, w_inT, BF, "proj_main", tm=tr, tn=1024, bt=True, n=OFF_DT)
    else:
        proj, late = _matmul(hx2, w_inT, BF, "proj_main", tm=tr, tn=1024, bt=True, n=OFF_DT, side=_gather_side(late_shard))
        W = {**W, **_unpack_gather(late, GATHER_LATE)}
    proj3 = proj.reshape(2, R, OFF_DT)
    dt_raw = _matmul(hx2, w_dtT, F32, "proj_dt", tm=tr, bt=True).reshape(2, R, 128)
    ypool = _pool_fwd(proj3, W["pool_w"], W["pool_scale"], tables, L)
    xbc = _conv_fwd(proj3, W["conv_w"], W["conv_b"], L)
    bias128 = jnp.pad(W["dt_bias"].reshape(1, 64), ((0, 0), (0, 64)))
    dt_loc = _dt_fwd(dt_raw, bias128)
    A = -jnp.exp(W["a_log"].reshape(2, NG, HPG))
    a_loc = jnp.zeros((NG, 8, 128), F32).at[:, 0, :16].set(A.transpose(1, 0, 2).reshape(NG, 16))
    y_f, hs_f, y_b, hs_b = _ssd_fwd(xbc, dt_loc, a_loc, L)
    dskip_e = jnp.repeat(W["d_skip"].reshape(1, 32), HEAD, axis=1)
    yn, y2b = _ssd_post_fwd(y_f, y_b, xbc, proj3, dskip_e, W["ssd_norm"], L)
    ypool2, yn2 = ypool.reshape(2 * L, D), yn.reshape(2 * L, DIN)
    P = _matmul(ypool2, W["w_proj_pool"], BF, "proj_pool", tm=tl, tn=1024).reshape(2, L, D)
    S = _matmul(yn2, W["w_proj_ssd"], BF, "proj_ssd", tm=tl, tn=1024).reshape(2, L, D)
    merged = _merge_fwd(proj3, P, S, W["b_merge"], L)
    merged2 = merged.reshape(2 * L, D)
    out3 = _matmul(merged2, W["w_out"], BF, "proj_out", tm=tl, tn=1024).reshape(2, L, D)
    dxo, dout, acc_f = _final(out3, x, loss_target, gtab, W["norm_post"], L)

    dout2 = dout.reshape(2 * L, D)
    g = {}
    g["w_out"] = _matmul_tn(merged2, dout2, "dw_out", ta=1024, tn=1024, tr=4 * tl)
    d_merged = _matmul(dout2, W["w_out"], BF, "d_merged", tm=tl, tn=1024, bt=True).reshape(2, L, D)
    dP, dS, dgp, acc_m = _merge_bwd(d_merged, proj3, P, S, W["b_merge"], L)
    dP2, dS2 = dP.reshape(2 * L, D), dS.reshape(2 * L, D)
    g["w_proj_pool"] = _matmul_tn(ypool2, dP2, "dw_proj_pool", ta=1024, tn=1024, tr=4 * tl)
    g["w_proj_ssd"] = _matmul_tn(yn2, dS2, "dw_proj_ssd", ta=1024, tn=1024, tr=4 * tl)
    d_ypool = _matmul(dP2, W["w_proj_pool"], BF, "d_ypool", tm=tl, tn=1024, bt=True).reshape(2, L, D)
    d_yn = _matmul(dS2, W["w_proj_ssd"], BF, "d_yn", tm=tl, tn=1024, bt=True).reshape(2, L, DIN)
    dv, dzp, g["pool_w"], acc_p = _pool_bwd(proj3, d_ypool, W["pool_w"], jnp.swapaxes(W["pool_w"], 1, 2),
                                            W["pool_scale"], tables, L)
    dy2, dzs, acc_s = _ssd_post_bwd(d_yn, y2b, xbc, proj3, W["ssd_norm"], L)
    dxs_f, dbc_f, ddt_f, dxs_b, dbc_b, ddt_b, acc_a = _ssd_bwd(xbc, dt_loc, a_loc, hs_f, hs_b, y_f, y_b, dy2, L)
    ident = lambda j: j
    dxr_xs, acc_cx = _conv_bwd(proj3, [dxs_f, dxs_b, dy2], [None, None, dskip_e], 0, DIN, [ident, ident, ident],
                               W["conv_w"], W["conv_b"], L, "conv_bwd_xs")
    bcmap = lambda j: 2 * lax.rem(j, NG) + j // NG
    dxr_bc, acc_cb = _conv_bwd(proj3, [dbc_f, dbc_b], [None, None], DIN, 2 * NG * NST, [bcmap, bcmap],
                               W["conv_w"], W["conv_b"], L, "conv_bwd_bc")
    ddtr, acc_d = _dt_bwd(dt_raw, bias128, ddt_f, ddt_b)
    pieces = [dv, dzp, dzs, dgp, dxr_xs, dxr_bc]
    dw_rows = [_matmul_tn(p.reshape(2 * R, p.shape[2]), hx2, "dw_in_%d" % i, ta=1024, tn=1024, tr=4 * tr)
               for i, p in enumerate(pieces)]
    dw_rows.append(_matmul_tn(ddtr.reshape(2 * R, 128), hx2, "dw_in_dt", ta=128, tn=1024, tr=tr)[:64])
    g["w_in"] = jnp.concatenate(dw_rows, axis=0)
    acc_c = jnp.concatenate([acc_cx[0] + acc_cx[1], acc_cb[0] + acc_cb[1]], axis=1)
    g["conv_w"] = acc_c[0:4]
    g["conv_b"] = acc_c[4:5]
    if exchange:
        gb = _pack_grads(g, GRADS_EARLY)
        pair = _pair_add(gb, _pair_exchange(gb, None, "grads_pair_exchange_early"), "grads_pair_add_early")
        dh, recv_early = _dhx(pieces, ddtr, w_inT, w_dtT, side=_chip_exchange_side(pair))
    else:
        dh, recv_early = _dhx(pieces, ddtr, w_inT, w_dtT), None
    grad_x, acc_n = _norm_mod_bwd(dh, x, ctx, tab, dxo)
    g["w_ada"], db_rows, sm_rows = _adaln_bwd(acc_n, acc_f, mod16, c16, npre, W["w_ada"])

    g["b_ada"] = db_rows[0:1]
    g["norm_pre"] = sm_rows[0:1]
    g["c_ctx"] = sm_rows[1]
    g["norm_post"] = acc_f[0, 1:2] + acc_f[1, 1:2]
    g["b_merge"] = acc_m[0, 0:1] + acc_m[1, 0:1]
    g["pool_scale"] = acc_p[:, 0, :].reshape(1, D)
    g["dt_bias"] = (acc_d[0, 0, :64] + acc_d[1, 0, :64]).reshape(2, 32)
    dA = (acc_a[0, :, 0, :16] + acc_a[1, :, 0, :16]).reshape(NG, 2, HPG).transpose(1, 0, 2)
    g["a_log"] = (dA * A).reshape(2, 32)
    g["d_skip"] = (acc_s[0, 1] + acc_s[1, 1]).reshape(32, HEAD).sum(axis=1).reshape(1, 32)
    g["ssd_norm"] = acc_s[0, 0:1] + acc_s[1, 0:1]
    loss_lanes = acc_f[:, 2, :]
    return loss_lanes, grad_x, g, recv_early


MESH = pl.DeviceIdType.MESH
ANY = pl.BlockSpec(memory_space=pl.ANY)


def _all_gather(shard):
    m_per, n = shard.shape

    def body(x_ref, out_ref, send_sems, recv_sems, local_sem):
        x, y, c = lax.axis_index("x"), lax.axis_index("y"), lax.axis_index("c")
        me, sibling = (x, y, c), (x, y, 1 - c)
        chips = [(1 - x, y), (x, 1 - y), (1 - x, 1 - y)]

        def rows(px, py, pc):
            return out_ref.at[pl.ds((4 * px + 2 * py + pc) * m_per, m_per), :]

        def copy(k, block, to, src=None):
            return pltpu.make_async_remote_copy(
                src_ref=rows(*block) if src is None else src, dst_ref=rows(*block),
                send_sem=send_sems.at[k], recv_sem=recv_sems.at[k], device_id=to, device_id_type=MESH)

        mine = pltpu.make_async_copy(x_ref, rows(*me), local_sem)
        mine.start()
        first = [copy(0, me, sibling, src=x_ref)]
        first += [copy(1 + j, me, (*chip, c), src=x_ref) for j, chip in enumerate(chips)]
        for cp in first:
            cp.start()
        passed = [copy(4 + j, (*chip, c), sibling) for j, chip in enumerate(chips)]
        for j, chip in enumerate(chips):
            copy(1 + j, (*chip, c), me).wait_recv()
            passed[j].start()
        copy(0, sibling, me).wait_recv()
        for j, chip in enumerate(chips):
            copy(4 + j, (*chip, 1 - c), me).wait_recv()
        for cp in first + passed:
            cp.wait_send()
        mine.wait()

    return pl.pallas_call(
        body, name="all_gather_weights",
        out_shape=jax.ShapeDtypeStruct((NDEV * m_per, n), shard.dtype),
        in_specs=[ANY], out_specs=ANY,
        scratch_shapes=[pltpu.SemaphoreType.DMA((7,)), pltpu.SemaphoreType.DMA((7,)), pltpu.SemaphoreType.DMA],
    )(shard)


PAIR_PIECES = 12


def _xor_peer(k, x, y, c):
    return (1 - x if k & 4 else x, 1 - y if k & 2 else y, 1 - c if k & 1 else c)


def _pair_exchange(big, small, name):
    _, nq, rows, n = big.shape
    piece = rows // PAIR_PIECES
    assert piece * PAIR_PIECES == rows and piece % 16 == 0
    with_small = small is not None

    def body(*refs):
        if with_small:
            big_ref, small_ref, got_ref, osmall_ref, send_sems, recv_sems, local_sem = refs
        else:
            big_ref, got_ref, send_sems, recv_sems, local_sem = refs
        x, y, c = lax.axis_index("x"), lax.axis_index("y"), lax.axis_index("c")
        me = 4 * x + 2 * y + c

        def rc(src, dst, sem, peer):
            return pltpu.make_async_remote_copy(src_ref=src, dst_ref=dst, send_sem=send_sems.at[sem],
                                                recv_sem=recv_sems.at[sem], device_id=peer, device_id_type=MESH)

        sib = _xor_peer(1, x, y, c)
        local, sends, recvs = [], [], []
        for q in range(nq):
            for h in range(PAIR_PIECES):
                rws = pl.ds(h * piece, piece)
                cp = rc(big_ref.at[1 - c, q, rws], got_ref.at[q, rws], 8 + q * PAIR_PIECES + h, sib)
                sends.append(cp)
                recvs.append(cp)
        if with_small:
            local.append(pltpu.make_async_copy(small_ref, osmall_ref.at[me], local_sem))
            for k in range(1, NDEV):
                px, py, pc = _xor_peer(k, x, y, c)
                sends.append(rc(small_ref, osmall_ref.at[me], k, (px, py, pc)))
                recvs.append(rc(small_ref, osmall_ref.at[4 * px + 2 * py + pc], k, (px, py, pc)))
        for cp in local + sends:
            cp.start()
        for cp in sends:
            cp.wait_send()
        for cp in recvs:
            cp.wait_recv()
        for cp in local:
            cp.wait()

    nsem = 8 + nq * PAIR_PIECES
    out_shape = [jax.ShapeDtypeStruct(big.shape[1:], big.dtype)]
    if with_small:
        out_shape.append(jax.ShapeDtypeStruct((NDEV,) + small.shape, small.dtype))
    out = pl.pallas_call(
        body, name=name, out_shape=tuple(out_shape),
        in_specs=[ANY] * (1 + with_small), out_specs=(ANY,) * (1 + with_small),
        scratch_shapes=[pltpu.SemaphoreType.DMA((nsem,)), pltpu.SemaphoreType.DMA((nsem,)), pltpu.SemaphoreType.DMA],
    )(*((big, small) if with_small else (big,)))
    return out if with_small else out[0]


def _pair_add(big, got, name):
    _, nq, rows, n = big.shape
    tile = rows // 4
    assert rows % 64 == 0

    def body(c_ref, a_ref, b_ref, o_ref):
        o_ref[0] = (a_ref[0, 0].astype(F32) + b_ref[0].astype(F32)).astype(BF)

    blk = pl.BlockSpec((1, tile, n), lambda q, i, c_ref: (q, i, 0))
    return pl.pallas_call(
        body, name=name,
        grid_spec=pltpu.PrefetchScalarGridSpec(
            num_scalar_prefetch=1, grid=(nq, rows // tile),
            in_specs=[pl.BlockSpec((1, 1, tile, n), lambda q, i, c_ref: (c_ref[0], q, i, 0)), blk], out_specs=blk),
        out_shape=jax.ShapeDtypeStruct(got.shape, BF), compiler_params=_params(("parallel", "parallel")),
    )(lax.axis_index("c").astype(jnp.int32).reshape(1), big, got)


def _chip_exchange_side(pair):
    def make(in_refs, out_refs, send_sems, recv_sems, local_sem, arrivals=True):
        (in_ref,), (out_ref,) = in_refs, out_refs
        x, y, c = lax.axis_index("x"), lax.axis_index("y"), lax.axis_index("c")
        q = 2 * x + y
        local = [pltpu.make_async_copy(in_ref.at[q], out_ref.at[q], local_sem)]
        sends, recvs = [], []
        for j in range(1, 4):
            px, py, pc = _xor_peer(2 * j, x, y, c)
            pq = 2 * px + py
            for lst, dst in ((sends, out_ref.at[q]), (recvs, out_ref.at[pq]))[:1 + arrivals]:
                lst.append(pltpu.make_async_remote_copy(
                    src_ref=in_ref.at[pq], dst_ref=dst, send_sem=send_sems.at[j - 1], recv_sem=recv_sems.at[j - 1],
                    device_id=(px, py, pc), device_id_type=MESH))
        return local, sends, recvs

    return _SideCopies([pair], [jax.ShapeDtypeStruct(pair.shape, pair.dtype)], make)


def _gather_side(shard):
    def make(in_refs, out_refs, send_sems, recv_sems, local_sem, arrivals=True):
        (src,), (dst,) = in_refs, out_refs
        x, y, c = lax.axis_index("x"), lax.axis_index("y"), lax.axis_index("c")
        me = 4 * x + 2 * y + c
        local = [pltpu.make_async_copy(src, dst.at[me], local_sem)]
        sends, recvs = [], []
        for k in range(1, NDEV):
            px, py, pc = _xor_peer(k, x, y, c)
            for lst, slot in ((sends, me), (recvs, 4 * px + 2 * py + pc))[:1 + arrivals]:
                lst.append(pltpu.make_async_remote_copy(
                    src_ref=src, dst_ref=dst.at[slot], send_sem=send_sems.at[k - 1], recv_sem=recv_sems.at[k - 1],
                    device_id=(px, py, pc), device_id_type=MESH))
        return local, sends, recvs

    return _SideCopies([shard], [jax.ShapeDtypeStruct((NDEV,) + shard.shape, shard.dtype)], make)


ADAM_TILE = 64
PACK_W = 1024


def _adamw(recv, w, m, v, name, side=None):
    rp = w.shape[0]
    tile = min(ADAM_TILE, rp)
    nsrc = recv.shape[0]
    grid = (rp // tile,)
    n_si, n_so = (len(side.inputs), len(side.out_shapes)) if side else (0, 0)

    def body(*refs):
        r_ref, w_ref, m_ref, v_ref = refs[:4]
        g_ref, d_ref, nm_ref, nv_ref = refs[4 + n_si:8 + n_si]
        side_refs = (refs[4:4 + n_si], refs[8 + n_si:8 + n_si + n_so], refs[8 + n_si + n_so:])
        if side:
            side.start(grid, *side_refs)
        g = r_ref[0].astype(F32)
        for i in range(1, nsrc):
            g = g + r_ref[i].astype(F32)
        m1 = ADAM_B1 * m_ref[...] + (1.0 - ADAM_B1) * g
        v1 = ADAM_B2 * v_ref[...] + (1.0 - ADAM_B2) * (g * g)
        m_hat = m1 / (1.0 - ADAM_B1 ** ADAM_STEP)
        v_hat = v1 / (1.0 - ADAM_B2 ** ADAM_STEP)
        g_ref[...] = g
        d_ref[...] = -ADAM_LR * (m_hat / (jnp.sqrt(v_hat) + ADAM_EPS) + ADAM_WD * w_ref[...])
        nm_ref[...] = m1
        nv_ref[...] = v1
        if side:
            side.wait(grid, *side_refs)

    blk = pl.BlockSpec((tile, PACK_W), lambda i: (i, 0))
    shp = jax.ShapeDtypeStruct((rp, PACK_W), F32)
    return pl.pallas_call(
        body, name=name, grid=grid,
        in_specs=[pl.BlockSpec((nsrc, tile, PACK_W), lambda i: (0, i, 0)), blk, blk, blk] + [ANY] * n_si,
        out_specs=(blk, blk, blk, blk) + (ANY,) * n_so,
        out_shape=(shp, shp, shp, shp) + tuple(side.out_shapes if side else ()),
        scratch_shapes=side.scratch() if side else [],
        compiler_params=_params(("arbitrary",) if side else ("parallel",)),
    )(recv, w, m, v, *(side.inputs if side else ()))


BIG = {"w_ada": ((3 * D, D), 0), "pool_w": ((4, PGW, PGW), 1), "w_proj_pool": ((D, D), 0), "w_proj_ssd": ((DIN, D), 0),
       "w_out": ((D, D), 0), "w_in": ((IN_COLS, D), 0), "conv_w": ((4, CONV_DIM), 1)}
TRANSPOSED = ("w_ada", "w_in")
PACK_ROWS = {"w_ada": 384, "w_in": 1168, "conv_w": 16, "pool_w": 32, "w_proj_pool": 128, "w_proj_ssd": 256, "w_out": 128}
GATHER_EARLY = ("w_ada", "w_in", "conv_w")
GATHER_LATE = ("pool_w", "w_proj_pool", "w_proj_ssd", "w_out")
GRADS_LATE = ("w_ada",)
GRADS_EARLY = tuple(n for n in PACK_ROWS if n not in GRADS_LATE)
SMALL = {"c_ctx": (D,), "b_ada": (1, 3 * D), "norm_pre": (1, D), "norm_post": (1, D), "b_merge": (1, 2 * D),
         "pool_scale": (1, D), "conv_b": (1, CONV_DIM), "dt_bias": (2, 32), "a_log": (2, 32), "d_skip": (1, 32),
         "ssd_norm": (1, DIN)}
LOSS_SLOT = 128
assert all(_r % 16 == 0 for _r in PACK_ROWS.values())
SMALL_ROWS = 16


def _shard_shape(name):
    shape, ax = BIG[name]
    return tuple(s // NDEV if i == ax else s for i, s in enumerate(shape))


def _as_rows(t, rows):
    pad = [(0, 0)] * (t.ndim - 1) + [(0, rows * PACK_W - t.shape[-1])]
    return jnp.pad(t, pad).reshape(t.shape[:-1] + (rows, PACK_W))


def _shard_rows(t, name):
    sh, r = _shard_shape(name), PACK_ROWS[name]
    lead = t.shape[:t.ndim - len(sh)]
    if len(sh) == 2 and sh[1] == PACK_W:
        return jnp.pad(t, [(0, 0)] * len(lead) + [(0, r - sh[0]), (0, 0)])
    if int(np.prod(sh)) == r * PACK_W:
        return t.reshape(lead + (r, PACK_W))
    return _as_rows(t.reshape(lead + (-1,)), r)


def _to_chunks(full, name):
    shape, ax = BIG[name]
    split = shape[:ax] + (NDEV, shape[ax] // NDEV) + shape[ax + 1:]
    return _shard_rows(jnp.moveaxis(full.reshape(split), ax, 0), name)


def _from_chunks(chunks, name):
    shape, ax = BIG[name]
    return jnp.moveaxis(chunks.reshape((NDEV,) + _shard_shape(name)), 0, ax).reshape(shape)


def _rows_of(names):
    return sum(PACK_ROWS[n] for n in names)


def _pack_state(t, names):
    return jnp.concatenate([_shard_rows(t[n], n) for n in names], axis=0)


def _pack_small(t, loss_part=None):
    slot = jnp.zeros((LOSS_SLOT,), F32)
    if loss_part is not None:
        slot = slot.at[0].set(loss_part)
    return _as_rows(jnp.concatenate([t[n].reshape(-1) for n in SMALL] + [slot]), SMALL_ROWS)


def _pack_grads(g, names):
    big = jnp.concatenate([_to_chunks(g[n], n).astype(BF) for n in names], axis=1)
    return jnp.swapaxes(big.reshape(4, 2, _rows_of(names), PACK_W), 0, 1)


def _unpack_state(big, names):
    out, off = {}, 0
    for n in names:
        sh, r = _shard_shape(n), PACK_ROWS[n]
        k = int(np.prod(sh))
        if len(sh) == 2 and sh[1] == PACK_W:
            out[n] = big[off:off + sh[0]]
        else:
            out[n] = big[off:off + r].reshape(-1)[:k].reshape(sh)
        off += r
    return out


def _unpack_small(small):
    out, flat, off = {}, small.reshape(-1), 0
    for n, sh in SMALL.items():
        k = int(np.prod(sh))
        out[n] = flat[off:off + k].reshape(sh)
        off += k
    out["loss"] = flat[off]
    return out


def _pack_gather(w, names):
    pieces = []
    for n in names:
        if n == "conv_w":
            pieces.append(_as_rows(jnp.concatenate([p.reshape(-1) for p in _split(w[n], 3)]), PACK_ROWS[n]))
        else:
            pieces.append(_shard_rows(w[n], n).astype(BF))
    return jnp.concatenate(pieces, axis=0)


def _unpack_gather(gathered, names):
    g = gathered.reshape(NDEV, _rows_of(names), PACK_W)
    out, off = {}, 0
    for n in names:
        r = PACK_ROWS[n]
        sh = _shard_shape(n)
        if n == "conv_w":
            k = int(np.prod(sh))
            terms = g[:, off:off + r].reshape(NDEV, -1)[:, :3 * k].astype(F32).reshape(NDEV, 3, k)
            out[n] = _from_chunks(terms[:, 0] + terms[:, 1] + terms[:, 2], n)
        elif len(sh) == 2 and sh[1] == PACK_W:
            out[n] = _from_chunks(g[:, off:off + sh[0]], n)
        else:
            out[n] = _from_chunks(g[:, off:off + r], n)
        off += r
    return out


PARAMS = ["c_ctx", "w_ada", "b_ada", "norm_pre", "norm_post", "w_in", "b_merge", "pool_w", "pool_scale", "conv_w", "conv_b",
          "dt_bias", "a_log", "d_skip", "ssd_norm", "w_proj_pool", "w_proj_ssd", "w_out"]


def kernel(x, c, ctx, c_ctx, w_ada, b_ada, norm_pre, norm_post, w_in, b_merge, pool_w, pool_scale, conv_w, conv_b, dt_bias, a_log, d_skip, ssd_norm, w_proj_pool, w_proj_ssd, w_out, loss_target, m_c_ctx, m_w_ada, m_b_ada, m_norm_pre, m_norm_post, m_w_in, m_b_merge, m_pool_w, m_pool_scale, m_conv_w, m_conv_b, m_dt_bias, m_a_log, m_d_skip, m_ssd_norm, m_w_proj_pool, m_w_proj_ssd, m_w_out, v_c_ctx, v_w_ada, v_b_ada, v_norm_pre, v_norm_post, v_w_in, v_b_merge, v_pool_w, v_pool_scale, v_conv_w, v_conv_b, v_dt_bias, v_a_log, v_d_skip, v_ssd_norm, v_w_proj_pool, v_w_proj_ssd, v_w_out):
    given = dict(locals())
    shapes = {n: given[n].shape for n in PARAMS}

    def local(prefix):
        t = {n: (given[prefix + n] if n == "c_ctx" else given[prefix + n][0]) for n in PARAMS}
        for n in TRANSPOSED:
            t[n] = t[n].T
        return {n: t[n].reshape(_shard_shape(n) if n in BIG else SMALL[n]) for n in PARAMS}

    w, m, v = local(""), local("m_"), local("v_")

    W = _unpack_gather(_all_gather(_pack_gather(w, GATHER_EARLY)), GATHER_EARLY)
    for n in SMALL:
        W[n] = w[n]
    lanes, grad_x, g, recv_early = _local_step(x, c, ctx, loss_target, W, late_shard=_pack_gather(w, GATHER_LATE),
                                               exchange=True)
    gb = _pack_grads(g, GRADS_LATE)
    got, recv_small = _pair_exchange(gb, _pack_small(g, (0.5 / D) * jnp.sum(lanes)), "grads_pair_exchange_late")
    late = _chip_exchange_side(_pair_add(gb, got, "grads_pair_add_late"))
    res = [{} for _ in range(4)]
    *early, recv_late = _adamw(recv_early, *[_pack_state(s, GRADS_EARLY) for s in (w, m, v)], "adamw_early", side=late)
    for r, t in zip(res, early):
        r.update(_unpack_state(t, GRADS_EARLY))
    for r, t in zip(res, _adamw(recv_late, *[_pack_state(s, GRADS_LATE) for s in (w, m, v)], "adamw_late")):
        r.update(_unpack_state(t, GRADS_LATE))
    for r, t in zip(res, _adamw(recv_small, *[_pack_small(s) for s in (w, m, v)], "adamw_small")):
        r.update(_unpack_small(t))
    outs = [res[0]["loss"], grad_x]
    for r in res:
        for n in TRANSPOSED:
            r[n] = r[n].T
        outs += [r[n].reshape(shapes[n]) for n in PARAMS]
    return tuple(outs)
```

```python
import functools

import numpy as np
import jax
import jax.numpy as jnp
from jax import lax
from jax.experimental import pallas as pl
from jax.experimental.pallas import tpu as pltpu

F32, BF = jnp.float32, jnp.bfloat16

D = 1024
GRID_W = 64
EPS = 1e-6
POOL_WINDOWS = (2, 4, 8, 16)
PGW = 256
DIN = 2048
HEAD = 64
NST = 128
NG = 4
HPG = 8
GWID = HPG * HEAD
Q = 128
CONV_DIM = 3072
OFF_GATE, OFF_XBC, OFF_DT, IN_COLS = 4096, 6144, 9216, 9280
NDEV = 8
ADAM_LR, ADAM_B1, ADAM_B2, ADAM_EPS, ADAM_WD, ADAM_STEP = 0.001, 0.9, 0.999, 1e-08, 0.01, 10

V7X_VMEM_LIMIT = 56 * 2 ** 20
ROW_TILE = 256
X_TILE = 512


def _params(sem=None):
    return pltpu.CompilerParams(dimension_semantics=sem, vmem_limit_bytes=V7X_VMEM_LIMIT)


def _dot(a, b):
    return jnp.dot(a.astype(BF), b.astype(BF), preferred_element_type=F32)


def _dot_nt(a, b):
    return lax.dot_general(a.astype(BF), b.astype(BF), (((1,), (1,)), ((), ())), preferred_element_type=F32)


def _dot_tn(a, b):
    return lax.dot_general(a.astype(BF), b.astype(BF), (((0,), (0,)), ((), ())), preferred_element_type=F32)


def _split(a, n):
    parts = []
    for _ in range(n):
        p = a.astype(BF)
        parts.append(p)
        a = a - p.astype(F32)
    return parts


def _dot_sl(a, b01, n=3):
    parts = _split(a, n)
    m = a.shape[0]
    if n == 1 or m % 16:
        return sum(jnp.dot(p, b01, preferred_element_type=F32) for p in parts)
    r = jnp.dot(jnp.concatenate(parts, axis=0), b01, preferred_element_type=F32)
    return sum(r[i * m:(i + 1) * m] for i in range(n))


def _dot_sr(a01, b, n=3):
    parts = _split(b, n)
    k = b.shape[1]
    if n == 1 or k % 128:
        return sum(jnp.dot(a01, p, preferred_element_type=F32) for p in parts)
    r = jnp.dot(a01, jnp.concatenate(parts, axis=1), preferred_element_type=F32)
    return sum(r[:, i * k:(i + 1) * k] for i in range(n))


def _sigmoid(x):
    return 1.0 / (1.0 + jnp.exp(-x))


class _SideCopies:
    NSEM = 8

    def __init__(self, inputs, out_shapes, make):
        self.inputs, self.out_shapes, self.make = list(inputs), list(out_shapes), make

    def scratch(self):
        return [pltpu.SemaphoreType.DMA((self.NSEM,)), pltpu.SemaphoreType.DMA((self.NSEM,)), pltpu.SemaphoreType.DMA]

    def start(self, grid, in_refs, out_refs, sems):
        @pl.when(functools.reduce(lambda p, q: p & q, [pl.program_id(i) == 0 for i in range(len(grid))]))
        def _():
            local, sends, _ = self.make(in_refs, out_refs, *sems, arrivals=False)
            for cp in local + sends:
                cp.start()

    def wait(self, grid, in_refs, out_refs, sems):
        @pl.when(functools.reduce(lambda p, q: p & q, [pl.program_id(i) == n - 1 for i, n in enumerate(grid)]))
        def _():
            local, sends, recvs = self.make(in_refs, out_refs, *sems)
            for cp in sends:
                cp.wait_send()
            for cp in recvs:
                cp.wait_recv()
            for cp in local:
                cp.wait()


def _matmul(a, b, out_dtype, name, tm=512, tn=512, tk=1024, bt=False, n=None, side=None):
    M, K = a.shape
    N = n if n is not None else (b.shape[0] if bt else b.shape[1])
    tm, tn, tk = min(tm, M), min(tn, N), min(tk, K)
    assert M % tm == 0 and N % tn == 0 and K % tk == 0, (a.shape, b.shape)
    nk = K // tk
    grid = (M // tm, N // tn, nk)
    n_si, n_so = (len(side.inputs), len(side.out_shapes)) if side else (0, 0)

    def body(*refs):
        a_ref, b_ref, o_ref = refs[0], refs[1], refs[2 + n_si]
        acc = refs[3 + n_si + n_so]
        side_refs = (refs[2:2 + n_si], refs[3 + n_si:3 + n_si + n_so], refs[4 + n_si + n_so:])
        if side:
            side.start(grid, *side_refs)
        k = pl.program_id(2)
        p = _dot_nt(a_ref[...], b_ref[...]) if bt else _dot(a_ref[...], b_ref[...])

        @pl.when(k == 0)
        def _():
            acc[...] = p

        @pl.when(k > 0)
        def _():
            acc[...] += p

        @pl.when(k == nk - 1)
        def _():
            o_ref[...] = acc[...].astype(o_ref.dtype)

        if side:
            side.wait(grid, *side_refs)

    out = pl.pallas_call(
        body, name=name, grid=grid,
        in_specs=[pl.BlockSpec((tm, tk), lambda i, j, k: (i, k)),
                  pl.BlockSpec((tn, tk), lambda i, j, k: (j, k)) if bt else pl.BlockSpec((tk, tn), lambda i, j, k: (k, j))]
        + [ANY] * n_si,
        out_specs=(pl.BlockSpec((tm, tn), lambda i, j, k: (i, j)),) + (ANY,) * n_so,
        out_shape=(jax.ShapeDtypeStruct((M, N), out_dtype),) + tuple(side.out_shapes if side else ()),
        scratch_shapes=[pltpu.VMEM((tm, tn), F32)] + (side.scratch() if side else []),
        compiler_params=_params(("arbitrary",) * 3 if side else ("parallel", "parallel", "arbitrary")),
    )(a, b, *(side.inputs if side else ()))
    return out if side else out[0]


def _matmul_tn(a, g, name, ta=512, tn=512, tr=512):
    M, Ka = a.shape
    N = g.shape[1]
    ta, tn, tr = min(ta, Ka), min(tn, N), min(tr, M)
    assert M % tr == 0 and N % tn == 0 and Ka % ta == 0, (a.shape, g.shape)
    nr = M // tr

    def body(a_ref, g_ref, o_ref):
        k = pl.program_id(2)
        p = _dot_tn(a_ref[...], g_ref[...])

        @pl.when(k == 0)
        def _():
            o_ref[...] = p

        @pl.when(k > 0)
        def _():
            o_ref[...] += p

    return pl.pallas_call(
        body, name=name, grid=(Ka // ta, N // tn, nr),
        in_specs=[pl.BlockSpec((tr, ta), lambda i, j, k: (k, i)), pl.BlockSpec((tr, tn), lambda i, j, k: (k, j))],
        out_specs=pl.BlockSpec((ta, tn), lambda i, j, k: (i, j)),
        out_shape=jax.ShapeDtypeStruct((Ka, N), F32),
        compiler_params=_params(("parallel", "parallel", "arbitrary")),
    )(a, g)


def _dhx(pieces, ddt, w_inT, w_dtT, side=None):
    _, R, _ = pieces[0].shape
    tm = R // 4
    kb = 1024
    starts, nblk = [], []
    for p in pieces:
        starts.append(sum(nblk))
        nblk.append(p.shape[2] // kb)
    nk = sum(nblk)
    assert nk * kb == OFF_DT and R % 128 == 0
    npc = len(pieces)
    grid = (2, R // tm, nk)
    n_si, n_so = (len(side.inputs), len(side.out_shapes)) if side else (0, 0)

    def body(*refs):
        a_refs, dt_ref, w_ref, wdt_ref = refs[:npc], refs[npc], refs[npc + 1], refs[npc + 2]
        o_ref, acc = refs[npc + 3 + n_si], refs[npc + 4 + n_si + n_so]
        side_refs = (refs[npc + 3:npc + 3 + n_si], refs[npc + 4 + n_si:npc + 4 + n_si + n_so], refs[npc + 5 + n_si + n_so:])
        if side:
            side.start(grid, *side_refs)
        k = pl.program_id(2)

        @pl.when(k == 0)
        def _():
            acc[...] = _dot(dt_ref[0], wdt_ref[...])

        for p in range(npc):
            @pl.when((k >= starts[p]) & (k < starts[p] + nblk[p]))
            def _(p=p):
                acc[...] += _dot(a_refs[p][0], w_ref[...])

        @pl.when(k == nk - 1)
        def _():
            o_ref[0] = acc[...].astype(BF)

        if side:
            side.wait(grid, *side_refs)

    in_specs = [pl.BlockSpec((1, tm, kb), functools.partial(
        lambda e, t, k, s, nb: (e, t, jnp.clip(k - s, 0, nb - 1)), s=starts[p], nb=nblk[p])) for p in range(npc)]
    in_specs += [pl.BlockSpec((1, tm, 128), lambda e, t, k: (e, t, 0)),
                 pl.BlockSpec((kb, D), lambda e, t, k: (k, 0)),
                 pl.BlockSpec((128, D), lambda e, t, k: (0, 0))]
    out = pl.pallas_call(
        body, name="d_hx", grid=grid, in_specs=in_specs + [ANY] * n_si,
        out_specs=(pl.BlockSpec((1, tm, D), lambda e, t, k: (e, t, 0)),) + (ANY,) * n_so,
        out_shape=(jax.ShapeDtypeStruct((2, R, D), BF),) + tuple(side.out_shapes if side else ()),
        scratch_shapes=[pltpu.VMEM((tm, D), F32)] + (side.scratch() if side else []),
        compiler_params=_params(("arbitrary",) * 3 if side else ("parallel", "parallel", "arbitrary")),
    )(*pieces, ddt, w_inT, w_dtT, *(side.inputs if side else ()))
    return out if side else out[0]


def _adaln_fwd(c16, w_adaT_bf, b_ada):
    def body(c_ref, w_ref, b_ref, o_ref):
        cc = c_ref[...]
        o_ref[...] = _dot_nt(cc * _sigmoid(cc), w_ref[...]) + b_ref[...]

    return pl.pallas_call(body, name="adaln_fwd", out_shape=jax.ShapeDtypeStruct((16, 3 * D), F32),
                          compiler_params=_params())(c16, w_adaT_bf, b_ada)


def _adaln_bwd(acc_n, acc_f, mod16, c16, norm_pre, w_adaT_bf):
    def body(an_ref, af_ref, mod_ref, c_ref, np_ref, wt_ref, dw_ref, db_ref, sm_ref, dmod):
        npre = np_ref[...]
        dmod[...] = jnp.zeros_like(dmod)
        dnp = jnp.zeros((1, D), F32)
        dshift_c = jnp.zeros((1, D), F32)
        dgpre_c = jnp.zeros((1, D), F32)
        scale_c = mod_ref[2:3, D:2 * D]
        for e in range(2):
            dg_x, ds_x = an_ref[e, 0, 0:1, :], an_ref[e, 0, 1:2, :]
            dg_c, ds_c = an_ref[e, 1, 0:1, :], an_ref[e, 1, 1:2, :]
            dmod[e:e + 1, 0:D] = ds_x
            dmod[e:e + 1, D:2 * D] = dg_x * npre
            dmod[e:e + 1, 2 * D:3 * D] = af_ref[e, 0:1, :]
            dnp = dnp + dg_x * (1.0 + mod_ref[e:e + 1, D:2 * D]) + dg_c * (1.0 + scale_c)
            dshift_c = dshift_c + ds_c
            dgpre_c = dgpre_c + dg_c
        dmod[2:3, 0:D] = dshift_c
        dmod[2:3, D:2 * D] = dgpre_c * npre
        dm = dmod[...]
        cc = c_ref[...]
        sg = _sigmoid(cc)
        dw_ref[...] = _dot_tn(dm, cc * sg)
        db_ref[...] = jnp.zeros_like(db_ref)
        db_ref[0:1, :] = jnp.sum(dm, axis=0, keepdims=True)
        dsilu = sg * (1.0 + cc * (1.0 - sg))
        dcs = _dot(dm, wt_ref[...]) * dsilu
        sm_ref[...] = jnp.zeros_like(sm_ref)
        sm_ref[0:1, :] = dnp
        sm_ref[1:2, :] = dcs[2:3, :]

    return pl.pallas_call(
        body, name="adaln_bwd",
        out_shape=(jax.ShapeDtypeStruct((3 * D, D), F32), jax.ShapeDtypeStruct((16, 3 * D), F32),
                   jax.ShapeDtypeStruct((8, D), F32)),
        scratch_shapes=[pltpu.VMEM((16, 3 * D), F32)],
        compiler_params=_params())(acc_n, acc_f, mod16, c16, norm_pre, w_adaT_bf)


def _row_specs(L):
    nx = L // ROW_TILE
    return (pl.BlockSpec((1, ROW_TILE, D), lambda e, t: (e, jnp.minimum(t, nx - 1), 0)),
            pl.BlockSpec((1, ROW_TILE, D), lambda e, t: (e, jnp.maximum(t - nx, 0), 0)))


def _norm_mod_fwd(x, ctx, tab):
    L = x.shape[1]
    R = L + ctx.shape[1]
    nx = L // ROW_TILE

    def body(x_ref, c_ref, t_ref, o_ref):
        t = t_ref[0, 0]
        gain, shift = t[0:1], t[1:2]

        def run(src):
            for r0 in range(0, ROW_TILE, 32):
                x = src[0, pl.ds(r0, 32), :]
                r = lax.rsqrt(jnp.mean(x * x, axis=-1, keepdims=True) + EPS)
                o_ref[0, pl.ds(r0, 32), :] = (x * r * gain + shift).astype(BF)

        @pl.when(pl.program_id(1) < nx)
        def _():
            run(x_ref)

        @pl.when(pl.program_id(1) >= nx)
        def _():
            run(c_ref)

    return pl.pallas_call(
        body, name="norm_mod_fwd", grid=(2, R // ROW_TILE),
        in_specs=[*_row_specs(L), pl.BlockSpec((1, 1, 8, D), lambda e, t: (e, t // nx, 0, 0))],
        out_specs=pl.BlockSpec((1, ROW_TILE, D), lambda e, t: (e, t, 0)),
        out_shape=jax.ShapeDtypeStruct((2, R, D), BF),
        compiler_params=_params(("parallel", "parallel")),
    )(x, ctx, tab)


def _norm_mod_bwd(dh, x, ctx, tab, dxo):
    L = x.shape[1]
    R = L + ctx.shape[1]
    nx = L // ROW_TILE

    def body(dh_ref, x_ref, c_ref, t_ref, dxo_ref, gx_ref, acc_ref):
        t = pl.program_id(1)
        x = jnp.where(t < nx, x_ref[0], c_ref[0])
        r = lax.rsqrt(jnp.mean(x * x, axis=-1, keepdims=True) + EPS)
        xn = x * r
        dh = dh_ref[0].astype(F32)

        @pl.when((t == 0) | (t == nx))
        def _():
            acc_ref[...] = jnp.zeros_like(acc_ref)

        acc_ref[0, 0, 0:1, :] += jnp.sum(dh * xn, axis=0, keepdims=True)
        acc_ref[0, 0, 1:2, :] += jnp.sum(dh, axis=0, keepdims=True)

        @pl.when(t < nx)
        def _():
            dxn = dh * t_ref[0, 0][0:1]
            dx = r * (dxn - xn * jnp.mean(dxn * xn, axis=-1, keepdims=True))
            gx_ref[0] = dxo_ref[0].astype(F32) + dx

    xspec, cspec = _row_specs(L)
    return pl.pallas_call(
        body, name="norm_mod_bwd", grid=(2, R // ROW_TILE),
        in_specs=[pl.BlockSpec((1, ROW_TILE, D), lambda e, t: (e, t, 0)), xspec, cspec,
                  pl.BlockSpec((1, 1, 8, D), lambda e, t: (e, t // nx, 0, 0)), xspec],
        out_specs=(xspec, pl.BlockSpec((1, 1, 8, D), lambda e, t: (e, t // nx, 0, 0))),
        out_shape=(jax.ShapeDtypeStruct((2, L, D), F32), jax.ShapeDtypeStruct((2, 2, 8, D), F32)),
        compiler_params=_params(("parallel", "arbitrary")),
    )(dh, x, ctx, tab, dxo)


POOL_TILE = 256


def _pool_tables(L):
    mats = np.zeros((4, POOL_TILE, POOL_TILE), np.float32)
    for gi, k in enumerate(POOL_WINDOWS):
        lo, hi = k // 2, k - 1 - k // 2
        m = np.zeros((GRID_W, GRID_W), np.float32)
        for t in range(GRID_W):
            m[t, max(t - lo, 0):min(t + hi, GRID_W - 1) + 1] = 1.0
        for b in range(POOL_TILE // GRID_W):
            mats[gi, b * GRID_W:(b + 1) * GRID_W, b * GRID_W:(b + 1) * GRID_W] = m
    matsT = np.ascontiguousarray(np.transpose(mats, (0, 2, 1)))
    return (jnp.asarray(mats, BF), jnp.asarray(matsT, BF))


def _pool_cols(get_tile, mat, cs_ref, L, n):
    def step(i, carry):
        off = pl.multiple_of(i * POOL_TILE, POOL_TILE)
        t = get_tile(off)
        cs_ref[pl.ds(GRID_W + off, POOL_TILE), :] = (jnp.dot(mat, t.astype(BF), preferred_element_type=F32) if n == 1
                                                     else _dot_sr(mat, t.astype(F32), n))
        return carry

    lax.fori_loop(0, L // POOL_TILE, step, 0)
    cs_ref[pl.ds(0, GRID_W), :] = jnp.zeros((GRID_W, PGW), F32)

    def prefix(r, carry):
        o = pl.multiple_of(r * GRID_W, GRID_W)
        cs_ref[pl.ds(o + GRID_W, GRID_W), :] = cs_ref[pl.ds(o + GRID_W, GRID_W), :] + cs_ref[pl.ds(o, GRID_W), :]
        return carry

    lax.fori_loop(0, L // GRID_W, prefix, 0)


def _pool_rows(cs_ref, off, below, above, L):
    rows = L // GRID_W
    r0 = off // GRID_W
    parts = []
    for i in range(POOL_TILE // GRID_W):
        hi = pl.multiple_of(jnp.minimum(r0 + i + above + 1, rows) * GRID_W, GRID_W)
        lo = pl.multiple_of(jnp.maximum(r0 + i - below, 0) * GRID_W, GRID_W)
        parts.append(cs_ref[pl.ds(hi, GRID_W), :] - cs_ref[pl.ds(lo, GRID_W), :])
    return jnp.concatenate(parts, axis=0)


PGS = 2


def _inv_count(off, half, L):
    t = off + lax.broadcasted_iota(jnp.int32, (POOL_TILE, 1), 0)
    r, c = jnp.right_shift(t, 6), t & (GRID_W - 1)
    cr = jnp.minimum(r + half - 1, L // GRID_W - 1) - jnp.maximum(r - half, 0) + 1
    cc = jnp.minimum(c + half - 1, GRID_W - 1) - jnp.maximum(c - half, 0) + 1
    return 1.0 / (cr * cc).astype(F32)


def _pool_fwd(proj3, pool_w_bf, pool_scale, tables, L):
    mats, _ = tables
    nt = L // POOL_TILE

    def body(v_ref, z_ref, pw_ref, ps_ref, m_ref, o_ref, cs_ref):
        for j in range(PGS):
            cols = slice(j * PGW, (j + 1) * PGW)
            _pool_cols(lambda off: v_ref[0, pl.ds(off, POOL_TILE), cols], m_ref[j], cs_ref, L, 1)
            half = lax.shift_left(1, PGS * pl.program_id(1) + j)

            def step(i, carry, j=j, cols=cols, half=half):
                off = pl.multiple_of(i * POOL_TILE, POOL_TILE)
                rows = pl.ds(off, POOL_TILE)
                v = v_ref[0, rows, cols].astype(F32)
                diff = _pool_rows(cs_ref, off, half, half - 1, L) * _inv_count(off, half, L) - v
                yp = _dot(diff, pw_ref[j])
                z = z_ref[0, rows, cols].astype(F32)
                o_ref[0, rows, cols] = (yp * ps_ref[:, cols] * (z * _sigmoid(z))).astype(BF)
                return carry

            lax.fori_loop(0, nt, step, 0)

    wide = PGS * PGW
    return pl.pallas_call(
        body, name="pool_fwd", grid=(2, 4 // PGS),
        in_specs=[pl.BlockSpec((1, L, wide), lambda e, g: (e, 0, g)),
                  pl.BlockSpec((1, L, wide), lambda e, g: (e, 0, 4 // PGS + g)),
                  pl.BlockSpec((PGS, PGW, PGW), lambda e, g: (g, 0, 0)),
                  pl.BlockSpec((1, wide), lambda e, g: (0, g)),
                  pl.BlockSpec((PGS, POOL_TILE, POOL_TILE), lambda e, g: (g, 0, 0))],
        out_specs=pl.BlockSpec((1, L, wide), lambda e, g: (e, 0, g)),
        out_shape=jax.ShapeDtypeStruct((2, L, D), BF),
        scratch_shapes=[pltpu.VMEM((L + GRID_W, PGW), F32)],
        compiler_params=_params(("parallel", "parallel")),
    )(proj3, proj3, pool_w_bf, pool_scale, mats)


def _pool_bwd(proj3, d_ypool, pool_w_bf, pool_wT_bf, pool_scale, tables, L):
    mats, matsT = tables
    nt = L // POOL_TILE
    R = proj3.shape[1]

    def body(v_ref, z_ref, dy_ref, pw_ref, pwt_ref, ps_ref, m_ref, mt_ref,
             dv_ref, dz_ref, dpw_ref, acc_ref, cs_ref, dd_ref):
        e = pl.program_id(1)

        @pl.when(e == 0)
        def _():
            dpw_ref[...] = jnp.zeros_like(dpw_ref)
            acc_ref[...] = jnp.zeros_like(acc_ref)

        for j in range(PGS):
            cols = slice(j * PGW, (j + 1) * PGW)
            _pool_cols(lambda off: v_ref[0, pl.ds(off, POOL_TILE), cols], m_ref[j], cs_ref, L, 1)
            half = lax.shift_left(1, PGS * pl.program_id(0) + j)
            ps = ps_ref[:, cols]

            def step(i, carry, j=j, cols=cols, half=half, ps=ps):
                off = pl.multiple_of(i * POOL_TILE, POOL_TILE)
                rows = pl.ds(off, POOL_TILE)
                v = v_ref[0, rows, cols].astype(F32)
                diff = _pool_rows(cs_ref, off, half, half - 1, L) * _inv_count(off, half, L) - v
                yp = _dot(diff, pw_ref[j])
                z = z_ref[0, rows, cols].astype(F32)
                sg = _sigmoid(z)
                sz = z * sg
                dy = dy_ref[0, rows, cols].astype(F32)
                dz_ref[0, rows, cols] = (dy * yp * ps * (sg * (1.0 + z * (1.0 - sg)))).astype(BF)
                dys = dy * sz
                acc_ref[j, 0:1, :] += jnp.sum(dys * yp, axis=0, keepdims=True)
                dyp = dys * ps
                dpw_ref[j] += _dot_tn(diff, dyp)
                dd_ref[rows, :] = _dot(dyp, pwt_ref[j])
                return carry

            lax.fori_loop(0, nt, step, 0)
            _pool_cols(lambda off, half=half: dd_ref[pl.ds(off, POOL_TILE), :] * _inv_count(off, half, L),
                       mt_ref[j], cs_ref, L, 1)

            def step2(i, carry, cols=cols, half=half):
                off = pl.multiple_of(i * POOL_TILE, POOL_TILE)
                rows = pl.ds(off, POOL_TILE)
                dv_ref[0, rows, cols] = (_pool_rows(cs_ref, off, half - 1, half, L) - dd_ref[rows, :]).astype(BF)
                return carry

            lax.fori_loop(0, nt, step2, 0)
        dv_ref[0, pl.ds(L, R - L), :] = jnp.zeros((R - L, PGS * PGW), BF)
        dz_ref[0, pl.ds(L, R - L), :] = jnp.zeros((R - L, PGS * PGW), BF)

    wide = PGS * PGW
    return pl.pallas_call(
        body, name="pool_bwd", grid=(4 // PGS, 2),
        in_specs=[pl.BlockSpec((1, L, wide), lambda g, e: (e, 0, g)),
                  pl.BlockSpec((1, L, wide), lambda g, e: (e, 0, 4 // PGS + g)),
                  pl.BlockSpec((1, L, wide), lambda g, e: (e, 0, g)),
                  pl.BlockSpec((PGS, PGW, PGW), lambda g, e: (g, 0, 0)),
                  pl.BlockSpec((PGS, PGW, PGW), lambda g, e: (g, 0, 0)),
                  pl.BlockSpec((1, wide), lambda g, e: (0, g)),
                  pl.BlockSpec((PGS, POOL_TILE, POOL_TILE), lambda g, e: (g, 0, 0)),
                  pl.BlockSpec((PGS, POOL_TILE, POOL_TILE), lambda g, e: (g, 0, 0))],
        out_specs=(pl.BlockSpec((1, R, wide), lambda g, e: (e, 0, g)),
                   pl.BlockSpec((1, R, wide), lambda g, e: (e, 0, g)),
                   pl.BlockSpec((PGS, PGW, PGW), lambda g, e: (g, 0, 0)),
                   pl.BlockSpec((PGS, 8, PGW), lambda g, e: (g, 0, 0))),
        out_shape=(jax.ShapeDtypeStruct((2, R, D), BF), jax.ShapeDtypeStruct((2, R, D), BF),
                   jax.ShapeDtypeStruct((4, PGW, PGW), F32), jax.ShapeDtypeStruct((4, 8, PGW), F32)),
        scratch_shapes=[pltpu.VMEM((L + GRID_W, PGW), F32), pltpu.VMEM((L, PGW), F32)],
        compiler_params=_params(("parallel", "arbitrary")),
    )(proj3, proj3, d_ypool, pool_w_bf, pool_wT_bf, pool_scale, mats, matsT)


CONV_BLOCK = 128


CONV_CHUNK = 64
CONV_HALO = 8


def _halo_buf_init(buf, val, R):
    z = jnp.zeros((CONV_HALO, CONV_BLOCK), F32)
    buf[pl.ds(0, CONV_HALO), :] = z
    buf[pl.ds(CONV_HALO + R, CONV_HALO), :] = z
    if val is not None:
        buf[pl.ds(CONV_HALO, R), :] = val


def _chunk_taps(buf, start, offs, L):
    n = CONV_CHUNK + 2 * CONV_HALO
    ext = buf[pl.ds(start, n), :]
    out = []
    for off in offs:
        if off == 0:
            out.append(ext[CONV_HALO:CONV_HALO + CONV_CHUNK])
            continue
        r = pltpu.roll(ext, (-off) % n, 0)[CONV_HALO:CONV_HALO + CONV_CHUNK]
        lo, hi = (start, start + CONV_CHUNK - 1 + off) if off > 0 else (start + off, start + CONV_CHUNK - 1)
        if lo < L <= hi:
            t = start + lax.broadcasted_iota(jnp.int32, (CONV_CHUNK, 1), 0)
            r = jnp.where((t < L) == (t + off < L), r, 0.0)
        out.append(r)
    return out


def _fold8(x):
    return sum(x[i * 8:(i + 1) * 8] for i in range(CONV_CHUNK // 8))


def _conv_fwd(proj3, conv_w, conv_b, L):
    _, R, _ = proj3.shape
    cb0 = OFF_XBC // CONV_BLOCK

    def body(u_ref, w_ref, b_ref, o_ref, ubuf):
        _halo_buf_init(ubuf, u_ref[0].astype(F32), R)
        w = w_ref[...]
        b = b_ref[...]
        for start in range(0, R, CONV_CHUNK):
            taps = _chunk_taps(ubuf, start, (-2, -1, 0, 1), L)
            pre = b + sum(taps[k] * w[k:k + 1, :] for k in range(4))
            o_ref[0, pl.ds(start, CONV_CHUNK), :] = (pre * _sigmoid(pre)).astype(BF)

    return pl.pallas_call(
        body, name="conv_fwd", grid=(2, CONV_DIM // CONV_BLOCK),
        in_specs=[pl.BlockSpec((1, R, CONV_BLOCK), lambda e, j: (e, 0, cb0 + j)),
                  pl.BlockSpec((4, CONV_BLOCK), lambda e, j: (0, j)),
                  pl.BlockSpec((1, CONV_BLOCK), lambda e, j: (0, j))],
        out_specs=pl.BlockSpec((1, R, CONV_BLOCK), lambda e, j: (e, 0, j)),
        out_shape=jax.ShapeDtypeStruct((2, R, CONV_DIM), BF),
        scratch_shapes=[pltpu.VMEM((R + 2 * CONV_HALO, CONV_BLOCK), F32)],
        compiler_params=_params(("parallel", "parallel")),
    )(proj3, conv_w, conv_b)


def _conv_bwd(proj3, addends, scales, col0, ncols, in_maps, conv_w, conv_b, L, name):
    _, R, _ = proj3.shape
    cb0 = (OFF_XBC + col0) // CONV_BLOCK
    wb0 = col0 // CONV_BLOCK
    na = len(addends)
    scaled = [i for i in range(na) if scales[i] is not None]

    def body(*refs):
        u_ref, w_ref, b_ref = refs[0], refs[1], refs[2]
        a_refs = refs[3:3 + na]
        s_refs = dict(zip(scaled, refs[3 + na:3 + na + len(scaled)]))
        o_ref, acc_ref, ubuf, dbuf = refs[3 + na + len(scaled):]
        _halo_buf_init(ubuf, u_ref[0].astype(F32), R)
        _halo_buf_init(dbuf, None, R)
        w = w_ref[...]
        b = b_ref[...]
        scl = {i: s_refs[i][...] for i in scaled}
        sums = [jnp.zeros((8, CONV_BLOCK), F32) for _ in range(5)]
        for start in range(0, R, CONV_CHUNK):
            rows = pl.ds(start, CONV_CHUNK)
            taps = _chunk_taps(ubuf, start, (-2, -1, 0, 1), L)
            pre = b + sum(taps[k] * w[k:k + 1, :] for k in range(4))
            sg = _sigmoid(pre)
            dxbc = None
            for i, a in enumerate(a_refs):
                t = a[0, rows, :].astype(F32)
                t = t * scl[i] if i in scl else t
                dxbc = t if dxbc is None else dxbc + t
            dpre = dxbc * (sg * (1.0 + pre * (1.0 - sg)))
            dbuf[pl.ds(start + CONV_HALO, CONV_CHUNK), :] = dpre
            for k in range(4):
                sums[k] = sums[k] + _fold8(dpre * taps[k])
            sums[4] = sums[4] + _fold8(dpre)
        acc_ref[...] = jnp.zeros_like(acc_ref)
        for k in range(5):
            acc_ref[0, k:k + 1, :] = jnp.sum(sums[k], axis=0, keepdims=True)
        for start in range(0, R, CONV_CHUNK):
            d = _chunk_taps(dbuf, start, (2, 1, 0, -1), L)
            o_ref[0, pl.ds(start, CONV_CHUNK), :] = sum(d[k] * w[k:k + 1, :] for k in range(4)).astype(BF)

    in_specs = [pl.BlockSpec((1, R, CONV_BLOCK), lambda e, j: (e, 0, cb0 + j)),
                pl.BlockSpec((4, CONV_BLOCK), lambda e, j: (0, wb0 + j)),
                pl.BlockSpec((1, CONV_BLOCK), lambda e, j: (0, wb0 + j))]
    for m in in_maps:
        in_specs.append(pl.BlockSpec((1, R, CONV_BLOCK), functools.partial(lambda e, j, m: (e, 0, m(j)), m=m)))
    for i in scaled:
        in_specs.append(pl.BlockSpec((1, CONV_BLOCK), functools.partial(lambda e, j, m: (0, m(j)), m=in_maps[i])))
    return pl.pallas_call(
        body, name=name, grid=(2, ncols // CONV_BLOCK),
        in_specs=in_specs,
        out_specs=(pl.BlockSpec((1, R, CONV_BLOCK), lambda e, j: (e, 0, j)),
                   pl.BlockSpec((1, 8, CONV_BLOCK), lambda e, j: (e, 0, j))),
        out_shape=(jax.ShapeDtypeStruct((2, R, ncols), BF), jax.ShapeDtypeStruct((2, 8, ncols), F32)),
        scratch_shapes=[pltpu.VMEM((R + 2 * CONV_HALO, CONV_BLOCK), F32)] * 2,
        compiler_params=_params(("parallel", "parallel")),
    )(proj3, conv_w, conv_b, *addends, *[scales[i] for i in scaled])


def _softplus(x):
    e = jnp.exp(-jnp.abs(x))
    u = 1.0 + e
    return jnp.maximum(x, 0.0) + jnp.where(u == 1.0, e, e * jnp.log(u) / (u - 1.0))


def _to_local_mat(g, transpose=False):
    r = lax.broadcasted_iota(jnp.int32, (128, 128), 1 if transpose else 0)
    c = lax.broadcasted_iota(jnp.int32, (128, 128), 0 if transpose else 1)
    return ((c < 2 * HPG) & (r == jnp.right_shift(c, 3) * (NG * HPG) + g * HPG + (c & (HPG - 1)))).astype(BF)


def _dt_fwd(dt_raw, bias128):
    _, R, _ = dt_raw.shape

    def body(x_ref, b_ref, o_ref):
        dt = _softplus(x_ref[0] + b_ref[...])
        for g in range(NG):
            o_ref[0, g] = _dot_sl(dt, _to_local_mat(g))

    tr = R // 4
    return pl.pallas_call(
        body, name="dt_fwd", grid=(2, 4),
        in_specs=[pl.BlockSpec((1, tr, 128), lambda e, t: (e, t, 0)), pl.BlockSpec((1, 128), lambda e, t: (0, 0))],
        out_specs=pl.BlockSpec((1, NG, tr, 128), lambda e, t: (e, 0, t, 0)),
        out_shape=jax.ShapeDtypeStruct((2, NG, R, 128), F32),
        compiler_params=_params(("parallel", "parallel")),
    )(dt_raw, bias128)


def _dt_bwd(dt_raw, bias128, ddt_f, ddt_b):
    _, R, _ = dt_raw.shape

    def body(x_ref, b_ref, f_ref, g_ref, o_ref, acc_ref):
        ddt = sum(_dot_sl(f_ref[0, g] + g_ref[0, g], _to_local_mat(g, transpose=True)) for g in range(NG))
        d = ddt * _sigmoid(x_ref[0] + b_ref[...])
        o_ref[0] = d.astype(BF)

        @pl.when(pl.program_id(1) == 0)
        def _():
            acc_ref[...] = jnp.zeros_like(acc_ref)

        acc_ref[0, 0:1, :] += jnp.sum(d, axis=0, keepdims=True)

    tr = R // 4
    blk = pl.BlockSpec((1, tr, 128), lambda e, t: (e, t, 0))
    loc = pl.BlockSpec((1, NG, tr, 128), lambda e, t: (e, 0, t, 0))
    return pl.pallas_call(
        body, name="dt_bwd", grid=(2, 4),
        in_specs=[blk, pl.BlockSpec((1, 128), lambda e, t: (0, 0)), loc, loc],
        out_specs=(blk, pl.BlockSpec((1, 8, 128), lambda e, t: (e, 0, 0))),
        out_shape=(jax.ShapeDtypeStruct(dt_raw.shape, BF), jax.ShapeDtypeStruct((2, 8, 128), F32)),
        compiler_params=_params(("parallel", "arbitrary")),
    )(dt_raw, bias128, ddt_f, ddt_b)


GPS = 4


def _tri(d):
    i = lax.broadcasted_iota(jnp.int32, (Q, Q), 0)
    j = lax.broadcasted_iota(jnp.int32, (Q, Q), 1)
    return (i >= j) if d == 0 else (i <= j)


def _expand_mat(d):
    r = lax.broadcasted_iota(jnp.int32, (128, GWID), 0)
    c = lax.broadcasted_iota(jnp.int32, (128, GWID), 1)
    return (r == d * HPG + jnp.right_shift(c, 6)).astype(BF)


def _reduce_mat(d):
    r = lax.broadcasted_iota(jnp.int32, (GWID, 128), 0)
    c = lax.broadcasted_iota(jnp.int32, (GWID, 128), 1)
    return (c == d * HPG + jnp.right_shift(r, 6)).astype(BF)


def _ssd_chunk(d, dt, A, xs, B, C):
    mask = _tri(d)
    T = mask.astype(BF)
    Tt = _tri(1 - d).astype(BF)
    a = dt * A
    acs = _dot_sr(T, a)
    E = _expand_mat(d)
    dt_e = _dot_sl(dt, E, 2)
    acs_e = _dot_sl(acs, E, 2)
    alast_e = acs_e[Q - 1:Q, :] if d == 0 else acs_e[0:1, :]
    return dict(mask=mask, T=T, Tt=Tt, acs=acs, acsT=acs.T, dt_e=dt_e, acs_e=acs_e, lam=jnp.exp(acs_e),
                w=jnp.exp(alast_e - acs_e), decay=jnp.exp(alast_e), xt=xs * dt_e, CB=_dot_nt(C, B))


def _head_decay(q, d, hh):
    col = q["acs"][:, d * HPG + hh:d * HPG + hh + 1]
    row = q["acsT"][d * HPG + hh:d * HPG + hh + 1, :]
    return jnp.exp(jnp.where(q["mask"], col - row, -jnp.inf))


def _chunk_maps(NX, NS):
    cf = lambda s: lax.rem(s + NX, NS)
    cb = lambda s: NS - 1 - s
    return cf, cb


def _ssd_fwd(xbc, dt_loc, a_loc, L):
    _, R, _ = xbc.shape
    NX, NS = L // Q, R // Q
    cf, cb = _chunk_maps(NX, NS)

    def body(xs_f, b_f, c_f, dt_f, xs_b, b_b, c_b, dt_b, a_ref, y_f, hs_f, y_b, hs_b, hT):
        @pl.when(pl.program_id(2) == 0)
        def _():
            hT[...] = jnp.zeros_like(hT)

        lane = lax.broadcasted_iota(jnp.int32, (Q, 128), 1)
        for d, (xs_ref, b_ref, c_ref, dt_ref, y_ref, hs_ref) in enumerate(
                ((xs_f, b_f, c_f, dt_f, y_f, hs_f), (xs_b, b_b, c_b, dt_b, y_b, hs_b))):
            for gi in range(GPS):
                cols = slice(gi * GWID, (gi + 1) * GWID)
                xs = xs_ref[0, :, cols].astype(F32)
                B, C = b_ref[0, :, gi * NST:(gi + 1) * NST], c_ref[0, :, gi * NST:(gi + 1) * NST]
                q = _ssd_chunk(d, dt_ref[0, gi], a_ref[gi, 0:1, :], xs, B, C)
                h = hT[d, :, cols]
                hb = h.astype(BF)
                hs_ref[0, 0, :, cols] = hb
                parts = []
                for pr in range(HPG // 2):
                    xp = q["xt"][:, pr * 128:(pr + 1) * 128]
                    xst = jnp.concatenate([jnp.where(lane < HEAD, xp, 0.0), jnp.where(lane < HEAD, 0.0, xp)], axis=0)
                    mst = jnp.concatenate([(q["CB"] * _head_decay(q, d, 2 * pr)).astype(BF),
                                           (q["CB"] * _head_decay(q, d, 2 * pr + 1)).astype(BF)], axis=1)
                    parts.append(_dot(mst, xst))
                y_ref[0, :, cols] = jnp.concatenate(parts, axis=1) + _dot(C, hb) * q["lam"]
                hT[d, :, cols] = q["decay"] * h + _dot_tn(B, q["xt"] * q["w"])

    def spec(shape, imap):
        return pl.BlockSpec(shape, imap)

    bc0 = DIN // (GPS * NST)

    def ins(c):
        return [spec((1, Q, GPS * GWID), lambda e, g, s: (e, c(s), g)),
                spec((1, Q, GPS * NST), lambda e, g, s: (e, c(s), bc0 + g)),
                spec((1, Q, GPS * NST), lambda e, g, s: (e, c(s), bc0 + NG // GPS + g)),
                spec((1, GPS, Q, 128), lambda e, g, s: (e, g, c(s), 0))]

    def outs(c):
        return [spec((1, Q, GPS * GWID), lambda e, g, s: (e, c(s), g)),
                spec((1, 1, NST, GPS * GWID), lambda e, g, s: (e, c(s), 0, g))]

    yshape = jax.ShapeDtypeStruct((2, R, DIN), F32)
    hshape = jax.ShapeDtypeStruct((2, NS, NST, DIN), BF)
    return pl.pallas_call(
        body, name="ssd_fwd", grid=(2, NG // GPS, NS),
        in_specs=ins(cf) + ins(cb) + [spec((GPS, 8, 128), lambda e, g, s: (g, 0, 0))],
        out_specs=tuple(outs(cf) + outs(cb)),
        out_shape=(yshape, hshape, yshape, hshape),
        scratch_shapes=[pltpu.VMEM((2, NST, GPS * GWID), F32)],
        compiler_params=_params(("parallel", "parallel", "arbitrary")),
    )(xbc, xbc, xbc, dt_loc, xbc, xbc, xbc, dt_loc, a_loc)


def _ssd_bwd(xbc, dt_loc, a_loc, hs_f, hs_b, y_f, y_b, dy, L):
    _, R, _ = xbc.shape
    NX, NS = L // Q, R // Q
    cf0, cb0 = _chunk_maps(NX, NS)
    cf = lambda sp: cf0(NS - 1 - sp)
    cb = lambda sp: cb0(NS - 1 - sp)

    def body(xs_f, b_f, c_f, dt_f, hs_f_, dy_f, y_f_, xs_b, b_b, c_b, dt_b, hs_b_, dy_b, y_b_, a_ref,
             dxs_f, dbc_f, ddt_f, dxs_b, dbc_b, ddt_b, da_ref, dhT):
        @pl.when(pl.program_id(2) == 0)
        def _():
            dhT[...] = jnp.zeros_like(dhT)
            da_ref[...] = jnp.zeros_like(da_ref)

        lane = lax.broadcasted_iota(jnp.int32, (Q, 128), 1)
        row = lax.broadcasted_iota(jnp.int32, (Q, 128), 0)

        def one_chain(d, gi, xs_ref, b_ref, c_ref, dt_ref, hs_ref, dy_ref, y_ref, dxs_ref, dbc_ref, ddt_ref):
            cols = slice(gi * GWID, (gi + 1) * GWID)
            A = a_ref[gi, 0:1, :]
            xs, dt = xs_ref[0, :, cols].astype(F32), dt_ref[0, gi]
            B, C = b_ref[0, :, gi * NST:(gi + 1) * NST], c_ref[0, :, gi * NST:(gi + 1) * NST]
            q = _ssd_chunk(d, dt, A, xs, B, C)
            xt, lam, w, decay = q["xt"], q["lam"], q["w"], q["decay"]
            H = hs_ref[0, 0, :, cols]
            dyv = dy_ref[0, :, cols].astype(F32)
            dh = dhT[d, :, cols]
            dZ = dyv * lam
            dC = _dot_nt(dZ, H)
            dH = _dot_tn(C, dZ)
            U = _dot(B, dh)
            xw = xt * w
            dxt = U * w
            dalast_e = (jnp.sum(U * xw, axis=0, keepdims=True)
                        + decay * jnp.sum(dh * H.astype(F32), axis=0, keepdims=True))
            dB = _dot_nt(xw, dh)
            dCB = jnp.zeros((Q, Q), F32)
            dxt_parts = []
            for pr in range(HPG // 2):
                xp = xt[:, pr * 128:(pr + 1) * 128]
                dyp = dyv[:, pr * 128:(pr + 1) * 128]
                L0, L1 = _head_decay(q, d, 2 * pr), _head_decay(q, d, 2 * pr + 1)
                dyst = jnp.concatenate([jnp.where(lane < HEAD, dyp, 0.0), jnp.where(lane < HEAD, 0.0, dyp)], axis=0)
                mst = jnp.concatenate([(q["CB"] * L0).astype(BF), (q["CB"] * L1).astype(BF)], axis=0)
                dxt_parts.append(_dot_tn(mst, dyst))
                dmst = _dot_nt(dyst, xp)
                dCB = dCB + dmst[:Q] * L0 + dmst[Q:] * L1
            dxt_diag = jnp.concatenate(dxt_parts, axis=1)
            dC = dC + _dot(dCB, B)
            dB = dB + _dot_tn(dCB, C)
            Rm = _reduce_mat(d)
            dacs = _dot_sl(dyv * y_ref[0, :, cols] - xt.astype(BF).astype(F32) * dxt_diag - U * xw, Rm, 2)
            dxt = dxt + dxt_diag
            dal = _dot_sl(jnp.broadcast_to(dalast_e, (8, GWID)), Rm, 2)[0:1, :]
            dacs = dacs + jnp.where(row == (Q - 1 if d == 0 else 0), dal, 0.0)
            da = _dot_sr(q["Tt"], dacs, 2)
            ddt_ref[0, gi] = da * A + _dot_sl(dxt * xs, Rm, 2)
            da_ref[0, gi, 0:1, :] += jnp.sum(da * dt, axis=0, keepdims=True)
            dxs_ref[0, :, cols] = (dxt * q["dt_e"]).astype(BF)
            dbc_ref[0, :, gi * 2 * NST:(gi + 1) * 2 * NST] = jnp.concatenate([dB, dC], axis=1).astype(BF)
            dhT[d, :, cols] = decay * dh + dH

        for gi in range(GPS):
            one_chain(0, gi, xs_f, b_f, c_f, dt_f, hs_f_, dy_f, y_f_, dxs_f, dbc_f, ddt_f)
            one_chain(1, gi, xs_b, b_b, c_b, dt_b, hs_b_, dy_b, y_b_, dxs_b, dbc_b, ddt_b)

    def spec(shape, imap):
        return pl.BlockSpec(shape, imap)

    bc0 = DIN // (GPS * NST)

    def ins(c):
        return [spec((1, Q, GPS * GWID), lambda e, g, s: (e, c(s), g)),
                spec((1, Q, GPS * NST), lambda e, g, s: (e, c(s), bc0 + g)),
                spec((1, Q, GPS * NST), lambda e, g, s: (e, c(s), bc0 + NG // GPS + g)),
                spec((1, GPS, Q, 128), lambda e, g, s: (e, g, c(s), 0)),
                spec((1, 1, NST, GPS * GWID), lambda e, g, s: (e, c(s), 0, g)),
                spec((1, Q, GPS * GWID), lambda e, g, s: (e, c(s), g)),
                spec((1, Q, GPS * GWID), lambda e, g, s: (e, c(s), g))]

    def outs(c):
        return [spec((1, Q, GPS * GWID), lambda e, g, s: (e, c(s), g)),
                spec((1, Q, GPS * 2 * NST), lambda e, g, s: (e, c(s), g)),
                spec((1, GPS, Q, 128), lambda e, g, s: (e, g, c(s), 0))]

    s_xs = jax.ShapeDtypeStruct((2, R, DIN), BF)
    s_bc = jax.ShapeDtypeStruct((2, R, 2 * NG * NST), BF)
    s_dt = jax.ShapeDtypeStruct((2, NG, R, 128), F32)
    return pl.pallas_call(
        body, name="ssd_bwd", grid=(2, NG // GPS, NS),
        in_specs=ins(cf) + ins(cb) + [spec((GPS, 8, 128), lambda e, g, s: (g, 0, 0))],
        out_specs=tuple(outs(cf) + outs(cb) + [spec((1, GPS, 8, 128), lambda e, g, s: (e, g, 0, 0))]),
        out_shape=(s_xs, s_bc, s_dt, s_xs, s_bc, s_dt, jax.ShapeDtypeStruct((2, NG, 8, 128), F32)),
        scratch_shapes=[pltpu.VMEM((2, NST, GPS * GWID), F32)],
        compiler_params=_params(("parallel", "parallel", "arbitrary")),
    )(xbc, xbc, xbc, dt_loc, hs_f, dy, y_f, xbc, xbc, xbc, dt_loc, hs_b, dy, y_b, a_loc)


def _ssd_post_fwd(y_f, y_b, xbc, proj3, dskip_e, ssd_norm, L):
    def body(yf_ref, yb_ref, xs_ref, z_ref, ds_ref, w_ref, o_ref, y2_ref):
        y2 = yf_ref[0] + yb_ref[0] + ds_ref[...] * xs_ref[0].astype(F32)
        y2_ref[0] = y2.astype(BF)
        z = z_ref[0].astype(F32)
        u = y2 * (z * _sigmoid(z))
        parts = []
        for g in range(NG):
            ug = u[:, g * GWID:(g + 1) * GWID]
            parts.append(ug * lax.rsqrt(jnp.mean(ug * ug, axis=-1, keepdims=True) + EPS))
        o_ref[0] = (jnp.concatenate(parts, axis=1) * w_ref[...]).astype(BF)

    blk = lambda c: pl.BlockSpec((1, X_TILE, DIN), lambda e, t: (e, t, c))
    vec = pl.BlockSpec((1, DIN), lambda e, t: (0, 0))
    return pl.pallas_call(
        body, name="ssd_post_fwd", grid=(2, L // X_TILE),
        in_specs=[blk(0), blk(0), blk(0), blk(1), vec, vec],
        out_specs=(blk(0), blk(0)),
        out_shape=(jax.ShapeDtypeStruct((2, L, DIN), BF), jax.ShapeDtypeStruct((2, L, DIN), BF)),
        compiler_params=_params(("parallel", "parallel")),
    )(y_f, y_b, xbc, proj3, dskip_e, ssd_norm)


def _ssd_post_bwd(d_yn, y2b, xbc, proj3, ssd_norm, L):
    _, R, _ = xbc.shape
    nx = L // ROW_TILE

    def body(dyn_ref, y2_ref, xs_ref, z_ref, w_ref, dy_ref, dz_ref, acc_ref):
        t = pl.program_id(1)

        @pl.when(t == 0)
        def _():
            acc_ref[...] = jnp.zeros_like(acc_ref)

        @pl.when(t >= nx)
        def _():
            dy_ref[...] = jnp.zeros_like(dy_ref)
            dz_ref[...] = jnp.zeros_like(dz_ref)

        @pl.when(t < nx)
        def _():
            xs = xs_ref[0].astype(F32)
            y2 = y2_ref[0].astype(F32)
            z = z_ref[0].astype(F32)
            sg = _sigmoid(z)
            sz = z * sg
            u = y2 * sz
            dyn = dyn_ref[0].astype(F32)
            dun = dyn * w_ref[...]
            uh_parts, du_parts = [], []
            for g in range(NG):
                sl = slice(g * GWID, (g + 1) * GWID)
                ug = u[:, sl]
                rg = lax.rsqrt(jnp.mean(ug * ug, axis=-1, keepdims=True) + EPS)
                uh = ug * rg
                dg = dun[:, sl]
                du_parts.append(rg * (dg - uh * jnp.mean(dg * uh, axis=-1, keepdims=True)))
                uh_parts.append(uh)
            du = jnp.concatenate(du_parts, axis=1)
            uh = jnp.concatenate(uh_parts, axis=1)
            dy2 = du * sz
            dy_ref[0] = dy2.astype(BF)
            dz_ref[0] = (du * y2 * (sg * (1.0 + z * (1.0 - sg)))).astype(BF)
            acc_ref[0, 0:1, :] += jnp.sum(dyn * uh, axis=0, keepdims=True)
            acc_ref[0, 1:2, :] += jnp.sum(dy2 * xs, axis=0, keepdims=True)

    xmap = lambda c: (lambda e, t: (e, jnp.minimum(t, nx - 1), c))
    blk = lambda c: pl.BlockSpec((1, ROW_TILE, DIN), xmap(c))
    oblk = pl.BlockSpec((1, ROW_TILE, DIN), lambda e, t: (e, t, 0))
    vec = pl.BlockSpec((1, DIN), lambda e, t: (0, 0))
    return pl.pallas_call(
        body, name="ssd_post_bwd", grid=(2, R // ROW_TILE),
        in_specs=[blk(0), blk(0), blk(0), blk(1), vec],
        out_specs=(oblk, oblk, pl.BlockSpec((1, 8, DIN), lambda e, t: (e, 0, 0))),
        out_shape=(jax.ShapeDtypeStruct((2, R, DIN), BF), jax.ShapeDtypeStruct((2, R, DIN), BF),
                   jax.ShapeDtypeStruct((2, 8, DIN), F32)),
        compiler_params=_params(("parallel", "arbitrary")),
    )(d_yn, y2b, xbc, proj3, ssd_norm)


def _merge_fwd(proj3, P, S, b_merge, L):
    def body(gp_ref, p_ref, s_ref, b_ref, o_ref):
        gt = _sigmoid(gp_ref[0].astype(F32) + b_ref[...])
        o_ref[0] = (gt[:, :D] * p_ref[0].astype(F32) + gt[:, D:] * s_ref[0].astype(F32)).astype(BF)

    tile = min(2 * X_TILE, L)
    blk = pl.BlockSpec((1, tile, D), lambda e, t: (e, t, 0))
    return pl.pallas_call(
        body, name="merge_fwd", grid=(2, L // tile),
        in_specs=[pl.BlockSpec((1, tile, 2 * D), lambda e, t: (e, t, OFF_GATE // (2 * D))), blk, blk,
                  pl.BlockSpec((1, 2 * D), lambda e, t: (0, 0))],
        out_specs=blk, out_shape=jax.ShapeDtypeStruct((2, L, D), BF),
        compiler_params=_params(("parallel", "parallel")),
    )(proj3, P, S, b_merge)


def _merge_bwd(d_merged, proj3, P, S, b_merge, L):
    _, R, _ = proj3.shape
    nx = L // ROW_TILE

    def body(dm_ref, gp_ref, p_ref, s_ref, b_ref, dp_ref, ds_ref, dg_ref, acc_ref):
        t = pl.program_id(1)

        @pl.when(t == 0)
        def _():
            acc_ref[...] = jnp.zeros_like(acc_ref)

        @pl.when(t >= nx)
        def _():
            dg_ref[...] = jnp.zeros_like(dg_ref)

        @pl.when(t < nx)
        def _():
            gt = _sigmoid(gp_ref[0].astype(F32) + b_ref[...])
            dm = dm_ref[0].astype(F32)
            g1, g2 = gt[:, :D], gt[:, D:]
            dp_ref[0] = (dm * g1).astype(BF)
            ds_ref[0] = (dm * g2).astype(BF)
            dgp = jnp.concatenate([dm * p_ref[0].astype(F32) * g1 * (1.0 - g1),
                                   dm * s_ref[0].astype(F32) * g2 * (1.0 - g2)], axis=1)
            dg_ref[0] = dgp.astype(BF)
            acc_ref[0, 0:1, :] += jnp.sum(dgp, axis=0, keepdims=True)

    xmap = lambda e, t: (e, jnp.minimum(t, nx - 1), 0)
    blk = pl.BlockSpec((1, ROW_TILE, D), xmap)
    return pl.pallas_call(
        body, name="merge_bwd", grid=(2, R // ROW_TILE),
        in_specs=[blk, pl.BlockSpec((1, ROW_TILE, 2 * D), lambda e, t: (e, jnp.minimum(t, nx - 1), OFF_GATE // (2 * D))),
                  blk, blk, pl.BlockSpec((1, 2 * D), lambda e, t: (0, 0))],
        out_specs=(blk, blk, pl.BlockSpec((1, ROW_TILE, 2 * D), lambda e, t: (e, t, 0)),
                   pl.BlockSpec((1, 8, 2 * D), lambda e, t: (e, 0, 0))),
        out_shape=(jax.ShapeDtypeStruct((2, L, D), BF), jax.ShapeDtypeStruct((2, L, D), BF),
                   jax.ShapeDtypeStruct((2, R, 2 * D), BF), jax.ShapeDtypeStruct((2, 8, 2 * D), F32)),
        compiler_params=_params(("parallel", "arbitrary")),
    )(d_merged, proj3, P, S, b_merge)


def _final(out3, x, tgt, gtab, norm_post, L):
    def body(o_ref, x_ref, t_ref, g_ref, n_ref, dxo_ref, do_ref, acc_ref):
        @pl.when(pl.program_id(1) == 0)
        def _():
            acc_ref[...] = jnp.zeros_like(acc_ref)

        o = o_ref[0].astype(F32)
        gate = g_ref[0, 0:1, :]
        npost = n_ref[...]
        r2 = lax.rsqrt(jnp.mean(o * o, axis=-1, keepdims=True) + EPS)
        nh = o * r2
        on = nh * npost
        err = x_ref[0] + gate * on - t_ref[0]
        dxo = err * (1.0 / D)
        dxo_ref[0] = dxo.astype(BF)
        dnh = dxo * gate * npost
        do_ref[0] = (r2 * (dnh - nh * jnp.mean(dnh * nh, axis=-1, keepdims=True))).astype(BF)
        acc_ref[0, 0:1, :] += jnp.sum(dxo * on, axis=0, keepdims=True)
        acc_ref[0, 1:2, :] += jnp.sum(dxo * gate * nh, axis=0, keepdims=True)
        acc_ref[0, 2:3, :] += jnp.sum(err * err, axis=0, keepdims=True)

    tile = min(2 * X_TILE, L)
    blk = pl.BlockSpec((1, tile, D), lambda e, t: (e, t, 0))
    return pl.pallas_call(
        body, name="final", grid=(2, L // tile),
        in_specs=[blk, blk, blk, pl.BlockSpec((1, 8, D), lambda e, t: (e, 0, 0)),
                  pl.BlockSpec((1, D), lambda e, t: (0, 0))],
        out_specs=(blk, blk, pl.BlockSpec((1, 8, D), lambda e, t: (e, 0, 0))),
        out_shape=(jax.ShapeDtypeStruct((2, L, D), BF), jax.ShapeDtypeStruct((2, L, D), BF),
                   jax.ShapeDtypeStruct((2, 8, D), F32)),
        compiler_params=_params(("parallel", "arbitrary")),
    )(out3, x, tgt, gtab, norm_post)


def _local_step(x, c, ctx, loss_target, W, late_shard=None, exchange=False):
    nb, L, _ = x.shape
    LC = ctx.shape[1]
    R = L + LC
    assert nb == 2 and L % ROW_TILE == 0 and LC % Q == 0 and L % POOL_TILE == 0
    w_inT = W["w_in"]
    w_dtT = jnp.pad(w_inT[OFF_DT:], ((0, 64), (0, 0)))
    tables = _pool_tables(L)
    tr, tl = (2 * R) // 8, (2 * L) // 8

    c16 = jnp.zeros((16, D), F32).at[0:2].set(c).at[2].set(W["c_ctx"])
    mod16 = _adaln_fwd(c16, W["w_ada"], W["b_ada"])
    shift, scale, gate = mod16[:, :D], mod16[:, D:2 * D], mod16[:, 2 * D:]
    npre = W["norm_pre"]
    tab = jnp.zeros((2, 2, 8, D), F32)
    for e in range(2):
        tab = tab.at[e, 0, 0].set(npre[0] * (1.0 + scale[e])).at[e, 0, 1].set(shift[e])
        tab = tab.at[e, 1, 0].set(npre[0] * (1.0 + scale[2])).at[e, 1, 1].set(shift[2])
    gtab = jnp.zeros((2, 8, D), F32).at[:, 0].set(gate[0:2])

    hx = _norm_mod_fwd(x, ctx, tab)
    hx2 = hx.reshape(2 * R, D)
    if late_shard is None:
        proj = _matmul(hx2, w_inT, BF, "proj_main", tm=tr, tn=1024, bt=True, n=OFF_DT)
    else:
        proj, late = _matmul(hx2, w_inT, BF, "proj_main", tm=tr, tn=1024, bt=True, n=OFF_DT, side=_gather_side(late_shard))
        W = {**W, **_unpack_gather(late, GATHER_LATE)}
    proj3 = proj.reshape(2, R, OFF_DT)
    dt_raw = _matmul(hx2, w_dtT, F32, "proj_dt", tm=tr, bt=True).reshape(2, R, 128)
    ypool = _pool_fwd(proj3, W["pool_w"], W["pool_scale"], tables, L)
    xbc = _conv_fwd(proj3, W["conv_w"], W["conv_b"], L)
    bias128 = jnp.pad(W["dt_bias"].reshape(1, 64), ((0, 0), (0, 64)))
    dt_loc = _dt_fwd(dt_raw, bias128)
    A = -jnp.exp(W["a_log"].reshape(2, NG, HPG))
    a_loc = jnp.zeros((NG, 8, 128), F32).at[:, 0, :16].set(A.transpose(1, 0, 2).reshape(NG, 16))
    y_f, hs_f, y_b, hs_b = _ssd_fwd(xbc, dt_loc, a_loc, L)
    dskip_e = jnp.repeat(W["d_skip"].reshape(1, 32), HEAD, axis=1)
    yn, y2b = _ssd_post_fwd(y_f, y_b, xbc, proj3, dskip_e, W["ssd_norm"], L)
    ypool2, yn2 = ypool.reshape(2 * L, D), yn.reshape(2 * L, DIN)
    P = _matmul(ypool2, W["w_proj_pool"], BF, "proj_pool", tm=tl, tn=1024).reshape(2, L, D)
    S = _matmul(yn2, W["w_proj_ssd"], BF, "proj_ssd", tm=tl, tn=1024).reshape(2, L, D)
    merged = _merge_fwd(proj3, P, S, W["b_merge"], L)
    merged2 = merged.reshape(2 * L, D)
    out3 = _matmul(merged2, W["w_out"], BF, "proj_out", tm=tl, tn=1024).reshape(2, L, D)
    dxo, dout, acc_f = _final(out3, x, loss_target, gtab, W["norm_post"], L)

    dout2 = dout.reshape(2 * L, D)
    g = {}
    g["w_out"] = _matmul_tn(merged2, dout2, "dw_out", ta=1024, tn=1024, tr=4 * tl)
    d_merged = _matmul(dout2, W["w_out"], BF, "d_merged", tm=tl, tn=1024, bt=True).reshape(2, L, D)
    dP, dS, dgp, acc_m = _merge_bwd(d_merged, proj3, P, S, W["b_merge"], L)
    dP2, dS2 = dP.reshape(2 * L, D), dS.reshape(2 * L, D)
    g["w_proj_pool"] = _matmul_tn(ypool2, dP2, "dw_proj_pool", ta=1024, tn=1024, tr=4 * tl)
    g["w_proj_ssd"] = _matmul_tn(yn2, dS2, "dw_proj_ssd", ta=1024, tn=1024, tr=4 * tl)
    d_ypool = _matmul(dP2, W["w_proj_pool"], BF, "d_ypool", tm=tl, tn=1024, bt=True).reshape(2, L, D)
    d_yn = _matmul(dS2, W["w_proj_ssd"], BF, "d_yn", tm=tl, tn=1024, bt=True).reshape(2, L, DIN)
    dv, dzp, g["pool_w"], acc_p = _pool_bwd(proj3, d_ypool, W["pool_w"], jnp.swapaxes(W["pool_w"], 1, 2),
                                            W["pool_scale"], tables, L)
    dy2, dzs, acc_s = _ssd_post_bwd(d_yn, y2b, xbc, proj3, W["ssd_norm"], L)
    dxs_f, dbc_f, ddt_f, dxs_b, dbc_b, ddt_b, acc_a = _ssd_bwd(xbc, dt_loc, a_loc, hs_f, hs_b, y_f, y_b, dy2, L)
    ident = lambda j: j
    dxr_xs, acc_cx = _conv_bwd(proj3, [dxs_f, dxs_b, dy2], [None, None, dskip_e], 0, DIN, [ident, ident, ident],
                               W["conv_w"], W["conv_b"], L, "conv_bwd_xs")
    bcmap = lambda j: 2 * lax.rem(j, NG) + j // NG
    dxr_bc, acc_cb = _conv_bwd(proj3, [dbc_f, dbc_b], [None, None], DIN, 2 * NG * NST, [bcmap, bcmap],
                               W["conv_w"], W["conv_b"], L, "conv_bwd_bc")
    ddtr, acc_d = _dt_bwd(dt_raw, bias128, ddt_f, ddt_b)
    pieces = [dv, dzp, dzs, dgp, dxr_xs, dxr_bc]
    dw_rows = [_matmul_tn(p.reshape(2 * R, p.shape[2]), hx2, "dw_in_%d" % i, ta=1024, tn=1024, tr=4 * tr)
               for i, p in enumerate(pieces)]
    dw_rows.append(_matmul_tn(ddtr.reshape(2 * R, 128), hx2, "dw_in_dt", ta=128, tn=1024, tr=tr)[:64])
    g["w_in"] = jnp.concatenate(dw_rows, axis=0)
    acc_c = jnp.concatenate([acc_cx[0] + acc_cx[1], acc_cb[0] + acc_cb[1]], axis=1)
    g["conv_w"] = acc_c[0:4]
    g["conv_b"] = acc_c[4:5]
    if exchange:
        gb = _pack_grads(g, GRADS_EARLY)
        pair = _pair_add(gb, _pair_exchange(gb, None, "grads_pair_exchange_early"), "grads_pair_add_early")
        dh, recv_early = _dhx(pieces, ddtr, w_inT, w_dtT, side=_chip_exchange_side(pair))
    else:
        dh, recv_early = _dhx(pieces, ddtr, w_inT, w_dtT), None
    grad_x, acc_n = _norm_mod_bwd(dh, x, ctx, tab, dxo)
    g["w_ada"], db_rows, sm_rows = _adaln_bwd(acc_n, acc_f, mod16, c16, npre, W["w_ada"])

    g["b_ada"] = db_rows[0:1]
    g["norm_pre"] = sm_rows[0:1]
    g["c_ctx"] = sm_rows[1]
    g["norm_post"] = acc_f[0, 1:2] + acc_f[1, 1:2]
    g["b_merge"] = acc_m[0, 0:1] + acc_m[1, 0:1]
    g["pool_scale"] = acc_p[:, 0, :].reshape(1, D)
    g["dt_bias"] = (acc_d[0, 0, :64] + acc_d[1, 0, :64]).reshape(2, 32)
    dA = (acc_a[0, :, 0, :16] + acc_a[1, :, 0, :16]).reshape(NG, 2, HPG).transpose(1, 0, 2)
    g["a_log"] = (dA * A).reshape(2, 32)
    g["d_skip"] = (acc_s[0, 1] + acc_s[1, 1]).reshape(32, HEAD).sum(axis=1).reshape(1, 32)
    g["ssd_norm"] = acc_s[0, 0:1] + acc_s[1, 0:1]
    loss_lanes = acc_f[:, 2, :]
    return loss_lanes, grad_x, g, recv_early


MESH = pl.DeviceIdType.MESH
ANY = pl.BlockSpec(memory_space=pl.ANY)


def _all_gather(shard):
    m_per, n = shard.shape

    def body(x_ref, out_ref, send_sems, recv_sems, local_sem):
        x, y, c = lax.axis_index("x"), lax.axis_index("y"), lax.axis_index("c")
        me, sibling = (x, y, c), (x, y, 1 - c)
        chips = [(1 - x, y), (x, 1 - y), (1 - x, 1 - y)]

        def rows(px, py, pc):
            return out_ref.at[pl.ds((4 * px + 2 * py + pc) * m_per, m_per), :]

        def copy(k, block, to, src=None):
            return pltpu.make_async_remote_copy(
                src_ref=rows(*block) if src is None else src, dst_ref=rows(*block),
                send_sem=send_sems.at[k], recv_sem=recv_sems.at[k], device_id=to, device_id_type=MESH)

        mine = pltpu.make_async_copy(x_ref, rows(*me), local_sem)
        mine.start()
        first = [copy(0, me, sibling, src=x_ref)]
        first += [copy(1 + j, me, (*chip, c), src=x_ref) for j, chip in enumerate(chips)]
        for cp in first:
            cp.start()
        passed = [copy(4 + j, (*chip, c), sibling) for j, chip in enumerate(chips)]
        for j, chip in enumerate(chips):
            copy(1 + j, (*chip, c), me).wait_recv()
            passed[j].start()
        copy(0, sibling, me).wait_recv()
        for j, chip in enumerate(chips):
            copy(4 + j, (*chip, 1 - c), me).wait_recv()
        for cp in first + passed:
            cp.wait_send()
        mine.wait()

    return pl.pallas_call(
        body, name="all_gather_weights",
        out_shape=jax.ShapeDtypeStruct((NDEV * m_per, n), shard.dtype),
        in_specs=[ANY], out_specs=ANY,
        scratch_shapes=[pltpu.SemaphoreType.DMA((7,)), pltpu.SemaphoreType.DMA((7,)), pltpu.SemaphoreType.DMA],
    )(shard)


PAIR_PIECES = 12


def _xor_peer(k, x, y, c):
    return (1 - x if k & 4 else x, 1 - y if k & 2 else y, 1 - c if k & 1 else c)


def _pair_exchange(big, small, name):
    _, nq, rows, n = big.shape
    piece = rows // PAIR_PIECES
    assert piece * PAIR_PIECES == rows and piece % 16 == 0
    with_small = small is not None

    def body(*refs):
        if with_small:
            big_ref, small_ref, got_ref, osmall_ref, send_sems, recv_sems, local_sem = refs
        else:
            big_ref, got_ref, send_sems, recv_sems, local_sem = refs
        x, y, c = lax.axis_index("x"), lax.axis_index("y"), lax.axis_index("c")
        me = 4 * x + 2 * y + c

        def rc(src, dst, sem, peer):
            return pltpu.make_async_remote_copy(src_ref=src, dst_ref=dst, send_sem=send_sems.at[sem],
                                                recv_sem=recv_sems.at[sem], device_id=peer, device_id_type=MESH)

        sib = _xor_peer(1, x, y, c)
        local, sends, recvs = [], [], []
        for q in range(nq):
            for h in range(PAIR_PIECES):
                rws = pl.ds(h * piece, piece)
                cp = rc(big_ref.at[1 - c, q, rws], got_ref.at[q, rws], 8 + q * PAIR_PIECES + h, sib)
                sends.append(cp)
                recvs.append(cp)
        if with_small:
            local.append(pltpu.make_async_copy(small_ref, osmall_ref.at[me], local_sem))
            for k in range(1, NDEV):
                px, py, pc = _xor_peer(k, x, y, c)
                sends.append(rc(small_ref, osmall_ref.at[me], k, (px, py, pc)))
                recvs.append(rc(small_ref, osmall_ref.at[4 * px + 2 * py + pc], k, (px, py, pc)))
        for cp in local + sends:
            cp.start()
        for cp in sends:
            cp.wait_send()
        for cp in recvs:
            cp.wait_recv()
        for cp in local:
            cp.wait()

    nsem = 8 + nq * PAIR_PIECES
    out_shape = [jax.ShapeDtypeStruct(big.shape[1:], big.dtype)]
    if with_small:
        out_shape.append(jax.ShapeDtypeStruct((NDEV,) + small.shape, small.dtype))
    out = pl.pallas_call(
        body, name=name, out_shape=tuple(out_shape),
        in_specs=[ANY] * (1 + with_small), out_specs=(ANY,) * (1 + with_small),
        scratch_shapes=[pltpu.SemaphoreType.DMA((nsem,)), pltpu.SemaphoreType.DMA((nsem,)), pltpu.SemaphoreType.DMA],
    )(*((big, small) if with_small else (big,)))
    return out if with_small else out[0]


def _pair_add(big, got, name):
    _, nq, rows, n = big.shape
    tile = rows // 4
    assert rows % 64 == 0

    def body(c_ref, a_ref, b_ref, o_ref):
        o_ref[0] = (a_ref[0, 0].astype(F32) + b_ref[0].astype(F32)).astype(BF)

    blk = pl.BlockSpec((1, tile, n), lambda q, i, c_ref: (q, i, 0))
    return pl.pallas_call(
        body, name=name,
        grid_spec=pltpu.PrefetchScalarGridSpec(
            num_scalar_prefetch=1, grid=(nq, rows // tile),
            in_specs=[pl.BlockSpec((1, 1, tile, n), lambda q, i, c_ref: (c_ref[0], q, i, 0)), blk], out_specs=blk),
        out_shape=jax.ShapeDtypeStruct(got.shape, BF), compiler_params=_params(("parallel", "parallel")),
    )(lax.axis_index("c").astype(jnp.int32).reshape(1), big, got)


def _chip_exchange_side(pair):
    def make(in_refs, out_refs, send_sems, recv_sems, local_sem, arrivals=True):
        (in_ref,), (out_ref,) = in_refs, out_refs
        x, y, c = lax.axis_index("x"), lax.axis_index("y"), lax.axis_index("c")
        q = 2 * x + y
        local = [pltpu.make_async_copy(in_ref.at[q], out_ref.at[q], local_sem)]
        sends, recvs = [], []
        for j in range(1, 4):
            px, py, pc = _xor_peer(2 * j, x, y, c)
            pq = 2 * px + py
            for lst, dst in ((sends, out_ref.at[q]), (recvs, out_ref.at[pq]))[:1 + arrivals]:
                lst.append(pltpu.make_async_remote_copy(
                    src_ref=in_ref.at[pq], dst_ref=dst, send_sem=send_sems.at[j - 1], recv_sem=recv_sems.at[j - 1],
                    device_id=(px, py, pc), device_id_type=MESH))
        return local, sends, recvs

    return _SideCopies([pair], [jax.ShapeDtypeStruct(pair.shape, pair.dtype)], make)


def _gather_side(shard):
    def make(in_refs, out_refs, send_sems, recv_sems, local_sem, arrivals=True):
        (src,), (dst,) = in_refs, out_refs
        x, y, c = lax.axis_index("x"), lax.axis_index("y"), lax.axis_index("c")
        me = 4 * x + 2 * y + c
        local = [pltpu.make_async_copy(src, dst.at[me], local_sem)]
        sends, recvs = [], []
        for k in range(1, NDEV):
            px, py, pc = _xor_peer(k, x, y, c)
            for lst, slot in ((sends, me), (recvs, 4 * px + 2 * py + pc))[:1 + arrivals]:
                lst.append(pltpu.make_async_remote_copy(
                    src_ref=src, dst_ref=dst.at[slot], send_sem=send_sems.at[k - 1], recv_sem=recv_sems.at[k - 1],
                    device_id=(px, py, pc), device_id_type=MESH))
        return local, sends, recvs

    return _SideCopies([shard], [jax.ShapeDtypeStruct((NDEV,) + shard.shape, shard.dtype)], make)


ADAM_TILE = 192
PACK_W = 1024


def _adamw(recv, w, m, v, name, side=None):
    rp = w.shape[0]
    tile = min(ADAM_TILE, rp)
    nsrc = recv.shape[0]
    grid = (rp // tile,)
    n_si, n_so = (len(side.inputs), len(side.out_shapes)) if side else (0, 0)

    def body(*refs):
        r_ref, w_ref, m_ref, v_ref = refs[:4]
        g_ref, d_ref, nm_ref, nv_ref = refs[4 + n_si:8 + n_si]
        side_refs = (refs[4:4 + n_si], refs[8 + n_si:8 + n_si + n_so], refs[8 + n_si + n_so:])
        if side:
            side.start(grid, *side_refs)
        g = r_ref[0].astype(F32)
        for i in range(1, nsrc):
            g = g + r_ref[i].astype(F32)
        m1 = ADAM_B1 * m_ref[...] + (1.0 - ADAM_B1) * g
        v1 = ADAM_B2 * v_ref[...] + (1.0 - ADAM_B2) * (g * g)
        m_hat = m1 / (1.0 - ADAM_B1 ** ADAM_STEP)
        v_hat = v1 / (1.0 - ADAM_B2 ** ADAM_STEP)
        g_ref[...] = g
        d_ref[...] = -ADAM_LR * (m_hat / (jnp.sqrt(v_hat) + ADAM_EPS) + ADAM_WD * w_ref[...])
        nm_ref[...] = m1
        nv_ref[...] = v1
        if side:
            side.wait(grid, *side_refs)

    blk = pl.BlockSpec((tile, PACK_W), lambda i: (i, 0))
    shp = jax.ShapeDtypeStruct((rp, PACK_W), F32)
    return pl.pallas_call(
        body, name=name, grid=grid,
        in_specs=[pl.BlockSpec((nsrc, tile, PACK_W), lambda i: (0, i, 0)), blk, blk, blk] + [ANY] * n_si,
        out_specs=(blk, blk, blk, blk) + (ANY,) * n_so,
        out_shape=(shp, shp, shp, shp) + tuple(side.out_shapes if side else ()),
        scratch_shapes=side.scratch() if side else [],
        compiler_params=_params(("arbitrary",) if side else ("parallel",)),
    )(recv, w, m, v, *(side.inputs if side else ()))


BIG = {"w_ada": ((3 * D, D), 0), "pool_w": ((4, PGW, PGW), 1), "w_proj_pool": ((D, D), 0), "w_proj_ssd": ((DIN, D), 0),
       "w_out": ((D, D), 0), "w_in": ((IN_COLS, D), 0), "conv_w": ((4, CONV_DIM), 1)}
TRANSPOSED = ("w_ada", "w_in")
PACK_ROWS = {"w_ada": 384, "w_in": 1168, "conv_w": 16, "pool_w": 32, "w_proj_pool": 128, "w_proj_ssd": 256, "w_out": 128}
GATHER_EARLY = ("w_ada", "w_in", "conv_w")
GATHER_LATE = ("pool_w", "w_proj_pool", "w_proj_ssd", "w_out")
GRADS_LATE = ("w_ada",)
GRADS_EARLY = tuple(n for n in PACK_ROWS if n not in GRADS_LATE)
SMALL = {"c_ctx": (D,), "b_ada": (1, 3 * D), "norm_pre": (1, D), "norm_post": (1, D), "b_merge": (1, 2 * D),
         "pool_scale": (1, D), "conv_b": (1, CONV_DIM), "dt_bias": (2, 32), "a_log": (2, 32), "d_skip": (1, 32),
         "ssd_norm": (1, DIN)}
LOSS_SLOT = 128
assert all(_r % 16 == 0 for _r in PACK_ROWS.values())
SMALL_ROWS = 16


def _shard_shape(name):
    shape, ax = BIG[name]
    return tuple(s // NDEV if i == ax else s for i, s in enumerate(shape))


def _as_rows(t, rows):
    pad = [(0, 0)] * (t.ndim - 1) + [(0, rows * PACK_W - t.shape[-1])]
    return jnp.pad(t, pad).reshape(t.shape[:-1] + (rows, PACK_W))


def _shard_rows(t, name):
    sh, r = _shard_shape(name), PACK_ROWS[name]
    lead = t.shape[:t.ndim - len(sh)]
    if len(sh) == 2 and sh[1] == PACK_W:
        return jnp.pad(t, [(0, 0)] * len(lead) + [(0, r - sh[0]), (0, 0)])
    if int(np.prod(sh)) == r * PACK_W:
        return t.reshape(lead + (r, PACK_W))
    return _as_rows(t.reshape(lead + (-1,)), r)


def _to_chunks(full, name):
    shape, ax = BIG[name]
    split = shape[:ax] + (NDEV, shape[ax] // NDEV) + shape[ax + 1:]
    return _shard_rows(jnp.moveaxis(full.reshape(split), ax, 0), name)


def _from_chunks(chunks, name):
    shape, ax = BIG[name]
    return jnp.moveaxis(chunks.reshape((NDEV,) + _shard_shape(name)), 0, ax).reshape(shape)


def _rows_of(names):
    return sum(PACK_ROWS[n] for n in names)


def _pack_state(t, names):
    return jnp.concatenate([_shard_rows(t[n], n) for n in names], axis=0)


def _pack_small(t, loss_part=None):
    slot = jnp.zeros((LOSS_SLOT,), F32)
    if loss_part is not None:
        slot = slot.at[0].set(loss_part)
    return _as_rows(jnp.concatenate([t[n].reshape(-1) for n in SMALL] + [slot]), SMALL_ROWS)


def _pack_grads(g, names):
    big = jnp.concatenate([_to_chunks(g[n], n).astype(BF) for n in names], axis=1)
    return jnp.swapaxes(big.reshape(4, 2, _rows_of(names), PACK_W), 0, 1)


def _unpack_state(big, names):
    out, off = {}, 0
    for n in names:
        sh, r = _shard_shape(n), PACK_ROWS[n]
        k = int(np.prod(sh))
        if len(sh) == 2 and sh[1] == PACK_W:
            out[n] = big[off:off + sh[0]]
        else:
            out[n] = big[off:off + r].reshape(-1)[:k].reshape(sh)
        off += r
    return out


def _unpack_small(small):
    out, flat, off = {}, small.reshape(-1), 0
    for n, sh in SMALL.items():
        k = int(np.prod(sh))
        out[n] = flat[off:off + k].reshape(sh)
        off += k
    out["loss"] = flat[off]
    return out


def _pack_gather(w, names):
    pieces = []
    for n in names:
        if n == "conv_w":
            pieces.append(_as_rows(jnp.concatenate([p.reshape(-1) for p in _split(w[n], 3)]), PACK_ROWS[n]))
        else:
            pieces.append(_shard_rows(w[n], n).astype(BF))
    return jnp.concatenate(pieces, axis=0)


def _unpack_gather(gathered, names):
    g = gathered.reshape(NDEV, _rows_of(names), PACK_W)
    out, off = {}, 0
    for n in names:
        r = PACK_ROWS[n]
        sh = _shard_shape(n)
        if n == "conv_w":
            k = int(np.prod(sh))
            terms = g[:, off:off + r].reshape(NDEV, -1)[:, :3 * k].astype(F32).reshape(NDEV, 3, k)
            out[n] = _from_chunks(terms[:, 0] + terms[:, 1] + terms[:, 2], n)
        elif len(sh) == 2 and sh[1] == PACK_W:
            out[n] = _from_chunks(g[:, off:off + sh[0]], n)
        else:
            out[n] = _from_chunks(g[:, off:off + r], n)
        off += r
    return out


PARAMS = ["c_ctx", "w_ada", "b_ada", "norm_pre", "norm_post", "w_in", "b_merge", "pool_w", "pool_scale", "conv_w", "conv_b",
          "dt_bias", "a_log", "d_skip", "ssd_norm", "w_proj_pool", "w_proj_ssd", "w_out"]


def kernel(x, c, ctx, c_ctx, w_ada, b_ada, norm_pre, norm_post, w_in, b_merge, pool_w, pool_scale, conv_w, conv_b, dt_bias, a_log, d_skip, ssd_norm, w_proj_pool, w_proj_ssd, w_out, loss_target, m_c_ctx, m_w_ada, m_b_ada, m_norm_pre, m_norm_post, m_w_in, m_b_merge, m_pool_w, m_pool_scale, m_conv_w, m_conv_b, m_dt_bias, m_a_log, m_d_skip, m_ssd_norm, m_w_proj_pool, m_w_proj_ssd, m_w_out, v_c_ctx, v_w_ada, v_b_ada, v_norm_pre, v_norm_post, v_w_in, v_b_merge, v_pool_w, v_pool_scale, v_conv_w, v_conv_b, v_dt_bias, v_a_log, v_d_skip, v_ssd_norm, v_w_proj_pool, v_w_proj_ssd, v_w_out):
    given = dict(locals())
    shapes = {n: given[n].shape for n in PARAMS}

    def local(prefix):
        t = {n: (given[prefix + n] if n == "c_ctx" else given[prefix + n][0]) for n in PARAMS}
        for n in TRANSPOSED:
            t[n] = t[n].T
        return {n: t[n].reshape(_shard_shape(n) if n in BIG else SMALL[n]) for n in PARAMS}

    w, m, v = local(""), local("m_"), local("v_")

    W = _unpack_gather(_all_gather(_pack_gather(w, GATHER_EARLY)), GATHER_EARLY)
    for n in SMALL:
        W[n] = w[n]
    lanes, grad_x, g, recv_early = _local_step(x, c, ctx, loss_target, W, late_shard=_pack_gather(w, GATHER_LATE),
                                               exchange=True)
    gb = _pack_grads(g, GRADS_LATE)
    got, recv_small = _pair_exchange(gb, _pack_small(g, (0.5 / D) * jnp.sum(lanes)), "grads_pair_exchange_late")
    late = _chip_exchange_side(_pair_add(gb, got, "grads_pair_add_late"))
    res = [{} for _ in range(4)]
    *early, recv_late = _adamw(recv_early, *[_pack_state(s, GRADS_EARLY) for s in (w, m, v)], "adamw_early", side=late)
    for r, t in zip(res, early):
        r.update(_unpack_state(t, GRADS_EARLY))
    for r, t in zip(res, _adamw(recv_late, *[_pack_state(s, GRADS_LATE) for s in (w, m, v)], "adamw_late")):
        r.update(_unpack_state(t, GRADS_LATE))
    for r, t in zip(res, _adamw(recv_small, *[_pack_small(s) for s in (w, m, v)], "adamw_small")):
        r.update(_unpack_small(t))
    outs = [res[0]["loss"], grad_x]
    for r in res:
        for n in TRANSPOSED:
            r[n] = r[n].T
        outs += [r[n].reshape(shapes[n]) for n in PARAMS]
    return tuple(outs)
```

```python
import functools

import numpy as np
import jax
import jax.numpy as jnp
from jax import lax
from jax.experimental import pallas as pl
from jax.experimental.pallas import tpu as pltpu

F32, BF = jnp.float32, jnp.bfloat16

D = 1024
GRID_W = 64
EPS = 1e-6
POOL_WINDOWS = (2, 4, 8, 16)
PGW = 256
DIN = 2048
HEAD = 64
NST = 128
NG = 4
HPG = 8
GWID = HPG * HEAD
Q = 128
CONV_DIM = 3072
OFF_GATE, OFF_XBC, OFF_DT, IN_COLS = 4096, 6144, 9216, 9280
NDEV = 8
ADAM_LR, ADAM_B1, ADAM_B2, ADAM_EPS, ADAM_WD, ADAM_STEP = 0.001, 0.9, 0.999, 1e-08, 0.01, 10

V7X_VMEM_LIMIT = 56 * 2 ** 20
ROW_TILE = 256
X_TILE = 512


def _params(sem=None):
    return pltpu.CompilerParams(dimension_semantics=sem, vmem_limit_bytes=V7X_VMEM_LIMIT)


def _dot(a, b):
    return jnp.dot(a.astype(BF), b.astype(BF), preferred_element_type=F32)


def _dot_nt(a, b):
    return lax.dot_general(a.astype(BF), b.astype(BF), (((1,), (1,)), ((), ())), preferred_element_type=F32)


def _dot_tn(a, b):
    return lax.dot_general(a.astype(BF), b.astype(BF), (((0,), (0,)), ((), ())), preferred_element_type=F32)


def _split(a, n):
    parts = []
    for _ in range(n):
        p = a.astype(BF)
        parts.append(p)
        a = a - p.astype(F32)
    return parts


def _dot_sl(a, b01, n=3):
    parts = _split(a, n)
    m = a.shape[0]
    if n == 1 or m % 16:
        return sum(jnp.dot(p, b01, preferred_element_type=F32) for p in parts)
    r = jnp.dot(jnp.concatenate(parts, axis=0), b01, preferred_element_type=F32)
    return sum(r[i * m:(i + 1) * m] for i in range(n))


def _dot_sr(a01, b, n=3):
    parts = _split(b, n)
    k = b.shape[1]
    if n == 1 or k % 128:
        return sum(jnp.dot(a01, p, preferred_element_type=F32) for p in parts)
    r = jnp.dot(a01, jnp.concatenate(parts, axis=1), preferred_element_type=F32)
    return sum(r[:, i * k:(i + 1) * k] for i in range(n))


def _sigmoid(x):
    return 1.0 / (1.0 + jnp.exp(-x))


class _SideCopies:
    NSEM = 8

    def __init__(self, inputs, out_shapes, make):
        self.inputs, self.out_shapes, self.make = list(inputs), list(out_shapes), make

    def scratch(self):
        return [pltpu.SemaphoreType.DMA((self.NSEM,)), pltpu.SemaphoreType.DMA((self.NSEM,)), pltpu.SemaphoreType.DMA]

    def start(self, grid, in_refs, out_refs, sems):
        @pl.when(functools.reduce(lambda p, q: p & q, [pl.program_id(i) == 0 for i in range(len(grid))]))
        def _():
            local, sends, _ = self.make(in_refs, out_refs, *sems, arrivals=False)
            for cp in local + sends:
                cp.start()

    def wait(self, grid, in_refs, out_refs, sems):
        @pl.when(functools.reduce(lambda p, q: p & q, [pl.program_id(i) == n - 1 for i, n in enumerate(grid)]))
        def _():
            local, sends, recvs = self.make(in_refs, out_refs, *sems)
            for cp in sends:
                cp.wait_send()
            for cp in recvs:
                cp.wait_recv()
            for cp in local:
                cp.wait()


def _matmul(a, b, out_dtype, name, tm=512, tn=512, tk=1024, bt=False, n=None, side=None):
    M, K = a.shape
    N = n if n is not None else (b.shape[0] if bt else b.shape[1])
    tm, tn, tk = min(tm, M), min(tn, N), min(tk, K)
    assert M % tm == 0 and N % tn == 0 and K % tk == 0, (a.shape, b.shape)
    nk = K // tk
    grid = (M // tm, N // tn, nk)
    n_si, n_so = (len(side.inputs), len(side.out_shapes)) if side else (0, 0)

    def body(*refs):
        a_ref, b_ref, o_ref = refs[0], refs[1], refs[2 + n_si]
        acc = refs[3 + n_si + n_so]
        side_refs = (refs[2:2 + n_si], refs[3 + n_si:3 + n_si + n_so], refs[4 + n_si + n_so:])
        if side:
            side.start(grid, *side_refs)
        k = pl.program_id(2)
        p = _dot_nt(a_ref[...], b_ref[...]) if bt else _dot(a_ref[...], b_ref[...])

        @pl.when(k == 0)
        def _():
            acc[...] = p

        @pl.when(k > 0)
        def _():
            acc[...] += p

        @pl.when(k == nk - 1)
        def _():
            o_ref[...] = acc[...].astype(o_ref.dtype)

        if side:
            side.wait(grid, *side_refs)

    out = pl.pallas_call(
        body, name=name, grid=grid,
        in_specs=[pl.BlockSpec((tm, tk), lambda i, j, k: (i, k)),
                  pl.BlockSpec((tn, tk), lambda i, j, k: (j, k)) if bt else pl.BlockSpec((tk, tn), lambda i, j, k: (k, j))]
        + [ANY] * n_si,
        out_specs=(pl.BlockSpec((tm, tn), lambda i, j, k: (i, j)),) + (ANY,) * n_so,
        out_shape=(jax.ShapeDtypeStruct((M, N), out_dtype),) + tuple(side.out_shapes if side else ()),
        scratch_shapes=[pltpu.VMEM((tm, tn), F32)] + (side.scratch() if side else []),
        compiler_params=_params(("arbitrary",) * 3 if side else ("parallel", "parallel", "arbitrary")),
    )(a, b, *(side.inputs if side else ()))
    return out if side else out[0]


def _matmul_tn(a, g, name, ta=512, tn=512, tr=512):
    M, Ka = a.shape
    N = g.shape[1]
    ta, tn, tr = min(ta, Ka), min(tn, N), min(tr, M)
    assert M % tr == 0 and N % tn == 0 and Ka % ta == 0, (a.shape, g.shape)
    nr = M // tr

    def body(a_ref, g_ref, o_ref):
        k = pl.program_id(2)
        p = _dot_tn(a_ref[...], g_ref[...])

        @pl.when(k == 0)
        def _():
            o_ref[...] = p

        @pl.when(k > 0)
        def _():
            o_ref[...] += p

    return pl.pallas_call(
        body, name=name, grid=(Ka // ta, N // tn, nr),
        in_specs=[pl.BlockSpec((tr, ta), lambda i, j, k: (k, i)), pl.BlockSpec((tr, tn), lambda i, j, k: (k, j))],
        out_specs=pl.BlockSpec((ta, tn), lambda i, j, k: (i, j)),
        out_shape=jax.ShapeDtypeStruct((Ka, N), F32),
        compiler_params=_params(("parallel", "parallel", "arbitrary")),
    )(a, g)


def _dhx(pieces, ddt, w_inT, w_dtT, side=None):
    _, R, _ = pieces[0].shape
    tm = R // 4
    kb = 1024
    starts, nblk = [], []
    for p in pieces:
        starts.append(sum(nblk))
        nblk.append(p.shape[2] // kb)
    nk = sum(nblk)
    assert nk * kb == OFF_DT and R % 128 == 0
    npc = len(pieces)
    grid = (2, R // tm, nk)
    n_si, n_so = (len(side.inputs), len(side.out_shapes)) if side else (0, 0)

    def body(*refs):
        a_refs, dt_ref, w_ref, wdt_ref = refs[:npc], refs[npc], refs[npc + 1], refs[npc + 2]
        o_ref, acc = refs[npc + 3 + n_si], refs[npc + 4 + n_si + n_so]
        side_refs = (refs[npc + 3:npc + 3 + n_si], refs[npc + 4 + n_si:npc + 4 + n_si + n_so], refs[npc + 5 + n_si + n_so:])
        if side:
            side.start(grid, *side_refs)
        k = pl.program_id(2)

        @pl.when(k == 0)
        def _():
            acc[...] = _dot(dt_ref[0], wdt_ref[...])

        for p in range(npc):
            @pl.when((k >= starts[p]) & (k < starts[p] + nblk[p]))
            def _(p=p):
                acc[...] += _dot(a_refs[p][0], w_ref[...])

        @pl.when(k == nk - 1)
        def _():
            o_ref[0] = acc[...].astype(BF)

        if side:
            side.wait(grid, *side_refs)

    in_specs = [pl.BlockSpec((1, tm, kb), functools.partial(
        lambda e, t, k, s, nb: (e, t, jnp.clip(k - s, 0, nb - 1)), s=starts[p], nb=nblk[p])) for p in range(npc)]
    in_specs += [pl.BlockSpec((1, tm, 128), lambda e, t, k: (e, t, 0)),
                 pl.BlockSpec((kb, D), lambda e, t, k: (k, 0)),
                 pl.BlockSpec((128, D), lambda e, t, k: (0, 0))]
    out = pl.pallas_call(
        body, name="d_hx", grid=grid, in_specs=in_specs + [ANY] * n_si,
        out_specs=(pl.BlockSpec((1, tm, D), lambda e, t, k: (e, t, 0)),) + (ANY,) * n_so,
        out_shape=(jax.ShapeDtypeStruct((2, R, D), BF),) + tuple(side.out_shapes if side else ()),
        scratch_shapes=[pltpu.VMEM((tm, D), F32)] + (side.scratch() if side else []),
        compiler_params=_params(("arbitrary",) * 3 if side else ("parallel", "parallel", "arbitrary")),
    )(*pieces, ddt, w_inT, w_dtT, *(side.inputs if side else ()))
    return out if side else out[0]


def _adaln_fwd(c16, w_adaT_bf, b_ada):
    def body(c_ref, w_ref, b_ref, o_ref):
        cc = c_ref[...]
        o_ref[...] = _dot_nt(cc * _sigmoid(cc), w_ref[...]) + b_ref[...]

    return pl.pallas_call(body, name="adaln_fwd", out_shape=jax.ShapeDtypeStruct((16, 3 * D), F32),
                          compiler_params=_params())(c16, w_adaT_bf, b_ada)


def _adaln_bwd(acc_n, acc_f, mod16, c16, norm_pre, w_adaT_bf):
    def body(an_ref, af_ref, mod_ref, c_ref, np_ref, wt_ref, dw_ref, db_ref, sm_ref, dmod):
        npre = np_ref[...]
        dmod[...] = jnp.zeros_like(dmod)
        dnp = jnp.zeros((1, D), F32)
        dshift_c = jnp.zeros((1, D), F32)
        dgpre_c = jnp.zeros((1, D), F32)
        scale_c = mod_ref[2:3, D:2 * D]
        for e in range(2):
            dg_x, ds_x = an_ref[e, 0, 0:1, :], an_ref[e, 0, 1:2, :]
            dg_c, ds_c = an_ref[e, 1, 0:1, :], an_ref[e, 1, 1:2, :]
            dmod[e:e + 1, 0:D] = ds_x
            dmod[e:e + 1, D:2 * D] = dg_x * npre
            dmod[e:e + 1, 2 * D:3 * D] = af_ref[e, 0:1, :]
            dnp = dnp + dg_x * (1.0 + mod_ref[e:e + 1, D:2 * D]) + dg_c * (1.0 + scale_c)
            dshift_c = dshift_c + ds_c
            dgpre_c = dgpre_c + dg_c
        dmod[2:3, 0:D] = dshift_c
        dmod[2:3, D:2 * D] = dgpre_c * npre
        dm = dmod[...]
        cc = c_ref[...]
        sg = _sigmoid(cc)
        dw_ref[...] = _dot_tn(dm, cc * sg)
        db_ref[...] = jnp.zeros_like(db_ref)
        db_ref[0:1, :] = jnp.sum(dm, axis=0, keepdims=True)
        dsilu = sg * (1.0 + cc * (1.0 - sg))
        dcs = _dot(dm, wt_ref[...]) * dsilu
        sm_ref[...] = jnp.zeros_like(sm_ref)
        sm_ref[0:1, :] = dnp
        sm_ref[1:2, :] = dcs[2:3, :]

    return pl.pallas_call(
        body, name="adaln_bwd",
        out_shape=(jax.ShapeDtypeStruct((3 * D, D), F32), jax.ShapeDtypeStruct((16, 3 * D), F32),
                   jax.ShapeDtypeStruct((8, D), F32)),
        scratch_shapes=[pltpu.VMEM((16, 3 * D), F32)],
        compiler_params=_params())(acc_n, acc_f, mod16, c16, norm_pre, w_adaT_bf)


def _row_specs(L):
    nx = L // ROW_TILE
    return (pl.BlockSpec((1, ROW_TILE, D), lambda e, t: (e, jnp.minimum(t, nx - 1), 0)),
            pl.BlockSpec((1, ROW_TILE, D), lambda e, t: (e, jnp.maximum(t - nx, 0), 0)))


def _norm_mod_fwd(x, ctx, tab):
    L = x.shape[1]
    R = L + ctx.shape[1]
    nx = L // ROW_TILE

    def body(x_ref, c_ref, t_ref, o_ref):
        t = t_ref[0, 0]
        gain, shift = t[0:1], t[1:2]

        def run(src):
            for r0 in range(0, ROW_TILE, 32):
                x = src[0, pl.ds(r0, 32), :]
                r = lax.rsqrt(jnp.mean(x * x, axis=-1, keepdims=True) + EPS)
                o_ref[0, pl.ds(r0, 32), :] = (x * r * gain + shift).astype(BF)

        @pl.when(pl.program_id(1) < nx)
        def _():
            run(x_ref)

        @pl.when(pl.program_id(1) >= nx)
        def _():
            run(c_ref)

    return pl.pallas_call(
        body, name="norm_mod_fwd", grid=(2, R // ROW_TILE),
        in_specs=[*_row_specs(L), pl.BlockSpec((1, 1, 8, D), lambda e, t: (e, t // nx, 0, 0))],
        out_specs=pl.BlockSpec((1, ROW_TILE, D), lambda e, t: (e, t, 0)),
        out_shape=jax.ShapeDtypeStruct((2, R, D), BF),
        compiler_params=_params(("parallel", "parallel")),
    )(x, ctx, tab)


def _norm_mod_bwd(dh, x, ctx, tab, dxo):
    L = x.shape[1]
    R = L + ctx.shape[1]
    nx = L // ROW_TILE

    def body(dh_ref, x_ref, c_ref, t_ref, dxo_ref, gx_ref, acc_ref):
        t = pl.program_id(1)
        x = jnp.where(t < nx, x_ref[0], c_ref[0])
        r = lax.rsqrt(jnp.mean(x * x, axis=-1, keepdims=True) + EPS)
        xn = x * r
        dh = dh_ref[0].astype(F32)

        @pl.when((t == 0) | (t == nx))
        def _():
            acc_ref[...] = jnp.zeros_like(acc_ref)

        acc_ref[0, 0, 0:1, :] += jnp.sum(dh * xn, axis=0, keepdims=True)
        acc_ref[0, 0, 1:2, :] += jnp.sum(dh, axis=0, keepdims=True)

        @pl.when(t < nx)
        def _():
            dxn = dh * t_ref[0, 0][0:1]
            dx = r * (dxn - xn * jnp.mean(dxn * xn, axis=-1, keepdims=True))
            gx_ref[0] = dxo_ref[0].astype(F32) + dx

    xspec, cspec = _row_specs(L)
    return pl.pallas_call(
        body, name="norm_mod_bwd", grid=(2, R // ROW_TILE),
        in_specs=[pl.BlockSpec((1, ROW_TILE, D), lambda e, t: (e, t, 0)), xspec, cspec,
                  pl.BlockSpec((1, 1, 8, D), lambda e, t: (e, t // nx, 0, 0)), xspec],
        out_specs=(xspec, pl.BlockSpec((1, 1, 8, D), lambda e, t: (e, t // nx, 0, 0))),
        out_shape=(jax.ShapeDtypeStruct((2, L, D), F32), jax.ShapeDtypeStruct((2, 2, 8, D), F32)),
        compiler_params=_params(("parallel", "arbitrary")),
    )(dh, x, ctx, tab, dxo)


POOL_TILE = 256


def _pool_tables(L):
    rows = L // GRID_W
    mats = np.zeros((4, POOL_TILE, POOL_TILE), np.float32)
    for gi, k in enumerate(POOL_WINDOWS):
        lo, hi = k // 2, k - 1 - k // 2
        m = np.zeros((GRID_W, GRID_W), np.float32)
        for t in range(GRID_W):
            m[t, max(t - lo, 0):min(t + hi, GRID_W - 1) + 1] = 1.0
        for b in range(POOL_TILE // GRID_W):
            mats[gi, b * GRID_W:(b + 1) * GRID_W, b * GRID_W:(b + 1) * GRID_W] = m
    matsT = np.ascontiguousarray(np.transpose(mats, (0, 2, 1)))
    return (jnp.asarray(mats, BF), jnp.asarray(matsT, BF))


def _pool_cols(get_tile, mat, cs_ref, L, n):
    def step(i, carry):
        off = pl.multiple_of(i * POOL_TILE, POOL_TILE)
        t = get_tile(off)
        cs_ref[pl.ds(GRID_W + off, POOL_TILE), :] = (jnp.dot(mat, t.astype(BF), preferred_element_type=F32) if n == 1
                                                     else _dot_sr(mat, t.astype(F32), n))
        return carry

    lax.fori_loop(0, L // POOL_TILE, step, 0)
    cs_ref[pl.ds(0, GRID_W), :] = jnp.zeros((GRID_W, PGW), F32)

    def prefix(r, carry):
        o = pl.multiple_of(r * GRID_W, GRID_W)
        cs_ref[pl.ds(o + GRID_W, GRID_W), :] = cs_ref[pl.ds(o + GRID_W, GRID_W), :] + cs_ref[pl.ds(o, GRID_W), :]
        return carry

    lax.fori_loop(0, L // GRID_W, prefix, 0)


def _pool_rows(cs_ref, off, below, above, L):
    rows = L // GRID_W
    r0 = off // GRID_W
    parts = []
    for i in range(POOL_TILE // GRID_W):
        hi = pl.multiple_of(jnp.minimum(r0 + i + above + 1, rows) * GRID_W, GRID_W)
        lo = pl.multiple_of(jnp.maximum(r0 + i - below, 0) * GRID_W, GRID_W)
        parts.append(cs_ref[pl.ds(hi, GRID_W), :] - cs_ref[pl.ds(lo, GRID_W), :])
    return jnp.concatenate(parts, axis=0)


PGS = 2


def _inv_count(off, half, L):
    t = off + lax.broadcasted_iota(jnp.int32, (POOL_TILE, 1), 0)
    r, c = jnp.right_shift(t, 6), t & (GRID_W - 1)
    cr = jnp.minimum(r + half - 1, L // GRID_W - 1) - jnp.maximum(r - half, 0) + 1
    cc = jnp.minimum(c + half - 1, GRID_W - 1) - jnp.maximum(c - half, 0) + 1
    return 1.0 / (cr * cc).astype(F32)


def _pool_fwd(proj3, pool_w_bf, pool_scale, tables, L):
    mats, _ = tables
    nt = L // POOL_TILE

    def body(v_ref, z_ref, pw_ref, ps_ref, m_ref, o_ref, cs_ref):
        for j in range(PGS):
            cols = slice(j * PGW, (j + 1) * PGW)
            _pool_cols(lambda off: v_ref[0, pl.ds(off, POOL_TILE), cols], m_ref[j], cs_ref, L, 1)
            half = lax.shift_left(1, PGS * pl.program_id(1) + j)

            def step(i, carry, j=j, cols=cols, half=half):
                off = pl.multiple_of(i * POOL_TILE, POOL_TILE)
                rows = pl.ds(off, POOL_TILE)
                v = v_ref[0, rows, cols].astype(F32)
                diff = _pool_rows(cs_ref, off, half, half - 1, L) * _inv_count(off, half, L) - v
                yp = _dot(diff, pw_ref[j])
                z = z_ref[0, rows, cols].astype(F32)
                o_ref[0, rows, cols] = (yp * ps_ref[:, cols] * (z * _sigmoid(z))).astype(BF)
                return carry

            lax.fori_loop(0, nt, step, 0)

    wide = PGS * PGW
    return pl.pallas_call(
        body, name="pool_fwd", grid=(2, 4 // PGS),
        in_specs=[pl.BlockSpec((1, L, wide), lambda e, g: (e, 0, g)),
                  pl.BlockSpec((1, L, wide), lambda e, g: (e, 0, 4 // PGS + g)),
                  pl.BlockSpec((PGS, PGW, PGW), lambda e, g: (g, 0, 0)),
                  pl.BlockSpec((1, wide), lambda e, g: (0, g)),
                  pl.BlockSpec((PGS, POOL_TILE, POOL_TILE), lambda e, g: (g, 0, 0))],
        out_specs=pl.BlockSpec((1, L, wide), lambda e, g: (e, 0, g)),
        out_shape=jax.ShapeDtypeStruct((2, L, D), BF),
        scratch_shapes=[pltpu.VMEM((L + GRID_W, PGW), F32)],
        compiler_params=_params(("parallel", "parallel")),
    )(proj3, proj3, pool_w_bf, pool_scale, mats)


def _pool_bwd(proj3, d_ypool, pool_w_bf, pool_wT_bf, pool_scale, tables, L):
    mats, matsT = tables
    nt = L // POOL_TILE
    R = proj3.shape[1]

    def body(v_ref, z_ref, dy_ref, pw_ref, pwt_ref, ps_ref, m_ref, mt_ref,
             dv_ref, dz_ref, dpw_ref, acc_ref, cs_ref, dd_ref):
        e = pl.program_id(1)

        @pl.when(e == 0)
        def _():
            dpw_ref[...] = jnp.zeros_like(dpw_ref)
            acc_ref[...] = jnp.zeros_like(acc_ref)

        for j in range(PGS):
            cols = slice(j * PGW, (j + 1) * PGW)
            _pool_cols(lambda off: v_ref[0, pl.ds(off, POOL_TILE), cols], m_ref[j], cs_ref, L, 1)
            half = lax.shift_left(1, PGS * pl.program_id(0) + j)
            ps = ps_ref[:, cols]

            def step(i, carry, j=j, cols=cols, half=half, ps=ps):
                off = pl.multiple_of(i * POOL_TILE, POOL_TILE)
                rows = pl.ds(off, POOL_TILE)
                v = v_ref[0, rows, cols].astype(F32)
                diff = _pool_rows(cs_ref, off, half, half - 1, L) * _inv_count(off, half, L) - v
                yp = _dot(diff, pw_ref[j])
                z = z_ref[0, rows, cols].astype(F32)
                sg = _sigmoid(z)
                sz = z * sg
                dy = dy_ref[0, rows, cols].astype(F32)
                dz_ref[0, rows, cols] = (dy * yp * ps * (sg * (1.0 + z * (1.0 - sg)))).astype(BF)
                dys = dy * sz
                acc_ref[j, 0:1, :] += jnp.sum(dys * yp, axis=0, keepdims=True)
                dyp = dys * ps
                dpw_ref[j] += _dot_tn(diff, dyp)
                dd_ref[rows, :] = _dot(dyp, pwt_ref[j])
                return carry

            lax.fori_loop(0, nt, step, 0)
            _pool_cols(lambda off, half=half: dd_ref[pl.ds(off, POOL_TILE), :] * _inv_count(off, half, L),
                       mt_ref[j], cs_ref, L, 1)

            def step2(i, carry, cols=cols, half=half):
                off = pl.multiple_of(i * POOL_TILE, POOL_TILE)
                rows = pl.ds(off, POOL_TILE)
                dv_ref[0, rows, cols] = (_pool_rows(cs_ref, off, half - 1, half, L) - dd_ref[rows, :]).astype(BF)
                return carry

            lax.fori_loop(0, nt, step2, 0)
        dv_ref[0, pl.ds(L, R - L), :] = jnp.zeros((R - L, PGS * PGW), BF)
        dz_ref[0, pl.ds(L, R - L), :] = jnp.zeros((R - L, PGS * PGW), BF)

    wide = PGS * PGW
    return pl.pallas_call(
        body, name="pool_bwd", grid=(4 // PGS, 2),
        in_specs=[pl.BlockSpec((1, L, wide), lambda g, e: (e, 0, g)),
                  pl.BlockSpec((1, L, wide), lambda g, e: (e, 0, 4 // PGS + g)),
                  pl.BlockSpec((1, L, wide), lambda g, e: (e, 0, g)),
                  pl.BlockSpec((PGS, PGW, PGW), lambda g, e: (g, 0, 0)),
                  pl.BlockSpec((PGS, PGW, PGW), lambda g, e: (g, 0, 0)),
                  pl.BlockSpec((1, wide), lambda g, e: (0, g)),
                  pl.BlockSpec((PGS, POOL_TILE, POOL_TILE), lambda g, e: (g, 0, 0)),
                  pl.BlockSpec((PGS, POOL_TILE, POOL_TILE), lambda g, e: (g, 0, 0))],
        out_specs=(pl.BlockSpec((1, R, wide), lambda g, e: (e, 0, g)),
                   pl.BlockSpec((1, R, wide), lambda g, e: (e, 0, g)),
                   pl.BlockSpec((PGS, PGW, PGW), lambda g, e: (g, 0, 0)),
                   pl.BlockSpec((PGS, 8, PGW), lambda g, e: (g, 0, 0))),
        out_shape=(jax.ShapeDtypeStruct((2, R, D), BF), jax.ShapeDtypeStruct((2, R, D), BF),
                   jax.ShapeDtypeStruct((4, PGW, PGW), F32), jax.ShapeDtypeStruct((4, 8, PGW), F32)),
        scratch_shapes=[pltpu.VMEM((L + GRID_W, PGW), F32), pltpu.VMEM((L, PGW), F32)],
        compiler_params=_params(("parallel", "arbitrary")),
    )(proj3, proj3, d_ypool, pool_w_bf, pool_wT_bf, pool_scale, mats, matsT)


CONV_BLOCK = 128


CONV_CHUNK = 64
CONV_HALO = 8


def _halo_buf_init(buf, val, R):
    z = jnp.zeros((CONV_HALO, CONV_BLOCK), F32)
    buf[pl.ds(0, CONV_HALO), :] = z
    buf[pl.ds(CONV_HALO + R, CONV_HALO), :] = z
    if val is not None:
        buf[pl.ds(CONV_HALO, R), :] = val


def _chunk_taps(buf, start, offs, L):
    n = CONV_CHUNK + 2 * CONV_HALO
    ext = buf[pl.ds(start, n), :]
    out = []
    for off in offs:
        if off == 0:
            out.append(ext[CONV_HALO:CONV_HALO + CONV_CHUNK])
            continue
        r = pltpu.roll(ext, (-off) % n, 0)[CONV_HALO:CONV_HALO + CONV_CHUNK]
        lo, hi = (start, start + CONV_CHUNK - 1 + off) if off > 0 else (start + off, start + CONV_CHUNK - 1)
        if lo < L <= hi:
            t = start + lax.broadcasted_iota(jnp.int32, (CONV_CHUNK, 1), 0)
            r = jnp.where((t < L) == (t + off < L), r, 0.0)
        out.append(r)
    return out


def _fold8(x):
    return sum(x[i * 8:(i + 1) * 8] for i in range(CONV_CHUNK // 8))


def _conv_fwd(proj3, conv_w, conv_b, L):
    _, R, _ = proj3.shape
    cb0 = OFF_XBC // CONV_BLOCK

    def body(u_ref, w_ref, b_ref, o_ref, ubuf):
        _halo_buf_init(ubuf, u_ref[0].astype(F32), R)
        w = w_ref[...]
        b = b_ref[...]
        for start in range(0, R, CONV_CHUNK):
            taps = _chunk_taps(ubuf, start, (-2, -1, 0, 1), L)
            pre = b + sum(taps[k] * w[k:k + 1, :] for k in range(4))
            o_ref[0, pl.ds(start, CONV_CHUNK), :] = (pre * _sigmoid(pre)).astype(BF)

    return pl.pallas_call(
        body, name="conv_fwd", grid=(2, CONV_DIM // CONV_BLOCK),
        in_specs=[pl.BlockSpec((1, R, CONV_BLOCK), lambda e, j: (e, 0, cb0 + j)),
                  pl.BlockSpec((4, CONV_BLOCK), lambda e, j: (0, j)),
                  pl.BlockSpec((1, CONV_BLOCK), lambda e, j: (0, j))],
        out_specs=pl.BlockSpec((1, R, CONV_BLOCK), lambda e, j: (e, 0, j)),
        out_shape=jax.ShapeDtypeStruct((2, R, CONV_DIM), BF),
        scratch_shapes=[pltpu.VMEM((R + 2 * CONV_HALO, CONV_BLOCK), F32)],
        compiler_params=_params(("parallel", "parallel")),
    )(proj3, conv_w, conv_b)


def _conv_bwd(proj3, addends, scales, col0, ncols, in_maps, conv_w, conv_b, L, name):
    _, R, _ = proj3.shape
    cb0 = (OFF_XBC + col0) // CONV_BLOCK
    wb0 = col0 // CONV_BLOCK
    na = len(addends)
    scaled = [i for i in range(na) if scales[i] is not None]

    def body(*refs):
        u_ref, w_ref, b_ref = refs[0], refs[1], refs[2]
        a_refs = refs[3:3 + na]
        s_refs = dict(zip(scaled, refs[3 + na:3 + na + len(scaled)]))
        o_ref, acc_ref, ubuf, dbuf = refs[3 + na + len(scaled):]
        _halo_buf_init(ubuf, u_ref[0].astype(F32), R)
        _halo_buf_init(dbuf, None, R)
        w = w_ref[...]
        b = b_ref[...]
        scl = {i: s_refs[i][...] for i in scaled}
        sums = [jnp.zeros((8, CONV_BLOCK), F32) for _ in range(5)]
        for start in range(0, R, CONV_CHUNK):
            rows = pl.ds(start, CONV_CHUNK)
            taps = _chunk_taps(ubuf, start, (-2, -1, 0, 1), L)
            pre = b + sum(taps[k] * w[k:k + 1, :] for k in range(4))
            sg = _sigmoid(pre)
            dxbc = None
            for i, a in enumerate(a_refs):
                t = a[0, rows, :].astype(F32)
                t = t * scl[i] if i in scl else t
                dxbc = t if dxbc is None else dxbc + t
            dpre = dxbc * (sg * (1.0 + pre * (1.0 - sg)))
            dbuf[pl.ds(start + CONV_HALO, CONV_CHUNK), :] = dpre
            for k in range(4):
                sums[k] = sums[k] + _fold8(dpre * taps[k])
            sums[4] = sums[4] + _fold8(dpre)
        acc_ref[...] = jnp.zeros_like(acc_ref)
        for k in range(5):
            acc_ref[0, k:k + 1, :] = jnp.sum(sums[k], axis=0, keepdims=True)
        for start in range(0, R, CONV_CHUNK):
            d = _chunk_taps(dbuf, start, (2, 1, 0, -1), L)
            o_ref[0, pl.ds(start, CONV_CHUNK), :] = sum(d[k] * w[k:k + 1, :] for k in range(4)).astype(BF)

    in_specs = [pl.BlockSpec((1, R, CONV_BLOCK), lambda e, j: (e, 0, cb0 + j)),
                pl.BlockSpec((4, CONV_BLOCK), lambda e, j: (0, wb0 + j)),
                pl.BlockSpec((1, CONV_BLOCK), lambda e, j: (0, wb0 + j))]
    for m in in_maps:
        in_specs.append(pl.BlockSpec((1, R, CONV_BLOCK), functools.partial(lambda e, j, m: (e, 0, m(j)), m=m)))
    for i in scaled:
        in_specs.append(pl.BlockSpec((1, CONV_BLOCK), functools.partial(lambda e, j, m: (0, m(j)), m=in_maps[i])))
    return pl.pallas_call(
        body, name=name, grid=(2, ncols // CONV_BLOCK),
        in_specs=in_specs,
        out_specs=(pl.BlockSpec((1, R, CONV_BLOCK), lambda e, j: (e, 0, j)),
                   pl.BlockSpec((1, 8, CONV_BLOCK), lambda e, j: (e, 0, j))),
        out_shape=(jax.ShapeDtypeStruct((2, R, ncols), BF), jax.ShapeDtypeStruct((2, 8, ncols), F32)),
        scratch_shapes=[pltpu.VMEM((R + 2 * CONV_HALO, CONV_BLOCK), F32)] * 2,
        compiler_params=_params(("parallel", "parallel")),
    )(proj3, conv_w, conv_b, *addends, *[scales[i] for i in scaled])


def _softplus(x):
    e = jnp.exp(-jnp.abs(x))
    u = 1.0 + e
    return jnp.maximum(x, 0.0) + jnp.where(u == 1.0, e, e * jnp.log(u) / (u - 1.0))


def _to_local_mat(g, transpose=False):
    r = lax.broadcasted_iota(jnp.int32, (128, 128), 1 if transpose else 0)
    c = lax.broadcasted_iota(jnp.int32, (128, 128), 0 if transpose else 1)
    return ((c < 2 * HPG) & (r == jnp.right_shift(c, 3) * (NG * HPG) + g * HPG + (c & (HPG - 1)))).astype(BF)


def _dt_fwd(dt_raw, bias128):
    _, R, _ = dt_raw.shape

    def body(x_ref, b_ref, o_ref):
        dt = _softplus(x_ref[0] + b_ref[...])
        for g in range(NG):
            o_ref[0, g] = _dot_sl(dt, _to_local_mat(g))

    tr = R // 4
    return pl.pallas_call(
        body, name="dt_fwd", grid=(2, 4),
        in_specs=[pl.BlockSpec((1, tr, 128), lambda e, t: (e, t, 0)), pl.BlockSpec((1, 128), lambda e, t: (0, 0))],
        out_specs=pl.BlockSpec((1, NG, tr, 128), lambda e, t: (e, 0, t, 0)),
        out_shape=jax.ShapeDtypeStruct((2, NG, R, 128), F32),
        compiler_params=_params(("parallel", "parallel")),
    )(dt_raw, bias128)


def _dt_bwd(dt_raw, bias128, ddt_f, ddt_b):
    _, R, _ = dt_raw.shape

    def body(x_ref, b_ref, f_ref, g_ref, o_ref, acc_ref):
        ddt = sum(_dot_sl(f_ref[0, g] + g_ref[0, g], _to_local_mat(g, transpose=True)) for g in range(NG))
        d = ddt * _sigmoid(x_ref[0] + b_ref[...])
        o_ref[0] = d.astype(BF)

        @pl.when(pl.program_id(1) == 0)
        def _():
            acc_ref[...] = jnp.zeros_like(acc_ref)

        acc_ref[0, 0:1, :] += jnp.sum(d, axis=0, keepdims=True)

    tr = R // 4
    blk = pl.BlockSpec((1, tr, 128), lambda e, t: (e, t, 0))
    loc = pl.BlockSpec((1, NG, tr, 128), lambda e, t: (e, 0, t, 0))
    return pl.pallas_call(
        body, name="dt_bwd", grid=(2, 4),
        in_specs=[blk, pl.BlockSpec((1, 128), lambda e, t: (0, 0)), loc, loc],
        out_specs=(blk, pl.BlockSpec((1, 8, 128), lambda e, t: (e, 0, 0))),
        out_shape=(jax.ShapeDtypeStruct(dt_raw.shape, BF), jax.ShapeDtypeStruct((2, 8, 128), F32)),
        compiler_params=_params(("parallel", "arbitrary")),
    )(dt_raw, bias128, ddt_f, ddt_b)


GPS = 4


def _tri(d):
    i = lax.broadcasted_iota(jnp.int32, (Q, Q), 0)
    j = lax.broadcasted_iota(jnp.int32, (Q, Q), 1)
    return (i >= j) if d == 0 else (i <= j)


def _expand_mat(d):
    r = lax.broadcasted_iota(jnp.int32, (128, GWID), 0)
    c = lax.broadcasted_iota(jnp.int32, (128, GWID), 1)
    return (r == d * HPG + jnp.right_shift(c, 6)).astype(BF)


def _reduce_mat(d):
    r = lax.broadcasted_iota(jnp.int32, (GWID, 128), 0)
    c = lax.broadcasted_iota(jnp.int32, (GWID, 128), 1)
    return (c == d * HPG + jnp.right_shift(r, 6)).astype(BF)


def _ssd_chunk(d, dt, A, xs, B, C):
    mask = _tri(d)
    T = mask.astype(BF)
    Tt = _tri(1 - d).astype(BF)
    a = dt * A
    acs = _dot_sr(T, a)
    E = _expand_mat(d)
    dt_e = _dot_sl(dt, E, 2)
    acs_e = _dot_sl(acs, E, 2)
    alast_e = acs_e[Q - 1:Q, :] if d == 0 else acs_e[0:1, :]
    return dict(mask=mask, T=T, Tt=Tt, acs=acs, acsT=acs.T, dt_e=dt_e, acs_e=acs_e, lam=jnp.exp(acs_e),
                w=jnp.exp(alast_e - acs_e), decay=jnp.exp(alast_e), xt=xs * dt_e, CB=_dot_nt(C, B))


def _head_decay(q, d, hh):
    col = q["acs"][:, d * HPG + hh:d * HPG + hh + 1]
    row = q["acsT"][d * HPG + hh:d * HPG + hh + 1, :]
    return jnp.exp(jnp.where(q["mask"], col - row, -jnp.inf))


def _chunk_maps(NX, NS):
    cf = lambda s: lax.rem(s + NX, NS)
    cb = lambda s: NS - 1 - s
    return cf, cb


def _ssd_fwd(xbc, dt_loc, a_loc, L):
    _, R, _ = xbc.shape
    NX, NS = L // Q, R // Q
    cf, cb = _chunk_maps(NX, NS)

    def body(xs_f, b_f, c_f, dt_f, xs_b, b_b, c_b, dt_b, a_ref, y_f, hs_f, y_b, hs_b, hT):
        @pl.when(pl.program_id(2) == 0)
        def _():
            hT[...] = jnp.zeros_like(hT)

        lane = lax.broadcasted_iota(jnp.int32, (Q, 128), 1)
        for d, (xs_ref, b_ref, c_ref, dt_ref, y_ref, hs_ref) in enumerate(
                ((xs_f, b_f, c_f, dt_f, y_f, hs_f), (xs_b, b_b, c_b, dt_b, y_b, hs_b))):
            for gi in range(GPS):
                cols = slice(gi * GWID, (gi + 1) * GWID)
                xs = xs_ref[0, :, cols].astype(F32)
                B, C = b_ref[0, :, gi * NST:(gi + 1) * NST], c_ref[0, :, gi * NST:(gi + 1) * NST]
                q = _ssd_chunk(d, dt_ref[0, gi], a_ref[gi, 0:1, :], xs, B, C)
                h = hT[d, :, cols]
                hb = h.astype(BF)
                hs_ref[0, 0, :, cols] = hb
                parts = []
                for pr in range(HPG // 2):
                    xp = q["xt"][:, pr * 128:(pr + 1) * 128]
                    xst = jnp.concatenate([jnp.where(lane < HEAD, xp, 0.0), jnp.where(lane < HEAD, 0.0, xp)], axis=0)
                    mst = jnp.concatenate([(q["CB"] * _head_decay(q, d, 2 * pr)).astype(BF),
                                           (q["CB"] * _head_decay(q, d, 2 * pr + 1)).astype(BF)], axis=1)
                    parts.append(_dot(mst, xst))
                y_ref[0, :, cols] = jnp.concatenate(parts, axis=1) + _dot(C, hb) * q["lam"]
                hT[d, :, cols] = q["decay"] * h + _dot_tn(B, q["xt"] * q["w"])

    def spec(shape, imap):
        return pl.BlockSpec(shape, imap)

    bc0 = DIN // (GPS * NST)

    def ins(c):
        return [spec((1, Q, GPS * GWID), lambda e, g, s: (e, c(s), g)),
                spec((1, Q, GPS * NST), lambda e, g, s: (e, c(s), bc0 + g)),
                spec((1, Q, GPS * NST), lambda e, g, s: (e, c(s), bc0 + NG // GPS + g)),
                spec((1, GPS, Q, 128), lambda e, g, s: (e, g, c(s), 0))]

    def outs(c):
        return [spec((1, Q, GPS * GWID), lambda e, g, s: (e, c(s), g)),
                spec((1, 1, NST, GPS * GWID), lambda e, g, s: (e, c(s), 0, g))]

    yshape = jax.ShapeDtypeStruct((2, R, DIN), F32)
    hshape = jax.ShapeDtypeStruct((2, NS, NST, DIN), BF)
    return pl.pallas_call(
        body, name="ssd_fwd", grid=(2, NG // GPS, NS),
        in_specs=ins(cf) + ins(cb) + [spec((GPS, 8, 128), lambda e, g, s: (g, 0, 0))],
        out_specs=tuple(outs(cf) + outs(cb)),
        out_shape=(yshape, hshape, yshape, hshape),
        scratch_shapes=[pltpu.VMEM((2, NST, GPS * GWID), F32)],
        compiler_params=_params(("parallel", "parallel", "arbitrary")),
    )(xbc, xbc, xbc, dt_loc, xbc, xbc, xbc, dt_loc, a_loc)


def _ssd_bwd(xbc, dt_loc, a_loc, hs_f, hs_b, y_f, y_b, dy, L):
    _, R, _ = xbc.shape
    NX, NS = L // Q, R // Q
    cf0, cb0 = _chunk_maps(NX, NS)
    cf = lambda sp: cf0(NS - 1 - sp)
    cb = lambda sp: cb0(NS - 1 - sp)

    def body(xs_f, b_f, c_f, dt_f, hs_f_, dy_f, y_f_, xs_b, b_b, c_b, dt_b, hs_b_, dy_b, y_b_, a_ref,
             dxs_f, dbc_f, ddt_f, dxs_b, dbc_b, ddt_b, da_ref, dhT):
        @pl.when(pl.program_id(2) == 0)
        def _():
            dhT[...] = jnp.zeros_like(dhT)
            da_ref[...] = jnp.zeros_like(da_ref)

        lane = lax.broadcasted_iota(jnp.int32, (Q, 128), 1)
        row = lax.broadcasted_iota(jnp.int32, (Q, 128), 0)

        def one_chain(d, gi, xs_ref, b_ref, c_ref, dt_ref, hs_ref, dy_ref, y_ref, dxs_ref, dbc_ref, ddt_ref):
            cols = slice(gi * GWID, (gi + 1) * GWID)
            A = a_ref[gi, 0:1, :]
            xs, dt = xs_ref[0, :, cols].astype(F32), dt_ref[0, gi]
            B, C = b_ref[0, :, gi * NST:(gi + 1) * NST], c_ref[0, :, gi * NST:(gi + 1) * NST]
            q = _ssd_chunk(d, dt, A, xs, B, C)
            xt, lam, w, decay = q["xt"], q["lam"], q["w"], q["decay"]
            H = hs_ref[0, 0, :, cols]
            dyv = dy_ref[0, :, cols].astype(F32)
            dh = dhT[d, :, cols]
            dZ = dyv * lam
            dC = _dot_nt(dZ, H)
            dH = _dot_tn(C, dZ)
            U = _dot(B, dh)
            xw = xt * w
            dxt = U * w
            dalast_e = (jnp.sum(U * xw, axis=0, keepdims=True)
                        + decay * jnp.sum(dh * H.astype(F32), axis=0, keepdims=True))
            dB = _dot_nt(xw, dh)
            dCB = jnp.zeros((Q, Q), F32)
            dxt_parts = []
            for pr in range(HPG // 2):
                xp = xt[:, pr * 128:(pr + 1) * 128]
                dyp = dyv[:, pr * 128:(pr + 1) * 128]
                L0, L1 = _head_decay(q, d, 2 * pr), _head_decay(q, d, 2 * pr + 1)
                dyst = jnp.concatenate([jnp.where(lane < HEAD, dyp, 0.0), jnp.where(lane < HEAD, 0.0, dyp)], axis=0)
                mst = jnp.concatenate([(q["CB"] * L0).astype(BF), (q["CB"] * L1).astype(BF)], axis=0)
                dxt_parts.append(_dot_tn(mst, dyst))
                dmst = _dot_nt(dyst, xp)
                dCB = dCB + dmst[:Q] * L0 + dmst[Q:] * L1
            dxt_diag = jnp.concatenate(dxt_parts, axis=1)
            dC = dC + _dot(dCB, B)
            dB = dB + _dot_tn(dCB, C)
            Rm = _reduce_mat(d)
            dacs = _dot_sl(dyv * y_ref[0, :, cols] - xt.astype(BF).astype(F32) * dxt_diag - U * xw, Rm, 2)
            dxt = dxt + dxt_diag
            dal = _dot_sl(jnp.broadcast_to(dalast_e, (8, GWID)), Rm, 2)[0:1, :]
            dacs = dacs + jnp.where(row == (Q - 1 if d == 0 else 0), dal, 0.0)
            da = _dot_sr(q["Tt"], dacs, 2)
            ddt_ref[0, gi] = da * A + _dot_sl(dxt * xs, Rm, 2)
            da_ref[0, gi, 0:1, :] += jnp.sum(da * dt, axis=0, keepdims=True)
            dxs_ref[0, :, cols] = (dxt * q["dt_e"]).astype(BF)
            dbc_ref[0, :, gi * 2 * NST:(gi + 1) * 2 * NST] = jnp.concatenate([dB, dC], axis=1).astype(BF)
            dhT[d, :, cols] = decay * dh + dH

        for gi in range(GPS):
            one_chain(0, gi, xs_f, b_f, c_f, dt_f, hs_f_, dy_f, y_f_, dxs_f, dbc_f, ddt_f)
            one_chain(1, gi, xs_b, b_b, c_b, dt_b, hs_b_, dy_b, y_b_, dxs_b, dbc_b, ddt_b)

    def spec(shape, imap):
        return pl.BlockSpec(shape, imap)

    bc0 = DIN // (GPS * NST)

    def ins(c):
        return [spec((1, Q, GPS * GWID), lambda e, g, s: (e, c(s), g)),
                spec((1, Q, GPS * NST), lambda e, g, s: (e, c(s), bc0 + g)),
                spec((1, Q, GPS * NST), lambda e, g, s: (e, c(s), bc0 + NG // GPS + g)),
                spec((1, GPS, Q, 128), lambda e, g, s: (e, g, c(s), 0)),
                spec((1, 1, NST, GPS * GWID), lambda e, g, s: (e, c(s), 0, g)),
                spec((1, Q, GPS * GWID), lambda e, g, s: (e, c(s), g)),
                spec((1, Q, GPS * GWID), lambda e, g, s: (e, c(s), g))]

    def outs(c):
        return [spec((1, Q, GPS * GWID), lambda e, g, s: (e, c(s), g)),
                spec((1, Q, GPS * 2 * NST), lambda e, g, s: (e, c(s), g)),
                spec((1, GPS, Q, 128), lambda e, g, s: (e, g, c(s), 0))]

    s_xs = jax.ShapeDtypeStruct((2, R, DIN), BF)
    s_bc = jax.ShapeDtypeStruct((2, R, 2 * NG * NST), BF)
    s_dt = jax.ShapeDtypeStruct((2, NG, R, 128), F32)
    return pl.pallas_call(
        body, name="ssd_bwd", grid=(2, NG // GPS, NS),
        in_specs=ins(cf) + ins(cb) + [spec((GPS, 8, 128), lambda e, g, s: (g, 0, 0))],
        out_specs=tuple(outs(cf) + outs(cb) + [spec((1, GPS, 8, 128), lambda e, g, s: (e, g, 0, 0))]),
        out_shape=(s_xs, s_bc, s_dt, s_xs, s_bc, s_dt, jax.ShapeDtypeStruct((2, NG, 8, 128), F32)),
        scratch_shapes=[pltpu.VMEM((2, NST, GPS * GWID), F32)],
        compiler_params=_params(("parallel", "parallel", "arbitrary")),
    )(xbc, xbc, xbc, dt_loc, hs_f, dy, y_f, xbc, xbc, xbc, dt_loc, hs_b, dy, y_b, a_loc)


def _ssd_post_fwd(y_f, y_b, xbc, proj3, dskip_e, ssd_norm, L):
    def body(yf_ref, yb_ref, xs_ref, z_ref, ds_ref, w_ref, o_ref, y2_ref):
        y2 = yf_ref[0] + yb_ref[0] + ds_ref[...] * xs_ref[0].astype(F32)
        y2_ref[0] = y2.astype(BF)
        z = z_ref[0].astype(F32)
        u = y2 * (z * _sigmoid(z))
        parts = []
        for g in range(NG):
            ug = u[:, g * GWID:(g + 1) * GWID]
            parts.append(ug * lax.rsqrt(jnp.mean(ug * ug, axis=-1, keepdims=True) + EPS))
        o_ref[0] = (jnp.concatenate(parts, axis=1) * w_ref[...]).astype(BF)

    blk = lambda c: pl.BlockSpec((1, X_TILE, DIN), lambda e, t: (e, t, c))
    vec = pl.BlockSpec((1, DIN), lambda e, t: (0, 0))
    return pl.pallas_call(
        body, name="ssd_post_fwd", grid=(2, L // X_TILE),
        in_specs=[blk(0), blk(0), blk(0), blk(1), vec, vec],
        out_specs=(blk(0), blk(0)),
        out_shape=(jax.ShapeDtypeStruct((2, L, DIN), BF), jax.ShapeDtypeStruct((2, L, DIN), BF)),
        compiler_params=_params(("parallel", "parallel")),
    )(y_f, y_b, xbc, proj3, dskip_e, ssd_norm)


def _ssd_post_bwd(d_yn, y2b, xbc, proj3, ssd_norm, L):
    _, R, _ = xbc.shape
    nx = L // ROW_TILE

    def body(dyn_ref, y2_ref, xs_ref, z_ref, w_ref, dy_ref, dz_ref, acc_ref):
        t = pl.program_id(1)

        @pl.when(t == 0)
        def _():
            acc_ref[...] = jnp.zeros_like(acc_ref)

        @pl.when(t >= nx)
        def _():
            dy_ref[...] = jnp.zeros_like(dy_ref)
            dz_ref[...] = jnp.zeros_like(dz_ref)

        @pl.when(t < nx)
        def _():
            xs = xs_ref[0].astype(F32)
            y2 = y2_ref[0].astype(F32)
            z = z_ref[0].astype(F32)
            sg = _sigmoid(z)
            sz = z * sg
            u = y2 * sz
            dyn = dyn_ref[0].astype(F32)
            dun = dyn * w_ref[...]
            uh_parts, du_parts = [], []
            for g in range(NG):
                sl = slice(g * GWID, (g + 1) * GWID)
                ug = u[:, sl]
                rg = lax.rsqrt(jnp.mean(ug * ug, axis=-1, keepdims=True) + EPS)
                uh = ug * rg
                dg = dun[:, sl]
                du_parts.append(rg * (dg - uh * jnp.mean(dg * uh, axis=-1, keepdims=True)))
                uh_parts.append(uh)
            du = jnp.concatenate(du_parts, axis=1)
            uh = jnp.concatenate(uh_parts, axis=1)
            dy2 = du * sz
            dy_ref[0] = dy2.astype(BF)
            dz_ref[0] = (du * y2 * (sg * (1.0 + z * (1.0 - sg)))).astype(BF)
            acc_ref[0, 0:1, :] += jnp.sum(dyn * uh, axis=0, keepdims=True)
            acc_ref[0, 1:2, :] += jnp.sum(dy2 * xs, axis=0, keepdims=True)

    xmap = lambda c: (lambda e, t: (e, jnp.minimum(t, nx - 1), c))
    blk = lambda c: pl.BlockSpec((1, ROW_TILE, DIN), xmap(c))
    oblk = pl.BlockSpec((1, ROW_TILE, DIN), lambda e, t: (e, t, 0))
    vec = pl.BlockSpec((1, DIN), lambda e, t: (0, 0))
    return pl.pallas_call(
        body, name="ssd_post_bwd", grid=(2, R // ROW_TILE),
        in_specs=[blk(0), blk(0), blk(0), blk(1), vec],
        out_specs=(oblk, oblk, pl.BlockSpec((1, 8, DIN), lambda e, t: (e, 0, 0))),
        out_shape=(jax.ShapeDtypeStruct((2, R, DIN), BF), jax.ShapeDtypeStruct((2, R, DIN), BF),
                   jax.ShapeDtypeStruct((2, 8, DIN), F32)),
        compiler_params=_params(("parallel", "arbitrary")),
    )(d_yn, y2b, xbc, proj3, ssd_norm)


def _merge_fwd(proj3, P, S, b_merge, L):
    def body(gp_ref, p_ref, s_ref, b_ref, o_ref):
        gt = _sigmoid(gp_ref[0].astype(F32) + b_ref[...])
        o_ref[0] = (gt[:, :D] * p_ref[0].astype(F32) + gt[:, D:] * s_ref[0].astype(F32)).astype(BF)

    blk = pl.BlockSpec((1, X_TILE, D), lambda e, t: (e, t, 0))
    return pl.pallas_call(
        body, name="merge_fwd", grid=(2, L // X_TILE),
        in_specs=[pl.BlockSpec((1, X_TILE, 2 * D), lambda e, t: (e, t, OFF_GATE // (2 * D))), blk, blk,
                  pl.BlockSpec((1, 2 * D), lambda e, t: (0, 0))],
        out_specs=blk, out_shape=jax.ShapeDtypeStruct((2, L, D), BF),
        compiler_params=_params(("parallel", "parallel")),
    )(proj3, P, S, b_merge)


def _merge_bwd(d_merged, proj3, P, S, b_merge, L):
    _, R, _ = proj3.shape
    nx = L // ROW_TILE

    def body(dm_ref, gp_ref, p_ref, s_ref, b_ref, dp_ref, ds_ref, dg_ref, acc_ref):
        t = pl.program_id(1)

        @pl.when(t == 0)
        def _():
            acc_ref[...] = jnp.zeros_like(acc_ref)

        @pl.when(t >= nx)
        def _():
            dg_ref[...] = jnp.zeros_like(dg_ref)

        @pl.when(t < nx)
        def _():
            gt = _sigmoid(gp_ref[0].astype(F32) + b_ref[...])
            dm = dm_ref[0].astype(F32)
            g1, g2 = gt[:, :D], gt[:, D:]
            dp_ref[0] = (dm * g1).astype(BF)
            ds_ref[0] = (dm * g2).astype(BF)
            dgp = jnp.concatenate([dm * p_ref[0].astype(F32) * g1 * (1.0 - g1),
                                   dm * s_ref[0].astype(F32) * g2 * (1.0 - g2)], axis=1)
            dg_ref[0] = dgp.astype(BF)
            acc_ref[0, 0:1, :] += jnp.sum(dgp, axis=0, keepdims=True)

    xmap = lambda e, t: (e, jnp.minimum(t, nx - 1), 0)
    blk = pl.BlockSpec((1, ROW_TILE, D), xmap)
    return pl.pallas_call(
        body, name="merge_bwd", grid=(2, R // ROW_TILE),
        in_specs=[blk, pl.BlockSpec((1, ROW_TILE, 2 * D), lambda e, t: (e, jnp.minimum(t, nx - 1), OFF_GATE // (2 * D))),
                  blk, blk, pl.BlockSpec((1, 2 * D), lambda e, t: (0, 0))],
        out_specs=(blk, blk, pl.BlockSpec((1, ROW_TILE, 2 * D), lambda e, t: (e, t, 0)),
                   pl.BlockSpec((1, 8, 2 * D), lambda e, t: (e, 0, 0))),
        out_shape=(jax.ShapeDtypeStruct((2, L, D), BF), jax.ShapeDtypeStruct((2, L, D), BF),
                   jax.ShapeDtypeStruct((2, R, 2 * D), BF), jax.ShapeDtypeStruct((2, 8, 2 * D), F32)),
        compiler_params=_params(("parallel", "arbitrary")),
    )(d_merged, proj3, P, S, b_merge)


def _final(out3, x, tgt, gtab, norm_post, L):
    def body(o_ref, x_ref, t_ref, g_ref, n_ref, dxo_ref, do_ref, acc_ref):
        @pl.when(pl.program_id(1) == 0)
        def _():
            acc_ref[...] = jnp.zeros_like(acc_ref)

        o = o_ref[0].astype(F32)
        gate = g_ref[0, 0:1, :]
        npost = n_ref[...]
        r2 = lax.rsqrt(jnp.mean(o * o, axis=-1, keepdims=True) + EPS)
        nh = o * r2
        on = nh * npost
        err = x_ref[0] + gate * on - t_ref[0]
        dxo = err * (1.0 / D)
        dxo_ref[0] = dxo.astype(BF)
        dnh = dxo * gate * npost
        do_ref[0] = (r2 * (dnh - nh * jnp.mean(dnh * nh, axis=-1, keepdims=True))).astype(BF)
        acc_ref[0, 0:1, :] += jnp.sum(dxo * on, axis=0, keepdims=True)
        acc_ref[0, 1:2, :] += jnp.sum(dxo * gate * nh, axis=0, keepdims=True)
        acc_ref[0, 2:3, :] += jnp.sum(err * err, axis=0, keepdims=True)

    blk = pl.BlockSpec((1, X_TILE, D), lambda e, t: (e, t, 0))
    return pl.pallas_call(
        body, name="final", grid=(2, L // X_TILE),
        in_specs=[blk, blk, blk, pl.BlockSpec((1, 8, D), lambda e, t: (e, 0, 0)),
                  pl.BlockSpec((1, D), lambda e, t: (0, 0))],
        out_specs=(blk, blk, pl.BlockSpec((1, 8, D), lambda e, t: (e, 0, 0))),
        out_shape=(jax.ShapeDtypeStruct((2, L, D), BF), jax.ShapeDtypeStruct((2, L, D), BF),
                   jax.ShapeDtypeStruct((2, 8, D), F32)),
        compiler_params=_params(("parallel", "arbitrary")),
    )(out3, x, tgt, gtab, norm_post)


def _local_step(x, c, ctx, loss_target, W, late_shard=None, exchange=False):
    nb, L, _ = x.shape
    LC = ctx.shape[1]
    R = L + LC
    assert nb == 2 and L % ROW_TILE == 0 and LC % Q == 0 and L % POOL_TILE == 0
    w_inT = W["w_in"]
    w_dtT = jnp.pad(w_inT[OFF_DT:], ((0, 64), (0, 0)))
    tables = _pool_tables(L)
    tr, tl = (2 * R) // 8, (2 * L) // 8

    c16 = jnp.zeros((16, D), F32).at[0:2].set(c).at[2].set(W["c_ctx"])
    mod16 = _adaln_fwd(c16, W["w_ada"], W["b_ada"])
    shift, scale, gate = mod16[:, :D], mod16[:, D:2 * D], mod16[:, 2 * D:]
    npre = W["norm_pre"]
    tab = jnp.zeros((2, 2, 8, D), F32)
    for e in range(2):
        tab = tab.at[e, 0, 0].set(npre[0] * (1.0 + scale[e])).at[e, 0, 1].set(shift[e])
        tab = tab.at[e, 1, 0].set(npre[0] * (1.0 + scale[2])).at[e, 1, 1].set(shift[2])
    gtab = jnp.zeros((2, 8, D), F32).at[:, 0].set(gate[0:2])

    hx = _norm_mod_fwd(x, ctx, tab)
    hx2 = hx.reshape(2 * R, D)
    if late_shard is None:
        proj = _matmul(hx2, w_inT, BF, "proj_main", tm=2 * tr, tn=1024, bt=True, n=OFF_DT)
    else:
        proj, late = _matmul(hx2, w_inT, BF, "proj_main", tm=2 * tr, tn=1024, bt=True, n=OFF_DT, side=_gather_side(late_shard))
        W = {**W, **_unpack_gather(late, GATHER_LATE)}
    proj3 = proj.reshape(2, R, OFF_DT)
    dt_raw = _matmul(hx2, w_dtT, F32, "proj_dt", tm=tr, bt=True).reshape(2, R, 128)
    ypool = _pool_fwd(proj3, W["pool_w"], W["pool_scale"], tables, L)
    xbc = _conv_fwd(proj3, W["conv_w"], W["conv_b"], L)
    bias128 = jnp.pad(W["dt_bias"].reshape(1, 64), ((0, 0), (0, 64)))
    dt_loc = _dt_fwd(dt_raw, bias128)
    A = -jnp.exp(W["a_log"].reshape(2, NG, HPG))
    a_loc = jnp.zeros((NG, 8, 128), F32).at[:, 0, :16].set(A.transpose(1, 0, 2).reshape(NG, 16))
    y_f, hs_f, y_b, hs_b = _ssd_fwd(xbc, dt_loc, a_loc, L)
    dskip_e = jnp.repeat(W["d_skip"].reshape(1, 32), HEAD, axis=1)
    yn, y2b = _ssd_post_fwd(y_f, y_b, xbc, proj3, dskip_e, W["ssd_norm"], L)
    ypool2, yn2 = ypool.reshape(2 * L, D), yn.reshape(2 * L, DIN)
    P = _matmul(ypool2, W["w_proj_pool"], BF, "proj_pool", tm=2 * tl, tn=1024).reshape(2, L, D)
    S = _matmul(yn2, W["w_proj_ssd"], BF, "proj_ssd", tm=2 * tl, tn=1024).reshape(2, L, D)
    merged = _merge_fwd(proj3, P, S, W["b_merge"], L)
    merged2 = merged.reshape(2 * L, D)
    out3 = _matmul(merged2, W["w_out"], BF, "proj_out", tm=2 * tl, tn=1024).reshape(2, L, D)
    dxo, dout, acc_f = _final(out3, x, loss_target, gtab, W["norm_post"], L)

    dout2 = dout.reshape(2 * L, D)
    g = {}
    g["w_out"] = _matmul_tn(merged2, dout2, "dw_out", ta=1024, tn=1024, tr=4 * tl)
    d_merged = _matmul(dout2, W["w_out"], BF, "d_merged", tm=2 * tl, tn=1024, bt=True).reshape(2, L, D)
    dP, dS, dgp, acc_m = _merge_bwd(d_merged, proj3, P, S, W["b_merge"], L)
    dP2, dS2 = dP.reshape(2 * L, D), dS.reshape(2 * L, D)
    g["w_proj_pool"] = _matmul_tn(ypool2, dP2, "dw_proj_pool", ta=1024, tn=1024, tr=4 * tl)
    g["w_proj_ssd"] = _matmul_tn(yn2, dS2, "dw_proj_ssd", ta=1024, tn=1024, tr=4 * tl)
    d_ypool = _matmul(dP2, W["w_proj_pool"], BF, "d_ypool", tm=2 * tl, tn=1024, bt=True).reshape(2, L, D)
    d_yn = _matmul(dS2, W["w_proj_ssd"], BF, "d_yn", tm=2 * tl, tn=1024, bt=True).reshape(2, L, DIN)
    dv, dzp, g["pool_w"], acc_p = _pool_bwd(proj3, d_ypool, W["pool_w"], jnp.swapaxes(W["pool_w"], 1, 2),
                                            W["pool_scale"], tables, L)
    dy2, dzs, acc_s = _ssd_post_bwd(d_yn, y2b, xbc, proj3, W["ssd_norm"], L)
    dxs_f, dbc_f, ddt_f, dxs_b, dbc_b, ddt_b, acc_a = _ssd_bwd(xbc, dt_loc, a_loc, hs_f, hs_b, y_f, y_b, dy2, L)
    ident = lambda j: j
    dxr_xs, acc_cx = _conv_bwd(proj3, [dxs_f, dxs_b, dy2], [None, None, dskip_e], 0, DIN, [ident, ident, ident],
                               W["conv_w"], W["conv_b"], L, "conv_bwd_xs")
    bcmap = lambda j: 2 * lax.rem(j, NG) + j // NG
    dxr_bc, acc_cb = _conv_bwd(proj3, [dbc_f, dbc_b], [None, None], DIN, 2 * NG * NST, [bcmap, bcmap],
                               W["conv_w"], W["conv_b"], L, "conv_bwd_bc")
    ddtr, acc_d = _dt_bwd(dt_raw, bias128, ddt_f, ddt_b)
    pieces = [dv, dzp, dzs, dgp, dxr_xs, dxr_bc]
    dw_rows = [_matmul_tn(p.reshape(2 * R, p.shape[2]), hx2, "dw_in_%d" % i, ta=1024, tn=1024, tr=4 * tr)
               for i, p in enumerate(pieces)]
    dw_rows.append(_matmul_tn(ddtr.reshape(2 * R, 128), hx2, "dw_in_dt", ta=128, tn=1024, tr=tr)[:64])
    g["w_in"] = jnp.concatenate(dw_rows, axis=0)
    acc_c = jnp.concatenate([acc_cx[0] + acc_cx[1], acc_cb[0] + acc_cb[1]], axis=1)
    g["conv_w"] = acc_c[0:4]
    g["conv_b"] = acc_c[4:5]
    if exchange:
        gb = _pack_grads(g, GRADS_EARLY)
        pair = _pair_add(gb, _pair_exchange(gb, None, "grads_pair_exchange_early"), "grads_pair_add_early")
        dh, recv_early = _dhx(pieces, ddtr, w_inT, w_dtT, side=_chip_exchange_side(pair))
    else:
        dh, recv_early = _dhx(pieces, ddtr, w_inT, w_dtT), None
    grad_x, acc_n = _norm_mod_bwd(dh, x, ctx, tab, dxo)
    g["w_ada"], db_rows, sm_rows = _adaln_bwd(acc_n, acc_f, mod16, c16, npre, W["w_ada"])

    g["b_ada"] = db_rows[0:1]
    g["norm_pre"] = sm_rows[0:1]
    g["c_ctx"] = sm_rows[1]
    g["norm_post"] = acc_f[0, 1:2] + acc_f[1, 1:2]
    g["b_merge"] = acc_m[0, 0:1] + acc_m[1, 0:1]
    g["pool_scale"] = acc_p[:, 0, :].reshape(1, D)
    g["dt_bias"] = (acc_d[0, 0, :64] + acc_d[1, 0, :64]).reshape(2, 32)
    dA = (acc_a[0, :, 0, :16] + acc_a[1, :, 0, :16]).reshape(NG, 2, HPG).transpose(1, 0, 2)
    g["a_log"] = (dA * A).reshape(2, 32)
    g["d_skip"] = (acc_s[0, 1] + acc_s[1, 1]).reshape(32, HEAD).sum(axis=1).reshape(1, 32)
    g["ssd_norm"] = acc_s[0, 0:1] + acc_s[1, 0:1]
    loss_lanes = acc_f[:, 2, :]
    return loss_lanes, grad_x, g, recv_early


MESH = pl.DeviceIdType.MESH
ANY = pl.BlockSpec(memory_space=pl.ANY)


def _all_gather(shard):
    m_per, n = shard.shape

    def body(x_ref, out_ref, send_sems, recv_sems, local_sem):
        x, y, c = lax.axis_index("x"), lax.axis_index("y"), lax.axis_index("c")
        me, sibling = (x, y, c), (x, y, 1 - c)
        chips = [(1 - x, y), (x, 1 - y), (1 - x, 1 - y)]

        def rows(px, py, pc):
            return out_ref.at[pl.ds((4 * px + 2 * py + pc) * m_per, m_per), :]

        def copy(k, block, to, src=None):
            return pltpu.make_async_remote_copy(
                src_ref=rows(*block) if src is None else src, dst_ref=rows(*block),
                send_sem=send_sems.at[k], recv_sem=recv_sems.at[k], device_id=to, device_id_type=MESH)

        mine = pltpu.make_async_copy(x_ref, rows(*me), local_sem)
        mine.start()
        first = [copy(0, me, sibling, src=x_ref)]
        first += [copy(1 + j, me, (*chip, c), src=x_ref) for j, chip in enumerate(chips)]
        for cp in first:
            cp.start()
        passed = [copy(4 + j, (*chip, c), sibling) for j, chip in enumerate(chips)]
        for j, chip in enumerate(chips):
            copy(1 + j, (*chip, c), me).wait_recv()
            passed[j].start()
        copy(0, sibling, me).wait_recv()
        for j, chip in enumerate(chips):
            copy(4 + j, (*chip, 1 - c), me).wait_recv()
        for cp in first + passed:
            cp.wait_send()
        mine.wait()

    return pl.pallas_call(
        body, name="all_gather_weights",
        out_shape=jax.ShapeDtypeStruct((NDEV * m_per, n), shard.dtype),
        in_specs=[ANY], out_specs=ANY,
        scratch_shapes=[pltpu.SemaphoreType.DMA((7,)), pltpu.SemaphoreType.DMA((7,)), pltpu.SemaphoreType.DMA],
    )(shard)


PAIR_PIECES = 12


def _xor_peer(k, x, y, c):
    return (1 - x if k & 4 else x, 1 - y if k & 2 else y, 1 - c if k & 1 else c)


def _pair_exchange(big, small, name):
    _, nq, rows, n = big.shape
    piece = rows // PAIR_PIECES
    assert piece * PAIR_PIECES == rows and piece % 16 == 0
    with_small = small is not None

    def body(*refs):
        if with_small:
            big_ref, small_ref, got_ref, osmall_ref, send_sems, recv_sems, local_sem = refs
        else:
            big_ref, got_ref, send_sems, recv_sems, local_sem = refs
        x, y, c = lax.axis_index("x"), lax.axis_index("y"), lax.axis_index("c")
        me = 4 * x + 2 * y + c

        def rc(src, dst, sem, peer):
            return pltpu.make_async_remote_copy(src_ref=src, dst_ref=dst, send_sem=send_sems.at[sem],
                                                recv_sem=recv_sems.at[sem], device_id=peer, device_id_type=MESH)

        sib = _xor_peer(1, x, y, c)
        local, sends, recvs = [], [], []
        for q in range(nq):
            for h in range(PAIR_PIECES):
                rws = pl.ds(h * piece, piece)
                cp = rc(big_ref.at[1 - c, q, rws], got_ref.at[q, rws], 8 + q * PAIR_PIECES + h, sib)
                sends.append(cp)
                recvs.append(cp)
        if with_small:
            local.append(pltpu.make_async_copy(small_ref, osmall_ref.at[me], local_sem))
            for k in range(1, NDEV):
                px, py, pc = _xor_peer(k, x, y, c)
                sends.append(rc(small_ref, osmall_ref.at[me], k, (px, py, pc)))
                recvs.append(rc(small_ref, osmall_ref.at[4 * px + 2 * py + pc], k, (px, py, pc)))
        for cp in local + sends:
            cp.start()
        for cp in sends:
            cp.wait_send()
        for cp in recvs:
            cp.wait_recv()
        for cp in local:
            cp.wait()

    nsem = 8 + nq * PAIR_PIECES
    out_shape = [jax.ShapeDtypeStruct(big.shape[1:], big.dtype)]
    if with_small:
        out_shape.append(jax.ShapeDtypeStruct((NDEV,) + small.shape, small.dtype))
    out = pl.pallas_call(
        body, name=name, out_shape=tuple(out_shape),
        in_specs=[ANY] * (1 + with_small), out_specs=(ANY,) * (1 + with_small),
        scratch_shapes=[pltpu.SemaphoreType.DMA((nsem,)), pltpu.SemaphoreType.DMA((nsem,)), pltpu.SemaphoreType.DMA],
    )(*((big, small) if with_small else (big,)))
    return out if with_small else out[0]


def _pair_add(big, got, name):
    _, nq, rows, n = big.shape
    tile = rows // 4
    assert rows % 64 == 0

    def body(c_ref, a_ref, b_ref, o_ref):
        o_ref[0] = (a_ref[0, 0].astype(F32) + b_ref[0].astype(F32)).astype(BF)

    blk = pl.BlockSpec((1, tile, n), lambda q, i, c_ref: (q, i, 0))
    return pl.pallas_call(
        body, name=name,
        grid_spec=pltpu.PrefetchScalarGridSpec(
            num_scalar_prefetch=1, grid=(nq, rows // tile),
            in_specs=[pl.BlockSpec((1, 1, tile, n), lambda q, i, c_ref: (c_ref[0], q, i, 0)), blk], out_specs=blk),
        out_shape=jax.ShapeDtypeStruct(got.shape, BF), compiler_params=_params(("parallel", "parallel")),
    )(lax.axis_index("c").astype(jnp.int32).reshape(1), big, got)


def _chip_exchange_side(pair):
    def make(in_refs, out_refs, send_sems, recv_sems, local_sem, arrivals=True):
        (in_ref,), (out_ref,) = in_refs, out_refs
        x, y, c = lax.axis_index("x"), lax.axis_index("y"), lax.axis_index("c")
        q = 2 * x + y
        local = [pltpu.make_async_copy(in_ref.at[q], out_ref.at[q], local_sem)]
        sends, recvs = [], []
        for j in range(1, 4):
            px, py, pc = _xor_peer(2 * j, x, y, c)
            pq = 2 * px + py
            for lst, dst in ((sends, out_ref.at[q]), (recvs, out_ref.at[pq]))[:1 + arrivals]:
                lst.append(pltpu.make_async_remote_copy(
                    src_ref=in_ref.at[pq], dst_ref=dst, send_sem=send_sems.at[j - 1], recv_sem=recv_sems.at[j - 1],
                    device_id=(px, py, pc), device_id_type=MESH))
        return local, sends, recvs

    return _SideCopies([pair], [jax.ShapeDtypeStruct(pair.shape, pair.dtype)], make)


def _gather_side(shard):
    def make(in_refs, out_refs, send_sems, recv_sems, local_sem, arrivals=True):
        (src,), (dst,) = in_refs, out_refs
        x, y, c = lax.axis_index("x"), lax.axis_index("y"), lax.axis_index("c")
        me = 4 * x + 2 * y + c
        local = [pltpu.make_async_copy(src, dst.at[me], local_sem)]
        sends, recvs = [], []
        for k in range(1, NDEV):
            px, py, pc = _xor_peer(k, x, y, c)
            for lst, slot in ((sends, me), (recvs, 4 * px + 2 * py + pc))[:1 + arrivals]:
                lst.append(pltpu.make_async_remote_copy(
                    src_ref=src, dst_ref=dst.at[slot], send_sem=send_sems.at[k - 1], recv_sem=recv_sems.at[k - 1],
                    device_id=(px, py, pc), device_id_type=MESH))
        return local, sends, recvs

    return _SideCopies([shard], [jax.ShapeDtypeStruct((NDEV,) + shard.shape, shard.dtype)], make)


ADAM_TILE = 64
PACK_W = 1024


def _adamw(recv, w, m, v, name, side=None):
    rp = w.shape[0]
    tile = min(ADAM_TILE, rp)
    nsrc = recv.shape[0]
    grid = (rp // tile,)
    n_si, n_so = (len(side.inputs), len(side.out_shapes)) if side else (0, 0)

    def body(*refs):
        r_ref, w_ref, m_ref, v_ref = refs[:4]
        g_ref, d_ref, nm_ref, nv_ref = refs[4 + n_si:8 + n_si]
        side_refs = (refs[4:4 + n_si], refs[8 + n_si:8 + n_si + n_so], refs[8 + n_si + n_so:])
        if side:
            side.start(grid, *side_refs)
        g = r_ref[0].astype(F32)
        for i in range(1, nsrc):
            g = g + r_ref[i].astype(F32)
        m1 = ADAM_B1 * m_ref[...] + (1.0 - ADAM_B1) * g
        v1 = ADAM_B2 * v_ref[...] + (1.0 - ADAM_B2) * (g * g)
        m_hat = m1 / (1.0 - ADAM_B1 ** ADAM_STEP)
        v_hat = v1 / (1.0 - ADAM_B2 ** ADAM_STEP)
        g_ref[...] = g
        d_ref[...] = -ADAM_LR * (m_hat / (jnp.sqrt(v_hat) + ADAM_EPS) + ADAM_WD * w_ref[...])
        nm_ref[...] = m1
        nv_ref[...] = v1
        if side:
            side.wait(grid, *side_refs)

    blk = pl.BlockSpec((tile, PACK_W), lambda i: (i, 0))
    shp = jax.ShapeDtypeStruct((rp, PACK_W), F32)
    return pl.pallas_call(
        body, name=name, grid=grid,
        in_specs=[pl.BlockSpec((nsrc, tile, PACK_W), lambda i: (0, i, 0)), blk, blk, blk] + [ANY] * n_si,
        out_specs=(blk, blk, blk, blk) + (ANY,) * n_so,
        out_shape=(shp, shp, shp, shp) + tuple(side.out_shapes if side else ()),
        scratch_shapes=side.scratch() if side else [],
        compiler_params=_params(("arbitrary",) if side else ("parallel",)),
    )(recv, w, m, v, *(side.inputs if side else ()))


BIG = {"w_ada": ((3 * D, D), 0), "pool_w": ((4, PGW, PGW), 1), "w_proj_pool": ((D, D), 0), "w_proj_ssd": ((DIN, D), 0),
       "w_out": ((D, D), 0), "w_in": ((IN_COLS, D), 0), "conv_w": ((4, CONV_DIM), 1)}
TRANSPOSED = ("w_ada", "w_in")
PACK_ROWS = {"w_ada": 384, "w_in": 1168, "conv_w": 16, "pool_w": 32, "w_proj_pool": 128, "w_proj_ssd": 256, "w_out": 128}
GATHER_EARLY = ("w_ada", "w_in", "conv_w")
GATHER_LATE = ("pool_w", "w_proj_pool", "w_proj_ssd", "w_out")
GRADS_LATE = ("w_ada",)
GRADS_EARLY = tuple(n for n in PACK_ROWS if n not in GRADS_LATE)
SMALL = {"c_ctx": (D,), "b_ada": (1, 3 * D), "norm_pre": (1, D), "norm_post": (1, D), "b_merge": (1, 2 * D),
         "pool_scale": (1, D), "conv_b": (1, CONV_DIM), "dt_bias": (2, 32), "a_log": (2, 32), "d_skip": (1, 32),
         "ssd_norm": (1, DIN)}
LOSS_SLOT = 128
assert all(_r % 16 == 0 for _r in PACK_ROWS.values())
SMALL_ROWS = 16


def _shard_shape(name):
    shape, ax = BIG[name]
    return tuple(s // NDEV if i == ax else s for i, s in enumerate(shape))


def _as_rows(t, rows):
    pad = [(0, 0)] * (t.ndim - 1) + [(0, rows * PACK_W - t.shape[-1])]
    return jnp.pad(t, pad).reshape(t.shape[:-1] + (rows, PACK_W))


def _shard_rows(t, name):
    sh, r = _shard_shape(name), PACK_ROWS[name]
    lead = t.shape[:t.ndim - len(sh)]
    if len(sh) == 2 and sh[1] == PACK_W:
        return jnp.pad(t, [(0, 0)] * len(lead) + [(0, r - sh[0]), (0, 0)])
    if int(np.prod(sh)) == r * PACK_W:
        return t.reshape(lead + (r, PACK_W))
    return _as_rows(t.reshape(lead + (-1,)), r)


def _to_chunks(full, name):
    shape, ax = BIG[name]
    split = shape[:ax] + (NDEV, shape[ax] // NDEV) + shape[ax + 1:]
    return _shard_rows(jnp.moveaxis(full.reshape(split), ax, 0), name)


def _from_chunks(chunks, name):
    shape, ax = BIG[name]
    return jnp.moveaxis(chunks.reshape((NDEV,) + _shard_shape(name)), 0, ax).reshape(shape)


def _rows_of(names):
    return sum(PACK_ROWS[n] for n in names)


def _pack_state(t, names):
    return jnp.concatenate([_shard_rows(t[n], n) for n in names], axis=0)


def _pack_small(t, loss_part=None):
    slot = jnp.zeros((LOSS_SLOT,), F32)
    if loss_part is not None:
        slot = slot.at[0].set(loss_part)
    return _as_rows(jnp.concatenate([t[n].reshape(-1) for n in SMALL] + [slot]), SMALL_ROWS)


def _pack_grads(g, names):
    big = jnp.concatenate([_to_chunks(g[n], n).astype(BF) for n in names], axis=1)
    return jnp.swapaxes(big.reshape(4, 2, _rows_of(names), PACK_W), 0, 1)


def _unpack_state(big, names):
    out, off = {}, 0
    for n in names:
        sh, r = _shard_shape(n), PACK_ROWS[n]
        k = int(np.prod(sh))
        if len(sh) == 2 and sh[1] == PACK_W:
            out[n] = big[off:off + sh[0]]
        else:
            out[n] = big[off:off + r].reshape(-1)[:k].reshape(sh)
        off += r
    return out


def _unpack_small(small):
    out, flat, off = {}, small.reshape(-1), 0
    for n, sh in SMALL.items():
        k = int(np.prod(sh))
        out[n] = flat[off:off + k].reshape(sh)
        off += k
    out["loss"] = flat[off]
    return out


def _pack_gather(w, names):
    pieces = []
    for n in names:
        if n == "conv_w":
            pieces.append(_as_rows(jnp.concatenate([p.reshape(-1) for p in _split(w[n], 3)]), PACK_ROWS[n]))
        else:
            pieces.append(_shard_rows(w[n], n).astype(BF))
    return jnp.concatenate(pieces, axis=0)


def _unpack_gather(gathered, names):
    g = gathered.reshape(NDEV, _rows_of(names), PACK_W)
    out, off = {}, 0
    for n in names:
        r = PACK_ROWS[n]
        sh = _shard_shape(n)
        if n == "conv_w":
            k = int(np.prod(sh))
            terms = g[:, off:off + r].reshape(NDEV, -1)[:, :3 * k].astype(F32).reshape(NDEV, 3, k)
            out[n] = _from_chunks(terms[:, 0] + terms[:, 1] + terms[:, 2], n)
        elif len(sh) == 2 and sh[1] == PACK_W:
            out[n] = _from_chunks(g[:, off:off + sh[0]], n)
        else:
            out[n] = _from_chunks(g[:, off:off + r], n)
        off += r
    return out


PARAMS = ["c_ctx", "w_ada", "b_ada", "norm_pre", "norm_post", "w_in", "b_merge", "pool_w", "pool_scale", "conv_w", "conv_b",
          "dt_bias", "a_log", "d_skip", "ssd_norm", "w_proj_pool", "w_proj_ssd", "w_out"]


def kernel(x, c, ctx, c_ctx, w_ada, b_ada, norm_pre, norm_post, w_in, b_merge, pool_w, pool_scale, conv_w, conv_b, dt_bias, a_log, d_skip, ssd_norm, w_proj_pool, w_proj_ssd, w_out, loss_target, m_c_ctx, m_w_ada, m_b_ada, m_norm_pre, m_norm_post, m_w_in, m_b_merge, m_pool_w, m_pool_scale, m_conv_w, m_conv_b, m_dt_bias, m_a_log, m_d_skip, m_ssd_norm, m_w_proj_pool, m_w_proj_ssd, m_w_out, v_c_ctx, v_w_ada, v_b_ada, v_norm_pre, v_norm_post, v_w_in, v_b_merge, v_pool_w, v_pool_scale, v_conv_w, v_conv_b, v_dt_bias, v_a_log, v_d_skip, v_ssd_norm, v_w_proj_pool, v_w_proj_ssd, v_w_out):
    given = dict(locals())
    shapes = {n: given[n].shape for n in PARAMS}

    def local(prefix):
        t = {n: (given[prefix + n] if n == "c_ctx" else given[prefix + n][0]) for n in PARAMS}
        for n in TRANSPOSED:
            t[n] = t[n].T
        return {n: t[n].reshape(_shard_shape(n) if n in BIG else SMALL[n]) for n in PARAMS}

    w, m, v = local(""), local("m_"), local("v_")

    W = _unpack_gather(_all_gather(_pack_gather(w, GATHER_EARLY)), GATHER_EARLY)
    for n in SMALL:
        W[n] = w[n]
    lanes, grad_x, g, recv_early = _local_step(x, c, ctx, loss_target, W, late_shard=_pack_gather(w, GATHER_LATE),
                                               exchange=True)
    gb = _pack_grads(g, GRADS_LATE)
    got, recv_small = _pair_exchange(gb, _pack_small(g, (0.5 / D) * jnp.sum(lanes)), "grads_pair_exchange_late")
    late = _chip_exchange_side(_pair_add(gb, got, "grads_pair_add_late"))
    res = [{} for _ in range(4)]
    *early, recv_late = _adamw(recv_early, *[_pack_state(s, GRADS_EARLY) for s in (w, m, v)], "adamw_early", side=late)
    for r, t in zip(res, early):
        r.update(_unpack_state(t, GRADS_EARLY))
    for r, t in zip(res, _adamw(recv_late, *[_pack_state(s, GRADS_LATE) for s in (w, m, v)], "adamw_late")):
        r.update(_unpack_state(t, GRADS_LATE))
    for r, t in zip(res, _adamw(recv_small, *[_pack_small(s) for s in (w, m, v)], "adamw_small")):
        r.update(_unpack_small(t))
    outs = [res[0]["loss"], grad_x]
    for r in res:
        for n in TRANSPOSED:
            r[n] = r[n].T
        outs += [r[n].reshape(shapes[n]) for n in PARAMS]
    return tuple(outs)
```

```python
import functools

import numpy as np
import jax
import jax.numpy as jnp
from jax import lax
from jax.experimental import pallas as pl
from jax.experimental.pallas import tpu as pltpu

F32, BF = jnp.float32, jnp.bfloat16

D = 1024
GRID_W = 64
EPS = 1e-6
POOL_WINDOWS = (2, 4, 8, 16)
PGW = 256
DIN = 2048
HEAD = 64
NST = 128
NG = 4
HPG = 8
GWID = HPG * HEAD
Q = 128
CONV_DIM = 3072
OFF_GATE, OFF_XBC, OFF_DT, IN_COLS = 4096, 6144, 9216, 9280
NDEV = 8
ADAM_LR, ADAM_B1, ADAM_B2, ADAM_EPS, ADAM_WD, ADAM_STEP = 0.001, 0.9, 0.999, 1e-08, 0.01, 10

V7X_VMEM_LIMIT = 56 * 2 ** 20
ROW_TILE = 256
X_TILE = 512


def _params(sem=None):
    return pltpu.CompilerParams(dimension_semantics=sem, vmem_limit_bytes=V7X_VMEM_LIMIT)


def _dot(a, b):
    return jnp.dot(a.astype(BF), b.astype(BF), preferred_element_type=F32)


def _dot_nt(a, b):
    return lax.dot_general(a.astype(BF), b.astype(BF), (((1,), (1,)), ((), ())), preferred_element_type=F32)


def _dot_tn(a, b):
    return lax.dot_general(a.astype(BF), b.astype(BF), (((0,), (0,)), ((), ())), preferred_element_type=F32)


def _split(a, n):
    parts = []
    for _ in range(n):
        p = a.astype(BF)
        parts.append(p)
        a = a - p.astype(F32)
    return parts


def _dot_sl(a, b01, n=3):
    parts = _split(a, n)
    m = a.shape[0]
    if n == 1 or m % 16:
        return sum(jnp.dot(p, b01, preferred_element_type=F32) for p in parts)
    r = jnp.dot(jnp.concatenate(parts, axis=0), b01, preferred_element_type=F32)
    return sum(r[i * m:(i + 1) * m] for i in range(n))


def _dot_sr(a01, b, n=3):
    parts = _split(b, n)
    k = b.shape[1]
    if n == 1 or k % 128:
        return sum(jnp.dot(a01, p, preferred_element_type=F32) for p in parts)
    r = jnp.dot(a01, jnp.concatenate(parts, axis=1), preferred_element_type=F32)
    return sum(r[:, i * k:(i + 1) * k] for i in range(n))


def _sigmoid(x):
    return 1.0 / (1.0 + jnp.exp(-x))


class _SideCopies:
    NSEM = 8

    def __init__(self, inputs, out_shapes, make):
        self.inputs, self.out_shapes, self.make = list(inputs), list(out_shapes), make

    def scratch(self):
        return [pltpu.SemaphoreType.DMA((self.NSEM,)), pltpu.SemaphoreType.DMA((self.NSEM,)), pltpu.SemaphoreType.DMA]

    def start(self, grid, in_refs, out_refs, sems):
        @pl.when(functools.reduce(lambda p, q: p & q, [pl.program_id(i) == 0 for i in range(len(grid))]))
        def _():
            local, sends, _ = self.make(in_refs, out_refs, *sems, arrivals=False)
            for cp in local + sends:
                cp.start()

    def wait(self, grid, in_refs, out_refs, sems):
        @pl.when(functools.reduce(lambda p, q: p & q, [pl.program_id(i) == n - 1 for i, n in enumerate(grid)]))
        def _():
            local, sends, recvs = self.make(in_refs, out_refs, *sems)
            for cp in sends:
                cp.wait_send()
            for cp in recvs:
                cp.wait_recv()
            for cp in local:
                cp.wait()


def _matmul(a, b, out_dtype, name, tm=512, tn=512, tk=1024, bt=False, n=None, side=None):
    M, K = a.shape
    N = n if n is not None else (b.shape[0] if bt else b.shape[1])
    tm, tn, tk = min(tm, M), min(tn, N), min(tk, K)
    assert M % tm == 0 and N % tn == 0 and K % tk == 0, (a.shape, b.shape)
    nk = K // tk
    grid = (M // tm, N // tn, nk)
    n_si, n_so = (len(side.inputs), len(side.out_shapes)) if side else (0, 0)

    def body(*refs):
        a_ref, b_ref, o_ref = refs[0], refs[1], refs[2 + n_si]
        acc = refs[3 + n_si + n_so]
        side_refs = (refs[2:2 + n_si], refs[3 + n_si:3 + n_si + n_so], refs[4 + n_si + n_so:])
        if side:
            side.start(grid, *side_refs)
        k = pl.program_id(2)
        p = _dot_nt(a_ref[...], b_ref[...]) if bt else _dot(a_ref[...], b_ref[...])

        @pl.when(k == 0)
        def _():
            acc[...] = p

        @pl.when(k > 0)
        def _():
            acc[...] += p

        @pl.when(k == nk - 1)
        def _():
            o_ref[...] = acc[...].astype(o_ref.dtype)

        if side:
            side.wait(grid, *side_refs)

    out = pl.pallas_call(
        body, name=name, grid=grid,
        in_specs=[pl.BlockSpec((tm, tk), lambda i, j, k: (i, k)),
                  pl.BlockSpec((tn, tk), lambda i, j, k: (j, k)) if bt else pl.BlockSpec((tk, tn), lambda i, j, k: (k, j))]
        + [ANY] * n_si,
        out_specs=(pl.BlockSpec((tm, tn), lambda i, j, k: (i, j)),) + (ANY,) * n_so,
        out_shape=(jax.ShapeDtypeStruct((M, N), out_dtype),) + tuple(side.out_shapes if side else ()),
        scratch_shapes=[pltpu.VMEM((tm, tn), F32)] + (side.scratch() if side else []),
        compiler_params=_params(("arbitrary",) * 3 if side else ("parallel", "parallel", "arbitrary")),
    )(a, b, *(side.inputs if side else ()))
    return out if side else out[0]


def _matmul_tn(a, g, name, ta=512, tn=512, tr=512):
    M, Ka = a.shape
    N = g.shape[1]
    ta, tn, tr = min(ta, Ka), min(tn, N), min(tr, M)
    assert M % tr == 0 and N % tn == 0 and Ka % ta == 0, (a.shape, g.shape)
    nr = M // tr

    def body(a_ref, g_ref, o_ref):
        k = pl.program_id(2)
        p = _dot_tn(a_ref[...], g_ref[...])

        @pl.when(k == 0)
        def _():
            o_ref[...] = p

        @pl.when(k > 0)
        def _():
            o_ref[...] += p

    return pl.pallas_call(
        body, name=name, grid=(Ka // ta, N // tn, nr),
        in_specs=[pl.BlockSpec((tr, ta), lambda i, j, k: (k, i)), pl.BlockSpec((tr, tn), lambda i, j, k: (k, j))],
        out_specs=pl.BlockSpec((ta, tn), lambda i, j, k: (i, j)),
        out_shape=jax.ShapeDtypeStruct((Ka, N), F32),
        compiler_params=_params(("parallel", "parallel", "arbitrary")),
    )(a, g)


def _dhx(pieces, ddt, w_inT, w_dtT, side=None):
    _, R, _ = pieces[0].shape
    tm = R // 4
    kb = 1024
    starts, nblk = [], []
    for p in pieces:
        starts.append(sum(nblk))
        nblk.append(p.shape[2] // kb)
    nk = sum(nblk)
    assert nk * kb == OFF_DT and R % 128 == 0
    npc = len(pieces)
    grid = (2, R // tm, nk)
    n_si, n_so = (len(side.inputs), len(side.out_shapes)) if side else (0, 0)

    def body(*refs):
        a_refs, dt_ref, w_ref, wdt_ref = refs[:npc], refs[npc], refs[npc + 1], refs[npc + 2]
        o_ref, acc = refs[npc + 3 + n_si], refs[npc + 4 + n_si + n_so]
        side_refs = (refs[npc + 3:npc + 3 + n_si], refs[npc + 4 + n_si:npc + 4 + n_si + n_so], refs[npc + 5 + n_si + n_so:])
        if side:
            side.start(grid, *side_refs)
        k = pl.program_id(2)

        @pl.when(k == 0)
        def _():
            acc[...] = _dot(dt_ref[0], wdt_ref[...])

        for p in range(npc):
            @pl.when((k >= starts[p]) & (k < starts[p] + nblk[p]))
            def _(p=p):
                acc[...] += _dot(a_refs[p][0], w_ref[...])

        @pl.when(k == nk - 1)
        def _():
            o_ref[0] = acc[...].astype(BF)

        if side:
            side.wait(grid, *side_refs)

    in_specs = [pl.BlockSpec((1, tm, kb), functools.partial(
        lambda e, t, k, s, nb: (e, t, jnp.clip(k - s, 0, nb - 1)), s=starts[p], nb=nblk[p])) for p in range(npc)]
    in_specs += [pl.BlockSpec((1, tm, 128), lambda e, t, k: (e, t, 0)),
                 pl.BlockSpec((kb, D), lambda e, t, k: (k, 0)),
                 pl.BlockSpec((128, D), lambda e, t, k: (0, 0))]
    out = pl.pallas_call(
        body, name="d_hx", grid=grid, in_specs=in_specs + [ANY] * n_si,
        out_specs=(pl.BlockSpec((1, tm, D), lambda e, t, k: (e, t, 0)),) + (ANY,) * n_so,
        out_shape=(jax.ShapeDtypeStruct((2, R, D), BF),) + tuple(side.out_shapes if side else ()),
        scratch_shapes=[pltpu.VMEM((tm, D), F32)] + (side.scratch() if side else []),
        compiler_params=_params(("arbitrary",) * 3 if side else ("parallel", "parallel", "arbitrary")),
    )(*pieces, ddt, w_inT, w_dtT, *(side.inputs if side else ()))
    return out if side else out[0]


def _adaln_fwd(c16, w_adaT_bf, b_ada):
    def body(c_ref, w_ref, b_ref, o_ref):
        cc = c_ref[...]
        o_ref[...] = _dot_nt(cc * _sigmoid(cc), w_ref[...]) + b_ref[...]

    return pl.pallas_call(body, name="adaln_fwd", out_shape=jax.ShapeDtypeStruct((16, 3 * D), F32),
                          compiler_params=_params())(c16, w_adaT_bf, b_ada)


def _adaln_bwd(acc_n, acc_f, mod16, c16, norm_pre, w_adaT_bf):
    def body(an_ref, af_ref, mod_ref, c_ref, np_ref, wt_ref, dw_ref, db_ref, sm_ref, dmod):
        npre = np_ref[...]
        dmod[...] = jnp.zeros_like(dmod)
        dnp = jnp.zeros((1, D), F32)
        dshift_c = jnp.zeros((1, D), F32)
        dgpre_c = jnp.zeros((1, D), F32)
        scale_c = mod_ref[2:3, D:2 * D]
        for e in range(2):
            dg_x, ds_x = an_ref[e, 0, 0:1, :], an_ref[e, 0, 1:2, :]
            dg_c, ds_c = an_ref[e, 1, 0:1, :], an_ref[e, 1, 1:2, :]
            dmod[e:e + 1, 0:D] = ds_x
            dmod[e:e + 1, D:2 * D] = dg_x * npre
            dmod[e:e + 1, 2 * D:3 * D] = af_ref[e, 0:1, :]
            dnp = dnp + dg_x * (1.0 + mod_ref[e:e + 1, D:2 * D]) + dg_c * (1.0 + scale_c)
            dshift_c = dshift_c + ds_c
            dgpre_c = dgpre_c + dg_c
        dmod[2:3, 0:D] = dshift_c
        dmod[2:3, D:2 * D] = dgpre_c * npre
        dm = dmod[...]
        cc = c_ref[...]
        sg = _sigmoid(cc)
        dw_ref[...] = _dot_tn(dm, cc * sg)
        db_ref[...] = jnp.zeros_like(db_ref)
        db_ref[0:1, :] = jnp.sum(dm, axis=0, keepdims=True)
        dsilu = sg * (1.0 + cc * (1.0 - sg))
        dcs = _dot(dm, wt_ref[...]) * dsilu
        sm_ref[...] = jnp.zeros_like(sm_ref)
        sm_ref[0:1, :] = dnp
        sm_ref[1:2, :] = dcs[2:3, :]

    return pl.pallas_call(
        body, name="adaln_bwd",
        out_shape=(jax.ShapeDtypeStruct((3 * D, D), F32), jax.ShapeDtypeStruct((16, 3 * D), F32),
                   jax.ShapeDtypeStruct((8, D), F32)),
        scratch_shapes=[pltpu.VMEM((16, 3 * D), F32)],
        compiler_params=_params())(acc_n, acc_f, mod16, c16, norm_pre, w_adaT_bf)


def _row_specs(L):
    nx = L // ROW_TILE
    return (pl.BlockSpec((1, ROW_TILE, D), lambda e, t: (e, jnp.minimum(t, nx - 1), 0)),
            pl.BlockSpec((1, ROW_TILE, D), lambda e, t: (e, jnp.maximum(t - nx, 0), 0)))


def _norm_mod_fwd(x, ctx, tab):
    L = x.shape[1]
    R = L + ctx.shape[1]
    nx = L // ROW_TILE

    def body(x_ref, c_ref, t_ref, o_ref):
        t = t_ref[0, 0]
        gain, shift = t[0:1], t[1:2]

        def run(src):
            for r0 in range(0, ROW_TILE, 32):
                x = src[0, pl.ds(r0, 32), :]
                r = lax.rsqrt(jnp.mean(x * x, axis=-1, keepdims=True) + EPS)
                o_ref[0, pl.ds(r0, 32), :] = (x * r * gain + shift).astype(BF)

        @pl.when(pl.program_id(1) < nx)
        def _():
            run(x_ref)

        @pl.when(pl.program_id(1) >= nx)
        def _():
            run(c_ref)

    return pl.pallas_call(
        body, name="norm_mod_fwd", grid=(2, R // ROW_TILE),
        in_specs=[*_row_specs(L), pl.BlockSpec((1, 1, 8, D), lambda e, t: (e, t // nx, 0, 0))],
        out_specs=pl.BlockSpec((1, ROW_TILE, D), lambda e, t: (e, t, 0)),
        out_shape=jax.ShapeDtypeStruct((2, R, D), BF),
        compiler_params=_params(("parallel", "parallel")),
    )(x, ctx, tab)


def _norm_mod_bwd(dh, x, ctx, tab, dxo):
    L = x.shape[1]
    R = L + ctx.shape[1]
    nx = L // ROW_TILE

    def body(dh_ref, x_ref, c_ref, t_ref, dxo_ref, gx_ref, acc_ref):
        t = pl.program_id(1)
        x = jnp.where(t < nx, x_ref[0], c_ref[0])
        r = lax.rsqrt(jnp.mean(x * x, axis=-1, keepdims=True) + EPS)
        xn = x * r
        dh = dh_ref[0].astype(F32)

        @pl.when((t == 0) | (t == nx))
        def _():
            acc_ref[...] = jnp.zeros_like(acc_ref)

        acc_ref[0, 0, 0:1, :] += jnp.sum(dh * xn, axis=0, keepdims=True)
        acc_ref[0, 0, 1:2, :] += jnp.sum(dh, axis=0, keepdims=True)

        @pl.when(t < nx)
        def _():
            dxn = dh * t_ref[0, 0][0:1]
            dx = r * (dxn - xn * jnp.mean(dxn * xn, axis=-1, keepdims=True))
            gx_ref[0] = dxo_ref[0].astype(F32) + dx

    xspec, cspec = _row_specs(L)
    return pl.pallas_call(
        body, name="norm_mod_bwd", grid=(2, R // ROW_TILE),
        in_specs=[pl.BlockSpec((1, ROW_TILE, D), lambda e, t: (e, t, 0)), xspec, cspec,
                  pl.BlockSpec((1, 1, 8, D), lambda e, t: (e, t // nx, 0, 0)), xspec],
        out_specs=(xspec, pl.BlockSpec((1, 1, 8, D), lambda e, t: (e, t // nx, 0, 0))),
        out_shape=(jax.ShapeDtypeStruct((2, L, D), F32), jax.ShapeDtypeStruct((2, 2, 8, D), F32)),
        compiler_params=_params(("parallel", "arbitrary")),
    )(dh, x, ctx, tab, dxo)


POOL_TILE = 256


def _pool_tables(L):
    rows = L // GRID_W
    mats = np.zeros((4, POOL_TILE, POOL_TILE), np.float32)
    for gi, k in enumerate(POOL_WINDOWS):
        lo, hi = k // 2, k - 1 - k // 2
        m = np.zeros((GRID_W, GRID_W), np.float32)
        for t in range(GRID_W):
            m[t, max(t - lo, 0):min(t + hi, GRID_W - 1) + 1] = 1.0
        for b in range(POOL_TILE // GRID_W):
            mats[gi, b * GRID_W:(b + 1) * GRID_W, b * GRID_W:(b + 1) * GRID_W] = m
    matsT = np.ascontiguousarray(np.transpose(mats, (0, 2, 1)))
    return (jnp.asarray(mats, BF), jnp.asarray(matsT, BF))


def _pool_cols(get_tile, mat, cs_ref, L, n):
    def step(i, carry):
        off = pl.multiple_of(i * POOL_TILE, POOL_TILE)
        t = get_tile(off)
        cs_ref[pl.ds(GRID_W + off, POOL_TILE), :] = (jnp.dot(mat, t.astype(BF), preferred_element_type=F32) if n == 1
                                                     else _dot_sr(mat, t.astype(F32), n))
        return carry

    lax.fori_loop(0, L // POOL_TILE, step, 0)
    cs_ref[pl.ds(0, GRID_W), :] = jnp.zeros((GRID_W, PGW), F32)

    def prefix(r, carry):
        o = pl.multiple_of(r * GRID_W, GRID_W)
        cs_ref[pl.ds(o + GRID_W, GRID_W), :] = cs_ref[pl.ds(o + GRID_W, GRID_W), :] + cs_ref[pl.ds(o, GRID_W), :]
        return carry

    lax.fori_loop(0, L // GRID_W, prefix, 0)


def _pool_rows(cs_ref, off, below, above, L):
    rows = L // GRID_W
    r0 = off // GRID_W
    parts = []
    for i in range(POOL_TILE // GRID_W):
        hi = pl.multiple_of(jnp.minimum(r0 + i + above + 1, rows) * GRID_W, GRID_W)
        lo = pl.multiple_of(jnp.maximum(r0 + i - below, 0) * GRID_W, GRID_W)
        parts.append(cs_ref[pl.ds(hi, GRID_W), :] - cs_ref[pl.ds(lo, GRID_W), :])
    return jnp.concatenate(parts, axis=0)


PGS = 2


def _inv_count(off, half, L):
    t = off + lax.broadcasted_iota(jnp.int32, (POOL_TILE, 1), 0)
    r, c = jnp.right_shift(t, 6), t & (GRID_W - 1)
    cr = jnp.minimum(r + half - 1, L // GRID_W - 1) - jnp.maximum(r - half, 0) + 1
    cc = jnp.minimum(c + half - 1, GRID_W - 1) - jnp.maximum(c - half, 0) + 1
    return 1.0 / (cr * cc).astype(F32)


def _pool_fwd(proj3, pool_w_bf, pool_scale, tables, L):
    mats, _ = tables
    nt = L // POOL_TILE

    def body(v_ref, z_ref, pw_ref, ps_ref, m_ref, o_ref, cs_ref):
        for j in range(PGS):
            cols = slice(j * PGW, (j + 1) * PGW)
            _pool_cols(lambda off: v_ref[0, pl.ds(off, POOL_TILE), cols], m_ref[j], cs_ref, L, 1)
            half = lax.shift_left(1, PGS * pl.program_id(1) + j)

            def step(i, carry, j=j, cols=cols, half=half):
                off = pl.multiple_of(i * POOL_TILE, POOL_TILE)
                rows = pl.ds(off, POOL_TILE)
                v = v_ref[0, rows, cols].astype(F32)
                diff = _pool_rows(cs_ref, off, half, half - 1, L) * _inv_count(off, half, L) - v
                yp = _dot(diff, pw_ref[j])
                z = z_ref[0, rows, cols].astype(F32)
                o_ref[0, rows, cols] = (yp * ps_ref[:, cols] * (z * _sigmoid(z))).astype(BF)
                return carry

            lax.fori_loop(0, nt, step, 0)

    wide = PGS * PGW
    return pl.pallas_call(
        body, name="pool_fwd", grid=(2, 4 // PGS),
        in_specs=[pl.BlockSpec((1, L, wide), lambda e, g: (e, 0, g)),
                  pl.BlockSpec((1, L, wide), lambda e, g: (e, 0, 4 // PGS + g)),
                  pl.BlockSpec((PGS, PGW, PGW), lambda e, g: (g, 0, 0)),
                  pl.BlockSpec((1, wide), lambda e, g: (0, g)),
                  pl.BlockSpec((PGS, POOL_TILE, POOL_TILE), lambda e, g: (g, 0, 0))],
        out_specs=pl.BlockSpec((1, L, wide), lambda e, g: (e, 0, g)),
        out_shape=jax.ShapeDtypeStruct((2, L, D), BF),
        scratch_shapes=[pltpu.VMEM((L + GRID_W, PGW), F32)],
        compiler_params=_params(("parallel", "parallel")),
    )(proj3, proj3, pool_w_bf, pool_scale, mats)


def _pool_bwd(proj3, d_ypool, pool_w_bf, pool_wT_bf, pool_scale, tables, L):
    mats, matsT = tables
    nt = L // POOL_TILE
    R = proj3.shape[1]

    def body(v_ref, z_ref, dy_ref, pw_ref, pwt_ref, ps_ref, m_ref, mt_ref,
             dv_ref, dz_ref, dpw_ref, acc_ref, cs_ref, dd_ref):
        e = pl.program_id(1)

        @pl.when(e == 0)
        def _():
            dpw_ref[...] = jnp.zeros_like(dpw_ref)
            acc_ref[...] = jnp.zeros_like(acc_ref)

        for j in range(PGS):
            cols = slice(j * PGW, (j + 1) * PGW)
            _pool_cols(lambda off: v_ref[0, pl.ds(off, POOL_TILE), cols], m_ref[j], cs_ref, L, 1)
            half = lax.shift_left(1, PGS * pl.program_id(0) + j)
            ps = ps_ref[:, cols]

            def step(i, carry, j=j, cols=cols, half=half, ps=ps):
                off = pl.multiple_of(i * POOL_TILE, POOL_TILE)
                rows = pl.ds(off, POOL_TILE)
                v = v_ref[0, rows, cols].astype(F32)
                diff = _pool_rows(cs_ref, off, half, half - 1, L) * _inv_count(off, half, L) - v
                yp = _dot(diff, pw_ref[j])
                z = z_ref[0, rows, cols].astype(F32)
                sg = _sigmoid(z)
                sz = z * sg
                dy = dy_ref[0, rows, cols].astype(F32)
                dz_ref[0, rows, cols] = (dy * yp * ps * (sg * (1.0 + z * (1.0 - sg)))).astype(BF)
                dys = dy * sz
                acc_ref[j, 0:1, :] += jnp.sum(dys * yp, axis=0, keepdims=True)
                dyp = dys * ps
                dpw_ref[j] += _dot_tn(diff, dyp)
                dd_ref[rows, :] = _dot(dyp, pwt_ref[j])
                return carry

            lax.fori_loop(0, nt, step, 0)
            _pool_cols(lambda off, half=half: dd_ref[pl.ds(off, POOL_TILE), :] * _inv_count(off, half, L),
                       mt_ref[j], cs_ref, L, 1)

            def step2(i, carry, cols=cols, half=half):
                off = pl.multiple_of(i * POOL_TILE, POOL_TILE)
                rows = pl.ds(off, POOL_TILE)
                dv_ref[0, rows, cols] = (_pool_rows(cs_ref, off, half - 1, half, L) - dd_ref[rows, :]).astype(BF)
                return carry

            lax.fori_loop(0, nt, step2, 0)
        dv_ref[0, pl.ds(L, R - L), :] = jnp.zeros((R - L, PGS * PGW), BF)
        dz_ref[0, pl.ds(L, R - L), :] = jnp.zeros((R - L, PGS * PGW), BF)

    wide = PGS * PGW
    return pl.pallas_call(
        body, name="pool_bwd", grid=(4 // PGS, 2),
        in_specs=[pl.BlockSpec((1, L, wide), lambda g, e: (e, 0, g)),
                  pl.BlockSpec((1, L, wide), lambda g, e: (e, 0, 4 // PGS + g)),
                  pl.BlockSpec((1, L, wide), lambda g, e: (e, 0, g)),
                  pl.BlockSpec((PGS, PGW, PGW), lambda g, e: (g, 0, 0)),
                  pl.BlockSpec((PGS, PGW, PGW), lambda g, e: (g, 0, 0)),
                  pl.BlockSpec((1, wide), lambda g, e: (0, g)),
                  pl.BlockSpec((PGS, POOL_TILE, POOL_TILE), lambda g, e: (g, 0, 0)),
                  pl.BlockSpec((PGS, POOL_TILE, POOL_TILE), lambda g, e: (g, 0, 0))],
        out_specs=(pl.BlockSpec((1, R, wide), lambda g, e: (e, 0, g)),
                   pl.BlockSpec((1, R, wide), lambda g, e: (e, 0, g)),
                   pl.BlockSpec((PGS, PGW, PGW), lambda g, e: (g, 0, 0)),
                   pl.BlockSpec((PGS, 8, PGW), lambda g, e: (g, 0, 0))),
        out_shape=(jax.ShapeDtypeStruct((2, R, D), BF), jax.ShapeDtypeStruct((2, R, D), BF),
                   jax.ShapeDtypeStruct((4, PGW, PGW), F32), jax.ShapeDtypeStruct((4, 8, PGW), F32)),
        scratch_shapes=[pltpu.VMEM((L + GRID_W, PGW), F32), pltpu.VMEM((L, PGW), F32)],
        compiler_params=_params(("parallel", "arbitrary")),
    )(proj3, proj3, d_ypool, pool_w_bf, pool_wT_bf, pool_scale, mats, matsT)


CONV_BLOCK = 128


CONV_CHUNK = 64
CONV_HALO = 8


def _halo_buf_init(buf, val, R):
    z = jnp.zeros((CONV_HALO, CONV_BLOCK), F32)
    buf[pl.ds(0, CONV_HALO), :] = z
    buf[pl.ds(CONV_HALO + R, CONV_HALO), :] = z
    if val is not None:
        buf[pl.ds(CONV_HALO, R), :] = val


def _chunk_taps(buf, start, offs, L):
    n = CONV_CHUNK + 2 * CONV_HALO
    ext = buf[pl.ds(start, n), :]
    out = []
    for off in offs:
        if off == 0:
            out.append(ext[CONV_HALO:CONV_HALO + CONV_CHUNK])
            continue
        r = pltpu.roll(ext, (-off) % n, 0)[CONV_HALO:CONV_HALO + CONV_CHUNK]
        lo, hi = (start, start + CONV_CHUNK - 1 + off) if off > 0 else (start + off, start + CONV_CHUNK - 1)
        if lo < L <= hi:
            t = start + lax.broadcasted_iota(jnp.int32, (CONV_CHUNK, 1), 0)
            r = jnp.where((t < L) == (t + off < L), r, 0.0)
        out.append(r)
    return out


def _fold8(x):
    return sum(x[i * 8:(i + 1) * 8] for i in range(CONV_CHUNK // 8))


def _conv_fwd(proj3, conv_w, conv_b, L):
    _, R, _ = proj3.shape
    cb0 = OFF_XBC // CONV_BLOCK

    def body(u_ref, w_ref, b_ref, o_ref, ubuf):
        _halo_buf_init(ubuf, u_ref[0].astype(F32), R)
        w = w_ref[...]
        b = b_ref[...]
        for start in range(0, R, CONV_CHUNK):
            taps = _chunk_taps(ubuf, start, (-2, -1, 0, 1), L)
            pre = b + sum(taps[k] * w[k:k + 1, :] for k in range(4))
            o_ref[0, pl.ds(start, CONV_CHUNK), :] = (pre * _sigmoid(pre)).astype(BF)

    return pl.pallas_call(
        body, name="conv_fwd", grid=(2, CONV_DIM // CONV_BLOCK),
        in_specs=[pl.BlockSpec((1, R, CONV_BLOCK), lambda e, j: (e, 0, cb0 + j)),
                  pl.BlockSpec((4, CONV_BLOCK), lambda e, j: (0, j)),
                  pl.BlockSpec((1, CONV_BLOCK), lambda e, j: (0, j))],
        out_specs=pl.BlockSpec((1, R, CONV_BLOCK), lambda e, j: (e, 0, j)),
        out_shape=jax.ShapeDtypeStruct((2, R, CONV_DIM), BF),
        scratch_shapes=[pltpu.VMEM((R + 2 * CONV_HALO, CONV_BLOCK), F32)],
        compiler_params=_params(("parallel", "parallel")),
    )(proj3, conv_w, conv_b)


def _conv_bwd(proj3, addends, scales, col0, ncols, in_maps, conv_w, conv_b, L, name):
    _, R, _ = proj3.shape
    cb0 = (OFF_XBC + col0) // CONV_BLOCK
    wb0 = col0 // CONV_BLOCK
    na = len(addends)
    scaled = [i for i in range(na) if scales[i] is not None]

    def body(*refs):
        u_ref, w_ref, b_ref = refs[0], refs[1], refs[2]
        a_refs = refs[3:3 + na]
        s_refs = dict(zip(scaled, refs[3 + na:3 + na + len(scaled)]))
        o_ref, acc_ref, ubuf, dbuf = refs[3 + na + len(scaled):]
        _halo_buf_init(ubuf, u_ref[0].astype(F32), R)
        _halo_buf_init(dbuf, None, R)
        w = w_ref[...]
        b = b_ref[...]
        scl = {i: s_refs[i][...] for i in scaled}
        sums = [jnp.zeros((8, CONV_BLOCK), F32) for _ in range(5)]
        for start in range(0, R, CONV_CHUNK):
            rows = pl.ds(start, CONV_CHUNK)
            taps = _chunk_taps(ubuf, start, (-2, -1, 0, 1), L)
            pre = b + sum(taps[k] * w[k:k + 1, :] for k in range(4))
            sg = _sigmoid(pre)
            dxbc = None
            for i, a in enumerate(a_refs):
                t = a[0, rows, :].astype(F32)
                t = t * scl[i] if i in scl else t
                dxbc = t if dxbc is None else dxbc + t
            dpre = dxbc * (sg * (1.0 + pre * (1.0 - sg)))
            dbuf[pl.ds(start + CONV_HALO, CONV_CHUNK), :] = dpre
            for k in range(4):
                sums[k] = sums[k] + _fold8(dpre * taps[k])
            sums[4] = sums[4] + _fold8(dpre)
        acc_ref[...] = jnp.zeros_like(acc_ref)
        for k in range(5):
            acc_ref[0, k:k + 1, :] = jnp.sum(sums[k], axis=0, keepdims=True)
        for start in range(0, R, CONV_CHUNK):
            d = _chunk_taps(dbuf, start, (2, 1, 0, -1), L)
            o_ref[0, pl.ds(start, CONV_CHUNK), :] = sum(d[k] * w[k:k + 1, :] for k in range(4)).astype(BF)

    in_specs = [pl.BlockSpec((1, R, CONV_BLOCK), lambda e, j: (e, 0, cb0 + j)),
                pl.BlockSpec((4, CONV_BLOCK), lambda e, j: (0, wb0 + j)),
                pl.BlockSpec((1, CONV_BLOCK), lambda e, j: (0, wb0 + j))]
    for m in in_maps:
        in_specs.append(pl.BlockSpec((1, R, CONV_BLOCK), functools.partial(lambda e, j, m: (e, 0, m(j)), m=m)))
    for i in scaled:
        in_specs.append(pl.BlockSpec((1, CONV_BLOCK), functools.partial(lambda e, j, m: (0, m(j)), m=in_maps[i])))
    return pl.pallas_call(
        body, name=name, grid=(2, ncols // CONV_BLOCK),
        in_specs=in_specs,
        out_specs=(pl.BlockSpec((1, R, CONV_BLOCK), lambda e, j: (e, 0, j)),
                   pl.BlockSpec((1, 8, CONV_BLOCK), lambda e, j: (e, 0, j))),
        out_shape=(jax.ShapeDtypeStruct((2, R, ncols), BF), jax.ShapeDtypeStruct((2, 8, ncols), F32)),
        scratch_shapes=[pltpu.VMEM((R + 2 * CONV_HALO, CONV_BLOCK), F32)] * 2,
        compiler_params=_params(("parallel", "parallel")),
    )(proj3, conv_w, conv_b, *addends, *[scales[i] for i in scaled])


def _softplus(x):
    e = jnp.exp(-jnp.abs(x))
    u = 1.0 + e
    return jnp.maximum(x, 0.0) + jnp.where(u == 1.0, e, e * jnp.log(u) / (u - 1.0))


def _to_local_mat(g, transpose=False):
    r = lax.broadcasted_iota(jnp.int32, (128, 128), 1 if transpose else 0)
    c = lax.broadcasted_iota(jnp.int32, (128, 128), 0 if transpose else 1)
    return ((c < 2 * HPG) & (r == jnp.right_shift(c, 3) * (NG * HPG) + g * HPG + (c & (HPG - 1)))).astype(BF)


def _dt_fwd(dt_raw, bias128):
    _, R, _ = dt_raw.shape

    def body(x_ref, b_ref, o_ref):
        dt = _softplus(x_ref[0] + b_ref[...])
        for g in range(NG):
            o_ref[0, g] = _dot_sl(dt, _to_local_mat(g))

    tr = R // 4
    return pl.pallas_call(
        body, name="dt_fwd", grid=(2, 4),
        in_specs=[pl.BlockSpec((1, tr, 128), lambda e, t: (e, t, 0)), pl.BlockSpec((1, 128), lambda e, t: (0, 0))],
        out_specs=pl.BlockSpec((1, NG, tr, 128), lambda e, t: (e, 0, t, 0)),
        out_shape=jax.ShapeDtypeStruct((2, NG, R, 128), F32),
        compiler_params=_params(("parallel", "parallel")),
    )(dt_raw, bias128)


def _dt_bwd(dt_raw, bias128, ddt_f, ddt_b):
    _, R, _ = dt_raw.shape

    def body(x_ref, b_ref, f_ref, g_ref, o_ref, acc_ref):
        ddt = sum(_dot_sl(f_ref[0, g] + g_ref[0, g], _to_local_mat(g, transpose=True)) for g in range(NG))
        d = ddt * _sigmoid(x_ref[0] + b_ref[...])
        o_ref[0] = d.astype(BF)

        @pl.when(pl.program_id(1) == 0)
        def _():
            acc_ref[...] = jnp.zeros_like(acc_ref)

        acc_ref[0, 0:1, :] += jnp.sum(d, axis=0, keepdims=True)

    tr = R // 4
    blk = pl.BlockSpec((1, tr, 128), lambda e, t: (e, t, 0))
    loc = pl.BlockSpec((1, NG, tr, 128), lambda e, t: (e, 0, t, 0))
    return pl.pallas_call(
        body, name="dt_bwd", grid=(2, 4),
        in_specs=[blk, pl.BlockSpec((1, 128), lambda e, t: (0, 0)), loc, loc],
        out_specs=(blk, pl.BlockSpec((1, 8, 128), lambda e, t: (e, 0, 0))),
        out_shape=(jax.ShapeDtypeStruct(dt_raw.shape, BF), jax.ShapeDtypeStruct((2, 8, 128), F32)),
        compiler_params=_params(("parallel", "arbitrary")),
    )(dt_raw, bias128, ddt_f, ddt_b)


GPS = 4


def _tri(d):
    i = lax.broadcasted_iota(jnp.int32, (Q, Q), 0)
    j = lax.broadcasted_iota(jnp.int32, (Q, Q), 1)
    return (i >= j) if d == 0 else (i <= j)


def _expand_mat(d):
    r = lax.broadcasted_iota(jnp.int32, (128, GWID), 0)
    c = lax.broadcasted_iota(jnp.int32, (128, GWID), 1)
    return (r == d * HPG + jnp.right_shift(c, 6)).astype(BF)


def _reduce_mat(d):
    r = lax.broadcasted_iota(jnp.int32, (GWID, 128), 0)
    c = lax.broadcasted_iota(jnp.int32, (GWID, 128), 1)
    return (c == d * HPG + jnp.right_shift(r, 6)).astype(BF)


def _ssd_chunk(d, dt, A, xs, B, C):
    mask = _tri(d)
    T = mask.astype(BF)
    Tt = _tri(1 - d).astype(BF)
    a = dt * A
    acs = _dot_sr(T, a)
    E = _expand_mat(d)
    dt_e = _dot_sl(dt, E, 2)
    acs_e = _dot_sl(acs, E, 2)
    alast_e = acs_e[Q - 1:Q, :] if d == 0 else acs_e[0:1, :]
    return dict(mask=mask, T=T, Tt=Tt, acs=acs, acsT=acs.T, dt_e=dt_e, acs_e=acs_e, lam=jnp.exp(acs_e),
                w=jnp.exp(alast_e - acs_e), decay=jnp.exp(alast_e), xt=xs * dt_e, CB=_dot_nt(C, B))


def _head_decay(q, d, hh):
    col = q["acs"][:, d * HPG + hh:d * HPG + hh + 1]
    row = q["acsT"][d * HPG + hh:d * HPG + hh + 1, :]
    return jnp.exp(jnp.where(q["mask"], col - row, -jnp.inf))


def _chunk_maps(NX, NS):
    cf = lambda s: lax.rem(s + NX, NS)
    cb = lambda s: NS - 1 - s
    return cf, cb


def _ssd_fwd(xbc, dt_loc, a_loc, L):
    _, R, _ = xbc.shape
    NX, NS = L // Q, R // Q
    cf, cb = _chunk_maps(NX, NS)

    def body(xs_f, b_f, c_f, dt_f, xs_b, b_b, c_b, dt_b, a_ref, y_f, hs_f, y_b, hs_b, hT):
        @pl.when(pl.program_id(2) == 0)
        def _():
            hT[...] = jnp.zeros_like(hT)

        lane = lax.broadcasted_iota(jnp.int32, (Q, 128), 1)
        for d, (xs_ref, b_ref, c_ref, dt_ref, y_ref, hs_ref) in enumerate(
                ((xs_f, b_f, c_f, dt_f, y_f, hs_f), (xs_b, b_b, c_b, dt_b, y_b, hs_b))):
            for gi in range(GPS):
                cols = slice(gi * GWID, (gi + 1) * GWID)
                xs = xs_ref[0, :, cols].astype(F32)
                B, C = b_ref[0, :, gi * NST:(gi + 1) * NST], c_ref[0, :, gi * NST:(gi + 1) * NST]
                q = _ssd_chunk(d, dt_ref[0, gi], a_ref[gi, 0:1, :], xs, B, C)
                h = hT[d, :, cols]
                hb = h.astype(BF)
                hs_ref[0, 0, :, cols] = hb
                parts = []
                for pr in range(HPG // 2):
                    xp = q["xt"][:, pr * 128:(pr + 1) * 128]
                    xst = jnp.concatenate([jnp.where(lane < HEAD, xp, 0.0), jnp.where(lane < HEAD, 0.0, xp)], axis=0)
                    mst = jnp.concatenate([(q["CB"] * _head_decay(q, d, 2 * pr)).astype(BF),
                                           (q["CB"] * _head_decay(q, d, 2 * pr + 1)).astype(BF)], axis=1)
                    parts.append(_dot(mst, xst))
                y_ref[0, :, cols] = jnp.concatenate(parts, axis=1) + _dot(C, hb) * q["lam"]
                hT[d, :, cols] = q["decay"] * h + _dot_tn(B, q["xt"] * q["w"])

    def spec(shape, imap):
        return pl.BlockSpec(shape, imap)

    bc0 = DIN // (GPS * NST)

    def ins(c):
        return [spec((1, Q, GPS * GWID), lambda e, g, s: (e, c(s), g)),
                spec((1, Q, GPS * NST), lambda e, g, s: (e, c(s), bc0 + g)),
                spec((1, Q, GPS * NST), lambda e, g, s: (e, c(s), bc0 + NG // GPS + g)),
                spec((1, GPS, Q, 128), lambda e, g, s: (e, g, c(s), 0))]

    def outs(c):
        return [spec((1, Q, GPS * GWID), lambda e, g, s: (e, c(s), g)),
                spec((1, 1, NST, GPS * GWID), lambda e, g, s: (e, c(s), 0, g))]

    yshape = jax.ShapeDtypeStruct((2, R, DIN), F32)
    hshape = jax.ShapeDtypeStruct((2, NS, NST, DIN), BF)
    return pl.pallas_call(
        body, name="ssd_fwd", grid=(2, NG // GPS, NS),
        in_specs=ins(cf) + ins(cb) + [spec((GPS, 8, 128), lambda e, g, s: (g, 0, 0))],
        out_specs=tuple(outs(cf) + outs(cb)),
        out_shape=(yshape, hshape, yshape, hshape),
        scratch_shapes=[pltpu.VMEM((2, NST, GPS * GWID), F32)],
        compiler_params=_params(("parallel", "parallel", "arbitrary")),
    )(xbc, xbc, xbc, dt_loc, xbc, xbc, xbc, dt_loc, a_loc)


def _ssd_bwd(xbc, dt_loc, a_loc, hs_f, hs_b, y_f, y_b, dy, L):
    _, R, _ = xbc.shape
    NX, NS = L // Q, R // Q
    cf0, cb0 = _chunk_maps(NX, NS)
    cf = lambda sp: cf0(NS - 1 - sp)
    cb = lambda sp: cb0(NS - 1 - sp)

    def body(xs_f, b_f, c_f, dt_f, hs_f_, dy_f, y_f_, xs_b, b_b, c_b, dt_b, hs_b_, dy_b, y_b_, a_ref,
             dxs_f, dbc_f, ddt_f, dxs_b, dbc_b, ddt_b, da_ref, dhT):
        @pl.when(pl.program_id(2) == 0)
        def _():
            dhT[...] = jnp.zeros_like(dhT)
            da_ref[...] = jnp.zeros_like(da_ref)

        lane = lax.broadcasted_iota(jnp.int32, (Q, 128), 1)
        row = lax.broadcasted_iota(jnp.int32, (Q, 128), 0)

        def one_chain(d, gi, xs_ref, b_ref, c_ref, dt_ref, hs_ref, dy_ref, y_ref, dxs_ref, dbc_ref, ddt_ref):
            cols = slice(gi * GWID, (gi + 1) * GWID)
            A = a_ref[gi, 0:1, :]
            xs, dt = xs_ref[0, :, cols].astype(F32), dt_ref[0, gi]
            B, C = b_ref[0, :, gi * NST:(gi + 1) * NST], c_ref[0, :, gi * NST:(gi + 1) * NST]
            q = _ssd_chunk(d, dt, A, xs, B, C)
            xt, lam, w, decay = q["xt"], q["lam"], q["w"], q["decay"]
            H = hs_ref[0, 0, :, cols]
            dyv = dy_ref[0, :, cols].astype(F32)
            dh = dhT[d, :, cols]
            dZ = dyv * lam
            dC = _dot_nt(dZ, H)
            dH = _dot_tn(C, dZ)
            U = _dot(B, dh)
            xw = xt * w
            dxt = U * w
            dalast_e = (jnp.sum(U * xw, axis=0, keepdims=True)
                        + decay * jnp.sum(dh * H.astype(F32), axis=0, keepdims=True))
            dB = _dot_nt(xw, dh)
            dCB = jnp.zeros((Q, Q), F32)
            dxt_parts = []
            for pr in range(HPG // 2):
                xp = xt[:, pr * 128:(pr + 1) * 128]
                dyp = dyv[:, pr * 128:(pr + 1) * 128]
                L0, L1 = _head_decay(q, d, 2 * pr), _head_decay(q, d, 2 * pr + 1)
                dyst = jnp.concatenate([jnp.where(lane < HEAD, dyp, 0.0), jnp.where(lane < HEAD, 0.0, dyp)], axis=0)
                mst = jnp.concatenate([(q["CB"] * L0).astype(BF), (q["CB"] * L1).astype(BF)], axis=0)
                dxt_parts.append(_dot_tn(mst, dyst))
                dmst = _dot_nt(dyst, xp)
                dCB = dCB + dmst[:Q] * L0 + dmst[Q:] * L1
            dxt_diag = jnp.concatenate(dxt_parts, axis=1)
            dC = dC + _dot(dCB, B)
            dB = dB + _dot_tn(dCB, C)
            Rm = _reduce_mat(d)
            dacs = _dot_sl(dyv * y_ref[0, :, cols] - xt.astype(BF).astype(F32) * dxt_diag - U * xw, Rm, 2)
            dxt = dxt + dxt_diag
            dal = _dot_sl(jnp.broadcast_to(dalast_e, (8, GWID)), Rm, 2)[0:1, :]
            dacs = dacs + jnp.where(row == (Q - 1 if d == 0 else 0), dal, 0.0)
            da = _dot_sr(q["Tt"], dacs, 2)
            ddt_ref[0, gi] = da * A + _dot_sl(dxt * xs, Rm, 2)
            da_ref[0, gi, 0:1, :] += jnp.sum(da * dt, axis=0, keepdims=True)
            dxs_ref[0, :, cols] = (dxt * q["dt_e"]).astype(BF)
            dbc_ref[0, :, gi * 2 * NST:(gi + 1) * 2 * NST] = jnp.concatenate([dB, dC], axis=1).astype(BF)
            dhT[d, :, cols] = decay * dh + dH

        for gi in range(GPS):
            one_chain(0, gi, xs_f, b_f, c_f, dt_f, hs_f_, dy_f, y_f_, dxs_f, dbc_f, ddt_f)
            one_chain(1, gi, xs_b, b_b, c_b, dt_b, hs_b_, dy_b, y_b_, dxs_b, dbc_b, ddt_b)

    def spec(shape, imap):
        return pl.BlockSpec(shape, imap)

    bc0 = DIN // (GPS * NST)

    def ins(c):
        return [spec((1, Q, GPS * GWID), lambda e, g, s: (e, c(s), g)),
                spec((1, Q, GPS * NST), lambda e, g, s: (e, c(s), bc0 + g)),
                spec((1, Q, GPS * NST), lambda e, g, s: (e, c(s), bc0 + NG // GPS + g)),
                spec((1, GPS, Q, 128), lambda e, g, s: (e, g, c(s), 0)),
                spec((1, 1, NST, GPS * GWID), lambda e, g, s: (e, c(s), 0, g)),
                spec((1, Q, GPS * GWID), lambda e, g, s: (e, c(s), g)),
                spec((1, Q, GPS * GWID), lambda e, g, s: (e, c(s), g))]

    def outs(c):
        return [spec((1, Q, GPS * GWID), lambda e, g, s: (e, c(s), g)),
                spec((1, Q, GPS * 2 * NST), lambda e, g, s: (e, c(s), g)),
                spec((1, GPS, Q, 128), lambda e, g, s: (e, g, c(s), 0))]

    s_xs = jax.ShapeDtypeStruct((2, R, DIN), BF)
    s_bc = jax.ShapeDtypeStruct((2, R, 2 * NG * NST), BF)
    s_dt = jax.ShapeDtypeStruct((2, NG, R, 128), F32)
    return pl.pallas_call(
        body, name="ssd_bwd", grid=(2, NG // GPS, NS),
        in_specs=ins(cf) + ins(cb) + [spec((GPS, 8, 128), lambda e, g, s: (g, 0, 0))],
        out_specs=tuple(outs(cf) + outs(cb) + [spec((1, GPS, 8, 128), lambda e, g, s: (e, g, 0, 0))]),
        out_shape=(s_xs, s_bc, s_dt, s_xs, s_bc, s_dt, jax.ShapeDtypeStruct((2, NG, 8, 128), F32)),
        scratch_shapes=[pltpu.VMEM((2, NST, GPS * GWID), F32)],
        compiler_params=_params(("parallel", "parallel", "arbitrary")),
    )(xbc, xbc, xbc, dt_loc, hs_f, dy, y_f, xbc, xbc, xbc, dt_loc, hs_b, dy, y_b, a_loc)


def _ssd_post_fwd(y_f, y_b, xbc, proj3, dskip_e, ssd_norm, L):
    def body(yf_ref, yb_ref, xs_ref, z_ref, ds_ref, w_ref, o_ref, y2_ref):
        y2 = yf_ref[0] + yb_ref[0] + ds_ref[...] * xs_ref[0].astype(F32)
        y2_ref[0] = y2.astype(BF)
        z = z_ref[0].astype(F32)
        u = y2 * (z * _sigmoid(z))
        parts = []
        for g in range(NG):
            ug = u[:, g * GWID:(g + 1) * GWID]
            parts.append(ug * lax.rsqrt(jnp.mean(ug * ug, axis=-1, keepdims=True) + EPS))
        o_ref[0] = (jnp.concatenate(parts, axis=1) * w_ref[...]).astype(BF)

    blk = lambda c: pl.BlockSpec((1, X_TILE, DIN), lambda e, t: (e, t, c))
    vec = pl.BlockSpec((1, DIN), lambda e, t: (0, 0))
    return pl.pallas_call(
        body, name="ssd_post_fwd", grid=(2, L // X_TILE),
        in_specs=[blk(0), blk(0), blk(0), blk(1), vec, vec],
        out_specs=(blk(0), blk(0)),
        out_shape=(jax.ShapeDtypeStruct((2, L, DIN), BF), jax.ShapeDtypeStruct((2, L, DIN), BF)),
        compiler_params=_params(("parallel", "parallel")),
    )(y_f, y_b, xbc, proj3, dskip_e, ssd_norm)


def _ssd_post_bwd(d_yn, y2b, xbc, proj3, ssd_norm, L):
    _, R, _ = xbc.shape
    nx = L // ROW_TILE

    def body(dyn_ref, y2_ref, xs_ref, z_ref, w_ref, dy_ref, dz_ref, acc_ref):
        t = pl.program_id(1)

        @pl.when(t == 0)
        def _():
            acc_ref[...] = jnp.zeros_like(acc_ref)

        @pl.when(t >= nx)
        def _():
            dy_ref[...] = jnp.zeros_like(dy_ref)
            dz_ref[...] = jnp.zeros_like(dz_ref)

        @pl.when(t < nx)
        def _():
            for g in range(NG):
                sl = slice(g * GWID, (g + 1) * GWID)
                xs = xs_ref[0, :, sl].astype(F32)
                y2 = y2_ref[0, :, sl].astype(F32)
                z = z_ref[0, :, sl].astype(F32)
                sg = _sigmoid(z)
                sz = z * sg
                ug = y2 * sz
                dyn = dyn_ref[0, :, sl].astype(F32)
                dg = dyn * w_ref[:, sl]
                rg = lax.rsqrt(jnp.mean(ug * ug, axis=-1, keepdims=True) + EPS)
                uh = ug * rg
                du = rg * (dg - uh * jnp.mean(dg * uh, axis=-1, keepdims=True))
                dy2 = du * sz
                dy_ref[0, :, sl] = dy2.astype(BF)
                dz_ref[0, :, sl] = (du * y2 * (sg * (1.0 + z * (1.0 - sg)))).astype(BF)
                acc_ref[0, 0:1, sl] += jnp.sum(dyn * uh, axis=0, keepdims=True)
                acc_ref[0, 1:2, sl] += jnp.sum(dy2 * xs, axis=0, keepdims=True)

    xmap = lambda c: (lambda e, t: (e, jnp.minimum(t, nx - 1), c))
    blk = lambda c: pl.BlockSpec((1, ROW_TILE, DIN), xmap(c))
    oblk = pl.BlockSpec((1, ROW_TILE, DIN), lambda e, t: (e, t, 0))
    vec = pl.BlockSpec((1, DIN), lambda e, t: (0, 0))
    return pl.pallas_call(
        body, name="ssd_post_bwd", grid=(2, R // ROW_TILE),
        in_specs=[blk(0), blk(0), blk(0), blk(1), vec],
        out_specs=(oblk, oblk, pl.BlockSpec((1, 8, DIN), lambda e, t: (e, 0, 0))),
        out_shape=(jax.ShapeDtypeStruct((2, R, DIN), BF), jax.ShapeDtypeStruct((2, R, DIN), BF),
                   jax.ShapeDtypeStruct((2, 8, DIN), F32)),
        compiler_params=_params(("parallel", "arbitrary")),
    )(d_yn, y2b, xbc, proj3, ssd_norm)


def _merge_fwd(proj3, P, S, b_merge, L):
    def body(gp_ref, p_ref, s_ref, b_ref, o_ref):
        gt = _sigmoid(gp_ref[0].astype(F32) + b_ref[...])
        o_ref[0] = (gt[:, :D] * p_ref[0].astype(F32) + gt[:, D:] * s_ref[0].astype(F32)).astype(BF)

    blk = pl.BlockSpec((1, X_TILE, D), lambda e, t: (e, t, 0))
    return pl.pallas_call(
        body, name="merge_fwd", grid=(2, L // X_TILE),
        in_specs=[pl.BlockSpec((1, X_TILE, 2 * D), lambda e, t: (e, t, OFF_GATE // (2 * D))), blk, blk,
                  pl.BlockSpec((1, 2 * D), lambda e, t: (0, 0))],
        out_specs=blk, out_shape=jax.ShapeDtypeStruct((2, L, D), BF),
        compiler_params=_params(("parallel", "parallel")),
    )(proj3, P, S, b_merge)


def _merge_bwd(d_merged, proj3, P, S, b_merge, L):
    _, R, _ = proj3.shape
    nx = L // ROW_TILE

    def body(dm_ref, gp_ref, p_ref, s_ref, b_ref, dp_ref, ds_ref, dg_ref, acc_ref):
        t = pl.program_id(1)

        @pl.when(t == 0)
        def _():
            acc_ref[...] = jnp.zeros_like(acc_ref)

        @pl.when(t >= nx)
        def _():
            dg_ref[...] = jnp.zeros_like(dg_ref)

        @pl.when(t < nx)
        def _():
            gt = _sigmoid(gp_ref[0].astype(F32) + b_ref[...])
            dm = dm_ref[0].astype(F32)
            g1, g2 = gt[:, :D], gt[:, D:]
            dp_ref[0] = (dm * g1).astype(BF)
            ds_ref[0] = (dm * g2).astype(BF)
            dgp = jnp.concatenate([dm * p_ref[0].astype(F32) * g1 * (1.0 - g1),
                                   dm * s_ref[0].astype(F32) * g2 * (1.0 - g2)], axis=1)
            dg_ref[0] = dgp.astype(BF)
            acc_ref[0, 0:1, :] += jnp.sum(dgp, axis=0, keepdims=True)

    xmap = lambda e, t: (e, jnp.minimum(t, nx - 1), 0)
    blk = pl.BlockSpec((1, ROW_TILE, D), xmap)
    return pl.pallas_call(
        body, name="merge_bwd", grid=(2, R // ROW_TILE),
        in_specs=[blk, pl.BlockSpec((1, ROW_TILE, 2 * D), lambda e, t: (e, jnp.minimum(t, nx - 1), OFF_GATE // (2 * D))),
                  blk, blk, pl.BlockSpec((1, 2 * D), lambda e, t: (0, 0))],
        out_specs=(blk, blk, pl.BlockSpec((1, ROW_TILE, 2 * D), lambda e, t: (e, t, 0)),
                   pl.BlockSpec((1, 8, 2 * D), lambda e, t: (e, 0, 0))),
        out_shape=(jax.ShapeDtypeStruct((2, L, D), BF), jax.ShapeDtypeStruct((2, L, D), BF),
                   jax.ShapeDtypeStruct((2, R, 2 * D), BF), jax.ShapeDtypeStruct((2, 8, 2 * D), F32)),
        compiler_params=_params(("parallel", "arbitrary")),
    )(d_merged, proj3, P, S, b_merge)


def _final(out3, x, tgt, gtab, norm_post, L):
    def body(o_ref, x_ref, t_ref, g_ref, n_ref, dxo_ref, do_ref, acc_ref):
        @pl.when(pl.program_id(1) == 0)
        def _():
            acc_ref[...] = jnp.zeros_like(acc_ref)

        o = o_ref[0].astype(F32)
        gate = g_ref[0, 0:1, :]
        npost = n_ref[...]
        r2 = lax.rsqrt(jnp.mean(o * o, axis=-1, keepdims=True) + EPS)
        nh = o * r2
        on = nh * npost
        err = x_ref[0] + gate * on - t_ref[0]
        dxo = err * (1.0 / D)
        dxo_ref[0] = dxo.astype(BF)
        dnh = dxo * gate * npost
        do_ref[0] = (r2 * (dnh - nh * jnp.mean(dnh * nh, axis=-1, keepdims=True))).astype(BF)
        acc_ref[0, 0:1, :] += jnp.sum(dxo * on, axis=0, keepdims=True)
        acc_ref[0, 1:2, :] += jnp.sum(dxo * gate * nh, axis=0, keepdims=True)
        acc_ref[0, 2:3, :] += jnp.sum(err * err, axis=0, keepdims=True)

    blk = pl.BlockSpec((1, X_TILE, D), lambda e, t: (e, t, 0))
    return pl.pallas_call(
        body, name="final", grid=(2, L // X_TILE),
        in_specs=[blk, blk, blk, pl.BlockSpec((1, 8, D), lambda e, t: (e, 0, 0)),
                  pl.BlockSpec((1, D), lambda e, t: (0, 0))],
        out_specs=(blk, blk, pl.BlockSpec((1, 8, D), lambda e, t: (e, 0, 0))),
        out_shape=(jax.ShapeDtypeStruct((2, L, D), BF), jax.ShapeDtypeStruct((2, L, D), BF),
                   jax.ShapeDtypeStruct((2, 8, D), F32)),
        compiler_params=_params(("parallel", "arbitrary")),
    )(out3, x, tgt, gtab, norm_post)


def _local_step(x, c, ctx, loss_target, W, late_shard=None, exchange=False):
    nb, L, _ = x.shape
    LC = ctx.shape[1]
    R = L + LC
    assert nb == 2 and L % ROW_TILE == 0 and LC % Q == 0 and L % POOL_TILE == 0
    w_inT = W["w_in"]
    w_dtT = jnp.pad(w_inT[OFF_DT:], ((0, 64), (0, 0)))
    tables = _pool_tables(L)
    tr, tl = (2 * R) // 8, (2 * L) // 8

    c16 = jnp.zeros((16, D), F32).at[0:2].set(c).at[2].set(W["c_ctx"])
    mod16 = _adaln_fwd(c16, W["w_ada"], W["b_ada"])
    shift, scale, gate = mod16[:, :D], mod16[:, D:2 * D], mod16[:, 2 * D:]
    npre = W["norm_pre"]
    tab = jnp.zeros((2, 2, 8, D), F32)
    for e in range(2):
        tab = tab.at[e, 0, 0].set(npre[0] * (1.0 + scale[e])).at[e, 0, 1].set(shift[e])
        tab = tab.at[e, 1, 0].set(npre[0] * (1.0 + scale[2])).at[e, 1, 1].set(shift[2])
    gtab = jnp.zeros((2, 8, D), F32).at[:, 0].set(gate[0:2])

    hx = _norm_mod_fwd(x, ctx, tab)
    hx2 = hx.reshape(2 * R, D)
    if late_shard is None:
        proj = _matmul(hx2, w_inT, BF, "proj_main", tm=2 * tr, tn=1024, bt=True, n=OFF_DT)
    else:
        proj, late = _matmul(hx2, w_inT, BF, "proj_main", tm=2 * tr, tn=1024, bt=True, n=OFF_DT, side=_gather_side(late_shard))
        W = {**W, **_unpack_gather(late, GATHER_LATE)}
    proj3 = proj.reshape(2, R, OFF_DT)
    dt_raw = _matmul(hx2, w_dtT, F32, "proj_dt", tm=tr, bt=True).reshape(2, R, 128)
    ypool = _pool_fwd(proj3, W["pool_w"], W["pool_scale"], tables, L)
    xbc = _conv_fwd(proj3, W["conv_w"], W["conv_b"], L)
    bias128 = jnp.pad(W["dt_bias"].reshape(1, 64), ((0, 0), (0, 64)))
    dt_loc = _dt_fwd(dt_raw, bias128)
    A = -jnp.exp(W["a_log"].reshape(2, NG, HPG))
    a_loc = jnp.zeros((NG, 8, 128), F32).at[:, 0, :16].set(A.transpose(1, 0, 2).reshape(NG, 16))
    y_f, hs_f, y_b, hs_b = _ssd_fwd(xbc, dt_loc, a_loc, L)
    dskip_e = jnp.repeat(W["d_skip"].reshape(1, 32), HEAD, axis=1)
    yn, y2b = _ssd_post_fwd(y_f, y_b, xbc, proj3, dskip_e, W["ssd_norm"], L)
    ypool2, yn2 = ypool.reshape(2 * L, D), yn.reshape(2 * L, DIN)
    P = _matmul(ypool2, W["w_proj_pool"], BF, "proj_pool", tm=2 * tl, tn=1024).reshape(2, L, D)
    S = _matmul(yn2, W["w_proj_ssd"], BF, "proj_ssd", tm=2 * tl, tn=1024).reshape(2, L, D)
    merged = _merge_fwd(proj3, P, S, W["b_merge"], L)
    merged2 = merged.reshape(2 * L, D)
    out3 = _matmul(merged2, W["w_out"], BF, "proj_out", tm=2 * tl, tn=1024).reshape(2, L, D)
    dxo, dout, acc_f = _final(out3, x, loss_target, gtab, W["norm_post"], L)

    dout2 = dout.reshape(2 * L, D)
    g = {}
    g["w_out"] = _matmul_tn(merged2, dout2, "dw_out", ta=1024, tn=1024, tr=4 * tl)
    d_merged = _matmul(dout2, W["w_out"], BF, "d_merged", tm=2 * tl, tn=1024, bt=True).reshape(2, L, D)
    dP, dS, dgp, acc_m = _merge_bwd(d_merged, proj3, P, S, W["b_merge"], L)
    dP2, dS2 = dP.reshape(2 * L, D), dS.reshape(2 * L, D)
    g["w_proj_pool"] = _matmul_tn(ypool2, dP2, "dw_proj_pool", ta=1024, tn=1024, tr=4 * tl)
    g["w_proj_ssd"] = _matmul_tn(yn2, dS2, "dw_proj_ssd", ta=1024, tn=1024, tr=4 * tl)
    d_ypool = _matmul(dP2, W["w_proj_pool"], BF, "d_ypool", tm=2 * tl, tn=1024, bt=True).reshape(2, L, D)
    d_yn = _matmul(dS2, W["w_proj_ssd"], BF, "d_yn", tm=2 * tl, tn=1024, bt=True).reshape(2, L, DIN)
    dv, dzp, g["pool_w"], acc_p = _pool_bwd(proj3, d_ypool, W["pool_w"], jnp.swapaxes(W["pool_w"], 1, 2),
                                            W["pool_scale"], tables, L)
    dy2, dzs, acc_s = _ssd_post_bwd(d_yn, y2b, xbc, proj3, W["ssd_norm"], L)
    dxs_f, dbc_f, ddt_f, dxs_b, dbc_b, ddt_b, acc_a = _ssd_bwd(xbc, dt_loc, a_loc, hs_f, hs_b, y_f, y_b, dy2, L)
    ident = lambda j: j
    dxr_xs, acc_cx = _conv_bwd(proj3, [dxs_f, dxs_b, dy2], [None, None, dskip_e], 0, DIN, [ident, ident, ident],
                               W["conv_w"], W["conv_b"], L, "conv_bwd_xs")
    bcmap = lambda j: 2 * lax.rem(j, NG) + j // NG
    dxr_bc, acc_cb = _conv_bwd(proj3, [dbc_f, dbc_b], [None, None], DIN, 2 * NG * NST, [bcmap, bcmap],
                               W["conv_w"], W["conv_b"], L, "conv_bwd_bc")
    ddtr, acc_d = _dt_bwd(dt_raw, bias128, ddt_f, ddt_b)
    pieces = [dv, dzp, dzs, dgp, dxr_xs, dxr_bc]
    dw_rows = [_matmul_tn(p.reshape(2 * R, p.shape[2]), hx2, "dw_in_%d" % i, ta=1024, tn=1024, tr=4 * tr)
               for i, p in enumerate(pieces)]
    dw_rows.append(_matmul_tn(ddtr.reshape(2 * R, 128), hx2, "dw_in_dt", ta=128, tn=1024, tr=tr)[:64])
    g["w_in"] = jnp.concatenate(dw_rows, axis=0)
    acc_c = jnp.concatenate([acc_cx[0] + acc_cx[1], acc_cb[0] + acc_cb[1]], axis=1)
    g["conv_w"] = acc_c[0:4]
    g["conv_b"] = acc_c[4:5]
    if exchange:
        gb = _pack_grads(g, GRADS_EARLY)
        pair = _pair_add(gb, _pair_exchange(gb, None, "grads_pair_exchange_early"), "grads_pair_add_early")
        dh, recv_early = _dhx(pieces, ddtr, w_inT, w_dtT, side=_chip_exchange_side(pair))
    else:
        dh, recv_early = _dhx(pieces, ddtr, w_inT, w_dtT), None
    grad_x, acc_n = _norm_mod_bwd(dh, x, ctx, tab, dxo)
    g["w_ada"], db_rows, sm_rows = _adaln_bwd(acc_n, acc_f, mod16, c16, npre, W["w_ada"])

    g["b_ada"] = db_rows[0:1]
    g["norm_pre"] = sm_rows[0:1]
    g["c_ctx"] = sm_rows[1]
    g["norm_post"] = acc_f[0, 1:2] + acc_f[1, 1:2]
    g["b_merge"] = acc_m[0, 0:1] + acc_m[1, 0:1]
    g["pool_scale"] = acc_p[:, 0, :].reshape(1, D)
    g["dt_bias"] = (acc_d[0, 0, :64] + acc_d[1, 0, :64]).reshape(2, 32)
    dA = (acc_a[0, :, 0, :16] + acc_a[1, :, 0, :16]).reshape(NG, 2, HPG).transpose(1, 0, 2)
    g["a_log"] = (dA * A).reshape(2, 32)
    g["d_skip"] = (acc_s[0, 1] + acc_s[1, 1]).reshape(32, HEAD).sum(axis=1).reshape(1, 32)
    g["ssd_norm"] = acc_s[0, 0:1] + acc_s[1, 0:1]
    loss_lanes = acc_f[:, 2, :]
    return loss_lanes, grad_x, g, recv_early


MESH = pl.DeviceIdType.MESH
ANY = pl.BlockSpec(memory_space=pl.ANY)


def _all_gather(shard):
    m_per, n = shard.shape

    def body(x_ref, out_ref, send_sems, recv_sems, local_sem):
        x, y, c = lax.axis_index("x"), lax.axis_index("y"), lax.axis_index("c")
        me, sibling = (x, y, c), (x, y, 1 - c)
        chips = [(1 - x, y), (x, 1 - y), (1 - x, 1 - y)]

        def rows(px, py, pc):
            return out_ref.at[pl.ds((4 * px + 2 * py + pc) * m_per, m_per), :]

        def copy(k, block, to, src=None):
            return pltpu.make_async_remote_copy(
                src_ref=rows(*block) if src is None else src, dst_ref=rows(*block),
                send_sem=send_sems.at[k], recv_sem=recv_sems.at[k], device_id=to, device_id_type=MESH)

        mine = pltpu.make_async_copy(x_ref, rows(*me), local_sem)
        mine.start()
        first = [copy(0, me, sibling, src=x_ref)]
        first += [copy(1 + j, me, (*chip, c), src=x_ref) for j, chip in enumerate(chips)]
        for cp in first:
            cp.start()
        passed = [copy(4 + j, (*chip, c), sibling) for j, chip in enumerate(chips)]
        for j, chip in enumerate(chips):
            copy(1 + j, (*chip, c), me).wait_recv()
            passed[j].start()
        copy(0, sibling, me).wait_recv()
        for j, chip in enumerate(chips):
            copy(4 + j, (*chip, 1 - c), me).wait_recv()
        for cp in first + passed:
            cp.wait_send()
        mine.wait()

    return pl.pallas_call(
        body, name="all_gather_weights",
        out_shape=jax.ShapeDtypeStruct((NDEV * m_per, n), shard.dtype),
        in_specs=[ANY], out_specs=ANY,
        scratch_shapes=[pltpu.SemaphoreType.DMA((7,)), pltpu.SemaphoreType.DMA((7,)), pltpu.SemaphoreType.DMA],
    )(shard)


PAIR_PIECES = 12


def _xor_peer(k, x, y, c):
    return (1 - x if k & 4 else x, 1 - y if k & 2 else y, 1 - c if k & 1 else c)


def _pair_exchange(big, small, name):
    _, nq, rows, n = big.shape
    piece = rows // PAIR_PIECES
    assert piece * PAIR_PIECES == rows and piece % 16 == 0
    with_small = small is not None

    def body(*refs):
        if with_small:
            big_ref, small_ref, got_ref, osmall_ref, send_sems, recv_sems, local_sem = refs
        else:
            big_ref, got_ref, send_sems, recv_sems, local_sem = refs
        x, y, c = lax.axis_index("x"), lax.axis_index("y"), lax.axis_index("c")
        me = 4 * x + 2 * y + c

        def rc(src, dst, sem, peer):
            return pltpu.make_async_remote_copy(src_ref=src, dst_ref=dst, send_sem=send_sems.at[sem],
                                                recv_sem=recv_sems.at[sem], device_id=peer, device_id_type=MESH)

        sib = _xor_peer(1, x, y, c)
        local, sends, recvs = [], [], []
        for q in range(nq):
            for h in range(PAIR_PIECES):
                rws = pl.ds(h * piece, piece)
                cp = rc(big_ref.at[1 - c, q, rws], got_ref.at[q, rws], 8 + q * PAIR_PIECES + h, sib)
                sends.append(cp)
                recvs.append(cp)
        if with_small:
            local.append(pltpu.make_async_copy(small_ref, osmall_ref.at[me], local_sem))
            for k in range(1, NDEV):
                px, py, pc = _xor_peer(k, x, y, c)
                sends.append(rc(small_ref, osmall_ref.at[me], k, (px, py, pc)))
                recvs.append(rc(small_ref, osmall_ref.at[4 * px + 2 * py + pc], k, (px, py, pc)))
        for cp in local + sends:
            cp.start()
        for cp in sends:
            cp.wait_send()
        for cp in recvs:
            cp.wait_recv()
        for cp in local:
            cp.wait()

    nsem = 8 + nq * PAIR_PIECES
    out_shape = [jax.ShapeDtypeStruct(big.shape[1:], big.dtype)]
    if with_small:
        out_shape.append(jax.ShapeDtypeStruct((NDEV,) + small.shape, small.dtype))
    out = pl.pallas_call(
        body, name=name, out_shape=tuple(out_shape),
        in_specs=[ANY] * (1 + with_small), out_specs=(ANY,) * (1 + with_small),
        scratch_shapes=[pltpu.SemaphoreType.DMA((nsem,)), pltpu.SemaphoreType.DMA((nsem,)), pltpu.SemaphoreType.DMA],
    )(*((big, small) if with_small else (big,)))
    return out if with_small else out[0]


def _pair_add(big, got, name):
    _, nq, rows, n = big.shape
    tile = rows // 4
    assert rows % 64 == 0

    def body(c_ref, a_ref, b_ref, o_ref):
        o_ref[0] = (a_ref[0, 0].astype(F32) + b_ref[0].astype(F32)).astype(BF)

    blk = pl.BlockSpec((1, tile, n), lambda q, i, c_ref: (q, i, 0))
    return pl.pallas_call(
        body, name=name,
        grid_spec=pltpu.PrefetchScalarGridSpec(
            num_scalar_prefetch=1, grid=(nq, rows // tile),
            in_specs=[pl.BlockSpec((1, 1, tile, n), lambda q, i, c_ref: (c_ref[0], q, i, 0)), blk], out_specs=blk),
        out_shape=jax.ShapeDtypeStruct(got.shape, BF), compiler_params=_params(("parallel", "parallel")),
    )(lax.axis_index("c").astype(jnp.int32).reshape(1), big, got)


def _chip_exchange_side(pair):
    def make(in_refs, out_refs, send_sems, recv_sems, local_sem, arrivals=True):
        (in_ref,), (out_ref,) = in_refs, out_refs
        x, y, c = lax.axis_index("x"), lax.axis_index("y"), lax.axis_index("c")
        q = 2 * x + y
        local = [pltpu.make_async_copy(in_ref.at[q], out_ref.at[q], local_sem)]
        sends, recvs = [], []
        for j in range(1, 4):
            px, py, pc = _xor_peer(2 * j, x, y, c)
            pq = 2 * px + py
            for lst, dst in ((sends, out_ref.at[q]), (recvs, out_ref.at[pq]))[:1 + arrivals]:
                lst.append(pltpu.make_async_remote_copy(
                    src_ref=in_ref.at[pq], dst_ref=dst, send_sem=send_sems.at[j - 1], recv_sem=recv_sems.at[j - 1],
                    device_id=(px, py, pc), device_id_type=MESH))
        return local, sends, recvs

    return _SideCopies([pair], [jax.ShapeDtypeStruct(pair.shape, pair.dtype)], make)


def _gather_side(shard):
    def make(in_refs, out_refs, send_sems, recv_sems, local_sem, arrivals=True):
        (src,), (dst,) = in_refs, out_refs
        x, y, c = lax.axis_index("x"), lax.axis_index("y"), lax.axis_index("c")
        me = 4 * x + 2 * y + c
        local = [pltpu.make_async_copy(src, dst.at[me], local_sem)]
        sends, recvs = [], []
        for k in range(1, NDEV):
            px, py, pc = _xor_peer(k, x, y, c)
            for lst, slot in ((sends, me), (recvs, 4 * px + 2 * py + pc))[:1 + arrivals]:
                lst.append(pltpu.make_async_remote_copy(
                    src_ref=src, dst_ref=dst.at[slot], send_sem=send_sems.at[k - 1], recv_sem=recv_sems.at[k - 1],
                    device_id=(px, py, pc), device_id_type=MESH))
        return local, sends, recvs

    return _SideCopies([shard], [jax.ShapeDtypeStruct((NDEV,) + shard.shape, shard.dtype)], make)


ADAM_TILE = 64
PACK_W = 1024


def _adamw(recv, w, m, v, name, side=None):
    rp = w.shape[0]
    tile = min(ADAM_TILE, rp)
    nsrc = recv.shape[0]
    grid = (rp // tile,)
    n_si, n_so = (len(side.inputs), len(side.out_shapes)) if side else (0, 0)

    def body(*refs):
        r_ref, w_ref, m_ref, v_ref = refs[:4]
        g_ref, d_ref, nm_ref, nv_ref = refs[4 + n_si:8 + n_si]
        side_refs = (refs[4:4 + n_si], refs[8 + n_si:8 + n_si + n_so], refs[8 + n_si + n_so:])
        if side:
            side.start(grid, *side_refs)
        g = r_ref[0].astype(F32)
        for i in range(1, nsrc):
            g = g + r_ref[i].astype(F32)
        m1 = ADAM_B1 * m_ref[...] + (1.0 - ADAM_B1) * g
        v1 = ADAM_B2 * v_ref[...] + (1.0 - ADAM_B2) * (g * g)
        m_hat = m1 / (1.0 - ADAM_B1 ** ADAM_STEP)
        v_hat = v1 / (1.0 - ADAM_B2 ** ADAM_STEP)
        g_ref[...] = g
        d_ref[...] = -ADAM_LR * (m_hat / (jnp.sqrt(v_hat) + ADAM_EPS) + ADAM_WD * w_ref[...])
        nm_ref[...] = m1
        nv_ref[...] = v1
        if side:
            side.wait(grid, *side_refs)

    blk = pl.BlockSpec((tile, PACK_W), lambda i: (i, 0))
    shp = jax.ShapeDtypeStruct((rp, PACK_W), F32)
    return pl.pallas_call(
        body, name=name, grid=grid,
        in_specs=[pl.BlockSpec((nsrc, tile, PACK_W), lambda i: (0, i, 0)), blk, blk, blk] + [ANY] * n_si,
        out_specs=(blk, blk, blk, blk) + (ANY,) * n_so,
        out_shape=(shp, shp, shp, shp) + tuple(side.out_shapes if side else ()),
        scratch_shapes=side.scratch() if side else [],
        compiler_params=_params(("arbitrary",) if side else ("parallel",)),
    )(recv, w, m, v, *(side.inputs if side else ()))


BIG = {"w_ada": ((3 * D, D), 0), "pool_w": ((4, PGW, PGW), 1), "w_proj_pool": ((D, D), 0), "w_proj_ssd": ((DIN, D), 0),
       "w_out": ((D, D), 0), "w_in": ((IN_COLS, D), 0), "conv_w": ((4, CONV_DIM), 1)}
TRANSPOSED = ("w_ada", "w_in")
PACK_ROWS = {"w_ada": 384, "w_in": 1168, "conv_w": 16, "pool_w": 32, "w_proj_pool": 128, "w_proj_ssd": 256, "w_out": 128}
GATHER_EARLY = ("w_ada", "w_in", "conv_w")
GATHER_LATE = ("pool_w", "w_proj_pool", "w_proj_ssd", "w_out")
GRADS_LATE = ("w_ada",)
GRADS_EARLY = tuple(n for n in PACK_ROWS if n not in GRADS_LATE)
SMALL = {"c_ctx": (D,), "b_ada": (1, 3 * D), "norm_pre": (1, D), "norm_post": (1, D), "b_merge": (1, 2 * D),
         "pool_scale": (1, D), "conv_b": (1, CONV_DIM), "dt_bias": (2, 32), "a_log": (2, 32), "d_skip": (1, 32),
         "ssd_norm": (1, DIN)}
LOSS_SLOT = 128
assert all(_r % 16 == 0 for _r in PACK_ROWS.values())
SMALL_ROWS = 16


def _shard_shape(name):
    shape, ax = BIG[name]
    return tuple(s // NDEV if i == ax else s for i, s in enumerate(shape))


def _as_rows(t, rows):
    pad = [(0, 0)] * (t.ndim - 1) + [(0, rows * PACK_W - t.shape[-1])]
    return jnp.pad(t, pad).reshape(t.shape[:-1] + (rows, PACK_W))


def _shard_rows(t, name):
    sh, r = _shard_shape(name), PACK_ROWS[name]
    lead = t.shape[:t.ndim - len(sh)]
    if len(sh) == 2 and sh[1] == PACK_W:
        return jnp.pad(t, [(0, 0)] * len(lead) + [(0, r - sh[0]), (0, 0)])
    if int(np.prod(sh)) == r * PACK_W:
        return t.reshape(lead + (r, PACK_W))
    return _as_rows(t.reshape(lead + (-1,)), r)


def _to_chunks(full, name):
    shape, ax = BIG[name]
    split = shape[:ax] + (NDEV, shape[ax] // NDEV) + shape[ax + 1:]
    return _shard_rows(jnp.moveaxis(full.reshape(split), ax, 0), name)


def _from_chunks(chunks, name):
    shape, ax = BIG[name]
    return jnp.moveaxis(chunks.reshape((NDEV,) + _shard_shape(name)), 0, ax).reshape(shape)


def _rows_of(names):
    return sum(PACK_ROWS[n] for n in names)


def _pack_state(t, names):
    return jnp.concatenate([_shard_rows(t[n], n) for n in names], axis=0)


def _pack_small(t, loss_part=None):
    slot = jnp.zeros((LOSS_SLOT,), F32)
    if loss_part is not None:
        slot = slot.at[0].set(loss_part)
    return _as_rows(jnp.concatenate([t[n].reshape(-1) for n in SMALL] + [slot]), SMALL_ROWS)


def _pack_grads(g, names):
    big = jnp.concatenate([_to_chunks(g[n], n).astype(BF) for n in names], axis=1)
    return jnp.swapaxes(big.reshape(4, 2, _rows_of(names), PACK_W), 0, 1)


def _unpack_state(big, names):
    out, off = {}, 0
    for n in names:
        sh, r = _shard_shape(n), PACK_ROWS[n]
        k = int(np.prod(sh))
        if len(sh) == 2 and sh[1] == PACK_W:
            out[n] = big[off:off + sh[0]]
        else:
            out[n] = big[off:off + r].reshape(-1)[:k].reshape(sh)
        off += r
    return out


def _unpack_small(small):
    out, flat, off = {}, small.reshape(-1), 0
    for n, sh in SMALL.items():
        k = int(np.prod(sh))
        out[n] = flat[off:off + k].reshape(sh)
        off += k
    out["loss"] = flat[off]
    return out


def _pack_gather(w, names):
    pieces = []
    for n in names:
        if n == "conv_w":
            pieces.append(_as_rows(jnp.concatenate([p.reshape(-1) for p in _split(w[n], 3)]), PACK_ROWS[n]))
        else:
            pieces.append(_shard_rows(w[n], n).astype(BF))
    return jnp.concatenate(pieces, axis=0)


def _unpack_gather(gathered, names):
    g = gathered.reshape(NDEV, _rows_of(names), PACK_W)
    out, off = {}, 0
    for n in names:
        r = PACK_ROWS[n]
        sh = _shard_shape(n)
        if n == "conv_w":
            k = int(np.prod(sh))
            terms = g[:, off:off + r].reshape(NDEV, -1)[:, :3 * k].astype(F32).reshape(NDEV, 3, k)
            out[n] = _from_chunks(terms[:, 0] + terms[:, 1] + terms[:, 2], n)
        elif len(sh) == 2 and sh[1] == PACK_W:
            out[n] = _from_chunks(g[:, off:off + sh[0]], n)
        else:
            out[n] = _from_chunks(g[:, off:off + r], n)
        off += r
    return out


PARAMS = ["c_ctx", "w_ada", "b_ada", "norm_pre", "norm_post", "w_in", "b_merge", "pool_w", "pool_scale", "conv_w", "conv_b",
          "dt_bias", "a_log", "d_skip", "ssd_norm", "w_proj_pool", "w_proj_ssd", "w_out"]


def kernel(x, c, ctx, c_ctx, w_ada, b_ada, norm_pre, norm_post, w_in, b_merge, pool_w, pool_scale, conv_w, conv_b, dt_bias, a_log, d_skip, ssd_norm, w_proj_pool, w_proj_ssd, w_out, loss_target, m_c_ctx, m_w_ada, m_b_ada, m_norm_pre, m_norm_post, m_w_in, m_b_merge, m_pool_w, m_pool_scale, m_conv_w, m_conv_b, m_dt_bias, m_a_log, m_d_skip, m_ssd_norm, m_w_proj_pool, m_w_proj_ssd, m_w_out, v_c_ctx, v_w_ada, v_b_ada, v_norm_pre, v_norm_post, v_w_in, v_b_merge, v_pool_w, v_pool_scale, v_conv_w, v_conv_b, v_dt_bias, v_a_log, v_d_skip, v_ssd_norm, v_w_proj_pool, v_w_proj_ssd, v_w_out):
    given = dict(locals())
    shapes = {n: given[n].shape for n in PARAMS}

    def local(prefix):
        t = {n: (given[prefix + n] if n == "c_ctx" else given[prefix + n][0]) for n in PARAMS}
        for n in TRANSPOSED:
            t[n] = t[n].T
        return {n: t[n].reshape(_shard_shape(n) if n in BIG else SMALL[n]) for n in PARAMS}

    w, m, v = local(""), local("m_"), local("v_")

    W = _unpack_gather(_all_gather(_pack_gather(w, GATHER_EARLY)), GATHER_EARLY)
    for n in SMALL:
        W[n] = w[n]
    lanes, grad_x, g, recv_early = _local_step(x, c, ctx, loss_target, W, late_shard=_pack_gather(w, GATHER_LATE),
                                               exchange=True)
    gb = _pack_grads(g, GRADS_LATE)
    got, recv_small = _pair_exchange(gb, _pack_small(g, (0.5 / D) * jnp.sum(lanes)), "grads_pair_exchange_late")
    late = _chip_exchange_side(_pair_add(gb, got, "grads_pair_add_late"))
    res = [{} for _ in range(4)]
    *early, recv_late = _adamw(recv_early, *[_pack_state(s, GRADS_EARLY) for s in (w, m, v)], "adamw_early", side=late)
    for r, t in zip(res, early):
        r.update(_unpack_state(t, GRADS_EARLY))
    for r, t in zip(res, _adamw(recv_late, *[_pack_state(s, GRADS_LATE) for s in (w, m, v)], "adamw_late")):
        r.update(_unpack_state(t, GRADS_LATE))
    for r, t in zip(res, _adamw(recv_small, *[_pack_small(s) for s in (w, m, v)], "adamw_small")):
        r.update(_unpack_small(t))
    outs = [res[0]["loss"], grad_x]
    for r in res:
        for n in TRANSPOSED:
            r[n] = r[n].T
        outs += [r[n].reshape(shapes[n]) for n in PARAMS]
    return tuple(outs)
```
